```python
import math
import jax, jax.numpy as jnp
from jax import lax
import numpy as np

D_MODEL = 1024
BATCH = 8
SEQ = 8192
DEPTH = 1

NORM_EPS = 1e-6
CHUNK = 128
N_BRANCHES = 2
GMLP_WIDTH = D_MODEL
GMLP_GROUPS = 8
GMLP_GROUP_DIM = GMLP_WIDTH // GMLP_GROUPS
SSM_EXPAND = 2
D_INNER = SSM_EXPAND * D_MODEL
HEAD_DIM = 64
N_SSM_HEADS = D_INNER // HEAD_DIM
N_SSM_GROUPS = 8
HEADS_PER_GROUP = N_SSM_HEADS // N_SSM_GROUPS
D_STATE = 128
CONV_WIDTH = 4
CONV_DIM = D_INNER + 2 * N_SSM_GROUPS * D_STATE
SSM_NORM_GROUP = D_INNER // N_SSM_GROUPS
D_FF = 4 * D_MODEL
IN_PROJ_DIM = 2 * GMLP_WIDTH + D_INNER + CONV_DIM + N_SSM_HEADS + N_BRANCHES * D_MODEL
_SPLITS = tuple(np.cumsum([2 * GMLP_WIDTH, D_INNER, CONV_DIM, N_SSM_HEADS]).tolist())

kernel_name = "hybrid_gmlp_ssd_gated_block"


def rms_norm(x, g, eps=NORM_EPS):
    xf = x.astype(jnp.float32)
    out = xf * lax.rsqrt(jnp.mean(xf * xf, axis=-1, keepdims=True) + eps)
    return out.astype(x.dtype) * g


def layer_norm(x, g, b, eps=NORM_EPS):
    xf = x.astype(jnp.float32)
    mu = jnp.mean(xf, axis=-1, keepdims=True)
    var = jnp.mean(jnp.square(xf - mu), axis=-1, keepdims=True)
    out = (xf - mu) * lax.rsqrt(var + eps)
    return out.astype(x.dtype) * g + b


def gmlp_spatial_gating(uv, v_g, v_b, w_spatial, b_spatial):
    bsz, seqlen, _ = uv.shape
    nc = seqlen // CHUNK
    z = jax.nn.gelu(uv, approximate=False)
    u, v = jnp.split(z, 2, axis=-1)
    v = layer_norm(v, v_g, v_b)
    v = v.reshape(bsz, nc, CHUNK, GMLP_GROUPS, GMLP_GROUP_DIM)
    causal = jnp.tril(jnp.ones((CHUNK, CHUNK), dtype=bool))
    w = jnp.where(causal[None], w_spatial, jnp.zeros_like(w_spatial))
    s = jnp.einsum("gij,bcjgd->bcigd", w, v) + b_spatial.T[None, None, :, :, None]
    return u * s.reshape(bsz, seqlen, GMLP_WIDTH)


def causal_depthwise_conv(x, w, b):
    y = lax.conv_general_dilated(
        x, w, window_strides=(1,), padding=[(CONV_WIDTH - 1, 0)],
        dimension_numbers=("NWC", "WIO", "NWC"), feature_group_count=x.shape[-1])
    return y + b


def ssd_chunked(xh, dt, a, bm, cm):
    bsz, seqlen = xh.shape[:2]
    nc = seqlen // CHUNK
    xc = xh.reshape(bsz, nc, CHUNK, N_SSM_GROUPS, HEADS_PER_GROUP, HEAD_DIM)
    dtc = dt.reshape(bsz, nc, CHUNK, N_SSM_GROUPS, HEADS_PER_GROUP)
    bc = bm.reshape(bsz, nc, CHUNK, N_SSM_GROUPS, D_STATE)
    cc = cm.reshape(bsz, nc, CHUNK, N_SSM_GROUPS, D_STATE)
    xdt = xc * dtc[..., None]
    da = (dtc * a).astype(jnp.float32).transpose(0, 3, 4, 1, 2)
    cs = jnp.cumsum(da, axis=-1)
    causal = jnp.tril(jnp.ones((CHUNK, CHUNK), dtype=bool))
    seg = cs[..., :, None] - cs[..., None, :]
    lmat = jnp.exp(jnp.where(causal, seg, -jnp.inf))
    cb = jnp.einsum("bclgn,bcsgn->bgcls", cc, bc)
    m = cb[:, :, None] * lmat
    y_diag = jnp.einsum("bgrcls,bcsgrp->bclgrp", m, xdt)
    decay_states = jnp.exp(cs[..., -1:] - cs)
    states = jnp.einsum("bcsgn,bgrcs,bcsgrp->bcgrpn", bc, decay_states, xdt)
    chunk_decay = jnp.exp(cs[..., -1])

    def step(h, inp):
        st, dec = inp
        return h * dec[..., None, None] + st, h

    h0 = jnp.zeros_like(states[:, 0])
    _, prev = lax.scan(step, h0, (jnp.moveaxis(states, 1, 0), jnp.moveaxis(chunk_decay, -1, 0)))
    prev = jnp.moveaxis(prev, 0, 1)
    y_off = jnp.einsum("bclgn,bcgrpn,bgrcl->bclgrp", cc, prev, jnp.exp(cs))
    y = (y_diag + y_off).reshape(bsz, seqlen, N_SSM_GROUPS, HEADS_PER_GROUP, HEAD_DIM)
    return y.astype(xh.dtype)


def mamba2_branch(z, xbc, dt_raw, conv_w, conv_b, dt_bias, a_log, d_skip, ssm_norm_g):
    bsz, seqlen, _ = z.shape
    xbc = jax.nn.silu(causal_depthwise_conv(xbc, conv_w, conv_b))
    xs, bm, cm = jnp.split(xbc, [D_INNER, D_INNER + N_SSM_GROUPS * D_STATE], axis=-1)
    xh = xs.reshape(bsz, seqlen, N_SSM_GROUPS, HEADS_PER_GROUP, HEAD_DIM)
    bm = bm.reshape(bsz, seqlen, N_SSM_GROUPS, D_STATE)
    cm = cm.reshape(bsz, seqlen, N_SSM_GROUPS, D_STATE)
    dt = jax.nn.softplus(dt_raw + dt_bias).reshape(bsz, seqlen, N_SSM_GROUPS, HEADS_PER_GROUP)
    a = -jnp.exp(a_log.astype(jnp.float32)).reshape(N_SSM_GROUPS, HEADS_PER_GROUP)
    y = ssd_chunked(xh, dt, a, bm, cm)
    y = y + d_skip.reshape(N_SSM_GROUPS, HEADS_PER_GROUP)[:, :, None] * xh
    y = y.reshape(bsz, seqlen, D_INNER)
    yg = (y * jax.nn.silu(z)).reshape(bsz, seqlen, N_SSM_GROUPS, SSM_NORM_GROUP)
    yf = yg.astype(jnp.float32)
    yn = yf * lax.rsqrt(jnp.mean(yf * yf, axis=-1, keepdims=True) + NORM_EPS)
    return yn.reshape(bsz, seqlen, D_INNER).astype(z.dtype) * ssm_norm_g


def _fwd_setup_inputs(seed: int = 0) -> dict:
    key = jax.random.key(seed)
    ks = jax.random.split(key, 24)
    L = DEPTH

    def nrm(k, shape, scale):
        return jax.random.normal(k, shape, jnp.float32) * scale

    x = nrm(ks[0], (BATCH, SEQ, D_MODEL), 1.0)
    norm_mix_g = 1.0 + nrm(ks[1], (L, D_MODEL), 0.02)
    w_in = nrm(ks[2], (L, D_MODEL, IN_PROJ_DIM), D_MODEL ** -0.5)
    conv_w = nrm(ks[3], (L, CONV_WIDTH, 1, CONV_DIM), CONV_WIDTH ** -0.5)
    conv_b = nrm(ks[4], (L, CONV_DIM), 0.02)
    dt0 = jnp.exp(jax.random.uniform(ks[5], (L, N_SSM_HEADS), jnp.float32,
                                     minval=math.log(1e-3), maxval=math.log(1e-1)))
    dt_bias = dt0 + jnp.log(-jnp.expm1(-dt0))
    a_log = jnp.log(jax.random.uniform(ks[6], (L, N_SSM_HEADS), jnp.float32, minval=1.0, maxval=16.0))
    d_skip = 1.0 + nrm(ks[7], (L, N_SSM_HEADS), 0.02)
    ssm_norm_g = 1.0 + nrm(ks[8], (L, D_INNER), 0.02)
    v_norm_g = 1.0 + nrm(ks[9], (L, GMLP_WIDTH), 0.02)
    v_norm_b = nrm(ks[10], (L, GMLP_WIDTH), 0.02)
    w_spatial = nrm(ks[11], (L, GMLP_GROUPS, CHUNK, CHUNK), CHUNK ** -0.5)
    b_spatial = 1.0 + nrm(ks[12], (L, GMLP_GROUPS, CHUNK), 0.02)
    b_gates = nrm(ks[13], (L, N_BRANCHES * D_MODEL), 0.02)
    w_proj_a = nrm(ks[14], (L, GMLP_WIDTH, D_MODEL), GMLP_WIDTH ** -0.5)
    w_proj_b = nrm(ks[15], (L, D_INNER, D_MODEL), D_INNER ** -0.5)
    w_out = nrm(ks[16], (L, D_MODEL, D_MODEL), D_MODEL ** -0.5)
    norm_mlp_g = 1.0 + nrm(ks[17], (L, D_MODEL), 0.02)
    w_mlp_up = nrm(ks[18], (L, D_MODEL, D_FF), D_MODEL ** -0.5)
    w_mlp_down = nrm(ks[19], (L, D_FF, D_MODEL), D_FF ** -0.5)
    norm_final_g = 1.0 + nrm(ks[20], (D_MODEL,), 0.02)
    return {"x": x, "norm_mix_g": norm_mix_g, "w_in": w_in, "conv_w": conv_w, "conv_b": conv_b,
            "dt_bias": dt_bias, "a_log": a_log, "d_skip": d_skip, "ssm_norm_g": ssm_norm_g,
            "v_norm_g": v_norm_g, "v_norm_b": v_norm_b, "w_spatial": w_spatial, "b_spatial": b_spatial,
            "b_gates": b_gates, "w_proj_a": w_proj_a, "w_proj_b": w_proj_b, "w_out": w_out,
            "norm_mlp_g": norm_mlp_g, "w_mlp_up": w_mlp_up, "w_mlp_down": w_mlp_down,
            "norm_final_g": norm_final_g}


def _fwd_reference(x, norm_mix_g, w_in, conv_w, conv_b, dt_bias, a_log, d_skip, ssm_norm_g,
              v_norm_g, v_norm_b, w_spatial, b_spatial, b_gates, w_proj_a, w_proj_b, w_out,
              norm_mlp_g, w_mlp_up, w_mlp_down, norm_final_g):
    for i in range(DEPTH):
        h = rms_norm(x, norm_mix_g[i])
        proj = h @ w_in[i]
        uv, z, xbc, dt_raw, gate_logits = jnp.split(proj, _SPLITS, axis=-1)
        y_a = gmlp_spatial_gating(uv, v_norm_g[i], v_norm_b[i], w_spatial[i], b_spatial[i])
        y_b = mamba2_branch(z, xbc, dt_raw, conv_w[i], conv_b[i], dt_bias[i], a_log[i],
                            d_skip[i], ssm_norm_g[i])
        gates = jax.nn.sigmoid(gate_logits + b_gates[i])
        gate_a, gate_b = jnp.split(gates, 2, axis=-1)
        merged = gate_a * (y_a @ w_proj_a[i]) + gate_b * (y_b @ w_proj_b[i])
        x = x + merged @ w_out[i]
        h2 = rms_norm(x, norm_mlp_g[i])
        x = x + jnp.square(jax.nn.relu(h2 @ w_mlp_up[i])) @ w_mlp_down[i]
    return rms_norm(x, norm_final_g)


import jax as _jax
import jax.numpy as _jnp

TWIN_FORMAT = 'train_step'
FWD_PARAMS = ['x', 'norm_mix_g', 'w_in', 'conv_w', 'conv_b', 'dt_bias', 'a_log', 'd_skip', 'ssm_norm_g', 'v_norm_g', 'v_norm_b', 'w_spatial', 'b_spatial', 'b_gates', 'w_proj_a', 'w_proj_b', 'w_out', 'norm_mlp_g', 'w_mlp_up', 'w_mlp_down', 'norm_final_g']
TWIN_WEIGHTS = ['norm_mix_g', 'w_in', 'conv_w', 'conv_b', 'dt_bias', 'a_log', 'd_skip', 'ssm_norm_g', 'v_norm_g', 'v_norm_b', 'w_spatial', 'b_spatial', 'b_gates', 'w_proj_a', 'w_proj_b', 'w_out', 'norm_mlp_g', 'w_mlp_up', 'w_mlp_down', 'norm_final_g']
TWIN_DIFF_INPUT = 'x'
TWIN_INPUTS = ['x', 'norm_mix_g', 'w_in', 'conv_w', 'conv_b', 'dt_bias', 'a_log', 'd_skip', 'ssm_norm_g', 'v_norm_g', 'v_norm_b', 'w_spatial', 'b_spatial', 'b_gates', 'w_proj_a', 'w_proj_b', 'w_out', 'norm_mlp_g', 'w_mlp_up', 'w_mlp_down', 'norm_final_g', 'loss_target', 'm_norm_mix_g', 'm_w_in', 'm_conv_w', 'm_conv_b', 'm_dt_bias', 'm_a_log', 'm_d_skip', 'm_ssm_norm_g', 'm_v_norm_g', 'm_v_norm_b', 'm_w_spatial', 'm_b_spatial', 'm_b_gates', 'm_w_proj_a', 'm_w_proj_b', 'm_w_out', 'm_norm_mlp_g', 'm_w_mlp_up', 'm_w_mlp_down', 'm_norm_final_g', 'v_norm_mix_g', 'v_w_in', 'v_conv_w', 'v_conv_b', 'v_dt_bias', 'v_a_log', 'v_d_skip', 'v_ssm_norm_g', 'v_v_norm_g', 'v_v_norm_b', 'v_w_spatial', 'v_b_spatial', 'v_b_gates', 'v_w_proj_a', 'v_w_proj_b', 'v_w_out', 'v_norm_mlp_g', 'v_w_mlp_up', 'v_w_mlp_down', 'v_norm_final_g']
TWIN_OUTPUTS = ['loss', 'grad_x', 'grad_norm_mix_g', 'grad_w_in', 'grad_conv_w', 'grad_conv_b', 'grad_dt_bias', 'grad_a_log', 'grad_d_skip', 'grad_ssm_norm_g', 'grad_v_norm_g', 'grad_v_norm_b', 'grad_w_spatial', 'grad_b_spatial', 'grad_b_gates', 'grad_w_proj_a', 'grad_w_proj_b', 'grad_w_out', 'grad_norm_mlp_g', 'grad_w_mlp_up', 'grad_w_mlp_down', 'grad_norm_final_g', 'delta_norm_mix_g', 'delta_w_in', 'delta_conv_w', 'delta_conv_b', 'delta_dt_bias', 'delta_a_log', 'delta_d_skip', 'delta_ssm_norm_g', 'delta_v_norm_g', 'delta_v_norm_b', 'delta_w_spatial', 'delta_b_spatial', 'delta_b_gates', 'delta_w_proj_a', 'delta_w_proj_b', 'delta_w_out', 'delta_norm_mlp_g', 'delta_w_mlp_up', 'delta_w_mlp_down', 'delta_norm_final_g', 'new_m_norm_mix_g', 'new_m_w_in', 'new_m_conv_w', 'new_m_conv_b', 'new_m_dt_bias', 'new_m_a_log', 'new_m_d_skip', 'new_m_ssm_norm_g', 'new_m_v_norm_g', 'new_m_v_norm_b', 'new_m_w_spatial', 'new_m_b_spatial', 'new_m_b_gates', 'new_m_w_proj_a', 'new_m_w_proj_b', 'new_m_w_out', 'new_m_norm_mlp_g', 'new_m_w_mlp_up', 'new_m_w_mlp_down', 'new_m_norm_final_g', 'new_v_norm_mix_g', 'new_v_w_in', 'new_v_conv_w', 'new_v_conv_b', 'new_v_dt_bias', 'new_v_a_log', 'new_v_d_skip', 'new_v_ssm_norm_g', 'new_v_v_norm_g', 'new_v_v_norm_b', 'new_v_w_spatial', 'new_v_b_spatial', 'new_v_b_gates', 'new_v_w_proj_a', 'new_v_w_proj_b', 'new_v_w_out', 'new_v_norm_mlp_g', 'new_v_w_mlp_up', 'new_v_w_mlp_down', 'new_v_norm_final_g']
TWIN_LEAF_KINDS = {'loss': 'loss', 'grad_x': 'grad_x', 'grad_norm_mix_g': 'grad_w', 'grad_w_in': 'grad_w', 'grad_conv_w': 'grad_w', 'grad_conv_b': 'grad_w', 'grad_dt_bias': 'grad_w', 'grad_a_log': 'grad_w', 'grad_d_skip': 'grad_w', 'grad_ssm_norm_g': 'grad_w', 'grad_v_norm_g': 'grad_w', 'grad_v_norm_b': 'grad_w', 'grad_w_spatial': 'grad_w', 'grad_b_spatial': 'grad_w', 'grad_b_gates': 'grad_w', 'grad_w_proj_a': 'grad_w', 'grad_w_proj_b': 'grad_w', 'grad_w_out': 'grad_w', 'grad_norm_mlp_g': 'grad_w', 'grad_w_mlp_up': 'grad_w', 'grad_w_mlp_down': 'grad_w', 'grad_norm_final_g': 'grad_w', 'delta_norm_mix_g': 'delta_w', 'delta_w_in': 'delta_w', 'delta_conv_w': 'delta_w', 'delta_conv_b': 'delta_w', 'delta_dt_bias': 'delta_w', 'delta_a_log': 'delta_w', 'delta_d_skip': 'delta_w', 'delta_ssm_norm_g': 'delta_w', 'delta_v_norm_g': 'delta_w', 'delta_v_norm_b': 'delta_w', 'delta_w_spatial': 'delta_w', 'delta_b_spatial': 'delta_w', 'delta_b_gates': 'delta_w', 'delta_w_proj_a': 'delta_w', 'delta_w_proj_b': 'delta_w', 'delta_w_out': 'delta_w', 'delta_norm_mlp_g': 'delta_w', 'delta_w_mlp_up': 'delta_w', 'delta_w_mlp_down': 'delta_w', 'delta_norm_final_g': 'delta_w', 'new_m_norm_mix_g': 'new_m', 'new_m_w_in': 'new_m', 'new_m_conv_w': 'new_m', 'new_m_conv_b': 'new_m', 'new_m_dt_bias': 'new_m', 'new_m_a_log': 'new_m', 'new_m_d_skip': 'new_m', 'new_m_ssm_norm_g': 'new_m', 'new_m_v_norm_g': 'new_m', 'new_m_v_norm_b': 'new_m', 'new_m_w_spatial': 'new_m', 'new_m_b_spatial': 'new_m', 'new_m_b_gates': 'new_m', 'new_m_w_proj_a': 'new_m', 'new_m_w_proj_b': 'new_m', 'new_m_w_out': 'new_m', 'new_m_norm_mlp_g': 'new_m', 'new_m_w_mlp_up': 'new_m', 'new_m_w_mlp_down': 'new_m', 'new_m_norm_final_g': 'new_m', 'new_v_norm_mix_g': 'new_v', 'new_v_w_in': 'new_v', 'new_v_conv_w': 'new_v', 'new_v_conv_b': 'new_v', 'new_v_dt_bias': 'new_v', 'new_v_a_log': 'new_v', 'new_v_d_skip': 'new_v', 'new_v_ssm_norm_g': 'new_v', 'new_v_v_norm_g': 'new_v', 'new_v_v_norm_b': 'new_v', 'new_v_w_spatial': 'new_v', 'new_v_b_spatial': 'new_v', 'new_v_b_gates': 'new_v', 'new_v_w_proj_a': 'new_v', 'new_v_w_proj_b': 'new_v', 'new_v_w_out': 'new_v', 'new_v_norm_mlp_g': 'new_v', 'new_v_w_mlp_up': 'new_v', 'new_v_w_mlp_down': 'new_v', 'new_v_norm_final_g': 'new_v'}


def _forward(args):
    return _fwd_reference(*[args[k] for k in FWD_PARAMS])


def _output_shape():
    def fwd():
        inp = _fwd_setup_inputs(0)
        return _fwd_reference(*[inp[k] for k in FWD_PARAMS])
    out = _jax.eval_shape(fwd)
    return out.shape, out.dtype

N_MICROBATCH = 1
ADAM_LR = 0.001
ADAM_B1 = 0.9
ADAM_B2 = 0.999
ADAM_EPS = 1e-08
ADAM_WD = 0.01
ADAM_STEP = 10
PER_EXAMPLE_BATCH_AXIS = {'x': 0, 'loss_target': 0}
SHARED_INPUTS = []
_WEIGHT_DTYPES = {'norm_mix_g': _jnp.float32, 'w_in': _jnp.float32, 'conv_w': _jnp.float32, 'conv_b': _jnp.float32, 'dt_bias': _jnp.float32, 'a_log': _jnp.float32, 'd_skip': _jnp.float32, 'ssm_norm_g': _jnp.float32, 'v_norm_g': _jnp.float32, 'v_norm_b': _jnp.float32, 'w_spatial': _jnp.float32, 'b_spatial': _jnp.float32, 'b_gates': _jnp.float32, 'w_proj_a': _jnp.float32, 'w_proj_b': _jnp.float32, 'w_out': _jnp.float32, 'norm_mlp_g': _jnp.float32, 'w_mlp_up': _jnp.float32, 'w_mlp_down': _jnp.float32, 'norm_final_g': _jnp.float32}
MOMENT_SCALE = {'norm_mix_g': 2.214545e-01, 'w_in': 7.156750e-02, 'conv_w': 6.550258e-02, 'conv_b': 1.002162e-01, 'dt_bias': 1.916040e-01, 'a_log': 3.544484e-01, 'd_skip': 4.149710e-01, 'ssm_norm_g': 9.165978e-02, 'v_norm_g': 5.600506e-02, 'v_norm_b': 6.056200e-02, 'w_spatial': 5.542070e-02, 'b_spatial': 8.209252e-02, 'b_gates': 4.192976e-02, 'w_proj_a': 1.042266e-01, 'w_proj_b': 1.244192e-01, 'w_out': 1.633469e-01, 'norm_mlp_g': 2.290960e-01, 'w_mlp_up': 1.043489e-01, 'w_mlp_down': 2.213073e-01, 'norm_final_g': 6.471471e+01}


def _to_microbatches(a, axis):
    t = _jnp.moveaxis(a, axis, 0)
    t = t.reshape((N_MICROBATCH, t.shape[0] // N_MICROBATCH) + t.shape[1:])
    return _jnp.moveaxis(t, 1, axis + 1)


def setup_inputs(seed: int = 0) -> dict:
    inp = _fwd_setup_inputs(seed)
    key = _jax.random.fold_in(_jax.random.key(seed), 7919)
    shape, _ = _output_shape()
    out = dict(inp)
    out["loss_target"] = _jax.random.normal(_jax.random.fold_in(key, 0), shape, _jnp.float32)
    for i, name in enumerate(TWIN_WEIGHTS):
        w = inp[name].astype(_jnp.float32)
        if MOMENT_SCALE is None:
            s = _jnp.sqrt(_jnp.mean(_jnp.square(w)) + 1e-30)
        else:
            s = MOMENT_SCALE[name]
        km, kv = _jax.random.split(_jax.random.fold_in(key, i + 1))
        out[name] = w
        out["m_" + name] = s * _jax.random.normal(km, w.shape, _jnp.float32)
        out["v_" + name] = (s * s) * _jax.random.uniform(kv, w.shape, _jnp.float32, 0.5, 1.5)
    if N_MICROBATCH > 1:
        for name, axis in PER_EXAMPLE_BATCH_AXIS.items():
            out[name] = _to_microbatches(out[name], axis)
    return {'x': out['x'], 'norm_mix_g': out['norm_mix_g'], 'w_in': out['w_in'], 'conv_w': out['conv_w'], 'conv_b': out['conv_b'], 'dt_bias': out['dt_bias'], 'a_log': out['a_log'], 'd_skip': out['d_skip'], 'ssm_norm_g': out['ssm_norm_g'], 'v_norm_g': out['v_norm_g'], 'v_norm_b': out['v_norm_b'], 'w_spatial': out['w_spatial'], 'b_spatial': out['b_spatial'], 'b_gates': out['b_gates'], 'w_proj_a': out['w_proj_a'], 'w_proj_b': out['w_proj_b'], 'w_out': out['w_out'], 'norm_mlp_g': out['norm_mlp_g'], 'w_mlp_up': out['w_mlp_up'], 'w_mlp_down': out['w_mlp_down'], 'norm_final_g': out['norm_final_g'], 'loss_target': out['loss_target'], 'm_norm_mix_g': out['m_norm_mix_g'], 'm_w_in': out['m_w_in'], 'm_conv_w': out['m_conv_w'], 'm_conv_b': out['m_conv_b'], 'm_dt_bias': out['m_dt_bias'], 'm_a_log': out['m_a_log'], 'm_d_skip': out['m_d_skip'], 'm_ssm_norm_g': out['m_ssm_norm_g'], 'm_v_norm_g': out['m_v_norm_g'], 'm_v_norm_b': out['m_v_norm_b'], 'm_w_spatial': out['m_w_spatial'], 'm_b_spatial': out['m_b_spatial'], 'm_b_gates': out['m_b_gates'], 'm_w_proj_a': out['m_w_proj_a'], 'm_w_proj_b': out['m_w_proj_b'], 'm_w_out': out['m_w_out'], 'm_norm_mlp_g': out['m_norm_mlp_g'], 'm_w_mlp_up': out['m_w_mlp_up'], 'm_w_mlp_down': out['m_w_mlp_down'], 'm_norm_final_g': out['m_norm_final_g'], 'v_norm_mix_g': out['v_norm_mix_g'], 'v_w_in': out['v_w_in'], 'v_conv_w': out['v_conv_w'], 'v_conv_b': out['v_conv_b'], 'v_dt_bias': out['v_dt_bias'], 'v_a_log': out['v_a_log'], 'v_d_skip': out['v_d_skip'], 'v_ssm_norm_g': out['v_ssm_norm_g'], 'v_v_norm_g': out['v_v_norm_g'], 'v_v_norm_b': out['v_v_norm_b'], 'v_w_spatial': out['v_w_spatial'], 'v_b_spatial': out['v_b_spatial'], 'v_b_gates': out['v_b_gates'], 'v_w_proj_a': out['v_w_proj_a'], 'v_w_proj_b': out['v_w_proj_b'], 'v_w_out': out['v_w_out'], 'v_norm_mlp_g': out['v_norm_mlp_g'], 'v_w_mlp_up': out['v_w_mlp_up'], 'v_w_mlp_down': out['v_w_mlp_down'], 'v_norm_final_g': out['v_norm_final_g']}


def _loss(weights, diff, rest, loss_target):
    with _jax.named_scope("forward"):
        args = {**rest, TWIN_DIFF_INPUT: diff, **{k: w.astype(_WEIGHT_DTYPES[k]) for k, w in weights.items()}}
        y = _forward(args)
    with _jax.named_scope("loss_head"):
        err = _jnp.square(y.astype(_jnp.float32) - loss_target)
        return 0.5 * _jnp.sum(_jnp.mean(err, axis=-1)) if err.ndim else 0.5 * err


def _adamw(w, g, m, v):
    m = ADAM_B1 * m + (1.0 - ADAM_B1) * g
    v = ADAM_B2 * v + (1.0 - ADAM_B2) * _jnp.square(g)
    m_hat = m / (1.0 - ADAM_B1 ** ADAM_STEP)
    v_hat = v / (1.0 - ADAM_B2 ** ADAM_STEP)
    delta = -ADAM_LR * (m_hat / (_jnp.sqrt(v_hat) + ADAM_EPS) + ADAM_WD * w)
    return delta, m, v


def reference(x, norm_mix_g, w_in, conv_w, conv_b, dt_bias, a_log, d_skip, ssm_norm_g, v_norm_g, v_norm_b, w_spatial, b_spatial, b_gates, w_proj_a, w_proj_b, w_out, norm_mlp_g, w_mlp_up, w_mlp_down, norm_final_g, loss_target, m_norm_mix_g, m_w_in, m_conv_w, m_conv_b, m_dt_bias, m_a_log, m_d_skip, m_ssm_norm_g, m_v_norm_g, m_v_norm_b, m_w_spatial, m_b_spatial, m_b_gates, m_w_proj_a, m_w_proj_b, m_w_out, m_norm_mlp_g, m_w_mlp_up, m_w_mlp_down, m_norm_final_g, v_norm_mix_g, v_w_in, v_conv_w, v_conv_b, v_dt_bias, v_a_log, v_d_skip, v_ssm_norm_g, v_v_norm_g, v_v_norm_b, v_w_spatial, v_b_spatial, v_b_gates, v_w_proj_a, v_w_proj_b, v_w_out, v_norm_mlp_g, v_w_mlp_up, v_w_mlp_down, v_norm_final_g):
    given = dict(x=x, norm_mix_g=norm_mix_g, w_in=w_in, conv_w=conv_w, conv_b=conv_b, dt_bias=dt_bias, a_log=a_log, d_skip=d_skip, ssm_norm_g=ssm_norm_g, v_norm_g=v_norm_g, v_norm_b=v_norm_b, w_spatial=w_spatial, b_spatial=b_spatial, b_gates=b_gates, w_proj_a=w_proj_a, w_proj_b=w_proj_b, w_out=w_out, norm_mlp_g=norm_mlp_g, w_mlp_up=w_mlp_up, w_mlp_down=w_mlp_down, norm_final_g=norm_final_g, loss_target=loss_target, m_norm_mix_g=m_norm_mix_g, m_w_in=m_w_in, m_conv_w=m_conv_w, m_conv_b=m_conv_b, m_dt_bias=m_dt_bias, m_a_log=m_a_log, m_d_skip=m_d_skip, m_ssm_norm_g=m_ssm_norm_g, m_v_norm_g=m_v_norm_g, m_v_norm_b=m_v_norm_b, m_w_spatial=m_w_spatial, m_b_spatial=m_b_spatial, m_b_gates=m_b_gates, m_w_proj_a=m_w_proj_a, m_w_proj_b=m_w_proj_b, m_w_out=m_w_out, m_norm_mlp_g=m_norm_mlp_g, m_w_mlp_up=m_w_mlp_up, m_w_mlp_down=m_w_mlp_down, m_norm_final_g=m_norm_final_g, v_norm_mix_g=v_norm_mix_g, v_w_in=v_w_in, v_conv_w=v_conv_w, v_conv_b=v_conv_b, v_dt_bias=v_dt_bias, v_a_log=v_a_log, v_d_skip=v_d_skip, v_ssm_norm_g=v_ssm_norm_g, v_v_norm_g=v_v_norm_g, v_v_norm_b=v_v_norm_b, v_w_spatial=v_w_spatial, v_b_spatial=v_b_spatial, v_b_gates=v_b_gates, v_w_proj_a=v_w_proj_a, v_w_proj_b=v_w_proj_b, v_w_out=v_w_out, v_norm_mlp_g=v_norm_mlp_g, v_w_mlp_up=v_w_mlp_up, v_w_mlp_down=v_w_mlp_down, v_norm_final_g=v_norm_final_g)
    weights = {n: given[n] for n in TWIN_WEIGHTS}
    shared = {n: given[n] for n in SHARED_INPUTS}
    per_example = {n: given[n] for n in ['x']}
    grad_fn = _jax.value_and_grad(_loss, argnums=(0, 1))

    def one_microbatch(ex, loss_target):
        ex = dict(ex)
        diff = ex.pop(TWIN_DIFF_INPUT)
        return grad_fn(weights, diff, {**shared, **ex}, loss_target)

    if N_MICROBATCH == 1:
        loss, (grad_w, grad_x) = one_microbatch(per_example, given["loss_target"])
    else:
        def body(carry, xs):
            loss_sum, grad_sum = carry
            l_k, (gw_k, gx_k) = one_microbatch(xs[0], xs[1])
            with _jax.named_scope("update"):
                return (loss_sum + l_k, _jax.tree.map(_jnp.add, grad_sum, gw_k)), gx_k

        init = (_jnp.zeros((), _jnp.float32), _jax.tree.map(_jnp.zeros_like, weights))
        (loss, grad_w), grad_x = _jax.lax.scan(body, init, (per_example, given["loss_target"]))
    with _jax.named_scope("update"):
        delta_w, new_m, new_v = {}, {}, {}
        for n in TWIN_WEIGHTS:
            delta_w[n], new_m[n], new_v[n] = _adamw(weights[n], grad_w[n], given["m_" + n], given["v_" + n])
    return (loss, grad_x, *[grad_w[n] for n in TWIN_WEIGHTS], *[delta_w[n] for n in TWIN_WEIGHTS],
            *[new_m[n] for n in TWIN_WEIGHTS], *[new_v[n] for n in TWIN_WEIGHTS])
```

```python
import functools
import math

import jax
import jax.numpy as jnp
from jax import lax
from jax.experimental import pallas as pl
from jax.experimental.pallas import tpu as pltpu

F32 = jnp.float32
BF16 = jnp.bfloat16
MESH = pl.DeviceIdType.MESH

D_MODEL = 1024
NORM_EPS = 1e-6
CHUNK = 128
GROUPS = 8
D_INNER = 2048
HEAD_DIM = 64
N_HEADS = 32
D_STATE = 128
CONV_WIDTH = 4
CONV_DIM = 4096
D_FF = 4096
GROUP_W = D_INNER // GROUPS
N_DEV = 8
N_CHIP = 4

ADAM_LR = 0.001
ADAM_B1 = 0.9
ADAM_B2 = 0.999
ADAM_EPS = 1e-08
ADAM_WD = 0.01
ADAM_STEP = 10

MAIN_W = 2 * D_MODEL + D_INNER + CONV_DIM + 2 * D_MODEL
COL_Z = 2048
COL_XBC = 4096
COL_GATE = 8192
DT_PAD = 128

LANES = 128
SUBLANES = 8
VMEM_BYTES_V7X = 64 * 1024 * 1024
VMEM_BODY_TEMP = 24 * 1024 * 1024


def _vmem_limit(block_bytes):
    return int(min(2 * block_bytes + VMEM_BODY_TEMP, VMEM_BYTES_V7X - 8 * 1024 * 1024))


def _nbytes(shape, dtype):
    return math.prod(shape) * jnp.dtype(dtype).itemsize


def _params(sem, block_bytes):
    return pltpu.CompilerParams(dimension_semantics=sem, vmem_limit_bytes=_vmem_limit(block_bytes))


def _sigmoid(x):
    return 1.0 / (1.0 + jnp.exp(-x))


def _softplus(x):
    e = jnp.exp(-jnp.abs(x))
    u = 1.0 + e
    log1p_e = jnp.where(u == 1.0, e, jnp.log(u) * (e / jnp.where(u == 1.0, 1.0, u - 1.0)))
    return jnp.maximum(x, 0.0) + log1p_e


_SQRT_HALF = 0.7071067811865476
_INV_SQRT_2PI = 0.3989422804014327


def _gelu(x):
    return x * (lax.erf(x * _SQRT_HALF) + 1.0) * 0.5


def _gelu_grad(x):
    return 0.5 * (1.0 + lax.erf(x * _SQRT_HALF)) + x * jnp.exp(-0.5 * x * x) * _INV_SQRT_2PI


def _dot(a, b, dims):
    return lax.dot_general(a, b, (dims, ((), ())), preferred_element_type=F32)


_NN = ((1,), (0,))
_NT = ((1,), (1,))
_TN = ((0,), (0,))


def _split3(x):
    hi = x.astype(BF16)
    r1 = x - hi.astype(F32)
    mid = r1.astype(BF16)
    lo = (r1 - mid.astype(F32)).astype(BF16)
    return hi, mid, lo


def _dot_exact_rhs(x, e, dims):
    hi, mid, lo = _split3(x)
    return _dot(hi, e, dims) + _dot(mid, e, dims) + _dot(lo, e, dims)


def _dot_exact_lhs(e, x, dims):
    hi, mid, lo = _split3(x)
    return _dot(e, hi, dims) + _dot(e, mid, dims) + _dot(e, lo, dims)


def _tri(lower):
    r = lax.broadcasted_iota(jnp.int32, (CHUNK, CHUNK), 0)
    c = lax.broadcasted_iota(jnp.int32, (CHUNK, CHUNK), 1)
    return (r >= c) if lower else (r <= c)


def _matmul(a, b, *, mode, tm, tn, tk, out_dtypes, name, epilogue=None, extras=(), extra_specs=(), j_outer=False):
    if mode == "nn":
        (m, k), (_, n) = a.shape, b.shape
    elif mode == "nt":
        (m, k), (n, _) = a.shape, b.shape
    else:
        (k, m), (_, n) = a.shape, b.shape
    assert m % tm == 0 and n % tn == 0 and k % tk == 0, (name, m, n, k, tm, tn, tk)
    nk = k // tk
    n_extra, n_out = len(extras), len(out_dtypes)
    dims = {"nn": _NN, "nt": _NT, "tn": _TN}[mode]
    if epilogue is None:
        def epilogue(acc, ex, outs):
            outs[0][...] = acc.astype(outs[0].dtype)

    def body(*refs):
        a_ref, b_ref = refs[0], refs[1]
        ex_refs = refs[2:2 + n_extra]
        outs = refs[2 + n_extra:2 + n_extra + n_out]
        p = _dot(a_ref[...], b_ref[...], dims)
        if nk == 1:
            epilogue(p, ex_refs, outs)
        else:
            acc_ref = refs[2 + n_extra + n_out]
            kk = pl.program_id(2)

            @pl.when(kk == 0)
            def _():
                acc_ref[...] = p

            @pl.when(kk > 0)
            def _():
                acc_ref[...] += p

            @pl.when(kk == nk - 1)
            def _():
                epilogue(acc_ref[...], ex_refs, outs)

    if j_outer:
        grid = (n // tn, m // tm, nk)
        ij = lambda g0, g1: (g1, g0)
    else:
        grid = (m // tm, n // tn, nk)
        ij = lambda g0, g1: (g0, g1)

    def wrap(fn):
        return lambda g0, g1, kk: fn(*ij(g0, g1), kk)

    if mode == "nn":
        a_spec = pl.BlockSpec((tm, tk), wrap(lambda i, j, kk: (i, kk)))
        b_spec = pl.BlockSpec((tk, tn), wrap(lambda i, j, kk: (kk, j)))
        a_blk, b_blk = (tm, tk), (tk, tn)
    elif mode == "nt":
        a_spec = pl.BlockSpec((tm, tk), wrap(lambda i, j, kk: (i, kk)))
        b_spec = pl.BlockSpec((tn, tk), wrap(lambda i, j, kk: (j, kk)))
        a_blk, b_blk = (tm, tk), (tn, tk)
    else:
        a_spec = pl.BlockSpec((tk, tm), wrap(lambda i, j, kk: (kk, i)))
        b_spec = pl.BlockSpec((tk, tn), wrap(lambda i, j, kk: (kk, j)))
        a_blk, b_blk = (tk, tm), (tk, tn)
    ex_specs = [pl.BlockSpec(shape, wrap(lambda i, j, kk, f=f: f(i, j))) for shape, f in extra_specs]
    out_spec = [pl.BlockSpec((tm, tn), wrap(lambda i, j, kk: (i, j))) for _ in out_dtypes]
    out_shape = [jax.ShapeDtypeStruct((m, n), dt) for dt in out_dtypes]
    blk = (_nbytes(a_blk, a.dtype) + _nbytes(b_blk, b.dtype) + sum(_nbytes(s, F32) for s, _ in extra_specs)
           + sum(_nbytes((tm, tn), dt) for dt in out_dtypes) + _nbytes((tm, tn), F32))
    res = pl.pallas_call(
        body, name=name, grid=grid,
        in_specs=[a_spec, b_spec] + ex_specs, out_specs=out_spec, out_shape=out_shape,
        scratch_shapes=[pltpu.VMEM((tm, tn), F32)] if nk > 1 else [],
        compiler_params=_params(("parallel", "parallel", "arbitrary"), blk),
    )(a, b, *extras)
    return res[0] if n_out == 1 else res


ROW_TILE = 256


def _row_spec(width, col_block=0, tile=ROW_TILE):
    return pl.BlockSpec((tile, width), lambda i, cb=col_block: (i, cb))


def _vec_spec(width, col_block=0):
    return pl.BlockSpec((1, width), lambda i, cb=col_block: (0, cb))


def _rms_fwd(x, g, name):
    t = x.shape[0]

    def body(x_ref, g_ref, h_ref):
        xv = x_ref[...]
        r = lax.rsqrt(jnp.mean(xv * xv, axis=-1, keepdims=True) + NORM_EPS)
        h_ref[...] = (xv * r * g_ref[...]).astype(BF16)

    return pl.pallas_call(
        body, name=name, grid=(t // ROW_TILE,),
        in_specs=[_row_spec(D_MODEL), _vec_spec(D_MODEL)], out_specs=_row_spec(D_MODEL),
        out_shape=jax.ShapeDtypeStruct((t, D_MODEL), BF16),
        compiler_params=_params(("parallel",), 3 * _nbytes((ROW_TILE, D_MODEL), F32)),
    )(x, g)


def _rms_bwd(x, g, dh, dres, name):
    t = x.shape[0]

    def body(x_ref, g_ref, dh_ref, dres_ref, dx_ref, dxb_ref, gg_ref):
        xv = x_ref[...]
        r = lax.rsqrt(jnp.mean(xv * xv, axis=-1, keepdims=True) + NORM_EPS)
        xh = xv * r
        dhv = dh_ref[...]
        dyg = dhv * g_ref[...]
        dx = r * (dyg - xh * jnp.mean(dyg * xh, axis=-1, keepdims=True)) + dres_ref[...]
        dx_ref[...] = dx
        dxb_ref[...] = dx.astype(BF16)

        @pl.when(pl.program_id(0) == 0)
        def _():
            gg_ref[...] = jnp.zeros_like(gg_ref)

        gg_ref[...] += jnp.sum(dhv * xh, axis=0, keepdims=True)

    return pl.pallas_call(
        body, name=name, grid=(t // ROW_TILE,),
        in_specs=[_row_spec(D_MODEL), _vec_spec(D_MODEL), _row_spec(D_MODEL), _row_spec(D_MODEL)],
        out_specs=[_row_spec(D_MODEL), _row_spec(D_MODEL), _vec_spec(D_MODEL)],
        out_shape=[jax.ShapeDtypeStruct((t, D_MODEL), F32), jax.ShapeDtypeStruct((t, D_MODEL), BF16),
                   jax.ShapeDtypeStruct((1, D_MODEL), F32)],
        compiler_params=_params(("arbitrary",), 5 * _nbytes((ROW_TILE, D_MODEL), F32)),
    )(x, g, dh, dres)


def _loss_head(x2, gf, target, name):
    t = x2.shape[0]

    def body(x_ref, g_ref, t_ref, loss_ref, dx_ref, dxb_ref, gg_ref, tot_ref):
        xv = x_ref[...]
        gv = g_ref[...]
        r = lax.rsqrt(jnp.mean(xv * xv, axis=-1, keepdims=True) + NORM_EPS)
        xh = xv * r
        err = xh * gv - t_ref[...]
        dy = err * (1.0 / D_MODEL)
        dyg = dy * gv
        dx = r * (dyg - xh * jnp.mean(dyg * xh, axis=-1, keepdims=True))
        dx_ref[...] = dx
        dxb_ref[...] = dx.astype(BF16)

        @pl.when(pl.program_id(0) == 0)
        def _():
            gg_ref[...] = jnp.zeros_like(gg_ref)
            loss_ref[...] = jnp.zeros_like(loss_ref)

        gg_ref[...] += jnp.sum(dy * xh, axis=0, keepdims=True)
        loss_ref[...] += jnp.sum(err * err, axis=0, keepdims=True)
        tot_ref[...] = jnp.broadcast_to(jnp.sum(loss_ref[...], axis=1, keepdims=True) * (0.5 / D_MODEL), tot_ref.shape)

    return pl.pallas_call(
        body, name=name, grid=(t // ROW_TILE,),
        in_specs=[_row_spec(D_MODEL), _vec_spec(D_MODEL), _row_spec(D_MODEL)],
        out_specs=[_vec_spec(D_MODEL), _row_spec(D_MODEL), _row_spec(D_MODEL), _vec_spec(D_MODEL), _vec_spec(LANES)],
        out_shape=[jax.ShapeDtypeStruct((1, D_MODEL), F32), jax.ShapeDtypeStruct((t, D_MODEL), F32),
                   jax.ShapeDtypeStruct((t, D_MODEL), BF16), jax.ShapeDtypeStruct((1, D_MODEL), F32),
                   jax.ShapeDtypeStruct((1, LANES), F32)],
        compiler_params=_params(("arbitrary",), 5 * _nbytes((ROW_TILE, D_MODEL), F32)),
    )(x2, gf, target)


def _merge_fwd(pa, pb, proj, b_gates, name):
    t = pa.shape[0]
    gcb = COL_GATE // D_MODEL

    def body(pa_ref, pb_ref, la_ref, lb_ref, ba_ref, bb_ref, out_ref):
        ga = _sigmoid(la_ref[...] + ba_ref[...])
        gb = _sigmoid(lb_ref[...] + bb_ref[...])
        out_ref[...] = (ga * pa_ref[...] + gb * pb_ref[...]).astype(BF16)

    return pl.pallas_call(
        body, name=name, grid=(t // ROW_TILE,),
        in_specs=[_row_spec(D_MODEL), _row_spec(D_MODEL), _row_spec(D_MODEL, gcb), _row_spec(D_MODEL, gcb + 1),
                  _vec_spec(D_MODEL, 0), _vec_spec(D_MODEL, 1)],
        out_specs=_row_spec(D_MODEL),
        out_shape=jax.ShapeDtypeStruct((t, D_MODEL), BF16),
        compiler_params=_params(("parallel",), 5 * _nbytes((ROW_TILE, D_MODEL), F32)),
    )(pa, pb, proj, proj, b_gates, b_gates)


def _merge_bwd(dmerged, pa, pb, proj, b_gates, name):
    t = pa.shape[0]
    gcb = COL_GATE // D_MODEL

    def body(dm_ref, pa_ref, pb_ref, la_ref, lb_ref, ba_ref, bb_ref, dpa_ref, dpb_ref, dgl_ref, gb_ref):
        dm = dm_ref[...]
        ga = _sigmoid(la_ref[...] + ba_ref[...])
        gb = _sigmoid(lb_ref[...] + bb_ref[...])
        dpa_ref[...] = (dm * ga).astype(BF16)
        dpb_ref[...] = (dm * gb).astype(BF16)
        dla = dm * pa_ref[...] * ga * (1.0 - ga)
        dlb = dm * pb_ref[...] * gb * (1.0 - gb)
        dgl_ref[:, :D_MODEL] = dla.astype(BF16)
        dgl_ref[:, D_MODEL:] = dlb.astype(BF16)

        @pl.when(pl.program_id(0) == 0)
        def _():
            gb_ref[...] = jnp.zeros_like(gb_ref)

        gb_ref[:, :D_MODEL] += jnp.sum(dla, axis=0, keepdims=True)
        gb_ref[:, D_MODEL:] += jnp.sum(dlb, axis=0, keepdims=True)

    return pl.pallas_call(
        body, name=name, grid=(t // ROW_TILE,),
        in_specs=[_row_spec(D_MODEL), _row_spec(D_MODEL), _row_spec(D_MODEL), _row_spec(D_MODEL, gcb),
                  _row_spec(D_MODEL, gcb + 1), _vec_spec(D_MODEL, 0), _vec_spec(D_MODEL, 1)],
        out_specs=[_row_spec(D_MODEL), _row_spec(D_MODEL), _row_spec(2 * D_MODEL, COL_GATE // (2 * D_MODEL)),
                   _vec_spec(2 * D_MODEL)],
        out_shape=[jax.ShapeDtypeStruct((t, D_MODEL), BF16), jax.ShapeDtypeStruct((t, D_MODEL), BF16),
                   jax.ShapeDtypeStruct((t, MAIN_W), BF16), jax.ShapeDtypeStruct((1, 2 * D_MODEL), F32)],
        compiler_params=_params(("arbitrary",), 8 * _nbytes((ROW_TILE, D_MODEL), F32)),
    )(dmerged, pa, pb, proj, proj, b_gates, b_gates)


GMLP_TILE = 512
GMLP_NC = GMLP_TILE // CHUNK


def _gmlp_common(u_pre, v_pre, vg, vb):
    u = _gelu(u_pre)
    v = _gelu(v_pre)
    mu = jnp.mean(v, axis=-1, keepdims=True)
    vc = v - mu
    rstd = lax.rsqrt(jnp.mean(vc * vc, axis=-1, keepdims=True) + NORM_EPS)
    vh = vc * rstd
    vn = vh * vg + vb
    return u, vh, vn, rstd


def _chunks_to_lanes(x, g):
    return jnp.concatenate([x[c * CHUNK:(c + 1) * CHUNK, g * CHUNK:(g + 1) * CHUNK] for c in range(GMLP_NC)], axis=1)


def _gmlp_fwd(proj, vg, vb, wsp, bsp_t, name):
    t = proj.shape[0]

    def body(u_ref, v_ref, vg_ref, vb_ref, w_ref, b_ref, ya_ref):
        u, _, vn, _ = _gmlp_common(u_ref[...], v_ref[...], vg_ref[...], vb_ref[...])
        mask = _tri(True)
        bt = b_ref[...]
        for g in range(GROUPS):
            w = jnp.where(mask, w_ref[g], 0.0).astype(BF16)
            vcat = _chunks_to_lanes(vn, g).astype(BF16)
            s = _dot(w, vcat, _NN) + bt[:, g:g + 1]
            for c in range(GMLP_NC):
                rows, cols = slice(c * CHUNK, (c + 1) * CHUNK), slice(g * CHUNK, (g + 1) * CHUNK)
                ya_ref[rows, cols] = (u[rows, cols] * s[:, c * CHUNK:(c + 1) * CHUNK]).astype(BF16)

    return pl.pallas_call(
        body, name=name, grid=(t // GMLP_TILE,),
        in_specs=[_row_spec(D_MODEL, 0, GMLP_TILE), _row_spec(D_MODEL, 1, GMLP_TILE), _vec_spec(D_MODEL),
                  _vec_spec(D_MODEL), pl.BlockSpec((GROUPS, CHUNK, CHUNK), lambda i: (0, 0, 0)),
                  pl.BlockSpec((CHUNK, GROUPS), lambda i: (0, 0))],
        out_specs=_row_spec(D_MODEL, 0, GMLP_TILE),
        out_shape=jax.ShapeDtypeStruct((t, D_MODEL), BF16),
        compiler_params=_params(("parallel",), 3 * _nbytes((GMLP_TILE, D_MODEL), F32)),
    )(proj, proj, vg, vb, wsp, bsp_t)


def _gmlp_bwd(proj, dya, vg, vb, wsp, bsp_t, dproj, name):
    t = proj.shape[0]

    def body(u_ref, v_ref, dya_ref, vg_ref, vb_ref, w_ref, b_ref, dproj_in, duv_ref, gw_ref, gbt_ref, gvg_ref, gvb_ref,
             dvn_scr, du_scr):
        del dproj_in
        u_pre, v_pre = u_ref[...], v_ref[...]
        vgv = vg_ref[...]
        u, vh, vn, rstd = _gmlp_common(u_pre, v_pre, vgv, vb_ref[...])
        dya = dya_ref[...]
        mask = _tri(True)
        bt = b_ref[...]
        first = pl.program_id(0) == 0

        @pl.when(first)
        def _():
            gw_ref[...] = jnp.zeros_like(gw_ref)
            gbt_ref[...] = jnp.zeros_like(gbt_ref)
            gvg_ref[...] = jnp.zeros_like(gvg_ref)
            gvb_ref[...] = jnp.zeros_like(gvb_ref)

        lane = lax.broadcasted_iota(jnp.int32, (CHUNK, GROUPS), 1)
        gbt = jnp.zeros((CHUNK, GROUPS), F32)
        for g in range(GROUPS):
            w = jnp.where(mask, w_ref[g], 0.0).astype(BF16)
            vcat = _chunks_to_lanes(vn, g).astype(BF16)
            s = _dot(w, vcat, _NN) + bt[:, g:g + 1]
            ds = _chunks_to_lanes(dya * u, g)
            gbt = jnp.where(lane == g, jnp.sum(ds, axis=1, keepdims=True), gbt)
            dsb = ds.astype(BF16)
            gw_ref[g] += jnp.where(mask, _dot(dsb, vcat, _NT), 0.0)
            dv = _dot(w, dsb, _TN)
            for c in range(GMLP_NC):
                rows, cols = slice(c * CHUNK, (c + 1) * CHUNK), slice(g * CHUNK, (g + 1) * CHUNK)
                dvn_scr[rows, cols] = dv[:, c * CHUNK:(c + 1) * CHUNK]
                du_scr[rows, cols] = dya[rows, cols] * s[:, c * CHUNK:(c + 1) * CHUNK]
        gbt_ref[...] += gbt
        dvn = dvn_scr[...]
        gvg_ref[...] += jnp.sum(dvn * vh, axis=0, keepdims=True)
        gvb_ref[...] += jnp.sum(dvn, axis=0, keepdims=True)
        dvh = dvn * vgv
        dv = rstd * (dvh - jnp.mean(dvh, axis=-1, keepdims=True) - vh * jnp.mean(dvh * vh, axis=-1, keepdims=True))
        duv_ref[:, :D_MODEL] = (du_scr[...] * _gelu_grad(u_pre)).astype(BF16)
        duv_ref[:, D_MODEL:] = (dv * _gelu_grad(v_pre)).astype(BF16)

    return pl.pallas_call(
        body, name=name, grid=(t // GMLP_TILE,),
        in_specs=[_row_spec(D_MODEL, 0, GMLP_TILE), _row_spec(D_MODEL, 1, GMLP_TILE), _row_spec(D_MODEL, 0, GMLP_TILE),
                  _vec_spec(D_MODEL), _vec_spec(D_MODEL), pl.BlockSpec((GROUPS, CHUNK, CHUNK), lambda i: (0, 0, 0)),
                  pl.BlockSpec((CHUNK, GROUPS), lambda i: (0, 0)), pl.BlockSpec(memory_space=pl.ANY)],
        out_specs=[_row_spec(2 * D_MODEL, 0, GMLP_TILE), pl.BlockSpec((GROUPS, CHUNK, CHUNK), lambda i: (0, 0, 0)),
                   pl.BlockSpec((CHUNK, GROUPS), lambda i: (0, 0)), _vec_spec(D_MODEL), _vec_spec(D_MODEL)],
        out_shape=[jax.ShapeDtypeStruct(dproj.shape, BF16), jax.ShapeDtypeStruct((GROUPS, CHUNK, CHUNK), F32),
                   jax.ShapeDtypeStruct((CHUNK, GROUPS), F32), jax.ShapeDtypeStruct((1, D_MODEL), F32),
                   jax.ShapeDtypeStruct((1, D_MODEL), F32)],
        scratch_shapes=[pltpu.VMEM((GMLP_TILE, D_MODEL), F32), pltpu.VMEM((GMLP_TILE, D_MODEL), F32)],
        input_output_aliases={7: 0},
        compiler_params=_params(("arbitrary",), 6 * _nbytes((GMLP_TILE, D_MODEL), F32)),
    )(proj, proj, dya, vg, vb, wsp, bsp_t, dproj)


CONV_TILE = 256
CONV_COLS = 512
HALO = SUBLANES


def _conv_taps(xe, cw, first_row, rows):
    acc = None
    for k in range(CONV_WIDTH):
        term = cw[k:k + 1, :] * xe[first_row - (CONV_WIDTH - 1) + k:first_row - (CONV_WIDTH - 1) + k + rows, :]
        acc = term if acc is None else acc + term
    return acc


def _conv_fwd(proj, cw, cb, name):
    t = proj.shape[0]
    nj = CONV_DIM // CONV_COLS
    xcb = COL_XBC // CONV_COLS
    rb = CONV_TILE // HALO

    def body(x_ref, prev_ref, cw_ref, cb_ref, xc_ref):
        i = pl.program_id(1)
        prev = jnp.where(i > 0, prev_ref[...], 0.0)
        xe = jnp.concatenate([prev, x_ref[...]], axis=0)
        pre = _conv_taps(xe, cw_ref[...], HALO, CONV_TILE) + cb_ref[...]
        xc_ref[...] = pre * _sigmoid(pre)

    return pl.pallas_call(
        body, name=name, grid=(nj, t // CONV_TILE),
        in_specs=[pl.BlockSpec((CONV_TILE, CONV_COLS), lambda j, i: (i, xcb + j)),
                  pl.BlockSpec((HALO, CONV_COLS), lambda j, i: (jnp.maximum(i * rb - 1, 0), xcb + j)),
                  pl.BlockSpec((CONV_WIDTH, CONV_COLS), lambda j, i: (0, j)),
                  pl.BlockSpec((1, CONV_COLS), lambda j, i: (0, j))],
        out_specs=pl.BlockSpec((CONV_TILE, CONV_COLS), lambda j, i: (i, j)),
        out_shape=jax.ShapeDtypeStruct((t, CONV_DIM), F32),
        compiler_params=_params(("parallel", "parallel"), 3 * _nbytes((CONV_TILE, CONV_COLS), F32)),
    )(proj, proj, cw, cb)


def _conv_bwd(proj, dxc, cw, cb, dproj, name):
    t = proj.shape[0]
    nj = CONV_DIM // CONV_COLS
    ni = t // CONV_TILE
    xcb = COL_XBC // CONV_COLS
    rb = CONV_TILE // HALO
    last_rb = t // HALO - 1

    def body(x_ref, prev_ref, next_ref, d_ref, dnext_ref, cw_ref, cb_ref, dproj_in, dx_ref, gw_ref, gb_ref):
        del dproj_in
        i = pl.program_id(1)
        cw_v = cw_ref[...]
        prev = jnp.where(i > 0, prev_ref[...], 0.0)
        xe = jnp.concatenate([prev, x_ref[...], next_ref[...]], axis=0)
        pre = _conv_taps(xe, cw_v, HALO, CONV_TILE + HALO) + cb_ref[...]
        de = jnp.concatenate([d_ref[...], jnp.where(i < ni - 1, dnext_ref[...], 0.0)], axis=0)
        sg = _sigmoid(pre)
        dpre_e = de * sg * (1.0 + pre * (1.0 - sg))
        dpre = dpre_e[:CONV_TILE]

        @pl.when(i == 0)
        def _():
            gw_ref[...] = jnp.zeros_like(gw_ref)
            gb_ref[...] = jnp.zeros_like(gb_ref)

        gb_ref[...] += jnp.sum(dpre, axis=0, keepdims=True)
        acc = None
        for k in range(CONV_WIDTH):
            off = HALO - (CONV_WIDTH - 1) + k
            gw_ref[k:k + 1, :] += jnp.sum(dpre * xe[off:off + CONV_TILE, :], axis=0, keepdims=True)
            shift = CONV_WIDTH - 1 - k
            term = cw_v[k:k + 1, :] * dpre_e[shift:shift + CONV_TILE, :]
            acc = term if acc is None else acc + term
        dx_ref[...] = acc.astype(BF16)

    return pl.pallas_call(
        body, name=name, grid=(nj, ni),
        in_specs=[pl.BlockSpec((CONV_TILE, CONV_COLS), lambda j, i: (i, xcb + j)),
                  pl.BlockSpec((HALO, CONV_COLS), lambda j, i: (jnp.maximum(i * rb - 1, 0), xcb + j)),
                  pl.BlockSpec((HALO, CONV_COLS), lambda j, i: (jnp.minimum((i + 1) * rb, last_rb), xcb + j)),
                  pl.BlockSpec((CONV_TILE, CONV_COLS), lambda j, i: (i, j)),
                  pl.BlockSpec((HALO, CONV_COLS), lambda j, i: (jnp.minimum((i + 1) * rb, last_rb), j)),
                  pl.BlockSpec((CONV_WIDTH, CONV_COLS), lambda j, i: (0, j)),
                  pl.BlockSpec((1, CONV_COLS), lambda j, i: (0, j)),
                  pl.BlockSpec(memory_space=pl.ANY)],
        out_specs=[pl.BlockSpec((CONV_TILE, CONV_COLS), lambda j, i: (i, xcb + j)),
                   pl.BlockSpec((CONV_WIDTH, CONV_COLS), lambda j, i: (0, j)),
                   pl.BlockSpec((1, CONV_COLS), lambda j, i: (0, j))],
        out_shape=[jax.ShapeDtypeStruct(dproj.shape, BF16), jax.ShapeDtypeStruct((CONV_WIDTH, CONV_DIM), F32),
                   jax.ShapeDtypeStruct((1, CONV_DIM), F32)],
        input_output_aliases={7: 0},
        compiler_params=_params(("parallel", "arbitrary"), 4 * _nbytes((CONV_TILE, CONV_COLS), F32)),
    )(proj, proj, proj, dxc, dxc, cw, cb, dproj)


def _ssd_decays(dt_raw, dtb, alog, e_bf, tril_bf):
    dtv = _softplus(dt_raw + dtb)
    a = -jnp.exp(alog)
    cs = _dot_exact_lhs(tril_bf, dtv * a, _NN)
    cs_full = _dot_exact_rhs(cs, e_bf, _NN)
    dt_full = _dot_exact_rhs(dtv, e_bf, _NN)
    cs_last = cs[CHUNK - 1:CHUNK, :]
    cs_last_full = cs_full[CHUNK - 1:CHUNK, :]
    return dtv, a, cs, cs_last, cs_full, cs_last_full, dt_full


def _head_mats(cs, cs_t, cb, h, mask):
    seg = cs[:, h:h + 1] - cs_t[h:h + 1, :]
    lmat = jnp.exp(jnp.where(mask, seg, -jnp.inf))
    return lmat, cb * lmat


def _ssd_fwd(xc, proj, dt_raw, dtb, alog, dskip_full, ng, e_bf, name):
    t = xc.shape[0]
    nc = t // CHUNK
    zcb = COL_Z // D_INNER

    def body(xc_ref, z_ref, dt_ref, dtb_ref, alog_ref, dsk_ref, ng_ref, e_ref, y_ref, yb_ref, sprev_ref, s_scr):
        @pl.when(pl.program_id(0) == 0)
        def _():
            s_scr[...] = jnp.zeros_like(s_scr)

        mask = _tri(True)
        tril_bf = mask.astype(BF16)
        e_v = e_ref[...]
        dtv, a, cs, cs_last, cs_full, cs_last_full, dt_full = _ssd_decays(dt_ref[...], dtb_ref[...], alog_ref[...], e_v, tril_bf)
        cs_t = cs.T
        xs = xc_ref[:, :D_INNER]
        xdt = xs * dt_full
        xdec = (xdt * jnp.exp(cs_last_full - cs_full)).astype(BF16)
        xdt_b = xdt.astype(BF16)
        ecs_full = jnp.exp(cs_full)
        cdec_full = jnp.exp(cs_last_full)
        s_prev = s_scr[...]
        sprev_ref[0] = s_prev
        zv = z_ref[...]
        gate = zv * _sigmoid(zv)
        for g in range(GROUPS):
            gc = slice(g * GROUP_W, (g + 1) * GROUP_W)
            bg = xc_ref[:, D_INNER + g * D_STATE:D_INNER + (g + 1) * D_STATE].astype(BF16)
            cg = xc_ref[:, D_INNER + GROUPS * D_STATE + g * D_STATE:D_INNER + GROUPS * D_STATE + (g + 1) * D_STATE].astype(BF16)
            cb = _dot(cg, bg, _NT)
            y_off = ecs_full[:, gc] * _dot(cg, s_prev[:, gc].astype(BF16), _NN)
            s_scr[:, gc] = s_prev[:, gc] * cdec_full[:, gc] + _dot(bg, xdec[:, gc], _TN)
            parts = []
            for r in range(GROUP_W // HEAD_DIM):
                h = g * (GROUP_W // HEAD_DIM) + r
                _, m = _head_mats(cs, cs_t, cb, h, mask)
                parts.append(_dot(m.astype(BF16), xdt_b[:, h * HEAD_DIM:(h + 1) * HEAD_DIM], _NN))
            yg = jnp.concatenate(parts, axis=1) + y_off + dsk_ref[:, gc] * xs[:, gc]
            y_ref[:, gc] = yg
            ygate = yg * gate[:, gc]
            rstd = lax.rsqrt(jnp.mean(ygate * ygate, axis=-1, keepdims=True) + NORM_EPS)
            yb_ref[:, gc] = (ygate * rstd * ng_ref[:, gc]).astype(BF16)

    vec = lambda w: pl.BlockSpec((1, w), lambda i: (0, 0))
    blk = _nbytes((CHUNK, CONV_DIM), F32) + 3 * _nbytes((CHUNK, D_INNER), F32) + _nbytes((D_STATE, D_INNER), F32)
    return pl.pallas_call(
        body, name=name, grid=(nc,),
        in_specs=[pl.BlockSpec((CHUNK, CONV_DIM), lambda i: (i, 0)), pl.BlockSpec((CHUNK, D_INNER), lambda i: (i, zcb)),
                  pl.BlockSpec((CHUNK, DT_PAD), lambda i: (i, 0)), vec(DT_PAD), vec(DT_PAD), vec(D_INNER), vec(D_INNER),
                  pl.BlockSpec((DT_PAD, D_INNER), lambda i: (0, 0))],
        out_specs=[pl.BlockSpec((CHUNK, D_INNER), lambda i: (i, 0)), pl.BlockSpec((CHUNK, D_INNER), lambda i: (i, 0)),
                   pl.BlockSpec((1, D_STATE, D_INNER), lambda i: (i, 0, 0))],
        out_shape=[jax.ShapeDtypeStruct((t, D_INNER), F32), jax.ShapeDtypeStruct((t, D_INNER), BF16),
                   jax.ShapeDtypeStruct((nc, D_STATE, D_INNER), F32)],
        scratch_shapes=[pltpu.VMEM((D_STATE, D_INNER), F32)],
        compiler_params=_params(("arbitrary",), blk),
    )(xc, proj, dt_raw, dtb, alog, dskip_full, ng, e_bf)


def _ssd_bwd(dyb, y, xc, proj, dt_raw, sprev, dtb, alog, dskip_full, ng, e_bf, dproj, name):
    t = xc.shape[0]
    nc = t // CHUNK
    zcb = COL_Z // D_INNER
    hpg = GROUP_W // HEAD_DIM
    rev = lambda i: nc - 1 - i

    def body(dyb_ref, y_ref, xc_ref, z_ref, dt_ref, sprev_ref, dtb_ref, alog_ref, dsk_ref, ng_ref, e_ref, dproj_in,
             dz_ref, dxc_ref, ddt_ref, gng_ref, gdsk_ref, galog_ref, gdtb_ref, ds_scr, dy_scr, dxdt_scr):
        del dproj_in

        @pl.when(pl.program_id(0) == 0)
        def _():
            ds_scr[...] = jnp.zeros_like(ds_scr)
            gng_ref[...] = jnp.zeros_like(gng_ref)
            gdsk_ref[...] = jnp.zeros_like(gdsk_ref)
            galog_ref[...] = jnp.zeros_like(galog_ref)
            gdtb_ref[...] = jnp.zeros_like(gdtb_ref)

        mask = _tri(True)
        tril_bf = mask.astype(BF16)
        triu_bf = _tri(False).astype(BF16)
        e_v = e_ref[...]
        dt_in = dt_ref[...] + dtb_ref[...]
        dtv, a, cs, cs_last, cs_full, cs_last_full, dt_full = _ssd_decays(dt_ref[...], dtb_ref[...], alog_ref[...], e_v, tril_bf)
        cs_t = cs.T
        xs = xc_ref[:, :D_INNER]
        xdt = xs * dt_full
        decay_full = jnp.exp(cs_last_full - cs_full)
        xdec_b = (xdt * decay_full).astype(BF16)
        xdt_b = xdt.astype(BF16)
        ecs_full = jnp.exp(cs_full)
        cdec_full = jnp.exp(cs_last_full)
        s_prev = sprev_ref[0]
        ds_next = ds_scr[...]

        zv = z_ref[...]
        sg = _sigmoid(zv)
        gate = zv * sg
        yv = y_ref[...]
        dybv = dyb_ref[...]
        ngv = ng_ref[...]
        for g in range(GROUPS):
            gc = slice(g * GROUP_W, (g + 1) * GROUP_W)
            ygate = yv[:, gc] * gate[:, gc]
            rstd = lax.rsqrt(jnp.mean(ygate * ygate, axis=-1, keepdims=True) + NORM_EPS)
            yn = ygate * rstd
            gng_ref[:, gc] += jnp.sum(dybv[:, gc] * yn, axis=0, keepdims=True)
            dyn = dybv[:, gc] * ngv[:, gc]
            dyg = rstd * (dyn - yn * jnp.mean(dyn * yn, axis=-1, keepdims=True))
            dz_ref[:, gc] = (dyg * yv[:, gc] * sg[:, gc] * (1.0 + zv[:, gc] * (1.0 - sg[:, gc]))).astype(BF16)
            dy_scr[:, gc] = dyg * gate[:, gc]
        dy = dy_scr[...]
        dyo = dy * ecs_full
        dyo_b = dyo.astype(BF16)
        dy_b = dy.astype(BF16)

        lane_h = lax.broadcasted_iota(jnp.int32, (CHUNK, DT_PAD), 1)
        sub_h = lax.broadcasted_iota(jnp.int32, (DT_PAD, CHUNK), 0)
        dcs_rows = jnp.zeros((CHUNK, DT_PAD), F32)
        dcs_cols_t = jnp.zeros((DT_PAD, CHUNK), F32)
        dec_cols = []
        for g in range(GROUPS):
            gc = slice(g * GROUP_W, (g + 1) * GROUP_W)
            b_cols = slice(D_INNER + g * D_STATE, D_INNER + (g + 1) * D_STATE)
            c_cols = slice(D_INNER + GROUPS * D_STATE + g * D_STATE, D_INNER + GROUPS * D_STATE + (g + 1) * D_STATE)
            bg = xc_ref[:, b_cols].astype(BF16)
            cg = xc_ref[:, c_cols].astype(BF16)
            sg_prev_b = s_prev[:, gc].astype(BF16)
            dsg = ds_next[:, gc]
            dsg_b = dsg.astype(BF16)
            cb = _dot(cg, bg, _NT)
            c_s = _dot(cg, sg_prev_b, _NN)
            b_ds = _dot(bg, dsg_b, _NN)
            dcb = jnp.zeros((CHUNK, CHUNK), F32)
            parts = []
            for r in range(hpg):
                h = g * hpg + r
                hc = slice(h * HEAD_DIM, (h + 1) * HEAD_DIM)
                lmat, m = _head_mats(cs, cs_t, cb, h, mask)
                dm = _dot(dy_b[:, hc], xdt_b[:, hc], _NT)
                parts.append(_dot(m.astype(BF16), dy_b[:, hc], _TN))
                dcb = dcb + dm * lmat
                w = dm * m
                dcs_rows = jnp.where(lane_h == h, jnp.sum(w, axis=1, keepdims=True), dcs_rows)
                dcs_cols_t = jnp.where(sub_h == h, jnp.sum(w, axis=0, keepdims=True), dcs_cols_t)
            dxdt_scr[:, gc] = jnp.concatenate(parts, axis=1) + decay_full[:, gc] * b_ds
            dcb_b = dcb.astype(BF16)
            dxc_ref[:, c_cols] = _dot(dcb_b, bg, _NN) + _dot(dyo_b[:, gc], sg_prev_b, _NT)
            dxc_ref[:, b_cols] = _dot(dcb_b, cg, _TN) + _dot(xdec_b[:, gc], dsg_b, _NT)
            ds_scr[:, gc] = _dot(cg, dyo_b[:, gc], _TN) + cdec_full[:, gc] * dsg
            dec_prod = xdt[:, gc] * decay_full[:, gc] * b_ds
            dec_cols.append(jnp.sum(dec_prod, axis=0, keepdims=True))
            dy_scr[:, gc] = dyo[:, gc] * c_s - dec_prod
        dxdt = dxdt_scr[...]
        dxc_ref[:, :D_INNER] = dxdt * dt_full + dsk_ref[...] * dy
        t_cs = _dot_exact_rhs(dy_scr[...], e_v, _NT)
        t_dt = _dot_exact_rhs(dxdt * xs, e_v, _NT)
        sd = jnp.sum(ds_next * s_prev, axis=0, keepdims=True)
        last_full = jnp.concatenate(dec_cols, axis=1) + cdec_full * sd
        tail = jnp.concatenate([last_full, jnp.sum(dy * xs, axis=0, keepdims=True),
                                jnp.zeros((SUBLANES - 2, D_INNER), F32)], axis=0)
        t_tail = _dot_exact_rhs(tail, e_v, _NT)
        gdsk_ref[...] += t_tail[1:2, :]
        row = lax.broadcasted_iota(jnp.int32, (CHUNK, DT_PAD), 0)
        dcs = dcs_rows - dcs_cols_t.T + t_cs + jnp.where(row == CHUNK - 1, t_tail[0:1, :], 0.0)
        dda = _dot_exact_lhs(triu_bf, dcs, _NN)
        galog_ref[...] += jnp.sum(dda * dtv, axis=0, keepdims=True) * a
        ddt = dda * a + t_dt
        ddt_raw = jnp.where(lane_h < N_HEADS, ddt * _sigmoid(dt_in), 0.0)
        gdtb_ref[...] += jnp.sum(ddt_raw, axis=0, keepdims=True)
        ddt_ref[...] = ddt_raw.astype(BF16)

    vec = lambda w: pl.BlockSpec((1, w), lambda i: (0, 0))
    blk = (2 * _nbytes((CHUNK, CONV_DIM), F32) + 4 * _nbytes((CHUNK, D_INNER), F32) + 4 * _nbytes((D_STATE, D_INNER), F32))
    return pl.pallas_call(
        body, name=name, grid=(nc,),
        in_specs=[pl.BlockSpec((CHUNK, D_INNER), lambda i: (rev(i), 0)), pl.BlockSpec((CHUNK, D_INNER), lambda i: (rev(i), 0)),
                  pl.BlockSpec((CHUNK, CONV_DIM), lambda i: (rev(i), 0)), pl.BlockSpec((CHUNK, D_INNER), lambda i: (rev(i), zcb)),
                  pl.BlockSpec((CHUNK, DT_PAD), lambda i: (rev(i), 0)), pl.BlockSpec((1, D_STATE, D_INNER), lambda i: (rev(i), 0, 0)),
                  vec(DT_PAD), vec(DT_PAD), vec(D_INNER), vec(D_INNER), pl.BlockSpec((DT_PAD, D_INNER), lambda i: (0, 0)),
                  pl.BlockSpec(memory_space=pl.ANY)],
        out_specs=[pl.BlockSpec((CHUNK, D_INNER), lambda i: (rev(i), zcb)), pl.BlockSpec((CHUNK, CONV_DIM), lambda i: (rev(i), 0)),
                   pl.BlockSpec((CHUNK, DT_PAD), lambda i: (rev(i), 0)), vec(D_INNER), vec(DT_PAD), vec(DT_PAD), vec(DT_PAD)],
        out_shape=[jax.ShapeDtypeStruct(dproj.shape, BF16), jax.ShapeDtypeStruct((t, CONV_DIM), F32),
                   jax.ShapeDtypeStruct((t, DT_PAD), BF16), jax.ShapeDtypeStruct((1, D_INNER), F32),
                   jax.ShapeDtypeStruct((1, DT_PAD), F32), jax.ShapeDtypeStruct((1, DT_PAD), F32),
                   jax.ShapeDtypeStruct((1, DT_PAD), F32)],
        scratch_shapes=[pltpu.VMEM((D_STATE, D_INNER), F32), pltpu.VMEM((CHUNK, D_INNER), F32),
                        pltpu.VMEM((CHUNK, D_INNER), F32)],
        input_output_aliases={11: 0},
        compiler_params=_params(("arbitrary",), blk),
    )(dyb, y, xc, proj, dt_raw, sprev, dtb, alog, dskip_full, ng, e_bf, dproj)


_HBM = pl.BlockSpec(memory_space=pl.ANY)


def _mesh_pos():
    return lax.axis_index("x"), lax.axis_index("y"), lax.axis_index("c")


def _other_chips(x, y):
    return [(1 - x, y), (x, 1 - y), (1 - x, 1 - y)]


def _all_gather(shards, name):
    n = len(shards)

    def body(*refs):
        ins, outs = refs[:n], refs[n:2 * n]
        send_sems, recv_sems, local_sems = refs[2 * n:]
        x, y, c = _mesh_pos()
        me, sibling = (x, y, c), (x, y, 1 - c)
        chips = _other_chips(x, y)

        def slot(p):
            return 4 * p[0] + 2 * p[1] + p[2]

        def copy(a, k, block, to, src=None):
            dst = outs[a].at[slot(block)]
            return pltpu.make_async_remote_copy(
                src_ref=dst if src is None else src, dst_ref=dst, send_sem=send_sems.at[a * 7 + k],
                recv_sem=recv_sems.at[a * 7 + k], device_id=to, device_id_type=MESH)

        started = []
        own = []
        for a in range(n):
            mine = pltpu.make_async_copy(ins[a], outs[a].at[slot(me)], local_sems.at[a])
            mine.start()
            own.append(mine)
            first = [copy(a, 0, me, sibling, src=ins[a])]
            first += [copy(a, 1 + j, me, (*chip, c), src=ins[a]) for j, chip in enumerate(chips)]
            for cp in first:
                cp.start()
            started += first
        for a in range(n):
            for j, chip in enumerate(chips):
                copy(a, 1 + j, (*chip, c), me).wait_recv()
                fwd = copy(a, 4 + j, (*chip, c), sibling)
                fwd.start()
                started.append(fwd)
        for a in range(n):
            copy(a, 0, sibling, me).wait_recv()
            for j, chip in enumerate(chips):
                copy(a, 4 + j, (*chip, 1 - c), me).wait_recv()
        for cp in started:
            cp.wait_send()
        for mine in own:
            mine.wait()

    return pl.pallas_call(
        body, name=name,
        in_specs=[_HBM] * n, out_specs=[_HBM] * n,
        out_shape=[jax.ShapeDtypeStruct((N_DEV,) + s.shape, s.dtype) for s in shards],
        scratch_shapes=[pltpu.SemaphoreType.DMA((7 * n,)), pltpu.SemaphoreType.DMA((7 * n,)),
                        pltpu.SemaphoreType.DMA((n,))],
    )(*shards)


def _exchange_all(packed, name):
    def body(in_ref, out_ref, send_sems, recv_sems, local_sem):
        x, y, c = _mesh_pos()
        my_slot = 4 * x + 2 * y + c
        mine = pltpu.make_async_copy(in_ref, out_ref.at[my_slot], local_sem)
        mine.start()
        copies = []
        for k in range(1, N_DEV):
            fx, fy, fc = (k >> 2) & 1, (k >> 1) & 1, k & 1
            peer = (x + fx - 2 * x * fx, y + fy - 2 * y * fy, c + fc - 2 * c * fc)
            peer_slot = 4 * peer[0] + 2 * peer[1] + peer[2]
            send = pltpu.make_async_remote_copy(
                src_ref=in_ref, dst_ref=out_ref.at[my_slot], send_sem=send_sems.at[k - 1], recv_sem=recv_sems.at[k - 1],
                device_id=peer, device_id_type=MESH)
            send.start()
            recv = pltpu.make_async_remote_copy(
                src_ref=in_ref, dst_ref=out_ref.at[peer_slot], send_sem=send_sems.at[k - 1], recv_sem=recv_sems.at[k - 1],
                device_id=peer, device_id_type=MESH)
            copies.append((send, recv))
        for send, recv in copies:
            send.wait_send()
            recv.wait_recv()
        mine.wait()

    return pl.pallas_call(
        body, name=name, in_specs=[_HBM], out_specs=_HBM,
        out_shape=jax.ShapeDtypeStruct((N_DEV,) + packed.shape, packed.dtype),
        scratch_shapes=[pltpu.SemaphoreType.DMA((N_DEV - 1,)), pltpu.SemaphoreType.DMA((N_DEV - 1,)),
                        pltpu.SemaphoreType.DMA],
    )(packed)


def _swap_with_sibling(grads, name):
    n = len(grads)

    def body(*refs):
        ins, outs = refs[:n], refs[n:2 * n]
        send_sems, recv_sems = refs[2 * n:]
        x, y, c = _mesh_pos()
        copies = []
        for a in range(n):
            for k in range(N_CHIP):
                cp = pltpu.make_async_remote_copy(
                    src_ref=ins[a].at[(1 - c) + 2 * k], dst_ref=outs[a].at[k], send_sem=send_sems.at[a * N_CHIP + k],
                    recv_sem=recv_sems.at[a * N_CHIP + k], device_id=(x, y, 1 - c), device_id_type=MESH)
                cp.start()
                copies.append(cp)
        for cp in copies:
            cp.wait()

    return pl.pallas_call(
        body, name=name, in_specs=[_HBM] * n, out_specs=[_HBM] * n,
        out_shape=[jax.ShapeDtypeStruct((N_CHIP,) + g.shape[1:], g.dtype) for g in grads],
        scratch_shapes=[pltpu.SemaphoreType.DMA((N_CHIP * n,)), pltpu.SemaphoreType.DMA((N_CHIP * n,))],
    )(*grads)


def _scatter_to_chips(parts, name):
    n = len(parts)

    def body(*refs):
        ins, outs = refs[:n], refs[n:2 * n]
        send_sems, recv_sems, local_sems = refs[2 * n:]
        x, y, c = _mesh_pos()
        my_chip = 2 * x + y
        own, copies = [], []
        for a in range(n):
            mine = pltpu.make_async_copy(ins[a].at[my_chip], outs[a].at[my_chip], local_sems.at[a])
            mine.start()
            own.append(mine)
            for j, chip in enumerate(_other_chips(x, y)):
                send = pltpu.make_async_remote_copy(
                    src_ref=ins[a].at[2 * chip[0] + chip[1]], dst_ref=outs[a].at[my_chip], send_sem=send_sems.at[a * 3 + j],
                    recv_sem=recv_sems.at[a * 3 + j], device_id=(*chip, c), device_id_type=MESH)
                send.start()
                recv = pltpu.make_async_remote_copy(
                    src_ref=ins[a].at[my_chip], dst_ref=outs[a].at[2 * chip[0] + chip[1]], send_sem=send_sems.at[a * 3 + j],
                    recv_sem=recv_sems.at[a * 3 + j], device_id=(*chip, c), device_id_type=MESH)
                copies.append((send, recv))
        for send, recv in copies:
            send.wait_send()
            recv.wait_recv()
        for mine in own:
            mine.wait()

    return pl.pallas_call(
        body, name=name, in_specs=[_HBM] * n, out_specs=[_HBM] * n,
        out_shape=[jax.ShapeDtypeStruct(p.shape, p.dtype) for p in parts],
        scratch_shapes=[pltpu.SemaphoreType.DMA((3 * n,)), pltpu.SemaphoreType.DMA((3 * n,)),
                        pltpu.SemaphoreType.DMA((n,))],
    )(*parts)


def _row_block(rows, cols, slots):
    budget = 2 * 1024 * 1024
    if rows % SUBLANES:
        return rows
    br = rows
    while br % 2 == 0 and (br // 2) % SUBLANES == 0 and slots * br * cols * 4 > budget:
        br //= 2
    return br


def _add_sibling(grads, recv, c_idx, name):
    _, rows, cols = grads.shape
    br = _row_block(rows, cols, 3)

    def body(c_ref, g_ref, r_ref, out_ref):
        del c_ref
        out_ref[...] = g_ref[...] + r_ref[...]

    grid_spec = pltpu.PrefetchScalarGridSpec(
        num_scalar_prefetch=1, grid=(N_CHIP, rows // br),
        in_specs=[pl.BlockSpec((1, br, cols), lambda k, i, c_ref: (c_ref[0] + 2 * k, i, 0)),
                  pl.BlockSpec((1, br, cols), lambda k, i, c_ref: (k, i, 0))],
        out_specs=pl.BlockSpec((1, br, cols), lambda k, i, c_ref: (k, i, 0)))
    return pl.pallas_call(
        body, name=name, grid_spec=grid_spec, out_shape=jax.ShapeDtypeStruct((N_CHIP, rows, cols), F32),
        compiler_params=_params(("parallel", "parallel"), 3 * _nbytes((br, cols), F32)),
    )(c_idx, grads, recv)


def _adamw(slots, w, m, v, name):
    ns, rows, cols = slots.shape
    br = _row_block(rows, cols, ns + 7)
    c1 = 1.0 / (1.0 - ADAM_B1 ** ADAM_STEP)
    c2 = 1.0 / (1.0 - ADAM_B2 ** ADAM_STEP)

    def body(s_ref, w_ref, m_ref, v_ref, g_ref, d_ref, m2_ref, v2_ref):
        g = s_ref[0]
        for k in range(1, ns):
            g = g + s_ref[k]
        m2 = ADAM_B1 * m_ref[...] + (1.0 - ADAM_B1) * g
        v2 = ADAM_B2 * v_ref[...] + (1.0 - ADAM_B2) * (g * g)
        g_ref[...] = g
        m2_ref[...] = m2
        v2_ref[...] = v2
        d_ref[...] = -ADAM_LR * ((m2 * c1) / (jnp.sqrt(v2 * c2) + ADAM_EPS) + ADAM_WD * w_ref[...])

    blk = pl.BlockSpec((br, cols), lambda i: (i, 0))
    return pl.pallas_call(
        body, name=name, grid=(rows // br,),
        in_specs=[pl.BlockSpec((ns, br, cols), lambda i: (0, i, 0)), blk, blk, blk],
        out_specs=[blk, blk, blk, blk],
        out_shape=[jax.ShapeDtypeStruct((rows, cols), F32)] * 4,
        compiler_params=_params(("parallel",), (ns + 7) * _nbytes((br, cols), F32)),
    )(slots, w, m, v)


_SMALL = ["norm_mix_g", "conv_b", "dt_bias", "a_log", "d_skip", "ssm_norm_g", "v_norm_g", "v_norm_b", "w_spatial",
          "b_spatial", "b_gates", "norm_mlp_g", "norm_final_g"]
_SHARDED = ["w_in", "w_proj_a", "w_proj_b", "w_out", "w_mlp_up", "w_mlp_down"]


def _pack(arrays):
    flat = []
    for arr in arrays:
        f = arr.reshape(-1).astype(F32)
        pad = (-f.shape[0]) % LANES
        flat.append(jnp.pad(f, (0, pad)) if pad else f)
    out = jnp.concatenate(flat)
    pad = (-out.shape[0]) % (SUBLANES * LANES)
    if pad:
        out = jnp.pad(out, (0, pad))
    return out.reshape(-1, LANES)


def _unpack(packed, shapes):
    flat = packed.reshape(-1)
    out, off = [], 0
    for shape in shapes:
        size = math.prod(shape)
        out.append(flat[off:off + size].reshape(shape))
        off += size + ((-size) % LANES)
    return out


def _mm_tiles(m, n, k):
    return min(m, 512), min(n, 1024), min(k, 1024)


def _local_step(x, target, wts, small):
    t = x.shape[0]
    w_main, w_dt = wts["w_main"], wts["w_dt"]
    bsp_t = small["b_spatial"].T
    pad32 = lambda a: jnp.pad(a, ((0, 0), (0, DT_PAD - N_HEADS)))
    dtb, alog = pad32(small["dt_bias"]), pad32(small["a_log"])
    dskip_full = jnp.repeat(small["d_skip"], HEAD_DIM, axis=1)
    head_of_col = lax.broadcasted_iota(jnp.int32, (DT_PAD, D_INNER), 1) // HEAD_DIM
    e_bf = (head_of_col == lax.broadcasted_iota(jnp.int32, (DT_PAD, D_INNER), 0)).astype(BF16)

    def mm(a, b, mode, name, **kw):
        if mode == "nn":
            m, k, n = a.shape[0], a.shape[1], b.shape[1]
        elif mode == "nt":
            m, k, n = a.shape[0], a.shape[1], b.shape[0]
        else:
            m, k, n = a.shape[1], a.shape[0], b.shape[1]
        tm, tn, tk = _mm_tiles(m, n, k)
        if mode == "tn":
            tm, tn, tk = min(m, 1024), min(n, 1024), min(k, 512)
        kw.setdefault("out_dtypes", (F32,))
        return _matmul(a, b, mode=mode, tm=tm, tn=tn, tk=tk, name=name, **kw)

    tile = lambda w: ((min(t, 512), w), lambda i, j: (i, j))

    h = _rms_fwd(x, small["norm_mix_g"], "rms_mix")
    proj = mm(h, w_main, "nn", "proj_main", j_outer=True)
    dt_raw = mm(h, w_dt, "nn", "proj_dt")
    y_a = _gmlp_fwd(proj, small["v_norm_g"], small["v_norm_b"], small["w_spatial"], bsp_t, "gmlp_fwd")
    xc = _conv_fwd(proj, wts["conv_w"], small["conv_b"], "conv_fwd")
    y_ssd, y_b, sprev = _ssd_fwd(xc, proj, dt_raw, dtb, alog, dskip_full, small["ssm_norm_g"], e_bf, "ssd_fwd")
    pa = mm(y_a, wts["w_proj_a"], "nn", "proj_a")
    pb = mm(y_b, wts["w_proj_b"], "nn", "proj_b")
    merged = _merge_fwd(pa, pb, proj, small["b_gates"], "merge_fwd")

    def add_residual(acc, ex, outs):
        outs[0][...] = acc + ex[0][...]

    x1 = mm(merged, wts["w_out"], "nn", "out_proj", epilogue=add_residual, extras=(x,), extra_specs=(tile(1024),))
    h2 = _rms_fwd(x1, small["norm_mlp_g"], "rms_mlp")

    def relu_sq(acc, ex, outs):
        outs[0][...] = acc
        r = jnp.maximum(acc, 0.0)
        outs[1][...] = (r * r).astype(BF16)

    up, act = mm(h2, wts["w_mlp_up"], "nn", "mlp_up", epilogue=relu_sq, out_dtypes=(F32, BF16), j_outer=True)
    x2 = mm(act, wts["w_mlp_down"], "nn", "mlp_down", epilogue=add_residual, extras=(x1,), extra_specs=(tile(1024),))
    _, dx2, dx2_b, g_final, loss = _loss_head(x2, small["norm_final_g"], target, "loss_head")

    def relu_sq_bwd(acc, ex, outs):
        outs[0][...] = (acc * 2.0 * jnp.maximum(ex[0][...], 0.0)).astype(BF16)

    dup = mm(dx2_b, wts["w_mlp_down"], "nt", "d_act", epilogue=relu_sq_bwd, extras=(up,), extra_specs=(tile(1024),),
             out_dtypes=(BF16,), j_outer=True)
    g_down = mm(act, dx2_b, "tn", "g_mlp_down")
    g_up = mm(h2, dup, "tn", "g_mlp_up")
    dh2 = mm(dup, wts["w_mlp_up"], "nt", "d_h2")
    dx1, dx1_b, g_mlp = _rms_bwd(x1, small["norm_mlp_g"], dh2, dx2, "rms_mlp_bwd")

    g_out = mm(merged, dx1_b, "tn", "g_out")
    dmerged = mm(dx1_b, wts["w_out"], "nt", "d_merged")
    dpa, dpb, dproj, g_bgates = _merge_bwd(dmerged, pa, pb, proj, small["b_gates"], "merge_bwd")
    g_pa = mm(y_a, dpa, "tn", "g_proj_a")
    g_pb = mm(y_b, dpb, "tn", "g_proj_b")
    dya = mm(dpa, wts["w_proj_a"], "nt", "d_ya")
    dyb = mm(dpb, wts["w_proj_b"], "nt", "d_yb")

    dproj, g_wsp, g_bsp_t, g_vg, g_vb = _gmlp_bwd(proj, dya, small["v_norm_g"], small["v_norm_b"], small["w_spatial"],
                                                   bsp_t, dproj, "gmlp_bwd")
    dproj, dxc, ddt, g_ng, g_dskip, g_alog, g_dtb = _ssd_bwd(dyb, y_ssd, xc, proj, dt_raw, sprev, dtb, alog, dskip_full,
                                                             small["ssm_norm_g"], e_bf, dproj, "ssd_bwd")
    dproj, g_convw, g_convb = _conv_bwd(proj, dxc, wts["conv_w"], small["conv_b"], dproj, "conv_bwd")

    g_main = mm(h, dproj, "tn", "g_in_main")
    g_dt = mm(h, ddt, "tn", "g_in_dt")

    def add_dt(acc, ex, outs):
        outs[0][...] = acc + _dot(ex[0][...], ex[1][...], _NT)

    dh = mm(dproj, w_main, "nt", "d_h", epilogue=add_dt, extras=(ddt, w_dt),
            extra_specs=(((min(t, 512), DT_PAD), lambda i, j: (i, 0)), ((D_MODEL, DT_PAD), lambda i, j: (0, 0))))
    grad_x, _, g_mix = _rms_bwd(x, small["norm_mix_g"], dh, dx1, "rms_mix_bwd")

    grads = {
        "w_main": g_main, "w_dt": g_dt, "w_proj_a": g_pa, "w_proj_b": g_pb, "w_out": g_out, "w_mlp_up": g_up,
        "w_mlp_down": g_down, "conv_w": g_convw,
        "norm_mix_g": g_mix, "conv_b": g_convb, "dt_bias": g_dtb[:, :N_HEADS], "a_log": g_alog[:, :N_HEADS],
        "d_skip": g_dskip[:, :N_HEADS], "ssm_norm_g": g_ng, "v_norm_g": g_vg, "v_norm_b": g_vb, "w_spatial": g_wsp,
        "b_spatial": g_bsp_t.T, "b_gates": g_bgates, "norm_mlp_g": g_mlp, "norm_final_g": g_final,
    }
    return loss, grad_x, grads


def _split_w_in(w_full):
    dt0 = COL_GATE
    w_main = jnp.concatenate([w_full[:, :dt0], w_full[:, dt0 + N_HEADS:]], axis=1)
    w_dt = jnp.pad(w_full[:, dt0:dt0 + N_HEADS], ((0, 0), (0, DT_PAD - N_HEADS)))
    return w_main, w_dt


def _join_w_in(g_main, g_dt):
    dt0 = COL_GATE
    return jnp.concatenate([g_main[:, :dt0], g_dt[:, :N_HEADS], g_main[:, dt0:]], axis=1)


def kernel(x, norm_mix_g, w_in, conv_w, conv_b, dt_bias, a_log, d_skip, ssm_norm_g, v_norm_g, v_norm_b, w_spatial, b_spatial, b_gates, w_proj_a, w_proj_b, w_out, norm_mlp_g, w_mlp_up, w_mlp_down, norm_final_g, loss_target, m_norm_mix_g, m_w_in, m_conv_w, m_conv_b, m_dt_bias, m_a_log, m_d_skip, m_ssm_norm_g, m_v_norm_g, m_v_norm_b, m_w_spatial, m_b_spatial, m_b_gates, m_w_proj_a, m_w_proj_b, m_w_out, m_norm_mlp_g, m_w_mlp_up, m_w_mlp_down, m_norm_final_g, v_norm_mix_g, v_w_in, v_conv_w, v_conv_b, v_dt_bias, v_a_log, v_d_skip, v_ssm_norm_g, v_v_norm_g, v_v_norm_b, v_w_spatial, v_b_spatial, v_b_gates, v_w_proj_a, v_w_proj_b, v_w_out, v_norm_mlp_g, v_w_mlp_up, v_w_mlp_down, v_norm_final_g):
    given = dict(locals())
    names = ["norm_mix_g", "w_in", "conv_w", "conv_b", "dt_bias", "a_log", "d_skip", "ssm_norm_g", "v_norm_g", "v_norm_b",
             "w_spatial", "b_spatial", "b_gates", "w_proj_a", "w_proj_b", "w_out", "norm_mlp_g", "w_mlp_up", "w_mlp_down",
             "norm_final_g"]
    shapes = {n: given[n].shape for n in names}
    t = x.shape[1]
    dev = 4 * lax.axis_index("x") + 2 * lax.axis_index("y") + lax.axis_index("c")

    shard2d = {"w_in": w_in[0], "w_proj_a": w_proj_a[0], "w_proj_b": w_proj_b[0], "w_out": w_out[0],
               "w_mlp_up": w_mlp_up[0], "w_mlp_down": w_mlp_down[0]}
    conv_shard = conv_w.reshape(CONV_WIDTH, -1)
    gathered = _all_gather([shard2d[n].astype(BF16) for n in _SHARDED] + [conv_shard], "gather_weights")
    gw = dict(zip(_SHARDED, gathered[:-1]))
    by_cols = lambda g: jnp.transpose(g, (1, 0, 2)).reshape(g.shape[1], -1)
    w_main, w_dt = _split_w_in(by_cols(gw["w_in"]))
    wts = {"w_main": w_main, "w_dt": w_dt, "w_proj_a": gw["w_proj_a"].reshape(-1, D_MODEL),
           "w_proj_b": gw["w_proj_b"].reshape(-1, D_MODEL), "w_out": gw["w_out"].reshape(-1, D_MODEL),
           "w_mlp_up": by_cols(gw["w_mlp_up"]), "w_mlp_down": gw["w_mlp_down"].reshape(-1, D_MODEL),
           "conv_w": by_cols(gathered[-1])}
    small = {"norm_mix_g": norm_mix_g, "conv_b": conv_b, "dt_bias": dt_bias, "a_log": a_log, "d_skip": d_skip,
             "ssm_norm_g": ssm_norm_g, "v_norm_g": v_norm_g, "v_norm_b": v_norm_b, "w_spatial": w_spatial[0],
             "b_spatial": b_spatial[0], "b_gates": b_gates, "norm_mlp_g": norm_mlp_g,
             "norm_final_g": norm_final_g.reshape(1, -1)}

    loss_part, grad_x, grads = _local_step(x[0], loss_target[0], wts, small)

    by_dev_cols = lambda g: jnp.transpose(g.reshape(g.shape[0], N_DEV, -1), (1, 0, 2))
    by_dev_rows = lambda g: g.reshape(N_DEV, -1, g.shape[1])
    full = {"w_in": by_dev_cols(_join_w_in(grads["w_main"], grads["w_dt"])), "w_proj_a": by_dev_rows(grads["w_proj_a"]),
            "w_proj_b": by_dev_rows(grads["w_proj_b"]), "w_out": by_dev_rows(grads["w_out"]),
            "w_mlp_up": by_dev_cols(grads["w_mlp_up"]), "w_mlp_down": by_dev_rows(grads["w_mlp_down"])}
    from_sibling = _swap_with_sibling([full[n] for n in _SHARDED], "reduce_cores")
    c_idx = lax.axis_index("c").astype(jnp.int32).reshape(1)
    chip_part = [_add_sibling(full[n], r, c_idx, "add_cores_" + n) for n, r in zip(_SHARDED, from_sibling)]
    by_chip = _scatter_to_chips(chip_part, "reduce_chips")
    out = {}
    for n, slots in zip(_SHARDED, by_chip):
        res = _adamw(slots, shard2d[n], given["m_" + n][0], given["v_" + n][0], "adamw_" + n)
        out[n] = [r.reshape(shapes[n]) for r in res]

    small_shapes = [shapes[n] for n in _SMALL]
    extra = [grads["conv_w"], loss_part]
    packed_g = _pack([grads[n] for n in _SMALL] + extra)
    zeros_extra = [jnp.zeros_like(e) for e in extra]
    all_g = _exchange_all(packed_g, "exchange_small")
    res = _adamw(all_g, _pack([given[n] for n in _SMALL] + zeros_extra), _pack([given["m_" + n] for n in _SMALL] + zeros_extra),
                 _pack([given["v_" + n] for n in _SMALL] + zeros_extra), "adamw_small")
    extra_shapes = [grads["conv_w"].shape, loss_part.shape]
    unpacked = [_unpack(r, small_shapes + extra_shapes) for r in res]
    for i, n in enumerate(_SMALL):
        out[n] = [u[i] for u in unpacked]
    g_conv_full, loss_all = unpacked[0][-2], unpacked[0][-1]
    width = shapes["conv_w"][-1]
    g_conv = lax.dynamic_slice(g_conv_full, (0, dev * width), (CONV_WIDTH, width))
    res = _adamw(g_conv[None], conv_shard, m_conv_w.reshape(CONV_WIDTH, -1), v_conv_w.reshape(CONV_WIDTH, -1), "adamw_conv_w")
    out["conv_w"] = [r.reshape(shapes["conv_w"]) for r in res]

    loss = loss_all[0, 0]
    return (loss, grad_x[None], *[out[n][0] for n in names], *[out[n][1] for n in names],
            *[out[n][2] for n in names], *[out[n][3] for n in names])
```

```python
import functools
import math

import jax
import jax.numpy as jnp
from jax import lax
from jax.experimental import pallas as pl
from jax.experimental.pallas import tpu as pltpu

F32 = jnp.float32
BF16 = jnp.bfloat16
MESH = pl.DeviceIdType.MESH

D_MODEL = 1024
NORM_EPS = 1e-6
CHUNK = 128
GROUPS = 8
D_INNER = 2048
HEAD_DIM = 64
N_HEADS = 32
D_STATE = 128
CONV_WIDTH = 4
CONV_DIM = 4096
D_FF = 4096
GROUP_W = D_INNER // GROUPS
N_DEV = 8
N_CHIP = 4

ADAM_LR = 0.001
ADAM_B1 = 0.9
ADAM_B2 = 0.999
ADAM_EPS = 1e-08
ADAM_WD = 0.01
ADAM_STEP = 10

MAIN_W = 2 * D_MODEL + D_INNER + CONV_DIM + 2 * D_MODEL
COL_Z = 2048
COL_XBC = 4096
COL_GATE = 8192
DT_PAD = 128

LANES = 128
SUBLANES = 8
VMEM_BYTES_V7X = 64 * 1024 * 1024
VMEM_BODY_TEMP = 24 * 1024 * 1024


def _vmem_limit(block_bytes):
    return int(min(2 * block_bytes + VMEM_BODY_TEMP, VMEM_BYTES_V7X - 8 * 1024 * 1024))


def _nbytes(shape, dtype):
    return math.prod(shape) * jnp.dtype(dtype).itemsize


def _params(sem, block_bytes):
    return pltpu.CompilerParams(dimension_semantics=sem, vmem_limit_bytes=_vmem_limit(block_bytes))


def _sigmoid(x):
    return 1.0 / (1.0 + jnp.exp(-x))


def _softplus(x):
    e = jnp.exp(-jnp.abs(x))
    u = 1.0 + e
    log1p_e = jnp.where(u == 1.0, e, jnp.log(u) * (e / jnp.where(u == 1.0, 1.0, u - 1.0)))
    return jnp.maximum(x, 0.0) + log1p_e


_SQRT_HALF = 0.7071067811865476
_INV_SQRT_2PI = 0.3989422804014327


def _gelu(x):
    return x * (lax.erf(x * _SQRT_HALF) + 1.0) * 0.5


def _gelu_grad(x):
    return 0.5 * (1.0 + lax.erf(x * _SQRT_HALF)) + x * jnp.exp(-0.5 * x * x) * _INV_SQRT_2PI


def _dot(a, b, dims):
    return lax.dot_general(a, b, (dims, ((), ())), preferred_element_type=F32)


_NN = ((1,), (0,))
_NT = ((1,), (1,))
_TN = ((0,), (0,))


def _split3(x):
    hi = x.astype(BF16)
    r1 = x - hi.astype(F32)
    mid = r1.astype(BF16)
    lo = (r1 - mid.astype(F32)).astype(BF16)
    return hi, mid, lo


def _dot_exact_rhs(x, e, dims):
    hi, mid, lo = _split3(x)
    return _dot(hi, e, dims) + _dot(mid, e, dims) + _dot(lo, e, dims)


def _dot_exact_lhs(e, x, dims):
    hi, mid, lo = _split3(x)
    return _dot(e, hi, dims) + _dot(e, mid, dims) + _dot(e, lo, dims)


def _tri(lower):
    r = lax.broadcasted_iota(jnp.int32, (CHUNK, CHUNK), 0)
    c = lax.broadcasted_iota(jnp.int32, (CHUNK, CHUNK), 1)
    return (r >= c) if lower else (r <= c)


def _matmul(a, b, *, mode, tm, tn, tk, out_dtypes, name, epilogue=None, extras=(), extra_specs=(), j_outer=False):
    if mode == "nn":
        (m, k), (_, n) = a.shape, b.shape
    elif mode == "nt":
        (m, k), (n, _) = a.shape, b.shape
    else:
        (k, m), (_, n) = a.shape, b.shape
    assert m % tm == 0 and n % tn == 0 and k % tk == 0, (name, m, n, k, tm, tn, tk)
    nk = k // tk
    n_extra, n_out = len(extras), len(out_dtypes)
    dims = {"nn": _NN, "nt": _NT, "tn": _TN}[mode]
    if epilogue is None:
        def epilogue(acc, ex, outs):
            outs[0][...] = acc.astype(outs[0].dtype)

    def body(*refs):
        a_ref, b_ref = refs[0], refs[1]
        ex_refs = refs[2:2 + n_extra]
        outs = refs[2 + n_extra:2 + n_extra + n_out]
        p = _dot(a_ref[...], b_ref[...], dims)
        if nk == 1:
            epilogue(p, ex_refs, outs)
        else:
            acc_ref = refs[2 + n_extra + n_out]
            kk = pl.program_id(2)

            @pl.when(kk == 0)
            def _():
                acc_ref[...] = p

            @pl.when(kk > 0)
            def _():
                acc_ref[...] += p

            @pl.when(kk == nk - 1)
            def _():
                epilogue(acc_ref[...], ex_refs, outs)

    if j_outer:
        grid = (n // tn, m // tm, nk)
        ij = lambda g0, g1: (g1, g0)
    else:
        grid = (m // tm, n // tn, nk)
        ij = lambda g0, g1: (g0, g1)

    def wrap(fn):
        return lambda g0, g1, kk: fn(*ij(g0, g1), kk)

    if mode == "nn":
        a_spec = pl.BlockSpec((tm, tk), wrap(lambda i, j, kk: (i, kk)))
        b_spec = pl.BlockSpec((tk, tn), wrap(lambda i, j, kk: (kk, j)))
        a_blk, b_blk = (tm, tk), (tk, tn)
    elif mode == "nt":
        a_spec = pl.BlockSpec((tm, tk), wrap(lambda i, j, kk: (i, kk)))
        b_spec = pl.BlockSpec((tn, tk), wrap(lambda i, j, kk: (j, kk)))
        a_blk, b_blk = (tm, tk), (tn, tk)
    else:
        a_spec = pl.BlockSpec((tk, tm), wrap(lambda i, j, kk: (kk, i)))
        b_spec = pl.BlockSpec((tk, tn), wrap(lambda i, j, kk: (kk, j)))
        a_blk, b_blk = (tk, tm), (tk, tn)
    ex_specs = [pl.BlockSpec(shape, wrap(lambda i, j, kk, f=f: f(i, j))) for shape, f in extra_specs]
    out_spec = [pl.BlockSpec((tm, tn), wrap(lambda i, j, kk: (i, j))) for _ in out_dtypes]
    out_shape = [jax.ShapeDtypeStruct((m, n), dt) for dt in out_dtypes]
    blk = (_nbytes(a_blk, a.dtype) + _nbytes(b_blk, b.dtype) + sum(_nbytes(s, F32) for s, _ in extra_specs)
           + sum(_nbytes((tm, tn), dt) for dt in out_dtypes) + _nbytes((tm, tn), F32))
    res = pl.pallas_call(
        body, name=name, grid=grid,
        in_specs=[a_spec, b_spec] + ex_specs, out_specs=out_spec, out_shape=out_shape,
        scratch_shapes=[pltpu.VMEM((tm, tn), F32)] if nk > 1 else [],
        compiler_params=_params(("parallel", "parallel", "arbitrary"), blk),
    )(a, b, *extras)
    return res[0] if n_out == 1 else res


ROW_TILE = 256


def _row_spec(width, col_block=0, tile=ROW_TILE):
    return pl.BlockSpec((tile, width), lambda i, cb=col_block: (i, cb))


def _vec_spec(width, col_block=0):
    return pl.BlockSpec((1, width), lambda i, cb=col_block: (0, cb))


def _rms_fwd(x, g, name):
    t = x.shape[0]

    def body(x_ref, g_ref, h_ref):
        xv = x_ref[...]
        r = lax.rsqrt(jnp.mean(xv * xv, axis=-1, keepdims=True) + NORM_EPS)
        h_ref[...] = (xv * r * g_ref[...]).astype(BF16)

    return pl.pallas_call(
        body, name=name, grid=(t // ROW_TILE,),
        in_specs=[_row_spec(D_MODEL), _vec_spec(D_MODEL)], out_specs=_row_spec(D_MODEL),
        out_shape=jax.ShapeDtypeStruct((t, D_MODEL), BF16),
        compiler_params=_params(("parallel",), 3 * _nbytes((ROW_TILE, D_MODEL), F32)),
    )(x, g)


def _rms_bwd(x, g, dh, dres, name):
    t = x.shape[0]

    def body(x_ref, g_ref, dh_ref, dres_ref, dx_ref, dxb_ref, gg_ref):
        xv = x_ref[...]
        r = lax.rsqrt(jnp.mean(xv * xv, axis=-1, keepdims=True) + NORM_EPS)
        xh = xv * r
        dhv = dh_ref[...]
        dyg = dhv * g_ref[...]
        dx = r * (dyg - xh * jnp.mean(dyg * xh, axis=-1, keepdims=True)) + dres_ref[...]
        dx_ref[...] = dx
        dxb_ref[...] = dx.astype(BF16)

        @pl.when(pl.program_id(0) == 0)
        def _():
            gg_ref[...] = jnp.zeros_like(gg_ref)

        gg_ref[...] += jnp.sum(dhv * xh, axis=0, keepdims=True)

    return pl.pallas_call(
        body, name=name, grid=(t // ROW_TILE,),
        in_specs=[_row_spec(D_MODEL), _vec_spec(D_MODEL), _row_spec(D_MODEL), _row_spec(D_MODEL)],
        out_specs=[_row_spec(D_MODEL), _row_spec(D_MODEL), _vec_spec(D_MODEL)],
        out_shape=[jax.ShapeDtypeStruct((t, D_MODEL), F32), jax.ShapeDtypeStruct((t, D_MODEL), BF16),
                   jax.ShapeDtypeStruct((1, D_MODEL), F32)],
        compiler_params=_params(("arbitrary",), 5 * _nbytes((ROW_TILE, D_MODEL), F32)),
    )(x, g, dh, dres)


def _loss_head(x2, gf, target, name):
    t = x2.shape[0]

    def body(x_ref, g_ref, t_ref, loss_ref, dx_ref, dxb_ref, gg_ref, tot_ref):
        xv = x_ref[...]
        gv = g_ref[...]
        r = lax.rsqrt(jnp.mean(xv * xv, axis=-1, keepdims=True) + NORM_EPS)
        xh = xv * r
        err = xh * gv - t_ref[...]
        dy = err * (1.0 / D_MODEL)
        dyg = dy * gv
        dx = r * (dyg - xh * jnp.mean(dyg * xh, axis=-1, keepdims=True))
        dx_ref[...] = dx
        dxb_ref[...] = dx.astype(BF16)

        @pl.when(pl.program_id(0) == 0)
        def _():
            gg_ref[...] = jnp.zeros_like(gg_ref)
            loss_ref[...] = jnp.zeros_like(loss_ref)

        gg_ref[...] += jnp.sum(dy * xh, axis=0, keepdims=True)
        loss_ref[...] += jnp.sum(err * err, axis=0, keepdims=True)
        tot_ref[...] = jnp.broadcast_to(jnp.sum(loss_ref[...], axis=1, keepdims=True) * (0.5 / D_MODEL), tot_ref.shape)

    return pl.pallas_call(
        body, name=name, grid=(t // ROW_TILE,),
        in_specs=[_row_spec(D_MODEL), _vec_spec(D_MODEL), _row_spec(D_MODEL)],
        out_specs=[_vec_spec(D_MODEL), _row_spec(D_MODEL), _row_spec(D_MODEL), _vec_spec(D_MODEL), _vec_spec(LANES)],
        out_shape=[jax.ShapeDtypeStruct((1, D_MODEL), F32), jax.ShapeDtypeStruct((t, D_MODEL), F32),
                   jax.ShapeDtypeStruct((t, D_MODEL), BF16), jax.ShapeDtypeStruct((1, D_MODEL), F32),
                   jax.ShapeDtypeStruct((1, LANES), F32)],
        compiler_params=_params(("arbitrary",), 5 * _nbytes((ROW_TILE, D_MODEL), F32)),
    )(x2, gf, target)


def _merge_fwd(pa, pb, proj, b_gates, name):
    t = pa.shape[0]
    gcb = COL_GATE // D_MODEL

    def body(pa_ref, pb_ref, la_ref, lb_ref, ba_ref, bb_ref, out_ref):
        ga = _sigmoid(la_ref[...] + ba_ref[...])
        gb = _sigmoid(lb_ref[...] + bb_ref[...])
        out_ref[...] = (ga * pa_ref[...] + gb * pb_ref[...]).astype(BF16)

    return pl.pallas_call(
        body, name=name, grid=(t // ROW_TILE,),
        in_specs=[_row_spec(D_MODEL), _row_spec(D_MODEL), _row_spec(D_MODEL, gcb), _row_spec(D_MODEL, gcb + 1),
                  _vec_spec(D_MODEL, 0), _vec_spec(D_MODEL, 1)],
        out_specs=_row_spec(D_MODEL),
        out_shape=jax.ShapeDtypeStruct((t, D_MODEL), BF16),
        compiler_params=_params(("parallel",), 5 * _nbytes((ROW_TILE, D_MODEL), F32)),
    )(pa, pb, proj, proj, b_gates, b_gates)


def _merge_bwd(dmerged, pa, pb, proj, b_gates, name):
    t = pa.shape[0]
    gcb = COL_GATE // D_MODEL

    def body(dm_ref, pa_ref, pb_ref, la_ref, lb_ref, ba_ref, bb_ref, dpa_ref, dpb_ref, dgl_ref, gb_ref):
        dm = dm_ref[...]
        ga = _sigmoid(la_ref[...] + ba_ref[...])
        gb = _sigmoid(lb_ref[...] + bb_ref[...])
        dpa_ref[...] = (dm * ga).astype(BF16)
        dpb_ref[...] = (dm * gb).astype(BF16)
        dla = dm * pa_ref[...] * ga * (1.0 - ga)
        dlb = dm * pb_ref[...] * gb * (1.0 - gb)
        dgl_ref[:, :D_MODEL] = dla.astype(BF16)
        dgl_ref[:, D_MODEL:] = dlb.astype(BF16)

        @pl.when(pl.program_id(0) == 0)
        def _():
            gb_ref[...] = jnp.zeros_like(gb_ref)

        gb_ref[:, :D_MODEL] += jnp.sum(dla, axis=0, keepdims=True)
        gb_ref[:, D_MODEL:] += jnp.sum(dlb, axis=0, keepdims=True)

    return pl.pallas_call(
        body, name=name, grid=(t // ROW_TILE,),
        in_specs=[_row_spec(D_MODEL), _row_spec(D_MODEL), _row_spec(D_MODEL), _row_spec(D_MODEL, gcb),
                  _row_spec(D_MODEL, gcb + 1), _vec_spec(D_MODEL, 0), _vec_spec(D_MODEL, 1)],
        out_specs=[_row_spec(D_MODEL), _row_spec(D_MODEL), _row_spec(2 * D_MODEL, COL_GATE // (2 * D_MODEL)),
                   _vec_spec(2 * D_MODEL)],
        out_shape=[jax.ShapeDtypeStruct((t, D_MODEL), BF16), jax.ShapeDtypeStruct((t, D_MODEL), BF16),
                   jax.ShapeDtypeStruct((t, MAIN_W), BF16), jax.ShapeDtypeStruct((1, 2 * D_MODEL), F32)],
        compiler_params=_params(("arbitrary",), 8 * _nbytes((ROW_TILE, D_MODEL), F32)),
    )(dmerged, pa, pb, proj, proj, b_gates, b_gates)


GMLP_TILE = 512
GMLP_NC = GMLP_TILE // CHUNK


def _gmlp_common(u_pre, v_pre, vg, vb):
    u = _gelu(u_pre)
    v = _gelu(v_pre)
    mu = jnp.mean(v, axis=-1, keepdims=True)
    vc = v - mu
    rstd = lax.rsqrt(jnp.mean(vc * vc, axis=-1, keepdims=True) + NORM_EPS)
    vh = vc * rstd
    vn = vh * vg + vb
    return u, vh, vn, rstd


def _chunks_to_lanes(x, g):
    return jnp.concatenate([x[c * CHUNK:(c + 1) * CHUNK, g * CHUNK:(g + 1) * CHUNK] for c in range(GMLP_NC)], axis=1)


def _gmlp_fwd(proj, vg, vb, wsp, bsp_t, name):
    t = proj.shape[0]

    def body(u_ref, v_ref, vg_ref, vb_ref, w_ref, b_ref, ya_ref):
        u, _, vn, _ = _gmlp_common(u_ref[...], v_ref[...], vg_ref[...], vb_ref[...])
        mask = _tri(True)
        bt = b_ref[...]
        for g in range(GROUPS):
            w = jnp.where(mask, w_ref[g], 0.0).astype(BF16)
            vcat = _chunks_to_lanes(vn, g).astype(BF16)
            s = _dot(w, vcat, _NN) + bt[:, g:g + 1]
            for c in range(GMLP_NC):
                rows, cols = slice(c * CHUNK, (c + 1) * CHUNK), slice(g * CHUNK, (g + 1) * CHUNK)
                ya_ref[rows, cols] = (u[rows, cols] * s[:, c * CHUNK:(c + 1) * CHUNK]).astype(BF16)

    return pl.pallas_call(
        body, name=name, grid=(t // GMLP_TILE,),
        in_specs=[_row_spec(D_MODEL, 0, GMLP_TILE), _row_spec(D_MODEL, 1, GMLP_TILE), _vec_spec(D_MODEL),
                  _vec_spec(D_MODEL), pl.BlockSpec((GROUPS, CHUNK, CHUNK), lambda i: (0, 0, 0)),
                  pl.BlockSpec((CHUNK, GROUPS), lambda i: (0, 0))],
        out_specs=_row_spec(D_MODEL, 0, GMLP_TILE),
        out_shape=jax.ShapeDtypeStruct((t, D_MODEL), BF16),
        compiler_params=_params(("parallel",), 3 * _nbytes((GMLP_TILE, D_MODEL), F32)),
    )(proj, proj, vg, vb, wsp, bsp_t)


def _gmlp_bwd(proj, dya, vg, vb, wsp, bsp_t, dproj, name):
    t = proj.shape[0]

    def body(u_ref, v_ref, dya_ref, vg_ref, vb_ref, w_ref, b_ref, dproj_in, duv_ref, gw_ref, gbt_ref, gvg_ref, gvb_ref,
             dvn_scr, du_scr):
        del dproj_in
        u_pre, v_pre = u_ref[...], v_ref[...]
        vgv = vg_ref[...]
        u, vh, vn, rstd = _gmlp_common(u_pre, v_pre, vgv, vb_ref[...])
        dya = dya_ref[...]
        mask = _tri(True)
        bt = b_ref[...]
        first = pl.program_id(0) == 0

        @pl.when(first)
        def _():
            gw_ref[...] = jnp.zeros_like(gw_ref)
            gbt_ref[...] = jnp.zeros_like(gbt_ref)
            gvg_ref[...] = jnp.zeros_like(gvg_ref)
            gvb_ref[...] = jnp.zeros_like(gvb_ref)

        lane = lax.broadcasted_iota(jnp.int32, (CHUNK, GROUPS), 1)
        gbt = jnp.zeros((CHUNK, GROUPS), F32)
        for g in range(GROUPS):
            w = jnp.where(mask, w_ref[g], 0.0).astype(BF16)
            vcat = _chunks_to_lanes(vn, g).astype(BF16)
            s = _dot(w, vcat, _NN) + bt[:, g:g + 1]
            ds = _chunks_to_lanes(dya * u, g)
            gbt = jnp.where(lane == g, jnp.sum(ds, axis=1, keepdims=True), gbt)
            dsb = ds.astype(BF16)
            gw_ref[g] += jnp.where(mask, _dot(dsb, vcat, _NT), 0.0)
            dv = _dot(w, dsb, _TN)
            for c in range(GMLP_NC):
                rows, cols = slice(c * CHUNK, (c + 1) * CHUNK), slice(g * CHUNK, (g + 1) * CHUNK)
                dvn_scr[rows, cols] = dv[:, c * CHUNK:(c + 1) * CHUNK]
                du_scr[rows, cols] = dya[rows, cols] * s[:, c * CHUNK:(c + 1) * CHUNK]
        gbt_ref[...] += gbt
        dvn = dvn_scr[...]
        gvg_ref[...] += jnp.sum(dvn * vh, axis=0, keepdims=True)
        gvb_ref[...] += jnp.sum(dvn, axis=0, keepdims=True)
        dvh = dvn * vgv
        dv = rstd * (dvh - jnp.mean(dvh, axis=-1, keepdims=True) - vh * jnp.mean(dvh * vh, axis=-1, keepdims=True))
        duv_ref[:, :D_MODEL] = (du_scr[...] * _gelu_grad(u_pre)).astype(BF16)
        duv_ref[:, D_MODEL:] = (dv * _gelu_grad(v_pre)).astype(BF16)

    return pl.pallas_call(
        body, name=name, grid=(t // GMLP_TILE,),
        in_specs=[_row_spec(D_MODEL, 0, GMLP_TILE), _row_spec(D_MODEL, 1, GMLP_TILE), _row_spec(D_MODEL, 0, GMLP_TILE),
                  _vec_spec(D_MODEL), _vec_spec(D_MODEL), pl.BlockSpec((GROUPS, CHUNK, CHUNK), lambda i: (0, 0, 0)),
                  pl.BlockSpec((CHUNK, GROUPS), lambda i: (0, 0)), pl.BlockSpec(memory_space=pl.ANY)],
        out_specs=[_row_spec(2 * D_MODEL, 0, GMLP_TILE), pl.BlockSpec((GROUPS, CHUNK, CHUNK), lambda i: (0, 0, 0)),
                   pl.BlockSpec((CHUNK, GROUPS), lambda i: (0, 0)), _vec_spec(D_MODEL), _vec_spec(D_MODEL)],
        out_shape=[jax.ShapeDtypeStruct(dproj.shape, BF16), jax.ShapeDtypeStruct((GROUPS, CHUNK, CHUNK), F32),
                   jax.ShapeDtypeStruct((CHUNK, GROUPS), F32), jax.ShapeDtypeStruct((1, D_MODEL), F32),
                   jax.ShapeDtypeStruct((1, D_MODEL), F32)],
        scratch_shapes=[pltpu.VMEM((GMLP_TILE, D_MODEL), F32), pltpu.VMEM((GMLP_TILE, D_MODEL), F32)],
        input_output_aliases={7: 0},
        compiler_params=_params(("arbitrary",), 6 * _nbytes((GMLP_TILE, D_MODEL), F32)),
    )(proj, proj, dya, vg, vb, wsp, bsp_t, dproj)


CONV_TILE = 512
CONV_COLS = 1024
HALO = SUBLANES


def _conv_taps(xe, cw, first_row, rows):
    acc = None
    for k in range(CONV_WIDTH):
        term = cw[k:k + 1, :] * xe[first_row - (CONV_WIDTH - 1) + k:first_row - (CONV_WIDTH - 1) + k + rows, :]
        acc = term if acc is None else acc + term
    return acc


def _conv_fwd(proj, cw, cb, name):
    t = proj.shape[0]
    nj = CONV_DIM // CONV_COLS
    xcb = COL_XBC // CONV_COLS
    rb = CONV_TILE // HALO

    def body(x_ref, prev_ref, cw_ref, cb_ref, xc_ref):
        i = pl.program_id(1)
        prev = jnp.where(i > 0, prev_ref[...], 0.0)
        xe = jnp.concatenate([prev, x_ref[...]], axis=0)
        pre = _conv_taps(xe, cw_ref[...], HALO, CONV_TILE) + cb_ref[...]
        xc_ref[...] = pre * _sigmoid(pre)

    return pl.pallas_call(
        body, name=name, grid=(nj, t // CONV_TILE),
        in_specs=[pl.BlockSpec((CONV_TILE, CONV_COLS), lambda j, i: (i, xcb + j)),
                  pl.BlockSpec((HALO, CONV_COLS), lambda j, i: (jnp.maximum(i * rb - 1, 0), xcb + j)),
                  pl.BlockSpec((CONV_WIDTH, CONV_COLS), lambda j, i: (0, j)),
                  pl.BlockSpec((1, CONV_COLS), lambda j, i: (0, j))],
        out_specs=pl.BlockSpec((CONV_TILE, CONV_COLS), lambda j, i: (i, j)),
        out_shape=jax.ShapeDtypeStruct((t, CONV_DIM), F32),
        compiler_params=_params(("parallel", "parallel"), 3 * _nbytes((CONV_TILE, CONV_COLS), F32)),
    )(proj, proj, cw, cb)


def _conv_bwd(proj, dxc, cw, cb, dproj, name):
    t = proj.shape[0]
    nj = CONV_DIM // CONV_COLS
    ni = t // CONV_TILE
    xcb = COL_XBC // CONV_COLS
    rb = CONV_TILE // HALO
    last_rb = t // HALO - 1

    def body(x_ref, prev_ref, next_ref, d_ref, dnext_ref, cw_ref, cb_ref, dproj_in, dx_ref, gw_ref, gb_ref):
        del dproj_in
        i = pl.program_id(1)
        cw_v = cw_ref[...]
        prev = jnp.where(i > 0, prev_ref[...], 0.0)
        xe = jnp.concatenate([prev, x_ref[...], next_ref[...]], axis=0)
        pre = _conv_taps(xe, cw_v, HALO, CONV_TILE + HALO) + cb_ref[...]
        de = jnp.concatenate([d_ref[...], jnp.where(i < ni - 1, dnext_ref[...], 0.0)], axis=0)
        sg = _sigmoid(pre)
        dpre_e = de * sg * (1.0 + pre * (1.0 - sg))
        dpre = dpre_e[:CONV_TILE]

        @pl.when(i == 0)
        def _():
            gw_ref[...] = jnp.zeros_like(gw_ref)
            gb_ref[...] = jnp.zeros_like(gb_ref)

        gb_ref[...] += jnp.sum(dpre, axis=0, keepdims=True)
        acc = None
        for k in range(CONV_WIDTH):
            off = HALO - (CONV_WIDTH - 1) + k
            gw_ref[k:k + 1, :] += jnp.sum(dpre * xe[off:off + CONV_TILE, :], axis=0, keepdims=True)
            shift = CONV_WIDTH - 1 - k
            term = cw_v[k:k + 1, :] * dpre_e[shift:shift + CONV_TILE, :]
            acc = term if acc is None else acc + term
        dx_ref[...] = acc.astype(BF16)

    return pl.pallas_call(
        body, name=name, grid=(nj, ni),
        in_specs=[pl.BlockSpec((CONV_TILE, CONV_COLS), lambda j, i: (i, xcb + j)),
                  pl.BlockSpec((HALO, CONV_COLS), lambda j, i: (jnp.maximum(i * rb - 1, 0), xcb + j)),
                  pl.BlockSpec((HALO, CONV_COLS), lambda j, i: (jnp.minimum((i + 1) * rb, last_rb), xcb + j)),
                  pl.BlockSpec((CONV_TILE, CONV_COLS), lambda j, i: (i, j)),
                  pl.BlockSpec((HALO, CONV_COLS), lambda j, i: (jnp.minimum((i + 1) * rb, last_rb), j)),
                  pl.BlockSpec((CONV_WIDTH, CONV_COLS), lambda j, i: (0, j)),
                  pl.BlockSpec((1, CONV_COLS), lambda j, i: (0, j)),
                  pl.BlockSpec(memory_space=pl.ANY)],
        out_specs=[pl.BlockSpec((CONV_TILE, CONV_COLS), lambda j, i: (i, xcb + j)),
                   pl.BlockSpec((CONV_WIDTH, CONV_COLS), lambda j, i: (0, j)),
                   pl.BlockSpec((1, CONV_COLS), lambda j, i: (0, j))],
        out_shape=[jax.ShapeDtypeStruct(dproj.shape, BF16), jax.ShapeDtypeStruct((CONV_WIDTH, CONV_DIM), F32),
                   jax.ShapeDtypeStruct((1, CONV_DIM), F32)],
        input_output_aliases={7: 0},
        compiler_params=_params(("parallel", "arbitrary"), 4 * _nbytes((CONV_TILE, CONV_COLS), F32)),
    )(proj, proj, proj, dxc, dxc, cw, cb, dproj)


def _ssd_decays(dt_raw, dtb, alog, e_bf, tril_bf):
    dtv = _softplus(dt_raw + dtb)
    a = -jnp.exp(alog)
    cs = _dot_exact_lhs(tril_bf, dtv * a, _NN)
    cs_full = _dot_exact_rhs(cs, e_bf, _NN)
    dt_full = _dot_exact_rhs(dtv, e_bf, _NN)
    cs_last = cs[CHUNK - 1:CHUNK, :]
    cs_last_full = cs_full[CHUNK - 1:CHUNK, :]
    return dtv, a, cs, cs_last, cs_full, cs_last_full, dt_full


def _head_mats(cs, cs_t, cb, h, mask):
    seg = cs[:, h:h + 1] - cs_t[h:h + 1, :]
    lmat = jnp.exp(jnp.where(mask, seg, -jnp.inf))
    return lmat, cb * lmat


def _ssd_fwd(xc, proj, dt_raw, dtb, alog, dskip_full, ng, e_bf, name):
    t = xc.shape[0]
    nc = t // CHUNK
    zcb = COL_Z // D_INNER

    def body(xc_ref, z_ref, dt_ref, dtb_ref, alog_ref, dsk_ref, ng_ref, e_ref, y_ref, yb_ref, sprev_ref, s_scr):
        @pl.when(pl.program_id(0) == 0)
        def _():
            s_scr[...] = jnp.zeros_like(s_scr)

        mask = _tri(True)
        tril_bf = mask.astype(BF16)
        e_v = e_ref[...]
        dtv, a, cs, cs_last, cs_full, cs_last_full, dt_full = _ssd_decays(dt_ref[...], dtb_ref[...], alog_ref[...], e_v, tril_bf)
        cs_t = cs.T
        xs = xc_ref[:, :D_INNER]
        xdt = xs * dt_full
        xdec = (xdt * jnp.exp(cs_last_full - cs_full)).astype(BF16)
        xdt_b = xdt.astype(BF16)
        ecs_full = jnp.exp(cs_full)
        cdec_full = jnp.exp(cs_last_full)
        s_prev = s_scr[...]
        sprev_ref[0] = s_prev
        zv = z_ref[...]
        gate = zv * _sigmoid(zv)
        for g in range(GROUPS):
            gc = slice(g * GROUP_W, (g + 1) * GROUP_W)
            bg = xc_ref[:, D_INNER + g * D_STATE:D_INNER + (g + 1) * D_STATE].astype(BF16)
            cg = xc_ref[:, D_INNER + GROUPS * D_STATE + g * D_STATE:D_INNER + GROUPS * D_STATE + (g + 1) * D_STATE].astype(BF16)
            cb = _dot(cg, bg, _NT)
            y_off = ecs_full[:, gc] * _dot(cg, s_prev[:, gc].astype(BF16), _NN)
            s_scr[:, gc] = s_prev[:, gc] * cdec_full[:, gc] + _dot(bg, xdec[:, gc], _TN)
            parts = []
            for r in range(GROUP_W // HEAD_DIM):
                h = g * (GROUP_W // HEAD_DIM) + r
                _, m = _head_mats(cs, cs_t, cb, h, mask)
                parts.append(_dot(m.astype(BF16), xdt_b[:, h * HEAD_DIM:(h + 1) * HEAD_DIM], _NN))
            yg = jnp.concatenate(parts, axis=1) + y_off + dsk_ref[:, gc] * xs[:, gc]
            y_ref[:, gc] = yg
            ygate = yg * gate[:, gc]
            rstd = lax.rsqrt(jnp.mean(ygate * ygate, axis=-1, keepdims=True) + NORM_EPS)
            yb_ref[:, gc] = (ygate * rstd * ng_ref[:, gc]).astype(BF16)

    vec = lambda w: pl.BlockSpec((1, w), lambda i: (0, 0))
    blk = _nbytes((CHUNK, CONV_DIM), F32) + 3 * _nbytes((CHUNK, D_INNER), F32) + _nbytes((D_STATE, D_INNER), F32)
    return pl.pallas_call(
        body, name=name, grid=(nc,),
        in_specs=[pl.BlockSpec((CHUNK, CONV_DIM), lambda i: (i, 0)), pl.BlockSpec((CHUNK, D_INNER), lambda i: (i, zcb)),
                  pl.BlockSpec((CHUNK, DT_PAD), lambda i: (i, 0)), vec(DT_PAD), vec(DT_PAD), vec(D_INNER), vec(D_INNER),
                  pl.BlockSpec((DT_PAD, D_INNER), lambda i: (0, 0))],
        out_specs=[pl.BlockSpec((CHUNK, D_INNER), lambda i: (i, 0)), pl.BlockSpec((CHUNK, D_INNER), lambda i: (i, 0)),
                   pl.BlockSpec((1, D_STATE, D_INNER), lambda i: (i, 0, 0))],
        out_shape=[jax.ShapeDtypeStruct((t, D_INNER), F32), jax.ShapeDtypeStruct((t, D_INNER), BF16),
                   jax.ShapeDtypeStruct((nc, D_STATE, D_INNER), F32)],
        scratch_shapes=[pltpu.VMEM((D_STATE, D_INNER), F32)],
        compiler_params=_params(("arbitrary",), blk),
    )(xc, proj, dt_raw, dtb, alog, dskip_full, ng, e_bf)


def _ssd_bwd(dyb, y, xc, proj, dt_raw, sprev, dtb, alog, dskip_full, ng, e_bf, dproj, name):
    t = xc.shape[0]
    nc = t // CHUNK
    zcb = COL_Z // D_INNER
    hpg = GROUP_W // HEAD_DIM
    rev = lambda i: nc - 1 - i

    def body(dyb_ref, y_ref, xc_ref, z_ref, dt_ref, sprev_ref, dtb_ref, alog_ref, dsk_ref, ng_ref, e_ref, dproj_in,
             dz_ref, dxc_ref, ddt_ref, gng_ref, gdsk_ref, galog_ref, gdtb_ref, ds_scr, dy_scr, dxdt_scr):
        del dproj_in

        @pl.when(pl.program_id(0) == 0)
        def _():
            ds_scr[...] = jnp.zeros_like(ds_scr)
            gng_ref[...] = jnp.zeros_like(gng_ref)
            gdsk_ref[...] = jnp.zeros_like(gdsk_ref)
            galog_ref[...] = jnp.zeros_like(galog_ref)
            gdtb_ref[...] = jnp.zeros_like(gdtb_ref)

        mask = _tri(True)
        tril_bf = mask.astype(BF16)
        triu_bf = _tri(False).astype(BF16)
        e_v = e_ref[...]
        dt_in = dt_ref[...] + dtb_ref[...]
        dtv, a, cs, cs_last, cs_full, cs_last_full, dt_full = _ssd_decays(dt_ref[...], dtb_ref[...], alog_ref[...], e_v, tril_bf)
        cs_t = cs.T
        xs = xc_ref[:, :D_INNER]
        xdt = xs * dt_full
        decay_full = jnp.exp(cs_last_full - cs_full)
        xdec_b = (xdt * decay_full).astype(BF16)
        xdt_b = xdt.astype(BF16)
        ecs_full = jnp.exp(cs_full)
        cdec_full = jnp.exp(cs_last_full)
        s_prev = sprev_ref[0]
        ds_next = ds_scr[...]

        zv = z_ref[...]
        sg = _sigmoid(zv)
        gate = zv * sg
        yv = y_ref[...]
        dybv = dyb_ref[...]
        ngv = ng_ref[...]
        for g in range(GROUPS):
            gc = slice(g * GROUP_W, (g + 1) * GROUP_W)
            ygate = yv[:, gc] * gate[:, gc]
            rstd = lax.rsqrt(jnp.mean(ygate * ygate, axis=-1, keepdims=True) + NORM_EPS)
            yn = ygate * rstd
            gng_ref[:, gc] += jnp.sum(dybv[:, gc] * yn, axis=0, keepdims=True)
            dyn = dybv[:, gc] * ngv[:, gc]
            dyg = rstd * (dyn - yn * jnp.mean(dyn * yn, axis=-1, keepdims=True))
            dz_ref[:, gc] = (dyg * yv[:, gc] * sg[:, gc] * (1.0 + zv[:, gc] * (1.0 - sg[:, gc]))).astype(BF16)
            dy_scr[:, gc] = dyg * gate[:, gc]
        dy = dy_scr[...]
        dyo = dy * ecs_full
        dyo_b = dyo.astype(BF16)
        dy_b = dy.astype(BF16)

        lane_h = lax.broadcasted_iota(jnp.int32, (CHUNK, DT_PAD), 1)
        sub_h = lax.broadcasted_iota(jnp.int32, (DT_PAD, CHUNK), 0)
        dcs_rows = jnp.zeros((CHUNK, DT_PAD), F32)
        dcs_cols_t = jnp.zeros((DT_PAD, CHUNK), F32)
        dec_cols = []
        for g in range(GROUPS):
            gc = slice(g * GROUP_W, (g + 1) * GROUP_W)
            b_cols = slice(D_INNER + g * D_STATE, D_INNER + (g + 1) * D_STATE)
            c_cols = slice(D_INNER + GROUPS * D_STATE + g * D_STATE, D_INNER + GROUPS * D_STATE + (g + 1) * D_STATE)
            bg = xc_ref[:, b_cols].astype(BF16)
            cg = xc_ref[:, c_cols].astype(BF16)
            sg_prev_b = s_prev[:, gc].astype(BF16)
            dsg = ds_next[:, gc]
            dsg_b = dsg.astype(BF16)
            cb = _dot(cg, bg, _NT)
            c_s = _dot(cg, sg_prev_b, _NN)
            b_ds = _dot(bg, dsg_b, _NN)
            dcb = jnp.zeros((CHUNK, CHUNK), F32)
            parts = []
            for r in range(hpg):
                h = g * hpg + r
                hc = slice(h * HEAD_DIM, (h + 1) * HEAD_DIM)
                lmat, m = _head_mats(cs, cs_t, cb, h, mask)
                dm = _dot(dy_b[:, hc], xdt_b[:, hc], _NT)
                parts.append(_dot(m.astype(BF16), dy_b[:, hc], _TN))
                dcb = dcb + dm * lmat
                w = dm * m
                dcs_rows = jnp.where(lane_h == h, jnp.sum(w, axis=1, keepdims=True), dcs_rows)
                dcs_cols_t = jnp.where(sub_h == h, jnp.sum(w, axis=0, keepdims=True), dcs_cols_t)
            dxdt_scr[:, gc] = jnp.concatenate(parts, axis=1) + decay_full[:, gc] * b_ds
            dcb_b = dcb.astype(BF16)
            dxc_ref[:, c_cols] = _dot(dcb_b, bg, _NN) + _dot(dyo_b[:, gc], sg_prev_b, _NT)
            dxc_ref[:, b_cols] = _dot(dcb_b, cg, _TN) + _dot(xdec_b[:, gc], dsg_b, _NT)
            ds_scr[:, gc] = _dot(cg, dyo_b[:, gc], _TN) + cdec_full[:, gc] * dsg
            dec_prod = xdt[:, gc] * decay_full[:, gc] * b_ds
            dec_cols.append(jnp.sum(dec_prod, axis=0, keepdims=True))
            dy_scr[:, gc] = dyo[:, gc] * c_s - dec_prod
        dxdt = dxdt_scr[...]
        dxc_ref[:, :D_INNER] = dxdt * dt_full + dsk_ref[...] * dy
        t_cs = _dot_exact_rhs(dy_scr[...], e_v, _NT)
        t_dt = _dot_exact_rhs(dxdt * xs, e_v, _NT)
        sd = jnp.sum(ds_next * s_prev, axis=0, keepdims=True)
        last_full = jnp.concatenate(dec_cols, axis=1) + cdec_full * sd
        tail = jnp.concatenate([last_full, jnp.sum(dy * xs, axis=0, keepdims=True),
                                jnp.zeros((SUBLANES - 2, D_INNER), F32)], axis=0)
        t_tail = _dot_exact_rhs(tail, e_v, _NT)
        gdsk_ref[...] += t_tail[1:2, :]
        row = lax.broadcasted_iota(jnp.int32, (CHUNK, DT_PAD), 0)
        dcs = dcs_rows - dcs_cols_t.T + t_cs + jnp.where(row == CHUNK - 1, t_tail[0:1, :], 0.0)
        dda = _dot_exact_lhs(triu_bf, dcs, _NN)
        galog_ref[...] += jnp.sum(dda * dtv, axis=0, keepdims=True) * a
        ddt = dda * a + t_dt
        ddt_raw = jnp.where(lane_h < N_HEADS, ddt * _sigmoid(dt_in), 0.0)
        gdtb_ref[...] += jnp.sum(ddt_raw, axis=0, keepdims=True)
        ddt_ref[...] = ddt_raw.astype(BF16)

    vec = lambda w: pl.BlockSpec((1, w), lambda i: (0, 0))
    blk = (2 * _nbytes((CHUNK, CONV_DIM), F32) + 4 * _nbytes((CHUNK, D_INNER), F32) + 4 * _nbytes((D_STATE, D_INNER), F32))
    return pl.pallas_call(
        body, name=name, grid=(nc,),
        in_specs=[pl.BlockSpec((CHUNK, D_INNER), lambda i: (rev(i), 0)), pl.BlockSpec((CHUNK, D_INNER), lambda i: (rev(i), 0)),
                  pl.BlockSpec((CHUNK, CONV_DIM), lambda i: (rev(i), 0)), pl.BlockSpec((CHUNK, D_INNER), lambda i: (rev(i), zcb)),
                  pl.BlockSpec((CHUNK, DT_PAD), lambda i: (rev(i), 0)), pl.BlockSpec((1, D_STATE, D_INNER), lambda i: (rev(i), 0, 0)),
                  vec(DT_PAD), vec(DT_PAD), vec(D_INNER), vec(D_INNER), pl.BlockSpec((DT_PAD, D_INNER), lambda i: (0, 0)),
                  pl.BlockSpec(memory_space=pl.ANY)],
        out_specs=[pl.BlockSpec((CHUNK, D_INNER), lambda i: (rev(i), zcb)), pl.BlockSpec((CHUNK, CONV_DIM), lambda i: (rev(i), 0)),
                   pl.BlockSpec((CHUNK, DT_PAD), lambda i: (rev(i), 0)), vec(D_INNER), vec(DT_PAD), vec(DT_PAD), vec(DT_PAD)],
        out_shape=[jax.ShapeDtypeStruct(dproj.shape, BF16), jax.ShapeDtypeStruct((t, CONV_DIM), F32),
                   jax.ShapeDtypeStruct((t, DT_PAD), BF16), jax.ShapeDtypeStruct((1, D_INNER), F32),
                   jax.ShapeDtypeStruct((1, DT_PAD), F32), jax.ShapeDtypeStruct((1, DT_PAD), F32),
                   jax.ShapeDtypeStruct((1, DT_PAD), F32)],
        scratch_shapes=[pltpu.VMEM((D_STATE, D_INNER), F32), pltpu.VMEM((CHUNK, D_INNER), F32),
                        pltpu.VMEM((CHUNK, D_INNER), F32)],
        input_output_aliases={11: 0},
        compiler_params=_params(("arbitrary",), blk),
    )(dyb, y, xc, proj, dt_raw, sprev, dtb, alog, dskip_full, ng, e_bf, dproj)


_HBM = pl.BlockSpec(memory_space=pl.ANY)


def _mesh_pos():
    return lax.axis_index("x"), lax.axis_index("y"), lax.axis_index("c")


def _other_chips(x, y):
    return [(1 - x, y), (x, 1 - y), (1 - x, 1 - y)]


def _all_gather(shards, name):
    n = len(shards)

    def body(*refs):
        ins, outs = refs[:n], refs[n:2 * n]
        send_sems, recv_sems, local_sems = refs[2 * n:]
        x, y, c = _mesh_pos()
        me, sibling = (x, y, c), (x, y, 1 - c)
        chips = _other_chips(x, y)

        def slot(p):
            return 4 * p[0] + 2 * p[1] + p[2]

        def copy(a, k, block, to, src=None):
            dst = outs[a].at[slot(block)]
            return pltpu.make_async_remote_copy(
                src_ref=dst if src is None else src, dst_ref=dst, send_sem=send_sems.at[a * 7 + k],
                recv_sem=recv_sems.at[a * 7 + k], device_id=to, device_id_type=MESH)

        started = []
        own = []
        for a in range(n):
            mine = pltpu.make_async_copy(ins[a], outs[a].at[slot(me)], local_sems.at[a])
            mine.start()
            own.append(mine)
            first = [copy(a, 0, me, sibling, src=ins[a])]
            first += [copy(a, 1 + j, me, (*chip, c), src=ins[a]) for j, chip in enumerate(chips)]
            for cp in first:
                cp.start()
            started += first
        for a in range(n):
            for j, chip in enumerate(chips):
                copy(a, 1 + j, (*chip, c), me).wait_recv()
                fwd = copy(a, 4 + j, (*chip, c), sibling)
                fwd.start()
                started.append(fwd)
        for a in range(n):
            copy(a, 0, sibling, me).wait_recv()
            for j, chip in enumerate(chips):
                copy(a, 4 + j, (*chip, 1 - c), me).wait_recv()
        for cp in started:
            cp.wait_send()
        for mine in own:
            mine.wait()

    return pl.pallas_call(
        body, name=name,
        in_specs=[_HBM] * n, out_specs=[_HBM] * n,
        out_shape=[jax.ShapeDtypeStruct((N_DEV,) + s.shape, s.dtype) for s in shards],
        scratch_shapes=[pltpu.SemaphoreType.DMA((7 * n,)), pltpu.SemaphoreType.DMA((7 * n,)),
                        pltpu.SemaphoreType.DMA((n,))],
    )(*shards)


def _exchange_all(packed, name):
    def body(in_ref, out_ref, send_sems, recv_sems, local_sem):
        x, y, c = _mesh_pos()
        my_slot = 4 * x + 2 * y + c
        mine = pltpu.make_async_copy(in_ref, out_ref.at[my_slot], local_sem)
        mine.start()
        copies = []
        for k in range(1, N_DEV):
            fx, fy, fc = (k >> 2) & 1, (k >> 1) & 1, k & 1
            peer = (x + fx - 2 * x * fx, y + fy - 2 * y * fy, c + fc - 2 * c * fc)
            peer_slot = 4 * peer[0] + 2 * peer[1] + peer[2]
            send = pltpu.make_async_remote_copy(
                src_ref=in_ref, dst_ref=out_ref.at[my_slot], send_sem=send_sems.at[k - 1], recv_sem=recv_sems.at[k - 1],
                device_id=peer, device_id_type=MESH)
            send.start()
            recv = pltpu.make_async_remote_copy(
                src_ref=in_ref, dst_ref=out_ref.at[peer_slot], send_sem=send_sems.at[k - 1], recv_sem=recv_sems.at[k - 1],
                device_id=peer, device_id_type=MESH)
            copies.append((send, recv))
        for send, recv in copies:
            send.wait_send()
            recv.wait_recv()
        mine.wait()

    return pl.pallas_call(
        body, name=name, in_specs=[_HBM], out_specs=_HBM,
        out_shape=jax.ShapeDtypeStruct((N_DEV,) + packed.shape, packed.dtype),
        scratch_shapes=[pltpu.SemaphoreType.DMA((N_DEV - 1,)), pltpu.SemaphoreType.DMA((N_DEV - 1,)),
                        pltpu.SemaphoreType.DMA],
    )(packed)


def _swap_with_sibling(grads, name):
    n = len(grads)

    def body(*refs):
        ins, outs = refs[:n], refs[n:2 * n]
        send_sems, recv_sems = refs[2 * n:]
        x, y, c = _mesh_pos()
        copies = []
        for a in range(n):
            for k in range(N_CHIP):
                cp = pltpu.make_async_remote_copy(
                    src_ref=ins[a].at[(1 - c) + 2 * k], dst_ref=outs[a].at[k], send_sem=send_sems.at[a * N_CHIP + k],
                    recv_sem=recv_sems.at[a * N_CHIP + k], device_id=(x, y, 1 - c), device_id_type=MESH)
                cp.start()
                copies.append(cp)
        for cp in copies:
            cp.wait()

    return pl.pallas_call(
        body, name=name, in_specs=[_HBM] * n, out_specs=[_HBM] * n,
        out_shape=[jax.ShapeDtypeStruct((N_CHIP,) + g.shape[1:], g.dtype) for g in grads],
        scratch_shapes=[pltpu.SemaphoreType.DMA((N_CHIP * n,)), pltpu.SemaphoreType.DMA((N_CHIP * n,))],
    )(*grads)


def _scatter_to_chips(parts, name):
    n = len(parts)

    def body(*refs):
        ins, outs = refs[:n], refs[n:2 * n]
        send_sems, recv_sems, local_sems = refs[2 * n:]
        x, y, c = _mesh_pos()
        my_chip = 2 * x + y
        own, copies = [], []
        for a in range(n):
            mine = pltpu.make_async_copy(ins[a].at[my_chip], outs[a].at[my_chip], local_sems.at[a])
            mine.start()
            own.append(mine)
            for j, chip in enumerate(_other_chips(x, y)):
                send = pltpu.make_async_remote_copy(
                    src_ref=ins[a].at[2 * chip[0] + chip[1]], dst_ref=outs[a].at[my_chip], send_sem=send_sems.at[a * 3 + j],
                    recv_sem=recv_sems.at[a * 3 + j], device_id=(*chip, c), device_id_type=MESH)
                send.start()
                recv = pltpu.make_async_remote_copy(
                    src_ref=ins[a].at[my_chip], dst_ref=outs[a].at[2 * chip[0] + chip[1]], send_sem=send_sems.at[a * 3 + j],
                    recv_sem=recv_sems.at[a * 3 + j], device_id=(*chip, c), device_id_type=MESH)
                copies.append((send, recv))
        for send, recv in copies:
            send.wait_send()
            recv.wait_recv()
        for mine in own:
            mine.wait()

    return pl.pallas_call(
        body, name=name, in_specs=[_HBM] * n, out_specs=[_HBM] * n,
        out_shape=[jax.ShapeDtypeStruct(p.shape, p.dtype) for p in parts],
        scratch_shapes=[pltpu.SemaphoreType.DMA((3 * n,)), pltpu.SemaphoreType.DMA((3 * n,)),
                        pltpu.SemaphoreType.DMA((n,))],
    )(*parts)


def _row_block(rows, cols, slots):
    budget = 2 * 1024 * 1024
    if rows % SUBLANES:
        return rows
    br = rows
    while br % 2 == 0 and (br // 2) % SUBLANES == 0 and slots * br * cols * 4 > budget:
        br //= 2
    return br


def _add_sibling(grads, recv, c_idx, name):
    _, rows, cols = grads.shape
    br = _row_block(rows, cols, 3)

    def body(c_ref, g_ref, r_ref, out_ref):
        del c_ref
        out_ref[...] = (g_ref[...].astype(F32) + r_ref[...].astype(F32)).astype(out_ref.dtype)

    grid_spec = pltpu.PrefetchScalarGridSpec(
        num_scalar_prefetch=1, grid=(N_CHIP, rows // br),
        in_specs=[pl.BlockSpec((1, br, cols), lambda k, i, c_ref: (c_ref[0] + 2 * k, i, 0)),
                  pl.BlockSpec((1, br, cols), lambda k, i, c_ref: (k, i, 0))],
        out_specs=pl.BlockSpec((1, br, cols), lambda k, i, c_ref: (k, i, 0)))
    return pl.pallas_call(
        body, name=name, grid_spec=grid_spec, out_shape=jax.ShapeDtypeStruct((N_CHIP, rows, cols), grads.dtype),
        compiler_params=_params(("parallel", "parallel"), 3 * _nbytes((br, cols), F32)),
    )(c_idx, grads, recv)


def _adamw(slots, w, m, v, name):
    ns, rows, cols = slots.shape
    br = _row_block(rows, cols, ns + 7)
    c1 = 1.0 / (1.0 - ADAM_B1 ** ADAM_STEP)
    c2 = 1.0 / (1.0 - ADAM_B2 ** ADAM_STEP)

    def body(s_ref, w_ref, m_ref, v_ref, g_ref, d_ref, m2_ref, v2_ref):
        g = s_ref[0].astype(F32)
        for k in range(1, ns):
            g = g + s_ref[k].astype(F32)
        m2 = ADAM_B1 * m_ref[...] + (1.0 - ADAM_B1) * g
        v2 = ADAM_B2 * v_ref[...] + (1.0 - ADAM_B2) * (g * g)
        g_ref[...] = g
        m2_ref[...] = m2
        v2_ref[...] = v2
        d_ref[...] = -ADAM_LR * ((m2 * c1) / (jnp.sqrt(v2 * c2) + ADAM_EPS) + ADAM_WD * w_ref[...])

    blk = pl.BlockSpec((br, cols), lambda i: (i, 0))
    return pl.pallas_call(
        body, name=name, grid=(rows // br,),
        in_specs=[pl.BlockSpec((ns, br, cols), lambda i: (0, i, 0)), blk, blk, blk],
        out_specs=[blk, blk, blk, blk],
        out_shape=[jax.ShapeDtypeStruct((rows, cols), F32)] * 4,
        compiler_params=_params(("parallel",), (ns + 7) * _nbytes((br, cols), F32)),
    )(slots, w, m, v)


_SMALL = ["norm_mix_g", "conv_b", "dt_bias", "a_log", "d_skip", "ssm_norm_g", "v_norm_g", "v_norm_b", "w_spatial",
          "b_spatial", "b_gates", "norm_mlp_g", "norm_final_g"]
_SHARDED = ["w_in", "w_proj_a", "w_proj_b", "w_out", "w_mlp_up", "w_mlp_down"]


def _pack(arrays):
    flat = []
    for arr in arrays:
        f = arr.reshape(-1).astype(F32)
        pad = (-f.shape[0]) % LANES
        flat.append(jnp.pad(f, (0, pad)) if pad else f)
    out = jnp.concatenate(flat)
    pad = (-out.shape[0]) % (SUBLANES * LANES)
    if pad:
        out = jnp.pad(out, (0, pad))
    return out.reshape(-1, LANES)


def _unpack(packed, shapes):
    flat = packed.reshape(-1)
    out, off = [], 0
    for shape in shapes:
        size = math.prod(shape)
        out.append(flat[off:off + size].reshape(shape))
        off += size + ((-size) % LANES)
    return out


def _mm_tiles(mode, m, n, k):
    tn = min(n, 1024)
    if mode == "tn":
        return min(m, 1024), tn, min(k, 2048)
    if k <= 2048:
        return min(m, 1024), tn, k
    if k <= 4096:
        return min(m, 512), tn, k
    return min(m, 1024), tn, 2048


def _local_step(x, target, wts, small):
    t = x.shape[0]
    w_main, w_dt = wts["w_main"], wts["w_dt"]
    bsp_t = small["b_spatial"].T
    pad32 = lambda a: jnp.pad(a, ((0, 0), (0, DT_PAD - N_HEADS)))
    dtb, alog = pad32(small["dt_bias"]), pad32(small["a_log"])
    dskip_full = jnp.repeat(small["d_skip"], HEAD_DIM, axis=1)
    head_of_col = lax.broadcasted_iota(jnp.int32, (DT_PAD, D_INNER), 1) // HEAD_DIM
    e_bf = (head_of_col == lax.broadcasted_iota(jnp.int32, (DT_PAD, D_INNER), 0)).astype(BF16)

    def mm(a, b, mode, name, **kw):
        if mode == "nn":
            m, k, n = a.shape[0], a.shape[1], b.shape[1]
        elif mode == "nt":
            m, k, n = a.shape[0], a.shape[1], b.shape[0]
        else:
            m, k, n = a.shape[1], a.shape[0], b.shape[1]
        tm, tn, tk = _mm_tiles(mode, m, n, k)
        kw.setdefault("out_dtypes", (BF16,) if mode == "tn" else (F32,))
        if "extra_specs" in kw:
            kw["extra_specs"] = kw["extra_specs"](tm, tn)
        return _matmul(a, b, mode=mode, tm=tm, tn=tn, tk=tk, name=name, **kw)

    def out_tile(tm, tn):
        return (((tm, tn), lambda i, j: (i, j)),)

    h = _rms_fwd(x, small["norm_mix_g"], "rms_mix")
    proj = mm(h, w_main, "nn", "proj_main", j_outer=True)
    dt_raw = mm(h, w_dt, "nn", "proj_dt")
    y_a = _gmlp_fwd(proj, small["v_norm_g"], small["v_norm_b"], small["w_spatial"], bsp_t, "gmlp_fwd")
    xc = _conv_fwd(proj, wts["conv_w"], small["conv_b"], "conv_fwd")
    y_ssd, y_b, sprev = _ssd_fwd(xc, proj, dt_raw, dtb, alog, dskip_full, small["ssm_norm_g"], e_bf, "ssd_fwd")
    pa = mm(y_a, wts["w_proj_a"], "nn", "proj_a")
    pb = mm(y_b, wts["w_proj_b"], "nn", "proj_b")
    merged = _merge_fwd(pa, pb, proj, small["b_gates"], "merge_fwd")

    def add_residual(acc, ex, outs):
        outs[0][...] = acc + ex[0][...]

    x1 = mm(merged, wts["w_out"], "nn", "out_proj", epilogue=add_residual, extras=(x,), extra_specs=out_tile)
    h2 = _rms_fwd(x1, small["norm_mlp_g"], "rms_mlp")

    def relu_sq(acc, ex, outs):
        outs[0][...] = acc
        r = jnp.maximum(acc, 0.0)
        outs[1][...] = (r * r).astype(BF16)

    up, act = mm(h2, wts["w_mlp_up"], "nn", "mlp_up", epilogue=relu_sq, out_dtypes=(F32, BF16), j_outer=True)
    x2 = mm(act, wts["w_mlp_down"], "nn", "mlp_down", epilogue=add_residual, extras=(x1,), extra_specs=out_tile)
    _, dx2, dx2_b, g_final, loss = _loss_head(x2, small["norm_final_g"], target, "loss_head")

    def relu_sq_bwd(acc, ex, outs):
        outs[0][...] = (acc * 2.0 * jnp.maximum(ex[0][...], 0.0)).astype(BF16)

    dup = mm(dx2_b, wts["w_mlp_down"], "nt", "d_act", epilogue=relu_sq_bwd, extras=(up,), extra_specs=out_tile,
             out_dtypes=(BF16,), j_outer=True)
    g_down = mm(act, dx2_b, "tn", "g_mlp_down")
    g_up = mm(h2, dup, "tn", "g_mlp_up")
    dh2 = mm(dup, wts["w_mlp_up"], "nt", "d_h2")
    dx1, dx1_b, g_mlp = _rms_bwd(x1, small["norm_mlp_g"], dh2, dx2, "rms_mlp_bwd")

    g_out = mm(merged, dx1_b, "tn", "g_out")
    dmerged = mm(dx1_b, wts["w_out"], "nt", "d_merged")
    dpa, dpb, dproj, g_bgates = _merge_bwd(dmerged, pa, pb, proj, small["b_gates"], "merge_bwd")
    g_pa = mm(y_a, dpa, "tn", "g_proj_a")
    g_pb = mm(y_b, dpb, "tn", "g_proj_b")
    dya = mm(dpa, wts["w_proj_a"], "nt", "d_ya")
    dyb = mm(dpb, wts["w_proj_b"], "nt", "d_yb")

    dproj, g_wsp, g_bsp_t, g_vg, g_vb = _gmlp_bwd(proj, dya, small["v_norm_g"], small["v_norm_b"], small["w_spatial"],
                                                   bsp_t, dproj, "gmlp_bwd")
    dproj, dxc, ddt, g_ng, g_dskip, g_alog, g_dtb = _ssd_bwd(dyb, y_ssd, xc, proj, dt_raw, sprev, dtb, alog, dskip_full,
                                                             small["ssm_norm_g"], e_bf, dproj, "ssd_bwd")
    dproj, g_convw, g_convb = _conv_bwd(proj, dxc, wts["conv_w"], small["conv_b"], dproj, "conv_bwd")

    g_main = mm(h, dproj, "tn", "g_in_main")
    g_dt = mm(h, ddt, "tn", "g_in_dt")

    def add_dt(acc, ex, outs):
        outs[0][...] = acc + _dot(ex[0][...], ex[1][...], _NT)

    dh = mm(dproj, w_main, "nt", "d_h", epilogue=add_dt, extras=(ddt, w_dt),
            extra_specs=lambda tm, tn: (((tm, DT_PAD), lambda i, j: (i, 0)), ((D_MODEL, DT_PAD), lambda i, j: (0, 0))))
    grad_x, _, g_mix = _rms_bwd(x, small["norm_mix_g"], dh, dx1, "rms_mix_bwd")

    grads = {
        "w_main": g_main, "w_dt": g_dt, "w_proj_a": g_pa, "w_proj_b": g_pb, "w_out": g_out, "w_mlp_up": g_up,
        "w_mlp_down": g_down, "conv_w": g_convw,
        "norm_mix_g": g_mix, "conv_b": g_convb, "dt_bias": g_dtb[:, :N_HEADS], "a_log": g_alog[:, :N_HEADS],
        "d_skip": g_dskip[:, :N_HEADS], "ssm_norm_g": g_ng, "v_norm_g": g_vg, "v_norm_b": g_vb, "w_spatial": g_wsp,
        "b_spatial": g_bsp_t.T, "b_gates": g_bgates, "norm_mlp_g": g_mlp, "norm_final_g": g_final,
    }
    return loss, grad_x, grads


def _split_w_in(w_full):
    dt0 = COL_GATE
    w_main = jnp.concatenate([w_full[:, :dt0], w_full[:, dt0 + N_HEADS:]], axis=1)
    w_dt = jnp.pad(w_full[:, dt0:dt0 + N_HEADS], ((0, 0), (0, DT_PAD - N_HEADS)))
    return w_main, w_dt


def _join_w_in(g_main, g_dt):
    dt0 = COL_GATE
    return jnp.concatenate([g_main[:, :dt0], g_dt[:, :N_HEADS], g_main[:, dt0:]], axis=1)


def kernel(x, norm_mix_g, w_in, conv_w, conv_b, dt_bias, a_log, d_skip, ssm_norm_g, v_norm_g, v_norm_b, w_spatial, b_spatial, b_gates, w_proj_a, w_proj_b, w_out, norm_mlp_g, w_mlp_up, w_mlp_down, norm_final_g, loss_target, m_norm_mix_g, m_w_in, m_conv_w, m_conv_b, m_dt_bias, m_a_log, m_d_skip, m_ssm_norm_g, m_v_norm_g, m_v_norm_b, m_w_spatial, m_b_spatial, m_b_gates, m_w_proj_a, m_w_proj_b, m_w_out, m_norm_mlp_g, m_w_mlp_up, m_w_mlp_down, m_norm_final_g, v_norm_mix_g, v_w_in, v_conv_w, v_conv_b, v_dt_bias, v_a_log, v_d_skip, v_ssm_norm_g, v_v_norm_g, v_v_norm_b, v_w_spatial, v_b_spatial, v_b_gates, v_w_proj_a, v_w_proj_b, v_w_out, v_norm_mlp_g, v_w_mlp_up, v_w_mlp_down, v_norm_final_g):
    given = dict(locals())
    names = ["norm_mix_g", "w_in", "conv_w", "conv_b", "dt_bias", "a_log", "d_skip", "ssm_norm_g", "v_norm_g", "v_norm_b",
             "w_spatial", "b_spatial", "b_gates", "w_proj_a", "w_proj_b", "w_out", "norm_mlp_g", "w_mlp_up", "w_mlp_down",
             "norm_final_g"]
    shapes = {n: given[n].shape for n in names}
    t = x.shape[1]
    dev = 4 * lax.axis_index("x") + 2 * lax.axis_index("y") + lax.axis_index("c")

    shard2d = {"w_in": w_in[0], "w_proj_a": w_proj_a[0], "w_proj_b": w_proj_b[0], "w_out": w_out[0],
               "w_mlp_up": w_mlp_up[0], "w_mlp_down": w_mlp_down[0]}
    conv_shard = conv_w.reshape(CONV_WIDTH, -1)
    gathered = _all_gather([shard2d[n].astype(BF16) for n in _SHARDED] + [conv_shard], "gather_weights")
    gw = dict(zip(_SHARDED, gathered[:-1]))
    by_cols = lambda g: jnp.transpose(g, (1, 0, 2)).reshape(g.shape[1], -1)
    w_main, w_dt = _split_w_in(by_cols(gw["w_in"]))
    wts = {"w_main": w_main, "w_dt": w_dt, "w_proj_a": gw["w_proj_a"].reshape(-1, D_MODEL),
           "w_proj_b": gw["w_proj_b"].reshape(-1, D_MODEL), "w_out": gw["w_out"].reshape(-1, D_MODEL),
           "w_mlp_up": by_cols(gw["w_mlp_up"]), "w_mlp_down": gw["w_mlp_down"].reshape(-1, D_MODEL),
           "conv_w": by_cols(gathered[-1])}
    small = {"norm_mix_g": norm_mix_g, "conv_b": conv_b, "dt_bias": dt_bias, "a_log": a_log, "d_skip": d_skip,
             "ssm_norm_g": ssm_norm_g, "v_norm_g": v_norm_g, "v_norm_b": v_norm_b, "w_spatial": w_spatial[0],
             "b_spatial": b_spatial[0], "b_gates": b_gates, "norm_mlp_g": norm_mlp_g,
             "norm_final_g": norm_final_g.reshape(1, -1)}

    loss_part, grad_x, grads = _local_step(x[0], loss_target[0], wts, small)

    by_dev_cols = lambda g: jnp.transpose(g.reshape(g.shape[0], N_DEV, -1), (1, 0, 2))
    by_dev_rows = lambda g: g.reshape(N_DEV, -1, g.shape[1])
    full = {"w_in": by_dev_cols(_join_w_in(grads["w_main"], grads["w_dt"])), "w_proj_a": by_dev_rows(grads["w_proj_a"]),
            "w_proj_b": by_dev_rows(grads["w_proj_b"]), "w_out": by_dev_rows(grads["w_out"]),
            "w_mlp_up": by_dev_cols(grads["w_mlp_up"]), "w_mlp_down": by_dev_rows(grads["w_mlp_down"])}
    from_sibling = _swap_with_sibling([full[n] for n in _SHARDED], "reduce_cores")
    c_idx = lax.axis_index("c").astype(jnp.int32).reshape(1)
    chip_part = [_add_sibling(full[n], r, c_idx, "add_cores_" + n) for n, r in zip(_SHARDED, from_sibling)]
    by_chip = _scatter_to_chips(chip_part, "reduce_chips")
    out = {}
    for n, slots in zip(_SHARDED, by_chip):
        res = _adamw(slots, shard2d[n], given["m_" + n][0], given["v_" + n][0], "adamw_" + n)
        out[n] = [r.reshape(shapes[n]) for r in res]

    small_shapes = [shapes[n] for n in _SMALL]
    extra = [grads["conv_w"], loss_part]
    packed_g = _pack([grads[n] for n in _SMALL] + extra)
    zeros_extra = [jnp.zeros_like(e) for e in extra]
    all_g = _exchange_all(packed_g, "exchange_small")
    res = _adamw(all_g, _pack([given[n] for n in _SMALL] + zeros_extra), _pack([given["m_" + n] for n in _SMALL] + zeros_extra),
                 _pack([given["v_" + n] for n in _SMALL] + zeros_extra), "adamw_small")
    extra_shapes = [grads["conv_w"].shape, loss_part.shape]
    unpacked = [_unpack(r, small_shapes + extra_shapes) for r in res]
    for i, n in enumerate(_SMALL):
        out[n] = [u[i] for u in unpacked]
    g_conv_full, loss_all = unpacked[0][-2], unpacked[0][-1]
    width = shapes["conv_w"][-1]
    g_conv = lax.dynamic_slice(g_conv_full, (0, dev * width), (CONV_WIDTH, width))
    res = _adamw(g_conv[None], conv_shard, m_conv_w.reshape(CONV_WIDTH, -1), v_conv_w.reshape(CONV_WIDTH, -1), "adamw_conv_w")
    out["conv_w"] = [r.reshape(shapes["conv_w"]) for r in res]

    loss = loss_all[0, 0]
    return (loss, grad_x[None], *[out[n][0] for n in names], *[out[n][1] for n in names],
            *[out[n][2] for n in names], *[out[n][3] for n in names])
```

```python
import functools
import math

import jax
import jax.numpy as jnp
from jax import lax
from jax.experimental import pallas as pl
from jax.experimental.pallas import tpu as pltpu

F32 = jnp.float32
BF16 = jnp.bfloat16
MESH = pl.DeviceIdType.MESH

D_MODEL = 1024
NORM_EPS = 1e-6
CHUNK = 128
GROUPS = 8
D_INNER = 2048
HEAD_DIM = 64
N_HEADS = 32
D_STATE = 128
CONV_WIDTH = 4
CONV_DIM = 4096
D_FF = 4096
GROUP_W = D_INNER // GROUPS
N_DEV = 8
N_CHIP = 4

ADAM_LR = 0.001
ADAM_B1 = 0.9
ADAM_B2 = 0.999
ADAM_EPS = 1e-08
ADAM_WD = 0.01
ADAM_STEP = 10

MAIN_W = 2 * D_MODEL + D_INNER + CONV_DIM + 2 * D_MODEL
COL_Z = 2048
COL_XBC = 4096
COL_GATE = 8192
DT_PAD = 128

LANES = 128
SUBLANES = 8
VMEM_BYTES_V7X = 64 * 1024 * 1024
VMEM_BODY_TEMP = 24 * 1024 * 1024


def _vmem_limit(block_bytes):
    return int(min(2 * block_bytes + VMEM_BODY_TEMP, VMEM_BYTES_V7X - 8 * 1024 * 1024))


def _nbytes(shape, dtype):
    return math.prod(shape) * jnp.dtype(dtype).itemsize


def _params(sem, block_bytes):
    return pltpu.CompilerParams(dimension_semantics=sem, vmem_limit_bytes=_vmem_limit(block_bytes))


def _sigmoid(x):
    return 1.0 / (1.0 + jnp.exp(-x))


def _softplus(x):
    e = jnp.exp(-jnp.abs(x))
    u = 1.0 + e
    log1p_e = jnp.where(u == 1.0, e, jnp.log(u) * (e / jnp.where(u == 1.0, 1.0, u - 1.0)))
    return jnp.maximum(x, 0.0) + log1p_e


_SQRT_HALF = 0.7071067811865476
_INV_SQRT_2PI = 0.3989422804014327


def _gelu(x):
    return x * (lax.erf(x * _SQRT_HALF) + 1.0) * 0.5


def _gelu_grad(x):
    return 0.5 * (1.0 + lax.erf(x * _SQRT_HALF)) + x * jnp.exp(-0.5 * x * x) * _INV_SQRT_2PI


def _dot(a, b, dims):
    return lax.dot_general(a, b, (dims, ((), ())), preferred_element_type=F32)


_NN = ((1,), (0,))
_NT = ((1,), (1,))
_TN = ((0,), (0,))


def _split3(x):
    hi = x.astype(BF16)
    r1 = x - hi.astype(F32)
    mid = r1.astype(BF16)
    lo = (r1 - mid.astype(F32)).astype(BF16)
    return hi, mid, lo


def _dot_exact_rhs(x, e, dims):
    hi, mid, lo = _split3(x)
    return _dot(hi, e, dims) + _dot(mid, e, dims) + _dot(lo, e, dims)


def _dot_exact_lhs(e, x, dims):
    hi, mid, lo = _split3(x)
    return _dot(e, hi, dims) + _dot(e, mid, dims) + _dot(e, lo, dims)


def _tri(lower):
    r = lax.broadcasted_iota(jnp.int32, (CHUNK, CHUNK), 0)
    c = lax.broadcasted_iota(jnp.int32, (CHUNK, CHUNK), 1)
    return (r >= c) if lower else (r <= c)


def _matmul(a, b, *, mode, tm, tn, tk, out_dtypes, name, epilogue=None, extras=(), extra_specs=(), j_outer=False):
    if mode == "nn":
        (m, k), (_, n) = a.shape, b.shape
    elif mode == "nt":
        (m, k), (n, _) = a.shape, b.shape
    else:
        (k, m), (_, n) = a.shape, b.shape
    assert m % tm == 0 and n % tn == 0 and k % tk == 0, (name, m, n, k, tm, tn, tk)
    nk = k // tk
    n_extra, n_out = len(extras), len(out_dtypes)
    dims = {"nn": _NN, "nt": _NT, "tn": _TN}[mode]
    if epilogue is None:
        def epilogue(acc, ex, outs):
            outs[0][...] = acc.astype(outs[0].dtype)

    def body(*refs):
        a_ref, b_ref = refs[0], refs[1]
        ex_refs = refs[2:2 + n_extra]
        outs = refs[2 + n_extra:2 + n_extra + n_out]
        p = _dot(a_ref[...], b_ref[...], dims)
        if nk == 1:
            epilogue(p, ex_refs, outs)
        else:
            acc_ref = refs[2 + n_extra + n_out]
            kk = pl.program_id(2)

            @pl.when(kk == 0)
            def _():
                acc_ref[...] = p

            @pl.when(kk > 0)
            def _():
                acc_ref[...] += p

            @pl.when(kk == nk - 1)
            def _():
                epilogue(acc_ref[...], ex_refs, outs)

    if j_outer:
        grid = (n // tn, m // tm, nk)
        ij = lambda g0, g1: (g1, g0)
    else:
        grid = (m // tm, n // tn, nk)
        ij = lambda g0, g1: (g0, g1)

    def wrap(fn):
        return lambda g0, g1, kk: fn(*ij(g0, g1), kk)

    if mode == "nn":
        a_spec = pl.BlockSpec((tm, tk), wrap(lambda i, j, kk: (i, kk)))
        b_spec = pl.BlockSpec((tk, tn), wrap(lambda i, j, kk: (kk, j)))
        a_blk, b_blk = (tm, tk), (tk, tn)
    elif mode == "nt":
        a_spec = pl.BlockSpec((tm, tk), wrap(lambda i, j, kk: (i, kk)))
        b_spec = pl.BlockSpec((tn, tk), wrap(lambda i, j, kk: (j, kk)))
        a_blk, b_blk = (tm, tk), (tn, tk)
    else:
        a_spec = pl.BlockSpec((tk, tm), wrap(lambda i, j, kk: (kk, i)))
        b_spec = pl.BlockSpec((tk, tn), wrap(lambda i, j, kk: (kk, j)))
        a_blk, b_blk = (tk, tm), (tk, tn)
    ex_specs = [pl.BlockSpec(shape, wrap(lambda i, j, kk, f=f: f(i, j))) for shape, f in extra_specs]
    out_spec = [pl.BlockSpec((tm, tn), wrap(lambda i, j, kk: (i, j))) for _ in out_dtypes]
    out_shape = [jax.ShapeDtypeStruct((m, n), dt) for dt in out_dtypes]
    blk = (_nbytes(a_blk, a.dtype) + _nbytes(b_blk, b.dtype) + sum(_nbytes(s, F32) for s, _ in extra_specs)
           + sum(_nbytes((tm, tn), dt) for dt in out_dtypes) + _nbytes((tm, tn), F32))
    res = pl.pallas_call(
        body, name=name, grid=grid,
        in_specs=[a_spec, b_spec] + ex_specs, out_specs=out_spec, out_shape=out_shape,
        scratch_shapes=[pltpu.VMEM((tm, tn), F32)] if nk > 1 else [],
        compiler_params=_params(("parallel", "parallel", "arbitrary"), blk),
    )(a, b, *extras)
    return res[0] if n_out == 1 else res


ROW_TILE = 256


def _row_spec(width, col_block=0, tile=ROW_TILE):
    return pl.BlockSpec((tile, width), lambda i, cb=col_block: (i, cb))


def _vec_spec(width, col_block=0):
    return pl.BlockSpec((1, width), lambda i, cb=col_block: (0, cb))


def _rms_fwd(x, g, name):
    t = x.shape[0]

    def body(x_ref, g_ref, h_ref):
        xv = x_ref[...]
        r = lax.rsqrt(jnp.mean(xv * xv, axis=-1, keepdims=True) + NORM_EPS)
        h_ref[...] = (xv * r * g_ref[...]).astype(BF16)

    return pl.pallas_call(
        body, name=name, grid=(t // ROW_TILE,),
        in_specs=[_row_spec(D_MODEL), _vec_spec(D_MODEL)], out_specs=_row_spec(D_MODEL),
        out_shape=jax.ShapeDtypeStruct((t, D_MODEL), BF16),
        compiler_params=_params(("parallel",), 3 * _nbytes((ROW_TILE, D_MODEL), F32)),
    )(x, g)


def _rms_bwd(x, g, dh, dres, name):
    t = x.shape[0]

    def body(x_ref, g_ref, dh_ref, dres_ref, dx_ref, dxb_ref, gg_ref):
        xv = x_ref[...]
        r = lax.rsqrt(jnp.mean(xv * xv, axis=-1, keepdims=True) + NORM_EPS)
        xh = xv * r
        dhv = dh_ref[...]
        dyg = dhv * g_ref[...]
        dx = r * (dyg - xh * jnp.mean(dyg * xh, axis=-1, keepdims=True)) + dres_ref[...]
        dx_ref[...] = dx
        dxb_ref[...] = dx.astype(BF16)

        @pl.when(pl.program_id(0) == 0)
        def _():
            gg_ref[...] = jnp.zeros_like(gg_ref)

        gg_ref[...] += jnp.sum(dhv * xh, axis=0, keepdims=True)

    return pl.pallas_call(
        body, name=name, grid=(t // ROW_TILE,),
        in_specs=[_row_spec(D_MODEL), _vec_spec(D_MODEL), _row_spec(D_MODEL), _row_spec(D_MODEL)],
        out_specs=[_row_spec(D_MODEL), _row_spec(D_MODEL), _vec_spec(D_MODEL)],
        out_shape=[jax.ShapeDtypeStruct((t, D_MODEL), F32), jax.ShapeDtypeStruct((t, D_MODEL), BF16),
                   jax.ShapeDtypeStruct((1, D_MODEL), F32)],
        compiler_params=_params(("arbitrary",), 5 * _nbytes((ROW_TILE, D_MODEL), F32)),
    )(x, g, dh, dres)


def _loss_head(x2, gf, target, name):
    t = x2.shape[0]

    def body(x_ref, g_ref, t_ref, loss_ref, dx_ref, dxb_ref, gg_ref, tot_ref):
        xv = x_ref[...]
        gv = g_ref[...]
        r = lax.rsqrt(jnp.mean(xv * xv, axis=-1, keepdims=True) + NORM_EPS)
        xh = xv * r
        err = xh * gv - t_ref[...]
        dy = err * (1.0 / D_MODEL)
        dyg = dy * gv
        dx = r * (dyg - xh * jnp.mean(dyg * xh, axis=-1, keepdims=True))
        dx_ref[...] = dx
        dxb_ref[...] = dx.astype(BF16)

        @pl.when(pl.program_id(0) == 0)
        def _():
            gg_ref[...] = jnp.zeros_like(gg_ref)
            loss_ref[...] = jnp.zeros_like(loss_ref)

        gg_ref[...] += jnp.sum(dy * xh, axis=0, keepdims=True)
        loss_ref[...] += jnp.sum(err * err, axis=0, keepdims=True)
        tot_ref[...] = jnp.broadcast_to(jnp.sum(loss_ref[...], axis=1, keepdims=True) * (0.5 / D_MODEL), tot_ref.shape)

    return pl.pallas_call(
        body, name=name, grid=(t // ROW_TILE,),
        in_specs=[_row_spec(D_MODEL), _vec_spec(D_MODEL), _row_spec(D_MODEL)],
        out_specs=[_vec_spec(D_MODEL), _row_spec(D_MODEL), _row_spec(D_MODEL), _vec_spec(D_MODEL), _vec_spec(LANES)],
        out_shape=[jax.ShapeDtypeStruct((1, D_MODEL), F32), jax.ShapeDtypeStruct((t, D_MODEL), F32),
                   jax.ShapeDtypeStruct((t, D_MODEL), BF16), jax.ShapeDtypeStruct((1, D_MODEL), F32),
                   jax.ShapeDtypeStruct((1, LANES), F32)],
        compiler_params=_params(("arbitrary",), 5 * _nbytes((ROW_TILE, D_MODEL), F32)),
    )(x2, gf, target)


def _merge_fwd(pa, pb, proj, b_gates, name):
    t = pa.shape[0]
    gcb = COL_GATE // D_MODEL

    def body(pa_ref, pb_ref, la_ref, lb_ref, ba_ref, bb_ref, out_ref):
        ga = _sigmoid(la_ref[...] + ba_ref[...])
        gb = _sigmoid(lb_ref[...] + bb_ref[...])
        out_ref[...] = (ga * pa_ref[...] + gb * pb_ref[...]).astype(BF16)

    return pl.pallas_call(
        body, name=name, grid=(t // ROW_TILE,),
        in_specs=[_row_spec(D_MODEL), _row_spec(D_MODEL), _row_spec(D_MODEL, gcb), _row_spec(D_MODEL, gcb + 1),
                  _vec_spec(D_MODEL, 0), _vec_spec(D_MODEL, 1)],
        out_specs=_row_spec(D_MODEL),
        out_shape=jax.ShapeDtypeStruct((t, D_MODEL), BF16),
        compiler_params=_params(("parallel",), 5 * _nbytes((ROW_TILE, D_MODEL), F32)),
    )(pa, pb, proj, proj, b_gates, b_gates)


def _merge_bwd(dmerged, pa, pb, proj, b_gates, name):
    t = pa.shape[0]
    gcb = COL_GATE // D_MODEL

    def body(dm_ref, pa_ref, pb_ref, la_ref, lb_ref, ba_ref, bb_ref, dpa_ref, dpb_ref, dgl_ref, gb_ref):
        dm = dm_ref[...]
        ga = _sigmoid(la_ref[...] + ba_ref[...])
        gb = _sigmoid(lb_ref[...] + bb_ref[...])
        dpa_ref[...] = (dm * ga).astype(BF16)
        dpb_ref[...] = (dm * gb).astype(BF16)
        dla = dm * pa_ref[...] * ga * (1.0 - ga)
        dlb = dm * pb_ref[...] * gb * (1.0 - gb)
        dgl_ref[:, :D_MODEL] = dla.astype(BF16)
        dgl_ref[:, D_MODEL:] = dlb.astype(BF16)

        @pl.when(pl.program_id(0) == 0)
        def _():
            gb_ref[...] = jnp.zeros_like(gb_ref)

        gb_ref[:, :D_MODEL] += jnp.sum(dla, axis=0, keepdims=True)
        gb_ref[:, D_MODEL:] += jnp.sum(dlb, axis=0, keepdims=True)

    return pl.pallas_call(
        body, name=name, grid=(t // ROW_TILE,),
        in_specs=[_row_spec(D_MODEL), _row_spec(D_MODEL), _row_spec(D_MODEL), _row_spec(D_MODEL, gcb),
                  _row_spec(D_MODEL, gcb + 1), _vec_spec(D_MODEL, 0), _vec_spec(D_MODEL, 1)],
        out_specs=[_row_spec(D_MODEL), _row_spec(D_MODEL), _row_spec(2 * D_MODEL, COL_GATE // (2 * D_MODEL)),
                   _vec_spec(2 * D_MODEL)],
        out_shape=[jax.ShapeDtypeStruct((t, D_MODEL), BF16), jax.ShapeDtypeStruct((t, D_MODEL), BF16),
                   jax.ShapeDtypeStruct((t, MAIN_W), BF16), jax.ShapeDtypeStruct((1, 2 * D_MODEL), F32)],
        compiler_params=_params(("arbitrary",), 8 * _nbytes((ROW_TILE, D_MODEL), F32)),
    )(dmerged, pa, pb, proj, proj, b_gates, b_gates)


GMLP_TILE = 512
GMLP_NC = GMLP_TILE // CHUNK


def _gmlp_common(u_pre, v_pre, vg, vb):
    u = _gelu(u_pre)
    v = _gelu(v_pre)
    mu = jnp.mean(v, axis=-1, keepdims=True)
    vc = v - mu
    rstd = lax.rsqrt(jnp.mean(vc * vc, axis=-1, keepdims=True) + NORM_EPS)
    vh = vc * rstd
    vn = vh * vg + vb
    return u, vh, vn, rstd


def _chunks_to_lanes(x, g):
    return jnp.concatenate([x[c * CHUNK:(c + 1) * CHUNK, g * CHUNK:(g + 1) * CHUNK] for c in range(GMLP_NC)], axis=1)


def _gmlp_fwd(proj, vg, vb, wsp, bsp_t, name):
    t = proj.shape[0]

    def body(u_ref, v_ref, vg_ref, vb_ref, w_ref, b_ref, ya_ref):
        u, _, vn, _ = _gmlp_common(u_ref[...], v_ref[...], vg_ref[...], vb_ref[...])
        mask = _tri(True)
        bt = b_ref[...]
        for g in range(GROUPS):
            w = jnp.where(mask, w_ref[g], 0.0).astype(BF16)
            vcat = _chunks_to_lanes(vn, g).astype(BF16)
            s = _dot(w, vcat, _NN) + bt[:, g:g + 1]
            for c in range(GMLP_NC):
                rows, cols = slice(c * CHUNK, (c + 1) * CHUNK), slice(g * CHUNK, (g + 1) * CHUNK)
                ya_ref[rows, cols] = (u[rows, cols] * s[:, c * CHUNK:(c + 1) * CHUNK]).astype(BF16)

    return pl.pallas_call(
        body, name=name, grid=(t // GMLP_TILE,),
        in_specs=[_row_spec(D_MODEL, 0, GMLP_TILE), _row_spec(D_MODEL, 1, GMLP_TILE), _vec_spec(D_MODEL),
                  _vec_spec(D_MODEL), pl.BlockSpec((GROUPS, CHUNK, CHUNK), lambda i: (0, 0, 0)),
                  pl.BlockSpec((CHUNK, GROUPS), lambda i: (0, 0))],
        out_specs=_row_spec(D_MODEL, 0, GMLP_TILE),
        out_shape=jax.ShapeDtypeStruct((t, D_MODEL), BF16),
        compiler_params=_params(("parallel",), 3 * _nbytes((GMLP_TILE, D_MODEL), F32)),
    )(proj, proj, vg, vb, wsp, bsp_t)


def _gmlp_bwd(proj, dya, vg, vb, wsp, bsp_t, dproj, name):
    t = proj.shape[0]

    def body(u_ref, v_ref, dya_ref, vg_ref, vb_ref, w_ref, b_ref, dproj_in, duv_ref, gw_ref, gbt_ref, gvg_ref, gvb_ref,
             dvn_scr, du_scr):
        del dproj_in
        u_pre, v_pre = u_ref[...], v_ref[...]
        vgv = vg_ref[...]
        u, vh, vn, rstd = _gmlp_common(u_pre, v_pre, vgv, vb_ref[...])
        dya = dya_ref[...]
        mask = _tri(True)
        bt = b_ref[...]
        first = pl.program_id(0) == 0

        @pl.when(first)
        def _():
            gw_ref[...] = jnp.zeros_like(gw_ref)
            gbt_ref[...] = jnp.zeros_like(gbt_ref)
            gvg_ref[...] = jnp.zeros_like(gvg_ref)
            gvb_ref[...] = jnp.zeros_like(gvb_ref)

        lane = lax.broadcasted_iota(jnp.int32, (CHUNK, GROUPS), 1)
        gbt = jnp.zeros((CHUNK, GROUPS), F32)
        for g in range(GROUPS):
            w = jnp.where(mask, w_ref[g], 0.0).astype(BF16)
            vcat = _chunks_to_lanes(vn, g).astype(BF16)
            s = _dot(w, vcat, _NN) + bt[:, g:g + 1]
            ds = _chunks_to_lanes(dya * u, g)
            gbt = jnp.where(lane == g, jnp.sum(ds, axis=1, keepdims=True), gbt)
            dsb = ds.astype(BF16)
            gw_ref[g] += jnp.where(mask, _dot(dsb, vcat, _NT), 0.0)
            dv = _dot(w, dsb, _TN)
            for c in range(GMLP_NC):
                rows, cols = slice(c * CHUNK, (c + 1) * CHUNK), slice(g * CHUNK, (g + 1) * CHUNK)
                dvn_scr[rows, cols] = dv[:, c * CHUNK:(c + 1) * CHUNK]
                du_scr[rows, cols] = dya[rows, cols] * s[:, c * CHUNK:(c + 1) * CHUNK]
        gbt_ref[...] += gbt
        dvn = dvn_scr[...]
        gvg_ref[...] += jnp.sum(dvn * vh, axis=0, keepdims=True)
        gvb_ref[...] += jnp.sum(dvn, axis=0, keepdims=True)
        dvh = dvn * vgv
        dv = rstd * (dvh - jnp.mean(dvh, axis=-1, keepdims=True) - vh * jnp.mean(dvh * vh, axis=-1, keepdims=True))
        duv_ref[:, :D_MODEL] = (du_scr[...] * _gelu_grad(u_pre)).astype(BF16)
        duv_ref[:, D_MODEL:] = (dv * _gelu_grad(v_pre)).astype(BF16)

    return pl.pallas_call(
        body, name=name, grid=(t // GMLP_TILE,),
        in_specs=[_row_spec(D_MODEL, 0, GMLP_TILE), _row_spec(D_MODEL, 1, GMLP_TILE), _row_spec(D_MODEL, 0, GMLP_TILE),
                  _vec_spec(D_MODEL), _vec_spec(D_MODEL), pl.BlockSpec((GROUPS, CHUNK, CHUNK), lambda i: (0, 0, 0)),
                  pl.BlockSpec((CHUNK, GROUPS), lambda i: (0, 0)), pl.BlockSpec(memory_space=pl.ANY)],
        out_specs=[_row_spec(2 * D_MODEL, 0, GMLP_TILE), pl.BlockSpec((GROUPS, CHUNK, CHUNK), lambda i: (0, 0, 0)),
                   pl.BlockSpec((CHUNK, GROUPS), lambda i: (0, 0)), _vec_spec(D_MODEL), _vec_spec(D_MODEL)],
        out_shape=[jax.ShapeDtypeStruct(dproj.shape, BF16), jax.ShapeDtypeStruct((GROUPS, CHUNK, CHUNK), F32),
                   jax.ShapeDtypeStruct((CHUNK, GROUPS), F32), jax.ShapeDtypeStruct((1, D_MODEL), F32),
                   jax.ShapeDtypeStruct((1, D_MODEL), F32)],
        scratch_shapes=[pltpu.VMEM((GMLP_TILE, D_MODEL), F32), pltpu.VMEM((GMLP_TILE, D_MODEL), F32)],
        input_output_aliases={7: 0},
        compiler_params=_params(("arbitrary",), 6 * _nbytes((GMLP_TILE, D_MODEL), F32)),
    )(proj, proj, dya, vg, vb, wsp, bsp_t, dproj)


CONV_TILE = 512
CONV_COLS = 1024
CONV_RB = 32
HALO = SUBLANES


def _conv_fwd(proj, cw, cb, name):
    t = proj.shape[0]
    nj = CONV_DIM // CONV_COLS
    xcb = COL_XBC // CONV_COLS
    rb = CONV_TILE // HALO

    def body(x_ref, prev_ref, cw_ref, cb_ref, pre_ref, xc_ref):
        i = pl.program_id(1)
        cw_v = cw_ref[...]
        cb_v = cb_ref[...]
        for b in range(CONV_TILE // CONV_RB):
            if b == 0:
                ext = jnp.concatenate([jnp.where(i > 0, prev_ref[...], 0.0), x_ref[:CONV_RB, :]], axis=0)
            else:
                ext = x_ref[b * CONV_RB - HALO:(b + 1) * CONV_RB, :]
            pre = cb_v + cw_v[CONV_WIDTH - 1:CONV_WIDTH, :] * ext[HALO:, :]
            for k in range(CONV_WIDTH - 1):
                back = CONV_WIDTH - 1 - k
                pre = pre + cw_v[k:k + 1, :] * pltpu.roll(ext, back, 0)[HALO:, :]
            pre_ref[b * CONV_RB:(b + 1) * CONV_RB, :] = pre
            xc_ref[b * CONV_RB:(b + 1) * CONV_RB, :] = pre * _sigmoid(pre)

    tile = pl.BlockSpec((CONV_TILE, CONV_COLS), lambda j, i: (i, j))
    return pl.pallas_call(
        body, name=name, grid=(nj, t // CONV_TILE),
        in_specs=[pl.BlockSpec((CONV_TILE, CONV_COLS), lambda j, i: (i, xcb + j)),
                  pl.BlockSpec((HALO, CONV_COLS), lambda j, i: (jnp.maximum(i * rb - 1, 0), xcb + j)),
                  pl.BlockSpec((CONV_WIDTH, CONV_COLS), lambda j, i: (0, j)),
                  pl.BlockSpec((1, CONV_COLS), lambda j, i: (0, j))],
        out_specs=[tile, tile],
        out_shape=[jax.ShapeDtypeStruct((t, CONV_DIM), F32), jax.ShapeDtypeStruct((t, CONV_DIM), F32)],
        compiler_params=_params(("parallel", "parallel"), 4 * _nbytes((CONV_TILE, CONV_COLS), F32)),
    )(proj, proj, cw, cb)


def _fold_rows(v):
    out = v[:SUBLANES]
    for r in range(1, v.shape[0] // SUBLANES):
        out = out + v[r * SUBLANES:(r + 1) * SUBLANES]
    return out


def _conv_bwd(proj, pre, dxc, cw, dproj, name):
    t = proj.shape[0]
    nj = CONV_DIM // CONV_COLS
    ni = t // CONV_TILE
    xcb = COL_XBC // CONV_COLS
    rb = CONV_TILE // HALO
    last_rb = t // HALO - 1

    def body(x_ref, p_ref, pnext_ref, d_ref, dnext_ref, cw_ref, dproj_in, dx_ref, gw_ref, gb_ref):
        del dproj_in
        i = pl.program_id(1)
        cw_v = cw_ref[...]

        def dpre_of(p, d):
            sg = _sigmoid(p)
            return d * sg * (1.0 + p * (1.0 - sg))

        @pl.when(i == 0)
        def _():
            gw_ref[...] = jnp.zeros_like(gw_ref)
            gb_ref[...] = jnp.zeros_like(gb_ref)

        head = dpre_of(pnext_ref[...], jnp.where(i < ni - 1, dnext_ref[...], 0.0))
        gb_acc = jnp.zeros((SUBLANES, CONV_COLS), F32)
        gw_acc = [jnp.zeros((SUBLANES, CONV_COLS), F32) for _ in range(CONV_WIDTH)]
        for b in reversed(range(CONV_TILE // CONV_RB)):
            rows = slice(b * CONV_RB, (b + 1) * CONV_RB)
            cur = dpre_of(p_ref[rows, :], d_ref[rows, :])
            ext = jnp.concatenate([cur, head], axis=0)
            xv = x_ref[rows, :]
            dx = None
            for k in range(CONV_WIDTH):
                shift = CONV_WIDTH - 1 - k
                win = cur if shift == 0 else pltpu.roll(ext, CONV_RB + HALO - shift, 0)[:CONV_RB, :]
                term = cw_v[k:k + 1, :] * win
                dx = term if dx is None else dx + term
                gw_acc[k] = gw_acc[k] + _fold_rows(win * xv)
            dx_ref[rows, :] = dx.astype(BF16)
            gb_acc = gb_acc + _fold_rows(cur)
            head = cur[:HALO]
        gb_ref[...] += jnp.sum(gb_acc, axis=0, keepdims=True)
        for k in range(CONV_WIDTH):
            gw_ref[k:k + 1, :] += jnp.sum(gw_acc[k], axis=0, keepdims=True)

    tile = pl.BlockSpec((CONV_TILE, CONV_COLS), lambda j, i: (i, j))
    after = pl.BlockSpec((HALO, CONV_COLS), lambda j, i: (jnp.minimum((i + 1) * rb, last_rb), j))
    return pl.pallas_call(
        body, name=name, grid=(nj, ni),
        in_specs=[pl.BlockSpec((CONV_TILE, CONV_COLS), lambda j, i: (i, xcb + j)), tile, after, tile, after,
                  pl.BlockSpec((CONV_WIDTH, CONV_COLS), lambda j, i: (0, j)),
                  pl.BlockSpec(memory_space=pl.ANY)],
        out_specs=[pl.BlockSpec((CONV_TILE, CONV_COLS), lambda j, i: (i, xcb + j)),
                   pl.BlockSpec((CONV_WIDTH, CONV_COLS), lambda j, i: (0, j)),
                   pl.BlockSpec((1, CONV_COLS), lambda j, i: (0, j))],
        out_shape=[jax.ShapeDtypeStruct(dproj.shape, BF16), jax.ShapeDtypeStruct((CONV_WIDTH, CONV_DIM), F32),
                   jax.ShapeDtypeStruct((1, CONV_DIM), F32)],
        input_output_aliases={6: 0},
        compiler_params=_params(("parallel", "arbitrary"), 4 * _nbytes((CONV_TILE, CONV_COLS), F32)),
    )(proj, pre, pre, dxc, dxc, cw, dproj)


def _ssd_decays(dt_raw, dtb, alog, e_bf, tril_bf):
    dtv = _softplus(dt_raw + dtb)
    a = -jnp.exp(alog)
    cs = _dot_exact_lhs(tril_bf, dtv * a, _NN)
    cs_last = cs[CHUNK - 1:CHUNK, :]
    stack = jnp.concatenate([dtv, jnp.exp(cs), jnp.exp(cs_last - cs)], axis=0)
    full = _head_expand(stack, e_bf)
    return dtv, a, cs, full[:CHUNK], full[CHUNK:2 * CHUNK], full[2 * CHUNK:]


def _split2(x):
    hi = x.astype(BF16)
    return hi, (x - hi.astype(F32)).astype(BF16)


def _head_expand(x, e_bf):
    hi, mid = _split2(x)
    return _dot(hi, e_bf, _NN) + _dot(mid, e_bf, _NN)


def _head_sums(x, e_bf):
    hi, mid = _split2(x)
    return _dot(hi, e_bf, _NT) + _dot(mid, e_bf, _NT)


def _head_mats(cs, cs_t, cb, h, mask):
    seg = cs[:, h:h + 1] - cs_t[h:h + 1, :]
    lmat = jnp.exp(jnp.where(mask, seg, -jnp.inf))
    return lmat, cb * lmat


def _ssd_fwd(xc, proj, dt_raw, dtb, alog, dskip_full, ng, e_bf, name):
    t = xc.shape[0]
    nc = t // CHUNK
    zcb = COL_Z // D_INNER

    def body(xc_ref, z_ref, dt_ref, dtb_ref, alog_ref, dsk_ref, ng_ref, e_ref, y_ref, yb_ref, sprev_ref, s_scr):
        @pl.when(pl.program_id(0) == 0)
        def _():
            s_scr[...] = jnp.zeros_like(s_scr)

        mask = _tri(True)
        tril_bf = mask.astype(BF16)
        e_v = e_ref[...]
        _, _, cs, dt_full, ecs_full, decay_full = _ssd_decays(dt_ref[...], dtb_ref[...], alog_ref[...], e_v, tril_bf)
        cs_t = cs.T
        sprev_ref[0] = s_scr[...]
        for g in range(GROUPS):
            gc = slice(g * GROUP_W, (g + 1) * GROUP_W)
            xs = xc_ref[:, gc]
            xdt = xs * dt_full[:, gc]
            xdt_b = xdt.astype(BF16)
            xdec = (xdt * decay_full[:, gc]).astype(BF16)
            bg = xc_ref[:, D_INNER + g * D_STATE:D_INNER + (g + 1) * D_STATE].astype(BF16)
            cg = xc_ref[:, D_INNER + GROUPS * D_STATE + g * D_STATE:D_INNER + GROUPS * D_STATE + (g + 1) * D_STATE].astype(BF16)
            cb = _dot(cg, bg, _NT)
            s_prev = s_scr[:, gc]
            y_off = ecs_full[:, gc] * _dot(cg, s_prev.astype(BF16), _NN)
            s_scr[:, gc] = s_prev * ecs_full[CHUNK - 1:CHUNK, gc] + _dot(bg, xdec, _TN)
            parts = []
            for r in range(GROUP_W // HEAD_DIM):
                h = g * (GROUP_W // HEAD_DIM) + r
                _, m = _head_mats(cs, cs_t, cb, h, mask)
                parts.append(_dot(m.astype(BF16), xdt_b[:, r * HEAD_DIM:(r + 1) * HEAD_DIM], _NN))
            yg = jnp.concatenate(parts, axis=1) + y_off + dsk_ref[:, gc] * xs
            y_ref[:, gc] = yg
            zv = z_ref[:, gc]
            ygate = yg * (zv * _sigmoid(zv))
            rstd = lax.rsqrt(jnp.mean(ygate * ygate, axis=-1, keepdims=True) + NORM_EPS)
            yb_ref[:, gc] = (ygate * rstd * ng_ref[:, gc]).astype(BF16)

    vec = lambda w: pl.BlockSpec((1, w), lambda i: (0, 0))
    blk = _nbytes((CHUNK, CONV_DIM), F32) + 3 * _nbytes((CHUNK, D_INNER), F32) + _nbytes((D_STATE, D_INNER), F32)
    return pl.pallas_call(
        body, name=name, grid=(nc,),
        in_specs=[pl.BlockSpec((CHUNK, CONV_DIM), lambda i: (i, 0)), pl.BlockSpec((CHUNK, D_INNER), lambda i: (i, zcb)),
                  pl.BlockSpec((CHUNK, DT_PAD), lambda i: (i, 0)), vec(DT_PAD), vec(DT_PAD), vec(D_INNER), vec(D_INNER),
                  pl.BlockSpec((DT_PAD, D_INNER), lambda i: (0, 0))],
        out_specs=[pl.BlockSpec((CHUNK, D_INNER), lambda i: (i, 0)), pl.BlockSpec((CHUNK, D_INNER), lambda i: (i, 0)),
                   pl.BlockSpec((1, D_STATE, D_INNER), lambda i: (i, 0, 0))],
        out_shape=[jax.ShapeDtypeStruct((t, D_INNER), F32), jax.ShapeDtypeStruct((t, D_INNER), BF16),
                   jax.ShapeDtypeStruct((nc, D_STATE, D_INNER), F32)],
        scratch_shapes=[pltpu.VMEM((D_STATE, D_INNER), F32)],
        compiler_params=_params(("arbitrary",), blk),
    )(xc, proj, dt_raw, dtb, alog, dskip_full, ng, e_bf)


def _ssd_bwd(dyb, y, xc, proj, dt_raw, sprev, dtb, alog, dskip_full, ng, e_bf, dproj, name):
    t = xc.shape[0]
    nc = t // CHUNK
    zcb = COL_Z // D_INNER
    hpg = GROUP_W // HEAD_DIM
    rev = lambda i: nc - 1 - i

    def body(dyb_ref, y_ref, xc_ref, z_ref, dt_ref, sprev_ref, dtb_ref, alog_ref, dsk_ref, ng_ref, e_ref, dproj_in,
             dz_ref, dxc_ref, ddt_ref, gng_ref, gdsk_ref, galog_ref, gdtb_ref, ds_scr, sums_scr):
        del dproj_in

        @pl.when(pl.program_id(0) == 0)
        def _():
            ds_scr[...] = jnp.zeros_like(ds_scr)
            gng_ref[...] = jnp.zeros_like(gng_ref)
            gdsk_ref[...] = jnp.zeros_like(gdsk_ref)
            galog_ref[...] = jnp.zeros_like(galog_ref)
            gdtb_ref[...] = jnp.zeros_like(gdtb_ref)

        mask = _tri(True)
        tril_bf = mask.astype(BF16)
        triu_bf = _tri(False).astype(BF16)
        e_v = e_ref[...]
        dt_in = dt_ref[...] + dtb_ref[...]
        dtv, a, cs, dt_full, ecs_full, decay_full = _ssd_decays(dt_ref[...], dtb_ref[...], alog_ref[...], e_v, tril_bf)
        cs_t = cs.T

        lane_h = lax.broadcasted_iota(jnp.int32, (CHUNK, DT_PAD), 1)
        sub_h = lax.broadcasted_iota(jnp.int32, (DT_PAD, CHUNK), 0)
        dcs_rows = jnp.zeros((CHUNK, DT_PAD), F32)
        dcs_cols_t = jnp.zeros((DT_PAD, CHUNK), F32)
        last_cols, dsk_cols = [], []
        for g in range(GROUPS):
            gc = slice(g * GROUP_W, (g + 1) * GROUP_W)
            b_cols = slice(D_INNER + g * D_STATE, D_INNER + (g + 1) * D_STATE)
            c_cols = slice(D_INNER + GROUPS * D_STATE + g * D_STATE, D_INNER + GROUPS * D_STATE + (g + 1) * D_STATE)
            xs = xc_ref[:, gc]
            xdt = xs * dt_full[:, gc]
            xdt_b = xdt.astype(BF16)
            xdec = xdt * decay_full[:, gc]
            xdec_b = xdec.astype(BF16)
            zv = z_ref[:, gc]
            sg = _sigmoid(zv)
            gate = zv * sg
            yv = y_ref[:, gc]
            dybv = dyb_ref[:, gc]
            ygate = yv * gate
            rstd = lax.rsqrt(jnp.mean(ygate * ygate, axis=-1, keepdims=True) + NORM_EPS)
            yn = ygate * rstd
            gng_ref[:, gc] += jnp.sum(dybv * yn, axis=0, keepdims=True)
            dyn = dybv * ng_ref[:, gc]
            dyg = rstd * (dyn - yn * jnp.mean(dyn * yn, axis=-1, keepdims=True))
            dz_ref[:, gc] = (dyg * yv * sg * (1.0 + zv * (1.0 - sg))).astype(BF16)
            dy = dyg * gate
            dy_b = dy.astype(BF16)
            dyo = dy * ecs_full[:, gc]
            dyo_b = dyo.astype(BF16)
            dsk_cols.append(jnp.sum(dy * xs, axis=0, keepdims=True))

            bg = xc_ref[:, b_cols].astype(BF16)
            cg = xc_ref[:, c_cols].astype(BF16)
            s_prev = sprev_ref[0, :, gc]
            s_prev_b = s_prev.astype(BF16)
            dsg = ds_scr[:, gc]
            dsg_b = dsg.astype(BF16)
            cb = _dot(cg, bg, _NT)
            c_s = _dot(cg, s_prev_b, _NN)
            b_ds = _dot(bg, dsg_b, _NN)
            dcb = jnp.zeros((CHUNK, CHUNK), F32)
            parts = []
            for r in range(hpg):
                h = g * hpg + r
                hc = slice(r * HEAD_DIM, (r + 1) * HEAD_DIM)
                lmat, m = _head_mats(cs, cs_t, cb, h, mask)
                dm = _dot(dy_b[:, hc], xdt_b[:, hc], _NT)
                parts.append(_dot(m.astype(BF16), dy_b[:, hc], _TN))
                dcb = dcb + dm * lmat
                w = dm * m
                dcs_rows = jnp.where(lane_h == h, jnp.sum(w, axis=1, keepdims=True), dcs_rows)
                dcs_cols_t = jnp.where(sub_h == h, jnp.sum(w, axis=0, keepdims=True), dcs_cols_t)
            dxdt = jnp.concatenate(parts, axis=1) + decay_full[:, gc] * b_ds
            dcb_b = dcb.astype(BF16)
            dxc_ref[:, c_cols] = _dot(dcb_b, bg, _NN) + _dot(dyo_b, s_prev_b, _NT)
            dxc_ref[:, b_cols] = _dot(dcb_b, cg, _TN) + _dot(xdec_b, dsg_b, _NT)
            cdec = ecs_full[CHUNK - 1:CHUNK, gc]
            ds_scr[:, gc] = _dot(cg, dyo_b, _TN) + cdec * dsg
            dxc_ref[:, gc] = dxdt * dt_full[:, gc] + dsk_ref[:, gc] * dy
            dec_prod = xdec * b_ds
            sums_scr[:CHUNK, gc] = dyo * c_s - dec_prod
            sums_scr[CHUNK:, gc] = dxdt * xs
            last_cols.append(jnp.sum(dec_prod, axis=0, keepdims=True) + cdec * jnp.sum(dsg * s_prev, axis=0, keepdims=True))
        t_sums = _head_sums(sums_scr[...], e_v)
        tail = jnp.concatenate([jnp.concatenate(last_cols, axis=1), jnp.concatenate(dsk_cols, axis=1),
                                jnp.zeros((SUBLANES - 2, D_INNER), F32)], axis=0)
        t_tail = _dot_exact_rhs(tail, e_v, _NT)
        gdsk_ref[...] += t_tail[1:2, :]
        row = lax.broadcasted_iota(jnp.int32, (CHUNK, DT_PAD), 0)
        dcs = dcs_rows - dcs_cols_t.T + t_sums[:CHUNK] + jnp.where(row == CHUNK - 1, t_tail[0:1, :], 0.0)
        dda = _dot_exact_lhs(triu_bf, dcs, _NN)
        galog_ref[...] += jnp.sum(dda * dtv, axis=0, keepdims=True) * a
        ddt = dda * a + t_sums[CHUNK:]
        ddt_raw = jnp.where(lane_h < N_HEADS, ddt * _sigmoid(dt_in), 0.0)
        gdtb_ref[...] += jnp.sum(ddt_raw, axis=0, keepdims=True)
        ddt_ref[...] = ddt_raw.astype(BF16)

    vec = lambda w: pl.BlockSpec((1, w), lambda i: (0, 0))
    blk = (2 * _nbytes((CHUNK, CONV_DIM), F32) + 4 * _nbytes((CHUNK, D_INNER), F32) + 4 * _nbytes((D_STATE, D_INNER), F32))
    return pl.pallas_call(
        body, name=name, grid=(nc,),
        in_specs=[pl.BlockSpec((CHUNK, D_INNER), lambda i: (rev(i), 0)), pl.BlockSpec((CHUNK, D_INNER), lambda i: (rev(i), 0)),
                  pl.BlockSpec((CHUNK, CONV_DIM), lambda i: (rev(i), 0)), pl.BlockSpec((CHUNK, D_INNER), lambda i: (rev(i), zcb)),
                  pl.BlockSpec((CHUNK, DT_PAD), lambda i: (rev(i), 0)), pl.BlockSpec((1, D_STATE, D_INNER), lambda i: (rev(i), 0, 0)),
                  vec(DT_PAD), vec(DT_PAD), vec(D_INNER), vec(D_INNER), pl.BlockSpec((DT_PAD, D_INNER), lambda i: (0, 0)),
                  pl.BlockSpec(memory_space=pl.ANY)],
        out_specs=[pl.BlockSpec((CHUNK, D_INNER), lambda i: (rev(i), zcb)), pl.BlockSpec((CHUNK, CONV_DIM), lambda i: (rev(i), 0)),
                   pl.BlockSpec((CHUNK, DT_PAD), lambda i: (rev(i), 0)), vec(D_INNER), vec(DT_PAD), vec(DT_PAD), vec(DT_PAD)],
        out_shape=[jax.ShapeDtypeStruct(dproj.shape, BF16), jax.ShapeDtypeStruct((t, CONV_DIM), F32),
                   jax.ShapeDtypeStruct((t, DT_PAD), BF16), jax.ShapeDtypeStruct((1, D_INNER), F32),
                   jax.ShapeDtypeStruct((1, DT_PAD), F32), jax.ShapeDtypeStruct((1, DT_PAD), F32),
                   jax.ShapeDtypeStruct((1, DT_PAD), F32)],
        scratch_shapes=[pltpu.VMEM((D_STATE, D_INNER), F32), pltpu.VMEM((2 * CHUNK, D_INNER), F32)],
        input_output_aliases={11: 0},
        compiler_params=_params(("arbitrary",), blk),
    )(dyb, y, xc, proj, dt_raw, sprev, dtb, alog, dskip_full, ng, e_bf, dproj)


_HBM = pl.BlockSpec(memory_space=pl.ANY)


def _mesh_pos():
    return lax.axis_index("x"), lax.axis_index("y"), lax.axis_index("c")


def _other_chips(x, y):
    return [(1 - x, y), (x, 1 - y), (1 - x, 1 - y)]


def _all_gather(shards, name):
    n = len(shards)

    def body(*refs):
        ins, outs = refs[:n], refs[n:2 * n]
        send_sems, recv_sems, local_sems = refs[2 * n:]
        x, y, c = _mesh_pos()
        me, sibling = (x, y, c), (x, y, 1 - c)
        chips = _other_chips(x, y)

        def slot(p):
            return 4 * p[0] + 2 * p[1] + p[2]

        def copy(a, k, block, to, src=None):
            dst = outs[a].at[slot(block)]
            return pltpu.make_async_remote_copy(
                src_ref=dst if src is None else src, dst_ref=dst, send_sem=send_sems.at[a * 7 + k],
                recv_sem=recv_sems.at[a * 7 + k], device_id=to, device_id_type=MESH)

        started = []
        own = []
        for a in range(n):
            mine = pltpu.make_async_copy(ins[a], outs[a].at[slot(me)], local_sems.at[a])
            mine.start()
            own.append(mine)
            first = [copy(a, 0, me, sibling, src=ins[a])]
            first += [copy(a, 1 + j, me, (*chip, c), src=ins[a]) for j, chip in enumerate(chips)]
            for cp in first:
                cp.start()
            started += first
        for a in range(n):
            for j, chip in enumerate(chips):
                copy(a, 1 + j, (*chip, c), me).wait_recv()
                fwd = copy(a, 4 + j, (*chip, c), sibling)
                fwd.start()
                started.append(fwd)
        for a in range(n):
            copy(a, 0, sibling, me).wait_recv()
            for j, chip in enumerate(chips):
                copy(a, 4 + j, (*chip, 1 - c), me).wait_recv()
        for cp in started:
            cp.wait_send()
        for mine in own:
            mine.wait()

    return pl.pallas_call(
        body, name=name,
        in_specs=[_HBM] * n, out_specs=[_HBM] * n,
        out_shape=[jax.ShapeDtypeStruct((N_DEV,) + s.shape, s.dtype) for s in shards],
        scratch_shapes=[pltpu.SemaphoreType.DMA((7 * n,)), pltpu.SemaphoreType.DMA((7 * n,)),
                        pltpu.SemaphoreType.DMA((n,))],
    )(*shards)


def _exchange_all(packed, name):
    def body(in_ref, out_ref, send_sems, recv_sems, local_sem):
        x, y, c = _mesh_pos()
        my_slot = 4 * x + 2 * y + c
        mine = pltpu.make_async_copy(in_ref, out_ref.at[my_slot], local_sem)
        mine.start()
        copies = []
        for k in range(1, N_DEV):
            fx, fy, fc = (k >> 2) & 1, (k >> 1) & 1, k & 1
            peer = (x + fx - 2 * x * fx, y + fy - 2 * y * fy, c + fc - 2 * c * fc)
            peer_slot = 4 * peer[0] + 2 * peer[1] + peer[2]
            send = pltpu.make_async_remote_copy(
                src_ref=in_ref, dst_ref=out_ref.at[my_slot], send_sem=send_sems.at[k - 1], recv_sem=recv_sems.at[k - 1],
                device_id=peer, device_id_type=MESH)
            send.start()
            recv = pltpu.make_async_remote_copy(
                src_ref=in_ref, dst_ref=out_ref.at[peer_slot], send_sem=send_sems.at[k - 1], recv_sem=recv_sems.at[k - 1],
                device_id=peer, device_id_type=MESH)
            copies.append((send, recv))
        for send, recv in copies:
            send.wait_send()
            recv.wait_recv()
        mine.wait()

    return pl.pallas_call(
        body, name=name, in_specs=[_HBM], out_specs=_HBM,
        out_shape=jax.ShapeDtypeStruct((N_DEV,) + packed.shape, packed.dtype),
        scratch_shapes=[pltpu.SemaphoreType.DMA((N_DEV - 1,)), pltpu.SemaphoreType.DMA((N_DEV - 1,)),
                        pltpu.SemaphoreType.DMA],
    )(packed)


def _swap_with_sibling(grads, name):
    n = len(grads)

    def body(*refs):
        ins, outs = refs[:n], refs[n:2 * n]
        send_sems, recv_sems = refs[2 * n:]
        x, y, c = _mesh_pos()
        copies = []
        for a in range(n):
            for k in range(N_CHIP):
                cp = pltpu.make_async_remote_copy(
                    src_ref=ins[a].at[(1 - c) + 2 * k], dst_ref=outs[a].at[k], send_sem=send_sems.at[a * N_CHIP + k],
                    recv_sem=recv_sems.at[a * N_CHIP + k], device_id=(x, y, 1 - c), device_id_type=MESH)
                cp.start()
                copies.append(cp)
        for cp in copies:
            cp.wait()

    return pl.pallas_call(
        body, name=name, in_specs=[_HBM] * n, out_specs=[_HBM] * n,
        out_shape=[jax.ShapeDtypeStruct((N_CHIP,) + g.shape[1:], g.dtype) for g in grads],
        scratch_shapes=[pltpu.SemaphoreType.DMA((N_CHIP * n,)), pltpu.SemaphoreType.DMA((N_CHIP * n,))],
    )(*grads)


def _scatter_to_chips(parts, name):
    n = len(parts)

    def body(*refs):
        ins, outs = refs[:n], refs[n:2 * n]
        send_sems, recv_sems, local_sems = refs[2 * n:]
        x, y, c = _mesh_pos()
        my_chip = 2 * x + y
        own, copies = [], []
        for a in range(n):
            mine = pltpu.make_async_copy(ins[a].at[my_chip], outs[a].at[my_chip], local_sems.at[a])
            mine.start()
            own.append(mine)
            for j, chip in enumerate(_other_chips(x, y)):
                send = pltpu.make_async_remote_copy(
                    src_ref=ins[a].at[2 * chip[0] + chip[1]], dst_ref=outs[a].at[my_chip], send_sem=send_sems.at[a * 3 + j],
                    recv_sem=recv_sems.at[a * 3 + j], device_id=(*chip, c), device_id_type=MESH)
                send.start()
                recv = pltpu.make_async_remote_copy(
                    src_ref=ins[a].at[my_chip], dst_ref=outs[a].at[2 * chip[0] + chip[1]], send_sem=send_sems.at[a * 3 + j],
                    recv_sem=recv_sems.at[a * 3 + j], device_id=(*chip, c), device_id_type=MESH)
                copies.append((send, recv))
        for send, recv in copies:
            send.wait_send()
            recv.wait_recv()
        for mine in own:
            mine.wait()

    return pl.pallas_call(
        body, name=name, in_specs=[_HBM] * n, out_specs=[_HBM] * n,
        out_shape=[jax.ShapeDtypeStruct(p.shape, p.dtype) for p in parts],
        scratch_shapes=[pltpu.SemaphoreType.DMA((3 * n,)), pltpu.SemaphoreType.DMA((3 * n,)),
                        pltpu.SemaphoreType.DMA((n,))],
    )(*parts)


def _row_block(rows, cols, slots):
    budget = 2 * 1024 * 1024
    if rows % SUBLANES:
        return rows
    br = rows
    while br % 2 == 0 and (br // 2) % SUBLANES == 0 and slots * br * cols * 4 > budget:
        br //= 2
    return br


def _add_sibling(grads, recv, c_idx, name):
    _, rows, cols = grads.shape
    br = _row_block(rows, cols, 3)

    def body(c_ref, g_ref, r_ref, out_ref):
        del c_ref
        out_ref[...] = (g_ref[...].astype(F32) + r_ref[...].astype(F32)).astype(out_ref.dtype)

    grid_spec = pltpu.PrefetchScalarGridSpec(
        num_scalar_prefetch=1, grid=(N_CHIP, rows // br),
        in_specs=[pl.BlockSpec((1, br, cols), lambda k, i, c_ref: (c_ref[0] + 2 * k, i, 0)),
                  pl.BlockSpec((1, br, cols), lambda k, i, c_ref: (k, i, 0))],
        out_specs=pl.BlockSpec((1, br, cols), lambda k, i, c_ref: (k, i, 0)))
    return pl.pallas_call(
        body, name=name, grid_spec=grid_spec, out_shape=jax.ShapeDtypeStruct((N_CHIP, rows, cols), grads.dtype),
        compiler_params=_params(("parallel", "parallel"), 3 * _nbytes((br, cols), F32)),
    )(c_idx, grads, recv)


def _adamw(slots, w, m, v, name):
    ns, rows, cols = slots.shape
    br = _row_block(rows, cols, ns + 7)
    c1 = 1.0 / (1.0 - ADAM_B1 ** ADAM_STEP)
    c2 = 1.0 / (1.0 - ADAM_B2 ** ADAM_STEP)

    def body(s_ref, w_ref, m_ref, v_ref, g_ref, d_ref, m2_ref, v2_ref):
        g = s_ref[0].astype(F32)
        for k in range(1, ns):
            g = g + s_ref[k].astype(F32)
        m2 = ADAM_B1 * m_ref[...] + (1.0 - ADAM_B1) * g
        v2 = ADAM_B2 * v_ref[...] + (1.0 - ADAM_B2) * (g * g)
        g_ref[...] = g
        m2_ref[...] = m2
        v2_ref[...] = v2
        d_ref[...] = -ADAM_LR * ((m2 * c1) / (jnp.sqrt(v2 * c2) + ADAM_EPS) + ADAM_WD * w_ref[...])

    blk = pl.BlockSpec((br, cols), lambda i: (i, 0))
    return pl.pallas_call(
        body, name=name, grid=(rows // br,),
        in_specs=[pl.BlockSpec((ns, br, cols), lambda i: (0, i, 0)), blk, blk, blk],
        out_specs=[blk, blk, blk, blk],
        out_shape=[jax.ShapeDtypeStruct((rows, cols), F32)] * 4,
        compiler_params=_params(("parallel",), (ns + 7) * _nbytes((br, cols), F32)),
    )(slots, w, m, v)


_SMALL = ["norm_mix_g", "conv_b", "dt_bias", "a_log", "d_skip", "ssm_norm_g", "v_norm_g", "v_norm_b", "w_spatial",
          "b_spatial", "b_gates", "norm_mlp_g", "norm_final_g"]
_SHARDED = ["w_in", "w_proj_a", "w_proj_b", "w_out", "w_mlp_up", "w_mlp_down"]


def _pack(arrays):
    flat = []
    for arr in arrays:
        f = arr.reshape(-1).astype(F32)
        pad = (-f.shape[0]) % LANES
        flat.append(jnp.pad(f, (0, pad)) if pad else f)
    out = jnp.concatenate(flat)
    pad = (-out.shape[0]) % (SUBLANES * LANES)
    if pad:
        out = jnp.pad(out, (0, pad))
    return out.reshape(-1, LANES)


def _unpack(packed, shapes):
    flat = packed.reshape(-1)
    out, off = [], 0
    for shape in shapes:
        size = math.prod(shape)
        out.append(flat[off:off + size].reshape(shape))
        off += size + ((-size) % LANES)
    return out


def _mm_tiles(mode, m, n, k):
    tn = min(n, 1024)
    if mode == "tn":
        return min(m, 1024), tn, min(k, 2048)
    if k <= 2048:
        return min(m, 1024), tn, k
    if k <= 4096:
        return min(m, 512), tn, k
    return min(m, 1024), tn, 2048


def _local_step(x, target, wts, small):
    t = x.shape[0]
    w_main, w_dt = wts["w_main"], wts["w_dt"]
    bsp_t = small["b_spatial"].T
    pad32 = lambda a: jnp.pad(a, ((0, 0), (0, DT_PAD - N_HEADS)))
    dtb, alog = pad32(small["dt_bias"]), pad32(small["a_log"])
    dskip_full = jnp.repeat(small["d_skip"], HEAD_DIM, axis=1)
    head_of_col = lax.broadcasted_iota(jnp.int32, (DT_PAD, D_INNER), 1) // HEAD_DIM
    e_bf = (head_of_col == lax.broadcasted_iota(jnp.int32, (DT_PAD, D_INNER), 0)).astype(BF16)

    def mm(a, b, mode, name, **kw):
        if mode == "nn":
            m, k, n = a.shape[0], a.shape[1], b.shape[1]
        elif mode == "nt":
            m, k, n = a.shape[0], a.shape[1], b.shape[0]
        else:
            m, k, n = a.shape[1], a.shape[0], b.shape[1]
        tm, tn, tk = _mm_tiles(mode, m, n, k)
        kw.setdefault("out_dtypes", (BF16,) if mode == "tn" else (F32,))
        if "extra_specs" in kw:
            kw["extra_specs"] = kw["extra_specs"](tm, tn)
        return _matmul(a, b, mode=mode, tm=tm, tn=tn, tk=tk, name=name, **kw)

    def out_tile(tm, tn):
        return (((tm, tn), lambda i, j: (i, j)),)

    h = _rms_fwd(x, small["norm_mix_g"], "rms_mix")
    proj = mm(h, w_main, "nn", "proj_main", j_outer=True)
    dt_raw = mm(h, w_dt, "nn", "proj_dt")
    y_a = _gmlp_fwd(proj, small["v_norm_g"], small["v_norm_b"], small["w_spatial"], bsp_t, "gmlp_fwd")
    pre_conv, xc = _conv_fwd(proj, wts["conv_w"], small["conv_b"], "conv_fwd")
    y_ssd, y_b, sprev = _ssd_fwd(xc, proj, dt_raw, dtb, alog, dskip_full, small["ssm_norm_g"], e_bf, "ssd_fwd")
    pa = mm(y_a, wts["w_proj_a"], "nn", "proj_a")
    pb = mm(y_b, wts["w_proj_b"], "nn", "proj_b")
    merged = _merge_fwd(pa, pb, proj, small["b_gates"], "merge_fwd")

    def add_residual(acc, ex, outs):
        outs[0][...] = acc + ex[0][...]

    x1 = mm(merged, wts["w_out"], "nn", "out_proj", epilogue=add_residual, extras=(x,), extra_specs=out_tile)
    h2 = _rms_fwd(x1, small["norm_mlp_g"], "rms_mlp")

    def relu_sq(acc, ex, outs):
        outs[0][...] = acc
        r = jnp.maximum(acc, 0.0)
        outs[1][...] = (r * r).astype(BF16)

    up, act = mm(h2, wts["w_mlp_up"], "nn", "mlp_up", epilogue=relu_sq, out_dtypes=(F32, BF16), j_outer=True)
    x2 = mm(act, wts["w_mlp_down"], "nn", "mlp_down", epilogue=add_residual, extras=(x1,), extra_specs=out_tile)
    _, dx2, dx2_b, g_final, loss = _loss_head(x2, small["norm_final_g"], target, "loss_head")

    def relu_sq_bwd(acc, ex, outs):
        outs[0][...] = (acc * 2.0 * jnp.maximum(ex[0][...], 0.0)).astype(BF16)

    dup = mm(dx2_b, wts["w_mlp_down"], "nt", "d_act", epilogue=relu_sq_bwd, extras=(up,), extra_specs=out_tile,
             out_dtypes=(BF16,), j_outer=True)
    g_down = mm(act, dx2_b, "tn", "g_mlp_down")
    g_up = mm(h2, dup, "tn", "g_mlp_up")
    dh2 = mm(dup, wts["w_mlp_up"], "nt", "d_h2")
    dx1, dx1_b, g_mlp = _rms_bwd(x1, small["norm_mlp_g"], dh2, dx2, "rms_mlp_bwd")

    g_out = mm(merged, dx1_b, "tn", "g_out")
    dmerged = mm(dx1_b, wts["w_out"], "nt", "d_merged")
    dpa, dpb, dproj, g_bgates = _merge_bwd(dmerged, pa, pb, proj, small["b_gates"], "merge_bwd")
    g_pa = mm(y_a, dpa, "tn", "g_proj_a")
    g_pb = mm(y_b, dpb, "tn", "g_proj_b")
    dya = mm(dpa, wts["w_proj_a"], "nt", "d_ya")
    dyb = mm(dpb, wts["w_proj_b"], "nt", "d_yb")

    dproj, g_wsp, g_bsp_t, g_vg, g_vb = _gmlp_bwd(proj, dya, small["v_norm_g"], small["v_norm_b"], small["w_spatial"],
                                                   bsp_t, dproj, "gmlp_bwd")
    dproj, dxc, ddt, g_ng, g_dskip, g_alog, g_dtb = _ssd_bwd(dyb, y_ssd, xc, proj, dt_raw, sprev, dtb, alog, dskip_full,
                                                             small["ssm_norm_g"], e_bf, dproj, "ssd_bwd")
    dproj, g_convw, g_convb = _conv_bwd(proj, pre_conv, dxc, wts["conv_w"], dproj, "conv_bwd")

    g_main = mm(h, dproj, "tn", "g_in_main")
    g_dt = mm(h, ddt, "tn", "g_in_dt")

    def add_dt(acc, ex, outs):
        outs[0][...] = acc + _dot(ex[0][...], ex[1][...], _NT)

    dh = mm(dproj, w_main, "nt", "d_h", epilogue=add_dt, extras=(ddt, w_dt),
            extra_specs=lambda tm, tn: (((tm, DT_PAD), lambda i, j: (i, 0)), ((D_MODEL, DT_PAD), lambda i, j: (0, 0))))
    grad_x, _, g_mix = _rms_bwd(x, small["norm_mix_g"], dh, dx1, "rms_mix_bwd")

    grads = {
        "w_main": g_main, "w_dt": g_dt, "w_proj_a": g_pa, "w_proj_b": g_pb, "w_out": g_out, "w_mlp_up": g_up,
        "w_mlp_down": g_down, "conv_w": g_convw,
        "norm_mix_g": g_mix, "conv_b": g_convb, "dt_bias": g_dtb[:, :N_HEADS], "a_log": g_alog[:, :N_HEADS],
        "d_skip": g_dskip[:, :N_HEADS], "ssm_norm_g": g_ng, "v_norm_g": g_vg, "v_norm_b": g_vb, "w_spatial": g_wsp,
        "b_spatial": g_bsp_t.T, "b_gates": g_bgates, "norm_mlp_g": g_mlp, "norm_final_g": g_final,
    }
    return loss, grad_x, grads


def _split_w_in(w_full):
    dt0 = COL_GATE
    w_main = jnp.concatenate([w_full[:, :dt0], w_full[:, dt0 + N_HEADS:]], axis=1)
    w_dt = jnp.pad(w_full[:, dt0:dt0 + N_HEADS], ((0, 0), (0, DT_PAD - N_HEADS)))
    return w_main, w_dt


def _join_w_in(g_main, g_dt):
    dt0 = COL_GATE
    return jnp.concatenate([g_main[:, :dt0], g_dt[:, :N_HEADS], g_main[:, dt0:]], axis=1)


def kernel(x, norm_mix_g, w_in, conv_w, conv_b, dt_bias, a_log, d_skip, ssm_norm_g, v_norm_g, v_norm_b, w_spatial, b_spatial, b_gates, w_proj_a, w_proj_b, w_out, norm_mlp_g, w_mlp_up, w_mlp_down, norm_final_g, loss_target, m_norm_mix_g, m_w_in, m_conv_w, m_conv_b, m_dt_bias, m_a_log, m_d_skip, m_ssm_norm_g, m_v_norm_g, m_v_norm_b, m_w_spatial, m_b_spatial, m_b_gates, m_w_proj_a, m_w_proj_b, m_w_out, m_norm_mlp_g, m_w_mlp_up, m_w_mlp_down, m_norm_final_g, v_norm_mix_g, v_w_in, v_conv_w, v_conv_b, v_dt_bias, v_a_log, v_d_skip, v_ssm_norm_g, v_v_norm_g, v_v_norm_b, v_w_spatial, v_b_spatial, v_b_gates, v_w_proj_a, v_w_proj_b, v_w_out, v_norm_mlp_g, v_w_mlp_up, v_w_mlp_down, v_norm_final_g):
    given = dict(locals())
    names = ["norm_mix_g", "w_in", "conv_w", "conv_b", "dt_bias", "a_log", "d_skip", "ssm_norm_g", "v_norm_g", "v_norm_b",
             "w_spatial", "b_spatial", "b_gates", "w_proj_a", "w_proj_b", "w_out", "norm_mlp_g", "w_mlp_up", "w_mlp_down",
             "norm_final_g"]
    shapes = {n: given[n].shape for n in names}
    t = x.shape[1]
    dev = 4 * lax.axis_index("x") + 2 * lax.axis_index("y") + lax.axis_index("c")

    shard2d = {"w_in": w_in[0], "w_proj_a": w_proj_a[0], "w_proj_b": w_proj_b[0], "w_out": w_out[0],
               "w_mlp_up": w_mlp_up[0], "w_mlp_down": w_mlp_down[0]}
    conv_shard = conv_w.reshape(CONV_WIDTH, -1)
    gathered = _all_gather([shard2d[n].astype(BF16) for n in _SHARDED] + [conv_shard], "gather_weights")
    gw = dict(zip(_SHARDED, gathered[:-1]))
    by_cols = lambda g: jnp.transpose(g, (1, 0, 2)).reshape(g.shape[1], -1)
    w_main, w_dt = _split_w_in(by_cols(gw["w_in"]))
    wts = {"w_main": w_main, "w_dt": w_dt, "w_proj_a": gw["w_proj_a"].reshape(-1, D_MODEL),
           "w_proj_b": gw["w_proj_b"].reshape(-1, D_MODEL), "w_out": gw["w_out"].reshape(-1, D_MODEL),
           "w_mlp_up": by_cols(gw["w_mlp_up"]), "w_mlp_down": gw["w_mlp_down"].reshape(-1, D_MODEL),
           "conv_w": by_cols(gathered[-1])}
    small = {"norm_mix_g": norm_mix_g, "conv_b": conv_b, "dt_bias": dt_bias, "a_log": a_log, "d_skip": d_skip,
             "ssm_norm_g": ssm_norm_g, "v_norm_g": v_norm_g, "v_norm_b": v_norm_b, "w_spatial": w_spatial[0],
             "b_spatial": b_spatial[0], "b_gates": b_gates, "norm_mlp_g": norm_mlp_g,
             "norm_final_g": norm_final_g.reshape(1, -1)}

    loss_part, grad_x, grads = _local_step(x[0], loss_target[0], wts, small)

    by_dev_cols = lambda g: jnp.transpose(g.reshape(g.shape[0], N_DEV, -1), (1, 0, 2))
    by_dev_rows = lambda g: g.reshape(N_DEV, -1, g.shape[1])
    full = {"w_in": by_dev_cols(_join_w_in(grads["w_main"], grads["w_dt"])), "w_proj_a": by_dev_rows(grads["w_proj_a"]),
            "w_proj_b": by_dev_rows(grads["w_proj_b"]), "w_out": by_dev_rows(grads["w_out"]),
            "w_mlp_up": by_dev_cols(grads["w_mlp_up"]), "w_mlp_down": by_dev_rows(grads["w_mlp_down"])}
    from_sibling = _swap_with_sibling([full[n] for n in _SHARDED], "reduce_cores")
    c_idx = lax.axis_index("c").astype(jnp.int32).reshape(1)
    chip_part = [_add_sibling(full[n], r, c_idx, "add_cores_" + n) for n, r in zip(_SHARDED, from_sibling)]
    by_chip = _scatter_to_chips(chip_part, "reduce_chips")
    out = {}
    for n, slots in zip(_SHARDED, by_chip):
        res = _adamw(slots, shard2d[n], given["m_" + n][0], given["v_" + n][0], "adamw_" + n)
        out[n] = [r.reshape(shapes[n]) for r in res]

    small_shapes = [shapes[n] for n in _SMALL]
    extra = [grads["conv_w"], loss_part]
    packed_g = _pack([grads[n] for n in _SMALL] + extra)
    zeros_extra = [jnp.zeros_like(e) for e in extra]
    all_g = _exchange_all(packed_g, "exchange_small")
    res = _adamw(all_g, _pack([given[n] for n in _SMALL] + zeros_extra), _pack([given["m_" + n] for n in _SMALL] + zeros_extra),
                 _pack([given["v_" + n] for n in _SMALL] + zeros_extra), "adamw_small")
    extra_shapes = [grads["conv_w"].shape, loss_part.shape]
    unpacked = [_unpack(r, small_shapes + extra_shapes) for r in res]
    for i, n in enumerate(_SMALL):
        out[n] = [u[i] for u in unpacked]
    g_conv_full, loss_all = unpacked[0][-2], unpacked[0][-1]
    width = shapes["conv_w"][-1]
    g_conv = lax.dynamic_slice(g_conv_full, (0, dev * width), (CONV_WIDTH, width))
    res = _adamw(g_conv[None], conv_shard, m_conv_w.reshape(CONV_WIDTH, -1), v_conv_w.reshape(CONV_WIDTH, -1), "adamw_conv_w")
    out["conv_w"] = [r.reshape(shapes["conv_w"]) for r in res]

    loss = loss_all[0, 0]
    return (loss, grad_x[None], *[out[n][0] for n in names], *[out[n][1] for n in names],
            *[out[n][2] for n in names], *[out[n][3] for n in names])
```

```python
import functools
import math

import jax
import jax.numpy as jnp
from jax import lax
from jax.experimental import pallas as pl
from jax.experimental.pallas import tpu as pltpu

F32 = jnp.float32
BF16 = jnp.bfloat16
MESH = pl.DeviceIdType.MESH

D_MODEL = 1024
NORM_EPS = 1e-6
CHUNK = 128
GROUPS = 8
D_INNER = 2048
HEAD_DIM = 64
N_HEADS = 32
D_STATE = 128
CONV_WIDTH = 4
CONV_DIM = 4096
D_FF = 4096
GROUP_W = D_INNER // GROUPS
N_DEV = 8
N_CHIP = 4

ADAM_LR = 0.001
ADAM_B1 = 0.9
ADAM_B2 = 0.999
ADAM_EPS = 1e-08
ADAM_WD = 0.01
ADAM_STEP = 10

MAIN_W = 2 * D_MODEL + D_INNER + CONV_DIM + 2 * D_MODEL
COL_Z = 2048
COL_XBC = 4096
COL_GATE = 8192
DT_PAD = 128

LANES = 128
SUBLANES = 8
VMEM_BYTES_V7X = 64 * 1024 * 1024
VMEM_BODY_TEMP = 24 * 1024 * 1024


def _vmem_limit(block_bytes):
    return int(min(2 * block_bytes + VMEM_BODY_TEMP, VMEM_BYTES_V7X - 8 * 1024 * 1024))


def _nbytes(shape, dtype):
    return math.prod(shape) * jnp.dtype(dtype).itemsize


def _params(sem, block_bytes):
    return pltpu.CompilerParams(dimension_semantics=sem, vmem_limit_bytes=_vmem_limit(block_bytes))


def _sigmoid(x):
    return 1.0 / (1.0 + jnp.exp(-x))


def _softplus(x):
    e = jnp.exp(-jnp.abs(x))
    u = 1.0 + e
    log1p_e = jnp.where(u == 1.0, e, jnp.log(u) * (e / jnp.where(u == 1.0, 1.0, u - 1.0)))
    return jnp.maximum(x, 0.0) + log1p_e


_SQRT_HALF = 0.7071067811865476
_INV_SQRT_2PI = 0.3989422804014327


def _gelu(x):
    return x * (lax.erf(x * _SQRT_HALF) + 1.0) * 0.5


def _gelu_grad(x):
    return 0.5 * (1.0 + lax.erf(x * _SQRT_HALF)) + x * jnp.exp(-0.5 * x * x) * _INV_SQRT_2PI


def _dot(a, b, dims):
    return lax.dot_general(a, b, (dims, ((), ())), preferred_element_type=F32)


_NN = ((1,), (0,))
_NT = ((1,), (1,))
_TN = ((0,), (0,))


def _split3(x):
    hi = x.astype(BF16)
    r1 = x - hi.astype(F32)
    mid = r1.astype(BF16)
    lo = (r1 - mid.astype(F32)).astype(BF16)
    return hi, mid, lo


def _dot_exact_rhs(x, e, dims):
    hi, mid, lo = _split3(x)
    return _dot(hi, e, dims) + _dot(mid, e, dims) + _dot(lo, e, dims)


def _dot_exact_lhs(e, x, dims):
    hi, mid, lo = _split3(x)
    return _dot(e, hi, dims) + _dot(e, mid, dims) + _dot(e, lo, dims)


def _tri(lower):
    r = lax.broadcasted_iota(jnp.int32, (CHUNK, CHUNK), 0)
    c = lax.broadcasted_iota(jnp.int32, (CHUNK, CHUNK), 1)
    return (r >= c) if lower else (r <= c)


def _matmul(a, b, *, mode, tm, tn, tk, out_dtypes, name, epilogue=None, extras=(), extra_specs=(), j_outer=False):
    if mode == "nn":
        (m, k), (_, n) = a.shape, b.shape
    elif mode == "nt":
        (m, k), (n, _) = a.shape, b.shape
    else:
        (k, m), (_, n) = a.shape, b.shape
    assert m % tm == 0 and n % tn == 0 and k % tk == 0, (name, m, n, k, tm, tn, tk)
    nk = k // tk
    n_extra, n_out = len(extras), len(out_dtypes)
    dims = {"nn": _NN, "nt": _NT, "tn": _TN}[mode]
    if epilogue is None:
        def epilogue(acc, ex, outs):
            outs[0][...] = acc.astype(outs[0].dtype)

    def body(*refs):
        a_ref, b_ref = refs[0], refs[1]
        ex_refs = refs[2:2 + n_extra]
        outs = refs[2 + n_extra:2 + n_extra + n_out]
        p = _dot(a_ref[...], b_ref[...], dims)
        if nk == 1:
            epilogue(p, ex_refs, outs)
        else:
            acc_ref = refs[2 + n_extra + n_out]
            kk = pl.program_id(2)

            @pl.when(kk == 0)
            def _():
                acc_ref[...] = p

            @pl.when(kk > 0)
            def _():
                acc_ref[...] += p

            @pl.when(kk == nk - 1)
            def _():
                epilogue(acc_ref[...], ex_refs, outs)

    if j_outer:
        grid = (n // tn, m // tm, nk)
        ij = lambda g0, g1: (g1, g0)
    else:
        grid = (m // tm, n // tn, nk)
        ij = lambda g0, g1: (g0, g1)

    def wrap(fn):
        return lambda g0, g1, kk: fn(*ij(g0, g1), kk)

    if mode == "nn":
        a_spec = pl.BlockSpec((tm, tk), wrap(lambda i, j, kk: (i, kk)))
        b_spec = pl.BlockSpec((tk, tn), wrap(lambda i, j, kk: (kk, j)))
        a_blk, b_blk = (tm, tk), (tk, tn)
    elif mode == "nt":
        a_spec = pl.BlockSpec((tm, tk), wrap(lambda i, j, kk: (i, kk)))
        b_spec = pl.BlockSpec((tn, tk), wrap(lambda i, j, kk: (j, kk)))
        a_blk, b_blk = (tm, tk), (tn, tk)
    else:
        a_spec = pl.BlockSpec((tk, tm), wrap(lambda i, j, kk: (kk, i)))
        b_spec = pl.BlockSpec((tk, tn), wrap(lambda i, j, kk: (kk, j)))
        a_blk, b_blk = (tk, tm), (tk, tn)
    ex_specs = [pl.BlockSpec(shape, wrap(lambda i, j, kk, f=f: f(i, j))) for shape, f in extra_specs]
    out_spec = [pl.BlockSpec((tm, tn), wrap(lambda i, j, kk: (i, j))) for _ in out_dtypes]
    out_shape = [jax.ShapeDtypeStruct((m, n), dt) for dt in out_dtypes]
    blk = (_nbytes(a_blk, a.dtype) + _nbytes(b_blk, b.dtype) + sum(_nbytes(s, F32) for s, _ in extra_specs)
           + sum(_nbytes((tm, tn), dt) for dt in out_dtypes) + _nbytes((tm, tn), F32))
    res = pl.pallas_call(
        body, name=name, grid=grid,
        in_specs=[a_spec, b_spec] + ex_specs, out_specs=out_spec, out_shape=out_shape,
        scratch_shapes=[pltpu.VMEM((tm, tn), F32)] if nk > 1 else [],
        compiler_params=_params(("parallel", "parallel", "arbitrary"), blk),
    )(a, b, *extras)
    return res[0] if n_out == 1 else res


ROW_TILE = 256


def _row_spec(width, col_block=0, tile=ROW_TILE):
    return pl.BlockSpec((tile, width), lambda i, cb=col_block: (i, cb))


def _vec_spec(width, col_block=0):
    return pl.BlockSpec((1, width), lambda i, cb=col_block: (0, cb))


def _rms_fwd(x, g, name):
    t = x.shape[0]

    def body(x_ref, g_ref, h_ref):
        xv = x_ref[...]
        r = lax.rsqrt(jnp.mean(xv * xv, axis=-1, keepdims=True) + NORM_EPS)
        h_ref[...] = (xv * r * g_ref[...]).astype(BF16)

    return pl.pallas_call(
        body, name=name, grid=(t // ROW_TILE,),
        in_specs=[_row_spec(D_MODEL), _vec_spec(D_MODEL)], out_specs=_row_spec(D_MODEL),
        out_shape=jax.ShapeDtypeStruct((t, D_MODEL), BF16),
        compiler_params=_params(("parallel",), 3 * _nbytes((ROW_TILE, D_MODEL), F32)),
    )(x, g)


def _rms_bwd(x, g, dh, dres, name):
    t = x.shape[0]

    def body(x_ref, g_ref, dh_ref, dres_ref, dx_ref, dxb_ref, gg_ref):
        xv = x_ref[...]
        r = lax.rsqrt(jnp.mean(xv * xv, axis=-1, keepdims=True) + NORM_EPS)
        xh = xv * r
        dhv = dh_ref[...]
        dyg = dhv * g_ref[...]
        dx = r * (dyg - xh * jnp.mean(dyg * xh, axis=-1, keepdims=True)) + dres_ref[...]
        dx_ref[...] = dx
        dxb_ref[...] = dx.astype(BF16)

        @pl.when(pl.program_id(0) == 0)
        def _():
            gg_ref[...] = jnp.zeros_like(gg_ref)

        gg_ref[...] += jnp.sum(dhv * xh, axis=0, keepdims=True)

    return pl.pallas_call(
        body, name=name, grid=(t // ROW_TILE,),
        in_specs=[_row_spec(D_MODEL), _vec_spec(D_MODEL), _row_spec(D_MODEL), _row_spec(D_MODEL)],
        out_specs=[_row_spec(D_MODEL), _row_spec(D_MODEL), _vec_spec(D_MODEL)],
        out_shape=[jax.ShapeDtypeStruct((t, D_MODEL), F32), jax.ShapeDtypeStruct((t, D_MODEL), BF16),
                   jax.ShapeDtypeStruct((1, D_MODEL), F32)],
        compiler_params=_params(("arbitrary",), 5 * _nbytes((ROW_TILE, D_MODEL), F32)),
    )(x, g, dh, dres)


def _loss_head(x2, gf, target, name):
    t = x2.shape[0]

    def body(x_ref, g_ref, t_ref, loss_ref, dx_ref, dxb_ref, gg_ref, tot_ref):
        xv = x_ref[...]
        gv = g_ref[...]
        r = lax.rsqrt(jnp.mean(xv * xv, axis=-1, keepdims=True) + NORM_EPS)
        xh = xv * r
        err = xh * gv - t_ref[...]
        dy = err * (1.0 / D_MODEL)
        dyg = dy * gv
        dx = r * (dyg - xh * jnp.mean(dyg * xh, axis=-1, keepdims=True))
        dx_ref[...] = dx
        dxb_ref[...] = dx.astype(BF16)

        @pl.when(pl.program_id(0) == 0)
        def _():
            gg_ref[...] = jnp.zeros_like(gg_ref)
            loss_ref[...] = jnp.zeros_like(loss_ref)

        gg_ref[...] += jnp.sum(dy * xh, axis=0, keepdims=True)
        loss_ref[...] += jnp.sum(err * err, axis=0, keepdims=True)
        tot_ref[...] = jnp.broadcast_to(jnp.sum(loss_ref[...], axis=1, keepdims=True) * (0.5 / D_MODEL), tot_ref.shape)

    return pl.pallas_call(
        body, name=name, grid=(t // ROW_TILE,),
        in_specs=[_row_spec(D_MODEL), _vec_spec(D_MODEL), _row_spec(D_MODEL)],
        out_specs=[_vec_spec(D_MODEL), _row_spec(D_MODEL), _row_spec(D_MODEL), _vec_spec(D_MODEL), _vec_spec(LANES)],
        out_shape=[jax.ShapeDtypeStruct((1, D_MODEL), F32), jax.ShapeDtypeStruct((t, D_MODEL), F32),
                   jax.ShapeDtypeStruct((t, D_MODEL), BF16), jax.ShapeDtypeStruct((1, D_MODEL), F32),
                   jax.ShapeDtypeStruct((1, LANES), F32)],
        compiler_params=_params(("arbitrary",), 5 * _nbytes((ROW_TILE, D_MODEL), F32)),
    )(x2, gf, target)


def _merge_fwd(pa, pb, proj, b_gates, name):
    t = pa.shape[0]
    gcb = COL_GATE // D_MODEL

    def body(pa_ref, pb_ref, la_ref, lb_ref, ba_ref, bb_ref, out_ref):
        ga = _sigmoid(la_ref[...] + ba_ref[...])
        gb = _sigmoid(lb_ref[...] + bb_ref[...])
        out_ref[...] = (ga * pa_ref[...] + gb * pb_ref[...]).astype(BF16)

    return pl.pallas_call(
        body, name=name, grid=(t // ROW_TILE,),
        in_specs=[_row_spec(D_MODEL), _row_spec(D_MODEL), _row_spec(D_MODEL, gcb), _row_spec(D_MODEL, gcb + 1),
                  _vec_spec(D_MODEL, 0), _vec_spec(D_MODEL, 1)],
        out_specs=_row_spec(D_MODEL),
        out_shape=jax.ShapeDtypeStruct((t, D_MODEL), BF16),
        compiler_params=_params(("parallel",), 5 * _nbytes((ROW_TILE, D_MODEL), F32)),
    )(pa, pb, proj, proj, b_gates, b_gates)


def _merge_bwd(dmerged, pa, pb, proj, b_gates, name):
    t = pa.shape[0]
    gcb = COL_GATE // D_MODEL

    def body(dm_ref, pa_ref, pb_ref, la_ref, lb_ref, ba_ref, bb_ref, dpa_ref, dpb_ref, dgl_ref, gb_ref):
        dm = dm_ref[...]
        ga = _sigmoid(la_ref[...] + ba_ref[...])
        gb = _sigmoid(lb_ref[...] + bb_ref[...])
        dpa_ref[...] = (dm * ga).astype(BF16)
        dpb_ref[...] = (dm * gb).astype(BF16)
        dla = dm * pa_ref[...] * ga * (1.0 - ga)
        dlb = dm * pb_ref[...] * gb * (1.0 - gb)
        dgl_ref[:, :D_MODEL] = dla.astype(BF16)
        dgl_ref[:, D_MODEL:] = dlb.astype(BF16)

        @pl.when(pl.program_id(0) == 0)
        def _():
            gb_ref[...] = jnp.zeros_like(gb_ref)

        gb_ref[:, :D_MODEL] += jnp.sum(dla, axis=0, keepdims=True)
        gb_ref[:, D_MODEL:] += jnp.sum(dlb, axis=0, keepdims=True)

    return pl.pallas_call(
        body, name=name, grid=(t // ROW_TILE,),
        in_specs=[_row_spec(D_MODEL), _row_spec(D_MODEL), _row_spec(D_MODEL), _row_spec(D_MODEL, gcb),
                  _row_spec(D_MODEL, gcb + 1), _vec_spec(D_MODEL, 0), _vec_spec(D_MODEL, 1)],
        out_specs=[_row_spec(D_MODEL), _row_spec(D_MODEL), _row_spec(2 * D_MODEL, COL_GATE // (2 * D_MODEL)),
                   _vec_spec(2 * D_MODEL)],
        out_shape=[jax.ShapeDtypeStruct((t, D_MODEL), BF16), jax.ShapeDtypeStruct((t, D_MODEL), BF16),
                   jax.ShapeDtypeStruct((t, MAIN_W), BF16), jax.ShapeDtypeStruct((1, 2 * D_MODEL), F32)],
        compiler_params=_params(("arbitrary",), 8 * _nbytes((ROW_TILE, D_MODEL), F32)),
    )(dmerged, pa, pb, proj, proj, b_gates, b_gates)


GMLP_TILE = 512
GMLP_NC = GMLP_TILE // CHUNK


def _gmlp_common(u_pre, v_pre, vg, vb):
    u = _gelu(u_pre)
    v = _gelu(v_pre)
    mu = jnp.mean(v, axis=-1, keepdims=True)
    vc = v - mu
    rstd = lax.rsqrt(jnp.mean(vc * vc, axis=-1, keepdims=True) + NORM_EPS)
    vh = vc * rstd
    vn = vh * vg + vb
    return u, vh, vn, rstd


def _chunks_to_lanes(x, g):
    return jnp.concatenate([x[c * CHUNK:(c + 1) * CHUNK, g * CHUNK:(g + 1) * CHUNK] for c in range(GMLP_NC)], axis=1)


def _gmlp_fwd(proj, vg, vb, wsp, bsp_t, name):
    t = proj.shape[0]

    def body(u_ref, v_ref, vg_ref, vb_ref, w_ref, b_ref, ya_ref):
        u, _, vn, _ = _gmlp_common(u_ref[...], v_ref[...], vg_ref[...], vb_ref[...])
        mask = _tri(True)
        bt = b_ref[...]
        for g in range(GROUPS):
            w = jnp.where(mask, w_ref[g], 0.0).astype(BF16)
            vcat = _chunks_to_lanes(vn, g).astype(BF16)
            s = _dot(w, vcat, _NN) + bt[:, g:g + 1]
            for c in range(GMLP_NC):
                rows, cols = slice(c * CHUNK, (c + 1) * CHUNK), slice(g * CHUNK, (g + 1) * CHUNK)
                ya_ref[rows, cols] = (u[rows, cols] * s[:, c * CHUNK:(c + 1) * CHUNK]).astype(BF16)

    return pl.pallas_call(
        body, name=name, grid=(t // GMLP_TILE,),
        in_specs=[_row_spec(D_MODEL, 0, GMLP_TILE), _row_spec(D_MODEL, 1, GMLP_TILE), _vec_spec(D_MODEL),
                  _vec_spec(D_MODEL), pl.BlockSpec((GROUPS, CHUNK, CHUNK), lambda i: (0, 0, 0)),
                  pl.BlockSpec((CHUNK, GROUPS), lambda i: (0, 0))],
        out_specs=_row_spec(D_MODEL, 0, GMLP_TILE),
        out_shape=jax.ShapeDtypeStruct((t, D_MODEL), BF16),
        compiler_params=_params(("parallel",), 3 * _nbytes((GMLP_TILE, D_MODEL), F32)),
    )(proj, proj, vg, vb, wsp, bsp_t)


def _gmlp_bwd(proj, dya, vg, vb, wsp, bsp_t, dproj, name):
    t = proj.shape[0]

    def body(u_ref, v_ref, dya_ref, vg_ref, vb_ref, w_ref, b_ref, dproj_in, duv_ref, gw_ref, gbt_ref, gvg_ref, gvb_ref,
             dvn_scr, du_scr):
        del dproj_in
        u_pre, v_pre = u_ref[...], v_ref[...]
        vgv = vg_ref[...]
        u, vh, vn, rstd = _gmlp_common(u_pre, v_pre, vgv, vb_ref[...])
        dya = dya_ref[...]
        mask = _tri(True)
        bt = b_ref[...]
        first = pl.program_id(0) == 0

        @pl.when(first)
        def _():
            gw_ref[...] = jnp.zeros_like(gw_ref)
            gbt_ref[...] = jnp.zeros_like(gbt_ref)
            gvg_ref[...] = jnp.zeros_like(gvg_ref)
            gvb_ref[...] = jnp.zeros_like(gvb_ref)

        lane = lax.broadcasted_iota(jnp.int32, (CHUNK, GROUPS), 1)
        gbt = jnp.zeros((CHUNK, GROUPS), F32)
        for g in range(GROUPS):
            w = jnp.where(mask, w_ref[g], 0.0).astype(BF16)
            vcat = _chunks_to_lanes(vn, g).astype(BF16)
            s = _dot(w, vcat, _NN) + bt[:, g:g + 1]
            ds = _chunks_to_lanes(dya * u, g)
            gbt = jnp.where(lane == g, jnp.sum(ds, axis=1, keepdims=True), gbt)
            dsb = ds.astype(BF16)
            gw_ref[g] += jnp.where(mask, _dot(dsb, vcat, _NT), 0.0)
            dv = _dot(w, dsb, _TN)
            for c in range(GMLP_NC):
                rows, cols = slice(c * CHUNK, (c + 1) * CHUNK), slice(g * CHUNK, (g + 1) * CHUNK)
                dvn_scr[rows, cols] = dv[:, c * CHUNK:(c + 1) * CHUNK]
                du_scr[rows, cols] = dya[rows, cols] * s[:, c * CHUNK:(c + 1) * CHUNK]
        gbt_ref[...] += gbt
        dvn = dvn_scr[...]
        gvg_ref[...] += jnp.sum(dvn * vh, axis=0, keepdims=True)
        gvb_ref[...] += jnp.sum(dvn, axis=0, keepdims=True)
        dvh = dvn * vgv
        dv = rstd * (dvh - jnp.mean(dvh, axis=-1, keepdims=True) - vh * jnp.mean(dvh * vh, axis=-1, keepdims=True))
        duv_ref[:, :D_MODEL] = (du_scr[...] * _gelu_grad(u_pre)).astype(BF16)
        duv_ref[:, D_MODEL:] = (dv * _gelu_grad(v_pre)).astype(BF16)

    return pl.pallas_call(
        body, name=name, grid=(t // GMLP_TILE,),
        in_specs=[_row_spec(D_MODEL, 0, GMLP_TILE), _row_spec(D_MODEL, 1, GMLP_TILE), _row_spec(D_MODEL, 0, GMLP_TILE),
                  _vec_spec(D_MODEL), _vec_spec(D_MODEL), pl.BlockSpec((GROUPS, CHUNK, CHUNK), lambda i: (0, 0, 0)),
                  pl.BlockSpec((CHUNK, GROUPS), lambda i: (0, 0)), pl.BlockSpec(memory_space=pl.ANY)],
        out_specs=[_row_spec(2 * D_MODEL, 0, GMLP_TILE), pl.BlockSpec((GROUPS, CHUNK, CHUNK), lambda i: (0, 0, 0)),
                   pl.BlockSpec((CHUNK, GROUPS), lambda i: (0, 0)), _vec_spec(D_MODEL), _vec_spec(D_MODEL)],
        out_shape=[jax.ShapeDtypeStruct(dproj.shape, BF16), jax.ShapeDtypeStruct((GROUPS, CHUNK, CHUNK), F32),
                   jax.ShapeDtypeStruct((CHUNK, GROUPS), F32), jax.ShapeDtypeStruct((1, D_MODEL), F32),
                   jax.ShapeDtypeStruct((1, D_MODEL), F32)],
        scratch_shapes=[pltpu.VMEM((GMLP_TILE, D_MODEL), F32), pltpu.VMEM((GMLP_TILE, D_MODEL), F32)],
        input_output_aliases={7: 0},
        compiler_params=_params(("arbitrary",), 6 * _nbytes((GMLP_TILE, D_MODEL), F32)),
    )(proj, proj, dya, vg, vb, wsp, bsp_t, dproj)


CONV_TILE = 512
CONV_COLS = 1024
CONV_RB = 32
HALO = SUBLANES


def _conv_fwd(proj, cw, cb, name):
    t = proj.shape[0]
    nj = CONV_DIM // CONV_COLS
    xcb = COL_XBC // CONV_COLS
    rb = CONV_TILE // HALO

    def body(x_ref, prev_ref, cw_ref, cb_ref, pre_ref, xc_ref):
        i = pl.program_id(1)
        cw_v = cw_ref[...]
        cb_v = cb_ref[...]
        for b in range(CONV_TILE // CONV_RB):
            if b == 0:
                ext = jnp.concatenate([jnp.where(i > 0, prev_ref[...], 0.0), x_ref[:CONV_RB, :]], axis=0)
            else:
                ext = x_ref[b * CONV_RB - HALO:(b + 1) * CONV_RB, :]
            pre = cb_v + cw_v[CONV_WIDTH - 1:CONV_WIDTH, :] * ext[HALO:, :]
            for k in range(CONV_WIDTH - 1):
                back = CONV_WIDTH - 1 - k
                pre = pre + cw_v[k:k + 1, :] * pltpu.roll(ext, back, 0)[HALO:, :]
            pre_ref[b * CONV_RB:(b + 1) * CONV_RB, :] = pre
            xc_ref[b * CONV_RB:(b + 1) * CONV_RB, :] = pre * _sigmoid(pre)

    tile = pl.BlockSpec((CONV_TILE, CONV_COLS), lambda j, i: (i, j))
    return pl.pallas_call(
        body, name=name, grid=(nj, t // CONV_TILE),
        in_specs=[pl.BlockSpec((CONV_TILE, CONV_COLS), lambda j, i: (i, xcb + j)),
                  pl.BlockSpec((HALO, CONV_COLS), lambda j, i: (jnp.maximum(i * rb - 1, 0), xcb + j)),
                  pl.BlockSpec((CONV_WIDTH, CONV_COLS), lambda j, i: (0, j)),
                  pl.BlockSpec((1, CONV_COLS), lambda j, i: (0, j))],
        out_specs=[tile, tile],
        out_shape=[jax.ShapeDtypeStruct((t, CONV_DIM), F32), jax.ShapeDtypeStruct((t, CONV_DIM), F32)],
        compiler_params=_params(("parallel", "parallel"), 4 * _nbytes((CONV_TILE, CONV_COLS), F32)),
    )(proj, proj, cw, cb)


def _fold_rows(v):
    out = v[:SUBLANES]
    for r in range(1, v.shape[0] // SUBLANES):
        out = out + v[r * SUBLANES:(r + 1) * SUBLANES]
    return out


def _conv_bwd(proj, pre, dxc, cw, dproj, name):
    t = proj.shape[0]
    nj = CONV_DIM // CONV_COLS
    ni = t // CONV_TILE
    xcb = COL_XBC // CONV_COLS
    rb = CONV_TILE // HALO
    last_rb = t // HALO - 1

    def body(x_ref, p_ref, pnext_ref, d_ref, dnext_ref, cw_ref, dproj_in, dx_ref, gw_ref, gb_ref):
        del dproj_in
        i = pl.program_id(1)
        cw_v = cw_ref[...]

        def dpre_of(p, d):
            sg = _sigmoid(p)
            return d * sg * (1.0 + p * (1.0 - sg))

        @pl.when(i == 0)
        def _():
            gw_ref[...] = jnp.zeros_like(gw_ref)
            gb_ref[...] = jnp.zeros_like(gb_ref)

        head = dpre_of(pnext_ref[...], jnp.where(i < ni - 1, dnext_ref[...], 0.0))
        gb_acc = jnp.zeros((SUBLANES, CONV_COLS), F32)
        gw_acc = [jnp.zeros((SUBLANES, CONV_COLS), F32) for _ in range(CONV_WIDTH)]
        for b in reversed(range(CONV_TILE // CONV_RB)):
            rows = slice(b * CONV_RB, (b + 1) * CONV_RB)
            cur = dpre_of(p_ref[rows, :], d_ref[rows, :])
            ext = jnp.concatenate([cur, head], axis=0)
            xv = x_ref[rows, :]
            dx = None
            for k in range(CONV_WIDTH):
                shift = CONV_WIDTH - 1 - k
                win = cur if shift == 0 else pltpu.roll(ext, CONV_RB + HALO - shift, 0)[:CONV_RB, :]
                term = cw_v[k:k + 1, :] * win
                dx = term if dx is None else dx + term
                gw_acc[k] = gw_acc[k] + _fold_rows(win * xv)
            dx_ref[rows, :] = dx.astype(BF16)
            gb_acc = gb_acc + _fold_rows(cur)
            head = cur[:HALO]
        gb_ref[...] += jnp.sum(gb_acc, axis=0, keepdims=True)
        for k in range(CONV_WIDTH):
            gw_ref[k:k + 1, :] += jnp.sum(gw_acc[k], axis=0, keepdims=True)

    tile = pl.BlockSpec((CONV_TILE, CONV_COLS), lambda j, i: (i, j))
    after = pl.BlockSpec((HALO, CONV_COLS), lambda j, i: (jnp.minimum((i + 1) * rb, last_rb), j))
    return pl.pallas_call(
        body, name=name, grid=(nj, ni),
        in_specs=[pl.BlockSpec((CONV_TILE, CONV_COLS), lambda j, i: (i, xcb + j)), tile, after, tile, after,
                  pl.BlockSpec((CONV_WIDTH, CONV_COLS), lambda j, i: (0, j)),
                  pl.BlockSpec(memory_space=pl.ANY)],
        out_specs=[pl.BlockSpec((CONV_TILE, CONV_COLS), lambda j, i: (i, xcb + j)),
                   pl.BlockSpec((CONV_WIDTH, CONV_COLS), lambda j, i: (0, j)),
                   pl.BlockSpec((1, CONV_COLS), lambda j, i: (0, j))],
        out_shape=[jax.ShapeDtypeStruct(dproj.shape, BF16), jax.ShapeDtypeStruct((CONV_WIDTH, CONV_DIM), F32),
                   jax.ShapeDtypeStruct((1, CONV_DIM), F32)],
        input_output_aliases={6: 0},
        compiler_params=_params(("parallel", "arbitrary"), 4 * _nbytes((CONV_TILE, CONV_COLS), F32)),
    )(proj, pre, pre, dxc, dxc, cw, dproj)


def _ssd_decays(dt_raw, dtb, alog, e_bf, tril_bf):
    dtv = _softplus(dt_raw + dtb)
    a = -jnp.exp(alog)
    cs = _dot_exact_lhs(tril_bf, dtv * a, _NN)
    cs_last = cs[CHUNK - 1:CHUNK, :]
    stack = jnp.concatenate([dtv, jnp.exp(cs), jnp.exp(cs_last - cs)], axis=0)
    full = _head_expand(stack, e_bf)
    return dtv, a, cs, full[:CHUNK], full[CHUNK:2 * CHUNK], full[2 * CHUNK:]


def _split2(x):
    hi = x.astype(BF16)
    return hi, (x - hi.astype(F32)).astype(BF16)


def _head_expand(x, e_bf):
    hi, mid = _split2(x)
    return _dot(hi, e_bf, _NN) + _dot(mid, e_bf, _NN)


def _head_sums(x, e_bf):
    return _dot_exact_rhs(x, e_bf, _NT)


def _head_mats(cs, cs_t, cb, h, mask):
    seg = cs[:, h:h + 1] - cs_t[h:h + 1, :]
    lmat = jnp.exp(jnp.where(mask, seg, -jnp.inf))
    return lmat, cb * lmat


def _ssd_fwd(xc, proj, dt_raw, dtb, alog, dskip_full, ng, e_bf, name):
    t = xc.shape[0]
    nc = t // CHUNK
    zcb = COL_Z // D_INNER

    def body(xc_ref, z_ref, dt_ref, dtb_ref, alog_ref, dsk_ref, ng_ref, e_ref, y_ref, yb_ref, sprev_ref, s_scr):
        @pl.when(pl.program_id(0) == 0)
        def _():
            s_scr[...] = jnp.zeros_like(s_scr)

        mask = _tri(True)
        tril_bf = mask.astype(BF16)
        e_v = e_ref[...]
        _, _, cs, dt_full, ecs_full, decay_full = _ssd_decays(dt_ref[...], dtb_ref[...], alog_ref[...], e_v, tril_bf)
        cs_t = cs.T
        sprev_ref[0] = s_scr[...]
        for g in range(GROUPS):
            gc = slice(g * GROUP_W, (g + 1) * GROUP_W)
            xs = xc_ref[:, gc]
            xdt = xs * dt_full[:, gc]
            xdt_b = xdt.astype(BF16)
            xdec = (xdt * decay_full[:, gc]).astype(BF16)
            bg = xc_ref[:, D_INNER + g * D_STATE:D_INNER + (g + 1) * D_STATE].astype(BF16)
            cg = xc_ref[:, D_INNER + GROUPS * D_STATE + g * D_STATE:D_INNER + GROUPS * D_STATE + (g + 1) * D_STATE].astype(BF16)
            cb = _dot(cg, bg, _NT)
            s_prev = s_scr[:, gc]
            y_off = ecs_full[:, gc] * _dot(cg, s_prev.astype(BF16), _NN)
            s_scr[:, gc] = s_prev * ecs_full[CHUNK - 1:CHUNK, gc] + _dot(bg, xdec, _TN)
            parts = []
            for r in range(GROUP_W // HEAD_DIM):
                h = g * (GROUP_W // HEAD_DIM) + r
                _, m = _head_mats(cs, cs_t, cb, h, mask)
                parts.append(_dot(m.astype(BF16), xdt_b[:, r * HEAD_DIM:(r + 1) * HEAD_DIM], _NN))
            yg = jnp.concatenate(parts, axis=1) + y_off + dsk_ref[:, gc] * xs
            y_ref[:, gc] = yg
            zv = z_ref[:, gc]
            ygate = yg * (zv * _sigmoid(zv))
            rstd = lax.rsqrt(jnp.mean(ygate * ygate, axis=-1, keepdims=True) + NORM_EPS)
            yb_ref[:, gc] = (ygate * rstd * ng_ref[:, gc]).astype(BF16)

    vec = lambda w: pl.BlockSpec((1, w), lambda i: (0, 0))
    blk = _nbytes((CHUNK, CONV_DIM), F32) + 3 * _nbytes((CHUNK, D_INNER), F32) + _nbytes((D_STATE, D_INNER), F32)
    return pl.pallas_call(
        body, name=name, grid=(nc,),
        in_specs=[pl.BlockSpec((CHUNK, CONV_DIM), lambda i: (i, 0)), pl.BlockSpec((CHUNK, D_INNER), lambda i: (i, zcb)),
                  pl.BlockSpec((CHUNK, DT_PAD), lambda i: (i, 0)), vec(DT_PAD), vec(DT_PAD), vec(D_INNER), vec(D_INNER),
                  pl.BlockSpec((DT_PAD, D_INNER), lambda i: (0, 0))],
        out_specs=[pl.BlockSpec((CHUNK, D_INNER), lambda i: (i, 0)), pl.BlockSpec((CHUNK, D_INNER), lambda i: (i, 0)),
                   pl.BlockSpec((1, D_STATE, D_INNER), lambda i: (i, 0, 0))],
        out_shape=[jax.ShapeDtypeStruct((t, D_INNER), F32), jax.ShapeDtypeStruct((t, D_INNER), BF16),
                   jax.ShapeDtypeStruct((nc, D_STATE, D_INNER), F32)],
        scratch_shapes=[pltpu.VMEM((D_STATE, D_INNER), F32)],
        compiler_params=_params(("arbitrary",), blk),
    )(xc, proj, dt_raw, dtb, alog, dskip_full, ng, e_bf)


def _ssd_bwd(dyb, y, xc, proj, dt_raw, sprev, dtb, alog, dskip_full, ng, e_bf, dproj, name):
    t = xc.shape[0]
    nc = t // CHUNK
    zcb = COL_Z // D_INNER
    hpg = GROUP_W // HEAD_DIM
    rev = lambda i: nc - 1 - i

    def body(dyb_ref, y_ref, xc_ref, z_ref, dt_ref, sprev_ref, dtb_ref, alog_ref, dsk_ref, ng_ref, e_ref, dproj_in,
             dz_ref, dxc_ref, ddt_ref, gng_ref, gdsk_ref, galog_ref, gdtb_ref, ds_scr, sums_scr):
        del dproj_in

        @pl.when(pl.program_id(0) == 0)
        def _():
            ds_scr[...] = jnp.zeros_like(ds_scr)
            gng_ref[...] = jnp.zeros_like(gng_ref)
            gdsk_ref[...] = jnp.zeros_like(gdsk_ref)
            galog_ref[...] = jnp.zeros_like(galog_ref)
            gdtb_ref[...] = jnp.zeros_like(gdtb_ref)

        mask = _tri(True)
        tril_bf = mask.astype(BF16)
        triu_bf = _tri(False).astype(BF16)
        e_v = e_ref[...]
        dt_in = dt_ref[...] + dtb_ref[...]
        dtv, a, cs, dt_full, ecs_full, decay_full = _ssd_decays(dt_ref[...], dtb_ref[...], alog_ref[...], e_v, tril_bf)
        cs_t = cs.T

        lane_h = lax.broadcasted_iota(jnp.int32, (CHUNK, DT_PAD), 1)
        sub_h = lax.broadcasted_iota(jnp.int32, (DT_PAD, CHUNK), 0)
        dcs_rows = jnp.zeros((CHUNK, DT_PAD), F32)
        dcs_cols_t = jnp.zeros((DT_PAD, CHUNK), F32)
        last_cols, dsk_cols = [], []
        for g in range(GROUPS):
            gc = slice(g * GROUP_W, (g + 1) * GROUP_W)
            b_cols = slice(D_INNER + g * D_STATE, D_INNER + (g + 1) * D_STATE)
            c_cols = slice(D_INNER + GROUPS * D_STATE + g * D_STATE, D_INNER + GROUPS * D_STATE + (g + 1) * D_STATE)
            xs = xc_ref[:, gc]
            xdt = xs * dt_full[:, gc]
            xdt_b = xdt.astype(BF16)
            xdec = xdt * decay_full[:, gc]
            xdec_b = xdec.astype(BF16)
            zv = z_ref[:, gc]
            sg = _sigmoid(zv)
            gate = zv * sg
            yv = y_ref[:, gc]
            dybv = dyb_ref[:, gc]
            ygate = yv * gate
            rstd = lax.rsqrt(jnp.mean(ygate * ygate, axis=-1, keepdims=True) + NORM_EPS)
            yn = ygate * rstd
            gng_ref[:, gc] += jnp.sum(dybv * yn, axis=0, keepdims=True)
            dyn = dybv * ng_ref[:, gc]
            dyg = rstd * (dyn - yn * jnp.mean(dyn * yn, axis=-1, keepdims=True))
            dz_ref[:, gc] = (dyg * yv * sg * (1.0 + zv * (1.0 - sg))).astype(BF16)
            dy = dyg * gate
            dy_b = dy.astype(BF16)
            dyo = dy * ecs_full[:, gc]
            dyo_b = dyo.astype(BF16)
            dsk_cols.append(jnp.sum(dy * xs, axis=0, keepdims=True))

            bg = xc_ref[:, b_cols].astype(BF16)
            cg = xc_ref[:, c_cols].astype(BF16)
            s_prev = sprev_ref[0, :, gc]
            s_prev_b = s_prev.astype(BF16)
            dsg = ds_scr[:, gc]
            dsg_b = dsg.astype(BF16)
            cb = _dot(cg, bg, _NT)
            c_s = _dot(cg, s_prev_b, _NN)
            b_ds = _dot(bg, dsg_b, _NN)
            dcb = jnp.zeros((CHUNK, CHUNK), F32)
            parts = []
            for r in range(hpg):
                h = g * hpg + r
                hc = slice(r * HEAD_DIM, (r + 1) * HEAD_DIM)
                lmat, m = _head_mats(cs, cs_t, cb, h, mask)
                dm = _dot(dy_b[:, hc], xdt_b[:, hc], _NT)
                parts.append(_dot(m.astype(BF16), dy_b[:, hc], _TN))
                dcb = dcb + dm * lmat
                w = dm * m
                dcs_rows = jnp.where(lane_h == h, jnp.sum(w, axis=1, keepdims=True), dcs_rows)
                dcs_cols_t = jnp.where(sub_h == h, jnp.sum(w, axis=0, keepdims=True), dcs_cols_t)
            dxdt = jnp.concatenate(parts, axis=1) + decay_full[:, gc] * b_ds
            dcb_b = dcb.astype(BF16)
            dxc_ref[:, c_cols] = _dot(dcb_b, bg, _NN) + _dot(dyo_b, s_prev_b, _NT)
            dxc_ref[:, b_cols] = _dot(dcb_b, cg, _TN) + _dot(xdec_b, dsg_b, _NT)
            cdec = ecs_full[CHUNK - 1:CHUNK, gc]
            ds_scr[:, gc] = _dot(cg, dyo_b, _TN) + cdec * dsg
            dxc_ref[:, gc] = dxdt * dt_full[:, gc] + dsk_ref[:, gc] * dy
            dec_prod = xdec * b_ds
            sums_scr[:CHUNK, gc] = dyo * c_s - dec_prod
            sums_scr[CHUNK:, gc] = dxdt * xs
            last_cols.append(jnp.sum(dec_prod, axis=0, keepdims=True) + cdec * jnp.sum(dsg * s_prev, axis=0, keepdims=True))
        t_sums = _head_sums(sums_scr[...], e_v)
        tail = jnp.concatenate([jnp.concatenate(last_cols, axis=1), jnp.concatenate(dsk_cols, axis=1),
                                jnp.zeros((SUBLANES - 2, D_INNER), F32)], axis=0)
        t_tail = _dot_exact_rhs(tail, e_v, _NT)
        gdsk_ref[...] += t_tail[1:2, :]
        row = lax.broadcasted_iota(jnp.int32, (CHUNK, DT_PAD), 0)
        dcs = dcs_rows - dcs_cols_t.T + t_sums[:CHUNK] + jnp.where(row == CHUNK - 1, t_tail[0:1, :], 0.0)
        dda = _dot_exact_lhs(triu_bf, dcs, _NN)
        galog_ref[...] += jnp.sum(dda * dtv, axis=0, keepdims=True) * a
        ddt = dda * a + t_sums[CHUNK:]
        ddt_raw = jnp.where(lane_h < N_HEADS, ddt * _sigmoid(dt_in), 0.0)
        gdtb_ref[...] += jnp.sum(ddt_raw, axis=0, keepdims=True)
        ddt_ref[...] = ddt_raw.astype(BF16)

    vec = lambda w: pl.BlockSpec((1, w), lambda i: (0, 0))
    blk = (2 * _nbytes((CHUNK, CONV_DIM), F32) + 4 * _nbytes((CHUNK, D_INNER), F32) + 4 * _nbytes((D_STATE, D_INNER), F32))
    return pl.pallas_call(
        body, name=name, grid=(nc,),
        in_specs=[pl.BlockSpec((CHUNK, D_INNER), lambda i: (rev(i), 0)), pl.BlockSpec((CHUNK, D_INNER), lambda i: (rev(i), 0)),
                  pl.BlockSpec((CHUNK, CONV_DIM), lambda i: (rev(i), 0)), pl.BlockSpec((CHUNK, D_INNER), lambda i: (rev(i), zcb)),
                  pl.BlockSpec((CHUNK, DT_PAD), lambda i: (rev(i), 0)), pl.BlockSpec((1, D_STATE, D_INNER), lambda i: (rev(i), 0, 0)),
                  vec(DT_PAD), vec(DT_PAD), vec(D_INNER), vec(D_INNER), pl.BlockSpec((DT_PAD, D_INNER), lambda i: (0, 0)),
                  pl.BlockSpec(memory_space=pl.ANY)],
        out_specs=[pl.BlockSpec((CHUNK, D_INNER), lambda i: (rev(i), zcb)), pl.BlockSpec((CHUNK, CONV_DIM), lambda i: (rev(i), 0)),
                   pl.BlockSpec((CHUNK, DT_PAD), lambda i: (rev(i), 0)), vec(D_INNER), vec(DT_PAD), vec(DT_PAD), vec(DT_PAD)],
        out_shape=[jax.ShapeDtypeStruct(dproj.shape, BF16), jax.ShapeDtypeStruct((t, CONV_DIM), F32),
                   jax.ShapeDtypeStruct((t, DT_PAD), BF16), jax.ShapeDtypeStruct((1, D_INNER), F32),
                   jax.ShapeDtypeStruct((1, DT_PAD), F32), jax.ShapeDtypeStruct((1, DT_PAD), F32),
                   jax.ShapeDtypeStruct((1, DT_PAD), F32)],
        scratch_shapes=[pltpu.VMEM((D_STATE, D_INNER), F32), pltpu.VMEM((2 * CHUNK, D_INNER), F32)],
        input_output_aliases={11: 0},
        compiler_params=_params(("arbitrary",), blk),
    )(dyb, y, xc, proj, dt_raw, sprev, dtb, alog, dskip_full, ng, e_bf, dproj)


_HBM = pl.BlockSpec(memory_space=pl.ANY)


def _mesh_pos():
    return lax.axis_index("x"), lax.axis_index("y"), lax.axis_index("c")


def _other_chips(x, y):
    return [(1 - x, y), (x, 1 - y), (1 - x, 1 - y)]


def _all_gather(shards, name):
    n = len(shards)

    def body(*refs):
        ins, outs = refs[:n], refs[n:2 * n]
        send_sems, recv_sems, local_sems = refs[2 * n:]
        x, y, c = _mesh_pos()
        me, sibling = (x, y, c), (x, y, 1 - c)
        chips = _other_chips(x, y)

        def slot(p):
            return 4 * p[0] + 2 * p[1] + p[2]

        def copy(a, k, block, to, src=None):
            dst = outs[a].at[slot(block)]
            return pltpu.make_async_remote_copy(
                src_ref=dst if src is None else src, dst_ref=dst, send_sem=send_sems.at[a * 7 + k],
                recv_sem=recv_sems.at[a * 7 + k], device_id=to, device_id_type=MESH)

        started = []
        own = []
        for a in range(n):
            mine = pltpu.make_async_copy(ins[a], outs[a].at[slot(me)], local_sems.at[a])
            mine.start()
            own.append(mine)
            first = [copy(a, 0, me, sibling, src=ins[a])]
            first += [copy(a, 1 + j, me, (*chip, c), src=ins[a]) for j, chip in enumerate(chips)]
            for cp in first:
                cp.start()
            started += first
        for a in range(n):
            for j, chip in enumerate(chips):
                copy(a, 1 + j, (*chip, c), me).wait_recv()
                fwd = copy(a, 4 + j, (*chip, c), sibling)
                fwd.start()
                started.append(fwd)
        for a in range(n):
            copy(a, 0, sibling, me).wait_recv()
            for j, chip in enumerate(chips):
                copy(a, 4 + j, (*chip, 1 - c), me).wait_recv()
        for cp in started:
            cp.wait_send()
        for mine in own:
            mine.wait()

    return pl.pallas_call(
        body, name=name,
        in_specs=[_HBM] * n, out_specs=[_HBM] * n,
        out_shape=[jax.ShapeDtypeStruct((N_DEV,) + s.shape, s.dtype) for s in shards],
        scratch_shapes=[pltpu.SemaphoreType.DMA((7 * n,)), pltpu.SemaphoreType.DMA((7 * n,)),
                        pltpu.SemaphoreType.DMA((n,))],
    )(*shards)


def _exchange_all(packed, name):
    def body(in_ref, out_ref, send_sems, recv_sems, local_sem):
        x, y, c = _mesh_pos()
        my_slot = 4 * x + 2 * y + c
        mine = pltpu.make_async_copy(in_ref, out_ref.at[my_slot], local_sem)
        mine.start()
        copies = []
        for k in range(1, N_DEV):
            fx, fy, fc = (k >> 2) & 1, (k >> 1) & 1, k & 1
            peer = (x + fx - 2 * x * fx, y + fy - 2 * y * fy, c + fc - 2 * c * fc)
            peer_slot = 4 * peer[0] + 2 * peer[1] + peer[2]
            send = pltpu.make_async_remote_copy(
                src_ref=in_ref, dst_ref=out_ref.at[my_slot], send_sem=send_sems.at[k - 1], recv_sem=recv_sems.at[k - 1],
                device_id=peer, device_id_type=MESH)
            send.start()
            recv = pltpu.make_async_remote_copy(
                src_ref=in_ref, dst_ref=out_ref.at[peer_slot], send_sem=send_sems.at[k - 1], recv_sem=recv_sems.at[k - 1],
                device_id=peer, device_id_type=MESH)
            copies.append((send, recv))
        for send, recv in copies:
            send.wait_send()
            recv.wait_recv()
        mine.wait()

    return pl.pallas_call(
        body, name=name, in_specs=[_HBM], out_specs=_HBM,
        out_shape=jax.ShapeDtypeStruct((N_DEV,) + packed.shape, packed.dtype),
        scratch_shapes=[pltpu.SemaphoreType.DMA((N_DEV - 1,)), pltpu.SemaphoreType.DMA((N_DEV - 1,)),
                        pltpu.SemaphoreType.DMA],
    )(packed)


def _swap_with_sibling(grads, name):
    n = len(grads)

    def body(*refs):
        ins, outs = refs[:n], refs[n:2 * n]
        send_sems, recv_sems = refs[2 * n:]
        x, y, c = _mesh_pos()
        copies = []
        for a in range(n):
            for k in range(N_CHIP):
                cp = pltpu.make_async_remote_copy(
                    src_ref=ins[a].at[(1 - c) + 2 * k], dst_ref=outs[a].at[k], send_sem=send_sems.at[a * N_CHIP + k],
                    recv_sem=recv_sems.at[a * N_CHIP + k], device_id=(x, y, 1 - c), device_id_type=MESH)
                cp.start()
                copies.append(cp)
        for cp in copies:
            cp.wait()

    return pl.pallas_call(
        body, name=name, in_specs=[_HBM] * n, out_specs=[_HBM] * n,
        out_shape=[jax.ShapeDtypeStruct((N_CHIP,) + g.shape[1:], g.dtype) for g in grads],
        scratch_shapes=[pltpu.SemaphoreType.DMA((N_CHIP * n,)), pltpu.SemaphoreType.DMA((N_CHIP * n,))],
    )(*grads)


def _scatter_to_chips(parts, name):
    n = len(parts)

    def body(*refs):
        ins, outs = refs[:n], refs[n:2 * n]
        send_sems, recv_sems, local_sems = refs[2 * n:]
        x, y, c = _mesh_pos()
        my_chip = 2 * x + y
        own, copies = [], []
        for a in range(n):
            mine = pltpu.make_async_copy(ins[a].at[my_chip], outs[a].at[my_chip], local_sems.at[a])
            mine.start()
            own.append(mine)
            for j, chip in enumerate(_other_chips(x, y)):
                send = pltpu.make_async_remote_copy(
                    src_ref=ins[a].at[2 * chip[0] + chip[1]], dst_ref=outs[a].at[my_chip], send_sem=send_sems.at[a * 3 + j],
                    recv_sem=recv_sems.at[a * 3 + j], device_id=(*chip, c), device_id_type=MESH)
                send.start()
                recv = pltpu.make_async_remote_copy(
                    src_ref=ins[a].at[my_chip], dst_ref=outs[a].at[2 * chip[0] + chip[1]], send_sem=send_sems.at[a * 3 + j],
                    recv_sem=recv_sems.at[a * 3 + j], device_id=(*chip, c), device_id_type=MESH)
                copies.append((send, recv))
        for send, recv in copies:
            send.wait_send()
            recv.wait_recv()
        for mine in own:
            mine.wait()

    return pl.pallas_call(
        body, name=name, in_specs=[_HBM] * n, out_specs=[_HBM] * n,
        out_shape=[jax.ShapeDtypeStruct(p.shape, p.dtype) for p in parts],
        scratch_shapes=[pltpu.SemaphoreType.DMA((3 * n,)), pltpu.SemaphoreType.DMA((3 * n,)),
                        pltpu.SemaphoreType.DMA((n,))],
    )(*parts)


def _ew_block(rows, cols, slots):
    budget = 2 * 1024 * 1024
    br, bc = rows, cols
    while slots * br * bc * 4 > budget:
        if br % 2 == 0 and (br // 2) % (2 * SUBLANES) == 0:
            br //= 2
        elif bc % 2 == 0 and (bc // 2) % LANES == 0:
            bc //= 2
        else:
            break
    return br, bc


def _add_sibling(grads, recv, c_idx, name):
    _, rows, cols = grads.shape
    br, bc = _ew_block(rows, cols, 3)

    def body(c_ref, g_ref, r_ref, out_ref):
        del c_ref
        out_ref[...] = (g_ref[...].astype(F32) + r_ref[...].astype(F32)).astype(out_ref.dtype)

    grid_spec = pltpu.PrefetchScalarGridSpec(
        num_scalar_prefetch=1, grid=(N_CHIP, rows // br, cols // bc),
        in_specs=[pl.BlockSpec((1, br, bc), lambda k, i, j, c_ref: (c_ref[0] + 2 * k, i, j)),
                  pl.BlockSpec((1, br, bc), lambda k, i, j, c_ref: (k, i, j))],
        out_specs=pl.BlockSpec((1, br, bc), lambda k, i, j, c_ref: (k, i, j)))
    return pl.pallas_call(
        body, name=name, grid_spec=grid_spec, out_shape=jax.ShapeDtypeStruct((N_CHIP, rows, cols), grads.dtype),
        compiler_params=_params(("parallel", "parallel", "parallel"), 3 * _nbytes((br, bc), F32)),
    )(c_idx, grads, recv)


def _adamw(slots, w, m, v, name):
    ns, rows, cols = slots.shape
    br, bc = _ew_block(rows, cols, ns + 7)
    c1 = 1.0 / (1.0 - ADAM_B1 ** ADAM_STEP)
    c2 = 1.0 / (1.0 - ADAM_B2 ** ADAM_STEP)

    def body(s_ref, w_ref, m_ref, v_ref, g_ref, d_ref, m2_ref, v2_ref):
        g = s_ref[0].astype(F32)
        for k in range(1, ns):
            g = g + s_ref[k].astype(F32)
        m2 = ADAM_B1 * m_ref[...] + (1.0 - ADAM_B1) * g
        v2 = ADAM_B2 * v_ref[...] + (1.0 - ADAM_B2) * (g * g)
        g_ref[...] = g
        m2_ref[...] = m2
        v2_ref[...] = v2
        d_ref[...] = -ADAM_LR * ((m2 * c1) / (jnp.sqrt(v2 * c2) + ADAM_EPS) + ADAM_WD * w_ref[...])

    blk = pl.BlockSpec((br, bc), lambda i, j: (i, j))
    return pl.pallas_call(
        body, name=name, grid=(rows // br, cols // bc),
        in_specs=[pl.BlockSpec((ns, br, bc), lambda i, j: (0, i, j)), blk, blk, blk],
        out_specs=[blk, blk, blk, blk],
        out_shape=[jax.ShapeDtypeStruct((rows, cols), F32)] * 4,
        compiler_params=_params(("parallel", "parallel"), (ns + 7) * _nbytes((br, bc), F32)),
    )(slots, w, m, v)


_SMALL = ["norm_mix_g", "conv_b", "dt_bias", "a_log", "d_skip", "ssm_norm_g", "v_norm_g", "v_norm_b", "w_spatial",
          "b_spatial", "b_gates", "norm_mlp_g", "norm_final_g"]
_SHARDED = ["w_in", "w_proj_a", "w_proj_b", "w_out", "w_mlp_up", "w_mlp_down"]


def _pack(arrays):
    flat = []
    for arr in arrays:
        f = arr.reshape(-1).astype(F32)
        pad = (-f.shape[0]) % LANES
        flat.append(jnp.pad(f, (0, pad)) if pad else f)
    out = jnp.concatenate(flat)
    pad = (-out.shape[0]) % (SUBLANES * LANES)
    if pad:
        out = jnp.pad(out, (0, pad))
    return out.reshape(-1, LANES)


def _unpack(packed, shapes):
    flat = packed.reshape(-1)
    out, off = [], 0
    for shape in shapes:
        size = math.prod(shape)
        out.append(flat[off:off + size].reshape(shape))
        off += size + ((-size) % LANES)
    return out


def _mm_tiles(mode, m, n, k):
    tn = min(n, 1024)
    if mode == "tn":
        return min(m, 1024), tn, min(k, 2048)
    if k <= 2048:
        return min(m, 1024), tn, k
    if k <= 4096:
        return min(m, 512), tn, k
    return min(m, 1024), tn, 2048


def _local_step(x, target, wts, small):
    t = x.shape[0]
    w_main_t, w_dt_t = wts["w_main_t"], wts["w_dt_t"]
    bsp_t = small["b_spatial"].T
    pad32 = lambda a: jnp.pad(a, ((0, 0), (0, DT_PAD - N_HEADS)))
    dtb, alog = pad32(small["dt_bias"]), pad32(small["a_log"])
    dskip_full = jnp.repeat(small["d_skip"], HEAD_DIM, axis=1)
    head_of_col = lax.broadcasted_iota(jnp.int32, (DT_PAD, D_INNER), 1) // HEAD_DIM
    e_bf = (head_of_col == lax.broadcasted_iota(jnp.int32, (DT_PAD, D_INNER), 0)).astype(BF16)

    def mm(a, b, mode, name, **kw):
        if mode == "nn":
            m, k, n = a.shape[0], a.shape[1], b.shape[1]
        elif mode == "nt":
            m, k, n = a.shape[0], a.shape[1], b.shape[0]
        else:
            m, k, n = a.shape[1], a.shape[0], b.shape[1]
        tm, tn, tk = _mm_tiles(mode, m, n, k)
        kw.setdefault("out_dtypes", (BF16,) if mode == "tn" else (F32,))
        if "extra_specs" in kw:
            kw["extra_specs"] = kw["extra_specs"](tm, tn)
        return _matmul(a, b, mode=mode, tm=tm, tn=tn, tk=tk, name=name, **kw)

    def out_tile(tm, tn):
        return (((tm, tn), lambda i, j: (i, j)),)

    h = _rms_fwd(x, small["norm_mix_g"], "rms_mix")
    proj = mm(h, w_main_t, "nt", "proj_main", j_outer=True)
    dt_raw = mm(h, w_dt_t, "nt", "proj_dt")
    y_a = _gmlp_fwd(proj, small["v_norm_g"], small["v_norm_b"], small["w_spatial"], bsp_t, "gmlp_fwd")
    pre_conv, xc = _conv_fwd(proj, wts["conv_w"], small["conv_b"], "conv_fwd")
    y_ssd, y_b, sprev = _ssd_fwd(xc, proj, dt_raw, dtb, alog, dskip_full, small["ssm_norm_g"], e_bf, "ssd_fwd")
    pa = mm(y_a, wts["w_proj_a"], "nn", "proj_a")
    pb = mm(y_b, wts["w_proj_b"], "nn", "proj_b")
    merged = _merge_fwd(pa, pb, proj, small["b_gates"], "merge_fwd")

    def add_residual(acc, ex, outs):
        outs[0][...] = acc + ex[0][...]

    x1 = mm(merged, wts["w_out"], "nn", "out_proj", epilogue=add_residual, extras=(x,), extra_specs=out_tile)
    h2 = _rms_fwd(x1, small["norm_mlp_g"], "rms_mlp")

    def relu_sq(acc, ex, outs):
        outs[0][...] = acc
        r = jnp.maximum(acc, 0.0)
        outs[1][...] = (r * r).astype(BF16)

    up, act = mm(h2, wts["w_mlp_up"], "nn", "mlp_up", epilogue=relu_sq, out_dtypes=(F32, BF16), j_outer=True)
    x2 = mm(act, wts["w_mlp_down"], "nn", "mlp_down", epilogue=add_residual, extras=(x1,), extra_specs=out_tile)
    _, dx2, dx2_b, g_final, loss = _loss_head(x2, small["norm_final_g"], target, "loss_head")

    def relu_sq_bwd(acc, ex, outs):
        outs[0][...] = (acc * 2.0 * jnp.maximum(ex[0][...], 0.0)).astype(BF16)

    dup = mm(dx2_b, wts["w_mlp_down"], "nt", "d_act", epilogue=relu_sq_bwd, extras=(up,), extra_specs=out_tile,
             out_dtypes=(BF16,), j_outer=True)
    g_down = mm(act, dx2_b, "tn", "g_mlp_down")
    g_up = mm(h2, dup, "tn", "g_mlp_up")
    dh2 = mm(dup, wts["w_mlp_up"], "nt", "d_h2")
    dx1, dx1_b, g_mlp = _rms_bwd(x1, small["norm_mlp_g"], dh2, dx2, "rms_mlp_bwd")

    g_out = mm(merged, dx1_b, "tn", "g_out")
    dmerged = mm(dx1_b, wts["w_out"], "nt", "d_merged")
    dpa, dpb, dproj, g_bgates = _merge_bwd(dmerged, pa, pb, proj, small["b_gates"], "merge_bwd")
    g_pa = mm(y_a, dpa, "tn", "g_proj_a")
    g_pb = mm(y_b, dpb, "tn", "g_proj_b")
    dya = mm(dpa, wts["w_proj_a"], "nt", "d_ya")
    dyb = mm(dpb, wts["w_proj_b"], "nt", "d_yb")

    dproj, g_wsp, g_bsp_t, g_vg, g_vb = _gmlp_bwd(proj, dya, small["v_norm_g"], small["v_norm_b"], small["w_spatial"],
                                                   bsp_t, dproj, "gmlp_bwd")
    dproj, dxc, ddt, g_ng, g_dskip, g_alog, g_dtb = _ssd_bwd(dyb, y_ssd, xc, proj, dt_raw, sprev, dtb, alog, dskip_full,
                                                             small["ssm_norm_g"], e_bf, dproj, "ssd_bwd")
    dproj, g_convw, g_convb = _conv_bwd(proj, pre_conv, dxc, wts["conv_w"], dproj, "conv_bwd")

    g_main_t = mm(dproj, h, "tn", "g_in_main")
    g_dt_t = mm(ddt, h, "tn", "g_in_dt")

    def add_dt(acc, ex, outs):
        outs[0][...] = acc + _dot(ex[0][...], ex[1][...], _NN)

    dh = mm(dproj, w_main_t, "nn", "d_h", epilogue=add_dt, extras=(ddt, w_dt_t),
            extra_specs=lambda tm, tn: (((tm, DT_PAD), lambda i, j: (i, 0)), ((DT_PAD, D_MODEL), lambda i, j: (0, 0))))
    grad_x, _, g_mix = _rms_bwd(x, small["norm_mix_g"], dh, dx1, "rms_mix_bwd")

    grads = {
        "w_main_t": g_main_t, "w_dt_t": g_dt_t, "w_proj_a": g_pa, "w_proj_b": g_pb, "w_out": g_out, "w_mlp_up": g_up,
        "w_mlp_down": g_down, "conv_w": g_convw,
        "norm_mix_g": g_mix, "conv_b": g_convb, "dt_bias": g_dtb[:, :N_HEADS], "a_log": g_alog[:, :N_HEADS],
        "d_skip": g_dskip[:, :N_HEADS], "ssm_norm_g": g_ng, "v_norm_g": g_vg, "v_norm_b": g_vb, "w_spatial": g_wsp,
        "b_spatial": g_bsp_t.T, "b_gates": g_bgates, "norm_mlp_g": g_mlp, "norm_final_g": g_final,
    }
    return loss, grad_x, grads


def _split_w_in(w_full_t):
    dt0 = COL_GATE
    w_main_t = jnp.concatenate([w_full_t[:dt0], w_full_t[dt0 + N_HEADS:]], axis=0)
    w_dt_t = jnp.pad(w_full_t[dt0:dt0 + N_HEADS], ((0, DT_PAD - N_HEADS), (0, 0)))
    return w_main_t, w_dt_t


def _join_w_in(g_main_t, g_dt_t):
    dt0 = COL_GATE
    return jnp.concatenate([g_main_t[:dt0], g_dt_t[:N_HEADS], g_main_t[dt0:]], axis=0)


def kernel(x, norm_mix_g, w_in, conv_w, conv_b, dt_bias, a_log, d_skip, ssm_norm_g, v_norm_g, v_norm_b, w_spatial, b_spatial, b_gates, w_proj_a, w_proj_b, w_out, norm_mlp_g, w_mlp_up, w_mlp_down, norm_final_g, loss_target, m_norm_mix_g, m_w_in, m_conv_w, m_conv_b, m_dt_bias, m_a_log, m_d_skip, m_ssm_norm_g, m_v_norm_g, m_v_norm_b, m_w_spatial, m_b_spatial, m_b_gates, m_w_proj_a, m_w_proj_b, m_w_out, m_norm_mlp_g, m_w_mlp_up, m_w_mlp_down, m_norm_final_g, v_norm_mix_g, v_w_in, v_conv_w, v_conv_b, v_dt_bias, v_a_log, v_d_skip, v_ssm_norm_g, v_v_norm_g, v_v_norm_b, v_w_spatial, v_b_spatial, v_b_gates, v_w_proj_a, v_w_proj_b, v_w_out, v_norm_mlp_g, v_w_mlp_up, v_w_mlp_down, v_norm_final_g):
    given = dict(locals())
    names = ["norm_mix_g", "w_in", "conv_w", "conv_b", "dt_bias", "a_log", "d_skip", "ssm_norm_g", "v_norm_g", "v_norm_b",
             "w_spatial", "b_spatial", "b_gates", "w_proj_a", "w_proj_b", "w_out", "norm_mlp_g", "w_mlp_up", "w_mlp_down",
             "norm_final_g"]
    shapes = {n: given[n].shape for n in names}
    t = x.shape[1]
    dev = 4 * lax.axis_index("x") + 2 * lax.axis_index("y") + lax.axis_index("c")

    shard2d = {"w_in": w_in[0].T, "w_proj_a": w_proj_a[0], "w_proj_b": w_proj_b[0], "w_out": w_out[0],
               "w_mlp_up": w_mlp_up[0], "w_mlp_down": w_mlp_down[0]}
    conv_shard = conv_w.reshape(CONV_WIDTH, -1)
    gathered = _all_gather([shard2d[n].astype(BF16) for n in _SHARDED] + [conv_shard], "gather_weights")
    gw = dict(zip(_SHARDED, gathered[:-1]))
    by_cols = lambda g: jnp.transpose(g, (1, 0, 2)).reshape(g.shape[1], -1)
    w_main_t, w_dt_t = _split_w_in(gw["w_in"].reshape(-1, D_MODEL))
    wts = {"w_main_t": w_main_t, "w_dt_t": w_dt_t, "w_proj_a": gw["w_proj_a"].reshape(-1, D_MODEL),
           "w_proj_b": gw["w_proj_b"].reshape(-1, D_MODEL), "w_out": gw["w_out"].reshape(-1, D_MODEL),
           "w_mlp_up": by_cols(gw["w_mlp_up"]), "w_mlp_down": gw["w_mlp_down"].reshape(-1, D_MODEL),
           "conv_w": by_cols(gathered[-1])}
    small = {"norm_mix_g": norm_mix_g, "conv_b": conv_b, "dt_bias": dt_bias, "a_log": a_log, "d_skip": d_skip,
             "ssm_norm_g": ssm_norm_g, "v_norm_g": v_norm_g, "v_norm_b": v_norm_b, "w_spatial": w_spatial[0],
             "b_spatial": b_spatial[0], "b_gates": b_gates, "norm_mlp_g": norm_mlp_g,
             "norm_final_g": norm_final_g.reshape(1, -1)}

    loss_part, grad_x, grads = _local_step(x[0], loss_target[0], wts, small)

    by_dev_cols = lambda g: jnp.transpose(g.reshape(g.shape[0], N_DEV, -1), (1, 0, 2))
    by_dev_rows = lambda g: g.reshape(N_DEV, -1, g.shape[1])
    full = {"w_in": by_dev_rows(_join_w_in(grads["w_main_t"], grads["w_dt_t"])), "w_proj_a": by_dev_rows(grads["w_proj_a"]),
            "w_proj_b": by_dev_rows(grads["w_proj_b"]), "w_out": by_dev_rows(grads["w_out"]),
            "w_mlp_up": by_dev_cols(grads["w_mlp_up"]), "w_mlp_down": by_dev_rows(grads["w_mlp_down"])}
    from_sibling = _swap_with_sibling([full[n] for n in _SHARDED], "reduce_cores")
    c_idx = lax.axis_index("c").astype(jnp.int32).reshape(1)
    chip_part = [_add_sibling(full[n], r, c_idx, "add_cores_" + n) for n, r in zip(_SHARDED, from_sibling)]
    by_chip = _scatter_to_chips(chip_part, "reduce_chips")
    out = {}
    for n, slots in zip(_SHARDED, by_chip):
        moments = [given["m_" + n][0], given["v_" + n][0]]
        if n == "w_in":
            moments = [mom.T for mom in moments]
        res = _adamw(slots, shard2d[n], *moments, "adamw_" + n)
        out[n] = [(r.T if n == "w_in" else r).reshape(shapes[n]) for r in res]

    small_shapes = [shapes[n] for n in _SMALL]
    extra = [grads["conv_w"], loss_part]
    packed_g = _pack([grads[n] for n in _SMALL] + extra)
    zeros_extra = [jnp.zeros_like(e) for e in extra]
    all_g = _exchange_all(packed_g, "exchange_small")
    res = _adamw(all_g, _pack([given[n] for n in _SMALL] + zeros_extra), _pack([given["m_" + n] for n in _SMALL] + zeros_extra),
                 _pack([given["v_" + n] for n in _SMALL] + zeros_extra), "adamw_small")
    extra_shapes = [grads["conv_w"].shape, loss_part.shape]
    unpacked = [_unpack(r, small_shapes + extra_shapes) for r in res]
    for i, n in enumerate(_SMALL):
        out[n] = [u[i] for u in unpacked]
    g_conv_full, loss_all = unpacked[0][-2], unpacked[0][-1]
    width = shapes["conv_w"][-1]
    g_conv = lax.dynamic_slice(g_conv_full, (0, dev * width), (CONV_WIDTH, width))
    res = _adamw(g_conv[None], conv_shard, m_conv_w.reshape(CONV_WIDTH, -1), v_conv_w.reshape(CONV_WIDTH, -1), "adamw_conv_w")
    out["conv_w"] = [r.reshape(shapes["conv_w"]) for r in res]

    loss = loss_all[0, 0]
    return (loss, grad_x[None], *[out[n][0] for n in names], *[out[n][1] for n in names],
            *[out[n][2] for n in names], *[out[n][3] for n in names])
```

```python
import functools
import math

import jax
import jax.numpy as jnp
from jax import lax
from jax.experimental import pallas as pl
from jax.experimental.pallas import tpu as pltpu

F32 = jnp.float32
BF16 = jnp.bfloat16
MESH = pl.DeviceIdType.MESH

D_MODEL = 1024
NORM_EPS = 1e-6
CHUNK = 128
GROUPS = 8
D_INNER = 2048
HEAD_DIM = 64
N_HEADS = 32
D_STATE = 128
CONV_WIDTH = 4
CONV_DIM = 4096
D_FF = 4096
GROUP_W = D_INNER // GROUPS
N_DEV = 8
N_CHIP = 4

ADAM_LR = 0.001
ADAM_B1 = 0.9
ADAM_B2 = 0.999
ADAM_EPS = 1e-08
ADAM_WD = 0.01
ADAM_STEP = 10

MAIN_W = 2 * D_MODEL + D_INNER + CONV_DIM + 2 * D_MODEL
COL_Z = 2048
COL_XBC = 4096
COL_GATE = 8192
DT_PAD = 128

LANES = 128
SUBLANES = 8
VMEM_BYTES_V7X = 64 * 1024 * 1024
VMEM_BODY_TEMP = 24 * 1024 * 1024


def _vmem_limit(block_bytes):
    return int(min(2 * block_bytes + VMEM_BODY_TEMP, VMEM_BYTES_V7X - 8 * 1024 * 1024))


def _nbytes(shape, dtype):
    return math.prod(shape) * jnp.dtype(dtype).itemsize


_HBM = pl.BlockSpec(memory_space=pl.ANY)


def _params(sem, block_bytes):
    return pltpu.CompilerParams(dimension_semantics=sem, vmem_limit_bytes=_vmem_limit(block_bytes))


def _sigmoid(x):
    return 1.0 / (1.0 + jnp.exp(-x))


def _softplus(x):
    e = jnp.exp(-jnp.abs(x))
    u = 1.0 + e
    log1p_e = jnp.where(u == 1.0, e, jnp.log(u) * (e / jnp.where(u == 1.0, 1.0, u - 1.0)))
    return jnp.maximum(x, 0.0) + log1p_e


_SQRT_HALF = 0.7071067811865476
_INV_SQRT_2PI = 0.3989422804014327


def _gelu(x):
    return x * (lax.erf(x * _SQRT_HALF) + 1.0) * 0.5


def _gelu_grad(x):
    return 0.5 * (1.0 + lax.erf(x * _SQRT_HALF)) + x * jnp.exp(-0.5 * x * x) * _INV_SQRT_2PI


def _dot(a, b, dims):
    return lax.dot_general(a, b, (dims, ((), ())), preferred_element_type=F32)


_NN = ((1,), (0,))
_NT = ((1,), (1,))
_TN = ((0,), (0,))


def _split3(x):
    hi = x.astype(BF16)
    r1 = x - hi.astype(F32)
    mid = r1.astype(BF16)
    lo = (r1 - mid.astype(F32)).astype(BF16)
    return hi, mid, lo


def _dot_exact_rhs(x, e, dims):
    hi, mid, lo = _split3(x)
    return _dot(hi, e, dims) + _dot(mid, e, dims) + _dot(lo, e, dims)


def _dot_exact_lhs(e, x, dims):
    hi, mid, lo = _split3(x)
    return _dot(e, hi, dims) + _dot(e, mid, dims) + _dot(e, lo, dims)


def _tri(lower):
    r = lax.broadcasted_iota(jnp.int32, (CHUNK, CHUNK), 0)
    c = lax.broadcasted_iota(jnp.int32, (CHUNK, CHUNK), 1)
    return (r >= c) if lower else (r <= c)


def _matmul(a, b, *, mode, tm, tn, tk, out_dtypes, name, epilogue=None, extras=(), extra_specs=(), j_outer=False, deps=()):
    if mode == "nn":
        (m, k), (_, n) = a.shape, b.shape
    elif mode == "nt":
        (m, k), (n, _) = a.shape, b.shape
    else:
        (k, m), (_, n) = a.shape, b.shape
    assert m % tm == 0 and n % tn == 0 and k % tk == 0, (name, m, n, k, tm, tn, tk)
    nk = k // tk
    n_extra, n_out = len(extras), len(out_dtypes)
    first_out = 2 + n_extra + len(deps)
    dims = {"nn": _NN, "nt": _NT, "tn": _TN}[mode]
    if epilogue is None:
        def epilogue(acc, ex, outs):
            outs[0][...] = acc.astype(outs[0].dtype)

    def body(*refs):
        a_ref, b_ref = refs[0], refs[1]
        ex_refs = refs[2:2 + n_extra]
        outs = refs[first_out:first_out + n_out]
        p = _dot(a_ref[...], b_ref[...], dims)
        if nk == 1:
            epilogue(p, ex_refs, outs)
        else:
            acc_ref = refs[first_out + n_out]
            kk = pl.program_id(2)

            @pl.when(kk == 0)
            def _():
                acc_ref[...] = p

            @pl.when(kk > 0)
            def _():
                acc_ref[...] += p

            @pl.when(kk == nk - 1)
            def _():
                epilogue(acc_ref[...], ex_refs, outs)

    if j_outer:
        grid = (n // tn, m // tm, nk)
        ij = lambda g0, g1: (g1, g0)
    else:
        grid = (m // tm, n // tn, nk)
        ij = lambda g0, g1: (g0, g1)

    def wrap(fn):
        return lambda g0, g1, kk: fn(*ij(g0, g1), kk)

    if mode == "nn":
        a_spec = pl.BlockSpec((tm, tk), wrap(lambda i, j, kk: (i, kk)))
        b_spec = pl.BlockSpec((tk, tn), wrap(lambda i, j, kk: (kk, j)))
        a_blk, b_blk = (tm, tk), (tk, tn)
    elif mode == "nt":
        a_spec = pl.BlockSpec((tm, tk), wrap(lambda i, j, kk: (i, kk)))
        b_spec = pl.BlockSpec((tn, tk), wrap(lambda i, j, kk: (j, kk)))
        a_blk, b_blk = (tm, tk), (tn, tk)
    else:
        a_spec = pl.BlockSpec((tk, tm), wrap(lambda i, j, kk: (kk, i)))
        b_spec = pl.BlockSpec((tk, tn), wrap(lambda i, j, kk: (kk, j)))
        a_blk, b_blk = (tk, tm), (tk, tn)
    ex_specs = [pl.BlockSpec(shape, wrap(lambda i, j, kk, f=f: f(i, j))) for shape, f in extra_specs]
    out_spec = [pl.BlockSpec((tm, tn), wrap(lambda i, j, kk: (i, j))) for _ in out_dtypes]
    out_shape = [jax.ShapeDtypeStruct((m, n), dt) for dt in out_dtypes]
    blk = (_nbytes(a_blk, a.dtype) + _nbytes(b_blk, b.dtype) + sum(_nbytes(s, F32) for s, _ in extra_specs)
           + sum(_nbytes((tm, tn), dt) for dt in out_dtypes) + _nbytes((tm, tn), F32))
    res = pl.pallas_call(
        body, name=name, grid=grid,
        in_specs=[a_spec, b_spec] + ex_specs + [_HBM] * len(deps), out_specs=out_spec, out_shape=out_shape,
        scratch_shapes=[pltpu.VMEM((tm, tn), F32)] if nk > 1 else [],
        compiler_params=_params(("parallel", "parallel", "arbitrary"), blk),
    )(a, b, *extras, *deps)
    return res[0] if n_out == 1 else res


ROW_TILE = 256


def _row_spec(width, col_block=0, tile=ROW_TILE):
    return pl.BlockSpec((tile, width), lambda i, cb=col_block: (i, cb))


def _vec_spec(width, col_block=0):
    return pl.BlockSpec((1, width), lambda i, cb=col_block: (0, cb))


def _rms_fwd(x, g, name, deps=()):
    t = x.shape[0]

    def body(x_ref, g_ref, *rest):
        h_ref = rest[-1]
        xv = x_ref[...]
        r = lax.rsqrt(jnp.mean(xv * xv, axis=-1, keepdims=True) + NORM_EPS)
        h_ref[...] = (xv * r * g_ref[...]).astype(BF16)

    return pl.pallas_call(
        body, name=name, grid=(t // ROW_TILE,),
        in_specs=[_row_spec(D_MODEL), _vec_spec(D_MODEL)] + [_HBM] * len(deps), out_specs=_row_spec(D_MODEL),
        out_shape=jax.ShapeDtypeStruct((t, D_MODEL), BF16),
        compiler_params=_params(("parallel",), 3 * _nbytes((ROW_TILE, D_MODEL), F32)),
    )(x, g, *deps)


def _rms_bwd(x, g, dh, dres, name):
    t = x.shape[0]

    def body(x_ref, g_ref, dh_ref, dres_ref, dx_ref, dxb_ref, gg_ref):
        xv = x_ref[...]
        r = lax.rsqrt(jnp.mean(xv * xv, axis=-1, keepdims=True) + NORM_EPS)
        xh = xv * r
        dhv = dh_ref[...]
        dyg = dhv * g_ref[...]
        dx = r * (dyg - xh * jnp.mean(dyg * xh, axis=-1, keepdims=True)) + dres_ref[...]
        dx_ref[...] = dx
        dxb_ref[...] = dx.astype(BF16)

        @pl.when(pl.program_id(0) == 0)
        def _():
            gg_ref[...] = jnp.zeros_like(gg_ref)

        gg_ref[...] += jnp.sum(dhv * xh, axis=0, keepdims=True)

    return pl.pallas_call(
        body, name=name, grid=(t // ROW_TILE,),
        in_specs=[_row_spec(D_MODEL), _vec_spec(D_MODEL), _row_spec(D_MODEL), _row_spec(D_MODEL)],
        out_specs=[_row_spec(D_MODEL), _row_spec(D_MODEL), _vec_spec(D_MODEL)],
        out_shape=[jax.ShapeDtypeStruct((t, D_MODEL), F32), jax.ShapeDtypeStruct((t, D_MODEL), BF16),
                   jax.ShapeDtypeStruct((1, D_MODEL), F32)],
        compiler_params=_params(("arbitrary",), 5 * _nbytes((ROW_TILE, D_MODEL), F32)),
    )(x, g, dh, dres)


def _loss_head(x2, gf, target, name):
    t = x2.shape[0]

    def body(x_ref, g_ref, t_ref, loss_ref, dx_ref, dxb_ref, gg_ref, tot_ref):
        xv = x_ref[...]
        gv = g_ref[...]
        r = lax.rsqrt(jnp.mean(xv * xv, axis=-1, keepdims=True) + NORM_EPS)
        xh = xv * r
        err = xh * gv - t_ref[...]
        dy = err * (1.0 / D_MODEL)
        dyg = dy * gv
        dx = r * (dyg - xh * jnp.mean(dyg * xh, axis=-1, keepdims=True))
        dx_ref[...] = dx
        dxb_ref[...] = dx.astype(BF16)

        @pl.when(pl.program_id(0) == 0)
        def _():
            gg_ref[...] = jnp.zeros_like(gg_ref)
            loss_ref[...] = jnp.zeros_like(loss_ref)

        gg_ref[...] += jnp.sum(dy * xh, axis=0, keepdims=True)
        loss_ref[...] += jnp.sum(err * err, axis=0, keepdims=True)
        tot_ref[...] = jnp.broadcast_to(jnp.sum(loss_ref[...], axis=1, keepdims=True) * (0.5 / D_MODEL), tot_ref.shape)

    return pl.pallas_call(
        body, name=name, grid=(t // ROW_TILE,),
        in_specs=[_row_spec(D_MODEL), _vec_spec(D_MODEL), _row_spec(D_MODEL)],
        out_specs=[_vec_spec(D_MODEL), _row_spec(D_MODEL), _row_spec(D_MODEL), _vec_spec(D_MODEL), _vec_spec(LANES)],
        out_shape=[jax.ShapeDtypeStruct((1, D_MODEL), F32), jax.ShapeDtypeStruct((t, D_MODEL), F32),
                   jax.ShapeDtypeStruct((t, D_MODEL), BF16), jax.ShapeDtypeStruct((1, D_MODEL), F32),
                   jax.ShapeDtypeStruct((1, LANES), F32)],
        compiler_params=_params(("arbitrary",), 5 * _nbytes((ROW_TILE, D_MODEL), F32)),
    )(x2, gf, target)


def _merge_fwd(pa, pb, proj, b_gates, name):
    t = pa.shape[0]
    gcb = COL_GATE // D_MODEL

    def body(pa_ref, pb_ref, la_ref, lb_ref, ba_ref, bb_ref, out_ref):
        ga = _sigmoid(la_ref[...] + ba_ref[...])
        gb = _sigmoid(lb_ref[...] + bb_ref[...])
        out_ref[...] = (ga * pa_ref[...] + gb * pb_ref[...]).astype(BF16)

    return pl.pallas_call(
        body, name=name, grid=(t // ROW_TILE,),
        in_specs=[_row_spec(D_MODEL), _row_spec(D_MODEL), _row_spec(D_MODEL, gcb), _row_spec(D_MODEL, gcb + 1),
                  _vec_spec(D_MODEL, 0), _vec_spec(D_MODEL, 1)],
        out_specs=_row_spec(D_MODEL),
        out_shape=jax.ShapeDtypeStruct((t, D_MODEL), BF16),
        compiler_params=_params(("parallel",), 5 * _nbytes((ROW_TILE, D_MODEL), F32)),
    )(pa, pb, proj, proj, b_gates, b_gates)


def _merge_bwd(dmerged, pa, pb, proj, b_gates, name):
    t = pa.shape[0]
    gcb = COL_GATE // D_MODEL

    def body(dm_ref, pa_ref, pb_ref, la_ref, lb_ref, ba_ref, bb_ref, dpa_ref, dpb_ref, dgl_ref, gb_ref):
        dm = dm_ref[...]
        ga = _sigmoid(la_ref[...] + ba_ref[...])
        gb = _sigmoid(lb_ref[...] + bb_ref[...])
        dpa_ref[...] = (dm * ga).astype(BF16)
        dpb_ref[...] = (dm * gb).astype(BF16)
        dla = dm * pa_ref[...] * ga * (1.0 - ga)
        dlb = dm * pb_ref[...] * gb * (1.0 - gb)
        dgl_ref[:, :D_MODEL] = dla.astype(BF16)
        dgl_ref[:, D_MODEL:] = dlb.astype(BF16)

        @pl.when(pl.program_id(0) == 0)
        def _():
            gb_ref[...] = jnp.zeros_like(gb_ref)

        gb_ref[:, :D_MODEL] += jnp.sum(dla, axis=0, keepdims=True)
        gb_ref[:, D_MODEL:] += jnp.sum(dlb, axis=0, keepdims=True)

    return pl.pallas_call(
        body, name=name, grid=(t // ROW_TILE,),
        in_specs=[_row_spec(D_MODEL), _row_spec(D_MODEL), _row_spec(D_MODEL), _row_spec(D_MODEL, gcb),
                  _row_spec(D_MODEL, gcb + 1), _vec_spec(D_MODEL, 0), _vec_spec(D_MODEL, 1)],
        out_specs=[_row_spec(D_MODEL), _row_spec(D_MODEL), _row_spec(2 * D_MODEL, COL_GATE // (2 * D_MODEL)),
                   _vec_spec(2 * D_MODEL)],
        out_shape=[jax.ShapeDtypeStruct((t, D_MODEL), BF16), jax.ShapeDtypeStruct((t, D_MODEL), BF16),
                   jax.ShapeDtypeStruct((t, MAIN_W), BF16), jax.ShapeDtypeStruct((1, 2 * D_MODEL), F32)],
        compiler_params=_params(("arbitrary",), 8 * _nbytes((ROW_TILE, D_MODEL), F32)),
    )(dmerged, pa, pb, proj, proj, b_gates, b_gates)


GMLP_TILE = 512
GMLP_NC = GMLP_TILE // CHUNK


def _gmlp_common(u_pre, v_pre, vg, vb):
    u = _gelu(u_pre)
    v = _gelu(v_pre)
    mu = jnp.mean(v, axis=-1, keepdims=True)
    vc = v - mu
    rstd = lax.rsqrt(jnp.mean(vc * vc, axis=-1, keepdims=True) + NORM_EPS)
    vh = vc * rstd
    vn = vh * vg + vb
    return u, vh, vn, rstd


def _chunks_to_lanes(x, g):
    return jnp.concatenate([x[c * CHUNK:(c + 1) * CHUNK, g * CHUNK:(g + 1) * CHUNK] for c in range(GMLP_NC)], axis=1)


def _gmlp_fwd(proj, vg, vb, wsp, bsp_t, name):
    t = proj.shape[0]

    def body(u_ref, v_ref, vg_ref, vb_ref, w_ref, b_ref, ya_ref):
        u, _, vn, _ = _gmlp_common(u_ref[...], v_ref[...], vg_ref[...], vb_ref[...])
        mask = _tri(True)
        bt = b_ref[...]
        for g in range(GROUPS):
            w = jnp.where(mask, w_ref[g], 0.0).astype(BF16)
            vcat = _chunks_to_lanes(vn, g).astype(BF16)
            s = _dot(w, vcat, _NN) + bt[:, g:g + 1]
            for c in range(GMLP_NC):
                rows, cols = slice(c * CHUNK, (c + 1) * CHUNK), slice(g * CHUNK, (g + 1) * CHUNK)
                ya_ref[rows, cols] = (u[rows, cols] * s[:, c * CHUNK:(c + 1) * CHUNK]).astype(BF16)

    return pl.pallas_call(
        body, name=name, grid=(t // GMLP_TILE,),
        in_specs=[_row_spec(D_MODEL, 0, GMLP_TILE), _row_spec(D_MODEL, 1, GMLP_TILE), _vec_spec(D_MODEL),
                  _vec_spec(D_MODEL), pl.BlockSpec((GROUPS, CHUNK, CHUNK), lambda i: (0, 0, 0)),
                  pl.BlockSpec((CHUNK, GROUPS), lambda i: (0, 0))],
        out_specs=_row_spec(D_MODEL, 0, GMLP_TILE),
        out_shape=jax.ShapeDtypeStruct((t, D_MODEL), BF16),
        compiler_params=_params(("parallel",), 3 * _nbytes((GMLP_TILE, D_MODEL), F32)),
    )(proj, proj, vg, vb, wsp, bsp_t)


def _gmlp_bwd(proj, dya, vg, vb, wsp, bsp_t, dproj, name):
    t = proj.shape[0]

    def body(u_ref, v_ref, dya_ref, vg_ref, vb_ref, w_ref, b_ref, dproj_in, duv_ref, gw_ref, gbt_ref, gvg_ref, gvb_ref,
             dvn_scr, du_scr):
        del dproj_in
        u_pre, v_pre = u_ref[...], v_ref[...]
        vgv = vg_ref[...]
        u, vh, vn, rstd = _gmlp_common(u_pre, v_pre, vgv, vb_ref[...])
        dya = dya_ref[...]
        mask = _tri(True)
        bt = b_ref[...]
        first = pl.program_id(0) == 0

        @pl.when(first)
        def _():
            gw_ref[...] = jnp.zeros_like(gw_ref)
            gbt_ref[...] = jnp.zeros_like(gbt_ref)
            gvg_ref[...] = jnp.zeros_like(gvg_ref)
            gvb_ref[...] = jnp.zeros_like(gvb_ref)

        lane = lax.broadcasted_iota(jnp.int32, (CHUNK, GROUPS), 1)
        gbt = jnp.zeros((CHUNK, GROUPS), F32)
        for g in range(GROUPS):
            w = jnp.where(mask, w_ref[g], 0.0).astype(BF16)
            vcat = _chunks_to_lanes(vn, g).astype(BF16)
            s = _dot(w, vcat, _NN) + bt[:, g:g + 1]
            ds = _chunks_to_lanes(dya * u, g)
            gbt = jnp.where(lane == g, jnp.sum(ds, axis=1, keepdims=True), gbt)
            dsb = ds.astype(BF16)
            gw_ref[g] += jnp.where(mask, _dot(dsb, vcat, _NT), 0.0)
            dv = _dot(w, dsb, _TN)
            for c in range(GMLP_NC):
                rows, cols = slice(c * CHUNK, (c + 1) * CHUNK), slice(g * CHUNK, (g + 1) * CHUNK)
                dvn_scr[rows, cols] = dv[:, c * CHUNK:(c + 1) * CHUNK]
                du_scr[rows, cols] = dya[rows, cols] * s[:, c * CHUNK:(c + 1) * CHUNK]
        gbt_ref[...] += gbt
        dvn = dvn_scr[...]
        gvg_ref[...] += jnp.sum(dvn * vh, axis=0, keepdims=True)
        gvb_ref[...] += jnp.sum(dvn, axis=0, keepdims=True)
        dvh = dvn * vgv
        dv = rstd * (dvh - jnp.mean(dvh, axis=-1, keepdims=True) - vh * jnp.mean(dvh * vh, axis=-1, keepdims=True))
        duv_ref[:, :D_MODEL] = (du_scr[...] * _gelu_grad(u_pre)).astype(BF16)
        duv_ref[:, D_MODEL:] = (dv * _gelu_grad(v_pre)).astype(BF16)

    return pl.pallas_call(
        body, name=name, grid=(t // GMLP_TILE,),
        in_specs=[_row_spec(D_MODEL, 0, GMLP_TILE), _row_spec(D_MODEL, 1, GMLP_TILE), _row_spec(D_MODEL, 0, GMLP_TILE),
                  _vec_spec(D_MODEL), _vec_spec(D_MODEL), pl.BlockSpec((GROUPS, CHUNK, CHUNK), lambda i: (0, 0, 0)),
                  pl.BlockSpec((CHUNK, GROUPS), lambda i: (0, 0)), pl.BlockSpec(memory_space=pl.ANY)],
        out_specs=[_row_spec(2 * D_MODEL, 0, GMLP_TILE), pl.BlockSpec((GROUPS, CHUNK, CHUNK), lambda i: (0, 0, 0)),
                   pl.BlockSpec((CHUNK, GROUPS), lambda i: (0, 0)), _vec_spec(D_MODEL), _vec_spec(D_MODEL)],
        out_shape=[jax.ShapeDtypeStruct(dproj.shape, BF16), jax.ShapeDtypeStruct((GROUPS, CHUNK, CHUNK), F32),
                   jax.ShapeDtypeStruct((CHUNK, GROUPS), F32), jax.ShapeDtypeStruct((1, D_MODEL), F32),
                   jax.ShapeDtypeStruct((1, D_MODEL), F32)],
        scratch_shapes=[pltpu.VMEM((GMLP_TILE, D_MODEL), F32), pltpu.VMEM((GMLP_TILE, D_MODEL), F32)],
        input_output_aliases={7: 0},
        compiler_params=_params(("arbitrary",), 6 * _nbytes((GMLP_TILE, D_MODEL), F32)),
    )(proj, proj, dya, vg, vb, wsp, bsp_t, dproj)


CONV_TILE = 512
CONV_COLS = 1024
CONV_RB = 32
HALO = SUBLANES


def _conv_fwd(proj, cw, cb, name):
    t = proj.shape[0]
    nj = CONV_DIM // CONV_COLS
    xcb = COL_XBC // CONV_COLS
    rb = CONV_TILE // HALO

    def body(x_ref, prev_ref, cw_ref, cb_ref, pre_ref, xc_ref):
        i = pl.program_id(1)
        cw_v = cw_ref[...]
        cb_v = cb_ref[...]
        for b in range(CONV_TILE // CONV_RB):
            if b == 0:
                ext = jnp.concatenate([jnp.where(i > 0, prev_ref[...], 0.0), x_ref[:CONV_RB, :]], axis=0)
            else:
                ext = x_ref[b * CONV_RB - HALO:(b + 1) * CONV_RB, :]
            pre = cb_v + cw_v[CONV_WIDTH - 1:CONV_WIDTH, :] * ext[HALO:, :]
            for k in range(CONV_WIDTH - 1):
                back = CONV_WIDTH - 1 - k
                pre = pre + cw_v[k:k + 1, :] * pltpu.roll(ext, back, 0)[HALO:, :]
            pre_ref[b * CONV_RB:(b + 1) * CONV_RB, :] = pre
            xc_ref[b * CONV_RB:(b + 1) * CONV_RB, :] = pre * _sigmoid(pre)

    tile = pl.BlockSpec((CONV_TILE, CONV_COLS), lambda j, i: (i, j))
    return pl.pallas_call(
        body, name=name, grid=(nj, t // CONV_TILE),
        in_specs=[pl.BlockSpec((CONV_TILE, CONV_COLS), lambda j, i: (i, xcb + j)),
                  pl.BlockSpec((HALO, CONV_COLS), lambda j, i: (jnp.maximum(i * rb - 1, 0), xcb + j)),
                  pl.BlockSpec((CONV_WIDTH, CONV_COLS), lambda j, i: (0, j)),
                  pl.BlockSpec((1, CONV_COLS), lambda j, i: (0, j))],
        out_specs=[tile, tile],
        out_shape=[jax.ShapeDtypeStruct((t, CONV_DIM), F32), jax.ShapeDtypeStruct((t, CONV_DIM), F32)],
        compiler_params=_params(("parallel", "parallel"), 4 * _nbytes((CONV_TILE, CONV_COLS), F32)),
    )(proj, proj, cw, cb)


def _fold_rows(v):
    out = v[:SUBLANES]
    for r in range(1, v.shape[0] // SUBLANES):
        out = out + v[r * SUBLANES:(r + 1) * SUBLANES]
    return out


def _conv_bwd(proj, pre, dxc, cw, dproj, name):
    t = proj.shape[0]
    nj = CONV_DIM // CONV_COLS
    ni = t // CONV_TILE
    xcb = COL_XBC // CONV_COLS
    rb = CONV_TILE // HALO
    last_rb = t // HALO - 1

    def body(x_ref, p_ref, pnext_ref, d_ref, dnext_ref, cw_ref, dproj_in, dx_ref, gw_ref, gb_ref):
        del dproj_in
        i = pl.program_id(1)
        cw_v = cw_ref[...]

        def dpre_of(p, d):
            sg = _sigmoid(p)
            return d * sg * (1.0 + p * (1.0 - sg))

        @pl.when(i == 0)
        def _():
            gw_ref[...] = jnp.zeros_like(gw_ref)
            gb_ref[...] = jnp.zeros_like(gb_ref)

        head = dpre_of(pnext_ref[...], jnp.where(i < ni - 1, dnext_ref[...], 0.0))
        gb_acc = jnp.zeros((SUBLANES, CONV_COLS), F32)
        gw_acc = [jnp.zeros((SUBLANES, CONV_COLS), F32) for _ in range(CONV_WIDTH)]
        for b in reversed(range(CONV_TILE // CONV_RB)):
            rows = slice(b * CONV_RB, (b + 1) * CONV_RB)
            cur = dpre_of(p_ref[rows, :], d_ref[rows, :])
            ext = jnp.concatenate([cur, head], axis=0)
            xv = x_ref[rows, :]
            dx = None
            for k in range(CONV_WIDTH):
                shift = CONV_WIDTH - 1 - k
                win = cur if shift == 0 else pltpu.roll(ext, CONV_RB + HALO - shift, 0)[:CONV_RB, :]
                term = cw_v[k:k + 1, :] * win
                dx = term if dx is None else dx + term
                gw_acc[k] = gw_acc[k] + _fold_rows(win * xv)
            dx_ref[rows, :] = dx.astype(BF16)
            gb_acc = gb_acc + _fold_rows(cur)
            head = cur[:HALO]
        gb_ref[...] += jnp.sum(gb_acc, axis=0, keepdims=True)
        for k in range(CONV_WIDTH):
            gw_ref[k:k + 1, :] += jnp.sum(gw_acc[k], axis=0, keepdims=True)

    tile = pl.BlockSpec((CONV_TILE, CONV_COLS), lambda j, i: (i, j))
    after = pl.BlockSpec((HALO, CONV_COLS), lambda j, i: (jnp.minimum((i + 1) * rb, last_rb), j))
    return pl.pallas_call(
        body, name=name, grid=(nj, ni),
        in_specs=[pl.BlockSpec((CONV_TILE, CONV_COLS), lambda j, i: (i, xcb + j)), tile, after, tile, after,
                  pl.BlockSpec((CONV_WIDTH, CONV_COLS), lambda j, i: (0, j)),
                  pl.BlockSpec(memory_space=pl.ANY)],
        out_specs=[pl.BlockSpec((CONV_TILE, CONV_COLS), lambda j, i: (i, xcb + j)),
                   pl.BlockSpec((CONV_WIDTH, CONV_COLS), lambda j, i: (0, j)),
                   pl.BlockSpec((1, CONV_COLS), lambda j, i: (0, j))],
        out_shape=[jax.ShapeDtypeStruct(dproj.shape, BF16), jax.ShapeDtypeStruct((CONV_WIDTH, CONV_DIM), F32),
                   jax.ShapeDtypeStruct((1, CONV_DIM), F32)],
        input_output_aliases={6: 0},
        compiler_params=_params(("parallel", "arbitrary"), 4 * _nbytes((CONV_TILE, CONV_COLS), F32)),
    )(proj, pre, pre, dxc, dxc, cw, dproj)


def _ssd_decays(dt_raw, dtb, alog, e_bf, tril_bf):
    dtv = _softplus(dt_raw + dtb)
    a = -jnp.exp(alog)
    cs = _dot_exact_lhs(tril_bf, dtv * a, _NN)
    cs_last = cs[CHUNK - 1:CHUNK, :]
    stack = jnp.concatenate([dtv, jnp.exp(cs), jnp.exp(cs_last - cs)], axis=0)
    full = _head_expand(stack, e_bf)
    return dtv, a, cs, full[:CHUNK], full[CHUNK:2 * CHUNK], full[2 * CHUNK:]


def _split2(x):
    hi = x.astype(BF16)
    return hi, (x - hi.astype(F32)).astype(BF16)


def _head_expand(x, e_bf):
    hi, mid = _split2(x)
    return _dot(hi, e_bf, _NN) + _dot(mid, e_bf, _NN)


def _head_sums(x, e_bf):
    return _dot_exact_rhs(x, e_bf, _NT)


def _head_mats(cs, cs_t, cb, h, mask):
    seg = cs[:, h:h + 1] - cs_t[h:h + 1, :]
    lmat = jnp.exp(jnp.where(mask, seg, -jnp.inf))
    return lmat, cb * lmat


def _ssd_fwd(xc, proj, dt_raw, dtb, alog, dskip_full, ng, e_bf, name):
    t = xc.shape[0]
    nc = t // CHUNK
    zcb = COL_Z // D_INNER

    def body(xc_ref, z_ref, dt_ref, dtb_ref, alog_ref, dsk_ref, ng_ref, e_ref, y_ref, yb_ref, sprev_ref, s_scr):
        @pl.when(pl.program_id(0) == 0)
        def _():
            s_scr[...] = jnp.zeros_like(s_scr)

        mask = _tri(True)
        tril_bf = mask.astype(BF16)
        e_v = e_ref[...]
        _, _, cs, dt_full, ecs_full, decay_full = _ssd_decays(dt_ref[...], dtb_ref[...], alog_ref[...], e_v, tril_bf)
        cs_t = cs.T
        sprev_ref[0] = s_scr[...]
        for g in range(GROUPS):
            gc = slice(g * GROUP_W, (g + 1) * GROUP_W)
            xs = xc_ref[:, gc]
            xdt = xs * dt_full[:, gc]
            xdt_b = xdt.astype(BF16)
            xdec = (xdt * decay_full[:, gc]).astype(BF16)
            bg = xc_ref[:, D_INNER + g * D_STATE:D_INNER + (g + 1) * D_STATE].astype(BF16)
            cg = xc_ref[:, D_INNER + GROUPS * D_STATE + g * D_STATE:D_INNER + GROUPS * D_STATE + (g + 1) * D_STATE].astype(BF16)
            cb = _dot(cg, bg, _NT)
            s_prev = s_scr[:, gc]
            y_off = ecs_full[:, gc] * _dot(cg, s_prev.astype(BF16), _NN)
            s_scr[:, gc] = s_prev * ecs_full[CHUNK - 1:CHUNK, gc] + _dot(bg, xdec, _TN)
            parts = []
            for r in range(GROUP_W // HEAD_DIM):
                h = g * (GROUP_W // HEAD_DIM) + r
                _, m = _head_mats(cs, cs_t, cb, h, mask)
                parts.append(_dot(m.astype(BF16), xdt_b[:, r * HEAD_DIM:(r + 1) * HEAD_DIM], _NN))
            yg = jnp.concatenate(parts, axis=1) + y_off + dsk_ref[:, gc] * xs
            y_ref[:, gc] = yg
            zv = z_ref[:, gc]
            ygate = yg * (zv * _sigmoid(zv))
            rstd = lax.rsqrt(jnp.mean(ygate * ygate, axis=-1, keepdims=True) + NORM_EPS)
            yb_ref[:, gc] = (ygate * rstd * ng_ref[:, gc]).astype(BF16)

    vec = lambda w: pl.BlockSpec((1, w), lambda i: (0, 0))
    blk = _nbytes((CHUNK, CONV_DIM), F32) + 3 * _nbytes((CHUNK, D_INNER), F32) + _nbytes((D_STATE, D_INNER), F32)
    return pl.pallas_call(
        body, name=name, grid=(nc,),
        in_specs=[pl.BlockSpec((CHUNK, CONV_DIM), lambda i: (i, 0)), pl.BlockSpec((CHUNK, D_INNER), lambda i: (i, zcb)),
                  pl.BlockSpec((CHUNK, DT_PAD), lambda i: (i, 0)), vec(DT_PAD), vec(DT_PAD), vec(D_INNER), vec(D_INNER),
                  pl.BlockSpec((DT_PAD, D_INNER), lambda i: (0, 0))],
        out_specs=[pl.BlockSpec((CHUNK, D_INNER), lambda i: (i, 0)), pl.BlockSpec((CHUNK, D_INNER), lambda i: (i, 0)),
                   pl.BlockSpec((1, D_STATE, D_INNER), lambda i: (i, 0, 0))],
        out_shape=[jax.ShapeDtypeStruct((t, D_INNER), F32), jax.ShapeDtypeStruct((t, D_INNER), BF16),
                   jax.ShapeDtypeStruct((nc, D_STATE, D_INNER), F32)],
        scratch_shapes=[pltpu.VMEM((D_STATE, D_INNER), F32)],
        compiler_params=_params(("arbitrary",), blk),
    )(xc, proj, dt_raw, dtb, alog, dskip_full, ng, e_bf)


def _ssd_bwd(dyb, y, xc, proj, dt_raw, sprev, dtb, alog, dskip_full, ng, e_bf, dproj, name):
    t = xc.shape[0]
    nc = t // CHUNK
    zcb = COL_Z // D_INNER
    hpg = GROUP_W // HEAD_DIM
    rev = lambda i: nc - 1 - i

    def body(dyb_ref, y_ref, xc_ref, z_ref, dt_ref, sprev_ref, dtb_ref, alog_ref, dsk_ref, ng_ref, e_ref, dproj_in,
             dz_ref, dxc_ref, ddt_ref, gng_ref, gdsk_ref, galog_ref, gdtb_ref, ds_scr, sums_scr):
        del dproj_in

        @pl.when(pl.program_id(0) == 0)
        def _():
            ds_scr[...] = jnp.zeros_like(ds_scr)
            gng_ref[...] = jnp.zeros_like(gng_ref)
            gdsk_ref[...] = jnp.zeros_like(gdsk_ref)
            galog_ref[...] = jnp.zeros_like(galog_ref)
            gdtb_ref[...] = jnp.zeros_like(gdtb_ref)

        mask = _tri(True)
        tril_bf = mask.astype(BF16)
        triu_bf = _tri(False).astype(BF16)
        e_v = e_ref[...]
        dt_in = dt_ref[...] + dtb_ref[...]
        dtv, a, cs, dt_full, ecs_full, decay_full = _ssd_decays(dt_ref[...], dtb_ref[...], alog_ref[...], e_v, tril_bf)
        cs_t = cs.T

        lane_h = lax.broadcasted_iota(jnp.int32, (CHUNK, DT_PAD), 1)
        sub_h = lax.broadcasted_iota(jnp.int32, (DT_PAD, CHUNK), 0)
        dcs_rows = jnp.zeros((CHUNK, DT_PAD), F32)
        dcs_cols_t = jnp.zeros((DT_PAD, CHUNK), F32)
        last_cols, dsk_cols = [], []
        for g in range(GROUPS):
            gc = slice(g * GROUP_W, (g + 1) * GROUP_W)
            b_cols = slice(D_INNER + g * D_STATE, D_INNER + (g + 1) * D_STATE)
            c_cols = slice(D_INNER + GROUPS * D_STATE + g * D_STATE, D_INNER + GROUPS * D_STATE + (g + 1) * D_STATE)
            xs = xc_ref[:, gc]
            xdt = xs * dt_full[:, gc]
            xdt_b = xdt.astype(BF16)
            xdec = xdt * decay_full[:, gc]
            xdec_b = xdec.astype(BF16)
            zv = z_ref[:, gc]
            sg = _sigmoid(zv)
            gate = zv * sg
            yv = y_ref[:, gc]
            dybv = dyb_ref[:, gc]
            ygate = yv * gate
            rstd = lax.rsqrt(jnp.mean(ygate * ygate, axis=-1, keepdims=True) + NORM_EPS)
            yn = ygate * rstd
            gng_ref[:, gc] += jnp.sum(dybv * yn, axis=0, keepdims=True)
            dyn = dybv * ng_ref[:, gc]
            dyg = rstd * (dyn - yn * jnp.mean(dyn * yn, axis=-1, keepdims=True))
            dz_ref[:, gc] = (dyg * yv * sg * (1.0 + zv * (1.0 - sg))).astype(BF16)
            dy = dyg * gate
            dy_b = dy.astype(BF16)
            dyo = dy * ecs_full[:, gc]
            dyo_b = dyo.astype(BF16)
            dsk_cols.append(jnp.sum(dy * xs, axis=0, keepdims=True))

            bg = xc_ref[:, b_cols].astype(BF16)
            cg = xc_ref[:, c_cols].astype(BF16)
            s_prev = sprev_ref[0, :, gc]
            s_prev_b = s_prev.astype(BF16)
            dsg = ds_scr[:, gc]
            dsg_b = dsg.astype(BF16)
            cb = _dot(cg, bg, _NT)
            c_s = _dot(cg, s_prev_b, _NN)
            b_ds = _dot(bg, dsg_b, _NN)
            dcb = jnp.zeros((CHUNK, CHUNK), F32)
            parts = []
            for r in range(hpg):
                h = g * hpg + r
                hc = slice(r * HEAD_DIM, (r + 1) * HEAD_DIM)
                lmat, m = _head_mats(cs, cs_t, cb, h, mask)
                dm = _dot(dy_b[:, hc], xdt_b[:, hc], _NT)
                parts.append(_dot(m.astype(BF16), dy_b[:, hc], _TN))
                dcb = dcb + dm * lmat
                w = dm * m
                dcs_rows = jnp.where(lane_h == h, jnp.sum(w, axis=1, keepdims=True), dcs_rows)
                dcs_cols_t = jnp.where(sub_h == h, jnp.sum(w, axis=0, keepdims=True), dcs_cols_t)
            dxdt = jnp.concatenate(parts, axis=1) + decay_full[:, gc] * b_ds
            dcb_b = dcb.astype(BF16)
            dxc_ref[:, c_cols] = _dot(dcb_b, bg, _NN) + _dot(dyo_b, s_prev_b, _NT)
            dxc_ref[:, b_cols] = _dot(dcb_b, cg, _TN) + _dot(xdec_b, dsg_b, _NT)
            cdec = ecs_full[CHUNK - 1:CHUNK, gc]
            ds_scr[:, gc] = _dot(cg, dyo_b, _TN) + cdec * dsg
            dxc_ref[:, gc] = dxdt * dt_full[:, gc] + dsk_ref[:, gc] * dy
            dec_prod = xdec * b_ds
            sums_scr[:CHUNK, gc] = dyo * c_s - dec_prod
            sums_scr[CHUNK:, gc] = dxdt * xs
            last_cols.append(jnp.sum(dec_prod, axis=0, keepdims=True) + cdec * jnp.sum(dsg * s_prev, axis=0, keepdims=True))
        t_sums = _head_sums(sums_scr[...], e_v)
        tail = jnp.concatenate([jnp.concatenate(last_cols, axis=1), jnp.concatenate(dsk_cols, axis=1),
                                jnp.zeros((SUBLANES - 2, D_INNER), F32)], axis=0)
        t_tail = _dot_exact_rhs(tail, e_v, _NT)
        gdsk_ref[...] += t_tail[1:2, :]
        row = lax.broadcasted_iota(jnp.int32, (CHUNK, DT_PAD), 0)
        dcs = dcs_rows - dcs_cols_t.T + t_sums[:CHUNK] + jnp.where(row == CHUNK - 1, t_tail[0:1, :], 0.0)
        dda = _dot_exact_lhs(triu_bf, dcs, _NN)
        galog_ref[...] += jnp.sum(dda * dtv, axis=0, keepdims=True) * a
        ddt = dda * a + t_sums[CHUNK:]
        ddt_raw = jnp.where(lane_h < N_HEADS, ddt * _sigmoid(dt_in), 0.0)
        gdtb_ref[...] += jnp.sum(ddt_raw, axis=0, keepdims=True)
        ddt_ref[...] = ddt_raw.astype(BF16)

    vec = lambda w: pl.BlockSpec((1, w), lambda i: (0, 0))
    blk = (2 * _nbytes((CHUNK, CONV_DIM), F32) + 4 * _nbytes((CHUNK, D_INNER), F32) + 4 * _nbytes((D_STATE, D_INNER), F32))
    return pl.pallas_call(
        body, name=name, grid=(nc,),
        in_specs=[pl.BlockSpec((CHUNK, D_INNER), lambda i: (rev(i), 0)), pl.BlockSpec((CHUNK, D_INNER), lambda i: (rev(i), 0)),
                  pl.BlockSpec((CHUNK, CONV_DIM), lambda i: (rev(i), 0)), pl.BlockSpec((CHUNK, D_INNER), lambda i: (rev(i), zcb)),
                  pl.BlockSpec((CHUNK, DT_PAD), lambda i: (rev(i), 0)), pl.BlockSpec((1, D_STATE, D_INNER), lambda i: (rev(i), 0, 0)),
                  vec(DT_PAD), vec(DT_PAD), vec(D_INNER), vec(D_INNER), pl.BlockSpec((DT_PAD, D_INNER), lambda i: (0, 0)),
                  pl.BlockSpec(memory_space=pl.ANY)],
        out_specs=[pl.BlockSpec((CHUNK, D_INNER), lambda i: (rev(i), zcb)), pl.BlockSpec((CHUNK, CONV_DIM), lambda i: (rev(i), 0)),
                   pl.BlockSpec((CHUNK, DT_PAD), lambda i: (rev(i), 0)), vec(D_INNER), vec(DT_PAD), vec(DT_PAD), vec(DT_PAD)],
        out_shape=[jax.ShapeDtypeStruct(dproj.shape, BF16), jax.ShapeDtypeStruct((t, CONV_DIM), F32),
                   jax.ShapeDtypeStruct((t, DT_PAD), BF16), jax.ShapeDtypeStruct((1, D_INNER), F32),
                   jax.ShapeDtypeStruct((1, DT_PAD), F32), jax.ShapeDtypeStruct((1, DT_PAD), F32),
                   jax.ShapeDtypeStruct((1, DT_PAD), F32)],
        scratch_shapes=[pltpu.VMEM((D_STATE, D_INNER), F32), pltpu.VMEM((2 * CHUNK, D_INNER), F32)],
        input_output_aliases={11: 0},
        compiler_params=_params(("arbitrary",), blk),
    )(dyb, y, xc, proj, dt_raw, sprev, dtb, alog, dskip_full, ng, e_bf, dproj)


def _mesh_pos():
    return lax.axis_index("x"), lax.axis_index("y"), lax.axis_index("c")


def _other_chips(x, y):
    return [(1 - x, y), (x, 1 - y), (1 - x, 1 - y)]


def _all_gather(shards, name, own_only=()):
    n, n_own = len(shards), len(own_only)

    def body(*refs):
        ins, own_ins = refs[:n], refs[n:n + n_own]
        outs, own_outs = refs[n + n_own:2 * n + n_own], refs[2 * n + n_own:2 * (n + n_own)]
        send_sems, recv_sems, local_sems = refs[2 * (n + n_own):]
        x, y, c = _mesh_pos()
        me, sibling = (x, y, c), (x, y, 1 - c)
        chips = _other_chips(x, y)

        def slot(p):
            return 4 * p[0] + 2 * p[1] + p[2]

        def copy(a, k, block, to, src=None):
            dst = outs[a].at[slot(block)]
            return pltpu.make_async_remote_copy(
                src_ref=dst if src is None else src, dst_ref=dst, send_sem=send_sems.at[a * 7 + k],
                recv_sem=recv_sems.at[a * 7 + k], device_id=to, device_id_type=MESH)

        started = []
        own = []
        for a in range(n_own):
            mine = pltpu.make_async_copy(own_ins[a], own_outs[a].at[slot(me)], local_sems.at[n + a])
            mine.start()
            own.append(mine)
        for a in range(n):
            mine = pltpu.make_async_copy(ins[a], outs[a].at[slot(me)], local_sems.at[a])
            mine.start()
            own.append(mine)
            first = [copy(a, 0, me, sibling, src=ins[a])]
            first += [copy(a, 1 + j, me, (*chip, c), src=ins[a]) for j, chip in enumerate(chips)]
            for cp in first:
                cp.start()
            started += first
        for a in range(n):
            for j, chip in enumerate(chips):
                copy(a, 1 + j, (*chip, c), me).wait_recv()
                fwd = copy(a, 4 + j, (*chip, c), sibling)
                fwd.start()
                started.append(fwd)
        for a in range(n):
            copy(a, 0, sibling, me).wait_recv()
            for j, chip in enumerate(chips):
                copy(a, 4 + j, (*chip, 1 - c), me).wait_recv()
        for cp in started:
            cp.wait_send()
        for mine in own:
            mine.wait()

    return pl.pallas_call(
        body, name=name,
        in_specs=[_HBM] * (n + n_own), out_specs=[_HBM] * (n + n_own),
        out_shape=[jax.ShapeDtypeStruct((N_DEV,) + s.shape, s.dtype) for s in (*shards, *own_only)],
        scratch_shapes=[pltpu.SemaphoreType.DMA((7 * n,)), pltpu.SemaphoreType.DMA((7 * n,)),
                        pltpu.SemaphoreType.DMA((n + n_own,))],
    )(*shards, *own_only)


def _exchange_all(packed, name):
    def body(in_ref, out_ref, send_sems, recv_sems, local_sem):
        x, y, c = _mesh_pos()
        my_slot = 4 * x + 2 * y + c
        mine = pltpu.make_async_copy(in_ref, out_ref.at[my_slot], local_sem)
        mine.start()
        copies = []
        for k in range(1, N_DEV):
            fx, fy, fc = (k >> 2) & 1, (k >> 1) & 1, k & 1
            peer = (x + fx - 2 * x * fx, y + fy - 2 * y * fy, c + fc - 2 * c * fc)
            peer_slot = 4 * peer[0] + 2 * peer[1] + peer[2]
            send = pltpu.make_async_remote_copy(
                src_ref=in_ref, dst_ref=out_ref.at[my_slot], send_sem=send_sems.at[k - 1], recv_sem=recv_sems.at[k - 1],
                device_id=peer, device_id_type=MESH)
            send.start()
            recv = pltpu.make_async_remote_copy(
                src_ref=in_ref, dst_ref=out_ref.at[peer_slot], send_sem=send_sems.at[k - 1], recv_sem=recv_sems.at[k - 1],
                device_id=peer, device_id_type=MESH)
            copies.append((send, recv))
        for send, recv in copies:
            send.wait_send()
            recv.wait_recv()
        mine.wait()

    return pl.pallas_call(
        body, name=name, in_specs=[_HBM], out_specs=_HBM,
        out_shape=jax.ShapeDtypeStruct((N_DEV,) + packed.shape, packed.dtype),
        scratch_shapes=[pltpu.SemaphoreType.DMA((N_DEV - 1,)), pltpu.SemaphoreType.DMA((N_DEV - 1,)),
                        pltpu.SemaphoreType.DMA],
    )(packed)


def _swap_with_sibling(grads, name):
    n = len(grads)

    def body(*refs):
        ins, outs = refs[:n], refs[n:2 * n]
        send_sems, recv_sems = refs[2 * n:]
        x, y, c = _mesh_pos()
        copies = []
        for a in range(n):
            for k in range(N_CHIP):
                cp = pltpu.make_async_remote_copy(
                    src_ref=ins[a].at[(1 - c) + 2 * k], dst_ref=outs[a].at[k], send_sem=send_sems.at[a * N_CHIP + k],
                    recv_sem=recv_sems.at[a * N_CHIP + k], device_id=(x, y, 1 - c), device_id_type=MESH)
                cp.start()
                copies.append(cp)
        for cp in copies:
            cp.wait()

    return pl.pallas_call(
        body, name=name, in_specs=[_HBM] * n, out_specs=[_HBM] * n,
        out_shape=[jax.ShapeDtypeStruct((N_CHIP,) + g.shape[1:], g.dtype) for g in grads],
        scratch_shapes=[pltpu.SemaphoreType.DMA((N_CHIP * n,)), pltpu.SemaphoreType.DMA((N_CHIP * n,))],
    )(*grads)


_SEM = pl.BlockSpec(memory_space=pltpu.SEMAPHORE)
_IN_HBM = pl.BlockSpec(memory_space=pltpu.HBM)
_EFFECT = pltpu.SideEffectType.DATAFLOW_SIDE_EFFECTING


def _in_hbm(a):
    return pltpu.with_memory_space_constraint(a, pltpu.HBM)


def _all_peers(x, y, c):
    peers = []
    for k in range(1, N_DEV):
        fx, fy, fc = (k >> 2) & 1, (k >> 1) & 1, k & 1
        px, py, pc = x + fx - 2 * x * fx, y + fy - 2 * y * fy, c + fc - 2 * c * fc
        peers.append(((px, py, pc), 4 * px + 2 * py + pc))
    return peers


def _gather_copies(ins, lands, send_sems, recv_sems):
    x, y, c = _mesh_pos()
    my_slot = 4 * x + 2 * y + c
    pairs = []
    for a in range(len(ins)):
        for k, (peer, peer_slot) in enumerate(_all_peers(x, y, c)):
            sems = dict(send_sem=send_sems.at[a * (N_DEV - 1) + k], recv_sem=recv_sems.at[a * (N_DEV - 1) + k],
                        device_id=peer, device_id_type=MESH)
            pairs.append((pltpu.make_async_remote_copy(src_ref=ins[a], dst_ref=lands[a].at[my_slot], **sems),
                          pltpu.make_async_remote_copy(src_ref=ins[a], dst_ref=lands[a].at[peer_slot], **sems)))
    return pairs


def _scatter_copies(ins, lands, send_sems, recv_sems):
    x, y, c = _mesh_pos()
    my_chip = 2 * x + y
    pairs = []
    for a in range(len(ins)):
        for j, chip in enumerate(_other_chips(x, y)):
            there = 2 * chip[0] + chip[1]
            sems = dict(send_sem=send_sems.at[a * 3 + j], recv_sem=recv_sems.at[a * 3 + j],
                        device_id=(*chip, c), device_id_type=MESH)
            pairs.append((pltpu.make_async_remote_copy(src_ref=ins[a].at[there], dst_ref=lands[a].at[my_chip], **sems),
                          pltpu.make_async_remote_copy(src_ref=ins[a].at[my_chip], dst_ref=lands[a].at[there], **sems)))
    return pairs


def _split_start(srcs, lands, copies, per_array, name):
    n = len(srcs)

    def body(*refs):
        ins, land_refs = refs[:n], refs[n:2 * n]
        send_sems, recv_sems = refs[2 * n], refs[2 * n + 1]
        token = refs[-1]
        for send, _ in copies(ins, land_refs, send_sems, recv_sems):
            send.start()
        token[...] = jnp.zeros_like(token)

    outs = pl.pallas_call(
        body, name=name,
        out_shape=(pltpu.SemaphoreType.DMA((per_array * n,)), pltpu.SemaphoreType.DMA((per_array * n,)),
                   *[pltpu.HBM(s.shape, s.dtype) for s in srcs], *[pltpu.HBM(l.shape, l.dtype) for l in lands],
                   jax.ShapeDtypeStruct((SUBLANES, LANES), F32)),
        in_specs=[_IN_HBM] * (2 * n),
        out_specs=(_SEM, _SEM, *[_IN_HBM] * (2 * n), pl.BlockSpec(memory_space=pltpu.VMEM)),
        input_output_aliases={i: 2 + i for i in range(2 * n)},
        compiler_params=pltpu.CompilerParams(has_side_effects=_EFFECT),
    )(*[_in_hbm(s) for s in srcs], *[_in_hbm(l) for l in lands])
    return outs[0], outs[1], list(outs[2:2 + n]), list(outs[2 + n:2 + 2 * n]), outs[-1]


def _split_wait(started, copies, after, name):
    send_sems, recv_sems, srcs, lands, _ = started
    n = len(srcs)

    def body(*refs):
        ins, land_refs = refs[:n], refs[n:2 * n]
        for send, recv in copies(ins, land_refs, refs[2 * n], refs[2 * n + 1]):
            send.wait_send()
            recv.wait_recv()

    outs = pl.pallas_call(
        body, name=name,
        out_shape=(*[pltpu.HBM(s.shape, s.dtype) for s in srcs], *[pltpu.HBM(l.shape, l.dtype) for l in lands]),
        in_specs=[_IN_HBM] * (2 * n) + [_SEM, _SEM, _HBM],
        out_specs=[_IN_HBM] * (2 * n),
        input_output_aliases={i: i for i in range(2 * n)},
        compiler_params=pltpu.CompilerParams(has_side_effects=_EFFECT),
    )(*srcs, *lands, send_sems, recv_sems, after)
    return list(outs[:n]), list(outs[n:])


def _ew_block(rows, cols, slots):
    budget = 2 * 1024 * 1024
    br, bc = rows, cols
    while slots * br * bc * 4 > budget:
        if br % 2 == 0 and (br // 2) % (2 * SUBLANES) == 0:
            br //= 2
        elif bc % 2 == 0 and (bc // 2) % LANES == 0:
            bc //= 2
        else:
            break
    return br, bc


def _add_sibling(grads, recv, c_idx, name):
    _, rows, cols = grads.shape
    br, bc = _ew_block(rows, cols, 3)

    def body(c_ref, g_ref, r_ref, out_ref):
        del c_ref
        out_ref[...] = (g_ref[...].astype(F32) + r_ref[...].astype(F32)).astype(out_ref.dtype)

    grid_spec = pltpu.PrefetchScalarGridSpec(
        num_scalar_prefetch=1, grid=(N_CHIP, rows // br, cols // bc),
        in_specs=[pl.BlockSpec((1, br, bc), lambda k, i, j, c_ref: (c_ref[0] + 2 * k, i, j)),
                  pl.BlockSpec((1, br, bc), lambda k, i, j, c_ref: (k, i, j))],
        out_specs=pl.BlockSpec((1, br, bc), lambda k, i, j, c_ref: (k, i, j)))
    return pl.pallas_call(
        body, name=name, grid_spec=grid_spec, out_shape=jax.ShapeDtypeStruct((N_CHIP, rows, cols), grads.dtype),
        compiler_params=_params(("parallel", "parallel", "parallel"), 3 * _nbytes((br, bc), F32)),
    )(c_idx, grads, recv)


def _adamw(slots, w, m, v, name, own=None, own_slot=None):
    ns, rows, cols = slots.shape
    br, bc = _ew_block(rows, cols, 2 * ns + 7)
    c1 = 1.0 / (1.0 - ADAM_B1 ** ADAM_STEP)
    c2 = 1.0 / (1.0 - ADAM_B2 ** ADAM_STEP)

    def update(g, w_ref, m_ref, v_ref, g_ref, d_ref, m2_ref, v2_ref):
        m2 = ADAM_B1 * m_ref[...] + (1.0 - ADAM_B1) * g
        v2 = ADAM_B2 * v_ref[...] + (1.0 - ADAM_B2) * (g * g)
        g_ref[...] = g
        m2_ref[...] = m2
        v2_ref[...] = v2
        d_ref[...] = -ADAM_LR * ((m2 * c1) / (jnp.sqrt(v2 * c2) + ADAM_EPS) + ADAM_WD * w_ref[...])

    out_shape = [jax.ShapeDtypeStruct((rows, cols), F32)] * 4
    params = _params(("parallel", "parallel"), (2 * ns + 7) * _nbytes((br, bc), F32))
    grid = (rows // br, cols // bc)
    if own is None:
        def body(s_ref, *rest):
            g = s_ref[0].astype(F32)
            for k in range(1, ns):
                g = g + s_ref[k].astype(F32)
            update(g, *rest)

        blk = pl.BlockSpec((br, bc), lambda i, j: (i, j))
        return pl.pallas_call(
            body, name=name, grid=grid,
            in_specs=[pl.BlockSpec((ns, br, bc), lambda i, j: (0, i, j)), blk, blk, blk], out_specs=[blk] * 4,
            out_shape=out_shape, compiler_params=params,
        )(slots, w, m, v)

    def body_own(slot_ref, s_ref, o_ref, *rest):
        g = None
        for k in range(ns):
            term = jnp.where(slot_ref[0] == k, o_ref[k].astype(F32), s_ref[k].astype(F32))
            g = term if g is None else g + term
        update(g, *rest)

    blk = pl.BlockSpec((br, bc), lambda i, j, slot_ref: (i, j))
    stack = pl.BlockSpec((ns, br, bc), lambda i, j, slot_ref: (0, i, j))
    grid_spec = pltpu.PrefetchScalarGridSpec(num_scalar_prefetch=1, grid=grid, in_specs=[stack, stack, blk, blk, blk],
                                             out_specs=[blk] * 4)
    return pl.pallas_call(body_own, name=name, grid_spec=grid_spec, out_shape=out_shape, compiler_params=params,
                          )(own_slot, slots, own, w, m, v)


_SMALL = ["norm_mix_g", "conv_b", "dt_bias", "a_log", "d_skip", "ssm_norm_g", "v_norm_g", "v_norm_b", "w_spatial",
          "b_spatial", "b_gates", "norm_mlp_g", "norm_final_g"]


def _pack(arrays):
    flat = []
    for arr in arrays:
        f = arr.reshape(-1).astype(F32)
        pad = (-f.shape[0]) % LANES
        flat.append(jnp.pad(f, (0, pad)) if pad else f)
    out = jnp.concatenate(flat)
    pad = (-out.shape[0]) % (SUBLANES * LANES)
    if pad:
        out = jnp.pad(out, (0, pad))
    return out.reshape(-1, LANES)


def _unpack(packed, shapes):
    flat = packed.reshape(-1)
    out, off = [], 0
    for shape in shapes:
        size = math.prod(shape)
        out.append(flat[off:off + size].reshape(shape))
        off += size + ((-size) % LANES)
    return out


def _mm_tiles(mode, m, n, k):
    tn = min(n, 1024)
    if mode == "tn":
        return min(m, 1024), tn, min(k, 2048)
    if k <= 2048:
        return min(m, 1024), tn, k
    if k <= 4096:
        return min(m, 512), tn, k
    return min(m, 1024), tn, 2048


def _local_step(x, target, wts, small, exchange):
    t = x.shape[0]
    w_main_t, w_dt_t = wts["w_main_t"], wts["w_dt_t"]
    bsp_t = small["b_spatial"].T
    pad32 = lambda a: jnp.pad(a, ((0, 0), (0, DT_PAD - N_HEADS)))
    dtb, alog = pad32(small["dt_bias"]), pad32(small["a_log"])
    dskip_full = jnp.repeat(small["d_skip"], HEAD_DIM, axis=1)
    head_of_col = lax.broadcasted_iota(jnp.int32, (DT_PAD, D_INNER), 1) // HEAD_DIM
    e_bf = (head_of_col == lax.broadcasted_iota(jnp.int32, (DT_PAD, D_INNER), 0)).astype(BF16)

    def mm(a, b, mode, name, **kw):
        if mode == "nn":
            m, k, n = a.shape[0], a.shape[1], b.shape[1]
        elif mode == "nt":
            m, k, n = a.shape[0], a.shape[1], b.shape[0]
        else:
            m, k, n = a.shape[1], a.shape[0], b.shape[1]
        tm, tn, tk = _mm_tiles(mode, m, n, k)
        kw.setdefault("out_dtypes", (BF16,) if mode == "tn" else (F32,))
        if "extra_specs" in kw:
            kw["extra_specs"] = kw["extra_specs"](tm, tn)
        return _matmul(a, b, mode=mode, tm=tm, tn=tn, tk=tk, name=name, **kw)

    def out_tile(tm, tn):
        return (((tm, tn), lambda i, j: (i, j)),)

    h = _rms_fwd(x, small["norm_mix_g"], "rms_mix", deps=exchange.begin())
    proj = mm(h, w_main_t, "nt", "proj_main", j_outer=True)
    dt_raw = mm(h, w_dt_t, "nt", "proj_dt")
    y_a = _gmlp_fwd(proj, small["v_norm_g"], small["v_norm_b"], small["w_spatial"], bsp_t, "gmlp_fwd")
    pre_conv, xc = _conv_fwd(proj, wts["conv_w"], small["conv_b"], "conv_fwd")
    y_ssd, y_b, sprev = _ssd_fwd(xc, proj, dt_raw, dtb, alog, dskip_full, small["ssm_norm_g"], e_bf, "ssd_fwd")
    wts = {**wts, **exchange.late_weights(y_b)}
    pa = mm(y_a, wts["w_proj_a"], "nn", "proj_a")
    pb = mm(y_b, wts["w_proj_b"], "nn", "proj_b")
    merged = _merge_fwd(pa, pb, proj, small["b_gates"], "merge_fwd")

    def add_residual(acc, ex, outs):
        outs[0][...] = acc + ex[0][...]

    x1 = mm(merged, wts["w_out"], "nn", "out_proj", epilogue=add_residual, extras=(x,), extra_specs=out_tile)
    h2 = _rms_fwd(x1, small["norm_mlp_g"], "rms_mlp")

    def relu_sq(acc, ex, outs):
        outs[0][...] = acc
        r = jnp.maximum(acc, 0.0)
        outs[1][...] = (r * r).astype(BF16)

    up, act = mm(h2, wts["w_mlp_up"], "nn", "mlp_up", epilogue=relu_sq, out_dtypes=(F32, BF16), j_outer=True)
    x2 = mm(act, wts["w_mlp_down"], "nn", "mlp_down", epilogue=add_residual, extras=(x1,), extra_specs=out_tile)
    _, dx2, dx2_b, g_final, loss = _loss_head(x2, small["norm_final_g"], target, "loss_head")

    def relu_sq_bwd(acc, ex, outs):
        outs[0][...] = (acc * 2.0 * jnp.maximum(ex[0][...], 0.0)).astype(BF16)

    dup = mm(dx2_b, wts["w_mlp_down"], "nt", "d_act", epilogue=relu_sq_bwd, extras=(up,), extra_specs=out_tile,
             out_dtypes=(BF16,), j_outer=True)
    g_down = mm(act, dx2_b, "tn", "g_mlp_down")
    g_up = mm(h2, dup, "tn", "g_mlp_up")
    started = exchange.reduce("mlp", {"w_mlp_down": g_down, "w_mlp_up": g_up})
    dh2 = mm(dup, wts["w_mlp_up"], "nt", "d_h2", deps=started)
    dx1, dx1_b, g_mlp = _rms_bwd(x1, small["norm_mlp_g"], dh2, dx2, "rms_mlp_bwd")

    g_out = mm(merged, dx1_b, "tn", "g_out")
    dmerged = mm(dx1_b, wts["w_out"], "nt", "d_merged")
    dpa, dpb, dproj, g_bgates = _merge_bwd(dmerged, pa, pb, proj, small["b_gates"], "merge_bwd")
    g_pa = mm(y_a, dpa, "tn", "g_proj_a")
    g_pb = mm(y_b, dpb, "tn", "g_proj_b")
    started = exchange.reduce("proj", {"w_out": g_out, "w_proj_a": g_pa, "w_proj_b": g_pb})
    dya = mm(dpa, wts["w_proj_a"], "nt", "d_ya", deps=started)
    dyb = mm(dpb, wts["w_proj_b"], "nt", "d_yb")

    dproj, g_wsp, g_bsp_t, g_vg, g_vb = _gmlp_bwd(proj, dya, small["v_norm_g"], small["v_norm_b"], small["w_spatial"],
                                                   bsp_t, dproj, "gmlp_bwd")
    dproj, dxc, ddt, g_ng, g_dskip, g_alog, g_dtb = _ssd_bwd(dyb, y_ssd, xc, proj, dt_raw, sprev, dtb, alog, dskip_full,
                                                             small["ssm_norm_g"], e_bf, dproj, "ssd_bwd")
    dproj, g_convw, g_convb = _conv_bwd(proj, pre_conv, dxc, wts["conv_w"], dproj, "conv_bwd")

    g_main_t = mm(dproj, h, "tn", "g_in_main")
    g_dt_t = mm(ddt, h, "tn", "g_in_dt")
    started = exchange.reduce("in", {"w_in": _join_w_in(g_main_t, g_dt_t)})

    def add_dt(acc, ex, outs):
        outs[0][...] = acc + _dot(ex[0][...], ex[1][...], _NN)

    dh = mm(dproj, w_main_t, "nn", "d_h", epilogue=add_dt, extras=(ddt, w_dt_t), deps=started,
            extra_specs=lambda tm, tn: (((tm, DT_PAD), lambda i, j: (i, 0)), ((DT_PAD, D_MODEL), lambda i, j: (0, 0))))
    grad_x, _, g_mix = _rms_bwd(x, small["norm_mix_g"], dh, dx1, "rms_mix_bwd")

    grads = {
        "conv_w": g_convw,
        "norm_mix_g": g_mix, "conv_b": g_convb, "dt_bias": g_dtb[:, :N_HEADS], "a_log": g_alog[:, :N_HEADS],
        "d_skip": g_dskip[:, :N_HEADS], "ssm_norm_g": g_ng, "v_norm_g": g_vg, "v_norm_b": g_vb, "w_spatial": g_wsp,
        "b_spatial": g_bsp_t.T, "b_gates": g_bgates, "norm_mlp_g": g_mlp, "norm_final_g": g_final,
    }
    return loss, grad_x, grads


def _split_w_in(w_full_t):
    dt0 = COL_GATE
    w_main_t = jnp.concatenate([w_full_t[:dt0], w_full_t[dt0 + N_HEADS:]], axis=0)
    w_dt_t = jnp.pad(w_full_t[dt0:dt0 + N_HEADS], ((0, DT_PAD - N_HEADS), (0, 0)))
    return w_main_t, w_dt_t


def _join_w_in(g_main_t, g_dt_t):
    dt0 = COL_GATE
    return jnp.concatenate([g_main_t[:dt0], g_dt_t[:N_HEADS], g_main_t[dt0:]], axis=0)


_LATE = ["w_proj_a", "w_proj_b", "w_out", "w_mlp_up", "w_mlp_down"]
_BY_COLS = ("w_mlp_up",)


class _Exchange:
    def __init__(self, late_shards, late_lands):
        self.late_shards, self.late_lands = late_shards, late_lands
        self.c_idx = lax.axis_index("c").astype(jnp.int32).reshape(1)
        self.chip_idx = (2 * lax.axis_index("x") + lax.axis_index("y")).astype(jnp.int32).reshape(1)
        self.pending = []

    def begin(self):
        self.late = _split_start(self.late_shards, self.late_lands, _gather_copies, N_DEV - 1, "gather_late_start")
        return [self.late[-1]]

    def late_weights(self, after):
        _, lands = _split_wait(self.late, _gather_copies, after, "gather_late_wait")
        whole = {}
        for n, g in zip(_LATE, lands):
            whole[n] = jnp.transpose(g, (1, 0, 2)).reshape(g.shape[1], -1) if n in _BY_COLS else g.reshape(-1, g.shape[2])
        return whole

    def reduce(self, tag, grads):
        names = list(grads)
        by_dev = []
        for n in names:
            g = grads[n]
            if n in _BY_COLS:
                by_dev.append(jnp.transpose(g.reshape(g.shape[0], N_DEV, -1), (1, 0, 2)))
            else:
                by_dev.append(g.reshape(N_DEV, -1, g.shape[1]))
        from_sibling = _swap_with_sibling(by_dev, "reduce_cores_" + tag)
        parts = [_add_sibling(g, r, self.c_idx, "add_cores_" + n) for n, g, r in zip(names, by_dev, from_sibling)]
        lands = [lax.empty(p.shape, p.dtype) for p in parts]
        started = _split_start(parts, lands, _scatter_copies, 3, "reduce_chips_start_" + tag)
        self.pending.append((tag, names, started))
        return [started[-1]]

    def finish(self, after):
        done = {}
        for tag, names, started in self.pending:
            parts, lands = _split_wait(started, _scatter_copies, after, "reduce_chips_wait_" + tag)
            for n, land, part in zip(names, lands, parts):
                done[n] = (land, part, self.chip_idx)
        return done


def kernel(x, norm_mix_g, w_in, conv_w, conv_b, dt_bias, a_log, d_skip, ssm_norm_g, v_norm_g, v_norm_b, w_spatial, b_spatial, b_gates, w_proj_a, w_proj_b, w_out, norm_mlp_g, w_mlp_up, w_mlp_down, norm_final_g, loss_target, m_norm_mix_g, m_w_in, m_conv_w, m_conv_b, m_dt_bias, m_a_log, m_d_skip, m_ssm_norm_g, m_v_norm_g, m_v_norm_b, m_w_spatial, m_b_spatial, m_b_gates, m_w_proj_a, m_w_proj_b, m_w_out, m_norm_mlp_g, m_w_mlp_up, m_w_mlp_down, m_norm_final_g, v_norm_mix_g, v_w_in, v_conv_w, v_conv_b, v_dt_bias, v_a_log, v_d_skip, v_ssm_norm_g, v_v_norm_g, v_v_norm_b, v_w_spatial, v_b_spatial, v_b_gates, v_w_proj_a, v_w_proj_b, v_w_out, v_norm_mlp_g, v_w_mlp_up, v_w_mlp_down, v_norm_final_g):
    given = dict(locals())
    names = ["norm_mix_g", "w_in", "conv_w", "conv_b", "dt_bias", "a_log", "d_skip", "ssm_norm_g", "v_norm_g", "v_norm_b",
             "w_spatial", "b_spatial", "b_gates", "w_proj_a", "w_proj_b", "w_out", "norm_mlp_g", "w_mlp_up", "w_mlp_down",
             "norm_final_g"]
    shapes = {n: given[n].shape for n in names}
    dev = 4 * lax.axis_index("x") + 2 * lax.axis_index("y") + lax.axis_index("c")

    shard2d = {"w_in": w_in[0].T, "w_proj_a": w_proj_a[0], "w_proj_b": w_proj_b[0], "w_out": w_out[0],
               "w_mlp_up": w_mlp_up[0], "w_mlp_down": w_mlp_down[0]}
    conv_shard = conv_w.reshape(CONV_WIDTH, -1)
    late_shards = [shard2d[n].astype(BF16) for n in _LATE]
    w_in_all, conv_all, *late_lands = _all_gather([shard2d["w_in"].astype(BF16), conv_shard], "gather_first",
                                                  own_only=late_shards)
    w_main_t, w_dt_t = _split_w_in(w_in_all.reshape(-1, D_MODEL))
    wts = {"w_main_t": w_main_t, "w_dt_t": w_dt_t, "conv_w": jnp.transpose(conv_all, (1, 0, 2)).reshape(CONV_WIDTH, -1)}
    small = {"norm_mix_g": norm_mix_g, "conv_b": conv_b, "dt_bias": dt_bias, "a_log": a_log, "d_skip": d_skip,
             "ssm_norm_g": ssm_norm_g, "v_norm_g": v_norm_g, "v_norm_b": v_norm_b, "w_spatial": w_spatial[0],
             "b_spatial": b_spatial[0], "b_gates": b_gates, "norm_mlp_g": norm_mlp_g,
             "norm_final_g": norm_final_g.reshape(1, -1)}

    exchange = _Exchange(late_shards, late_lands)
    loss_part, grad_x, grads = _local_step(x[0], loss_target[0], wts, small, exchange)

    out = {}
    for n, (slots, own, own_slot) in exchange.finish(grad_x).items():
        moments = [given["m_" + n][0], given["v_" + n][0]]
        if n == "w_in":
            moments = [mom.T for mom in moments]
        res = _adamw(slots, shard2d[n], *moments, "adamw_" + n, own=own, own_slot=own_slot)
        out[n] = [(r.T if n == "w_in" else r).reshape(shapes[n]) for r in res]

    small_shapes = [shapes[n] for n in _SMALL]
    extra = [grads["conv_w"], loss_part]
    packed_g = _pack([grads[n] for n in _SMALL] + extra)
    zeros_extra = [jnp.zeros_like(e) for e in extra]
    all_g = _exchange_all(packed_g, "exchange_small")
    res = _adamw(all_g, _pack([given[n] for n in _SMALL] + zeros_extra), _pack([given["m_" + n] for n in _SMALL] + zeros_extra),
                 _pack([given["v_" + n] for n in _SMALL] + zeros_extra), "adamw_small")
    extra_shapes = [grads["conv_w"].shape, loss_part.shape]
    unpacked = [_unpack(r, small_shapes + extra_shapes) for r in res]
    for i, n in enumerate(_SMALL):
        out[n] = [u[i] for u in unpacked]
    g_conv_full, loss_all = unpacked[0][-2], unpacked[0][-1]
    width = shapes["conv_w"][-1]
    g_conv = lax.dynamic_slice(g_conv_full, (0, dev * width), (CONV_WIDTH, width))
    res = _adamw(g_conv[None], conv_shard, m_conv_w.reshape(CONV_WIDTH, -1), v_conv_w.reshape(CONV_WIDTH, -1), "adamw_conv_w")
    out["conv_w"] = [r.reshape(shapes["conv_w"]) for r in res]

    loss = loss_all[0, 0]
    return (loss, grad_x[None], *[out[n][0] for n in names], *[out[n][1] for n in names],
            *[out[n][2] for n in names], *[out[n][3] for n in names])
```

```python
import functools
import math

import jax
import jax.numpy as jnp
from jax import lax
from jax.experimental import pallas as pl
from jax.experimental.pallas import tpu as pltpu

F32 = jnp.float32
BF16 = jnp.bfloat16
MESH = pl.DeviceIdType.MESH

D_MODEL = 1024
NORM_EPS = 1e-6
CHUNK = 128
GROUPS = 8
D_INNER = 2048
HEAD_DIM = 64
N_HEADS = 32
D_STATE = 128
CONV_WIDTH = 4
CONV_DIM = 4096
D_FF = 4096
GROUP_W = D_INNER // GROUPS
N_DEV = 8
N_CHIP = 4

ADAM_LR = 0.001
ADAM_B1 = 0.9
ADAM_B2 = 0.999
ADAM_EPS = 1e-08
ADAM_WD = 0.01
ADAM_STEP = 10

MAIN_W = 2 * D_MODEL + D_INNER + CONV_DIM + 2 * D_MODEL
COL_Z = 2048
COL_XBC = 4096
COL_GATE = 8192
DT_PAD = 128

LANES = 128
SUBLANES = 8
VMEM_BYTES_V7X = 64 * 1024 * 1024
VMEM_BODY_TEMP = 24 * 1024 * 1024


def _vmem_limit(block_bytes):
    return int(min(2 * block_bytes + VMEM_BODY_TEMP, VMEM_BYTES_V7X - 8 * 1024 * 1024))


def _nbytes(shape, dtype):
    return math.prod(shape) * jnp.dtype(dtype).itemsize


_HBM = pl.BlockSpec(memory_space=pl.ANY)


def _params(sem, block_bytes):
    return pltpu.CompilerParams(dimension_semantics=sem, vmem_limit_bytes=_vmem_limit(block_bytes))


def _sigmoid(x):
    return 1.0 / (1.0 + jnp.exp(-x))


def _softplus(x):
    e = jnp.exp(-jnp.abs(x))
    u = 1.0 + e
    log1p_e = jnp.where(u == 1.0, e, jnp.log(u) * (e / jnp.where(u == 1.0, 1.0, u - 1.0)))
    return jnp.maximum(x, 0.0) + log1p_e


_SQRT_HALF = 0.7071067811865476
_INV_SQRT_2PI = 0.3989422804014327


def _gelu(x):
    return x * (lax.erf(x * _SQRT_HALF) + 1.0) * 0.5


def _gelu_grad(x):
    return 0.5 * (1.0 + lax.erf(x * _SQRT_HALF)) + x * jnp.exp(-0.5 * x * x) * _INV_SQRT_2PI


def _dot(a, b, dims):
    return lax.dot_general(a, b, (dims, ((), ())), preferred_element_type=F32)


_NN = ((1,), (0,))
_NT = ((1,), (1,))
_TN = ((0,), (0,))


def _split3(x):
    hi = x.astype(BF16)
    r1 = x - hi.astype(F32)
    mid = r1.astype(BF16)
    lo = (r1 - mid.astype(F32)).astype(BF16)
    return hi, mid, lo


def _dot_exact_rhs(x, e, dims):
    hi, mid, lo = _split3(x)
    return _dot(hi, e, dims) + _dot(mid, e, dims) + _dot(lo, e, dims)


def _dot_exact_lhs(e, x, dims):
    hi, mid, lo = _split3(x)
    return _dot(e, hi, dims) + _dot(e, mid, dims) + _dot(e, lo, dims)


def _tri(lower):
    r = lax.broadcasted_iota(jnp.int32, (CHUNK, CHUNK), 0)
    c = lax.broadcasted_iota(jnp.int32, (CHUNK, CHUNK), 1)
    return (r >= c) if lower else (r <= c)


def _matmul(a, b, *, mode, tm, tn, tk, out_dtypes, name, epilogue=None, extras=(), extra_specs=(), j_outer=False, deps=()):
    if mode == "nn":
        (m, k), (_, n) = a.shape, b.shape
    elif mode == "nt":
        (m, k), (n, _) = a.shape, b.shape
    else:
        (k, m), (_, n) = a.shape, b.shape
    assert m % tm == 0 and n % tn == 0 and k % tk == 0, (name, m, n, k, tm, tn, tk)
    nk = k // tk
    n_extra, n_out = len(extras), len(out_dtypes)
    first_out = 2 + n_extra + len(deps)
    dims = {"nn": _NN, "nt": _NT, "tn": _TN}[mode]
    if epilogue is None:
        def epilogue(acc, ex, outs):
            outs[0][...] = acc.astype(outs[0].dtype)

    def body(*refs):
        a_ref, b_ref = refs[0], refs[1]
        ex_refs = refs[2:2 + n_extra]
        outs = refs[first_out:first_out + n_out]
        p = _dot(a_ref[...], b_ref[...], dims)
        if nk == 1:
            epilogue(p, ex_refs, outs)
        else:
            acc_ref = refs[first_out + n_out]
            kk = pl.program_id(2)

            @pl.when(kk == 0)
            def _():
                acc_ref[...] = p

            @pl.when(kk > 0)
            def _():
                acc_ref[...] += p

            @pl.when(kk == nk - 1)
            def _():
                epilogue(acc_ref[...], ex_refs, outs)

    if j_outer:
        grid = (n // tn, m // tm, nk)
        ij = lambda g0, g1: (g1, g0)
    else:
        grid = (m // tm, n // tn, nk)
        ij = lambda g0, g1: (g0, g1)

    def wrap(fn):
        return lambda g0, g1, kk: fn(*ij(g0, g1), kk)

    if mode == "nn":
        a_spec = pl.BlockSpec((tm, tk), wrap(lambda i, j, kk: (i, kk)))
        b_spec = pl.BlockSpec((tk, tn), wrap(lambda i, j, kk: (kk, j)))
        a_blk, b_blk = (tm, tk), (tk, tn)
    elif mode == "nt":
        a_spec = pl.BlockSpec((tm, tk), wrap(lambda i, j, kk: (i, kk)))
        b_spec = pl.BlockSpec((tn, tk), wrap(lambda i, j, kk: (j, kk)))
        a_blk, b_blk = (tm, tk), (tn, tk)
    else:
        a_spec = pl.BlockSpec((tk, tm), wrap(lambda i, j, kk: (kk, i)))
        b_spec = pl.BlockSpec((tk, tn), wrap(lambda i, j, kk: (kk, j)))
        a_blk, b_blk = (tk, tm), (tk, tn)
    ex_specs = [pl.BlockSpec(shape, wrap(lambda i, j, kk, f=f: f(i, j))) for shape, f in extra_specs]
    out_spec = [pl.BlockSpec((tm, tn), wrap(lambda i, j, kk: (i, j))) for _ in out_dtypes]
    out_shape = [jax.ShapeDtypeStruct((m, n), dt) for dt in out_dtypes]
    blk = (_nbytes(a_blk, a.dtype) + _nbytes(b_blk, b.dtype) + sum(_nbytes(s, F32) for s, _ in extra_specs)
           + sum(_nbytes((tm, tn), dt) for dt in out_dtypes) + _nbytes((tm, tn), F32))
    res = pl.pallas_call(
        body, name=name, grid=grid,
        in_specs=[a_spec, b_spec] + ex_specs + [_HBM] * len(deps), out_specs=out_spec, out_shape=out_shape,
        scratch_shapes=[pltpu.VMEM((tm, tn), F32)] if nk > 1 else [],
        compiler_params=_params(("parallel", "parallel", "arbitrary"), blk),
    )(a, b, *extras, *deps)
    return res[0] if n_out == 1 else res


ROW_TILE = 256


def _row_spec(width, col_block=0, tile=ROW_TILE):
    return pl.BlockSpec((tile, width), lambda i, cb=col_block: (i, cb))


def _vec_spec(width, col_block=0):
    return pl.BlockSpec((1, width), lambda i, cb=col_block: (0, cb))


def _rms_fwd(x, g, name, deps=()):
    t = x.shape[0]

    def body(x_ref, g_ref, *rest):
        h_ref = rest[-1]
        xv = x_ref[...]
        r = lax.rsqrt(jnp.mean(xv * xv, axis=-1, keepdims=True) + NORM_EPS)
        h_ref[...] = (xv * r * g_ref[...]).astype(BF16)

    return pl.pallas_call(
        body, name=name, grid=(t // ROW_TILE,),
        in_specs=[_row_spec(D_MODEL), _vec_spec(D_MODEL)] + [_HBM] * len(deps), out_specs=_row_spec(D_MODEL),
        out_shape=jax.ShapeDtypeStruct((t, D_MODEL), BF16),
        compiler_params=_params(("parallel",), 3 * _nbytes((ROW_TILE, D_MODEL), F32)),
    )(x, g, *deps)


def _rms_bwd(x, g, dh, dres, name):
    t = x.shape[0]

    def body(x_ref, g_ref, dh_ref, dres_ref, dx_ref, dxb_ref, gg_ref):
        xv = x_ref[...]
        r = lax.rsqrt(jnp.mean(xv * xv, axis=-1, keepdims=True) + NORM_EPS)
        xh = xv * r
        dhv = dh_ref[...]
        dyg = dhv * g_ref[...]
        dx = r * (dyg - xh * jnp.mean(dyg * xh, axis=-1, keepdims=True)) + dres_ref[...]
        dx_ref[...] = dx
        dxb_ref[...] = dx.astype(BF16)

        @pl.when(pl.program_id(0) == 0)
        def _():
            gg_ref[...] = jnp.zeros_like(gg_ref)

        gg_ref[...] += jnp.sum(dhv * xh, axis=0, keepdims=True)

    return pl.pallas_call(
        body, name=name, grid=(t // ROW_TILE,),
        in_specs=[_row_spec(D_MODEL), _vec_spec(D_MODEL), _row_spec(D_MODEL), _row_spec(D_MODEL)],
        out_specs=[_row_spec(D_MODEL), _row_spec(D_MODEL), _vec_spec(D_MODEL)],
        out_shape=[jax.ShapeDtypeStruct((t, D_MODEL), F32), jax.ShapeDtypeStruct((t, D_MODEL), BF16),
                   jax.ShapeDtypeStruct((1, D_MODEL), F32)],
        compiler_params=_params(("arbitrary",), 5 * _nbytes((ROW_TILE, D_MODEL), F32)),
    )(x, g, dh, dres)


def _loss_head(x2, gf, target, name):
    t = x2.shape[0]

    def body(x_ref, g_ref, t_ref, loss_ref, dx_ref, dxb_ref, gg_ref, tot_ref):
        xv = x_ref[...]
        gv = g_ref[...]
        r = lax.rsqrt(jnp.mean(xv * xv, axis=-1, keepdims=True) + NORM_EPS)
        xh = xv * r
        err = xh * gv - t_ref[...]
        dy = err * (1.0 / D_MODEL)
        dyg = dy * gv
        dx = r * (dyg - xh * jnp.mean(dyg * xh, axis=-1, keepdims=True))
        dx_ref[...] = dx
        dxb_ref[...] = dx.astype(BF16)

        @pl.when(pl.program_id(0) == 0)
        def _():
            gg_ref[...] = jnp.zeros_like(gg_ref)
            loss_ref[...] = jnp.zeros_like(loss_ref)

        gg_ref[...] += jnp.sum(dy * xh, axis=0, keepdims=True)
        loss_ref[...] += jnp.sum(err * err, axis=0, keepdims=True)
        tot_ref[...] = jnp.broadcast_to(jnp.sum(loss_ref[...], axis=1, keepdims=True) * (0.5 / D_MODEL), tot_ref.shape)

    return pl.pallas_call(
        body, name=name, grid=(t // ROW_TILE,),
        in_specs=[_row_spec(D_MODEL), _vec_spec(D_MODEL), _row_spec(D_MODEL)],
        out_specs=[_vec_spec(D_MODEL), _row_spec(D_MODEL), _row_spec(D_MODEL), _vec_spec(D_MODEL), _vec_spec(LANES)],
        out_shape=[jax.ShapeDtypeStruct((1, D_MODEL), F32), jax.ShapeDtypeStruct((t, D_MODEL), F32),
                   jax.ShapeDtypeStruct((t, D_MODEL), BF16), jax.ShapeDtypeStruct((1, D_MODEL), F32),
                   jax.ShapeDtypeStruct((1, LANES), F32)],
        compiler_params=_params(("arbitrary",), 5 * _nbytes((ROW_TILE, D_MODEL), F32)),
    )(x2, gf, target)


def _merge_fwd(pa, pb, proj, b_gates, name):
    t = pa.shape[0]
    gcb = COL_GATE // D_MODEL

    def body(pa_ref, pb_ref, la_ref, lb_ref, ba_ref, bb_ref, out_ref):
        ga = _sigmoid(la_ref[...] + ba_ref[...])
        gb = _sigmoid(lb_ref[...] + bb_ref[...])
        out_ref[...] = (ga * pa_ref[...] + gb * pb_ref[...]).astype(BF16)

    return pl.pallas_call(
        body, name=name, grid=(t // ROW_TILE,),
        in_specs=[_row_spec(D_MODEL), _row_spec(D_MODEL), _row_spec(D_MODEL, gcb), _row_spec(D_MODEL, gcb + 1),
                  _vec_spec(D_MODEL, 0), _vec_spec(D_MODEL, 1)],
        out_specs=_row_spec(D_MODEL),
        out_shape=jax.ShapeDtypeStruct((t, D_MODEL), BF16),
        compiler_params=_params(("parallel",), 5 * _nbytes((ROW_TILE, D_MODEL), F32)),
    )(pa, pb, proj, proj, b_gates, b_gates)


def _merge_bwd(dmerged, pa, pb, proj, b_gates, name):
    t = pa.shape[0]
    gcb = COL_GATE // D_MODEL

    def body(dm_ref, pa_ref, pb_ref, la_ref, lb_ref, ba_ref, bb_ref, dpa_ref, dpb_ref, dgl_ref, gb_ref):
        dm = dm_ref[...]
        ga = _sigmoid(la_ref[...] + ba_ref[...])
        gb = _sigmoid(lb_ref[...] + bb_ref[...])
        dpa_ref[...] = (dm * ga).astype(BF16)
        dpb_ref[...] = (dm * gb).astype(BF16)
        dla = dm * pa_ref[...] * ga * (1.0 - ga)
        dlb = dm * pb_ref[...] * gb * (1.0 - gb)
        dgl_ref[:, :D_MODEL] = dla.astype(BF16)
        dgl_ref[:, D_MODEL:] = dlb.astype(BF16)

        @pl.when(pl.program_id(0) == 0)
        def _():
            gb_ref[...] = jnp.zeros_like(gb_ref)

        gb_ref[:, :D_MODEL] += jnp.sum(dla, axis=0, keepdims=True)
        gb_ref[:, D_MODEL:] += jnp.sum(dlb, axis=0, keepdims=True)

    return pl.pallas_call(
        body, name=name, grid=(t // ROW_TILE,),
        in_specs=[_row_spec(D_MODEL), _row_spec(D_MODEL), _row_spec(D_MODEL), _row_spec(D_MODEL, gcb),
                  _row_spec(D_MODEL, gcb + 1), _vec_spec(D_MODEL, 0), _vec_spec(D_MODEL, 1)],
        out_specs=[_row_spec(D_MODEL), _row_spec(D_MODEL), _row_spec(2 * D_MODEL, COL_GATE // (2 * D_MODEL)),
                   _vec_spec(2 * D_MODEL)],
        out_shape=[jax.ShapeDtypeStruct((t, D_MODEL), BF16), jax.ShapeDtypeStruct((t, D_MODEL), BF16),
                   jax.ShapeDtypeStruct((t, MAIN_W), BF16), jax.ShapeDtypeStruct((1, 2 * D_MODEL), F32)],
        compiler_params=_params(("arbitrary",), 8 * _nbytes((ROW_TILE, D_MODEL), F32)),
    )(dmerged, pa, pb, proj, proj, b_gates, b_gates)


GMLP_TILE = 512
GMLP_NC = GMLP_TILE // CHUNK


def _gmlp_common(u_pre, v_pre, vg, vb):
    u = _gelu(u_pre)
    v = _gelu(v_pre)
    mu = jnp.mean(v, axis=-1, keepdims=True)
    vc = v - mu
    rstd = lax.rsqrt(jnp.mean(vc * vc, axis=-1, keepdims=True) + NORM_EPS)
    vh = vc * rstd
    vn = vh * vg + vb
    return u, vh, vn, rstd


def _chunks_to_lanes(x, g):
    return jnp.concatenate([x[c * CHUNK:(c + 1) * CHUNK, g * CHUNK:(g + 1) * CHUNK] for c in range(GMLP_NC)], axis=1)


def _gmlp_fwd(proj, vg, vb, wsp, bsp_t, name):
    t = proj.shape[0]

    def body(u_ref, v_ref, vg_ref, vb_ref, w_ref, b_ref, ya_ref):
        u, _, vn, _ = _gmlp_common(u_ref[...], v_ref[...], vg_ref[...], vb_ref[...])
        mask = _tri(True)
        bt = b_ref[...]
        for g in range(GROUPS):
            w = jnp.where(mask, w_ref[g], 0.0).astype(BF16)
            vcat = _chunks_to_lanes(vn, g).astype(BF16)
            s = _dot(w, vcat, _NN) + bt[:, g:g + 1]
            for c in range(GMLP_NC):
                rows, cols = slice(c * CHUNK, (c + 1) * CHUNK), slice(g * CHUNK, (g + 1) * CHUNK)
                ya_ref[rows, cols] = (u[rows, cols] * s[:, c * CHUNK:(c + 1) * CHUNK]).astype(BF16)

    return pl.pallas_call(
        body, name=name, grid=(t // GMLP_TILE,),
        in_specs=[_row_spec(D_MODEL, 0, GMLP_TILE), _row_spec(D_MODEL, 1, GMLP_TILE), _vec_spec(D_MODEL),
                  _vec_spec(D_MODEL), pl.BlockSpec((GROUPS, CHUNK, CHUNK), lambda i: (0, 0, 0)),
                  pl.BlockSpec((CHUNK, GROUPS), lambda i: (0, 0))],
        out_specs=_row_spec(D_MODEL, 0, GMLP_TILE),
        out_shape=jax.ShapeDtypeStruct((t, D_MODEL), BF16),
        compiler_params=_params(("parallel",), 3 * _nbytes((GMLP_TILE, D_MODEL), F32)),
    )(proj, proj, vg, vb, wsp, bsp_t)


def _gmlp_bwd(proj, dya, vg, vb, wsp, bsp_t, dproj, name):
    t = proj.shape[0]

    def body(u_ref, v_ref, dya_ref, vg_ref, vb_ref, w_ref, b_ref, dproj_in, duv_ref, gw_ref, gbt_ref, gvg_ref, gvb_ref,
             dvn_scr, du_scr):
        del dproj_in
        u_pre, v_pre = u_ref[...], v_ref[...]
        vgv = vg_ref[...]
        u, vh, vn, rstd = _gmlp_common(u_pre, v_pre, vgv, vb_ref[...])
        dya = dya_ref[...]
        mask = _tri(True)
        bt = b_ref[...]
        first = pl.program_id(0) == 0

        @pl.when(first)
        def _():
            gw_ref[...] = jnp.zeros_like(gw_ref)
            gbt_ref[...] = jnp.zeros_like(gbt_ref)
            gvg_ref[...] = jnp.zeros_like(gvg_ref)
            gvb_ref[...] = jnp.zeros_like(gvb_ref)

        lane = lax.broadcasted_iota(jnp.int32, (CHUNK, GROUPS), 1)
        gbt = jnp.zeros((CHUNK, GROUPS), F32)
        for g in range(GROUPS):
            w = jnp.where(mask, w_ref[g], 0.0).astype(BF16)
            vcat = _chunks_to_lanes(vn, g).astype(BF16)
            s = _dot(w, vcat, _NN) + bt[:, g:g + 1]
            ds = _chunks_to_lanes(dya * u, g)
            gbt = jnp.where(lane == g, jnp.sum(ds, axis=1, keepdims=True), gbt)
            dsb = ds.astype(BF16)
            gw_ref[g] += jnp.where(mask, _dot(dsb, vcat, _NT), 0.0)
            dv = _dot(w, dsb, _TN)
            for c in range(GMLP_NC):
                rows, cols = slice(c * CHUNK, (c + 1) * CHUNK), slice(g * CHUNK, (g + 1) * CHUNK)
                dvn_scr[rows, cols] = dv[:, c * CHUNK:(c + 1) * CHUNK]
                du_scr[rows, cols] = dya[rows, cols] * s[:, c * CHUNK:(c + 1) * CHUNK]
        gbt_ref[...] += gbt
        dvn = dvn_scr[...]
        gvg_ref[...] += jnp.sum(dvn * vh, axis=0, keepdims=True)
        gvb_ref[...] += jnp.sum(dvn, axis=0, keepdims=True)
        dvh = dvn * vgv
        dv = rstd * (dvh - jnp.mean(dvh, axis=-1, keepdims=True) - vh * jnp.mean(dvh * vh, axis=-1, keepdims=True))
        duv_ref[:, :D_MODEL] = (du_scr[...] * _gelu_grad(u_pre)).astype(BF16)
        duv_ref[:, D_MODEL:] = (dv * _gelu_grad(v_pre)).astype(BF16)

    return pl.pallas_call(
        body, name=name, grid=(t // GMLP_TILE,),
        in_specs=[_row_spec(D_MODEL, 0, GMLP_TILE), _row_spec(D_MODEL, 1, GMLP_TILE), _row_spec(D_MODEL, 0, GMLP_TILE),
                  _vec_spec(D_MODEL), _vec_spec(D_MODEL), pl.BlockSpec((GROUPS, CHUNK, CHUNK), lambda i: (0, 0, 0)),
                  pl.BlockSpec((CHUNK, GROUPS), lambda i: (0, 0)), pl.BlockSpec(memory_space=pl.ANY)],
        out_specs=[_row_spec(2 * D_MODEL, 0, GMLP_TILE), pl.BlockSpec((GROUPS, CHUNK, CHUNK), lambda i: (0, 0, 0)),
                   pl.BlockSpec((CHUNK, GROUPS), lambda i: (0, 0)), _vec_spec(D_MODEL), _vec_spec(D_MODEL)],
        out_shape=[jax.ShapeDtypeStruct(dproj.shape, BF16), jax.ShapeDtypeStruct((GROUPS, CHUNK, CHUNK), F32),
                   jax.ShapeDtypeStruct((CHUNK, GROUPS), F32), jax.ShapeDtypeStruct((1, D_MODEL), F32),
                   jax.ShapeDtypeStruct((1, D_MODEL), F32)],
        scratch_shapes=[pltpu.VMEM((GMLP_TILE, D_MODEL), F32), pltpu.VMEM((GMLP_TILE, D_MODEL), F32)],
        input_output_aliases={7: 0},
        compiler_params=_params(("arbitrary",), 6 * _nbytes((GMLP_TILE, D_MODEL), F32)),
    )(proj, proj, dya, vg, vb, wsp, bsp_t, dproj)


CONV_TILE = 512
CONV_COLS = 1024
CONV_RB = 32
HALO = SUBLANES


def _conv_fwd(proj, cw, cb, name):
    t = proj.shape[0]
    nj = CONV_DIM // CONV_COLS
    xcb = COL_XBC // CONV_COLS
    rb = CONV_TILE // HALO

    def body(x_ref, prev_ref, cw_ref, cb_ref, pre_ref, xc_ref):
        i = pl.program_id(1)
        cw_v = cw_ref[...]
        cb_v = cb_ref[...]
        for b in range(CONV_TILE // CONV_RB):
            if b == 0:
                ext = jnp.concatenate([jnp.where(i > 0, prev_ref[...], 0.0), x_ref[:CONV_RB, :]], axis=0)
            else:
                ext = x_ref[b * CONV_RB - HALO:(b + 1) * CONV_RB, :]
            pre = cb_v + cw_v[CONV_WIDTH - 1:CONV_WIDTH, :] * ext[HALO:, :]
            for k in range(CONV_WIDTH - 1):
                back = CONV_WIDTH - 1 - k
                pre = pre + cw_v[k:k + 1, :] * pltpu.roll(ext, back, 0)[HALO:, :]
            pre_ref[b * CONV_RB:(b + 1) * CONV_RB, :] = pre
            xc_ref[b * CONV_RB:(b + 1) * CONV_RB, :] = pre * _sigmoid(pre)

    tile = pl.BlockSpec((CONV_TILE, CONV_COLS), lambda j, i: (i, j))
    return pl.pallas_call(
        body, name=name, grid=(nj, t // CONV_TILE),
        in_specs=[pl.BlockSpec((CONV_TILE, CONV_COLS), lambda j, i: (i, xcb + j)),
                  pl.BlockSpec((HALO, CONV_COLS), lambda j, i: (jnp.maximum(i * rb - 1, 0), xcb + j)),
                  pl.BlockSpec((CONV_WIDTH, CONV_COLS), lambda j, i: (0, j)),
                  pl.BlockSpec((1, CONV_COLS), lambda j, i: (0, j))],
        out_specs=[tile, tile],
        out_shape=[jax.ShapeDtypeStruct((t, CONV_DIM), F32), jax.ShapeDtypeStruct((t, CONV_DIM), F32)],
        compiler_params=_params(("parallel", "parallel"), 4 * _nbytes((CONV_TILE, CONV_COLS), F32)),
    )(proj, proj, cw, cb)


def _fold_rows(v):
    out = v[:SUBLANES]
    for r in range(1, v.shape[0] // SUBLANES):
        out = out + v[r * SUBLANES:(r + 1) * SUBLANES]
    return out


def _conv_bwd(proj, pre, dxc, cw, dproj, name):
    t = proj.shape[0]
    nj = CONV_DIM // CONV_COLS
    ni = t // CONV_TILE
    xcb = COL_XBC // CONV_COLS
    rb = CONV_TILE // HALO
    last_rb = t // HALO - 1

    def body(x_ref, p_ref, pnext_ref, d_ref, dnext_ref, cw_ref, dproj_in, dx_ref, gw_ref, gb_ref):
        del dproj_in
        i = pl.program_id(1)
        cw_v = cw_ref[...]

        def dpre_of(p, d):
            sg = _sigmoid(p)
            return d * sg * (1.0 + p * (1.0 - sg))

        @pl.when(i == 0)
        def _():
            gw_ref[...] = jnp.zeros_like(gw_ref)
            gb_ref[...] = jnp.zeros_like(gb_ref)

        head = dpre_of(pnext_ref[...], jnp.where(i < ni - 1, dnext_ref[...], 0.0))
        gb_acc = jnp.zeros((SUBLANES, CONV_COLS), F32)
        gw_acc = [jnp.zeros((SUBLANES, CONV_COLS), F32) for _ in range(CONV_WIDTH)]
        for b in reversed(range(CONV_TILE // CONV_RB)):
            rows = slice(b * CONV_RB, (b + 1) * CONV_RB)
            cur = dpre_of(p_ref[rows, :], d_ref[rows, :])
            ext = jnp.concatenate([cur, head], axis=0)
            xv = x_ref[rows, :]
            dx = None
            for k in range(CONV_WIDTH):
                shift = CONV_WIDTH - 1 - k
                win = cur if shift == 0 else pltpu.roll(ext, CONV_RB + HALO - shift, 0)[:CONV_RB, :]
                term = cw_v[k:k + 1, :] * win
                dx = term if dx is None else dx + term
                gw_acc[k] = gw_acc[k] + _fold_rows(win * xv)
            dx_ref[rows, :] = dx.astype(BF16)
            gb_acc = gb_acc + _fold_rows(cur)
            head = cur[:HALO]
        gb_ref[...] += jnp.sum(gb_acc, axis=0, keepdims=True)
        for k in range(CONV_WIDTH):
            gw_ref[k:k + 1, :] += jnp.sum(gw_acc[k], axis=0, keepdims=True)

    tile = pl.BlockSpec((CONV_TILE, CONV_COLS), lambda j, i: (i, j))
    after = pl.BlockSpec((HALO, CONV_COLS), lambda j, i: (jnp.minimum((i + 1) * rb, last_rb), j))
    return pl.pallas_call(
        body, name=name, grid=(nj, ni),
        in_specs=[pl.BlockSpec((CONV_TILE, CONV_COLS), lambda j, i: (i, xcb + j)), tile, after, tile, after,
                  pl.BlockSpec((CONV_WIDTH, CONV_COLS), lambda j, i: (0, j)),
                  pl.BlockSpec(memory_space=pl.ANY)],
        out_specs=[pl.BlockSpec((CONV_TILE, CONV_COLS), lambda j, i: (i, xcb + j)),
                   pl.BlockSpec((CONV_WIDTH, CONV_COLS), lambda j, i: (0, j)),
                   pl.BlockSpec((1, CONV_COLS), lambda j, i: (0, j))],
        out_shape=[jax.ShapeDtypeStruct(dproj.shape, BF16), jax.ShapeDtypeStruct((CONV_WIDTH, CONV_DIM), F32),
                   jax.ShapeDtypeStruct((1, CONV_DIM), F32)],
        input_output_aliases={6: 0},
        compiler_params=_params(("parallel", "arbitrary"), 4 * _nbytes((CONV_TILE, CONV_COLS), F32)),
    )(proj, pre, pre, dxc, dxc, cw, dproj)


def _ssd_decays(dt_raw, dtb, alog, e_bf, tril_bf):
    dtv = _softplus(dt_raw + dtb)
    a = -jnp.exp(alog)
    cs = _dot_exact_lhs(tril_bf, dtv * a, _NN)
    cs_last = cs[CHUNK - 1:CHUNK, :]
    stack = jnp.concatenate([dtv, jnp.exp(cs), jnp.exp(cs_last - cs)], axis=0)
    full = _head_expand(stack, e_bf)
    return dtv, a, cs, full[:CHUNK], full[CHUNK:2 * CHUNK], full[2 * CHUNK:]


def _split2(x):
    hi = x.astype(BF16)
    return hi, (x - hi.astype(F32)).astype(BF16)


def _head_expand(x, e_bf):
    hi, mid = _split2(x)
    return _dot(hi, e_bf, _NN) + _dot(mid, e_bf, _NN)


def _head_sums(x, e_bf):
    return _dot_exact_rhs(x, e_bf, _NT)


def _head_mats(cs, cs_t, cb, h, mask):
    seg = cs[:, h:h + 1] - cs_t[h:h + 1, :]
    lmat = jnp.exp(jnp.where(mask, seg, -jnp.inf))
    return lmat, cb * lmat


def _ssd_fwd(xc, proj, dt_raw, dtb, alog, dskip_full, ng, e_bf, name):
    t = xc.shape[0]
    nc = t // CHUNK
    zcb = COL_Z // D_INNER

    def body(xc_ref, z_ref, dt_ref, dtb_ref, alog_ref, dsk_ref, ng_ref, e_ref, y_ref, yb_ref, sprev_ref, s_scr):
        @pl.when(pl.program_id(0) == 0)
        def _():
            s_scr[...] = jnp.zeros_like(s_scr)

        mask = _tri(True)
        tril_bf = mask.astype(BF16)
        e_v = e_ref[...]
        _, _, cs, dt_full, ecs_full, decay_full = _ssd_decays(dt_ref[...], dtb_ref[...], alog_ref[...], e_v, tril_bf)
        cs_t = cs.T
        sprev_ref[0] = s_scr[...]
        for g in range(GROUPS):
            gc = slice(g * GROUP_W, (g + 1) * GROUP_W)
            xs = xc_ref[:, gc]
            xdt = xs * dt_full[:, gc]
            xdt_b = xdt.astype(BF16)
            xdec = (xdt * decay_full[:, gc]).astype(BF16)
            bg = xc_ref[:, D_INNER + g * D_STATE:D_INNER + (g + 1) * D_STATE].astype(BF16)
            cg = xc_ref[:, D_INNER + GROUPS * D_STATE + g * D_STATE:D_INNER + GROUPS * D_STATE + (g + 1) * D_STATE].astype(BF16)
            cb = _dot(cg, bg, _NT)
            s_prev = s_scr[:, gc]
            y_off = ecs_full[:, gc] * _dot(cg, s_prev.astype(BF16), _NN)
            s_scr[:, gc] = s_prev * ecs_full[CHUNK - 1:CHUNK, gc] + _dot(bg, xdec, _TN)
            parts = []
            for r in range(GROUP_W // HEAD_DIM):
                h = g * (GROUP_W // HEAD_DIM) + r
                _, m = _head_mats(cs, cs_t, cb, h, mask)
                parts.append(_dot(m.astype(BF16), xdt_b[:, r * HEAD_DIM:(r + 1) * HEAD_DIM], _NN))
            yg = jnp.concatenate(parts, axis=1) + y_off + dsk_ref[:, gc] * xs
            y_ref[:, gc] = yg
            zv = z_ref[:, gc]
            ygate = yg * (zv * _sigmoid(zv))
            rstd = lax.rsqrt(jnp.mean(ygate * ygate, axis=-1, keepdims=True) + NORM_EPS)
            yb_ref[:, gc] = (ygate * rstd * ng_ref[:, gc]).astype(BF16)

    vec = lambda w: pl.BlockSpec((1, w), lambda i: (0, 0))
    blk = _nbytes((CHUNK, CONV_DIM), F32) + 3 * _nbytes((CHUNK, D_INNER), F32) + _nbytes((D_STATE, D_INNER), F32)
    return pl.pallas_call(
        body, name=name, grid=(nc,),
        in_specs=[pl.BlockSpec((CHUNK, CONV_DIM), lambda i: (i, 0)), pl.BlockSpec((CHUNK, D_INNER), lambda i: (i, zcb)),
                  pl.BlockSpec((CHUNK, DT_PAD), lambda i: (i, 0)), vec(DT_PAD), vec(DT_PAD), vec(D_INNER), vec(D_INNER),
                  pl.BlockSpec((DT_PAD, D_INNER), lambda i: (0, 0))],
        out_specs=[pl.BlockSpec((CHUNK, D_INNER), lambda i: (i, 0)), pl.BlockSpec((CHUNK, D_INNER), lambda i: (i, 0)),
                   pl.BlockSpec((1, D_STATE, D_INNER), lambda i: (i, 0, 0))],
        out_shape=[jax.ShapeDtypeStruct((t, D_INNER), F32), jax.ShapeDtypeStruct((t, D_INNER), BF16),
                   jax.ShapeDtypeStruct((nc, D_STATE, D_INNER), F32)],
        scratch_shapes=[pltpu.VMEM((D_STATE, D_INNER), F32)],
        compiler_params=_params(("arbitrary",), blk),
    )(xc, proj, dt_raw, dtb, alog, dskip_full, ng, e_bf)


def _ssd_bwd(dyb, y, xc, proj, dt_raw, sprev, dtb, alog, dskip_full, ng, e_bf, dproj, name):
    t = xc.shape[0]
    nc = t // CHUNK
    zcb = COL_Z // D_INNER
    hpg = GROUP_W // HEAD_DIM
    rev = lambda i: nc - 1 - i

    def body(dyb_ref, y_ref, xc_ref, z_ref, dt_ref, sprev_ref, dtb_ref, alog_ref, dsk_ref, ng_ref, e_ref, dproj_in,
             dz_ref, dxc_ref, ddt_ref, gng_ref, gdsk_ref, galog_ref, gdtb_ref, ds_scr, sums_scr):
        del dproj_in

        @pl.when(pl.program_id(0) == 0)
        def _():
            ds_scr[...] = jnp.zeros_like(ds_scr)
            gng_ref[...] = jnp.zeros_like(gng_ref)
            gdsk_ref[...] = jnp.zeros_like(gdsk_ref)
            galog_ref[...] = jnp.zeros_like(galog_ref)
            gdtb_ref[...] = jnp.zeros_like(gdtb_ref)

        mask = _tri(True)
        tril_bf = mask.astype(BF16)
        triu_bf = _tri(False).astype(BF16)
        e_v = e_ref[...]
        dt_in = dt_ref[...] + dtb_ref[...]
        dtv, a, cs, dt_full, ecs_full, decay_full = _ssd_decays(dt_ref[...], dtb_ref[...], alog_ref[...], e_v, tril_bf)
        cs_t = cs.T

        lane_h = lax.broadcasted_iota(jnp.int32, (CHUNK, DT_PAD), 1)
        sub_h = lax.broadcasted_iota(jnp.int32, (DT_PAD, CHUNK), 0)
        dcs_rows = jnp.zeros((CHUNK, DT_PAD), F32)
        dcs_cols_t = jnp.zeros((DT_PAD, CHUNK), F32)
        last_cols, dsk_cols = [], []
        for g in range(GROUPS):
            gc = slice(g * GROUP_W, (g + 1) * GROUP_W)
            b_cols = slice(D_INNER + g * D_STATE, D_INNER + (g + 1) * D_STATE)
            c_cols = slice(D_INNER + GROUPS * D_STATE + g * D_STATE, D_INNER + GROUPS * D_STATE + (g + 1) * D_STATE)
            xs = xc_ref[:, gc]
            xdt = xs * dt_full[:, gc]
            xdt_b = xdt.astype(BF16)
            xdec = xdt * decay_full[:, gc]
            xdec_b = xdec.astype(BF16)
            zv = z_ref[:, gc]
            sg = _sigmoid(zv)
            gate = zv * sg
            yv = y_ref[:, gc]
            dybv = dyb_ref[:, gc]
            ygate = yv * gate
            rstd = lax.rsqrt(jnp.mean(ygate * ygate, axis=-1, keepdims=True) + NORM_EPS)
            yn = ygate * rstd
            gng_ref[:, gc] += jnp.sum(dybv * yn, axis=0, keepdims=True)
            dyn = dybv * ng_ref[:, gc]
            dyg = rstd * (dyn - yn * jnp.mean(dyn * yn, axis=-1, keepdims=True))
            dz_ref[:, gc] = (dyg * yv * sg * (1.0 + zv * (1.0 - sg))).astype(BF16)
            dy = dyg * gate
            dy_b = dy.astype(BF16)
            dyo = dy * ecs_full[:, gc]
            dyo_b = dyo.astype(BF16)
            dsk_cols.append(jnp.sum(dy * xs, axis=0, keepdims=True))

            bg = xc_ref[:, b_cols].astype(BF16)
            cg = xc_ref[:, c_cols].astype(BF16)
            s_prev = sprev_ref[0, :, gc]
            s_prev_b = s_prev.astype(BF16)
            dsg = ds_scr[:, gc]
            dsg_b = dsg.astype(BF16)
            cb = _dot(cg, bg, _NT)
            c_s = _dot(cg, s_prev_b, _NN)
            b_ds = _dot(bg, dsg_b, _NN)
            dcb = jnp.zeros((CHUNK, CHUNK), F32)
            parts = []
            for r in range(hpg):
                h = g * hpg + r
                hc = slice(r * HEAD_DIM, (r + 1) * HEAD_DIM)
                lmat, m = _head_mats(cs, cs_t, cb, h, mask)
                dm = _dot(dy_b[:, hc], xdt_b[:, hc], _NT)
                parts.append(_dot(m.astype(BF16), dy_b[:, hc], _TN))
                dcb = dcb + dm * lmat
                w = dm * m
                dcs_rows = jnp.where(lane_h == h, jnp.sum(w, axis=1, keepdims=True), dcs_rows)
                dcs_cols_t = jnp.where(sub_h == h, jnp.sum(w, axis=0, keepdims=True), dcs_cols_t)
            dxdt = jnp.concatenate(parts, axis=1) + decay_full[:, gc] * b_ds
            dcb_b = dcb.astype(BF16)
            dxc_ref[:, c_cols] = _dot(dcb_b, bg, _NN) + _dot(dyo_b, s_prev_b, _NT)
            dxc_ref[:, b_cols] = _dot(dcb_b, cg, _TN) + _dot(xdec_b, dsg_b, _NT)
            cdec = ecs_full[CHUNK - 1:CHUNK, gc]
            ds_scr[:, gc] = _dot(cg, dyo_b, _TN) + cdec * dsg
            dxc_ref[:, gc] = dxdt * dt_full[:, gc] + dsk_ref[:, gc] * dy
            dec_prod = xdec * b_ds
            sums_scr[:CHUNK, gc] = dyo * c_s - dec_prod
            sums_scr[CHUNK:, gc] = dxdt * xs
            last_cols.append(jnp.sum(dec_prod, axis=0, keepdims=True) + cdec * jnp.sum(dsg * s_prev, axis=0, keepdims=True))
        t_sums = _head_sums(sums_scr[...], e_v)
        tail = jnp.concatenate([jnp.concatenate(last_cols, axis=1), jnp.concatenate(dsk_cols, axis=1),
                                jnp.zeros((SUBLANES - 2, D_INNER), F32)], axis=0)
        t_tail = _dot_exact_rhs(tail, e_v, _NT)
        gdsk_ref[...] += t_tail[1:2, :]
        row = lax.broadcasted_iota(jnp.int32, (CHUNK, DT_PAD), 0)
        dcs = dcs_rows - dcs_cols_t.T + t_sums[:CHUNK] + jnp.where(row == CHUNK - 1, t_tail[0:1, :], 0.0)
        dda = _dot_exact_lhs(triu_bf, dcs, _NN)
        galog_ref[...] += jnp.sum(dda * dtv, axis=0, keepdims=True) * a
        ddt = dda * a + t_sums[CHUNK:]
        ddt_raw = jnp.where(lane_h < N_HEADS, ddt * _sigmoid(dt_in), 0.0)
        gdtb_ref[...] += jnp.sum(ddt_raw, axis=0, keepdims=True)
        ddt_ref[...] = ddt_raw.astype(BF16)

    vec = lambda w: pl.BlockSpec((1, w), lambda i: (0, 0))
    blk = (2 * _nbytes((CHUNK, CONV_DIM), F32) + 4 * _nbytes((CHUNK, D_INNER), F32) + 4 * _nbytes((D_STATE, D_INNER), F32))
    return pl.pallas_call(
        body, name=name, grid=(nc,),
        in_specs=[pl.BlockSpec((CHUNK, D_INNER), lambda i: (rev(i), 0)), pl.BlockSpec((CHUNK, D_INNER), lambda i: (rev(i), 0)),
                  pl.BlockSpec((CHUNK, CONV_DIM), lambda i: (rev(i), 0)), pl.BlockSpec((CHUNK, D_INNER), lambda i: (rev(i), zcb)),
                  pl.BlockSpec((CHUNK, DT_PAD), lambda i: (rev(i), 0)), pl.BlockSpec((1, D_STATE, D_INNER), lambda i: (rev(i), 0, 0)),
                  vec(DT_PAD), vec(DT_PAD), vec(D_INNER), vec(D_INNER), pl.BlockSpec((DT_PAD, D_INNER), lambda i: (0, 0)),
                  pl.BlockSpec(memory_space=pl.ANY)],
        out_specs=[pl.BlockSpec((CHUNK, D_INNER), lambda i: (rev(i), zcb)), pl.BlockSpec((CHUNK, CONV_DIM), lambda i: (rev(i), 0)),
                   pl.BlockSpec((CHUNK, DT_PAD), lambda i: (rev(i), 0)), vec(D_INNER), vec(DT_PAD), vec(DT_PAD), vec(DT_PAD)],
        out_shape=[jax.ShapeDtypeStruct(dproj.shape, BF16), jax.ShapeDtypeStruct((t, CONV_DIM), F32),
                   jax.ShapeDtypeStruct((t, DT_PAD), BF16), jax.ShapeDtypeStruct((1, D_INNER), F32),
                   jax.ShapeDtypeStruct((1, DT_PAD), F32), jax.ShapeDtypeStruct((1, DT_PAD), F32),
                   jax.ShapeDtypeStruct((1, DT_PAD), F32)],
        scratch_shapes=[pltpu.VMEM((D_STATE, D_INNER), F32), pltpu.VMEM((2 * CHUNK, D_INNER), F32)],
        input_output_aliases={11: 0},
        compiler_params=_params(("arbitrary",), blk),
    )(dyb, y, xc, proj, dt_raw, sprev, dtb, alog, dskip_full, ng, e_bf, dproj)


def _mesh_pos():
    return lax.axis_index("x"), lax.axis_index("y"), lax.axis_index("c")


def _other_chips(x, y):
    return [(1 - x, y), (x, 1 - y), (1 - x, 1 - y)]


def _all_peers(x, y, c):
    peers = []
    for k in range(1, N_DEV):
        fx, fy, fc = (k >> 2) & 1, (k >> 1) & 1, k & 1
        px, py, pc = x + fx - 2 * x * fx, y + fy - 2 * y * fy, c + fc - 2 * c * fc
        peers.append(((px, py, pc), 4 * px + 2 * py + pc))
    return peers


def _all_gather(shards, name, own_only=()):
    n, n_own = len(shards), len(own_only)

    def body(*refs):
        ins, own_ins = refs[:n], refs[n:n + n_own]
        outs, own_outs = refs[n + n_own:2 * n + n_own], refs[2 * n + n_own:2 * (n + n_own)]
        send_sems, recv_sems, local_sems = refs[2 * (n + n_own):]
        x, y, c = _mesh_pos()
        me, sibling = (x, y, c), (x, y, 1 - c)
        chips = _other_chips(x, y)

        def slot(p):
            return 4 * p[0] + 2 * p[1] + p[2]

        def copy(a, k, block, to, src=None):
            dst = outs[a].at[slot(block)]
            return pltpu.make_async_remote_copy(
                src_ref=dst if src is None else src, dst_ref=dst, send_sem=send_sems.at[a * 7 + k],
                recv_sem=recv_sems.at[a * 7 + k], device_id=to, device_id_type=MESH)

        started = []
        own = []
        for a in range(n_own):
            mine = pltpu.make_async_copy(own_ins[a], own_outs[a].at[slot(me)], local_sems.at[n + a])
            mine.start()
            own.append(mine)
        for a in range(n):
            mine = pltpu.make_async_copy(ins[a], outs[a].at[slot(me)], local_sems.at[a])
            mine.start()
            own.append(mine)
            first = [copy(a, 0, me, sibling, src=ins[a])]
            first += [copy(a, 1 + j, me, (*chip, c), src=ins[a]) for j, chip in enumerate(chips)]
            for cp in first:
                cp.start()
            started += first
        for a in range(n):
            for j, chip in enumerate(chips):
                copy(a, 1 + j, (*chip, c), me).wait_recv()
                fwd = copy(a, 4 + j, (*chip, c), sibling)
                fwd.start()
                started.append(fwd)
        for a in range(n):
            copy(a, 0, sibling, me).wait_recv()
            for j, chip in enumerate(chips):
                copy(a, 4 + j, (*chip, 1 - c), me).wait_recv()
        for cp in started:
            cp.wait_send()
        for mine in own:
            mine.wait()

    return pl.pallas_call(
        body, name=name,
        in_specs=[_HBM] * (n + n_own), out_specs=[_HBM] * (n + n_own),
        out_shape=[jax.ShapeDtypeStruct((N_DEV,) + s.shape, s.dtype) for s in (*shards, *own_only)],
        scratch_shapes=[pltpu.SemaphoreType.DMA((7 * n,)), pltpu.SemaphoreType.DMA((7 * n,)),
                        pltpu.SemaphoreType.DMA((n + n_own,))],
    )(*shards, *own_only)


_SMALL_ROWS = (("norm_mix_g", 8), ("conv_b", 32), ("dt_bias", 1), ("a_log", 1), ("d_skip", 1), ("ssm_norm_g", 16),
               ("v_norm_g", 8), ("v_norm_b", 8), ("w_spatial", 1024), ("b_spatial", 8), ("b_gates", 16), ("norm_mlp_g", 8),
               ("norm_final_g", 8), ("conv_w", 128), ("loss", 1))
_SMALL_PACKED_ROWS = -(-sum(r for _, r in _SMALL_ROWS) // SUBLANES) * SUBLANES


def _small_offsets():
    offs, r = {}, 0
    for name, rows in _SMALL_ROWS:
        offs[name] = r
        r += rows
    return offs


def _rows_from(src_ref, dst_ref, r0):
    k, w = src_ref.shape
    if w <= LANES:
        dst_ref[r0:r0 + k, 0:w] = src_ref[...]
        return
    per = w // LANES
    for i in range(k):
        for j in range(per):
            dst_ref[r0 + i * per + j:r0 + i * per + j + 1, :] = src_ref[i:i + 1, j * LANES:(j + 1) * LANES]


def _rows_to(src_ref, r0, dst_ref):
    k, w = dst_ref.shape
    if w <= LANES:
        dst_ref[...] = src_ref[r0:r0 + k, 0:w]
        return
    per = w // LANES
    for i in range(k):
        for j in range(per):
            dst_ref[i:i + 1, j * LANES:(j + 1) * LANES] = src_ref[r0 + i * per + j:r0 + i * per + j + 1, :]


def _exchange_small(grads, name):
    names = [n for n, _ in _SMALL_ROWS]
    offs = _small_offsets()
    n_in = len(names)

    def body(*refs):
        ins, out_ref = refs[:n_in], refs[n_in]
        packed, send_sems, recv_sems, local_sem = refs[n_in + 1:]
        packed[...] = jnp.zeros_like(packed)
        for n, ref in zip(names, ins):
            _rows_from(ref, packed, offs[n])
        x, y, c = _mesh_pos()
        my_slot = 4 * x + 2 * y + c
        mine = pltpu.make_async_copy(packed, out_ref.at[my_slot], local_sem)
        mine.start()
        copies = []
        for k, (peer, peer_slot) in enumerate(_all_peers(x, y, c)):
            sems = dict(send_sem=send_sems.at[k], recv_sem=recv_sems.at[k], device_id=peer, device_id_type=MESH)
            send = pltpu.make_async_remote_copy(src_ref=packed, dst_ref=out_ref.at[my_slot], **sems)
            send.start()
            copies.append((send, pltpu.make_async_remote_copy(src_ref=packed, dst_ref=out_ref.at[peer_slot], **sems)))
        for send, recv in copies:
            send.wait_send()
            recv.wait_recv()
        mine.wait()

    return pl.pallas_call(
        body, name=name, in_specs=[pl.BlockSpec(memory_space=pltpu.VMEM)] * n_in, out_specs=_HBM,
        out_shape=jax.ShapeDtypeStruct((N_DEV, _SMALL_PACKED_ROWS, LANES), F32),
        scratch_shapes=[pltpu.VMEM((_SMALL_PACKED_ROWS, LANES), F32), pltpu.SemaphoreType.DMA((N_DEV - 1,)),
                        pltpu.SemaphoreType.DMA((N_DEV - 1,)), pltpu.SemaphoreType.DMA],
    )(*[grads[n] for n in names])


def _swap_with_sibling(grads, name):
    n = len(grads)

    def body(*refs):
        ins, outs = refs[:n], refs[n:2 * n]
        send_sems, recv_sems = refs[2 * n:]
        x, y, c = _mesh_pos()
        copies = []
        for a in range(n):
            for k in range(N_CHIP):
                cp = pltpu.make_async_remote_copy(
                    src_ref=ins[a].at[(1 - c) + 2 * k], dst_ref=outs[a].at[k], send_sem=send_sems.at[a * N_CHIP + k],
                    recv_sem=recv_sems.at[a * N_CHIP + k], device_id=(x, y, 1 - c), device_id_type=MESH)
                cp.start()
                copies.append(cp)
        for cp in copies:
            cp.wait()

    return pl.pallas_call(
        body, name=name, in_specs=[_HBM] * n, out_specs=[_HBM] * n,
        out_shape=[jax.ShapeDtypeStruct((N_CHIP,) + g.shape[1:], g.dtype) for g in grads],
        scratch_shapes=[pltpu.SemaphoreType.DMA((N_CHIP * n,)), pltpu.SemaphoreType.DMA((N_CHIP * n,))],
    )(*grads)


_SEM = pl.BlockSpec(memory_space=pltpu.SEMAPHORE)
_IN_HBM = pl.BlockSpec(memory_space=pltpu.HBM)
_EFFECT = pltpu.SideEffectType.DATAFLOW_SIDE_EFFECTING


def _in_hbm(a):
    return pltpu.with_memory_space_constraint(a, pltpu.HBM)


def _gather_copies(ins, lands, send_sems, recv_sems):
    x, y, c = _mesh_pos()
    my_slot = 4 * x + 2 * y + c
    pairs = []
    for a in range(len(ins)):
        for k, (peer, peer_slot) in enumerate(_all_peers(x, y, c)):
            sems = dict(send_sem=send_sems.at[a * (N_DEV - 1) + k], recv_sem=recv_sems.at[a * (N_DEV - 1) + k],
                        device_id=peer, device_id_type=MESH)
            pairs.append((pltpu.make_async_remote_copy(src_ref=ins[a], dst_ref=lands[a].at[my_slot], **sems),
                          pltpu.make_async_remote_copy(src_ref=ins[a], dst_ref=lands[a].at[peer_slot], **sems)))
    return pairs


def _scatter_copies(ins, lands, send_sems, recv_sems):
    x, y, c = _mesh_pos()
    my_chip = 2 * x + y
    pairs = []
    for a in range(len(ins)):
        for j, chip in enumerate(_other_chips(x, y)):
            there = 2 * chip[0] + chip[1]
            sems = dict(send_sem=send_sems.at[a * 3 + j], recv_sem=recv_sems.at[a * 3 + j],
                        device_id=(*chip, c), device_id_type=MESH)
            pairs.append((pltpu.make_async_remote_copy(src_ref=ins[a].at[there], dst_ref=lands[a].at[my_chip], **sems),
                          pltpu.make_async_remote_copy(src_ref=ins[a].at[my_chip], dst_ref=lands[a].at[there], **sems)))
    return pairs


def _split_start(srcs, lands, copies, per_array, name):
    n = len(srcs)

    def body(*refs):
        ins, land_refs = refs[:n], refs[n:2 * n]
        send_sems, recv_sems = refs[2 * n], refs[2 * n + 1]
        token = refs[-1]
        for send, _ in copies(ins, land_refs, send_sems, recv_sems):
            send.start()
        token[...] = jnp.zeros_like(token)

    outs = pl.pallas_call(
        body, name=name,
        out_shape=(pltpu.SemaphoreType.DMA((per_array * n,)), pltpu.SemaphoreType.DMA((per_array * n,)),
                   *[pltpu.HBM(s.shape, s.dtype) for s in srcs], *[pltpu.HBM(l.shape, l.dtype) for l in lands],
                   jax.ShapeDtypeStruct((SUBLANES, LANES), F32)),
        in_specs=[_IN_HBM] * (2 * n),
        out_specs=(_SEM, _SEM, *[_IN_HBM] * (2 * n), pl.BlockSpec(memory_space=pltpu.VMEM)),
        input_output_aliases={i: 2 + i for i in range(2 * n)},
        compiler_params=pltpu.CompilerParams(has_side_effects=_EFFECT),
    )(*[_in_hbm(s) for s in srcs], *[_in_hbm(l) for l in lands])
    return outs[0], outs[1], list(outs[2:2 + n]), list(outs[2 + n:2 + 2 * n]), outs[-1]


def _split_wait(started, copies, after, name):
    send_sems, recv_sems, srcs, lands, _ = started
    n = len(srcs)

    def body(*refs):
        ins, land_refs = refs[:n], refs[n:2 * n]
        for send, recv in copies(ins, land_refs, refs[2 * n], refs[2 * n + 1]):
            send.wait_send()
            recv.wait_recv()

    outs = pl.pallas_call(
        body, name=name,
        out_shape=(*[pltpu.HBM(s.shape, s.dtype) for s in srcs], *[pltpu.HBM(l.shape, l.dtype) for l in lands]),
        in_specs=[_IN_HBM] * (2 * n) + [_SEM, _SEM, _HBM],
        out_specs=[_IN_HBM] * (2 * n),
        input_output_aliases={i: i for i in range(2 * n)},
        compiler_params=pltpu.CompilerParams(has_side_effects=_EFFECT),
    )(*srcs, *lands, send_sems, recv_sems, after)
    return list(outs[:n]), list(outs[n:])


def _ew_block(rows, cols, slots):
    budget = 2 * 1024 * 1024
    br, bc = rows, cols
    while slots * br * bc * 4 > budget:
        if br % 2 == 0 and (br // 2) % (2 * SUBLANES) == 0:
            br //= 2
        elif bc % 2 == 0 and (bc // 2) % LANES == 0:
            bc //= 2
        else:
            break
    return br, bc


def _add_sibling(grads, recv, c_idx, name):
    _, rows, cols = grads.shape
    br, bc = _ew_block(rows, cols, 3)

    def body(c_ref, g_ref, r_ref, out_ref):
        del c_ref
        out_ref[...] = (g_ref[...].astype(F32) + r_ref[...].astype(F32)).astype(out_ref.dtype)

    grid_spec = pltpu.PrefetchScalarGridSpec(
        num_scalar_prefetch=1, grid=(N_CHIP, rows // br, cols // bc),
        in_specs=[pl.BlockSpec((1, br, bc), lambda k, i, j, c_ref: (c_ref[0] + 2 * k, i, j)),
                  pl.BlockSpec((1, br, bc), lambda k, i, j, c_ref: (k, i, j))],
        out_specs=pl.BlockSpec((1, br, bc), lambda k, i, j, c_ref: (k, i, j)))
    return pl.pallas_call(
        body, name=name, grid_spec=grid_spec, out_shape=jax.ShapeDtypeStruct((N_CHIP, rows, cols), grads.dtype),
        compiler_params=_params(("parallel", "parallel", "parallel"), 3 * _nbytes((br, bc), F32)),
    )(c_idx, grads, recv)


def _adam_math(g, w, m, v):
    m2 = ADAM_B1 * m + (1.0 - ADAM_B1) * g
    v2 = ADAM_B2 * v + (1.0 - ADAM_B2) * (g * g)
    m_hat = m2 * (1.0 / (1.0 - ADAM_B1 ** ADAM_STEP))
    v_hat = v2 * (1.0 / (1.0 - ADAM_B2 ** ADAM_STEP))
    return -ADAM_LR * (m_hat / (jnp.sqrt(v_hat) + ADAM_EPS) + ADAM_WD * w), m2, v2


def _adamw(slots, w, m, v, name, own=None, own_slot=None):
    ns, rows, cols = slots.shape
    br, bc = _ew_block(rows, cols, 2 * ns + 7)

    def update(g, w_ref, m_ref, v_ref, g_ref, d_ref, m2_ref, v2_ref):
        g_ref[...] = g
        d_ref[...], m2_ref[...], v2_ref[...] = _adam_math(g, w_ref[...], m_ref[...], v_ref[...])

    out_shape = [jax.ShapeDtypeStruct((rows, cols), F32)] * 4
    params = _params(("parallel", "parallel"), (2 * ns + 7) * _nbytes((br, bc), F32))
    grid = (rows // br, cols // bc)
    if own is None:
        def body(s_ref, *rest):
            g = s_ref[0].astype(F32)
            for k in range(1, ns):
                g = g + s_ref[k].astype(F32)
            update(g, *rest)

        blk = pl.BlockSpec((br, bc), lambda i, j: (i, j))
        return pl.pallas_call(
            body, name=name, grid=grid,
            in_specs=[pl.BlockSpec((ns, br, bc), lambda i, j: (0, i, j)), blk, blk, blk], out_specs=[blk] * 4,
            out_shape=out_shape, compiler_params=params,
        )(slots, w, m, v)

    def body_own(slot_ref, s_ref, o_ref, *rest):
        g = None
        for k in range(ns):
            term = jnp.where(slot_ref[0] == k, o_ref[k].astype(F32), s_ref[k].astype(F32))
            g = term if g is None else g + term
        update(g, *rest)

    blk = pl.BlockSpec((br, bc), lambda i, j, slot_ref: (i, j))
    stack = pl.BlockSpec((ns, br, bc), lambda i, j, slot_ref: (0, i, j))
    grid_spec = pltpu.PrefetchScalarGridSpec(num_scalar_prefetch=1, grid=grid, in_specs=[stack, stack, blk, blk, blk],
                                             out_specs=[blk] * 4)
    return pl.pallas_call(body_own, name=name, grid_spec=grid_spec, out_shape=out_shape, compiler_params=params,
                          )(own_slot, slots, own, w, m, v)


def _adamw_small(all_g, params, extra_shapes, name):
    names = [n for n, _ in _SMALL_ROWS if n in params]
    extras = [n for n, _ in _SMALL_ROWS if n not in params]
    offs = _small_offsets()
    n_p = len(names)

    def body(*refs):
        s_ref = refs[0]
        wmv = refs[1:1 + 3 * n_p]
        outs = refs[1 + 3 * n_p:1 + 7 * n_p]
        extra_refs = refs[1 + 7 * n_p:1 + 7 * n_p + len(extras)]
        summed = refs[-1]
        g = s_ref[0]
        for k in range(1, N_DEV):
            g = g + s_ref[k]
        summed[...] = g
        for i, n in enumerate(names):
            w_ref, m_ref, v_ref = wmv[3 * i:3 * i + 3]
            g_ref, d_ref, m2_ref, v2_ref = outs[4 * i:4 * i + 4]
            _rows_to(summed, offs[n], g_ref)
            d_ref[...], m2_ref[...], v2_ref[...] = _adam_math(g_ref[...], w_ref[...], m_ref[...], v_ref[...])
        for n, ref in zip(extras, extra_refs):
            _rows_to(summed, offs[n], ref)

    flat = [a for n in names for a in params[n]]
    out_shape = [jax.ShapeDtypeStruct(params[n][0].shape, F32) for n in names for _ in range(4)]
    out_shape += [jax.ShapeDtypeStruct(s, F32) for s in extra_shapes]
    vmem = pl.BlockSpec(memory_space=pltpu.VMEM)
    res = pl.pallas_call(
        body, name=name, in_specs=[vmem] * (1 + len(flat)), out_specs=[vmem] * len(out_shape), out_shape=out_shape,
        scratch_shapes=[pltpu.VMEM((_SMALL_PACKED_ROWS, LANES), F32)],
        compiler_params=pltpu.CompilerParams(vmem_limit_bytes=_vmem_limit(_nbytes(all_g.shape, F32))),
    )(all_g, *flat)
    return {n: res[4 * i:4 * i + 4] for i, n in enumerate(names)}, res[4 * n_p:]


def _mm_tiles(mode, m, n, k):
    tn = min(n, 1024)
    if mode == "tn":
        return min(m, 1024), tn, min(k, 2048)
    if k <= 2048:
        return min(m, 1024), tn, k
    if k <= 4096:
        return min(m, 512), tn, k
    return min(m, 1024), tn, 2048


def _local_step(x, target, wts, small, exchange):
    t = x.shape[0]
    w_main_t, w_dt_t = wts["w_main_t"], wts["w_dt_t"]
    bsp_t = small["b_spatial"].T
    pad32 = lambda a: jnp.pad(a, ((0, 0), (0, DT_PAD - N_HEADS)))
    dtb, alog = pad32(small["dt_bias"]), pad32(small["a_log"])
    dskip_full = jnp.repeat(small["d_skip"], HEAD_DIM, axis=1)
    head_of_col = lax.broadcasted_iota(jnp.int32, (DT_PAD, D_INNER), 1) // HEAD_DIM
    e_bf = (head_of_col == lax.broadcasted_iota(jnp.int32, (DT_PAD, D_INNER), 0)).astype(BF16)

    def mm(a, b, mode, name, **kw):
        if mode == "nn":
            m, k, n = a.shape[0], a.shape[1], b.shape[1]
        elif mode == "nt":
            m, k, n = a.shape[0], a.shape[1], b.shape[0]
        else:
            m, k, n = a.shape[1], a.shape[0], b.shape[1]
        tm, tn, tk = _mm_tiles(mode, m, n, k)
        kw.setdefault("out_dtypes", (BF16,) if mode == "tn" else (F32,))
        if "extra_specs" in kw:
            kw["extra_specs"] = kw["extra_specs"](tm, tn)
        return _matmul(a, b, mode=mode, tm=tm, tn=tn, tk=tk, name=name, **kw)

    def out_tile(tm, tn):
        return (((tm, tn), lambda i, j: (i, j)),)

    h = _rms_fwd(x, small["norm_mix_g"], "rms_mix", deps=exchange.begin())
    proj = mm(h, w_main_t, "nt", "proj_main", j_outer=True)
    dt_raw = mm(h, w_dt_t, "nt", "proj_dt")
    y_a = _gmlp_fwd(proj, small["v_norm_g"], small["v_norm_b"], small["w_spatial"], bsp_t, "gmlp_fwd")
    pre_conv, xc = _conv_fwd(proj, wts["conv_w"], small["conv_b"], "conv_fwd")
    y_ssd, y_b, sprev = _ssd_fwd(xc, proj, dt_raw, dtb, alog, dskip_full, small["ssm_norm_g"], e_bf, "ssd_fwd")
    wts = {**wts, **exchange.late_weights(y_b)}
    pa = mm(y_a, wts["w_proj_a"], "nn", "proj_a")
    pb = mm(y_b, wts["w_proj_b"], "nn", "proj_b")
    merged = _merge_fwd(pa, pb, proj, small["b_gates"], "merge_fwd")

    def add_residual(acc, ex, outs):
        outs[0][...] = acc + ex[0][...]

    x1 = mm(merged, wts["w_out"], "nn", "out_proj", epilogue=add_residual, extras=(x,), extra_specs=out_tile)
    h2 = _rms_fwd(x1, small["norm_mlp_g"], "rms_mlp")

    def relu_sq(acc, ex, outs):
        outs[0][...] = acc
        r = jnp.maximum(acc, 0.0)
        outs[1][...] = (r * r).astype(BF16)

    up, act = mm(h2, wts["w_mlp_up"], "nn", "mlp_up", epilogue=relu_sq, out_dtypes=(F32, BF16), j_outer=True)
    x2 = mm(act, wts["w_mlp_down"], "nn", "mlp_down", epilogue=add_residual, extras=(x1,), extra_specs=out_tile)
    _, dx2, dx2_b, g_final, loss = _loss_head(x2, small["norm_final_g"], target, "loss_head")

    def relu_sq_bwd(acc, ex, outs):
        outs[0][...] = (acc * 2.0 * jnp.maximum(ex[0][...], 0.0)).astype(BF16)

    dup = mm(dx2_b, wts["w_mlp_down"], "nt", "d_act", epilogue=relu_sq_bwd, extras=(up,), extra_specs=out_tile,
             out_dtypes=(BF16,), j_outer=True)
    g_down = mm(act, dx2_b, "tn", "g_mlp_down")
    g_up = mm(h2, dup, "tn", "g_mlp_up")
    started = exchange.reduce("mlp", {"w_mlp_down": g_down, "w_mlp_up": g_up})
    dh2 = mm(dup, wts["w_mlp_up"], "nt", "d_h2", deps=started)
    dx1, dx1_b, g_mlp = _rms_bwd(x1, small["norm_mlp_g"], dh2, dx2, "rms_mlp_bwd")

    g_out = mm(merged, dx1_b, "tn", "g_out")
    dmerged = mm(dx1_b, wts["w_out"], "nt", "d_merged")
    dpa, dpb, dproj, g_bgates = _merge_bwd(dmerged, pa, pb, proj, small["b_gates"], "merge_bwd")
    g_pa = mm(y_a, dpa, "tn", "g_proj_a")
    g_pb = mm(y_b, dpb, "tn", "g_proj_b")
    started = exchange.reduce("proj", {"w_out": g_out, "w_proj_a": g_pa, "w_proj_b": g_pb})
    dya = mm(dpa, wts["w_proj_a"], "nt", "d_ya", deps=started)
    dyb = mm(dpb, wts["w_proj_b"], "nt", "d_yb")

    dproj, g_wsp, g_bsp_t, g_vg, g_vb = _gmlp_bwd(proj, dya, small["v_norm_g"], small["v_norm_b"], small["w_spatial"],
                                                   bsp_t, dproj, "gmlp_bwd")
    dproj, dxc, ddt, g_ng, g_dskip, g_alog, g_dtb = _ssd_bwd(dyb, y_ssd, xc, proj, dt_raw, sprev, dtb, alog, dskip_full,
                                                             small["ssm_norm_g"], e_bf, dproj, "ssd_bwd")
    dproj, g_convw, g_convb = _conv_bwd(proj, pre_conv, dxc, wts["conv_w"], dproj, "conv_bwd")

    g_main_t = mm(dproj, h, "tn", "g_in_main")
    g_dt_t = mm(ddt, h, "tn", "g_in_dt")
    started = exchange.reduce("in", {"w_in": _join_w_in(g_main_t, g_dt_t)})

    def add_dt(acc, ex, outs):
        outs[0][...] = acc + _dot(ex[0][...], ex[1][...], _NN)

    dh = mm(dproj, w_main_t, "nn", "d_h", epilogue=add_dt, extras=(ddt, w_dt_t), deps=started,
            extra_specs=lambda tm, tn: (((tm, DT_PAD), lambda i, j: (i, 0)), ((DT_PAD, D_MODEL), lambda i, j: (0, 0))))
    grad_x, _, g_mix = _rms_bwd(x, small["norm_mix_g"], dh, dx1, "rms_mix_bwd")

    grads = {
        "conv_w": g_convw,
        "norm_mix_g": g_mix, "conv_b": g_convb, "dt_bias": g_dtb, "a_log": g_alog, "d_skip": g_dskip, "ssm_norm_g": g_ng,
        "v_norm_g": g_vg, "v_norm_b": g_vb, "w_spatial": g_wsp.reshape(GROUPS * CHUNK, CHUNK), "b_spatial": g_bsp_t.T, "b_gates": g_bgates, "norm_mlp_g": g_mlp, "norm_final_g": g_final,
    }
    return loss, grad_x, grads


def _split_w_in(w_full_t):
    dt0 = COL_GATE
    w_main_t = jnp.concatenate([w_full_t[:dt0], w_full_t[dt0 + N_HEADS:]], axis=0)
    w_dt_t = jnp.pad(w_full_t[dt0:dt0 + N_HEADS], ((0, DT_PAD - N_HEADS), (0, 0)))
    return w_main_t, w_dt_t


def _join_w_in(g_main_t, g_dt_t):
    dt0 = COL_GATE
    return jnp.concatenate([g_main_t[:dt0], g_dt_t[:N_HEADS], g_main_t[dt0:]], axis=0)


_LATE = ["w_proj_a", "w_proj_b", "w_out", "w_mlp_up", "w_mlp_down"]
_BY_COLS = ("w_mlp_up",)


class _Exchange:
    def __init__(self, late_shards, late_lands):
        self.late_shards, self.late_lands = late_shards, late_lands
        self.c_idx = lax.axis_index("c").astype(jnp.int32).reshape(1)
        self.chip_idx = (2 * lax.axis_index("x") + lax.axis_index("y")).astype(jnp.int32).reshape(1)
        self.pending = []

    def begin(self):
        self.late = _split_start(self.late_shards, self.late_lands, _gather_copies, N_DEV - 1, "gather_late_start")
        return [self.late[-1]]

    def late_weights(self, after):
        _, lands = _split_wait(self.late, _gather_copies, after, "gather_late_wait")
        whole = {}
        for n, g in zip(_LATE, lands):
            whole[n] = jnp.transpose(g, (1, 0, 2)).reshape(g.shape[1], -1) if n in _BY_COLS else g.reshape(-1, g.shape[2])
        return whole

    def reduce(self, tag, grads):
        names = list(grads)
        by_dev = []
        for n in names:
            g = grads[n]
            if n in _BY_COLS:
                by_dev.append(jnp.transpose(g.reshape(g.shape[0], N_DEV, -1), (1, 0, 2)))
            else:
                by_dev.append(g.reshape(N_DEV, -1, g.shape[1]))
        from_sibling = _swap_with_sibling(by_dev, "reduce_cores_" + tag)
        parts = [_add_sibling(g, r, self.c_idx, "add_cores_" + n) for n, g, r in zip(names, by_dev, from_sibling)]
        lands = [lax.empty(p.shape, p.dtype) for p in parts]
        started = _split_start(parts, lands, _scatter_copies, 3, "reduce_chips_start_" + tag)
        self.pending.append((tag, names, started))
        return [started[-1]]

    def finish(self, after):
        done = {}
        for tag, names, started in self.pending:
            parts, lands = _split_wait(started, _scatter_copies, after, "reduce_chips_wait_" + tag)
            for n, land, part in zip(names, lands, parts):
                done[n] = (land, part, self.chip_idx)
        return done


def kernel(x, norm_mix_g, w_in, conv_w, conv_b, dt_bias, a_log, d_skip, ssm_norm_g, v_norm_g, v_norm_b, w_spatial, b_spatial, b_gates, w_proj_a, w_proj_b, w_out, norm_mlp_g, w_mlp_up, w_mlp_down, norm_final_g, loss_target, m_norm_mix_g, m_w_in, m_conv_w, m_conv_b, m_dt_bias, m_a_log, m_d_skip, m_ssm_norm_g, m_v_norm_g, m_v_norm_b, m_w_spatial, m_b_spatial, m_b_gates, m_w_proj_a, m_w_proj_b, m_w_out, m_norm_mlp_g, m_w_mlp_up, m_w_mlp_down, m_norm_final_g, v_norm_mix_g, v_w_in, v_conv_w, v_conv_b, v_dt_bias, v_a_log, v_d_skip, v_ssm_norm_g, v_v_norm_g, v_v_norm_b, v_w_spatial, v_b_spatial, v_b_gates, v_w_proj_a, v_w_proj_b, v_w_out, v_norm_mlp_g, v_w_mlp_up, v_w_mlp_down, v_norm_final_g):
    given = dict(locals())
    names = ["norm_mix_g", "w_in", "conv_w", "conv_b", "dt_bias", "a_log", "d_skip", "ssm_norm_g", "v_norm_g", "v_norm_b",
             "w_spatial", "b_spatial", "b_gates", "w_proj_a", "w_proj_b", "w_out", "norm_mlp_g", "w_mlp_up", "w_mlp_down",
             "norm_final_g"]
    shapes = {n: given[n].shape for n in names}
    dev = 4 * lax.axis_index("x") + 2 * lax.axis_index("y") + lax.axis_index("c")

    shard2d = {"w_in": w_in[0].T, "w_proj_a": w_proj_a[0], "w_proj_b": w_proj_b[0], "w_out": w_out[0],
               "w_mlp_up": w_mlp_up[0], "w_mlp_down": w_mlp_down[0]}
    conv_shard = conv_w.reshape(CONV_WIDTH, -1)
    late_shards = [shard2d[n].astype(BF16) for n in _LATE]
    w_in_all, conv_all, *late_lands = _all_gather([shard2d["w_in"].astype(BF16), conv_shard], "gather_first",
                                                  own_only=late_shards)
    w_main_t, w_dt_t = _split_w_in(w_in_all.reshape(-1, D_MODEL))
    wts = {"w_main_t": w_main_t, "w_dt_t": w_dt_t, "conv_w": jnp.transpose(conv_all, (1, 0, 2)).reshape(CONV_WIDTH, -1)}
    small = {"norm_mix_g": norm_mix_g, "conv_b": conv_b, "dt_bias": dt_bias, "a_log": a_log, "d_skip": d_skip,
             "ssm_norm_g": ssm_norm_g, "v_norm_g": v_norm_g, "v_norm_b": v_norm_b, "w_spatial": w_spatial[0],
             "b_spatial": b_spatial[0], "b_gates": b_gates, "norm_mlp_g": norm_mlp_g,
             "norm_final_g": norm_final_g.reshape(1, -1)}

    exchange = _Exchange(late_shards, late_lands)
    loss_part, grad_x, grads = _local_step(x[0], loss_target[0], wts, small, exchange)

    out = {}
    for n, (slots, own, own_slot) in exchange.finish(grad_x).items():
        moments = [given["m_" + n][0], given["v_" + n][0]]
        if n == "w_in":
            moments = [mom.T for mom in moments]
        res = _adamw(slots, shard2d[n], *moments, "adamw_" + n, own=own, own_slot=own_slot)
        out[n] = [(r.T if n == "w_in" else r).reshape(shapes[n]) for r in res]

    all_g = _exchange_small({**grads, "loss": loss_part}, "exchange_small")
    small["w_spatial"] = small["w_spatial"].reshape(GROUPS * CHUNK, CHUNK)
    params = {n: (w2d, given["m_" + n].reshape(w2d.shape), given["v_" + n].reshape(w2d.shape)) for n, w2d in small.items()}
    updated, (g_conv_full, loss_all) = _adamw_small(all_g, params, [grads["conv_w"].shape, loss_part.shape], "adamw_small")
    for n, res in updated.items():
        out[n] = [r.reshape(shapes[n]) for r in res]
    width = shapes["conv_w"][-1]
    g_conv = lax.dynamic_slice(g_conv_full, (0, dev * width), (CONV_WIDTH, width))
    res = _adamw(g_conv[None], conv_shard, m_conv_w.reshape(CONV_WIDTH, -1), v_conv_w.reshape(CONV_WIDTH, -1), "adamw_conv_w")
    out["conv_w"] = [r.reshape(shapes["conv_w"]) for r in res]

    loss = loss_all[0, 0]
    return (loss, grad_x[None], *[out[n][0] for n in names], *[out[n][1] for n in names],
            *[out[n][2] for n in names], *[out[n][3] for n in names])
```

```python
import functools
import math

import jax
import jax.numpy as jnp
from jax import lax
from jax.experimental import pallas as pl
from jax.experimental.pallas import tpu as pltpu

F32 = jnp.float32
BF16 = jnp.bfloat16
MESH = pl.DeviceIdType.MESH

D_MODEL = 1024
NORM_EPS = 1e-6
CHUNK = 128
GROUPS = 8
D_INNER = 2048
HEAD_DIM = 64
N_HEADS = 32
D_STATE = 128
CONV_WIDTH = 4
CONV_DIM = 4096
D_FF = 4096
GROUP_W = D_INNER // GROUPS
N_DEV = 8
N_CHIP = 4

ADAM_LR = 0.001
ADAM_B1 = 0.9
ADAM_B2 = 0.999
ADAM_EPS = 1e-08
ADAM_WD = 0.01
ADAM_STEP = 10

MAIN_W = 2 * D_MODEL + D_INNER + CONV_DIM + 2 * D_MODEL
COL_Z = 2048
COL_XBC = 4096
COL_GATE = 8192
DT_PAD = 128

LANES = 128
SUBLANES = 8
VMEM_BYTES_V7X = 64 * 1024 * 1024
VMEM_BODY_TEMP = 24 * 1024 * 1024


def _vmem_limit(block_bytes):
    return int(min(2 * block_bytes + VMEM_BODY_TEMP, VMEM_BYTES_V7X - 8 * 1024 * 1024))


def _nbytes(shape, dtype):
    return math.prod(shape) * jnp.dtype(dtype).itemsize


_HBM = pl.BlockSpec(memory_space=pl.ANY)


def _params(sem, block_bytes):
    return pltpu.CompilerParams(dimension_semantics=sem, vmem_limit_bytes=_vmem_limit(block_bytes))


def _sigmoid(x):
    return 1.0 / (1.0 + jnp.exp(-x))


def _softplus(x):
    e = jnp.exp(-jnp.abs(x))
    u = 1.0 + e
    log1p_e = jnp.where(u == 1.0, e, jnp.log(u) * (e / jnp.where(u == 1.0, 1.0, u - 1.0)))
    return jnp.maximum(x, 0.0) + log1p_e


_SQRT_HALF = 0.7071067811865476
_INV_SQRT_2PI = 0.3989422804014327


def _gelu(x):
    return x * (lax.erf(x * _SQRT_HALF) + 1.0) * 0.5


def _gelu_grad(x):
    return 0.5 * (1.0 + lax.erf(x * _SQRT_HALF)) + x * jnp.exp(-0.5 * x * x) * _INV_SQRT_2PI


def _dot(a, b, dims):
    return lax.dot_general(a, b, (dims, ((), ())), preferred_element_type=F32)


_NN = ((1,), (0,))
_NT = ((1,), (1,))
_TN = ((0,), (0,))


def _split3(x):
    hi = x.astype(BF16)
    r1 = x - hi.astype(F32)
    mid = r1.astype(BF16)
    lo = (r1 - mid.astype(F32)).astype(BF16)
    return hi, mid, lo


def _dot_exact_rhs(x, e, dims):
    hi, mid, lo = _split3(x)
    return _dot(hi, e, dims) + _dot(mid, e, dims) + _dot(lo, e, dims)


def _dot_exact_lhs(e, x, dims):
    hi, mid, lo = _split3(x)
    return _dot(e, hi, dims) + _dot(e, mid, dims) + _dot(e, lo, dims)


def _tri(lower):
    r = lax.broadcasted_iota(jnp.int32, (CHUNK, CHUNK), 0)
    c = lax.broadcasted_iota(jnp.int32, (CHUNK, CHUNK), 1)
    return (r >= c) if lower else (r <= c)


def _matmul(a, b, *, mode, tm, tn, tk, out_dtypes, name, epilogue=None, extras=(), extra_specs=(), j_outer=False, deps=(),
            carry=False):
    if mode == "nn":
        (m, k), (_, n) = a.shape, b.shape
    elif mode == "nt":
        (m, k), (n, _) = a.shape, b.shape
    else:
        (k, m), (_, n) = a.shape, b.shape
    assert m % tm == 0 and n % tn == 0 and k % tk == 0, (name, m, n, k, tm, tn, tk)
    nk = k // tk
    n_extra, n_out = len(extras), len(out_dtypes)
    first_out = 2 + n_extra + len(deps)
    dims = {"nn": _NN, "nt": _NT, "tn": _TN}[mode]
    if epilogue is None:
        def epilogue(acc, ex, outs):
            outs[0][...] = acc.astype(outs[0].dtype)

    def body(*refs):
        a_ref, b_ref = refs[0], refs[1]
        ex_refs = refs[2:2 + n_extra]
        outs = refs[first_out:first_out + n_out]
        p = _dot(a_ref[...], b_ref[...], dims)
        if nk == 1:
            epilogue(p, ex_refs, outs)
        else:
            acc_ref = refs[first_out + n_out]
            kk = pl.program_id(2)

            @pl.when(kk == 0)
            def _():
                acc_ref[...] = p

            @pl.when(kk > 0)
            def _():
                acc_ref[...] += p

            @pl.when(kk == nk - 1)
            def _():
                epilogue(acc_ref[...], ex_refs, outs)

    if j_outer:
        grid = (n // tn, m // tm, nk)
        ij = lambda g0, g1: (g1, g0)
    else:
        grid = (m // tm, n // tn, nk)
        ij = lambda g0, g1: (g0, g1)

    def wrap(fn):
        return lambda g0, g1, kk: fn(*ij(g0, g1), kk)

    if mode == "nn":
        a_spec = pl.BlockSpec((tm, tk), wrap(lambda i, j, kk: (i, kk)))
        b_spec = pl.BlockSpec((tk, tn), wrap(lambda i, j, kk: (kk, j)))
        a_blk, b_blk = (tm, tk), (tk, tn)
    elif mode == "nt":
        a_spec = pl.BlockSpec((tm, tk), wrap(lambda i, j, kk: (i, kk)))
        b_spec = pl.BlockSpec((tn, tk), wrap(lambda i, j, kk: (j, kk)))
        a_blk, b_blk = (tm, tk), (tn, tk)
    else:
        a_spec = pl.BlockSpec((tk, tm), wrap(lambda i, j, kk: (kk, i)))
        b_spec = pl.BlockSpec((tk, tn), wrap(lambda i, j, kk: (kk, j)))
        a_blk, b_blk = (tk, tm), (tk, tn)
    ex_specs = [pl.BlockSpec(shape, wrap(lambda i, j, kk, f=f: f(i, j))) for shape, f in extra_specs]
    outs = [o if isinstance(o, tuple) else ((m, n), o, (tm, tn), lambda i, j: (i, j)) for o in out_dtypes]
    out_spec = [pl.BlockSpec(blk_shape, wrap(lambda i, j, kk, f=f: f(i, j))) for _, _, blk_shape, f in outs]
    out_shape = [jax.ShapeDtypeStruct(shape, dt) for shape, dt, _, _ in outs]
    blk = (_nbytes(a_blk, a.dtype) + _nbytes(b_blk, b.dtype) + sum(_nbytes(s, F32) for s, _ in extra_specs)
           + sum(_nbytes(blk_shape, dt) for _, dt, blk_shape, _ in outs) + _nbytes((tm, tn), F32))
    order = ("arbitrary",) * 3 if carry else ("parallel", "parallel", "arbitrary")
    res = pl.pallas_call(
        body, name=name, grid=grid,
        in_specs=[a_spec, b_spec] + ex_specs + [_HBM] * len(deps), out_specs=out_spec, out_shape=out_shape,
        scratch_shapes=[pltpu.VMEM((tm, tn), F32)] if nk > 1 else [],
        compiler_params=_params(order, blk),
    )(a, b, *extras, *deps)
    return res[0] if n_out == 1 else res


ROW_TILE = 256


def _row_spec(width, col_block=0, tile=ROW_TILE):
    return pl.BlockSpec((tile, width), lambda i, cb=col_block: (i, cb))


def _vec_spec(width, col_block=0):
    return pl.BlockSpec((1, width), lambda i, cb=col_block: (0, cb))


def _rms_fwd(x, g, name, deps=()):
    t = x.shape[0]

    def body(x_ref, g_ref, *rest):
        h_ref = rest[-1]
        xv = x_ref[...]
        r = lax.rsqrt(jnp.mean(xv * xv, axis=-1, keepdims=True) + NORM_EPS)
        h_ref[...] = (xv * r * g_ref[...]).astype(BF16)

    return pl.pallas_call(
        body, name=name, grid=(t // ROW_TILE,),
        in_specs=[_row_spec(D_MODEL), _vec_spec(D_MODEL)] + [_HBM] * len(deps), out_specs=_row_spec(D_MODEL),
        out_shape=jax.ShapeDtypeStruct((t, D_MODEL), BF16),
        compiler_params=_params(("parallel",), 3 * _nbytes((ROW_TILE, D_MODEL), F32)),
    )(x, g, *deps)


def _rms_scale(xv):
    r = lax.rsqrt(jnp.mean(xv * xv, axis=-1, keepdims=True) + NORM_EPS)
    return r, xv * r


def _rms_pullback(xv, g, dh):
    r, xh = _rms_scale(xv)
    dyg = dh * g
    return r * (dyg - xh * jnp.mean(dyg * xh, axis=-1, keepdims=True)), jnp.sum(dh * xh, axis=0, keepdims=True)


def _first_row_tile():
    return pl.program_id(0) == 0


def _residual_rms_epilogue(acc, ex, outs):
    x1 = acc + ex[0][...]
    outs[0][...] = x1
    _, xh = _rms_scale(x1)
    outs[1][...] = (xh * ex[1][...]).astype(BF16)


def _loss_epilogue(acc, ex, outs):
    dx_ref, dxb_ref, gg_ref, sq_ref, tot_ref = outs
    gv = ex[1][...]
    r, xh = _rms_scale(acc + ex[0][...])
    err = xh * gv - ex[2][...]
    dy = err * (1.0 / D_MODEL)
    dyg = dy * gv
    dx = r * (dyg - xh * jnp.mean(dyg * xh, axis=-1, keepdims=True))
    dx_ref[...] = dx
    dxb_ref[...] = dx.astype(BF16)

    @pl.when(_first_row_tile())
    def _():
        gg_ref[...] = jnp.zeros_like(gg_ref)
        sq_ref[...] = jnp.zeros_like(sq_ref)

    gg_ref[...] += jnp.sum(dy * xh, axis=0, keepdims=True)
    sq_ref[...] += jnp.sum(err * err, axis=0, keepdims=True)
    tot_ref[...] = jnp.broadcast_to(jnp.sum(sq_ref[...], axis=1, keepdims=True) * (0.5 / D_MODEL), tot_ref.shape)


def _rms_bwd_epilogue(dh, ex, outs):
    dx, gg = _rms_pullback(ex[0][...], ex[1][...], dh)
    dx = dx + ex[2][...]
    outs[0][...] = dx
    if len(outs) == 3:
        outs[1][...] = dx.astype(BF16)

    @pl.when(_first_row_tile())
    def _():
        outs[-1][...] = jnp.zeros_like(outs[-1])

    outs[-1][...] += gg


def _merge_epilogue(acc, ex, outs):
    outs[0][...] = acc
    ga = _sigmoid(ex[1][...] + ex[3][...])
    gb = _sigmoid(ex[2][...] + ex[4][...])
    outs[1][...] = (ga * ex[0][...] + gb * acc).astype(BF16)


def _merge_bwd_epilogue(dm, ex, outs):
    dpa_ref, dpb_ref, dgl_ref, gb_ref = outs
    ga = _sigmoid(ex[2][...] + ex[4][...])
    gb = _sigmoid(ex[3][...] + ex[5][...])
    dpa_ref[...] = (dm * ga).astype(BF16)
    dpb_ref[...] = (dm * gb).astype(BF16)
    dla = dm * ex[0][...] * ga * (1.0 - ga)
    dlb = dm * ex[1][...] * gb * (1.0 - gb)
    dgl_ref[:, :D_MODEL] = dla.astype(BF16)
    dgl_ref[:, D_MODEL:] = dlb.astype(BF16)

    @pl.when(_first_row_tile())
    def _():
        gb_ref[...] = jnp.zeros_like(gb_ref)

    gb_ref[:, :D_MODEL] += jnp.sum(dla, axis=0, keepdims=True)
    gb_ref[:, D_MODEL:] += jnp.sum(dlb, axis=0, keepdims=True)


GMLP_TILE = 512
GMLP_NC = GMLP_TILE // CHUNK


def _gmlp_common(u_pre, v_pre, vg, vb):
    u = _gelu(u_pre)
    v = _gelu(v_pre)
    mu = jnp.mean(v, axis=-1, keepdims=True)
    vc = v - mu
    rstd = lax.rsqrt(jnp.mean(vc * vc, axis=-1, keepdims=True) + NORM_EPS)
    vh = vc * rstd
    vn = vh * vg + vb
    return u, vh, vn, rstd


def _chunks_to_lanes(x, g):
    return jnp.concatenate([x[c * CHUNK:(c + 1) * CHUNK, g * CHUNK:(g + 1) * CHUNK] for c in range(GMLP_NC)], axis=1)


def _gmlp_fwd(proj, vg, vb, wsp, bsp_t, name):
    t = proj.shape[0]

    def body(u_ref, v_ref, vg_ref, vb_ref, w_ref, b_ref, ya_ref):
        u, _, vn, _ = _gmlp_common(u_ref[...], v_ref[...], vg_ref[...], vb_ref[...])
        mask = _tri(True)
        bt = b_ref[...]
        for g in range(GROUPS):
            w = jnp.where(mask, w_ref[g], 0.0).astype(BF16)
            vcat = _chunks_to_lanes(vn, g).astype(BF16)
            s = _dot(w, vcat, _NN) + bt[:, g:g + 1]
            for c in range(GMLP_NC):
                rows, cols = slice(c * CHUNK, (c + 1) * CHUNK), slice(g * CHUNK, (g + 1) * CHUNK)
                ya_ref[rows, cols] = (u[rows, cols] * s[:, c * CHUNK:(c + 1) * CHUNK]).astype(BF16)

    return pl.pallas_call(
        body, name=name, grid=(t // GMLP_TILE,),
        in_specs=[_row_spec(D_MODEL, 0, GMLP_TILE), _row_spec(D_MODEL, 1, GMLP_TILE), _vec_spec(D_MODEL),
                  _vec_spec(D_MODEL), pl.BlockSpec((GROUPS, CHUNK, CHUNK), lambda i: (0, 0, 0)),
                  pl.BlockSpec((CHUNK, GROUPS), lambda i: (0, 0))],
        out_specs=_row_spec(D_MODEL, 0, GMLP_TILE),
        out_shape=jax.ShapeDtypeStruct((t, D_MODEL), BF16),
        compiler_params=_params(("parallel",), 3 * _nbytes((GMLP_TILE, D_MODEL), F32)),
    )(proj, proj, vg, vb, wsp, bsp_t)


def _gmlp_bwd(proj, dya, vg, vb, wsp, bsp_t, dproj, name):
    t = proj.shape[0]

    def body(u_ref, v_ref, dya_ref, vg_ref, vb_ref, w_ref, b_ref, dproj_in, duv_ref, gw_ref, gbt_ref, gvg_ref, gvb_ref,
             dvn_scr, du_scr):
        del dproj_in
        u_pre, v_pre = u_ref[...], v_ref[...]
        vgv = vg_ref[...]
        u, vh, vn, rstd = _gmlp_common(u_pre, v_pre, vgv, vb_ref[...])
        dya = dya_ref[...]
        mask = _tri(True)
        bt = b_ref[...]
        first = pl.program_id(0) == 0

        @pl.when(first)
        def _():
            gw_ref[...] = jnp.zeros_like(gw_ref)
            gbt_ref[...] = jnp.zeros_like(gbt_ref)
            gvg_ref[...] = jnp.zeros_like(gvg_ref)
            gvb_ref[...] = jnp.zeros_like(gvb_ref)

        lane = lax.broadcasted_iota(jnp.int32, (CHUNK, GROUPS), 1)
        gbt = jnp.zeros((CHUNK, GROUPS), F32)
        for g in range(GROUPS):
            w = jnp.where(mask, w_ref[g], 0.0).astype(BF16)
            vcat = _chunks_to_lanes(vn, g).astype(BF16)
            s = _dot(w, vcat, _NN) + bt[:, g:g + 1]
            ds = _chunks_to_lanes(dya * u, g)
            gbt = jnp.where(lane == g, jnp.sum(ds, axis=1, keepdims=True), gbt)
            dsb = ds.astype(BF16)
            gw_ref[g] += jnp.where(mask, _dot(dsb, vcat, _NT), 0.0)
            dv = _dot(w, dsb, _TN)
            for c in range(GMLP_NC):
                rows, cols = slice(c * CHUNK, (c + 1) * CHUNK), slice(g * CHUNK, (g + 1) * CHUNK)
                dvn_scr[rows, cols] = dv[:, c * CHUNK:(c + 1) * CHUNK]
                du_scr[rows, cols] = dya[rows, cols] * s[:, c * CHUNK:(c + 1) * CHUNK]
        gbt_ref[...] += gbt
        dvn = dvn_scr[...]
        gvg_ref[...] += jnp.sum(dvn * vh, axis=0, keepdims=True)
        gvb_ref[...] += jnp.sum(dvn, axis=0, keepdims=True)
        dvh = dvn * vgv
        dv = rstd * (dvh - jnp.mean(dvh, axis=-1, keepdims=True) - vh * jnp.mean(dvh * vh, axis=-1, keepdims=True))
        duv_ref[:, :D_MODEL] = (du_scr[...] * _gelu_grad(u_pre)).astype(BF16)
        duv_ref[:, D_MODEL:] = (dv * _gelu_grad(v_pre)).astype(BF16)

    return pl.pallas_call(
        body, name=name, grid=(t // GMLP_TILE,),
        in_specs=[_row_spec(D_MODEL, 0, GMLP_TILE), _row_spec(D_MODEL, 1, GMLP_TILE), _row_spec(D_MODEL, 0, GMLP_TILE),
                  _vec_spec(D_MODEL), _vec_spec(D_MODEL), pl.BlockSpec((GROUPS, CHUNK, CHUNK), lambda i: (0, 0, 0)),
                  pl.BlockSpec((CHUNK, GROUPS), lambda i: (0, 0)), pl.BlockSpec(memory_space=pl.ANY)],
        out_specs=[_row_spec(2 * D_MODEL, 0, GMLP_TILE), pl.BlockSpec((GROUPS, CHUNK, CHUNK), lambda i: (0, 0, 0)),
                   pl.BlockSpec((CHUNK, GROUPS), lambda i: (0, 0)), _vec_spec(D_MODEL), _vec_spec(D_MODEL)],
        out_shape=[jax.ShapeDtypeStruct(dproj.shape, BF16), jax.ShapeDtypeStruct((GROUPS, CHUNK, CHUNK), F32),
                   jax.ShapeDtypeStruct((CHUNK, GROUPS), F32), jax.ShapeDtypeStruct((1, D_MODEL), F32),
                   jax.ShapeDtypeStruct((1, D_MODEL), F32)],
        scratch_shapes=[pltpu.VMEM((GMLP_TILE, D_MODEL), F32), pltpu.VMEM((GMLP_TILE, D_MODEL), F32)],
        input_output_aliases={7: 0},
        compiler_params=_params(("arbitrary",), 6 * _nbytes((GMLP_TILE, D_MODEL), F32)),
    )(proj, proj, dya, vg, vb, wsp, bsp_t, dproj)


CONV_TILE = 512
CONV_COLS = 1024
CONV_RB = 32
HALO = SUBLANES


def _conv_fwd(proj, cw, cb, name):
    t = proj.shape[0]
    nj = CONV_DIM // CONV_COLS
    xcb = COL_XBC // CONV_COLS
    rb = CONV_TILE // HALO

    def body(x_ref, prev_ref, cw_ref, cb_ref, pre_ref, xc_ref):
        i = pl.program_id(1)
        cw_v = cw_ref[...]
        cb_v = cb_ref[...]
        for b in range(CONV_TILE // CONV_RB):
            if b == 0:
                ext = jnp.concatenate([jnp.where(i > 0, prev_ref[...], 0.0), x_ref[:CONV_RB, :]], axis=0)
            else:
                ext = x_ref[b * CONV_RB - HALO:(b + 1) * CONV_RB, :]
            pre = cb_v + cw_v[CONV_WIDTH - 1:CONV_WIDTH, :] * ext[HALO:, :]
            for k in range(CONV_WIDTH - 1):
                back = CONV_WIDTH - 1 - k
                pre = pre + cw_v[k:k + 1, :] * pltpu.roll(ext, back, 0)[HALO:, :]
            pre_ref[b * CONV_RB:(b + 1) * CONV_RB, :] = pre
            xc_ref[b * CONV_RB:(b + 1) * CONV_RB, :] = pre * _sigmoid(pre)

    tile = pl.BlockSpec((CONV_TILE, CONV_COLS), lambda j, i: (i, j))
    return pl.pallas_call(
        body, name=name, grid=(nj, t // CONV_TILE),
        in_specs=[pl.BlockSpec((CONV_TILE, CONV_COLS), lambda j, i: (i, xcb + j)),
                  pl.BlockSpec((HALO, CONV_COLS), lambda j, i: (jnp.maximum(i * rb - 1, 0), xcb + j)),
                  pl.BlockSpec((CONV_WIDTH, CONV_COLS), lambda j, i: (0, j)),
                  pl.BlockSpec((1, CONV_COLS), lambda j, i: (0, j))],
        out_specs=[tile, tile],
        out_shape=[jax.ShapeDtypeStruct((t, CONV_DIM), F32), jax.ShapeDtypeStruct((t, CONV_DIM), F32)],
        compiler_params=_params(("parallel", "parallel"), 4 * _nbytes((CONV_TILE, CONV_COLS), F32)),
    )(proj, proj, cw, cb)


def _fold_rows(v):
    out = v[:SUBLANES]
    for r in range(1, v.shape[0] // SUBLANES):
        out = out + v[r * SUBLANES:(r + 1) * SUBLANES]
    return out


def _conv_bwd(proj, pre, dxc, cw, dproj, name):
    t = proj.shape[0]
    nj = CONV_DIM // CONV_COLS
    ni = t // CONV_TILE
    xcb = COL_XBC // CONV_COLS
    rb = CONV_TILE // HALO
    last_rb = t // HALO - 1

    def body(x_ref, p_ref, pnext_ref, d_ref, dnext_ref, cw_ref, dproj_in, dx_ref, gw_ref, gb_ref):
        del dproj_in
        i = pl.program_id(1)
        cw_v = cw_ref[...]

        def dpre_of(p, d):
            sg = _sigmoid(p)
            return d * sg * (1.0 + p * (1.0 - sg))

        @pl.when(i == 0)
        def _():
            gw_ref[...] = jnp.zeros_like(gw_ref)
            gb_ref[...] = jnp.zeros_like(gb_ref)

        head = dpre_of(pnext_ref[...], jnp.where(i < ni - 1, dnext_ref[...], 0.0))
        gb_acc = jnp.zeros((SUBLANES, CONV_COLS), F32)
        gw_acc = [jnp.zeros((SUBLANES, CONV_COLS), F32) for _ in range(CONV_WIDTH)]
        for b in reversed(range(CONV_TILE // CONV_RB)):
            rows = slice(b * CONV_RB, (b + 1) * CONV_RB)
            cur = dpre_of(p_ref[rows, :], d_ref[rows, :])
            ext = jnp.concatenate([cur, head], axis=0)
            xv = x_ref[rows, :]
            dx = None
            for k in range(CONV_WIDTH):
                shift = CONV_WIDTH - 1 - k
                win = cur if shift == 0 else pltpu.roll(ext, CONV_RB + HALO - shift, 0)[:CONV_RB, :]
                term = cw_v[k:k + 1, :] * win
                dx = term if dx is None else dx + term
                gw_acc[k] = gw_acc[k] + _fold_rows(win * xv)
            dx_ref[rows, :] = dx.astype(BF16)
            gb_acc = gb_acc + _fold_rows(cur)
            head = cur[:HALO]
        gb_ref[...] += jnp.sum(gb_acc, axis=0, keepdims=True)
        for k in range(CONV_WIDTH):
            gw_ref[k:k + 1, :] += jnp.sum(gw_acc[k], axis=0, keepdims=True)

    tile = pl.BlockSpec((CONV_TILE, CONV_COLS), lambda j, i: (i, j))
    after = pl.BlockSpec((HALO, CONV_COLS), lambda j, i: (jnp.minimum((i + 1) * rb, last_rb), j))
    return pl.pallas_call(
        body, name=name, grid=(nj, ni),
        in_specs=[pl.BlockSpec((CONV_TILE, CONV_COLS), lambda j, i: (i, xcb + j)), tile, after, tile, after,
                  pl.BlockSpec((CONV_WIDTH, CONV_COLS), lambda j, i: (0, j)),
                  pl.BlockSpec(memory_space=pl.ANY)],
        out_specs=[pl.BlockSpec((CONV_TILE, CONV_COLS), lambda j, i: (i, xcb + j)),
                   pl.BlockSpec((CONV_WIDTH, CONV_COLS), lambda j, i: (0, j)),
                   pl.BlockSpec((1, CONV_COLS), lambda j, i: (0, j))],
        out_shape=[jax.ShapeDtypeStruct(dproj.shape, BF16), jax.ShapeDtypeStruct((CONV_WIDTH, CONV_DIM), F32),
                   jax.ShapeDtypeStruct((1, CONV_DIM), F32)],
        input_output_aliases={6: 0},
        compiler_params=_params(("parallel", "arbitrary"), 4 * _nbytes((CONV_TILE, CONV_COLS), F32)),
    )(proj, pre, pre, dxc, dxc, cw, dproj)


def _ssd_decays(dt_raw, dtb, alog, e_bf, tril_bf):
    dtv = _softplus(dt_raw + dtb)
    a = -jnp.exp(alog)
    cs = _dot_exact_lhs(tril_bf, dtv * a, _NN)
    cs_last = cs[CHUNK - 1:CHUNK, :]
    stack = jnp.concatenate([dtv, jnp.exp(cs), jnp.exp(cs_last - cs)], axis=0)
    full = _head_expand(stack, e_bf)
    return dtv, a, cs, full[:CHUNK], full[CHUNK:2 * CHUNK], full[2 * CHUNK:]


def _split2(x):
    hi = x.astype(BF16)
    return hi, (x - hi.astype(F32)).astype(BF16)


def _head_expand(x, e_bf):
    hi, mid = _split2(x)
    return _dot(hi, e_bf, _NN) + _dot(mid, e_bf, _NN)


def _head_sums(x, e_bf):
    hi, mid = _split2(x)
    return _dot(hi, e_bf, _NT) + _dot(mid, e_bf, _NT)


def _head_mats(cs, cs_t, cb, h, mask):
    seg = cs[:, h:h + 1] - cs_t[h:h + 1, :]
    lmat = jnp.exp(jnp.where(mask, seg, -jnp.inf))
    return lmat, cb * lmat


def _ssd_fwd(xc, proj, dt_raw, dtb, alog, dskip_full, ng, e_bf, name):
    t = xc.shape[0]
    nc = t // CHUNK
    zcb = COL_Z // D_INNER

    def body(xc_ref, z_ref, dt_ref, dtb_ref, alog_ref, dsk_ref, ng_ref, e_ref, y_ref, yb_ref, sprev_ref, s_scr):
        @pl.when(pl.program_id(0) == 0)
        def _():
            s_scr[...] = jnp.zeros_like(s_scr)

        mask = _tri(True)
        tril_bf = mask.astype(BF16)
        e_v = e_ref[...]
        _, _, cs, dt_full, ecs_full, decay_full = _ssd_decays(dt_ref[...], dtb_ref[...], alog_ref[...], e_v, tril_bf)
        cs_t = cs.T
        sprev_ref[0] = s_scr[...]
        for g in range(GROUPS):
            gc = slice(g * GROUP_W, (g + 1) * GROUP_W)
            xs = xc_ref[:, gc]
            xdt = xs * dt_full[:, gc]
            xdt_b = xdt.astype(BF16)
            xdec = (xdt * decay_full[:, gc]).astype(BF16)
            bg = xc_ref[:, D_INNER + g * D_STATE:D_INNER + (g + 1) * D_STATE].astype(BF16)
            cg = xc_ref[:, D_INNER + GROUPS * D_STATE + g * D_STATE:D_INNER + GROUPS * D_STATE + (g + 1) * D_STATE].astype(BF16)
            cb = _dot(cg, bg, _NT)
            s_prev = s_scr[:, gc]
            y_off = ecs_full[:, gc] * _dot(cg, s_prev.astype(BF16), _NN)
            s_scr[:, gc] = s_prev * ecs_full[CHUNK - 1:CHUNK, gc] + _dot(bg, xdec, _TN)
            parts = []
            for r in range(GROUP_W // HEAD_DIM):
                h = g * (GROUP_W // HEAD_DIM) + r
                _, m = _head_mats(cs, cs_t, cb, h, mask)
                parts.append(_dot(m.astype(BF16), xdt_b[:, r * HEAD_DIM:(r + 1) * HEAD_DIM], _NN))
            yg = jnp.concatenate(parts, axis=1) + y_off + dsk_ref[:, gc] * xs
            y_ref[:, gc] = yg
            zv = z_ref[:, gc]
            ygate = yg * (zv * _sigmoid(zv))
            rstd = lax.rsqrt(jnp.mean(ygate * ygate, axis=-1, keepdims=True) + NORM_EPS)
            yb_ref[:, gc] = (ygate * rstd * ng_ref[:, gc]).astype(BF16)

    vec = lambda w: pl.BlockSpec((1, w), lambda i: (0, 0))
    blk = _nbytes((CHUNK, CONV_DIM), F32) + 3 * _nbytes((CHUNK, D_INNER), F32) + _nbytes((D_STATE, D_INNER), F32)
    return pl.pallas_call(
        body, name=name, grid=(nc,),
        in_specs=[pl.BlockSpec((CHUNK, CONV_DIM), lambda i: (i, 0)), pl.BlockSpec((CHUNK, D_INNER), lambda i: (i, zcb)),
                  pl.BlockSpec((CHUNK, DT_PAD), lambda i: (i, 0)), vec(DT_PAD), vec(DT_PAD), vec(D_INNER), vec(D_INNER),
                  pl.BlockSpec((DT_PAD, D_INNER), lambda i: (0, 0))],
        out_specs=[pl.BlockSpec((CHUNK, D_INNER), lambda i: (i, 0)), pl.BlockSpec((CHUNK, D_INNER), lambda i: (i, 0)),
                   pl.BlockSpec((1, D_STATE, D_INNER), lambda i: (i, 0, 0))],
        out_shape=[jax.ShapeDtypeStruct((t, D_INNER), F32), jax.ShapeDtypeStruct((t, D_INNER), BF16),
                   jax.ShapeDtypeStruct((nc, D_STATE, D_INNER), F32)],
        scratch_shapes=[pltpu.VMEM((D_STATE, D_INNER), F32)],
        compiler_params=_params(("arbitrary",), blk),
    )(xc, proj, dt_raw, dtb, alog, dskip_full, ng, e_bf)


def _ssd_bwd(dyb, y, xc, proj, dt_raw, sprev, dtb, alog, dskip_full, ng, e_bf, dproj, name):
    t = xc.shape[0]
    nc = t // CHUNK
    zcb = COL_Z // D_INNER
    hpg = GROUP_W // HEAD_DIM
    rev = lambda i: nc - 1 - i

    def body(dyb_ref, y_ref, xc_ref, z_ref, dt_ref, sprev_ref, dtb_ref, alog_ref, dsk_ref, ng_ref, e_ref, dproj_in,
             dz_ref, dxc_ref, ddt_ref, gng_ref, gdsk_ref, galog_ref, gdtb_ref, ds_scr, sums_scr):
        del dproj_in

        @pl.when(pl.program_id(0) == 0)
        def _():
            ds_scr[...] = jnp.zeros_like(ds_scr)
            gng_ref[...] = jnp.zeros_like(gng_ref)
            gdsk_ref[...] = jnp.zeros_like(gdsk_ref)
            galog_ref[...] = jnp.zeros_like(galog_ref)
            gdtb_ref[...] = jnp.zeros_like(gdtb_ref)

        mask = _tri(True)
        tril_bf = mask.astype(BF16)
        triu_bf = _tri(False).astype(BF16)
        e_v = e_ref[...]
        dt_in = dt_ref[...] + dtb_ref[...]
        dtv, a, cs, dt_full, ecs_full, decay_full = _ssd_decays(dt_ref[...], dtb_ref[...], alog_ref[...], e_v, tril_bf)
        cs_t = cs.T

        lane_h = lax.broadcasted_iota(jnp.int32, (CHUNK, DT_PAD), 1)
        sub_h = lax.broadcasted_iota(jnp.int32, (DT_PAD, CHUNK), 0)
        dcs_rows = jnp.zeros((CHUNK, DT_PAD), F32)
        dcs_cols_t = jnp.zeros((DT_PAD, CHUNK), F32)
        last_cols, dsk_cols = [], []
        for g in range(GROUPS):
            gc = slice(g * GROUP_W, (g + 1) * GROUP_W)
            b_cols = slice(D_INNER + g * D_STATE, D_INNER + (g + 1) * D_STATE)
            c_cols = slice(D_INNER + GROUPS * D_STATE + g * D_STATE, D_INNER + GROUPS * D_STATE + (g + 1) * D_STATE)
            xs = xc_ref[:, gc]
            xdt = xs * dt_full[:, gc]
            xdt_b = xdt.astype(BF16)
            xdec = xdt * decay_full[:, gc]
            xdec_b = xdec.astype(BF16)
            zv = z_ref[:, gc]
            sg = _sigmoid(zv)
            gate = zv * sg
            yv = y_ref[:, gc]
            dybv = dyb_ref[:, gc]
            ygate = yv * gate
            rstd = lax.rsqrt(jnp.mean(ygate * ygate, axis=-1, keepdims=True) + NORM_EPS)
            yn = ygate * rstd
            gng_ref[:, gc] += jnp.sum(dybv * yn, axis=0, keepdims=True)
            dyn = dybv * ng_ref[:, gc]
            dyg = rstd * (dyn - yn * jnp.mean(dyn * yn, axis=-1, keepdims=True))
            dz_ref[:, gc] = (dyg * yv * sg * (1.0 + zv * (1.0 - sg))).astype(BF16)
            dy = dyg * gate
            dy_b = dy.astype(BF16)
            dyo = dy * ecs_full[:, gc]
            dyo_b = dyo.astype(BF16)
            dsk_cols.append(jnp.sum(dy * xs, axis=0, keepdims=True))

            bg = xc_ref[:, b_cols].astype(BF16)
            cg = xc_ref[:, c_cols].astype(BF16)
            s_prev = sprev_ref[0, :, gc]
            s_prev_b = s_prev.astype(BF16)
            dsg = ds_scr[:, gc]
            dsg_b = dsg.astype(BF16)
            cb = _dot(cg, bg, _NT)
            c_s = _dot(cg, s_prev_b, _NN)
            b_ds = _dot(bg, dsg_b, _NN)
            dcb = jnp.zeros((CHUNK, CHUNK), F32)
            parts = []
            for r in range(hpg):
                h = g * hpg + r
                hc = slice(r * HEAD_DIM, (r + 1) * HEAD_DIM)
                lmat, m = _head_mats(cs, cs_t, cb, h, mask)
                dm = _dot(dy_b[:, hc], xdt_b[:, hc], _NT)
                parts.append(_dot(m.astype(BF16), dy_b[:, hc], _TN))
                dcb = dcb + dm * lmat
                w = dm * m
                dcs_rows = jnp.where(lane_h == h, jnp.sum(w, axis=1, keepdims=True), dcs_rows)
                dcs_cols_t = jnp.where(sub_h == h, jnp.sum(w, axis=0, keepdims=True), dcs_cols_t)
            dxdt = jnp.concatenate(parts, axis=1) + decay_full[:, gc] * b_ds
            dcb_b = dcb.astype(BF16)
            dxc_ref[:, c_cols] = _dot(dcb_b, bg, _NN) + _dot(dyo_b, s_prev_b, _NT)
            dxc_ref[:, b_cols] = _dot(dcb_b, cg, _TN) + _dot(xdec_b, dsg_b, _NT)
            cdec = ecs_full[CHUNK - 1:CHUNK, gc]
            ds_scr[:, gc] = _dot(cg, dyo_b, _TN) + cdec * dsg
            dxc_ref[:, gc] = dxdt * dt_full[:, gc] + dsk_ref[:, gc] * dy
            dec_prod = xdec * b_ds
            sums_scr[:CHUNK, gc] = dyo * c_s - dec_prod
            sums_scr[CHUNK:, gc] = dxdt * xs
            last_cols.append(jnp.sum(dec_prod, axis=0, keepdims=True) + cdec * jnp.sum(dsg * s_prev, axis=0, keepdims=True))
        t_sums = _head_sums(sums_scr[...], e_v)
        tail = jnp.concatenate([jnp.concatenate(last_cols, axis=1), jnp.concatenate(dsk_cols, axis=1),
                                jnp.zeros((SUBLANES - 2, D_INNER), F32)], axis=0)
        t_tail = _dot_exact_rhs(tail, e_v, _NT)
        gdsk_ref[...] += t_tail[1:2, :]
        row = lax.broadcasted_iota(jnp.int32, (CHUNK, DT_PAD), 0)
        dcs = dcs_rows - dcs_cols_t.T + t_sums[:CHUNK] + jnp.where(row == CHUNK - 1, t_tail[0:1, :], 0.0)
        dda = _dot_exact_lhs(triu_bf, dcs, _NN)
        galog_ref[...] += jnp.sum(dda * dtv, axis=0, keepdims=True) * a
        ddt = dda * a + t_sums[CHUNK:]
        ddt_raw = jnp.where(lane_h < N_HEADS, ddt * _sigmoid(dt_in), 0.0)
        gdtb_ref[...] += jnp.sum(ddt_raw, axis=0, keepdims=True)
        ddt_ref[...] = ddt_raw.astype(BF16)

    vec = lambda w: pl.BlockSpec((1, w), lambda i: (0, 0))
    blk = (2 * _nbytes((CHUNK, CONV_DIM), F32) + 4 * _nbytes((CHUNK, D_INNER), F32) + 4 * _nbytes((D_STATE, D_INNER), F32))
    return pl.pallas_call(
        body, name=name, grid=(nc,),
        in_specs=[pl.BlockSpec((CHUNK, D_INNER), lambda i: (rev(i), 0)), pl.BlockSpec((CHUNK, D_INNER), lambda i: (rev(i), 0)),
                  pl.BlockSpec((CHUNK, CONV_DIM), lambda i: (rev(i), 0)), pl.BlockSpec((CHUNK, D_INNER), lambda i: (rev(i), zcb)),
                  pl.BlockSpec((CHUNK, DT_PAD), lambda i: (rev(i), 0)), pl.BlockSpec((1, D_STATE, D_INNER), lambda i: (rev(i), 0, 0)),
                  vec(DT_PAD), vec(DT_PAD), vec(D_INNER), vec(D_INNER), pl.BlockSpec((DT_PAD, D_INNER), lambda i: (0, 0)),
                  pl.BlockSpec(memory_space=pl.ANY)],
        out_specs=[pl.BlockSpec((CHUNK, D_INNER), lambda i: (rev(i), zcb)), pl.BlockSpec((CHUNK, CONV_DIM), lambda i: (rev(i), 0)),
                   pl.BlockSpec((CHUNK, DT_PAD), lambda i: (rev(i), 0)), vec(D_INNER), vec(DT_PAD), vec(DT_PAD), vec(DT_PAD)],
        out_shape=[jax.ShapeDtypeStruct(dproj.shape, BF16), jax.ShapeDtypeStruct((t, CONV_DIM), F32),
                   jax.ShapeDtypeStruct((t, DT_PAD), BF16), jax.ShapeDtypeStruct((1, D_INNER), F32),
                   jax.ShapeDtypeStruct((1, DT_PAD), F32), jax.ShapeDtypeStruct((1, DT_PAD), F32),
                   jax.ShapeDtypeStruct((1, DT_PAD), F32)],
        scratch_shapes=[pltpu.VMEM((D_STATE, D_INNER), F32), pltpu.VMEM((2 * CHUNK, D_INNER), F32)],
        input_output_aliases={11: 0},
        compiler_params=_params(("arbitrary",), blk),
    )(dyb, y, xc, proj, dt_raw, sprev, dtb, alog, dskip_full, ng, e_bf, dproj)


def _mesh_pos():
    return lax.axis_index("x"), lax.axis_index("y"), lax.axis_index("c")


def _other_chips(x, y):
    return [(1 - x, y), (x, 1 - y), (1 - x, 1 - y)]


def _all_peers(x, y, c):
    peers = []
    for k in range(1, N_DEV):
        fx, fy, fc = (k >> 2) & 1, (k >> 1) & 1, k & 1
        px, py, pc = x + fx - 2 * x * fx, y + fy - 2 * y * fy, c + fc - 2 * c * fc
        peers.append(((px, py, pc), 4 * px + 2 * py + pc))
    return peers


def _all_gather(shards, name, own_only=()):
    n, n_own = len(shards), len(own_only)

    def body(*refs):
        ins, own_ins = refs[:n], refs[n:n + n_own]
        outs, own_outs = refs[n + n_own:2 * n + n_own], refs[2 * n + n_own:2 * (n + n_own)]
        send_sems, recv_sems, local_sems = refs[2 * (n + n_own):]
        x, y, c = _mesh_pos()
        me, sibling = (x, y, c), (x, y, 1 - c)
        chips = _other_chips(x, y)

        def slot(p):
            return 4 * p[0] + 2 * p[1] + p[2]

        def copy(a, k, block, to, src=None):
            dst = outs[a].at[slot(block)]
            return pltpu.make_async_remote_copy(
                src_ref=dst if src is None else src, dst_ref=dst, send_sem=send_sems.at[a * 7 + k],
                recv_sem=recv_sems.at[a * 7 + k], device_id=to, device_id_type=MESH)

        started = []
        own = []
        for a in range(n_own):
            mine = pltpu.make_async_copy(own_ins[a], own_outs[a].at[slot(me)], local_sems.at[n + a])
            mine.start()
            own.append(mine)
        for a in range(n):
            mine = pltpu.make_async_copy(ins[a], outs[a].at[slot(me)], local_sems.at[a])
            mine.start()
            own.append(mine)
            first = [copy(a, 0, me, sibling, src=ins[a])]
            first += [copy(a, 1 + j, me, (*chip, c), src=ins[a]) for j, chip in enumerate(chips)]
            for cp in first:
                cp.start()
            started += first
        for a in range(n):
            for j, chip in enumerate(chips):
                copy(a, 1 + j, (*chip, c), me).wait_recv()
                fwd = copy(a, 4 + j, (*chip, c), sibling)
                fwd.start()
                started.append(fwd)
        for a in range(n):
            copy(a, 0, sibling, me).wait_recv()
            for j, chip in enumerate(chips):
                copy(a, 4 + j, (*chip, 1 - c), me).wait_recv()
        for cp in started:
            cp.wait_send()
        for mine in own:
            mine.wait()

    return pl.pallas_call(
        body, name=name,
        in_specs=[_HBM] * (n + n_own), out_specs=[_HBM] * (n + n_own),
        out_shape=[jax.ShapeDtypeStruct((N_DEV,) + s.shape, s.dtype) for s in (*shards, *own_only)],
        scratch_shapes=[pltpu.SemaphoreType.DMA((7 * n,)), pltpu.SemaphoreType.DMA((7 * n,)),
                        pltpu.SemaphoreType.DMA((n + n_own,))],
    )(*shards, *own_only)


_SMALL_ROWS = (("norm_mix_g", 8), ("conv_b", 32), ("dt_bias", 1), ("a_log", 1), ("d_skip", 1), ("ssm_norm_g", 16),
               ("v_norm_g", 8), ("v_norm_b", 8), ("w_spatial", 1024), ("b_spatial", 8), ("b_gates", 16), ("norm_mlp_g", 8),
               ("norm_final_g", 8), ("conv_w", 128), ("loss", 1))
_SMALL_PACKED_ROWS = -(-sum(r for _, r in _SMALL_ROWS) // SUBLANES) * SUBLANES


def _small_offsets():
    offs, r = {}, 0
    for name, rows in _SMALL_ROWS:
        offs[name] = r
        r += rows
    return offs


def _rows_from(src_ref, dst_ref, r0):
    k, w = src_ref.shape
    if w <= LANES:
        dst_ref[r0:r0 + k, 0:w] = src_ref[...]
        return
    per = w // LANES
    for i in range(k):
        for j in range(per):
            dst_ref[r0 + i * per + j:r0 + i * per + j + 1, :] = src_ref[i:i + 1, j * LANES:(j + 1) * LANES]


def _rows_to(src_ref, r0, dst_ref):
    k, w = dst_ref.shape
    if w <= LANES:
        dst_ref[...] = src_ref[r0:r0 + k, 0:w]
        return
    per = w // LANES
    for i in range(k):
        for j in range(per):
            dst_ref[i:i + 1, j * LANES:(j + 1) * LANES] = src_ref[r0 + i * per + j:r0 + i * per + j + 1, :]


def _exchange_small(grads, name):
    names = [n for n, _ in _SMALL_ROWS]
    offs = _small_offsets()
    n_in = len(names)

    def body(*refs):
        ins, out_ref = refs[:n_in], refs[n_in]
        packed, send_sems, recv_sems, local_sem = refs[n_in + 1:]
        packed[...] = jnp.zeros_like(packed)
        for n, ref in zip(names, ins):
            _rows_from(ref, packed, offs[n])
        x, y, c = _mesh_pos()
        my_slot = 4 * x + 2 * y + c
        mine = pltpu.make_async_copy(packed, out_ref.at[my_slot], local_sem)
        mine.start()
        copies = []
        for k, (peer, peer_slot) in enumerate(_all_peers(x, y, c)):
            sems = dict(send_sem=send_sems.at[k], recv_sem=recv_sems.at[k], device_id=peer, device_id_type=MESH)
            send = pltpu.make_async_remote_copy(src_ref=packed, dst_ref=out_ref.at[my_slot], **sems)
            send.start()
            copies.append((send, pltpu.make_async_remote_copy(src_ref=packed, dst_ref=out_ref.at[peer_slot], **sems)))
        for send, recv in copies:
            send.wait_send()
            recv.wait_recv()
        mine.wait()

    return pl.pallas_call(
        body, name=name, in_specs=[pl.BlockSpec(memory_space=pltpu.VMEM)] * n_in, out_specs=_HBM,
        out_shape=jax.ShapeDtypeStruct((N_DEV, _SMALL_PACKED_ROWS, LANES), F32),
        scratch_shapes=[pltpu.VMEM((_SMALL_PACKED_ROWS, LANES), F32), pltpu.SemaphoreType.DMA((N_DEV - 1,)),
                        pltpu.SemaphoreType.DMA((N_DEV - 1,)), pltpu.SemaphoreType.DMA],
    )(*[grads[n] for n in names])


def _swap_with_sibling(grads, name):
    n = len(grads)

    def body(*refs):
        ins, outs = refs[:n], refs[n:2 * n]
        send_sems, recv_sems = refs[2 * n:]
        x, y, c = _mesh_pos()
        copies = []
        for a in range(n):
            for k in range(N_CHIP):
                cp = pltpu.make_async_remote_copy(
                    src_ref=ins[a].at[(1 - c) + 2 * k], dst_ref=outs[a].at[k], send_sem=send_sems.at[a * N_CHIP + k],
                    recv_sem=recv_sems.at[a * N_CHIP + k], device_id=(x, y, 1 - c), device_id_type=MESH)
                cp.start()
                copies.append(cp)
        for cp in copies:
            cp.wait()

    return pl.pallas_call(
        body, name=name, in_specs=[_HBM] * n, out_specs=[_HBM] * n,
        out_shape=[jax.ShapeDtypeStruct((N_CHIP,) + g.shape[1:], g.dtype) for g in grads],
        scratch_shapes=[pltpu.SemaphoreType.DMA((N_CHIP * n,)), pltpu.SemaphoreType.DMA((N_CHIP * n,))],
    )(*grads)


_SEM = pl.BlockSpec(memory_space=pltpu.SEMAPHORE)
_IN_HBM = pl.BlockSpec(memory_space=pltpu.HBM)
_EFFECT = pltpu.SideEffectType.DATAFLOW_SIDE_EFFECTING


def _in_hbm(a):
    return pltpu.with_memory_space_constraint(a, pltpu.HBM)


def _gather_copies(ins, lands, send_sems, recv_sems):
    x, y, c = _mesh_pos()
    my_slot = 4 * x + 2 * y + c
    pairs = []
    for a in range(len(ins)):
        for k, (peer, peer_slot) in enumerate(_all_peers(x, y, c)):
            sems = dict(send_sem=send_sems.at[a * (N_DEV - 1) + k], recv_sem=recv_sems.at[a * (N_DEV - 1) + k],
                        device_id=peer, device_id_type=MESH)
            pairs.append((pltpu.make_async_remote_copy(src_ref=ins[a], dst_ref=lands[a].at[my_slot], **sems),
                          pltpu.make_async_remote_copy(src_ref=ins[a], dst_ref=lands[a].at[peer_slot], **sems)))
    return pairs


def _scatter_copies(ins, lands, send_sems, recv_sems):
    x, y, c = _mesh_pos()
    my_chip = 2 * x + y
    pairs = []
    for a in range(len(ins)):
        for j, chip in enumerate(_other_chips(x, y)):
            there = 2 * chip[0] + chip[1]
            sems = dict(send_sem=send_sems.at[a * 3 + j], recv_sem=recv_sems.at[a * 3 + j],
                        device_id=(*chip, c), device_id_type=MESH)
            pairs.append((pltpu.make_async_remote_copy(src_ref=ins[a].at[there], dst_ref=lands[a].at[my_chip], **sems),
                          pltpu.make_async_remote_copy(src_ref=ins[a].at[my_chip], dst_ref=lands[a].at[there], **sems)))
    return pairs


def _split_start(srcs, lands, copies, per_array, name):
    n = len(srcs)

    def body(*refs):
        ins, land_refs = refs[:n], refs[n:2 * n]
        send_sems, recv_sems = refs[2 * n], refs[2 * n + 1]
        token = refs[-1]
        for send, _ in copies(ins, land_refs, send_sems, recv_sems):
            send.start()
        token[...] = jnp.zeros_like(token)

    outs = pl.pallas_call(
        body, name=name,
        out_shape=(pltpu.SemaphoreType.DMA((per_array * n,)), pltpu.SemaphoreType.DMA((per_array * n,)),
                   *[pltpu.HBM(s.shape, s.dtype) for s in srcs], *[pltpu.HBM(l.shape, l.dtype) for l in lands],
                   jax.ShapeDtypeStruct((SUBLANES, LANES), F32)),
        in_specs=[_IN_HBM] * (2 * n),
        out_specs=(_SEM, _SEM, *[_IN_HBM] * (2 * n), pl.BlockSpec(memory_space=pltpu.VMEM)),
        input_output_aliases={i: 2 + i for i in range(2 * n)},
        compiler_params=pltpu.CompilerParams(has_side_effects=_EFFECT),
    )(*[_in_hbm(s) for s in srcs], *[_in_hbm(l) for l in lands])
    return outs[0], outs[1], list(outs[2:2 + n]), list(outs[2 + n:2 + 2 * n]), outs[-1]


def _split_wait(started, copies, after, name):
    send_sems, recv_sems, srcs, lands, _ = started
    n = len(srcs)

    def body(*refs):
        ins, land_refs = refs[:n], refs[n:2 * n]
        for send, recv in copies(ins, land_refs, refs[2 * n], refs[2 * n + 1]):
            send.wait_send()
            recv.wait_recv()

    outs = pl.pallas_call(
        body, name=name,
        out_shape=(*[pltpu.HBM(s.shape, s.dtype) for s in srcs], *[pltpu.HBM(l.shape, l.dtype) for l in lands]),
        in_specs=[_IN_HBM] * (2 * n) + [_SEM, _SEM, _HBM],
        out_specs=[_IN_HBM] * (2 * n),
        input_output_aliases={i: i for i in range(2 * n)},
        compiler_params=pltpu.CompilerParams(has_side_effects=_EFFECT),
    )(*srcs, *lands, send_sems, recv_sems, after)
    return list(outs[:n]), list(outs[n:])


def _ew_block(rows, cols, slots):
    budget = 2 * 1024 * 1024
    br, bc = rows, cols
    while slots * br * bc * 4 > budget:
        if br % 2 == 0 and (br // 2) % (2 * SUBLANES) == 0:
            br //= 2
        elif bc % 2 == 0 and (bc // 2) % LANES == 0:
            bc //= 2
        else:
            break
    return br, bc


def _add_sibling(grads, recv, c_idx, name):
    _, rows, cols = grads.shape
    br, bc = _ew_block(rows, cols, 3)

    def body(c_ref, g_ref, r_ref, out_ref):
        del c_ref
        out_ref[...] = (g_ref[...].astype(F32) + r_ref[...].astype(F32)).astype(out_ref.dtype)

    grid_spec = pltpu.PrefetchScalarGridSpec(
        num_scalar_prefetch=1, grid=(N_CHIP, rows // br, cols // bc),
        in_specs=[pl.BlockSpec((1, br, bc), lambda k, i, j, c_ref: (c_ref[0] + 2 * k, i, j)),
                  pl.BlockSpec((1, br, bc), lambda k, i, j, c_ref: (k, i, j))],
        out_specs=pl.BlockSpec((1, br, bc), lambda k, i, j, c_ref: (k, i, j)))
    return pl.pallas_call(
        body, name=name, grid_spec=grid_spec, out_shape=jax.ShapeDtypeStruct((N_CHIP, rows, cols), grads.dtype),
        compiler_params=_params(("parallel", "parallel", "parallel"), 3 * _nbytes((br, bc), F32)),
    )(c_idx, grads, recv)


def _adam_math(g, w, m, v):
    m2 = ADAM_B1 * m + (1.0 - ADAM_B1) * g
    v2 = ADAM_B2 * v + (1.0 - ADAM_B2) * (g * g)
    m_hat = m2 * (1.0 / (1.0 - ADAM_B1 ** ADAM_STEP))
    v_hat = v2 * (1.0 / (1.0 - ADAM_B2 ** ADAM_STEP))
    return -ADAM_LR * (m_hat / (jnp.sqrt(v_hat) + ADAM_EPS) + ADAM_WD * w), m2, v2


def _adamw(slots, w, m, v, name, own=None, own_slot=None):
    ns, rows, cols = slots.shape
    br, bc = _ew_block(rows, cols, 2 * ns + 7)

    def update(g, w_ref, m_ref, v_ref, g_ref, d_ref, m2_ref, v2_ref):
        g_ref[...] = g
        d_ref[...], m2_ref[...], v2_ref[...] = _adam_math(g, w_ref[...], m_ref[...], v_ref[...])

    out_shape = [jax.ShapeDtypeStruct((rows, cols), F32)] * 4
    params = _params(("parallel", "parallel"), (2 * ns + 7) * _nbytes((br, bc), F32))
    grid = (rows // br, cols // bc)
    if own is None:
        def body(s_ref, *rest):
            g = s_ref[0].astype(F32)
            for k in range(1, ns):
                g = g + s_ref[k].astype(F32)
            update(g, *rest)

        blk = pl.BlockSpec((br, bc), lambda i, j: (i, j))
        return pl.pallas_call(
            body, name=name, grid=grid,
            in_specs=[pl.BlockSpec((ns, br, bc), lambda i, j: (0, i, j)), blk, blk, blk], out_specs=[blk] * 4,
            out_shape=out_shape, compiler_params=params,
        )(slots, w, m, v)

    def body_own(slot_ref, s_ref, o_ref, *rest):
        g = None
        for k in range(ns):
            term = jnp.where(slot_ref[0] == k, o_ref[k].astype(F32), s_ref[k].astype(F32))
            g = term if g is None else g + term
        update(g, *rest)

    blk = pl.BlockSpec((br, bc), lambda i, j, slot_ref: (i, j))
    stack = pl.BlockSpec((ns, br, bc), lambda i, j, slot_ref: (0, i, j))
    grid_spec = pltpu.PrefetchScalarGridSpec(num_scalar_prefetch=1, grid=grid, in_specs=[stack, stack, blk, blk, blk],
                                             out_specs=[blk] * 4)
    return pl.pallas_call(body_own, name=name, grid_spec=grid_spec, out_shape=out_shape, compiler_params=params,
                          )(own_slot, slots, own, w, m, v)


def _adamw_small(all_g, params, extra_shapes, name):
    names = [n for n, _ in _SMALL_ROWS if n in params]
    extras = [n for n, _ in _SMALL_ROWS if n not in params]
    offs = _small_offsets()
    n_p = len(names)

    def body(*refs):
        s_ref = refs[0]
        wmv = refs[1:1 + 3 * n_p]
        outs = refs[1 + 3 * n_p:1 + 7 * n_p]
        extra_refs = refs[1 + 7 * n_p:1 + 7 * n_p + len(extras)]
        summed = refs[-1]
        g = s_ref[0]
        for k in range(1, N_DEV):
            g = g + s_ref[k]
        summed[...] = g
        for i, n in enumerate(names):
            w_ref, m_ref, v_ref = wmv[3 * i:3 * i + 3]
            g_ref, d_ref, m2_ref, v2_ref = outs[4 * i:4 * i + 4]
            _rows_to(summed, offs[n], g_ref)
            d_ref[...], m2_ref[...], v2_ref[...] = _adam_math(g_ref[...], w_ref[...], m_ref[...], v_ref[...])
        for n, ref in zip(extras, extra_refs):
            _rows_to(summed, offs[n], ref)

    flat = [a for n in names for a in params[n]]
    out_shape = [jax.ShapeDtypeStruct(params[n][0].shape, F32) for n in names for _ in range(4)]
    out_shape += [jax.ShapeDtypeStruct(s, F32) for s in extra_shapes]
    vmem = pl.BlockSpec(memory_space=pltpu.VMEM)
    res = pl.pallas_call(
        body, name=name, in_specs=[vmem] * (1 + len(flat)), out_specs=[vmem] * len(out_shape), out_shape=out_shape,
        scratch_shapes=[pltpu.VMEM((_SMALL_PACKED_ROWS, LANES), F32)],
        compiler_params=pltpu.CompilerParams(vmem_limit_bytes=_vmem_limit(_nbytes(all_g.shape, F32))),
    )(all_g, *flat)
    return {n: res[4 * i:4 * i + 4] for i, n in enumerate(names)}, res[4 * n_p:]


def _mm_tiles(mode, m, n, k):
    tn = min(n, 1024)
    if mode == "tn":
        return min(m, 1024), tn, min(k, 2048)
    if k <= 2048:
        return min(m, 1024), tn, k
    if k <= 4096:
        return min(m, 512), tn, k
    return min(m, 1024), tn, 2048


def _local_step(x, target, wts, small, exchange):
    t = x.shape[0]
    w_main_t, w_dt_t = wts["w_main_t"], wts["w_dt_t"]
    bsp_t = small["b_spatial"].T
    pad32 = lambda a: jnp.pad(a, ((0, 0), (0, DT_PAD - N_HEADS)))
    dtb, alog = pad32(small["dt_bias"]), pad32(small["a_log"])
    dskip_full = jnp.repeat(small["d_skip"], HEAD_DIM, axis=1)
    head_of_col = lax.broadcasted_iota(jnp.int32, (DT_PAD, D_INNER), 1) // HEAD_DIM
    e_bf = (head_of_col == lax.broadcasted_iota(jnp.int32, (DT_PAD, D_INNER), 0)).astype(BF16)

    def mm(a, b, mode, name, **kw):
        if mode == "nn":
            m, k, n = a.shape[0], a.shape[1], b.shape[1]
        elif mode == "nt":
            m, k, n = a.shape[0], a.shape[1], b.shape[0]
        else:
            m, k, n = a.shape[1], a.shape[0], b.shape[1]
        tm, tn, tk = _mm_tiles(mode, m, n, k)
        tm = min(tm, kw.pop("max_tm", tm))
        kw.setdefault("out_dtypes", (BF16,) if mode == "tn" else (F32,))
        if "extra_specs" in kw:
            kw["extra_specs"] = kw["extra_specs"](tm, tn)
        return _matmul(a, b, mode=mode, tm=tm, tn=tn, tk=tk, name=name, **kw)

    def out_tile(tm, tn):
        return (((tm, tn), lambda i, j: (i, j)),)

    def row_tiles(n_tiles, *vectors, gate_logits=False):
        def specs(tm, tn):
            out = [((tm, tn), lambda i, j: (i, j))] * n_tiles
            if gate_logits:
                out += [((tm, D_MODEL), lambda i, j, cb=COL_GATE // D_MODEL + half: (i, cb)) for half in range(2)]
            return tuple(out) + tuple(((1, w), lambda i, j, cb=cb: (0, cb)) for w, cb in vectors)
        return specs

    vec = lambda w: ((1, w), F32, (1, w), lambda i, j: (0, 0))
    fused_tm = 512

    h = _rms_fwd(x, small["norm_mix_g"], "rms_mix", deps=exchange.begin())
    proj = mm(h, w_main_t, "nt", "proj_main", j_outer=True)
    dt_raw = mm(h, w_dt_t, "nt", "proj_dt")
    y_a = _gmlp_fwd(proj, small["v_norm_g"], small["v_norm_b"], small["w_spatial"], bsp_t, "gmlp_fwd")
    pre_conv, xc = _conv_fwd(proj, wts["conv_w"], small["conv_b"], "conv_fwd")
    y_ssd, y_b, sprev = _ssd_fwd(xc, proj, dt_raw, dtb, alog, dskip_full, small["ssm_norm_g"], e_bf, "ssd_fwd")
    wts = {**wts, **exchange.late_weights(y_b)}
    pa = mm(y_a, wts["w_proj_a"], "nn", "proj_a")
    pb, merged = mm(y_b, wts["w_proj_b"], "nn", "proj_b", epilogue=_merge_epilogue, out_dtypes=(F32, BF16), max_tm=fused_tm,
                    extras=(pa, proj, proj, small["b_gates"], small["b_gates"]),
                    extra_specs=row_tiles(1, (D_MODEL, 0), (D_MODEL, 1), gate_logits=True))
    x1, h2 = mm(merged, wts["w_out"], "nn", "out_proj", epilogue=_residual_rms_epilogue, out_dtypes=(F32, BF16),
                extras=(x, small["norm_mlp_g"]), extra_specs=row_tiles(1, (D_MODEL, 0)))

    def relu_sq(acc, ex, outs):
        outs[0][...] = acc
        r = jnp.maximum(acc, 0.0)
        outs[1][...] = (r * r).astype(BF16)

    up, act = mm(h2, wts["w_mlp_up"], "nn", "mlp_up", epilogue=relu_sq, out_dtypes=(F32, BF16), j_outer=True)
    dx2, dx2_b, g_final, _, loss = mm(
        act, wts["w_mlp_down"], "nn", "mlp_down", epilogue=_loss_epilogue, carry=True,
        out_dtypes=(F32, BF16, vec(D_MODEL), vec(D_MODEL), vec(LANES)),
        extras=(x1, small["norm_final_g"], target), extra_specs=lambda tm, tn: (
            ((tm, tn), lambda i, j: (i, j)), ((1, tn), lambda i, j: (0, 0)), ((tm, tn), lambda i, j: (i, j))))

    def relu_sq_bwd(acc, ex, outs):
        outs[0][...] = (acc * 2.0 * jnp.maximum(ex[0][...], 0.0)).astype(BF16)

    dup = mm(dx2_b, wts["w_mlp_down"], "nt", "d_act", epilogue=relu_sq_bwd, extras=(up,), extra_specs=out_tile,
             out_dtypes=(BF16,), j_outer=True)
    g_down = mm(act, dx2_b, "tn", "g_mlp_down")
    g_up = mm(h2, dup, "tn", "g_mlp_up")
    started = exchange.reduce("mlp", {"w_mlp_down": g_down, "w_mlp_up": g_up})
    dx1, dx1_b, g_mlp = mm(
        dup, wts["w_mlp_up"], "nt", "d_h2", deps=started, epilogue=_rms_bwd_epilogue, carry=True,
        out_dtypes=(F32, BF16, vec(D_MODEL)), extras=(x1, small["norm_mlp_g"], dx2), extra_specs=lambda tm, tn: (
            ((tm, tn), lambda i, j: (i, j)), ((1, tn), lambda i, j: (0, 0)), ((tm, tn), lambda i, j: (i, j))))

    g_out = mm(merged, dx1_b, "tn", "g_out")
    dpa, dpb, dproj, g_bgates = mm(
        dx1_b, wts["w_out"], "nt", "d_merged", epilogue=_merge_bwd_epilogue, carry=True, max_tm=fused_tm,
        out_dtypes=(BF16, BF16, ((t, MAIN_W), BF16, (fused_tm, 2 * D_MODEL), lambda i, j: (i, COL_GATE // (2 * D_MODEL))),
                    vec(2 * D_MODEL)),
        extras=(pa, pb, proj, proj, small["b_gates"], small["b_gates"]),
        extra_specs=row_tiles(2, (D_MODEL, 0), (D_MODEL, 1), gate_logits=True))
    g_pa = mm(y_a, dpa, "tn", "g_proj_a")
    g_pb = mm(y_b, dpb, "tn", "g_proj_b")
    started = exchange.reduce("proj", {"w_out": g_out, "w_proj_a": g_pa, "w_proj_b": g_pb})
    dya = mm(dpa, wts["w_proj_a"], "nt", "d_ya", deps=started)
    dyb = mm(dpb, wts["w_proj_b"], "nt", "d_yb")

    dproj, g_wsp, g_bsp_t, g_vg, g_vb = _gmlp_bwd(proj, dya, small["v_norm_g"], small["v_norm_b"], small["w_spatial"],
                                                   bsp_t, dproj, "gmlp_bwd")
    dproj, dxc, ddt, g_ng, g_dskip, g_alog, g_dtb = _ssd_bwd(dyb, y_ssd, xc, proj, dt_raw, sprev, dtb, alog, dskip_full,
                                                             small["ssm_norm_g"], e_bf, dproj, "ssd_bwd")
    dproj, g_convw, g_convb = _conv_bwd(proj, pre_conv, dxc, wts["conv_w"], dproj, "conv_bwd")

    g_main_t = mm(dproj, h, "tn", "g_in_main")
    g_dt_t = mm(ddt, h, "tn", "g_in_dt")
    started = exchange.reduce("in", {"w_in": _join_w_in(g_main_t, g_dt_t)})

    def input_grad(acc, ex, outs):
        dh = acc + _dot(ex[3][...], ex[4][...], _NN)
        _rms_bwd_epilogue(dh, ex, outs)

    grad_x, g_mix = mm(
        dproj, w_main_t, "nn", "d_h", epilogue=input_grad, deps=started, carry=True, max_tm=fused_tm,
        out_dtypes=(F32, vec(D_MODEL)), extras=(x, small["norm_mix_g"], dx1, ddt, w_dt_t), extra_specs=lambda tm, tn: (
            ((tm, tn), lambda i, j: (i, j)), ((1, tn), lambda i, j: (0, 0)), ((tm, tn), lambda i, j: (i, j)),
            ((tm, DT_PAD), lambda i, j: (i, 0)), ((DT_PAD, D_MODEL), lambda i, j: (0, 0))))

    grads = {
        "conv_w": g_convw,
        "norm_mix_g": g_mix, "conv_b": g_convb, "dt_bias": g_dtb, "a_log": g_alog, "d_skip": g_dskip, "ssm_norm_g": g_ng,
        "v_norm_g": g_vg, "v_norm_b": g_vb, "w_spatial": g_wsp.reshape(GROUPS * CHUNK, CHUNK), "b_spatial": g_bsp_t.T, "b_gates": g_bgates, "norm_mlp_g": g_mlp, "norm_final_g": g_final,
    }
    return loss, grad_x, grads


def _split_w_in(w_full_t):
    dt0 = COL_GATE
    w_main_t = jnp.concatenate([w_full_t[:dt0], w_full_t[dt0 + N_HEADS:]], axis=0)
    w_dt_t = jnp.pad(w_full_t[dt0:dt0 + N_HEADS], ((0, DT_PAD - N_HEADS), (0, 0)))
    return w_main_t, w_dt_t


def _join_w_in(g_main_t, g_dt_t):
    dt0 = COL_GATE
    return jnp.concatenate([g_main_t[:dt0], g_dt_t[:N_HEADS], g_main_t[dt0:]], axis=0)


_LATE = ["w_proj_a", "w_proj_b", "w_out", "w_mlp_up", "w_mlp_down"]
_BY_COLS = ("w_mlp_up",)


class _Exchange:
    def __init__(self, late_shards, late_lands):
        self.late_shards, self.late_lands = late_shards, late_lands
        self.c_idx = lax.axis_index("c").astype(jnp.int32).reshape(1)
        self.chip_idx = (2 * lax.axis_index("x") + lax.axis_index("y")).astype(jnp.int32).reshape(1)
        self.pending = []

    def begin(self):
        self.late = _split_start(self.late_shards, self.late_lands, _gather_copies, N_DEV - 1, "gather_late_start")
        return [self.late[-1]]

    def late_weights(self, after):
        _, lands = _split_wait(self.late, _gather_copies, after, "gather_late_wait")
        whole = {}
        for n, g in zip(_LATE, lands):
            whole[n] = jnp.transpose(g, (1, 0, 2)).reshape(g.shape[1], -1) if n in _BY_COLS else g.reshape(-1, g.shape[2])
        return whole

    def reduce(self, tag, grads):
        names = list(grads)
        by_dev = []
        for n in names:
            g = grads[n]
            if n in _BY_COLS:
                by_dev.append(jnp.transpose(g.reshape(g.shape[0], N_DEV, -1), (1, 0, 2)))
            else:
                by_dev.append(g.reshape(N_DEV, -1, g.shape[1]))
        from_sibling = _swap_with_sibling(by_dev, "reduce_cores_" + tag)
        parts = [_add_sibling(g, r, self.c_idx, "add_cores_" + n) for n, g, r in zip(names, by_dev, from_sibling)]
        lands = [lax.empty(p.shape, p.dtype) for p in parts]
        started = _split_start(parts, lands, _scatter_copies, 3, "reduce_chips_start_" + tag)
        self.pending.append((tag, names, started))
        return [started[-1]]

    def finish(self, after):
        done = {}
        for tag, names, started in self.pending:
            parts, lands = _split_wait(started, _scatter_copies, after, "reduce_chips_wait_" + tag)
            for n, land, part in zip(names, lands, parts):
                done[n] = (land, part, self.chip_idx)
        return done


def kernel(x, norm_mix_g, w_in, conv_w, conv_b, dt_bias, a_log, d_skip, ssm_norm_g, v_norm_g, v_norm_b, w_spatial, b_spatial, b_gates, w_proj_a, w_proj_b, w_out, norm_mlp_g, w_mlp_up, w_mlp_down, norm_final_g, loss_target, m_norm_mix_g, m_w_in, m_conv_w, m_conv_b, m_dt_bias, m_a_log, m_d_skip, m_ssm_norm_g, m_v_norm_g, m_v_norm_b, m_w_spatial, m_b_spatial, m_b_gates, m_w_proj_a, m_w_proj_b, m_w_out, m_norm_mlp_g, m_w_mlp_up, m_w_mlp_down, m_norm_final_g, v_norm_mix_g, v_w_in, v_conv_w, v_conv_b, v_dt_bias, v_a_log, v_d_skip, v_ssm_norm_g, v_v_norm_g, v_v_norm_b, v_w_spatial, v_b_spatial, v_b_gates, v_w_proj_a, v_w_proj_b, v_w_out, v_norm_mlp_g, v_w_mlp_up, v_w_mlp_down, v_norm_final_g):
    given = dict(locals())
    names = ["norm_mix_g", "w_in", "conv_w", "conv_b", "dt_bias", "a_log", "d_skip", "ssm_norm_g", "v_norm_g", "v_norm_b",
             "w_spatial", "b_spatial", "b_gates", "w_proj_a", "w_proj_b", "w_out", "norm_mlp_g", "w_mlp_up", "w_mlp_down",
             "norm_final_g"]
    shapes = {n: given[n].shape for n in names}
    dev = 4 * lax.axis_index("x") + 2 * lax.axis_index("y") + lax.axis_index("c")

    shard2d = {"w_in": w_in[0].T, "w_proj_a": w_proj_a[0], "w_proj_b": w_proj_b[0], "w_out": w_out[0],
               "w_mlp_up": w_mlp_up[0], "w_mlp_down": w_mlp_down[0]}
    conv_shard = conv_w.reshape(CONV_WIDTH, -1)
    late_shards = [shard2d[n].astype(BF16) for n in _LATE]
    w_in_all, conv_all, *late_lands = _all_gather([shard2d["w_in"].astype(BF16), conv_shard], "gather_first",
                                                  own_only=late_shards)
    w_main_t, w_dt_t = _split_w_in(w_in_all.reshape(-1, D_MODEL))
    wts = {"w_main_t": w_main_t, "w_dt_t": w_dt_t, "conv_w": jnp.transpose(conv_all, (1, 0, 2)).reshape(CONV_WIDTH, -1)}
    small = {"norm_mix_g": norm_mix_g, "conv_b": conv_b, "dt_bias": dt_bias, "a_log": a_log, "d_skip": d_skip,
             "ssm_norm_g": ssm_norm_g, "v_norm_g": v_norm_g, "v_norm_b": v_norm_b, "w_spatial": w_spatial[0],
             "b_spatial": b_spatial[0], "b_gates": b_gates, "norm_mlp_g": norm_mlp_g,
             "norm_final_g": norm_final_g.reshape(1, -1)}

    exchange = _Exchange(late_shards, late_lands)
    loss_part, grad_x, grads = _local_step(x[0], loss_target[0], wts, small, exchange)

    out = {}
    for n, (slots, own, own_slot) in exchange.finish(grad_x).items():
        moments = [given["m_" + n][0], given["v_" + n][0]]
        if n == "w_in":
            moments = [mom.T for mom in moments]
        res = _adamw(slots, shard2d[n], *moments, "adamw_" + n, own=own, own_slot=own_slot)
        out[n] = [(r.T if n == "w_in" else r).reshape(shapes[n]) for r in res]

    all_g = _exchange_small({**grads, "loss": loss_part}, "exchange_small")
    small["w_spatial"] = small["w_spatial"].reshape(GROUPS * CHUNK, CHUNK)
    params = {n: (w2d, given["m_" + n].reshape(w2d.shape), given["v_" + n].reshape(w2d.shape)) for n, w2d in small.items()}
    updated, (g_conv_full, loss_all) = _adamw_small(all_g, params, [grads["conv_w"].shape, loss_part.shape], "adamw_small")
    for n, res in updated.items():
        out[n] = [r.reshape(shapes[n]) for r in res]
    width = shapes["conv_w"][-1]
    g_conv = lax.dynamic_slice(g_conv_full, (0, dev * width), (CONV_WIDTH, width))
    res = _adamw(g_conv[None], conv_shard, m_conv_w.reshape(CONV_WIDTH, -1), v_conv_w.reshape(CONV_WIDTH, -1), "adamw_conv_w")
    out["conv_w"] = [r.reshape(shapes["conv_w"]) for r in res]

    loss = loss_all[0, 0]
    return (loss, grad_x[None], *[out[n][0] for n in names], *[out[n][1] for n in names],
            *[out[n][2] for n in names], *[out[n][3] for n in names])
```

```python
import functools
import math

import jax
import jax.numpy as jnp
from jax import lax
from jax.experimental import pallas as pl
from jax.experimental.pallas import tpu as pltpu

F32 = jnp.float32
BF16 = jnp.bfloat16
MESH = pl.DeviceIdType.MESH

D_MODEL = 1024
NORM_EPS = 1e-6
CHUNK = 128
GROUPS = 8
D_INNER = 2048
HEAD_DIM = 64
N_HEADS = 32
D_STATE = 128
CONV_WIDTH = 4
CONV_DIM = 4096
D_FF = 4096
GROUP_W = D_INNER // GROUPS
N_DEV = 8
N_CHIP = 4

ADAM_LR = 0.001
ADAM_B1 = 0.9
ADAM_B2 = 0.999
ADAM_EPS = 1e-08
ADAM_WD = 0.01
ADAM_STEP = 10

MAIN_W = 2 * D_MODEL + D_INNER + CONV_DIM + 2 * D_MODEL
COL_Z = 2048
COL_XBC = 4096
COL_GATE = 8192
DT_PAD = 128

LANES = 128
SUBLANES = 8
VMEM_BYTES_V7X = 64 * 1024 * 1024
VMEM_BODY_TEMP = 24 * 1024 * 1024


def _vmem_limit(block_bytes):
    return int(min(2 * block_bytes + VMEM_BODY_TEMP, VMEM_BYTES_V7X - 8 * 1024 * 1024))


def _nbytes(shape, dtype):
    return math.prod(shape) * jnp.dtype(dtype).itemsize


_HBM = pl.BlockSpec(memory_space=pl.ANY)


def _params(sem, block_bytes):
    return pltpu.CompilerParams(dimension_semantics=sem, vmem_limit_bytes=_vmem_limit(block_bytes))


def _sigmoid(x):
    return 1.0 / (1.0 + jnp.exp(-x))


def _softplus(x):
    e = jnp.exp(-jnp.abs(x))
    u = 1.0 + e
    log1p_e = jnp.where(u == 1.0, e, jnp.log(u) * (e / jnp.where(u == 1.0, 1.0, u - 1.0)))
    return jnp.maximum(x, 0.0) + log1p_e


_SQRT_HALF = 0.7071067811865476
_INV_SQRT_2PI = 0.3989422804014327


def _gelu(x):
    return x * (lax.erf(x * _SQRT_HALF) + 1.0) * 0.5


def _gelu_grad(x):
    return 0.5 * (1.0 + lax.erf(x * _SQRT_HALF)) + x * jnp.exp(-0.5 * x * x) * _INV_SQRT_2PI


def _dot(a, b, dims):
    return lax.dot_general(a, b, (dims, ((), ())), preferred_element_type=F32)


_NN = ((1,), (0,))
_NT = ((1,), (1,))
_TN = ((0,), (0,))


def _split3(x):
    hi = x.astype(BF16)
    r1 = x - hi.astype(F32)
    mid = r1.astype(BF16)
    lo = (r1 - mid.astype(F32)).astype(BF16)
    return hi, mid, lo


def _dot_exact_rhs(x, e, dims):
    hi, mid, lo = _split3(x)
    return _dot(hi, e, dims) + _dot(mid, e, dims) + _dot(lo, e, dims)


def _dot_exact_lhs(e, x, dims):
    hi, mid, lo = _split3(x)
    return _dot(e, hi, dims) + _dot(e, mid, dims) + _dot(e, lo, dims)


def _tri(lower):
    r = lax.broadcasted_iota(jnp.int32, (CHUNK, CHUNK), 0)
    c = lax.broadcasted_iota(jnp.int32, (CHUNK, CHUNK), 1)
    return (r >= c) if lower else (r <= c)


def _matmul(a, b, *, mode, tm, tn, tk, out_dtypes, name, epilogue=None, extras=(), extra_specs=(), j_outer=False, deps=(),
            carry=False):
    if mode == "nn":
        (m, k), (_, n) = a.shape, b.shape
    elif mode == "nt":
        (m, k), (n, _) = a.shape, b.shape
    else:
        (k, m), (_, n) = a.shape, b.shape
    assert m % tm == 0 and n % tn == 0 and k % tk == 0, (name, m, n, k, tm, tn, tk)
    nk = k // tk
    n_extra, n_out = len(extras), len(out_dtypes)
    first_out = 2 + n_extra + len(deps)
    dims = {"nn": _NN, "nt": _NT, "tn": _TN}[mode]
    if epilogue is None:
        def epilogue(acc, ex, outs):
            outs[0][...] = acc.astype(outs[0].dtype)

    def body(*refs):
        a_ref, b_ref = refs[0], refs[1]
        ex_refs = refs[2:2 + n_extra]
        outs = refs[first_out:first_out + n_out]
        p = _dot(a_ref[...], b_ref[...], dims)
        if nk == 1:
            epilogue(p, ex_refs, outs)
        else:
            acc_ref = refs[first_out + n_out]
            kk = pl.program_id(2)

            @pl.when(kk == 0)
            def _():
                acc_ref[...] = p

            @pl.when(kk > 0)
            def _():
                acc_ref[...] += p

            @pl.when(kk == nk - 1)
            def _():
                epilogue(acc_ref[...], ex_refs, outs)

    if j_outer:
        grid = (n // tn, m // tm, nk)
        ij = lambda g0, g1: (g1, g0)
    else:
        grid = (m // tm, n // tn, nk)
        ij = lambda g0, g1: (g0, g1)

    def wrap(fn):
        return lambda g0, g1, kk: fn(*ij(g0, g1), kk)

    if mode == "nn":
        a_spec = pl.BlockSpec((tm, tk), wrap(lambda i, j, kk: (i, kk)))
        b_spec = pl.BlockSpec((tk, tn), wrap(lambda i, j, kk: (kk, j)))
        a_blk, b_blk = (tm, tk), (tk, tn)
    elif mode == "nt":
        a_spec = pl.BlockSpec((tm, tk), wrap(lambda i, j, kk: (i, kk)))
        b_spec = pl.BlockSpec((tn, tk), wrap(lambda i, j, kk: (j, kk)))
        a_blk, b_blk = (tm, tk), (tn, tk)
    else:
        a_spec = pl.BlockSpec((tk, tm), wrap(lambda i, j, kk: (kk, i)))
        b_spec = pl.BlockSpec((tk, tn), wrap(lambda i, j, kk: (kk, j)))
        a_blk, b_blk = (tk, tm), (tk, tn)
    ex_specs = [pl.BlockSpec(shape, wrap(lambda i, j, kk, f=f: f(i, j))) for shape, f in extra_specs]
    outs = [o if isinstance(o, tuple) else ((m, n), o, (tm, tn), lambda i, j: (i, j)) for o in out_dtypes]
    out_spec = [pl.BlockSpec(blk_shape, wrap(lambda i, j, kk, f=f: f(i, j))) for _, _, blk_shape, f in outs]
    out_shape = [jax.ShapeDtypeStruct(shape, dt) for shape, dt, _, _ in outs]
    blk = (_nbytes(a_blk, a.dtype) + _nbytes(b_blk, b.dtype) + sum(_nbytes(s, F32) for s, _ in extra_specs)
           + sum(_nbytes(blk_shape, dt) for _, dt, blk_shape, _ in outs) + _nbytes((tm, tn), F32))
    order = ("arbitrary",) * 3 if carry else ("parallel", "parallel", "arbitrary")
    res = pl.pallas_call(
        body, name=name, grid=grid,
        in_specs=[a_spec, b_spec] + ex_specs + [_HBM] * len(deps), out_specs=out_spec, out_shape=out_shape,
        scratch_shapes=[pltpu.VMEM((tm, tn), F32)] if nk > 1 else [],
        compiler_params=_params(order, blk),
    )(a, b, *extras, *deps)
    return res[0] if n_out == 1 else res


ROW_TILE = 256


def _row_spec(width, col_block=0, tile=ROW_TILE):
    return pl.BlockSpec((tile, width), lambda i, cb=col_block: (i, cb))


def _vec_spec(width, col_block=0):
    return pl.BlockSpec((1, width), lambda i, cb=col_block: (0, cb))


def _rms_fwd(x, g, name, deps=()):
    t = x.shape[0]

    def body(x_ref, g_ref, *rest):
        h_ref = rest[-1]
        xv = x_ref[...]
        r = lax.rsqrt(jnp.mean(xv * xv, axis=-1, keepdims=True) + NORM_EPS)
        h_ref[...] = (xv * r * g_ref[...]).astype(BF16)

    return pl.pallas_call(
        body, name=name, grid=(t // ROW_TILE,),
        in_specs=[_row_spec(D_MODEL), _vec_spec(D_MODEL)] + [_HBM] * len(deps), out_specs=_row_spec(D_MODEL),
        out_shape=jax.ShapeDtypeStruct((t, D_MODEL), BF16),
        compiler_params=_params(("parallel",), 3 * _nbytes((ROW_TILE, D_MODEL), F32)),
    )(x, g, *deps)


def _rms_scale(xv):
    r = lax.rsqrt(jnp.mean(xv * xv, axis=-1, keepdims=True) + NORM_EPS)
    return r, xv * r


def _rms_pullback(xv, g, dh):
    r, xh = _rms_scale(xv)
    dyg = dh * g
    return r * (dyg - xh * jnp.mean(dyg * xh, axis=-1, keepdims=True)), jnp.sum(dh * xh, axis=0, keepdims=True)


def _first_row_tile():
    return pl.program_id(0) == 0


def _residual_rms_epilogue(acc, ex, outs):
    x1 = acc + ex[0][...]
    outs[0][...] = x1
    _, xh = _rms_scale(x1)
    outs[1][...] = (xh * ex[1][...]).astype(BF16)


def _loss_epilogue(acc, ex, outs):
    dx_ref, dxb_ref, gg_ref, sq_ref, tot_ref = outs
    gv = ex[1][...]
    r, xh = _rms_scale(acc + ex[0][...])
    err = xh * gv - ex[2][...]
    dy = err * (1.0 / D_MODEL)
    dyg = dy * gv
    dx = r * (dyg - xh * jnp.mean(dyg * xh, axis=-1, keepdims=True))
    dx_ref[...] = dx
    dxb_ref[...] = dx.astype(BF16)

    @pl.when(_first_row_tile())
    def _():
        gg_ref[...] = jnp.zeros_like(gg_ref)
        sq_ref[...] = jnp.zeros_like(sq_ref)

    gg_ref[...] += jnp.sum(dy * xh, axis=0, keepdims=True)
    sq_ref[...] += jnp.sum(err * err, axis=0, keepdims=True)
    tot_ref[...] = jnp.broadcast_to(jnp.sum(sq_ref[...], axis=1, keepdims=True) * (0.5 / D_MODEL), tot_ref.shape)


def _rms_bwd_epilogue(dh, ex, outs):
    dx, gg = _rms_pullback(ex[0][...], ex[1][...], dh)
    dx = dx + ex[2][...]
    outs[0][...] = dx
    if len(outs) == 3:
        outs[1][...] = dx.astype(BF16)

    @pl.when(_first_row_tile())
    def _():
        outs[-1][...] = jnp.zeros_like(outs[-1])

    outs[-1][...] += gg


def _merge_epilogue(acc, ex, outs):
    outs[0][...] = acc
    ga = _sigmoid(ex[1][...] + ex[3][...])
    gb = _sigmoid(ex[2][...] + ex[4][...])
    outs[1][...] = (ga * ex[0][...] + gb * acc).astype(BF16)


def _merge_bwd_epilogue(dm, ex, outs):
    dpa_ref, dpb_ref, dgl_ref, gb_ref = outs
    ga = _sigmoid(ex[2][...] + ex[4][...])
    gb = _sigmoid(ex[3][...] + ex[5][...])
    dpa_ref[...] = (dm * ga).astype(BF16)
    dpb_ref[...] = (dm * gb).astype(BF16)
    dla = dm * ex[0][...] * ga * (1.0 - ga)
    dlb = dm * ex[1][...] * gb * (1.0 - gb)
    dgl_ref[:, :D_MODEL] = dla.astype(BF16)
    dgl_ref[:, D_MODEL:] = dlb.astype(BF16)

    @pl.when(_first_row_tile())
    def _():
        gb_ref[...] = jnp.zeros_like(gb_ref)

    gb_ref[:, :D_MODEL] += jnp.sum(dla, axis=0, keepdims=True)
    gb_ref[:, D_MODEL:] += jnp.sum(dlb, axis=0, keepdims=True)


GMLP_TILE = 512
GMLP_NC = GMLP_TILE // CHUNK


def _gmlp_common(u_pre, v_pre, vg, vb):
    u = _gelu(u_pre)
    v = _gelu(v_pre)
    mu = jnp.mean(v, axis=-1, keepdims=True)
    vc = v - mu
    rstd = lax.rsqrt(jnp.mean(vc * vc, axis=-1, keepdims=True) + NORM_EPS)
    vh = vc * rstd
    vn = vh * vg + vb
    return u, vh, vn, rstd


def _chunks_to_lanes(x, g):
    return jnp.concatenate([x[c * CHUNK:(c + 1) * CHUNK, g * CHUNK:(g + 1) * CHUNK] for c in range(GMLP_NC)], axis=1)


def _gmlp_fwd(proj, vg, vb, wsp, bsp_t, name):
    t = proj.shape[0]

    def body(u_ref, v_ref, vg_ref, vb_ref, w_ref, b_ref, ya_ref):
        u, _, vn, _ = _gmlp_common(u_ref[...], v_ref[...], vg_ref[...], vb_ref[...])
        mask = _tri(True)
        bt = b_ref[...]
        for g in range(GROUPS):
            w = jnp.where(mask, w_ref[g], 0.0).astype(BF16)
            vcat = _chunks_to_lanes(vn, g).astype(BF16)
            s = _dot(w, vcat, _NN) + bt[:, g:g + 1]
            for c in range(GMLP_NC):
                rows, cols = slice(c * CHUNK, (c + 1) * CHUNK), slice(g * CHUNK, (g + 1) * CHUNK)
                ya_ref[rows, cols] = (u[rows, cols] * s[:, c * CHUNK:(c + 1) * CHUNK]).astype(BF16)

    return pl.pallas_call(
        body, name=name, grid=(t // GMLP_TILE,),
        in_specs=[_row_spec(D_MODEL, 0, GMLP_TILE), _row_spec(D_MODEL, 1, GMLP_TILE), _vec_spec(D_MODEL),
                  _vec_spec(D_MODEL), pl.BlockSpec((GROUPS, CHUNK, CHUNK), lambda i: (0, 0, 0)),
                  pl.BlockSpec((CHUNK, GROUPS), lambda i: (0, 0))],
        out_specs=_row_spec(D_MODEL, 0, GMLP_TILE),
        out_shape=jax.ShapeDtypeStruct((t, D_MODEL), BF16),
        compiler_params=_params(("parallel",), 3 * _nbytes((GMLP_TILE, D_MODEL), F32)),
    )(proj, proj, vg, vb, wsp, bsp_t)


def _gmlp_bwd(proj, dya, vg, vb, wsp, bsp_t, dproj, name):
    t = proj.shape[0]

    def body(u_ref, v_ref, dya_ref, vg_ref, vb_ref, w_ref, b_ref, dproj_in, duv_ref, gw_ref, gbt_ref, gvg_ref, gvb_ref,
             dvn_scr, du_scr):
        del dproj_in
        u_pre, v_pre = u_ref[...], v_ref[...]
        vgv = vg_ref[...]
        u, vh, vn, rstd = _gmlp_common(u_pre, v_pre, vgv, vb_ref[...])
        dya = dya_ref[...]
        mask = _tri(True)
        bt = b_ref[...]
        first = pl.program_id(0) == 0

        @pl.when(first)
        def _():
            gw_ref[...] = jnp.zeros_like(gw_ref)
            gbt_ref[...] = jnp.zeros_like(gbt_ref)
            gvg_ref[...] = jnp.zeros_like(gvg_ref)
            gvb_ref[...] = jnp.zeros_like(gvb_ref)

        lane = lax.broadcasted_iota(jnp.int32, (CHUNK, GROUPS), 1)
        gbt = jnp.zeros((CHUNK, GROUPS), F32)
        for g in range(GROUPS):
            w = jnp.where(mask, w_ref[g], 0.0).astype(BF16)
            vcat = _chunks_to_lanes(vn, g).astype(BF16)
            s = _dot(w, vcat, _NN) + bt[:, g:g + 1]
            ds = _chunks_to_lanes(dya * u, g)
            gbt = jnp.where(lane == g, jnp.sum(ds, axis=1, keepdims=True), gbt)
            dsb = ds.astype(BF16)
            gw_ref[g] += jnp.where(mask, _dot(dsb, vcat, _NT), 0.0)
            dv = _dot(w, dsb, _TN)
            for c in range(GMLP_NC):
                rows, cols = slice(c * CHUNK, (c + 1) * CHUNK), slice(g * CHUNK, (g + 1) * CHUNK)
                dvn_scr[rows, cols] = dv[:, c * CHUNK:(c + 1) * CHUNK]
                du_scr[rows, cols] = dya[rows, cols] * s[:, c * CHUNK:(c + 1) * CHUNK]
        gbt_ref[...] += gbt
        dvn = dvn_scr[...]
        gvg_ref[...] += jnp.sum(dvn * vh, axis=0, keepdims=True)
        gvb_ref[...] += jnp.sum(dvn, axis=0, keepdims=True)
        dvh = dvn * vgv
        dv = rstd * (dvh - jnp.mean(dvh, axis=-1, keepdims=True) - vh * jnp.mean(dvh * vh, axis=-1, keepdims=True))
        duv_ref[:, :D_MODEL] = (du_scr[...] * _gelu_grad(u_pre)).astype(BF16)
        duv_ref[:, D_MODEL:] = (dv * _gelu_grad(v_pre)).astype(BF16)

    return pl.pallas_call(
        body, name=name, grid=(t // GMLP_TILE,),
        in_specs=[_row_spec(D_MODEL, 0, GMLP_TILE), _row_spec(D_MODEL, 1, GMLP_TILE), _row_spec(D_MODEL, 0, GMLP_TILE),
                  _vec_spec(D_MODEL), _vec_spec(D_MODEL), pl.BlockSpec((GROUPS, CHUNK, CHUNK), lambda i: (0, 0, 0)),
                  pl.BlockSpec((CHUNK, GROUPS), lambda i: (0, 0)), pl.BlockSpec(memory_space=pl.ANY)],
        out_specs=[_row_spec(2 * D_MODEL, 0, GMLP_TILE), pl.BlockSpec((GROUPS, CHUNK, CHUNK), lambda i: (0, 0, 0)),
                   pl.BlockSpec((CHUNK, GROUPS), lambda i: (0, 0)), _vec_spec(D_MODEL), _vec_spec(D_MODEL)],
        out_shape=[jax.ShapeDtypeStruct(dproj.shape, BF16), jax.ShapeDtypeStruct((GROUPS, CHUNK, CHUNK), F32),
                   jax.ShapeDtypeStruct((CHUNK, GROUPS), F32), jax.ShapeDtypeStruct((1, D_MODEL), F32),
                   jax.ShapeDtypeStruct((1, D_MODEL), F32)],
        scratch_shapes=[pltpu.VMEM((GMLP_TILE, D_MODEL), F32), pltpu.VMEM((GMLP_TILE, D_MODEL), F32)],
        input_output_aliases={7: 0},
        compiler_params=_params(("arbitrary",), 6 * _nbytes((GMLP_TILE, D_MODEL), F32)),
    )(proj, proj, dya, vg, vb, wsp, bsp_t, dproj)


CONV_TILE = 512
CONV_COLS = 1024
CONV_RB = 32
HALO = SUBLANES


def _conv_fwd(proj, cw, cb, name):
    t = proj.shape[0]
    nj = CONV_DIM // CONV_COLS
    xcb = COL_XBC // CONV_COLS
    rb = CONV_TILE // HALO

    def body(x_ref, prev_ref, cw_ref, cb_ref, pre_ref, xc_ref):
        i = pl.program_id(1)
        cw_v = cw_ref[...]
        cb_v = cb_ref[...]
        for b in range(CONV_TILE // CONV_RB):
            if b == 0:
                ext = jnp.concatenate([jnp.where(i > 0, prev_ref[...], 0.0), x_ref[:CONV_RB, :]], axis=0)
            else:
                ext = x_ref[b * CONV_RB - HALO:(b + 1) * CONV_RB, :]
            pre = cb_v + cw_v[CONV_WIDTH - 1:CONV_WIDTH, :] * ext[HALO:, :]
            for k in range(CONV_WIDTH - 1):
                back = CONV_WIDTH - 1 - k
                pre = pre + cw_v[k:k + 1, :] * pltpu.roll(ext, back, 0)[HALO:, :]
            pre_ref[b * CONV_RB:(b + 1) * CONV_RB, :] = pre
            xc_ref[b * CONV_RB:(b + 1) * CONV_RB, :] = pre * _sigmoid(pre)

    tile = pl.BlockSpec((CONV_TILE, CONV_COLS), lambda j, i: (i, j))
    return pl.pallas_call(
        body, name=name, grid=(nj, t // CONV_TILE),
        in_specs=[pl.BlockSpec((CONV_TILE, CONV_COLS), lambda j, i: (i, xcb + j)),
                  pl.BlockSpec((HALO, CONV_COLS), lambda j, i: (jnp.maximum(i * rb - 1, 0), xcb + j)),
                  pl.BlockSpec((CONV_WIDTH, CONV_COLS), lambda j, i: (0, j)),
                  pl.BlockSpec((1, CONV_COLS), lambda j, i: (0, j))],
        out_specs=[tile, tile],
        out_shape=[jax.ShapeDtypeStruct((t, CONV_DIM), F32), jax.ShapeDtypeStruct((t, CONV_DIM), F32)],
        compiler_params=_params(("parallel", "parallel"), 4 * _nbytes((CONV_TILE, CONV_COLS), F32)),
    )(proj, proj, cw, cb)


def _fold_rows(v):
    out = v[:SUBLANES]
    for r in range(1, v.shape[0] // SUBLANES):
        out = out + v[r * SUBLANES:(r + 1) * SUBLANES]
    return out


def _conv_bwd(proj, pre, dxc, cw, dproj, name):
    t = proj.shape[0]
    nj = CONV_DIM // CONV_COLS
    ni = t // CONV_TILE
    xcb = COL_XBC // CONV_COLS
    rb = CONV_TILE // HALO
    last_rb = t // HALO - 1

    def body(x_ref, p_ref, pnext_ref, d_ref, dnext_ref, cw_ref, dproj_in, dx_ref, gw_ref, gb_ref):
        del dproj_in
        i = pl.program_id(1)
        cw_v = cw_ref[...]

        def dpre_of(p, d):
            sg = _sigmoid(p)
            return d * sg * (1.0 + p * (1.0 - sg))

        @pl.when(i == 0)
        def _():
            gw_ref[...] = jnp.zeros_like(gw_ref)
            gb_ref[...] = jnp.zeros_like(gb_ref)

        head = dpre_of(pnext_ref[...], jnp.where(i < ni - 1, dnext_ref[...], 0.0))
        gb_acc = jnp.zeros((SUBLANES, CONV_COLS), F32)
        gw_acc = [jnp.zeros((SUBLANES, CONV_COLS), F32) for _ in range(CONV_WIDTH)]
        for b in reversed(range(CONV_TILE // CONV_RB)):
            rows = slice(b * CONV_RB, (b + 1) * CONV_RB)
            cur = dpre_of(p_ref[rows, :], d_ref[rows, :])
            ext = jnp.concatenate([cur, head], axis=0)
            xv = x_ref[rows, :]
            dx = None
            for k in range(CONV_WIDTH):
                shift = CONV_WIDTH - 1 - k
                win = cur if shift == 0 else pltpu.roll(ext, CONV_RB + HALO - shift, 0)[:CONV_RB, :]
                term = cw_v[k:k + 1, :] * win
                dx = term if dx is None else dx + term
                gw_acc[k] = gw_acc[k] + _fold_rows(win * xv)
            dx_ref[rows, :] = dx.astype(BF16)
            gb_acc = gb_acc + _fold_rows(cur)
            head = cur[:HALO]
        gb_ref[...] += jnp.sum(gb_acc, axis=0, keepdims=True)
        for k in range(CONV_WIDTH):
            gw_ref[k:k + 1, :] += jnp.sum(gw_acc[k], axis=0, keepdims=True)

    tile = pl.BlockSpec((CONV_TILE, CONV_COLS), lambda j, i: (i, j))
    after = pl.BlockSpec((HALO, CONV_COLS), lambda j, i: (jnp.minimum((i + 1) * rb, last_rb), j))
    return pl.pallas_call(
        body, name=name, grid=(nj, ni),
        in_specs=[pl.BlockSpec((CONV_TILE, CONV_COLS), lambda j, i: (i, xcb + j)), tile, after, tile, after,
                  pl.BlockSpec((CONV_WIDTH, CONV_COLS), lambda j, i: (0, j)),
                  pl.BlockSpec(memory_space=pl.ANY)],
        out_specs=[pl.BlockSpec((CONV_TILE, CONV_COLS), lambda j, i: (i, xcb + j)),
                   pl.BlockSpec((CONV_WIDTH, CONV_COLS), lambda j, i: (0, j)),
                   pl.BlockSpec((1, CONV_COLS), lambda j, i: (0, j))],
        out_shape=[jax.ShapeDtypeStruct(dproj.shape, BF16), jax.ShapeDtypeStruct((CONV_WIDTH, CONV_DIM), F32),
                   jax.ShapeDtypeStruct((1, CONV_DIM), F32)],
        input_output_aliases={6: 0},
        compiler_params=_params(("parallel", "arbitrary"), 4 * _nbytes((CONV_TILE, CONV_COLS), F32)),
    )(proj, pre, pre, dxc, dxc, cw, dproj)


def _ssd_decays(dt_raw, dtb, alog, e_bf, tril_bf):
    dtv = _softplus(dt_raw + dtb)
    a = -jnp.exp(alog)
    cs = _dot_exact_lhs(tril_bf, dtv * a, _NN)
    cs_last = cs[CHUNK - 1:CHUNK, :]
    stack = jnp.concatenate([dtv, jnp.exp(cs), jnp.exp(cs_last - cs)], axis=0)
    full = _head_expand(stack, e_bf)
    return dtv, a, cs, full[:CHUNK], full[CHUNK:2 * CHUNK], full[2 * CHUNK:]


def _split2(x):
    hi = x.astype(BF16)
    return hi, (x - hi.astype(F32)).astype(BF16)


def _head_expand(x, e_bf):
    hi, mid = _split2(x)
    return _dot(hi, e_bf, _NN) + _dot(mid, e_bf, _NN)


def _head_sums(x, e_bf):
    hi, mid = _split2(x)
    return _dot(hi, e_bf, _NT) + _dot(mid, e_bf, _NT)


def _head_mats(cs, cs_t, cb, h, mask):
    seg = cs[:, h:h + 1] - cs_t[h:h + 1, :]
    lmat = jnp.exp(jnp.where(mask, seg, -jnp.inf))
    return lmat, cb * lmat


def _ssd_fwd(xc, proj, dt_raw, dtb, alog, dskip_full, ng, e_bf, name):
    t = xc.shape[0]
    nc = t // CHUNK
    zcb = COL_Z // D_INNER

    def body(xc_ref, z_ref, dt_ref, dtb_ref, alog_ref, dsk_ref, ng_ref, e_ref, y_ref, yb_ref, sprev_ref, s_scr):
        @pl.when(pl.program_id(0) == 0)
        def _():
            s_scr[...] = jnp.zeros_like(s_scr)

        mask = _tri(True)
        tril_bf = mask.astype(BF16)
        e_v = e_ref[...]
        _, _, cs, dt_full, ecs_full, decay_full = _ssd_decays(dt_ref[...], dtb_ref[...], alog_ref[...], e_v, tril_bf)
        cs_t = cs.T
        sprev_ref[0] = s_scr[...]
        for g in range(GROUPS):
            gc = slice(g * GROUP_W, (g + 1) * GROUP_W)
            xs = xc_ref[:, gc]
            xdt = xs * dt_full[:, gc]
            xdt_b = xdt.astype(BF16)
            xdec = (xdt * decay_full[:, gc]).astype(BF16)
            bg = xc_ref[:, D_INNER + g * D_STATE:D_INNER + (g + 1) * D_STATE].astype(BF16)
            cg = xc_ref[:, D_INNER + GROUPS * D_STATE + g * D_STATE:D_INNER + GROUPS * D_STATE + (g + 1) * D_STATE].astype(BF16)
            cb = _dot(cg, bg, _NT)
            s_prev = s_scr[:, gc]
            y_off = ecs_full[:, gc] * _dot(cg, s_prev.astype(BF16), _NN)
            s_scr[:, gc] = s_prev * ecs_full[CHUNK - 1:CHUNK, gc] + _dot(bg, xdec, _TN)
            parts = []
            for r in range(GROUP_W // HEAD_DIM):
                h = g * (GROUP_W // HEAD_DIM) + r
                _, m = _head_mats(cs, cs_t, cb, h, mask)
                parts.append(_dot(m.astype(BF16), xdt_b[:, r * HEAD_DIM:(r + 1) * HEAD_DIM], _NN))
            yg = jnp.concatenate(parts, axis=1) + y_off + dsk_ref[:, gc] * xs
            y_ref[:, gc] = yg
            zv = z_ref[:, gc]
            ygate = yg * (zv * _sigmoid(zv))
            rstd = lax.rsqrt(jnp.mean(ygate * ygate, axis=-1, keepdims=True) + NORM_EPS)
            yb_ref[:, gc] = (ygate * rstd * ng_ref[:, gc]).astype(BF16)

    vec = lambda w: pl.BlockSpec((1, w), lambda i: (0, 0))
    blk = _nbytes((CHUNK, CONV_DIM), F32) + 3 * _nbytes((CHUNK, D_INNER), F32) + _nbytes((D_STATE, D_INNER), F32)
    return pl.pallas_call(
        body, name=name, grid=(nc,),
        in_specs=[pl.BlockSpec((CHUNK, CONV_DIM), lambda i: (i, 0)), pl.BlockSpec((CHUNK, D_INNER), lambda i: (i, zcb)),
                  pl.BlockSpec((CHUNK, DT_PAD), lambda i: (i, 0)), vec(DT_PAD), vec(DT_PAD), vec(D_INNER), vec(D_INNER),
                  pl.BlockSpec((DT_PAD, D_INNER), lambda i: (0, 0))],
        out_specs=[pl.BlockSpec((CHUNK, D_INNER), lambda i: (i, 0)), pl.BlockSpec((CHUNK, D_INNER), lambda i: (i, 0)),
                   pl.BlockSpec((1, D_STATE, D_INNER), lambda i: (i, 0, 0))],
        out_shape=[jax.ShapeDtypeStruct((t, D_INNER), F32), jax.ShapeDtypeStruct((t, D_INNER), BF16),
                   jax.ShapeDtypeStruct((nc, D_STATE, D_INNER), F32)],
        scratch_shapes=[pltpu.VMEM((D_STATE, D_INNER), F32)],
        compiler_params=_params(("arbitrary",), blk),
    )(xc, proj, dt_raw, dtb, alog, dskip_full, ng, e_bf)


def _ssd_bwd(dyb, y, xc, proj, dt_raw, sprev, dtb, alog, dskip_full, ng, e_bf, dproj, name):
    t = xc.shape[0]
    nc = t // CHUNK
    zcb = COL_Z // D_INNER
    hpg = GROUP_W // HEAD_DIM
    rev = lambda i: nc - 1 - i

    def body(dyb_ref, y_ref, xc_ref, z_ref, dt_ref, sprev_ref, dtb_ref, alog_ref, dsk_ref, ng_ref, e_ref, dproj_in,
             dz_ref, dxc_ref, ddt_ref, gng_ref, gdsk_ref, galog_ref, gdtb_ref, ds_scr, sums_scr):
        del dproj_in

        @pl.when(pl.program_id(0) == 0)
        def _():
            ds_scr[...] = jnp.zeros_like(ds_scr)
            gng_ref[...] = jnp.zeros_like(gng_ref)
            gdsk_ref[...] = jnp.zeros_like(gdsk_ref)
            galog_ref[...] = jnp.zeros_like(galog_ref)
            gdtb_ref[...] = jnp.zeros_like(gdtb_ref)

        mask = _tri(True)
        tril_bf = mask.astype(BF16)
        triu_bf = _tri(False).astype(BF16)
        e_v = e_ref[...]
        dt_in = dt_ref[...] + dtb_ref[...]
        dtv, a, cs, dt_full, ecs_full, decay_full = _ssd_decays(dt_ref[...], dtb_ref[...], alog_ref[...], e_v, tril_bf)
        cs_t = cs.T

        lane_h = lax.broadcasted_iota(jnp.int32, (CHUNK, DT_PAD), 1)
        sub_h = lax.broadcasted_iota(jnp.int32, (DT_PAD, CHUNK), 0)
        dcs_rows = jnp.zeros((CHUNK, DT_PAD), F32)
        dcs_cols_t = jnp.zeros((DT_PAD, CHUNK), F32)
        last_cols, dsk_cols = [], []
        for g in range(GROUPS):
            gc = slice(g * GROUP_W, (g + 1) * GROUP_W)
            b_cols = slice(D_INNER + g * D_STATE, D_INNER + (g + 1) * D_STATE)
            c_cols = slice(D_INNER + GROUPS * D_STATE + g * D_STATE, D_INNER + GROUPS * D_STATE + (g + 1) * D_STATE)
            xs = xc_ref[:, gc]
            xdt = xs * dt_full[:, gc]
            xdt_b = xdt.astype(BF16)
            xdec = xdt * decay_full[:, gc]
            xdec_b = xdec.astype(BF16)
            zv = z_ref[:, gc]
            sg = _sigmoid(zv)
            gate = zv * sg
            yv = y_ref[:, gc]
            dybv = dyb_ref[:, gc]
            ygate = yv * gate
            rstd = lax.rsqrt(jnp.mean(ygate * ygate, axis=-1, keepdims=True) + NORM_EPS)
            yn = ygate * rstd
            gng_ref[:, gc] += jnp.sum(dybv * yn, axis=0, keepdims=True)
            dyn = dybv * ng_ref[:, gc]
            dyg = rstd * (dyn - yn * jnp.mean(dyn * yn, axis=-1, keepdims=True))
            dz_ref[:, gc] = (dyg * yv * sg * (1.0 + zv * (1.0 - sg))).astype(BF16)
            dy = dyg * gate
            dy_b = dy.astype(BF16)
            dyo = dy * ecs_full[:, gc]
            dyo_b = dyo.astype(BF16)
            dsk_cols.append(jnp.sum(dy * xs, axis=0, keepdims=True))

            bg = xc_ref[:, b_cols].astype(BF16)
            cg = xc_ref[:, c_cols].astype(BF16)
            s_prev = sprev_ref[0, :, gc]
            s_prev_b = s_prev.astype(BF16)
            dsg = ds_scr[:, gc]
            dsg_b = dsg.astype(BF16)
            cb = _dot(cg, bg, _NT)
            c_s = _dot(cg, s_prev_b, _NN)
            b_ds = _dot(bg, dsg_b, _NN)
            dcb = jnp.zeros((CHUNK, CHUNK), F32)
            parts = []
            for r in range(hpg):
                h = g * hpg + r
                hc = slice(r * HEAD_DIM, (r + 1) * HEAD_DIM)
                lmat, m = _head_mats(cs, cs_t, cb, h, mask)
                dm = _dot(dy_b[:, hc], xdt_b[:, hc], _NT)
                parts.append(_dot(m.astype(BF16), dy_b[:, hc], _TN))
                dcb = dcb + dm * lmat
                w = dm * m
                dcs_rows = jnp.where(lane_h == h, jnp.sum(w, axis=1, keepdims=True), dcs_rows)
                dcs_cols_t = jnp.where(sub_h == h, jnp.sum(w, axis=0, keepdims=True), dcs_cols_t)
            dxdt = jnp.concatenate(parts, axis=1) + decay_full[:, gc] * b_ds
            dcb_b = dcb.astype(BF16)
            dxc_ref[:, c_cols] = _dot(dcb_b, bg, _NN) + _dot(dyo_b, s_prev_b, _NT)
            dxc_ref[:, b_cols] = _dot(dcb_b, cg, _TN) + _dot(xdec_b, dsg_b, _NT)
            cdec = ecs_full[CHUNK - 1:CHUNK, gc]
            ds_scr[:, gc] = _dot(cg, dyo_b, _TN) + cdec * dsg
            dxc_ref[:, gc] = dxdt * dt_full[:, gc] + dsk_ref[:, gc] * dy
            dec_prod = xdec * b_ds
            sums_scr[:CHUNK, gc] = dyo * c_s - dec_prod
            sums_scr[CHUNK:, gc] = dxdt * xs
            last_cols.append(jnp.sum(dec_prod, axis=0, keepdims=True) + cdec * jnp.sum(dsg * s_prev, axis=0, keepdims=True))
        t_sums = _head_sums(sums_scr[...], e_v)
        tail = jnp.concatenate([jnp.concatenate(last_cols, axis=1), jnp.concatenate(dsk_cols, axis=1),
                                jnp.zeros((SUBLANES - 2, D_INNER), F32)], axis=0)
        t_tail = _dot_exact_rhs(tail, e_v, _NT)
        gdsk_ref[...] += t_tail[1:2, :]
        row = lax.broadcasted_iota(jnp.int32, (CHUNK, DT_PAD), 0)
        dcs = dcs_rows - dcs_cols_t.T + t_sums[:CHUNK] + jnp.where(row == CHUNK - 1, t_tail[0:1, :], 0.0)
        dda = _dot_exact_lhs(triu_bf, dcs, _NN)
        galog_ref[...] += jnp.sum(dda * dtv, axis=0, keepdims=True) * a
        ddt = dda * a + t_sums[CHUNK:]
        ddt_raw = jnp.where(lane_h < N_HEADS, ddt * _sigmoid(dt_in), 0.0)
        gdtb_ref[...] += jnp.sum(ddt_raw, axis=0, keepdims=True)
        ddt_ref[...] = ddt_raw.astype(BF16)

    vec = lambda w: pl.BlockSpec((1, w), lambda i: (0, 0))
    blk = (2 * _nbytes((CHUNK, CONV_DIM), F32) + 4 * _nbytes((CHUNK, D_INNER), F32) + 4 * _nbytes((D_STATE, D_INNER), F32))
    return pl.pallas_call(
        body, name=name, grid=(nc,),
        in_specs=[pl.BlockSpec((CHUNK, D_INNER), lambda i: (rev(i), 0)), pl.BlockSpec((CHUNK, D_INNER), lambda i: (rev(i), 0)),
                  pl.BlockSpec((CHUNK, CONV_DIM), lambda i: (rev(i), 0)), pl.BlockSpec((CHUNK, D_INNER), lambda i: (rev(i), zcb)),
                  pl.BlockSpec((CHUNK, DT_PAD), lambda i: (rev(i), 0)), pl.BlockSpec((1, D_STATE, D_INNER), lambda i: (rev(i), 0, 0)),
                  vec(DT_PAD), vec(DT_PAD), vec(D_INNER), vec(D_INNER), pl.BlockSpec((DT_PAD, D_INNER), lambda i: (0, 0)),
                  pl.BlockSpec(memory_space=pl.ANY)],
        out_specs=[pl.BlockSpec((CHUNK, D_INNER), lambda i: (rev(i), zcb)), pl.BlockSpec((CHUNK, CONV_DIM), lambda i: (rev(i), 0)),
                   pl.BlockSpec((CHUNK, DT_PAD), lambda i: (rev(i), 0)), vec(D_INNER), vec(DT_PAD), vec(DT_PAD), vec(DT_PAD)],
        out_shape=[jax.ShapeDtypeStruct(dproj.shape, BF16), jax.ShapeDtypeStruct((t, CONV_DIM), F32),
                   jax.ShapeDtypeStruct((t, DT_PAD), BF16), jax.ShapeDtypeStruct((1, D_INNER), F32),
                   jax.ShapeDtypeStruct((1, DT_PAD), F32), jax.ShapeDtypeStruct((1, DT_PAD), F32),
                   jax.ShapeDtypeStruct((1, DT_PAD), F32)],
        scratch_shapes=[pltpu.VMEM((D_STATE, D_INNER), F32), pltpu.VMEM((2 * CHUNK, D_INNER), F32)],
        input_output_aliases={11: 0},
        compiler_params=_params(("arbitrary",), blk),
    )(dyb, y, xc, proj, dt_raw, sprev, dtb, alog, dskip_full, ng, e_bf, dproj)


def _mesh_pos():
    return lax.axis_index("x"), lax.axis_index("y"), lax.axis_index("c")


def _other_chips(x, y):
    return [(1 - x, y), (x, 1 - y), (1 - x, 1 - y)]


def _all_peers(x, y, c):
    peers = []
    for k in range(1, N_DEV):
        fx, fy, fc = (k >> 2) & 1, (k >> 1) & 1, k & 1
        px, py, pc = x + fx - 2 * x * fx, y + fy - 2 * y * fy, c + fc - 2 * c * fc
        peers.append(((px, py, pc), 4 * px + 2 * py + pc))
    return peers


def _all_gather(shards, name, own_only=()):
    n, n_own = len(shards), len(own_only)

    def body(*refs):
        ins, own_ins = refs[:n], refs[n:n + n_own]
        outs, own_outs = refs[n + n_own:2 * n + n_own], refs[2 * n + n_own:2 * (n + n_own)]
        send_sems, recv_sems, local_sems = refs[2 * (n + n_own):]
        x, y, c = _mesh_pos()
        me, sibling = (x, y, c), (x, y, 1 - c)
        chips = _other_chips(x, y)

        def slot(p):
            return 4 * p[0] + 2 * p[1] + p[2]

        def copy(a, k, block, to, src=None):
            dst = outs[a].at[slot(block)]
            return pltpu.make_async_remote_copy(
                src_ref=dst if src is None else src, dst_ref=dst, send_sem=send_sems.at[a * 7 + k],
                recv_sem=recv_sems.at[a * 7 + k], device_id=to, device_id_type=MESH)

        started = []
        own = []
        for a in range(n_own):
            mine = pltpu.make_async_copy(own_ins[a], own_outs[a].at[slot(me)], local_sems.at[n + a])
            mine.start()
            own.append(mine)
        for a in range(n):
            mine = pltpu.make_async_copy(ins[a], outs[a].at[slot(me)], local_sems.at[a])
            mine.start()
            own.append(mine)
            first = [copy(a, 0, me, sibling, src=ins[a])]
            first += [copy(a, 1 + j, me, (*chip, c), src=ins[a]) for j, chip in enumerate(chips)]
            for cp in first:
                cp.start()
            started += first
        for a in range(n):
            for j, chip in enumerate(chips):
                copy(a, 1 + j, (*chip, c), me).wait_recv()
                fwd = copy(a, 4 + j, (*chip, c), sibling)
                fwd.start()
                started.append(fwd)
        for a in range(n):
            copy(a, 0, sibling, me).wait_recv()
            for j, chip in enumerate(chips):
                copy(a, 4 + j, (*chip, 1 - c), me).wait_recv()
        for cp in started:
            cp.wait_send()
        for mine in own:
            mine.wait()

    return pl.pallas_call(
        body, name=name,
        in_specs=[_HBM] * (n + n_own), out_specs=[_HBM] * (n + n_own),
        out_shape=[jax.ShapeDtypeStruct((N_DEV,) + s.shape, s.dtype) for s in (*shards, *own_only)],
        scratch_shapes=[pltpu.SemaphoreType.DMA((7 * n,)), pltpu.SemaphoreType.DMA((7 * n,)),
                        pltpu.SemaphoreType.DMA((n + n_own,))],
    )(*shards, *own_only)


_SMALL_ROWS = (("norm_mix_g", 8), ("conv_b", 32), ("dt_bias", 1), ("a_log", 1), ("d_skip", 1), ("ssm_norm_g", 16),
               ("v_norm_g", 8), ("v_norm_b", 8), ("w_spatial", 1024), ("b_spatial", 8), ("b_gates", 16), ("norm_mlp_g", 8),
               ("norm_final_g", 8), ("conv_w", 128), ("loss", 1))
_LAST_SMALL = (("norm_mix_g", 8),)


def _packed_rows(table):
    return -(-sum(r for _, r in table) // SUBLANES) * SUBLANES


def _small_offsets(table=_SMALL_ROWS):
    offs, r = {}, 0
    for name, rows in table:
        offs[name] = r
        r += rows
    return offs


def _rows_from(src_ref, dst_ref, r0):
    k, w = src_ref.shape
    if w <= LANES:
        dst_ref[r0:r0 + k, 0:w] = src_ref[...]
        return
    per = w // LANES
    for i in range(k):
        for j in range(per):
            dst_ref[r0 + i * per + j:r0 + i * per + j + 1, :] = src_ref[i:i + 1, j * LANES:(j + 1) * LANES]


def _rows_to(src_ref, r0, dst_ref):
    k, w = dst_ref.shape
    if w <= LANES:
        dst_ref[...] = src_ref[r0:r0 + k, 0:w]
        return
    per = w // LANES
    for i in range(k):
        for j in range(per):
            dst_ref[i:i + 1, j * LANES:(j + 1) * LANES] = src_ref[r0 + i * per + j:r0 + i * per + j + 1, :]


def _pack_small(grads, slot_idx, name):
    names = [n for n, _ in _SMALL_ROWS if n in grads]
    offs = _small_offsets()
    rows = _packed_rows(_SMALL_ROWS)

    def body(slot_ref, *refs):
        del slot_ref
        ins, (packed_ref, land_ref) = refs[:len(names)], refs[len(names):]
        packed_ref[...] = jnp.zeros_like(packed_ref)
        for n, ref in zip(names, ins):
            _rows_from(ref, packed_ref, offs[n])
        land_ref[0] = packed_ref[...]

    whole = lambda shape: pl.BlockSpec(shape, lambda i, slot_ref: (0,) * len(shape))
    grid_spec = pltpu.PrefetchScalarGridSpec(
        num_scalar_prefetch=1, grid=(1,), in_specs=[whole(grads[n].shape) for n in names],
        out_specs=[whole((rows, LANES)), pl.BlockSpec((1, rows, LANES), lambda i, slot_ref: (slot_ref[0], 0, 0))])
    return pl.pallas_call(
        body, name=name, grid_spec=grid_spec,
        out_shape=[jax.ShapeDtypeStruct((rows, LANES), F32), jax.ShapeDtypeStruct((N_DEV, rows, LANES), F32)],
    )(slot_idx, *[grads[n] for n in names])


def _exchange_small(grads, table, name):
    names = [n for n, _ in table]
    offs = _small_offsets(table)
    n_in = len(names)
    packed_rows = _packed_rows(table)

    def body(*refs):
        ins, out_ref = refs[:n_in], refs[n_in]
        packed, send_sems, recv_sems, local_sem = refs[n_in + 1:]
        packed[...] = jnp.zeros_like(packed)
        for n, ref in zip(names, ins):
            _rows_from(ref, packed, offs[n])
        x, y, c = _mesh_pos()
        my_slot = 4 * x + 2 * y + c
        mine = pltpu.make_async_copy(packed, out_ref.at[my_slot], local_sem)
        mine.start()
        copies = []
        for k, (peer, peer_slot) in enumerate(_all_peers(x, y, c)):
            sems = dict(send_sem=send_sems.at[k], recv_sem=recv_sems.at[k], device_id=peer, device_id_type=MESH)
            send = pltpu.make_async_remote_copy(src_ref=packed, dst_ref=out_ref.at[my_slot], **sems)
            send.start()
            copies.append((send, pltpu.make_async_remote_copy(src_ref=packed, dst_ref=out_ref.at[peer_slot], **sems)))
        for send, recv in copies:
            send.wait_send()
            recv.wait_recv()
        mine.wait()

    return pl.pallas_call(
        body, name=name, in_specs=[pl.BlockSpec(memory_space=pltpu.VMEM)] * n_in, out_specs=_HBM,
        out_shape=jax.ShapeDtypeStruct((N_DEV, packed_rows, LANES), F32),
        scratch_shapes=[pltpu.VMEM((packed_rows, LANES), F32), pltpu.SemaphoreType.DMA((N_DEV - 1,)),
                        pltpu.SemaphoreType.DMA((N_DEV - 1,)), pltpu.SemaphoreType.DMA],
    )(*[grads[n] for n in names])


def _swap_with_sibling(grads, name):
    n = len(grads)

    def body(*refs):
        ins, outs = refs[:n], refs[n:2 * n]
        send_sems, recv_sems = refs[2 * n:]
        x, y, c = _mesh_pos()
        copies = []
        for a in range(n):
            for k in range(N_CHIP):
                cp = pltpu.make_async_remote_copy(
                    src_ref=ins[a].at[(1 - c) + 2 * k], dst_ref=outs[a].at[k], send_sem=send_sems.at[a * N_CHIP + k],
                    recv_sem=recv_sems.at[a * N_CHIP + k], device_id=(x, y, 1 - c), device_id_type=MESH)
                cp.start()
                copies.append(cp)
        for cp in copies:
            cp.wait()

    return pl.pallas_call(
        body, name=name, in_specs=[_HBM] * n, out_specs=[_HBM] * n,
        out_shape=[jax.ShapeDtypeStruct((N_CHIP,) + g.shape[1:], g.dtype) for g in grads],
        scratch_shapes=[pltpu.SemaphoreType.DMA((N_CHIP * n,)), pltpu.SemaphoreType.DMA((N_CHIP * n,))],
    )(*grads)


_SEM = pl.BlockSpec(memory_space=pltpu.SEMAPHORE)
_IN_HBM = pl.BlockSpec(memory_space=pltpu.HBM)
_EFFECT = pltpu.SideEffectType.DATAFLOW_SIDE_EFFECTING


def _in_hbm(a):
    return pltpu.with_memory_space_constraint(a, pltpu.HBM)


def _gather_copies(ins, lands, send_sems, recv_sems):
    x, y, c = _mesh_pos()
    my_slot = 4 * x + 2 * y + c
    pairs = []
    for a in range(len(ins)):
        for k, (peer, peer_slot) in enumerate(_all_peers(x, y, c)):
            sems = dict(send_sem=send_sems.at[a * (N_DEV - 1) + k], recv_sem=recv_sems.at[a * (N_DEV - 1) + k],
                        device_id=peer, device_id_type=MESH)
            pairs.append((pltpu.make_async_remote_copy(src_ref=ins[a], dst_ref=lands[a].at[my_slot], **sems),
                          pltpu.make_async_remote_copy(src_ref=ins[a], dst_ref=lands[a].at[peer_slot], **sems)))
    return pairs


def _scatter_copies(ins, lands, send_sems, recv_sems):
    x, y, c = _mesh_pos()
    my_chip = 2 * x + y
    pairs = []
    for a in range(len(ins)):
        for j, chip in enumerate(_other_chips(x, y)):
            there = 2 * chip[0] + chip[1]
            sems = dict(send_sem=send_sems.at[a * 3 + j], recv_sem=recv_sems.at[a * 3 + j],
                        device_id=(*chip, c), device_id_type=MESH)
            pairs.append((pltpu.make_async_remote_copy(src_ref=ins[a].at[there], dst_ref=lands[a].at[my_chip], **sems),
                          pltpu.make_async_remote_copy(src_ref=ins[a].at[my_chip], dst_ref=lands[a].at[there], **sems)))
    return pairs


def _split_start(srcs, lands, copies, per_array, name):
    n = len(srcs)

    def body(*refs):
        ins, land_refs = refs[:n], refs[n:2 * n]
        send_sems, recv_sems = refs[2 * n], refs[2 * n + 1]
        token = refs[-1]
        for send, _ in copies(ins, land_refs, send_sems, recv_sems):
            send.start()
        token[...] = jnp.zeros_like(token)

    outs = pl.pallas_call(
        body, name=name,
        out_shape=(pltpu.SemaphoreType.DMA((per_array * n,)), pltpu.SemaphoreType.DMA((per_array * n,)),
                   *[pltpu.HBM(s.shape, s.dtype) for s in srcs], *[pltpu.HBM(l.shape, l.dtype) for l in lands],
                   jax.ShapeDtypeStruct((SUBLANES, LANES), F32)),
        in_specs=[_IN_HBM] * (2 * n),
        out_specs=(_SEM, _SEM, *[_IN_HBM] * (2 * n), pl.BlockSpec(memory_space=pltpu.VMEM)),
        input_output_aliases={i: 2 + i for i in range(2 * n)},
        compiler_params=pltpu.CompilerParams(has_side_effects=_EFFECT),
    )(*[_in_hbm(s) for s in srcs], *[_in_hbm(l) for l in lands])
    return outs[0], outs[1], list(outs[2:2 + n]), list(outs[2 + n:2 + 2 * n]), outs[-1]


def _split_wait(started, copies, after, name):
    send_sems, recv_sems, srcs, lands, _ = started
    n = len(srcs)

    def body(*refs):
        ins, land_refs = refs[:n], refs[n:2 * n]
        for send, recv in copies(ins, land_refs, refs[2 * n], refs[2 * n + 1]):
            send.wait_send()
            recv.wait_recv()

    outs = pl.pallas_call(
        body, name=name,
        out_shape=(*[pltpu.HBM(s.shape, s.dtype) for s in srcs], *[pltpu.HBM(l.shape, l.dtype) for l in lands]),
        in_specs=[_IN_HBM] * (2 * n) + [_SEM, _SEM, _HBM],
        out_specs=[_IN_HBM] * (2 * n),
        input_output_aliases={i: i for i in range(2 * n)},
        compiler_params=pltpu.CompilerParams(has_side_effects=_EFFECT),
    )(*srcs, *lands, send_sems, recv_sems, after)
    return list(outs[:n]), list(outs[n:])


def _ew_block(rows, cols, slots):
    budget = 2 * 1024 * 1024
    br, bc = rows, cols
    while slots * br * bc * 4 > budget:
        if br % 2 == 0 and (br // 2) % (2 * SUBLANES) == 0:
            br //= 2
        elif bc % 2 == 0 and (bc // 2) % LANES == 0:
            bc //= 2
        else:
            break
    return br, bc


def _add_sibling(grads, recv, c_idx, name):
    _, rows, cols = grads.shape
    br, bc = _ew_block(rows, cols, 3)

    def body(c_ref, g_ref, r_ref, out_ref):
        del c_ref
        out_ref[...] = (g_ref[...].astype(F32) + r_ref[...].astype(F32)).astype(out_ref.dtype)

    grid_spec = pltpu.PrefetchScalarGridSpec(
        num_scalar_prefetch=1, grid=(N_CHIP, rows // br, cols // bc),
        in_specs=[pl.BlockSpec((1, br, bc), lambda k, i, j, c_ref: (c_ref[0] + 2 * k, i, j)),
                  pl.BlockSpec((1, br, bc), lambda k, i, j, c_ref: (k, i, j))],
        out_specs=pl.BlockSpec((1, br, bc), lambda k, i, j, c_ref: (k, i, j)))
    return pl.pallas_call(
        body, name=name, grid_spec=grid_spec, out_shape=jax.ShapeDtypeStruct((N_CHIP, rows, cols), grads.dtype),
        compiler_params=_params(("parallel", "parallel", "parallel"), 3 * _nbytes((br, bc), F32)),
    )(c_idx, grads, recv)


def _adam_math(g, w, m, v):
    m2 = ADAM_B1 * m + (1.0 - ADAM_B1) * g
    v2 = ADAM_B2 * v + (1.0 - ADAM_B2) * (g * g)
    m_hat = m2 * (1.0 / (1.0 - ADAM_B1 ** ADAM_STEP))
    v_hat = v2 * (1.0 / (1.0 - ADAM_B2 ** ADAM_STEP))
    return -ADAM_LR * (m_hat / (jnp.sqrt(v_hat) + ADAM_EPS) + ADAM_WD * w), m2, v2


def _adamw(slots, w, m, v, name, own=None, own_slot=None):
    ns, rows, cols = slots.shape
    br, bc = _ew_block(rows, cols, 2 * ns + 7)

    def update(g, w_ref, m_ref, v_ref, g_ref, d_ref, m2_ref, v2_ref):
        g_ref[...] = g
        d_ref[...], m2_ref[...], v2_ref[...] = _adam_math(g, w_ref[...], m_ref[...], v_ref[...])

    out_shape = [jax.ShapeDtypeStruct((rows, cols), F32)] * 4
    params = _params(("parallel", "parallel"), (2 * ns + 7) * _nbytes((br, bc), F32))
    grid = (rows // br, cols // bc)
    if own is None:
        def body(s_ref, *rest):
            g = s_ref[0].astype(F32)
            for k in range(1, ns):
                g = g + s_ref[k].astype(F32)
            update(g, *rest)

        blk = pl.BlockSpec((br, bc), lambda i, j: (i, j))
        return pl.pallas_call(
            body, name=name, grid=grid,
            in_specs=[pl.BlockSpec((ns, br, bc), lambda i, j: (0, i, j)), blk, blk, blk], out_specs=[blk] * 4,
            out_shape=out_shape, compiler_params=params,
        )(slots, w, m, v)

    def body_own(slot_ref, s_ref, o_ref, *rest):
        g = None
        for k in range(ns):
            term = jnp.where(slot_ref[0] == k, o_ref[k].astype(F32), s_ref[k].astype(F32))
            g = term if g is None else g + term
        update(g, *rest)

    blk = pl.BlockSpec((br, bc), lambda i, j, slot_ref: (i, j))
    stack = pl.BlockSpec((ns, br, bc), lambda i, j, slot_ref: (0, i, j))
    grid_spec = pltpu.PrefetchScalarGridSpec(num_scalar_prefetch=1, grid=grid, in_specs=[stack, stack, blk, blk, blk],
                                             out_specs=[blk] * 4)
    return pl.pallas_call(body_own, name=name, grid_spec=grid_spec, out_shape=out_shape, compiler_params=params,
                          )(own_slot, slots, own, w, m, v)


def _adamw_small(all_g, last_g, params, extra_shapes, name):
    names = [n for n, _ in _SMALL_ROWS if n in params]
    extras = [n for n, _ in _SMALL_ROWS if n not in params]
    offs = _small_offsets()
    n_p = len(names)

    def body(*refs):
        s_ref, last_ref = refs[0], refs[1]
        wmv = refs[2:2 + 3 * n_p]
        outs = refs[2 + 3 * n_p:2 + 7 * n_p]
        extra_refs = refs[2 + 7 * n_p:2 + 7 * n_p + len(extras)]
        summed = refs[-1]
        g, g_last = s_ref[0], last_ref[0]
        for k in range(1, N_DEV):
            g, g_last = g + s_ref[k], g_last + last_ref[k]
        summed[...] = g
        last_offs = _small_offsets(_LAST_SMALL)
        for n, rows in _LAST_SMALL:
            summed[offs[n]:offs[n] + rows, :] = g_last[last_offs[n]:last_offs[n] + rows, :]
        for i, n in enumerate(names):
            w_ref, m_ref, v_ref = wmv[3 * i:3 * i + 3]
            g_ref, d_ref, m2_ref, v2_ref = outs[4 * i:4 * i + 4]
            _rows_to(summed, offs[n], g_ref)
            d_ref[...], m2_ref[...], v2_ref[...] = _adam_math(g_ref[...], w_ref[...], m_ref[...], v_ref[...])
        for n, ref in zip(extras, extra_refs):
            _rows_to(summed, offs[n], ref)

    flat = [a for n in names for a in params[n]]
    out_shape = [jax.ShapeDtypeStruct(params[n][0].shape, F32) for n in names for _ in range(4)]
    out_shape += [jax.ShapeDtypeStruct(s, F32) for s in extra_shapes]
    vmem = pl.BlockSpec(memory_space=pltpu.VMEM)
    res = pl.pallas_call(
        body, name=name, in_specs=[vmem] * (2 + len(flat)), out_specs=[vmem] * len(out_shape), out_shape=out_shape,
        scratch_shapes=[pltpu.VMEM(all_g.shape[1:], F32)],
        compiler_params=pltpu.CompilerParams(vmem_limit_bytes=_vmem_limit(_nbytes(all_g.shape, F32))),
    )(all_g, last_g, *flat)
    return {n: res[4 * i:4 * i + 4] for i, n in enumerate(names)}, res[4 * n_p:]


def _mm_tiles(mode, m, n, k):
    tn = min(n, 1024)
    if mode == "tn":
        return min(m, 1024), tn, min(k, 2048)
    if k <= 2048:
        return min(m, 1024), tn, k
    if k <= 4096:
        return min(m, 512), tn, k
    return min(m, 1024), tn, 2048


def _local_step(x, target, wts, small, exchange):
    t = x.shape[0]
    w_main_t, w_dt_t = wts["w_main_t"], wts["w_dt_t"]
    bsp_t = small["b_spatial"].T
    pad32 = lambda a: jnp.pad(a, ((0, 0), (0, DT_PAD - N_HEADS)))
    dtb, alog = pad32(small["dt_bias"]), pad32(small["a_log"])
    dskip_full = jnp.repeat(small["d_skip"], HEAD_DIM, axis=1)
    head_of_col = lax.broadcasted_iota(jnp.int32, (DT_PAD, D_INNER), 1) // HEAD_DIM
    e_bf = (head_of_col == lax.broadcasted_iota(jnp.int32, (DT_PAD, D_INNER), 0)).astype(BF16)

    def mm(a, b, mode, name, **kw):
        if mode == "nn":
            m, k, n = a.shape[0], a.shape[1], b.shape[1]
        elif mode == "nt":
            m, k, n = a.shape[0], a.shape[1], b.shape[0]
        else:
            m, k, n = a.shape[1], a.shape[0], b.shape[1]
        tm, tn, tk = _mm_tiles(mode, m, n, k)
        tm = min(tm, kw.pop("max_tm", tm))
        kw.setdefault("out_dtypes", (BF16,) if mode == "tn" else (F32,))
        if "extra_specs" in kw:
            kw["extra_specs"] = kw["extra_specs"](tm, tn)
        return _matmul(a, b, mode=mode, tm=tm, tn=tn, tk=tk, name=name, **kw)

    def out_tile(tm, tn):
        return (((tm, tn), lambda i, j: (i, j)),)

    def row_tiles(n_tiles, *vectors, gate_logits=False):
        def specs(tm, tn):
            out = [((tm, tn), lambda i, j: (i, j))] * n_tiles
            if gate_logits:
                out += [((tm, D_MODEL), lambda i, j, cb=COL_GATE // D_MODEL + half: (i, cb)) for half in range(2)]
            return tuple(out) + tuple(((1, w), lambda i, j, cb=cb: (0, cb)) for w, cb in vectors)
        return specs

    vec = lambda w: ((1, w), F32, (1, w), lambda i, j: (0, 0))
    fused_tm = 512

    h = _rms_fwd(x, small["norm_mix_g"], "rms_mix", deps=exchange.begin())
    proj = mm(h, w_main_t, "nt", "proj_main", j_outer=True)
    dt_raw = mm(h, w_dt_t, "nt", "proj_dt")
    y_a = _gmlp_fwd(proj, small["v_norm_g"], small["v_norm_b"], small["w_spatial"], bsp_t, "gmlp_fwd")
    pre_conv, xc = _conv_fwd(proj, wts["conv_w"], small["conv_b"], "conv_fwd")
    y_ssd, y_b, sprev = _ssd_fwd(xc, proj, dt_raw, dtb, alog, dskip_full, small["ssm_norm_g"], e_bf, "ssd_fwd")
    wts = {**wts, **exchange.late_weights(y_b)}
    pa = mm(y_a, wts["w_proj_a"], "nn", "proj_a")
    pb, merged = mm(y_b, wts["w_proj_b"], "nn", "proj_b", epilogue=_merge_epilogue, out_dtypes=(F32, BF16), max_tm=fused_tm,
                    extras=(pa, proj, proj, small["b_gates"], small["b_gates"]),
                    extra_specs=row_tiles(1, (D_MODEL, 0), (D_MODEL, 1), gate_logits=True))
    x1, h2 = mm(merged, wts["w_out"], "nn", "out_proj", epilogue=_residual_rms_epilogue, out_dtypes=(F32, BF16),
                extras=(x, small["norm_mlp_g"]), extra_specs=row_tiles(1, (D_MODEL, 0)))

    def relu_sq(acc, ex, outs):
        r = jnp.maximum(acc, 0.0)
        outs[0][...] = (r * r).astype(BF16)

    act = mm(h2, wts["w_mlp_up"], "nn", "mlp_up", epilogue=relu_sq, out_dtypes=(BF16,), j_outer=True)
    dx2, dx2_b, g_final, _, loss = mm(
        act, wts["w_mlp_down"], "nn", "mlp_down", epilogue=_loss_epilogue, carry=True,
        out_dtypes=(F32, BF16, vec(D_MODEL), vec(D_MODEL), vec(LANES)),
        extras=(x1, small["norm_final_g"], target), extra_specs=lambda tm, tn: (
            ((tm, tn), lambda i, j: (i, j)), ((1, tn), lambda i, j: (0, 0)), ((tm, tn), lambda i, j: (i, j))))

    def relu_sq_bwd(acc, ex, outs):
        outs[0][...] = (acc * 2.0 * jnp.sqrt(ex[0][...].astype(F32))).astype(BF16)

    dup = mm(dx2_b, wts["w_mlp_down"], "nt", "d_act", epilogue=relu_sq_bwd, extras=(act,), extra_specs=out_tile,
             out_dtypes=(BF16,), j_outer=True)
    g_down = mm(act, dx2_b, "tn", "g_mlp_down")
    g_up = mm(h2, dup, "tn", "g_mlp_up")
    started = exchange.reduce("mlp", {"w_mlp_down": g_down, "w_mlp_up": g_up})
    dx1, dx1_b, g_mlp = mm(
        dup, wts["w_mlp_up"], "nt", "d_h2", deps=started, epilogue=_rms_bwd_epilogue, carry=True,
        out_dtypes=(F32, BF16, vec(D_MODEL)), extras=(x1, small["norm_mlp_g"], dx2), extra_specs=lambda tm, tn: (
            ((tm, tn), lambda i, j: (i, j)), ((1, tn), lambda i, j: (0, 0)), ((tm, tn), lambda i, j: (i, j))))

    g_out = mm(merged, dx1_b, "tn", "g_out")
    dpa, dpb, dproj, g_bgates = mm(
        dx1_b, wts["w_out"], "nt", "d_merged", epilogue=_merge_bwd_epilogue, carry=True, max_tm=fused_tm,
        out_dtypes=(BF16, BF16, ((t, MAIN_W), BF16, (fused_tm, 2 * D_MODEL), lambda i, j: (i, COL_GATE // (2 * D_MODEL))),
                    vec(2 * D_MODEL)),
        extras=(pa, pb, proj, proj, small["b_gates"], small["b_gates"]),
        extra_specs=row_tiles(2, (D_MODEL, 0), (D_MODEL, 1), gate_logits=True))
    g_pa = mm(y_a, dpa, "tn", "g_proj_a")
    g_pb = mm(y_b, dpb, "tn", "g_proj_b")
    started = exchange.reduce("proj", {"w_out": g_out, "w_proj_a": g_pa, "w_proj_b": g_pb})
    dya = mm(dpa, wts["w_proj_a"], "nt", "d_ya", deps=started)
    dyb = mm(dpb, wts["w_proj_b"], "nt", "d_yb")

    dproj, g_wsp, g_bsp_t, g_vg, g_vb = _gmlp_bwd(proj, dya, small["v_norm_g"], small["v_norm_b"], small["w_spatial"],
                                                   bsp_t, dproj, "gmlp_bwd")
    dproj, dxc, ddt, g_ng, g_dskip, g_alog, g_dtb = _ssd_bwd(dyb, y_ssd, xc, proj, dt_raw, sprev, dtb, alog, dskip_full,
                                                             small["ssm_norm_g"], e_bf, dproj, "ssd_bwd")
    dproj, g_convw, g_convb = _conv_bwd(proj, pre_conv, dxc, wts["conv_w"], dproj, "conv_bwd")

    small_grads = {
        "conv_w": g_convw, "loss": loss,
        "conv_b": g_convb, "dt_bias": g_dtb, "a_log": g_alog, "d_skip": g_dskip, "ssm_norm_g": g_ng,
        "v_norm_g": g_vg, "v_norm_b": g_vb, "w_spatial": g_wsp.reshape(GROUPS * CHUNK, CHUNK), "b_spatial": g_bsp_t.T,
        "b_gates": g_bgates, "norm_mlp_g": g_mlp, "norm_final_g": g_final,
    }
    g_main_t = mm(dproj, h, "tn", "g_in_main", deps=exchange.small(small_grads))
    g_dt_t = mm(ddt, h, "tn", "g_in_dt")
    started = exchange.reduce("in", {"w_in": _join_w_in(g_main_t, g_dt_t)})

    def input_grad(acc, ex, outs):
        dh = acc + _dot(ex[3][...], ex[4][...], _NN)
        _rms_bwd_epilogue(dh, ex, outs)

    grad_x, g_mix = mm(
        dproj, w_main_t, "nn", "d_h", epilogue=input_grad, deps=started, carry=True, max_tm=fused_tm,
        out_dtypes=(F32, vec(D_MODEL)), extras=(x, small["norm_mix_g"], dx1, ddt, w_dt_t), extra_specs=lambda tm, tn: (
            ((tm, tn), lambda i, j: (i, j)), ((1, tn), lambda i, j: (0, 0)), ((tm, tn), lambda i, j: (i, j)),
            ((tm, DT_PAD), lambda i, j: (i, 0)), ((DT_PAD, D_MODEL), lambda i, j: (0, 0))))

    return grad_x, g_mix


def _split_w_in(w_full_t):
    dt0 = COL_GATE
    w_main_t = jnp.concatenate([w_full_t[:dt0], w_full_t[dt0 + N_HEADS:]], axis=0)
    w_dt_t = jnp.pad(w_full_t[dt0:dt0 + N_HEADS], ((0, DT_PAD - N_HEADS), (0, 0)))
    return w_main_t, w_dt_t


def _join_w_in(g_main_t, g_dt_t):
    dt0 = COL_GATE
    return jnp.concatenate([g_main_t[:dt0], g_dt_t[:N_HEADS], g_main_t[dt0:]], axis=0)


_LATE = ["w_proj_a", "w_proj_b", "w_out", "w_mlp_up", "w_mlp_down"]
_BY_COLS = ("w_mlp_up",)


class _Exchange:
    def __init__(self, late_shards, late_lands):
        self.late_shards, self.late_lands = late_shards, late_lands
        self.c_idx = lax.axis_index("c").astype(jnp.int32).reshape(1)
        self.chip_idx = (2 * lax.axis_index("x") + lax.axis_index("y")).astype(jnp.int32).reshape(1)
        self.pending = []

    def begin(self):
        self.late = _split_start(self.late_shards, self.late_lands, _gather_copies, N_DEV - 1, "gather_late_start")
        return [self.late[-1]]

    def late_weights(self, after):
        _, lands = _split_wait(self.late, _gather_copies, after, "gather_late_wait")
        whole = {}
        for n, g in zip(_LATE, lands):
            whole[n] = jnp.transpose(g, (1, 0, 2)).reshape(g.shape[1], -1) if n in _BY_COLS else g.reshape(-1, g.shape[2])
        return whole

    def reduce(self, tag, grads):
        names = list(grads)
        by_dev = []
        for n in names:
            g = grads[n]
            if n in _BY_COLS:
                by_dev.append(jnp.transpose(g.reshape(g.shape[0], N_DEV, -1), (1, 0, 2)))
            else:
                by_dev.append(g.reshape(N_DEV, -1, g.shape[1]))
        from_sibling = _swap_with_sibling(by_dev, "reduce_cores_" + tag)
        parts = [_add_sibling(g, r, self.c_idx, "add_cores_" + n) for n, g, r in zip(names, by_dev, from_sibling)]
        lands = [lax.empty(p.shape, p.dtype) for p in parts]
        started = _split_start(parts, lands, _scatter_copies, 3, "reduce_chips_start_" + tag)
        self.pending.append((tag, names, started))
        return [started[-1]]

    def small(self, grads):
        dev = 2 * self.chip_idx + self.c_idx
        packed, land = _pack_small(grads, dev, "pack_small")
        self.small_started = _split_start([packed], [land], _gather_copies, N_DEV - 1, "exchange_small_start")
        return [self.small_started[-1]]

    def finish(self, after):
        _, (all_small,) = _split_wait(self.small_started, _gather_copies, after, "exchange_small_wait")
        done = {}
        for tag, names, started in self.pending:
            parts, lands = _split_wait(started, _scatter_copies, after, "reduce_chips_wait_" + tag)
            for n, land, part in zip(names, lands, parts):
                done[n] = (land, part, self.chip_idx)
        return all_small, done


def kernel(x, norm_mix_g, w_in, conv_w, conv_b, dt_bias, a_log, d_skip, ssm_norm_g, v_norm_g, v_norm_b, w_spatial, b_spatial, b_gates, w_proj_a, w_proj_b, w_out, norm_mlp_g, w_mlp_up, w_mlp_down, norm_final_g, loss_target, m_norm_mix_g, m_w_in, m_conv_w, m_conv_b, m_dt_bias, m_a_log, m_d_skip, m_ssm_norm_g, m_v_norm_g, m_v_norm_b, m_w_spatial, m_b_spatial, m_b_gates, m_w_proj_a, m_w_proj_b, m_w_out, m_norm_mlp_g, m_w_mlp_up, m_w_mlp_down, m_norm_final_g, v_norm_mix_g, v_w_in, v_conv_w, v_conv_b, v_dt_bias, v_a_log, v_d_skip, v_ssm_norm_g, v_v_norm_g, v_v_norm_b, v_w_spatial, v_b_spatial, v_b_gates, v_w_proj_a, v_w_proj_b, v_w_out, v_norm_mlp_g, v_w_mlp_up, v_w_mlp_down, v_norm_final_g):
    given = dict(locals())
    names = ["norm_mix_g", "w_in", "conv_w", "conv_b", "dt_bias", "a_log", "d_skip", "ssm_norm_g", "v_norm_g", "v_norm_b",
             "w_spatial", "b_spatial", "b_gates", "w_proj_a", "w_proj_b", "w_out", "norm_mlp_g", "w_mlp_up", "w_mlp_down",
             "norm_final_g"]
    shapes = {n: given[n].shape for n in names}
    dev = 4 * lax.axis_index("x") + 2 * lax.axis_index("y") + lax.axis_index("c")

    shard2d = {"w_in": w_in[0].T, "w_proj_a": w_proj_a[0], "w_proj_b": w_proj_b[0], "w_out": w_out[0],
               "w_mlp_up": w_mlp_up[0], "w_mlp_down": w_mlp_down[0]}
    conv_shard = conv_w.reshape(CONV_WIDTH, -1)
    late_shards = [shard2d[n].astype(BF16) for n in _LATE]
    w_in_all, conv_all, *late_lands = _all_gather([shard2d["w_in"].astype(BF16), conv_shard], "gather_first",
                                                  own_only=late_shards)
    w_main_t, w_dt_t = _split_w_in(w_in_all.reshape(-1, D_MODEL))
    wts = {"w_main_t": w_main_t, "w_dt_t": w_dt_t, "conv_w": jnp.transpose(conv_all, (1, 0, 2)).reshape(CONV_WIDTH, -1)}
    small = {"norm_mix_g": norm_mix_g, "conv_b": conv_b, "dt_bias": dt_bias, "a_log": a_log, "d_skip": d_skip,
             "ssm_norm_g": ssm_norm_g, "v_norm_g": v_norm_g, "v_norm_b": v_norm_b, "w_spatial": w_spatial[0],
             "b_spatial": b_spatial[0], "b_gates": b_gates, "norm_mlp_g": norm_mlp_g,
             "norm_final_g": norm_final_g.reshape(1, -1)}

    exchange = _Exchange(late_shards, late_lands)
    grad_x, g_mix = _local_step(x[0], loss_target[0], wts, small, exchange)

    out = {}
    all_small, large = exchange.finish(grad_x)
    for n, (slots, own, own_slot) in large.items():
        moments = [given["m_" + n][0], given["v_" + n][0]]
        if n == "w_in":
            moments = [mom.T for mom in moments]
        res = _adamw(slots, shard2d[n], *moments, "adamw_" + n, own=own, own_slot=own_slot)
        out[n] = [(r.T if n == "w_in" else r).reshape(shapes[n]) for r in res]

    last_small = _exchange_small({"norm_mix_g": g_mix}, _LAST_SMALL, "exchange_last")
    small["w_spatial"] = small["w_spatial"].reshape(GROUPS * CHUNK, CHUNK)
    params = {n: (w2d, given["m_" + n].reshape(w2d.shape), given["v_" + n].reshape(w2d.shape)) for n, w2d in small.items()}
    updated, (g_conv_full, loss_all) = _adamw_small(all_small, last_small, params, [(CONV_WIDTH, CONV_DIM), (1, LANES)],
                                                    "adamw_small")
    for n, res in updated.items():
        out[n] = [r.reshape(shapes[n]) for r in res]
    width = shapes["conv_w"][-1]
    g_conv = lax.dynamic_slice(g_conv_full, (0, dev * width), (CONV_WIDTH, width))
    res = _adamw(g_conv[None], conv_shard, m_conv_w.reshape(CONV_WIDTH, -1), v_conv_w.reshape(CONV_WIDTH, -1), "adamw_conv_w")
    out["conv_w"] = [r.reshape(shapes["conv_w"]) for r in res]

    loss = loss_all[0, 0]
    return (loss, grad_x[None], *[out[n][0] for n in names], *[out[n][1] for n in names],
            *[out[n][2] for n in names], *[out[n][3] for n in names])
```

```python
import functools
import math

import jax
import jax.numpy as jnp
from jax import lax
from jax.experimental import pallas as pl
from jax.experimental.pallas import tpu as pltpu

F32 = jnp.float32
BF16 = jnp.bfloat16
MESH = pl.DeviceIdType.MESH

D_MODEL = 1024
NORM_EPS = 1e-6
CHUNK = 128
GROUPS = 8
D_INNER = 2048
HEAD_DIM = 64
N_HEADS = 32
D_STATE = 128
CONV_WIDTH = 4
CONV_DIM = 4096
D_FF = 4096
GROUP_W = D_INNER // GROUPS
N_DEV = 8
N_CHIP = 4

ADAM_LR = 0.001
ADAM_B1 = 0.9
ADAM_B2 = 0.999
ADAM_EPS = 1e-08
ADAM_WD = 0.01
ADAM_STEP = 10

MAIN_W = 2 * D_MODEL + D_INNER + CONV_DIM + 2 * D_MODEL
COL_Z = 2048
COL_XBC = 4096
COL_GATE = 8192
DT_PAD = 128

LANES = 128
SUBLANES = 8
VMEM_BYTES_V7X = 64 * 1024 * 1024
VMEM_BODY_TEMP = 24 * 1024 * 1024


def _vmem_limit(block_bytes):
    return int(min(2 * block_bytes + VMEM_BODY_TEMP, VMEM_BYTES_V7X - 8 * 1024 * 1024))


def _nbytes(shape, dtype):
    return math.prod(shape) * jnp.dtype(dtype).itemsize


_HBM = pl.BlockSpec(memory_space=pl.ANY)


def _params(sem, block_bytes):
    return pltpu.CompilerParams(dimension_semantics=sem, vmem_limit_bytes=_vmem_limit(block_bytes))


def _sigmoid(x):
    return 1.0 / (1.0 + jnp.exp(-x))


def _softplus(x):
    e = jnp.exp(-jnp.abs(x))
    u = 1.0 + e
    log1p_e = jnp.where(u == 1.0, e, jnp.log(u) * (e / jnp.where(u == 1.0, 1.0, u - 1.0)))
    return jnp.maximum(x, 0.0) + log1p_e


_SQRT_HALF = 0.7071067811865476
_INV_SQRT_2PI = 0.3989422804014327


def _gelu(x):
    return x * (lax.erf(x * _SQRT_HALF) + 1.0) * 0.5


def _gelu_grad(x):
    return 0.5 * (1.0 + lax.erf(x * _SQRT_HALF)) + x * jnp.exp(-0.5 * x * x) * _INV_SQRT_2PI


def _dot(a, b, dims):
    return lax.dot_general(a, b, (dims, ((), ())), preferred_element_type=F32)


_NN = ((1,), (0,))
_NT = ((1,), (1,))
_TN = ((0,), (0,))


def _split3(x):
    hi = x.astype(BF16)
    r1 = x - hi.astype(F32)
    mid = r1.astype(BF16)
    lo = (r1 - mid.astype(F32)).astype(BF16)
    return hi, mid, lo


def _dot_exact_rhs(x, e, dims):
    hi, mid, lo = _split3(x)
    return _dot(hi, e, dims) + _dot(mid, e, dims) + _dot(lo, e, dims)


def _dot_exact_lhs(e, x, dims):
    hi, mid, lo = _split3(x)
    return _dot(e, hi, dims) + _dot(e, mid, dims) + _dot(e, lo, dims)


def _tri(lower):
    r = lax.broadcasted_iota(jnp.int32, (CHUNK, CHUNK), 0)
    c = lax.broadcasted_iota(jnp.int32, (CHUNK, CHUNK), 1)
    return (r >= c) if lower else (r <= c)


def _matmul(a, b, *, mode, tm, tn, tk, out_dtypes, name, epilogue=None, extras=(), extra_specs=(), j_outer=False, deps=(),
            carry=False):
    if mode == "nn":
        (m, k), (_, n) = a.shape, b.shape
    elif mode == "nt":
        (m, k), (n, _) = a.shape, b.shape
    else:
        (k, m), (_, n) = a.shape, b.shape
    assert m % tm == 0 and n % tn == 0 and k % tk == 0, (name, m, n, k, tm, tn, tk)
    nk = k // tk
    n_extra, n_out = len(extras), len(out_dtypes)
    first_out = 2 + n_extra + len(deps)
    dims = {"nn": _NN, "nt": _NT, "tn": _TN}[mode]
    if epilogue is None:
        def epilogue(acc, ex, outs):
            outs[0][...] = acc.astype(outs[0].dtype)

    def body(*refs):
        a_ref, b_ref = refs[0], refs[1]
        ex_refs = refs[2:2 + n_extra]
        outs = refs[first_out:first_out + n_out]
        p = _dot(a_ref[...], b_ref[...], dims)
        if nk == 1:
            epilogue(p, ex_refs, outs)
        else:
            acc_ref = refs[first_out + n_out]
            kk = pl.program_id(2)

            @pl.when(kk == 0)
            def _():
                acc_ref[...] = p

            @pl.when(kk > 0)
            def _():
                acc_ref[...] += p

            @pl.when(kk == nk - 1)
            def _():
                epilogue(acc_ref[...], ex_refs, outs)

    if j_outer:
        grid = (n // tn, m // tm, nk)
        ij = lambda g0, g1: (g1, g0)
    else:
        grid = (m // tm, n // tn, nk)
        ij = lambda g0, g1: (g0, g1)

    def wrap(fn):
        return lambda g0, g1, kk: fn(*ij(g0, g1), kk)

    if mode == "nn":
        a_spec = pl.BlockSpec((tm, tk), wrap(lambda i, j, kk: (i, kk)))
        b_spec = pl.BlockSpec((tk, tn), wrap(lambda i, j, kk: (kk, j)))
        a_blk, b_blk = (tm, tk), (tk, tn)
    elif mode == "nt":
        a_spec = pl.BlockSpec((tm, tk), wrap(lambda i, j, kk: (i, kk)))
        b_spec = pl.BlockSpec((tn, tk), wrap(lambda i, j, kk: (j, kk)))
        a_blk, b_blk = (tm, tk), (tn, tk)
    else:
        a_spec = pl.BlockSpec((tk, tm), wrap(lambda i, j, kk: (kk, i)))
        b_spec = pl.BlockSpec((tk, tn), wrap(lambda i, j, kk: (kk, j)))
        a_blk, b_blk = (tk, tm), (tk, tn)
    ex_specs = [pl.BlockSpec(shape, wrap(lambda i, j, kk, f=f: f(i, j))) for shape, f in extra_specs]
    outs = [o if isinstance(o, tuple) else ((m, n), o, (tm, tn), lambda i, j: (i, j)) for o in out_dtypes]
    out_spec = [pl.BlockSpec(blk_shape, wrap(lambda i, j, kk, f=f: f(i, j))) for _, _, blk_shape, f in outs]
    out_shape = [jax.ShapeDtypeStruct(shape, dt) for shape, dt, _, _ in outs]
    blk = (_nbytes(a_blk, a.dtype) + _nbytes(b_blk, b.dtype) + sum(_nbytes(s, F32) for s, _ in extra_specs)
           + sum(_nbytes(blk_shape, dt) for _, dt, blk_shape, _ in outs) + _nbytes((tm, tn), F32))
    order = ("arbitrary",) * 3 if carry else ("parallel", "parallel", "arbitrary")
    res = pl.pallas_call(
        body, name=name, grid=grid,
        in_specs=[a_spec, b_spec] + ex_specs + [_HBM] * len(deps), out_specs=out_spec, out_shape=out_shape,
        scratch_shapes=[pltpu.VMEM((tm, tn), F32)] if nk > 1 else [],
        compiler_params=_params(order, blk),
    )(a, b, *extras, *deps)
    return res[0] if n_out == 1 else res


ROW_TILE = 256


def _row_spec(width, col_block=0, tile=ROW_TILE):
    return pl.BlockSpec((tile, width), lambda i, cb=col_block: (i, cb))


def _vec_spec(width, col_block=0):
    return pl.BlockSpec((1, width), lambda i, cb=col_block: (0, cb))


def _rms_fwd(x, g, name, deps=()):
    t = x.shape[0]

    def body(x_ref, g_ref, *rest):
        h_ref = rest[-1]
        xv = x_ref[...]
        r = lax.rsqrt(jnp.mean(xv * xv, axis=-1, keepdims=True) + NORM_EPS)
        h_ref[...] = (xv * r * g_ref[...]).astype(BF16)

    return pl.pallas_call(
        body, name=name, grid=(t // ROW_TILE,),
        in_specs=[_row_spec(D_MODEL), _vec_spec(D_MODEL)] + [_HBM] * len(deps), out_specs=_row_spec(D_MODEL),
        out_shape=jax.ShapeDtypeStruct((t, D_MODEL), BF16),
        compiler_params=_params(("parallel",), 3 * _nbytes((ROW_TILE, D_MODEL), F32)),
    )(x, g, *deps)


def _rms_scale(xv):
    r = lax.rsqrt(jnp.mean(xv * xv, axis=-1, keepdims=True) + NORM_EPS)
    return r, xv * r


def _rms_pullback(xv, g, dh):
    r, xh = _rms_scale(xv)
    dyg = dh * g
    return r * (dyg - xh * jnp.mean(dyg * xh, axis=-1, keepdims=True)), jnp.sum(dh * xh, axis=0, keepdims=True)


def _first_row_tile():
    return pl.program_id(0) == 0


def _residual_rms_epilogue(acc, ex, outs):
    x1 = acc + ex[0][...]
    outs[0][...] = x1
    _, xh = _rms_scale(x1)
    outs[1][...] = (xh * ex[1][...]).astype(BF16)


def _loss_epilogue(acc, ex, outs):
    dx_ref, dxb_ref, gg_ref, sq_ref, tot_ref = outs
    gv = ex[1][...]
    r, xh = _rms_scale(acc + ex[0][...])
    err = xh * gv - ex[2][...]
    dy = err * (1.0 / D_MODEL)
    dyg = dy * gv
    dx = r * (dyg - xh * jnp.mean(dyg * xh, axis=-1, keepdims=True))
    dx_ref[...] = dx
    dxb_ref[...] = dx.astype(BF16)

    @pl.when(_first_row_tile())
    def _():
        gg_ref[...] = jnp.zeros_like(gg_ref)
        sq_ref[...] = jnp.zeros_like(sq_ref)

    gg_ref[...] += jnp.sum(dy * xh, axis=0, keepdims=True)
    sq_ref[...] += jnp.sum(err * err, axis=0, keepdims=True)
    tot_ref[...] = jnp.broadcast_to(jnp.sum(sq_ref[...], axis=1, keepdims=True) * (0.5 / D_MODEL), tot_ref.shape)


def _rms_bwd_epilogue(dh, ex, outs):
    dx, gg = _rms_pullback(ex[0][...], ex[1][...], dh)
    dx = dx + ex[2][...]
    outs[0][...] = dx
    if len(outs) == 3:
        outs[1][...] = dx.astype(BF16)

    @pl.when(_first_row_tile())
    def _():
        outs[-1][...] = jnp.zeros_like(outs[-1])

    outs[-1][...] += gg


def _merge_epilogue(acc, ex, outs):
    outs[0][...] = acc
    ga = _sigmoid(ex[1][...].astype(F32) + ex[3][...])
    gb = _sigmoid(ex[2][...].astype(F32) + ex[4][...])
    outs[1][...] = (ga * ex[0][...] + gb * acc).astype(BF16)


def _merge_bwd_epilogue(dm, ex, outs):
    dpa_ref, dpb_ref, dgl_ref, gb_ref = outs
    ga = _sigmoid(ex[2][...].astype(F32) + ex[4][...])
    gb = _sigmoid(ex[3][...].astype(F32) + ex[5][...])
    dpa_ref[...] = (dm * ga).astype(BF16)
    dpb_ref[...] = (dm * gb).astype(BF16)
    dla = dm * ex[0][...] * ga * (1.0 - ga)
    dlb = dm * ex[1][...] * gb * (1.0 - gb)
    dgl_ref[:, :D_MODEL] = dla.astype(BF16)
    dgl_ref[:, D_MODEL:] = dlb.astype(BF16)

    @pl.when(_first_row_tile())
    def _():
        gb_ref[...] = jnp.zeros_like(gb_ref)

    gb_ref[:, :D_MODEL] += jnp.sum(dla, axis=0, keepdims=True)
    gb_ref[:, D_MODEL:] += jnp.sum(dlb, axis=0, keepdims=True)


GMLP_TILE = 512
GMLP_NC = GMLP_TILE // CHUNK


def _gmlp_common(u_pre, v_pre, vg, vb):
    u = _gelu(u_pre)
    v = _gelu(v_pre)
    mu = jnp.mean(v, axis=-1, keepdims=True)
    vc = v - mu
    rstd = lax.rsqrt(jnp.mean(vc * vc, axis=-1, keepdims=True) + NORM_EPS)
    vh = vc * rstd
    vn = vh * vg + vb
    return u, vh, vn, rstd


def _chunks_to_lanes(x, g):
    return jnp.concatenate([x[c * CHUNK:(c + 1) * CHUNK, g * CHUNK:(g + 1) * CHUNK] for c in range(GMLP_NC)], axis=1)


def _gmlp_fwd(proj, vg, vb, wsp, bsp_t, name):
    t = proj.shape[0]

    def body(u_ref, v_ref, vg_ref, vb_ref, w_ref, b_ref, ya_ref):
        u, _, vn, _ = _gmlp_common(u_ref[...].astype(F32), v_ref[...].astype(F32), vg_ref[...], vb_ref[...])
        mask = _tri(True)
        bt = b_ref[...]
        for g in range(GROUPS):
            w = jnp.where(mask, w_ref[g], 0.0).astype(BF16)
            vcat = _chunks_to_lanes(vn, g).astype(BF16)
            s = _dot(w, vcat, _NN) + bt[:, g:g + 1]
            for c in range(GMLP_NC):
                rows, cols = slice(c * CHUNK, (c + 1) * CHUNK), slice(g * CHUNK, (g + 1) * CHUNK)
                ya_ref[rows, cols] = (u[rows, cols] * s[:, c * CHUNK:(c + 1) * CHUNK]).astype(BF16)

    return pl.pallas_call(
        body, name=name, grid=(t // GMLP_TILE,),
        in_specs=[_row_spec(D_MODEL, 0, GMLP_TILE), _row_spec(D_MODEL, 1, GMLP_TILE), _vec_spec(D_MODEL),
                  _vec_spec(D_MODEL), pl.BlockSpec((GROUPS, CHUNK, CHUNK), lambda i: (0, 0, 0)),
                  pl.BlockSpec((CHUNK, GROUPS), lambda i: (0, 0))],
        out_specs=_row_spec(D_MODEL, 0, GMLP_TILE),
        out_shape=jax.ShapeDtypeStruct((t, D_MODEL), BF16),
        compiler_params=_params(("parallel",), 3 * _nbytes((GMLP_TILE, D_MODEL), F32)),
    )(proj, proj, vg, vb, wsp, bsp_t)


def _gmlp_bwd(proj, dya, vg, vb, wsp, bsp_t, dproj, name):
    t = proj.shape[0]

    def body(u_ref, v_ref, dya_ref, vg_ref, vb_ref, w_ref, b_ref, dproj_in, duv_ref, gw_ref, gbt_ref, gvg_ref, gvb_ref,
             dvn_scr, du_scr):
        del dproj_in
        u_pre, v_pre = u_ref[...].astype(F32), v_ref[...].astype(F32)
        vgv = vg_ref[...]
        u, vh, vn, rstd = _gmlp_common(u_pre, v_pre, vgv, vb_ref[...])
        dya = dya_ref[...]
        mask = _tri(True)
        bt = b_ref[...]
        first = pl.program_id(0) == 0

        @pl.when(first)
        def _():
            gw_ref[...] = jnp.zeros_like(gw_ref)
            gbt_ref[...] = jnp.zeros_like(gbt_ref)
            gvg_ref[...] = jnp.zeros_like(gvg_ref)
            gvb_ref[...] = jnp.zeros_like(gvb_ref)

        lane = lax.broadcasted_iota(jnp.int32, (CHUNK, GROUPS), 1)
        gbt = jnp.zeros((CHUNK, GROUPS), F32)
        for g in range(GROUPS):
            w = jnp.where(mask, w_ref[g], 0.0).astype(BF16)
            vcat = _chunks_to_lanes(vn, g).astype(BF16)
            s = _dot(w, vcat, _NN) + bt[:, g:g + 1]
            ds = _chunks_to_lanes(dya * u, g)
            gbt = jnp.where(lane == g, jnp.sum(ds, axis=1, keepdims=True), gbt)
            dsb = ds.astype(BF16)
            gw_ref[g] += jnp.where(mask, _dot(dsb, vcat, _NT), 0.0)
            dv = _dot(w, dsb, _TN)
            for c in range(GMLP_NC):
                rows, cols = slice(c * CHUNK, (c + 1) * CHUNK), slice(g * CHUNK, (g + 1) * CHUNK)
                dvn_scr[rows, cols] = dv[:, c * CHUNK:(c + 1) * CHUNK]
                du_scr[rows, cols] = dya[rows, cols] * s[:, c * CHUNK:(c + 1) * CHUNK]
        gbt_ref[...] += gbt
        dvn = dvn_scr[...]
        gvg_ref[...] += jnp.sum(dvn * vh, axis=0, keepdims=True)
        gvb_ref[...] += jnp.sum(dvn, axis=0, keepdims=True)
        dvh = dvn * vgv
        dv = rstd * (dvh - jnp.mean(dvh, axis=-1, keepdims=True) - vh * jnp.mean(dvh * vh, axis=-1, keepdims=True))
        duv_ref[:, :D_MODEL] = (du_scr[...] * _gelu_grad(u_pre)).astype(BF16)
        duv_ref[:, D_MODEL:] = (dv * _gelu_grad(v_pre)).astype(BF16)

    return pl.pallas_call(
        body, name=name, grid=(t // GMLP_TILE,),
        in_specs=[_row_spec(D_MODEL, 0, GMLP_TILE), _row_spec(D_MODEL, 1, GMLP_TILE), _row_spec(D_MODEL, 0, GMLP_TILE),
                  _vec_spec(D_MODEL), _vec_spec(D_MODEL), pl.BlockSpec((GROUPS, CHUNK, CHUNK), lambda i: (0, 0, 0)),
                  pl.BlockSpec((CHUNK, GROUPS), lambda i: (0, 0)), pl.BlockSpec(memory_space=pl.ANY)],
        out_specs=[_row_spec(2 * D_MODEL, 0, GMLP_TILE), pl.BlockSpec((GROUPS, CHUNK, CHUNK), lambda i: (0, 0, 0)),
                   pl.BlockSpec((CHUNK, GROUPS), lambda i: (0, 0)), _vec_spec(D_MODEL), _vec_spec(D_MODEL)],
        out_shape=[jax.ShapeDtypeStruct(dproj.shape, BF16), jax.ShapeDtypeStruct((GROUPS, CHUNK, CHUNK), F32),
                   jax.ShapeDtypeStruct((CHUNK, GROUPS), F32), jax.ShapeDtypeStruct((1, D_MODEL), F32),
                   jax.ShapeDtypeStruct((1, D_MODEL), F32)],
        scratch_shapes=[pltpu.VMEM((GMLP_TILE, D_MODEL), F32), pltpu.VMEM((GMLP_TILE, D_MODEL), F32)],
        input_output_aliases={7: 0},
        compiler_params=_params(("arbitrary",), 6 * _nbytes((GMLP_TILE, D_MODEL), F32)),
    )(proj, proj, dya, vg, vb, wsp, bsp_t, dproj)


CONV_TILE = 512
CONV_COLS = 1024
CONV_RB = 32
HALO = SUBLANES


def _conv_fwd(proj, cw, cb, name):
    t = proj.shape[0]
    nj = CONV_DIM // CONV_COLS
    xcb = COL_XBC // CONV_COLS
    before = 2 * HALO
    rb = CONV_TILE // before

    def body(x_ref, prev_ref, cw_ref, cb_ref, pre_ref, xc_ref):
        i = pl.program_id(1)
        cw_v = cw_ref[...]
        cb_v = cb_ref[...]
        for b in range(CONV_TILE // CONV_RB):
            if b == 0:
                prev = jnp.where(i > 0, prev_ref[...].astype(F32)[HALO:, :], 0.0)
                ext = jnp.concatenate([prev, x_ref[:CONV_RB, :].astype(F32)], axis=0)
            else:
                ext = x_ref[b * CONV_RB - before:(b + 1) * CONV_RB, :].astype(F32)[HALO:, :]
            pre = cb_v + cw_v[CONV_WIDTH - 1:CONV_WIDTH, :] * ext[HALO:, :]
            for k in range(CONV_WIDTH - 1):
                back = CONV_WIDTH - 1 - k
                pre = pre + cw_v[k:k + 1, :] * pltpu.roll(ext, back, 0)[HALO:, :]
            pre_ref[b * CONV_RB:(b + 1) * CONV_RB, :] = pre
            xc_ref[b * CONV_RB:(b + 1) * CONV_RB, :] = pre * _sigmoid(pre)

    tile = pl.BlockSpec((CONV_TILE, CONV_COLS), lambda j, i: (i, j))
    return pl.pallas_call(
        body, name=name, grid=(nj, t // CONV_TILE),
        in_specs=[pl.BlockSpec((CONV_TILE, CONV_COLS), lambda j, i: (i, xcb + j)),
                  pl.BlockSpec((before, CONV_COLS), lambda j, i: (jnp.maximum(i * rb - 1, 0), xcb + j)),
                  pl.BlockSpec((CONV_WIDTH, CONV_COLS), lambda j, i: (0, j)),
                  pl.BlockSpec((1, CONV_COLS), lambda j, i: (0, j))],
        out_specs=[tile, tile],
        out_shape=[jax.ShapeDtypeStruct((t, CONV_DIM), F32), jax.ShapeDtypeStruct((t, CONV_DIM), F32)],
        compiler_params=_params(("parallel", "parallel"), 4 * _nbytes((CONV_TILE, CONV_COLS), F32)),
    )(proj, proj, cw, cb)


def _fold_rows(v):
    out = v[:SUBLANES]
    for r in range(1, v.shape[0] // SUBLANES):
        out = out + v[r * SUBLANES:(r + 1) * SUBLANES]
    return out


def _conv_bwd(proj, pre, dxc, cw, dproj, name):
    t = proj.shape[0]
    nj = CONV_DIM // CONV_COLS
    ni = t // CONV_TILE
    xcb = COL_XBC // CONV_COLS
    rb = CONV_TILE // HALO
    last_rb = t // HALO - 1

    def body(x_ref, p_ref, pnext_ref, d_ref, dnext_ref, cw_ref, dproj_in, dx_ref, gw_ref, gb_ref):
        del dproj_in
        i = pl.program_id(1)
        cw_v = cw_ref[...]

        def dpre_of(p, d):
            sg = _sigmoid(p)
            return d * sg * (1.0 + p * (1.0 - sg))

        @pl.when(i == 0)
        def _():
            gw_ref[...] = jnp.zeros_like(gw_ref)
            gb_ref[...] = jnp.zeros_like(gb_ref)

        head = dpre_of(pnext_ref[...], jnp.where(i < ni - 1, dnext_ref[...], 0.0))
        gb_acc = jnp.zeros((SUBLANES, CONV_COLS), F32)
        gw_acc = [jnp.zeros((SUBLANES, CONV_COLS), F32) for _ in range(CONV_WIDTH)]
        for b in reversed(range(CONV_TILE // CONV_RB)):
            rows = slice(b * CONV_RB, (b + 1) * CONV_RB)
            cur = dpre_of(p_ref[rows, :], d_ref[rows, :])
            ext = jnp.concatenate([cur, head], axis=0)
            xv = x_ref[rows, :].astype(F32)
            dx = None
            for k in range(CONV_WIDTH):
                shift = CONV_WIDTH - 1 - k
                win = cur if shift == 0 else pltpu.roll(ext, CONV_RB + HALO - shift, 0)[:CONV_RB, :]
                term = cw_v[k:k + 1, :] * win
                dx = term if dx is None else dx + term
                gw_acc[k] = gw_acc[k] + _fold_rows(win * xv)
            dx_ref[rows, :] = dx.astype(BF16)
            gb_acc = gb_acc + _fold_rows(cur)
            head = cur[:HALO]
        gb_ref[...] += jnp.sum(gb_acc, axis=0, keepdims=True)
        for k in range(CONV_WIDTH):
            gw_ref[k:k + 1, :] += jnp.sum(gw_acc[k], axis=0, keepdims=True)

    tile = pl.BlockSpec((CONV_TILE, CONV_COLS), lambda j, i: (i, j))
    after = pl.BlockSpec((HALO, CONV_COLS), lambda j, i: (jnp.minimum((i + 1) * rb, last_rb), j))
    return pl.pallas_call(
        body, name=name, grid=(nj, ni),
        in_specs=[pl.BlockSpec((CONV_TILE, CONV_COLS), lambda j, i: (i, xcb + j)), tile, after, tile, after,
                  pl.BlockSpec((CONV_WIDTH, CONV_COLS), lambda j, i: (0, j)),
                  pl.BlockSpec(memory_space=pl.ANY)],
        out_specs=[pl.BlockSpec((CONV_TILE, CONV_COLS), lambda j, i: (i, xcb + j)),
                   pl.BlockSpec((CONV_WIDTH, CONV_COLS), lambda j, i: (0, j)),
                   pl.BlockSpec((1, CONV_COLS), lambda j, i: (0, j))],
        out_shape=[jax.ShapeDtypeStruct(dproj.shape, BF16), jax.ShapeDtypeStruct((CONV_WIDTH, CONV_DIM), F32),
                   jax.ShapeDtypeStruct((1, CONV_DIM), F32)],
        input_output_aliases={6: 0},
        compiler_params=_params(("parallel", "arbitrary"), 4 * _nbytes((CONV_TILE, CONV_COLS), F32)),
    )(proj, pre, pre, dxc, dxc, cw, dproj)


def _ssd_decays(dt_raw, dtb, alog, e_bf, tril_bf):
    dtv = _softplus(dt_raw + dtb)
    a = -jnp.exp(alog)
    cs = _dot_exact_lhs(tril_bf, dtv * a, _NN)
    cs_last = cs[CHUNK - 1:CHUNK, :]
    stack = jnp.concatenate([dtv, jnp.exp(cs), jnp.exp(cs_last - cs)], axis=0)
    full = _head_expand(stack, e_bf)
    return dtv, a, cs, full[:CHUNK], full[CHUNK:2 * CHUNK], full[2 * CHUNK:]


def _split2(x):
    hi = x.astype(BF16)
    return hi, (x - hi.astype(F32)).astype(BF16)


def _head_expand(x, e_bf):
    hi, mid = _split2(x)
    return _dot(hi, e_bf, _NN) + _dot(mid, e_bf, _NN)


def _head_sums(x, e_bf):
    hi, mid = _split2(x)
    return _dot(hi, e_bf, _NT) + _dot(mid, e_bf, _NT)


def _head_mats(cs, cs_t, cb, h, mask):
    seg = cs[:, h:h + 1] - cs_t[h:h + 1, :]
    lmat = jnp.exp(jnp.where(mask, seg, -jnp.inf))
    return lmat, cb * lmat


def _ssd_fwd(xc, proj, dt_raw, dtb, alog, dskip_full, ng, e_bf, name):
    t = xc.shape[0]
    nc = t // CHUNK
    zcb = COL_Z // D_INNER

    def body(xc_ref, z_ref, dt_ref, dtb_ref, alog_ref, dsk_ref, ng_ref, e_ref, y_ref, yb_ref, sprev_ref, s_scr):
        @pl.when(pl.program_id(0) == 0)
        def _():
            s_scr[...] = jnp.zeros_like(s_scr)

        mask = _tri(True)
        tril_bf = mask.astype(BF16)
        e_v = e_ref[...]
        _, _, cs, dt_full, ecs_full, decay_full = _ssd_decays(dt_ref[...], dtb_ref[...], alog_ref[...], e_v, tril_bf)
        cs_t = cs.T
        sprev_ref[0] = s_scr[...]
        for g in range(GROUPS):
            gc = slice(g * GROUP_W, (g + 1) * GROUP_W)
            xs = xc_ref[:, gc]
            xdt = xs * dt_full[:, gc]
            xdt_b = xdt.astype(BF16)
            xdec = (xdt * decay_full[:, gc]).astype(BF16)
            bg = xc_ref[:, D_INNER + g * D_STATE:D_INNER + (g + 1) * D_STATE].astype(BF16)
            cg = xc_ref[:, D_INNER + GROUPS * D_STATE + g * D_STATE:D_INNER + GROUPS * D_STATE + (g + 1) * D_STATE].astype(BF16)
            cb = _dot(cg, bg, _NT)
            s_prev = s_scr[:, gc]
            y_off = ecs_full[:, gc] * _dot(cg, s_prev.astype(BF16), _NN)
            s_scr[:, gc] = s_prev * ecs_full[CHUNK - 1:CHUNK, gc] + _dot(bg, xdec, _TN)
            parts = []
            for r in range(GROUP_W // HEAD_DIM):
                h = g * (GROUP_W // HEAD_DIM) + r
                _, m = _head_mats(cs, cs_t, cb, h, mask)
                parts.append(_dot(m.astype(BF16), xdt_b[:, r * HEAD_DIM:(r + 1) * HEAD_DIM], _NN))
            yg = jnp.concatenate(parts, axis=1) + y_off + dsk_ref[:, gc] * xs
            y_ref[:, gc] = yg
            zv = z_ref[:, gc].astype(F32)
            ygate = yg * (zv * _sigmoid(zv))
            rstd = lax.rsqrt(jnp.mean(ygate * ygate, axis=-1, keepdims=True) + NORM_EPS)
            yb_ref[:, gc] = (ygate * rstd * ng_ref[:, gc]).astype(BF16)

    vec = lambda w: pl.BlockSpec((1, w), lambda i: (0, 0))
    blk = _nbytes((CHUNK, CONV_DIM), F32) + 3 * _nbytes((CHUNK, D_INNER), F32) + _nbytes((D_STATE, D_INNER), F32)
    return pl.pallas_call(
        body, name=name, grid=(nc,),
        in_specs=[pl.BlockSpec((CHUNK, CONV_DIM), lambda i: (i, 0)), pl.BlockSpec((CHUNK, D_INNER), lambda i: (i, zcb)),
                  pl.BlockSpec((CHUNK, DT_PAD), lambda i: (i, 0)), vec(DT_PAD), vec(DT_PAD), vec(D_INNER), vec(D_INNER),
                  pl.BlockSpec((DT_PAD, D_INNER), lambda i: (0, 0))],
        out_specs=[pl.BlockSpec((CHUNK, D_INNER), lambda i: (i, 0)), pl.BlockSpec((CHUNK, D_INNER), lambda i: (i, 0)),
                   pl.BlockSpec((1, D_STATE, D_INNER), lambda i: (i, 0, 0))],
        out_shape=[jax.ShapeDtypeStruct((t, D_INNER), F32), jax.ShapeDtypeStruct((t, D_INNER), BF16),
                   jax.ShapeDtypeStruct((nc, D_STATE, D_INNER), F32)],
        scratch_shapes=[pltpu.VMEM((D_STATE, D_INNER), F32)],
        compiler_params=_params(("arbitrary",), blk),
    )(xc, proj, dt_raw, dtb, alog, dskip_full, ng, e_bf)


def _ssd_bwd(dyb, y, xc, proj, dt_raw, sprev, dtb, alog, dskip_full, ng, e_bf, dproj, name):
    t = xc.shape[0]
    nc = t // CHUNK
    zcb = COL_Z // D_INNER
    hpg = GROUP_W // HEAD_DIM
    rev = lambda i: nc - 1 - i

    def body(dyb_ref, y_ref, xc_ref, z_ref, dt_ref, sprev_ref, dtb_ref, alog_ref, dsk_ref, ng_ref, e_ref, dproj_in,
             dz_ref, dxc_ref, ddt_ref, gng_ref, gdsk_ref, galog_ref, gdtb_ref, ds_scr, sums_scr):
        del dproj_in

        @pl.when(pl.program_id(0) == 0)
        def _():
            ds_scr[...] = jnp.zeros_like(ds_scr)
            gng_ref[...] = jnp.zeros_like(gng_ref)
            gdsk_ref[...] = jnp.zeros_like(gdsk_ref)
            galog_ref[...] = jnp.zeros_like(galog_ref)
            gdtb_ref[...] = jnp.zeros_like(gdtb_ref)

        mask = _tri(True)
        tril_bf = mask.astype(BF16)
        triu_bf = _tri(False).astype(BF16)
        e_v = e_ref[...]
        dt_in = dt_ref[...] + dtb_ref[...]
        dtv, a, cs, dt_full, ecs_full, decay_full = _ssd_decays(dt_ref[...], dtb_ref[...], alog_ref[...], e_v, tril_bf)
        cs_t = cs.T

        lane_h = lax.broadcasted_iota(jnp.int32, (CHUNK, DT_PAD), 1)
        sub_h = lax.broadcasted_iota(jnp.int32, (DT_PAD, CHUNK), 0)
        dcs_rows = jnp.zeros((CHUNK, DT_PAD), F32)
        dcs_cols_t = jnp.zeros((DT_PAD, CHUNK), F32)
        last_cols, dsk_cols = [], []
        for g in range(GROUPS):
            gc = slice(g * GROUP_W, (g + 1) * GROUP_W)
            b_cols = slice(D_INNER + g * D_STATE, D_INNER + (g + 1) * D_STATE)
            c_cols = slice(D_INNER + GROUPS * D_STATE + g * D_STATE, D_INNER + GROUPS * D_STATE + (g + 1) * D_STATE)
            xs = xc_ref[:, gc]
            xdt = xs * dt_full[:, gc]
            xdt_b = xdt.astype(BF16)
            xdec = xdt * decay_full[:, gc]
            xdec_b = xdec.astype(BF16)
            zv = z_ref[:, gc].astype(F32)
            sg = _sigmoid(zv)
            gate = zv * sg
            yv = y_ref[:, gc]
            dybv = dyb_ref[:, gc]
            ygate = yv * gate
            rstd = lax.rsqrt(jnp.mean(ygate * ygate, axis=-1, keepdims=True) + NORM_EPS)
            yn = ygate * rstd
            gng_ref[:, gc] += jnp.sum(dybv * yn, axis=0, keepdims=True)
            dyn = dybv * ng_ref[:, gc]
            dyg = rstd * (dyn - yn * jnp.mean(dyn * yn, axis=-1, keepdims=True))
            dz_ref[:, gc] = (dyg * yv * sg * (1.0 + zv * (1.0 - sg))).astype(BF16)
            dy = dyg * gate
            dy_b = dy.astype(BF16)
            dyo = dy * ecs_full[:, gc]
            dyo_b = dyo.astype(BF16)
            dsk_cols.append(jnp.sum(dy * xs, axis=0, keepdims=True))

            bg = xc_ref[:, b_cols].astype(BF16)
            cg = xc_ref[:, c_cols].astype(BF16)
            s_prev = sprev_ref[0, :, gc]
            s_prev_b = s_prev.astype(BF16)
            dsg = ds_scr[:, gc]
            dsg_b = dsg.astype(BF16)
            cb = _dot(cg, bg, _NT)
            c_s = _dot(cg, s_prev_b, _NN)
            b_ds = _dot(bg, dsg_b, _NN)
            dcb = jnp.zeros((CHUNK, CHUNK), F32)
            parts = []
            for r in range(hpg):
                h = g * hpg + r
                hc = slice(r * HEAD_DIM, (r + 1) * HEAD_DIM)
                lmat, m = _head_mats(cs, cs_t, cb, h, mask)
                dm = _dot(dy_b[:, hc], xdt_b[:, hc], _NT)
                parts.append(_dot(m.astype(BF16), dy_b[:, hc], _TN))
                dcb = dcb + dm * lmat
                w = dm * m
                dcs_rows = jnp.where(lane_h == h, jnp.sum(w, axis=1, keepdims=True), dcs_rows)
                dcs_cols_t = jnp.where(sub_h == h, jnp.sum(w, axis=0, keepdims=True), dcs_cols_t)
            dxdt = jnp.concatenate(parts, axis=1) + decay_full[:, gc] * b_ds
            dcb_b = dcb.astype(BF16)
            dxc_ref[:, c_cols] = _dot(dcb_b, bg, _NN) + _dot(dyo_b, s_prev_b, _NT)
            dxc_ref[:, b_cols] = _dot(dcb_b, cg, _TN) + _dot(xdec_b, dsg_b, _NT)
            cdec = ecs_full[CHUNK - 1:CHUNK, gc]
            ds_scr[:, gc] = _dot(cg, dyo_b, _TN) + cdec * dsg
            dxc_ref[:, gc] = dxdt * dt_full[:, gc] + dsk_ref[:, gc] * dy
            dec_prod = xdec * b_ds
            sums_scr[:CHUNK, gc] = dyo * c_s - dec_prod
            sums_scr[CHUNK:, gc] = dxdt * xs
            last_cols.append(jnp.sum(dec_prod, axis=0, keepdims=True) + cdec * jnp.sum(dsg * s_prev, axis=0, keepdims=True))
        t_sums = _head_sums(sums_scr[...], e_v)
        tail = jnp.concatenate([jnp.concatenate(last_cols, axis=1), jnp.concatenate(dsk_cols, axis=1),
                                jnp.zeros((SUBLANES - 2, D_INNER), F32)], axis=0)
        t_tail = _dot_exact_rhs(tail, e_v, _NT)
        gdsk_ref[...] += t_tail[1:2, :]
        row = lax.broadcasted_iota(jnp.int32, (CHUNK, DT_PAD), 0)
        dcs = dcs_rows - dcs_cols_t.T + t_sums[:CHUNK] + jnp.where(row == CHUNK - 1, t_tail[0:1, :], 0.0)
        dda = _dot_exact_lhs(triu_bf, dcs, _NN)
        galog_ref[...] += jnp.sum(dda * dtv, axis=0, keepdims=True) * a
        ddt = dda * a + t_sums[CHUNK:]
        ddt_raw = jnp.where(lane_h < N_HEADS, ddt * _sigmoid(dt_in), 0.0)
        gdtb_ref[...] += jnp.sum(ddt_raw, axis=0, keepdims=True)
        ddt_ref[...] = ddt_raw.astype(BF16)

    vec = lambda w: pl.BlockSpec((1, w), lambda i: (0, 0))
    blk = (2 * _nbytes((CHUNK, CONV_DIM), F32) + 4 * _nbytes((CHUNK, D_INNER), F32) + 4 * _nbytes((D_STATE, D_INNER), F32))
    return pl.pallas_call(
        body, name=name, grid=(nc,),
        in_specs=[pl.BlockSpec((CHUNK, D_INNER), lambda i: (rev(i), 0)), pl.BlockSpec((CHUNK, D_INNER), lambda i: (rev(i), 0)),
                  pl.BlockSpec((CHUNK, CONV_DIM), lambda i: (rev(i), 0)), pl.BlockSpec((CHUNK, D_INNER), lambda i: (rev(i), zcb)),
                  pl.BlockSpec((CHUNK, DT_PAD), lambda i: (rev(i), 0)), pl.BlockSpec((1, D_STATE, D_INNER), lambda i: (rev(i), 0, 0)),
                  vec(DT_PAD), vec(DT_PAD), vec(D_INNER), vec(D_INNER), pl.BlockSpec((DT_PAD, D_INNER), lambda i: (0, 0)),
                  pl.BlockSpec(memory_space=pl.ANY)],
        out_specs=[pl.BlockSpec((CHUNK, D_INNER), lambda i: (rev(i), zcb)), pl.BlockSpec((CHUNK, CONV_DIM), lambda i: (rev(i), 0)),
                   pl.BlockSpec((CHUNK, DT_PAD), lambda i: (rev(i), 0)), vec(D_INNER), vec(DT_PAD), vec(DT_PAD), vec(DT_PAD)],
        out_shape=[jax.ShapeDtypeStruct(dproj.shape, BF16), jax.ShapeDtypeStruct((t, CONV_DIM), F32),
                   jax.ShapeDtypeStruct((t, DT_PAD), BF16), jax.ShapeDtypeStruct((1, D_INNER), F32),
                   jax.ShapeDtypeStruct((1, DT_PAD), F32), jax.ShapeDtypeStruct((1, DT_PAD), F32),
                   jax.ShapeDtypeStruct((1, DT_PAD), F32)],
        scratch_shapes=[pltpu.VMEM((D_STATE, D_INNER), F32), pltpu.VMEM((2 * CHUNK, D_INNER), F32)],
        input_output_aliases={11: 0},
        compiler_params=_params(("arbitrary",), blk),
    )(dyb, y, xc, proj, dt_raw, sprev, dtb, alog, dskip_full, ng, e_bf, dproj)


def _mesh_pos():
    return lax.axis_index("x"), lax.axis_index("y"), lax.axis_index("c")


def _other_chips(x, y):
    return [(1 - x, y), (x, 1 - y), (1 - x, 1 - y)]


def _all_peers(x, y, c):
    peers = []
    for k in range(1, N_DEV):
        fx, fy, fc = (k >> 2) & 1, (k >> 1) & 1, k & 1
        px, py, pc = x + fx - 2 * x * fx, y + fy - 2 * y * fy, c + fc - 2 * c * fc
        peers.append(((px, py, pc), 4 * px + 2 * py + pc))
    return peers


def _all_gather(shards, name, own_only=()):
    n, n_own = len(shards), len(own_only)

    def body(*refs):
        ins, own_ins = refs[:n], refs[n:n + n_own]
        outs, own_outs = refs[n + n_own:2 * n + n_own], refs[2 * n + n_own:2 * (n + n_own)]
        send_sems, recv_sems, local_sems = refs[2 * (n + n_own):]
        x, y, c = _mesh_pos()
        me, sibling = (x, y, c), (x, y, 1 - c)
        chips = _other_chips(x, y)

        def slot(p):
            return 4 * p[0] + 2 * p[1] + p[2]

        def copy(a, k, block, to, src=None):
            dst = outs[a].at[slot(block)]
            return pltpu.make_async_remote_copy(
                src_ref=dst if src is None else src, dst_ref=dst, send_sem=send_sems.at[a * 7 + k],
                recv_sem=recv_sems.at[a * 7 + k], device_id=to, device_id_type=MESH)

        started = []
        own = []
        for a in range(n_own):
            mine = pltpu.make_async_copy(own_ins[a], own_outs[a].at[slot(me)], local_sems.at[n + a])
            mine.start()
            own.append(mine)
        for a in range(n):
            mine = pltpu.make_async_copy(ins[a], outs[a].at[slot(me)], local_sems.at[a])
            mine.start()
            own.append(mine)
            first = [copy(a, 0, me, sibling, src=ins[a])]
            first += [copy(a, 1 + j, me, (*chip, c), src=ins[a]) for j, chip in enumerate(chips)]
            for cp in first:
                cp.start()
            started += first
        for a in range(n):
            for j, chip in enumerate(chips):
                copy(a, 1 + j, (*chip, c), me).wait_recv()
                fwd = copy(a, 4 + j, (*chip, c), sibling)
                fwd.start()
                started.append(fwd)
        for a in range(n):
            copy(a, 0, sibling, me).wait_recv()
            for j, chip in enumerate(chips):
                copy(a, 4 + j, (*chip, 1 - c), me).wait_recv()
        for cp in started:
            cp.wait_send()
        for mine in own:
            mine.wait()

    return pl.pallas_call(
        body, name=name,
        in_specs=[_HBM] * (n + n_own), out_specs=[_HBM] * (n + n_own),
        out_shape=[jax.ShapeDtypeStruct((N_DEV,) + s.shape, s.dtype) for s in (*shards, *own_only)],
        scratch_shapes=[pltpu.SemaphoreType.DMA((7 * n,)), pltpu.SemaphoreType.DMA((7 * n,)),
                        pltpu.SemaphoreType.DMA((n + n_own,))],
    )(*shards, *own_only)


_SMALL_ROWS = (("norm_mix_g", 8), ("conv_b", 32), ("dt_bias", 1), ("a_log", 1), ("d_skip", 1), ("ssm_norm_g", 16),
               ("v_norm_g", 8), ("v_norm_b", 8), ("w_spatial", 1024), ("b_spatial", 8), ("b_gates", 16), ("norm_mlp_g", 8),
               ("norm_final_g", 8), ("conv_w", 128), ("loss", 1))
_LAST_SMALL = (("norm_mix_g", 8),)


def _packed_rows(table):
    return -(-sum(r for _, r in table) // SUBLANES) * SUBLANES


def _small_offsets(table=_SMALL_ROWS):
    offs, r = {}, 0
    for name, rows in table:
        offs[name] = r
        r += rows
    return offs


def _rows_from(src_ref, dst_ref, r0):
    k, w = src_ref.shape
    if w <= LANES:
        dst_ref[r0:r0 + k, 0:w] = src_ref[...]
        return
    per = w // LANES
    for i in range(k):
        for j in range(per):
            dst_ref[r0 + i * per + j:r0 + i * per + j + 1, :] = src_ref[i:i + 1, j * LANES:(j + 1) * LANES]


def _rows_to(src_ref, r0, dst_ref):
    k, w = dst_ref.shape
    if w <= LANES:
        dst_ref[...] = src_ref[r0:r0 + k, 0:w]
        return
    per = w // LANES
    for i in range(k):
        for j in range(per):
            dst_ref[i:i + 1, j * LANES:(j + 1) * LANES] = src_ref[r0 + i * per + j:r0 + i * per + j + 1, :]


def _pack_small(grads, slot_idx, name):
    names = [n for n, _ in _SMALL_ROWS if n in grads]
    offs = _small_offsets()
    rows = _packed_rows(_SMALL_ROWS)

    def body(slot_ref, *refs):
        del slot_ref
        ins, (packed_ref, land_ref) = refs[:len(names)], refs[len(names):]
        packed_ref[...] = jnp.zeros_like(packed_ref)
        for n, ref in zip(names, ins):
            _rows_from(ref, packed_ref, offs[n])
        land_ref[0] = packed_ref[...]

    whole = lambda shape: pl.BlockSpec(shape, lambda i, slot_ref: (0,) * len(shape))
    grid_spec = pltpu.PrefetchScalarGridSpec(
        num_scalar_prefetch=1, grid=(1,), in_specs=[whole(grads[n].shape) for n in names],
        out_specs=[whole((rows, LANES)), pl.BlockSpec((1, rows, LANES), lambda i, slot_ref: (slot_ref[0], 0, 0))])
    return pl.pallas_call(
        body, name=name, grid_spec=grid_spec,
        out_shape=[jax.ShapeDtypeStruct((rows, LANES), F32), jax.ShapeDtypeStruct((N_DEV, rows, LANES), F32)],
    )(slot_idx, *[grads[n] for n in names])


def _exchange_small(grads, table, name):
    names = [n for n, _ in table]
    offs = _small_offsets(table)
    n_in = len(names)
    packed_rows = _packed_rows(table)

    def body(*refs):
        ins, out_ref = refs[:n_in], refs[n_in]
        packed, send_sems, recv_sems, local_sem = refs[n_in + 1:]
        packed[...] = jnp.zeros_like(packed)
        for n, ref in zip(names, ins):
            _rows_from(ref, packed, offs[n])
        x, y, c = _mesh_pos()
        my_slot = 4 * x + 2 * y + c
        mine = pltpu.make_async_copy(packed, out_ref.at[my_slot], local_sem)
        mine.start()
        copies = []
        for k, (peer, peer_slot) in enumerate(_all_peers(x, y, c)):
            sems = dict(send_sem=send_sems.at[k], recv_sem=recv_sems.at[k], device_id=peer, device_id_type=MESH)
            send = pltpu.make_async_remote_copy(src_ref=packed, dst_ref=out_ref.at[my_slot], **sems)
            send.start()
            copies.append((send, pltpu.make_async_remote_copy(src_ref=packed, dst_ref=out_ref.at[peer_slot], **sems)))
        for send, recv in copies:
            send.wait_send()
            recv.wait_recv()
        mine.wait()

    return pl.pallas_call(
        body, name=name, in_specs=[pl.BlockSpec(memory_space=pltpu.VMEM)] * n_in, out_specs=_HBM,
        out_shape=jax.ShapeDtypeStruct((N_DEV, packed_rows, LANES), F32),
        scratch_shapes=[pltpu.VMEM((packed_rows, LANES), F32), pltpu.SemaphoreType.DMA((N_DEV - 1,)),
                        pltpu.SemaphoreType.DMA((N_DEV - 1,)), pltpu.SemaphoreType.DMA],
    )(*[grads[n] for n in names])


def _swap_with_sibling(grads, name):
    n = len(grads)

    def body(*refs):
        ins, outs = refs[:n], refs[n:2 * n]
        send_sems, recv_sems = refs[2 * n:]
        x, y, c = _mesh_pos()
        copies = []
        for a in range(n):
            for k in range(N_CHIP):
                cp = pltpu.make_async_remote_copy(
                    src_ref=ins[a].at[(1 - c) + 2 * k], dst_ref=outs[a].at[k], send_sem=send_sems.at[a * N_CHIP + k],
                    recv_sem=recv_sems.at[a * N_CHIP + k], device_id=(x, y, 1 - c), device_id_type=MESH)
                cp.start()
                copies.append(cp)
        for cp in copies:
            cp.wait()

    return pl.pallas_call(
        body, name=name, in_specs=[_HBM] * n, out_specs=[_HBM] * n,
        out_shape=[jax.ShapeDtypeStruct((N_CHIP,) + g.shape[1:], g.dtype) for g in grads],
        scratch_shapes=[pltpu.SemaphoreType.DMA((N_CHIP * n,)), pltpu.SemaphoreType.DMA((N_CHIP * n,))],
    )(*grads)


_SEM = pl.BlockSpec(memory_space=pltpu.SEMAPHORE)
_IN_HBM = pl.BlockSpec(memory_space=pltpu.HBM)
_EFFECT = pltpu.SideEffectType.DATAFLOW_SIDE_EFFECTING


def _in_hbm(a):
    return pltpu.with_memory_space_constraint(a, pltpu.HBM)


def _gather_copies(ins, lands, send_sems, recv_sems):
    x, y, c = _mesh_pos()
    my_slot = 4 * x + 2 * y + c
    pairs = []
    for a in range(len(ins)):
        for k, (peer, peer_slot) in enumerate(_all_peers(x, y, c)):
            sems = dict(send_sem=send_sems.at[a * (N_DEV - 1) + k], recv_sem=recv_sems.at[a * (N_DEV - 1) + k],
                        device_id=peer, device_id_type=MESH)
            pairs.append((pltpu.make_async_remote_copy(src_ref=ins[a], dst_ref=lands[a].at[my_slot], **sems),
                          pltpu.make_async_remote_copy(src_ref=ins[a], dst_ref=lands[a].at[peer_slot], **sems)))
    return pairs


def _scatter_copies(ins, lands, send_sems, recv_sems):
    x, y, c = _mesh_pos()
    my_chip = 2 * x + y
    pairs = []
    for a in range(len(ins)):
        for j, chip in enumerate(_other_chips(x, y)):
            there = 2 * chip[0] + chip[1]
            sems = dict(send_sem=send_sems.at[a * 3 + j], recv_sem=recv_sems.at[a * 3 + j],
                        device_id=(*chip, c), device_id_type=MESH)
            pairs.append((pltpu.make_async_remote_copy(src_ref=ins[a].at[there], dst_ref=lands[a].at[my_chip], **sems),
                          pltpu.make_async_remote_copy(src_ref=ins[a].at[my_chip], dst_ref=lands[a].at[there], **sems)))
    return pairs


def _split_start(srcs, lands, copies, per_array, name):
    n = len(srcs)

    def body(*refs):
        ins, land_refs = refs[:n], refs[n:2 * n]
        send_sems, recv_sems = refs[2 * n], refs[2 * n + 1]
        token = refs[-1]
        for send, _ in copies(ins, land_refs, send_sems, recv_sems):
            send.start()
        token[...] = jnp.zeros_like(token)

    outs = pl.pallas_call(
        body, name=name,
        out_shape=(pltpu.SemaphoreType.DMA((per_array * n,)), pltpu.SemaphoreType.DMA((per_array * n,)),
                   *[pltpu.HBM(s.shape, s.dtype) for s in srcs], *[pltpu.HBM(l.shape, l.dtype) for l in lands],
                   jax.ShapeDtypeStruct((SUBLANES, LANES), F32)),
        in_specs=[_IN_HBM] * (2 * n),
        out_specs=(_SEM, _SEM, *[_IN_HBM] * (2 * n), pl.BlockSpec(memory_space=pltpu.VMEM)),
        input_output_aliases={i: 2 + i for i in range(2 * n)},
        compiler_params=pltpu.CompilerParams(has_side_effects=_EFFECT),
    )(*[_in_hbm(s) for s in srcs], *[_in_hbm(l) for l in lands])
    return outs[0], outs[1], list(outs[2:2 + n]), list(outs[2 + n:2 + 2 * n]), outs[-1]


def _split_wait(started, copies, after, name):
    send_sems, recv_sems, srcs, lands, _ = started
    n = len(srcs)

    def body(*refs):
        ins, land_refs = refs[:n], refs[n:2 * n]
        for send, recv in copies(ins, land_refs, refs[2 * n], refs[2 * n + 1]):
            send.wait_send()
            recv.wait_recv()

    outs = pl.pallas_call(
        body, name=name,
        out_shape=(*[pltpu.HBM(s.shape, s.dtype) for s in srcs], *[pltpu.HBM(l.shape, l.dtype) for l in lands]),
        in_specs=[_IN_HBM] * (2 * n) + [_SEM, _SEM, _HBM],
        out_specs=[_IN_HBM] * (2 * n),
        input_output_aliases={i: i for i in range(2 * n)},
        compiler_params=pltpu.CompilerParams(has_side_effects=_EFFECT),
    )(*srcs, *lands, send_sems, recv_sems, after)
    return list(outs[:n]), list(outs[n:])


def _ew_block(rows, cols, slots):
    budget = 2 * 1024 * 1024
    br, bc = rows, cols
    while slots * br * bc * 4 > budget:
        if br % 2 == 0 and (br // 2) % (2 * SUBLANES) == 0:
            br //= 2
        elif bc % 2 == 0 and (bc // 2) % LANES == 0:
            bc //= 2
        else:
            break
    return br, bc


def _add_sibling(grads, recv, c_idx, name):
    _, rows, cols = grads.shape
    br, bc = _ew_block(rows, cols, 3)

    def body(c_ref, g_ref, r_ref, out_ref):
        del c_ref
        out_ref[...] = (g_ref[...].astype(F32) + r_ref[...].astype(F32)).astype(out_ref.dtype)

    grid_spec = pltpu.PrefetchScalarGridSpec(
        num_scalar_prefetch=1, grid=(N_CHIP, rows // br, cols // bc),
        in_specs=[pl.BlockSpec((1, br, bc), lambda k, i, j, c_ref: (c_ref[0] + 2 * k, i, j)),
                  pl.BlockSpec((1, br, bc), lambda k, i, j, c_ref: (k, i, j))],
        out_specs=pl.BlockSpec((1, br, bc), lambda k, i, j, c_ref: (k, i, j)))
    return pl.pallas_call(
        body, name=name, grid_spec=grid_spec, out_shape=jax.ShapeDtypeStruct((N_CHIP, rows, cols), grads.dtype),
        compiler_params=_params(("parallel", "parallel", "parallel"), 3 * _nbytes((br, bc), F32)),
    )(c_idx, grads, recv)


def _adam_math(g, w, m, v):
    m2 = ADAM_B1 * m + (1.0 - ADAM_B1) * g
    v2 = ADAM_B2 * v + (1.0 - ADAM_B2) * (g * g)
    m_hat = m2 * (1.0 / (1.0 - ADAM_B1 ** ADAM_STEP))
    v_hat = v2 * (1.0 / (1.0 - ADAM_B2 ** ADAM_STEP))
    return -ADAM_LR * (m_hat / (jnp.sqrt(v_hat) + ADAM_EPS) + ADAM_WD * w), m2, v2


def _adamw(slots, w, m, v, name, own=None, own_slot=None):
    ns, rows, cols = slots.shape
    br, bc = _ew_block(rows, cols, 2 * ns + 7)

    def update(g, w_ref, m_ref, v_ref, g_ref, d_ref, m2_ref, v2_ref):
        g_ref[...] = g
        d_ref[...], m2_ref[...], v2_ref[...] = _adam_math(g, w_ref[...], m_ref[...], v_ref[...])

    out_shape = [jax.ShapeDtypeStruct((rows, cols), F32)] * 4
    params = _params(("parallel", "parallel"), (2 * ns + 7) * _nbytes((br, bc), F32))
    grid = (rows // br, cols // bc)
    if own is None:
        def body(s_ref, *rest):
            g = s_ref[0].astype(F32)
            for k in range(1, ns):
                g = g + s_ref[k].astype(F32)
            update(g, *rest)

        blk = pl.BlockSpec((br, bc), lambda i, j: (i, j))
        return pl.pallas_call(
            body, name=name, grid=grid,
            in_specs=[pl.BlockSpec((ns, br, bc), lambda i, j: (0, i, j)), blk, blk, blk], out_specs=[blk] * 4,
            out_shape=out_shape, compiler_params=params,
        )(slots, w, m, v)

    def body_own(slot_ref, s_ref, o_ref, *rest):
        g = None
        for k in range(ns):
            term = jnp.where(slot_ref[0] == k, o_ref[k].astype(F32), s_ref[k].astype(F32))
            g = term if g is None else g + term
        update(g, *rest)

    blk = pl.BlockSpec((br, bc), lambda i, j, slot_ref: (i, j))
    stack = pl.BlockSpec((ns, br, bc), lambda i, j, slot_ref: (0, i, j))
    grid_spec = pltpu.PrefetchScalarGridSpec(num_scalar_prefetch=1, grid=grid, in_specs=[stack, stack, blk, blk, blk],
                                             out_specs=[blk] * 4)
    return pl.pallas_call(body_own, name=name, grid_spec=grid_spec, out_shape=out_shape, compiler_params=params,
                          )(own_slot, slots, own, w, m, v)


def _adamw_small(all_g, last_g, params, extra_shapes, name):
    names = [n for n, _ in _SMALL_ROWS if n in params]
    extras = [n for n, _ in _SMALL_ROWS if n not in params]
    offs = _small_offsets()
    n_p = len(names)

    def body(*refs):
        s_ref, last_ref = refs[0], refs[1]
        wmv = refs[2:2 + 3 * n_p]
        outs = refs[2 + 3 * n_p:2 + 7 * n_p]
        extra_refs = refs[2 + 7 * n_p:2 + 7 * n_p + len(extras)]
        summed = refs[-1]
        g, g_last = s_ref[0], last_ref[0]
        for k in range(1, N_DEV):
            g, g_last = g + s_ref[k], g_last + last_ref[k]
        summed[...] = g
        last_offs = _small_offsets(_LAST_SMALL)
        for n, rows in _LAST_SMALL:
            summed[offs[n]:offs[n] + rows, :] = g_last[last_offs[n]:last_offs[n] + rows, :]
        for i, n in enumerate(names):
            w_ref, m_ref, v_ref = wmv[3 * i:3 * i + 3]
            g_ref, d_ref, m2_ref, v2_ref = outs[4 * i:4 * i + 4]
            _rows_to(summed, offs[n], g_ref)
            d_ref[...], m2_ref[...], v2_ref[...] = _adam_math(g_ref[...], w_ref[...], m_ref[...], v_ref[...])
        for n, ref in zip(extras, extra_refs):
            _rows_to(summed, offs[n], ref)

    flat = [a for n in names for a in params[n]]
    out_shape = [jax.ShapeDtypeStruct(params[n][0].shape, F32) for n in names for _ in range(4)]
    out_shape += [jax.ShapeDtypeStruct(s, F32) for s in extra_shapes]
    vmem = pl.BlockSpec(memory_space=pltpu.VMEM)
    res = pl.pallas_call(
        body, name=name, in_specs=[vmem] * (2 + len(flat)), out_specs=[vmem] * len(out_shape), out_shape=out_shape,
        scratch_shapes=[pltpu.VMEM(all_g.shape[1:], F32)],
        compiler_params=pltpu.CompilerParams(vmem_limit_bytes=_vmem_limit(_nbytes(all_g.shape, F32))),
    )(all_g, last_g, *flat)
    return {n: res[4 * i:4 * i + 4] for i, n in enumerate(names)}, res[4 * n_p:]


def _mm_tiles(mode, m, n, k):
    tn = min(n, 1024)
    if mode == "tn":
        return min(m, 1024), tn, min(k, 2048)
    if k <= 2048:
        return min(m, 1024), tn, k
    if k <= 4096:
        return min(m, 512), tn, k
    return min(m, 1024), tn, 2048


def _local_step(x, target, wts, small, exchange):
    t = x.shape[0]
    w_main_t, w_dt_t = wts["w_main_t"], wts["w_dt_t"]
    bsp_t = small["b_spatial"].T
    pad32 = lambda a: jnp.pad(a, ((0, 0), (0, DT_PAD - N_HEADS)))
    dtb, alog = pad32(small["dt_bias"]), pad32(small["a_log"])
    dskip_full = jnp.repeat(small["d_skip"], HEAD_DIM, axis=1)
    head_of_col = lax.broadcasted_iota(jnp.int32, (DT_PAD, D_INNER), 1) // HEAD_DIM
    e_bf = (head_of_col == lax.broadcasted_iota(jnp.int32, (DT_PAD, D_INNER), 0)).astype(BF16)

    def mm(a, b, mode, name, **kw):
        if mode == "nn":
            m, k, n = a.shape[0], a.shape[1], b.shape[1]
        elif mode == "nt":
            m, k, n = a.shape[0], a.shape[1], b.shape[0]
        else:
            m, k, n = a.shape[1], a.shape[0], b.shape[1]
        tm, tn, tk = _mm_tiles(mode, m, n, k)
        tm = min(tm, kw.pop("max_tm", tm))
        kw.setdefault("out_dtypes", (BF16,) if mode == "tn" else (F32,))
        if "extra_specs" in kw:
            kw["extra_specs"] = kw["extra_specs"](tm, tn)
        return _matmul(a, b, mode=mode, tm=tm, tn=tn, tk=tk, name=name, **kw)

    def out_tile(tm, tn):
        return (((tm, tn), lambda i, j: (i, j)),)

    def row_tiles(n_tiles, *vectors, gate_logits=False):
        def specs(tm, tn):
            out = [((tm, tn), lambda i, j: (i, j))] * n_tiles
            if gate_logits:
                out += [((tm, D_MODEL), lambda i, j, cb=COL_GATE // D_MODEL + half: (i, cb)) for half in range(2)]
            return tuple(out) + tuple(((1, w), lambda i, j, cb=cb: (0, cb)) for w, cb in vectors)
        return specs

    vec = lambda w: ((1, w), F32, (1, w), lambda i, j: (0, 0))
    fused_tm = 512

    h = _rms_fwd(x, small["norm_mix_g"], "rms_mix", deps=exchange.begin())
    proj = mm(h, w_main_t, "nt", "proj_main", j_outer=True, out_dtypes=(BF16,))
    dt_raw = mm(h, w_dt_t, "nt", "proj_dt")
    y_a = _gmlp_fwd(proj, small["v_norm_g"], small["v_norm_b"], small["w_spatial"], bsp_t, "gmlp_fwd")
    pre_conv, xc = _conv_fwd(proj, wts["conv_w"], small["conv_b"], "conv_fwd")
    y_ssd, y_b, sprev = _ssd_fwd(xc, proj, dt_raw, dtb, alog, dskip_full, small["ssm_norm_g"], e_bf, "ssd_fwd")
    wts = {**wts, **exchange.late_weights(y_b)}
    pa = mm(y_a, wts["w_proj_a"], "nn", "proj_a")
    pb, merged = mm(y_b, wts["w_proj_b"], "nn", "proj_b", epilogue=_merge_epilogue, out_dtypes=(F32, BF16), max_tm=fused_tm,
                    extras=(pa, proj, proj, small["b_gates"], small["b_gates"]),
                    extra_specs=row_tiles(1, (D_MODEL, 0), (D_MODEL, 1), gate_logits=True))
    x1, h2 = mm(merged, wts["w_out"], "nn", "out_proj", epilogue=_residual_rms_epilogue, out_dtypes=(F32, BF16),
                extras=(x, small["norm_mlp_g"]), extra_specs=row_tiles(1, (D_MODEL, 0)))

    def relu_sq(acc, ex, outs):
        r = jnp.maximum(acc, 0.0)
        outs[0][...] = (r * r).astype(BF16)

    act = mm(h2, wts["w_mlp_up"], "nn", "mlp_up", epilogue=relu_sq, out_dtypes=(BF16,), j_outer=True)
    dx2, dx2_b, g_final, _, loss = mm(
        act, wts["w_mlp_down"], "nn", "mlp_down", epilogue=_loss_epilogue, carry=True,
        out_dtypes=(F32, BF16, vec(D_MODEL), vec(D_MODEL), vec(LANES)),
        extras=(x1, small["norm_final_g"], target), extra_specs=lambda tm, tn: (
            ((tm, tn), lambda i, j: (i, j)), ((1, tn), lambda i, j: (0, 0)), ((tm, tn), lambda i, j: (i, j))))

    def relu_sq_bwd(acc, ex, outs):
        outs[0][...] = (acc * 2.0 * jnp.sqrt(ex[0][...].astype(F32))).astype(BF16)

    dup = mm(dx2_b, wts["w_mlp_down"], "nt", "d_act", epilogue=relu_sq_bwd, extras=(act,), extra_specs=out_tile,
             out_dtypes=(BF16,), j_outer=True)
    g_down = mm(act, dx2_b, "tn", "g_mlp_down")
    g_up = mm(h2, dup, "tn", "g_mlp_up")
    started = exchange.reduce("mlp", {"w_mlp_down": g_down, "w_mlp_up": g_up})
    dx1, dx1_b, g_mlp = mm(
        dup, wts["w_mlp_up"], "nt", "d_h2", deps=started, epilogue=_rms_bwd_epilogue, carry=True,
        out_dtypes=(F32, BF16, vec(D_MODEL)), extras=(x1, small["norm_mlp_g"], dx2), extra_specs=lambda tm, tn: (
            ((tm, tn), lambda i, j: (i, j)), ((1, tn), lambda i, j: (0, 0)), ((tm, tn), lambda i, j: (i, j))))

    g_out = mm(merged, dx1_b, "tn", "g_out")
    dpa, dpb, dproj, g_bgates = mm(
        dx1_b, wts["w_out"], "nt", "d_merged", epilogue=_merge_bwd_epilogue, carry=True, max_tm=fused_tm,
        out_dtypes=(BF16, BF16, ((t, MAIN_W), BF16, (fused_tm, 2 * D_MODEL), lambda i, j: (i, COL_GATE // (2 * D_MODEL))),
                    vec(2 * D_MODEL)),
        extras=(pa, pb, proj, proj, small["b_gates"], small["b_gates"]),
        extra_specs=row_tiles(2, (D_MODEL, 0), (D_MODEL, 1), gate_logits=True))
    g_pa = mm(y_a, dpa, "tn", "g_proj_a")
    g_pb = mm(y_b, dpb, "tn", "g_proj_b")
    started = exchange.reduce("proj", {"w_out": g_out, "w_proj_a": g_pa, "w_proj_b": g_pb})
    dya = mm(dpa, wts["w_proj_a"], "nt", "d_ya", deps=started)
    dyb = mm(dpb, wts["w_proj_b"], "nt", "d_yb")

    dproj, g_wsp, g_bsp_t, g_vg, g_vb = _gmlp_bwd(proj, dya, small["v_norm_g"], small["v_norm_b"], small["w_spatial"],
                                                   bsp_t, dproj, "gmlp_bwd")
    dproj, dxc, ddt, g_ng, g_dskip, g_alog, g_dtb = _ssd_bwd(dyb, y_ssd, xc, proj, dt_raw, sprev, dtb, alog, dskip_full,
                                                             small["ssm_norm_g"], e_bf, dproj, "ssd_bwd")
    dproj, g_convw, g_convb = _conv_bwd(proj, pre_conv, dxc, wts["conv_w"], dproj, "conv_bwd")

    small_grads = {
        "conv_w": g_convw, "loss": loss,
        "conv_b": g_convb, "dt_bias": g_dtb, "a_log": g_alog, "d_skip": g_dskip, "ssm_norm_g": g_ng,
        "v_norm_g": g_vg, "v_norm_b": g_vb, "w_spatial": g_wsp.reshape(GROUPS * CHUNK, CHUNK), "b_spatial": g_bsp_t.T,
        "b_gates": g_bgates, "norm_mlp_g": g_mlp, "norm_final_g": g_final,
    }
    g_main_t = mm(dproj, h, "tn", "g_in_main", deps=exchange.small(small_grads))
    g_dt_t = mm(ddt, h, "tn", "g_in_dt")
    started = exchange.reduce("in", {"w_in": _join_w_in(g_main_t, g_dt_t)})

    def input_grad(acc, ex, outs):
        dh = acc + _dot(ex[3][...], ex[4][...], _NN)
        _rms_bwd_epilogue(dh, ex, outs)

    grad_x, g_mix = mm(
        dproj, w_main_t, "nn", "d_h", epilogue=input_grad, deps=started, carry=True, max_tm=fused_tm,
        out_dtypes=(F32, vec(D_MODEL)), extras=(x, small["norm_mix_g"], dx1, ddt, w_dt_t), extra_specs=lambda tm, tn: (
            ((tm, tn), lambda i, j: (i, j)), ((1, tn), lambda i, j: (0, 0)), ((tm, tn), lambda i, j: (i, j)),
            ((tm, DT_PAD), lambda i, j: (i, 0)), ((DT_PAD, D_MODEL), lambda i, j: (0, 0))))

    return grad_x, g_mix


def _split_w_in(w_full_t):
    dt0 = COL_GATE
    w_main_t = jnp.concatenate([w_full_t[:dt0], w_full_t[dt0 + N_HEADS:]], axis=0)
    w_dt_t = jnp.pad(w_full_t[dt0:dt0 + N_HEADS], ((0, DT_PAD - N_HEADS), (0, 0)))
    return w_main_t, w_dt_t


def _join_w_in(g_main_t, g_dt_t):
    dt0 = COL_GATE
    return jnp.concatenate([g_main_t[:dt0], g_dt_t[:N_HEADS], g_main_t[dt0:]], axis=0)


_LATE = ["w_proj_a", "w_proj_b", "w_out", "w_mlp_up", "w_mlp_down"]
_BY_COLS = ("w_mlp_up",)


class _Exchange:
    def __init__(self, late_shards, late_lands):
        self.late_shards, self.late_lands = late_shards, late_lands
        self.c_idx = lax.axis_index("c").astype(jnp.int32).reshape(1)
        self.chip_idx = (2 * lax.axis_index("x") + lax.axis_index("y")).astype(jnp.int32).reshape(1)
        self.pending = []

    def begin(self):
        self.late = _split_start(self.late_shards, self.late_lands, _gather_copies, N_DEV - 1, "gather_late_start")
        return [self.late[-1]]

    def late_weights(self, after):
        _, lands = _split_wait(self.late, _gather_copies, after, "gather_late_wait")
        whole = {}
        for n, g in zip(_LATE, lands):
            whole[n] = jnp.transpose(g, (1, 0, 2)).reshape(g.shape[1], -1) if n in _BY_COLS else g.reshape(-1, g.shape[2])
        return whole

    def reduce(self, tag, grads):
        names = list(grads)
        by_dev = []
        for n in names:
            g = grads[n]
            if n in _BY_COLS:
                by_dev.append(jnp.transpose(g.reshape(g.shape[0], N_DEV, -1), (1, 0, 2)))
            else:
                by_dev.append(g.reshape(N_DEV, -1, g.shape[1]))
        from_sibling = _swap_with_sibling(by_dev, "reduce_cores_" + tag)
        parts = [_add_sibling(g, r, self.c_idx, "add_cores_" + n) for n, g, r in zip(names, by_dev, from_sibling)]
        lands = [lax.empty(p.shape, p.dtype) for p in parts]
        started = _split_start(parts, lands, _scatter_copies, 3, "reduce_chips_start_" + tag)
        self.pending.append((tag, names, started))
        return [started[-1]]

    def small(self, grads):
        dev = 2 * self.chip_idx + self.c_idx
        packed, land = _pack_small(grads, dev, "pack_small")
        self.small_started = _split_start([packed], [land], _gather_copies, N_DEV - 1, "exchange_small_start")
        return [self.small_started[-1]]

    def finish(self, after):
        _, (all_small,) = _split_wait(self.small_started, _gather_copies, after, "exchange_small_wait")
        done = {}
        for tag, names, started in self.pending:
            parts, lands = _split_wait(started, _scatter_copies, after, "reduce_chips_wait_" + tag)
            for n, land, part in zip(names, lands, parts):
                done[n] = (land, part, self.chip_idx)
        return all_small, done


def kernel(x, norm_mix_g, w_in, conv_w, conv_b, dt_bias, a_log, d_skip, ssm_norm_g, v_norm_g, v_norm_b, w_spatial, b_spatial, b_gates, w_proj_a, w_proj_b, w_out, norm_mlp_g, w_mlp_up, w_mlp_down, norm_final_g, loss_target, m_norm_mix_g, m_w_in, m_conv_w, m_conv_b, m_dt_bias, m_a_log, m_d_skip, m_ssm_norm_g, m_v_norm_g, m_v_norm_b, m_w_spatial, m_b_spatial, m_b_gates, m_w_proj_a, m_w_proj_b, m_w_out, m_norm_mlp_g, m_w_mlp_up, m_w_mlp_down, m_norm_final_g, v_norm_mix_g, v_w_in, v_conv_w, v_conv_b, v_dt_bias, v_a_log, v_d_skip, v_ssm_norm_g, v_v_norm_g, v_v_norm_b, v_w_spatial, v_b_spatial, v_b_gates, v_w_proj_a, v_w_proj_b, v_w_out, v_norm_mlp_g, v_w_mlp_up, v_w_mlp_down, v_norm_final_g):
    given = dict(locals())
    names = ["norm_mix_g", "w_in", "conv_w", "conv_b", "dt_bias", "a_log", "d_skip", "ssm_norm_g", "v_norm_g", "v_norm_b",
             "w_spatial", "b_spatial", "b_gates", "w_proj_a", "w_proj_b", "w_out", "norm_mlp_g", "w_mlp_up", "w_mlp_down",
             "norm_final_g"]
    shapes = {n: given[n].shape for n in names}
    dev = 4 * lax.axis_index("x") + 2 * lax.axis_index("y") + lax.axis_index("c")

    shard2d = {"w_in": w_in[0].T, "w_proj_a": w_proj_a[0], "w_proj_b": w_proj_b[0], "w_out": w_out[0],
               "w_mlp_up": w_mlp_up[0], "w_mlp_down": w_mlp_down[0]}
    conv_shard = conv_w.reshape(CONV_WIDTH, -1)
    late_shards = [shard2d[n].astype(BF16) for n in _LATE]
    w_in_all, conv_all, *late_lands = _all_gather([shard2d["w_in"].astype(BF16), conv_shard], "gather_first",
                                                  own_only=late_shards)
    w_main_t, w_dt_t = _split_w_in(w_in_all.reshape(-1, D_MODEL))
    wts = {"w_main_t": w_main_t, "w_dt_t": w_dt_t, "conv_w": jnp.transpose(conv_all, (1, 0, 2)).reshape(CONV_WIDTH, -1)}
    small = {"norm_mix_g": norm_mix_g, "conv_b": conv_b, "dt_bias": dt_bias, "a_log": a_log, "d_skip": d_skip,
             "ssm_norm_g": ssm_norm_g, "v_norm_g": v_norm_g, "v_norm_b": v_norm_b, "w_spatial": w_spatial[0],
             "b_spatial": b_spatial[0], "b_gates": b_gates, "norm_mlp_g": norm_mlp_g,
             "norm_final_g": norm_final_g.reshape(1, -1)}

    exchange = _Exchange(late_shards, late_lands)
    grad_x, g_mix = _local_step(x[0], loss_target[0], wts, small, exchange)

    out = {}
    all_small, large = exchange.finish(grad_x)
    for n, (slots, own, own_slot) in large.items():
        moments = [given["m_" + n][0], given["v_" + n][0]]
        if n == "w_in":
            moments = [mom.T for mom in moments]
        res = _adamw(slots, shard2d[n], *moments, "adamw_" + n, own=own, own_slot=own_slot)
        out[n] = [(r.T if n == "w_in" else r).reshape(shapes[n]) for r in res]

    last_small = _exchange_small({"norm_mix_g": g_mix}, _LAST_SMALL, "exchange_last")
    small["w_spatial"] = small["w_spatial"].reshape(GROUPS * CHUNK, CHUNK)
    params = {n: (w2d, given["m_" + n].reshape(w2d.shape), given["v_" + n].reshape(w2d.shape)) for n, w2d in small.items()}
    updated, (g_conv_full, loss_all) = _adamw_small(all_small, last_small, params, [(CONV_WIDTH, CONV_DIM), (1, LANES)],
                                                    "adamw_small")
    for n, res in updated.items():
        out[n] = [r.reshape(shapes[n]) for r in res]
    width = shapes["conv_w"][-1]
    g_conv = lax.dynamic_slice(g_conv_full, (0, dev * width), (CONV_WIDTH, width))
    res = _adamw(g_conv[None], conv_shard, m_conv_w.reshape(CONV_WIDTH, -1), v_conv_w.reshape(CONV_WIDTH, -1), "adamw_conv_w")
    out["conv_w"] = [r.reshape(shapes["conv_w"]) for r in res]

    loss = loss_all[0, 0]
    return (loss, grad_x[None], *[out[n][0] for n in names], *[out[n][1] for n in names],
            *[out[n][2] for n in names], *[out[n][3] for n in names])
```

```python
import functools
import math

import jax
import jax.numpy as jnp
from jax import lax
from jax.experimental import pallas as pl
from jax.experimental.pallas import tpu as pltpu

F32 = jnp.float32
BF16 = jnp.bfloat16
MESH = pl.DeviceIdType.MESH

D_MODEL = 1024
NORM_EPS = 1e-6
CHUNK = 128
GROUPS = 8
D_INNER = 2048
HEAD_DIM = 64
N_HEADS = 32
D_STATE = 128
CONV_WIDTH = 4
CONV_DIM = 4096
D_FF = 4096
GROUP_W = D_INNER // GROUPS
N_DEV = 8
N_CHIP = 4

ADAM_LR = 0.001
ADAM_B1 = 0.9
ADAM_B2 = 0.999
ADAM_EPS = 1e-08
ADAM_WD = 0.01
ADAM_STEP = 10

MAIN_W = 2 * D_MODEL + D_INNER + CONV_DIM + 2 * D_MODEL
COL_Z = 2048
COL_XBC = 4096
COL_GATE = 8192
DT_PAD = 128

LANES = 128
SUBLANES = 8
VMEM_BYTES_V7X = 64 * 1024 * 1024
VMEM_BODY_TEMP = 24 * 1024 * 1024


def _vmem_limit(block_bytes):
    return int(min(2 * block_bytes + VMEM_BODY_TEMP, VMEM_BYTES_V7X - 8 * 1024 * 1024))


def _nbytes(shape, dtype):
    return math.prod(shape) * jnp.dtype(dtype).itemsize


_HBM = pl.BlockSpec(memory_space=pl.ANY)


def _params(sem, block_bytes):
    return pltpu.CompilerParams(dimension_semantics=sem, vmem_limit_bytes=_vmem_limit(block_bytes))


def _sigmoid(x):
    return 1.0 / (1.0 + jnp.exp(-x))


def _softplus(x):
    e = jnp.exp(-jnp.abs(x))
    u = 1.0 + e
    log1p_e = jnp.where(u == 1.0, e, jnp.log(u) * (e / jnp.where(u == 1.0, 1.0, u - 1.0)))
    return jnp.maximum(x, 0.0) + log1p_e


_SQRT_HALF = 0.7071067811865476
_INV_SQRT_2PI = 0.3989422804014327


def _gelu(x):
    return x * (lax.erf(x * _SQRT_HALF) + 1.0) * 0.5


def _gelu_grad(x):
    return 0.5 * (1.0 + lax.erf(x * _SQRT_HALF)) + x * jnp.exp(-0.5 * x * x) * _INV_SQRT_2PI


def _dot(a, b, dims):
    return lax.dot_general(a, b, (dims, ((), ())), preferred_element_type=F32)


_NN = ((1,), (0,))
_NT = ((1,), (1,))
_TN = ((0,), (0,))


def _split3(x):
    hi = x.astype(BF16)
    r1 = x - hi.astype(F32)
    mid = r1.astype(BF16)
    lo = (r1 - mid.astype(F32)).astype(BF16)
    return hi, mid, lo


def _dot_exact_rhs(x, e, dims):
    hi, mid, lo = _split3(x)
    return _dot(hi, e, dims) + _dot(mid, e, dims) + _dot(lo, e, dims)


def _dot_exact_lhs(e, x, dims):
    hi, mid, lo = _split3(x)
    return _dot(e, hi, dims) + _dot(e, mid, dims) + _dot(e, lo, dims)


def _tri(lower):
    r = lax.broadcasted_iota(jnp.int32, (CHUNK, CHUNK), 0)
    c = lax.broadcasted_iota(jnp.int32, (CHUNK, CHUNK), 1)
    return (r >= c) if lower else (r <= c)


def _matmul(a, b, *, mode, tm, tn, tk, out_dtypes, name, epilogue=None, extras=(), extra_specs=(), j_outer=False, deps=(),
            carry=False):
    if mode == "nn":
        (m, k), (_, n) = a.shape, b.shape
    elif mode == "nt":
        (m, k), (n, _) = a.shape, b.shape
    else:
        (k, m), (_, n) = a.shape, b.shape
    assert m % tm == 0 and n % tn == 0 and k % tk == 0, (name, m, n, k, tm, tn, tk)
    nk = k // tk
    n_extra, n_out = len(extras), len(out_dtypes)
    first_out = 2 + n_extra + len(deps)
    dims = {"nn": _NN, "nt": _NT, "tn": _TN}[mode]
    if epilogue is None:
        def epilogue(acc, ex, outs):
            outs[0][...] = acc.astype(outs[0].dtype)

    def body(*refs):
        a_ref, b_ref = refs[0], refs[1]
        ex_refs = refs[2:2 + n_extra]
        outs = refs[first_out:first_out + n_out]
        p = _dot(a_ref[...], b_ref[...], dims)
        if nk == 1:
            epilogue(p, ex_refs, outs)
        else:
            acc_ref = refs[first_out + n_out]
            kk = pl.program_id(2)

            @pl.when(kk == 0)
            def _():
                acc_ref[...] = p

            @pl.when(kk > 0)
            def _():
                acc_ref[...] += p

            @pl.when(kk == nk - 1)
            def _():
                epilogue(acc_ref[...], ex_refs, outs)

    if j_outer:
        grid = (n // tn, m // tm, nk)
        ij = lambda g0, g1: (g1, g0)
    else:
        grid = (m // tm, n // tn, nk)
        ij = lambda g0, g1: (g0, g1)

    def wrap(fn):
        return lambda g0, g1, kk: fn(*ij(g0, g1), kk)

    if mode == "nn":
        a_spec = pl.BlockSpec((tm, tk), wrap(lambda i, j, kk: (i, kk)))
        b_spec = pl.BlockSpec((tk, tn), wrap(lambda i, j, kk: (kk, j)))
        a_blk, b_blk = (tm, tk), (tk, tn)
    elif mode == "nt":
        a_spec = pl.BlockSpec((tm, tk), wrap(lambda i, j, kk: (i, kk)))
        b_spec = pl.BlockSpec((tn, tk), wrap(lambda i, j, kk: (j, kk)))
        a_blk, b_blk = (tm, tk), (tn, tk)
    else:
        a_spec = pl.BlockSpec((tk, tm), wrap(lambda i, j, kk: (kk, i)))
        b_spec = pl.BlockSpec((tk, tn), wrap(lambda i, j, kk: (kk, j)))
        a_blk, b_blk = (tk, tm), (tk, tn)
    ex_specs = [pl.BlockSpec(shape, wrap(lambda i, j, kk, f=f: f(i, j))) for shape, f in extra_specs]
    outs = [o if isinstance(o, tuple) else ((m, n), o, (tm, tn), lambda i, j: (i, j)) for o in out_dtypes]
    out_spec = [pl.BlockSpec(blk_shape, wrap(lambda i, j, kk, f=f: f(i, j))) for _, _, blk_shape, f in outs]
    out_shape = [jax.ShapeDtypeStruct(shape, dt) for shape, dt, _, _ in outs]
    blk = (_nbytes(a_blk, a.dtype) + _nbytes(b_blk, b.dtype) + sum(_nbytes(s, F32) for s, _ in extra_specs)
           + sum(_nbytes(blk_shape, dt) for _, dt, blk_shape, _ in outs) + _nbytes((tm, tn), F32))
    order = ("arbitrary",) * 3 if carry else ("parallel", "parallel", "arbitrary")
    res = pl.pallas_call(
        body, name=name, grid=grid,
        in_specs=[a_spec, b_spec] + ex_specs + [_HBM] * len(deps), out_specs=out_spec, out_shape=out_shape,
        scratch_shapes=[pltpu.VMEM((tm, tn), F32)] if nk > 1 else [],
        compiler_params=_params(order, blk),
    )(_in_hbm(a), _in_hbm(b), *extras, *deps)
    return res[0] if n_out == 1 else res


ROW_TILE = 256


def _row_spec(width, col_block=0, tile=ROW_TILE):
    return pl.BlockSpec((tile, width), lambda i, cb=col_block: (i, cb))


def _vec_spec(width, col_block=0):
    return pl.BlockSpec((1, width), lambda i, cb=col_block: (0, cb))


def _rms_fwd(x, g, name, deps=()):
    t = x.shape[0]

    def body(x_ref, g_ref, *rest):
        h_ref = rest[-1]
        xv = x_ref[...]
        r = lax.rsqrt(jnp.mean(xv * xv, axis=-1, keepdims=True) + NORM_EPS)
        h_ref[...] = (xv * r * g_ref[...]).astype(BF16)

    return pl.pallas_call(
        body, name=name, grid=(t // ROW_TILE,),
        in_specs=[_row_spec(D_MODEL), _vec_spec(D_MODEL)] + [_HBM] * len(deps), out_specs=_row_spec(D_MODEL),
        out_shape=jax.ShapeDtypeStruct((t, D_MODEL), BF16),
        compiler_params=_params(("parallel",), 3 * _nbytes((ROW_TILE, D_MODEL), F32)),
    )(x, g, *deps)


def _rms_scale(xv):
    r = lax.rsqrt(jnp.mean(xv * xv, axis=-1, keepdims=True) + NORM_EPS)
    return r, xv * r


def _rms_pullback(xv, g, dh):
    r, xh = _rms_scale(xv)
    dyg = dh * g
    return r * (dyg - xh * jnp.mean(dyg * xh, axis=-1, keepdims=True)), jnp.sum(dh * xh, axis=0, keepdims=True)


def _first_row_tile():
    return pl.program_id(0) == 0


def _residual_rms_epilogue(acc, ex, outs):
    x1 = acc + ex[0][...]
    outs[0][...] = x1
    _, xh = _rms_scale(x1)
    outs[1][...] = (xh * ex[1][...]).astype(BF16)


def _loss_epilogue(acc, ex, outs):
    dx_ref, dxb_ref, gg_ref, sq_ref, tot_ref = outs
    gv = ex[1][...]
    r, xh = _rms_scale(acc + ex[0][...])
    err = xh * gv - ex[2][...]
    dy = err * (1.0 / D_MODEL)
    dyg = dy * gv
    dx = r * (dyg - xh * jnp.mean(dyg * xh, axis=-1, keepdims=True))
    dx_ref[...] = dx
    dxb_ref[...] = dx.astype(BF16)

    @pl.when(_first_row_tile())
    def _():
        gg_ref[...] = jnp.zeros_like(gg_ref)
        sq_ref[...] = jnp.zeros_like(sq_ref)

    gg_ref[...] += jnp.sum(dy * xh, axis=0, keepdims=True)
    sq_ref[...] += jnp.sum(err * err, axis=0, keepdims=True)
    tot_ref[...] = jnp.broadcast_to(jnp.sum(sq_ref[...], axis=1, keepdims=True) * (0.5 / D_MODEL), tot_ref.shape)


def _rms_bwd_epilogue(dh, ex, outs):
    dx, gg = _rms_pullback(ex[0][...], ex[1][...], dh)
    dx = dx + ex[2][...]
    outs[0][...] = dx
    if len(outs) == 3:
        outs[1][...] = dx.astype(BF16)

    @pl.when(_first_row_tile())
    def _():
        outs[-1][...] = jnp.zeros_like(outs[-1])

    outs[-1][...] += gg


def _merge_epilogue(acc, ex, outs):
    outs[0][...] = acc
    ga = _sigmoid(ex[1][...].astype(F32) + ex[3][...])
    gb = _sigmoid(ex[2][...].astype(F32) + ex[4][...])
    outs[1][...] = (ga * ex[0][...] + gb * acc).astype(BF16)


def _merge_bwd_epilogue(dm, ex, outs):
    dpa_ref, dpb_ref, dgl_ref, gb_ref = outs
    ga = _sigmoid(ex[2][...].astype(F32) + ex[4][...])
    gb = _sigmoid(ex[3][...].astype(F32) + ex[5][...])
    dpa_ref[...] = (dm * ga).astype(BF16)
    dpb_ref[...] = (dm * gb).astype(BF16)
    dla = dm * ex[0][...] * ga * (1.0 - ga)
    dlb = dm * ex[1][...] * gb * (1.0 - gb)
    dgl_ref[:, :D_MODEL] = dla.astype(BF16)
    dgl_ref[:, D_MODEL:] = dlb.astype(BF16)

    @pl.when(_first_row_tile())
    def _():
        gb_ref[...] = jnp.zeros_like(gb_ref)

    gb_ref[:, :D_MODEL] += jnp.sum(dla, axis=0, keepdims=True)
    gb_ref[:, D_MODEL:] += jnp.sum(dlb, axis=0, keepdims=True)


GMLP_TILE = 512
GMLP_NC = GMLP_TILE // CHUNK


def _gmlp_common(u_pre, v_pre, vg, vb):
    u = _gelu(u_pre)
    v = _gelu(v_pre)
    mu = jnp.mean(v, axis=-1, keepdims=True)
    vc = v - mu
    rstd = lax.rsqrt(jnp.mean(vc * vc, axis=-1, keepdims=True) + NORM_EPS)
    vh = vc * rstd
    vn = vh * vg + vb
    return u, vh, vn, rstd


def _chunks_to_lanes(x, g):
    return jnp.concatenate([x[c * CHUNK:(c + 1) * CHUNK, g * CHUNK:(g + 1) * CHUNK] for c in range(GMLP_NC)], axis=1)


def _gmlp_fwd(proj, vg, vb, wsp, bsp_t, name):
    t = proj.shape[0]

    def body(u_ref, v_ref, vg_ref, vb_ref, w_ref, b_ref, ya_ref):
        u, _, vn, _ = _gmlp_common(u_ref[...].astype(F32), v_ref[...].astype(F32), vg_ref[...], vb_ref[...])
        mask = _tri(True)
        bt = b_ref[...]
        for g in range(GROUPS):
            w = jnp.where(mask, w_ref[g], 0.0).astype(BF16)
            vcat = _chunks_to_lanes(vn, g).astype(BF16)
            s = _dot(w, vcat, _NN) + bt[:, g:g + 1]
            for c in range(GMLP_NC):
                rows, cols = slice(c * CHUNK, (c + 1) * CHUNK), slice(g * CHUNK, (g + 1) * CHUNK)
                ya_ref[rows, cols] = (u[rows, cols] * s[:, c * CHUNK:(c + 1) * CHUNK]).astype(BF16)

    return pl.pallas_call(
        body, name=name, grid=(t // GMLP_TILE,),
        in_specs=[_row_spec(D_MODEL, 0, GMLP_TILE), _row_spec(D_MODEL, 1, GMLP_TILE), _vec_spec(D_MODEL),
                  _vec_spec(D_MODEL), pl.BlockSpec((GROUPS, CHUNK, CHUNK), lambda i: (0, 0, 0)),
                  pl.BlockSpec((CHUNK, GROUPS), lambda i: (0, 0))],
        out_specs=_row_spec(D_MODEL, 0, GMLP_TILE),
        out_shape=jax.ShapeDtypeStruct((t, D_MODEL), BF16),
        compiler_params=_params(("parallel",), 3 * _nbytes((GMLP_TILE, D_MODEL), F32)),
    )(proj, proj, vg, vb, wsp, bsp_t)


def _gmlp_bwd(proj, dya, vg, vb, wsp, bsp_t, dproj, name):
    t = proj.shape[0]

    def body(u_ref, v_ref, dya_ref, vg_ref, vb_ref, w_ref, b_ref, dproj_in, duv_ref, gw_ref, gbt_ref, gvg_ref, gvb_ref,
             dvn_scr, du_scr):
        del dproj_in
        u_pre, v_pre = u_ref[...].astype(F32), v_ref[...].astype(F32)
        vgv = vg_ref[...]
        u, vh, vn, rstd = _gmlp_common(u_pre, v_pre, vgv, vb_ref[...])
        dya = dya_ref[...]
        mask = _tri(True)
        bt = b_ref[...]
        first = pl.program_id(0) == 0

        @pl.when(first)
        def _():
            gw_ref[...] = jnp.zeros_like(gw_ref)
            gbt_ref[...] = jnp.zeros_like(gbt_ref)
            gvg_ref[...] = jnp.zeros_like(gvg_ref)
            gvb_ref[...] = jnp.zeros_like(gvb_ref)

        lane = lax.broadcasted_iota(jnp.int32, (CHUNK, GROUPS), 1)
        gbt = jnp.zeros((CHUNK, GROUPS), F32)
        for g in range(GROUPS):
            w = jnp.where(mask, w_ref[g], 0.0).astype(BF16)
            vcat = _chunks_to_lanes(vn, g).astype(BF16)
            s = _dot(w, vcat, _NN) + bt[:, g:g + 1]
            ds = _chunks_to_lanes(dya * u, g)
            gbt = jnp.where(lane == g, jnp.sum(ds, axis=1, keepdims=True), gbt)
            dsb = ds.astype(BF16)
            gw_ref[g] += jnp.where(mask, _dot(dsb, vcat, _NT), 0.0)
            dv = _dot(w, dsb, _TN)
            for c in range(GMLP_NC):
                rows, cols = slice(c * CHUNK, (c + 1) * CHUNK), slice(g * CHUNK, (g + 1) * CHUNK)
                dvn_scr[rows, cols] = dv[:, c * CHUNK:(c + 1) * CHUNK]
                du_scr[rows, cols] = dya[rows, cols] * s[:, c * CHUNK:(c + 1) * CHUNK]
        gbt_ref[...] += gbt
        dvn = dvn_scr[...]
        gvg_ref[...] += jnp.sum(dvn * vh, axis=0, keepdims=True)
        gvb_ref[...] += jnp.sum(dvn, axis=0, keepdims=True)
        dvh = dvn * vgv
        dv = rstd * (dvh - jnp.mean(dvh, axis=-1, keepdims=True) - vh * jnp.mean(dvh * vh, axis=-1, keepdims=True))
        duv_ref[:, :D_MODEL] = (du_scr[...] * _gelu_grad(u_pre)).astype(BF16)
        duv_ref[:, D_MODEL:] = (dv * _gelu_grad(v_pre)).astype(BF16)

    return pl.pallas_call(
        body, name=name, grid=(t // GMLP_TILE,),
        in_specs=[_row_spec(D_MODEL, 0, GMLP_TILE), _row_spec(D_MODEL, 1, GMLP_TILE), _row_spec(D_MODEL, 0, GMLP_TILE),
                  _vec_spec(D_MODEL), _vec_spec(D_MODEL), pl.BlockSpec((GROUPS, CHUNK, CHUNK), lambda i: (0, 0, 0)),
                  pl.BlockSpec((CHUNK, GROUPS), lambda i: (0, 0)), pl.BlockSpec(memory_space=pl.ANY)],
        out_specs=[_row_spec(2 * D_MODEL, 0, GMLP_TILE), pl.BlockSpec((GROUPS, CHUNK, CHUNK), lambda i: (0, 0, 0)),
                   pl.BlockSpec((CHUNK, GROUPS), lambda i: (0, 0)), _vec_spec(D_MODEL), _vec_spec(D_MODEL)],
        out_shape=[jax.ShapeDtypeStruct(dproj.shape, BF16), jax.ShapeDtypeStruct((GROUPS, CHUNK, CHUNK), F32),
                   jax.ShapeDtypeStruct((CHUNK, GROUPS), F32), jax.ShapeDtypeStruct((1, D_MODEL), F32),
                   jax.ShapeDtypeStruct((1, D_MODEL), F32)],
        scratch_shapes=[pltpu.VMEM((GMLP_TILE, D_MODEL), F32), pltpu.VMEM((GMLP_TILE, D_MODEL), F32)],
        input_output_aliases={7: 0},
        compiler_params=_params(("arbitrary",), 6 * _nbytes((GMLP_TILE, D_MODEL), F32)),
    )(proj, proj, dya, vg, vb, wsp, bsp_t, dproj)


CONV_TILE = 512
CONV_COLS = 1024
CONV_RB = 32
HALO = SUBLANES


def _conv_fwd(proj, cw, cb, name):
    t = proj.shape[0]
    nj = CONV_DIM // CONV_COLS
    xcb = COL_XBC // CONV_COLS
    before = 2 * HALO
    rb = CONV_TILE // before

    def body(x_ref, prev_ref, cw_ref, cb_ref, pre_ref, xc_ref):
        i = pl.program_id(1)
        cw_v = cw_ref[...]
        cb_v = cb_ref[...]
        for b in range(CONV_TILE // CONV_RB):
            if b == 0:
                prev = jnp.where(i > 0, prev_ref[...].astype(F32)[HALO:, :], 0.0)
                ext = jnp.concatenate([prev, x_ref[:CONV_RB, :].astype(F32)], axis=0)
            else:
                ext = x_ref[b * CONV_RB - before:(b + 1) * CONV_RB, :].astype(F32)[HALO:, :]
            pre = cb_v + cw_v[CONV_WIDTH - 1:CONV_WIDTH, :] * ext[HALO:, :]
            for k in range(CONV_WIDTH - 1):
                back = CONV_WIDTH - 1 - k
                pre = pre + cw_v[k:k + 1, :] * pltpu.roll(ext, back, 0)[HALO:, :]
            pre_ref[b * CONV_RB:(b + 1) * CONV_RB, :] = pre
            xc_ref[b * CONV_RB:(b + 1) * CONV_RB, :] = pre * _sigmoid(pre)

    tile = pl.BlockSpec((CONV_TILE, CONV_COLS), lambda j, i: (i, j))
    return pl.pallas_call(
        body, name=name, grid=(nj, t // CONV_TILE),
        in_specs=[pl.BlockSpec((CONV_TILE, CONV_COLS), lambda j, i: (i, xcb + j)),
                  pl.BlockSpec((before, CONV_COLS), lambda j, i: (jnp.maximum(i * rb - 1, 0), xcb + j)),
                  pl.BlockSpec((CONV_WIDTH, CONV_COLS), lambda j, i: (0, j)),
                  pl.BlockSpec((1, CONV_COLS), lambda j, i: (0, j))],
        out_specs=[tile, tile],
        out_shape=[jax.ShapeDtypeStruct((t, CONV_DIM), F32), jax.ShapeDtypeStruct((t, CONV_DIM), F32)],
        compiler_params=_params(("parallel", "parallel"), 4 * _nbytes((CONV_TILE, CONV_COLS), F32)),
    )(proj, proj, cw, cb)


def _fold_rows(v):
    out = v[:SUBLANES]
    for r in range(1, v.shape[0] // SUBLANES):
        out = out + v[r * SUBLANES:(r + 1) * SUBLANES]
    return out


def _conv_bwd(proj, pre, dxc, cw, dproj, name):
    t = proj.shape[0]
    nj = CONV_DIM // CONV_COLS
    ni = t // CONV_TILE
    xcb = COL_XBC // CONV_COLS
    rb = CONV_TILE // HALO
    last_rb = t // HALO - 1

    def body(x_ref, p_ref, pnext_ref, d_ref, dnext_ref, cw_ref, dproj_in, dx_ref, gw_ref, gb_ref):
        del dproj_in
        i = pl.program_id(1)
        cw_v = cw_ref[...]

        def dpre_of(p, d):
            sg = _sigmoid(p)
            return d * sg * (1.0 + p * (1.0 - sg))

        @pl.when(i == 0)
        def _():
            gw_ref[...] = jnp.zeros_like(gw_ref)
            gb_ref[...] = jnp.zeros_like(gb_ref)

        head = dpre_of(pnext_ref[...], jnp.where(i < ni - 1, dnext_ref[...], 0.0))
        gb_acc = jnp.zeros((SUBLANES, CONV_COLS), F32)
        gw_acc = [jnp.zeros((SUBLANES, CONV_COLS), F32) for _ in range(CONV_WIDTH)]
        for b in reversed(range(CONV_TILE // CONV_RB)):
            rows = slice(b * CONV_RB, (b + 1) * CONV_RB)
            cur = dpre_of(p_ref[rows, :], d_ref[rows, :])
            ext = jnp.concatenate([cur, head], axis=0)
            xv = x_ref[rows, :].astype(F32)
            dx = None
            for k in range(CONV_WIDTH):
                shift = CONV_WIDTH - 1 - k
                win = cur if shift == 0 else pltpu.roll(ext, CONV_RB + HALO - shift, 0)[:CONV_RB, :]
                term = cw_v[k:k + 1, :] * win
                dx = term if dx is None else dx + term
                gw_acc[k] = gw_acc[k] + _fold_rows(win * xv)
            dx_ref[rows, :] = dx.astype(BF16)
            gb_acc = gb_acc + _fold_rows(cur)
            head = cur[:HALO]
        gb_ref[...] += jnp.sum(gb_acc, axis=0, keepdims=True)
        for k in range(CONV_WIDTH):
            gw_ref[k:k + 1, :] += jnp.sum(gw_acc[k], axis=0, keepdims=True)

    tile = pl.BlockSpec((CONV_TILE, CONV_COLS), lambda j, i: (i, j))
    after = pl.BlockSpec((HALO, CONV_COLS), lambda j, i: (jnp.minimum((i + 1) * rb, last_rb), j))
    return pl.pallas_call(
        body, name=name, grid=(nj, ni),
        in_specs=[pl.BlockSpec((CONV_TILE, CONV_COLS), lambda j, i: (i, xcb + j)), tile, after, tile, after,
                  pl.BlockSpec((CONV_WIDTH, CONV_COLS), lambda j, i: (0, j)),
                  pl.BlockSpec(memory_space=pl.ANY)],
        out_specs=[pl.BlockSpec((CONV_TILE, CONV_COLS), lambda j, i: (i, xcb + j)),
                   pl.BlockSpec((CONV_WIDTH, CONV_COLS), lambda j, i: (0, j)),
                   pl.BlockSpec((1, CONV_COLS), lambda j, i: (0, j))],
        out_shape=[jax.ShapeDtypeStruct(dproj.shape, BF16), jax.ShapeDtypeStruct((CONV_WIDTH, CONV_DIM), F32),
                   jax.ShapeDtypeStruct((1, CONV_DIM), F32)],
        input_output_aliases={6: 0},
        compiler_params=_params(("parallel", "arbitrary"), 4 * _nbytes((CONV_TILE, CONV_COLS), F32)),
    )(proj, pre, pre, dxc, dxc, cw, dproj)


def _ssd_decays(dt_raw, dtb, alog, e_bf, tril_bf):
    dtv = _softplus(dt_raw + dtb)
    a = -jnp.exp(alog)
    cs = _dot_exact_lhs(tril_bf, dtv * a, _NN)
    cs_last = cs[CHUNK - 1:CHUNK, :]
    stack = jnp.concatenate([dtv, jnp.exp(cs), jnp.exp(cs_last - cs)], axis=0)
    full = _head_expand(stack, e_bf)
    return dtv, a, cs, full[:CHUNK], full[CHUNK:2 * CHUNK], full[2 * CHUNK:]


def _split2(x):
    hi = x.astype(BF16)
    return hi, (x - hi.astype(F32)).astype(BF16)


def _head_expand(x, e_bf):
    hi, mid = _split2(x)
    return _dot(hi, e_bf, _NN) + _dot(mid, e_bf, _NN)


def _head_sums(x, e_bf):
    hi, mid = _split2(x)
    return _dot(hi, e_bf, _NT) + _dot(mid, e_bf, _NT)


def _head_mats(cs, cs_t, cb, h, mask):
    seg = cs[:, h:h + 1] - cs_t[h:h + 1, :]
    lmat = jnp.exp(jnp.where(mask, seg, -jnp.inf))
    return lmat, cb * lmat


def _ssd_fwd(xc, proj, dt_raw, dtb, alog, dskip_full, ng, e_bf, name):
    t = xc.shape[0]
    nc = t // CHUNK
    zcb = COL_Z // D_INNER

    def body(xc_ref, z_ref, dt_ref, dtb_ref, alog_ref, dsk_ref, ng_ref, e_ref, y_ref, yb_ref, sprev_ref, s_scr):
        @pl.when(pl.program_id(0) == 0)
        def _():
            s_scr[...] = jnp.zeros_like(s_scr)

        mask = _tri(True)
        tril_bf = mask.astype(BF16)
        e_v = e_ref[...]
        _, _, cs, dt_full, ecs_full, decay_full = _ssd_decays(dt_ref[...], dtb_ref[...], alog_ref[...], e_v, tril_bf)
        cs_t = cs.T
        sprev_ref[0] = s_scr[...]
        for g in range(GROUPS):
            gc = slice(g * GROUP_W, (g + 1) * GROUP_W)
            xs = xc_ref[:, gc]
            xdt = xs * dt_full[:, gc]
            xdt_b = xdt.astype(BF16)
            xdec = (xdt * decay_full[:, gc]).astype(BF16)
            bg = xc_ref[:, D_INNER + g * D_STATE:D_INNER + (g + 1) * D_STATE].astype(BF16)
            cg = xc_ref[:, D_INNER + GROUPS * D_STATE + g * D_STATE:D_INNER + GROUPS * D_STATE + (g + 1) * D_STATE].astype(BF16)
            cb = _dot(cg, bg, _NT)
            s_prev = s_scr[:, gc]
            y_off = ecs_full[:, gc] * _dot(cg, s_prev.astype(BF16), _NN)
            s_scr[:, gc] = s_prev * ecs_full[CHUNK - 1:CHUNK, gc] + _dot(bg, xdec, _TN)
            parts = []
            for r in range(GROUP_W // HEAD_DIM):
                h = g * (GROUP_W // HEAD_DIM) + r
                _, m = _head_mats(cs, cs_t, cb, h, mask)
                parts.append(_dot(m.astype(BF16), xdt_b[:, r * HEAD_DIM:(r + 1) * HEAD_DIM], _NN))
            yg = jnp.concatenate(parts, axis=1) + y_off + dsk_ref[:, gc] * xs
            y_ref[:, gc] = yg
            zv = z_ref[:, gc].astype(F32)
            ygate = yg * (zv * _sigmoid(zv))
            rstd = lax.rsqrt(jnp.mean(ygate * ygate, axis=-1, keepdims=True) + NORM_EPS)
            yb_ref[:, gc] = (ygate * rstd * ng_ref[:, gc]).astype(BF16)

    vec = lambda w: pl.BlockSpec((1, w), lambda i: (0, 0))
    blk = _nbytes((CHUNK, CONV_DIM), F32) + 3 * _nbytes((CHUNK, D_INNER), F32) + _nbytes((D_STATE, D_INNER), F32)
    return pl.pallas_call(
        body, name=name, grid=(nc,),
        in_specs=[pl.BlockSpec((CHUNK, CONV_DIM), lambda i: (i, 0)), pl.BlockSpec((CHUNK, D_INNER), lambda i: (i, zcb)),
                  pl.BlockSpec((CHUNK, DT_PAD), lambda i: (i, 0)), vec(DT_PAD), vec(DT_PAD), vec(D_INNER), vec(D_INNER),
                  pl.BlockSpec((DT_PAD, D_INNER), lambda i: (0, 0))],
        out_specs=[pl.BlockSpec((CHUNK, D_INNER), lambda i: (i, 0)), pl.BlockSpec((CHUNK, D_INNER), lambda i: (i, 0)),
                   pl.BlockSpec((1, D_STATE, D_INNER), lambda i: (i, 0, 0))],
        out_shape=[jax.ShapeDtypeStruct((t, D_INNER), F32), jax.ShapeDtypeStruct((t, D_INNER), BF16),
                   jax.ShapeDtypeStruct((nc, D_STATE, D_INNER), F32)],
        scratch_shapes=[pltpu.VMEM((D_STATE, D_INNER), F32)],
        compiler_params=_params(("arbitrary",), blk),
    )(xc, proj, dt_raw, dtb, alog, dskip_full, ng, e_bf)


def _ssd_bwd(dyb, y, xc, proj, dt_raw, sprev, dtb, alog, dskip_full, ng, e_bf, dproj, name):
    t = xc.shape[0]
    nc = t // CHUNK
    zcb = COL_Z // D_INNER
    hpg = GROUP_W // HEAD_DIM
    rev = lambda i: nc - 1 - i

    def body(dyb_ref, y_ref, xc_ref, z_ref, dt_ref, sprev_ref, dtb_ref, alog_ref, dsk_ref, ng_ref, e_ref, dproj_in,
             dz_ref, dxc_ref, ddt_ref, gng_ref, gdsk_ref, galog_ref, gdtb_ref, ds_scr, sums_scr):
        del dproj_in

        @pl.when(pl.program_id(0) == 0)
        def _():
            ds_scr[...] = jnp.zeros_like(ds_scr)
            gng_ref[...] = jnp.zeros_like(gng_ref)
            gdsk_ref[...] = jnp.zeros_like(gdsk_ref)
            galog_ref[...] = jnp.zeros_like(galog_ref)
            gdtb_ref[...] = jnp.zeros_like(gdtb_ref)

        mask = _tri(True)
        tril_bf = mask.astype(BF16)
        triu_bf = _tri(False).astype(BF16)
        e_v = e_ref[...]
        dt_in = dt_ref[...] + dtb_ref[...]
        dtv, a, cs, dt_full, ecs_full, decay_full = _ssd_decays(dt_ref[...], dtb_ref[...], alog_ref[...], e_v, tril_bf)
        cs_t = cs.T

        lane_h = lax.broadcasted_iota(jnp.int32, (CHUNK, DT_PAD), 1)
        sub_h = lax.broadcasted_iota(jnp.int32, (DT_PAD, CHUNK), 0)
        dcs_rows = jnp.zeros((CHUNK, DT_PAD), F32)
        dcs_cols_t = jnp.zeros((DT_PAD, CHUNK), F32)
        last_cols, dsk_cols = [], []
        for g in range(GROUPS):
            gc = slice(g * GROUP_W, (g + 1) * GROUP_W)
            b_cols = slice(D_INNER + g * D_STATE, D_INNER + (g + 1) * D_STATE)
            c_cols = slice(D_INNER + GROUPS * D_STATE + g * D_STATE, D_INNER + GROUPS * D_STATE + (g + 1) * D_STATE)
            xs = xc_ref[:, gc]
            xdt = xs * dt_full[:, gc]
            xdt_b = xdt.astype(BF16)
            xdec = xdt * decay_full[:, gc]
            xdec_b = xdec.astype(BF16)
            zv = z_ref[:, gc].astype(F32)
            sg = _sigmoid(zv)
            gate = zv * sg
            yv = y_ref[:, gc]
            dybv = dyb_ref[:, gc]
            ygate = yv * gate
            rstd = lax.rsqrt(jnp.mean(ygate * ygate, axis=-1, keepdims=True) + NORM_EPS)
            yn = ygate * rstd
            gng_ref[:, gc] += jnp.sum(dybv * yn, axis=0, keepdims=True)
            dyn = dybv * ng_ref[:, gc]
            dyg = rstd * (dyn - yn * jnp.mean(dyn * yn, axis=-1, keepdims=True))
            dz_ref[:, gc] = (dyg * yv * sg * (1.0 + zv * (1.0 - sg))).astype(BF16)
            dy = dyg * gate
            dy_b = dy.astype(BF16)
            dyo = dy * ecs_full[:, gc]
            dyo_b = dyo.astype(BF16)
            dsk_cols.append(jnp.sum(dy * xs, axis=0, keepdims=True))

            bg = xc_ref[:, b_cols].astype(BF16)
            cg = xc_ref[:, c_cols].astype(BF16)
            s_prev = sprev_ref[0, :, gc]
            s_prev_b = s_prev.astype(BF16)
            dsg = ds_scr[:, gc]
            dsg_b = dsg.astype(BF16)
            cb = _dot(cg, bg, _NT)
            c_s = _dot(cg, s_prev_b, _NN)
            b_ds = _dot(bg, dsg_b, _NN)
            dcb = jnp.zeros((CHUNK, CHUNK), F32)
            parts = []
            for r in range(hpg):
                h = g * hpg + r
                hc = slice(r * HEAD_DIM, (r + 1) * HEAD_DIM)
                lmat, m = _head_mats(cs, cs_t, cb, h, mask)
                dm = _dot(dy_b[:, hc], xdt_b[:, hc], _NT)
                parts.append(_dot(m.astype(BF16), dy_b[:, hc], _TN))
                dcb = dcb + dm * lmat
                w = dm * m
                dcs_rows = jnp.where(lane_h == h, jnp.sum(w, axis=1, keepdims=True), dcs_rows)
                dcs_cols_t = jnp.where(sub_h == h, jnp.sum(w, axis=0, keepdims=True), dcs_cols_t)
            dxdt = jnp.concatenate(parts, axis=1) + decay_full[:, gc] * b_ds
            dcb_b = dcb.astype(BF16)
            dxc_ref[:, c_cols] = _dot(dcb_b, bg, _NN) + _dot(dyo_b, s_prev_b, _NT)
            dxc_ref[:, b_cols] = _dot(dcb_b, cg, _TN) + _dot(xdec_b, dsg_b, _NT)
            cdec = ecs_full[CHUNK - 1:CHUNK, gc]
            ds_scr[:, gc] = _dot(cg, dyo_b, _TN) + cdec * dsg
            dxc_ref[:, gc] = dxdt * dt_full[:, gc] + dsk_ref[:, gc] * dy
            dec_prod = xdec * b_ds
            sums_scr[:CHUNK, gc] = dyo * c_s - dec_prod
            sums_scr[CHUNK:, gc] = dxdt * xs
            last_cols.append(jnp.sum(dec_prod, axis=0, keepdims=True) + cdec * jnp.sum(dsg * s_prev, axis=0, keepdims=True))
        t_sums = _head_sums(sums_scr[...], e_v)
        tail = jnp.concatenate([jnp.concatenate(last_cols, axis=1), jnp.concatenate(dsk_cols, axis=1),
                                jnp.zeros((SUBLANES - 2, D_INNER), F32)], axis=0)
        t_tail = _dot_exact_rhs(tail, e_v, _NT)
        gdsk_ref[...] += t_tail[1:2, :]
        row = lax.broadcasted_iota(jnp.int32, (CHUNK, DT_PAD), 0)
        dcs = dcs_rows - dcs_cols_t.T + t_sums[:CHUNK] + jnp.where(row == CHUNK - 1, t_tail[0:1, :], 0.0)
        dda = _dot_exact_lhs(triu_bf, dcs, _NN)
        galog_ref[...] += jnp.sum(dda * dtv, axis=0, keepdims=True) * a
        ddt = dda * a + t_sums[CHUNK:]
        ddt_raw = jnp.where(lane_h < N_HEADS, ddt * _sigmoid(dt_in), 0.0)
        gdtb_ref[...] += jnp.sum(ddt_raw, axis=0, keepdims=True)
        ddt_ref[...] = ddt_raw.astype(BF16)

    vec = lambda w: pl.BlockSpec((1, w), lambda i: (0, 0))
    blk = (2 * _nbytes((CHUNK, CONV_DIM), F32) + 4 * _nbytes((CHUNK, D_INNER), F32) + 4 * _nbytes((D_STATE, D_INNER), F32))
    return pl.pallas_call(
        body, name=name, grid=(nc,),
        in_specs=[pl.BlockSpec((CHUNK, D_INNER), lambda i: (rev(i), 0)), pl.BlockSpec((CHUNK, D_INNER), lambda i: (rev(i), 0)),
                  pl.BlockSpec((CHUNK, CONV_DIM), lambda i: (rev(i), 0)), pl.BlockSpec((CHUNK, D_INNER), lambda i: (rev(i), zcb)),
                  pl.BlockSpec((CHUNK, DT_PAD), lambda i: (rev(i), 0)), pl.BlockSpec((1, D_STATE, D_INNER), lambda i: (rev(i), 0, 0)),
                  vec(DT_PAD), vec(DT_PAD), vec(D_INNER), vec(D_INNER), pl.BlockSpec((DT_PAD, D_INNER), lambda i: (0, 0)),
                  pl.BlockSpec(memory_space=pl.ANY)],
        out_specs=[pl.BlockSpec((CHUNK, D_INNER), lambda i: (rev(i), zcb)), pl.BlockSpec((CHUNK, CONV_DIM), lambda i: (rev(i), 0)),
                   pl.BlockSpec((CHUNK, DT_PAD), lambda i: (rev(i), 0)), vec(D_INNER), vec(DT_PAD), vec(DT_PAD), vec(DT_PAD)],
        out_shape=[jax.ShapeDtypeStruct(dproj.shape, BF16), jax.ShapeDtypeStruct((t, CONV_DIM), F32),
                   jax.ShapeDtypeStruct((t, DT_PAD), BF16), jax.ShapeDtypeStruct((1, D_INNER), F32),
                   jax.ShapeDtypeStruct((1, DT_PAD), F32), jax.ShapeDtypeStruct((1, DT_PAD), F32),
                   jax.ShapeDtypeStruct((1, DT_PAD), F32)],
        scratch_shapes=[pltpu.VMEM((D_STATE, D_INNER), F32), pltpu.VMEM((2 * CHUNK, D_INNER), F32)],
        input_output_aliases={11: 0},
        compiler_params=_params(("arbitrary",), blk),
    )(dyb, y, xc, proj, dt_raw, sprev, dtb, alog, dskip_full, ng, e_bf, dproj)


def _mesh_pos():
    return lax.axis_index("x"), lax.axis_index("y"), lax.axis_index("c")


def _other_chips(x, y):
    return [(1 - x, y), (x, 1 - y), (1 - x, 1 - y)]


def _all_peers(x, y, c):
    peers = []
    for k in range(1, N_DEV):
        fx, fy, fc = (k >> 2) & 1, (k >> 1) & 1, k & 1
        px, py, pc = x + fx - 2 * x * fx, y + fy - 2 * y * fy, c + fc - 2 * c * fc
        peers.append(((px, py, pc), 4 * px + 2 * py + pc))
    return peers


def _all_gather(shards, name, own_only=()):
    n, n_own = len(shards), len(own_only)

    def body(*refs):
        ins, own_ins = refs[:n], refs[n:n + n_own]
        outs, own_outs = refs[n + n_own:2 * n + n_own], refs[2 * n + n_own:2 * (n + n_own)]
        send_sems, recv_sems, local_sems = refs[2 * (n + n_own):]
        x, y, c = _mesh_pos()
        me, sibling = (x, y, c), (x, y, 1 - c)
        chips = _other_chips(x, y)

        def slot(p):
            return 4 * p[0] + 2 * p[1] + p[2]

        def copy(a, k, block, to, src=None):
            dst = outs[a].at[slot(block)]
            return pltpu.make_async_remote_copy(
                src_ref=dst if src is None else src, dst_ref=dst, send_sem=send_sems.at[a * 7 + k],
                recv_sem=recv_sems.at[a * 7 + k], device_id=to, device_id_type=MESH)

        started = []
        own = []
        for a in range(n_own):
            mine = pltpu.make_async_copy(own_ins[a], own_outs[a].at[slot(me)], local_sems.at[n + a])
            mine.start()
            own.append(mine)
        for a in range(n):
            mine = pltpu.make_async_copy(ins[a], outs[a].at[slot(me)], local_sems.at[a])
            mine.start()
            own.append(mine)
            first = [copy(a, 0, me, sibling, src=ins[a])]
            first += [copy(a, 1 + j, me, (*chip, c), src=ins[a]) for j, chip in enumerate(chips)]
            for cp in first:
                cp.start()
            started += first
        for a in range(n):
            for j, chip in enumerate(chips):
                copy(a, 1 + j, (*chip, c), me).wait_recv()
                fwd = copy(a, 4 + j, (*chip, c), sibling)
                fwd.start()
                started.append(fwd)
        for a in range(n):
            copy(a, 0, sibling, me).wait_recv()
            for j, chip in enumerate(chips):
                copy(a, 4 + j, (*chip, 1 - c), me).wait_recv()
        for cp in started:
            cp.wait_send()
        for mine in own:
            mine.wait()

    return pl.pallas_call(
        body, name=name,
        in_specs=[_HBM] * (n + n_own), out_specs=[_HBM] * (n + n_own),
        out_shape=[jax.ShapeDtypeStruct((N_DEV,) + s.shape, s.dtype) for s in (*shards, *own_only)],
        scratch_shapes=[pltpu.SemaphoreType.DMA((7 * n,)), pltpu.SemaphoreType.DMA((7 * n,)),
                        pltpu.SemaphoreType.DMA((n + n_own,))],
    )(*shards, *own_only)


_SMALL_ROWS = (("norm_mix_g", 8), ("conv_b", 32), ("dt_bias", 1), ("a_log", 1), ("d_skip", 1), ("ssm_norm_g", 16),
               ("v_norm_g", 8), ("v_norm_b", 8), ("w_spatial", 1024), ("b_spatial", 8), ("b_gates", 16), ("norm_mlp_g", 8),
               ("norm_final_g", 8), ("conv_w", 128), ("loss", 1))
_LAST_SMALL = (("norm_mix_g", 8),)


def _packed_rows(table):
    return -(-sum(r for _, r in table) // SUBLANES) * SUBLANES


def _small_offsets(table=_SMALL_ROWS):
    offs, r = {}, 0
    for name, rows in table:
        offs[name] = r
        r += rows
    return offs


def _rows_from(src_ref, dst_ref, r0):
    k, w = src_ref.shape
    if w <= LANES:
        dst_ref[r0:r0 + k, 0:w] = src_ref[...]
        return
    per = w // LANES
    for i in range(k):
        for j in range(per):
            dst_ref[r0 + i * per + j:r0 + i * per + j + 1, :] = src_ref[i:i + 1, j * LANES:(j + 1) * LANES]


def _rows_to(src_ref, r0, dst_ref):
    k, w = dst_ref.shape
    if w <= LANES:
        dst_ref[...] = src_ref[r0:r0 + k, 0:w]
        return
    per = w // LANES
    for i in range(k):
        for j in range(per):
            dst_ref[i:i + 1, j * LANES:(j + 1) * LANES] = src_ref[r0 + i * per + j:r0 + i * per + j + 1, :]


def _pack_small(grads, slot_idx, name):
    names = [n for n, _ in _SMALL_ROWS if n in grads]
    offs = _small_offsets()
    rows = _packed_rows(_SMALL_ROWS)

    def body(slot_ref, *refs):
        del slot_ref
        ins, (packed_ref, land_ref) = refs[:len(names)], refs[len(names):]
        packed_ref[...] = jnp.zeros_like(packed_ref)
        for n, ref in zip(names, ins):
            _rows_from(ref, packed_ref, offs[n])
        land_ref[0] = packed_ref[...]

    whole = lambda shape: pl.BlockSpec(shape, lambda i, slot_ref: (0,) * len(shape))
    grid_spec = pltpu.PrefetchScalarGridSpec(
        num_scalar_prefetch=1, grid=(1,), in_specs=[whole(grads[n].shape) for n in names],
        out_specs=[whole((rows, LANES)), pl.BlockSpec((1, rows, LANES), lambda i, slot_ref: (slot_ref[0], 0, 0))])
    return pl.pallas_call(
        body, name=name, grid_spec=grid_spec,
        out_shape=[jax.ShapeDtypeStruct((rows, LANES), F32), jax.ShapeDtypeStruct((N_DEV, rows, LANES), F32)],
    )(slot_idx, *[grads[n] for n in names])


def _exchange_small(grads, table, name):
    names = [n for n, _ in table]
    offs = _small_offsets(table)
    n_in = len(names)
    packed_rows = _packed_rows(table)

    def body(*refs):
        ins, out_ref = refs[:n_in], refs[n_in]
        packed, send_sems, recv_sems, local_sem = refs[n_in + 1:]
        packed[...] = jnp.zeros_like(packed)
        for n, ref in zip(names, ins):
            _rows_from(ref, packed, offs[n])
        x, y, c = _mesh_pos()
        my_slot = 4 * x + 2 * y + c
        mine = pltpu.make_async_copy(packed, out_ref.at[my_slot], local_sem)
        mine.start()
        copies = []
        for k, (peer, peer_slot) in enumerate(_all_peers(x, y, c)):
            sems = dict(send_sem=send_sems.at[k], recv_sem=recv_sems.at[k], device_id=peer, device_id_type=MESH)
            send = pltpu.make_async_remote_copy(src_ref=packed, dst_ref=out_ref.at[my_slot], **sems)
            send.start()
            copies.append((send, pltpu.make_async_remote_copy(src_ref=packed, dst_ref=out_ref.at[peer_slot], **sems)))
        for send, recv in copies:
            send.wait_send()
            recv.wait_recv()
        mine.wait()

    return pl.pallas_call(
        body, name=name, in_specs=[pl.BlockSpec(memory_space=pltpu.VMEM)] * n_in, out_specs=_HBM,
        out_shape=jax.ShapeDtypeStruct((N_DEV, packed_rows, LANES), F32),
        scratch_shapes=[pltpu.VMEM((packed_rows, LANES), F32), pltpu.SemaphoreType.DMA((N_DEV - 1,)),
                        pltpu.SemaphoreType.DMA((N_DEV - 1,)), pltpu.SemaphoreType.DMA],
    )(*[grads[n] for n in names])


def _swap_with_sibling(grads, name):
    n = len(grads)

    def body(*refs):
        ins, outs = refs[:n], refs[n:2 * n]
        send_sems, recv_sems = refs[2 * n:]
        x, y, c = _mesh_pos()
        copies = []
        for a in range(n):
            for k in range(N_CHIP):
                cp = pltpu.make_async_remote_copy(
                    src_ref=ins[a].at[(1 - c) + 2 * k], dst_ref=outs[a].at[k], send_sem=send_sems.at[a * N_CHIP + k],
                    recv_sem=recv_sems.at[a * N_CHIP + k], device_id=(x, y, 1 - c), device_id_type=MESH)
                cp.start()
                copies.append(cp)
        for cp in copies:
            cp.wait()

    return pl.pallas_call(
        body, name=name, in_specs=[_HBM] * n, out_specs=[_HBM] * n,
        out_shape=[jax.ShapeDtypeStruct((N_CHIP,) + g.shape[1:], g.dtype) for g in grads],
        scratch_shapes=[pltpu.SemaphoreType.DMA((N_CHIP * n,)), pltpu.SemaphoreType.DMA((N_CHIP * n,))],
    )(*grads)


_SEM = pl.BlockSpec(memory_space=pltpu.SEMAPHORE)
_IN_HBM = pl.BlockSpec(memory_space=pltpu.HBM)
_EFFECT = pltpu.SideEffectType.DATAFLOW_SIDE_EFFECTING


def _in_hbm(a):
    return pltpu.with_memory_space_constraint(a, pltpu.HBM)


def _gather_copies(ins, lands, send_sems, recv_sems):
    x, y, c = _mesh_pos()
    my_slot = 4 * x + 2 * y + c
    pairs = []
    for a in range(len(ins)):
        for k, (peer, peer_slot) in enumerate(_all_peers(x, y, c)):
            sems = dict(send_sem=send_sems.at[a * (N_DEV - 1) + k], recv_sem=recv_sems.at[a * (N_DEV - 1) + k],
                        device_id=peer, device_id_type=MESH)
            pairs.append((pltpu.make_async_remote_copy(src_ref=ins[a], dst_ref=lands[a].at[my_slot], **sems),
                          pltpu.make_async_remote_copy(src_ref=ins[a], dst_ref=lands[a].at[peer_slot], **sems)))
    return pairs


def _scatter_copies(ins, lands, send_sems, recv_sems):
    x, y, c = _mesh_pos()
    my_chip = 2 * x + y
    pairs = []
    for a in range(len(ins)):
        for j, chip in enumerate(_other_chips(x, y)):
            there = 2 * chip[0] + chip[1]
            sems = dict(send_sem=send_sems.at[a * 3 + j], recv_sem=recv_sems.at[a * 3 + j],
                        device_id=(*chip, c), device_id_type=MESH)
            pairs.append((pltpu.make_async_remote_copy(src_ref=ins[a].at[there], dst_ref=lands[a].at[my_chip], **sems),
                          pltpu.make_async_remote_copy(src_ref=ins[a].at[my_chip], dst_ref=lands[a].at[there], **sems)))
    return pairs


def _split_start(srcs, lands, copies, per_array, name):
    n = len(srcs)

    def body(*refs):
        ins, land_refs = refs[:n], refs[n:2 * n]
        send_sems, recv_sems = refs[2 * n], refs[2 * n + 1]
        token = refs[-1]
        for send, _ in copies(ins, land_refs, send_sems, recv_sems):
            send.start()
        token[...] = jnp.zeros_like(token)

    outs = pl.pallas_call(
        body, name=name,
        out_shape=(pltpu.SemaphoreType.DMA((per_array * n,)), pltpu.SemaphoreType.DMA((per_array * n,)),
                   *[pltpu.HBM(s.shape, s.dtype) for s in srcs], *[pltpu.HBM(l.shape, l.dtype) for l in lands],
                   jax.ShapeDtypeStruct((SUBLANES, LANES), F32)),
        in_specs=[_IN_HBM] * (2 * n),
        out_specs=(_SEM, _SEM, *[_IN_HBM] * (2 * n), pl.BlockSpec(memory_space=pltpu.VMEM)),
        input_output_aliases={i: 2 + i for i in range(2 * n)},
        compiler_params=pltpu.CompilerParams(has_side_effects=_EFFECT),
    )(*[_in_hbm(s) for s in srcs], *[_in_hbm(l) for l in lands])
    return outs[0], outs[1], list(outs[2:2 + n]), list(outs[2 + n:2 + 2 * n]), outs[-1]


def _split_wait(started, copies, after, name):
    send_sems, recv_sems, srcs, lands, _ = started
    n = len(srcs)

    def body(*refs):
        ins, land_refs = refs[:n], refs[n:2 * n]
        for send, recv in copies(ins, land_refs, refs[2 * n], refs[2 * n + 1]):
            send.wait_send()
            recv.wait_recv()

    outs = pl.pallas_call(
        body, name=name,
        out_shape=(*[pltpu.HBM(s.shape, s.dtype) for s in srcs], *[pltpu.HBM(l.shape, l.dtype) for l in lands]),
        in_specs=[_IN_HBM] * (2 * n) + [_SEM, _SEM, _HBM],
        out_specs=[_IN_HBM] * (2 * n),
        input_output_aliases={i: i for i in range(2 * n)},
        compiler_params=pltpu.CompilerParams(has_side_effects=_EFFECT),
    )(*srcs, *lands, send_sems, recv_sems, after)
    return list(outs[:n]), list(outs[n:])


def _ew_block(rows, cols, slots):
    budget = 2 * 1024 * 1024
    br, bc = rows, cols
    while slots * br * bc * 4 > budget:
        if br % 2 == 0 and (br // 2) % (2 * SUBLANES) == 0:
            br //= 2
        elif bc % 2 == 0 and (bc // 2) % LANES == 0:
            bc //= 2
        else:
            break
    return br, bc


def _add_sibling(grads, recv, c_idx, name):
    _, rows, cols = grads.shape
    br, bc = _ew_block(rows, cols, 3)

    def body(c_ref, g_ref, r_ref, out_ref):
        del c_ref
        out_ref[...] = (g_ref[...].astype(F32) + r_ref[...].astype(F32)).astype(out_ref.dtype)

    grid_spec = pltpu.PrefetchScalarGridSpec(
        num_scalar_prefetch=1, grid=(N_CHIP, rows // br, cols // bc),
        in_specs=[pl.BlockSpec((1, br, bc), lambda k, i, j, c_ref: (c_ref[0] + 2 * k, i, j)),
                  pl.BlockSpec((1, br, bc), lambda k, i, j, c_ref: (k, i, j))],
        out_specs=pl.BlockSpec((1, br, bc), lambda k, i, j, c_ref: (k, i, j)))
    return pl.pallas_call(
        body, name=name, grid_spec=grid_spec, out_shape=jax.ShapeDtypeStruct((N_CHIP, rows, cols), grads.dtype),
        compiler_params=_params(("parallel", "parallel", "parallel"), 3 * _nbytes((br, bc), F32)),
    )(c_idx, grads, recv)


def _adam_math(g, w, m, v):
    m2 = ADAM_B1 * m + (1.0 - ADAM_B1) * g
    v2 = ADAM_B2 * v + (1.0 - ADAM_B2) * (g * g)
    m_hat = m2 * (1.0 / (1.0 - ADAM_B1 ** ADAM_STEP))
    v_hat = v2 * (1.0 / (1.0 - ADAM_B2 ** ADAM_STEP))
    return -ADAM_LR * (m_hat / (jnp.sqrt(v_hat) + ADAM_EPS) + ADAM_WD * w), m2, v2


def _adamw(slots, w, m, v, name, own=None, own_slot=None):
    ns, rows, cols = slots.shape
    br, bc = _ew_block(rows, cols, 2 * ns + 7)

    def update(g, w_ref, m_ref, v_ref, g_ref, d_ref, m2_ref, v2_ref):
        g_ref[...] = g
        d_ref[...], m2_ref[...], v2_ref[...] = _adam_math(g, w_ref[...], m_ref[...], v_ref[...])

    out_shape = [jax.ShapeDtypeStruct((rows, cols), F32)] * 4
    params = _params(("parallel", "parallel"), (2 * ns + 7) * _nbytes((br, bc), F32))
    grid = (rows // br, cols // bc)
    if own is None:
        def body(s_ref, *rest):
            g = s_ref[0].astype(F32)
            for k in range(1, ns):
                g = g + s_ref[k].astype(F32)
            update(g, *rest)

        blk = pl.BlockSpec((br, bc), lambda i, j: (i, j))
        return pl.pallas_call(
            body, name=name, grid=grid,
            in_specs=[pl.BlockSpec((ns, br, bc), lambda i, j: (0, i, j)), blk, blk, blk], out_specs=[blk] * 4,
            out_shape=out_shape, compiler_params=params,
        )(slots, w, m, v)

    def body_own(slot_ref, s_ref, o_ref, *rest):
        g = None
        for k in range(ns):
            term = jnp.where(slot_ref[0] == k, o_ref[k].astype(F32), s_ref[k].astype(F32))
            g = term if g is None else g + term
        update(g, *rest)

    blk = pl.BlockSpec((br, bc), lambda i, j, slot_ref: (i, j))
    stack = pl.BlockSpec((ns, br, bc), lambda i, j, slot_ref: (0, i, j))
    grid_spec = pltpu.PrefetchScalarGridSpec(num_scalar_prefetch=1, grid=grid, in_specs=[stack, stack, blk, blk, blk],
                                             out_specs=[blk] * 4)
    return pl.pallas_call(body_own, name=name, grid_spec=grid_spec, out_shape=out_shape, compiler_params=params,
                          )(own_slot, slots, own, w, m, v)


def _adamw_small(all_g, last_g, params, extra_shapes, name):
    names = [n for n, _ in _SMALL_ROWS if n in params]
    extras = [n for n, _ in _SMALL_ROWS if n not in params]
    offs = _small_offsets()
    n_p = len(names)

    def body(*refs):
        s_ref, last_ref = refs[0], refs[1]
        wmv = refs[2:2 + 3 * n_p]
        outs = refs[2 + 3 * n_p:2 + 7 * n_p]
        extra_refs = refs[2 + 7 * n_p:2 + 7 * n_p + len(extras)]
        summed = refs[-1]
        g, g_last = s_ref[0], last_ref[0]
        for k in range(1, N_DEV):
            g, g_last = g + s_ref[k], g_last + last_ref[k]
        summed[...] = g
        last_offs = _small_offsets(_LAST_SMALL)
        for n, rows in _LAST_SMALL:
            summed[offs[n]:offs[n] + rows, :] = g_last[last_offs[n]:last_offs[n] + rows, :]
        for i, n in enumerate(names):
            w_ref, m_ref, v_ref = wmv[3 * i:3 * i + 3]
            g_ref, d_ref, m2_ref, v2_ref = outs[4 * i:4 * i + 4]
            _rows_to(summed, offs[n], g_ref)
            d_ref[...], m2_ref[...], v2_ref[...] = _adam_math(g_ref[...], w_ref[...], m_ref[...], v_ref[...])
        for n, ref in zip(extras, extra_refs):
            _rows_to(summed, offs[n], ref)

    flat = [a for n in names for a in params[n]]
    out_shape = [jax.ShapeDtypeStruct(params[n][0].shape, F32) for n in names for _ in range(4)]
    out_shape += [jax.ShapeDtypeStruct(s, F32) for s in extra_shapes]
    vmem = pl.BlockSpec(memory_space=pltpu.VMEM)
    res = pl.pallas_call(
        body, name=name, in_specs=[vmem] * (2 + len(flat)), out_specs=[vmem] * len(out_shape), out_shape=out_shape,
        scratch_shapes=[pltpu.VMEM(all_g.shape[1:], F32)],
        compiler_params=pltpu.CompilerParams(vmem_limit_bytes=_vmem_limit(_nbytes(all_g.shape, F32))),
    )(all_g, last_g, *flat)
    return {n: res[4 * i:4 * i + 4] for i, n in enumerate(names)}, res[4 * n_p:]


def _mm_tiles(mode, m, n, k):
    tn = min(n, 1024)
    if mode == "tn":
        return min(m, 1024), tn, min(k, 2048)
    if k <= 2048:
        return min(m, 1024), tn, k
    if k <= 4096:
        return min(m, 512), tn, k
    return min(m, 1024), tn, 2048


def _local_step(x, target, wts, small, exchange):
    t = x.shape[0]
    w_main_t, w_dt_t = wts["w_main_t"], wts["w_dt_t"]
    bsp_t = small["b_spatial"].T
    pad32 = lambda a: jnp.pad(a, ((0, 0), (0, DT_PAD - N_HEADS)))
    dtb, alog = pad32(small["dt_bias"]), pad32(small["a_log"])
    dskip_full = jnp.repeat(small["d_skip"], HEAD_DIM, axis=1)
    head_of_col = lax.broadcasted_iota(jnp.int32, (DT_PAD, D_INNER), 1) // HEAD_DIM
    e_bf = (head_of_col == lax.broadcasted_iota(jnp.int32, (DT_PAD, D_INNER), 0)).astype(BF16)

    def mm(a, b, mode, name, **kw):
        if mode == "nn":
            m, k, n = a.shape[0], a.shape[1], b.shape[1]
        elif mode == "nt":
            m, k, n = a.shape[0], a.shape[1], b.shape[0]
        else:
            m, k, n = a.shape[1], a.shape[0], b.shape[1]
        tm, tn, tk = _mm_tiles(mode, m, n, k)
        tm = min(tm, kw.pop("max_tm", tm))
        kw.setdefault("out_dtypes", (BF16,) if mode == "tn" else (F32,))
        if "extra_specs" in kw:
            kw["extra_specs"] = kw["extra_specs"](tm, tn)
        return _matmul(a, b, mode=mode, tm=tm, tn=tn, tk=tk, name=name, **kw)

    def out_tile(tm, tn):
        return (((tm, tn), lambda i, j: (i, j)),)

    def row_tiles(n_tiles, *vectors, gate_logits=False):
        def specs(tm, tn):
            out = [((tm, tn), lambda i, j: (i, j))] * n_tiles
            if gate_logits:
                out += [((tm, D_MODEL), lambda i, j, cb=COL_GATE // D_MODEL + half: (i, cb)) for half in range(2)]
            return tuple(out) + tuple(((1, w), lambda i, j, cb=cb: (0, cb)) for w, cb in vectors)
        return specs

    vec = lambda w: ((1, w), F32, (1, w), lambda i, j: (0, 0))
    fused_tm = 512

    h = _rms_fwd(x, small["norm_mix_g"], "rms_mix", deps=exchange.begin())
    proj = mm(h, w_main_t, "nt", "proj_main", j_outer=True, out_dtypes=(BF16,))
    dt_raw = mm(h, w_dt_t, "nt", "proj_dt")
    y_a = _gmlp_fwd(proj, small["v_norm_g"], small["v_norm_b"], small["w_spatial"], bsp_t, "gmlp_fwd")
    pre_conv, xc = _conv_fwd(proj, wts["conv_w"], small["conv_b"], "conv_fwd")
    y_ssd, y_b, sprev = _ssd_fwd(xc, proj, dt_raw, dtb, alog, dskip_full, small["ssm_norm_g"], e_bf, "ssd_fwd")
    wts = {**wts, **exchange.late_weights(y_b)}
    pa = mm(y_a, wts["w_proj_a"], "nn", "proj_a")
    pb, merged = mm(y_b, wts["w_proj_b"], "nn", "proj_b", epilogue=_merge_epilogue, out_dtypes=(F32, BF16), max_tm=fused_tm,
                    extras=(pa, proj, proj, small["b_gates"], small["b_gates"]),
                    extra_specs=row_tiles(1, (D_MODEL, 0), (D_MODEL, 1), gate_logits=True))
    x1, h2 = mm(merged, wts["w_out"], "nn", "out_proj", epilogue=_residual_rms_epilogue, out_dtypes=(F32, BF16),
                extras=(x, small["norm_mlp_g"]), extra_specs=row_tiles(1, (D_MODEL, 0)))

    def relu_sq(acc, ex, outs):
        r = jnp.maximum(acc, 0.0)
        outs[0][...] = (r * r).astype(BF16)

    act = mm(h2, wts["w_mlp_up"], "nn", "mlp_up", epilogue=relu_sq, out_dtypes=(BF16,), j_outer=True)
    dx2, dx2_b, g_final, _, loss = mm(
        act, wts["w_mlp_down"], "nn", "mlp_down", epilogue=_loss_epilogue, carry=True,
        out_dtypes=(F32, BF16, vec(D_MODEL), vec(D_MODEL), vec(LANES)),
        extras=(x1, small["norm_final_g"], target), extra_specs=lambda tm, tn: (
            ((tm, tn), lambda i, j: (i, j)), ((1, tn), lambda i, j: (0, 0)), ((tm, tn), lambda i, j: (i, j))))

    def relu_sq_bwd(acc, ex, outs):
        outs[0][...] = (acc * 2.0 * jnp.sqrt(ex[0][...].astype(F32))).astype(BF16)

    dup = mm(dx2_b, wts["w_mlp_down"], "nt", "d_act", epilogue=relu_sq_bwd, extras=(act,), extra_specs=out_tile,
             out_dtypes=(BF16,), j_outer=True)
    g_down = mm(act, dx2_b, "tn", "g_mlp_down")
    g_up = mm(h2, dup, "tn", "g_mlp_up")
    started = exchange.reduce("mlp", {"w_mlp_down": g_down, "w_mlp_up": g_up})
    dx1, dx1_b, g_mlp = mm(
        dup, wts["w_mlp_up"], "nt", "d_h2", deps=started, epilogue=_rms_bwd_epilogue, carry=True,
        out_dtypes=(F32, BF16, vec(D_MODEL)), extras=(x1, small["norm_mlp_g"], dx2), extra_specs=lambda tm, tn: (
            ((tm, tn), lambda i, j: (i, j)), ((1, tn), lambda i, j: (0, 0)), ((tm, tn), lambda i, j: (i, j))))

    g_out = mm(merged, dx1_b, "tn", "g_out")
    dpa, dpb, dproj, g_bgates = mm(
        dx1_b, wts["w_out"], "nt", "d_merged", epilogue=_merge_bwd_epilogue, carry=True, max_tm=fused_tm,
        out_dtypes=(BF16, BF16, ((t, MAIN_W), BF16, (fused_tm, 2 * D_MODEL), lambda i, j: (i, COL_GATE // (2 * D_MODEL))),
                    vec(2 * D_MODEL)),
        extras=(pa, pb, proj, proj, small["b_gates"], small["b_gates"]),
        extra_specs=row_tiles(2, (D_MODEL, 0), (D_MODEL, 1), gate_logits=True))
    g_pa = mm(y_a, dpa, "tn", "g_proj_a")
    g_pb = mm(y_b, dpb, "tn", "g_proj_b")
    started = exchange.reduce("proj", {"w_out": g_out, "w_proj_a": g_pa, "w_proj_b": g_pb})
    dya = mm(dpa, wts["w_proj_a"], "nt", "d_ya", deps=started)
    dyb = mm(dpb, wts["w_proj_b"], "nt", "d_yb")

    dproj, g_wsp, g_bsp_t, g_vg, g_vb = _gmlp_bwd(proj, dya, small["v_norm_g"], small["v_norm_b"], small["w_spatial"],
                                                   bsp_t, dproj, "gmlp_bwd")
    dproj, dxc, ddt, g_ng, g_dskip, g_alog, g_dtb = _ssd_bwd(dyb, y_ssd, xc, proj, dt_raw, sprev, dtb, alog, dskip_full,
                                                             small["ssm_norm_g"], e_bf, dproj, "ssd_bwd")
    dproj, g_convw, g_convb = _conv_bwd(proj, pre_conv, dxc, wts["conv_w"], dproj, "conv_bwd")

    small_grads = {
        "conv_w": g_convw, "loss": loss,
        "conv_b": g_convb, "dt_bias": g_dtb, "a_log": g_alog, "d_skip": g_dskip, "ssm_norm_g": g_ng,
        "v_norm_g": g_vg, "v_norm_b": g_vb, "w_spatial": g_wsp.reshape(GROUPS * CHUNK, CHUNK), "b_spatial": g_bsp_t.T,
        "b_gates": g_bgates, "norm_mlp_g": g_mlp, "norm_final_g": g_final,
    }
    g_main_t = mm(dproj, h, "tn", "g_in_main", deps=exchange.small(small_grads))
    g_dt_t = mm(ddt, h, "tn", "g_in_dt")
    started = exchange.reduce("in", {"w_in": _join_w_in(g_main_t, g_dt_t)})

    def input_grad(acc, ex, outs):
        x_ref, g_ref, res_ref, ddt_ref, wdt_ref = ex
        gg = jnp.zeros((1, D_MODEL), F32)
        for r in range(acc.shape[0] // ROW_TILE):
            rows = slice(r * ROW_TILE, (r + 1) * ROW_TILE)
            dh = acc[rows] + _dot(ddt_ref[rows, :], wdt_ref[...], _NN)
            dx, gg_r = _rms_pullback(x_ref[rows, :], g_ref[...], dh)
            outs[0][rows, :] = dx + res_ref[rows, :]
            gg = gg + gg_r

        @pl.when(_first_row_tile())
        def _():
            outs[1][...] = jnp.zeros_like(outs[1])

        outs[1][...] += gg

    grad_x, g_mix = mm(
        dproj, w_main_t, "nn", "d_h", epilogue=input_grad, deps=started, carry=True,
        out_dtypes=(F32, vec(D_MODEL)), extras=(x, small["norm_mix_g"], dx1, ddt, w_dt_t), extra_specs=lambda tm, tn: (
            ((tm, tn), lambda i, j: (i, j)), ((1, tn), lambda i, j: (0, 0)), ((tm, tn), lambda i, j: (i, j)),
            ((tm, DT_PAD), lambda i, j: (i, 0)), ((DT_PAD, D_MODEL), lambda i, j: (0, 0))))

    return grad_x, g_mix


def _split_w_in(w_full_t):
    dt0 = COL_GATE
    w_main_t = jnp.concatenate([w_full_t[:dt0], w_full_t[dt0 + N_HEADS:]], axis=0)
    w_dt_t = jnp.pad(w_full_t[dt0:dt0 + N_HEADS], ((0, DT_PAD - N_HEADS), (0, 0)))
    return w_main_t, w_dt_t


def _join_w_in(g_main_t, g_dt_t):
    dt0 = COL_GATE
    return jnp.concatenate([g_main_t[:dt0], g_dt_t[:N_HEADS], g_main_t[dt0:]], axis=0)


_LATE = ["w_proj_a", "w_proj_b", "w_out", "w_mlp_up", "w_mlp_down"]
_BY_COLS = ("w_mlp_up",)


class _Exchange:
    def __init__(self, late_shards, late_lands):
        self.late_shards, self.late_lands = late_shards, late_lands
        self.c_idx = lax.axis_index("c").astype(jnp.int32).reshape(1)
        self.chip_idx = (2 * lax.axis_index("x") + lax.axis_index("y")).astype(jnp.int32).reshape(1)
        self.pending = []

    def begin(self):
        self.late = _split_start(self.late_shards, self.late_lands, _gather_copies, N_DEV - 1, "gather_late_start")
        return [self.late[-1]]

    def late_weights(self, after):
        _, lands = _split_wait(self.late, _gather_copies, after, "gather_late_wait")
        whole = {}
        for n, g in zip(_LATE, lands):
            whole[n] = jnp.transpose(g, (1, 0, 2)).reshape(g.shape[1], -1) if n in _BY_COLS else g.reshape(-1, g.shape[2])
        return whole

    def reduce(self, tag, grads):
        names = list(grads)
        by_dev = []
        for n in names:
            g = grads[n]
            if n in _BY_COLS:
                by_dev.append(jnp.transpose(g.reshape(g.shape[0], N_DEV, -1), (1, 0, 2)))
            else:
                by_dev.append(g.reshape(N_DEV, -1, g.shape[1]))
        from_sibling = _swap_with_sibling(by_dev, "reduce_cores_" + tag)
        parts = [_add_sibling(g, r, self.c_idx, "add_cores_" + n) for n, g, r in zip(names, by_dev, from_sibling)]
        lands = [lax.empty(p.shape, p.dtype) for p in parts]
        started = _split_start(parts, lands, _scatter_copies, 3, "reduce_chips_start_" + tag)
        self.pending.append((tag, names, started))
        return [started[-1]]

    def small(self, grads):
        dev = 2 * self.chip_idx + self.c_idx
        packed, land = _pack_small(grads, dev, "pack_small")
        self.small_started = _split_start([packed], [land], _gather_copies, N_DEV - 1, "exchange_small_start")
        return [self.small_started[-1]]

    def finish(self, after):
        _, (all_small,) = _split_wait(self.small_started, _gather_copies, after, "exchange_small_wait")
        done = {}
        for tag, names, started in self.pending:
            parts, lands = _split_wait(started, _scatter_copies, after, "reduce_chips_wait_" + tag)
            for n, land, part in zip(names, lands, parts):
                done[n] = (land, part, self.chip_idx)
        return all_small, done


def kernel(x, norm_mix_g, w_in, conv_w, conv_b, dt_bias, a_log, d_skip, ssm_norm_g, v_norm_g, v_norm_b, w_spatial, b_spatial, b_gates, w_proj_a, w_proj_b, w_out, norm_mlp_g, w_mlp_up, w_mlp_down, norm_final_g, loss_target, m_norm_mix_g, m_w_in, m_conv_w, m_conv_b, m_dt_bias, m_a_log, m_d_skip, m_ssm_norm_g, m_v_norm_g, m_v_norm_b, m_w_spatial, m_b_spatial, m_b_gates, m_w_proj_a, m_w_proj_b, m_w_out, m_norm_mlp_g, m_w_mlp_up, m_w_mlp_down, m_norm_final_g, v_norm_mix_g, v_w_in, v_conv_w, v_conv_b, v_dt_bias, v_a_log, v_d_skip, v_ssm_norm_g, v_v_norm_g, v_v_norm_b, v_w_spatial, v_b_spatial, v_b_gates, v_w_proj_a, v_w_proj_b, v_w_out, v_norm_mlp_g, v_w_mlp_up, v_w_mlp_down, v_norm_final_g):
    given = dict(locals())
    names = ["norm_mix_g", "w_in", "conv_w", "conv_b", "dt_bias", "a_log", "d_skip", "ssm_norm_g", "v_norm_g", "v_norm_b",
             "w_spatial", "b_spatial", "b_gates", "w_proj_a", "w_proj_b", "w_out", "norm_mlp_g", "w_mlp_up", "w_mlp_down",
             "norm_final_g"]
    shapes = {n: given[n].shape for n in names}
    dev = 4 * lax.axis_index("x") + 2 * lax.axis_index("y") + lax.axis_index("c")

    shard2d = {"w_in": w_in[0].T, "w_proj_a": w_proj_a[0], "w_proj_b": w_proj_b[0], "w_out": w_out[0],
               "w_mlp_up": w_mlp_up[0], "w_mlp_down": w_mlp_down[0]}
    conv_shard = conv_w.reshape(CONV_WIDTH, -1)
    late_shards = [shard2d[n].astype(BF16) for n in _LATE]
    w_in_all, conv_all, *late_lands = _all_gather([shard2d["w_in"].astype(BF16), conv_shard], "gather_first",
                                                  own_only=late_shards)
    w_main_t, w_dt_t = _split_w_in(w_in_all.reshape(-1, D_MODEL))
    wts = {"w_main_t": w_main_t, "w_dt_t": w_dt_t, "conv_w": jnp.transpose(conv_all, (1, 0, 2)).reshape(CONV_WIDTH, -1)}
    small = {"norm_mix_g": norm_mix_g, "conv_b": conv_b, "dt_bias": dt_bias, "a_log": a_log, "d_skip": d_skip,
             "ssm_norm_g": ssm_norm_g, "v_norm_g": v_norm_g, "v_norm_b": v_norm_b, "w_spatial": w_spatial[0],
             "b_spatial": b_spatial[0], "b_gates": b_gates, "norm_mlp_g": norm_mlp_g,
             "norm_final_g": norm_final_g.reshape(1, -1)}

    exchange = _Exchange(late_shards, late_lands)
    grad_x, g_mix = _local_step(x[0], loss_target[0], wts, small, exchange)

    out = {}
    all_small, large = exchange.finish(grad_x)
    for n, (slots, own, own_slot) in large.items():
        moments = [given["m_" + n][0], given["v_" + n][0]]
        if n == "w_in":
            moments = [mom.T for mom in moments]
        res = _adamw(slots, shard2d[n], *moments, "adamw_" + n, own=own, own_slot=own_slot)
        out[n] = [(r.T if n == "w_in" else r).reshape(shapes[n]) for r in res]

    last_small = _exchange_small({"norm_mix_g": g_mix}, _LAST_SMALL, "exchange_last")
    small["w_spatial"] = small["w_spatial"].reshape(GROUPS * CHUNK, CHUNK)
    params = {n: (w2d, given["m_" + n].reshape(w2d.shape), given["v_" + n].reshape(w2d.shape)) for n, w2d in small.items()}
    updated, (g_conv_full, loss_all) = _adamw_small(all_small, last_small, params, [(CONV_WIDTH, CONV_DIM), (1, LANES)],
                                                    "adamw_small")
    for n, res in updated.items():
        out[n] = [r.reshape(shapes[n]) for r in res]
    width = shapes["conv_w"][-1]
    g_conv = lax.dynamic_slice(g_conv_full, (0, dev * width), (CONV_WIDTH, width))
    res = _adamw(g_conv[None], conv_shard, m_conv_w.reshape(CONV_WIDTH, -1), v_conv_w.reshape(CONV_WIDTH, -1), "adamw_conv_w")
    out["conv_w"] = [r.reshape(shapes["conv_w"]) for r in res]

    loss = loss_all[0, 0]
    return (loss, grad_x[None], *[out[n][0] for n in names], *[out[n][1] for n in names],
            *[out[n][2] for n in names], *[out[n][3] for n in names])
```

```python
import functools
import math

import jax
import jax.numpy as jnp
from jax import lax
from jax.experimental import pallas as pl
from jax.experimental.pallas import tpu as pltpu

F32 = jnp.float32
BF16 = jnp.bfloat16
MESH = pl.DeviceIdType.MESH

D_MODEL = 1024
NORM_EPS = 1e-6
CHUNK = 128
GROUPS = 8
D_INNER = 2048
HEAD_DIM = 64
N_HEADS = 32
D_STATE = 128
CONV_WIDTH = 4
CONV_DIM = 4096
D_FF = 4096
GROUP_W = D_INNER // GROUPS
N_DEV = 8
N_CHIP = 4

ADAM_LR = 0.001
ADAM_B1 = 0.9
ADAM_B2 = 0.999
ADAM_EPS = 1e-08
ADAM_WD = 0.01
ADAM_STEP = 10

MAIN_W = 2 * D_MODEL + D_INNER + CONV_DIM + 2 * D_MODEL
COL_Z = 2048
COL_XBC = 4096
COL_GATE = 8192
DT_PAD = 128

LANES = 128
SUBLANES = 8
VMEM_BYTES_V7X = 64 * 1024 * 1024
VMEM_BODY_TEMP = 24 * 1024 * 1024


def _vmem_limit(block_bytes):
    return int(min(2 * block_bytes + VMEM_BODY_TEMP, VMEM_BYTES_V7X - 8 * 1024 * 1024))


def _nbytes(shape, dtype):
    return math.prod(shape) * jnp.dtype(dtype).itemsize


_HBM = pl.BlockSpec(memory_space=pl.ANY)


def _params(sem, block_bytes):
    return pltpu.CompilerParams(dimension_semantics=sem, vmem_limit_bytes=_vmem_limit(block_bytes))


def _sigmoid(x):
    return 1.0 / (1.0 + jnp.exp(-x))


def _softplus(x):
    e = jnp.exp(-jnp.abs(x))
    u = 1.0 + e
    log1p_e = jnp.where(u == 1.0, e, jnp.log(u) * (e / jnp.where(u == 1.0, 1.0, u - 1.0)))
    return jnp.maximum(x, 0.0) + log1p_e


_SQRT_HALF = 0.7071067811865476
_INV_SQRT_2PI = 0.3989422804014327


def _gelu(x):
    return x * (lax.erf(x * _SQRT_HALF) + 1.0) * 0.5


def _gelu_grad(x):
    return 0.5 * (1.0 + lax.erf(x * _SQRT_HALF)) + x * jnp.exp(-0.5 * x * x) * _INV_SQRT_2PI


def _dot(a, b, dims):
    return lax.dot_general(a, b, (dims, ((), ())), preferred_element_type=F32)


_NN = ((1,), (0,))
_NT = ((1,), (1,))
_TN = ((0,), (0,))


def _split3(x):
    hi = x.astype(BF16)
    r1 = x - hi.astype(F32)
    mid = r1.astype(BF16)
    lo = (r1 - mid.astype(F32)).astype(BF16)
    return hi, mid, lo


def _dot_exact_rhs(x, e, dims):
    hi, mid, lo = _split3(x)
    return _dot(hi, e, dims) + _dot(mid, e, dims) + _dot(lo, e, dims)


def _dot_exact_lhs(e, x, dims):
    hi, mid, lo = _split3(x)
    return _dot(e, hi, dims) + _dot(e, mid, dims) + _dot(e, lo, dims)


def _tri(lower):
    r = lax.broadcasted_iota(jnp.int32, (CHUNK, CHUNK), 0)
    c = lax.broadcasted_iota(jnp.int32, (CHUNK, CHUNK), 1)
    return (r >= c) if lower else (r <= c)


def _matmul(a, b, *, mode, tm, tn, tk, out_dtypes, name, epilogue=None, extras=(), extra_specs=(), j_outer=False, deps=(),
            carry=False):
    if mode == "nn":
        (m, k), (_, n) = a.shape, b.shape
    elif mode == "nt":
        (m, k), (n, _) = a.shape, b.shape
    else:
        (k, m), (_, n) = a.shape, b.shape
    assert m % tm == 0 and n % tn == 0 and k % tk == 0, (name, m, n, k, tm, tn, tk)
    nk = k // tk
    n_extra, n_out = len(extras), len(out_dtypes)
    first_out = 2 + n_extra + len(deps)
    dims = {"nn": _NN, "nt": _NT, "tn": _TN}[mode]
    if epilogue is None:
        def epilogue(acc, ex, outs):
            outs[0][...] = acc.astype(outs[0].dtype)

    def body(*refs):
        a_ref, b_ref = refs[0], refs[1]
        ex_refs = refs[2:2 + n_extra]
        outs = refs[first_out:first_out + n_out]
        p = _dot(a_ref[...], b_ref[...], dims)
        if nk == 1:
            epilogue(p, ex_refs, outs)
        else:
            acc_ref = refs[first_out + n_out]
            kk = pl.program_id(2)

            @pl.when(kk == 0)
            def _():
                acc_ref[...] = p

            @pl.when(kk > 0)
            def _():
                acc_ref[...] += p

            @pl.when(kk == nk - 1)
            def _():
                epilogue(acc_ref[...], ex_refs, outs)

    if j_outer:
        grid = (n // tn, m // tm, nk)
        ij = lambda g0, g1: (g1, g0)
    else:
        grid = (m // tm, n // tn, nk)
        ij = lambda g0, g1: (g0, g1)

    def wrap(fn):
        return lambda g0, g1, kk: fn(*ij(g0, g1), kk)

    if mode == "nn":
        a_spec = pl.BlockSpec((tm, tk), wrap(lambda i, j, kk: (i, kk)))
        b_spec = pl.BlockSpec((tk, tn), wrap(lambda i, j, kk: (kk, j)))
        a_blk, b_blk = (tm, tk), (tk, tn)
    elif mode == "nt":
        a_spec = pl.BlockSpec((tm, tk), wrap(lambda i, j, kk: (i, kk)))
        b_spec = pl.BlockSpec((tn, tk), wrap(lambda i, j, kk: (j, kk)))
        a_blk, b_blk = (tm, tk), (tn, tk)
    else:
        a_spec = pl.BlockSpec((tk, tm), wrap(lambda i, j, kk: (kk, i)))
        b_spec = pl.BlockSpec((tk, tn), wrap(lambda i, j, kk: (kk, j)))
        a_blk, b_blk = (tk, tm), (tk, tn)
    ex_specs = [pl.BlockSpec(shape, wrap(lambda i, j, kk, f=f: f(i, j))) for shape, f in extra_specs]
    outs = [o if isinstance(o, tuple) else ((m, n), o, (tm, tn), lambda i, j: (i, j)) for o in out_dtypes]
    out_spec = [pl.BlockSpec(blk_shape, wrap(lambda i, j, kk, f=f: f(i, j))) for _, _, blk_shape, f in outs]
    out_shape = [jax.ShapeDtypeStruct(shape, dt) for shape, dt, _, _ in outs]
    blk = (_nbytes(a_blk, a.dtype) + _nbytes(b_blk, b.dtype) + sum(_nbytes(s, F32) for s, _ in extra_specs)
           + sum(_nbytes(blk_shape, dt) for _, dt, blk_shape, _ in outs) + _nbytes((tm, tn), F32))
    order = ("arbitrary",) * 3 if carry else ("parallel", "parallel", "arbitrary")
    res = pl.pallas_call(
        body, name=name, grid=grid,
        in_specs=[a_spec, b_spec] + ex_specs + [_HBM] * len(deps), out_specs=out_spec, out_shape=out_shape,
        scratch_shapes=[pltpu.VMEM((tm, tn), F32)] if nk > 1 else [],
        compiler_params=_params(order, blk),
    )(a, b, *extras, *deps)
    return res[0] if n_out == 1 else res


ROW_TILE = 256


def _row_spec(width, col_block=0, tile=ROW_TILE):
    return pl.BlockSpec((tile, width), lambda i, cb=col_block: (i, cb))


def _vec_spec(width, col_block=0):
    return pl.BlockSpec((1, width), lambda i, cb=col_block: (0, cb))


def _rms_fwd(x, g, name, deps=()):
    t = x.shape[0]

    def body(x_ref, g_ref, *rest):
        h_ref = rest[-1]
        xv = x_ref[...]
        r = lax.rsqrt(jnp.mean(xv * xv, axis=-1, keepdims=True) + NORM_EPS)
        h_ref[...] = (xv * r * g_ref[...]).astype(BF16)

    return pl.pallas_call(
        body, name=name, grid=(t // ROW_TILE,),
        in_specs=[_row_spec(D_MODEL), _vec_spec(D_MODEL)] + [_HBM] * len(deps), out_specs=_row_spec(D_MODEL),
        out_shape=jax.ShapeDtypeStruct((t, D_MODEL), BF16),
        compiler_params=_params(("parallel",), 3 * _nbytes((ROW_TILE, D_MODEL), F32)),
    )(x, g, *deps)


def _rms_scale(xv):
    r = lax.rsqrt(jnp.mean(xv * xv, axis=-1, keepdims=True) + NORM_EPS)
    return r, xv * r


def _rms_pullback(xv, g, dh):
    r, xh = _rms_scale(xv)
    dyg = dh * g
    return r * (dyg - xh * jnp.mean(dyg * xh, axis=-1, keepdims=True)), jnp.sum(dh * xh, axis=0, keepdims=True)


def _first_row_tile():
    return pl.program_id(0) == 0


def _residual_rms_epilogue(acc, ex, outs):
    x1 = acc + ex[0][...]
    outs[0][...] = x1
    _, xh = _rms_scale(x1)
    outs[1][...] = (xh * ex[1][...]).astype(BF16)


def _loss_epilogue(acc, ex, outs):
    dx_ref, dxb_ref, gg_ref, sq_ref, tot_ref = outs
    gv = ex[1][...]
    r, xh = _rms_scale(acc + ex[0][...])
    err = xh * gv - ex[2][...]
    dy = err * (1.0 / D_MODEL)
    dyg = dy * gv
    dx = r * (dyg - xh * jnp.mean(dyg * xh, axis=-1, keepdims=True))
    dx_ref[...] = dx
    dxb_ref[...] = dx.astype(BF16)

    @pl.when(_first_row_tile())
    def _():
        gg_ref[...] = jnp.zeros_like(gg_ref)
        sq_ref[...] = jnp.zeros_like(sq_ref)

    gg_ref[...] += jnp.sum(dy * xh, axis=0, keepdims=True)
    sq_ref[...] += jnp.sum(err * err, axis=0, keepdims=True)
    tot_ref[...] = jnp.broadcast_to(jnp.sum(sq_ref[...], axis=1, keepdims=True) * (0.5 / D_MODEL), tot_ref.shape)


def _rms_bwd_epilogue(dh, ex, outs):
    dx, gg = _rms_pullback(ex[0][...], ex[1][...], dh)
    dx = dx + ex[2][...]
    outs[0][...] = dx
    if len(outs) == 3:
        outs[1][...] = dx.astype(BF16)

    @pl.when(_first_row_tile())
    def _():
        outs[-1][...] = jnp.zeros_like(outs[-1])

    outs[-1][...] += gg


def _merge_epilogue(acc, ex, outs):
    outs[0][...] = acc
    ga = _sigmoid(ex[1][...].astype(F32) + ex[3][...])
    gb = _sigmoid(ex[2][...].astype(F32) + ex[4][...])
    outs[1][...] = (ga * ex[0][...] + gb * acc).astype(BF16)


def _merge_bwd_epilogue(dm, ex, outs):
    dpa_ref, dpb_ref, dgl_ref, gb_ref = outs
    ga = _sigmoid(ex[2][...].astype(F32) + ex[4][...])
    gb = _sigmoid(ex[3][...].astype(F32) + ex[5][...])
    dpa_ref[...] = (dm * ga).astype(BF16)
    dpb_ref[...] = (dm * gb).astype(BF16)
    dla = dm * ex[0][...] * ga * (1.0 - ga)
    dlb = dm * ex[1][...] * gb * (1.0 - gb)
    dgl_ref[:, :D_MODEL] = dla.astype(BF16)
    dgl_ref[:, D_MODEL:] = dlb.astype(BF16)

    @pl.when(_first_row_tile())
    def _():
        gb_ref[...] = jnp.zeros_like(gb_ref)

    gb_ref[:, :D_MODEL] += jnp.sum(dla, axis=0, keepdims=True)
    gb_ref[:, D_MODEL:] += jnp.sum(dlb, axis=0, keepdims=True)


GMLP_TILE = 512
GMLP_NC = GMLP_TILE // CHUNK


def _gmlp_common(u_pre, v_pre, vg, vb):
    u = _gelu(u_pre)
    v = _gelu(v_pre)
    mu = jnp.mean(v, axis=-1, keepdims=True)
    vc = v - mu
    rstd = lax.rsqrt(jnp.mean(vc * vc, axis=-1, keepdims=True) + NORM_EPS)
    vh = vc * rstd
    vn = vh * vg + vb
    return u, vh, vn, rstd


def _chunks_to_lanes(x, g):
    return jnp.concatenate([x[c * CHUNK:(c + 1) * CHUNK, g * CHUNK:(g + 1) * CHUNK] for c in range(GMLP_NC)], axis=1)


def _gmlp_fwd(proj, vg, vb, wsp, bsp_t, name):
    t = proj.shape[0]

    def body(u_ref, v_ref, vg_ref, vb_ref, w_ref, b_ref, ya_ref):
        u, _, vn, _ = _gmlp_common(u_ref[...].astype(F32), v_ref[...].astype(F32), vg_ref[...], vb_ref[...])
        mask = _tri(True)
        bt = b_ref[...]
        for g in range(GROUPS):
            w = jnp.where(mask, w_ref[g], 0.0).astype(BF16)
            vcat = _chunks_to_lanes(vn, g).astype(BF16)
            s = _dot(w, vcat, _NN) + bt[:, g:g + 1]
            for c in range(GMLP_NC):
                rows, cols = slice(c * CHUNK, (c + 1) * CHUNK), slice(g * CHUNK, (g + 1) * CHUNK)
                ya_ref[rows, cols] = (u[rows, cols] * s[:, c * CHUNK:(c + 1) * CHUNK]).astype(BF16)

    return pl.pallas_call(
        body, name=name, grid=(t // GMLP_TILE,),
        in_specs=[_row_spec(D_MODEL, 0, GMLP_TILE), _row_spec(D_MODEL, 1, GMLP_TILE), _vec_spec(D_MODEL),
                  _vec_spec(D_MODEL), pl.BlockSpec((GROUPS, CHUNK, CHUNK), lambda i: (0, 0, 0)),
                  pl.BlockSpec((CHUNK, GROUPS), lambda i: (0, 0))],
        out_specs=_row_spec(D_MODEL, 0, GMLP_TILE),
        out_shape=jax.ShapeDtypeStruct((t, D_MODEL), BF16),
        compiler_params=_params(("parallel",), 3 * _nbytes((GMLP_TILE, D_MODEL), F32)),
    )(proj, proj, vg, vb, wsp, bsp_t)


def _gmlp_bwd(proj, dya, vg, vb, wsp, bsp_t, dproj, name):
    t = proj.shape[0]

    def body(u_ref, v_ref, dya_ref, vg_ref, vb_ref, w_ref, b_ref, dproj_in, duv_ref, gw_ref, gbt_ref, gvg_ref, gvb_ref,
             dvn_scr, du_scr):
        del dproj_in
        u_pre, v_pre = u_ref[...].astype(F32), v_ref[...].astype(F32)
        vgv = vg_ref[...]
        u, vh, vn, rstd = _gmlp_common(u_pre, v_pre, vgv, vb_ref[...])
        dya = dya_ref[...]
        mask = _tri(True)
        bt = b_ref[...]
        first = pl.program_id(0) == 0

        @pl.when(first)
        def _():
            gw_ref[...] = jnp.zeros_like(gw_ref)
            gbt_ref[...] = jnp.zeros_like(gbt_ref)
            gvg_ref[...] = jnp.zeros_like(gvg_ref)
            gvb_ref[...] = jnp.zeros_like(gvb_ref)

        lane = lax.broadcasted_iota(jnp.int32, (CHUNK, GROUPS), 1)
        gbt = jnp.zeros((CHUNK, GROUPS), F32)
        for g in range(GROUPS):
            w = jnp.where(mask, w_ref[g], 0.0).astype(BF16)
            vcat = _chunks_to_lanes(vn, g).astype(BF16)
            s = _dot(w, vcat, _NN) + bt[:, g:g + 1]
            ds = _chunks_to_lanes(dya * u, g)
            gbt = jnp.where(lane == g, jnp.sum(ds, axis=1, keepdims=True), gbt)
            dsb = ds.astype(BF16)
            gw_ref[g] += jnp.where(mask, _dot(dsb, vcat, _NT), 0.0)
            dv = _dot(w, dsb, _TN)
            for c in range(GMLP_NC):
                rows, cols = slice(c * CHUNK, (c + 1) * CHUNK), slice(g * CHUNK, (g + 1) * CHUNK)
                dvn_scr[rows, cols] = dv[:, c * CHUNK:(c + 1) * CHUNK]
                du_scr[rows, cols] = dya[rows, cols] * s[:, c * CHUNK:(c + 1) * CHUNK]
        gbt_ref[...] += gbt
        dvn = dvn_scr[...]
        gvg_ref[...] += jnp.sum(dvn * vh, axis=0, keepdims=True)
        gvb_ref[...] += jnp.sum(dvn, axis=0, keepdims=True)
        dvh = dvn * vgv
        dv = rstd * (dvh - jnp.mean(dvh, axis=-1, keepdims=True) - vh * jnp.mean(dvh * vh, axis=-1, keepdims=True))
        duv_ref[:, :D_MODEL] = (du_scr[...] * _gelu_grad(u_pre)).astype(BF16)
        duv_ref[:, D_MODEL:] = (dv * _gelu_grad(v_pre)).astype(BF16)

    return pl.pallas_call(
        body, name=name, grid=(t // GMLP_TILE,),
        in_specs=[_row_spec(D_MODEL, 0, GMLP_TILE), _row_spec(D_MODEL, 1, GMLP_TILE), _row_spec(D_MODEL, 0, GMLP_TILE),
                  _vec_spec(D_MODEL), _vec_spec(D_MODEL), pl.BlockSpec((GROUPS, CHUNK, CHUNK), lambda i: (0, 0, 0)),
                  pl.BlockSpec((CHUNK, GROUPS), lambda i: (0, 0)), pl.BlockSpec(memory_space=pl.ANY)],
        out_specs=[_row_spec(2 * D_MODEL, 0, GMLP_TILE), pl.BlockSpec((GROUPS, CHUNK, CHUNK), lambda i: (0, 0, 0)),
                   pl.BlockSpec((CHUNK, GROUPS), lambda i: (0, 0)), _vec_spec(D_MODEL), _vec_spec(D_MODEL)],
        out_shape=[jax.ShapeDtypeStruct(dproj.shape, BF16), jax.ShapeDtypeStruct((GROUPS, CHUNK, CHUNK), F32),
                   jax.ShapeDtypeStruct((CHUNK, GROUPS), F32), jax.ShapeDtypeStruct((1, D_MODEL), F32),
                   jax.ShapeDtypeStruct((1, D_MODEL), F32)],
        scratch_shapes=[pltpu.VMEM((GMLP_TILE, D_MODEL), F32), pltpu.VMEM((GMLP_TILE, D_MODEL), F32)],
        input_output_aliases={7: 0},
        compiler_params=_params(("arbitrary",), 6 * _nbytes((GMLP_TILE, D_MODEL), F32)),
    )(proj, proj, dya, vg, vb, wsp, bsp_t, dproj)


CONV_TILE = 512
CONV_COLS = 1024
CONV_RB = 32
HALO = SUBLANES


def _conv_fwd(proj, cw, cb, name):
    t = proj.shape[0]
    nj = CONV_DIM // CONV_COLS
    xcb = COL_XBC // CONV_COLS
    before = 2 * HALO
    rb = CONV_TILE // before

    def body(x_ref, prev_ref, cw_ref, cb_ref, pre_ref, xc_ref):
        i = pl.program_id(1)
        cw_v = cw_ref[...]
        cb_v = cb_ref[...]
        for b in range(CONV_TILE // CONV_RB):
            if b == 0:
                prev = jnp.where(i > 0, prev_ref[...].astype(F32)[HALO:, :], 0.0)
                ext = jnp.concatenate([prev, x_ref[:CONV_RB, :].astype(F32)], axis=0)
            else:
                ext = x_ref[b * CONV_RB - before:(b + 1) * CONV_RB, :].astype(F32)[HALO:, :]
            pre = cb_v + cw_v[CONV_WIDTH - 1:CONV_WIDTH, :] * ext[HALO:, :]
            for k in range(CONV_WIDTH - 1):
                back = CONV_WIDTH - 1 - k
                pre = pre + cw_v[k:k + 1, :] * pltpu.roll(ext, back, 0)[HALO:, :]
            pre_ref[b * CONV_RB:(b + 1) * CONV_RB, :] = pre
            xc_ref[b * CONV_RB:(b + 1) * CONV_RB, :] = pre * _sigmoid(pre)

    tile = pl.BlockSpec((CONV_TILE, CONV_COLS), lambda j, i: (i, j))
    return pl.pallas_call(
        body, name=name, grid=(nj, t // CONV_TILE),
        in_specs=[pl.BlockSpec((CONV_TILE, CONV_COLS), lambda j, i: (i, xcb + j)),
                  pl.BlockSpec((before, CONV_COLS), lambda j, i: (jnp.maximum(i * rb - 1, 0), xcb + j)),
                  pl.BlockSpec((CONV_WIDTH, CONV_COLS), lambda j, i: (0, j)),
                  pl.BlockSpec((1, CONV_COLS), lambda j, i: (0, j))],
        out_specs=[tile, tile],
        out_shape=[jax.ShapeDtypeStruct((t, CONV_DIM), F32), jax.ShapeDtypeStruct((t, CONV_DIM), F32)],
        compiler_params=_params(("parallel", "parallel"), 4 * _nbytes((CONV_TILE, CONV_COLS), F32)),
    )(proj, proj, cw, cb)


def _fold_rows(v):
    out = v[:SUBLANES]
    for r in range(1, v.shape[0] // SUBLANES):
        out = out + v[r * SUBLANES:(r + 1) * SUBLANES]
    return out


def _conv_bwd(proj, pre, dxc, cw, dproj, name):
    t = proj.shape[0]
    nj = CONV_DIM // CONV_COLS
    ni = t // CONV_TILE
    xcb = COL_XBC // CONV_COLS
    rb = CONV_TILE // HALO
    last_rb = t // HALO - 1

    def body(x_ref, p_ref, pnext_ref, d_ref, dnext_ref, cw_ref, dproj_in, dx_ref, gw_ref, gb_ref):
        del dproj_in
        i = pl.program_id(1)
        cw_v = cw_ref[...]

        def dpre_of(p, d):
            sg = _sigmoid(p)
            return d * sg * (1.0 + p * (1.0 - sg))

        @pl.when(i == 0)
        def _():
            gw_ref[...] = jnp.zeros_like(gw_ref)
            gb_ref[...] = jnp.zeros_like(gb_ref)

        head = dpre_of(pnext_ref[...], jnp.where(i < ni - 1, dnext_ref[...], 0.0))
        gb_acc = jnp.zeros((SUBLANES, CONV_COLS), F32)
        gw_acc = [jnp.zeros((SUBLANES, CONV_COLS), F32) for _ in range(CONV_WIDTH)]
        for b in reversed(range(CONV_TILE // CONV_RB)):
            rows = slice(b * CONV_RB, (b + 1) * CONV_RB)
            cur = dpre_of(p_ref[rows, :], d_ref[rows, :])
            ext = jnp.concatenate([cur, head], axis=0)
            xv = x_ref[rows, :].astype(F32)
            dx = None
            for k in range(CONV_WIDTH):
                shift = CONV_WIDTH - 1 - k
                win = cur if shift == 0 else pltpu.roll(ext, CONV_RB + HALO - shift, 0)[:CONV_RB, :]
                term = cw_v[k:k + 1, :] * win
                dx = term if dx is None else dx + term
                gw_acc[k] = gw_acc[k] + _fold_rows(win * xv)
            dx_ref[rows, :] = dx.astype(BF16)
            gb_acc = gb_acc + _fold_rows(cur)
            head = cur[:HALO]
        gb_ref[...] += jnp.sum(gb_acc, axis=0, keepdims=True)
        for k in range(CONV_WIDTH):
            gw_ref[k:k + 1, :] += jnp.sum(gw_acc[k], axis=0, keepdims=True)

    tile = pl.BlockSpec((CONV_TILE, CONV_COLS), lambda j, i: (i, j))
    after = pl.BlockSpec((HALO, CONV_COLS), lambda j, i: (jnp.minimum((i + 1) * rb, last_rb), j))
    return pl.pallas_call(
        body, name=name, grid=(nj, ni),
        in_specs=[pl.BlockSpec((CONV_TILE, CONV_COLS), lambda j, i: (i, xcb + j)), tile, after, tile, after,
                  pl.BlockSpec((CONV_WIDTH, CONV_COLS), lambda j, i: (0, j)),
                  pl.BlockSpec(memory_space=pl.ANY)],
        out_specs=[pl.BlockSpec((CONV_TILE, CONV_COLS), lambda j, i: (i, xcb + j)),
                   pl.BlockSpec((CONV_WIDTH, CONV_COLS), lambda j, i: (0, j)),
                   pl.BlockSpec((1, CONV_COLS), lambda j, i: (0, j))],
        out_shape=[jax.ShapeDtypeStruct(dproj.shape, BF16), jax.ShapeDtypeStruct((CONV_WIDTH, CONV_DIM), F32),
                   jax.ShapeDtypeStruct((1, CONV_DIM), F32)],
        input_output_aliases={6: 0},
        compiler_params=_params(("parallel", "arbitrary"), 4 * _nbytes((CONV_TILE, CONV_COLS), F32)),
    )(proj, pre, pre, dxc, dxc, cw, dproj)


def _ssd_decays(dt_raw, dtb, alog, e_bf, tril_bf):
    dtv = _softplus(dt_raw + dtb)
    a = -jnp.exp(alog)
    cs = _dot_exact_lhs(tril_bf, dtv * a, _NN)
    cs_last = cs[CHUNK - 1:CHUNK, :]
    stack = jnp.concatenate([dtv, jnp.exp(cs), jnp.exp(cs_last - cs)], axis=0)
    full = _head_expand(stack, e_bf)
    return dtv, a, cs, full[:CHUNK], full[CHUNK:2 * CHUNK], full[2 * CHUNK:]


def _split2(x):
    hi = x.astype(BF16)
    return hi, (x - hi.astype(F32)).astype(BF16)


def _head_expand(x, e_bf):
    hi, mid = _split2(x)
    return _dot(hi, e_bf, _NN) + _dot(mid, e_bf, _NN)


def _head_sums(x, e_bf):
    hi, mid = _split2(x)
    return _dot(hi, e_bf, _NT) + _dot(mid, e_bf, _NT)


def _head_mats(cs, cs_t, cb, h, mask):
    seg = cs[:, h:h + 1] - cs_t[h:h + 1, :]
    lmat = jnp.exp(jnp.where(mask, seg, -jnp.inf))
    return lmat, cb * lmat


def _ssd_fwd(xc, proj, dt_raw, dtb, alog, dskip_full, ng, e_bf, name):
    t = xc.shape[0]
    nc = t // CHUNK
    zcb = COL_Z // D_INNER

    def body(xc_ref, z_ref, dt_ref, dtb_ref, alog_ref, dsk_ref, ng_ref, e_ref, y_ref, yb_ref, sprev_ref, s_scr):
        @pl.when(pl.program_id(0) == 0)
        def _():
            s_scr[...] = jnp.zeros_like(s_scr)

        mask = _tri(True)
        tril_bf = mask.astype(BF16)
        e_v = e_ref[...]
        _, _, cs, dt_full, ecs_full, decay_full = _ssd_decays(dt_ref[...], dtb_ref[...], alog_ref[...], e_v, tril_bf)
        cs_t = cs.T
        sprev_ref[0] = s_scr[...]
        for g in range(GROUPS):
            gc = slice(g * GROUP_W, (g + 1) * GROUP_W)
            xs = xc_ref[:, gc]
            xdt = xs * dt_full[:, gc]
            xdt_b = xdt.astype(BF16)
            xdec = (xdt * decay_full[:, gc]).astype(BF16)
            bg = xc_ref[:, D_INNER + g * D_STATE:D_INNER + (g + 1) * D_STATE].astype(BF16)
            cg = xc_ref[:, D_INNER + GROUPS * D_STATE + g * D_STATE:D_INNER + GROUPS * D_STATE + (g + 1) * D_STATE].astype(BF16)
            cb = _dot(cg, bg, _NT)
            s_prev = s_scr[:, gc]
            y_off = ecs_full[:, gc] * _dot(cg, s_prev.astype(BF16), _NN)
            s_scr[:, gc] = s_prev * ecs_full[CHUNK - 1:CHUNK, gc] + _dot(bg, xdec, _TN)
            parts = []
            for r in range(GROUP_W // HEAD_DIM):
                h = g * (GROUP_W // HEAD_DIM) + r
                _, m = _head_mats(cs, cs_t, cb, h, mask)
                parts.append(_dot(m.astype(BF16), xdt_b[:, r * HEAD_DIM:(r + 1) * HEAD_DIM], _NN))
            yg = jnp.concatenate(parts, axis=1) + y_off + dsk_ref[:, gc] * xs
            y_ref[:, gc] = yg
            zv = z_ref[:, gc].astype(F32)
            ygate = yg * (zv * _sigmoid(zv))
            rstd = lax.rsqrt(jnp.mean(ygate * ygate, axis=-1, keepdims=True) + NORM_EPS)
            yb_ref[:, gc] = (ygate * rstd * ng_ref[:, gc]).astype(BF16)

    vec = lambda w: pl.BlockSpec((1, w), lambda i: (0, 0))
    blk = _nbytes((CHUNK, CONV_DIM), F32) + 3 * _nbytes((CHUNK, D_INNER), F32) + _nbytes((D_STATE, D_INNER), F32)
    return pl.pallas_call(
        body, name=name, grid=(nc,),
        in_specs=[pl.BlockSpec((CHUNK, CONV_DIM), lambda i: (i, 0)), pl.BlockSpec((CHUNK, D_INNER), lambda i: (i, zcb)),
                  pl.BlockSpec((CHUNK, DT_PAD), lambda i: (i, 0)), vec(DT_PAD), vec(DT_PAD), vec(D_INNER), vec(D_INNER),
                  pl.BlockSpec((DT_PAD, D_INNER), lambda i: (0, 0))],
        out_specs=[pl.BlockSpec((CHUNK, D_INNER), lambda i: (i, 0)), pl.BlockSpec((CHUNK, D_INNER), lambda i: (i, 0)),
                   pl.BlockSpec((1, D_STATE, D_INNER), lambda i: (i, 0, 0))],
        out_shape=[jax.ShapeDtypeStruct((t, D_INNER), F32), jax.ShapeDtypeStruct((t, D_INNER), BF16),
                   jax.ShapeDtypeStruct((nc, D_STATE, D_INNER), F32)],
        scratch_shapes=[pltpu.VMEM((D_STATE, D_INNER), F32)],
        compiler_params=_params(("arbitrary",), blk),
    )(xc, proj, dt_raw, dtb, alog, dskip_full, ng, e_bf)


def _ssd_bwd(dyb, y, xc, proj, dt_raw, sprev, dtb, alog, dskip_full, ng, e_bf, dproj, name):
    t = xc.shape[0]
    nc = t // CHUNK
    zcb = COL_Z // D_INNER
    hpg = GROUP_W // HEAD_DIM
    rev = lambda i: nc - 1 - i

    def body(dyb_ref, y_ref, xc_ref, z_ref, dt_ref, sprev_ref, dtb_ref, alog_ref, dsk_ref, ng_ref, e_ref, dproj_in,
             dz_ref, dxc_ref, ddt_ref, gng_ref, gdsk_ref, galog_ref, gdtb_ref, ds_scr, sums_scr):
        del dproj_in

        @pl.when(pl.program_id(0) == 0)
        def _():
            ds_scr[...] = jnp.zeros_like(ds_scr)
            gng_ref[...] = jnp.zeros_like(gng_ref)
            gdsk_ref[...] = jnp.zeros_like(gdsk_ref)
            galog_ref[...] = jnp.zeros_like(galog_ref)
            gdtb_ref[...] = jnp.zeros_like(gdtb_ref)

        mask = _tri(True)
        tril_bf = mask.astype(BF16)
        triu_bf = _tri(False).astype(BF16)
        e_v = e_ref[...]
        dt_in = dt_ref[...] + dtb_ref[...]
        dtv, a, cs, dt_full, ecs_full, decay_full = _ssd_decays(dt_ref[...], dtb_ref[...], alog_ref[...], e_v, tril_bf)
        cs_t = cs.T

        lane_h = lax.broadcasted_iota(jnp.int32, (CHUNK, DT_PAD), 1)
        sub_h = lax.broadcasted_iota(jnp.int32, (DT_PAD, CHUNK), 0)
        dcs_rows = jnp.zeros((CHUNK, DT_PAD), F32)
        dcs_cols_t = jnp.zeros((DT_PAD, CHUNK), F32)
        last_cols, dsk_cols = [], []
        for g in range(GROUPS):
            gc = slice(g * GROUP_W, (g + 1) * GROUP_W)
            b_cols = slice(D_INNER + g * D_STATE, D_INNER + (g + 1) * D_STATE)
            c_cols = slice(D_INNER + GROUPS * D_STATE + g * D_STATE, D_INNER + GROUPS * D_STATE + (g + 1) * D_STATE)
            xs = xc_ref[:, gc]
            xdt = xs * dt_full[:, gc]
            xdt_b = xdt.astype(BF16)
            xdec = xdt * decay_full[:, gc]
            xdec_b = xdec.astype(BF16)
            zv = z_ref[:, gc].astype(F32)
            sg = _sigmoid(zv)
            gate = zv * sg
            yv = y_ref[:, gc]
            dybv = dyb_ref[:, gc]
            ygate = yv * gate
            rstd = lax.rsqrt(jnp.mean(ygate * ygate, axis=-1, keepdims=True) + NORM_EPS)
            yn = ygate * rstd
            gng_ref[:, gc] += jnp.sum(dybv * yn, axis=0, keepdims=True)
            dyn = dybv * ng_ref[:, gc]
            dyg = rstd * (dyn - yn * jnp.mean(dyn * yn, axis=-1, keepdims=True))
            dz_ref[:, gc] = (dyg * yv * sg * (1.0 + zv * (1.0 - sg))).astype(BF16)
            dy = dyg * gate
            dy_b = dy.astype(BF16)
            dyo = dy * ecs_full[:, gc]
            dyo_b = dyo.astype(BF16)
            dsk_cols.append(jnp.sum(dy * xs, axis=0, keepdims=True))

            bg = xc_ref[:, b_cols].astype(BF16)
            cg = xc_ref[:, c_cols].astype(BF16)
            s_prev = sprev_ref[0, :, gc]
            s_prev_b = s_prev.astype(BF16)
            dsg = ds_scr[:, gc]
            dsg_b = dsg.astype(BF16)
            cb = _dot(cg, bg, _NT)
            c_s = _dot(cg, s_prev_b, _NN)
            b_ds = _dot(bg, dsg_b, _NN)
            dcb = jnp.zeros((CHUNK, CHUNK), F32)
            parts = []
            for r in range(hpg):
                h = g * hpg + r
                hc = slice(r * HEAD_DIM, (r + 1) * HEAD_DIM)
                lmat, m = _head_mats(cs, cs_t, cb, h, mask)
                dm = _dot(dy_b[:, hc], xdt_b[:, hc], _NT)
                parts.append(_dot(m.astype(BF16), dy_b[:, hc], _TN))
                dcb = dcb + dm * lmat
                w = dm * m
                dcs_rows = jnp.where(lane_h == h, jnp.sum(w, axis=1, keepdims=True), dcs_rows)
                dcs_cols_t = jnp.where(sub_h == h, jnp.sum(w, axis=0, keepdims=True), dcs_cols_t)
            dxdt = jnp.concatenate(parts, axis=1) + decay_full[:, gc] * b_ds
            dcb_b = dcb.astype(BF16)
            dxc_ref[:, c_cols] = _dot(dcb_b, bg, _NN) + _dot(dyo_b, s_prev_b, _NT)
            dxc_ref[:, b_cols] = _dot(dcb_b, cg, _TN) + _dot(xdec_b, dsg_b, _NT)
            cdec = ecs_full[CHUNK - 1:CHUNK, gc]
            ds_scr[:, gc] = _dot(cg, dyo_b, _TN) + cdec * dsg
            dxc_ref[:, gc] = dxdt * dt_full[:, gc] + dsk_ref[:, gc] * dy
            dec_prod = xdec * b_ds
            sums_scr[:CHUNK, gc] = dyo * c_s - dec_prod
            sums_scr[CHUNK:, gc] = dxdt * xs
            last_cols.append(jnp.sum(dec_prod, axis=0, keepdims=True) + cdec * jnp.sum(dsg * s_prev, axis=0, keepdims=True))
        t_sums = _head_sums(sums_scr[...], e_v)
        tail = jnp.concatenate([jnp.concatenate(last_cols, axis=1), jnp.concatenate(dsk_cols, axis=1),
                                jnp.zeros((SUBLANES - 2, D_INNER), F32)], axis=0)
        t_tail = _dot_exact_rhs(tail, e_v, _NT)
        gdsk_ref[...] += t_tail[1:2, :]
        row = lax.broadcasted_iota(jnp.int32, (CHUNK, DT_PAD), 0)
        dcs = dcs_rows - dcs_cols_t.T + t_sums[:CHUNK] + jnp.where(row == CHUNK - 1, t_tail[0:1, :], 0.0)
        dda = _dot_exact_lhs(triu_bf, dcs, _NN)
        galog_ref[...] += jnp.sum(dda * dtv, axis=0, keepdims=True) * a
        ddt = dda * a + t_sums[CHUNK:]
        ddt_raw = jnp.where(lane_h < N_HEADS, ddt * _sigmoid(dt_in), 0.0)
        gdtb_ref[...] += jnp.sum(ddt_raw, axis=0, keepdims=True)
        ddt_ref[...] = ddt_raw.astype(BF16)

    vec = lambda w: pl.BlockSpec((1, w), lambda i: (0, 0))
    blk = (2 * _nbytes((CHUNK, CONV_DIM), F32) + 4 * _nbytes((CHUNK, D_INNER), F32) + 4 * _nbytes((D_STATE, D_INNER), F32))
    return pl.pallas_call(
        body, name=name, grid=(nc,),
        in_specs=[pl.BlockSpec((CHUNK, D_INNER), lambda i: (rev(i), 0)), pl.BlockSpec((CHUNK, D_INNER), lambda i: (rev(i), 0)),
                  pl.BlockSpec((CHUNK, CONV_DIM), lambda i: (rev(i), 0)), pl.BlockSpec((CHUNK, D_INNER), lambda i: (rev(i), zcb)),
                  pl.BlockSpec((CHUNK, DT_PAD), lambda i: (rev(i), 0)), pl.BlockSpec((1, D_STATE, D_INNER), lambda i: (rev(i), 0, 0)),
                  vec(DT_PAD), vec(DT_PAD), vec(D_INNER), vec(D_INNER), pl.BlockSpec((DT_PAD, D_INNER), lambda i: (0, 0)),
                  pl.BlockSpec(memory_space=pl.ANY)],
        out_specs=[pl.BlockSpec((CHUNK, D_INNER), lambda i: (rev(i), zcb)), pl.BlockSpec((CHUNK, CONV_DIM), lambda i: (rev(i), 0)),
                   pl.BlockSpec((CHUNK, DT_PAD), lambda i: (rev(i), 0)), vec(D_INNER), vec(DT_PAD), vec(DT_PAD), vec(DT_PAD)],
        out_shape=[jax.ShapeDtypeStruct(dproj.shape, BF16), jax.ShapeDtypeStruct((t, CONV_DIM), F32),
                   jax.ShapeDtypeStruct((t, DT_PAD), BF16), jax.ShapeDtypeStruct((1, D_INNER), F32),
                   jax.ShapeDtypeStruct((1, DT_PAD), F32), jax.ShapeDtypeStruct((1, DT_PAD), F32),
                   jax.ShapeDtypeStruct((1, DT_PAD), F32)],
        scratch_shapes=[pltpu.VMEM((D_STATE, D_INNER), F32), pltpu.VMEM((2 * CHUNK, D_INNER), F32)],
        input_output_aliases={11: 0},
        compiler_params=_params(("arbitrary",), blk),
    )(dyb, y, xc, proj, dt_raw, sprev, dtb, alog, dskip_full, ng, e_bf, dproj)


def _mesh_pos():
    return lax.axis_index("x"), lax.axis_index("y"), lax.axis_index("c")


def _other_chips(x, y):
    return [(1 - x, y), (x, 1 - y), (1 - x, 1 - y)]


def _all_peers(x, y, c):
    peers = []
    for k in range(1, N_DEV):
        fx, fy, fc = (k >> 2) & 1, (k >> 1) & 1, k & 1
        px, py, pc = x + fx - 2 * x * fx, y + fy - 2 * y * fy, c + fc - 2 * c * fc
        peers.append(((px, py, pc), 4 * px + 2 * py + pc))
    return peers


def _all_gather(shards, name, own_only=()):
    n, n_own = len(shards), len(own_only)

    def body(*refs):
        ins, own_ins = refs[:n], refs[n:n + n_own]
        outs, own_outs = refs[n + n_own:2 * n + n_own], refs[2 * n + n_own:2 * (n + n_own)]
        send_sems, recv_sems, local_sems = refs[2 * (n + n_own):]
        x, y, c = _mesh_pos()
        me, sibling = (x, y, c), (x, y, 1 - c)
        chips = _other_chips(x, y)

        def slot(p):
            return 4 * p[0] + 2 * p[1] + p[2]

        def copy(a, k, block, to, src=None):
            dst = outs[a].at[slot(block)]
            return pltpu.make_async_remote_copy(
                src_ref=dst if src is None else src, dst_ref=dst, send_sem=send_sems.at[a * 7 + k],
                recv_sem=recv_sems.at[a * 7 + k], device_id=to, device_id_type=MESH)

        started = []
        own = []
        for a in range(n_own):
            mine = pltpu.make_async_copy(own_ins[a], own_outs[a].at[slot(me)], local_sems.at[n + a])
            mine.start()
            own.append(mine)
        for a in range(n):
            mine = pltpu.make_async_copy(ins[a], outs[a].at[slot(me)], local_sems.at[a])
            mine.start()
            own.append(mine)
            first = [copy(a, 0, me, sibling, src=ins[a])]
            first += [copy(a, 1 + j, me, (*chip, c), src=ins[a]) for j, chip in enumerate(chips)]
            for cp in first:
                cp.start()
            started += first
        for a in range(n):
            for j, chip in enumerate(chips):
                copy(a, 1 + j, (*chip, c), me).wait_recv()
                fwd = copy(a, 4 + j, (*chip, c), sibling)
                fwd.start()
                started.append(fwd)
        for a in range(n):
            copy(a, 0, sibling, me).wait_recv()
            for j, chip in enumerate(chips):
                copy(a, 4 + j, (*chip, 1 - c), me).wait_recv()
        for cp in started:
            cp.wait_send()
        for mine in own:
            mine.wait()

    return pl.pallas_call(
        body, name=name,
        in_specs=[_HBM] * (n + n_own), out_specs=[_HBM] * (n + n_own),
        out_shape=[jax.ShapeDtypeStruct((N_DEV,) + s.shape, s.dtype) for s in (*shards, *own_only)],
        scratch_shapes=[pltpu.SemaphoreType.DMA((7 * n,)), pltpu.SemaphoreType.DMA((7 * n,)),
                        pltpu.SemaphoreType.DMA((n + n_own,))],
    )(*shards, *own_only)


_SMALL_ROWS = (("norm_mix_g", 8), ("conv_b", 32), ("dt_bias", 1), ("a_log", 1), ("d_skip", 1), ("ssm_norm_g", 16),
               ("v_norm_g", 8), ("v_norm_b", 8), ("w_spatial", 1024), ("b_spatial", 8), ("b_gates", 16), ("norm_mlp_g", 8),
               ("norm_final_g", 8), ("conv_w", 128), ("loss", 1))
_LAST_SMALL = (("norm_mix_g", 8),)


def _packed_rows(table):
    return -(-sum(r for _, r in table) // SUBLANES) * SUBLANES


def _small_offsets(table=_SMALL_ROWS):
    offs, r = {}, 0
    for name, rows in table:
        offs[name] = r
        r += rows
    return offs


def _rows_from(src_ref, dst_ref, r0):
    k, w = src_ref.shape
    if w <= LANES:
        dst_ref[r0:r0 + k, 0:w] = src_ref[...]
        return
    per = w // LANES
    for i in range(k):
        for j in range(per):
            dst_ref[r0 + i * per + j:r0 + i * per + j + 1, :] = src_ref[i:i + 1, j * LANES:(j + 1) * LANES]


def _rows_to(src_ref, r0, dst_ref):
    k, w = dst_ref.shape
    if w <= LANES:
        dst_ref[...] = src_ref[r0:r0 + k, 0:w]
        return
    per = w // LANES
    for i in range(k):
        for j in range(per):
            dst_ref[i:i + 1, j * LANES:(j + 1) * LANES] = src_ref[r0 + i * per + j:r0 + i * per + j + 1, :]


def _pack_small(grads, slot_idx, name):
    names = [n for n, _ in _SMALL_ROWS if n in grads]
    offs = _small_offsets()
    rows = _packed_rows(_SMALL_ROWS)

    def body(slot_ref, *refs):
        del slot_ref
        ins, (packed_ref, land_ref) = refs[:len(names)], refs[len(names):]
        packed_ref[...] = jnp.zeros_like(packed_ref)
        for n, ref in zip(names, ins):
            _rows_from(ref, packed_ref, offs[n])
        land_ref[0] = packed_ref[...]

    whole = lambda shape: pl.BlockSpec(shape, lambda i, slot_ref: (0,) * len(shape))
    grid_spec = pltpu.PrefetchScalarGridSpec(
        num_scalar_prefetch=1, grid=(1,), in_specs=[whole(grads[n].shape) for n in names],
        out_specs=[whole((rows, LANES)), pl.BlockSpec((1, rows, LANES), lambda i, slot_ref: (slot_ref[0], 0, 0))])
    return pl.pallas_call(
        body, name=name, grid_spec=grid_spec,
        out_shape=[jax.ShapeDtypeStruct((rows, LANES), F32), jax.ShapeDtypeStruct((N_DEV, rows, LANES), F32)],
    )(slot_idx, *[grads[n] for n in names])


def _exchange_small(grads, table, name):
    names = [n for n, _ in table]
    offs = _small_offsets(table)
    n_in = len(names)
    packed_rows = _packed_rows(table)

    def body(*refs):
        ins, out_ref = refs[:n_in], refs[n_in]
        packed, send_sems, recv_sems, local_sem = refs[n_in + 1:]
        packed[...] = jnp.zeros_like(packed)
        for n, ref in zip(names, ins):
            _rows_from(ref, packed, offs[n])
        x, y, c = _mesh_pos()
        my_slot = 4 * x + 2 * y + c
        mine = pltpu.make_async_copy(packed, out_ref.at[my_slot], local_sem)
        mine.start()
        copies = []
        for k, (peer, peer_slot) in enumerate(_all_peers(x, y, c)):
            sems = dict(send_sem=send_sems.at[k], recv_sem=recv_sems.at[k], device_id=peer, device_id_type=MESH)
            send = pltpu.make_async_remote_copy(src_ref=packed, dst_ref=out_ref.at[my_slot], **sems)
            send.start()
            copies.append((send, pltpu.make_async_remote_copy(src_ref=packed, dst_ref=out_ref.at[peer_slot], **sems)))
        for send, recv in copies:
            send.wait_send()
            recv.wait_recv()
        mine.wait()

    return pl.pallas_call(
        body, name=name, in_specs=[pl.BlockSpec(memory_space=pltpu.VMEM)] * n_in, out_specs=_HBM,
        out_shape=jax.ShapeDtypeStruct((N_DEV, packed_rows, LANES), F32),
        scratch_shapes=[pltpu.VMEM((packed_rows, LANES), F32), pltpu.SemaphoreType.DMA((N_DEV - 1,)),
                        pltpu.SemaphoreType.DMA((N_DEV - 1,)), pltpu.SemaphoreType.DMA],
    )(*[grads[n] for n in names])


def _swap_with_sibling(grads, name):
    n = len(grads)

    def body(*refs):
        ins, outs = refs[:n], refs[n:2 * n]
        send_sems, recv_sems = refs[2 * n:]
        x, y, c = _mesh_pos()
        copies = []
        for a in range(n):
            for k in range(N_CHIP):
                cp = pltpu.make_async_remote_copy(
                    src_ref=ins[a].at[(1 - c) + 2 * k], dst_ref=outs[a].at[k], send_sem=send_sems.at[a * N_CHIP + k],
                    recv_sem=recv_sems.at[a * N_CHIP + k], device_id=(x, y, 1 - c), device_id_type=MESH)
                cp.start()
                copies.append(cp)
        for cp in copies:
            cp.wait()

    return pl.pallas_call(
        body, name=name, in_specs=[_HBM] * n, out_specs=[_HBM] * n,
        out_shape=[jax.ShapeDtypeStruct((N_CHIP,) + g.shape[1:], g.dtype) for g in grads],
        scratch_shapes=[pltpu.SemaphoreType.DMA((N_CHIP * n,)), pltpu.SemaphoreType.DMA((N_CHIP * n,))],
    )(*grads)


_SEM = pl.BlockSpec(memory_space=pltpu.SEMAPHORE)
_IN_HBM = pl.BlockSpec(memory_space=pltpu.HBM)
_EFFECT = pltpu.SideEffectType.DATAFLOW_SIDE_EFFECTING


def _in_hbm(a):
    return pltpu.with_memory_space_constraint(a, pltpu.HBM)


def _gather_copies(ins, lands, send_sems, recv_sems):
    x, y, c = _mesh_pos()
    my_slot = 4 * x + 2 * y + c
    pairs = []
    for a in range(len(ins)):
        for k, (peer, peer_slot) in enumerate(_all_peers(x, y, c)):
            sems = dict(send_sem=send_sems.at[a * (N_DEV - 1) + k], recv_sem=recv_sems.at[a * (N_DEV - 1) + k],
                        device_id=peer, device_id_type=MESH)
            pairs.append((pltpu.make_async_remote_copy(src_ref=ins[a], dst_ref=lands[a].at[my_slot], **sems),
                          pltpu.make_async_remote_copy(src_ref=ins[a], dst_ref=lands[a].at[peer_slot], **sems)))
    return pairs


def _scatter_copies(ins, lands, send_sems, recv_sems):
    x, y, c = _mesh_pos()
    my_chip = 2 * x + y
    pairs = []
    for a in range(len(ins)):
        for j, chip in enumerate(_other_chips(x, y)):
            there = 2 * chip[0] + chip[1]
            sems = dict(send_sem=send_sems.at[a * 3 + j], recv_sem=recv_sems.at[a * 3 + j],
                        device_id=(*chip, c), device_id_type=MESH)
            pairs.append((pltpu.make_async_remote_copy(src_ref=ins[a].at[there], dst_ref=lands[a].at[my_chip], **sems),
                          pltpu.make_async_remote_copy(src_ref=ins[a].at[my_chip], dst_ref=lands[a].at[there], **sems)))
    return pairs


def _split_start(srcs, lands, copies, per_array, name):
    n = len(srcs)

    def body(*refs):
        ins, land_refs = refs[:n], refs[n:2 * n]
        send_sems, recv_sems = refs[2 * n], refs[2 * n + 1]
        token = refs[-1]
        for send, _ in copies(ins, land_refs, send_sems, recv_sems):
            send.start()
        token[...] = jnp.zeros_like(token)

    outs = pl.pallas_call(
        body, name=name,
        out_shape=(pltpu.SemaphoreType.DMA((per_array * n,)), pltpu.SemaphoreType.DMA((per_array * n,)),
                   *[pltpu.HBM(s.shape, s.dtype) for s in srcs], *[pltpu.HBM(l.shape, l.dtype) for l in lands],
                   jax.ShapeDtypeStruct((SUBLANES, LANES), F32)),
        in_specs=[_IN_HBM] * (2 * n),
        out_specs=(_SEM, _SEM, *[_IN_HBM] * (2 * n), pl.BlockSpec(memory_space=pltpu.VMEM)),
        input_output_aliases={i: 2 + i for i in range(2 * n)},
        compiler_params=pltpu.CompilerParams(has_side_effects=_EFFECT),
    )(*[_in_hbm(s) for s in srcs], *[_in_hbm(l) for l in lands])
    return outs[0], outs[1], list(outs[2:2 + n]), list(outs[2 + n:2 + 2 * n]), outs[-1]


def _split_wait(started, copies, after, name):
    send_sems, recv_sems, srcs, lands, _ = started
    n = len(srcs)

    def body(*refs):
        ins, land_refs = refs[:n], refs[n:2 * n]
        for send, recv in copies(ins, land_refs, refs[2 * n], refs[2 * n + 1]):
            send.wait_send()
            recv.wait_recv()

    outs = pl.pallas_call(
        body, name=name,
        out_shape=(*[pltpu.HBM(s.shape, s.dtype) for s in srcs], *[pltpu.HBM(l.shape, l.dtype) for l in lands]),
        in_specs=[_IN_HBM] * (2 * n) + [_SEM, _SEM, _HBM],
        out_specs=[_IN_HBM] * (2 * n),
        input_output_aliases={i: i for i in range(2 * n)},
        compiler_params=pltpu.CompilerParams(has_side_effects=_EFFECT),
    )(*srcs, *lands, send_sems, recv_sems, after)
    return list(outs[:n]), list(outs[n:])


def _ew_block(rows, cols, slots):
    budget = 2 * 1024 * 1024
    br, bc = rows, cols
    while slots * br * bc * 4 > budget:
        if br % 2 == 0 and (br // 2) % (2 * SUBLANES) == 0:
            br //= 2
        elif bc % 2 == 0 and (bc // 2) % LANES == 0:
            bc //= 2
        else:
            break
    return br, bc


def _add_sibling(grads, recv, c_idx, name):
    _, rows, cols = grads.shape
    br, bc = _ew_block(rows, cols, 3)

    def body(c_ref, g_ref, r_ref, out_ref):
        del c_ref
        out_ref[...] = (g_ref[...].astype(F32) + r_ref[...].astype(F32)).astype(out_ref.dtype)

    grid_spec = pltpu.PrefetchScalarGridSpec(
        num_scalar_prefetch=1, grid=(N_CHIP, rows // br, cols // bc),
        in_specs=[pl.BlockSpec((1, br, bc), lambda k, i, j, c_ref: (c_ref[0] + 2 * k, i, j)),
                  pl.BlockSpec((1, br, bc), lambda k, i, j, c_ref: (k, i, j))],
        out_specs=pl.BlockSpec((1, br, bc), lambda k, i, j, c_ref: (k, i, j)))
    return pl.pallas_call(
        body, name=name, grid_spec=grid_spec, out_shape=jax.ShapeDtypeStruct((N_CHIP, rows, cols), grads.dtype),
        compiler_params=_params(("parallel", "parallel", "parallel"), 3 * _nbytes((br, bc), F32)),
    )(c_idx, grads, recv)


def _adam_math(g, w, m, v):
    m2 = ADAM_B1 * m + (1.0 - ADAM_B1) * g
    v2 = ADAM_B2 * v + (1.0 - ADAM_B2) * (g * g)
    m_hat = m2 * (1.0 / (1.0 - ADAM_B1 ** ADAM_STEP))
    v_hat = v2 * (1.0 / (1.0 - ADAM_B2 ** ADAM_STEP))
    return -ADAM_LR * (m_hat / (jnp.sqrt(v_hat) + ADAM_EPS) + ADAM_WD * w), m2, v2


def _adamw(slots, w, m, v, name, own=None, own_slot=None):
    ns, rows, cols = slots.shape
    br, bc = _ew_block(rows, cols, 2 * ns + 7)

    def update(g, w_ref, m_ref, v_ref, g_ref, d_ref, m2_ref, v2_ref):
        g_ref[...] = g
        d_ref[...], m2_ref[...], v2_ref[...] = _adam_math(g, w_ref[...], m_ref[...], v_ref[...])

    out_shape = [jax.ShapeDtypeStruct((rows, cols), F32)] * 4
    params = _params(("parallel", "parallel"), (2 * ns + 7) * _nbytes((br, bc), F32))
    grid = (rows // br, cols // bc)
    if own is None:
        def body(s_ref, *rest):
            g = s_ref[0].astype(F32)
            for k in range(1, ns):
                g = g + s_ref[k].astype(F32)
            update(g, *rest)

        blk = pl.BlockSpec((br, bc), lambda i, j: (i, j))
        return pl.pallas_call(
            body, name=name, grid=grid,
            in_specs=[pl.BlockSpec((ns, br, bc), lambda i, j: (0, i, j)), blk, blk, blk], out_specs=[blk] * 4,
            out_shape=out_shape, compiler_params=params,
        )(slots, w, m, v)

    def body_own(slot_ref, s_ref, o_ref, *rest):
        g = None
        for k in range(ns):
            term = jnp.where(slot_ref[0] == k, o_ref[k].astype(F32), s_ref[k].astype(F32))
            g = term if g is None else g + term
        update(g, *rest)

    blk = pl.BlockSpec((br, bc), lambda i, j, slot_ref: (i, j))
    stack = pl.BlockSpec((ns, br, bc), lambda i, j, slot_ref: (0, i, j))
    grid_spec = pltpu.PrefetchScalarGridSpec(num_scalar_prefetch=1, grid=grid, in_specs=[stack, stack, blk, blk, blk],
                                             out_specs=[blk] * 4)
    return pl.pallas_call(body_own, name=name, grid_spec=grid_spec, out_shape=out_shape, compiler_params=params,
                          )(own_slot, slots, own, w, m, v)


def _adamw_small(all_g, last_g, params, extra_shapes, name):
    names = [n for n, _ in _SMALL_ROWS if n in params]
    extras = [n for n, _ in _SMALL_ROWS if n not in params]
    offs = _small_offsets()
    n_p = len(names)

    def body(*refs):
        s_ref, last_ref = refs[0], refs[1]
        wmv = refs[2:2 + 3 * n_p]
        outs = refs[2 + 3 * n_p:2 + 7 * n_p]
        extra_refs = refs[2 + 7 * n_p:2 + 7 * n_p + len(extras)]
        summed = refs[-1]
        g, g_last = s_ref[0], last_ref[0]
        for k in range(1, N_DEV):
            g, g_last = g + s_ref[k], g_last + last_ref[k]
        summed[...] = g
        last_offs = _small_offsets(_LAST_SMALL)
        for n, rows in _LAST_SMALL:
            summed[offs[n]:offs[n] + rows, :] = g_last[last_offs[n]:last_offs[n] + rows, :]
        for i, n in enumerate(names):
            w_ref, m_ref, v_ref = wmv[3 * i:3 * i + 3]
            g_ref, d_ref, m2_ref, v2_ref = outs[4 * i:4 * i + 4]
            _rows_to(summed, offs[n], g_ref)
            d_ref[...], m2_ref[...], v2_ref[...] = _adam_math(g_ref[...], w_ref[...], m_ref[...], v_ref[...])
        for n, ref in zip(extras, extra_refs):
            _rows_to(summed, offs[n], ref)

    flat = [a for n in names for a in params[n]]
    out_shape = [jax.ShapeDtypeStruct(params[n][0].shape, F32) for n in names for _ in range(4)]
    out_shape += [jax.ShapeDtypeStruct(s, F32) for s in extra_shapes]
    vmem = pl.BlockSpec(memory_space=pltpu.VMEM)
    res = pl.pallas_call(
        body, name=name, in_specs=[vmem] * (2 + len(flat)), out_specs=[vmem] * len(out_shape), out_shape=out_shape,
        scratch_shapes=[pltpu.VMEM(all_g.shape[1:], F32)],
        compiler_params=pltpu.CompilerParams(vmem_limit_bytes=_vmem_limit(_nbytes(all_g.shape, F32))),
    )(all_g, last_g, *flat)
    return {n: res[4 * i:4 * i + 4] for i, n in enumerate(names)}, res[4 * n_p:]


def _mm_tiles(mode, m, n, k):
    tn = min(n, 1024)
    if mode == "tn":
        return min(m, 1024), tn, min(k, 4096)
    if k <= 1024:
        return min(m, 2048), tn, k
    if k <= 2048:
        return min(m, 1024), tn, k
    if k <= 4096:
        return min(m, 512), tn, k
    return min(m, 1024), tn, 2048


def _local_step(x, target, wts, small, exchange):
    t = x.shape[0]
    w_main_t, w_dt_t = wts["w_main_t"], wts["w_dt_t"]
    bsp_t = small["b_spatial"].T
    pad32 = lambda a: jnp.pad(a, ((0, 0), (0, DT_PAD - N_HEADS)))
    dtb, alog = pad32(small["dt_bias"]), pad32(small["a_log"])
    dskip_full = jnp.repeat(small["d_skip"], HEAD_DIM, axis=1)
    head_of_col = lax.broadcasted_iota(jnp.int32, (DT_PAD, D_INNER), 1) // HEAD_DIM
    e_bf = (head_of_col == lax.broadcasted_iota(jnp.int32, (DT_PAD, D_INNER), 0)).astype(BF16)

    def mm(a, b, mode, name, **kw):
        if mode == "nn":
            m, k, n = a.shape[0], a.shape[1], b.shape[1]
        elif mode == "nt":
            m, k, n = a.shape[0], a.shape[1], b.shape[0]
        else:
            m, k, n = a.shape[1], a.shape[0], b.shape[1]
        tm, tn, tk = _mm_tiles(mode, m, n, k)
        tm = min(tm, kw.pop("max_tm", tm))
        kw.setdefault("out_dtypes", (BF16,) if mode == "tn" else (F32,))
        if "extra_specs" in kw:
            kw["extra_specs"] = kw["extra_specs"](tm, tn)
        return _matmul(a, b, mode=mode, tm=tm, tn=tn, tk=tk, name=name, **kw)

    def out_tile(tm, tn):
        return (((tm, tn), lambda i, j: (i, j)),)

    def row_tiles(n_tiles, *vectors, gate_logits=False):
        def specs(tm, tn):
            out = [((tm, tn), lambda i, j: (i, j))] * n_tiles
            if gate_logits:
                out += [((tm, D_MODEL), lambda i, j, cb=COL_GATE // D_MODEL + half: (i, cb)) for half in range(2)]
            return tuple(out) + tuple(((1, w), lambda i, j, cb=cb: (0, cb)) for w, cb in vectors)
        return specs

    vec = lambda w: ((1, w), F32, (1, w), lambda i, j: (0, 0))
    fused_tm = 512

    h = _rms_fwd(x, small["norm_mix_g"], "rms_mix", deps=exchange.begin())
    proj = mm(h, w_main_t, "nt", "proj_main", j_outer=True, out_dtypes=(BF16,))
    dt_raw = mm(h, w_dt_t, "nt", "proj_dt")
    y_a = _gmlp_fwd(proj, small["v_norm_g"], small["v_norm_b"], small["w_spatial"], bsp_t, "gmlp_fwd")
    pre_conv, xc = _conv_fwd(proj, wts["conv_w"], small["conv_b"], "conv_fwd")
    y_ssd, y_b, sprev = _ssd_fwd(xc, proj, dt_raw, dtb, alog, dskip_full, small["ssm_norm_g"], e_bf, "ssd_fwd")
    wts = {**wts, **exchange.late_weights(y_b)}
    pa = mm(y_a, wts["w_proj_a"], "nn", "proj_a")
    pb, merged = mm(y_b, wts["w_proj_b"], "nn", "proj_b", epilogue=_merge_epilogue, out_dtypes=(F32, BF16), max_tm=fused_tm,
                    extras=(pa, proj, proj, small["b_gates"], small["b_gates"]),
                    extra_specs=row_tiles(1, (D_MODEL, 0), (D_MODEL, 1), gate_logits=True))
    x1, h2 = mm(merged, wts["w_out"], "nn", "out_proj", epilogue=_residual_rms_epilogue, out_dtypes=(F32, BF16),
                max_tm=2 * fused_tm, extras=(x, small["norm_mlp_g"]), extra_specs=row_tiles(1, (D_MODEL, 0)))

    def relu_sq(acc, ex, outs):
        r = jnp.maximum(acc, 0.0)
        outs[0][...] = (r * r).astype(BF16)

    act = mm(h2, wts["w_mlp_up"], "nn", "mlp_up", epilogue=relu_sq, out_dtypes=(BF16,), j_outer=True)
    dx2, dx2_b, g_final, _, loss = mm(
        act, wts["w_mlp_down"], "nn", "mlp_down", epilogue=_loss_epilogue, carry=True,
        out_dtypes=(F32, BF16, vec(D_MODEL), vec(D_MODEL), vec(LANES)),
        extras=(x1, small["norm_final_g"], target), extra_specs=lambda tm, tn: (
            ((tm, tn), lambda i, j: (i, j)), ((1, tn), lambda i, j: (0, 0)), ((tm, tn), lambda i, j: (i, j))))

    def relu_sq_bwd(acc, ex, outs):
        outs[0][...] = (acc * 2.0 * jnp.sqrt(ex[0][...].astype(F32))).astype(BF16)

    dup = mm(dx2_b, wts["w_mlp_down"], "nt", "d_act", epilogue=relu_sq_bwd, extras=(act,), extra_specs=out_tile,
             out_dtypes=(BF16,), j_outer=True)
    g_down = mm(act, dx2_b, "tn", "g_mlp_down")
    g_up = mm(h2, dup, "tn", "g_mlp_up")
    started = exchange.reduce("mlp", {"w_mlp_down": g_down, "w_mlp_up": g_up})
    dx1, dx1_b, g_mlp = mm(
        dup, wts["w_mlp_up"], "nt", "d_h2", deps=started, epilogue=_rms_bwd_epilogue, carry=True,
        out_dtypes=(F32, BF16, vec(D_MODEL)), extras=(x1, small["norm_mlp_g"], dx2), extra_specs=lambda tm, tn: (
            ((tm, tn), lambda i, j: (i, j)), ((1, tn), lambda i, j: (0, 0)), ((tm, tn), lambda i, j: (i, j))))

    g_out = mm(merged, dx1_b, "tn", "g_out")
    dpa, dpb, dproj, g_bgates = mm(
        dx1_b, wts["w_out"], "nt", "d_merged", epilogue=_merge_bwd_epilogue, carry=True, max_tm=fused_tm,
        out_dtypes=(BF16, BF16, ((t, MAIN_W), BF16, (fused_tm, 2 * D_MODEL), lambda i, j: (i, COL_GATE // (2 * D_MODEL))),
                    vec(2 * D_MODEL)),
        extras=(pa, pb, proj, proj, small["b_gates"], small["b_gates"]),
        extra_specs=row_tiles(2, (D_MODEL, 0), (D_MODEL, 1), gate_logits=True))
    g_pa = mm(y_a, dpa, "tn", "g_proj_a")
    g_pb = mm(y_b, dpb, "tn", "g_proj_b")
    started = exchange.reduce("proj", {"w_out": g_out, "w_proj_a": g_pa, "w_proj_b": g_pb})
    dya = mm(dpa, wts["w_proj_a"], "nt", "d_ya", deps=started)
    dyb = mm(dpb, wts["w_proj_b"], "nt", "d_yb")

    dproj, g_wsp, g_bsp_t, g_vg, g_vb = _gmlp_bwd(proj, dya, small["v_norm_g"], small["v_norm_b"], small["w_spatial"],
                                                   bsp_t, dproj, "gmlp_bwd")
    dproj, dxc, ddt, g_ng, g_dskip, g_alog, g_dtb = _ssd_bwd(dyb, y_ssd, xc, proj, dt_raw, sprev, dtb, alog, dskip_full,
                                                             small["ssm_norm_g"], e_bf, dproj, "ssd_bwd")
    dproj, g_convw, g_convb = _conv_bwd(proj, pre_conv, dxc, wts["conv_w"], dproj, "conv_bwd")

    small_grads = {
        "conv_w": g_convw, "loss": loss,
        "conv_b": g_convb, "dt_bias": g_dtb, "a_log": g_alog, "d_skip": g_dskip, "ssm_norm_g": g_ng,
        "v_norm_g": g_vg, "v_norm_b": g_vb, "w_spatial": g_wsp.reshape(GROUPS * CHUNK, CHUNK), "b_spatial": g_bsp_t.T,
        "b_gates": g_bgates, "norm_mlp_g": g_mlp, "norm_final_g": g_final,
    }
    g_main_t = mm(dproj, h, "tn", "g_in_main", deps=exchange.small(small_grads))
    g_dt_t = mm(ddt, h, "tn", "g_in_dt")
    started = exchange.reduce("in", {"w_in": _join_w_in(g_main_t, g_dt_t)})

    def input_grad(acc, ex, outs):
        x_ref, g_ref, res_ref, ddt_ref, wdt_ref = ex
        gg = jnp.zeros((1, D_MODEL), F32)
        for r in range(acc.shape[0] // ROW_TILE):
            rows = slice(r * ROW_TILE, (r + 1) * ROW_TILE)
            dh = acc[rows] + _dot(ddt_ref[rows, :], wdt_ref[...], _NN)
            dx, gg_r = _rms_pullback(x_ref[rows, :], g_ref[...], dh)
            outs[0][rows, :] = dx + res_ref[rows, :]
            gg = gg + gg_r

        @pl.when(_first_row_tile())
        def _():
            outs[1][...] = jnp.zeros_like(outs[1])

        outs[1][...] += gg

    grad_x, g_mix = mm(
        dproj, w_main_t, "nn", "d_h", epilogue=input_grad, deps=started, carry=True,
        out_dtypes=(F32, vec(D_MODEL)), extras=(x, small["norm_mix_g"], dx1, ddt, w_dt_t), extra_specs=lambda tm, tn: (
            ((tm, tn), lambda i, j: (i, j)), ((1, tn), lambda i, j: (0, 0)), ((tm, tn), lambda i, j: (i, j)),
            ((tm, DT_PAD), lambda i, j: (i, 0)), ((DT_PAD, D_MODEL), lambda i, j: (0, 0))))

    return grad_x, g_mix


def _split_w_in(w_full_t):
    dt0 = COL_GATE
    w_main_t = jnp.concatenate([w_full_t[:dt0], w_full_t[dt0 + N_HEADS:]], axis=0)
    w_dt_t = jnp.pad(w_full_t[dt0:dt0 + N_HEADS], ((0, DT_PAD - N_HEADS), (0, 0)))
    return w_main_t, w_dt_t


def _join_w_in(g_main_t, g_dt_t):
    dt0 = COL_GATE
    return jnp.concatenate([g_main_t[:dt0], g_dt_t[:N_HEADS], g_main_t[dt0:]], axis=0)


_LATE = ["w_proj_a", "w_proj_b", "w_out", "w_mlp_up", "w_mlp_down"]
_BY_COLS = ("w_mlp_up",)


class _Exchange:
    def __init__(self, late_shards, late_lands):
        self.late_shards, self.late_lands = late_shards, late_lands
        self.c_idx = lax.axis_index("c").astype(jnp.int32).reshape(1)
        self.chip_idx = (2 * lax.axis_index("x") + lax.axis_index("y")).astype(jnp.int32).reshape(1)
        self.pending = []

    def begin(self):
        self.late = _split_start(self.late_shards, self.late_lands, _gather_copies, N_DEV - 1, "gather_late_start")
        return [self.late[-1]]

    def late_weights(self, after):
        _, lands = _split_wait(self.late, _gather_copies, after, "gather_late_wait")
        whole = {}
        for n, g in zip(_LATE, lands):
            whole[n] = jnp.transpose(g, (1, 0, 2)).reshape(g.shape[1], -1) if n in _BY_COLS else g.reshape(-1, g.shape[2])
        return whole

    def reduce(self, tag, grads):
        names = list(grads)
        by_dev = []
        for n in names:
            g = grads[n]
            if n in _BY_COLS:
                by_dev.append(jnp.transpose(g.reshape(g.shape[0], N_DEV, -1), (1, 0, 2)))
            else:
                by_dev.append(g.reshape(N_DEV, -1, g.shape[1]))
        from_sibling = _swap_with_sibling(by_dev, "reduce_cores_" + tag)
        parts = [_add_sibling(g, r, self.c_idx, "add_cores_" + n) for n, g, r in zip(names, by_dev, from_sibling)]
        lands = [lax.empty(p.shape, p.dtype) for p in parts]
        started = _split_start(parts, lands, _scatter_copies, 3, "reduce_chips_start_" + tag)
        self.pending.append((tag, names, started))
        return [started[-1]]

    def small(self, grads):
        dev = 2 * self.chip_idx + self.c_idx
        packed, land = _pack_small(grads, dev, "pack_small")
        self.small_started = _split_start([packed], [land], _gather_copies, N_DEV - 1, "exchange_small_start")
        return [self.small_started[-1]]

    def finish(self, after):
        _, (all_small,) = _split_wait(self.small_started, _gather_copies, after, "exchange_small_wait")
        done = {}
        for tag, names, started in self.pending:
            parts, lands = _split_wait(started, _scatter_copies, after, "reduce_chips_wait_" + tag)
            for n, land, part in zip(names, lands, parts):
                done[n] = (land, part, self.chip_idx)
        return all_small, done


def kernel(x, norm_mix_g, w_in, conv_w, conv_b, dt_bias, a_log, d_skip, ssm_norm_g, v_norm_g, v_norm_b, w_spatial, b_spatial, b_gates, w_proj_a, w_proj_b, w_out, norm_mlp_g, w_mlp_up, w_mlp_down, norm_final_g, loss_target, m_norm_mix_g, m_w_in, m_conv_w, m_conv_b, m_dt_bias, m_a_log, m_d_skip, m_ssm_norm_g, m_v_norm_g, m_v_norm_b, m_w_spatial, m_b_spatial, m_b_gates, m_w_proj_a, m_w_proj_b, m_w_out, m_norm_mlp_g, m_w_mlp_up, m_w_mlp_down, m_norm_final_g, v_norm_mix_g, v_w_in, v_conv_w, v_conv_b, v_dt_bias, v_a_log, v_d_skip, v_ssm_norm_g, v_v_norm_g, v_v_norm_b, v_w_spatial, v_b_spatial, v_b_gates, v_w_proj_a, v_w_proj_b, v_w_out, v_norm_mlp_g, v_w_mlp_up, v_w_mlp_down, v_norm_final_g):
    given = dict(locals())
    names = ["norm_mix_g", "w_in", "conv_w", "conv_b", "dt_bias", "a_log", "d_skip", "ssm_norm_g", "v_norm_g", "v_norm_b",
             "w_spatial", "b_spatial", "b_gates", "w_proj_a", "w_proj_b", "w_out", "norm_mlp_g", "w_mlp_up", "w_mlp_down",
             "norm_final_g"]
    shapes = {n: given[n].shape for n in names}
    dev = 4 * lax.axis_index("x") + 2 * lax.axis_index("y") + lax.axis_index("c")

    shard2d = {"w_in": w_in[0].T, "w_proj_a": w_proj_a[0], "w_proj_b": w_proj_b[0], "w_out": w_out[0],
               "w_mlp_up": w_mlp_up[0], "w_mlp_down": w_mlp_down[0]}
    conv_shard = conv_w.reshape(CONV_WIDTH, -1)
    late_shards = [shard2d[n].astype(BF16) for n in _LATE]
    w_in_all, conv_all, *late_lands = _all_gather([shard2d["w_in"].astype(BF16), conv_shard], "gather_first",
                                                  own_only=late_shards)
    w_main_t, w_dt_t = _split_w_in(w_in_all.reshape(-1, D_MODEL))
    wts = {"w_main_t": w_main_t, "w_dt_t": w_dt_t, "conv_w": jnp.transpose(conv_all, (1, 0, 2)).reshape(CONV_WIDTH, -1)}
    small = {"norm_mix_g": norm_mix_g, "conv_b": conv_b, "dt_bias": dt_bias, "a_log": a_log, "d_skip": d_skip,
             "ssm_norm_g": ssm_norm_g, "v_norm_g": v_norm_g, "v_norm_b": v_norm_b, "w_spatial": w_spatial[0],
             "b_spatial": b_spatial[0], "b_gates": b_gates, "norm_mlp_g": norm_mlp_g,
             "norm_final_g": norm_final_g.reshape(1, -1)}

    exchange = _Exchange(late_shards, late_lands)
    grad_x, g_mix = _local_step(x[0], loss_target[0], wts, small, exchange)

    out = {}
    all_small, large = exchange.finish(grad_x)
    for n, (slots, own, own_slot) in large.items():
        moments = [given["m_" + n][0], given["v_" + n][0]]
        if n == "w_in":
            moments = [mom.T for mom in moments]
        res = _adamw(slots, shard2d[n], *moments, "adamw_" + n, own=own, own_slot=own_slot)
        out[n] = [(r.T if n == "w_in" else r).reshape(shapes[n]) for r in res]

    last_small = _exchange_small({"norm_mix_g": g_mix}, _LAST_SMALL, "exchange_last")
    small["w_spatial"] = small["w_spatial"].reshape(GROUPS * CHUNK, CHUNK)
    params = {n: (w2d, given["m_" + n].reshape(w2d.shape), given["v_" + n].reshape(w2d.shape)) for n, w2d in small.items()}
    updated, (g_conv_full, loss_all) = _adamw_small(all_small, last_small, params, [(CONV_WIDTH, CONV_DIM), (1, LANES)],
                                                    "adamw_small")
    for n, res in updated.items():
        out[n] = [r.reshape(shapes[n]) for r in res]
    width = shapes["conv_w"][-1]
    g_conv = lax.dynamic_slice(g_conv_full, (0, dev * width), (CONV_WIDTH, width))
    res = _adamw(g_conv[None], conv_shard, m_conv_w.reshape(CONV_WIDTH, -1), v_conv_w.reshape(CONV_WIDTH, -1), "adamw_conv_w")
    out["conv_w"] = [r.reshape(shapes["conv_w"]) for r in res]

    loss = loss_all[0, 0]
    return (loss, grad_x[None], *[out[n][0] for n in names], *[out[n][1] for n in names],
            *[out[n][2] for n in names], *[out[n][3] for n in names])
```

```python
import functools
import math

import jax
import jax.numpy as jnp
from jax import lax
from jax.experimental import pallas as pl
from jax.experimental.pallas import tpu as pltpu

F32 = jnp.float32
BF16 = jnp.bfloat16
MESH = pl.DeviceIdType.MESH

D_MODEL = 1024
NORM_EPS = 1e-6
CHUNK = 128
GROUPS = 8
D_INNER = 2048
HEAD_DIM = 64
N_HEADS = 32
D_STATE = 128
CONV_WIDTH = 4
CONV_DIM = 4096
D_FF = 4096
GROUP_W = D_INNER // GROUPS
N_DEV = 8
N_CHIP = 4

ADAM_LR = 0.001
ADAM_B1 = 0.9
ADAM_B2 = 0.999
ADAM_EPS = 1e-08
ADAM_WD = 0.01
ADAM_STEP = 10

MAIN_W = 2 * D_MODEL + D_INNER + CONV_DIM + 2 * D_MODEL
COL_Z = 2048
COL_XBC = 4096
COL_GATE = 8192
DT_PAD = 128

LANES = 128
SUBLANES = 8
VMEM_BYTES_V7X = 64 * 1024 * 1024
VMEM_BODY_TEMP = 24 * 1024 * 1024


def _vmem_limit(block_bytes):
    return int(min(2 * block_bytes + VMEM_BODY_TEMP, VMEM_BYTES_V7X - 8 * 1024 * 1024))


def _nbytes(shape, dtype):
    return math.prod(shape) * jnp.dtype(dtype).itemsize


_HBM = pl.BlockSpec(memory_space=pl.ANY)


def _params(sem, block_bytes):
    return pltpu.CompilerParams(dimension_semantics=sem, vmem_limit_bytes=_vmem_limit(block_bytes))


def _sigmoid(x):
    return 1.0 / (1.0 + jnp.exp(-x))


def _softplus(x):
    e = jnp.exp(-jnp.abs(x))
    u = 1.0 + e
    log1p_e = jnp.where(u == 1.0, e, jnp.log(u) * (e / jnp.where(u == 1.0, 1.0, u - 1.0)))
    return jnp.maximum(x, 0.0) + log1p_e


_SQRT_HALF = 0.7071067811865476
_INV_SQRT_2PI = 0.3989422804014327


def _gelu(x):
    return x * (lax.erf(x * _SQRT_HALF) + 1.0) * 0.5


def _gelu_grad(x):
    return 0.5 * (1.0 + lax.erf(x * _SQRT_HALF)) + x * jnp.exp(-0.5 * x * x) * _INV_SQRT_2PI


def _dot(a, b, dims):
    return lax.dot_general(a, b, (dims, ((), ())), preferred_element_type=F32)


_NN = ((1,), (0,))
_NT = ((1,), (1,))
_TN = ((0,), (0,))


def _split3(x):
    hi = x.astype(BF16)
    r1 = x - hi.astype(F32)
    mid = r1.astype(BF16)
    lo = (r1 - mid.astype(F32)).astype(BF16)
    return hi, mid, lo


def _dot_exact_rhs(x, e, dims):
    hi, mid, lo = _split3(x)
    return _dot(hi, e, dims) + _dot(mid, e, dims) + _dot(lo, e, dims)


def _dot_exact_lhs(e, x, dims):
    hi, mid, lo = _split3(x)
    return _dot(e, hi, dims) + _dot(e, mid, dims) + _dot(e, lo, dims)


def _tri(lower):
    r = lax.broadcasted_iota(jnp.int32, (CHUNK, CHUNK), 0)
    c = lax.broadcasted_iota(jnp.int32, (CHUNK, CHUNK), 1)
    return (r >= c) if lower else (r <= c)


def _matmul(a, b, *, mode, tm, tn, tk, out_dtypes, name, epilogue=None, extras=(), extra_specs=(), j_outer=False, deps=(),
            carry=False):
    if mode == "nn":
        (m, k), (_, n) = a.shape, b.shape
    elif mode == "nt":
        (m, k), (n, _) = a.shape, b.shape
    else:
        (k, m), (_, n) = a.shape, b.shape
    assert m % tm == 0 and n % tn == 0 and k % tk == 0, (name, m, n, k, tm, tn, tk)
    nk = k // tk
    n_extra, n_out = len(extras), len(out_dtypes)
    first_out = 2 + n_extra + len(deps)
    dims = {"nn": _NN, "nt": _NT, "tn": _TN}[mode]
    if epilogue is None:
        def epilogue(acc, ex, outs):
            outs[0][...] = acc.astype(outs[0].dtype)

    def body(*refs):
        a_ref, b_ref = refs[0], refs[1]
        ex_refs = refs[2:2 + n_extra]
        outs = refs[first_out:first_out + n_out]
        p = _dot(a_ref[...], b_ref[...], dims)
        if nk == 1:
            epilogue(p, ex_refs, outs)
        else:
            acc_ref = refs[first_out + n_out]
            kk = pl.program_id(2)

            @pl.when(kk == 0)
            def _():
                acc_ref[...] = p

            @pl.when(kk > 0)
            def _():
                acc_ref[...] += p

            @pl.when(kk == nk - 1)
            def _():
                epilogue(acc_ref[...], ex_refs, outs)

    if j_outer:
        grid = (n // tn, m // tm, nk)
        ij = lambda g0, g1: (g1, g0)
    else:
        grid = (m // tm, n // tn, nk)
        ij = lambda g0, g1: (g0, g1)

    def wrap(fn):
        return lambda g0, g1, kk: fn(*ij(g0, g1), kk)

    if mode == "nn":
        a_spec = pl.BlockSpec((tm, tk), wrap(lambda i, j, kk: (i, kk)))
        b_spec = pl.BlockSpec((tk, tn), wrap(lambda i, j, kk: (kk, j)))
        a_blk, b_blk = (tm, tk), (tk, tn)
    elif mode == "nt":
        a_spec = pl.BlockSpec((tm, tk), wrap(lambda i, j, kk: (i, kk)))
        b_spec = pl.BlockSpec((tn, tk), wrap(lambda i, j, kk: (j, kk)))
        a_blk, b_blk = (tm, tk), (tn, tk)
    else:
        a_spec = pl.BlockSpec((tk, tm), wrap(lambda i, j, kk: (kk, i)))
        b_spec = pl.BlockSpec((tk, tn), wrap(lambda i, j, kk: (kk, j)))
        a_blk, b_blk = (tk, tm), (tk, tn)
    ex_specs = [pl.BlockSpec(shape, wrap(lambda i, j, kk, f=f: f(i, j))) for shape, f in extra_specs]
    outs = [o if isinstance(o, tuple) else ((m, n), o, (tm, tn), lambda i, j: (i, j)) for o in out_dtypes]
    out_spec = [pl.BlockSpec(blk_shape, wrap(lambda i, j, kk, f=f: f(i, j))) for _, _, blk_shape, f in outs]
    out_shape = [jax.ShapeDtypeStruct(shape, dt) for shape, dt, _, _ in outs]
    blk = (_nbytes(a_blk, a.dtype) + _nbytes(b_blk, b.dtype) + sum(_nbytes(s, F32) for s, _ in extra_specs)
           + sum(_nbytes(blk_shape, dt) for _, dt, blk_shape, _ in outs) + _nbytes((tm, tn), F32))
    order = ("arbitrary",) * 3 if carry else ("parallel", "parallel", "arbitrary")
    res = pl.pallas_call(
        body, name=name, grid=grid,
        in_specs=[a_spec, b_spec] + ex_specs + [_HBM] * len(deps), out_specs=out_spec, out_shape=out_shape,
        scratch_shapes=[pltpu.VMEM((tm, tn), F32)] if nk > 1 else [],
        compiler_params=_params(order, blk),
    )(a, b, *extras, *deps)
    return res[0] if n_out == 1 else res


ROW_TILE = 256


def _row_spec(width, col_block=0, tile=ROW_TILE):
    return pl.BlockSpec((tile, width), lambda i, cb=col_block: (i, cb))


def _vec_spec(width, col_block=0):
    return pl.BlockSpec((1, width), lambda i, cb=col_block: (0, cb))


def _rms_fwd(x, g, name, deps=()):
    t = x.shape[0]

    def body(x_ref, g_ref, *rest):
        h_ref = rest[-1]
        xv = x_ref[...]
        r = lax.rsqrt(jnp.mean(xv * xv, axis=-1, keepdims=True) + NORM_EPS)
        h_ref[...] = (xv * r * g_ref[...]).astype(BF16)

    return pl.pallas_call(
        body, name=name, grid=(t // ROW_TILE,),
        in_specs=[_row_spec(D_MODEL), _vec_spec(D_MODEL)] + [_HBM] * len(deps), out_specs=_row_spec(D_MODEL),
        out_shape=jax.ShapeDtypeStruct((t, D_MODEL), BF16),
        compiler_params=_params(("parallel",), 3 * _nbytes((ROW_TILE, D_MODEL), F32)),
    )(x, g, *deps)


def _rms_scale(xv):
    r = lax.rsqrt(jnp.mean(xv * xv, axis=-1, keepdims=True) + NORM_EPS)
    return r, xv * r


def _rms_pullback(xv, g, dh):
    r, xh = _rms_scale(xv)
    dyg = dh * g
    return r * (dyg - xh * jnp.mean(dyg * xh, axis=-1, keepdims=True)), jnp.sum(dh * xh, axis=0, keepdims=True)


def _first_row_tile():
    return pl.program_id(0) == 0


def _residual_rms_epilogue(acc, ex, outs):
    x1 = acc + ex[0][...]
    outs[0][...] = x1
    _, xh = _rms_scale(x1)
    outs[1][...] = (xh * ex[1][...]).astype(BF16)


def _loss_epilogue(acc, ex, outs):
    dx_ref, dxb_ref, gg_ref, sq_ref, tot_ref = outs
    gv = ex[1][...]
    r, xh = _rms_scale(acc + ex[0][...])
    err = xh * gv - ex[2][...]
    dy = err * (1.0 / D_MODEL)
    dyg = dy * gv
    dx = r * (dyg - xh * jnp.mean(dyg * xh, axis=-1, keepdims=True))
    dx_ref[...] = dx
    dxb_ref[...] = dx.astype(BF16)

    @pl.when(_first_row_tile())
    def _():
        gg_ref[...] = jnp.zeros_like(gg_ref)
        sq_ref[...] = jnp.zeros_like(sq_ref)

    gg_ref[...] += jnp.sum(dy * xh, axis=0, keepdims=True)
    sq_ref[...] += jnp.sum(err * err, axis=0, keepdims=True)
    tot_ref[...] = jnp.broadcast_to(jnp.sum(sq_ref[...], axis=1, keepdims=True) * (0.5 / D_MODEL), tot_ref.shape)


def _rms_bwd_epilogue(dh, ex, outs):
    dx, gg = _rms_pullback(ex[0][...], ex[1][...], dh)
    dx = dx + ex[2][...]
    outs[0][...] = dx
    if len(outs) == 3:
        outs[1][...] = dx.astype(BF16)

    @pl.when(_first_row_tile())
    def _():
        outs[-1][...] = jnp.zeros_like(outs[-1])

    outs[-1][...] += gg


def _merge_epilogue(acc, ex, outs):
    outs[0][...] = acc
    ga = _sigmoid(ex[1][...].astype(F32) + ex[3][...])
    gb = _sigmoid(ex[2][...].astype(F32) + ex[4][...])
    outs[1][...] = (ga * ex[0][...] + gb * acc).astype(BF16)


def _merge_bwd_epilogue(dm, ex, outs):
    dpa_ref, dpb_ref, dgl_ref, gb_ref = outs
    ga = _sigmoid(ex[2][...].astype(F32) + ex[4][...])
    gb = _sigmoid(ex[3][...].astype(F32) + ex[5][...])
    dpa_ref[...] = (dm * ga).astype(BF16)
    dpb_ref[...] = (dm * gb).astype(BF16)
    dla = dm * ex[0][...] * ga * (1.0 - ga)
    dlb = dm * ex[1][...] * gb * (1.0 - gb)
    dgl_ref[:, :D_MODEL] = dla.astype(BF16)
    dgl_ref[:, D_MODEL:] = dlb.astype(BF16)

    @pl.when(_first_row_tile())
    def _():
        gb_ref[...] = jnp.zeros_like(gb_ref)

    gb_ref[:, :D_MODEL] += jnp.sum(dla, axis=0, keepdims=True)
    gb_ref[:, D_MODEL:] += jnp.sum(dlb, axis=0, keepdims=True)


GMLP_TILE = 512
GMLP_NC = GMLP_TILE // CHUNK


def _gmlp_common(u_pre, v_pre, vg, vb):
    u = _gelu(u_pre)
    v = _gelu(v_pre)
    mu = jnp.mean(v, axis=-1, keepdims=True)
    vc = v - mu
    rstd = lax.rsqrt(jnp.mean(vc * vc, axis=-1, keepdims=True) + NORM_EPS)
    vh = vc * rstd
    vn = vh * vg + vb
    return u, vh, vn, rstd


def _chunks_to_lanes(x, g):
    return jnp.concatenate([x[c * CHUNK:(c + 1) * CHUNK, g * CHUNK:(g + 1) * CHUNK] for c in range(GMLP_NC)], axis=1)


def _gmlp_fwd(proj, vg, vb, wsp, bsp_t, name):
    t = proj.shape[0]

    def body(u_ref, v_ref, vg_ref, vb_ref, w_ref, b_ref, ya_ref):
        u, _, vn, _ = _gmlp_common(u_ref[...].astype(F32), v_ref[...].astype(F32), vg_ref[...], vb_ref[...])
        mask = _tri(True)
        bt = b_ref[...]
        for g in range(GROUPS):
            w = jnp.where(mask, w_ref[g], 0.0).astype(BF16)
            vcat = _chunks_to_lanes(vn, g).astype(BF16)
            s = _dot(w, vcat, _NN) + bt[:, g:g + 1]
            for c in range(GMLP_NC):
                rows, cols = slice(c * CHUNK, (c + 1) * CHUNK), slice(g * CHUNK, (g + 1) * CHUNK)
                ya_ref[rows, cols] = (u[rows, cols] * s[:, c * CHUNK:(c + 1) * CHUNK]).astype(BF16)

    return pl.pallas_call(
        body, name=name, grid=(t // GMLP_TILE,),
        in_specs=[_row_spec(D_MODEL, 0, GMLP_TILE), _row_spec(D_MODEL, 1, GMLP_TILE), _vec_spec(D_MODEL),
                  _vec_spec(D_MODEL), pl.BlockSpec((GROUPS, CHUNK, CHUNK), lambda i: (0, 0, 0)),
                  pl.BlockSpec((CHUNK, GROUPS), lambda i: (0, 0))],
        out_specs=_row_spec(D_MODEL, 0, GMLP_TILE),
        out_shape=jax.ShapeDtypeStruct((t, D_MODEL), BF16),
        compiler_params=_params(("parallel",), 3 * _nbytes((GMLP_TILE, D_MODEL), F32)),
    )(proj, proj, vg, vb, wsp, bsp_t)


def _gmlp_bwd(proj, dya, vg, vb, wsp, bsp_t, dproj, name):
    t = proj.shape[0]

    def body(u_ref, v_ref, dya_ref, vg_ref, vb_ref, w_ref, b_ref, dproj_in, duv_ref, gw_ref, gbt_ref, gvg_ref, gvb_ref,
             dvn_scr, du_scr):
        del dproj_in
        u_pre, v_pre = u_ref[...].astype(F32), v_ref[...].astype(F32)
        vgv = vg_ref[...]
        u, vh, vn, rstd = _gmlp_common(u_pre, v_pre, vgv, vb_ref[...])
        dya = dya_ref[...]
        mask = _tri(True)
        bt = b_ref[...]
        first = pl.program_id(0) == 0

        @pl.when(first)
        def _():
            gw_ref[...] = jnp.zeros_like(gw_ref)
            gbt_ref[...] = jnp.zeros_like(gbt_ref)
            gvg_ref[...] = jnp.zeros_like(gvg_ref)
            gvb_ref[...] = jnp.zeros_like(gvb_ref)

        lane = lax.broadcasted_iota(jnp.int32, (CHUNK, GROUPS), 1)
        gbt = jnp.zeros((CHUNK, GROUPS), F32)
        for g in range(GROUPS):
            w = jnp.where(mask, w_ref[g], 0.0).astype(BF16)
            vcat = _chunks_to_lanes(vn, g).astype(BF16)
            s = _dot(w, vcat, _NN) + bt[:, g:g + 1]
            ds = _chunks_to_lanes(dya * u, g)
            gbt = jnp.where(lane == g, jnp.sum(ds, axis=1, keepdims=True), gbt)
            dsb = ds.astype(BF16)
            gw_ref[g] += jnp.where(mask, _dot(dsb, vcat, _NT), 0.0)
            dv = _dot(w, dsb, _TN)
            for c in range(GMLP_NC):
                rows, cols = slice(c * CHUNK, (c + 1) * CHUNK), slice(g * CHUNK, (g + 1) * CHUNK)
                dvn_scr[rows, cols] = dv[:, c * CHUNK:(c + 1) * CHUNK]
                du_scr[rows, cols] = dya[rows, cols] * s[:, c * CHUNK:(c + 1) * CHUNK]
        gbt_ref[...] += gbt
        dvn = dvn_scr[...]
        gvg_ref[...] += jnp.sum(dvn * vh, axis=0, keepdims=True)
        gvb_ref[...] += jnp.sum(dvn, axis=0, keepdims=True)
        dvh = dvn * vgv
        dv = rstd * (dvh - jnp.mean(dvh, axis=-1, keepdims=True) - vh * jnp.mean(dvh * vh, axis=-1, keepdims=True))
        duv_ref[:, :D_MODEL] = (du_scr[...] * _gelu_grad(u_pre)).astype(BF16)
        duv_ref[:, D_MODEL:] = (dv * _gelu_grad(v_pre)).astype(BF16)

    return pl.pallas_call(
        body, name=name, grid=(t // GMLP_TILE,),
        in_specs=[_row_spec(D_MODEL, 0, GMLP_TILE), _row_spec(D_MODEL, 1, GMLP_TILE), _row_spec(D_MODEL, 0, GMLP_TILE),
                  _vec_spec(D_MODEL), _vec_spec(D_MODEL), pl.BlockSpec((GROUPS, CHUNK, CHUNK), lambda i: (0, 0, 0)),
                  pl.BlockSpec((CHUNK, GROUPS), lambda i: (0, 0)), pl.BlockSpec(memory_space=pl.ANY)],
        out_specs=[_row_spec(2 * D_MODEL, 0, GMLP_TILE), pl.BlockSpec((GROUPS, CHUNK, CHUNK), lambda i: (0, 0, 0)),
                   pl.BlockSpec((CHUNK, GROUPS), lambda i: (0, 0)), _vec_spec(D_MODEL), _vec_spec(D_MODEL)],
        out_shape=[jax.ShapeDtypeStruct(dproj.shape, BF16), jax.ShapeDtypeStruct((GROUPS, CHUNK, CHUNK), F32),
                   jax.ShapeDtypeStruct((CHUNK, GROUPS), F32), jax.ShapeDtypeStruct((1, D_MODEL), F32),
                   jax.ShapeDtypeStruct((1, D_MODEL), F32)],
        scratch_shapes=[pltpu.VMEM((GMLP_TILE, D_MODEL), F32), pltpu.VMEM((GMLP_TILE, D_MODEL), F32)],
        input_output_aliases={7: 0},
        compiler_params=_params(("arbitrary",), 6 * _nbytes((GMLP_TILE, D_MODEL), F32)),
    )(proj, proj, dya, vg, vb, wsp, bsp_t, dproj)


CONV_TILE = 512
CONV_COLS = 1024
CONV_RB = 32
HALO = SUBLANES


def _conv_fwd(proj, cw, cb, name):
    t = proj.shape[0]
    nj = CONV_DIM // CONV_COLS
    xcb = COL_XBC // CONV_COLS
    before = 2 * HALO
    rb = CONV_TILE // before

    def body(x_ref, prev_ref, cw_ref, cb_ref, pre_ref, xc_ref):
        i = pl.program_id(1)
        cw_v = cw_ref[...]
        cb_v = cb_ref[...]
        for b in range(CONV_TILE // CONV_RB):
            if b == 0:
                prev = jnp.where(i > 0, prev_ref[...].astype(F32)[HALO:, :], 0.0)
                ext = jnp.concatenate([prev, x_ref[:CONV_RB, :].astype(F32)], axis=0)
            else:
                ext = x_ref[b * CONV_RB - before:(b + 1) * CONV_RB, :].astype(F32)[HALO:, :]
            pre = cb_v + cw_v[CONV_WIDTH - 1:CONV_WIDTH, :] * ext[HALO:, :]
            for k in range(CONV_WIDTH - 1):
                back = CONV_WIDTH - 1 - k
                pre = pre + cw_v[k:k + 1, :] * pltpu.roll(ext, back, 0)[HALO:, :]
            pre_ref[b * CONV_RB:(b + 1) * CONV_RB, :] = pre
            xc_ref[b * CONV_RB:(b + 1) * CONV_RB, :] = pre * _sigmoid(pre)

    tile = pl.BlockSpec((CONV_TILE, CONV_COLS), lambda j, i: (i, j))
    return pl.pallas_call(
        body, name=name, grid=(nj, t // CONV_TILE),
        in_specs=[pl.BlockSpec((CONV_TILE, CONV_COLS), lambda j, i: (i, xcb + j)),
                  pl.BlockSpec((before, CONV_COLS), lambda j, i: (jnp.maximum(i * rb - 1, 0), xcb + j)),
                  pl.BlockSpec((CONV_WIDTH, CONV_COLS), lambda j, i: (0, j)),
                  pl.BlockSpec((1, CONV_COLS), lambda j, i: (0, j))],
        out_specs=[tile, tile],
        out_shape=[jax.ShapeDtypeStruct((t, CONV_DIM), F32), jax.ShapeDtypeStruct((t, CONV_DIM), F32)],
        compiler_params=_params(("parallel", "parallel"), 4 * _nbytes((CONV_TILE, CONV_COLS), F32)),
    )(proj, proj, cw, cb)


def _fold_rows(v):
    out = v[:SUBLANES]
    for r in range(1, v.shape[0] // SUBLANES):
        out = out + v[r * SUBLANES:(r + 1) * SUBLANES]
    return out


def _conv_bwd(proj, pre, dxc, cw, dproj, name):
    t = proj.shape[0]
    nj = CONV_DIM // CONV_COLS
    ni = t // CONV_TILE
    xcb = COL_XBC // CONV_COLS
    rb = CONV_TILE // HALO
    last_rb = t // HALO - 1

    def body(x_ref, p_ref, pnext_ref, d_ref, dnext_ref, cw_ref, dproj_in, dx_ref, gw_ref, gb_ref):
        del dproj_in
        i = pl.program_id(1)
        cw_v = cw_ref[...]

        def dpre_of(p, d):
            sg = _sigmoid(p)
            return d * sg * (1.0 + p * (1.0 - sg))

        @pl.when(i == 0)
        def _():
            gw_ref[...] = jnp.zeros_like(gw_ref)
            gb_ref[...] = jnp.zeros_like(gb_ref)

        head = dpre_of(pnext_ref[...], jnp.where(i < ni - 1, dnext_ref[...], 0.0))
        gb_acc = jnp.zeros((SUBLANES, CONV_COLS), F32)
        gw_acc = [jnp.zeros((SUBLANES, CONV_COLS), F32) for _ in range(CONV_WIDTH)]
        for b in reversed(range(CONV_TILE // CONV_RB)):
            rows = slice(b * CONV_RB, (b + 1) * CONV_RB)
            cur = dpre_of(p_ref[rows, :], d_ref[rows, :])
            ext = jnp.concatenate([cur, head], axis=0)
            xv = x_ref[rows, :].astype(F32)
            dx = None
            for k in range(CONV_WIDTH):
                shift = CONV_WIDTH - 1 - k
                win = cur if shift == 0 else pltpu.roll(ext, CONV_RB + HALO - shift, 0)[:CONV_RB, :]
                term = cw_v[k:k + 1, :] * win
                dx = term if dx is None else dx + term
                gw_acc[k] = gw_acc[k] + _fold_rows(win * xv)
            dx_ref[rows, :] = dx.astype(BF16)
            gb_acc = gb_acc + _fold_rows(cur)
            head = cur[:HALO]
        gb_ref[...] += jnp.sum(gb_acc, axis=0, keepdims=True)
        for k in range(CONV_WIDTH):
            gw_ref[k:k + 1, :] += jnp.sum(gw_acc[k], axis=0, keepdims=True)

    tile = pl.BlockSpec((CONV_TILE, CONV_COLS), lambda j, i: (i, j))
    after = pl.BlockSpec((HALO, CONV_COLS), lambda j, i: (jnp.minimum((i + 1) * rb, last_rb), j))
    return pl.pallas_call(
        body, name=name, grid=(nj, ni),
        in_specs=[pl.BlockSpec((CONV_TILE, CONV_COLS), lambda j, i: (i, xcb + j)), tile, after, tile, after,
                  pl.BlockSpec((CONV_WIDTH, CONV_COLS), lambda j, i: (0, j)),
                  pl.BlockSpec(memory_space=pl.ANY)],
        out_specs=[pl.BlockSpec((CONV_TILE, CONV_COLS), lambda j, i: (i, xcb + j)),
                   pl.BlockSpec((CONV_WIDTH, CONV_COLS), lambda j, i: (0, j)),
                   pl.BlockSpec((1, CONV_COLS), lambda j, i: (0, j))],
        out_shape=[jax.ShapeDtypeStruct(dproj.shape, BF16), jax.ShapeDtypeStruct((CONV_WIDTH, CONV_DIM), F32),
                   jax.ShapeDtypeStruct((1, CONV_DIM), F32)],
        input_output_aliases={6: 0},
        compiler_params=_params(("parallel", "arbitrary"), 4 * _nbytes((CONV_TILE, CONV_COLS), F32)),
    )(proj, pre, pre, dxc, dxc, cw, dproj)


def _ssd_decays(dt_raw, dtb, alog, e_bf, tril_bf):
    dtv = _softplus(dt_raw + dtb)
    a = -jnp.exp(alog)
    cs = _dot_exact_lhs(tril_bf, dtv * a, _NN)
    cs_last = cs[CHUNK - 1:CHUNK, :]
    stack = jnp.concatenate([dtv, jnp.exp(cs), jnp.exp(cs_last - cs)], axis=0)
    full = _head_expand(stack, e_bf)
    return dtv, a, cs, full[:CHUNK], full[CHUNK:2 * CHUNK], full[2 * CHUNK:]


def _split2(x):
    hi = x.astype(BF16)
    return hi, (x - hi.astype(F32)).astype(BF16)


def _head_expand(x, e_bf):
    hi, mid = _split2(x)
    return _dot(hi, e_bf, _NN) + _dot(mid, e_bf, _NN)


def _head_sums(x, e_bf):
    hi, mid = _split2(x)
    return _dot(hi, e_bf, _NT) + _dot(mid, e_bf, _NT)


def _head_mats(cs, cs_t, cb, h, mask):
    seg = cs[:, h:h + 1] - cs_t[h:h + 1, :]
    lmat = jnp.exp(jnp.where(mask, seg, -jnp.inf))
    return lmat, cb * lmat


def _ssd_fwd(xc, proj, dt_raw, dtb, alog, dskip_full, ng, e_bf, name):
    t = xc.shape[0]
    nc = t // CHUNK
    zcb = COL_Z // D_INNER

    def body(xc_ref, z_ref, dt_ref, dtb_ref, alog_ref, dsk_ref, ng_ref, e_ref, y_ref, yb_ref, sprev_ref, s_scr):
        @pl.when(pl.program_id(0) == 0)
        def _():
            s_scr[...] = jnp.zeros_like(s_scr)

        mask = _tri(True)
        tril_bf = mask.astype(BF16)
        e_v = e_ref[...]
        _, _, cs, dt_full, ecs_full, decay_full = _ssd_decays(dt_ref[...], dtb_ref[...], alog_ref[...], e_v, tril_bf)
        cs_t = cs.T
        sprev_ref[0] = s_scr[...]
        for g in range(GROUPS):
            gc = slice(g * GROUP_W, (g + 1) * GROUP_W)
            xs = xc_ref[:, gc]
            xdt = xs * dt_full[:, gc]
            xdt_b = xdt.astype(BF16)
            xdec = (xdt * decay_full[:, gc]).astype(BF16)
            bg = xc_ref[:, D_INNER + g * D_STATE:D_INNER + (g + 1) * D_STATE].astype(BF16)
            cg = xc_ref[:, D_INNER + GROUPS * D_STATE + g * D_STATE:D_INNER + GROUPS * D_STATE + (g + 1) * D_STATE].astype(BF16)
            cb = _dot(cg, bg, _NT)
            s_prev = s_scr[:, gc]
            y_off = ecs_full[:, gc] * _dot(cg, s_prev.astype(BF16), _NN)
            s_scr[:, gc] = s_prev * ecs_full[CHUNK - 1:CHUNK, gc] + _dot(bg, xdec, _TN)
            parts = []
            for r in range(GROUP_W // HEAD_DIM):
                h = g * (GROUP_W // HEAD_DIM) + r
                _, m = _head_mats(cs, cs_t, cb, h, mask)
                parts.append(_dot(m.astype(BF16), xdt_b[:, r * HEAD_DIM:(r + 1) * HEAD_DIM], _NN))
            yg = jnp.concatenate(parts, axis=1) + y_off + dsk_ref[:, gc] * xs
            y_ref[:, gc] = yg
            zv = z_ref[:, gc].astype(F32)
            ygate = yg * (zv * _sigmoid(zv))
            rstd = lax.rsqrt(jnp.mean(ygate * ygate, axis=-1, keepdims=True) + NORM_EPS)
            yb_ref[:, gc] = (ygate * rstd * ng_ref[:, gc]).astype(BF16)

    vec = lambda w: pl.BlockSpec((1, w), lambda i: (0, 0))
    blk = _nbytes((CHUNK, CONV_DIM), F32) + 3 * _nbytes((CHUNK, D_INNER), F32) + _nbytes((D_STATE, D_INNER), F32)
    return pl.pallas_call(
        body, name=name, grid=(nc,),
        in_specs=[pl.BlockSpec((CHUNK, CONV_DIM), lambda i: (i, 0)), pl.BlockSpec((CHUNK, D_INNER), lambda i: (i, zcb)),
                  pl.BlockSpec((CHUNK, DT_PAD), lambda i: (i, 0)), vec(DT_PAD), vec(DT_PAD), vec(D_INNER), vec(D_INNER),
                  pl.BlockSpec((DT_PAD, D_INNER), lambda i: (0, 0))],
        out_specs=[pl.BlockSpec((CHUNK, D_INNER), lambda i: (i, 0)), pl.BlockSpec((CHUNK, D_INNER), lambda i: (i, 0)),
                   pl.BlockSpec((1, D_STATE, D_INNER), lambda i: (i, 0, 0))],
        out_shape=[jax.ShapeDtypeStruct((t, D_INNER), F32), jax.ShapeDtypeStruct((t, D_INNER), BF16),
                   jax.ShapeDtypeStruct((nc, D_STATE, D_INNER), F32)],
        scratch_shapes=[pltpu.VMEM((D_STATE, D_INNER), F32)],
        compiler_params=_params(("arbitrary",), blk),
    )(xc, proj, dt_raw, dtb, alog, dskip_full, ng, e_bf)


def _ssd_bwd(dyb, y, xc, proj, dt_raw, sprev, dtb, alog, dskip_full, ng, e_bf, dproj, name):
    t = xc.shape[0]
    nc = t // CHUNK
    zcb = COL_Z // D_INNER
    hpg = GROUP_W // HEAD_DIM
    rev = lambda i: nc - 1 - i

    def body(dyb_ref, y_ref, xc_ref, z_ref, dt_ref, sprev_ref, dtb_ref, alog_ref, dsk_ref, ng_ref, e_ref, dproj_in,
             dz_ref, dxc_ref, ddt_ref, gng_ref, gdsk_ref, galog_ref, gdtb_ref, ds_scr, sums_scr):
        del dproj_in

        @pl.when(pl.program_id(0) == 0)
        def _():
            ds_scr[...] = jnp.zeros_like(ds_scr)
            gng_ref[...] = jnp.zeros_like(gng_ref)
            gdsk_ref[...] = jnp.zeros_like(gdsk_ref)
            galog_ref[...] = jnp.zeros_like(galog_ref)
            gdtb_ref[...] = jnp.zeros_like(gdtb_ref)

        mask = _tri(True)
        tril_bf = mask.astype(BF16)
        triu_bf = _tri(False).astype(BF16)
        e_v = e_ref[...]
        dt_in = dt_ref[...] + dtb_ref[...]
        dtv, a, cs, dt_full, ecs_full, decay_full = _ssd_decays(dt_ref[...], dtb_ref[...], alog_ref[...], e_v, tril_bf)
        cs_t = cs.T

        lane_h = lax.broadcasted_iota(jnp.int32, (CHUNK, DT_PAD), 1)
        sub_h = lax.broadcasted_iota(jnp.int32, (DT_PAD, CHUNK), 0)
        dcs_rows = jnp.zeros((CHUNK, DT_PAD), F32)
        dcs_cols_t = jnp.zeros((DT_PAD, CHUNK), F32)
        last_cols, dsk_cols = [], []
        for g in range(GROUPS):
            gc = slice(g * GROUP_W, (g + 1) * GROUP_W)
            b_cols = slice(D_INNER + g * D_STATE, D_INNER + (g + 1) * D_STATE)
            c_cols = slice(D_INNER + GROUPS * D_STATE + g * D_STATE, D_INNER + GROUPS * D_STATE + (g + 1) * D_STATE)
            xs = xc_ref[:, gc]
            xdt = xs * dt_full[:, gc]
            xdt_b = xdt.astype(BF16)
            xdec = xdt * decay_full[:, gc]
            xdec_b = xdec.astype(BF16)
            zv = z_ref[:, gc].astype(F32)
            sg = _sigmoid(zv)
            gate = zv * sg
            yv = y_ref[:, gc]
            dybv = dyb_ref[:, gc]
            ygate = yv * gate
            rstd = lax.rsqrt(jnp.mean(ygate * ygate, axis=-1, keepdims=True) + NORM_EPS)
            yn = ygate * rstd
            gng_ref[:, gc] += jnp.sum(dybv * yn, axis=0, keepdims=True)
            dyn = dybv * ng_ref[:, gc]
            dyg = rstd * (dyn - yn * jnp.mean(dyn * yn, axis=-1, keepdims=True))
            dz_ref[:, gc] = (dyg * yv * sg * (1.0 + zv * (1.0 - sg))).astype(BF16)
            dy = dyg * gate
            dy_b = dy.astype(BF16)
            dyo = dy * ecs_full[:, gc]
            dyo_b = dyo.astype(BF16)
            dsk_cols.append(jnp.sum(dy * xs, axis=0, keepdims=True))

            bg = xc_ref[:, b_cols].astype(BF16)
            cg = xc_ref[:, c_cols].astype(BF16)
            s_prev = sprev_ref[0, :, gc]
            s_prev_b = s_prev.astype(BF16)
            dsg = ds_scr[:, gc]
            dsg_b = dsg.astype(BF16)
            cb = _dot(cg, bg, _NT)
            c_s = _dot(cg, s_prev_b, _NN)
            b_ds = _dot(bg, dsg_b, _NN)
            dcb = jnp.zeros((CHUNK, CHUNK), F32)
            parts = []
            for r in range(hpg):
                h = g * hpg + r
                hc = slice(r * HEAD_DIM, (r + 1) * HEAD_DIM)
                lmat, m = _head_mats(cs, cs_t, cb, h, mask)
                dm = _dot(dy_b[:, hc], xdt_b[:, hc], _NT)
                parts.append(_dot(m.astype(BF16), dy_b[:, hc], _TN))
                dcb = dcb + dm * lmat
                w = dm * m
                dcs_rows = jnp.where(lane_h == h, jnp.sum(w, axis=1, keepdims=True), dcs_rows)
                dcs_cols_t = jnp.where(sub_h == h, jnp.sum(w, axis=0, keepdims=True), dcs_cols_t)
            dxdt = jnp.concatenate(parts, axis=1) + decay_full[:, gc] * b_ds
            dcb_b = dcb.astype(BF16)
            dxc_ref[:, c_cols] = _dot(dcb_b, bg, _NN) + _dot(dyo_b, s_prev_b, _NT)
            dxc_ref[:, b_cols] = _dot(dcb_b, cg, _TN) + _dot(xdec_b, dsg_b, _NT)
            cdec = ecs_full[CHUNK - 1:CHUNK, gc]
            ds_scr[:, gc] = _dot(cg, dyo_b, _TN) + cdec * dsg
            dxc_ref[:, gc] = dxdt * dt_full[:, gc] + dsk_ref[:, gc] * dy
            dec_prod = xdec * b_ds
            sums_scr[:CHUNK, gc] = dyo * c_s - dec_prod
            sums_scr[CHUNK:, gc] = dxdt * xs
            last_cols.append(jnp.sum(dec_prod, axis=0, keepdims=True) + cdec * jnp.sum(dsg * s_prev, axis=0, keepdims=True))
        t_sums = _head_sums(sums_scr[...], e_v)
        tail = jnp.concatenate([jnp.concatenate(last_cols, axis=1), jnp.concatenate(dsk_cols, axis=1),
                                jnp.zeros((SUBLANES - 2, D_INNER), F32)], axis=0)
        t_tail = _dot_exact_rhs(tail, e_v, _NT)
        gdsk_ref[...] += t_tail[1:2, :]
        row = lax.broadcasted_iota(jnp.int32, (CHUNK, DT_PAD), 0)
        dcs = dcs_rows - dcs_cols_t.T + t_sums[:CHUNK] + jnp.where(row == CHUNK - 1, t_tail[0:1, :], 0.0)
        dda = _dot_exact_lhs(triu_bf, dcs, _NN)
        galog_ref[...] += jnp.sum(dda * dtv, axis=0, keepdims=True) * a
        ddt = dda * a + t_sums[CHUNK:]
        ddt_raw = jnp.where(lane_h < N_HEADS, ddt * _sigmoid(dt_in), 0.0)
        gdtb_ref[...] += jnp.sum(ddt_raw, axis=0, keepdims=True)
        ddt_ref[...] = ddt_raw.astype(BF16)

    vec = lambda w: pl.BlockSpec((1, w), lambda i: (0, 0))
    blk = (2 * _nbytes((CHUNK, CONV_DIM), F32) + 4 * _nbytes((CHUNK, D_INNER), F32) + 4 * _nbytes((D_STATE, D_INNER), F32))
    return pl.pallas_call(
        body, name=name, grid=(nc,),
        in_specs=[pl.BlockSpec((CHUNK, D_INNER), lambda i: (rev(i), 0)), pl.BlockSpec((CHUNK, D_INNER), lambda i: (rev(i), 0)),
                  pl.BlockSpec((CHUNK, CONV_DIM), lambda i: (rev(i), 0)), pl.BlockSpec((CHUNK, D_INNER), lambda i: (rev(i), zcb)),
                  pl.BlockSpec((CHUNK, DT_PAD), lambda i: (rev(i), 0)), pl.BlockSpec((1, D_STATE, D_INNER), lambda i: (rev(i), 0, 0)),
                  vec(DT_PAD), vec(DT_PAD), vec(D_INNER), vec(D_INNER), pl.BlockSpec((DT_PAD, D_INNER), lambda i: (0, 0)),
                  pl.BlockSpec(memory_space=pl.ANY)],
        out_specs=[pl.BlockSpec((CHUNK, D_INNER), lambda i: (rev(i), zcb)), pl.BlockSpec((CHUNK, CONV_DIM), lambda i: (rev(i), 0)),
                   pl.BlockSpec((CHUNK, DT_PAD), lambda i: (rev(i), 0)), vec(D_INNER), vec(DT_PAD), vec(DT_PAD), vec(DT_PAD)],
        out_shape=[jax.ShapeDtypeStruct(dproj.shape, BF16), jax.ShapeDtypeStruct((t, CONV_DIM), F32),
                   jax.ShapeDtypeStruct((t, DT_PAD), BF16), jax.ShapeDtypeStruct((1, D_INNER), F32),
                   jax.ShapeDtypeStruct((1, DT_PAD), F32), jax.ShapeDtypeStruct((1, DT_PAD), F32),
                   jax.ShapeDtypeStruct((1, DT_PAD), F32)],
        scratch_shapes=[pltpu.VMEM((D_STATE, D_INNER), F32), pltpu.VMEM((2 * CHUNK, D_INNER), F32)],
        input_output_aliases={11: 0},
        compiler_params=_params(("arbitrary",), blk),
    )(dyb, y, xc, proj, dt_raw, sprev, dtb, alog, dskip_full, ng, e_bf, dproj)


def _mesh_pos():
    return lax.axis_index("x"), lax.axis_index("y"), lax.axis_index("c")


def _other_chips(x, y):
    return [(1 - x, y), (x, 1 - y), (1 - x, 1 - y)]


def _all_peers(x, y, c):
    peers = []
    for k in range(1, N_DEV):
        fx, fy, fc = (k >> 2) & 1, (k >> 1) & 1, k & 1
        px, py, pc = x + fx - 2 * x * fx, y + fy - 2 * y * fy, c + fc - 2 * c * fc
        peers.append(((px, py, pc), 4 * px + 2 * py + pc))
    return peers


def _all_gather(shards, name, own_only=()):
    n, n_own = len(shards), len(own_only)

    def body(*refs):
        ins, own_ins = refs[:n], refs[n:n + n_own]
        outs, own_outs = refs[n + n_own:2 * n + n_own], refs[2 * n + n_own:2 * (n + n_own)]
        send_sems, recv_sems, local_sems = refs[2 * (n + n_own):]
        x, y, c = _mesh_pos()
        me, sibling = (x, y, c), (x, y, 1 - c)
        chips = _other_chips(x, y)

        def slot(p):
            return 4 * p[0] + 2 * p[1] + p[2]

        def copy(a, k, block, to, src=None):
            dst = outs[a].at[slot(block)]
            return pltpu.make_async_remote_copy(
                src_ref=dst if src is None else src, dst_ref=dst, send_sem=send_sems.at[a * 7 + k],
                recv_sem=recv_sems.at[a * 7 + k], device_id=to, device_id_type=MESH)

        started = []
        own = []
        for a in range(n_own):
            mine = pltpu.make_async_copy(own_ins[a], own_outs[a].at[slot(me)], local_sems.at[n + a])
            mine.start()
            own.append(mine)
        for a in range(n):
            mine = pltpu.make_async_copy(ins[a], outs[a].at[slot(me)], local_sems.at[a])
            mine.start()
            own.append(mine)
            first = [copy(a, 0, me, sibling, src=ins[a])]
            first += [copy(a, 1 + j, me, (*chip, c), src=ins[a]) for j, chip in enumerate(chips)]
            for cp in first:
                cp.start()
            started += first
        for a in range(n):
            for j, chip in enumerate(chips):
                copy(a, 1 + j, (*chip, c), me).wait_recv()
                fwd = copy(a, 4 + j, (*chip, c), sibling)
                fwd.start()
                started.append(fwd)
        for a in range(n):
            copy(a, 0, sibling, me).wait_recv()
            for j, chip in enumerate(chips):
                copy(a, 4 + j, (*chip, 1 - c), me).wait_recv()
        for cp in started:
            cp.wait_send()
        for mine in own:
            mine.wait()

    return pl.pallas_call(
        body, name=name,
        in_specs=[_HBM] * (n + n_own), out_specs=[_HBM] * (n + n_own),
        out_shape=[jax.ShapeDtypeStruct((N_DEV,) + s.shape, s.dtype) for s in (*shards, *own_only)],
        scratch_shapes=[pltpu.SemaphoreType.DMA((7 * n,)), pltpu.SemaphoreType.DMA((7 * n,)),
                        pltpu.SemaphoreType.DMA((n + n_own,))],
    )(*shards, *own_only)


_SMALL_ROWS = (("norm_mix_g", 8), ("conv_b", 32), ("dt_bias", 1), ("a_log", 1), ("d_skip", 1), ("ssm_norm_g", 16),
               ("v_norm_g", 8), ("v_norm_b", 8), ("w_spatial", 1024), ("b_spatial", 8), ("b_gates", 16), ("norm_mlp_g", 8),
               ("norm_final_g", 8), ("conv_w", 128), ("loss", 1))
_LAST_SMALL = (("norm_mix_g", 8),)


def _packed_rows(table):
    return -(-sum(r for _, r in table) // SUBLANES) * SUBLANES


def _small_offsets(table=_SMALL_ROWS):
    offs, r = {}, 0
    for name, rows in table:
        offs[name] = r
        r += rows
    return offs


def _rows_from(src_ref, dst_ref, r0):
    k, w = src_ref.shape
    if w <= LANES:
        dst_ref[r0:r0 + k, 0:w] = src_ref[...]
        return
    per = w // LANES
    for i in range(k):
        for j in range(per):
            dst_ref[r0 + i * per + j:r0 + i * per + j + 1, :] = src_ref[i:i + 1, j * LANES:(j + 1) * LANES]


def _rows_to(src_ref, r0, dst_ref):
    k, w = dst_ref.shape
    if w <= LANES:
        dst_ref[...] = src_ref[r0:r0 + k, 0:w]
        return
    per = w // LANES
    for i in range(k):
        for j in range(per):
            dst_ref[i:i + 1, j * LANES:(j + 1) * LANES] = src_ref[r0 + i * per + j:r0 + i * per + j + 1, :]


def _pack_small(grads, slot_idx, name):
    names = [n for n, _ in _SMALL_ROWS if n in grads]
    offs = _small_offsets()
    rows = _packed_rows(_SMALL_ROWS)

    def body(slot_ref, *refs):
        del slot_ref
        ins, (packed_ref, land_ref) = refs[:len(names)], refs[len(names):]
        packed_ref[...] = jnp.zeros_like(packed_ref)
        for n, ref in zip(names, ins):
            _rows_from(ref, packed_ref, offs[n])
        land_ref[0] = packed_ref[...]

    whole = lambda shape: pl.BlockSpec(shape, lambda i, slot_ref: (0,) * len(shape))
    grid_spec = pltpu.PrefetchScalarGridSpec(
        num_scalar_prefetch=1, grid=(1,), in_specs=[whole(grads[n].shape) for n in names],
        out_specs=[whole((rows, LANES)), pl.BlockSpec((1, rows, LANES), lambda i, slot_ref: (slot_ref[0], 0, 0))])
    return pl.pallas_call(
        body, name=name, grid_spec=grid_spec,
        out_shape=[jax.ShapeDtypeStruct((rows, LANES), F32), jax.ShapeDtypeStruct((N_DEV, rows, LANES), F32)],
    )(slot_idx, *[grads[n] for n in names])


def _exchange_small(grads, table, name):
    names = [n for n, _ in table]
    offs = _small_offsets(table)
    n_in = len(names)
    packed_rows = _packed_rows(table)

    def body(*refs):
        ins, out_ref = refs[:n_in], refs[n_in]
        packed, send_sems, recv_sems, local_sem = refs[n_in + 1:]
        packed[...] = jnp.zeros_like(packed)
        for n, ref in zip(names, ins):
            _rows_from(ref, packed, offs[n])
        x, y, c = _mesh_pos()
        my_slot = 4 * x + 2 * y + c
        mine = pltpu.make_async_copy(packed, out_ref.at[my_slot], local_sem)
        mine.start()
        copies = []
        for k, (peer, peer_slot) in enumerate(_all_peers(x, y, c)):
            sems = dict(send_sem=send_sems.at[k], recv_sem=recv_sems.at[k], device_id=peer, device_id_type=MESH)
            send = pltpu.make_async_remote_copy(src_ref=packed, dst_ref=out_ref.at[my_slot], **sems)
            send.start()
            copies.append((send, pltpu.make_async_remote_copy(src_ref=packed, dst_ref=out_ref.at[peer_slot], **sems)))
        for send, recv in copies:
            send.wait_send()
            recv.wait_recv()
        mine.wait()

    return pl.pallas_call(
        body, name=name, in_specs=[pl.BlockSpec(memory_space=pltpu.VMEM)] * n_in, out_specs=_HBM,
        out_shape=jax.ShapeDtypeStruct((N_DEV, packed_rows, LANES), F32),
        scratch_shapes=[pltpu.VMEM((packed_rows, LANES), F32), pltpu.SemaphoreType.DMA((N_DEV - 1,)),
                        pltpu.SemaphoreType.DMA((N_DEV - 1,)), pltpu.SemaphoreType.DMA],
    )(*[grads[n] for n in names])


def _swap_with_sibling(grads, name):
    n = len(grads)

    def body(*refs):
        ins, outs = refs[:n], refs[n:2 * n]
        send_sems, recv_sems = refs[2 * n:]
        x, y, c = _mesh_pos()
        copies = []
        for a in range(n):
            for k in range(N_CHIP):
                cp = pltpu.make_async_remote_copy(
                    src_ref=ins[a].at[(1 - c) + 2 * k], dst_ref=outs[a].at[k], send_sem=send_sems.at[a * N_CHIP + k],
                    recv_sem=recv_sems.at[a * N_CHIP + k], device_id=(x, y, 1 - c), device_id_type=MESH)
                cp.start()
                copies.append(cp)
        for cp in copies:
            cp.wait()

    return pl.pallas_call(
        body, name=name, in_specs=[_HBM] * n, out_specs=[_HBM] * n,
        out_shape=[jax.ShapeDtypeStruct((N_CHIP,) + g.shape[1:], g.dtype) for g in grads],
        scratch_shapes=[pltpu.SemaphoreType.DMA((N_CHIP * n,)), pltpu.SemaphoreType.DMA((N_CHIP * n,))],
    )(*grads)


_SEM = pl.BlockSpec(memory_space=pltpu.SEMAPHORE)
_IN_HBM = pl.BlockSpec(memory_space=pltpu.HBM)
_EFFECT = pltpu.SideEffectType.DATAFLOW_SIDE_EFFECTING


def _in_hbm(a):
    return pltpu.with_memory_space_constraint(a, pltpu.HBM)


def _gather_copies(ins, lands, send_sems, recv_sems):
    x, y, c = _mesh_pos()
    my_slot = 4 * x + 2 * y + c
    pairs = []
    for a in range(len(ins)):
        for k, (peer, peer_slot) in enumerate(_all_peers(x, y, c)):
            sems = dict(send_sem=send_sems.at[a * (N_DEV - 1) + k], recv_sem=recv_sems.at[a * (N_DEV - 1) + k],
                        device_id=peer, device_id_type=MESH)
            pairs.append((pltpu.make_async_remote_copy(src_ref=ins[a], dst_ref=lands[a].at[my_slot], **sems),
                          pltpu.make_async_remote_copy(src_ref=ins[a], dst_ref=lands[a].at[peer_slot], **sems)))
    return pairs


def _scatter_copies(ins, lands, send_sems, recv_sems):
    x, y, c = _mesh_pos()
    my_chip = 2 * x + y
    pairs = []
    for a in range(len(ins)):
        for j, chip in enumerate(_other_chips(x, y)):
            there = 2 * chip[0] + chip[1]
            sems = dict(send_sem=send_sems.at[a * 3 + j], recv_sem=recv_sems.at[a * 3 + j],
                        device_id=(*chip, c), device_id_type=MESH)
            pairs.append((pltpu.make_async_remote_copy(src_ref=ins[a].at[there], dst_ref=lands[a].at[my_chip], **sems),
                          pltpu.make_async_remote_copy(src_ref=ins[a].at[my_chip], dst_ref=lands[a].at[there], **sems)))
    return pairs


def _split_start(srcs, lands, copies, per_array, name):
    n = len(srcs)

    def body(*refs):
        ins, land_refs = refs[:n], refs[n:2 * n]
        send_sems, recv_sems = refs[2 * n], refs[2 * n + 1]
        token = refs[-1]
        for send, _ in copies(ins, land_refs, send_sems, recv_sems):
            send.start()
        token[...] = jnp.zeros_like(token)

    outs = pl.pallas_call(
        body, name=name,
        out_shape=(pltpu.SemaphoreType.DMA((per_array * n,)), pltpu.SemaphoreType.DMA((per_array * n,)),
                   *[pltpu.HBM(s.shape, s.dtype) for s in srcs], *[pltpu.HBM(l.shape, l.dtype) for l in lands],
                   jax.ShapeDtypeStruct((SUBLANES, LANES), F32)),
        in_specs=[_IN_HBM] * (2 * n),
        out_specs=(_SEM, _SEM, *[_IN_HBM] * (2 * n), pl.BlockSpec(memory_space=pltpu.VMEM)),
        input_output_aliases={i: 2 + i for i in range(2 * n)},
        compiler_params=pltpu.CompilerParams(has_side_effects=_EFFECT),
    )(*[_in_hbm(s) for s in srcs], *[_in_hbm(l) for l in lands])
    return outs[0], outs[1], list(outs[2:2 + n]), list(outs[2 + n:2 + 2 * n]), outs[-1]


def _split_wait(started, copies, after, name):
    send_sems, recv_sems, srcs, lands, _ = started
    n = len(srcs)

    def body(*refs):
        ins, land_refs = refs[:n], refs[n:2 * n]
        for send, recv in copies(ins, land_refs, refs[2 * n], refs[2 * n + 1]):
            send.wait_send()
            recv.wait_recv()

    outs = pl.pallas_call(
        body, name=name,
        out_shape=(*[pltpu.HBM(s.shape, s.dtype) for s in srcs], *[pltpu.HBM(l.shape, l.dtype) for l in lands]),
        in_specs=[_IN_HBM] * (2 * n) + [_SEM, _SEM, _HBM],
        out_specs=[_IN_HBM] * (2 * n),
        input_output_aliases={i: i for i in range(2 * n)},
        compiler_params=pltpu.CompilerParams(has_side_effects=_EFFECT),
    )(*srcs, *lands, send_sems, recv_sems, after)
    return list(outs[:n]), list(outs[n:])


def _ew_block(rows, cols, slots):
    budget = 2 * 1024 * 1024
    br, bc = rows, cols
    while slots * br * bc * 4 > budget:
        if br % 2 == 0 and (br // 2) % (2 * SUBLANES) == 0:
            br //= 2
        elif bc % 2 == 0 and (bc // 2) % LANES == 0:
            bc //= 2
        else:
            break
    return br, bc


def _add_sibling(grads, recv, c_idx, name):
    _, rows, cols = grads.shape
    br, bc = _ew_block(rows, cols, 3)

    def body(c_ref, g_ref, r_ref, out_ref):
        del c_ref
        out_ref[...] = (g_ref[...].astype(F32) + r_ref[...].astype(F32)).astype(out_ref.dtype)

    grid_spec = pltpu.PrefetchScalarGridSpec(
        num_scalar_prefetch=1, grid=(N_CHIP, rows // br, cols // bc),
        in_specs=[pl.BlockSpec((1, br, bc), lambda k, i, j, c_ref: (c_ref[0] + 2 * k, i, j)),
                  pl.BlockSpec((1, br, bc), lambda k, i, j, c_ref: (k, i, j))],
        out_specs=pl.BlockSpec((1, br, bc), lambda k, i, j, c_ref: (k, i, j)))
    return pl.pallas_call(
        body, name=name, grid_spec=grid_spec, out_shape=jax.ShapeDtypeStruct((N_CHIP, rows, cols), grads.dtype),
        compiler_params=_params(("parallel", "parallel", "parallel"), 3 * _nbytes((br, bc), F32)),
    )(c_idx, grads, recv)


def _adam_math(g, w, m, v):
    m2 = ADAM_B1 * m + (1.0 - ADAM_B1) * g
    v2 = ADAM_B2 * v + (1.0 - ADAM_B2) * (g * g)
    m_hat = m2 * (1.0 / (1.0 - ADAM_B1 ** ADAM_STEP))
    v_hat = v2 * (1.0 / (1.0 - ADAM_B2 ** ADAM_STEP))
    return -ADAM_LR * (m_hat / (jnp.sqrt(v_hat) + ADAM_EPS) + ADAM_WD * w), m2, v2


def _adamw(slots, w, m, v, name, own=None, own_slot=None):
    ns, rows, cols = slots.shape
    br, bc = _ew_block(rows, cols, 2 * ns + 7)

    def update(g, w_ref, m_ref, v_ref, g_ref, d_ref, m2_ref, v2_ref):
        g_ref[...] = g
        d_ref[...], m2_ref[...], v2_ref[...] = _adam_math(g, w_ref[...], m_ref[...], v_ref[...])

    out_shape = [jax.ShapeDtypeStruct((rows, cols), F32)] * 4
    params = _params(("parallel", "parallel"), (2 * ns + 7) * _nbytes((br, bc), F32))
    grid = (rows // br, cols // bc)
    if own is None:
        def body(s_ref, *rest):
            g = s_ref[0].astype(F32)
            for k in range(1, ns):
                g = g + s_ref[k].astype(F32)
            update(g, *rest)

        blk = pl.BlockSpec((br, bc), lambda i, j: (i, j))
        return pl.pallas_call(
            body, name=name, grid=grid,
            in_specs=[pl.BlockSpec((ns, br, bc), lambda i, j: (0, i, j)), blk, blk, blk], out_specs=[blk] * 4,
            out_shape=out_shape, compiler_params=params,
        )(slots, w, m, v)

    def body_own(slot_ref, s_ref, o_ref, *rest):
        g = None
        for k in range(ns):
            term = jnp.where(slot_ref[0] == k, o_ref[k].astype(F32), s_ref[k].astype(F32))
            g = term if g is None else g + term
        update(g, *rest)

    blk = pl.BlockSpec((br, bc), lambda i, j, slot_ref: (i, j))
    stack = pl.BlockSpec((ns, br, bc), lambda i, j, slot_ref: (0, i, j))
    grid_spec = pltpu.PrefetchScalarGridSpec(num_scalar_prefetch=1, grid=grid, in_specs=[stack, stack, blk, blk, blk],
                                             out_specs=[blk] * 4)
    return pl.pallas_call(body_own, name=name, grid_spec=grid_spec, out_shape=out_shape, compiler_params=params,
                          )(own_slot, slots, own, w, m, v)


def _adamw_small(all_g, last_g, params, extra_shapes, name):
    names = [n for n, _ in _SMALL_ROWS if n in params]
    extras = [n for n, _ in _SMALL_ROWS if n not in params]
    offs = _small_offsets()
    n_p = len(names)

    def body(*refs):
        s_ref, last_ref = refs[0], refs[1]
        wmv = refs[2:2 + 3 * n_p]
        outs = refs[2 + 3 * n_p:2 + 7 * n_p]
        extra_refs = refs[2 + 7 * n_p:2 + 7 * n_p + len(extras)]
        summed = refs[-1]
        g, g_last = s_ref[0], last_ref[0]
        for k in range(1, N_DEV):
            g, g_last = g + s_ref[k], g_last + last_ref[k]
        summed[...] = g
        last_offs = _small_offsets(_LAST_SMALL)
        for n, rows in _LAST_SMALL:
            summed[offs[n]:offs[n] + rows, :] = g_last[last_offs[n]:last_offs[n] + rows, :]
        for i, n in enumerate(names):
            w_ref, m_ref, v_ref = wmv[3 * i:3 * i + 3]
            g_ref, d_ref, m2_ref, v2_ref = outs[4 * i:4 * i + 4]
            _rows_to(summed, offs[n], g_ref)
            d_ref[...], m2_ref[...], v2_ref[...] = _adam_math(g_ref[...], w_ref[...], m_ref[...], v_ref[...])
        for n, ref in zip(extras, extra_refs):
            _rows_to(summed, offs[n], ref)

    flat = [a for n in names for a in params[n]]
    out_shape = [jax.ShapeDtypeStruct(params[n][0].shape, F32) for n in names for _ in range(4)]
    out_shape += [jax.ShapeDtypeStruct(s, F32) for s in extra_shapes]
    vmem = pl.BlockSpec(memory_space=pltpu.VMEM)
    res = pl.pallas_call(
        body, name=name, in_specs=[vmem] * (2 + len(flat)), out_specs=[vmem] * len(out_shape), out_shape=out_shape,
        scratch_shapes=[pltpu.VMEM(all_g.shape[1:], F32)],
        compiler_params=pltpu.CompilerParams(vmem_limit_bytes=_vmem_limit(_nbytes(all_g.shape, F32))),
    )(all_g, last_g, *flat)
    return {n: res[4 * i:4 * i + 4] for i, n in enumerate(names)}, res[4 * n_p:]


def _mm_tiles(mode, m, n, k):
    tn = min(n, 1024)
    if mode == "tn":
        return min(m, 1024), tn, min(k, 4096)
    if k <= 1024:
        return min(m, 2048), tn, k
    if k <= 2048:
        return min(m, 1024), tn, k
    if k <= 4096:
        return min(m, 512), tn, k
    return min(m, 1024), tn, 2048


def _local_step(x, target, wts, small, exchange):
    t = x.shape[0]
    w_main_t, w_dt_t = wts["w_main_t"], wts["w_dt_t"]
    bsp_t = small["b_spatial"].T
    pad32 = lambda a: jnp.pad(a, ((0, 0), (0, DT_PAD - N_HEADS)))
    dtb, alog = pad32(small["dt_bias"]), pad32(small["a_log"])
    dskip_full = jnp.repeat(small["d_skip"], HEAD_DIM, axis=1)
    head_of_col = lax.broadcasted_iota(jnp.int32, (DT_PAD, D_INNER), 1) // HEAD_DIM
    e_bf = (head_of_col == lax.broadcasted_iota(jnp.int32, (DT_PAD, D_INNER), 0)).astype(BF16)

    def mm(a, b, mode, name, **kw):
        if mode == "nn":
            m, k, n = a.shape[0], a.shape[1], b.shape[1]
        elif mode == "nt":
            m, k, n = a.shape[0], a.shape[1], b.shape[0]
        else:
            m, k, n = a.shape[1], a.shape[0], b.shape[1]
        tm, tn, tk = _mm_tiles(mode, m, n, k)
        tm = min(tm, kw.pop("max_tm", tm))
        kw.setdefault("out_dtypes", (BF16,) if mode == "tn" else (F32,))
        if "extra_specs" in kw:
            kw["extra_specs"] = kw["extra_specs"](tm, tn)
        return _matmul(a, b, mode=mode, tm=tm, tn=tn, tk=tk, name=name, **kw)

    def out_tile(tm, tn):
        return (((tm, tn), lambda i, j: (i, j)),)

    def row_tiles(n_tiles, *vectors, gate_logits=False):
        def specs(tm, tn):
            out = [((tm, tn), lambda i, j: (i, j))] * n_tiles
            if gate_logits:
                out += [((tm, D_MODEL), lambda i, j, cb=COL_GATE // D_MODEL + half: (i, cb)) for half in range(2)]
            return tuple(out) + tuple(((1, w), lambda i, j, cb=cb: (0, cb)) for w, cb in vectors)
        return specs

    vec = lambda w: ((1, w), F32, (1, w), lambda i, j: (0, 0))
    fused_tm = 512

    h = _rms_fwd(x, small["norm_mix_g"], "rms_mix", deps=exchange.begin())
    proj = mm(h, w_main_t, "nt", "proj_main", j_outer=True, out_dtypes=(BF16,))
    dt_raw = mm(h, w_dt_t, "nt", "proj_dt")
    y_a = _gmlp_fwd(proj, small["v_norm_g"], small["v_norm_b"], small["w_spatial"], bsp_t, "gmlp_fwd")
    pre_conv, xc = _conv_fwd(proj, wts["conv_w"], small["conv_b"], "conv_fwd")
    y_ssd, y_b, sprev = _ssd_fwd(xc, proj, dt_raw, dtb, alog, dskip_full, small["ssm_norm_g"], e_bf, "ssd_fwd")
    wts = {**wts, **exchange.late_weights(y_b)}
    pa = mm(y_a, wts["w_proj_a"], "nn", "proj_a")
    pb, merged = mm(y_b, wts["w_proj_b"], "nn", "proj_b", epilogue=_merge_epilogue, out_dtypes=(F32, BF16), max_tm=fused_tm,
                    extras=(pa, proj, proj, small["b_gates"], small["b_gates"]),
                    extra_specs=row_tiles(1, (D_MODEL, 0), (D_MODEL, 1), gate_logits=True))
    x1, h2 = mm(merged, wts["w_out"], "nn", "out_proj", epilogue=_residual_rms_epilogue, out_dtypes=(F32, BF16),
                max_tm=2 * fused_tm, extras=(x, small["norm_mlp_g"]), extra_specs=row_tiles(1, (D_MODEL, 0)))

    def relu_sq(acc, ex, outs):
        r = jnp.maximum(acc, 0.0)
        outs[0][...] = (r * r).astype(BF16)

    act = mm(h2, wts["w_mlp_up"], "nn", "mlp_up", epilogue=relu_sq, out_dtypes=(BF16,), j_outer=True)
    dx2, dx2_b, g_final, _, loss = mm(
        act, wts["w_mlp_down"], "nn", "mlp_down", epilogue=_loss_epilogue, carry=True,
        out_dtypes=(F32, BF16, vec(D_MODEL), vec(D_MODEL), vec(LANES)),
        extras=(x1, small["norm_final_g"], target), extra_specs=lambda tm, tn: (
            ((tm, tn), lambda i, j: (i, j)), ((1, tn), lambda i, j: (0, 0)), ((tm, tn), lambda i, j: (i, j))))

    def relu_sq_bwd(acc, ex, outs):
        outs[0][...] = (acc * 2.0 * jnp.sqrt(ex[0][...].astype(F32))).astype(BF16)

    dup = mm(dx2_b, wts["w_mlp_down"], "nt", "d_act", epilogue=relu_sq_bwd, extras=(act,), extra_specs=out_tile,
             out_dtypes=(BF16,), j_outer=True)
    g_down = mm(act, dx2_b, "tn", "g_mlp_down")
    g_up = mm(h2, dup, "tn", "g_mlp_up")
    started = exchange.reduce("mlp", {"w_mlp_down": g_down, "w_mlp_up": g_up})
    dx1, dx1_b, g_mlp = mm(
        dup, wts["w_mlp_up"], "nt", "d_h2", deps=started, epilogue=_rms_bwd_epilogue, carry=True,
        out_dtypes=(F32, BF16, vec(D_MODEL)), extras=(x1, small["norm_mlp_g"], dx2), extra_specs=lambda tm, tn: (
            ((tm, tn), lambda i, j: (i, j)), ((1, tn), lambda i, j: (0, 0)), ((tm, tn), lambda i, j: (i, j))))

    g_out = mm(merged, dx1_b, "tn", "g_out")
    dpa, dpb, dproj, g_bgates = mm(
        dx1_b, wts["w_out"], "nt", "d_merged", epilogue=_merge_bwd_epilogue, carry=True, max_tm=fused_tm,
        out_dtypes=(BF16, BF16, ((t, MAIN_W), BF16, (fused_tm, 2 * D_MODEL), lambda i, j: (i, COL_GATE // (2 * D_MODEL))),
                    vec(2 * D_MODEL)),
        extras=(pa, pb, proj, proj, small["b_gates"], small["b_gates"]),
        extra_specs=row_tiles(2, (D_MODEL, 0), (D_MODEL, 1), gate_logits=True))
    g_pa = mm(y_a, dpa, "tn", "g_proj_a")
    g_pb = mm(y_b, dpb, "tn", "g_proj_b")
    started = exchange.reduce("proj", {"w_out": g_out, "w_proj_a": g_pa, "w_proj_b": g_pb})
    dya = mm(dpa, wts["w_proj_a"], "nt", "d_ya", deps=started)
    dyb = mm(dpb, wts["w_proj_b"], "nt", "d_yb")

    dproj, g_wsp, g_bsp_t, g_vg, g_vb = _gmlp_bwd(proj, dya, small["v_norm_g"], small["v_norm_b"], small["w_spatial"],
                                                   bsp_t, dproj, "gmlp_bwd")
    dproj, dxc, ddt, g_ng, g_dskip, g_alog, g_dtb = _ssd_bwd(dyb, y_ssd, xc, proj, dt_raw, sprev, dtb, alog, dskip_full,
                                                             small["ssm_norm_g"], e_bf, dproj, "ssd_bwd")
    dproj, g_convw, g_convb = _conv_bwd(proj, pre_conv, dxc, wts["conv_w"], dproj, "conv_bwd")

    small_grads = {
        "conv_w": g_convw, "loss": loss,
        "conv_b": g_convb, "dt_bias": g_dtb, "a_log": g_alog, "d_skip": g_dskip, "ssm_norm_g": g_ng,
        "v_norm_g": g_vg, "v_norm_b": g_vb, "w_spatial": g_wsp.reshape(GROUPS * CHUNK, CHUNK), "b_spatial": g_bsp_t.T,
        "b_gates": g_bgates, "norm_mlp_g": g_mlp, "norm_final_g": g_final,
    }
    g_main_t = mm(dproj, h, "tn", "g_in_main", deps=exchange.small(small_grads))
    g_dt_t = mm(ddt, h, "tn", "g_in_dt")
    started = exchange.reduce("in", {"w_in": (g_main_t, g_dt_t)})

    def input_grad(acc, ex, outs):
        x_ref, g_ref, res_ref, ddt_ref, wdt_ref = ex
        gg = jnp.zeros((1, D_MODEL), F32)
        for r in range(acc.shape[0] // ROW_TILE):
            rows = slice(r * ROW_TILE, (r + 1) * ROW_TILE)
            dh = acc[rows] + _dot(ddt_ref[rows, :], wdt_ref[...], _NN)
            dx, gg_r = _rms_pullback(x_ref[rows, :], g_ref[...], dh)
            outs[0][rows, :] = dx + res_ref[rows, :]
            gg = gg + gg_r

        @pl.when(_first_row_tile())
        def _():
            outs[1][...] = jnp.zeros_like(outs[1])

        outs[1][...] += gg

    grad_x, g_mix = mm(
        dproj, w_main_t, "nn", "d_h", epilogue=input_grad, deps=started, carry=True,
        out_dtypes=(F32, vec(D_MODEL)), extras=(x, small["norm_mix_g"], dx1, ddt, w_dt_t), extra_specs=lambda tm, tn: (
            ((tm, tn), lambda i, j: (i, j)), ((1, tn), lambda i, j: (0, 0)), ((tm, tn), lambda i, j: (i, j)),
            ((tm, DT_PAD), lambda i, j: (i, 0)), ((DT_PAD, D_MODEL), lambda i, j: (0, 0))))

    return grad_x, g_mix


SHARD_ROWS = (MAIN_W + N_HEADS) // N_DEV
REGROUP_IN = 2048


def _main_rows_of(gathered, name):
    n_dev, shard, d = gathered.shape
    blk = 1024
    nb = MAIN_W // blk

    def first_feature(b):
        return b * blk + (N_HEADS if b * blk >= COL_GATE else 0)

    def body(a_ref, b_ref, out_ref):
        for b in range(nb):
            s0, r0 = divmod(first_feature(b), shard)
            n1 = min(shard - r0, blk)

            @pl.when(pl.program_id(0) == b)
            def _(r0=r0, n1=n1):
                out_ref[0:n1, :] = a_ref[0, r0:r0 + n1, :]
                if n1 < blk:
                    out_ref[n1:blk, :] = b_ref[0, 0:blk - n1, :]

    def slot(b):
        return (b * blk + jnp.where(b * blk >= COL_GATE, N_HEADS, 0)) // shard

    return pl.pallas_call(
        body, name=name, grid=(nb,),
        in_specs=[pl.BlockSpec((1, shard, d), lambda b: (slot(b), 0, 0)),
                  pl.BlockSpec((1, shard, d), lambda b: (jnp.minimum(slot(b) + 1, n_dev - 1), 0, 0))],
        out_specs=pl.BlockSpec((blk, d), lambda b: (b, 0)),
        out_shape=jax.ShapeDtypeStruct((MAIN_W, d), gathered.dtype),
        compiler_params=_params(("parallel",), 3 * _nbytes((shard, d), gathered.dtype)),
    )(gathered, gathered)


def _by_device_rows(g_main_t, g_dt_t, name):
    d = g_main_t.shape[1]
    n_blocks = MAIN_W // REGROUP_IN
    dt_dev, dt_row = divmod(COL_GATE, SHARD_ROWS)

    def main_start(s):
        return s * SHARD_ROWS - (N_HEADS if s > dt_dev else 0)

    def body(a_ref, b_ref, dt_ref, out_ref):
        for s in range(N_DEV):
            m0 = main_start(s)
            k0, off = divmod(m0, REGROUP_IN)
            pieces = []
            if s == dt_dev:
                pieces = [(0, dt_row, m0), (dt_row, N_HEADS, None), (dt_row + N_HEADS, SHARD_ROWS - dt_row - N_HEADS, m0 + dt_row)]
            else:
                pieces = [(0, SHARD_ROWS, m0)]

            @pl.when(pl.program_id(0) == s)
            def _(pieces=pieces, k0=k0):
                for dst, n, src in pieces:
                    if src is None:
                        out_ref[0, dst:dst + n, :] = dt_ref[0:n, :]
                        continue
                    lo = src - k0 * REGROUP_IN
                    n_a = max(0, min(n, REGROUP_IN - lo))
                    if n_a:
                        out_ref[0, dst:dst + n_a, :] = a_ref[lo:lo + n_a, :]
                    if n_a < n:
                        lo_b = max(lo - REGROUP_IN, 0)
                        out_ref[0, dst + n_a:dst + n, :] = b_ref[lo_b:lo_b + n - n_a, :]

    def first_block(s):
        return (s * SHARD_ROWS - jnp.where(s > dt_dev, N_HEADS, 0)) // REGROUP_IN

    return pl.pallas_call(
        body, name=name, grid=(N_DEV,),
        in_specs=[pl.BlockSpec((REGROUP_IN, d), lambda s: (first_block(s), 0)),
                  pl.BlockSpec((REGROUP_IN, d), lambda s: (jnp.minimum(first_block(s) + 1, n_blocks - 1), 0)),
                  pl.BlockSpec((DT_PAD, d), lambda s: (0, 0))],
        out_specs=pl.BlockSpec((1, SHARD_ROWS, d), lambda s: (s, 0, 0)),
        out_shape=jax.ShapeDtypeStruct((N_DEV, SHARD_ROWS, d), g_main_t.dtype),
        compiler_params=_params(("parallel",), 3 * _nbytes((REGROUP_IN, d), g_main_t.dtype)),
    )(g_main_t, g_main_t, g_dt_t)


_LATE = ["w_proj_a", "w_proj_b", "w_out", "w_mlp_up", "w_mlp_down"]
_BY_COLS = ("w_mlp_up",)


class _Exchange:
    def __init__(self, late_shards, late_lands):
        self.late_shards, self.late_lands = late_shards, late_lands
        self.c_idx = lax.axis_index("c").astype(jnp.int32).reshape(1)
        self.chip_idx = (2 * lax.axis_index("x") + lax.axis_index("y")).astype(jnp.int32).reshape(1)
        self.pending = []

    def begin(self):
        self.late = _split_start(self.late_shards, self.late_lands, _gather_copies, N_DEV - 1, "gather_late_start")
        return [self.late[-1]]

    def late_weights(self, after):
        _, lands = _split_wait(self.late, _gather_copies, after, "gather_late_wait")
        whole = {}
        for n, g in zip(_LATE, lands):
            whole[n] = jnp.transpose(g, (1, 0, 2)).reshape(g.shape[1], -1) if n in _BY_COLS else g.reshape(-1, g.shape[2])
        return whole

    def reduce(self, tag, grads):
        names = list(grads)
        by_dev = []
        for n in names:
            g = grads[n]
            if n == "w_in":
                by_dev.append(_by_device_rows(*g, "regroup_g_in"))
            elif n in _BY_COLS:
                by_dev.append(jnp.transpose(g.reshape(g.shape[0], N_DEV, -1), (1, 0, 2)))
            else:
                by_dev.append(g.reshape(N_DEV, -1, g.shape[1]))
        from_sibling = _swap_with_sibling(by_dev, "reduce_cores_" + tag)
        parts = [_add_sibling(g, r, self.c_idx, "add_cores_" + n) for n, g, r in zip(names, by_dev, from_sibling)]
        lands = [lax.empty(p.shape, p.dtype) for p in parts]
        started = _split_start(parts, lands, _scatter_copies, 3, "reduce_chips_start_" + tag)
        self.pending.append((tag, names, started))
        return [started[-1]]

    def small(self, grads):
        dev = 2 * self.chip_idx + self.c_idx
        packed, land = _pack_small(grads, dev, "pack_small")
        self.small_started = _split_start([packed], [land], _gather_copies, N_DEV - 1, "exchange_small_start")
        return [self.small_started[-1]]

    def finish(self, after):
        _, (all_small,) = _split_wait(self.small_started, _gather_copies, after, "exchange_small_wait")
        done = {}
        for tag, names, started in self.pending:
            parts, lands = _split_wait(started, _scatter_copies, after, "reduce_chips_wait_" + tag)
            for n, land, part in zip(names, lands, parts):
                done[n] = (land, part, self.chip_idx)
        return all_small, done


def kernel(x, norm_mix_g, w_in, conv_w, conv_b, dt_bias, a_log, d_skip, ssm_norm_g, v_norm_g, v_norm_b, w_spatial, b_spatial, b_gates, w_proj_a, w_proj_b, w_out, norm_mlp_g, w_mlp_up, w_mlp_down, norm_final_g, loss_target, m_norm_mix_g, m_w_in, m_conv_w, m_conv_b, m_dt_bias, m_a_log, m_d_skip, m_ssm_norm_g, m_v_norm_g, m_v_norm_b, m_w_spatial, m_b_spatial, m_b_gates, m_w_proj_a, m_w_proj_b, m_w_out, m_norm_mlp_g, m_w_mlp_up, m_w_mlp_down, m_norm_final_g, v_norm_mix_g, v_w_in, v_conv_w, v_conv_b, v_dt_bias, v_a_log, v_d_skip, v_ssm_norm_g, v_v_norm_g, v_v_norm_b, v_w_spatial, v_b_spatial, v_b_gates, v_w_proj_a, v_w_proj_b, v_w_out, v_norm_mlp_g, v_w_mlp_up, v_w_mlp_down, v_norm_final_g):
    given = dict(locals())
    names = ["norm_mix_g", "w_in", "conv_w", "conv_b", "dt_bias", "a_log", "d_skip", "ssm_norm_g", "v_norm_g", "v_norm_b",
             "w_spatial", "b_spatial", "b_gates", "w_proj_a", "w_proj_b", "w_out", "norm_mlp_g", "w_mlp_up", "w_mlp_down",
             "norm_final_g"]
    shapes = {n: given[n].shape for n in names}
    dev = 4 * lax.axis_index("x") + 2 * lax.axis_index("y") + lax.axis_index("c")

    shard2d = {"w_in": w_in[0].T, "w_proj_a": w_proj_a[0], "w_proj_b": w_proj_b[0], "w_out": w_out[0],
               "w_mlp_up": w_mlp_up[0], "w_mlp_down": w_mlp_down[0]}
    conv_shard = conv_w.reshape(CONV_WIDTH, -1)
    late_shards = [shard2d[n].astype(BF16) for n in _LATE]
    w_in_all, conv_all, *late_lands = _all_gather([shard2d["w_in"].astype(BF16), conv_shard], "gather_first",
                                                  own_only=late_shards)
    dt_dev, dt_row = divmod(COL_GATE, SHARD_ROWS)
    w_dt_t = jnp.pad(w_in_all[dt_dev, dt_row:dt_row + N_HEADS], ((0, DT_PAD - N_HEADS), (0, 0)))
    wts = {"w_main_t": _main_rows_of(w_in_all, "regroup_w_in"), "w_dt_t": w_dt_t, "conv_w": jnp.transpose(conv_all, (1, 0, 2)).reshape(CONV_WIDTH, -1)}
    small = {"norm_mix_g": norm_mix_g, "conv_b": conv_b, "dt_bias": dt_bias, "a_log": a_log, "d_skip": d_skip,
             "ssm_norm_g": ssm_norm_g, "v_norm_g": v_norm_g, "v_norm_b": v_norm_b, "w_spatial": w_spatial[0],
             "b_spatial": b_spatial[0], "b_gates": b_gates, "norm_mlp_g": norm_mlp_g,
             "norm_final_g": norm_final_g.reshape(1, -1)}

    exchange = _Exchange(late_shards, late_lands)
    grad_x, g_mix = _local_step(x[0], loss_target[0], wts, small, exchange)

    out = {}
    all_small, large = exchange.finish(grad_x)
    for n, (slots, own, own_slot) in large.items():
        moments = [given["m_" + n][0], given["v_" + n][0]]
        if n == "w_in":
            moments = [mom.T for mom in moments]
        res = _adamw(slots, shard2d[n], *moments, "adamw_" + n, own=own, own_slot=own_slot)
        out[n] = [(r.T if n == "w_in" else r).reshape(shapes[n]) for r in res]

    last_small = _exchange_small({"norm_mix_g": g_mix}, _LAST_SMALL, "exchange_last")
    small["w_spatial"] = small["w_spatial"].reshape(GROUPS * CHUNK, CHUNK)
    params = {n: (w2d, given["m_" + n].reshape(w2d.shape), given["v_" + n].reshape(w2d.shape)) for n, w2d in small.items()}
    updated, (g_conv_full, loss_all) = _adamw_small(all_small, last_small, params, [(CONV_WIDTH, CONV_DIM), (1, LANES)],
                                                    "adamw_small")
    for n, res in updated.items():
        out[n] = [r.reshape(shapes[n]) for r in res]
    width = shapes["conv_w"][-1]
    g_conv = lax.dynamic_slice(g_conv_full, (0, dev * width), (CONV_WIDTH, width))
    res = _adamw(g_conv[None], conv_shard, m_conv_w.reshape(CONV_WIDTH, -1), v_conv_w.reshape(CONV_WIDTH, -1), "adamw_conv_w")
    out["conv_w"] = [r.reshape(shapes["conv_w"]) for r in res]

    loss = loss_all[0, 0]
    return (loss, grad_x[None], *[out[n][0] for n in names], *[out[n][1] for n in names],
            *[out[n][2] for n in names], *[out[n][3] for n in names])
```

```python
import functools
import math

import jax
import jax.numpy as jnp
from jax import lax
from jax.experimental import pallas as pl
from jax.experimental.pallas import tpu as pltpu

F32 = jnp.float32
BF16 = jnp.bfloat16
MESH = pl.DeviceIdType.MESH

D_MODEL = 1024
NORM_EPS = 1e-6
CHUNK = 128
GROUPS = 8
D_INNER = 2048
HEAD_DIM = 64
N_HEADS = 32
D_STATE = 128
CONV_WIDTH = 4
CONV_DIM = 4096
D_FF = 4096
GROUP_W = D_INNER // GROUPS
N_DEV = 8
N_CHIP = 4

ADAM_LR = 0.001
ADAM_B1 = 0.9
ADAM_B2 = 0.999
ADAM_EPS = 1e-08
ADAM_WD = 0.01
ADAM_STEP = 10

MAIN_W = 2 * D_MODEL + D_INNER + CONV_DIM + 2 * D_MODEL
COL_Z = 2048
COL_XBC = 4096
COL_GATE = 8192
DT_PAD = 128

LANES = 128
SUBLANES = 8
VMEM_BYTES_V7X = 64 * 1024 * 1024
VMEM_BODY_TEMP = 24 * 1024 * 1024


def _vmem_limit(block_bytes):
    return int(min(2 * block_bytes + VMEM_BODY_TEMP, VMEM_BYTES_V7X - 8 * 1024 * 1024))


def _nbytes(shape, dtype):
    return math.prod(shape) * jnp.dtype(dtype).itemsize


_HBM = pl.BlockSpec(memory_space=pl.ANY)


def _params(sem, block_bytes):
    return pltpu.CompilerParams(dimension_semantics=sem, vmem_limit_bytes=_vmem_limit(block_bytes))


def _sigmoid(x):
    return 1.0 / (1.0 + jnp.exp(-x))


def _softplus(x):
    e = jnp.exp(-jnp.abs(x))
    u = 1.0 + e
    log1p_e = jnp.where(u == 1.0, e, jnp.log(u) * (e / jnp.where(u == 1.0, 1.0, u - 1.0)))
    return jnp.maximum(x, 0.0) + log1p_e


_SQRT_HALF = 0.7071067811865476
_INV_SQRT_2PI = 0.3989422804014327


def _normal_cdf(x):
    return 0.5 * (1.0 + lax.erf(x * _SQRT_HALF))


def _gelu_grad(x, cdf):
    return cdf + x * jnp.exp(-0.5 * x * x) * _INV_SQRT_2PI


def _dot(a, b, dims):
    return lax.dot_general(a, b, (dims, ((), ())), preferred_element_type=F32)


_NN = ((1,), (0,))
_NT = ((1,), (1,))
_TN = ((0,), (0,))


def _split3(x):
    hi = x.astype(BF16)
    r1 = x - hi.astype(F32)
    mid = r1.astype(BF16)
    lo = (r1 - mid.astype(F32)).astype(BF16)
    return hi, mid, lo


def _dot_exact_rhs(x, e, dims):
    hi, mid, lo = _split3(x)
    return _dot(hi, e, dims) + _dot(mid, e, dims) + _dot(lo, e, dims)


def _dot_exact_lhs(e, x, dims):
    hi, mid, lo = _split3(x)
    return _dot(e, hi, dims) + _dot(e, mid, dims) + _dot(e, lo, dims)


def _tri(lower):
    r = lax.broadcasted_iota(jnp.int32, (CHUNK, CHUNK), 0)
    c = lax.broadcasted_iota(jnp.int32, (CHUNK, CHUNK), 1)
    return (r >= c) if lower else (r <= c)


def _matmul(a, b, *, mode, tm, tn, tk, out_dtypes, name, epilogue=None, extras=(), extra_specs=(), j_outer=False, deps=(),
            carry=False):
    if mode == "nn":
        (m, k), (_, n) = a.shape, b.shape
    elif mode == "nt":
        (m, k), (n, _) = a.shape, b.shape
    else:
        (k, m), (_, n) = a.shape, b.shape
    assert m % tm == 0 and n % tn == 0 and k % tk == 0, (name, m, n, k, tm, tn, tk)
    nk = k // tk
    n_extra, n_out = len(extras), len(out_dtypes)
    first_out = 2 + n_extra + len(deps)
    dims = {"nn": _NN, "nt": _NT, "tn": _TN}[mode]
    if epilogue is None:
        def epilogue(acc, ex, outs):
            outs[0][...] = acc.astype(outs[0].dtype)

    def body(*refs):
        a_ref, b_ref = refs[0], refs[1]
        ex_refs = refs[2:2 + n_extra]
        outs = refs[first_out:first_out + n_out]
        p = _dot(a_ref[...], b_ref[...], dims)
        if nk == 1:
            epilogue(p, ex_refs, outs)
        else:
            acc_ref = refs[first_out + n_out]
            kk = pl.program_id(2)

            @pl.when(kk == 0)
            def _():
                acc_ref[...] = p

            @pl.when(kk > 0)
            def _():
                acc_ref[...] += p

            @pl.when(kk == nk - 1)
            def _():
                epilogue(acc_ref[...], ex_refs, outs)

    if j_outer:
        grid = (n // tn, m // tm, nk)
        ij = lambda g0, g1: (g1, g0)
    else:
        grid = (m // tm, n // tn, nk)
        ij = lambda g0, g1: (g0, g1)

    def wrap(fn):
        return lambda g0, g1, kk: fn(*ij(g0, g1), kk)

    if mode == "nn":
        a_spec = pl.BlockSpec((tm, tk), wrap(lambda i, j, kk: (i, kk)))
        b_spec = pl.BlockSpec((tk, tn), wrap(lambda i, j, kk: (kk, j)))
        a_blk, b_blk = (tm, tk), (tk, tn)
    elif mode == "nt":
        a_spec = pl.BlockSpec((tm, tk), wrap(lambda i, j, kk: (i, kk)))
        b_spec = pl.BlockSpec((tn, tk), wrap(lambda i, j, kk: (j, kk)))
        a_blk, b_blk = (tm, tk), (tn, tk)
    else:
        a_spec = pl.BlockSpec((tk, tm), wrap(lambda i, j, kk: (kk, i)))
        b_spec = pl.BlockSpec((tk, tn), wrap(lambda i, j, kk: (kk, j)))
        a_blk, b_blk = (tk, tm), (tk, tn)
    ex_specs = [pl.BlockSpec(shape, wrap(lambda i, j, kk, f=f: f(i, j))) for shape, f in extra_specs]
    outs = [o if isinstance(o, tuple) else ((m, n), o, (tm, tn), lambda i, j: (i, j)) for o in out_dtypes]
    out_spec = [pl.BlockSpec(blk_shape, wrap(lambda i, j, kk, f=f: f(i, j))) for _, _, blk_shape, f in outs]
    out_shape = [jax.ShapeDtypeStruct(shape, dt) for shape, dt, _, _ in outs]
    blk = (_nbytes(a_blk, a.dtype) + _nbytes(b_blk, b.dtype) + sum(_nbytes(s, F32) for s, _ in extra_specs)
           + sum(_nbytes(blk_shape, dt) for _, dt, blk_shape, _ in outs) + _nbytes((tm, tn), F32))
    order = ("arbitrary",) * 3 if carry else ("parallel", "parallel", "arbitrary")
    res = pl.pallas_call(
        body, name=name, grid=grid,
        in_specs=[a_spec, b_spec] + ex_specs + [_HBM] * len(deps), out_specs=out_spec, out_shape=out_shape,
        scratch_shapes=[pltpu.VMEM((tm, tn), F32)] if nk > 1 else [],
        compiler_params=_params(order, blk),
    )(a, b, *extras, *deps)
    return res[0] if n_out == 1 else res


ROW_TILE = 256


def _row_spec(width, col_block=0, tile=ROW_TILE):
    return pl.BlockSpec((tile, width), lambda i, cb=col_block: (i, cb))


def _vec_spec(width, col_block=0):
    return pl.BlockSpec((1, width), lambda i, cb=col_block: (0, cb))


def _rms_fwd(x, g, w_t, name, deps=()):
    t = x.shape[0]
    n_small = w_t.shape[0]

    def body(x_ref, g_ref, w_ref, *rest):
        h_ref, small_ref = rest[-2:]
        xv = x_ref[...]
        r = lax.rsqrt(jnp.mean(xv * xv, axis=-1, keepdims=True) + NORM_EPS)
        h = (xv * r * g_ref[...]).astype(BF16)
        h_ref[...] = h
        small_ref[...] = _dot(h, w_ref[...], _NT)

    return pl.pallas_call(
        body, name=name, grid=(t // ROW_TILE,),
        in_specs=[_row_spec(D_MODEL), _vec_spec(D_MODEL), pl.BlockSpec((n_small, D_MODEL), lambda i: (0, 0))]
        + [_HBM] * len(deps),
        out_specs=[_row_spec(D_MODEL), _row_spec(n_small)],
        out_shape=[jax.ShapeDtypeStruct((t, D_MODEL), BF16), jax.ShapeDtypeStruct((t, n_small), F32)],
        compiler_params=_params(("parallel",), 3 * _nbytes((ROW_TILE, D_MODEL), F32)),
    )(x, g, w_t, *deps)


def _rms_scale(xv):
    r = lax.rsqrt(jnp.mean(xv * xv, axis=-1, keepdims=True) + NORM_EPS)
    return r, xv * r


def _rms_pullback(xv, g, dh):
    r, xh = _rms_scale(xv)
    dyg = dh * g
    return r * (dyg - xh * jnp.mean(dyg * xh, axis=-1, keepdims=True)), jnp.sum(dh * xh, axis=0, keepdims=True)


def _first_row_tile():
    return pl.program_id(0) == 0


def _residual_rms_epilogue(acc, ex, outs):
    x1 = acc + ex[0][...]
    outs[0][...] = x1
    _, xh = _rms_scale(x1)
    outs[1][...] = (xh * ex[1][...]).astype(BF16)


def _loss_epilogue(acc, ex, outs):
    dx_ref, dxb_ref, gg_ref, sq_ref, tot_ref = outs
    gv = ex[1][...]
    r, xh = _rms_scale(acc + ex[0][...])
    err = xh * gv - ex[2][...]
    dy = err * (1.0 / D_MODEL)
    dyg = dy * gv
    dx = r * (dyg - xh * jnp.mean(dyg * xh, axis=-1, keepdims=True))
    dx_ref[...] = dx
    dxb_ref[...] = dx.astype(BF16)

    @pl.when(_first_row_tile())
    def _():
        gg_ref[...] = jnp.zeros_like(gg_ref)
        sq_ref[...] = jnp.zeros_like(sq_ref)

    gg_ref[...] += jnp.sum(dy * xh, axis=0, keepdims=True)
    sq_ref[...] += jnp.sum(err * err, axis=0, keepdims=True)
    tot_ref[...] = jnp.broadcast_to(jnp.sum(sq_ref[...], axis=1, keepdims=True) * (0.5 / D_MODEL), tot_ref.shape)


def _rms_bwd_epilogue(dh, ex, outs):
    dx, gg = _rms_pullback(ex[0][...], ex[1][...], dh)
    dx = dx + ex[2][...]
    outs[0][...] = dx
    if len(outs) == 3:
        outs[1][...] = dx.astype(BF16)

    @pl.when(_first_row_tile())
    def _():
        outs[-1][...] = jnp.zeros_like(outs[-1])

    outs[-1][...] += gg


def _merge_epilogue(acc, ex, outs):
    outs[0][...] = acc
    ga = _sigmoid(ex[1][...].astype(F32) + ex[3][...])
    gb = _sigmoid(ex[2][...].astype(F32) + ex[4][...])
    outs[1][...] = (ga * ex[0][...] + gb * acc).astype(BF16)


def _merge_bwd_epilogue(dm, ex, outs):
    dpa_ref, dpb_ref, dgl_ref, gb_ref = outs
    ga = _sigmoid(ex[2][...].astype(F32) + ex[4][...])
    gb = _sigmoid(ex[3][...].astype(F32) + ex[5][...])
    dpa_ref[...] = (dm * ga).astype(BF16)
    dpb_ref[...] = (dm * gb).astype(BF16)
    dla = dm * ex[0][...] * ga * (1.0 - ga)
    dlb = dm * ex[1][...] * gb * (1.0 - gb)
    dgl_ref[:, :D_MODEL] = dla.astype(BF16)
    dgl_ref[:, D_MODEL:] = dlb.astype(BF16)

    @pl.when(_first_row_tile())
    def _():
        gb_ref[...] = jnp.zeros_like(gb_ref)

    gb_ref[:, :D_MODEL] += jnp.sum(dla, axis=0, keepdims=True)
    gb_ref[:, D_MODEL:] += jnp.sum(dlb, axis=0, keepdims=True)


GMLP_TILE = 512
GMLP_NC = GMLP_TILE // CHUNK


def _gmlp_common(u_pre, v_pre, vg, vb):
    cdf_u, cdf_v = _normal_cdf(u_pre), _normal_cdf(v_pre)
    u = u_pre * cdf_u
    v = v_pre * cdf_v
    mu = jnp.mean(v, axis=-1, keepdims=True)
    vc = v - mu
    rstd = lax.rsqrt(jnp.mean(vc * vc, axis=-1, keepdims=True) + NORM_EPS)
    vh = vc * rstd
    vn = vh * vg + vb
    return u, vh, vn, rstd, cdf_u, cdf_v


def _chunks_to_lanes(x, g):
    return jnp.concatenate([x[c * CHUNK:(c + 1) * CHUNK, g * CHUNK:(g + 1) * CHUNK] for c in range(GMLP_NC)], axis=1)


def _gmlp_fwd(proj, vg, vb, wsp, bsp_t, name):
    t = proj.shape[0]

    def body(u_ref, v_ref, vg_ref, vb_ref, w_ref, b_ref, ya_ref):
        u, _, vn, _, _, _ = _gmlp_common(u_ref[...].astype(F32), v_ref[...].astype(F32), vg_ref[...], vb_ref[...])
        mask = _tri(True)
        bt = b_ref[...]
        for g in range(GROUPS):
            w = jnp.where(mask, w_ref[g], 0.0).astype(BF16)
            vcat = _chunks_to_lanes(vn, g).astype(BF16)
            s = _dot(w, vcat, _NN) + bt[:, g:g + 1]
            for c in range(GMLP_NC):
                rows, cols = slice(c * CHUNK, (c + 1) * CHUNK), slice(g * CHUNK, (g + 1) * CHUNK)
                ya_ref[rows, cols] = (u[rows, cols] * s[:, c * CHUNK:(c + 1) * CHUNK]).astype(BF16)

    return pl.pallas_call(
        body, name=name, grid=(t // GMLP_TILE,),
        in_specs=[_row_spec(D_MODEL, 0, GMLP_TILE), _row_spec(D_MODEL, 1, GMLP_TILE), _vec_spec(D_MODEL),
                  _vec_spec(D_MODEL), pl.BlockSpec((GROUPS, CHUNK, CHUNK), lambda i: (0, 0, 0)),
                  pl.BlockSpec((CHUNK, GROUPS), lambda i: (0, 0))],
        out_specs=_row_spec(D_MODEL, 0, GMLP_TILE),
        out_shape=jax.ShapeDtypeStruct((t, D_MODEL), BF16),
        compiler_params=_params(("parallel",), 3 * _nbytes((GMLP_TILE, D_MODEL), F32)),
    )(proj, proj, vg, vb, wsp, bsp_t)


def _gmlp_bwd(proj, dya, vg, vb, wsp, bsp_t, dproj, name):
    t = proj.shape[0]

    def body(u_ref, v_ref, dya_ref, vg_ref, vb_ref, w_ref, b_ref, dproj_in, duv_ref, gw_ref, gbt_ref, gvg_ref, gvb_ref,
             dvn_scr, du_scr):
        del dproj_in
        u_pre, v_pre = u_ref[...].astype(F32), v_ref[...].astype(F32)
        vgv = vg_ref[...]
        u, vh, vn, rstd, cdf_u, cdf_v = _gmlp_common(u_pre, v_pre, vgv, vb_ref[...])
        dya = dya_ref[...]
        mask = _tri(True)
        bt = b_ref[...]
        first = pl.program_id(0) == 0

        @pl.when(first)
        def _():
            gw_ref[...] = jnp.zeros_like(gw_ref)
            gbt_ref[...] = jnp.zeros_like(gbt_ref)
            gvg_ref[...] = jnp.zeros_like(gvg_ref)
            gvb_ref[...] = jnp.zeros_like(gvb_ref)

        lane = lax.broadcasted_iota(jnp.int32, (CHUNK, GROUPS), 1)
        gbt = jnp.zeros((CHUNK, GROUPS), F32)
        for g in range(GROUPS):
            w = jnp.where(mask, w_ref[g], 0.0).astype(BF16)
            vcat = _chunks_to_lanes(vn, g).astype(BF16)
            s = _dot(w, vcat, _NN) + bt[:, g:g + 1]
            ds = _chunks_to_lanes(dya * u, g)
            gbt = jnp.where(lane == g, jnp.sum(ds, axis=1, keepdims=True), gbt)
            dsb = ds.astype(BF16)
            gw_ref[g] += jnp.where(mask, _dot(dsb, vcat, _NT), 0.0)
            dv = _dot(w, dsb, _TN)
            for c in range(GMLP_NC):
                rows, cols = slice(c * CHUNK, (c + 1) * CHUNK), slice(g * CHUNK, (g + 1) * CHUNK)
                dvn_scr[rows, cols] = dv[:, c * CHUNK:(c + 1) * CHUNK]
                du_scr[rows, cols] = dya[rows, cols] * s[:, c * CHUNK:(c + 1) * CHUNK]
        gbt_ref[...] += gbt
        dvn = dvn_scr[...]
        gvg_ref[...] += jnp.sum(dvn * vh, axis=0, keepdims=True)
        gvb_ref[...] += jnp.sum(dvn, axis=0, keepdims=True)
        dvh = dvn * vgv
        dv = rstd * (dvh - jnp.mean(dvh, axis=-1, keepdims=True) - vh * jnp.mean(dvh * vh, axis=-1, keepdims=True))
        duv_ref[:, :D_MODEL] = (du_scr[...] * _gelu_grad(u_pre, cdf_u)).astype(BF16)
        duv_ref[:, D_MODEL:] = (dv * _gelu_grad(v_pre, cdf_v)).astype(BF16)

    return pl.pallas_call(
        body, name=name, grid=(t // GMLP_TILE,),
        in_specs=[_row_spec(D_MODEL, 0, GMLP_TILE), _row_spec(D_MODEL, 1, GMLP_TILE), _row_spec(D_MODEL, 0, GMLP_TILE),
                  _vec_spec(D_MODEL), _vec_spec(D_MODEL), pl.BlockSpec((GROUPS, CHUNK, CHUNK), lambda i: (0, 0, 0)),
                  pl.BlockSpec((CHUNK, GROUPS), lambda i: (0, 0)), pl.BlockSpec(memory_space=pl.ANY)],
        out_specs=[_row_spec(2 * D_MODEL, 0, GMLP_TILE), pl.BlockSpec((GROUPS, CHUNK, CHUNK), lambda i: (0, 0, 0)),
                   pl.BlockSpec((CHUNK, GROUPS), lambda i: (0, 0)), _vec_spec(D_MODEL), _vec_spec(D_MODEL)],
        out_shape=[jax.ShapeDtypeStruct(dproj.shape, BF16), jax.ShapeDtypeStruct((GROUPS, CHUNK, CHUNK), F32),
                   jax.ShapeDtypeStruct((CHUNK, GROUPS), F32), jax.ShapeDtypeStruct((1, D_MODEL), F32),
                   jax.ShapeDtypeStruct((1, D_MODEL), F32)],
        scratch_shapes=[pltpu.VMEM((GMLP_TILE, D_MODEL), F32), pltpu.VMEM((GMLP_TILE, D_MODEL), F32)],
        input_output_aliases={7: 0},
        compiler_params=_params(("arbitrary",), 6 * _nbytes((GMLP_TILE, D_MODEL), F32)),
    )(proj, proj, dya, vg, vb, wsp, bsp_t, dproj)


CONV_TILE = 512
CONV_COLS = 1024
CONV_RB = 32
HALO = SUBLANES


def _conv_fwd(proj, cw, cb, name):
    t = proj.shape[0]
    nj = CONV_DIM // CONV_COLS
    xcb = COL_XBC // CONV_COLS
    before = 2 * HALO
    rb = CONV_TILE // before

    def body(x_ref, prev_ref, cw_ref, cb_ref, pre_ref, xc_ref):
        i = pl.program_id(1)
        cw_v = cw_ref[...]
        cb_v = cb_ref[...]
        for b in range(CONV_TILE // CONV_RB):
            if b == 0:
                prev = jnp.where(i > 0, prev_ref[...].astype(F32)[HALO:, :], 0.0)
                ext = jnp.concatenate([prev, x_ref[:CONV_RB, :].astype(F32)], axis=0)
            else:
                ext = x_ref[b * CONV_RB - before:(b + 1) * CONV_RB, :].astype(F32)[HALO:, :]
            pre = cb_v + cw_v[CONV_WIDTH - 1:CONV_WIDTH, :] * ext[HALO:, :]
            for k in range(CONV_WIDTH - 1):
                back = CONV_WIDTH - 1 - k
                pre = pre + cw_v[k:k + 1, :] * pltpu.roll(ext, back, 0)[HALO:, :]
            pre_ref[b * CONV_RB:(b + 1) * CONV_RB, :] = pre
            xc_ref[b * CONV_RB:(b + 1) * CONV_RB, :] = pre * _sigmoid(pre)

    tile = pl.BlockSpec((CONV_TILE, CONV_COLS), lambda j, i: (i, j))
    return pl.pallas_call(
        body, name=name, grid=(nj, t // CONV_TILE),
        in_specs=[pl.BlockSpec((CONV_TILE, CONV_COLS), lambda j, i: (i, xcb + j)),
                  pl.BlockSpec((before, CONV_COLS), lambda j, i: (jnp.maximum(i * rb - 1, 0), xcb + j)),
                  pl.BlockSpec((CONV_WIDTH, CONV_COLS), lambda j, i: (0, j)),
                  pl.BlockSpec((1, CONV_COLS), lambda j, i: (0, j))],
        out_specs=[tile, tile],
        out_shape=[jax.ShapeDtypeStruct((t, CONV_DIM), F32), jax.ShapeDtypeStruct((t, CONV_DIM), F32)],
        compiler_params=_params(("parallel", "parallel"), 4 * _nbytes((CONV_TILE, CONV_COLS), F32)),
    )(proj, proj, cw, cb)


def _fold_rows(v):
    out = v[:SUBLANES]
    for r in range(1, v.shape[0] // SUBLANES):
        out = out + v[r * SUBLANES:(r + 1) * SUBLANES]
    return out


def _conv_bwd(proj, pre, dxc, cw, dproj, name):
    t = proj.shape[0]
    nj = CONV_DIM // CONV_COLS
    ni = t // CONV_TILE
    xcb = COL_XBC // CONV_COLS
    rb = CONV_TILE // HALO
    last_rb = t // HALO - 1

    def body(x_ref, p_ref, pnext_ref, d_ref, dnext_ref, cw_ref, dproj_in, dx_ref, gw_ref, gb_ref):
        del dproj_in
        i = pl.program_id(1)
        cw_v = cw_ref[...]

        def dpre_of(p, d):
            sg = _sigmoid(p)
            return d * sg * (1.0 + p * (1.0 - sg))

        @pl.when(i == 0)
        def _():
            gw_ref[...] = jnp.zeros_like(gw_ref)
            gb_ref[...] = jnp.zeros_like(gb_ref)

        head = dpre_of(pnext_ref[...], jnp.where(i < ni - 1, dnext_ref[...], 0.0))
        gb_acc = jnp.zeros((SUBLANES, CONV_COLS), F32)
        gw_acc = [jnp.zeros((SUBLANES, CONV_COLS), F32) for _ in range(CONV_WIDTH)]
        for b in reversed(range(CONV_TILE // CONV_RB)):
            rows = slice(b * CONV_RB, (b + 1) * CONV_RB)
            cur = dpre_of(p_ref[rows, :], d_ref[rows, :])
            ext = jnp.concatenate([cur, head], axis=0)
            xv = x_ref[rows, :].astype(F32)
            dx = None
            for k in range(CONV_WIDTH):
                shift = CONV_WIDTH - 1 - k
                win = cur if shift == 0 else pltpu.roll(ext, CONV_RB + HALO - shift, 0)[:CONV_RB, :]
                term = cw_v[k:k + 1, :] * win
                dx = term if dx is None else dx + term
                gw_acc[k] = gw_acc[k] + _fold_rows(win * xv)
            dx_ref[rows, :] = dx.astype(BF16)
            gb_acc = gb_acc + _fold_rows(cur)
            head = cur[:HALO]
        gb_ref[...] += jnp.sum(gb_acc, axis=0, keepdims=True)
        for k in range(CONV_WIDTH):
            gw_ref[k:k + 1, :] += jnp.sum(gw_acc[k], axis=0, keepdims=True)

    tile = pl.BlockSpec((CONV_TILE, CONV_COLS), lambda j, i: (i, j))
    after = pl.BlockSpec((HALO, CONV_COLS), lambda j, i: (jnp.minimum((i + 1) * rb, last_rb), j))
    return pl.pallas_call(
        body, name=name, grid=(nj, ni),
        in_specs=[pl.BlockSpec((CONV_TILE, CONV_COLS), lambda j, i: (i, xcb + j)), tile, after, tile, after,
                  pl.BlockSpec((CONV_WIDTH, CONV_COLS), lambda j, i: (0, j)),
                  pl.BlockSpec(memory_space=pl.ANY)],
        out_specs=[pl.BlockSpec((CONV_TILE, CONV_COLS), lambda j, i: (i, xcb + j)),
                   pl.BlockSpec((CONV_WIDTH, CONV_COLS), lambda j, i: (0, j)),
                   pl.BlockSpec((1, CONV_COLS), lambda j, i: (0, j))],
        out_shape=[jax.ShapeDtypeStruct(dproj.shape, BF16), jax.ShapeDtypeStruct((CONV_WIDTH, CONV_DIM), F32),
                   jax.ShapeDtypeStruct((1, CONV_DIM), F32)],
        input_output_aliases={6: 0},
        compiler_params=_params(("parallel", "arbitrary"), 4 * _nbytes((CONV_TILE, CONV_COLS), F32)),
    )(proj, pre, pre, dxc, dxc, cw, dproj)


def _ssd_decays(dt_raw, dtb, alog, e_bf, tril_bf):
    dtv = _softplus(dt_raw + dtb)
    a = -jnp.exp(alog)
    cs = _dot_exact_lhs(tril_bf, dtv * a, _NN)
    cs_last = cs[CHUNK - 1:CHUNK, :]
    stack = jnp.concatenate([dtv, jnp.exp(cs), jnp.exp(cs_last - cs)], axis=0)
    full = _head_expand(stack, e_bf)
    return dtv, a, cs, full[:CHUNK], full[CHUNK:2 * CHUNK], full[2 * CHUNK:]


def _split2(x):
    hi = x.astype(BF16)
    return hi, (x - hi.astype(F32)).astype(BF16)


def _head_expand(x, e_bf):
    hi, mid = _split2(x)
    return _dot(hi, e_bf, _NN) + _dot(mid, e_bf, _NN)


def _head_sums(x, e_bf):
    hi, mid = _split2(x)
    return _dot(hi, e_bf, _NT) + _dot(mid, e_bf, _NT)


def _head_mats(cs, cs_t, cb, h, mask):
    seg = cs[:, h:h + 1] - cs_t[h:h + 1, :]
    lmat = jnp.exp(jnp.where(mask, seg, -jnp.inf))
    return lmat, cb * lmat


def _ssd_fwd(xc, proj, dt_raw, dtb, alog, dskip_full, ng, e_bf, name):
    t = xc.shape[0]
    nc = t // CHUNK
    zcb = COL_Z // D_INNER

    def body(xc_ref, z_ref, dt_ref, dtb_ref, alog_ref, dsk_ref, ng_ref, e_ref, y_ref, yb_ref, sprev_ref, s_scr):
        @pl.when(pl.program_id(0) == 0)
        def _():
            s_scr[...] = jnp.zeros_like(s_scr)

        mask = _tri(True)
        tril_bf = mask.astype(BF16)
        e_v = e_ref[...]
        _, _, cs, dt_full, ecs_full, decay_full = _ssd_decays(dt_ref[...], dtb_ref[...], alog_ref[...], e_v, tril_bf)
        cs_t = cs.T
        sprev_ref[0] = s_scr[...]
        for g in range(GROUPS):
            gc = slice(g * GROUP_W, (g + 1) * GROUP_W)
            xs = xc_ref[:, gc]
            xdt = xs * dt_full[:, gc]
            xdt_b = xdt.astype(BF16)
            xdec = (xdt * decay_full[:, gc]).astype(BF16)
            bg = xc_ref[:, D_INNER + g * D_STATE:D_INNER + (g + 1) * D_STATE].astype(BF16)
            cg = xc_ref[:, D_INNER + GROUPS * D_STATE + g * D_STATE:D_INNER + GROUPS * D_STATE + (g + 1) * D_STATE].astype(BF16)
            cb = _dot(cg, bg, _NT)
            s_prev = s_scr[:, gc]
            y_off = ecs_full[:, gc] * _dot(cg, s_prev.astype(BF16), _NN)
            s_scr[:, gc] = s_prev * ecs_full[CHUNK - 1:CHUNK, gc] + _dot(bg, xdec, _TN)
            parts = []
            for r in range(GROUP_W // HEAD_DIM):
                h = g * (GROUP_W // HEAD_DIM) + r
                _, m = _head_mats(cs, cs_t, cb, h, mask)
                parts.append(_dot(m.astype(BF16), xdt_b[:, r * HEAD_DIM:(r + 1) * HEAD_DIM], _NN))
            yg = jnp.concatenate(parts, axis=1) + y_off + dsk_ref[:, gc] * xs
            y_ref[:, gc] = yg
            zv = z_ref[:, gc].astype(F32)
            ygate = yg * (zv * _sigmoid(zv))
            rstd = lax.rsqrt(jnp.mean(ygate * ygate, axis=-1, keepdims=True) + NORM_EPS)
            yb_ref[:, gc] = (ygate * rstd * ng_ref[:, gc]).astype(BF16)

    vec = lambda w: pl.BlockSpec((1, w), lambda i: (0, 0))
    blk = _nbytes((CHUNK, CONV_DIM), F32) + 3 * _nbytes((CHUNK, D_INNER), F32) + _nbytes((D_STATE, D_INNER), F32)
    return pl.pallas_call(
        body, name=name, grid=(nc,),
        in_specs=[pl.BlockSpec((CHUNK, CONV_DIM), lambda i: (i, 0)), pl.BlockSpec((CHUNK, D_INNER), lambda i: (i, zcb)),
                  pl.BlockSpec((CHUNK, DT_PAD), lambda i: (i, 0)), vec(DT_PAD), vec(DT_PAD), vec(D_INNER), vec(D_INNER),
                  pl.BlockSpec((DT_PAD, D_INNER), lambda i: (0, 0))],
        out_specs=[pl.BlockSpec((CHUNK, D_INNER), lambda i: (i, 0)), pl.BlockSpec((CHUNK, D_INNER), lambda i: (i, 0)),
                   pl.BlockSpec((1, D_STATE, D_INNER), lambda i: (i, 0, 0))],
        out_shape=[jax.ShapeDtypeStruct((t, D_INNER), F32), jax.ShapeDtypeStruct((t, D_INNER), BF16),
                   jax.ShapeDtypeStruct((nc, D_STATE, D_INNER), F32)],
        scratch_shapes=[pltpu.VMEM((D_STATE, D_INNER), F32)],
        compiler_params=_params(("arbitrary",), blk),
    )(xc, proj, dt_raw, dtb, alog, dskip_full, ng, e_bf)


def _ssd_bwd(dyb, y, xc, proj, dt_raw, sprev, dtb, alog, dskip_full, ng, e_bf, dproj, name):
    t = xc.shape[0]
    nc = t // CHUNK
    zcb = COL_Z // D_INNER
    hpg = GROUP_W // HEAD_DIM
    rev = lambda i: nc - 1 - i

    def body(dyb_ref, y_ref, xc_ref, z_ref, dt_ref, sprev_ref, dtb_ref, alog_ref, dsk_ref, ng_ref, e_ref, dproj_in,
             dz_ref, dxc_ref, ddt_ref, gng_ref, gdsk_ref, galog_ref, gdtb_ref, ds_scr, sums_scr):
        del dproj_in

        @pl.when(pl.program_id(0) == 0)
        def _():
            ds_scr[...] = jnp.zeros_like(ds_scr)
            gng_ref[...] = jnp.zeros_like(gng_ref)
            gdsk_ref[...] = jnp.zeros_like(gdsk_ref)
            galog_ref[...] = jnp.zeros_like(galog_ref)
            gdtb_ref[...] = jnp.zeros_like(gdtb_ref)

        mask = _tri(True)
        tril_bf = mask.astype(BF16)
        triu_bf = _tri(False).astype(BF16)
        e_v = e_ref[...]
        dt_in = dt_ref[...] + dtb_ref[...]
        dtv, a, cs, dt_full, ecs_full, decay_full = _ssd_decays(dt_ref[...], dtb_ref[...], alog_ref[...], e_v, tril_bf)
        cs_t = cs.T

        lane_h = lax.broadcasted_iota(jnp.int32, (CHUNK, DT_PAD), 1)
        sub_h = lax.broadcasted_iota(jnp.int32, (DT_PAD, CHUNK), 0)
        dcs_rows = jnp.zeros((CHUNK, DT_PAD), F32)
        dcs_cols_t = jnp.zeros((DT_PAD, CHUNK), F32)
        last_cols, dsk_cols = [], []
        for g in range(GROUPS):
            gc = slice(g * GROUP_W, (g + 1) * GROUP_W)
            b_cols = slice(D_INNER + g * D_STATE, D_INNER + (g + 1) * D_STATE)
            c_cols = slice(D_INNER + GROUPS * D_STATE + g * D_STATE, D_INNER + GROUPS * D_STATE + (g + 1) * D_STATE)
            xs = xc_ref[:, gc]
            xdt = xs * dt_full[:, gc]
            xdt_b = xdt.astype(BF16)
            xdec = xdt * decay_full[:, gc]
            xdec_b = xdec.astype(BF16)
            zv = z_ref[:, gc].astype(F32)
            sg = _sigmoid(zv)
            gate = zv * sg
            yv = y_ref[:, gc]
            dybv = dyb_ref[:, gc]
            ygate = yv * gate
            rstd = lax.rsqrt(jnp.mean(ygate * ygate, axis=-1, keepdims=True) + NORM_EPS)
            yn = ygate * rstd
            gng_ref[:, gc] += jnp.sum(dybv * yn, axis=0, keepdims=True)
            dyn = dybv * ng_ref[:, gc]
            dyg = rstd * (dyn - yn * jnp.mean(dyn * yn, axis=-1, keepdims=True))
            dz_ref[:, gc] = (dyg * yv * sg * (1.0 + zv * (1.0 - sg))).astype(BF16)
            dy = dyg * gate
            dy_b = dy.astype(BF16)
            dyo = dy * ecs_full[:, gc]
            dyo_b = dyo.astype(BF16)
            dsk_cols.append(jnp.sum(dy * xs, axis=0, keepdims=True))

            bg = xc_ref[:, b_cols].astype(BF16)
            cg = xc_ref[:, c_cols].astype(BF16)
            s_prev = sprev_ref[0, :, gc]
            s_prev_b = s_prev.astype(BF16)
            dsg = ds_scr[:, gc]
            dsg_b = dsg.astype(BF16)
            cb = _dot(cg, bg, _NT)
            c_s = _dot(cg, s_prev_b, _NN)
            b_ds = _dot(bg, dsg_b, _NN)
            dcb = jnp.zeros((CHUNK, CHUNK), F32)
            parts = []
            for r in range(hpg):
                h = g * hpg + r
                hc = slice(r * HEAD_DIM, (r + 1) * HEAD_DIM)
                lmat, m = _head_mats(cs, cs_t, cb, h, mask)
                dm = _dot(dy_b[:, hc], xdt_b[:, hc], _NT)
                parts.append(_dot(m.astype(BF16), dy_b[:, hc], _TN))
                dcb = dcb + dm * lmat
                w = dm * m
                dcs_rows = jnp.where(lane_h == h, jnp.sum(w, axis=1, keepdims=True), dcs_rows)
                dcs_cols_t = jnp.where(sub_h == h, jnp.sum(w, axis=0, keepdims=True), dcs_cols_t)
            dxdt = jnp.concatenate(parts, axis=1) + decay_full[:, gc] * b_ds
            dcb_b = dcb.astype(BF16)
            dxc_ref[:, c_cols] = _dot(dcb_b, bg, _NN) + _dot(dyo_b, s_prev_b, _NT)
            dxc_ref[:, b_cols] = _dot(dcb_b, cg, _TN) + _dot(xdec_b, dsg_b, _NT)
            cdec = ecs_full[CHUNK - 1:CHUNK, gc]
            ds_scr[:, gc] = _dot(cg, dyo_b, _TN) + cdec * dsg
            dxc_ref[:, gc] = dxdt * dt_full[:, gc] + dsk_ref[:, gc] * dy
            dec_prod = xdec * b_ds
            sums_scr[:CHUNK, gc] = dyo * c_s - dec_prod
            sums_scr[CHUNK:, gc] = dxdt * xs
            last_cols.append(jnp.sum(dec_prod, axis=0, keepdims=True) + cdec * jnp.sum(dsg * s_prev, axis=0, keepdims=True))
        t_sums = _head_sums(sums_scr[...], e_v)
        tail = jnp.concatenate([jnp.concatenate(last_cols, axis=1), jnp.concatenate(dsk_cols, axis=1),
                                jnp.zeros((SUBLANES - 2, D_INNER), F32)], axis=0)
        t_tail = _dot_exact_rhs(tail, e_v, _NT)
        gdsk_ref[...] += t_tail[1:2, :]
        row = lax.broadcasted_iota(jnp.int32, (CHUNK, DT_PAD), 0)
        dcs = dcs_rows - dcs_cols_t.T + t_sums[:CHUNK] + jnp.where(row == CHUNK - 1, t_tail[0:1, :], 0.0)
        dda = _dot_exact_lhs(triu_bf, dcs, _NN)
        galog_ref[...] += jnp.sum(dda * dtv, axis=0, keepdims=True) * a
        ddt = dda * a + t_sums[CHUNK:]
        ddt_raw = jnp.where(lane_h < N_HEADS, ddt * _sigmoid(dt_in), 0.0)
        gdtb_ref[...] += jnp.sum(ddt_raw, axis=0, keepdims=True)
        ddt_ref[...] = ddt_raw.astype(BF16)

    vec = lambda w: pl.BlockSpec((1, w), lambda i: (0, 0))
    blk = (2 * _nbytes((CHUNK, CONV_DIM), F32) + 4 * _nbytes((CHUNK, D_INNER), F32) + 4 * _nbytes((D_STATE, D_INNER), F32))
    return pl.pallas_call(
        body, name=name, grid=(nc,),
        in_specs=[pl.BlockSpec((CHUNK, D_INNER), lambda i: (rev(i), 0)), pl.BlockSpec((CHUNK, D_INNER), lambda i: (rev(i), 0)),
                  pl.BlockSpec((CHUNK, CONV_DIM), lambda i: (rev(i), 0)), pl.BlockSpec((CHUNK, D_INNER), lambda i: (rev(i), zcb)),
                  pl.BlockSpec((CHUNK, DT_PAD), lambda i: (rev(i), 0)), pl.BlockSpec((1, D_STATE, D_INNER), lambda i: (rev(i), 0, 0)),
                  vec(DT_PAD), vec(DT_PAD), vec(D_INNER), vec(D_INNER), pl.BlockSpec((DT_PAD, D_INNER), lambda i: (0, 0)),
                  pl.BlockSpec(memory_space=pl.ANY)],
        out_specs=[pl.BlockSpec((CHUNK, D_INNER), lambda i: (rev(i), zcb)), pl.BlockSpec((CHUNK, CONV_DIM), lambda i: (rev(i), 0)),
                   pl.BlockSpec((CHUNK, DT_PAD), lambda i: (rev(i), 0)), vec(D_INNER), vec(DT_PAD), vec(DT_PAD), vec(DT_PAD)],
        out_shape=[jax.ShapeDtypeStruct(dproj.shape, BF16), jax.ShapeDtypeStruct((t, CONV_DIM), F32),
                   jax.ShapeDtypeStruct((t, DT_PAD), BF16), jax.ShapeDtypeStruct((1, D_INNER), F32),
                   jax.ShapeDtypeStruct((1, DT_PAD), F32), jax.ShapeDtypeStruct((1, DT_PAD), F32),
                   jax.ShapeDtypeStruct((1, DT_PAD), F32)],
        scratch_shapes=[pltpu.VMEM((D_STATE, D_INNER), F32), pltpu.VMEM((2 * CHUNK, D_INNER), F32)],
        input_output_aliases={11: 0},
        compiler_params=_params(("arbitrary",), blk),
    )(dyb, y, xc, proj, dt_raw, sprev, dtb, alog, dskip_full, ng, e_bf, dproj)


def _mesh_pos():
    return lax.axis_index("x"), lax.axis_index("y"), lax.axis_index("c")


def _other_chips(x, y):
    return [(1 - x, y), (x, 1 - y), (1 - x, 1 - y)]


def _all_peers(x, y, c):
    peers = []
    for k in range(1, N_DEV):
        fx, fy, fc = (k >> 2) & 1, (k >> 1) & 1, k & 1
        px, py, pc = x + fx - 2 * x * fx, y + fy - 2 * y * fy, c + fc - 2 * c * fc
        peers.append(((px, py, pc), 4 * px + 2 * py + pc))
    return peers


def _all_gather(shards, name, own_only=()):
    n, n_own = len(shards), len(own_only)

    def body(*refs):
        ins, own_ins = refs[:n], refs[n:n + n_own]
        outs, own_outs = refs[n + n_own:2 * n + n_own], refs[2 * n + n_own:2 * (n + n_own)]
        send_sems, recv_sems, local_sems = refs[2 * (n + n_own):]
        x, y, c = _mesh_pos()
        me, sibling = (x, y, c), (x, y, 1 - c)
        chips = _other_chips(x, y)

        def slot(p):
            return 4 * p[0] + 2 * p[1] + p[2]

        def copy(a, k, block, to, src=None):
            dst = outs[a].at[slot(block)]
            return pltpu.make_async_remote_copy(
                src_ref=dst if src is None else src, dst_ref=dst, send_sem=send_sems.at[a * 7 + k],
                recv_sem=recv_sems.at[a * 7 + k], device_id=to, device_id_type=MESH)

        started = []
        own = []
        for a in range(n_own):
            mine = pltpu.make_async_copy(own_ins[a], own_outs[a].at[slot(me)], local_sems.at[n + a])
            mine.start()
            own.append(mine)
        for a in range(n):
            mine = pltpu.make_async_copy(ins[a], outs[a].at[slot(me)], local_sems.at[a])
            mine.start()
            own.append(mine)
            first = [copy(a, 0, me, sibling, src=ins[a])]
            first += [copy(a, 1 + j, me, (*chip, c), src=ins[a]) for j, chip in enumerate(chips)]
            for cp in first:
                cp.start()
            started += first
        for a in range(n):
            for j, chip in enumerate(chips):
                copy(a, 1 + j, (*chip, c), me).wait_recv()
                fwd = copy(a, 4 + j, (*chip, c), sibling)
                fwd.start()
                started.append(fwd)
        for a in range(n):
            copy(a, 0, sibling, me).wait_recv()
            for j, chip in enumerate(chips):
                copy(a, 4 + j, (*chip, 1 - c), me).wait_recv()
        for cp in started:
            cp.wait_send()
        for mine in own:
            mine.wait()

    return pl.pallas_call(
        body, name=name,
        in_specs=[_HBM] * (n + n_own), out_specs=[_HBM] * (n + n_own),
        out_shape=[jax.ShapeDtypeStruct((N_DEV,) + s.shape, s.dtype) for s in (*shards, *own_only)],
        scratch_shapes=[pltpu.SemaphoreType.DMA((7 * n,)), pltpu.SemaphoreType.DMA((7 * n,)),
                        pltpu.SemaphoreType.DMA((n + n_own,))],
    )(*shards, *own_only)


_SMALL_ROWS = (("norm_mix_g", 8), ("conv_b", 32), ("dt_bias", 1), ("a_log", 1), ("d_skip", 1), ("ssm_norm_g", 16),
               ("v_norm_g", 8), ("v_norm_b", 8), ("w_spatial", 1024), ("b_spatial", 8), ("b_gates", 16), ("norm_mlp_g", 8),
               ("norm_final_g", 8), ("conv_w", 128), ("loss", 1))
_LAST_SMALL = (("norm_mix_g", 8),)


def _packed_rows(table):
    return -(-sum(r for _, r in table) // SUBLANES) * SUBLANES


def _small_offsets(table=_SMALL_ROWS):
    offs, r = {}, 0
    for name, rows in table:
        offs[name] = r
        r += rows
    return offs


def _rows_from(src_ref, dst_ref, r0):
    k, w = src_ref.shape
    if w <= LANES:
        dst_ref[r0:r0 + k, 0:w] = src_ref[...]
        return
    per = w // LANES
    for i in range(k):
        for j in range(per):
            dst_ref[r0 + i * per + j:r0 + i * per + j + 1, :] = src_ref[i:i + 1, j * LANES:(j + 1) * LANES]


def _rows_to(src_ref, r0, dst_ref):
    k, w = dst_ref.shape
    if w <= LANES:
        dst_ref[...] = src_ref[r0:r0 + k, 0:w]
        return
    per = w // LANES
    for i in range(k):
        for j in range(per):
            dst_ref[i:i + 1, j * LANES:(j + 1) * LANES] = src_ref[r0 + i * per + j:r0 + i * per + j + 1, :]


def _pack_small(grads, slot_idx, name):
    names = [n for n, _ in _SMALL_ROWS if n in grads]
    offs = _small_offsets()
    rows = _packed_rows(_SMALL_ROWS)

    def body(slot_ref, *refs):
        del slot_ref
        ins, (packed_ref, land_ref) = refs[:len(names)], refs[len(names):]
        packed_ref[...] = jnp.zeros_like(packed_ref)
        for n, ref in zip(names, ins):
            _rows_from(ref, packed_ref, offs[n])
        land_ref[0] = packed_ref[...]

    whole = lambda shape: pl.BlockSpec(shape, lambda i, slot_ref: (0,) * len(shape))
    grid_spec = pltpu.PrefetchScalarGridSpec(
        num_scalar_prefetch=1, grid=(1,), in_specs=[whole(grads[n].shape) for n in names],
        out_specs=[whole((rows, LANES)), pl.BlockSpec((1, rows, LANES), lambda i, slot_ref: (slot_ref[0], 0, 0))])
    return pl.pallas_call(
        body, name=name, grid_spec=grid_spec,
        out_shape=[jax.ShapeDtypeStruct((rows, LANES), F32), jax.ShapeDtypeStruct((N_DEV, rows, LANES), F32)],
    )(slot_idx, *[grads[n] for n in names])


def _exchange_small(grads, table, name):
    names = [n for n, _ in table]
    offs = _small_offsets(table)
    n_in = len(names)
    packed_rows = _packed_rows(table)

    def body(*refs):
        ins, out_ref = refs[:n_in], refs[n_in]
        packed, send_sems, recv_sems, local_sem = refs[n_in + 1:]
        packed[...] = jnp.zeros_like(packed)
        for n, ref in zip(names, ins):
            _rows_from(ref, packed, offs[n])
        x, y, c = _mesh_pos()
        my_slot = 4 * x + 2 * y + c
        mine = pltpu.make_async_copy(packed, out_ref.at[my_slot], local_sem)
        mine.start()
        copies = []
        for k, (peer, peer_slot) in enumerate(_all_peers(x, y, c)):
            sems = dict(send_sem=send_sems.at[k], recv_sem=recv_sems.at[k], device_id=peer, device_id_type=MESH)
            send = pltpu.make_async_remote_copy(src_ref=packed, dst_ref=out_ref.at[my_slot], **sems)
            send.start()
            copies.append((send, pltpu.make_async_remote_copy(src_ref=packed, dst_ref=out_ref.at[peer_slot], **sems)))
        for send, recv in copies:
            send.wait_send()
            recv.wait_recv()
        mine.wait()

    return pl.pallas_call(
        body, name=name, in_specs=[pl.BlockSpec(memory_space=pltpu.VMEM)] * n_in, out_specs=_HBM,
        out_shape=jax.ShapeDtypeStruct((N_DEV, packed_rows, LANES), F32),
        scratch_shapes=[pltpu.VMEM((packed_rows, LANES), F32), pltpu.SemaphoreType.DMA((N_DEV - 1,)),
                        pltpu.SemaphoreType.DMA((N_DEV - 1,)), pltpu.SemaphoreType.DMA],
    )(*[grads[n] for n in names])


def _swap_with_sibling(grads, name):
    n = len(grads)

    def body(*refs):
        ins, outs = refs[:n], refs[n:2 * n]
        send_sems, recv_sems = refs[2 * n:]
        x, y, c = _mesh_pos()
        copies = []
        for a in range(n):
            for k in range(N_CHIP):
                cp = pltpu.make_async_remote_copy(
                    src_ref=ins[a].at[(1 - c) + 2 * k], dst_ref=outs[a].at[k], send_sem=send_sems.at[a * N_CHIP + k],
                    recv_sem=recv_sems.at[a * N_CHIP + k], device_id=(x, y, 1 - c), device_id_type=MESH)
                cp.start()
                copies.append(cp)
        for cp in copies:
            cp.wait()

    return pl.pallas_call(
        body, name=name, in_specs=[_HBM] * n, out_specs=[_HBM] * n,
        out_shape=[jax.ShapeDtypeStruct((N_CHIP,) + g.shape[1:], g.dtype) for g in grads],
        scratch_shapes=[pltpu.SemaphoreType.DMA((N_CHIP * n,)), pltpu.SemaphoreType.DMA((N_CHIP * n,))],
    )(*grads)


_SEM = pl.BlockSpec(memory_space=pltpu.SEMAPHORE)
_IN_HBM = pl.BlockSpec(memory_space=pltpu.HBM)
_EFFECT = pltpu.SideEffectType.DATAFLOW_SIDE_EFFECTING


def _in_hbm(a):
    return pltpu.with_memory_space_constraint(a, pltpu.HBM)


def _gather_copies(ins, lands, send_sems, recv_sems):
    x, y, c = _mesh_pos()
    my_slot = 4 * x + 2 * y + c
    pairs = []
    for a in range(len(ins)):
        for k, (peer, peer_slot) in enumerate(_all_peers(x, y, c)):
            sems = dict(send_sem=send_sems.at[a * (N_DEV - 1) + k], recv_sem=recv_sems.at[a * (N_DEV - 1) + k],
                        device_id=peer, device_id_type=MESH)
            pairs.append((pltpu.make_async_remote_copy(src_ref=ins[a], dst_ref=lands[a].at[my_slot], **sems),
                          pltpu.make_async_remote_copy(src_ref=ins[a], dst_ref=lands[a].at[peer_slot], **sems)))
    return pairs


def _scatter_copies(ins, lands, send_sems, recv_sems):
    x, y, c = _mesh_pos()
    my_chip = 2 * x + y
    pairs = []
    for a in range(len(ins)):
        for j, chip in enumerate(_other_chips(x, y)):
            there = 2 * chip[0] + chip[1]
            sems = dict(send_sem=send_sems.at[a * 3 + j], recv_sem=recv_sems.at[a * 3 + j],
                        device_id=(*chip, c), device_id_type=MESH)
            pairs.append((pltpu.make_async_remote_copy(src_ref=ins[a].at[there], dst_ref=lands[a].at[my_chip], **sems),
                          pltpu.make_async_remote_copy(src_ref=ins[a].at[my_chip], dst_ref=lands[a].at[there], **sems)))
    return pairs


def _split_start(srcs, lands, copies, per_array, name):
    n = len(srcs)

    def body(*refs):
        ins, land_refs = refs[:n], refs[n:2 * n]
        send_sems, recv_sems = refs[2 * n], refs[2 * n + 1]
        token = refs[-1]
        for send, _ in copies(ins, land_refs, send_sems, recv_sems):
            send.start()
        token[...] = jnp.zeros_like(token)

    outs = pl.pallas_call(
        body, name=name,
        out_shape=(pltpu.SemaphoreType.DMA((per_array * n,)), pltpu.SemaphoreType.DMA((per_array * n,)),
                   *[pltpu.HBM(s.shape, s.dtype) for s in srcs], *[pltpu.HBM(l.shape, l.dtype) for l in lands],
                   jax.ShapeDtypeStruct((SUBLANES, LANES), F32)),
        in_specs=[_IN_HBM] * (2 * n),
        out_specs=(_SEM, _SEM, *[_IN_HBM] * (2 * n), pl.BlockSpec(memory_space=pltpu.VMEM)),
        input_output_aliases={i: 2 + i for i in range(2 * n)},
        compiler_params=pltpu.CompilerParams(has_side_effects=_EFFECT),
    )(*[_in_hbm(s) for s in srcs], *[_in_hbm(l) for l in lands])
    return outs[0], outs[1], list(outs[2:2 + n]), list(outs[2 + n:2 + 2 * n]), outs[-1]


def _split_wait(started, copies, after, name):
    send_sems, recv_sems, srcs, lands, _ = started
    n = len(srcs)

    def body(*refs):
        ins, land_refs = refs[:n], refs[n:2 * n]
        for send, recv in copies(ins, land_refs, refs[2 * n], refs[2 * n + 1]):
            send.wait_send()
            recv.wait_recv()

    outs = pl.pallas_call(
        body, name=name,
        out_shape=(*[pltpu.HBM(s.shape, s.dtype) for s in srcs], *[pltpu.HBM(l.shape, l.dtype) for l in lands]),
        in_specs=[_IN_HBM] * (2 * n) + [_SEM, _SEM, _HBM],
        out_specs=[_IN_HBM] * (2 * n),
        input_output_aliases={i: i for i in range(2 * n)},
        compiler_params=pltpu.CompilerParams(has_side_effects=_EFFECT),
    )(*srcs, *lands, send_sems, recv_sems, after)
    return list(outs[:n]), list(outs[n:])


def _ew_block(rows, cols, slots):
    budget = 2 * 1024 * 1024
    br, bc = rows, cols
    while slots * br * bc * 4 > budget:
        if br % 2 == 0 and (br // 2) % (2 * SUBLANES) == 0:
            br //= 2
        elif bc % 2 == 0 and (bc // 2) % LANES == 0:
            bc //= 2
        else:
            break
    return br, bc


def _add_sibling(grads, recv, c_idx, name):
    _, rows, cols = grads.shape
    br, bc = _ew_block(rows, cols, 3)

    def body(c_ref, g_ref, r_ref, out_ref):
        del c_ref
        out_ref[...] = (g_ref[...].astype(F32) + r_ref[...].astype(F32)).astype(out_ref.dtype)

    grid_spec = pltpu.PrefetchScalarGridSpec(
        num_scalar_prefetch=1, grid=(N_CHIP, rows // br, cols // bc),
        in_specs=[pl.BlockSpec((1, br, bc), lambda k, i, j, c_ref: (c_ref[0] + 2 * k, i, j)),
                  pl.BlockSpec((1, br, bc), lambda k, i, j, c_ref: (k, i, j))],
        out_specs=pl.BlockSpec((1, br, bc), lambda k, i, j, c_ref: (k, i, j)))
    return pl.pallas_call(
        body, name=name, grid_spec=grid_spec, out_shape=jax.ShapeDtypeStruct((N_CHIP, rows, cols), grads.dtype),
        compiler_params=_params(("parallel", "parallel", "parallel"), 3 * _nbytes((br, bc), F32)),
    )(c_idx, grads, recv)


def _adam_math(g, w, m, v):
    m2 = ADAM_B1 * m + (1.0 - ADAM_B1) * g
    v2 = ADAM_B2 * v + (1.0 - ADAM_B2) * (g * g)
    m_hat = m2 * (1.0 / (1.0 - ADAM_B1 ** ADAM_STEP))
    v_hat = v2 * (1.0 / (1.0 - ADAM_B2 ** ADAM_STEP))
    return -ADAM_LR * (m_hat / (jnp.sqrt(v_hat) + ADAM_EPS) + ADAM_WD * w), m2, v2


def _adamw(slots, w, m, v, name, own=None, own_slot=None):
    ns, rows, cols = slots.shape
    br, bc = _ew_block(rows, cols, 2 * ns + 7)

    def update(g, w_ref, m_ref, v_ref, g_ref, d_ref, m2_ref, v2_ref):
        g_ref[...] = g
        d_ref[...], m2_ref[...], v2_ref[...] = _adam_math(g, w_ref[...], m_ref[...], v_ref[...])

    out_shape = [jax.ShapeDtypeStruct((rows, cols), F32)] * 4
    params = _params(("parallel", "parallel"), (2 * ns + 7) * _nbytes((br, bc), F32))
    grid = (rows // br, cols // bc)
    if own is None:
        def body(s_ref, *rest):
            g = s_ref[0].astype(F32)
            for k in range(1, ns):
                g = g + s_ref[k].astype(F32)
            update(g, *rest)

        blk = pl.BlockSpec((br, bc), lambda i, j: (i, j))
        return pl.pallas_call(
            body, name=name, grid=grid,
            in_specs=[pl.BlockSpec((ns, br, bc), lambda i, j: (0, i, j)), blk, blk, blk], out_specs=[blk] * 4,
            out_shape=out_shape, compiler_params=params,
        )(slots, w, m, v)

    def body_own(slot_ref, s_ref, o_ref, *rest):
        g = None
        for k in range(ns):
            term = jnp.where(slot_ref[0] == k, o_ref[k].astype(F32), s_ref[k].astype(F32))
            g = term if g is None else g + term
        update(g, *rest)

    blk = pl.BlockSpec((br, bc), lambda i, j, slot_ref: (i, j))
    stack = pl.BlockSpec((ns, br, bc), lambda i, j, slot_ref: (0, i, j))
    grid_spec = pltpu.PrefetchScalarGridSpec(num_scalar_prefetch=1, grid=grid, in_specs=[stack, stack, blk, blk, blk],
                                             out_specs=[blk] * 4)
    return pl.pallas_call(body_own, name=name, grid_spec=grid_spec, out_shape=out_shape, compiler_params=params,
                          )(own_slot, slots, own, w, m, v)


def _adamw_small(all_g, last_g, params, extra_shapes, name):
    names = [n for n, _ in _SMALL_ROWS if n in params]
    extras = [n for n, _ in _SMALL_ROWS if n not in params]
    offs = _small_offsets()
    n_p = len(names)

    def body(*refs):
        s_ref, last_ref = refs[0], refs[1]
        wmv = refs[2:2 + 3 * n_p]
        outs = refs[2 + 3 * n_p:2 + 7 * n_p]
        extra_refs = refs[2 + 7 * n_p:2 + 7 * n_p + len(extras)]
        summed = refs[-1]
        g, g_last = s_ref[0], last_ref[0]
        for k in range(1, N_DEV):
            g, g_last = g + s_ref[k], g_last + last_ref[k]
        summed[...] = g
        last_offs = _small_offsets(_LAST_SMALL)
        for n, rows in _LAST_SMALL:
            summed[offs[n]:offs[n] + rows, :] = g_last[last_offs[n]:last_offs[n] + rows, :]
        for i, n in enumerate(names):
            w_ref, m_ref, v_ref = wmv[3 * i:3 * i + 3]
            g_ref, d_ref, m2_ref, v2_ref = outs[4 * i:4 * i + 4]
            _rows_to(summed, offs[n], g_ref)
            d_ref[...], m2_ref[...], v2_ref[...] = _adam_math(g_ref[...], w_ref[...], m_ref[...], v_ref[...])
        for n, ref in zip(extras, extra_refs):
            _rows_to(summed, offs[n], ref)

    flat = [a for n in names for a in params[n]]
    out_shape = [jax.ShapeDtypeStruct(params[n][0].shape, F32) for n in names for _ in range(4)]
    out_shape += [jax.ShapeDtypeStruct(s, F32) for s in extra_shapes]
    vmem = pl.BlockSpec(memory_space=pltpu.VMEM)
    res = pl.pallas_call(
        body, name=name, in_specs=[vmem] * (2 + len(flat)), out_specs=[vmem] * len(out_shape), out_shape=out_shape,
        scratch_shapes=[pltpu.VMEM(all_g.shape[1:], F32)],
        compiler_params=pltpu.CompilerParams(vmem_limit_bytes=_vmem_limit(_nbytes(all_g.shape, F32))),
    )(all_g, last_g, *flat)
    return {n: res[4 * i:4 * i + 4] for i, n in enumerate(names)}, res[4 * n_p:]


def _mm_tiles(mode, m, n, k):
    tn = min(n, 1024)
    if mode == "tn":
        return min(m, 1024), tn, min(k, 4096)
    if k <= 1024:
        return min(m, 2048), tn, k
    if k <= 2048:
        return min(m, 1024), tn, k
    if k <= 4096:
        return min(m, 512), tn, k
    return min(m, 1024), tn, 2048


def _local_step(x, target, wts, small, exchange):
    t = x.shape[0]
    w_main_t, w_dt_t = wts["w_main_t"], wts["w_dt_t"]
    bsp_t = small["b_spatial"].T
    pad32 = lambda a: jnp.pad(a, ((0, 0), (0, DT_PAD - N_HEADS)))
    dtb, alog = pad32(small["dt_bias"]), pad32(small["a_log"])
    dskip_full = jnp.repeat(small["d_skip"], HEAD_DIM, axis=1)
    head_of_col = lax.broadcasted_iota(jnp.int32, (DT_PAD, D_INNER), 1) // HEAD_DIM
    e_bf = (head_of_col == lax.broadcasted_iota(jnp.int32, (DT_PAD, D_INNER), 0)).astype(BF16)

    def mm(a, b, mode, name, **kw):
        if mode == "nn":
            m, k, n = a.shape[0], a.shape[1], b.shape[1]
        elif mode == "nt":
            m, k, n = a.shape[0], a.shape[1], b.shape[0]
        else:
            m, k, n = a.shape[1], a.shape[0], b.shape[1]
        tm, tn, tk = _mm_tiles(mode, m, n, k)
        tm = min(tm, kw.pop("max_tm", tm))
        kw.setdefault("out_dtypes", (BF16,) if mode == "tn" else (F32,))
        if "extra_specs" in kw:
            kw["extra_specs"] = kw["extra_specs"](tm, tn)
        return _matmul(a, b, mode=mode, tm=tm, tn=tn, tk=tk, name=name, **kw)

    def out_tile(tm, tn):
        return (((tm, tn), lambda i, j: (i, j)),)

    def row_tiles(n_tiles, *vectors, gate_logits=False):
        def specs(tm, tn):
            out = [((tm, tn), lambda i, j: (i, j))] * n_tiles
            if gate_logits:
                out += [((tm, D_MODEL), lambda i, j, cb=COL_GATE // D_MODEL + half: (i, cb)) for half in range(2)]
            return tuple(out) + tuple(((1, w), lambda i, j, cb=cb: (0, cb)) for w, cb in vectors)
        return specs

    vec = lambda w: ((1, w), F32, (1, w), lambda i, j: (0, 0))
    fused_tm = 512

    h, dt_raw = _rms_fwd(x, small["norm_mix_g"], w_dt_t, "rms_mix", deps=exchange.begin())
    proj = mm(h, w_main_t, "nt", "proj_main", j_outer=True, out_dtypes=(BF16,))
    y_a = _gmlp_fwd(proj, small["v_norm_g"], small["v_norm_b"], small["w_spatial"], bsp_t, "gmlp_fwd")
    pre_conv, xc = _conv_fwd(proj, wts["conv_w"], small["conv_b"], "conv_fwd")
    y_ssd, y_b, sprev = _ssd_fwd(xc, proj, dt_raw, dtb, alog, dskip_full, small["ssm_norm_g"], e_bf, "ssd_fwd")
    wts = {**wts, **exchange.late_weights(y_b)}
    pa = mm(y_a, wts["w_proj_a"], "nn", "proj_a")
    pb, merged = mm(y_b, wts["w_proj_b"], "nn", "proj_b", epilogue=_merge_epilogue, out_dtypes=(F32, BF16), max_tm=fused_tm,
                    extras=(pa, proj, proj, small["b_gates"], small["b_gates"]),
                    extra_specs=row_tiles(1, (D_MODEL, 0), (D_MODEL, 1), gate_logits=True))
    x1, h2 = mm(merged, wts["w_out"], "nn", "out_proj", epilogue=_residual_rms_epilogue, out_dtypes=(F32, BF16),
                max_tm=2 * fused_tm, extras=(x, small["norm_mlp_g"]), extra_specs=row_tiles(1, (D_MODEL, 0)))

    def relu_sq(acc, ex, outs):
        r = jnp.maximum(acc, 0.0)
        outs[0][...] = (r * r).astype(BF16)

    act = mm(h2, wts["w_mlp_up"], "nn", "mlp_up", epilogue=relu_sq, out_dtypes=(BF16,), j_outer=True)
    dx2, dx2_b, g_final, _, loss = mm(
        act, wts["w_mlp_down"], "nn", "mlp_down", epilogue=_loss_epilogue, carry=True,
        out_dtypes=(F32, BF16, vec(D_MODEL), vec(D_MODEL), vec(LANES)),
        extras=(x1, small["norm_final_g"], target), extra_specs=lambda tm, tn: (
            ((tm, tn), lambda i, j: (i, j)), ((1, tn), lambda i, j: (0, 0)), ((tm, tn), lambda i, j: (i, j))))

    def relu_sq_bwd(acc, ex, outs):
        outs[0][...] = (acc * 2.0 * jnp.sqrt(ex[0][...].astype(F32))).astype(BF16)

    dup = mm(dx2_b, wts["w_mlp_down"], "nt", "d_act", epilogue=relu_sq_bwd, extras=(act,), extra_specs=out_tile,
             out_dtypes=(BF16,), j_outer=True)
    g_down = mm(act, dx2_b, "tn", "g_mlp_down")
    g_up = mm(h2, dup, "tn", "g_mlp_up")
    dx1, dx1_b, g_mlp = mm(
        dup, wts["w_mlp_up"], "nt", "d_h2", epilogue=_rms_bwd_epilogue, carry=True,
        out_dtypes=(F32, BF16, vec(D_MODEL)), extras=(x1, small["norm_mlp_g"], dx2), extra_specs=lambda tm, tn: (
            ((tm, tn), lambda i, j: (i, j)), ((1, tn), lambda i, j: (0, 0)), ((tm, tn), lambda i, j: (i, j))))

    g_out = mm(merged, dx1_b, "tn", "g_out")
    dpa, dpb, dproj, g_bgates = mm(
        dx1_b, wts["w_out"], "nt", "d_merged", epilogue=_merge_bwd_epilogue, carry=True, max_tm=fused_tm,
        out_dtypes=(BF16, BF16, ((t, MAIN_W), BF16, (fused_tm, 2 * D_MODEL), lambda i, j: (i, COL_GATE // (2 * D_MODEL))),
                    vec(2 * D_MODEL)),
        extras=(pa, pb, proj, proj, small["b_gates"], small["b_gates"]),
        extra_specs=row_tiles(2, (D_MODEL, 0), (D_MODEL, 1), gate_logits=True))
    g_pa = mm(y_a, dpa, "tn", "g_proj_a")
    g_pb = mm(y_b, dpb, "tn", "g_proj_b")
    started = exchange.reduce("late", {"w_mlp_down": g_down, "w_mlp_up": g_up, "w_out": g_out, "w_proj_a": g_pa,
                                       "w_proj_b": g_pb})
    dya = mm(dpa, wts["w_proj_a"], "nt", "d_ya", deps=started)
    dyb = mm(dpb, wts["w_proj_b"], "nt", "d_yb")

    dproj, g_wsp, g_bsp_t, g_vg, g_vb = _gmlp_bwd(proj, dya, small["v_norm_g"], small["v_norm_b"], small["w_spatial"],
                                                   bsp_t, dproj, "gmlp_bwd")
    dproj, dxc, ddt, g_ng, g_dskip, g_alog, g_dtb = _ssd_bwd(dyb, y_ssd, xc, proj, dt_raw, sprev, dtb, alog, dskip_full,
                                                             small["ssm_norm_g"], e_bf, dproj, "ssd_bwd")
    dproj, g_convw, g_convb = _conv_bwd(proj, pre_conv, dxc, wts["conv_w"], dproj, "conv_bwd")

    small_grads = {
        "conv_w": g_convw, "loss": loss,
        "conv_b": g_convb, "dt_bias": g_dtb, "a_log": g_alog, "d_skip": g_dskip, "ssm_norm_g": g_ng,
        "v_norm_g": g_vg, "v_norm_b": g_vb, "w_spatial": g_wsp.reshape(GROUPS * CHUNK, CHUNK), "b_spatial": g_bsp_t.T,
        "b_gates": g_bgates, "norm_mlp_g": g_mlp, "norm_final_g": g_final,
    }
    g_main_t = mm(dproj, h, "tn", "g_in_main", deps=exchange.small(small_grads))
    g_dt_t = mm(ddt, h, "tn", "g_in_dt")
    started = exchange.reduce("in", {"w_in": (g_main_t, g_dt_t)})

    def input_grad(acc, ex, outs):
        x_ref, g_ref, res_ref, ddt_ref, wdt_ref = ex
        gg = jnp.zeros((1, D_MODEL), F32)
        for r in range(acc.shape[0] // ROW_TILE):
            rows = slice(r * ROW_TILE, (r + 1) * ROW_TILE)
            dh = acc[rows] + _dot(ddt_ref[rows, :], wdt_ref[...], _NN)
            dx, gg_r = _rms_pullback(x_ref[rows, :], g_ref[...], dh)
            outs[0][rows, :] = dx + res_ref[rows, :]
            gg = gg + gg_r

        @pl.when(_first_row_tile())
        def _():
            outs[1][...] = jnp.zeros_like(outs[1])

        outs[1][...] += gg

    grad_x, g_mix = mm(
        dproj, w_main_t, "nn", "d_h", epilogue=input_grad, deps=started, carry=True,
        out_dtypes=(F32, vec(D_MODEL)), extras=(x, small["norm_mix_g"], dx1, ddt, w_dt_t), extra_specs=lambda tm, tn: (
            ((tm, tn), lambda i, j: (i, j)), ((1, tn), lambda i, j: (0, 0)), ((tm, tn), lambda i, j: (i, j)),
            ((tm, DT_PAD), lambda i, j: (i, 0)), ((DT_PAD, D_MODEL), lambda i, j: (0, 0))))

    return grad_x, g_mix


SHARD_ROWS = (MAIN_W + N_HEADS) // N_DEV
REGROUP_IN = 2048


def _main_rows_of(gathered, name):
    n_dev, shard, d = gathered.shape
    blk = 1024
    nb = MAIN_W // blk

    def first_feature(b):
        return b * blk + (N_HEADS if b * blk >= COL_GATE else 0)

    def body(a_ref, b_ref, out_ref):
        for b in range(nb):
            s0, r0 = divmod(first_feature(b), shard)
            n1 = min(shard - r0, blk)

            @pl.when(pl.program_id(0) == b)
            def _(r0=r0, n1=n1):
                out_ref[0:n1, :] = a_ref[0, r0:r0 + n1, :]
                if n1 < blk:
                    out_ref[n1:blk, :] = b_ref[0, 0:blk - n1, :]

    def slot(b):
        return (b * blk + jnp.where(b * blk >= COL_GATE, N_HEADS, 0)) // shard

    return pl.pallas_call(
        body, name=name, grid=(nb,),
        in_specs=[pl.BlockSpec((1, shard, d), lambda b: (slot(b), 0, 0)),
                  pl.BlockSpec((1, shard, d), lambda b: (jnp.minimum(slot(b) + 1, n_dev - 1), 0, 0))],
        out_specs=pl.BlockSpec((blk, d), lambda b: (b, 0)),
        out_shape=jax.ShapeDtypeStruct((MAIN_W, d), gathered.dtype),
        compiler_params=_params(("parallel",), 3 * _nbytes((shard, d), gathered.dtype)),
    )(gathered, gathered)


def _by_device_rows(g_main_t, g_dt_t, name):
    d = g_main_t.shape[1]
    n_blocks = MAIN_W // REGROUP_IN
    dt_dev, dt_row = divmod(COL_GATE, SHARD_ROWS)

    def main_start(s):
        return s * SHARD_ROWS - (N_HEADS if s > dt_dev else 0)

    def body(a_ref, b_ref, dt_ref, out_ref):
        for s in range(N_DEV):
            m0 = main_start(s)
            k0, off = divmod(m0, REGROUP_IN)
            pieces = []
            if s == dt_dev:
                pieces = [(0, dt_row, m0), (dt_row, N_HEADS, None), (dt_row + N_HEADS, SHARD_ROWS - dt_row - N_HEADS, m0 + dt_row)]
            else:
                pieces = [(0, SHARD_ROWS, m0)]

            @pl.when(pl.program_id(0) == s)
            def _(pieces=pieces, k0=k0):
                for dst, n, src in pieces:
                    if src is None:
                        out_ref[0, dst:dst + n, :] = dt_ref[0:n, :]
                        continue
                    lo = src - k0 * REGROUP_IN
                    n_a = max(0, min(n, REGROUP_IN - lo))
                    if n_a:
                        out_ref[0, dst:dst + n_a, :] = a_ref[lo:lo + n_a, :]
                    if n_a < n:
                        lo_b = max(lo - REGROUP_IN, 0)
                        out_ref[0, dst + n_a:dst + n, :] = b_ref[lo_b:lo_b + n - n_a, :]

    def first_block(s):
        return (s * SHARD_ROWS - jnp.where(s > dt_dev, N_HEADS, 0)) // REGROUP_IN

    return pl.pallas_call(
        body, name=name, grid=(N_DEV,),
        in_specs=[pl.BlockSpec((REGROUP_IN, d), lambda s: (first_block(s), 0)),
                  pl.BlockSpec((REGROUP_IN, d), lambda s: (jnp.minimum(first_block(s) + 1, n_blocks - 1), 0)),
                  pl.BlockSpec((DT_PAD, d), lambda s: (0, 0))],
        out_specs=pl.BlockSpec((1, SHARD_ROWS, d), lambda s: (s, 0, 0)),
        out_shape=jax.ShapeDtypeStruct((N_DEV, SHARD_ROWS, d), g_main_t.dtype),
        compiler_params=_params(("parallel",), 3 * _nbytes((REGROUP_IN, d), g_main_t.dtype)),
    )(g_main_t, g_main_t, g_dt_t)


_LATE = ["w_proj_a", "w_proj_b", "w_out", "w_mlp_up", "w_mlp_down"]
_BY_COLS = ("w_mlp_up",)


class _Exchange:
    def __init__(self, late_shards, late_lands):
        self.late_shards, self.late_lands = late_shards, late_lands
        self.c_idx = lax.axis_index("c").astype(jnp.int32).reshape(1)
        self.chip_idx = (2 * lax.axis_index("x") + lax.axis_index("y")).astype(jnp.int32).reshape(1)
        self.pending = []

    def begin(self):
        self.late = _split_start(self.late_shards, self.late_lands, _gather_copies, N_DEV - 1, "gather_late_start")
        return [self.late[-1]]

    def late_weights(self, after):
        _, lands = _split_wait(self.late, _gather_copies, after, "gather_late_wait")
        whole = {}
        for n, g in zip(_LATE, lands):
            whole[n] = jnp.transpose(g, (1, 0, 2)).reshape(g.shape[1], -1) if n in _BY_COLS else g.reshape(-1, g.shape[2])
        return whole

    def reduce(self, tag, grads):
        names = list(grads)
        by_dev = []
        for n in names:
            g = grads[n]
            if n == "w_in":
                by_dev.append(_by_device_rows(*g, "regroup_g_in"))
            elif n in _BY_COLS:
                by_dev.append(jnp.transpose(g.reshape(g.shape[0], N_DEV, -1), (1, 0, 2)))
            else:
                by_dev.append(g.reshape(N_DEV, -1, g.shape[1]))
        from_sibling = _swap_with_sibling(by_dev, "reduce_cores_" + tag)
        parts = [_add_sibling(g, r, self.c_idx, "add_cores_" + n) for n, g, r in zip(names, by_dev, from_sibling)]
        lands = [lax.empty(p.shape, p.dtype) for p in parts]
        started = _split_start(parts, lands, _scatter_copies, 3, "reduce_chips_start_" + tag)
        self.pending.append((tag, names, started))
        return [started[-1]]

    def small(self, grads):
        dev = 2 * self.chip_idx + self.c_idx
        packed, land = _pack_small(grads, dev, "pack_small")
        self.small_started = _split_start([packed], [land], _gather_copies, N_DEV - 1, "exchange_small_start")
        return [self.small_started[-1]]

    def finish(self, after):
        _, (all_small,) = _split_wait(self.small_started, _gather_copies, after, "exchange_small_wait")
        done = {}
        for tag, names, started in self.pending:
            parts, lands = _split_wait(started, _scatter_copies, after, "reduce_chips_wait_" + tag)
            for n, land, part in zip(names, lands, parts):
                done[n] = (land, part, self.chip_idx)
        return all_small, done


def kernel(x, norm_mix_g, w_in, conv_w, conv_b, dt_bias, a_log, d_skip, ssm_norm_g, v_norm_g, v_norm_b, w_spatial, b_spatial, b_gates, w_proj_a, w_proj_b, w_out, norm_mlp_g, w_mlp_up, w_mlp_down, norm_final_g, loss_target, m_norm_mix_g, m_w_in, m_conv_w, m_conv_b, m_dt_bias, m_a_log, m_d_skip, m_ssm_norm_g, m_v_norm_g, m_v_norm_b, m_w_spatial, m_b_spatial, m_b_gates, m_w_proj_a, m_w_proj_b, m_w_out, m_norm_mlp_g, m_w_mlp_up, m_w_mlp_down, m_norm_final_g, v_norm_mix_g, v_w_in, v_conv_w, v_conv_b, v_dt_bias, v_a_log, v_d_skip, v_ssm_norm_g, v_v_norm_g, v_v_norm_b, v_w_spatial, v_b_spatial, v_b_gates, v_w_proj_a, v_w_proj_b, v_w_out, v_norm_mlp_g, v_w_mlp_up, v_w_mlp_down, v_norm_final_g):
    given = dict(locals())
    names = ["norm_mix_g", "w_in", "conv_w", "conv_b", "dt_bias", "a_log", "d_skip", "ssm_norm_g", "v_norm_g", "v_norm_b",
             "w_spatial", "b_spatial", "b_gates", "w_proj_a", "w_proj_b", "w_out", "norm_mlp_g", "w_mlp_up", "w_mlp_down",
             "norm_final_g"]
    shapes = {n: given[n].shape for n in names}
    dev = 4 * lax.axis_index("x") + 2 * lax.axis_index("y") + lax.axis_index("c")

    shard2d = {"w_in": w_in[0].T, "w_proj_a": w_proj_a[0], "w_proj_b": w_proj_b[0], "w_out": w_out[0],
               "w_mlp_up": w_mlp_up[0], "w_mlp_down": w_mlp_down[0]}
    conv_shard = conv_w.reshape(CONV_WIDTH, -1)
    late_shards = [shard2d[n].astype(BF16) for n in _LATE]
    w_in_all, conv_all, *late_lands = _all_gather([shard2d["w_in"].astype(BF16), conv_shard], "gather_first",
                                                  own_only=late_shards)
    dt_dev, dt_row = divmod(COL_GATE, SHARD_ROWS)
    w_dt_t = jnp.pad(w_in_all[dt_dev, dt_row:dt_row + N_HEADS], ((0, DT_PAD - N_HEADS), (0, 0)))
    wts = {"w_main_t": _main_rows_of(w_in_all, "regroup_w_in"), "w_dt_t": w_dt_t, "conv_w": jnp.transpose(conv_all, (1, 0, 2)).reshape(CONV_WIDTH, -1)}
    small = {"norm_mix_g": norm_mix_g, "conv_b": conv_b, "dt_bias": dt_bias, "a_log": a_log, "d_skip": d_skip,
             "ssm_norm_g": ssm_norm_g, "v_norm_g": v_norm_g, "v_norm_b": v_norm_b, "w_spatial": w_spatial[0],
             "b_spatial": b_spatial[0], "b_gates": b_gates, "norm_mlp_g": norm_mlp_g,
             "norm_final_g": norm_final_g.reshape(1, -1)}

    exchange = _Exchange(late_shards, late_lands)
    grad_x, g_mix = _local_step(x[0], loss_target[0], wts, small, exchange)

    out = {}
    all_small, large = exchange.finish(grad_x)
    for n, (slots, own, own_slot) in large.items():
        moments = [given["m_" + n][0], given["v_" + n][0]]
        if n == "w_in":
            moments = [mom.T for mom in moments]
        res = _adamw(slots, shard2d[n], *moments, "adamw_" + n, own=own, own_slot=own_slot)
        out[n] = [(r.T if n == "w_in" else r).reshape(shapes[n]) for r in res]

    last_small = _exchange_small({"norm_mix_g": g_mix}, _LAST_SMALL, "exchange_last")
    small["w_spatial"] = small["w_spatial"].reshape(GROUPS * CHUNK, CHUNK)
    params = {n: (w2d, given["m_" + n].reshape(w2d.shape), given["v_" + n].reshape(w2d.shape)) for n, w2d in small.items()}
    updated, (g_conv_full, loss_all) = _adamw_small(all_small, last_small, params, [(CONV_WIDTH, CONV_DIM), (1, LANES)],
                                                    "adamw_small")
    for n, res in updated.items():
        out[n] = [r.reshape(shapes[n]) for r in res]
    width = shapes["conv_w"][-1]
    g_conv = lax.dynamic_slice(g_conv_full, (0, dev * width), (CONV_WIDTH, width))
    res = _adamw(g_conv[None], conv_shard, m_conv_w.reshape(CONV_WIDTH, -1), v_conv_w.reshape(CONV_WIDTH, -1), "adamw_conv_w")
    out["conv_w"] = [r.reshape(shapes["conv_w"]) for r in res]

    loss = loss_all[0, 0]
    return (loss, grad_x[None], *[out[n][0] for n in names], *[out[n][1] for n in names],
            *[out[n][2] for n in names], *[out[n][3] for n in names])
```

```python
import functools
import math

import jax
import jax.numpy as jnp
from jax import lax
from jax.experimental import pallas as pl
from jax.experimental.pallas import tpu as pltpu

F32 = jnp.float32
BF16 = jnp.bfloat16
MESH = pl.DeviceIdType.MESH

D_MODEL = 1024
NORM_EPS = 1e-6
CHUNK = 128
GROUPS = 8
D_INNER = 2048
HEAD_DIM = 64
N_HEADS = 32
D_STATE = 128
CONV_WIDTH = 4
CONV_DIM = 4096
D_FF = 4096
GROUP_W = D_INNER // GROUPS
N_DEV = 8
N_CHIP = 4

ADAM_LR = 0.001
ADAM_B1 = 0.9
ADAM_B2 = 0.999
ADAM_EPS = 1e-08
ADAM_WD = 0.01
ADAM_STEP = 10

MAIN_W = 2 * D_MODEL + D_INNER + CONV_DIM + 2 * D_MODEL
COL_Z = 2048
COL_XBC = 4096
COL_GATE = 8192
DT_PAD = 128

LANES = 128
SUBLANES = 8
VMEM_BYTES_V7X = 64 * 1024 * 1024
VMEM_BODY_TEMP = 24 * 1024 * 1024


def _vmem_limit(block_bytes):
    return int(min(2 * block_bytes + VMEM_BODY_TEMP, VMEM_BYTES_V7X - 8 * 1024 * 1024))


def _nbytes(shape, dtype):
    return math.prod(shape) * jnp.dtype(dtype).itemsize


_HBM = pl.BlockSpec(memory_space=pl.ANY)


def _params(sem, block_bytes):
    return pltpu.CompilerParams(dimension_semantics=sem, vmem_limit_bytes=_vmem_limit(block_bytes))


def _sigmoid(x):
    return 1.0 / (1.0 + jnp.exp(-x))


def _softplus(x):
    e = jnp.exp(-jnp.abs(x))
    u = 1.0 + e
    log1p_e = jnp.where(u == 1.0, e, jnp.log(u) * (e / jnp.where(u == 1.0, 1.0, u - 1.0)))
    return jnp.maximum(x, 0.0) + log1p_e


_SQRT_HALF = 0.7071067811865476
_INV_SQRT_2PI = 0.3989422804014327


def _normal_cdf(x):
    return 0.5 * (1.0 + lax.erf(x * _SQRT_HALF))


def _gelu_grad(x, cdf):
    return cdf + x * jnp.exp(-0.5 * x * x) * _INV_SQRT_2PI


def _dot(a, b, dims):
    return lax.dot_general(a, b, (dims, ((), ())), preferred_element_type=F32)


_NN = ((1,), (0,))
_NT = ((1,), (1,))
_TN = ((0,), (0,))


def _split3(x):
    hi = x.astype(BF16)
    r1 = x - hi.astype(F32)
    mid = r1.astype(BF16)
    lo = (r1 - mid.astype(F32)).astype(BF16)
    return hi, mid, lo


def _dot_exact_rhs(x, e, dims):
    hi, mid, lo = _split3(x)
    return _dot(hi, e, dims) + _dot(mid, e, dims) + _dot(lo, e, dims)


def _dot_exact_lhs(e, x, dims):
    hi, mid, lo = _split3(x)
    return _dot(e, hi, dims) + _dot(e, mid, dims) + _dot(e, lo, dims)


def _tri(lower):
    r = lax.broadcasted_iota(jnp.int32, (CHUNK, CHUNK), 0)
    c = lax.broadcasted_iota(jnp.int32, (CHUNK, CHUNK), 1)
    return (r >= c) if lower else (r <= c)


def _matmul(a, b, *, mode, tm, tn, tk, out_dtypes, name, epilogue=None, extras=(), extra_specs=(), j_outer=False, deps=(),
            carry=False, row_parts=1):
    if mode == "nn":
        (m, k), (_, n) = a.shape, b.shape
    elif mode == "nt":
        (m, k), (n, _) = a.shape, b.shape
    else:
        (k, m), (_, n) = a.shape, b.shape
    assert m % tm == 0 and n % tn == 0 and k % tk == 0, (name, m, n, k, tm, tn, tk)
    nk = k // tk
    n_extra, n_out = len(extras), len(out_dtypes)
    first_out = 2 + n_extra + len(deps)
    dims = {"nn": _NN, "nt": _NT, "tn": _TN}[mode]
    if epilogue is None:
        def epilogue(acc, ex, outs, first):
            outs[0][...] = acc.astype(outs[0].dtype)
    assert row_parts == 1 or (nk == 1 and mode != "tn"), name
    part_rows = tm // row_parts

    def body(*refs):
        a_ref, b_ref = refs[0], refs[1]
        ex_refs = refs[2:2 + n_extra]
        outs = refs[first_out:first_out + n_out]
        first_tile = pl.program_id(0) == 0
        if nk == 1:
            for part in range(row_parts):
                rows = slice(part * part_rows, (part + 1) * part_rows)
                view = lambda ref: ref.at[rows] if row_parts > 1 and ref.shape[0] == tm else ref
                p = _dot(a_ref[rows, :] if row_parts > 1 else a_ref[...], b_ref[...], dims)
                epilogue(p, [view(r) for r in ex_refs], [view(o) for o in outs], first_tile if part == 0 else False)
            return
        p = _dot(a_ref[...], b_ref[...], dims)
        acc_ref = refs[first_out + n_out]
        kk = pl.program_id(2)

        @pl.when(kk == 0)
        def _():
            acc_ref[...] = p

        @pl.when(kk > 0)
        def _():
            acc_ref[...] += p

        @pl.when(kk == nk - 1)
        def _():
            epilogue(acc_ref[...], ex_refs, outs, first_tile)

    if j_outer:
        grid = (n // tn, m // tm, nk)
        ij = lambda g0, g1: (g1, g0)
    else:
        grid = (m // tm, n // tn, nk)
        ij = lambda g0, g1: (g0, g1)

    def wrap(fn):
        return lambda g0, g1, kk: fn(*ij(g0, g1), kk)

    if mode == "nn":
        a_spec = pl.BlockSpec((tm, tk), wrap(lambda i, j, kk: (i, kk)))
        b_spec = pl.BlockSpec((tk, tn), wrap(lambda i, j, kk: (kk, j)))
        a_blk, b_blk = (tm, tk), (tk, tn)
    elif mode == "nt":
        a_spec = pl.BlockSpec((tm, tk), wrap(lambda i, j, kk: (i, kk)))
        b_spec = pl.BlockSpec((tn, tk), wrap(lambda i, j, kk: (j, kk)))
        a_blk, b_blk = (tm, tk), (tn, tk)
    else:
        a_spec = pl.BlockSpec((tk, tm), wrap(lambda i, j, kk: (kk, i)))
        b_spec = pl.BlockSpec((tk, tn), wrap(lambda i, j, kk: (kk, j)))
        a_blk, b_blk = (tk, tm), (tk, tn)
    ex_specs = [pl.BlockSpec(shape, wrap(lambda i, j, kk, f=f: f(i, j))) for shape, f in extra_specs]
    outs = [o if isinstance(o, tuple) else ((m, n), o, (tm, tn), lambda i, j: (i, j)) for o in out_dtypes]
    out_spec = [pl.BlockSpec(blk_shape, wrap(lambda i, j, kk, f=f: f(i, j))) for _, _, blk_shape, f in outs]
    out_shape = [jax.ShapeDtypeStruct(shape, dt) for shape, dt, _, _ in outs]
    blk = (_nbytes(a_blk, a.dtype) + _nbytes(b_blk, b.dtype) + sum(_nbytes(s, F32) for s, _ in extra_specs)
           + sum(_nbytes(blk_shape, dt) for _, dt, blk_shape, _ in outs) + _nbytes((tm, tn), F32))
    order = ("arbitrary",) * 3 if carry else ("parallel", "parallel", "arbitrary")
    res = pl.pallas_call(
        body, name=name, grid=grid,
        in_specs=[a_spec, b_spec] + ex_specs + [_HBM] * len(deps), out_specs=out_spec, out_shape=out_shape,
        scratch_shapes=[pltpu.VMEM((tm, tn), F32)] if nk > 1 else [],
        compiler_params=_params(order, blk),
    )(a, b, *extras, *deps)
    return res[0] if n_out == 1 else res


ROW_TILE = 256


def _row_spec(width, col_block=0, tile=ROW_TILE):
    return pl.BlockSpec((tile, width), lambda i, cb=col_block: (i, cb))


def _vec_spec(width, col_block=0):
    return pl.BlockSpec((1, width), lambda i, cb=col_block: (0, cb))


def _rms_fwd(x, g, w_t, name, deps=()):
    t = x.shape[0]
    n_small = w_t.shape[0]

    def body(x_ref, g_ref, w_ref, *rest):
        h_ref, small_ref = rest[-2:]
        xv = x_ref[...]
        r = lax.rsqrt(jnp.mean(xv * xv, axis=-1, keepdims=True) + NORM_EPS)
        h = (xv * r * g_ref[...]).astype(BF16)
        h_ref[...] = h
        small_ref[...] = _dot(h, w_ref[...], _NT)

    return pl.pallas_call(
        body, name=name, grid=(t // ROW_TILE,),
        in_specs=[_row_spec(D_MODEL), _vec_spec(D_MODEL), pl.BlockSpec((n_small, D_MODEL), lambda i: (0, 0))]
        + [_HBM] * len(deps),
        out_specs=[_row_spec(D_MODEL), _row_spec(n_small)],
        out_shape=[jax.ShapeDtypeStruct((t, D_MODEL), BF16), jax.ShapeDtypeStruct((t, n_small), F32)],
        compiler_params=_params(("parallel",), 3 * _nbytes((ROW_TILE, D_MODEL), F32)),
    )(x, g, w_t, *deps)


def _rms_scale(xv):
    r = lax.rsqrt(jnp.mean(xv * xv, axis=-1, keepdims=True) + NORM_EPS)
    return r, xv * r


def _rms_pullback(xv, g, dh):
    r, xh = _rms_scale(xv)
    dyg = dh * g
    return r * (dyg - xh * jnp.mean(dyg * xh, axis=-1, keepdims=True)), jnp.sum(dh * xh, axis=0, keepdims=True)


def _zero_when(first, *refs):
    if first is False:
        return

    @pl.when(first)
    def _():
        for ref in refs:
            ref[...] = jnp.zeros_like(ref)


def _residual_rms_epilogue(acc, ex, outs, first):
    x1 = acc + ex[0][...]
    outs[0][...] = x1
    _, xh = _rms_scale(x1)
    outs[1][...] = (xh * ex[1][...]).astype(BF16)


def _loss_epilogue(acc, ex, outs, first):
    dx_ref, dxb_ref, gg_ref, sq_ref, tot_ref = outs
    gv = ex[1][...]
    r, xh = _rms_scale(acc + ex[0][...])
    err = xh * gv - ex[2][...]
    dy = err * (1.0 / D_MODEL)
    dyg = dy * gv
    dx = r * (dyg - xh * jnp.mean(dyg * xh, axis=-1, keepdims=True))
    dx_ref[...] = dx
    dxb_ref[...] = dx.astype(BF16)

    _zero_when(first, gg_ref, sq_ref)
    gg_ref[...] += jnp.sum(dy * xh, axis=0, keepdims=True)
    sq_ref[...] += jnp.sum(err * err, axis=0, keepdims=True)
    tot_ref[...] = jnp.broadcast_to(jnp.sum(sq_ref[...], axis=1, keepdims=True) * (0.5 / D_MODEL), tot_ref.shape)


def _rms_bwd_epilogue(dh, ex, outs, first):
    dx, gg = _rms_pullback(ex[0][...], ex[1][...], dh)
    dx = dx + ex[2][...]
    outs[0][...] = dx
    if len(outs) == 3:
        outs[1][...] = dx.astype(BF16)

    _zero_when(first, outs[-1])
    outs[-1][...] += gg


def _merge_epilogue(acc, ex, outs, first):
    outs[0][...] = acc
    ga = _sigmoid(ex[1][...].astype(F32) + ex[3][...])
    gb = _sigmoid(ex[2][...].astype(F32) + ex[4][...])
    outs[1][...] = (ga * ex[0][...] + gb * acc).astype(BF16)


def _merge_bwd_epilogue(dm, ex, outs, first):
    dpa_ref, dpb_ref, dgl_ref, gb_ref = outs
    ga = _sigmoid(ex[2][...].astype(F32) + ex[4][...])
    gb = _sigmoid(ex[3][...].astype(F32) + ex[5][...])
    dpa_ref[...] = (dm * ga).astype(BF16)
    dpb_ref[...] = (dm * gb).astype(BF16)
    dla = dm * ex[0][...] * ga * (1.0 - ga)
    dlb = dm * ex[1][...] * gb * (1.0 - gb)
    dgl_ref[:, :D_MODEL] = dla.astype(BF16)
    dgl_ref[:, D_MODEL:] = dlb.astype(BF16)

    _zero_when(first, gb_ref)
    gb_ref[:, :D_MODEL] += jnp.sum(dla, axis=0, keepdims=True)
    gb_ref[:, D_MODEL:] += jnp.sum(dlb, axis=0, keepdims=True)


GMLP_TILE = 512
GMLP_NC = GMLP_TILE // CHUNK


def _gmlp_common(u_pre, v_pre, vg, vb):
    cdf_u, cdf_v = _normal_cdf(u_pre), _normal_cdf(v_pre)
    u = u_pre * cdf_u
    v = v_pre * cdf_v
    mu = jnp.mean(v, axis=-1, keepdims=True)
    vc = v - mu
    rstd = lax.rsqrt(jnp.mean(vc * vc, axis=-1, keepdims=True) + NORM_EPS)
    vh = vc * rstd
    vn = vh * vg + vb
    return u, vh, vn, rstd, cdf_u, cdf_v


def _chunks_to_lanes(x, g):
    return jnp.concatenate([x[c * CHUNK:(c + 1) * CHUNK, g * CHUNK:(g + 1) * CHUNK] for c in range(GMLP_NC)], axis=1)


def _gmlp_fwd(proj, vg, vb, wsp, bsp_t, name):
    t = proj.shape[0]

    def body(u_ref, v_ref, vg_ref, vb_ref, w_ref, b_ref, ya_ref):
        u, _, vn, _, _, _ = _gmlp_common(u_ref[...].astype(F32), v_ref[...].astype(F32), vg_ref[...], vb_ref[...])
        mask = _tri(True)
        bt = b_ref[...]
        for g in range(GROUPS):
            w = jnp.where(mask, w_ref[g], 0.0).astype(BF16)
            vcat = _chunks_to_lanes(vn, g).astype(BF16)
            s = _dot(w, vcat, _NN) + bt[:, g:g + 1]
            for c in range(GMLP_NC):
                rows, cols = slice(c * CHUNK, (c + 1) * CHUNK), slice(g * CHUNK, (g + 1) * CHUNK)
                ya_ref[rows, cols] = (u[rows, cols] * s[:, c * CHUNK:(c + 1) * CHUNK]).astype(BF16)

    return pl.pallas_call(
        body, name=name, grid=(t // GMLP_TILE,),
        in_specs=[_row_spec(D_MODEL, 0, GMLP_TILE), _row_spec(D_MODEL, 1, GMLP_TILE), _vec_spec(D_MODEL),
                  _vec_spec(D_MODEL), pl.BlockSpec((GROUPS, CHUNK, CHUNK), lambda i: (0, 0, 0)),
                  pl.BlockSpec((CHUNK, GROUPS), lambda i: (0, 0))],
        out_specs=_row_spec(D_MODEL, 0, GMLP_TILE),
        out_shape=jax.ShapeDtypeStruct((t, D_MODEL), BF16),
        compiler_params=_params(("parallel",), 3 * _nbytes((GMLP_TILE, D_MODEL), F32)),
    )(proj, proj, vg, vb, wsp, bsp_t)


def _gmlp_bwd(proj, dya, vg, vb, wsp, bsp_t, dproj, name):
    t = proj.shape[0]

    def body(u_ref, v_ref, dya_ref, vg_ref, vb_ref, w_ref, b_ref, dproj_in, duv_ref, gw_ref, gbt_ref, gvg_ref, gvb_ref,
             dvn_scr, du_scr):
        del dproj_in
        u_pre, v_pre = u_ref[...].astype(F32), v_ref[...].astype(F32)
        vgv = vg_ref[...]
        u, vh, vn, rstd, cdf_u, cdf_v = _gmlp_common(u_pre, v_pre, vgv, vb_ref[...])
        dya = dya_ref[...]
        mask = _tri(True)
        bt = b_ref[...]
        first = pl.program_id(0) == 0

        @pl.when(first)
        def _():
            gw_ref[...] = jnp.zeros_like(gw_ref)
            gbt_ref[...] = jnp.zeros_like(gbt_ref)
            gvg_ref[...] = jnp.zeros_like(gvg_ref)
            gvb_ref[...] = jnp.zeros_like(gvb_ref)

        lane = lax.broadcasted_iota(jnp.int32, (CHUNK, GROUPS), 1)
        gbt = jnp.zeros((CHUNK, GROUPS), F32)
        for g in range(GROUPS):
            w = jnp.where(mask, w_ref[g], 0.0).astype(BF16)
            vcat = _chunks_to_lanes(vn, g).astype(BF16)
            s = _dot(w, vcat, _NN) + bt[:, g:g + 1]
            ds = _chunks_to_lanes(dya * u, g)
            gbt = jnp.where(lane == g, jnp.sum(ds, axis=1, keepdims=True), gbt)
            dsb = ds.astype(BF16)
            gw_ref[g] += jnp.where(mask, _dot(dsb, vcat, _NT), 0.0)
            dv = _dot(w, dsb, _TN)
            for c in range(GMLP_NC):
                rows, cols = slice(c * CHUNK, (c + 1) * CHUNK), slice(g * CHUNK, (g + 1) * CHUNK)
                dvn_scr[rows, cols] = dv[:, c * CHUNK:(c + 1) * CHUNK]
                du_scr[rows, cols] = dya[rows, cols] * s[:, c * CHUNK:(c + 1) * CHUNK]
        gbt_ref[...] += gbt
        dvn = dvn_scr[...]
        gvg_ref[...] += jnp.sum(dvn * vh, axis=0, keepdims=True)
        gvb_ref[...] += jnp.sum(dvn, axis=0, keepdims=True)
        dvh = dvn * vgv
        dv = rstd * (dvh - jnp.mean(dvh, axis=-1, keepdims=True) - vh * jnp.mean(dvh * vh, axis=-1, keepdims=True))
        duv_ref[:, :D_MODEL] = (du_scr[...] * _gelu_grad(u_pre, cdf_u)).astype(BF16)
        duv_ref[:, D_MODEL:] = (dv * _gelu_grad(v_pre, cdf_v)).astype(BF16)

    return pl.pallas_call(
        body, name=name, grid=(t // GMLP_TILE,),
        in_specs=[_row_spec(D_MODEL, 0, GMLP_TILE), _row_spec(D_MODEL, 1, GMLP_TILE), _row_spec(D_MODEL, 0, GMLP_TILE),
                  _vec_spec(D_MODEL), _vec_spec(D_MODEL), pl.BlockSpec((GROUPS, CHUNK, CHUNK), lambda i: (0, 0, 0)),
                  pl.BlockSpec((CHUNK, GROUPS), lambda i: (0, 0)), pl.BlockSpec(memory_space=pl.ANY)],
        out_specs=[_row_spec(2 * D_MODEL, 0, GMLP_TILE), pl.BlockSpec((GROUPS, CHUNK, CHUNK), lambda i: (0, 0, 0)),
                   pl.BlockSpec((CHUNK, GROUPS), lambda i: (0, 0)), _vec_spec(D_MODEL), _vec_spec(D_MODEL)],
        out_shape=[jax.ShapeDtypeStruct(dproj.shape, BF16), jax.ShapeDtypeStruct((GROUPS, CHUNK, CHUNK), F32),
                   jax.ShapeDtypeStruct((CHUNK, GROUPS), F32), jax.ShapeDtypeStruct((1, D_MODEL), F32),
                   jax.ShapeDtypeStruct((1, D_MODEL), F32)],
        scratch_shapes=[pltpu.VMEM((GMLP_TILE, D_MODEL), F32), pltpu.VMEM((GMLP_TILE, D_MODEL), F32)],
        input_output_aliases={7: 0},
        compiler_params=_params(("arbitrary",), 6 * _nbytes((GMLP_TILE, D_MODEL), F32)),
    )(proj, proj, dya, vg, vb, wsp, bsp_t, dproj)


CONV_TILE = 512
CONV_COLS = 1024
CONV_RB = 32
HALO = SUBLANES


def _conv_fwd(proj, cw, cb, name):
    t = proj.shape[0]
    nj = CONV_DIM // CONV_COLS
    xcb = COL_XBC // CONV_COLS
    before = 2 * HALO
    rb = CONV_TILE // before

    def body(x_ref, prev_ref, cw_ref, cb_ref, pre_ref, xc_ref):
        i = pl.program_id(1)
        cw_v = cw_ref[...]
        cb_v = cb_ref[...]
        for b in range(CONV_TILE // CONV_RB):
            if b == 0:
                prev = jnp.where(i > 0, prev_ref[...].astype(F32)[HALO:, :], 0.0)
                ext = jnp.concatenate([prev, x_ref[:CONV_RB, :].astype(F32)], axis=0)
            else:
                ext = x_ref[b * CONV_RB - before:(b + 1) * CONV_RB, :].astype(F32)[HALO:, :]
            pre = cb_v + cw_v[CONV_WIDTH - 1:CONV_WIDTH, :] * ext[HALO:, :]
            for k in range(CONV_WIDTH - 1):
                back = CONV_WIDTH - 1 - k
                pre = pre + cw_v[k:k + 1, :] * pltpu.roll(ext, back, 0)[HALO:, :]
            pre_ref[b * CONV_RB:(b + 1) * CONV_RB, :] = pre
            xc_ref[b * CONV_RB:(b + 1) * CONV_RB, :] = pre * _sigmoid(pre)

    tile = pl.BlockSpec((CONV_TILE, CONV_COLS), lambda j, i: (i, j))
    return pl.pallas_call(
        body, name=name, grid=(nj, t // CONV_TILE),
        in_specs=[pl.BlockSpec((CONV_TILE, CONV_COLS), lambda j, i: (i, xcb + j)),
                  pl.BlockSpec((before, CONV_COLS), lambda j, i: (jnp.maximum(i * rb - 1, 0), xcb + j)),
                  pl.BlockSpec((CONV_WIDTH, CONV_COLS), lambda j, i: (0, j)),
                  pl.BlockSpec((1, CONV_COLS), lambda j, i: (0, j))],
        out_specs=[tile, tile],
        out_shape=[jax.ShapeDtypeStruct((t, CONV_DIM), F32), jax.ShapeDtypeStruct((t, CONV_DIM), F32)],
        compiler_params=_params(("parallel", "parallel"), 4 * _nbytes((CONV_TILE, CONV_COLS), F32)),
    )(proj, proj, cw, cb)


def _fold_rows(v):
    out = v[:SUBLANES]
    for r in range(1, v.shape[0] // SUBLANES):
        out = out + v[r * SUBLANES:(r + 1) * SUBLANES]
    return out


def _conv_bwd(proj, pre, dxc, cw, dproj, name):
    t = proj.shape[0]
    nj = CONV_DIM // CONV_COLS
    ni = t // CONV_TILE
    xcb = COL_XBC // CONV_COLS
    rb = CONV_TILE // HALO
    last_rb = t // HALO - 1

    def body(x_ref, p_ref, pnext_ref, d_ref, dnext_ref, cw_ref, dproj_in, dx_ref, gw_ref, gb_ref):
        del dproj_in
        i = pl.program_id(1)
        cw_v = cw_ref[...]

        def dpre_of(p, d):
            sg = _sigmoid(p)
            return d * sg * (1.0 + p * (1.0 - sg))

        @pl.when(i == 0)
        def _():
            gw_ref[...] = jnp.zeros_like(gw_ref)
            gb_ref[...] = jnp.zeros_like(gb_ref)

        head = dpre_of(pnext_ref[...], jnp.where(i < ni - 1, dnext_ref[...], 0.0))
        gb_acc = jnp.zeros((SUBLANES, CONV_COLS), F32)
        gw_acc = [jnp.zeros((SUBLANES, CONV_COLS), F32) for _ in range(CONV_WIDTH)]
        for b in reversed(range(CONV_TILE // CONV_RB)):
            rows = slice(b * CONV_RB, (b + 1) * CONV_RB)
            cur = dpre_of(p_ref[rows, :], d_ref[rows, :])
            ext = jnp.concatenate([cur, head], axis=0)
            xv = x_ref[rows, :].astype(F32)
            dx = None
            for k in range(CONV_WIDTH):
                shift = CONV_WIDTH - 1 - k
                win = cur if shift == 0 else pltpu.roll(ext, CONV_RB + HALO - shift, 0)[:CONV_RB, :]
                term = cw_v[k:k + 1, :] * win
                dx = term if dx is None else dx + term
                gw_acc[k] = gw_acc[k] + _fold_rows(win * xv)
            dx_ref[rows, :] = dx.astype(BF16)
            gb_acc = gb_acc + _fold_rows(cur)
            head = cur[:HALO]
        gb_ref[...] += jnp.sum(gb_acc, axis=0, keepdims=True)
        for k in range(CONV_WIDTH):
            gw_ref[k:k + 1, :] += jnp.sum(gw_acc[k], axis=0, keepdims=True)

    tile = pl.BlockSpec((CONV_TILE, CONV_COLS), lambda j, i: (i, j))
    after = pl.BlockSpec((HALO, CONV_COLS), lambda j, i: (jnp.minimum((i + 1) * rb, last_rb), j))
    return pl.pallas_call(
        body, name=name, grid=(nj, ni),
        in_specs=[pl.BlockSpec((CONV_TILE, CONV_COLS), lambda j, i: (i, xcb + j)), tile, after, tile, after,
                  pl.BlockSpec((CONV_WIDTH, CONV_COLS), lambda j, i: (0, j)),
                  pl.BlockSpec(memory_space=pl.ANY)],
        out_specs=[pl.BlockSpec((CONV_TILE, CONV_COLS), lambda j, i: (i, xcb + j)),
                   pl.BlockSpec((CONV_WIDTH, CONV_COLS), lambda j, i: (0, j)),
                   pl.BlockSpec((1, CONV_COLS), lambda j, i: (0, j))],
        out_shape=[jax.ShapeDtypeStruct(dproj.shape, BF16), jax.ShapeDtypeStruct((CONV_WIDTH, CONV_DIM), F32),
                   jax.ShapeDtypeStruct((1, CONV_DIM), F32)],
        input_output_aliases={6: 0},
        compiler_params=_params(("parallel", "arbitrary"), 4 * _nbytes((CONV_TILE, CONV_COLS), F32)),
    )(proj, pre, pre, dxc, dxc, cw, dproj)


def _ssd_decays(dt_raw, dtb, alog, e_bf, tril_bf):
    dtv = _softplus(dt_raw + dtb)
    a = -jnp.exp(alog)
    cs = _dot_exact_lhs(tril_bf, dtv * a, _NN)
    cs_last = cs[CHUNK - 1:CHUNK, :]
    stack = jnp.concatenate([dtv, jnp.exp(cs), jnp.exp(cs_last - cs)], axis=0)
    full = _head_expand(stack, e_bf)
    return dtv, a, cs, full[:CHUNK], full[CHUNK:2 * CHUNK], full[2 * CHUNK:]


def _split2(x):
    hi = x.astype(BF16)
    return hi, (x - hi.astype(F32)).astype(BF16)


def _head_expand(x, e_bf):
    hi, mid = _split2(x)
    return _dot(hi, e_bf, _NN) + _dot(mid, e_bf, _NN)


def _head_sums(x, e_bf):
    hi, mid = _split2(x)
    return _dot(hi, e_bf, _NT) + _dot(mid, e_bf, _NT)


def _head_mats(cs, cs_t, cb, h, mask):
    seg = cs[:, h:h + 1] - cs_t[h:h + 1, :]
    lmat = jnp.exp(jnp.where(mask, seg, -jnp.inf))
    return lmat, cb * lmat


def _ssd_fwd(xc, proj, dt_raw, dtb, alog, dskip_full, ng, e_bf, name):
    t = xc.shape[0]
    nc = t // CHUNK
    zcb = COL_Z // D_INNER

    def body(xc_ref, z_ref, dt_ref, dtb_ref, alog_ref, dsk_ref, ng_ref, e_ref, y_ref, yb_ref, sprev_ref, s_scr):
        @pl.when(pl.program_id(0) == 0)
        def _():
            s_scr[...] = jnp.zeros_like(s_scr)

        mask = _tri(True)
        tril_bf = mask.astype(BF16)
        e_v = e_ref[...]
        _, _, cs, dt_full, ecs_full, decay_full = _ssd_decays(dt_ref[...], dtb_ref[...], alog_ref[...], e_v, tril_bf)
        cs_t = cs.T
        sprev_ref[0] = s_scr[...]
        for g in range(GROUPS):
            gc = slice(g * GROUP_W, (g + 1) * GROUP_W)
            xs = xc_ref[:, gc]
            xdt = xs * dt_full[:, gc]
            xdt_b = xdt.astype(BF16)
            xdec = (xdt * decay_full[:, gc]).astype(BF16)
            bg = xc_ref[:, D_INNER + g * D_STATE:D_INNER + (g + 1) * D_STATE].astype(BF16)
            cg = xc_ref[:, D_INNER + GROUPS * D_STATE + g * D_STATE:D_INNER + GROUPS * D_STATE + (g + 1) * D_STATE].astype(BF16)
            cb = _dot(cg, bg, _NT)
            s_prev = s_scr[:, gc]
            y_off = ecs_full[:, gc] * _dot(cg, s_prev.astype(BF16), _NN)
            s_scr[:, gc] = s_prev * ecs_full[CHUNK - 1:CHUNK, gc] + _dot(bg, xdec, _TN)
            parts = []
            for r in range(GROUP_W // HEAD_DIM):
                h = g * (GROUP_W // HEAD_DIM) + r
                _, m = _head_mats(cs, cs_t, cb, h, mask)
                parts.append(_dot(m.astype(BF16), xdt_b[:, r * HEAD_DIM:(r + 1) * HEAD_DIM], _NN))
            yg = jnp.concatenate(parts, axis=1) + y_off + dsk_ref[:, gc] * xs
            y_ref[:, gc] = yg
            zv = z_ref[:, gc].astype(F32)
            ygate = yg * (zv * _sigmoid(zv))
            rstd = lax.rsqrt(jnp.mean(ygate * ygate, axis=-1, keepdims=True) + NORM_EPS)
            yb_ref[:, gc] = (ygate * rstd * ng_ref[:, gc]).astype(BF16)

    vec = lambda w: pl.BlockSpec((1, w), lambda i: (0, 0))
    blk = _nbytes((CHUNK, CONV_DIM), F32) + 3 * _nbytes((CHUNK, D_INNER), F32) + _nbytes((D_STATE, D_INNER), F32)
    return pl.pallas_call(
        body, name=name, grid=(nc,),
        in_specs=[pl.BlockSpec((CHUNK, CONV_DIM), lambda i: (i, 0)), pl.BlockSpec((CHUNK, D_INNER), lambda i: (i, zcb)),
                  pl.BlockSpec((CHUNK, DT_PAD), lambda i: (i, 0)), vec(DT_PAD), vec(DT_PAD), vec(D_INNER), vec(D_INNER),
                  pl.BlockSpec((DT_PAD, D_INNER), lambda i: (0, 0))],
        out_specs=[pl.BlockSpec((CHUNK, D_INNER), lambda i: (i, 0)), pl.BlockSpec((CHUNK, D_INNER), lambda i: (i, 0)),
                   pl.BlockSpec((1, D_STATE, D_INNER), lambda i: (i, 0, 0))],
        out_shape=[jax.ShapeDtypeStruct((t, D_INNER), F32), jax.ShapeDtypeStruct((t, D_INNER), BF16),
                   jax.ShapeDtypeStruct((nc, D_STATE, D_INNER), F32)],
        scratch_shapes=[pltpu.VMEM((D_STATE, D_INNER), F32)],
        compiler_params=_params(("arbitrary",), blk),
    )(xc, proj, dt_raw, dtb, alog, dskip_full, ng, e_bf)


def _ssd_bwd(dyb, y, xc, proj, dt_raw, sprev, dtb, alog, dskip_full, ng, e_bf, dproj, name):
    t = xc.shape[0]
    nc = t // CHUNK
    zcb = COL_Z // D_INNER
    hpg = GROUP_W // HEAD_DIM
    rev = lambda i: nc - 1 - i

    def body(dyb_ref, y_ref, xc_ref, z_ref, dt_ref, sprev_ref, dtb_ref, alog_ref, dsk_ref, ng_ref, e_ref, dproj_in,
             dz_ref, dxc_ref, ddt_ref, gng_ref, gdsk_ref, galog_ref, gdtb_ref, ds_scr, sums_scr):
        del dproj_in

        @pl.when(pl.program_id(0) == 0)
        def _():
            ds_scr[...] = jnp.zeros_like(ds_scr)
            gng_ref[...] = jnp.zeros_like(gng_ref)
            gdsk_ref[...] = jnp.zeros_like(gdsk_ref)
            galog_ref[...] = jnp.zeros_like(galog_ref)
            gdtb_ref[...] = jnp.zeros_like(gdtb_ref)

        mask = _tri(True)
        tril_bf = mask.astype(BF16)
        triu_bf = _tri(False).astype(BF16)
        e_v = e_ref[...]
        dt_in = dt_ref[...] + dtb_ref[...]
        dtv, a, cs, dt_full, ecs_full, decay_full = _ssd_decays(dt_ref[...], dtb_ref[...], alog_ref[...], e_v, tril_bf)
        cs_t = cs.T

        lane_h = lax.broadcasted_iota(jnp.int32, (CHUNK, DT_PAD), 1)
        sub_h = lax.broadcasted_iota(jnp.int32, (DT_PAD, CHUNK), 0)
        dcs_rows = jnp.zeros((CHUNK, DT_PAD), F32)
        dcs_cols_t = jnp.zeros((DT_PAD, CHUNK), F32)
        last_cols, dsk_cols = [], []
        for g in range(GROUPS):
            gc = slice(g * GROUP_W, (g + 1) * GROUP_W)
            b_cols = slice(D_INNER + g * D_STATE, D_INNER + (g + 1) * D_STATE)
            c_cols = slice(D_INNER + GROUPS * D_STATE + g * D_STATE, D_INNER + GROUPS * D_STATE + (g + 1) * D_STATE)
            xs = xc_ref[:, gc]
            xdt = xs * dt_full[:, gc]
            xdt_b = xdt.astype(BF16)
            xdec = xdt * decay_full[:, gc]
            xdec_b = xdec.astype(BF16)
            zv = z_ref[:, gc].astype(F32)
            sg = _sigmoid(zv)
            gate = zv * sg
            yv = y_ref[:, gc]
            dybv = dyb_ref[:, gc]
            ygate = yv * gate
            rstd = lax.rsqrt(jnp.mean(ygate * ygate, axis=-1, keepdims=True) + NORM_EPS)
            yn = ygate * rstd
            gng_ref[:, gc] += jnp.sum(dybv * yn, axis=0, keepdims=True)
            dyn = dybv * ng_ref[:, gc]
            dyg = rstd * (dyn - yn * jnp.mean(dyn * yn, axis=-1, keepdims=True))
            dz_ref[:, gc] = (dyg * yv * sg * (1.0 + zv * (1.0 - sg))).astype(BF16)
            dy = dyg * gate
            dy_b = dy.astype(BF16)
            dyo = dy * ecs_full[:, gc]
            dyo_b = dyo.astype(BF16)
            dsk_cols.append(jnp.sum(dy * xs, axis=0, keepdims=True))

            bg = xc_ref[:, b_cols].astype(BF16)
            cg = xc_ref[:, c_cols].astype(BF16)
            s_prev = sprev_ref[0, :, gc]
            s_prev_b = s_prev.astype(BF16)
            dsg = ds_scr[:, gc]
            dsg_b = dsg.astype(BF16)
            cb = _dot(cg, bg, _NT)
            c_s = _dot(cg, s_prev_b, _NN)
            b_ds = _dot(bg, dsg_b, _NN)
            dcb = jnp.zeros((CHUNK, CHUNK), F32)
            parts = []
            for r in range(hpg):
                h = g * hpg + r
                hc = slice(r * HEAD_DIM, (r + 1) * HEAD_DIM)
                lmat, m = _head_mats(cs, cs_t, cb, h, mask)
                dm = _dot(dy_b[:, hc], xdt_b[:, hc], _NT)
                parts.append(_dot(m.astype(BF16), dy_b[:, hc], _TN))
                dcb = dcb + dm * lmat
                w = dm * m
                dcs_rows = jnp.where(lane_h == h, jnp.sum(w, axis=1, keepdims=True), dcs_rows)
                dcs_cols_t = jnp.where(sub_h == h, jnp.sum(w, axis=0, keepdims=True), dcs_cols_t)
            dxdt = jnp.concatenate(parts, axis=1) + decay_full[:, gc] * b_ds
            dcb_b = dcb.astype(BF16)
            dxc_ref[:, c_cols] = _dot(dcb_b, bg, _NN) + _dot(dyo_b, s_prev_b, _NT)
            dxc_ref[:, b_cols] = _dot(dcb_b, cg, _TN) + _dot(xdec_b, dsg_b, _NT)
            cdec = ecs_full[CHUNK - 1:CHUNK, gc]
            ds_scr[:, gc] = _dot(cg, dyo_b, _TN) + cdec * dsg
            dxc_ref[:, gc] = dxdt * dt_full[:, gc] + dsk_ref[:, gc] * dy
            dec_prod = xdec * b_ds
            sums_scr[:CHUNK, gc] = dyo * c_s - dec_prod
            sums_scr[CHUNK:, gc] = dxdt * xs
            last_cols.append(jnp.sum(dec_prod, axis=0, keepdims=True) + cdec * jnp.sum(dsg * s_prev, axis=0, keepdims=True))
        t_sums = _head_sums(sums_scr[...], e_v)
        tail = jnp.concatenate([jnp.concatenate(last_cols, axis=1), jnp.concatenate(dsk_cols, axis=1),
                                jnp.zeros((SUBLANES - 2, D_INNER), F32)], axis=0)
        t_tail = _dot_exact_rhs(tail, e_v, _NT)
        gdsk_ref[...] += t_tail[1:2, :]
        row = lax.broadcasted_iota(jnp.int32, (CHUNK, DT_PAD), 0)
        dcs = dcs_rows - dcs_cols_t.T + t_sums[:CHUNK] + jnp.where(row == CHUNK - 1, t_tail[0:1, :], 0.0)
        dda = _dot_exact_lhs(triu_bf, dcs, _NN)
        galog_ref[...] += jnp.sum(dda * dtv, axis=0, keepdims=True) * a
        ddt = dda * a + t_sums[CHUNK:]
        ddt_raw = jnp.where(lane_h < N_HEADS, ddt * _sigmoid(dt_in), 0.0)
        gdtb_ref[...] += jnp.sum(ddt_raw, axis=0, keepdims=True)
        ddt_ref[...] = ddt_raw.astype(BF16)

    vec = lambda w: pl.BlockSpec((1, w), lambda i: (0, 0))
    blk = (2 * _nbytes((CHUNK, CONV_DIM), F32) + 4 * _nbytes((CHUNK, D_INNER), F32) + 4 * _nbytes((D_STATE, D_INNER), F32))
    return pl.pallas_call(
        body, name=name, grid=(nc,),
        in_specs=[pl.BlockSpec((CHUNK, D_INNER), lambda i: (rev(i), 0)), pl.BlockSpec((CHUNK, D_INNER), lambda i: (rev(i), 0)),
                  pl.BlockSpec((CHUNK, CONV_DIM), lambda i: (rev(i), 0)), pl.BlockSpec((CHUNK, D_INNER), lambda i: (rev(i), zcb)),
                  pl.BlockSpec((CHUNK, DT_PAD), lambda i: (rev(i), 0)), pl.BlockSpec((1, D_STATE, D_INNER), lambda i: (rev(i), 0, 0)),
                  vec(DT_PAD), vec(DT_PAD), vec(D_INNER), vec(D_INNER), pl.BlockSpec((DT_PAD, D_INNER), lambda i: (0, 0)),
                  pl.BlockSpec(memory_space=pl.ANY)],
        out_specs=[pl.BlockSpec((CHUNK, D_INNER), lambda i: (rev(i), zcb)), pl.BlockSpec((CHUNK, CONV_DIM), lambda i: (rev(i), 0)),
                   pl.BlockSpec((CHUNK, DT_PAD), lambda i: (rev(i), 0)), vec(D_INNER), vec(DT_PAD), vec(DT_PAD), vec(DT_PAD)],
        out_shape=[jax.ShapeDtypeStruct(dproj.shape, BF16), jax.ShapeDtypeStruct((t, CONV_DIM), F32),
                   jax.ShapeDtypeStruct((t, DT_PAD), BF16), jax.ShapeDtypeStruct((1, D_INNER), F32),
                   jax.ShapeDtypeStruct((1, DT_PAD), F32), jax.ShapeDtypeStruct((1, DT_PAD), F32),
                   jax.ShapeDtypeStruct((1, DT_PAD), F32)],
        scratch_shapes=[pltpu.VMEM((D_STATE, D_INNER), F32), pltpu.VMEM((2 * CHUNK, D_INNER), F32)],
        input_output_aliases={11: 0},
        compiler_params=_params(("arbitrary",), blk),
    )(dyb, y, xc, proj, dt_raw, sprev, dtb, alog, dskip_full, ng, e_bf, dproj)


def _mesh_pos():
    return lax.axis_index("x"), lax.axis_index("y"), lax.axis_index("c")


def _other_chips(x, y):
    return [(1 - x, y), (x, 1 - y), (1 - x, 1 - y)]


def _all_peers(x, y, c):
    peers = []
    for k in range(1, N_DEV):
        fx, fy, fc = (k >> 2) & 1, (k >> 1) & 1, k & 1
        px, py, pc = x + fx - 2 * x * fx, y + fy - 2 * y * fy, c + fc - 2 * c * fc
        peers.append(((px, py, pc), 4 * px + 2 * py + pc))
    return peers


def _all_gather(shards, name, own_only=()):
    n, n_own = len(shards), len(own_only)

    def body(*refs):
        ins, own_ins = refs[:n], refs[n:n + n_own]
        outs, own_outs = refs[n + n_own:2 * n + n_own], refs[2 * n + n_own:2 * (n + n_own)]
        send_sems, recv_sems, local_sems = refs[2 * (n + n_own):]
        x, y, c = _mesh_pos()
        me, sibling = (x, y, c), (x, y, 1 - c)
        chips = _other_chips(x, y)

        def slot(p):
            return 4 * p[0] + 2 * p[1] + p[2]

        def copy(a, k, block, to, src=None):
            dst = outs[a].at[slot(block)]
            return pltpu.make_async_remote_copy(
                src_ref=dst if src is None else src, dst_ref=dst, send_sem=send_sems.at[a * 7 + k],
                recv_sem=recv_sems.at[a * 7 + k], device_id=to, device_id_type=MESH)

        started = []
        own = []
        for a in range(n_own):
            mine = pltpu.make_async_copy(own_ins[a], own_outs[a].at[slot(me)], local_sems.at[n + a])
            mine.start()
            own.append(mine)
        for a in range(n):
            mine = pltpu.make_async_copy(ins[a], outs[a].at[slot(me)], local_sems.at[a])
            mine.start()
            own.append(mine)
            first = [copy(a, 0, me, sibling, src=ins[a])]
            first += [copy(a, 1 + j, me, (*chip, c), src=ins[a]) for j, chip in enumerate(chips)]
            for cp in first:
                cp.start()
            started += first
        for a in range(n):
            for j, chip in enumerate(chips):
                copy(a, 1 + j, (*chip, c), me).wait_recv()
                fwd = copy(a, 4 + j, (*chip, c), sibling)
                fwd.start()
                started.append(fwd)
        for a in range(n):
            copy(a, 0, sibling, me).wait_recv()
            for j, chip in enumerate(chips):
                copy(a, 4 + j, (*chip, 1 - c), me).wait_recv()
        for cp in started:
            cp.wait_send()
        for mine in own:
            mine.wait()

    return pl.pallas_call(
        body, name=name,
        in_specs=[_HBM] * (n + n_own), out_specs=[_HBM] * (n + n_own),
        out_shape=[jax.ShapeDtypeStruct((N_DEV,) + s.shape, s.dtype) for s in (*shards, *own_only)],
        scratch_shapes=[pltpu.SemaphoreType.DMA((7 * n,)), pltpu.SemaphoreType.DMA((7 * n,)),
                        pltpu.SemaphoreType.DMA((n + n_own,))],
    )(*shards, *own_only)


_SMALL_ROWS = (("norm_mix_g", 8), ("conv_b", 32), ("dt_bias", 1), ("a_log", 1), ("d_skip", 1), ("ssm_norm_g", 16),
               ("v_norm_g", 8), ("v_norm_b", 8), ("w_spatial", 1024), ("b_spatial", 8), ("b_gates", 16), ("norm_mlp_g", 8),
               ("norm_final_g", 8), ("conv_w", 128), ("loss", 1))
_LAST_SMALL = (("norm_mix_g", 8),)


def _packed_rows(table):
    return -(-sum(r for _, r in table) // SUBLANES) * SUBLANES


def _small_offsets(table=_SMALL_ROWS):
    offs, r = {}, 0
    for name, rows in table:
        offs[name] = r
        r += rows
    return offs


def _rows_from(src_ref, dst_ref, r0):
    k, w = src_ref.shape
    if w <= LANES:
        dst_ref[r0:r0 + k, 0:w] = src_ref[...]
        return
    per = w // LANES
    for i in range(k):
        for j in range(per):
            dst_ref[r0 + i * per + j:r0 + i * per + j + 1, :] = src_ref[i:i + 1, j * LANES:(j + 1) * LANES]


def _rows_to(src_ref, r0, dst_ref):
    k, w = dst_ref.shape
    if w <= LANES:
        dst_ref[...] = src_ref[r0:r0 + k, 0:w]
        return
    per = w // LANES
    for i in range(k):
        for j in range(per):
            dst_ref[i:i + 1, j * LANES:(j + 1) * LANES] = src_ref[r0 + i * per + j:r0 + i * per + j + 1, :]


def _pack_small(grads, slot_idx, name):
    names = [n for n, _ in _SMALL_ROWS if n in grads]
    offs = _small_offsets()
    rows = _packed_rows(_SMALL_ROWS)

    def body(slot_ref, *refs):
        del slot_ref
        ins, (packed_ref, land_ref) = refs[:len(names)], refs[len(names):]
        packed_ref[...] = jnp.zeros_like(packed_ref)
        for n, ref in zip(names, ins):
            _rows_from(ref, packed_ref, offs[n])
        land_ref[0] = packed_ref[...]

    whole = lambda shape: pl.BlockSpec(shape, lambda i, slot_ref: (0,) * len(shape))
    grid_spec = pltpu.PrefetchScalarGridSpec(
        num_scalar_prefetch=1, grid=(1,), in_specs=[whole(grads[n].shape) for n in names],
        out_specs=[whole((rows, LANES)), pl.BlockSpec((1, rows, LANES), lambda i, slot_ref: (slot_ref[0], 0, 0))])
    return pl.pallas_call(
        body, name=name, grid_spec=grid_spec,
        out_shape=[jax.ShapeDtypeStruct((rows, LANES), F32), jax.ShapeDtypeStruct((N_DEV, rows, LANES), F32)],
    )(slot_idx, *[grads[n] for n in names])


def _exchange_small(grads, table, name):
    names = [n for n, _ in table]
    offs = _small_offsets(table)
    n_in = len(names)
    packed_rows = _packed_rows(table)

    def body(*refs):
        ins, out_ref = refs[:n_in], refs[n_in]
        packed, send_sems, recv_sems, local_sem = refs[n_in + 1:]
        packed[...] = jnp.zeros_like(packed)
        for n, ref in zip(names, ins):
            _rows_from(ref, packed, offs[n])
        x, y, c = _mesh_pos()
        my_slot = 4 * x + 2 * y + c
        mine = pltpu.make_async_copy(packed, out_ref.at[my_slot], local_sem)
        mine.start()
        copies = []
        for k, (peer, peer_slot) in enumerate(_all_peers(x, y, c)):
            sems = dict(send_sem=send_sems.at[k], recv_sem=recv_sems.at[k], device_id=peer, device_id_type=MESH)
            send = pltpu.make_async_remote_copy(src_ref=packed, dst_ref=out_ref.at[my_slot], **sems)
            send.start()
            copies.append((send, pltpu.make_async_remote_copy(src_ref=packed, dst_ref=out_ref.at[peer_slot], **sems)))
        for send, recv in copies:
            send.wait_send()
            recv.wait_recv()
        mine.wait()

    return pl.pallas_call(
        body, name=name, in_specs=[pl.BlockSpec(memory_space=pltpu.VMEM)] * n_in, out_specs=_HBM,
        out_shape=jax.ShapeDtypeStruct((N_DEV, packed_rows, LANES), F32),
        scratch_shapes=[pltpu.VMEM((packed_rows, LANES), F32), pltpu.SemaphoreType.DMA((N_DEV - 1,)),
                        pltpu.SemaphoreType.DMA((N_DEV - 1,)), pltpu.SemaphoreType.DMA],
    )(*[grads[n] for n in names])


def _swap_with_sibling(grads, name):
    n = len(grads)

    def body(*refs):
        ins, outs = refs[:n], refs[n:2 * n]
        send_sems, recv_sems = refs[2 * n:]
        x, y, c = _mesh_pos()
        copies = []
        for a in range(n):
            for k in range(N_CHIP):
                cp = pltpu.make_async_remote_copy(
                    src_ref=ins[a].at[(1 - c) + 2 * k], dst_ref=outs[a].at[k], send_sem=send_sems.at[a * N_CHIP + k],
                    recv_sem=recv_sems.at[a * N_CHIP + k], device_id=(x, y, 1 - c), device_id_type=MESH)
                cp.start()
                copies.append(cp)
        for cp in copies:
            cp.wait()

    return pl.pallas_call(
        body, name=name, in_specs=[_HBM] * n, out_specs=[_HBM] * n,
        out_shape=[jax.ShapeDtypeStruct((N_CHIP,) + g.shape[1:], g.dtype) for g in grads],
        scratch_shapes=[pltpu.SemaphoreType.DMA((N_CHIP * n,)), pltpu.SemaphoreType.DMA((N_CHIP * n,))],
    )(*grads)


_SEM = pl.BlockSpec(memory_space=pltpu.SEMAPHORE)
_IN_HBM = pl.BlockSpec(memory_space=pltpu.HBM)
_EFFECT = pltpu.SideEffectType.DATAFLOW_SIDE_EFFECTING


def _in_hbm(a):
    return pltpu.with_memory_space_constraint(a, pltpu.HBM)


def _gather_copies(ins, lands, send_sems, recv_sems):
    x, y, c = _mesh_pos()
    my_slot = 4 * x + 2 * y + c
    pairs = []
    for a in range(len(ins)):
        for k, (peer, peer_slot) in enumerate(_all_peers(x, y, c)):
            sems = dict(send_sem=send_sems.at[a * (N_DEV - 1) + k], recv_sem=recv_sems.at[a * (N_DEV - 1) + k],
                        device_id=peer, device_id_type=MESH)
            pairs.append((pltpu.make_async_remote_copy(src_ref=ins[a], dst_ref=lands[a].at[my_slot], **sems),
                          pltpu.make_async_remote_copy(src_ref=ins[a], dst_ref=lands[a].at[peer_slot], **sems)))
    return pairs


def _scatter_copies(ins, lands, send_sems, recv_sems):
    x, y, c = _mesh_pos()
    my_chip = 2 * x + y
    pairs = []
    for a in range(len(ins)):
        for j, chip in enumerate(_other_chips(x, y)):
            there = 2 * chip[0] + chip[1]
            sems = dict(send_sem=send_sems.at[a * 3 + j], recv_sem=recv_sems.at[a * 3 + j],
                        device_id=(*chip, c), device_id_type=MESH)
            pairs.append((pltpu.make_async_remote_copy(src_ref=ins[a].at[there], dst_ref=lands[a].at[my_chip], **sems),
                          pltpu.make_async_remote_copy(src_ref=ins[a].at[my_chip], dst_ref=lands[a].at[there], **sems)))
    return pairs


def _split_start(srcs, lands, copies, per_array, name):
    n = len(srcs)

    def body(*refs):
        ins, land_refs = refs[:n], refs[n:2 * n]
        send_sems, recv_sems = refs[2 * n], refs[2 * n + 1]
        token = refs[-1]
        for send, _ in copies(ins, land_refs, send_sems, recv_sems):
            send.start()
        token[...] = jnp.zeros_like(token)

    outs = pl.pallas_call(
        body, name=name,
        out_shape=(pltpu.SemaphoreType.DMA((per_array * n,)), pltpu.SemaphoreType.DMA((per_array * n,)),
                   *[pltpu.HBM(s.shape, s.dtype) for s in srcs], *[pltpu.HBM(l.shape, l.dtype) for l in lands],
                   jax.ShapeDtypeStruct((SUBLANES, LANES), F32)),
        in_specs=[_IN_HBM] * (2 * n),
        out_specs=(_SEM, _SEM, *[_IN_HBM] * (2 * n), pl.BlockSpec(memory_space=pltpu.VMEM)),
        input_output_aliases={i: 2 + i for i in range(2 * n)},
        compiler_params=pltpu.CompilerParams(has_side_effects=_EFFECT),
    )(*[_in_hbm(s) for s in srcs], *[_in_hbm(l) for l in lands])
    return outs[0], outs[1], list(outs[2:2 + n]), list(outs[2 + n:2 + 2 * n]), outs[-1]


def _split_wait(started, copies, after, name):
    send_sems, recv_sems, srcs, lands, _ = started
    n = len(srcs)

    def body(*refs):
        ins, land_refs = refs[:n], refs[n:2 * n]
        for send, recv in copies(ins, land_refs, refs[2 * n], refs[2 * n + 1]):
            send.wait_send()
            recv.wait_recv()

    outs = pl.pallas_call(
        body, name=name,
        out_shape=(*[pltpu.HBM(s.shape, s.dtype) for s in srcs], *[pltpu.HBM(l.shape, l.dtype) for l in lands]),
        in_specs=[_IN_HBM] * (2 * n) + [_SEM, _SEM, _HBM],
        out_specs=[_IN_HBM] * (2 * n),
        input_output_aliases={i: i for i in range(2 * n)},
        compiler_params=pltpu.CompilerParams(has_side_effects=_EFFECT),
    )(*srcs, *lands, send_sems, recv_sems, after)
    return list(outs[:n]), list(outs[n:])


def _ew_block(rows, cols, slots):
    budget = 2 * 1024 * 1024
    br, bc = rows, cols
    while slots * br * bc * 4 > budget:
        if br % 2 == 0 and (br // 2) % (2 * SUBLANES) == 0:
            br //= 2
        elif bc % 2 == 0 and (bc // 2) % LANES == 0:
            bc //= 2
        else:
            break
    return br, bc


def _add_sibling(grads, recv, c_idx, name):
    _, rows, cols = grads.shape
    br, bc = _ew_block(rows, cols, 3)

    def body(c_ref, g_ref, r_ref, out_ref):
        del c_ref
        out_ref[...] = (g_ref[...].astype(F32) + r_ref[...].astype(F32)).astype(out_ref.dtype)

    grid_spec = pltpu.PrefetchScalarGridSpec(
        num_scalar_prefetch=1, grid=(N_CHIP, rows // br, cols // bc),
        in_specs=[pl.BlockSpec((1, br, bc), lambda k, i, j, c_ref: (c_ref[0] + 2 * k, i, j)),
                  pl.BlockSpec((1, br, bc), lambda k, i, j, c_ref: (k, i, j))],
        out_specs=pl.BlockSpec((1, br, bc), lambda k, i, j, c_ref: (k, i, j)))
    return pl.pallas_call(
        body, name=name, grid_spec=grid_spec, out_shape=jax.ShapeDtypeStruct((N_CHIP, rows, cols), grads.dtype),
        compiler_params=_params(("parallel", "parallel", "parallel"), 3 * _nbytes((br, bc), F32)),
    )(c_idx, grads, recv)


def _adam_math(g, w, m, v):
    m2 = ADAM_B1 * m + (1.0 - ADAM_B1) * g
    v2 = ADAM_B2 * v + (1.0 - ADAM_B2) * (g * g)
    m_hat = m2 * (1.0 / (1.0 - ADAM_B1 ** ADAM_STEP))
    v_hat = v2 * (1.0 / (1.0 - ADAM_B2 ** ADAM_STEP))
    return -ADAM_LR * (m_hat / (jnp.sqrt(v_hat) + ADAM_EPS) + ADAM_WD * w), m2, v2


def _adamw(slots, w, m, v, name, own=None, own_slot=None):
    ns, rows, cols = slots.shape
    br, bc = _ew_block(rows, cols, 2 * ns + 7)

    def update(g, w_ref, m_ref, v_ref, g_ref, d_ref, m2_ref, v2_ref):
        g_ref[...] = g
        d_ref[...], m2_ref[...], v2_ref[...] = _adam_math(g, w_ref[...], m_ref[...], v_ref[...])

    out_shape = [jax.ShapeDtypeStruct((rows, cols), F32)] * 4
    params = _params(("parallel", "parallel"), (2 * ns + 7) * _nbytes((br, bc), F32))
    grid = (rows // br, cols // bc)
    if own is None:
        def body(s_ref, *rest):
            g = s_ref[0].astype(F32)
            for k in range(1, ns):
                g = g + s_ref[k].astype(F32)
            update(g, *rest)

        blk = pl.BlockSpec((br, bc), lambda i, j: (i, j))
        return pl.pallas_call(
            body, name=name, grid=grid,
            in_specs=[pl.BlockSpec((ns, br, bc), lambda i, j: (0, i, j)), blk, blk, blk], out_specs=[blk] * 4,
            out_shape=out_shape, compiler_params=params,
        )(slots, w, m, v)

    def body_own(slot_ref, s_ref, o_ref, *rest):
        g = None
        for k in range(ns):
            term = jnp.where(slot_ref[0] == k, o_ref[k].astype(F32), s_ref[k].astype(F32))
            g = term if g is None else g + term
        update(g, *rest)

    blk = pl.BlockSpec((br, bc), lambda i, j, slot_ref: (i, j))
    stack = pl.BlockSpec((ns, br, bc), lambda i, j, slot_ref: (0, i, j))
    grid_spec = pltpu.PrefetchScalarGridSpec(num_scalar_prefetch=1, grid=grid, in_specs=[stack, stack, blk, blk, blk],
                                             out_specs=[blk] * 4)
    return pl.pallas_call(body_own, name=name, grid_spec=grid_spec, out_shape=out_shape, compiler_params=params,
                          )(own_slot, slots, own, w, m, v)


def _adamw_small(all_g, last_g, params, extra_shapes, name):
    names = [n for n, _ in _SMALL_ROWS if n in params]
    extras = [n for n, _ in _SMALL_ROWS if n not in params]
    offs = _small_offsets()
    n_p = len(names)

    def body(*refs):
        s_ref, last_ref = refs[0], refs[1]
        wmv = refs[2:2 + 3 * n_p]
        outs = refs[2 + 3 * n_p:2 + 7 * n_p]
        extra_refs = refs[2 + 7 * n_p:2 + 7 * n_p + len(extras)]
        summed = refs[-1]
        g, g_last = s_ref[0], last_ref[0]
        for k in range(1, N_DEV):
            g, g_last = g + s_ref[k], g_last + last_ref[k]
        summed[...] = g
        last_offs = _small_offsets(_LAST_SMALL)
        for n, rows in _LAST_SMALL:
            summed[offs[n]:offs[n] + rows, :] = g_last[last_offs[n]:last_offs[n] + rows, :]
        for i, n in enumerate(names):
            w_ref, m_ref, v_ref = wmv[3 * i:3 * i + 3]
            g_ref, d_ref, m2_ref, v2_ref = outs[4 * i:4 * i + 4]
            _rows_to(summed, offs[n], g_ref)
            d_ref[...], m2_ref[...], v2_ref[...] = _adam_math(g_ref[...], w_ref[...], m_ref[...], v_ref[...])
        for n, ref in zip(extras, extra_refs):
            _rows_to(summed, offs[n], ref)

    flat = [a for n in names for a in params[n]]
    out_shape = [jax.ShapeDtypeStruct(params[n][0].shape, F32) for n in names for _ in range(4)]
    out_shape += [jax.ShapeDtypeStruct(s, F32) for s in extra_shapes]
    vmem = pl.BlockSpec(memory_space=pltpu.VMEM)
    res = pl.pallas_call(
        body, name=name, in_specs=[vmem] * (2 + len(flat)), out_specs=[vmem] * len(out_shape), out_shape=out_shape,
        scratch_shapes=[pltpu.VMEM(all_g.shape[1:], F32)],
        compiler_params=pltpu.CompilerParams(vmem_limit_bytes=_vmem_limit(_nbytes(all_g.shape, F32))),
    )(all_g, last_g, *flat)
    return {n: res[4 * i:4 * i + 4] for i, n in enumerate(names)}, res[4 * n_p:]


def _mm_tiles(mode, m, n, k):
    tn = min(n, 1024)
    if mode == "tn":
        return min(m, 1024), tn, min(k, 4096)
    if k <= 1024:
        return min(m, 2048), tn, k
    if k <= 2048:
        return min(m, 1024), tn, k
    if k <= 4096:
        return min(m, 512), tn, k
    return min(m, 1024), tn, 2048


def _local_step(x, target, wts, small, exchange):
    t = x.shape[0]
    w_main_t, w_dt_t = wts["w_main_t"], wts["w_dt_t"]
    bsp_t = small["b_spatial"].T
    pad32 = lambda a: jnp.pad(a, ((0, 0), (0, DT_PAD - N_HEADS)))
    dtb, alog = pad32(small["dt_bias"]), pad32(small["a_log"])
    dskip_full = jnp.repeat(small["d_skip"], HEAD_DIM, axis=1)
    head_of_col = lax.broadcasted_iota(jnp.int32, (DT_PAD, D_INNER), 1) // HEAD_DIM
    e_bf = (head_of_col == lax.broadcasted_iota(jnp.int32, (DT_PAD, D_INNER), 0)).astype(BF16)

    def mm(a, b, mode, name, **kw):
        if mode == "nn":
            m, k, n = a.shape[0], a.shape[1], b.shape[1]
        elif mode == "nt":
            m, k, n = a.shape[0], a.shape[1], b.shape[0]
        else:
            m, k, n = a.shape[1], a.shape[0], b.shape[1]
        tm, tn, tk = _mm_tiles(mode, m, n, k)
        tm = min(tm, kw.pop("max_tm", tm))
        kw.setdefault("out_dtypes", (BF16,) if mode == "tn" else (F32,))
        if "extra_specs" in kw:
            kw["extra_specs"] = kw["extra_specs"](tm, tn)
        return _matmul(a, b, mode=mode, tm=tm, tn=tn, tk=tk, name=name, **kw)

    def out_tile(tm, tn):
        return (((tm, tn), lambda i, j: (i, j)),)

    def row_tiles(n_tiles, *vectors, gate_logits=False):
        def specs(tm, tn):
            out = [((tm, tn), lambda i, j: (i, j))] * n_tiles
            if gate_logits:
                out += [((tm, D_MODEL), lambda i, j, cb=COL_GATE // D_MODEL + half: (i, cb)) for half in range(2)]
            return tuple(out) + tuple(((1, w), lambda i, j, cb=cb: (0, cb)) for w, cb in vectors)
        return specs

    vec = lambda w: ((1, w), F32, (1, w), lambda i, j: (0, 0))
    fused_tm = 512

    h, dt_raw = _rms_fwd(x, small["norm_mix_g"], w_dt_t, "rms_mix", deps=exchange.begin())
    proj = mm(h, w_main_t, "nt", "proj_main", j_outer=True, out_dtypes=(BF16,))
    y_a = _gmlp_fwd(proj, small["v_norm_g"], small["v_norm_b"], small["w_spatial"], bsp_t, "gmlp_fwd")
    pre_conv, xc = _conv_fwd(proj, wts["conv_w"], small["conv_b"], "conv_fwd")
    y_ssd, y_b, sprev = _ssd_fwd(xc, proj, dt_raw, dtb, alog, dskip_full, small["ssm_norm_g"], e_bf, "ssd_fwd")
    wts = {**wts, **exchange.late_weights(y_b)}
    pa = mm(y_a, wts["w_proj_a"], "nn", "proj_a")
    pb, merged = mm(y_b, wts["w_proj_b"], "nn", "proj_b", epilogue=_merge_epilogue, out_dtypes=(F32, BF16), max_tm=fused_tm, row_parts=2,
                    extras=(pa, proj, proj, small["b_gates"], small["b_gates"]),
                    extra_specs=row_tiles(1, (D_MODEL, 0), (D_MODEL, 1), gate_logits=True))
    x1, h2 = mm(merged, wts["w_out"], "nn", "out_proj", epilogue=_residual_rms_epilogue, out_dtypes=(F32, BF16),
                max_tm=2 * fused_tm, row_parts=4, extras=(x, small["norm_mlp_g"]), extra_specs=row_tiles(1, (D_MODEL, 0)))

    def relu_sq(acc, ex, outs, first):
        r = jnp.maximum(acc, 0.0)
        outs[0][...] = (r * r).astype(BF16)

    act = mm(h2, wts["w_mlp_up"], "nn", "mlp_up", epilogue=relu_sq, out_dtypes=(BF16,), j_outer=True)
    dx2, dx2_b, g_final, _, loss = mm(
        act, wts["w_mlp_down"], "nn", "mlp_down", epilogue=_loss_epilogue, carry=True, row_parts=2,
        out_dtypes=(F32, BF16, vec(D_MODEL), vec(D_MODEL), vec(LANES)),
        extras=(x1, small["norm_final_g"], target), extra_specs=lambda tm, tn: (
            ((tm, tn), lambda i, j: (i, j)), ((1, tn), lambda i, j: (0, 0)), ((tm, tn), lambda i, j: (i, j))))

    def relu_sq_bwd(acc, ex, outs, first):
        outs[0][...] = (acc * 2.0 * jnp.sqrt(ex[0][...].astype(F32))).astype(BF16)

    dup = mm(dx2_b, wts["w_mlp_down"], "nt", "d_act", epilogue=relu_sq_bwd, extras=(act,), extra_specs=out_tile, row_parts=4,
             out_dtypes=(BF16,), j_outer=True)
    g_down = mm(act, dx2_b, "tn", "g_mlp_down")
    g_up = mm(h2, dup, "tn", "g_mlp_up")
    dx1, dx1_b, g_mlp = mm(
        dup, wts["w_mlp_up"], "nt", "d_h2", epilogue=_rms_bwd_epilogue, carry=True, row_parts=2,
        out_dtypes=(F32, BF16, vec(D_MODEL)), extras=(x1, small["norm_mlp_g"], dx2), extra_specs=lambda tm, tn: (
            ((tm, tn), lambda i, j: (i, j)), ((1, tn), lambda i, j: (0, 0)), ((tm, tn), lambda i, j: (i, j))))

    g_out = mm(merged, dx1_b, "tn", "g_out")
    dpa, dpb, dproj, g_bgates = mm(
        dx1_b, wts["w_out"], "nt", "d_merged", epilogue=_merge_bwd_epilogue, carry=True, max_tm=fused_tm, row_parts=2,
        out_dtypes=(BF16, BF16, ((t, MAIN_W), BF16, (fused_tm, 2 * D_MODEL), lambda i, j: (i, COL_GATE // (2 * D_MODEL))),
                    vec(2 * D_MODEL)),
        extras=(pa, pb, proj, proj, small["b_gates"], small["b_gates"]),
        extra_specs=row_tiles(2, (D_MODEL, 0), (D_MODEL, 1), gate_logits=True))
    g_pa = mm(y_a, dpa, "tn", "g_proj_a")
    g_pb = mm(y_b, dpb, "tn", "g_proj_b")
    started = exchange.reduce("late", {"w_mlp_down": g_down, "w_mlp_up": g_up, "w_out": g_out, "w_proj_a": g_pa,
                                       "w_proj_b": g_pb})
    dya = mm(dpa, wts["w_proj_a"], "nt", "d_ya", deps=started)
    dyb = mm(dpb, wts["w_proj_b"], "nt", "d_yb")

    dproj, g_wsp, g_bsp_t, g_vg, g_vb = _gmlp_bwd(proj, dya, small["v_norm_g"], small["v_norm_b"], small["w_spatial"],
                                                   bsp_t, dproj, "gmlp_bwd")
    dproj, dxc, ddt, g_ng, g_dskip, g_alog, g_dtb = _ssd_bwd(dyb, y_ssd, xc, proj, dt_raw, sprev, dtb, alog, dskip_full,
                                                             small["ssm_norm_g"], e_bf, dproj, "ssd_bwd")
    dproj, g_convw, g_convb = _conv_bwd(proj, pre_conv, dxc, wts["conv_w"], dproj, "conv_bwd")

    small_grads = {
        "conv_w": g_convw, "loss": loss,
        "conv_b": g_convb, "dt_bias": g_dtb, "a_log": g_alog, "d_skip": g_dskip, "ssm_norm_g": g_ng,
        "v_norm_g": g_vg, "v_norm_b": g_vb, "w_spatial": g_wsp.reshape(GROUPS * CHUNK, CHUNK), "b_spatial": g_bsp_t.T,
        "b_gates": g_bgates, "norm_mlp_g": g_mlp, "norm_final_g": g_final,
    }
    g_main_t = mm(dproj, h, "tn", "g_in_main", deps=exchange.small(small_grads))
    g_dt_t = mm(ddt, h, "tn", "g_in_dt")
    started = exchange.reduce("in", {"w_in": (g_main_t, g_dt_t)})

    def input_grad(acc, ex, outs, first):
        x_ref, g_ref, res_ref, ddt_ref, wdt_ref = ex
        gg = jnp.zeros((1, D_MODEL), F32)
        for r in range(acc.shape[0] // ROW_TILE):
            rows = slice(r * ROW_TILE, (r + 1) * ROW_TILE)
            dh = acc[rows] + _dot(ddt_ref[rows, :], wdt_ref[...], _NN)
            dx, gg_r = _rms_pullback(x_ref[rows, :], g_ref[...], dh)
            outs[0][rows, :] = dx + res_ref[rows, :]
            gg = gg + gg_r

        _zero_when(first, outs[1])
        outs[1][...] += gg

    grad_x, g_mix = mm(
        dproj, w_main_t, "nn", "d_h", epilogue=input_grad, deps=started, carry=True,
        out_dtypes=(F32, vec(D_MODEL)), extras=(x, small["norm_mix_g"], dx1, ddt, w_dt_t), extra_specs=lambda tm, tn: (
            ((tm, tn), lambda i, j: (i, j)), ((1, tn), lambda i, j: (0, 0)), ((tm, tn), lambda i, j: (i, j)),
            ((tm, DT_PAD), lambda i, j: (i, 0)), ((DT_PAD, D_MODEL), lambda i, j: (0, 0))))

    return grad_x, g_mix


SHARD_ROWS = (MAIN_W + N_HEADS) // N_DEV
REGROUP_IN = 2048


def _main_rows_of(gathered, name):
    n_dev, shard, d = gathered.shape
    blk = 1024
    nb = MAIN_W // blk

    def first_feature(b):
        return b * blk + (N_HEADS if b * blk >= COL_GATE else 0)

    def body(a_ref, b_ref, out_ref):
        for b in range(nb):
            s0, r0 = divmod(first_feature(b), shard)
            n1 = min(shard - r0, blk)

            @pl.when(pl.program_id(0) == b)
            def _(r0=r0, n1=n1):
                out_ref[0:n1, :] = a_ref[0, r0:r0 + n1, :]
                if n1 < blk:
                    out_ref[n1:blk, :] = b_ref[0, 0:blk - n1, :]

    def slot(b):
        return (b * blk + jnp.where(b * blk >= COL_GATE, N_HEADS, 0)) // shard

    return pl.pallas_call(
        body, name=name, grid=(nb,),
        in_specs=[pl.BlockSpec((1, shard, d), lambda b: (slot(b), 0, 0)),
                  pl.BlockSpec((1, shard, d), lambda b: (jnp.minimum(slot(b) + 1, n_dev - 1), 0, 0))],
        out_specs=pl.BlockSpec((blk, d), lambda b: (b, 0)),
        out_shape=jax.ShapeDtypeStruct((MAIN_W, d), gathered.dtype),
        compiler_params=_params(("parallel",), 3 * _nbytes((shard, d), gathered.dtype)),
    )(gathered, gathered)


def _by_device_rows(g_main_t, g_dt_t, name):
    d = g_main_t.shape[1]
    n_blocks = MAIN_W // REGROUP_IN
    dt_dev, dt_row = divmod(COL_GATE, SHARD_ROWS)

    def main_start(s):
        return s * SHARD_ROWS - (N_HEADS if s > dt_dev else 0)

    def body(a_ref, b_ref, dt_ref, out_ref):
        for s in range(N_DEV):
            m0 = main_start(s)
            k0, off = divmod(m0, REGROUP_IN)
            pieces = []
            if s == dt_dev:
                pieces = [(0, dt_row, m0), (dt_row, N_HEADS, None), (dt_row + N_HEADS, SHARD_ROWS - dt_row - N_HEADS, m0 + dt_row)]
            else:
                pieces = [(0, SHARD_ROWS, m0)]

            @pl.when(pl.program_id(0) == s)
            def _(pieces=pieces, k0=k0):
                for dst, n, src in pieces:
                    if src is None:
                        out_ref[0, dst:dst + n, :] = dt_ref[0:n, :]
                        continue
                    lo = src - k0 * REGROUP_IN
                    n_a = max(0, min(n, REGROUP_IN - lo))
                    if n_a:
                        out_ref[0, dst:dst + n_a, :] = a_ref[lo:lo + n_a, :]
                    if n_a < n:
                        lo_b = max(lo - REGROUP_IN, 0)
                        out_ref[0, dst + n_a:dst + n, :] = b_ref[lo_b:lo_b + n - n_a, :]

    def first_block(s):
        return (s * SHARD_ROWS - jnp.where(s > dt_dev, N_HEADS, 0)) // REGROUP_IN

    return pl.pallas_call(
        body, name=name, grid=(N_DEV,),
        in_specs=[pl.BlockSpec((REGROUP_IN, d), lambda s: (first_block(s), 0)),
                  pl.BlockSpec((REGROUP_IN, d), lambda s: (jnp.minimum(first_block(s) + 1, n_blocks - 1), 0)),
                  pl.BlockSpec((DT_PAD, d), lambda s: (0, 0))],
        out_specs=pl.BlockSpec((1, SHARD_ROWS, d), lambda s: (s, 0, 0)),
        out_shape=jax.ShapeDtypeStruct((N_DEV, SHARD_ROWS, d), g_main_t.dtype),
        compiler_params=_params(("parallel",), 3 * _nbytes((REGROUP_IN, d), g_main_t.dtype)),
    )(g_main_t, g_main_t, g_dt_t)


_LATE = ["w_proj_a", "w_proj_b", "w_out", "w_mlp_up", "w_mlp_down"]
_BY_COLS = ("w_mlp_up",)


class _Exchange:
    def __init__(self, late_shards, late_lands):
        self.late_shards, self.late_lands = late_shards, late_lands
        self.c_idx = lax.axis_index("c").astype(jnp.int32).reshape(1)
        self.chip_idx = (2 * lax.axis_index("x") + lax.axis_index("y")).astype(jnp.int32).reshape(1)
        self.pending = []

    def begin(self):
        self.late = _split_start(self.late_shards, self.late_lands, _gather_copies, N_DEV - 1, "gather_late_start")
        return [self.late[-1]]

    def late_weights(self, after):
        _, lands = _split_wait(self.late, _gather_copies, after, "gather_late_wait")
        whole = {}
        for n, g in zip(_LATE, lands):
            whole[n] = jnp.transpose(g, (1, 0, 2)).reshape(g.shape[1], -1) if n in _BY_COLS else g.reshape(-1, g.shape[2])
        return whole

    def reduce(self, tag, grads):
        names = list(grads)
        by_dev = []
        for n in names:
            g = grads[n]
            if n == "w_in":
                by_dev.append(_by_device_rows(*g, "regroup_g_in"))
            elif n in _BY_COLS:
                by_dev.append(jnp.transpose(g.reshape(g.shape[0], N_DEV, -1), (1, 0, 2)))
            else:
                by_dev.append(g.reshape(N_DEV, -1, g.shape[1]))
        from_sibling = _swap_with_sibling(by_dev, "reduce_cores_" + tag)
        parts = [_add_sibling(g, r, self.c_idx, "add_cores_" + n) for n, g, r in zip(names, by_dev, from_sibling)]
        lands = [lax.empty(p.shape, p.dtype) for p in parts]
        started = _split_start(parts, lands, _scatter_copies, 3, "reduce_chips_start_" + tag)
        self.pending.append((tag, names, started))
        return [started[-1]]

    def small(self, grads):
        dev = 2 * self.chip_idx + self.c_idx
        packed, land = _pack_small(grads, dev, "pack_small")
        self.small_started = _split_start([packed], [land], _gather_copies, N_DEV - 1, "exchange_small_start")
        return [self.small_started[-1]]

    def finish(self, after):
        _, (all_small,) = _split_wait(self.small_started, _gather_copies, after, "exchange_small_wait")
        done = {}
        for tag, names, started in self.pending:
            parts, lands = _split_wait(started, _scatter_copies, after, "reduce_chips_wait_" + tag)
            for n, land, part in zip(names, lands, parts):
                done[n] = (land, part, self.chip_idx)
        return all_small, done


def kernel(x, norm_mix_g, w_in, conv_w, conv_b, dt_bias, a_log, d_skip, ssm_norm_g, v_norm_g, v_norm_b, w_spatial, b_spatial, b_gates, w_proj_a, w_proj_b, w_out, norm_mlp_g, w_mlp_up, w_mlp_down, norm_final_g, loss_target, m_norm_mix_g, m_w_in, m_conv_w, m_conv_b, m_dt_bias, m_a_log, m_d_skip, m_ssm_norm_g, m_v_norm_g, m_v_norm_b, m_w_spatial, m_b_spatial, m_b_gates, m_w_proj_a, m_w_proj_b, m_w_out, m_norm_mlp_g, m_w_mlp_up, m_w_mlp_down, m_norm_final_g, v_norm_mix_g, v_w_in, v_conv_w, v_conv_b, v_dt_bias, v_a_log, v_d_skip, v_ssm_norm_g, v_v_norm_g, v_v_norm_b, v_w_spatial, v_b_spatial, v_b_gates, v_w_proj_a, v_w_proj_b, v_w_out, v_norm_mlp_g, v_w_mlp_up, v_w_mlp_down, v_norm_final_g):
    given = dict(locals())
    names = ["norm_mix_g", "w_in", "conv_w", "conv_b", "dt_bias", "a_log", "d_skip", "ssm_norm_g", "v_norm_g", "v_norm_b",
             "w_spatial", "b_spatial", "b_gates", "w_proj_a", "w_proj_b", "w_out", "norm_mlp_g", "w_mlp_up", "w_mlp_down",
             "norm_final_g"]
    shapes = {n: given[n].shape for n in names}
    dev = 4 * lax.axis_index("x") + 2 * lax.axis_index("y") + lax.axis_index("c")

    shard2d = {"w_in": w_in[0].T, "w_proj_a": w_proj_a[0], "w_proj_b": w_proj_b[0], "w_out": w_out[0],
               "w_mlp_up": w_mlp_up[0], "w_mlp_down": w_mlp_down[0]}
    conv_shard = conv_w.reshape(CONV_WIDTH, -1)
    late_shards = [shard2d[n].astype(BF16) for n in _LATE]
    w_in_all, conv_all, *late_lands = _all_gather([shard2d["w_in"].astype(BF16), conv_shard], "gather_first",
                                                  own_only=late_shards)
    dt_dev, dt_row = divmod(COL_GATE, SHARD_ROWS)
    w_dt_t = jnp.pad(w_in_all[dt_dev, dt_row:dt_row + N_HEADS], ((0, DT_PAD - N_HEADS), (0, 0)))
    wts = {"w_main_t": _main_rows_of(w_in_all, "regroup_w_in"), "w_dt_t": w_dt_t, "conv_w": jnp.transpose(conv_all, (1, 0, 2)).reshape(CONV_WIDTH, -1)}
    small = {"norm_mix_g": norm_mix_g, "conv_b": conv_b, "dt_bias": dt_bias, "a_log": a_log, "d_skip": d_skip,
             "ssm_norm_g": ssm_norm_g, "v_norm_g": v_norm_g, "v_norm_b": v_norm_b, "w_spatial": w_spatial[0],
             "b_spatial": b_spatial[0], "b_gates": b_gates, "norm_mlp_g": norm_mlp_g,
             "norm_final_g": norm_final_g.reshape(1, -1)}

    exchange = _Exchange(late_shards, late_lands)
    grad_x, g_mix = _local_step(x[0], loss_target[0], wts, small, exchange)

    out = {}
    all_small, large = exchange.finish(grad_x)
    for n, (slots, own, own_slot) in large.items():
        moments = [given["m_" + n][0], given["v_" + n][0]]
        if n == "w_in":
            moments = [mom.T for mom in moments]
        res = _adamw(slots, shard2d[n], *moments, "adamw_" + n, own=own, own_slot=own_slot)
        out[n] = [(r.T if n == "w_in" else r).reshape(shapes[n]) for r in res]

    last_small = _exchange_small({"norm_mix_g": g_mix}, _LAST_SMALL, "exchange_last")
    small["w_spatial"] = small["w_spatial"].reshape(GROUPS * CHUNK, CHUNK)
    params = {n: (w2d, given["m_" + n].reshape(w2d.shape), given["v_" + n].reshape(w2d.shape)) for n, w2d in small.items()}
    updated, (g_conv_full, loss_all) = _adamw_small(all_small, last_small, params, [(CONV_WIDTH, CONV_DIM), (1, LANES)],
                                                    "adamw_small")
    for n, res in updated.items():
        out[n] = [r.reshape(shapes[n]) for r in res]
    width = shapes["conv_w"][-1]
    g_conv = lax.dynamic_slice(g_conv_full, (0, dev * width), (CONV_WIDTH, width))
    res = _adamw(g_conv[None], conv_shard, m_conv_w.reshape(CONV_WIDTH, -1), v_conv_w.reshape(CONV_WIDTH, -1), "adamw_conv_w")
    out["conv_w"] = [r.reshape(shapes["conv_w"]) for r in res]

    loss = loss_all[0, 0]
    return (loss, grad_x[None], *[out[n][0] for n in names], *[out[n][1] for n in names],
            *[out[n][2] for n in names], *[out[n][3] for n in names])
```

```python
import functools
import math

import jax
import jax.numpy as jnp
from jax import lax
from jax.experimental import pallas as pl
from jax.experimental.pallas import tpu as pltpu

F32 = jnp.float32
BF16 = jnp.bfloat16
MESH = pl.DeviceIdType.MESH

D_MODEL = 1024
NORM_EPS = 1e-6
CHUNK = 128
GROUPS = 8
D_INNER = 2048
HEAD_DIM = 64
N_HEADS = 32
D_STATE = 128
CONV_WIDTH = 4
CONV_DIM = 4096
D_FF = 4096
GROUP_W = D_INNER // GROUPS
N_DEV = 8
N_CHIP = 4

ADAM_LR = 0.001
ADAM_B1 = 0.9
ADAM_B2 = 0.999
ADAM_EPS = 1e-08
ADAM_WD = 0.01
ADAM_STEP = 10

MAIN_W = 2 * D_MODEL + D_INNER + CONV_DIM + 2 * D_MODEL
COL_Z = 2048
COL_XBC = 4096
COL_GATE = 8192
DT_PAD = 128

LANES = 128
SUBLANES = 8
VMEM_BYTES_V7X = 64 * 1024 * 1024
VMEM_BODY_TEMP = 24 * 1024 * 1024


def _vmem_limit(block_bytes):
    return int(min(2 * block_bytes + VMEM_BODY_TEMP, VMEM_BYTES_V7X - 8 * 1024 * 1024))


def _nbytes(shape, dtype):
    return math.prod(shape) * jnp.dtype(dtype).itemsize


_HBM = pl.BlockSpec(memory_space=pl.ANY)


def _params(sem, block_bytes):
    return pltpu.CompilerParams(dimension_semantics=sem, vmem_limit_bytes=_vmem_limit(block_bytes))


def _sigmoid(x):
    return 1.0 / (1.0 + jnp.exp(-x))


def _softplus(x):
    e = jnp.exp(-jnp.abs(x))
    u = 1.0 + e
    log1p_e = jnp.where(u == 1.0, e, jnp.log(u) * (e / jnp.where(u == 1.0, 1.0, u - 1.0)))
    return jnp.maximum(x, 0.0) + log1p_e


_SQRT_HALF = 0.7071067811865476
_INV_SQRT_2PI = 0.3989422804014327


def _normal_cdf(x):
    return 0.5 * (1.0 + lax.erf(x * _SQRT_HALF))


def _gelu_grad(x, cdf):
    return cdf + x * jnp.exp(-0.5 * x * x) * _INV_SQRT_2PI


def _dot(a, b, dims):
    return lax.dot_general(a, b, (dims, ((), ())), preferred_element_type=F32)


_NN = ((1,), (0,))
_NT = ((1,), (1,))
_TN = ((0,), (0,))


def _split3(x):
    hi = x.astype(BF16)
    r1 = x - hi.astype(F32)
    mid = r1.astype(BF16)
    lo = (r1 - mid.astype(F32)).astype(BF16)
    return hi, mid, lo


def _dot_exact_rhs(x, e, dims):
    hi, mid, lo = _split3(x)
    return _dot(hi, e, dims) + _dot(mid, e, dims) + _dot(lo, e, dims)


def _dot_exact_lhs(e, x, dims):
    hi, mid, lo = _split3(x)
    return _dot(e, hi, dims) + _dot(e, mid, dims) + _dot(e, lo, dims)


def _tri(lower):
    r = lax.broadcasted_iota(jnp.int32, (CHUNK, CHUNK), 0)
    c = lax.broadcasted_iota(jnp.int32, (CHUNK, CHUNK), 1)
    return (r >= c) if lower else (r <= c)


def _matmul(a, b, *, mode, tm, tn, tk, out_dtypes, name, epilogue=None, extras=(), extra_specs=(), j_outer=False, deps=(),
            carry=False):
    if mode == "nn":
        (m, k), (_, n) = a.shape, b.shape
    elif mode == "nt":
        (m, k), (n, _) = a.shape, b.shape
    else:
        (k, m), (_, n) = a.shape, b.shape
    assert m % tm == 0 and n % tn == 0 and k % tk == 0, (name, m, n, k, tm, tn, tk)
    nk = k // tk
    n_extra, n_out = len(extras), len(out_dtypes)
    first_out = 2 + n_extra + len(deps)
    dims = {"nn": _NN, "nt": _NT, "tn": _TN}[mode]
    if epilogue is None:
        def epilogue(acc, ex, outs, first):
            outs[0][...] = acc.astype(outs[0].dtype)

    def body(*refs):
        a_ref, b_ref = refs[0], refs[1]
        ex_refs = refs[2:2 + n_extra]
        outs = refs[first_out:first_out + n_out]
        first_tile = pl.program_id(0) == 0
        p = _dot(a_ref[...], b_ref[...], dims)
        if nk == 1:
            epilogue(p, ex_refs, outs, first_tile)
            return
        acc_ref = refs[first_out + n_out]
        kk = pl.program_id(2)

        @pl.when(kk == 0)
        def _():
            acc_ref[...] = p

        @pl.when(kk > 0)
        def _():
            acc_ref[...] += p

        @pl.when(kk == nk - 1)
        def _():
            epilogue(acc_ref[...], ex_refs, outs, first_tile)

    if j_outer:
        grid = (n // tn, m // tm, nk)
        ij = lambda g0, g1: (g1, g0)
    else:
        grid = (m // tm, n // tn, nk)
        ij = lambda g0, g1: (g0, g1)

    def wrap(fn):
        return lambda g0, g1, kk: fn(*ij(g0, g1), kk)

    if mode == "nn":
        a_spec = pl.BlockSpec((tm, tk), wrap(lambda i, j, kk: (i, kk)))
        b_spec = pl.BlockSpec((tk, tn), wrap(lambda i, j, kk: (kk, j)))
        a_blk, b_blk = (tm, tk), (tk, tn)
    elif mode == "nt":
        a_spec = pl.BlockSpec((tm, tk), wrap(lambda i, j, kk: (i, kk)))
        b_spec = pl.BlockSpec((tn, tk), wrap(lambda i, j, kk: (j, kk)))
        a_blk, b_blk = (tm, tk), (tn, tk)
    else:
        a_spec = pl.BlockSpec((tk, tm), wrap(lambda i, j, kk: (kk, i)))
        b_spec = pl.BlockSpec((tk, tn), wrap(lambda i, j, kk: (kk, j)))
        a_blk, b_blk = (tk, tm), (tk, tn)
    ex_specs = [pl.BlockSpec(shape, wrap(lambda i, j, kk, f=f: f(i, j))) for shape, f in extra_specs]
    outs = [o if isinstance(o, tuple) else ((m, n), o, (tm, tn), lambda i, j: (i, j)) for o in out_dtypes]
    out_spec = [pl.BlockSpec(blk_shape, wrap(lambda i, j, kk, f=f: f(i, j))) for _, _, blk_shape, f in outs]
    out_shape = [jax.ShapeDtypeStruct(shape, dt) for shape, dt, _, _ in outs]
    blk = (_nbytes(a_blk, a.dtype) + _nbytes(b_blk, b.dtype) + sum(_nbytes(s, F32) for s, _ in extra_specs)
           + sum(_nbytes(blk_shape, dt) for _, dt, blk_shape, _ in outs) + _nbytes((tm, tn), F32))
    order = ("arbitrary",) * 3 if carry else ("parallel", "parallel", "arbitrary")
    res = pl.pallas_call(
        body, name=name, grid=grid,
        in_specs=[a_spec, b_spec] + ex_specs + [_HBM] * len(deps), out_specs=out_spec, out_shape=out_shape,
        scratch_shapes=[pltpu.VMEM((tm, tn), F32)] if nk > 1 else [],
        compiler_params=_params(order, blk),
    )(a, b, *extras, *deps)
    return res[0] if n_out == 1 else res


ROW_TILE = 256


def _row_spec(width, col_block=0, tile=ROW_TILE):
    return pl.BlockSpec((tile, width), lambda i, cb=col_block: (i, cb))


def _vec_spec(width, col_block=0):
    return pl.BlockSpec((1, width), lambda i, cb=col_block: (0, cb))


def _rms_fwd(x, g, w_t, name, deps=()):
    t = x.shape[0]
    n_small = w_t.shape[0]

    def body(x_ref, g_ref, w_ref, *rest):
        h_ref, small_ref = rest[-2:]
        xv = x_ref[...]
        r = lax.rsqrt(jnp.mean(xv * xv, axis=-1, keepdims=True) + NORM_EPS)
        h = (xv * r * g_ref[...]).astype(BF16)
        h_ref[...] = h
        small_ref[...] = _dot(h, w_ref[...], _NT)

    return pl.pallas_call(
        body, name=name, grid=(t // ROW_TILE,),
        in_specs=[_row_spec(D_MODEL), _vec_spec(D_MODEL), pl.BlockSpec((n_small, D_MODEL), lambda i: (0, 0))]
        + [_HBM] * len(deps),
        out_specs=[_row_spec(D_MODEL), _row_spec(n_small)],
        out_shape=[jax.ShapeDtypeStruct((t, D_MODEL), BF16), jax.ShapeDtypeStruct((t, n_small), F32)],
        compiler_params=_params(("parallel",), 3 * _nbytes((ROW_TILE, D_MODEL), F32)),
    )(x, g, w_t, *deps)


def _rms_scale(xv):
    r = lax.rsqrt(jnp.mean(xv * xv, axis=-1, keepdims=True) + NORM_EPS)
    return r, xv * r


def _rms_pullback(xv, g, dh):
    r, xh = _rms_scale(xv)
    dyg = dh * g
    return r * (dyg - xh * jnp.mean(dyg * xh, axis=-1, keepdims=True)), jnp.sum(dh * xh, axis=0, keepdims=True)


def _zero_when(first, *refs):
    @pl.when(first)
    def _():
        for ref in refs:
            ref[...] = jnp.zeros_like(ref)


def _residual_rms_epilogue(acc, ex, outs, first):
    x1 = acc + ex[0][...]
    outs[0][...] = x1
    _, xh = _rms_scale(x1)
    outs[1][...] = (xh * ex[1][...]).astype(BF16)


def _loss_epilogue(acc, ex, outs, first):
    dx_ref, dxb_ref, gg_ref, sq_ref, tot_ref = outs
    gv = ex[1][...]
    r, xh = _rms_scale(acc + ex[0][...])
    err = xh * gv - ex[2][...]
    dy = err * (1.0 / D_MODEL)
    dyg = dy * gv
    dx = r * (dyg - xh * jnp.mean(dyg * xh, axis=-1, keepdims=True))
    dx_ref[...] = dx
    dxb_ref[...] = dx.astype(BF16)

    _zero_when(first, gg_ref, sq_ref)
    gg_ref[...] += jnp.sum(dy * xh, axis=0, keepdims=True)
    sq_ref[...] += jnp.sum(err * err, axis=0, keepdims=True)
    tot_ref[...] = jnp.broadcast_to(jnp.sum(sq_ref[...], axis=1, keepdims=True) * (0.5 / D_MODEL), tot_ref.shape)


def _rms_bwd_epilogue(dh, ex, outs, first):
    dx, gg = _rms_pullback(ex[0][...], ex[1][...], dh)
    dx = dx + ex[2][...]
    outs[0][...] = dx
    if len(outs) == 3:
        outs[1][...] = dx.astype(BF16)

    _zero_when(first, outs[-1])
    outs[-1][...] += gg


def _merge_epilogue(acc, ex, outs, first):
    outs[0][...] = acc
    ga = _sigmoid(ex[1][...].astype(F32) + ex[3][...])
    gb = _sigmoid(ex[2][...].astype(F32) + ex[4][...])
    outs[1][...] = (ga * ex[0][...] + gb * acc).astype(BF16)


def _merge_bwd_epilogue(dm, ex, outs, first):
    dpa_ref, dpb_ref, dgl_ref, gb_ref = outs
    ga = _sigmoid(ex[2][...].astype(F32) + ex[4][...])
    gb = _sigmoid(ex[3][...].astype(F32) + ex[5][...])
    dpa_ref[...] = (dm * ga).astype(BF16)
    dpb_ref[...] = (dm * gb).astype(BF16)
    dla = dm * ex[0][...] * ga * (1.0 - ga)
    dlb = dm * ex[1][...] * gb * (1.0 - gb)
    dgl_ref[:, :D_MODEL] = dla.astype(BF16)
    dgl_ref[:, D_MODEL:] = dlb.astype(BF16)

    _zero_when(first, gb_ref)
    gb_ref[:, :D_MODEL] += jnp.sum(dla, axis=0, keepdims=True)
    gb_ref[:, D_MODEL:] += jnp.sum(dlb, axis=0, keepdims=True)


GMLP_TILE = 512
GMLP_NC = GMLP_TILE // CHUNK


def _gmlp_common(u_pre, v_pre, vg, vb):
    cdf_u, cdf_v = _normal_cdf(u_pre), _normal_cdf(v_pre)
    u = u_pre * cdf_u
    v = v_pre * cdf_v
    mu = jnp.mean(v, axis=-1, keepdims=True)
    vc = v - mu
    rstd = lax.rsqrt(jnp.mean(vc * vc, axis=-1, keepdims=True) + NORM_EPS)
    vh = vc * rstd
    vn = vh * vg + vb
    return u, vh, vn, rstd, cdf_u, cdf_v


def _chunks_to_lanes(x, g):
    return jnp.concatenate([x[c * CHUNK:(c + 1) * CHUNK, g * CHUNK:(g + 1) * CHUNK] for c in range(GMLP_NC)], axis=1)


def _gmlp_fwd(proj, vg, vb, wsp, bsp_t, name):
    t = proj.shape[0]

    def body(u_ref, v_ref, vg_ref, vb_ref, w_ref, b_ref, ya_ref):
        u, _, vn, _, _, _ = _gmlp_common(u_ref[...].astype(F32), v_ref[...].astype(F32), vg_ref[...], vb_ref[...])
        mask = _tri(True)
        bt = b_ref[...]
        for g in range(GROUPS):
            w = jnp.where(mask, w_ref[g], 0.0).astype(BF16)
            vcat = _chunks_to_lanes(vn, g).astype(BF16)
            s = _dot(w, vcat, _NN) + bt[:, g:g + 1]
            for c in range(GMLP_NC):
                rows, cols = slice(c * CHUNK, (c + 1) * CHUNK), slice(g * CHUNK, (g + 1) * CHUNK)
                ya_ref[rows, cols] = (u[rows, cols] * s[:, c * CHUNK:(c + 1) * CHUNK]).astype(BF16)

    return pl.pallas_call(
        body, name=name, grid=(t // GMLP_TILE,),
        in_specs=[_row_spec(D_MODEL, 0, GMLP_TILE), _row_spec(D_MODEL, 1, GMLP_TILE), _vec_spec(D_MODEL),
                  _vec_spec(D_MODEL), pl.BlockSpec((GROUPS, CHUNK, CHUNK), lambda i: (0, 0, 0)),
                  pl.BlockSpec((CHUNK, GROUPS), lambda i: (0, 0))],
        out_specs=_row_spec(D_MODEL, 0, GMLP_TILE),
        out_shape=jax.ShapeDtypeStruct((t, D_MODEL), BF16),
        compiler_params=_params(("parallel",), 3 * _nbytes((GMLP_TILE, D_MODEL), F32)),
    )(proj, proj, vg, vb, wsp, bsp_t)


def _gmlp_bwd(proj, dya, vg, vb, wsp, bsp_t, dproj, name):
    t = proj.shape[0]

    def body(u_ref, v_ref, dya_ref, vg_ref, vb_ref, w_ref, b_ref, dproj_in, duv_ref, gw_ref, gbt_ref, gvg_ref, gvb_ref,
             dvn_scr, du_scr):
        del dproj_in
        u_pre, v_pre = u_ref[...].astype(F32), v_ref[...].astype(F32)
        vgv = vg_ref[...]
        u, vh, vn, rstd, cdf_u, cdf_v = _gmlp_common(u_pre, v_pre, vgv, vb_ref[...])
        dya = dya_ref[...]
        mask = _tri(True)
        bt = b_ref[...]
        first = pl.program_id(0) == 0

        @pl.when(first)
        def _():
            gw_ref[...] = jnp.zeros_like(gw_ref)
            gbt_ref[...] = jnp.zeros_like(gbt_ref)
            gvg_ref[...] = jnp.zeros_like(gvg_ref)
            gvb_ref[...] = jnp.zeros_like(gvb_ref)

        lane = lax.broadcasted_iota(jnp.int32, (CHUNK, GROUPS), 1)
        gbt = jnp.zeros((CHUNK, GROUPS), F32)
        for g in range(GROUPS):
            w = jnp.where(mask, w_ref[g], 0.0).astype(BF16)
            vcat = _chunks_to_lanes(vn, g).astype(BF16)
            s = _dot(w, vcat, _NN) + bt[:, g:g + 1]
            ds = _chunks_to_lanes(dya * u, g)
            gbt = jnp.where(lane == g, jnp.sum(ds, axis=1, keepdims=True), gbt)
            dsb = ds.astype(BF16)
            gw_ref[g] += jnp.where(mask, _dot(dsb, vcat, _NT), 0.0)
            dv = _dot(w, dsb, _TN)
            for c in range(GMLP_NC):
                rows, cols = slice(c * CHUNK, (c + 1) * CHUNK), slice(g * CHUNK, (g + 1) * CHUNK)
                dvn_scr[rows, cols] = dv[:, c * CHUNK:(c + 1) * CHUNK]
                du_scr[rows, cols] = dya[rows, cols] * s[:, c * CHUNK:(c + 1) * CHUNK]
        gbt_ref[...] += gbt
        dvn = dvn_scr[...]
        gvg_ref[...] += jnp.sum(dvn * vh, axis=0, keepdims=True)
        gvb_ref[...] += jnp.sum(dvn, axis=0, keepdims=True)
        dvh = dvn * vgv
        dv = rstd * (dvh - jnp.mean(dvh, axis=-1, keepdims=True) - vh * jnp.mean(dvh * vh, axis=-1, keepdims=True))
        duv_ref[:, :D_MODEL] = (du_scr[...] * _gelu_grad(u_pre, cdf_u)).astype(BF16)
        duv_ref[:, D_MODEL:] = (dv * _gelu_grad(v_pre, cdf_v)).astype(BF16)

    return pl.pallas_call(
        body, name=name, grid=(t // GMLP_TILE,),
        in_specs=[_row_spec(D_MODEL, 0, GMLP_TILE), _row_spec(D_MODEL, 1, GMLP_TILE), _row_spec(D_MODEL, 0, GMLP_TILE),
                  _vec_spec(D_MODEL), _vec_spec(D_MODEL), pl.BlockSpec((GROUPS, CHUNK, CHUNK), lambda i: (0, 0, 0)),
                  pl.BlockSpec((CHUNK, GROUPS), lambda i: (0, 0)), pl.BlockSpec(memory_space=pl.ANY)],
        out_specs=[_row_spec(2 * D_MODEL, 0, GMLP_TILE), pl.BlockSpec((GROUPS, CHUNK, CHUNK), lambda i: (0, 0, 0)),
                   pl.BlockSpec((CHUNK, GROUPS), lambda i: (0, 0)), _vec_spec(D_MODEL), _vec_spec(D_MODEL)],
        out_shape=[jax.ShapeDtypeStruct(dproj.shape, BF16), jax.ShapeDtypeStruct((GROUPS, CHUNK, CHUNK), F32),
                   jax.ShapeDtypeStruct((CHUNK, GROUPS), F32), jax.ShapeDtypeStruct((1, D_MODEL), F32),
                   jax.ShapeDtypeStruct((1, D_MODEL), F32)],
        scratch_shapes=[pltpu.VMEM((GMLP_TILE, D_MODEL), F32), pltpu.VMEM((GMLP_TILE, D_MODEL), F32)],
        input_output_aliases={7: 0},
        compiler_params=_params(("arbitrary",), 6 * _nbytes((GMLP_TILE, D_MODEL), F32)),
    )(proj, proj, dya, vg, vb, wsp, bsp_t, dproj)


CONV_TILE = 1024
CONV_COLS = 1024
CONV_RB = 32
HALO = SUBLANES


def _conv_fwd(proj, cw, cb, name):
    t = proj.shape[0]
    nj = CONV_DIM // CONV_COLS
    xcb = COL_XBC // CONV_COLS
    before = 2 * HALO
    rb = CONV_TILE // before

    def body(x_ref, prev_ref, cw_ref, cb_ref, pre_ref, xc_ref):
        i = pl.program_id(1)
        cw_v = cw_ref[...]
        cb_v = cb_ref[...]
        for b in range(CONV_TILE // CONV_RB):
            if b == 0:
                prev = jnp.where(i > 0, prev_ref[...].astype(F32)[HALO:, :], 0.0)
                ext = jnp.concatenate([prev, x_ref[:CONV_RB, :].astype(F32)], axis=0)
            else:
                ext = x_ref[b * CONV_RB - before:(b + 1) * CONV_RB, :].astype(F32)[HALO:, :]
            pre = cb_v + cw_v[CONV_WIDTH - 1:CONV_WIDTH, :] * ext[HALO:, :]
            for k in range(CONV_WIDTH - 1):
                back = CONV_WIDTH - 1 - k
                pre = pre + cw_v[k:k + 1, :] * pltpu.roll(ext, back, 0)[HALO:, :]
            pre_ref[b * CONV_RB:(b + 1) * CONV_RB, :] = pre
            xc_ref[b * CONV_RB:(b + 1) * CONV_RB, :] = pre * _sigmoid(pre)

    tile = pl.BlockSpec((CONV_TILE, CONV_COLS), lambda j, i: (i, j))
    return pl.pallas_call(
        body, name=name, grid=(nj, t // CONV_TILE),
        in_specs=[pl.BlockSpec((CONV_TILE, CONV_COLS), lambda j, i: (i, xcb + j)),
                  pl.BlockSpec((before, CONV_COLS), lambda j, i: (jnp.maximum(i * rb - 1, 0), xcb + j)),
                  pl.BlockSpec((CONV_WIDTH, CONV_COLS), lambda j, i: (0, j)),
                  pl.BlockSpec((1, CONV_COLS), lambda j, i: (0, j))],
        out_specs=[tile, tile],
        out_shape=[jax.ShapeDtypeStruct((t, CONV_DIM), F32), jax.ShapeDtypeStruct((t, CONV_DIM), F32)],
        compiler_params=_params(("parallel", "parallel"), 4 * _nbytes((CONV_TILE, CONV_COLS), F32)),
    )(proj, proj, cw, cb)


def _fold_rows(v):
    out = v[:SUBLANES]
    for r in range(1, v.shape[0] // SUBLANES):
        out = out + v[r * SUBLANES:(r + 1) * SUBLANES]
    return out


def _conv_bwd(proj, pre, dxc, cw, dproj, name):
    t = proj.shape[0]
    nj = CONV_DIM // CONV_COLS
    ni = t // CONV_TILE
    xcb = COL_XBC // CONV_COLS
    rb = CONV_TILE // HALO
    last_rb = t // HALO - 1

    def body(x_ref, p_ref, pnext_ref, d_ref, dnext_ref, cw_ref, dproj_in, dx_ref, gw_ref, gb_ref):
        del dproj_in
        i = pl.program_id(1)
        cw_v = cw_ref[...]

        def dpre_of(p, d):
            sg = _sigmoid(p)
            return d * sg * (1.0 + p * (1.0 - sg))

        @pl.when(i == 0)
        def _():
            gw_ref[...] = jnp.zeros_like(gw_ref)
            gb_ref[...] = jnp.zeros_like(gb_ref)

        head = dpre_of(pnext_ref[...], jnp.where(i < ni - 1, dnext_ref[...], 0.0))
        gb_acc = jnp.zeros((SUBLANES, CONV_COLS), F32)
        gw_acc = [jnp.zeros((SUBLANES, CONV_COLS), F32) for _ in range(CONV_WIDTH)]
        for b in reversed(range(CONV_TILE // CONV_RB)):
            rows = slice(b * CONV_RB, (b + 1) * CONV_RB)
            cur = dpre_of(p_ref[rows, :], d_ref[rows, :])
            ext = jnp.concatenate([cur, head], axis=0)
            xv = x_ref[rows, :].astype(F32)
            dx = None
            for k in range(CONV_WIDTH):
                shift = CONV_WIDTH - 1 - k
                win = cur if shift == 0 else pltpu.roll(ext, CONV_RB + HALO - shift, 0)[:CONV_RB, :]
                term = cw_v[k:k + 1, :] * win
                dx = term if dx is None else dx + term
                gw_acc[k] = gw_acc[k] + _fold_rows(win * xv)
            dx_ref[rows, :] = dx.astype(BF16)
            gb_acc = gb_acc + _fold_rows(cur)
            head = cur[:HALO]
        gb_ref[...] += jnp.sum(gb_acc, axis=0, keepdims=True)
        for k in range(CONV_WIDTH):
            gw_ref[k:k + 1, :] += jnp.sum(gw_acc[k], axis=0, keepdims=True)

    tile = pl.BlockSpec((CONV_TILE, CONV_COLS), lambda j, i: (i, j))
    after = pl.BlockSpec((HALO, CONV_COLS), lambda j, i: (jnp.minimum((i + 1) * rb, last_rb), j))
    return pl.pallas_call(
        body, name=name, grid=(nj, ni),
        in_specs=[pl.BlockSpec((CONV_TILE, CONV_COLS), lambda j, i: (i, xcb + j)), tile, after, tile, after,
                  pl.BlockSpec((CONV_WIDTH, CONV_COLS), lambda j, i: (0, j)),
                  pl.BlockSpec(memory_space=pl.ANY)],
        out_specs=[pl.BlockSpec((CONV_TILE, CONV_COLS), lambda j, i: (i, xcb + j)),
                   pl.BlockSpec((CONV_WIDTH, CONV_COLS), lambda j, i: (0, j)),
                   pl.BlockSpec((1, CONV_COLS), lambda j, i: (0, j))],
        out_shape=[jax.ShapeDtypeStruct(dproj.shape, BF16), jax.ShapeDtypeStruct((CONV_WIDTH, CONV_DIM), F32),
                   jax.ShapeDtypeStruct((1, CONV_DIM), F32)],
        input_output_aliases={6: 0},
        compiler_params=_params(("parallel", "arbitrary"), 4 * _nbytes((CONV_TILE, CONV_COLS), F32)),
    )(proj, pre, pre, dxc, dxc, cw, dproj)


def _ssd_decays(dt_raw, dtb, alog, e_bf, tril_bf):
    dtv = _softplus(dt_raw + dtb)
    a = -jnp.exp(alog)
    cs = _dot_exact_lhs(tril_bf, dtv * a, _NN)
    cs_last = cs[CHUNK - 1:CHUNK, :]
    stack = jnp.concatenate([dtv, jnp.exp(cs), jnp.exp(cs_last - cs)], axis=0)
    full = _head_expand(stack, e_bf)
    return dtv, a, cs, full[:CHUNK], full[CHUNK:2 * CHUNK], full[2 * CHUNK:]


def _split2(x):
    hi = x.astype(BF16)
    return hi, (x - hi.astype(F32)).astype(BF16)


def _head_expand(x, e_bf):
    hi, mid = _split2(x)
    return _dot(hi, e_bf, _NN) + _dot(mid, e_bf, _NN)


def _head_sums(x, e_bf):
    hi, mid = _split2(x)
    return _dot(hi, e_bf, _NT) + _dot(mid, e_bf, _NT)


def _head_mats(cs, cs_t, cb, h, mask):
    seg = cs[:, h:h + 1] - cs_t[h:h + 1, :]
    lmat = jnp.exp(jnp.where(mask, seg, -jnp.inf))
    return lmat, cb * lmat


def _ssd_fwd(xc, proj, dt_raw, dtb, alog, dskip_full, ng, e_bf, name):
    t = xc.shape[0]
    nc = t // CHUNK
    zcb = COL_Z // D_INNER

    def body(xc_ref, z_ref, dt_ref, dtb_ref, alog_ref, dsk_ref, ng_ref, e_ref, y_ref, yb_ref, sprev_ref, s_scr):
        @pl.when(pl.program_id(0) == 0)
        def _():
            s_scr[...] = jnp.zeros_like(s_scr)

        mask = _tri(True)
        tril_bf = mask.astype(BF16)
        e_v = e_ref[...]
        _, _, cs, dt_full, ecs_full, decay_full = _ssd_decays(dt_ref[...], dtb_ref[...], alog_ref[...], e_v, tril_bf)
        cs_t = cs.T
        sprev_ref[0] = s_scr[...]
        for g in range(GROUPS):
            gc = slice(g * GROUP_W, (g + 1) * GROUP_W)
            xs = xc_ref[:, gc]
            xdt = xs * dt_full[:, gc]
            xdt_b = xdt.astype(BF16)
            xdec = (xdt * decay_full[:, gc]).astype(BF16)
            bg = xc_ref[:, D_INNER + g * D_STATE:D_INNER + (g + 1) * D_STATE].astype(BF16)
            cg = xc_ref[:, D_INNER + GROUPS * D_STATE + g * D_STATE:D_INNER + GROUPS * D_STATE + (g + 1) * D_STATE].astype(BF16)
            cb = _dot(cg, bg, _NT)
            s_prev = s_scr[:, gc]
            y_off = ecs_full[:, gc] * _dot(cg, s_prev.astype(BF16), _NN)
            s_scr[:, gc] = s_prev * ecs_full[CHUNK - 1:CHUNK, gc] + _dot(bg, xdec, _TN)
            parts = []
            for r in range(GROUP_W // HEAD_DIM):
                h = g * (GROUP_W // HEAD_DIM) + r
                _, m = _head_mats(cs, cs_t, cb, h, mask)
                parts.append(_dot(m.astype(BF16), xdt_b[:, r * HEAD_DIM:(r + 1) * HEAD_DIM], _NN))
            yg = jnp.concatenate(parts, axis=1) + y_off + dsk_ref[:, gc] * xs
            y_ref[:, gc] = yg
            zv = z_ref[:, gc].astype(F32)
            ygate = yg * (zv * _sigmoid(zv))
            rstd = lax.rsqrt(jnp.mean(ygate * ygate, axis=-1, keepdims=True) + NORM_EPS)
            yb_ref[:, gc] = (ygate * rstd * ng_ref[:, gc]).astype(BF16)

    vec = lambda w: pl.BlockSpec((1, w), lambda i: (0, 0))
    blk = _nbytes((CHUNK, CONV_DIM), F32) + 3 * _nbytes((CHUNK, D_INNER), F32) + _nbytes((D_STATE, D_INNER), F32)
    return pl.pallas_call(
        body, name=name, grid=(nc,),
        in_specs=[pl.BlockSpec((CHUNK, CONV_DIM), lambda i: (i, 0)), pl.BlockSpec((CHUNK, D_INNER), lambda i: (i, zcb)),
                  pl.BlockSpec((CHUNK, DT_PAD), lambda i: (i, 0)), vec(DT_PAD), vec(DT_PAD), vec(D_INNER), vec(D_INNER),
                  pl.BlockSpec((DT_PAD, D_INNER), lambda i: (0, 0))],
        out_specs=[pl.BlockSpec((CHUNK, D_INNER), lambda i: (i, 0)), pl.BlockSpec((CHUNK, D_INNER), lambda i: (i, 0)),
                   pl.BlockSpec((1, D_STATE, D_INNER), lambda i: (i, 0, 0))],
        out_shape=[jax.ShapeDtypeStruct((t, D_INNER), F32), jax.ShapeDtypeStruct((t, D_INNER), BF16),
                   jax.ShapeDtypeStruct((nc, D_STATE, D_INNER), F32)],
        scratch_shapes=[pltpu.VMEM((D_STATE, D_INNER), F32)],
        compiler_params=_params(("arbitrary",), blk),
    )(xc, proj, dt_raw, dtb, alog, dskip_full, ng, e_bf)


def _ssd_bwd(dyb, y, xc, proj, dt_raw, sprev, dtb, alog, dskip_full, ng, e_bf, dproj, name):
    t = xc.shape[0]
    nc = t // CHUNK
    zcb = COL_Z // D_INNER
    hpg = GROUP_W // HEAD_DIM
    rev = lambda i: nc - 1 - i

    def body(dyb_ref, y_ref, xc_ref, z_ref, dt_ref, sprev_ref, dtb_ref, alog_ref, dsk_ref, ng_ref, e_ref, dproj_in,
             dz_ref, dxc_ref, ddt_ref, gng_ref, gdsk_ref, galog_ref, gdtb_ref, ds_scr, sums_scr):
        del dproj_in

        @pl.when(pl.program_id(0) == 0)
        def _():
            ds_scr[...] = jnp.zeros_like(ds_scr)
            gng_ref[...] = jnp.zeros_like(gng_ref)
            gdsk_ref[...] = jnp.zeros_like(gdsk_ref)
            galog_ref[...] = jnp.zeros_like(galog_ref)
            gdtb_ref[...] = jnp.zeros_like(gdtb_ref)

        mask = _tri(True)
        tril_bf = mask.astype(BF16)
        triu_bf = _tri(False).astype(BF16)
        e_v = e_ref[...]
        dt_in = dt_ref[...] + dtb_ref[...]
        dtv, a, cs, dt_full, ecs_full, decay_full = _ssd_decays(dt_ref[...], dtb_ref[...], alog_ref[...], e_v, tril_bf)
        cs_t = cs.T

        lane_h = lax.broadcasted_iota(jnp.int32, (CHUNK, DT_PAD), 1)
        sub_h = lax.broadcasted_iota(jnp.int32, (DT_PAD, CHUNK), 0)
        dcs_rows = jnp.zeros((CHUNK, DT_PAD), F32)
        dcs_cols_t = jnp.zeros((DT_PAD, CHUNK), F32)
        last_cols, dsk_cols = [], []
        for g in range(GROUPS):
            gc = slice(g * GROUP_W, (g + 1) * GROUP_W)
            b_cols = slice(D_INNER + g * D_STATE, D_INNER + (g + 1) * D_STATE)
            c_cols = slice(D_INNER + GROUPS * D_STATE + g * D_STATE, D_INNER + GROUPS * D_STATE + (g + 1) * D_STATE)
            xs = xc_ref[:, gc]
            xdt = xs * dt_full[:, gc]
            xdt_b = xdt.astype(BF16)
            xdec = xdt * decay_full[:, gc]
            xdec_b = xdec.astype(BF16)
            zv = z_ref[:, gc].astype(F32)
            sg = _sigmoid(zv)
            gate = zv * sg
            yv = y_ref[:, gc]
            dybv = dyb_ref[:, gc]
            ygate = yv * gate
            rstd = lax.rsqrt(jnp.mean(ygate * ygate, axis=-1, keepdims=True) + NORM_EPS)
            yn = ygate * rstd
            gng_ref[:, gc] += jnp.sum(dybv * yn, axis=0, keepdims=True)
            dyn = dybv * ng_ref[:, gc]
            dyg = rstd * (dyn - yn * jnp.mean(dyn * yn, axis=-1, keepdims=True))
            dz_ref[:, gc] = (dyg * yv * sg * (1.0 + zv * (1.0 - sg))).astype(BF16)
            dy = dyg * gate
            dy_b = dy.astype(BF16)
            dyo = dy * ecs_full[:, gc]
            dyo_b = dyo.astype(BF16)
            dsk_cols.append(jnp.sum(dy * xs, axis=0, keepdims=True))

            bg = xc_ref[:, b_cols].astype(BF16)
            cg = xc_ref[:, c_cols].astype(BF16)
            s_prev = sprev_ref[0, :, gc]
            s_prev_b = s_prev.astype(BF16)
            dsg = ds_scr[:, gc]
            dsg_b = dsg.astype(BF16)
            cb = _dot(cg, bg, _NT)
            c_s = _dot(cg, s_prev_b, _NN)
            b_ds = _dot(bg, dsg_b, _NN)
            dcb = jnp.zeros((CHUNK, CHUNK), F32)
            parts = []
            for r in range(hpg):
                h = g * hpg + r
                hc = slice(r * HEAD_DIM, (r + 1) * HEAD_DIM)
                lmat, m = _head_mats(cs, cs_t, cb, h, mask)
                dm = _dot(dy_b[:, hc], xdt_b[:, hc], _NT)
                parts.append(_dot(m.astype(BF16), dy_b[:, hc], _TN))
                dcb = dcb + dm * lmat
                w = dm * m
                dcs_rows = jnp.where(lane_h == h, jnp.sum(w, axis=1, keepdims=True), dcs_rows)
                dcs_cols_t = jnp.where(sub_h == h, jnp.sum(w, axis=0, keepdims=True), dcs_cols_t)
            dxdt = jnp.concatenate(parts, axis=1) + decay_full[:, gc] * b_ds
            dcb_b = dcb.astype(BF16)
            dxc_ref[:, c_cols] = _dot(dcb_b, bg, _NN) + _dot(dyo_b, s_prev_b, _NT)
            dxc_ref[:, b_cols] = _dot(dcb_b, cg, _TN) + _dot(xdec_b, dsg_b, _NT)
            cdec = ecs_full[CHUNK - 1:CHUNK, gc]
            ds_scr[:, gc] = _dot(cg, dyo_b, _TN) + cdec * dsg
            dxc_ref[:, gc] = dxdt * dt_full[:, gc] + dsk_ref[:, gc] * dy
            dec_prod = xdec * b_ds
            sums_scr[:CHUNK, gc] = dyo * c_s - dec_prod
            sums_scr[CHUNK:, gc] = dxdt * xs
            last_cols.append(jnp.sum(dec_prod, axis=0, keepdims=True) + cdec * jnp.sum(dsg * s_prev, axis=0, keepdims=True))
        t_sums = _head_sums(sums_scr[...], e_v)
        tail = jnp.concatenate([jnp.concatenate(last_cols, axis=1), jnp.concatenate(dsk_cols, axis=1),
                                jnp.zeros((SUBLANES - 2, D_INNER), F32)], axis=0)
        t_tail = _dot_exact_rhs(tail, e_v, _NT)
        gdsk_ref[...] += t_tail[1:2, :]
        row = lax.broadcasted_iota(jnp.int32, (CHUNK, DT_PAD), 0)
        dcs = dcs_rows - dcs_cols_t.T + t_sums[:CHUNK] + jnp.where(row == CHUNK - 1, t_tail[0:1, :], 0.0)
        dda = _dot_exact_lhs(triu_bf, dcs, _NN)
        galog_ref[...] += jnp.sum(dda * dtv, axis=0, keepdims=True) * a
        ddt = dda * a + t_sums[CHUNK:]
        ddt_raw = jnp.where(lane_h < N_HEADS, ddt * _sigmoid(dt_in), 0.0)
        gdtb_ref[...] += jnp.sum(ddt_raw, axis=0, keepdims=True)
        ddt_ref[...] = ddt_raw.astype(BF16)

    vec = lambda w: pl.BlockSpec((1, w), lambda i: (0, 0))
    blk = (2 * _nbytes((CHUNK, CONV_DIM), F32) + 4 * _nbytes((CHUNK, D_INNER), F32) + 4 * _nbytes((D_STATE, D_INNER), F32))
    return pl.pallas_call(
        body, name=name, grid=(nc,),
        in_specs=[pl.BlockSpec((CHUNK, D_INNER), lambda i: (rev(i), 0)), pl.BlockSpec((CHUNK, D_INNER), lambda i: (rev(i), 0)),
                  pl.BlockSpec((CHUNK, CONV_DIM), lambda i: (rev(i), 0)), pl.BlockSpec((CHUNK, D_INNER), lambda i: (rev(i), zcb)),
                  pl.BlockSpec((CHUNK, DT_PAD), lambda i: (rev(i), 0)), pl.BlockSpec((1, D_STATE, D_INNER), lambda i: (rev(i), 0, 0)),
                  vec(DT_PAD), vec(DT_PAD), vec(D_INNER), vec(D_INNER), pl.BlockSpec((DT_PAD, D_INNER), lambda i: (0, 0)),
                  pl.BlockSpec(memory_space=pl.ANY)],
        out_specs=[pl.BlockSpec((CHUNK, D_INNER), lambda i: (rev(i), zcb)), pl.BlockSpec((CHUNK, CONV_DIM), lambda i: (rev(i), 0)),
                   pl.BlockSpec((CHUNK, DT_PAD), lambda i: (rev(i), 0)), vec(D_INNER), vec(DT_PAD), vec(DT_PAD), vec(DT_PAD)],
        out_shape=[jax.ShapeDtypeStruct(dproj.shape, BF16), jax.ShapeDtypeStruct((t, CONV_DIM), F32),
                   jax.ShapeDtypeStruct((t, DT_PAD), BF16), jax.ShapeDtypeStruct((1, D_INNER), F32),
                   jax.ShapeDtypeStruct((1, DT_PAD), F32), jax.ShapeDtypeStruct((1, DT_PAD), F32),
                   jax.ShapeDtypeStruct((1, DT_PAD), F32)],
        scratch_shapes=[pltpu.VMEM((D_STATE, D_INNER), F32), pltpu.VMEM((2 * CHUNK, D_INNER), F32)],
        input_output_aliases={11: 0},
        compiler_params=_params(("arbitrary",), blk),
    )(dyb, y, xc, proj, dt_raw, sprev, dtb, alog, dskip_full, ng, e_bf, dproj)


def _mesh_pos():
    return lax.axis_index("x"), lax.axis_index("y"), lax.axis_index("c")


def _other_chips(x, y):
    return [(1 - x, y), (x, 1 - y), (1 - x, 1 - y)]


def _all_peers(x, y, c):
    peers = []
    for k in range(1, N_DEV):
        fx, fy, fc = (k >> 2) & 1, (k >> 1) & 1, k & 1
        px, py, pc = x + fx - 2 * x * fx, y + fy - 2 * y * fy, c + fc - 2 * c * fc
        peers.append(((px, py, pc), 4 * px + 2 * py + pc))
    return peers


def _all_gather(shards, name, own_only=()):
    n, n_own = len(shards), len(own_only)

    def body(*refs):
        ins, own_ins = refs[:n], refs[n:n + n_own]
        outs, own_outs = refs[n + n_own:2 * n + n_own], refs[2 * n + n_own:2 * (n + n_own)]
        send_sems, recv_sems, local_sems = refs[2 * (n + n_own):]
        x, y, c = _mesh_pos()
        me, sibling = (x, y, c), (x, y, 1 - c)
        chips = _other_chips(x, y)

        def slot(p):
            return 4 * p[0] + 2 * p[1] + p[2]

        def copy(a, k, block, to, src=None):
            dst = outs[a].at[slot(block)]
            return pltpu.make_async_remote_copy(
                src_ref=dst if src is None else src, dst_ref=dst, send_sem=send_sems.at[a * 7 + k],
                recv_sem=recv_sems.at[a * 7 + k], device_id=to, device_id_type=MESH)

        started = []
        own = []
        for a in range(n_own):
            mine = pltpu.make_async_copy(own_ins[a], own_outs[a].at[slot(me)], local_sems.at[n + a])
            mine.start()
            own.append(mine)
        for a in range(n):
            mine = pltpu.make_async_copy(ins[a], outs[a].at[slot(me)], local_sems.at[a])
            mine.start()
            own.append(mine)
            first = [copy(a, 0, me, sibling, src=ins[a])]
            first += [copy(a, 1 + j, me, (*chip, c), src=ins[a]) for j, chip in enumerate(chips)]
            for cp in first:
                cp.start()
            started += first
        for a in range(n):
            for j, chip in enumerate(chips):
                copy(a, 1 + j, (*chip, c), me).wait_recv()
                fwd = copy(a, 4 + j, (*chip, c), sibling)
                fwd.start()
                started.append(fwd)
        for a in range(n):
            copy(a, 0, sibling, me).wait_recv()
            for j, chip in enumerate(chips):
                copy(a, 4 + j, (*chip, 1 - c), me).wait_recv()
        for cp in started:
            cp.wait_send()
        for mine in own:
            mine.wait()

    return pl.pallas_call(
        body, name=name,
        in_specs=[_HBM] * (n + n_own), out_specs=[_HBM] * (n + n_own),
        out_shape=[jax.ShapeDtypeStruct((N_DEV,) + s.shape, s.dtype) for s in (*shards, *own_only)],
        scratch_shapes=[pltpu.SemaphoreType.DMA((7 * n,)), pltpu.SemaphoreType.DMA((7 * n,)),
                        pltpu.SemaphoreType.DMA((n + n_own,))],
    )(*shards, *own_only)


_SMALL_ROWS = (("norm_mix_g", 8), ("conv_b", 32), ("dt_bias", 1), ("a_log", 1), ("d_skip", 1), ("ssm_norm_g", 16),
               ("v_norm_g", 8), ("v_norm_b", 8), ("w_spatial", 1024), ("b_spatial", 8), ("b_gates", 16), ("norm_mlp_g", 8),
               ("norm_final_g", 8), ("conv_w", 128), ("loss", 1))
_LAST_SMALL = (("norm_mix_g", 8),)


def _packed_rows(table):
    return -(-sum(r for _, r in table) // SUBLANES) * SUBLANES


def _small_offsets(table=_SMALL_ROWS):
    offs, r = {}, 0
    for name, rows in table:
        offs[name] = r
        r += rows
    return offs


def _rows_from(src_ref, dst_ref, r0):
    k, w = src_ref.shape
    if w <= LANES:
        dst_ref[r0:r0 + k, 0:w] = src_ref[...]
        return
    per = w // LANES
    for i in range(k):
        for j in range(per):
            dst_ref[r0 + i * per + j:r0 + i * per + j + 1, :] = src_ref[i:i + 1, j * LANES:(j + 1) * LANES]


def _rows_to(src_ref, r0, dst_ref):
    k, w = dst_ref.shape
    if w <= LANES:
        dst_ref[...] = src_ref[r0:r0 + k, 0:w]
        return
    per = w // LANES
    for i in range(k):
        for j in range(per):
            dst_ref[i:i + 1, j * LANES:(j + 1) * LANES] = src_ref[r0 + i * per + j:r0 + i * per + j + 1, :]


def _pack_small(grads, slot_idx, name):
    names = [n for n, _ in _SMALL_ROWS if n in grads]
    offs = _small_offsets()
    rows = _packed_rows(_SMALL_ROWS)

    def body(slot_ref, *refs):
        del slot_ref
        ins, (packed_ref, land_ref) = refs[:len(names)], refs[len(names):]
        packed_ref[...] = jnp.zeros_like(packed_ref)
        for n, ref in zip(names, ins):
            _rows_from(ref, packed_ref, offs[n])
        land_ref[0] = packed_ref[...]

    whole = lambda shape: pl.BlockSpec(shape, lambda i, slot_ref: (0,) * len(shape))
    grid_spec = pltpu.PrefetchScalarGridSpec(
        num_scalar_prefetch=1, grid=(1,), in_specs=[whole(grads[n].shape) for n in names],
        out_specs=[whole((rows, LANES)), pl.BlockSpec((1, rows, LANES), lambda i, slot_ref: (slot_ref[0], 0, 0))])
    return pl.pallas_call(
        body, name=name, grid_spec=grid_spec,
        out_shape=[jax.ShapeDtypeStruct((rows, LANES), F32), jax.ShapeDtypeStruct((N_DEV, rows, LANES), F32)],
    )(slot_idx, *[grads[n] for n in names])


def _exchange_small(grads, table, name):
    names = [n for n, _ in table]
    offs = _small_offsets(table)
    n_in = len(names)
    packed_rows = _packed_rows(table)

    def body(*refs):
        ins, out_ref = refs[:n_in], refs[n_in]
        packed, send_sems, recv_sems, local_sem = refs[n_in + 1:]
        packed[...] = jnp.zeros_like(packed)
        for n, ref in zip(names, ins):
            _rows_from(ref, packed, offs[n])
        x, y, c = _mesh_pos()
        my_slot = 4 * x + 2 * y + c
        mine = pltpu.make_async_copy(packed, out_ref.at[my_slot], local_sem)
        mine.start()
        copies = []
        for k, (peer, peer_slot) in enumerate(_all_peers(x, y, c)):
            sems = dict(send_sem=send_sems.at[k], recv_sem=recv_sems.at[k], device_id=peer, device_id_type=MESH)
            send = pltpu.make_async_remote_copy(src_ref=packed, dst_ref=out_ref.at[my_slot], **sems)
            send.start()
            copies.append((send, pltpu.make_async_remote_copy(src_ref=packed, dst_ref=out_ref.at[peer_slot], **sems)))
        for send, recv in copies:
            send.wait_send()
            recv.wait_recv()
        mine.wait()

    return pl.pallas_call(
        body, name=name, in_specs=[pl.BlockSpec(memory_space=pltpu.VMEM)] * n_in, out_specs=_HBM,
        out_shape=jax.ShapeDtypeStruct((N_DEV, packed_rows, LANES), F32),
        scratch_shapes=[pltpu.VMEM((packed_rows, LANES), F32), pltpu.SemaphoreType.DMA((N_DEV - 1,)),
                        pltpu.SemaphoreType.DMA((N_DEV - 1,)), pltpu.SemaphoreType.DMA],
    )(*[grads[n] for n in names])


def _swap_with_sibling(grads, name):
    n = len(grads)

    def body(*refs):
        ins, outs = refs[:n], refs[n:2 * n]
        send_sems, recv_sems = refs[2 * n:]
        x, y, c = _mesh_pos()
        copies = []
        for a in range(n):
            for k in range(N_CHIP):
                cp = pltpu.make_async_remote_copy(
                    src_ref=ins[a].at[(1 - c) + 2 * k], dst_ref=outs[a].at[k], send_sem=send_sems.at[a * N_CHIP + k],
                    recv_sem=recv_sems.at[a * N_CHIP + k], device_id=(x, y, 1 - c), device_id_type=MESH)
                cp.start()
                copies.append(cp)
        for cp in copies:
            cp.wait()

    return pl.pallas_call(
        body, name=name, in_specs=[_HBM] * n, out_specs=[_HBM] * n,
        out_shape=[jax.ShapeDtypeStruct((N_CHIP,) + g.shape[1:], g.dtype) for g in grads],
        scratch_shapes=[pltpu.SemaphoreType.DMA((N_CHIP * n,)), pltpu.SemaphoreType.DMA((N_CHIP * n,))],
    )(*grads)


_SEM = pl.BlockSpec(memory_space=pltpu.SEMAPHORE)
_IN_HBM = pl.BlockSpec(memory_space=pltpu.HBM)
_EFFECT = pltpu.SideEffectType.DATAFLOW_SIDE_EFFECTING


def _in_hbm(a):
    return pltpu.with_memory_space_constraint(a, pltpu.HBM)


def _gather_copies(ins, lands, send_sems, recv_sems):
    x, y, c = _mesh_pos()
    my_slot = 4 * x + 2 * y + c
    pairs = []
    for a in range(len(ins)):
        for k, (peer, peer_slot) in enumerate(_all_peers(x, y, c)):
            sems = dict(send_sem=send_sems.at[a * (N_DEV - 1) + k], recv_sem=recv_sems.at[a * (N_DEV - 1) + k],
                        device_id=peer, device_id_type=MESH)
            pairs.append((pltpu.make_async_remote_copy(src_ref=ins[a], dst_ref=lands[a].at[my_slot], **sems),
                          pltpu.make_async_remote_copy(src_ref=ins[a], dst_ref=lands[a].at[peer_slot], **sems)))
    return pairs


def _scatter_copies(ins, lands, send_sems, recv_sems):
    x, y, c = _mesh_pos()
    my_chip = 2 * x + y
    pairs = []
    for a in range(len(ins)):
        for j, chip in enumerate(_other_chips(x, y)):
            there = 2 * chip[0] + chip[1]
            sems = dict(send_sem=send_sems.at[a * 3 + j], recv_sem=recv_sems.at[a * 3 + j],
                        device_id=(*chip, c), device_id_type=MESH)
            pairs.append((pltpu.make_async_remote_copy(src_ref=ins[a].at[there], dst_ref=lands[a].at[my_chip], **sems),
                          pltpu.make_async_remote_copy(src_ref=ins[a].at[my_chip], dst_ref=lands[a].at[there], **sems)))
    return pairs


def _split_start(srcs, lands, copies, per_array, name):
    n = len(srcs)

    def body(*refs):
        ins, land_refs = refs[:n], refs[n:2 * n]
        send_sems, recv_sems = refs[2 * n], refs[2 * n + 1]
        token = refs[-1]
        for send, _ in copies(ins, land_refs, send_sems, recv_sems):
            send.start()
        token[...] = jnp.zeros_like(token)

    outs = pl.pallas_call(
        body, name=name,
        out_shape=(pltpu.SemaphoreType.DMA((per_array * n,)), pltpu.SemaphoreType.DMA((per_array * n,)),
                   *[pltpu.HBM(s.shape, s.dtype) for s in srcs], *[pltpu.HBM(l.shape, l.dtype) for l in lands],
                   jax.ShapeDtypeStruct((SUBLANES, LANES), F32)),
        in_specs=[_IN_HBM] * (2 * n),
        out_specs=(_SEM, _SEM, *[_IN_HBM] * (2 * n), pl.BlockSpec(memory_space=pltpu.VMEM)),
        input_output_aliases={i: 2 + i for i in range(2 * n)},
        compiler_params=pltpu.CompilerParams(has_side_effects=_EFFECT),
    )(*[_in_hbm(s) for s in srcs], *[_in_hbm(l) for l in lands])
    return outs[0], outs[1], list(outs[2:2 + n]), list(outs[2 + n:2 + 2 * n]), outs[-1]


def _split_wait(started, copies, after, name):
    send_sems, recv_sems, srcs, lands, _ = started
    n = len(srcs)

    def body(*refs):
        ins, land_refs = refs[:n], refs[n:2 * n]
        for send, recv in copies(ins, land_refs, refs[2 * n], refs[2 * n + 1]):
            send.wait_send()
            recv.wait_recv()

    outs = pl.pallas_call(
        body, name=name,
        out_shape=(*[pltpu.HBM(s.shape, s.dtype) for s in srcs], *[pltpu.HBM(l.shape, l.dtype) for l in lands]),
        in_specs=[_IN_HBM] * (2 * n) + [_SEM, _SEM, _HBM],
        out_specs=[_IN_HBM] * (2 * n),
        input_output_aliases={i: i for i in range(2 * n)},
        compiler_params=pltpu.CompilerParams(has_side_effects=_EFFECT),
    )(*srcs, *lands, send_sems, recv_sems, after)
    return list(outs[:n]), list(outs[n:])


def _ew_block(rows, cols, slots):
    budget = 2 * 1024 * 1024
    br, bc = rows, cols
    while slots * br * bc * 4 > budget:
        if br % 2 == 0 and (br // 2) % (2 * SUBLANES) == 0:
            br //= 2
        elif bc % 2 == 0 and (bc // 2) % LANES == 0:
            bc //= 2
        else:
            break
    return br, bc


def _add_sibling(grads, recv, c_idx, name):
    _, rows, cols = grads.shape
    br, bc = _ew_block(rows, cols, 3)

    def body(c_ref, g_ref, r_ref, out_ref):
        del c_ref
        out_ref[...] = (g_ref[...].astype(F32) + r_ref[...].astype(F32)).astype(out_ref.dtype)

    grid_spec = pltpu.PrefetchScalarGridSpec(
        num_scalar_prefetch=1, grid=(N_CHIP, rows // br, cols // bc),
        in_specs=[pl.BlockSpec((1, br, bc), lambda k, i, j, c_ref: (c_ref[0] + 2 * k, i, j)),
                  pl.BlockSpec((1, br, bc), lambda k, i, j, c_ref: (k, i, j))],
        out_specs=pl.BlockSpec((1, br, bc), lambda k, i, j, c_ref: (k, i, j)))
    return pl.pallas_call(
        body, name=name, grid_spec=grid_spec, out_shape=jax.ShapeDtypeStruct((N_CHIP, rows, cols), grads.dtype),
        compiler_params=_params(("parallel", "parallel", "parallel"), 3 * _nbytes((br, bc), F32)),
    )(c_idx, grads, recv)


def _adam_math(g, w, m, v):
    m2 = ADAM_B1 * m + (1.0 - ADAM_B1) * g
    v2 = ADAM_B2 * v + (1.0 - ADAM_B2) * (g * g)
    m_hat = m2 * (1.0 / (1.0 - ADAM_B1 ** ADAM_STEP))
    v_hat = v2 * (1.0 / (1.0 - ADAM_B2 ** ADAM_STEP))
    return -ADAM_LR * (m_hat / (jnp.sqrt(v_hat) + ADAM_EPS) + ADAM_WD * w), m2, v2


def _adamw(slots, w, m, v, name, own=None, own_slot=None):
    ns, rows, cols = slots.shape
    br, bc = _ew_block(rows, cols, 2 * ns + 7)

    def update(g, w_ref, m_ref, v_ref, g_ref, d_ref, m2_ref, v2_ref):
        g_ref[...] = g
        d_ref[...], m2_ref[...], v2_ref[...] = _adam_math(g, w_ref[...], m_ref[...], v_ref[...])

    out_shape = [jax.ShapeDtypeStruct((rows, cols), F32)] * 4
    params = _params(("parallel", "parallel"), (2 * ns + 7) * _nbytes((br, bc), F32))
    grid = (rows // br, cols // bc)
    if own is None:
        def body(s_ref, *rest):
            g = s_ref[0].astype(F32)
            for k in range(1, ns):
                g = g + s_ref[k].astype(F32)
            update(g, *rest)

        blk = pl.BlockSpec((br, bc), lambda i, j: (i, j))
        return pl.pallas_call(
            body, name=name, grid=grid,
            in_specs=[pl.BlockSpec((ns, br, bc), lambda i, j: (0, i, j)), blk, blk, blk], out_specs=[blk] * 4,
            out_shape=out_shape, compiler_params=params,
        )(slots, w, m, v)

    def body_own(slot_ref, s_ref, o_ref, *rest):
        g = None
        for k in range(ns):
            term = jnp.where(slot_ref[0] == k, o_ref[k].astype(F32), s_ref[k].astype(F32))
            g = term if g is None else g + term
        update(g, *rest)

    blk = pl.BlockSpec((br, bc), lambda i, j, slot_ref: (i, j))
    stack = pl.BlockSpec((ns, br, bc), lambda i, j, slot_ref: (0, i, j))
    grid_spec = pltpu.PrefetchScalarGridSpec(num_scalar_prefetch=1, grid=grid, in_specs=[stack, stack, blk, blk, blk],
                                             out_specs=[blk] * 4)
    return pl.pallas_call(body_own, name=name, grid_spec=grid_spec, out_shape=out_shape, compiler_params=params,
                          )(own_slot, slots, own, w, m, v)


def _adamw_small(all_g, last_g, params, extra_shapes, name):
    names = [n for n, _ in _SMALL_ROWS if n in params]
    extras = [n for n, _ in _SMALL_ROWS if n not in params]
    offs = _small_offsets()
    n_p = len(names)

    def body(*refs):
        s_ref, last_ref = refs[0], refs[1]
        wmv = refs[2:2 + 3 * n_p]
        outs = refs[2 + 3 * n_p:2 + 7 * n_p]
        extra_refs = refs[2 + 7 * n_p:2 + 7 * n_p + len(extras)]
        summed = refs[-1]
        g, g_last = s_ref[0], last_ref[0]
        for k in range(1, N_DEV):
            g, g_last = g + s_ref[k], g_last + last_ref[k]
        summed[...] = g
        last_offs = _small_offsets(_LAST_SMALL)
        for n, rows in _LAST_SMALL:
            summed[offs[n]:offs[n] + rows, :] = g_last[last_offs[n]:last_offs[n] + rows, :]
        for i, n in enumerate(names):
            w_ref, m_ref, v_ref = wmv[3 * i:3 * i + 3]
            g_ref, d_ref, m2_ref, v2_ref = outs[4 * i:4 * i + 4]
            _rows_to(summed, offs[n], g_ref)
            d_ref[...], m2_ref[...], v2_ref[...] = _adam_math(g_ref[...], w_ref[...], m_ref[...], v_ref[...])
        for n, ref in zip(extras, extra_refs):
            _rows_to(summed, offs[n], ref)

    flat = [a for n in names for a in params[n]]
    out_shape = [jax.ShapeDtypeStruct(params[n][0].shape, F32) for n in names for _ in range(4)]
    out_shape += [jax.ShapeDtypeStruct(s, F32) for s in extra_shapes]
    vmem = pl.BlockSpec(memory_space=pltpu.VMEM)
    res = pl.pallas_call(
        body, name=name, in_specs=[vmem] * (2 + len(flat)), out_specs=[vmem] * len(out_shape), out_shape=out_shape,
        scratch_shapes=[pltpu.VMEM(all_g.shape[1:], F32)],
        compiler_params=pltpu.CompilerParams(vmem_limit_bytes=_vmem_limit(_nbytes(all_g.shape, F32))),
    )(all_g, last_g, *flat)
    return {n: res[4 * i:4 * i + 4] for i, n in enumerate(names)}, res[4 * n_p:]


def _mm_tiles(mode, m, n, k):
    tn = min(n, 1024)
    if mode == "tn":
        return min(m, 1024), tn, min(k, 4096)
    if k <= 1024:
        return min(m, 2048), tn, k
    if k <= 2048:
        return min(m, 1024), tn, k
    if k <= 4096:
        return min(m, 512), tn, k
    return min(m, 1024), tn, 2048


def _local_step(x, target, wts, small, exchange):
    t = x.shape[0]
    w_main_t, w_dt_t = wts["w_main_t"], wts["w_dt_t"]
    bsp_t = small["b_spatial"].T
    pad32 = lambda a: jnp.pad(a, ((0, 0), (0, DT_PAD - N_HEADS)))
    dtb, alog = pad32(small["dt_bias"]), pad32(small["a_log"])
    dskip_full = jnp.repeat(small["d_skip"], HEAD_DIM, axis=1)
    head_of_col = lax.broadcasted_iota(jnp.int32, (DT_PAD, D_INNER), 1) // HEAD_DIM
    e_bf = (head_of_col == lax.broadcasted_iota(jnp.int32, (DT_PAD, D_INNER), 0)).astype(BF16)

    def mm(a, b, mode, name, **kw):
        if mode == "nn":
            m, k, n = a.shape[0], a.shape[1], b.shape[1]
        elif mode == "nt":
            m, k, n = a.shape[0], a.shape[1], b.shape[0]
        else:
            m, k, n = a.shape[1], a.shape[0], b.shape[1]
        tm, tn, tk = _mm_tiles(mode, m, n, k)
        tm = min(tm, kw.pop("max_tm", tm))
        kw.setdefault("out_dtypes", (BF16,) if mode == "tn" else (F32,))
        if "extra_specs" in kw:
            kw["extra_specs"] = kw["extra_specs"](tm, tn)
        return _matmul(a, b, mode=mode, tm=tm, tn=tn, tk=tk, name=name, **kw)

    def out_tile(tm, tn):
        return (((tm, tn), lambda i, j: (i, j)),)

    def row_tiles(n_tiles, *vectors, gate_logits=False):
        def specs(tm, tn):
            out = [((tm, tn), lambda i, j: (i, j))] * n_tiles
            if gate_logits:
                out += [((tm, D_MODEL), lambda i, j, cb=COL_GATE // D_MODEL + half: (i, cb)) for half in range(2)]
            return tuple(out) + tuple(((1, w), lambda i, j, cb=cb: (0, cb)) for w, cb in vectors)
        return specs

    vec = lambda w: ((1, w), F32, (1, w), lambda i, j: (0, 0))
    fused_tm = 512

    h, dt_raw = _rms_fwd(x, small["norm_mix_g"], w_dt_t, "rms_mix", deps=exchange.begin())
    proj = mm(h, w_main_t, "nt", "proj_main", j_outer=True, out_dtypes=(BF16,))
    y_a = _gmlp_fwd(proj, small["v_norm_g"], small["v_norm_b"], small["w_spatial"], bsp_t, "gmlp_fwd")
    pre_conv, xc = _conv_fwd(proj, wts["conv_w"], small["conv_b"], "conv_fwd")
    y_ssd, y_b, sprev = _ssd_fwd(xc, proj, dt_raw, dtb, alog, dskip_full, small["ssm_norm_g"], e_bf, "ssd_fwd")
    wts = {**wts, **exchange.late_weights(y_b)}
    pa = mm(y_a, wts["w_proj_a"], "nn", "proj_a")
    pb, merged = mm(y_b, wts["w_proj_b"], "nn", "proj_b", epilogue=_merge_epilogue, out_dtypes=(F32, BF16), max_tm=fused_tm,
                    extras=(pa, proj, proj, small["b_gates"], small["b_gates"]),
                    extra_specs=row_tiles(1, (D_MODEL, 0), (D_MODEL, 1), gate_logits=True))
    x1, h2 = mm(merged, wts["w_out"], "nn", "out_proj", epilogue=_residual_rms_epilogue, out_dtypes=(F32, BF16),
                max_tm=2 * fused_tm, extras=(x, small["norm_mlp_g"]), extra_specs=row_tiles(1, (D_MODEL, 0)))

    def relu_sq(acc, ex, outs, first):
        r = jnp.maximum(acc, 0.0)
        outs[0][...] = (r * r).astype(BF16)

    act = mm(h2, wts["w_mlp_up"], "nn", "mlp_up", epilogue=relu_sq, out_dtypes=(BF16,), j_outer=True)
    dx2, dx2_b, g_final, _, loss = mm(
        act, wts["w_mlp_down"], "nn", "mlp_down", epilogue=_loss_epilogue, carry=True,
        out_dtypes=(F32, BF16, vec(D_MODEL), vec(D_MODEL), vec(LANES)),
        extras=(x1, small["norm_final_g"], target), extra_specs=lambda tm, tn: (
            ((tm, tn), lambda i, j: (i, j)), ((1, tn), lambda i, j: (0, 0)), ((tm, tn), lambda i, j: (i, j))))

    def relu_sq_bwd(acc, ex, outs, first):
        outs[0][...] = (acc * 2.0 * jnp.sqrt(ex[0][...].astype(F32))).astype(BF16)

    dup = mm(dx2_b, wts["w_mlp_down"], "nt", "d_act", epilogue=relu_sq_bwd, extras=(act,), extra_specs=out_tile,
             out_dtypes=(BF16,), j_outer=True)
    g_down = mm(act, dx2_b, "tn", "g_mlp_down")
    g_up = mm(h2, dup, "tn", "g_mlp_up")
    dx1, dx1_b, g_mlp = mm(
        dup, wts["w_mlp_up"], "nt", "d_h2", epilogue=_rms_bwd_epilogue, carry=True,
        out_dtypes=(F32, BF16, vec(D_MODEL)), extras=(x1, small["norm_mlp_g"], dx2), extra_specs=lambda tm, tn: (
            ((tm, tn), lambda i, j: (i, j)), ((1, tn), lambda i, j: (0, 0)), ((tm, tn), lambda i, j: (i, j))))

    g_out = mm(merged, dx1_b, "tn", "g_out")
    dpa, dpb, dproj, g_bgates = mm(
        dx1_b, wts["w_out"], "nt", "d_merged", epilogue=_merge_bwd_epilogue, carry=True, max_tm=fused_tm,
        out_dtypes=(BF16, BF16, ((t, MAIN_W), BF16, (fused_tm, 2 * D_MODEL), lambda i, j: (i, COL_GATE // (2 * D_MODEL))),
                    vec(2 * D_MODEL)),
        extras=(pa, pb, proj, proj, small["b_gates"], small["b_gates"]),
        extra_specs=row_tiles(2, (D_MODEL, 0), (D_MODEL, 1), gate_logits=True))
    g_pa = mm(y_a, dpa, "tn", "g_proj_a")
    g_pb = mm(y_b, dpb, "tn", "g_proj_b")
    started = exchange.reduce("late", {"w_mlp_down": g_down, "w_mlp_up": g_up, "w_out": g_out, "w_proj_a": g_pa,
                                       "w_proj_b": g_pb})
    dya = mm(dpa, wts["w_proj_a"], "nt", "d_ya", deps=started)
    dyb = mm(dpb, wts["w_proj_b"], "nt", "d_yb")

    dproj, g_wsp, g_bsp_t, g_vg, g_vb = _gmlp_bwd(proj, dya, small["v_norm_g"], small["v_norm_b"], small["w_spatial"],
                                                   bsp_t, dproj, "gmlp_bwd")
    dproj, dxc, ddt, g_ng, g_dskip, g_alog, g_dtb = _ssd_bwd(dyb, y_ssd, xc, proj, dt_raw, sprev, dtb, alog, dskip_full,
                                                             small["ssm_norm_g"], e_bf, dproj, "ssd_bwd")
    dproj, g_convw, g_convb = _conv_bwd(proj, pre_conv, dxc, wts["conv_w"], dproj, "conv_bwd")

    small_grads = {
        "conv_w": g_convw, "loss": loss,
        "conv_b": g_convb, "dt_bias": g_dtb, "a_log": g_alog, "d_skip": g_dskip, "ssm_norm_g": g_ng,
        "v_norm_g": g_vg, "v_norm_b": g_vb, "w_spatial": g_wsp.reshape(GROUPS * CHUNK, CHUNK), "b_spatial": g_bsp_t.T,
        "b_gates": g_bgates, "norm_mlp_g": g_mlp, "norm_final_g": g_final,
    }
    g_main_t = mm(dproj, h, "tn", "g_in_main", deps=exchange.small(small_grads))
    g_dt_t = mm(ddt, h, "tn", "g_in_dt")
    started = exchange.reduce("in", {"w_in": (g_main_t, g_dt_t)})

    def input_grad(acc, ex, outs, first):
        x_ref, g_ref, res_ref, ddt_ref, wdt_ref = ex
        gg = jnp.zeros((1, D_MODEL), F32)
        for r in range(acc.shape[0] // ROW_TILE):
            rows = slice(r * ROW_TILE, (r + 1) * ROW_TILE)
            dh = acc[rows] + _dot(ddt_ref[rows, :], wdt_ref[...], _NN)
            dx, gg_r = _rms_pullback(x_ref[rows, :], g_ref[...], dh)
            outs[0][rows, :] = dx + res_ref[rows, :]
            gg = gg + gg_r

        _zero_when(first, outs[1])
        outs[1][...] += gg

    grad_x, g_mix = mm(
        dproj, w_main_t, "nn", "d_h", epilogue=input_grad, deps=started, carry=True,
        out_dtypes=(F32, vec(D_MODEL)), extras=(x, small["norm_mix_g"], dx1, ddt, w_dt_t), extra_specs=lambda tm, tn: (
            ((tm, tn), lambda i, j: (i, j)), ((1, tn), lambda i, j: (0, 0)), ((tm, tn), lambda i, j: (i, j)),
            ((tm, DT_PAD), lambda i, j: (i, 0)), ((DT_PAD, D_MODEL), lambda i, j: (0, 0))))

    return grad_x, g_mix


SHARD_ROWS = (MAIN_W + N_HEADS) // N_DEV
REGROUP_IN = 2048


def _main_rows_of(gathered, name):
    n_dev, shard, d = gathered.shape
    blk = 1024
    nb = MAIN_W // blk

    def first_feature(b):
        return b * blk + (N_HEADS if b * blk >= COL_GATE else 0)

    def body(a_ref, b_ref, out_ref):
        for b in range(nb):
            s0, r0 = divmod(first_feature(b), shard)
            n1 = min(shard - r0, blk)

            @pl.when(pl.program_id(0) == b)
            def _(r0=r0, n1=n1):
                out_ref[0:n1, :] = a_ref[0, r0:r0 + n1, :]
                if n1 < blk:
                    out_ref[n1:blk, :] = b_ref[0, 0:blk - n1, :]

    def slot(b):
        return (b * blk + jnp.where(b * blk >= COL_GATE, N_HEADS, 0)) // shard

    return pl.pallas_call(
        body, name=name, grid=(nb,),
        in_specs=[pl.BlockSpec((1, shard, d), lambda b: (slot(b), 0, 0)),
                  pl.BlockSpec((1, shard, d), lambda b: (jnp.minimum(slot(b) + 1, n_dev - 1), 0, 0))],
        out_specs=pl.BlockSpec((blk, d), lambda b: (b, 0)),
        out_shape=jax.ShapeDtypeStruct((MAIN_W, d), gathered.dtype),
        compiler_params=_params(("parallel",), 3 * _nbytes((shard, d), gathered.dtype)),
    )(gathered, gathered)


def _by_device_rows(g_main_t, g_dt_t, name):
    d = g_main_t.shape[1]
    n_blocks = MAIN_W // REGROUP_IN
    dt_dev, dt_row = divmod(COL_GATE, SHARD_ROWS)

    def main_start(s):
        return s * SHARD_ROWS - (N_HEADS if s > dt_dev else 0)

    def body(a_ref, b_ref, dt_ref, out_ref):
        for s in range(N_DEV):
            m0 = main_start(s)
            k0, off = divmod(m0, REGROUP_IN)
            pieces = []
            if s == dt_dev:
                pieces = [(0, dt_row, m0), (dt_row, N_HEADS, None), (dt_row + N_HEADS, SHARD_ROWS - dt_row - N_HEADS, m0 + dt_row)]
            else:
                pieces = [(0, SHARD_ROWS, m0)]

            @pl.when(pl.program_id(0) == s)
            def _(pieces=pieces, k0=k0):
                for dst, n, src in pieces:
                    if src is None:
                        out_ref[0, dst:dst + n, :] = dt_ref[0:n, :]
                        continue
                    lo = src - k0 * REGROUP_IN
                    n_a = max(0, min(n, REGROUP_IN - lo))
                    if n_a:
                        out_ref[0, dst:dst + n_a, :] = a_ref[lo:lo + n_a, :]
                    if n_a < n:
                        lo_b = max(lo - REGROUP_IN, 0)
                        out_ref[0, dst + n_a:dst + n, :] = b_ref[lo_b:lo_b + n - n_a, :]

    def first_block(s):
        return (s * SHARD_ROWS - jnp.where(s > dt_dev, N_HEADS, 0)) // REGROUP_IN

    return pl.pallas_call(
        body, name=name, grid=(N_DEV,),
        in_specs=[pl.BlockSpec((REGROUP_IN, d), lambda s: (first_block(s), 0)),
                  pl.BlockSpec((REGROUP_IN, d), lambda s: (jnp.minimum(first_block(s) + 1, n_blocks - 1), 0)),
                  pl.BlockSpec((DT_PAD, d), lambda s: (0, 0))],
        out_specs=pl.BlockSpec((1, SHARD_ROWS, d), lambda s: (s, 0, 0)),
        out_shape=jax.ShapeDtypeStruct((N_DEV, SHARD_ROWS, d), g_main_t.dtype),
        compiler_params=_params(("parallel",), 3 * _nbytes((REGROUP_IN, d), g_main_t.dtype)),
    )(g_main_t, g_main_t, g_dt_t)


_LATE = ["w_proj_a", "w_proj_b", "w_out", "w_mlp_up", "w_mlp_down"]
_BY_COLS = ("w_mlp_up",)


class _Exchange:
    def __init__(self, late_shards, late_lands):
        self.late_shards, self.late_lands = late_shards, late_lands
        self.c_idx = lax.axis_index("c").astype(jnp.int32).reshape(1)
        self.chip_idx = (2 * lax.axis_index("x") + lax.axis_index("y")).astype(jnp.int32).reshape(1)
        self.pending = []

    def begin(self):
        self.late = _split_start(self.late_shards, self.late_lands, _gather_copies, N_DEV - 1, "gather_late_start")
        return [self.late[-1]]

    def late_weights(self, after):
        _, lands = _split_wait(self.late, _gather_copies, after, "gather_late_wait")
        whole = {}
        for n, g in zip(_LATE, lands):
            whole[n] = jnp.transpose(g, (1, 0, 2)).reshape(g.shape[1], -1) if n in _BY_COLS else g.reshape(-1, g.shape[2])
        return whole

    def reduce(self, tag, grads):
        names = list(grads)
        by_dev = []
        for n in names:
            g = grads[n]
            if n == "w_in":
                by_dev.append(_by_device_rows(*g, "regroup_g_in"))
            elif n in _BY_COLS:
                by_dev.append(jnp.transpose(g.reshape(g.shape[0], N_DEV, -1), (1, 0, 2)))
            else:
                by_dev.append(g.reshape(N_DEV, -1, g.shape[1]))
        from_sibling = _swap_with_sibling(by_dev, "reduce_cores_" + tag)
        parts = [_add_sibling(g, r, self.c_idx, "add_cores_" + n) for n, g, r in zip(names, by_dev, from_sibling)]
        lands = [lax.empty(p.shape, p.dtype) for p in parts]
        started = _split_start(parts, lands, _scatter_copies, 3, "reduce_chips_start_" + tag)
        self.pending.append((tag, names, started))
        return [started[-1]]

    def small(self, grads):
        dev = 2 * self.chip_idx + self.c_idx
        packed, land = _pack_small(grads, dev, "pack_small")
        self.small_started = _split_start([packed], [land], _gather_copies, N_DEV - 1, "exchange_small_start")
        return [self.small_started[-1]]

    def finish(self, after):
        _, (all_small,) = _split_wait(self.small_started, _gather_copies, after, "exchange_small_wait")
        done = {}
        for tag, names, started in self.pending:
            parts, lands = _split_wait(started, _scatter_copies, after, "reduce_chips_wait_" + tag)
            for n, land, part in zip(names, lands, parts):
                done[n] = (land, part, self.chip_idx)
        return all_small, done


def kernel(x, norm_mix_g, w_in, conv_w, conv_b, dt_bias, a_log, d_skip, ssm_norm_g, v_norm_g, v_norm_b, w_spatial, b_spatial, b_gates, w_proj_a, w_proj_b, w_out, norm_mlp_g, w_mlp_up, w_mlp_down, norm_final_g, loss_target, m_norm_mix_g, m_w_in, m_conv_w, m_conv_b, m_dt_bias, m_a_log, m_d_skip, m_ssm_norm_g, m_v_norm_g, m_v_norm_b, m_w_spatial, m_b_spatial, m_b_gates, m_w_proj_a, m_w_proj_b, m_w_out, m_norm_mlp_g, m_w_mlp_up, m_w_mlp_down, m_norm_final_g, v_norm_mix_g, v_w_in, v_conv_w, v_conv_b, v_dt_bias, v_a_log, v_d_skip, v_ssm_norm_g, v_v_norm_g, v_v_norm_b, v_w_spatial, v_b_spatial, v_b_gates, v_w_proj_a, v_w_proj_b, v_w_out, v_norm_mlp_g, v_w_mlp_up, v_w_mlp_down, v_norm_final_g):
    given = dict(locals())
    names = ["norm_mix_g", "w_in", "conv_w", "conv_b", "dt_bias", "a_log", "d_skip", "ssm_norm_g", "v_norm_g", "v_norm_b",
             "w_spatial", "b_spatial", "b_gates", "w_proj_a", "w_proj_b", "w_out", "norm_mlp_g", "w_mlp_up", "w_mlp_down",
             "norm_final_g"]
    shapes = {n: given[n].shape for n in names}
    dev = 4 * lax.axis_index("x") + 2 * lax.axis_index("y") + lax.axis_index("c")

    shard2d = {"w_in": w_in[0].T, "w_proj_a": w_proj_a[0], "w_proj_b": w_proj_b[0], "w_out": w_out[0],
               "w_mlp_up": w_mlp_up[0], "w_mlp_down": w_mlp_down[0]}
    conv_shard = conv_w.reshape(CONV_WIDTH, -1)
    late_shards = [shard2d[n].astype(BF16) for n in _LATE]
    w_in_all, conv_all, *late_lands = _all_gather([shard2d["w_in"].astype(BF16), conv_shard], "gather_first",
                                                  own_only=late_shards)
    dt_dev, dt_row = divmod(COL_GATE, SHARD_ROWS)
    w_dt_t = jnp.pad(w_in_all[dt_dev, dt_row:dt_row + N_HEADS], ((0, DT_PAD - N_HEADS), (0, 0)))
    wts = {"w_main_t": _main_rows_of(w_in_all, "regroup_w_in"), "w_dt_t": w_dt_t, "conv_w": jnp.transpose(conv_all, (1, 0, 2)).reshape(CONV_WIDTH, -1)}
    small = {"norm_mix_g": norm_mix_g, "conv_b": conv_b, "dt_bias": dt_bias, "a_log": a_log, "d_skip": d_skip,
             "ssm_norm_g": ssm_norm_g, "v_norm_g": v_norm_g, "v_norm_b": v_norm_b, "w_spatial": w_spatial[0],
             "b_spatial": b_spatial[0], "b_gates": b_gates, "norm_mlp_g": norm_mlp_g,
             "norm_final_g": norm_final_g.reshape(1, -1)}

    exchange = _Exchange(late_shards, late_lands)
    grad_x, g_mix = _local_step(x[0], loss_target[0], wts, small, exchange)

    out = {}
    all_small, large = exchange.finish(grad_x)
    for n, (slots, own, own_slot) in large.items():
        moments = [given["m_" + n][0], given["v_" + n][0]]
        if n == "w_in":
            moments = [mom.T for mom in moments]
        res = _adamw(slots, shard2d[n], *moments, "adamw_" + n, own=own, own_slot=own_slot)
        out[n] = [(r.T if n == "w_in" else r).reshape(shapes[n]) for r in res]

    last_small = _exchange_small({"norm_mix_g": g_mix}, _LAST_SMALL, "exchange_last")
    small["w_spatial"] = small["w_spatial"].reshape(GROUPS * CHUNK, CHUNK)
    params = {n: (w2d, given["m_" + n].reshape(w2d.shape), given["v_" + n].reshape(w2d.shape)) for n, w2d in small.items()}
    updated, (g_conv_full, loss_all) = _adamw_small(all_small, last_small, params, [(CONV_WIDTH, CONV_DIM), (1, LANES)],
                                                    "adamw_small")
    for n, res in updated.items():
        out[n] = [r.reshape(shapes[n]) for r in res]
    width = shapes["conv_w"][-1]
    g_conv = lax.dynamic_slice(g_conv_full, (0, dev * width), (CONV_WIDTH, width))
    res = _adamw(g_conv[None], conv_shard, m_conv_w.reshape(CONV_WIDTH, -1), v_conv_w.reshape(CONV_WIDTH, -1), "adamw_conv_w")
    out["conv_w"] = [r.reshape(shapes["conv_w"]) for r in res]

    loss = loss_all[0, 0]
    return (loss, grad_x[None], *[out[n][0] for n in names], *[out[n][1] for n in names],
            *[out[n][2] for n in names], *[out[n][3] for n in names])
```

```python
import functools
import math

import jax
import jax.numpy as jnp
from jax import lax
from jax.experimental import pallas as pl
from jax.experimental.pallas import tpu as pltpu

F32 = jnp.float32
BF16 = jnp.bfloat16
MESH = pl.DeviceIdType.MESH

D_MODEL = 1024
NORM_EPS = 1e-6
CHUNK = 128
GROUPS = 8
D_INNER = 2048
HEAD_DIM = 64
N_HEADS = 32
D_STATE = 128
CONV_WIDTH = 4
CONV_DIM = 4096
D_FF = 4096
GROUP_W = D_INNER // GROUPS
N_DEV = 8
N_CHIP = 4

ADAM_LR = 0.001
ADAM_B1 = 0.9
ADAM_B2 = 0.999
ADAM_EPS = 1e-08
ADAM_WD = 0.01
ADAM_STEP = 10

MAIN_W = 2 * D_MODEL + D_INNER + CONV_DIM + 2 * D_MODEL
COL_Z = 2048
COL_XBC = 4096
COL_GATE = 8192
DT_PAD = 128

LANES = 128
SUBLANES = 8
VMEM_BYTES_V7X = 64 * 1024 * 1024
VMEM_BODY_TEMP = 24 * 1024 * 1024


def _vmem_limit(block_bytes):
    return int(min(2 * block_bytes + VMEM_BODY_TEMP, VMEM_BYTES_V7X - 8 * 1024 * 1024))


def _nbytes(shape, dtype):
    return math.prod(shape) * jnp.dtype(dtype).itemsize


_HBM = pl.BlockSpec(memory_space=pl.ANY)


def _params(sem, block_bytes):
    return pltpu.CompilerParams(dimension_semantics=sem, vmem_limit_bytes=_vmem_limit(block_bytes))


def _sigmoid(x):
    return 1.0 / (1.0 + jnp.exp(-x))


def _softplus(x):
    e = jnp.exp(-jnp.abs(x))
    u = 1.0 + e
    log1p_e = jnp.where(u == 1.0, e, jnp.log(u) * (e / jnp.where(u == 1.0, 1.0, u - 1.0)))
    return jnp.maximum(x, 0.0) + log1p_e


_SQRT_HALF = 0.7071067811865476
_INV_SQRT_2PI = 0.3989422804014327


def _normal_cdf(x):
    return 0.5 * (1.0 + lax.erf(x * _SQRT_HALF))


def _gelu_grad(x, cdf):
    return cdf + x * jnp.exp(-0.5 * x * x) * _INV_SQRT_2PI


def _dot(a, b, dims):
    return lax.dot_general(a, b, (dims, ((), ())), preferred_element_type=F32)


_NN = ((1,), (0,))
_NT = ((1,), (1,))
_TN = ((0,), (0,))


def _split3(x):
    hi = x.astype(BF16)
    r1 = x - hi.astype(F32)
    mid = r1.astype(BF16)
    lo = (r1 - mid.astype(F32)).astype(BF16)
    return hi, mid, lo


def _dot_exact_rhs(x, e, dims):
    hi, mid, lo = _split3(x)
    return _dot(hi, e, dims) + _dot(mid, e, dims) + _dot(lo, e, dims)


def _dot_exact_lhs(e, x, dims):
    hi, mid, lo = _split3(x)
    return _dot(e, hi, dims) + _dot(e, mid, dims) + _dot(e, lo, dims)


def _tri(lower):
    r = lax.broadcasted_iota(jnp.int32, (CHUNK, CHUNK), 0)
    c = lax.broadcasted_iota(jnp.int32, (CHUNK, CHUNK), 1)
    return (r >= c) if lower else (r <= c)


def _matmul(a, b, *, mode, tm, tn, tk, out_dtypes, name, epilogue=None, extras=(), extra_specs=(), j_outer=False, deps=(),
            carry=False):
    if mode == "nn":
        (m, k), (_, n) = a.shape, b.shape
    elif mode == "nt":
        (m, k), (n, _) = a.shape, b.shape
    else:
        (k, m), (_, n) = a.shape, b.shape
    assert m % tm == 0 and n % tn == 0 and k % tk == 0, (name, m, n, k, tm, tn, tk)
    nk = k // tk
    n_extra, n_out = len(extras), len(out_dtypes)
    first_out = 2 + n_extra + len(deps)
    dims = {"nn": _NN, "nt": _NT, "tn": _TN}[mode]
    if epilogue is None:
        def epilogue(acc, ex, outs, first):
            outs[0][...] = acc.astype(outs[0].dtype)

    def body(*refs):
        a_ref, b_ref = refs[0], refs[1]
        ex_refs = refs[2:2 + n_extra]
        outs = refs[first_out:first_out + n_out]
        first_tile = pl.program_id(0) == 0
        p = _dot(a_ref[...], b_ref[...], dims)
        if nk == 1:
            epilogue(p, ex_refs, outs, first_tile)
            return
        acc_ref = refs[first_out + n_out]
        kk = pl.program_id(2)

        @pl.when(kk == 0)
        def _():
            acc_ref[...] = p

        @pl.when(kk > 0)
        def _():
            acc_ref[...] += p

        @pl.when(kk == nk - 1)
        def _():
            epilogue(acc_ref[...], ex_refs, outs, first_tile)

    if j_outer:
        grid = (n // tn, m // tm, nk)
        ij = lambda g0, g1: (g1, g0)
    else:
        grid = (m // tm, n // tn, nk)
        ij = lambda g0, g1: (g0, g1)

    def wrap(fn):
        return lambda g0, g1, kk: fn(*ij(g0, g1), kk)

    if mode == "nn":
        a_spec = pl.BlockSpec((tm, tk), wrap(lambda i, j, kk: (i, kk)))
        b_spec = pl.BlockSpec((tk, tn), wrap(lambda i, j, kk: (kk, j)))
        a_blk, b_blk = (tm, tk), (tk, tn)
    elif mode == "nt":
        a_spec = pl.BlockSpec((tm, tk), wrap(lambda i, j, kk: (i, kk)))
        b_spec = pl.BlockSpec((tn, tk), wrap(lambda i, j, kk: (j, kk)))
        a_blk, b_blk = (tm, tk), (tn, tk)
    else:
        a_spec = pl.BlockSpec((tk, tm), wrap(lambda i, j, kk: (kk, i)))
        b_spec = pl.BlockSpec((tk, tn), wrap(lambda i, j, kk: (kk, j)))
        a_blk, b_blk = (tk, tm), (tk, tn)
    ex_specs = [pl.BlockSpec(shape, wrap(lambda i, j, kk, f=f: f(i, j))) for shape, f in extra_specs]
    outs = [o if isinstance(o, tuple) else ((m, n), o, (tm, tn), lambda i, j: (i, j)) for o in out_dtypes]
    out_spec = [pl.BlockSpec(blk_shape, wrap(lambda i, j, kk, f=f: f(i, j))) for _, _, blk_shape, f in outs]
    out_shape = [jax.ShapeDtypeStruct(shape, dt) for shape, dt, _, _ in outs]
    blk = (_nbytes(a_blk, a.dtype) + _nbytes(b_blk, b.dtype) + sum(_nbytes(s, F32) for s, _ in extra_specs)
           + sum(_nbytes(blk_shape, dt) for _, dt, blk_shape, _ in outs) + _nbytes((tm, tn), F32))
    order = ("arbitrary",) * 3 if carry else ("parallel", "parallel", "arbitrary")
    res = pl.pallas_call(
        body, name=name, grid=grid,
        in_specs=[a_spec, b_spec] + ex_specs + [_HBM] * len(deps), out_specs=out_spec, out_shape=out_shape,
        scratch_shapes=[pltpu.VMEM((tm, tn), F32)] if nk > 1 else [],
        compiler_params=_params(order, blk),
    )(a, b, *extras, *deps)
    return res[0] if n_out == 1 else res


ROW_TILE = 256


def _row_spec(width, col_block=0, tile=ROW_TILE):
    return pl.BlockSpec((tile, width), lambda i, cb=col_block: (i, cb))


def _vec_spec(width, col_block=0):
    return pl.BlockSpec((1, width), lambda i, cb=col_block: (0, cb))


def _rms_fwd(x, g, w_t, name, deps=()):
    t = x.shape[0]
    n_small = w_t.shape[0]
    tile = 2 * ROW_TILE

    def body(x_ref, g_ref, w_ref, *rest):
        h_ref, small_ref = rest[-2:]
        xv = x_ref[...]
        r = lax.rsqrt(jnp.mean(xv * xv, axis=-1, keepdims=True) + NORM_EPS)
        h = (xv * r * g_ref[...]).astype(BF16)
        h_ref[...] = h
        small_ref[...] = _dot(h, w_ref[...], _NT)

    return pl.pallas_call(
        body, name=name, grid=(t // tile,),
        in_specs=[_row_spec(D_MODEL, 0, tile), _vec_spec(D_MODEL), pl.BlockSpec((n_small, D_MODEL), lambda i: (0, 0))]
        + [_HBM] * len(deps),
        out_specs=[_row_spec(D_MODEL, 0, tile), _row_spec(n_small, 0, tile)],
        out_shape=[jax.ShapeDtypeStruct((t, D_MODEL), BF16), jax.ShapeDtypeStruct((t, n_small), F32)],
        compiler_params=_params(("parallel",), 3 * _nbytes((tile, D_MODEL), F32)),
    )(x, g, w_t, *deps)


def _rms_scale(xv):
    r = lax.rsqrt(jnp.mean(xv * xv, axis=-1, keepdims=True) + NORM_EPS)
    return r, xv * r


def _rms_pullback(xv, g, dh):
    r, xh = _rms_scale(xv)
    dyg = dh * g
    return r * (dyg - xh * jnp.mean(dyg * xh, axis=-1, keepdims=True)), jnp.sum(dh * xh, axis=0, keepdims=True)


def _zero_when(first, *refs):
    @pl.when(first)
    def _():
        for ref in refs:
            ref[...] = jnp.zeros_like(ref)


def _residual_rms_epilogue(acc, ex, outs, first):
    x1 = acc + ex[0][...]
    outs[0][...] = x1
    _, xh = _rms_scale(x1)
    outs[1][...] = (xh * ex[1][...]).astype(BF16)


def _loss_epilogue(acc, ex, outs, first):
    dx_ref, dxb_ref, gg_ref, sq_ref, tot_ref = outs
    gv = ex[1][...]
    r, xh = _rms_scale(acc + ex[0][...])
    err = xh * gv - ex[2][...]
    dy = err * (1.0 / D_MODEL)
    dyg = dy * gv
    dx = r * (dyg - xh * jnp.mean(dyg * xh, axis=-1, keepdims=True))
    dx_ref[...] = dx
    dxb_ref[...] = dx.astype(BF16)

    _zero_when(first, gg_ref, sq_ref)
    gg_ref[...] += jnp.sum(dy * xh, axis=0, keepdims=True)
    sq_ref[...] += jnp.sum(err * err, axis=0, keepdims=True)
    tot_ref[...] = jnp.broadcast_to(jnp.sum(sq_ref[...], axis=1, keepdims=True) * (0.5 / D_MODEL), tot_ref.shape)


def _rms_bwd_epilogue(dh, ex, outs, first):
    dx, gg = _rms_pullback(ex[0][...], ex[1][...], dh)
    dx = dx + ex[2][...]
    outs[0][...] = dx
    if len(outs) == 3:
        outs[1][...] = dx.astype(BF16)

    _zero_when(first, outs[-1])
    outs[-1][...] += gg


def _merge_epilogue(acc, ex, outs, first):
    outs[0][...] = acc
    ga = _sigmoid(ex[1][...].astype(F32) + ex[3][...])
    gb = _sigmoid(ex[2][...].astype(F32) + ex[4][...])
    outs[1][...] = (ga * ex[0][...] + gb * acc).astype(BF16)


def _merge_bwd_epilogue(dm, ex, outs, first):
    dpa_ref, dpb_ref, dgl_ref, gb_ref = outs
    ga = _sigmoid(ex[2][...].astype(F32) + ex[4][...])
    gb = _sigmoid(ex[3][...].astype(F32) + ex[5][...])
    dpa_ref[...] = (dm * ga).astype(BF16)
    dpb_ref[...] = (dm * gb).astype(BF16)
    dla = dm * ex[0][...] * ga * (1.0 - ga)
    dlb = dm * ex[1][...] * gb * (1.0 - gb)
    dgl_ref[:, :D_MODEL] = dla.astype(BF16)
    dgl_ref[:, D_MODEL:] = dlb.astype(BF16)

    _zero_when(first, gb_ref)
    gb_ref[:, :D_MODEL] += jnp.sum(dla, axis=0, keepdims=True)
    gb_ref[:, D_MODEL:] += jnp.sum(dlb, axis=0, keepdims=True)


GMLP_TILE = 512
GMLP_NC = GMLP_TILE // CHUNK


def _gmlp_common(u_pre, v_pre, vg, vb):
    cdf_u, cdf_v = _normal_cdf(u_pre), _normal_cdf(v_pre)
    u = u_pre * cdf_u
    v = v_pre * cdf_v
    mu = jnp.mean(v, axis=-1, keepdims=True)
    vc = v - mu
    rstd = lax.rsqrt(jnp.mean(vc * vc, axis=-1, keepdims=True) + NORM_EPS)
    vh = vc * rstd
    vn = vh * vg + vb
    return u, vh, vn, rstd, cdf_u, cdf_v


def _chunks_to_lanes(x, g):
    return jnp.concatenate([x[c * CHUNK:(c + 1) * CHUNK, g * CHUNK:(g + 1) * CHUNK] for c in range(GMLP_NC)], axis=1)


def _gmlp_fwd(proj, vg, vb, wsp, bsp_t, name):
    t = proj.shape[0]

    def body(u_ref, v_ref, vg_ref, vb_ref, w_ref, b_ref, ya_ref):
        u, _, vn, _, _, _ = _gmlp_common(u_ref[...].astype(F32), v_ref[...].astype(F32), vg_ref[...], vb_ref[...])
        mask = _tri(True)
        bt = b_ref[...]
        for g in range(GROUPS):
            w = jnp.where(mask, w_ref[g], 0.0).astype(BF16)
            vcat = _chunks_to_lanes(vn, g).astype(BF16)
            s = _dot(w, vcat, _NN) + bt[:, g:g + 1]
            for c in range(GMLP_NC):
                rows, cols = slice(c * CHUNK, (c + 1) * CHUNK), slice(g * CHUNK, (g + 1) * CHUNK)
                ya_ref[rows, cols] = (u[rows, cols] * s[:, c * CHUNK:(c + 1) * CHUNK]).astype(BF16)

    return pl.pallas_call(
        body, name=name, grid=(t // GMLP_TILE,),
        in_specs=[_row_spec(D_MODEL, 0, GMLP_TILE), _row_spec(D_MODEL, 1, GMLP_TILE), _vec_spec(D_MODEL),
                  _vec_spec(D_MODEL), pl.BlockSpec((GROUPS, CHUNK, CHUNK), lambda i: (0, 0, 0)),
                  pl.BlockSpec((CHUNK, GROUPS), lambda i: (0, 0))],
        out_specs=_row_spec(D_MODEL, 0, GMLP_TILE),
        out_shape=jax.ShapeDtypeStruct((t, D_MODEL), BF16),
        compiler_params=_params(("parallel",), 3 * _nbytes((GMLP_TILE, D_MODEL), F32)),
    )(proj, proj, vg, vb, wsp, bsp_t)


def _gmlp_bwd(proj, dya, vg, vb, wsp, bsp_t, dproj, name):
    t = proj.shape[0]

    def body(u_ref, v_ref, dya_ref, vg_ref, vb_ref, w_ref, b_ref, dproj_in, duv_ref, gw_ref, gbt_ref, gvg_ref, gvb_ref,
             dvn_scr, du_scr):
        del dproj_in
        u_pre, v_pre = u_ref[...].astype(F32), v_ref[...].astype(F32)
        vgv = vg_ref[...]
        u, vh, vn, rstd, cdf_u, cdf_v = _gmlp_common(u_pre, v_pre, vgv, vb_ref[...])
        dya = dya_ref[...]
        mask = _tri(True)
        bt = b_ref[...]
        first = pl.program_id(0) == 0

        @pl.when(first)
        def _():
            gw_ref[...] = jnp.zeros_like(gw_ref)
            gbt_ref[...] = jnp.zeros_like(gbt_ref)
            gvg_ref[...] = jnp.zeros_like(gvg_ref)
            gvb_ref[...] = jnp.zeros_like(gvb_ref)

        lane = lax.broadcasted_iota(jnp.int32, (CHUNK, GROUPS), 1)
        gbt = jnp.zeros((CHUNK, GROUPS), F32)
        for g in range(GROUPS):
            w = jnp.where(mask, w_ref[g], 0.0).astype(BF16)
            vcat = _chunks_to_lanes(vn, g).astype(BF16)
            s = _dot(w, vcat, _NN) + bt[:, g:g + 1]
            ds = _chunks_to_lanes(dya * u, g)
            gbt = jnp.where(lane == g, jnp.sum(ds, axis=1, keepdims=True), gbt)
            dsb = ds.astype(BF16)
            gw_ref[g] += jnp.where(mask, _dot(dsb, vcat, _NT), 0.0)
            dv = _dot(w, dsb, _TN)
            for c in range(GMLP_NC):
                rows, cols = slice(c * CHUNK, (c + 1) * CHUNK), slice(g * CHUNK, (g + 1) * CHUNK)
                dvn_scr[rows, cols] = dv[:, c * CHUNK:(c + 1) * CHUNK]
                du_scr[rows, cols] = dya[rows, cols] * s[:, c * CHUNK:(c + 1) * CHUNK]
        gbt_ref[...] += gbt
        dvn = dvn_scr[...]
        gvg_ref[...] += jnp.sum(dvn * vh, axis=0, keepdims=True)
        gvb_ref[...] += jnp.sum(dvn, axis=0, keepdims=True)
        dvh = dvn * vgv
        dv = rstd * (dvh - jnp.mean(dvh, axis=-1, keepdims=True) - vh * jnp.mean(dvh * vh, axis=-1, keepdims=True))
        duv_ref[:, :D_MODEL] = (du_scr[...] * _gelu_grad(u_pre, cdf_u)).astype(BF16)
        duv_ref[:, D_MODEL:] = (dv * _gelu_grad(v_pre, cdf_v)).astype(BF16)

    return pl.pallas_call(
        body, name=name, grid=(t // GMLP_TILE,),
        in_specs=[_row_spec(D_MODEL, 0, GMLP_TILE), _row_spec(D_MODEL, 1, GMLP_TILE), _row_spec(D_MODEL, 0, GMLP_TILE),
                  _vec_spec(D_MODEL), _vec_spec(D_MODEL), pl.BlockSpec((GROUPS, CHUNK, CHUNK), lambda i: (0, 0, 0)),
                  pl.BlockSpec((CHUNK, GROUPS), lambda i: (0, 0)), pl.BlockSpec(memory_space=pl.ANY)],
        out_specs=[_row_spec(2 * D_MODEL, 0, GMLP_TILE), pl.BlockSpec((GROUPS, CHUNK, CHUNK), lambda i: (0, 0, 0)),
                   pl.BlockSpec((CHUNK, GROUPS), lambda i: (0, 0)), _vec_spec(D_MODEL), _vec_spec(D_MODEL)],
        out_shape=[jax.ShapeDtypeStruct(dproj.shape, BF16), jax.ShapeDtypeStruct((GROUPS, CHUNK, CHUNK), F32),
                   jax.ShapeDtypeStruct((CHUNK, GROUPS), F32), jax.ShapeDtypeStruct((1, D_MODEL), F32),
                   jax.ShapeDtypeStruct((1, D_MODEL), F32)],
        scratch_shapes=[pltpu.VMEM((GMLP_TILE, D_MODEL), F32), pltpu.VMEM((GMLP_TILE, D_MODEL), F32)],
        input_output_aliases={7: 0},
        compiler_params=_params(("arbitrary",), 6 * _nbytes((GMLP_TILE, D_MODEL), F32)),
    )(proj, proj, dya, vg, vb, wsp, bsp_t, dproj)


CONV_TILE = 1024
CONV_COLS = 1024
CONV_RB = 32
HALO = SUBLANES


def _conv_fwd(proj, cw, cb, name):
    t = proj.shape[0]
    nj = CONV_DIM // CONV_COLS
    xcb = COL_XBC // CONV_COLS
    before = 2 * HALO
    rb = CONV_TILE // before

    def body(x_ref, prev_ref, cw_ref, cb_ref, pre_ref, xc_ref):
        i = pl.program_id(1)
        cw_v = cw_ref[...]
        cb_v = cb_ref[...]
        for b in range(CONV_TILE // CONV_RB):
            if b == 0:
                prev = jnp.where(i > 0, prev_ref[...].astype(F32)[HALO:, :], 0.0)
                ext = jnp.concatenate([prev, x_ref[:CONV_RB, :].astype(F32)], axis=0)
            else:
                ext = x_ref[b * CONV_RB - before:(b + 1) * CONV_RB, :].astype(F32)[HALO:, :]
            pre = cb_v + cw_v[CONV_WIDTH - 1:CONV_WIDTH, :] * ext[HALO:, :]
            for k in range(CONV_WIDTH - 1):
                back = CONV_WIDTH - 1 - k
                pre = pre + cw_v[k:k + 1, :] * pltpu.roll(ext, back, 0)[HALO:, :]
            pre_ref[b * CONV_RB:(b + 1) * CONV_RB, :] = pre
            xc_ref[b * CONV_RB:(b + 1) * CONV_RB, :] = pre * _sigmoid(pre)

    tile = pl.BlockSpec((CONV_TILE, CONV_COLS), lambda j, i: (i, j))
    return pl.pallas_call(
        body, name=name, grid=(nj, t // CONV_TILE),
        in_specs=[pl.BlockSpec((CONV_TILE, CONV_COLS), lambda j, i: (i, xcb + j)),
                  pl.BlockSpec((before, CONV_COLS), lambda j, i: (jnp.maximum(i * rb - 1, 0), xcb + j)),
                  pl.BlockSpec((CONV_WIDTH, CONV_COLS), lambda j, i: (0, j)),
                  pl.BlockSpec((1, CONV_COLS), lambda j, i: (0, j))],
        out_specs=[tile, tile],
        out_shape=[jax.ShapeDtypeStruct((t, CONV_DIM), F32), jax.ShapeDtypeStruct((t, CONV_DIM), F32)],
        compiler_params=_params(("parallel", "parallel"), 4 * _nbytes((CONV_TILE, CONV_COLS), F32)),
    )(proj, proj, cw, cb)


def _fold_rows(v):
    out = v[:SUBLANES]
    for r in range(1, v.shape[0] // SUBLANES):
        out = out + v[r * SUBLANES:(r + 1) * SUBLANES]
    return out


def _conv_bwd(proj, pre, dxc, cw, dproj, name):
    t = proj.shape[0]
    nj = CONV_DIM // CONV_COLS
    ni = t // CONV_TILE
    xcb = COL_XBC // CONV_COLS
    rb = CONV_TILE // HALO
    last_rb = t // HALO - 1

    def body(x_ref, p_ref, pnext_ref, d_ref, dnext_ref, cw_ref, dproj_in, dx_ref, gw_ref, gb_ref):
        del dproj_in
        i = pl.program_id(1)
        cw_v = cw_ref[...]

        def dpre_of(p, d):
            sg = _sigmoid(p)
            return d * sg * (1.0 + p * (1.0 - sg))

        @pl.when(i == 0)
        def _():
            gw_ref[...] = jnp.zeros_like(gw_ref)
            gb_ref[...] = jnp.zeros_like(gb_ref)

        head = dpre_of(pnext_ref[...], jnp.where(i < ni - 1, dnext_ref[...], 0.0))
        gb_acc = jnp.zeros((SUBLANES, CONV_COLS), F32)
        gw_acc = [jnp.zeros((SUBLANES, CONV_COLS), F32) for _ in range(CONV_WIDTH)]
        for b in reversed(range(CONV_TILE // CONV_RB)):
            rows = slice(b * CONV_RB, (b + 1) * CONV_RB)
            cur = dpre_of(p_ref[rows, :], d_ref[rows, :])
            ext = jnp.concatenate([cur, head], axis=0)
            xv = x_ref[rows, :].astype(F32)
            dx = None
            for k in range(CONV_WIDTH):
                shift = CONV_WIDTH - 1 - k
                win = cur if shift == 0 else pltpu.roll(ext, CONV_RB + HALO - shift, 0)[:CONV_RB, :]
                term = cw_v[k:k + 1, :] * win
                dx = term if dx is None else dx + term
                gw_acc[k] = gw_acc[k] + _fold_rows(win * xv)
            dx_ref[rows, :] = dx.astype(BF16)
            gb_acc = gb_acc + _fold_rows(cur)
            head = cur[:HALO]
        gb_ref[...] += jnp.sum(gb_acc, axis=0, keepdims=True)
        for k in range(CONV_WIDTH):
            gw_ref[k:k + 1, :] += jnp.sum(gw_acc[k], axis=0, keepdims=True)

    tile = pl.BlockSpec((CONV_TILE, CONV_COLS), lambda j, i: (i, j))
    after = pl.BlockSpec((HALO, CONV_COLS), lambda j, i: (jnp.minimum((i + 1) * rb, last_rb), j))
    return pl.pallas_call(
        body, name=name, grid=(nj, ni),
        in_specs=[pl.BlockSpec((CONV_TILE, CONV_COLS), lambda j, i: (i, xcb + j)), tile, after, tile, after,
                  pl.BlockSpec((CONV_WIDTH, CONV_COLS), lambda j, i: (0, j)),
                  pl.BlockSpec(memory_space=pl.ANY)],
        out_specs=[pl.BlockSpec((CONV_TILE, CONV_COLS), lambda j, i: (i, xcb + j)),
                   pl.BlockSpec((CONV_WIDTH, CONV_COLS), lambda j, i: (0, j)),
                   pl.BlockSpec((1, CONV_COLS), lambda j, i: (0, j))],
        out_shape=[jax.ShapeDtypeStruct(dproj.shape, BF16), jax.ShapeDtypeStruct((CONV_WIDTH, CONV_DIM), F32),
                   jax.ShapeDtypeStruct((1, CONV_DIM), F32)],
        input_output_aliases={6: 0},
        compiler_params=_params(("parallel", "arbitrary"), 4 * _nbytes((CONV_TILE, CONV_COLS), F32)),
    )(proj, pre, pre, dxc, dxc, cw, dproj)


def _ssd_decays(dt_raw, dtb, alog, e_bf, tril_bf):
    dtv = _softplus(dt_raw + dtb)
    a = -jnp.exp(alog)
    cs = _dot_exact_lhs(tril_bf, dtv * a, _NN)
    cs_last = cs[CHUNK - 1:CHUNK, :]
    stack = jnp.concatenate([dtv, jnp.exp(cs), jnp.exp(cs_last - cs)], axis=0)
    full = _head_expand(stack, e_bf)
    return dtv, a, cs, full[:CHUNK], full[CHUNK:2 * CHUNK], full[2 * CHUNK:]


def _split2(x):
    hi = x.astype(BF16)
    return hi, (x - hi.astype(F32)).astype(BF16)


def _head_expand(x, e_bf):
    hi, mid = _split2(x)
    return _dot(hi, e_bf, _NN) + _dot(mid, e_bf, _NN)


def _head_sums(x, e_bf):
    hi, mid = _split2(x)
    return _dot(hi, e_bf, _NT) + _dot(mid, e_bf, _NT)


def _head_mats(cs, cs_t, cb, h, mask):
    seg = cs[:, h:h + 1] - cs_t[h:h + 1, :]
    lmat = jnp.exp(jnp.where(mask, seg, -jnp.inf))
    return lmat, cb * lmat


def _ssd_fwd(xc, proj, dt_raw, dtb, alog, dskip_full, ng, e_bf, name):
    t = xc.shape[0]
    nc = t // CHUNK
    zcb = COL_Z // D_INNER

    def body(xc_ref, z_ref, dt_ref, dtb_ref, alog_ref, dsk_ref, ng_ref, e_ref, y_ref, yb_ref, sprev_ref, s_scr):
        @pl.when(pl.program_id(0) == 0)
        def _():
            s_scr[...] = jnp.zeros_like(s_scr)

        mask = _tri(True)
        tril_bf = mask.astype(BF16)
        e_v = e_ref[...]
        _, _, cs, dt_full, ecs_full, decay_full = _ssd_decays(dt_ref[...], dtb_ref[...], alog_ref[...], e_v, tril_bf)
        cs_t = cs.T
        sprev_ref[0] = s_scr[...]
        for g in range(GROUPS):
            gc = slice(g * GROUP_W, (g + 1) * GROUP_W)
            xs = xc_ref[:, gc]
            xdt = xs * dt_full[:, gc]
            xdt_b = xdt.astype(BF16)
            xdec = (xdt * decay_full[:, gc]).astype(BF16)
            bg = xc_ref[:, D_INNER + g * D_STATE:D_INNER + (g + 1) * D_STATE].astype(BF16)
            cg = xc_ref[:, D_INNER + GROUPS * D_STATE + g * D_STATE:D_INNER + GROUPS * D_STATE + (g + 1) * D_STATE].astype(BF16)
            cb = _dot(cg, bg, _NT)
            s_prev = s_scr[:, gc]
            y_off = ecs_full[:, gc] * _dot(cg, s_prev.astype(BF16), _NN)
            s_scr[:, gc] = s_prev * ecs_full[CHUNK - 1:CHUNK, gc] + _dot(bg, xdec, _TN)
            parts = []
            for r in range(GROUP_W // HEAD_DIM):
                h = g * (GROUP_W // HEAD_DIM) + r
                _, m = _head_mats(cs, cs_t, cb, h, mask)
                parts.append(_dot(m.astype(BF16), xdt_b[:, r * HEAD_DIM:(r + 1) * HEAD_DIM], _NN))
            yg = jnp.concatenate(parts, axis=1) + y_off + dsk_ref[:, gc] * xs
            y_ref[:, gc] = yg
            zv = z_ref[:, gc].astype(F32)
            ygate = yg * (zv * _sigmoid(zv))
            rstd = lax.rsqrt(jnp.mean(ygate * ygate, axis=-1, keepdims=True) + NORM_EPS)
            yb_ref[:, gc] = (ygate * rstd * ng_ref[:, gc]).astype(BF16)

    vec = lambda w: pl.BlockSpec((1, w), lambda i: (0, 0))
    blk = _nbytes((CHUNK, CONV_DIM), F32) + 3 * _nbytes((CHUNK, D_INNER), F32) + _nbytes((D_STATE, D_INNER), F32)
    return pl.pallas_call(
        body, name=name, grid=(nc,),
        in_specs=[pl.BlockSpec((CHUNK, CONV_DIM), lambda i: (i, 0)), pl.BlockSpec((CHUNK, D_INNER), lambda i: (i, zcb)),
                  pl.BlockSpec((CHUNK, DT_PAD), lambda i: (i, 0)), vec(DT_PAD), vec(DT_PAD), vec(D_INNER), vec(D_INNER),
                  pl.BlockSpec((DT_PAD, D_INNER), lambda i: (0, 0))],
        out_specs=[pl.BlockSpec((CHUNK, D_INNER), lambda i: (i, 0)), pl.BlockSpec((CHUNK, D_INNER), lambda i: (i, 0)),
                   pl.BlockSpec((1, D_STATE, D_INNER), lambda i: (i, 0, 0))],
        out_shape=[jax.ShapeDtypeStruct((t, D_INNER), F32), jax.ShapeDtypeStruct((t, D_INNER), BF16),
                   jax.ShapeDtypeStruct((nc, D_STATE, D_INNER), F32)],
        scratch_shapes=[pltpu.VMEM((D_STATE, D_INNER), F32)],
        compiler_params=_params(("arbitrary",), blk),
    )(xc, proj, dt_raw, dtb, alog, dskip_full, ng, e_bf)


def _ssd_bwd(dyb, y, xc, proj, dt_raw, sprev, dtb, alog, dskip_full, ng, e_bf, dproj, name):
    t = xc.shape[0]
    nc = t // CHUNK
    zcb = COL_Z // D_INNER
    hpg = GROUP_W // HEAD_DIM
    rev = lambda i: nc - 1 - i

    def body(dyb_ref, y_ref, xc_ref, z_ref, dt_ref, sprev_ref, dtb_ref, alog_ref, dsk_ref, ng_ref, e_ref, dproj_in,
             dz_ref, dxc_ref, ddt_ref, gng_ref, gdsk_ref, galog_ref, gdtb_ref, ds_scr, sums_scr):
        del dproj_in

        @pl.when(pl.program_id(0) == 0)
        def _():
            ds_scr[...] = jnp.zeros_like(ds_scr)
            gng_ref[...] = jnp.zeros_like(gng_ref)
            gdsk_ref[...] = jnp.zeros_like(gdsk_ref)
            galog_ref[...] = jnp.zeros_like(galog_ref)
            gdtb_ref[...] = jnp.zeros_like(gdtb_ref)

        mask = _tri(True)
        tril_bf = mask.astype(BF16)
        triu_bf = _tri(False).astype(BF16)
        e_v = e_ref[...]
        dt_in = dt_ref[...] + dtb_ref[...]
        dtv, a, cs, dt_full, ecs_full, decay_full = _ssd_decays(dt_ref[...], dtb_ref[...], alog_ref[...], e_v, tril_bf)
        cs_t = cs.T

        lane_h = lax.broadcasted_iota(jnp.int32, (CHUNK, DT_PAD), 1)
        sub_h = lax.broadcasted_iota(jnp.int32, (DT_PAD, CHUNK), 0)
        dcs_rows = jnp.zeros((CHUNK, DT_PAD), F32)
        dcs_cols_t = jnp.zeros((DT_PAD, CHUNK), F32)
        last_cols, dsk_cols = [], []
        for g in range(GROUPS):
            gc = slice(g * GROUP_W, (g + 1) * GROUP_W)
            b_cols = slice(D_INNER + g * D_STATE, D_INNER + (g + 1) * D_STATE)
            c_cols = slice(D_INNER + GROUPS * D_STATE + g * D_STATE, D_INNER + GROUPS * D_STATE + (g + 1) * D_STATE)
            xs = xc_ref[:, gc]
            xdt = xs * dt_full[:, gc]
            xdt_b = xdt.astype(BF16)
            xdec = xdt * decay_full[:, gc]
            xdec_b = xdec.astype(BF16)
            zv = z_ref[:, gc].astype(F32)
            sg = _sigmoid(zv)
            gate = zv * sg
            yv = y_ref[:, gc]
            dybv = dyb_ref[:, gc]
            ygate = yv * gate
            rstd = lax.rsqrt(jnp.mean(ygate * ygate, axis=-1, keepdims=True) + NORM_EPS)
            yn = ygate * rstd
            gng_ref[:, gc] += jnp.sum(dybv * yn, axis=0, keepdims=True)
            dyn = dybv * ng_ref[:, gc]
            dyg = rstd * (dyn - yn * jnp.mean(dyn * yn, axis=-1, keepdims=True))
            dz_ref[:, gc] = (dyg * yv * sg * (1.0 + zv * (1.0 - sg))).astype(BF16)
            dy = dyg * gate
            dy_b = dy.astype(BF16)
            dyo = dy * ecs_full[:, gc]
            dyo_b = dyo.astype(BF16)
            dsk_cols.append(jnp.sum(dy * xs, axis=0, keepdims=True))

            bg = xc_ref[:, b_cols].astype(BF16)
            cg = xc_ref[:, c_cols].astype(BF16)
            s_prev = sprev_ref[0, :, gc]
            s_prev_b = s_prev.astype(BF16)
            dsg = ds_scr[:, gc]
            dsg_b = dsg.astype(BF16)
            cb = _dot(cg, bg, _NT)
            c_s = _dot(cg, s_prev_b, _NN)
            b_ds = _dot(bg, dsg_b, _NN)
            dcb = jnp.zeros((CHUNK, CHUNK), F32)
            parts = []
            for r in range(hpg):
                h = g * hpg + r
                hc = slice(r * HEAD_DIM, (r + 1) * HEAD_DIM)
                lmat, m = _head_mats(cs, cs_t, cb, h, mask)
                dm = _dot(dy_b[:, hc], xdt_b[:, hc], _NT)
                parts.append(_dot(m.astype(BF16), dy_b[:, hc], _TN))
                dcb = dcb + dm * lmat
                w = dm * m
                dcs_rows = jnp.where(lane_h == h, jnp.sum(w, axis=1, keepdims=True), dcs_rows)
                dcs_cols_t = jnp.where(sub_h == h, jnp.sum(w, axis=0, keepdims=True), dcs_cols_t)
            dxdt = jnp.concatenate(parts, axis=1) + decay_full[:, gc] * b_ds
            dcb_b = dcb.astype(BF16)
            dxc_ref[:, c_cols] = _dot(dcb_b, bg, _NN) + _dot(dyo_b, s_prev_b, _NT)
            dxc_ref[:, b_cols] = _dot(dcb_b, cg, _TN) + _dot(xdec_b, dsg_b, _NT)
            cdec = ecs_full[CHUNK - 1:CHUNK, gc]
            ds_scr[:, gc] = _dot(cg, dyo_b, _TN) + cdec * dsg
            dxc_ref[:, gc] = dxdt * dt_full[:, gc] + dsk_ref[:, gc] * dy
            dec_prod = xdec * b_ds
            sums_scr[:CHUNK, gc] = dyo * c_s - dec_prod
            sums_scr[CHUNK:, gc] = dxdt * xs
            last_cols.append(jnp.sum(dec_prod, axis=0, keepdims=True) + cdec * jnp.sum(dsg * s_prev, axis=0, keepdims=True))
        t_sums = _head_sums(sums_scr[...], e_v)
        tail = jnp.concatenate([jnp.concatenate(last_cols, axis=1), jnp.concatenate(dsk_cols, axis=1),
                                jnp.zeros((SUBLANES - 2, D_INNER), F32)], axis=0)
        t_tail = _dot_exact_rhs(tail, e_v, _NT)
        gdsk_ref[...] += t_tail[1:2, :]
        row = lax.broadcasted_iota(jnp.int32, (CHUNK, DT_PAD), 0)
        dcs = dcs_rows - dcs_cols_t.T + t_sums[:CHUNK] + jnp.where(row == CHUNK - 1, t_tail[0:1, :], 0.0)
        dda = _dot_exact_lhs(triu_bf, dcs, _NN)
        galog_ref[...] += jnp.sum(dda * dtv, axis=0, keepdims=True) * a
        ddt = dda * a + t_sums[CHUNK:]
        ddt_raw = jnp.where(lane_h < N_HEADS, ddt * _sigmoid(dt_in), 0.0)
        gdtb_ref[...] += jnp.sum(ddt_raw, axis=0, keepdims=True)
        ddt_ref[...] = ddt_raw.astype(BF16)

    vec = lambda w: pl.BlockSpec((1, w), lambda i: (0, 0))
    blk = (2 * _nbytes((CHUNK, CONV_DIM), F32) + 4 * _nbytes((CHUNK, D_INNER), F32) + 4 * _nbytes((D_STATE, D_INNER), F32))
    return pl.pallas_call(
        body, name=name, grid=(nc,),
        in_specs=[pl.BlockSpec((CHUNK, D_INNER), lambda i: (rev(i), 0)), pl.BlockSpec((CHUNK, D_INNER), lambda i: (rev(i), 0)),
                  pl.BlockSpec((CHUNK, CONV_DIM), lambda i: (rev(i), 0)), pl.BlockSpec((CHUNK, D_INNER), lambda i: (rev(i), zcb)),
                  pl.BlockSpec((CHUNK, DT_PAD), lambda i: (rev(i), 0)), pl.BlockSpec((1, D_STATE, D_INNER), lambda i: (rev(i), 0, 0)),
                  vec(DT_PAD), vec(DT_PAD), vec(D_INNER), vec(D_INNER), pl.BlockSpec((DT_PAD, D_INNER), lambda i: (0, 0)),
                  pl.BlockSpec(memory_space=pl.ANY)],
        out_specs=[pl.BlockSpec((CHUNK, D_INNER), lambda i: (rev(i), zcb)), pl.BlockSpec((CHUNK, CONV_DIM), lambda i: (rev(i), 0)),
                   pl.BlockSpec((CHUNK, DT_PAD), lambda i: (rev(i), 0)), vec(D_INNER), vec(DT_PAD), vec(DT_PAD), vec(DT_PAD)],
        out_shape=[jax.ShapeDtypeStruct(dproj.shape, BF16), jax.ShapeDtypeStruct((t, CONV_DIM), F32),
                   jax.ShapeDtypeStruct((t, DT_PAD), BF16), jax.ShapeDtypeStruct((1, D_INNER), F32),
                   jax.ShapeDtypeStruct((1, DT_PAD), F32), jax.ShapeDtypeStruct((1, DT_PAD), F32),
                   jax.ShapeDtypeStruct((1, DT_PAD), F32)],
        scratch_shapes=[pltpu.VMEM((D_STATE, D_INNER), F32), pltpu.VMEM((2 * CHUNK, D_INNER), F32)],
        input_output_aliases={11: 0},
        compiler_params=_params(("arbitrary",), blk),
    )(dyb, y, xc, proj, dt_raw, sprev, dtb, alog, dskip_full, ng, e_bf, dproj)


def _mesh_pos():
    return lax.axis_index("x"), lax.axis_index("y"), lax.axis_index("c")


def _other_chips(x, y):
    return [(1 - x, y), (x, 1 - y), (1 - x, 1 - y)]


def _all_peers(x, y, c):
    peers = []
    for k in range(1, N_DEV):
        fx, fy, fc = (k >> 2) & 1, (k >> 1) & 1, k & 1
        px, py, pc = x + fx - 2 * x * fx, y + fy - 2 * y * fy, c + fc - 2 * c * fc
        peers.append(((px, py, pc), 4 * px + 2 * py + pc))
    return peers


def _all_gather(shards, name, own_only=()):
    n, n_own = len(shards), len(own_only)

    def body(*refs):
        ins, own_ins = refs[:n], refs[n:n + n_own]
        outs, own_outs = refs[n + n_own:2 * n + n_own], refs[2 * n + n_own:2 * (n + n_own)]
        send_sems, recv_sems, local_sems = refs[2 * (n + n_own):]
        x, y, c = _mesh_pos()
        me, sibling = (x, y, c), (x, y, 1 - c)
        chips = _other_chips(x, y)

        def slot(p):
            return 4 * p[0] + 2 * p[1] + p[2]

        def copy(a, k, block, to, src=None):
            dst = outs[a].at[slot(block)]
            return pltpu.make_async_remote_copy(
                src_ref=dst if src is None else src, dst_ref=dst, send_sem=send_sems.at[a * 7 + k],
                recv_sem=recv_sems.at[a * 7 + k], device_id=to, device_id_type=MESH)

        started = []
        own = []
        for a in range(n_own):
            mine = pltpu.make_async_copy(own_ins[a], own_outs[a].at[slot(me)], local_sems.at[n + a])
            mine.start()
            own.append(mine)
        for a in range(n):
            mine = pltpu.make_async_copy(ins[a], outs[a].at[slot(me)], local_sems.at[a])
            mine.start()
            own.append(mine)
            first = [copy(a, 0, me, sibling, src=ins[a])]
            first += [copy(a, 1 + j, me, (*chip, c), src=ins[a]) for j, chip in enumerate(chips)]
            for cp in first:
                cp.start()
            started += first
        for a in range(n):
            for j, chip in enumerate(chips):
                copy(a, 1 + j, (*chip, c), me).wait_recv()
                fwd = copy(a, 4 + j, (*chip, c), sibling)
                fwd.start()
                started.append(fwd)
        for a in range(n):
            copy(a, 0, sibling, me).wait_recv()
            for j, chip in enumerate(chips):
                copy(a, 4 + j, (*chip, 1 - c), me).wait_recv()
        for cp in started:
            cp.wait_send()
        for mine in own:
            mine.wait()

    return pl.pallas_call(
        body, name=name,
        in_specs=[_HBM] * (n + n_own), out_specs=[_HBM] * (n + n_own),
        out_shape=[jax.ShapeDtypeStruct((N_DEV,) + s.shape, s.dtype) for s in (*shards, *own_only)],
        scratch_shapes=[pltpu.SemaphoreType.DMA((7 * n,)), pltpu.SemaphoreType.DMA((7 * n,)),
                        pltpu.SemaphoreType.DMA((n + n_own,))],
    )(*shards, *own_only)


_SMALL_ROWS = (("norm_mix_g", 8), ("conv_b", 32), ("dt_bias", 1), ("a_log", 1), ("d_skip", 1), ("ssm_norm_g", 16),
               ("v_norm_g", 8), ("v_norm_b", 8), ("w_spatial", 1024), ("b_spatial", 8), ("b_gates", 16), ("norm_mlp_g", 8),
               ("norm_final_g", 8), ("conv_w", 128), ("loss", 1))
_LAST_SMALL = (("norm_mix_g", 8),)


def _packed_rows(table):
    return -(-sum(r for _, r in table) // SUBLANES) * SUBLANES


def _small_offsets(table=_SMALL_ROWS):
    offs, r = {}, 0
    for name, rows in table:
        offs[name] = r
        r += rows
    return offs


def _rows_from(src_ref, dst_ref, r0):
    k, w = src_ref.shape
    if w <= LANES:
        dst_ref[r0:r0 + k, 0:w] = src_ref[...]
        return
    per = w // LANES
    for i in range(k):
        for j in range(per):
            dst_ref[r0 + i * per + j:r0 + i * per + j + 1, :] = src_ref[i:i + 1, j * LANES:(j + 1) * LANES]


def _rows_to(src_ref, r0, dst_ref):
    k, w = dst_ref.shape
    if w <= LANES:
        dst_ref[...] = src_ref[r0:r0 + k, 0:w]
        return
    per = w // LANES
    for i in range(k):
        for j in range(per):
            dst_ref[i:i + 1, j * LANES:(j + 1) * LANES] = src_ref[r0 + i * per + j:r0 + i * per + j + 1, :]


def _pack_small(grads, slot_idx, name):
    names = [n for n, _ in _SMALL_ROWS if n in grads]
    offs = _small_offsets()
    rows = _packed_rows(_SMALL_ROWS)

    def body(slot_ref, *refs):
        del slot_ref
        ins, (packed_ref, land_ref) = refs[:len(names)], refs[len(names):]
        packed_ref[...] = jnp.zeros_like(packed_ref)
        for n, ref in zip(names, ins):
            _rows_from(ref, packed_ref, offs[n])
        land_ref[0] = packed_ref[...]

    whole = lambda shape: pl.BlockSpec(shape, lambda i, slot_ref: (0,) * len(shape))
    grid_spec = pltpu.PrefetchScalarGridSpec(
        num_scalar_prefetch=1, grid=(1,), in_specs=[whole(grads[n].shape) for n in names],
        out_specs=[whole((rows, LANES)), pl.BlockSpec((1, rows, LANES), lambda i, slot_ref: (slot_ref[0], 0, 0))])
    return pl.pallas_call(
        body, name=name, grid_spec=grid_spec,
        out_shape=[jax.ShapeDtypeStruct((rows, LANES), F32), jax.ShapeDtypeStruct((N_DEV, rows, LANES), F32)],
    )(slot_idx, *[grads[n] for n in names])


def _exchange_small(grads, table, name):
    names = [n for n, _ in table]
    offs = _small_offsets(table)
    n_in = len(names)
    packed_rows = _packed_rows(table)

    def body(*refs):
        ins, out_ref = refs[:n_in], refs[n_in]
        packed, send_sems, recv_sems, local_sem = refs[n_in + 1:]
        packed[...] = jnp.zeros_like(packed)
        for n, ref in zip(names, ins):
            _rows_from(ref, packed, offs[n])
        x, y, c = _mesh_pos()
        my_slot = 4 * x + 2 * y + c
        mine = pltpu.make_async_copy(packed, out_ref.at[my_slot], local_sem)
        mine.start()
        copies = []
        for k, (peer, peer_slot) in enumerate(_all_peers(x, y, c)):
            sems = dict(send_sem=send_sems.at[k], recv_sem=recv_sems.at[k], device_id=peer, device_id_type=MESH)
            send = pltpu.make_async_remote_copy(src_ref=packed, dst_ref=out_ref.at[my_slot], **sems)
            send.start()
            copies.append((send, pltpu.make_async_remote_copy(src_ref=packed, dst_ref=out_ref.at[peer_slot], **sems)))
        for send, recv in copies:
            send.wait_send()
            recv.wait_recv()
        mine.wait()

    return pl.pallas_call(
        body, name=name, in_specs=[pl.BlockSpec(memory_space=pltpu.VMEM)] * n_in, out_specs=_HBM,
        out_shape=jax.ShapeDtypeStruct((N_DEV, packed_rows, LANES), F32),
        scratch_shapes=[pltpu.VMEM((packed_rows, LANES), F32), pltpu.SemaphoreType.DMA((N_DEV - 1,)),
                        pltpu.SemaphoreType.DMA((N_DEV - 1,)), pltpu.SemaphoreType.DMA],
    )(*[grads[n] for n in names])


def _swap_with_sibling(grads, name):
    n = len(grads)

    def body(*refs):
        ins, outs = refs[:n], refs[n:2 * n]
        send_sems, recv_sems = refs[2 * n:]
        x, y, c = _mesh_pos()
        copies = []
        for a in range(n):
            for k in range(N_CHIP):
                cp = pltpu.make_async_remote_copy(
                    src_ref=ins[a].at[(1 - c) + 2 * k], dst_ref=outs[a].at[k], send_sem=send_sems.at[a * N_CHIP + k],
                    recv_sem=recv_sems.at[a * N_CHIP + k], device_id=(x, y, 1 - c), device_id_type=MESH)
                cp.start()
                copies.append(cp)
        for cp in copies:
            cp.wait()

    return pl.pallas_call(
        body, name=name, in_specs=[_HBM] * n, out_specs=[_HBM] * n,
        out_shape=[jax.ShapeDtypeStruct((N_CHIP,) + g.shape[1:], g.dtype) for g in grads],
        scratch_shapes=[pltpu.SemaphoreType.DMA((N_CHIP * n,)), pltpu.SemaphoreType.DMA((N_CHIP * n,))],
    )(*grads)


_SEM = pl.BlockSpec(memory_space=pltpu.SEMAPHORE)
_IN_HBM = pl.BlockSpec(memory_space=pltpu.HBM)
_EFFECT = pltpu.SideEffectType.DATAFLOW_SIDE_EFFECTING


def _in_hbm(a):
    return pltpu.with_memory_space_constraint(a, pltpu.HBM)


def _gather_copies(ins, lands, send_sems, recv_sems):
    x, y, c = _mesh_pos()
    my_slot = 4 * x + 2 * y + c
    pairs = []
    for a in range(len(ins)):
        for k, (peer, peer_slot) in enumerate(_all_peers(x, y, c)):
            sems = dict(send_sem=send_sems.at[a * (N_DEV - 1) + k], recv_sem=recv_sems.at[a * (N_DEV - 1) + k],
                        device_id=peer, device_id_type=MESH)
            pairs.append((pltpu.make_async_remote_copy(src_ref=ins[a], dst_ref=lands[a].at[my_slot], **sems),
                          pltpu.make_async_remote_copy(src_ref=ins[a], dst_ref=lands[a].at[peer_slot], **sems)))
    return pairs


def _scatter_copies(ins, lands, send_sems, recv_sems):
    x, y, c = _mesh_pos()
    my_chip = 2 * x + y
    pairs = []
    for a in range(len(ins)):
        for j, chip in enumerate(_other_chips(x, y)):
            there = 2 * chip[0] + chip[1]
            sems = dict(send_sem=send_sems.at[a * 3 + j], recv_sem=recv_sems.at[a * 3 + j],
                        device_id=(*chip, c), device_id_type=MESH)
            pairs.append((pltpu.make_async_remote_copy(src_ref=ins[a].at[there], dst_ref=lands[a].at[my_chip], **sems),
                          pltpu.make_async_remote_copy(src_ref=ins[a].at[my_chip], dst_ref=lands[a].at[there], **sems)))
    return pairs


def _split_start(srcs, lands, copies, per_array, name):
    n = len(srcs)

    def body(*refs):
        ins, land_refs = refs[:n], refs[n:2 * n]
        send_sems, recv_sems = refs[2 * n], refs[2 * n + 1]
        token = refs[-1]
        for send, _ in copies(ins, land_refs, send_sems, recv_sems):
            send.start()
        token[...] = jnp.zeros_like(token)

    outs = pl.pallas_call(
        body, name=name,
        out_shape=(pltpu.SemaphoreType.DMA((per_array * n,)), pltpu.SemaphoreType.DMA((per_array * n,)),
                   *[pltpu.HBM(s.shape, s.dtype) for s in srcs], *[pltpu.HBM(l.shape, l.dtype) for l in lands],
                   jax.ShapeDtypeStruct((SUBLANES, LANES), F32)),
        in_specs=[_IN_HBM] * (2 * n),
        out_specs=(_SEM, _SEM, *[_IN_HBM] * (2 * n), pl.BlockSpec(memory_space=pltpu.VMEM)),
        input_output_aliases={i: 2 + i for i in range(2 * n)},
        compiler_params=pltpu.CompilerParams(has_side_effects=_EFFECT),
    )(*[_in_hbm(s) for s in srcs], *[_in_hbm(l) for l in lands])
    return outs[0], outs[1], list(outs[2:2 + n]), list(outs[2 + n:2 + 2 * n]), outs[-1]


def _split_wait(started, copies, after, name):
    send_sems, recv_sems, srcs, lands, _ = started
    n = len(srcs)

    def body(*refs):
        ins, land_refs = refs[:n], refs[n:2 * n]
        for send, recv in copies(ins, land_refs, refs[2 * n], refs[2 * n + 1]):
            send.wait_send()
            recv.wait_recv()

    outs = pl.pallas_call(
        body, name=name,
        out_shape=(*[pltpu.HBM(s.shape, s.dtype) for s in srcs], *[pltpu.HBM(l.shape, l.dtype) for l in lands]),
        in_specs=[_IN_HBM] * (2 * n) + [_SEM, _SEM, _HBM],
        out_specs=[_IN_HBM] * (2 * n),
        input_output_aliases={i: i for i in range(2 * n)},
        compiler_params=pltpu.CompilerParams(has_side_effects=_EFFECT),
    )(*srcs, *lands, send_sems, recv_sems, after)
    return list(outs[:n]), list(outs[n:])


def _ew_block(rows, cols, slots):
    budget = 8 * 1024 * 1024
    br, bc = rows, cols
    while slots * br * bc * 4 > budget:
        if br % 2 == 0 and (br // 2) % (2 * SUBLANES) == 0:
            br //= 2
        elif bc % 2 == 0 and (bc // 2) % LANES == 0:
            bc //= 2
        else:
            break
    return br, bc


def _add_sibling(grads, recv, c_idx, name):
    _, rows, cols = grads.shape
    br, bc = _ew_block(rows, cols, 3)

    def body(c_ref, g_ref, r_ref, out_ref):
        del c_ref
        out_ref[...] = (g_ref[...].astype(F32) + r_ref[...].astype(F32)).astype(out_ref.dtype)

    grid_spec = pltpu.PrefetchScalarGridSpec(
        num_scalar_prefetch=1, grid=(N_CHIP, rows // br, cols // bc),
        in_specs=[pl.BlockSpec((1, br, bc), lambda k, i, j, c_ref: (c_ref[0] + 2 * k, i, j)),
                  pl.BlockSpec((1, br, bc), lambda k, i, j, c_ref: (k, i, j))],
        out_specs=pl.BlockSpec((1, br, bc), lambda k, i, j, c_ref: (k, i, j)))
    return pl.pallas_call(
        body, name=name, grid_spec=grid_spec, out_shape=jax.ShapeDtypeStruct((N_CHIP, rows, cols), grads.dtype),
        compiler_params=_params(("parallel", "parallel", "parallel"), 3 * _nbytes((br, bc), F32)),
    )(c_idx, grads, recv)


def _adam_math(g, w, m, v):
    m2 = ADAM_B1 * m + (1.0 - ADAM_B1) * g
    v2 = ADAM_B2 * v + (1.0 - ADAM_B2) * (g * g)
    m_hat = m2 * (1.0 / (1.0 - ADAM_B1 ** ADAM_STEP))
    v_hat = v2 * (1.0 / (1.0 - ADAM_B2 ** ADAM_STEP))
    return -ADAM_LR * (m_hat / (jnp.sqrt(v_hat) + ADAM_EPS) + ADAM_WD * w), m2, v2


def _adamw(slots, w, m, v, name, own=None, own_slot=None):
    ns, rows, cols = slots.shape
    br, bc = _ew_block(rows, cols, 2 * ns + 7)

    def update(g, w_ref, m_ref, v_ref, g_ref, d_ref, m2_ref, v2_ref):
        g_ref[...] = g
        d_ref[...], m2_ref[...], v2_ref[...] = _adam_math(g, w_ref[...], m_ref[...], v_ref[...])

    out_shape = [jax.ShapeDtypeStruct((rows, cols), F32)] * 4
    params = _params(("parallel", "parallel"), (2 * ns + 7) * _nbytes((br, bc), F32))
    grid = (rows // br, cols // bc)
    if own is None:
        def body(s_ref, *rest):
            g = s_ref[0].astype(F32)
            for k in range(1, ns):
                g = g + s_ref[k].astype(F32)
            update(g, *rest)

        blk = pl.BlockSpec((br, bc), lambda i, j: (i, j))
        return pl.pallas_call(
            body, name=name, grid=grid,
            in_specs=[pl.BlockSpec((ns, br, bc), lambda i, j: (0, i, j)), blk, blk, blk], out_specs=[blk] * 4,
            out_shape=out_shape, compiler_params=params,
        )(slots, w, m, v)

    def body_own(slot_ref, s_ref, o_ref, *rest):
        g = None
        for k in range(ns):
            term = jnp.where(slot_ref[0] == k, o_ref[k].astype(F32), s_ref[k].astype(F32))
            g = term if g is None else g + term
        update(g, *rest)

    blk = pl.BlockSpec((br, bc), lambda i, j, slot_ref: (i, j))
    stack = pl.BlockSpec((ns, br, bc), lambda i, j, slot_ref: (0, i, j))
    grid_spec = pltpu.PrefetchScalarGridSpec(num_scalar_prefetch=1, grid=grid, in_specs=[stack, stack, blk, blk, blk],
                                             out_specs=[blk] * 4)
    return pl.pallas_call(body_own, name=name, grid_spec=grid_spec, out_shape=out_shape, compiler_params=params,
                          )(own_slot, slots, own, w, m, v)


def _adamw_small(all_g, last_g, params, extra_shapes, name):
    names = [n for n, _ in _SMALL_ROWS if n in params]
    extras = [n for n, _ in _SMALL_ROWS if n not in params]
    offs = _small_offsets()
    n_p = len(names)

    def body(*refs):
        s_ref, last_ref = refs[0], refs[1]
        wmv = refs[2:2 + 3 * n_p]
        outs = refs[2 + 3 * n_p:2 + 7 * n_p]
        extra_refs = refs[2 + 7 * n_p:2 + 7 * n_p + len(extras)]
        summed = refs[-1]
        g, g_last = s_ref[0], last_ref[0]
        for k in range(1, N_DEV):
            g, g_last = g + s_ref[k], g_last + last_ref[k]
        summed[...] = g
        last_offs = _small_offsets(_LAST_SMALL)
        for n, rows in _LAST_SMALL:
            summed[offs[n]:offs[n] + rows, :] = g_last[last_offs[n]:last_offs[n] + rows, :]
        for i, n in enumerate(names):
            w_ref, m_ref, v_ref = wmv[3 * i:3 * i + 3]
            g_ref, d_ref, m2_ref, v2_ref = outs[4 * i:4 * i + 4]
            _rows_to(summed, offs[n], g_ref)
            d_ref[...], m2_ref[...], v2_ref[...] = _adam_math(g_ref[...], w_ref[...], m_ref[...], v_ref[...])
        for n, ref in zip(extras, extra_refs):
            _rows_to(summed, offs[n], ref)

    flat = [a for n in names for a in params[n]]
    out_shape = [jax.ShapeDtypeStruct(params[n][0].shape, F32) for n in names for _ in range(4)]
    out_shape += [jax.ShapeDtypeStruct(s, F32) for s in extra_shapes]
    vmem = pl.BlockSpec(memory_space=pltpu.VMEM)
    res = pl.pallas_call(
        body, name=name, in_specs=[vmem] * (2 + len(flat)), out_specs=[vmem] * len(out_shape), out_shape=out_shape,
        scratch_shapes=[pltpu.VMEM(all_g.shape[1:], F32)],
        compiler_params=pltpu.CompilerParams(vmem_limit_bytes=_vmem_limit(_nbytes(all_g.shape, F32))),
    )(all_g, last_g, *flat)
    return {n: res[4 * i:4 * i + 4] for i, n in enumerate(names)}, res[4 * n_p:]


def _mm_tiles(mode, m, n, k):
    tn = min(n, 1024)
    if mode == "tn":
        return min(m, 1024), tn, min(k, 4096)
    if k <= 1024:
        return min(m, 2048), tn, k
    if k <= 2048:
        return min(m, 1024), tn, k
    if k <= 4096:
        return min(m, 512), tn, k
    return min(m, 1024), tn, 2048


def _local_step(x, target, wts, small, exchange):
    t = x.shape[0]
    assert t % CONV_TILE == 0 and t % GMLP_TILE == 0 and t % (2 * ROW_TILE) == 0, t
    w_main_t, w_dt_t = wts["w_main_t"], wts["w_dt_t"]
    bsp_t = small["b_spatial"].T
    pad32 = lambda a: jnp.pad(a, ((0, 0), (0, DT_PAD - N_HEADS)))
    dtb, alog = pad32(small["dt_bias"]), pad32(small["a_log"])
    dskip_full = jnp.repeat(small["d_skip"], HEAD_DIM, axis=1)
    head_of_col = lax.broadcasted_iota(jnp.int32, (DT_PAD, D_INNER), 1) // HEAD_DIM
    e_bf = (head_of_col == lax.broadcasted_iota(jnp.int32, (DT_PAD, D_INNER), 0)).astype(BF16)

    def mm(a, b, mode, name, **kw):
        if mode == "nn":
            m, k, n = a.shape[0], a.shape[1], b.shape[1]
        elif mode == "nt":
            m, k, n = a.shape[0], a.shape[1], b.shape[0]
        else:
            m, k, n = a.shape[1], a.shape[0], b.shape[1]
        tm, tn, tk = _mm_tiles(mode, m, n, k)
        tm = min(tm, kw.pop("max_tm", tm))
        kw.setdefault("out_dtypes", (BF16,) if mode == "tn" else (F32,))
        if "extra_specs" in kw:
            kw["extra_specs"] = kw["extra_specs"](tm, tn)
        return _matmul(a, b, mode=mode, tm=tm, tn=tn, tk=tk, name=name, **kw)

    def out_tile(tm, tn):
        return (((tm, tn), lambda i, j: (i, j)),)

    def row_tiles(n_tiles, *vectors, gate_logits=False):
        def specs(tm, tn):
            out = [((tm, tn), lambda i, j: (i, j))] * n_tiles
            if gate_logits:
                out += [((tm, D_MODEL), lambda i, j, cb=COL_GATE // D_MODEL + half: (i, cb)) for half in range(2)]
            return tuple(out) + tuple(((1, w), lambda i, j, cb=cb: (0, cb)) for w, cb in vectors)
        return specs

    vec = lambda w: ((1, w), F32, (1, w), lambda i, j: (0, 0))
    fused_tm = 512

    h, dt_raw = _rms_fwd(x, small["norm_mix_g"], w_dt_t, "rms_mix", deps=exchange.begin())
    proj = mm(h, w_main_t, "nt", "proj_main", j_outer=True, out_dtypes=(BF16,))
    y_a = _gmlp_fwd(proj, small["v_norm_g"], small["v_norm_b"], small["w_spatial"], bsp_t, "gmlp_fwd")
    pre_conv, xc = _conv_fwd(proj, wts["conv_w"], small["conv_b"], "conv_fwd")
    y_ssd, y_b, sprev = _ssd_fwd(xc, proj, dt_raw, dtb, alog, dskip_full, small["ssm_norm_g"], e_bf, "ssd_fwd")
    wts = {**wts, **exchange.late_weights(y_b)}
    pa = mm(y_a, wts["w_proj_a"], "nn", "proj_a")
    pb, merged = mm(y_b, wts["w_proj_b"], "nn", "proj_b", epilogue=_merge_epilogue, out_dtypes=(F32, BF16), max_tm=fused_tm,
                    extras=(pa, proj, proj, small["b_gates"], small["b_gates"]),
                    extra_specs=row_tiles(1, (D_MODEL, 0), (D_MODEL, 1), gate_logits=True))
    x1, h2 = mm(merged, wts["w_out"], "nn", "out_proj", epilogue=_residual_rms_epilogue, out_dtypes=(F32, BF16),
                max_tm=2 * fused_tm, extras=(x, small["norm_mlp_g"]), extra_specs=row_tiles(1, (D_MODEL, 0)))

    def relu_sq(acc, ex, outs, first):
        r = jnp.maximum(acc, 0.0)
        outs[0][...] = (r * r).astype(BF16)

    act = mm(h2, wts["w_mlp_up"], "nn", "mlp_up", epilogue=relu_sq, out_dtypes=(BF16,), j_outer=True)
    dx2, dx2_b, g_final, _, loss = mm(
        act, wts["w_mlp_down"], "nn", "mlp_down", epilogue=_loss_epilogue, carry=True,
        out_dtypes=(F32, BF16, vec(D_MODEL), vec(D_MODEL), vec(LANES)),
        extras=(x1, small["norm_final_g"], target), extra_specs=lambda tm, tn: (
            ((tm, tn), lambda i, j: (i, j)), ((1, tn), lambda i, j: (0, 0)), ((tm, tn), lambda i, j: (i, j))))

    def relu_sq_bwd(acc, ex, outs, first):
        outs[0][...] = (acc * 2.0 * jnp.sqrt(ex[0][...].astype(F32))).astype(BF16)

    dup = mm(dx2_b, wts["w_mlp_down"], "nt", "d_act", epilogue=relu_sq_bwd, extras=(act,), extra_specs=out_tile,
             out_dtypes=(BF16,), j_outer=True)
    g_down = mm(act, dx2_b, "tn", "g_mlp_down")
    g_up = mm(h2, dup, "tn", "g_mlp_up")
    dx1, dx1_b, g_mlp = mm(
        dup, wts["w_mlp_up"], "nt", "d_h2", epilogue=_rms_bwd_epilogue, carry=True,
        out_dtypes=(F32, BF16, vec(D_MODEL)), extras=(x1, small["norm_mlp_g"], dx2), extra_specs=lambda tm, tn: (
            ((tm, tn), lambda i, j: (i, j)), ((1, tn), lambda i, j: (0, 0)), ((tm, tn), lambda i, j: (i, j))))

    g_out = mm(merged, dx1_b, "tn", "g_out")
    dpa, dpb, dproj, g_bgates = mm(
        dx1_b, wts["w_out"], "nt", "d_merged", epilogue=_merge_bwd_epilogue, carry=True, max_tm=fused_tm,
        out_dtypes=(BF16, BF16, ((t, MAIN_W), BF16, (fused_tm, 2 * D_MODEL), lambda i, j: (i, COL_GATE // (2 * D_MODEL))),
                    vec(2 * D_MODEL)),
        extras=(pa, pb, proj, proj, small["b_gates"], small["b_gates"]),
        extra_specs=row_tiles(2, (D_MODEL, 0), (D_MODEL, 1), gate_logits=True))
    g_pa = mm(y_a, dpa, "tn", "g_proj_a")
    g_pb = mm(y_b, dpb, "tn", "g_proj_b")
    started = exchange.reduce("late", {"w_mlp_down": g_down, "w_mlp_up": g_up, "w_out": g_out, "w_proj_a": g_pa,
                                       "w_proj_b": g_pb})
    dya = mm(dpa, wts["w_proj_a"], "nt", "d_ya", deps=started)
    dyb = mm(dpb, wts["w_proj_b"], "nt", "d_yb")

    dproj, g_wsp, g_bsp_t, g_vg, g_vb = _gmlp_bwd(proj, dya, small["v_norm_g"], small["v_norm_b"], small["w_spatial"],
                                                   bsp_t, dproj, "gmlp_bwd")
    dproj, dxc, ddt, g_ng, g_dskip, g_alog, g_dtb = _ssd_bwd(dyb, y_ssd, xc, proj, dt_raw, sprev, dtb, alog, dskip_full,
                                                             small["ssm_norm_g"], e_bf, dproj, "ssd_bwd")
    dproj, g_convw, g_convb = _conv_bwd(proj, pre_conv, dxc, wts["conv_w"], dproj, "conv_bwd")

    small_grads = {
        "conv_w": g_convw, "loss": loss,
        "conv_b": g_convb, "dt_bias": g_dtb, "a_log": g_alog, "d_skip": g_dskip, "ssm_norm_g": g_ng,
        "v_norm_g": g_vg, "v_norm_b": g_vb, "w_spatial": g_wsp.reshape(GROUPS * CHUNK, CHUNK), "b_spatial": g_bsp_t.T,
        "b_gates": g_bgates, "norm_mlp_g": g_mlp, "norm_final_g": g_final,
    }
    g_main_t = mm(dproj, h, "tn", "g_in_main", deps=exchange.small(small_grads))
    g_dt_t = mm(ddt, h, "tn", "g_in_dt")
    started = exchange.reduce("in", {"w_in": (g_main_t, g_dt_t)})

    def input_grad(acc, ex, outs, first):
        x_ref, g_ref, res_ref, ddt_ref, wdt_ref = ex
        gg = jnp.zeros((1, D_MODEL), F32)
        for r in range(acc.shape[0] // ROW_TILE):
            rows = slice(r * ROW_TILE, (r + 1) * ROW_TILE)
            dh = acc[rows] + _dot(ddt_ref[rows, :], wdt_ref[...], _NN)
            dx, gg_r = _rms_pullback(x_ref[rows, :], g_ref[...], dh)
            outs[0][rows, :] = dx + res_ref[rows, :]
            gg = gg + gg_r

        _zero_when(first, outs[1])
        outs[1][...] += gg

    grad_x, g_mix = mm(
        dproj, w_main_t, "nn", "d_h", epilogue=input_grad, deps=started, carry=True,
        out_dtypes=(F32, vec(D_MODEL)), extras=(x, small["norm_mix_g"], dx1, ddt, w_dt_t), extra_specs=lambda tm, tn: (
            ((tm, tn), lambda i, j: (i, j)), ((1, tn), lambda i, j: (0, 0)), ((tm, tn), lambda i, j: (i, j)),
            ((tm, DT_PAD), lambda i, j: (i, 0)), ((DT_PAD, D_MODEL), lambda i, j: (0, 0))))

    return grad_x, g_mix


SHARD_ROWS = (MAIN_W + N_HEADS) // N_DEV
REGROUP_IN = 2048


def _main_rows_of(gathered, name):
    n_dev, shard, d = gathered.shape
    blk = 1024
    nb = MAIN_W // blk

    def first_feature(b):
        return b * blk + (N_HEADS if b * blk >= COL_GATE else 0)

    def body(a_ref, b_ref, out_ref):
        for b in range(nb):
            s0, r0 = divmod(first_feature(b), shard)
            n1 = min(shard - r0, blk)

            @pl.when(pl.program_id(0) == b)
            def _(r0=r0, n1=n1):
                out_ref[0:n1, :] = a_ref[0, r0:r0 + n1, :]
                if n1 < blk:
                    out_ref[n1:blk, :] = b_ref[0, 0:blk - n1, :]

    def slot(b):
        return (b * blk + jnp.where(b * blk >= COL_GATE, N_HEADS, 0)) // shard

    return pl.pallas_call(
        body, name=name, grid=(nb,),
        in_specs=[pl.BlockSpec((1, shard, d), lambda b: (slot(b), 0, 0)),
                  pl.BlockSpec((1, shard, d), lambda b: (jnp.minimum(slot(b) + 1, n_dev - 1), 0, 0))],
        out_specs=pl.BlockSpec((blk, d), lambda b: (b, 0)),
        out_shape=jax.ShapeDtypeStruct((MAIN_W, d), gathered.dtype),
        compiler_params=_params(("parallel",), 3 * _nbytes((shard, d), gathered.dtype)),
    )(gathered, gathered)


def _by_device_rows(g_main_t, g_dt_t, name):
    d = g_main_t.shape[1]
    n_blocks = MAIN_W // REGROUP_IN
    dt_dev, dt_row = divmod(COL_GATE, SHARD_ROWS)

    def main_start(s):
        return s * SHARD_ROWS - (N_HEADS if s > dt_dev else 0)

    def body(a_ref, b_ref, dt_ref, out_ref):
        for s in range(N_DEV):
            m0 = main_start(s)
            k0, off = divmod(m0, REGROUP_IN)
            pieces = []
            if s == dt_dev:
                pieces = [(0, dt_row, m0), (dt_row, N_HEADS, None), (dt_row + N_HEADS, SHARD_ROWS - dt_row - N_HEADS, m0 + dt_row)]
            else:
                pieces = [(0, SHARD_ROWS, m0)]

            @pl.when(pl.program_id(0) == s)
            def _(pieces=pieces, k0=k0):
                for dst, n, src in pieces:
                    if src is None:
                        out_ref[0, dst:dst + n, :] = dt_ref[0:n, :]
                        continue
                    lo = src - k0 * REGROUP_IN
                    n_a = max(0, min(n, REGROUP_IN - lo))
                    if n_a:
                        out_ref[0, dst:dst + n_a, :] = a_ref[lo:lo + n_a, :]
                    if n_a < n:
                        lo_b = max(lo - REGROUP_IN, 0)
                        out_ref[0, dst + n_a:dst + n, :] = b_ref[lo_b:lo_b + n - n_a, :]

    def first_block(s):
        return (s * SHARD_ROWS - jnp.where(s > dt_dev, N_HEADS, 0)) // REGROUP_IN

    return pl.pallas_call(
        body, name=name, grid=(N_DEV,),
        in_specs=[pl.BlockSpec((REGROUP_IN, d), lambda s: (first_block(s), 0)),
                  pl.BlockSpec((REGROUP_IN, d), lambda s: (jnp.minimum(first_block(s) + 1, n_blocks - 1), 0)),
                  pl.BlockSpec((DT_PAD, d), lambda s: (0, 0))],
        out_specs=pl.BlockSpec((1, SHARD_ROWS, d), lambda s: (s, 0, 0)),
        out_shape=jax.ShapeDtypeStruct((N_DEV, SHARD_ROWS, d), g_main_t.dtype),
        compiler_params=_params(("parallel",), 3 * _nbytes((REGROUP_IN, d), g_main_t.dtype)),
    )(g_main_t, g_main_t, g_dt_t)


_LATE = ["w_proj_a", "w_proj_b", "w_out", "w_mlp_up", "w_mlp_down"]
_BY_COLS = ("w_mlp_up",)


class _Exchange:
    def __init__(self, late_shards, late_lands):
        self.late_shards, self.late_lands = late_shards, late_lands
        self.c_idx = lax.axis_index("c").astype(jnp.int32).reshape(1)
        self.chip_idx = (2 * lax.axis_index("x") + lax.axis_index("y")).astype(jnp.int32).reshape(1)
        self.pending = []

    def begin(self):
        self.late = _split_start(self.late_shards, self.late_lands, _gather_copies, N_DEV - 1, "gather_late_start")
        return [self.late[-1]]

    def late_weights(self, after):
        _, lands = _split_wait(self.late, _gather_copies, after, "gather_late_wait")
        whole = {}
        for n, g in zip(_LATE, lands):
            whole[n] = jnp.transpose(g, (1, 0, 2)).reshape(g.shape[1], -1) if n in _BY_COLS else g.reshape(-1, g.shape[2])
        return whole

    def reduce(self, tag, grads):
        names = list(grads)
        by_dev = []
        for n in names:
            g = grads[n]
            if n == "w_in":
                by_dev.append(_by_device_rows(*g, "regroup_g_in"))
            elif n in _BY_COLS:
                by_dev.append(jnp.transpose(g.reshape(g.shape[0], N_DEV, -1), (1, 0, 2)))
            else:
                by_dev.append(g.reshape(N_DEV, -1, g.shape[1]))
        from_sibling = _swap_with_sibling(by_dev, "reduce_cores_" + tag)
        parts = [_add_sibling(g, r, self.c_idx, "add_cores_" + n) for n, g, r in zip(names, by_dev, from_sibling)]
        lands = [lax.empty(p.shape, p.dtype) for p in parts]
        started = _split_start(parts, lands, _scatter_copies, 3, "reduce_chips_start_" + tag)
        self.pending.append((tag, names, started))
        return [started[-1]]

    def small(self, grads):
        dev = 2 * self.chip_idx + self.c_idx
        packed, land = _pack_small(grads, dev, "pack_small")
        self.small_started = _split_start([packed], [land], _gather_copies, N_DEV - 1, "exchange_small_start")
        return [self.small_started[-1]]

    def finish(self, after):
        _, (all_small,) = _split_wait(self.small_started, _gather_copies, after, "exchange_small_wait")
        done = {}
        for tag, names, started in self.pending:
            parts, lands = _split_wait(started, _scatter_copies, after, "reduce_chips_wait_" + tag)
            for n, land, part in zip(names, lands, parts):
                done[n] = (land, part, self.chip_idx)
        return all_small, done


def kernel(x, norm_mix_g, w_in, conv_w, conv_b, dt_bias, a_log, d_skip, ssm_norm_g, v_norm_g, v_norm_b, w_spatial, b_spatial, b_gates, w_proj_a, w_proj_b, w_out, norm_mlp_g, w_mlp_up, w_mlp_down, norm_final_g, loss_target, m_norm_mix_g, m_w_in, m_conv_w, m_conv_b, m_dt_bias, m_a_log, m_d_skip, m_ssm_norm_g, m_v_norm_g, m_v_norm_b, m_w_spatial, m_b_spatial, m_b_gates, m_w_proj_a, m_w_proj_b, m_w_out, m_norm_mlp_g, m_w_mlp_up, m_w_mlp_down, m_norm_final_g, v_norm_mix_g, v_w_in, v_conv_w, v_conv_b, v_dt_bias, v_a_log, v_d_skip, v_ssm_norm_g, v_v_norm_g, v_v_norm_b, v_w_spatial, v_b_spatial, v_b_gates, v_w_proj_a, v_w_proj_b, v_w_out, v_norm_mlp_g, v_w_mlp_up, v_w_mlp_down, v_norm_final_g):
    given = dict(locals())
    names = ["norm_mix_g", "w_in", "conv_w", "conv_b", "dt_bias", "a_log", "d_skip", "ssm_norm_g", "v_norm_g", "v_norm_b",
             "w_spatial", "b_spatial", "b_gates", "w_proj_a", "w_proj_b", "w_out", "norm_mlp_g", "w_mlp_up", "w_mlp_down",
             "norm_final_g"]
    shapes = {n: given[n].shape for n in names}
    dev = 4 * lax.axis_index("x") + 2 * lax.axis_index("y") + lax.axis_index("c")

    shard2d = {"w_in": w_in[0].T, "w_proj_a": w_proj_a[0], "w_proj_b": w_proj_b[0], "w_out": w_out[0],
               "w_mlp_up": w_mlp_up[0], "w_mlp_down": w_mlp_down[0]}
    conv_shard = conv_w.reshape(CONV_WIDTH, -1)
    late_shards = [shard2d[n].astype(BF16) for n in _LATE]
    w_in_all, conv_all, *late_lands = _all_gather([shard2d["w_in"].astype(BF16), conv_shard], "gather_first",
                                                  own_only=late_shards)
    dt_dev, dt_row = divmod(COL_GATE, SHARD_ROWS)
    w_dt_t = jnp.pad(w_in_all[dt_dev, dt_row:dt_row + N_HEADS], ((0, DT_PAD - N_HEADS), (0, 0)))
    wts = {"w_main_t": _main_rows_of(w_in_all, "regroup_w_in"), "w_dt_t": w_dt_t, "conv_w": jnp.transpose(conv_all, (1, 0, 2)).reshape(CONV_WIDTH, -1)}
    small = {"norm_mix_g": norm_mix_g, "conv_b": conv_b, "dt_bias": dt_bias, "a_log": a_log, "d_skip": d_skip,
             "ssm_norm_g": ssm_norm_g, "v_norm_g": v_norm_g, "v_norm_b": v_norm_b, "w_spatial": w_spatial[0],
             "b_spatial": b_spatial[0], "b_gates": b_gates, "norm_mlp_g": norm_mlp_g,
             "norm_final_g": norm_final_g.reshape(1, -1)}

    exchange = _Exchange(late_shards, late_lands)
    grad_x, g_mix = _local_step(x[0], loss_target[0], wts, small, exchange)

    out = {}
    all_small, large = exchange.finish(grad_x)
    for n, (slots, own, own_slot) in large.items():
        moments = [given["m_" + n][0], given["v_" + n][0]]
        if n == "w_in":
            moments = [mom.T for mom in moments]
        res = _adamw(slots, shard2d[n], *moments, "adamw_" + n, own=own, own_slot=own_slot)
        out[n] = [(r.T if n == "w_in" else r).reshape(shapes[n]) for r in res]

    last_small = _exchange_small({"norm_mix_g": g_mix}, _LAST_SMALL, "exchange_last")
    small["w_spatial"] = small["w_spatial"].reshape(GROUPS * CHUNK, CHUNK)
    params = {n: (w2d, given["m_" + n].reshape(w2d.shape), given["v_" + n].reshape(w2d.shape)) for n, w2d in small.items()}
    updated, (g_conv_full, loss_all) = _adamw_small(all_small, last_small, params, [(CONV_WIDTH, CONV_DIM), (1, LANES)],
                                                    "adamw_small")
    for n, res in updated.items():
        out[n] = [r.reshape(shapes[n]) for r in res]
    width = shapes["conv_w"][-1]
    g_conv = lax.dynamic_slice(g_conv_full, (0, dev * width), (CONV_WIDTH, width))
    res = _adamw(g_conv[None], conv_shard, m_conv_w.reshape(CONV_WIDTH, -1), v_conv_w.reshape(CONV_WIDTH, -1), "adamw_conv_w")
    out["conv_w"] = [r.reshape(shapes["conv_w"]) for r in res]

    loss = loss_all[0, 0]
    return (loss, grad_x[None], *[out[n][0] for n in names], *[out[n][1] for n in names],
            *[out[n][2] for n in names], *[out[n][3] for n in names])
```

```python
import functools
import math

import jax
import jax.numpy as jnp
from jax import lax
from jax.experimental import pallas as pl
from jax.experimental.pallas import tpu as pltpu

F32 = jnp.float32
BF16 = jnp.bfloat16
MESH = pl.DeviceIdType.MESH

D_MODEL = 1024
NORM_EPS = 1e-6
CHUNK = 128
GROUPS = 8
D_INNER = 2048
HEAD_DIM = 64
N_HEADS = 32
D_STATE = 128
CONV_WIDTH = 4
CONV_DIM = 4096
D_FF = 4096
GROUP_W = D_INNER // GROUPS
N_DEV = 8
N_CHIP = 4

ADAM_LR = 0.001
ADAM_B1 = 0.9
ADAM_B2 = 0.999
ADAM_EPS = 1e-08
ADAM_WD = 0.01
ADAM_STEP = 10

MAIN_W = 2 * D_MODEL + D_INNER + CONV_DIM + 2 * D_MODEL
COL_Z = 2048
COL_XBC = 4096
COL_GATE = 8192
DT_PAD = 128

LANES = 128
SUBLANES = 8
VMEM_BYTES_V7X = 64 * 1024 * 1024
VMEM_BODY_TEMP = 24 * 1024 * 1024


def _vmem_limit(block_bytes):
    return int(min(2 * block_bytes + VMEM_BODY_TEMP, VMEM_BYTES_V7X - 8 * 1024 * 1024))


def _nbytes(shape, dtype):
    return math.prod(shape) * jnp.dtype(dtype).itemsize


_HBM = pl.BlockSpec(memory_space=pl.ANY)


def _params(sem, block_bytes):
    return pltpu.CompilerParams(dimension_semantics=sem, vmem_limit_bytes=_vmem_limit(block_bytes))


def _sigmoid(x):
    return 1.0 / (1.0 + jnp.exp(-x))


def _softplus(x):
    e = jnp.exp(-jnp.abs(x))
    u = 1.0 + e
    log1p_e = jnp.where(u == 1.0, e, jnp.log(u) * (e / jnp.where(u == 1.0, 1.0, u - 1.0)))
    return jnp.maximum(x, 0.0) + log1p_e


_SQRT_HALF = 0.7071067811865476
_INV_SQRT_2PI = 0.3989422804014327


def _normal_cdf(x):
    return 0.5 * (1.0 + lax.erf(x * _SQRT_HALF))


def _gelu_grad(x, cdf):
    return cdf + x * jnp.exp(-0.5 * x * x) * _INV_SQRT_2PI


def _dot(a, b, dims):
    return lax.dot_general(a, b, (dims, ((), ())), preferred_element_type=F32)


_NN = ((1,), (0,))
_NT = ((1,), (1,))
_TN = ((0,), (0,))


def _split3(x):
    hi = x.astype(BF16)
    r1 = x - hi.astype(F32)
    mid = r1.astype(BF16)
    lo = (r1 - mid.astype(F32)).astype(BF16)
    return hi, mid, lo


def _dot_exact_rhs(x, e, dims):
    hi, mid, lo = _split3(x)
    return _dot(hi, e, dims) + _dot(mid, e, dims) + _dot(lo, e, dims)


def _dot_exact_lhs(e, x, dims):
    hi, mid, lo = _split3(x)
    return _dot(e, hi, dims) + _dot(e, mid, dims) + _dot(e, lo, dims)


def _tri(lower):
    r = lax.broadcasted_iota(jnp.int32, (CHUNK, CHUNK), 0)
    c = lax.broadcasted_iota(jnp.int32, (CHUNK, CHUNK), 1)
    return (r >= c) if lower else (r <= c)


def _matmul(a, b, *, mode, tm, tn, tk, out_dtypes, name, epilogue=None, extras=(), extra_specs=(), j_outer=False, deps=(),
            carry=False):
    if mode == "nn":
        (m, k), (_, n) = a.shape, b.shape
    elif mode == "nt":
        (m, k), (n, _) = a.shape, b.shape
    else:
        (k, m), (_, n) = a.shape, b.shape
    assert m % tm == 0 and n % tn == 0 and k % tk == 0, (name, m, n, k, tm, tn, tk)
    nk = k // tk
    n_extra, n_out = len(extras), len(out_dtypes)
    first_out = 2 + n_extra + len(deps)
    dims = {"nn": _NN, "nt": _NT, "tn": _TN}[mode]
    if epilogue is None:
        def epilogue(acc, ex, outs, first):
            outs[0][...] = acc.astype(outs[0].dtype)

    def body(*refs):
        a_ref, b_ref = refs[0], refs[1]
        ex_refs = refs[2:2 + n_extra]
        outs = refs[first_out:first_out + n_out]
        first_tile = pl.program_id(0) == 0
        p = _dot(a_ref[...], b_ref[...], dims)
        if nk == 1:
            epilogue(p, ex_refs, outs, first_tile)
            return
        acc_ref = refs[first_out + n_out]
        kk = pl.program_id(2)

        @pl.when(kk == 0)
        def _():
            acc_ref[...] = p

        @pl.when(kk > 0)
        def _():
            acc_ref[...] += p

        @pl.when(kk == nk - 1)
        def _():
            epilogue(acc_ref[...], ex_refs, outs, first_tile)

    if j_outer:
        grid = (n // tn, m // tm, nk)
        ij = lambda g0, g1: (g1, g0)
    else:
        grid = (m // tm, n // tn, nk)
        ij = lambda g0, g1: (g0, g1)

    def wrap(fn):
        return lambda g0, g1, kk: fn(*ij(g0, g1), kk)

    if mode == "nn":
        a_spec = pl.BlockSpec((tm, tk), wrap(lambda i, j, kk: (i, kk)))
        b_spec = pl.BlockSpec((tk, tn), wrap(lambda i, j, kk: (kk, j)))
        a_blk, b_blk = (tm, tk), (tk, tn)
    elif mode == "nt":
        a_spec = pl.BlockSpec((tm, tk), wrap(lambda i, j, kk: (i, kk)))
        b_spec = pl.BlockSpec((tn, tk), wrap(lambda i, j, kk: (j, kk)))
        a_blk, b_blk = (tm, tk), (tn, tk)
    else:
        a_spec = pl.BlockSpec((tk, tm), wrap(lambda i, j, kk: (kk, i)))
        b_spec = pl.BlockSpec((tk, tn), wrap(lambda i, j, kk: (kk, j)))
        a_blk, b_blk = (tk, tm), (tk, tn)
    ex_specs = [pl.BlockSpec(shape, wrap(lambda i, j, kk, f=f: f(i, j))) for shape, f in extra_specs]
    outs = [o if isinstance(o, tuple) else ((m, n), o, (tm, tn), lambda i, j: (i, j)) for o in out_dtypes]
    out_spec = [pl.BlockSpec(blk_shape, wrap(lambda i, j, kk, f=f: f(i, j))) for _, _, blk_shape, f in outs]
    out_shape = [jax.ShapeDtypeStruct(shape, dt) for shape, dt, _, _ in outs]
    blk = (_nbytes(a_blk, a.dtype) + _nbytes(b_blk, b.dtype) + sum(_nbytes(s, F32) for s, _ in extra_specs)
           + sum(_nbytes(blk_shape, dt) for _, dt, blk_shape, _ in outs) + _nbytes((tm, tn), F32))
    order = ("arbitrary",) * 3 if carry else ("parallel", "parallel", "arbitrary")
    res = pl.pallas_call(
        body, name=name, grid=grid,
        in_specs=[a_spec, b_spec] + ex_specs + [_HBM] * len(deps), out_specs=out_spec, out_shape=out_shape,
        scratch_shapes=[pltpu.VMEM((tm, tn), F32)] if nk > 1 else [],
        compiler_params=_params(order, blk),
    )(a, b, *extras, *deps)
    return res[0] if n_out == 1 else res


ROW_TILE = 256


def _row_spec(width, col_block=0, tile=ROW_TILE):
    return pl.BlockSpec((tile, width), lambda i, cb=col_block: (i, cb))


def _vec_spec(width, col_block=0):
    return pl.BlockSpec((1, width), lambda i, cb=col_block: (0, cb))


def _rms_fwd(x, g, w_t, name, deps=()):
    t = x.shape[0]
    n_small = w_t.shape[0]
    tile = 2 * ROW_TILE

    def body(x_ref, g_ref, w_ref, *rest):
        h_ref, small_ref = rest[-2:]
        xv = x_ref[...]
        r = lax.rsqrt(jnp.mean(xv * xv, axis=-1, keepdims=True) + NORM_EPS)
        h = (xv * r * g_ref[...]).astype(BF16)
        h_ref[...] = h
        small_ref[...] = _dot(h, w_ref[...], _NT)

    return pl.pallas_call(
        body, name=name, grid=(t // tile,),
        in_specs=[_row_spec(D_MODEL, 0, tile), _vec_spec(D_MODEL), pl.BlockSpec((n_small, D_MODEL), lambda i: (0, 0))]
        + [_HBM] * len(deps),
        out_specs=[_row_spec(D_MODEL, 0, tile), _row_spec(n_small, 0, tile)],
        out_shape=[jax.ShapeDtypeStruct((t, D_MODEL), BF16), jax.ShapeDtypeStruct((t, n_small), F32)],
        compiler_params=_params(("parallel",), 3 * _nbytes((tile, D_MODEL), F32)),
    )(x, g, w_t, *deps)


def _rms_scale(xv):
    r = lax.rsqrt(jnp.mean(xv * xv, axis=-1, keepdims=True) + NORM_EPS)
    return r, xv * r


def _rms_pullback(xv, g, dh):
    r, xh = _rms_scale(xv)
    dyg = dh * g
    return r * (dyg - xh * jnp.mean(dyg * xh, axis=-1, keepdims=True)), jnp.sum(dh * xh, axis=0, keepdims=True)


def _zero_when(first, *refs):
    @pl.when(first)
    def _():
        for ref in refs:
            ref[...] = jnp.zeros_like(ref)


def _residual_rms_epilogue(acc, ex, outs, first):
    x1 = acc + ex[0][...]
    outs[0][...] = x1
    _, xh = _rms_scale(x1)
    outs[1][...] = (xh * ex[1][...]).astype(BF16)


def _loss_epilogue(acc, ex, outs, first):
    dx_ref, dxb_ref, gg_ref, sq_ref, tot_ref = outs
    gv = ex[1][...]
    r, xh = _rms_scale(acc + ex[0][...])
    err = xh * gv - ex[2][...]
    dy = err * (1.0 / D_MODEL)
    dyg = dy * gv
    dx = r * (dyg - xh * jnp.mean(dyg * xh, axis=-1, keepdims=True))
    dx_ref[...] = dx
    dxb_ref[...] = dx.astype(BF16)

    _zero_when(first, gg_ref, sq_ref)
    gg_ref[...] += jnp.sum(dy * xh, axis=0, keepdims=True)
    sq_ref[...] += jnp.sum(err * err, axis=0, keepdims=True)
    tot_ref[...] = jnp.broadcast_to(jnp.sum(sq_ref[...], axis=1, keepdims=True) * (0.5 / D_MODEL), tot_ref.shape)


def _rms_bwd_epilogue(dh, ex, outs, first):
    dx, gg = _rms_pullback(ex[0][...], ex[1][...], dh)
    dx = dx + ex[2][...]
    outs[0][...] = dx
    if len(outs) == 3:
        outs[1][...] = dx.astype(BF16)

    _zero_when(first, outs[-1])
    outs[-1][...] += gg


def _merge_epilogue(acc, ex, outs, first):
    outs[0][...] = acc
    ga = _sigmoid(ex[1][...].astype(F32) + ex[3][...])
    gb = _sigmoid(ex[2][...].astype(F32) + ex[4][...])
    outs[1][...] = (ga * ex[0][...] + gb * acc).astype(BF16)


def _merge_bwd_epilogue(dm, ex, outs, first):
    dpa_ref, dpb_ref, dgl_ref, gb_ref = outs
    ga = _sigmoid(ex[2][...].astype(F32) + ex[4][...])
    gb = _sigmoid(ex[3][...].astype(F32) + ex[5][...])
    dpa_ref[...] = (dm * ga).astype(BF16)
    dpb_ref[...] = (dm * gb).astype(BF16)
    dla = dm * ex[0][...] * ga * (1.0 - ga)
    dlb = dm * ex[1][...] * gb * (1.0 - gb)
    dgl_ref[:, :D_MODEL] = dla.astype(BF16)
    dgl_ref[:, D_MODEL:] = dlb.astype(BF16)

    _zero_when(first, gb_ref)
    gb_ref[:, :D_MODEL] += jnp.sum(dla, axis=0, keepdims=True)
    gb_ref[:, D_MODEL:] += jnp.sum(dlb, axis=0, keepdims=True)


GMLP_TILE = 512
GMLP_NC = GMLP_TILE // CHUNK


def _gmlp_common(u_pre, v_pre, vg, vb):
    cdf_u, cdf_v = _normal_cdf(u_pre), _normal_cdf(v_pre)
    u = u_pre * cdf_u
    v = v_pre * cdf_v
    mu = jnp.mean(v, axis=-1, keepdims=True)
    vc = v - mu
    rstd = lax.rsqrt(jnp.mean(vc * vc, axis=-1, keepdims=True) + NORM_EPS)
    vh = vc * rstd
    vn = vh * vg + vb
    return u, vh, vn, rstd, cdf_u, cdf_v


def _chunks_to_lanes(x, g):
    return jnp.concatenate([x[c * CHUNK:(c + 1) * CHUNK, g * CHUNK:(g + 1) * CHUNK] for c in range(GMLP_NC)], axis=1)


def _gmlp_fwd(proj, vg, vb, wsp, bsp_t, name):
    t = proj.shape[0]

    def body(u_ref, v_ref, vg_ref, vb_ref, w_ref, b_ref, ya_ref):
        u, _, vn, _, _, _ = _gmlp_common(u_ref[...].astype(F32), v_ref[...].astype(F32), vg_ref[...], vb_ref[...])
        mask = _tri(True)
        bt = b_ref[...]
        for g in range(GROUPS):
            w = jnp.where(mask, w_ref[g], 0.0).astype(BF16)
            vcat = _chunks_to_lanes(vn, g).astype(BF16)
            s = _dot(w, vcat, _NN) + bt[:, g:g + 1]
            for c in range(GMLP_NC):
                rows, cols = slice(c * CHUNK, (c + 1) * CHUNK), slice(g * CHUNK, (g + 1) * CHUNK)
                ya_ref[rows, cols] = (u[rows, cols] * s[:, c * CHUNK:(c + 1) * CHUNK]).astype(BF16)

    return pl.pallas_call(
        body, name=name, grid=(t // GMLP_TILE,),
        in_specs=[_row_spec(D_MODEL, 0, GMLP_TILE), _row_spec(D_MODEL, 1, GMLP_TILE), _vec_spec(D_MODEL),
                  _vec_spec(D_MODEL), pl.BlockSpec((GROUPS, CHUNK, CHUNK), lambda i: (0, 0, 0)),
                  pl.BlockSpec((CHUNK, GROUPS), lambda i: (0, 0))],
        out_specs=_row_spec(D_MODEL, 0, GMLP_TILE),
        out_shape=jax.ShapeDtypeStruct((t, D_MODEL), BF16),
        compiler_params=_params(("parallel",), 3 * _nbytes((GMLP_TILE, D_MODEL), F32)),
    )(proj, proj, vg, vb, wsp, bsp_t)


def _gmlp_bwd(proj, dya, vg, vb, wsp, bsp_t, dproj, name):
    t = proj.shape[0]

    def body(u_ref, v_ref, dya_ref, vg_ref, vb_ref, w_ref, b_ref, dproj_in, duv_ref, gw_ref, gbt_ref, gvg_ref, gvb_ref,
             dvn_scr, du_scr):
        del dproj_in
        u_pre, v_pre = u_ref[...].astype(F32), v_ref[...].astype(F32)
        vgv = vg_ref[...]
        u, vh, vn, rstd, cdf_u, cdf_v = _gmlp_common(u_pre, v_pre, vgv, vb_ref[...])
        dya = dya_ref[...]
        mask = _tri(True)
        bt = b_ref[...]
        first = pl.program_id(0) == 0

        @pl.when(first)
        def _():
            gw_ref[...] = jnp.zeros_like(gw_ref)
            gbt_ref[...] = jnp.zeros_like(gbt_ref)
            gvg_ref[...] = jnp.zeros_like(gvg_ref)
            gvb_ref[...] = jnp.zeros_like(gvb_ref)

        lane = lax.broadcasted_iota(jnp.int32, (CHUNK, GROUPS), 1)
        gbt = jnp.zeros((CHUNK, GROUPS), F32)
        for g in range(GROUPS):
            w = jnp.where(mask, w_ref[g], 0.0).astype(BF16)
            vcat = _chunks_to_lanes(vn, g).astype(BF16)
            s = _dot(w, vcat, _NN) + bt[:, g:g + 1]
            ds = _chunks_to_lanes(dya * u, g)
            gbt = jnp.where(lane == g, jnp.sum(ds, axis=1, keepdims=True), gbt)
            dsb = ds.astype(BF16)
            gw_ref[g] += jnp.where(mask, _dot(dsb, vcat, _NT), 0.0)
            dv = _dot(w, dsb, _TN)
            for c in range(GMLP_NC):
                rows, cols = slice(c * CHUNK, (c + 1) * CHUNK), slice(g * CHUNK, (g + 1) * CHUNK)
                dvn_scr[rows, cols] = dv[:, c * CHUNK:(c + 1) * CHUNK]
                du_scr[rows, cols] = dya[rows, cols] * s[:, c * CHUNK:(c + 1) * CHUNK]
        gbt_ref[...] += gbt
        dvn = dvn_scr[...]
        gvg_ref[...] += jnp.sum(dvn * vh, axis=0, keepdims=True)
        gvb_ref[...] += jnp.sum(dvn, axis=0, keepdims=True)
        dvh = dvn * vgv
        dv = rstd * (dvh - jnp.mean(dvh, axis=-1, keepdims=True) - vh * jnp.mean(dvh * vh, axis=-1, keepdims=True))
        duv_ref[:, :D_MODEL] = (du_scr[...] * _gelu_grad(u_pre, cdf_u)).astype(BF16)
        duv_ref[:, D_MODEL:] = (dv * _gelu_grad(v_pre, cdf_v)).astype(BF16)

    return pl.pallas_call(
        body, name=name, grid=(t // GMLP_TILE,),
        in_specs=[_row_spec(D_MODEL, 0, GMLP_TILE), _row_spec(D_MODEL, 1, GMLP_TILE), _row_spec(D_MODEL, 0, GMLP_TILE),
                  _vec_spec(D_MODEL), _vec_spec(D_MODEL), pl.BlockSpec((GROUPS, CHUNK, CHUNK), lambda i: (0, 0, 0)),
                  pl.BlockSpec((CHUNK, GROUPS), lambda i: (0, 0)), pl.BlockSpec(memory_space=pl.ANY)],
        out_specs=[_row_spec(2 * D_MODEL, 0, GMLP_TILE), pl.BlockSpec((GROUPS, CHUNK, CHUNK), lambda i: (0, 0, 0)),
                   pl.BlockSpec((CHUNK, GROUPS), lambda i: (0, 0)), _vec_spec(D_MODEL), _vec_spec(D_MODEL)],
        out_shape=[jax.ShapeDtypeStruct(dproj.shape, BF16), jax.ShapeDtypeStruct((GROUPS, CHUNK, CHUNK), F32),
                   jax.ShapeDtypeStruct((CHUNK, GROUPS), F32), jax.ShapeDtypeStruct((1, D_MODEL), F32),
                   jax.ShapeDtypeStruct((1, D_MODEL), F32)],
        scratch_shapes=[pltpu.VMEM((GMLP_TILE, D_MODEL), F32), pltpu.VMEM((GMLP_TILE, D_MODEL), F32)],
        input_output_aliases={7: 0},
        compiler_params=_params(("arbitrary",), 6 * _nbytes((GMLP_TILE, D_MODEL), F32)),
    )(proj, proj, dya, vg, vb, wsp, bsp_t, dproj)


CONV_TILE = 1024
CONV_FWD_TILE = 2048
CONV_COLS = 1024
CONV_RB = 32
HALO = SUBLANES


def _conv_fwd(proj, cw, cb, name):
    t = proj.shape[0]
    nj = CONV_DIM // CONV_COLS
    xcb = COL_XBC // CONV_COLS
    before = 2 * HALO
    rb = CONV_FWD_TILE // before

    def body(x_ref, prev_ref, cw_ref, cb_ref, pre_ref, xc_ref):
        i = pl.program_id(1)
        cw_v = cw_ref[...]
        cb_v = cb_ref[...]
        for b in range(CONV_FWD_TILE // CONV_RB):
            if b == 0:
                prev = jnp.where(i > 0, prev_ref[...].astype(F32)[HALO:, :], 0.0)
                ext = jnp.concatenate([prev, x_ref[:CONV_RB, :].astype(F32)], axis=0)
            else:
                ext = x_ref[b * CONV_RB - before:(b + 1) * CONV_RB, :].astype(F32)[HALO:, :]
            pre = cb_v + cw_v[CONV_WIDTH - 1:CONV_WIDTH, :] * ext[HALO:, :]
            for k in range(CONV_WIDTH - 1):
                back = CONV_WIDTH - 1 - k
                pre = pre + cw_v[k:k + 1, :] * pltpu.roll(ext, back, 0)[HALO:, :]
            pre_ref[b * CONV_RB:(b + 1) * CONV_RB, :] = pre
            xc_ref[b * CONV_RB:(b + 1) * CONV_RB, :] = pre * _sigmoid(pre)

    tile = pl.BlockSpec((CONV_FWD_TILE, CONV_COLS), lambda j, i: (i, j))
    return pl.pallas_call(
        body, name=name, grid=(nj, t // CONV_FWD_TILE),
        in_specs=[pl.BlockSpec((CONV_FWD_TILE, CONV_COLS), lambda j, i: (i, xcb + j)),
                  pl.BlockSpec((before, CONV_COLS), lambda j, i: (jnp.maximum(i * rb - 1, 0), xcb + j)),
                  pl.BlockSpec((CONV_WIDTH, CONV_COLS), lambda j, i: (0, j)),
                  pl.BlockSpec((1, CONV_COLS), lambda j, i: (0, j))],
        out_specs=[tile, tile],
        out_shape=[jax.ShapeDtypeStruct((t, CONV_DIM), F32), jax.ShapeDtypeStruct((t, CONV_DIM), F32)],
        compiler_params=_params(("parallel", "parallel"), 4 * _nbytes((CONV_FWD_TILE, CONV_COLS), F32)),
    )(proj, proj, cw, cb)


def _fold_rows(v):
    out = v[:SUBLANES]
    for r in range(1, v.shape[0] // SUBLANES):
        out = out + v[r * SUBLANES:(r + 1) * SUBLANES]
    return out


def _conv_bwd(proj, pre, dxc, cw, dproj, name):
    t = proj.shape[0]
    nj = CONV_DIM // CONV_COLS
    ni = t // CONV_TILE
    xcb = COL_XBC // CONV_COLS
    rb = CONV_TILE // HALO
    last_rb = t // HALO - 1

    def body(x_ref, p_ref, pnext_ref, d_ref, dnext_ref, cw_ref, dproj_in, dx_ref, gw_ref, gb_ref):
        del dproj_in
        i = pl.program_id(1)
        cw_v = cw_ref[...]

        def dpre_of(p, d):
            sg = _sigmoid(p)
            return d * sg * (1.0 + p * (1.0 - sg))

        @pl.when(i == 0)
        def _():
            gw_ref[...] = jnp.zeros_like(gw_ref)
            gb_ref[...] = jnp.zeros_like(gb_ref)

        head = dpre_of(pnext_ref[...], jnp.where(i < ni - 1, dnext_ref[...], 0.0))
        gb_acc = jnp.zeros((SUBLANES, CONV_COLS), F32)
        gw_acc = [jnp.zeros((SUBLANES, CONV_COLS), F32) for _ in range(CONV_WIDTH)]
        for b in reversed(range(CONV_TILE // CONV_RB)):
            rows = slice(b * CONV_RB, (b + 1) * CONV_RB)
            cur = dpre_of(p_ref[rows, :], d_ref[rows, :])
            ext = jnp.concatenate([cur, head], axis=0)
            xv = x_ref[rows, :].astype(F32)
            dx = None
            for k in range(CONV_WIDTH):
                shift = CONV_WIDTH - 1 - k
                win = cur if shift == 0 else pltpu.roll(ext, CONV_RB + HALO - shift, 0)[:CONV_RB, :]
                term = cw_v[k:k + 1, :] * win
                dx = term if dx is None else dx + term
                gw_acc[k] = gw_acc[k] + _fold_rows(win * xv)
            dx_ref[rows, :] = dx.astype(BF16)
            gb_acc = gb_acc + _fold_rows(cur)
            head = cur[:HALO]
        gb_ref[...] += jnp.sum(gb_acc, axis=0, keepdims=True)
        for k in range(CONV_WIDTH):
            gw_ref[k:k + 1, :] += jnp.sum(gw_acc[k], axis=0, keepdims=True)

    tile = pl.BlockSpec((CONV_TILE, CONV_COLS), lambda j, i: (i, j))
    after = pl.BlockSpec((HALO, CONV_COLS), lambda j, i: (jnp.minimum((i + 1) * rb, last_rb), j))
    return pl.pallas_call(
        body, name=name, grid=(nj, ni),
        in_specs=[pl.BlockSpec((CONV_TILE, CONV_COLS), lambda j, i: (i, xcb + j)), tile, after, tile, after,
                  pl.BlockSpec((CONV_WIDTH, CONV_COLS), lambda j, i: (0, j)),
                  pl.BlockSpec(memory_space=pl.ANY)],
        out_specs=[pl.BlockSpec((CONV_TILE, CONV_COLS), lambda j, i: (i, xcb + j)),
                   pl.BlockSpec((CONV_WIDTH, CONV_COLS), lambda j, i: (0, j)),
                   pl.BlockSpec((1, CONV_COLS), lambda j, i: (0, j))],
        out_shape=[jax.ShapeDtypeStruct(dproj.shape, BF16), jax.ShapeDtypeStruct((CONV_WIDTH, CONV_DIM), F32),
                   jax.ShapeDtypeStruct((1, CONV_DIM), F32)],
        input_output_aliases={6: 0},
        compiler_params=_params(("parallel", "arbitrary"), 4 * _nbytes((CONV_TILE, CONV_COLS), F32)),
    )(proj, pre, pre, dxc, dxc, cw, dproj)


def _ssd_decays(dt_raw, dtb, alog, e_bf, tril_bf):
    dtv = _softplus(dt_raw + dtb)
    a = -jnp.exp(alog)
    cs = _dot_exact_lhs(tril_bf, dtv * a, _NN)
    cs_last = cs[CHUNK - 1:CHUNK, :]
    stack = jnp.concatenate([dtv, jnp.exp(cs), jnp.exp(cs_last - cs)], axis=0)
    full = _head_expand(stack, e_bf)
    return dtv, a, cs, full[:CHUNK], full[CHUNK:2 * CHUNK], full[2 * CHUNK:]


def _split2(x):
    hi = x.astype(BF16)
    return hi, (x - hi.astype(F32)).astype(BF16)


def _head_expand(x, e_bf):
    hi, mid = _split2(x)
    return _dot(hi, e_bf, _NN) + _dot(mid, e_bf, _NN)


def _head_sums(x, e_bf):
    hi, mid = _split2(x)
    return _dot(hi, e_bf, _NT) + _dot(mid, e_bf, _NT)


def _head_mats(cs, cs_t, cb, h, mask):
    seg = cs[:, h:h + 1] - cs_t[h:h + 1, :]
    lmat = jnp.exp(jnp.where(mask, seg, -jnp.inf))
    return lmat, cb * lmat


def _ssd_fwd(xc, proj, dt_raw, dtb, alog, dskip_full, ng, e_bf, name):
    t = xc.shape[0]
    nc = t // CHUNK
    zcb = COL_Z // D_INNER

    def body(xc_ref, z_ref, dt_ref, dtb_ref, alog_ref, dsk_ref, ng_ref, e_ref, y_ref, yb_ref, sprev_ref, s_scr):
        @pl.when(pl.program_id(0) == 0)
        def _():
            s_scr[...] = jnp.zeros_like(s_scr)

        mask = _tri(True)
        tril_bf = mask.astype(BF16)
        e_v = e_ref[...]
        _, _, cs, dt_full, ecs_full, decay_full = _ssd_decays(dt_ref[...], dtb_ref[...], alog_ref[...], e_v, tril_bf)
        cs_t = cs.T
        sprev_ref[0] = s_scr[...]
        for g in range(GROUPS):
            gc = slice(g * GROUP_W, (g + 1) * GROUP_W)
            xs = xc_ref[:, gc]
            xdt = xs * dt_full[:, gc]
            xdt_b = xdt.astype(BF16)
            xdec = (xdt * decay_full[:, gc]).astype(BF16)
            bg = xc_ref[:, D_INNER + g * D_STATE:D_INNER + (g + 1) * D_STATE].astype(BF16)
            cg = xc_ref[:, D_INNER + GROUPS * D_STATE + g * D_STATE:D_INNER + GROUPS * D_STATE + (g + 1) * D_STATE].astype(BF16)
            cb = _dot(cg, bg, _NT)
            s_prev = s_scr[:, gc]
            y_off = ecs_full[:, gc] * _dot(cg, s_prev.astype(BF16), _NN)
            s_scr[:, gc] = s_prev * ecs_full[CHUNK - 1:CHUNK, gc] + _dot(bg, xdec, _TN)
            parts = []
            for r in range(GROUP_W // HEAD_DIM):
                h = g * (GROUP_W // HEAD_DIM) + r
                _, m = _head_mats(cs, cs_t, cb, h, mask)
                parts.append(_dot(m.astype(BF16), xdt_b[:, r * HEAD_DIM:(r + 1) * HEAD_DIM], _NN))
            yg = jnp.concatenate(parts, axis=1) + y_off + dsk_ref[:, gc] * xs
            y_ref[:, gc] = yg
            zv = z_ref[:, gc].astype(F32)
            ygate = yg * (zv * _sigmoid(zv))
            rstd = lax.rsqrt(jnp.mean(ygate * ygate, axis=-1, keepdims=True) + NORM_EPS)
            yb_ref[:, gc] = (ygate * rstd * ng_ref[:, gc]).astype(BF16)

    vec = lambda w: pl.BlockSpec((1, w), lambda i: (0, 0))
    blk = _nbytes((CHUNK, CONV_DIM), F32) + 3 * _nbytes((CHUNK, D_INNER), F32) + _nbytes((D_STATE, D_INNER), F32)
    return pl.pallas_call(
        body, name=name, grid=(nc,),
        in_specs=[pl.BlockSpec((CHUNK, CONV_DIM), lambda i: (i, 0)), pl.BlockSpec((CHUNK, D_INNER), lambda i: (i, zcb)),
                  pl.BlockSpec((CHUNK, DT_PAD), lambda i: (i, 0)), vec(DT_PAD), vec(DT_PAD), vec(D_INNER), vec(D_INNER),
                  pl.BlockSpec((DT_PAD, D_INNER), lambda i: (0, 0))],
        out_specs=[pl.BlockSpec((CHUNK, D_INNER), lambda i: (i, 0)), pl.BlockSpec((CHUNK, D_INNER), lambda i: (i, 0)),
                   pl.BlockSpec((1, D_STATE, D_INNER), lambda i: (i, 0, 0))],
        out_shape=[jax.ShapeDtypeStruct((t, D_INNER), F32), jax.ShapeDtypeStruct((t, D_INNER), BF16),
                   jax.ShapeDtypeStruct((nc, D_STATE, D_INNER), F32)],
        scratch_shapes=[pltpu.VMEM((D_STATE, D_INNER), F32)],
        compiler_params=_params(("arbitrary",), blk),
    )(xc, proj, dt_raw, dtb, alog, dskip_full, ng, e_bf)


def _ssd_bwd(dyb, y, xc, proj, dt_raw, sprev, dtb, alog, dskip_full, ng, e_bf, h, dproj, name):
    t = xc.shape[0]
    nc = t // CHUNK
    zcb = COL_Z // D_INNER
    hpg = GROUP_W // HEAD_DIM
    rev = lambda i: nc - 1 - i

    def body(dyb_ref, y_ref, xc_ref, z_ref, dt_ref, sprev_ref, dtb_ref, alog_ref, dsk_ref, ng_ref, e_ref, h_ref, dproj_in,
             dz_ref, dxc_ref, ddt_ref, gng_ref, gdsk_ref, galog_ref, gdtb_ref, gwdt_ref, ds_scr, sums_scr):
        del dproj_in

        @pl.when(pl.program_id(0) == 0)
        def _():
            ds_scr[...] = jnp.zeros_like(ds_scr)
            gng_ref[...] = jnp.zeros_like(gng_ref)
            gdsk_ref[...] = jnp.zeros_like(gdsk_ref)
            galog_ref[...] = jnp.zeros_like(galog_ref)
            gdtb_ref[...] = jnp.zeros_like(gdtb_ref)
            gwdt_ref[...] = jnp.zeros_like(gwdt_ref)

        mask = _tri(True)
        tril_bf = mask.astype(BF16)
        triu_bf = _tri(False).astype(BF16)
        e_v = e_ref[...]
        dt_in = dt_ref[...] + dtb_ref[...]
        dtv, a, cs, dt_full, ecs_full, decay_full = _ssd_decays(dt_ref[...], dtb_ref[...], alog_ref[...], e_v, tril_bf)
        cs_t = cs.T

        lane_h = lax.broadcasted_iota(jnp.int32, (CHUNK, DT_PAD), 1)
        sub_h = lax.broadcasted_iota(jnp.int32, (DT_PAD, CHUNK), 0)
        dcs_rows = jnp.zeros((CHUNK, DT_PAD), F32)
        dcs_cols_t = jnp.zeros((DT_PAD, CHUNK), F32)
        last_cols, dsk_cols = [], []
        for g in range(GROUPS):
            gc = slice(g * GROUP_W, (g + 1) * GROUP_W)
            b_cols = slice(D_INNER + g * D_STATE, D_INNER + (g + 1) * D_STATE)
            c_cols = slice(D_INNER + GROUPS * D_STATE + g * D_STATE, D_INNER + GROUPS * D_STATE + (g + 1) * D_STATE)
            xs = xc_ref[:, gc]
            xdt = xs * dt_full[:, gc]
            xdt_b = xdt.astype(BF16)
            xdec = xdt * decay_full[:, gc]
            xdec_b = xdec.astype(BF16)
            zv = z_ref[:, gc].astype(F32)
            sg = _sigmoid(zv)
            gate = zv * sg
            yv = y_ref[:, gc]
            dybv = dyb_ref[:, gc]
            ygate = yv * gate
            rstd = lax.rsqrt(jnp.mean(ygate * ygate, axis=-1, keepdims=True) + NORM_EPS)
            yn = ygate * rstd
            gng_ref[:, gc] += jnp.sum(dybv * yn, axis=0, keepdims=True)
            dyn = dybv * ng_ref[:, gc]
            dyg = rstd * (dyn - yn * jnp.mean(dyn * yn, axis=-1, keepdims=True))
            dz_ref[:, gc] = (dyg * yv * sg * (1.0 + zv * (1.0 - sg))).astype(BF16)
            dy = dyg * gate
            dy_b = dy.astype(BF16)
            dyo = dy * ecs_full[:, gc]
            dyo_b = dyo.astype(BF16)
            dsk_cols.append(jnp.sum(dy * xs, axis=0, keepdims=True))

            bg = xc_ref[:, b_cols].astype(BF16)
            cg = xc_ref[:, c_cols].astype(BF16)
            s_prev = sprev_ref[0, :, gc]
            s_prev_b = s_prev.astype(BF16)
            dsg = ds_scr[:, gc]
            dsg_b = dsg.astype(BF16)
            cb = _dot(cg, bg, _NT)
            c_s = _dot(cg, s_prev_b, _NN)
            b_ds = _dot(bg, dsg_b, _NN)
            dcb = jnp.zeros((CHUNK, CHUNK), F32)
            parts = []
            for r in range(hpg):
                h = g * hpg + r
                hc = slice(r * HEAD_DIM, (r + 1) * HEAD_DIM)
                lmat, m = _head_mats(cs, cs_t, cb, h, mask)
                dm = _dot(dy_b[:, hc], xdt_b[:, hc], _NT)
                parts.append(_dot(m.astype(BF16), dy_b[:, hc], _TN))
                dcb = dcb + dm * lmat
                w = dm * m
                dcs_rows = jnp.where(lane_h == h, jnp.sum(w, axis=1, keepdims=True), dcs_rows)
                dcs_cols_t = jnp.where(sub_h == h, jnp.sum(w, axis=0, keepdims=True), dcs_cols_t)
            dxdt = jnp.concatenate(parts, axis=1) + decay_full[:, gc] * b_ds
            dcb_b = dcb.astype(BF16)
            dxc_ref[:, c_cols] = _dot(dcb_b, bg, _NN) + _dot(dyo_b, s_prev_b, _NT)
            dxc_ref[:, b_cols] = _dot(dcb_b, cg, _TN) + _dot(xdec_b, dsg_b, _NT)
            cdec = ecs_full[CHUNK - 1:CHUNK, gc]
            ds_scr[:, gc] = _dot(cg, dyo_b, _TN) + cdec * dsg
            dxc_ref[:, gc] = dxdt * dt_full[:, gc] + dsk_ref[:, gc] * dy
            dec_prod = xdec * b_ds
            sums_scr[:CHUNK, gc] = dyo * c_s - dec_prod
            sums_scr[CHUNK:, gc] = dxdt * xs
            last_cols.append(jnp.sum(dec_prod, axis=0, keepdims=True) + cdec * jnp.sum(dsg * s_prev, axis=0, keepdims=True))
        t_sums = _head_sums(sums_scr[...], e_v)
        tail = jnp.concatenate([jnp.concatenate(last_cols, axis=1), jnp.concatenate(dsk_cols, axis=1),
                                jnp.zeros((SUBLANES - 2, D_INNER), F32)], axis=0)
        t_tail = _dot_exact_rhs(tail, e_v, _NT)
        gdsk_ref[...] += t_tail[1:2, :]
        row = lax.broadcasted_iota(jnp.int32, (CHUNK, DT_PAD), 0)
        dcs = dcs_rows - dcs_cols_t.T + t_sums[:CHUNK] + jnp.where(row == CHUNK - 1, t_tail[0:1, :], 0.0)
        dda = _dot_exact_lhs(triu_bf, dcs, _NN)
        galog_ref[...] += jnp.sum(dda * dtv, axis=0, keepdims=True) * a
        ddt = dda * a + t_sums[CHUNK:]
        ddt_raw = jnp.where(lane_h < N_HEADS, ddt * _sigmoid(dt_in), 0.0)
        gdtb_ref[...] += jnp.sum(ddt_raw, axis=0, keepdims=True)
        ddt_b = ddt_raw.astype(BF16)
        ddt_ref[...] = ddt_b
        gwdt_ref[...] += _dot(ddt_b, h_ref[...], _TN)

    vec = lambda w: pl.BlockSpec((1, w), lambda i: (0, 0))
    blk = (2 * _nbytes((CHUNK, CONV_DIM), F32) + 4 * _nbytes((CHUNK, D_INNER), F32) + 4 * _nbytes((D_STATE, D_INNER), F32))
    return pl.pallas_call(
        body, name=name, grid=(nc,),
        in_specs=[pl.BlockSpec((CHUNK, D_INNER), lambda i: (rev(i), 0)), pl.BlockSpec((CHUNK, D_INNER), lambda i: (rev(i), 0)),
                  pl.BlockSpec((CHUNK, CONV_DIM), lambda i: (rev(i), 0)), pl.BlockSpec((CHUNK, D_INNER), lambda i: (rev(i), zcb)),
                  pl.BlockSpec((CHUNK, DT_PAD), lambda i: (rev(i), 0)), pl.BlockSpec((1, D_STATE, D_INNER), lambda i: (rev(i), 0, 0)),
                  vec(DT_PAD), vec(DT_PAD), vec(D_INNER), vec(D_INNER), pl.BlockSpec((DT_PAD, D_INNER), lambda i: (0, 0)),
                  pl.BlockSpec((CHUNK, D_MODEL), lambda i: (rev(i), 0)), pl.BlockSpec(memory_space=pl.ANY)],
        out_specs=[pl.BlockSpec((CHUNK, D_INNER), lambda i: (rev(i), zcb)), pl.BlockSpec((CHUNK, CONV_DIM), lambda i: (rev(i), 0)),
                   pl.BlockSpec((CHUNK, DT_PAD), lambda i: (rev(i), 0)), vec(D_INNER), vec(DT_PAD), vec(DT_PAD), vec(DT_PAD),
                   pl.BlockSpec((DT_PAD, D_MODEL), lambda i: (0, 0))],
        out_shape=[jax.ShapeDtypeStruct(dproj.shape, BF16), jax.ShapeDtypeStruct((t, CONV_DIM), F32),
                   jax.ShapeDtypeStruct((t, DT_PAD), BF16), jax.ShapeDtypeStruct((1, D_INNER), F32),
                   jax.ShapeDtypeStruct((1, DT_PAD), F32), jax.ShapeDtypeStruct((1, DT_PAD), F32),
                   jax.ShapeDtypeStruct((1, DT_PAD), F32), jax.ShapeDtypeStruct((DT_PAD, D_MODEL), F32)],
        scratch_shapes=[pltpu.VMEM((D_STATE, D_INNER), F32), pltpu.VMEM((2 * CHUNK, D_INNER), F32)],
        input_output_aliases={12: 0},
        compiler_params=_params(("arbitrary",), blk),
    )(dyb, y, xc, proj, dt_raw, sprev, dtb, alog, dskip_full, ng, e_bf, h, dproj)


def _mesh_pos():
    return lax.axis_index("x"), lax.axis_index("y"), lax.axis_index("c")


def _other_chips(x, y):
    return [(1 - x, y), (x, 1 - y), (1 - x, 1 - y)]


def _all_peers(x, y, c):
    peers = []
    for k in range(1, N_DEV):
        fx, fy, fc = (k >> 2) & 1, (k >> 1) & 1, k & 1
        px, py, pc = x + fx - 2 * x * fx, y + fy - 2 * y * fy, c + fc - 2 * c * fc
        peers.append(((px, py, pc), 4 * px + 2 * py + pc))
    return peers


def _all_gather(shards, name, own_only=()):
    n, n_own = len(shards), len(own_only)

    def body(*refs):
        ins, own_ins = refs[:n], refs[n:n + n_own]
        outs, own_outs = refs[n + n_own:2 * n + n_own], refs[2 * n + n_own:2 * (n + n_own)]
        send_sems, recv_sems, local_sems = refs[2 * (n + n_own):]
        x, y, c = _mesh_pos()
        me, sibling = (x, y, c), (x, y, 1 - c)
        chips = _other_chips(x, y)

        def slot(p):
            return 4 * p[0] + 2 * p[1] + p[2]

        def copy(a, k, block, to, src=None):
            dst = outs[a].at[slot(block)]
            return pltpu.make_async_remote_copy(
                src_ref=dst if src is None else src, dst_ref=dst, send_sem=send_sems.at[a * 7 + k],
                recv_sem=recv_sems.at[a * 7 + k], device_id=to, device_id_type=MESH)

        started = []
        own = []
        for a in range(n_own):
            mine = pltpu.make_async_copy(own_ins[a], own_outs[a].at[slot(me)], local_sems.at[n + a])
            mine.start()
            own.append(mine)
        for a in range(n):
            mine = pltpu.make_async_copy(ins[a], outs[a].at[slot(me)], local_sems.at[a])
            mine.start()
            own.append(mine)
            first = [copy(a, 0, me, sibling, src=ins[a])]
            first += [copy(a, 1 + j, me, (*chip, c), src=ins[a]) for j, chip in enumerate(chips)]
            for cp in first:
                cp.start()
            started += first
        for a in range(n):
            for j, chip in enumerate(chips):
                copy(a, 1 + j, (*chip, c), me).wait_recv()
                fwd = copy(a, 4 + j, (*chip, c), sibling)
                fwd.start()
                started.append(fwd)
        for a in range(n):
            copy(a, 0, sibling, me).wait_recv()
            for j, chip in enumerate(chips):
                copy(a, 4 + j, (*chip, 1 - c), me).wait_recv()
        for cp in started:
            cp.wait_send()
        for mine in own:
            mine.wait()

    return pl.pallas_call(
        body, name=name,
        in_specs=[_HBM] * (n + n_own), out_specs=[_HBM] * (n + n_own),
        out_shape=[jax.ShapeDtypeStruct((N_DEV,) + s.shape, s.dtype) for s in (*shards, *own_only)],
        scratch_shapes=[pltpu.SemaphoreType.DMA((7 * n,)), pltpu.SemaphoreType.DMA((7 * n,)),
                        pltpu.SemaphoreType.DMA((n + n_own,))],
    )(*shards, *own_only)


_SMALL_ROWS = (("norm_mix_g", 8), ("conv_b", 32), ("dt_bias", 1), ("a_log", 1), ("d_skip", 1), ("ssm_norm_g", 16),
               ("v_norm_g", 8), ("v_norm_b", 8), ("w_spatial", 1024), ("b_spatial", 8), ("b_gates", 16), ("norm_mlp_g", 8),
               ("norm_final_g", 8), ("conv_w", 128), ("loss", 1))
_LAST_SMALL = (("norm_mix_g", 8),)


def _packed_rows(table):
    return -(-sum(r for _, r in table) // SUBLANES) * SUBLANES


def _small_offsets(table=_SMALL_ROWS):
    offs, r = {}, 0
    for name, rows in table:
        offs[name] = r
        r += rows
    return offs


def _rows_from(src_ref, dst_ref, r0):
    k, w = src_ref.shape
    if w <= LANES:
        dst_ref[r0:r0 + k, 0:w] = src_ref[...]
        return
    per = w // LANES
    for i in range(k):
        for j in range(per):
            dst_ref[r0 + i * per + j:r0 + i * per + j + 1, :] = src_ref[i:i + 1, j * LANES:(j + 1) * LANES]


def _rows_to(src_ref, r0, dst_ref):
    k, w = dst_ref.shape
    if w <= LANES:
        dst_ref[...] = src_ref[r0:r0 + k, 0:w]
        return
    per = w // LANES
    for i in range(k):
        for j in range(per):
            dst_ref[i:i + 1, j * LANES:(j + 1) * LANES] = src_ref[r0 + i * per + j:r0 + i * per + j + 1, :]


def _pack_small(grads, slot_idx, name):
    names = [n for n, _ in _SMALL_ROWS if n in grads]
    offs = _small_offsets()
    rows = _packed_rows(_SMALL_ROWS)

    def body(slot_ref, *refs):
        del slot_ref
        ins, (packed_ref, land_ref) = refs[:len(names)], refs[len(names):]
        packed_ref[...] = jnp.zeros_like(packed_ref)
        for n, ref in zip(names, ins):
            _rows_from(ref, packed_ref, offs[n])
        land_ref[0] = packed_ref[...]

    whole = lambda shape: pl.BlockSpec(shape, lambda i, slot_ref: (0,) * len(shape))
    grid_spec = pltpu.PrefetchScalarGridSpec(
        num_scalar_prefetch=1, grid=(1,), in_specs=[whole(grads[n].shape) for n in names],
        out_specs=[whole((rows, LANES)), pl.BlockSpec((1, rows, LANES), lambda i, slot_ref: (slot_ref[0], 0, 0))])
    return pl.pallas_call(
        body, name=name, grid_spec=grid_spec,
        out_shape=[jax.ShapeDtypeStruct((rows, LANES), F32), jax.ShapeDtypeStruct((N_DEV, rows, LANES), F32)],
    )(slot_idx, *[grads[n] for n in names])


def _exchange_small(grads, table, name):
    names = [n for n, _ in table]
    offs = _small_offsets(table)
    n_in = len(names)
    packed_rows = _packed_rows(table)

    def body(*refs):
        ins, out_ref = refs[:n_in], refs[n_in]
        packed, send_sems, recv_sems, local_sem = refs[n_in + 1:]
        packed[...] = jnp.zeros_like(packed)
        for n, ref in zip(names, ins):
            _rows_from(ref, packed, offs[n])
        x, y, c = _mesh_pos()
        my_slot = 4 * x + 2 * y + c
        mine = pltpu.make_async_copy(packed, out_ref.at[my_slot], local_sem)
        mine.start()
        copies = []
        for k, (peer, peer_slot) in enumerate(_all_peers(x, y, c)):
            sems = dict(send_sem=send_sems.at[k], recv_sem=recv_sems.at[k], device_id=peer, device_id_type=MESH)
            send = pltpu.make_async_remote_copy(src_ref=packed, dst_ref=out_ref.at[my_slot], **sems)
            send.start()
            copies.append((send, pltpu.make_async_remote_copy(src_ref=packed, dst_ref=out_ref.at[peer_slot], **sems)))
        for send, recv in copies:
            send.wait_send()
            recv.wait_recv()
        mine.wait()

    return pl.pallas_call(
        body, name=name, in_specs=[pl.BlockSpec(memory_space=pltpu.VMEM)] * n_in, out_specs=_HBM,
        out_shape=jax.ShapeDtypeStruct((N_DEV, packed_rows, LANES), F32),
        scratch_shapes=[pltpu.VMEM((packed_rows, LANES), F32), pltpu.SemaphoreType.DMA((N_DEV - 1,)),
                        pltpu.SemaphoreType.DMA((N_DEV - 1,)), pltpu.SemaphoreType.DMA],
    )(*[grads[n] for n in names])


def _swap_with_sibling(grads, name):
    n = len(grads)

    def body(*refs):
        ins, outs = refs[:n], refs[n:2 * n]
        send_sems, recv_sems = refs[2 * n:]
        x, y, c = _mesh_pos()
        copies = []
        for a in range(n):
            for k in range(N_CHIP):
                cp = pltpu.make_async_remote_copy(
                    src_ref=ins[a].at[(1 - c) + 2 * k], dst_ref=outs[a].at[k], send_sem=send_sems.at[a * N_CHIP + k],
                    recv_sem=recv_sems.at[a * N_CHIP + k], device_id=(x, y, 1 - c), device_id_type=MESH)
                cp.start()
                copies.append(cp)
        for cp in copies:
            cp.wait()

    return pl.pallas_call(
        body, name=name, in_specs=[_HBM] * n, out_specs=[_HBM] * n,
        out_shape=[jax.ShapeDtypeStruct((N_CHIP,) + g.shape[1:], g.dtype) for g in grads],
        scratch_shapes=[pltpu.SemaphoreType.DMA((N_CHIP * n,)), pltpu.SemaphoreType.DMA((N_CHIP * n,))],
    )(*grads)


_SEM = pl.BlockSpec(memory_space=pltpu.SEMAPHORE)
_IN_HBM = pl.BlockSpec(memory_space=pltpu.HBM)
_EFFECT = pltpu.SideEffectType.DATAFLOW_SIDE_EFFECTING


def _in_hbm(a):
    return pltpu.with_memory_space_constraint(a, pltpu.HBM)


def _gather_copies(ins, lands, send_sems, recv_sems):
    x, y, c = _mesh_pos()
    my_slot = 4 * x + 2 * y + c
    pairs = []
    for a in range(len(ins)):
        for k, (peer, peer_slot) in enumerate(_all_peers(x, y, c)):
            sems = dict(send_sem=send_sems.at[a * (N_DEV - 1) + k], recv_sem=recv_sems.at[a * (N_DEV - 1) + k],
                        device_id=peer, device_id_type=MESH)
            pairs.append((pltpu.make_async_remote_copy(src_ref=ins[a], dst_ref=lands[a].at[my_slot], **sems),
                          pltpu.make_async_remote_copy(src_ref=ins[a], dst_ref=lands[a].at[peer_slot], **sems)))
    return pairs


def _scatter_copies(ins, lands, send_sems, recv_sems):
    x, y, c = _mesh_pos()
    my_chip = 2 * x + y
    pairs = []
    for a in range(len(ins)):
        for j, chip in enumerate(_other_chips(x, y)):
            there = 2 * chip[0] + chip[1]
            sems = dict(send_sem=send_sems.at[a * 3 + j], recv_sem=recv_sems.at[a * 3 + j],
                        device_id=(*chip, c), device_id_type=MESH)
            pairs.append((pltpu.make_async_remote_copy(src_ref=ins[a].at[there], dst_ref=lands[a].at[my_chip], **sems),
                          pltpu.make_async_remote_copy(src_ref=ins[a].at[my_chip], dst_ref=lands[a].at[there], **sems)))
    return pairs


def _split_start(srcs, lands, copies, per_array, name):
    n = len(srcs)

    def body(*refs):
        ins, land_refs = refs[:n], refs[n:2 * n]
        send_sems, recv_sems = refs[2 * n], refs[2 * n + 1]
        token = refs[-1]
        for send, _ in copies(ins, land_refs, send_sems, recv_sems):
            send.start()
        token[...] = jnp.zeros_like(token)

    outs = pl.pallas_call(
        body, name=name,
        out_shape=(pltpu.SemaphoreType.DMA((per_array * n,)), pltpu.SemaphoreType.DMA((per_array * n,)),
                   *[pltpu.HBM(s.shape, s.dtype) for s in srcs], *[pltpu.HBM(l.shape, l.dtype) for l in lands],
                   jax.ShapeDtypeStruct((SUBLANES, LANES), F32)),
        in_specs=[_IN_HBM] * (2 * n),
        out_specs=(_SEM, _SEM, *[_IN_HBM] * (2 * n), pl.BlockSpec(memory_space=pltpu.VMEM)),
        input_output_aliases={i: 2 + i for i in range(2 * n)},
        compiler_params=pltpu.CompilerParams(has_side_effects=_EFFECT),
    )(*[_in_hbm(s) for s in srcs], *[_in_hbm(l) for l in lands])
    return outs[0], outs[1], list(outs[2:2 + n]), list(outs[2 + n:2 + 2 * n]), outs[-1]


def _split_wait(started, copies, after, name):
    send_sems, recv_sems, srcs, lands, _ = started
    n = len(srcs)

    def body(*refs):
        ins, land_refs = refs[:n], refs[n:2 * n]
        for send, recv in copies(ins, land_refs, refs[2 * n], refs[2 * n + 1]):
            send.wait_send()
            recv.wait_recv()

    outs = pl.pallas_call(
        body, name=name,
        out_shape=(*[pltpu.HBM(s.shape, s.dtype) for s in srcs], *[pltpu.HBM(l.shape, l.dtype) for l in lands]),
        in_specs=[_IN_HBM] * (2 * n) + [_SEM, _SEM, _HBM],
        out_specs=[_IN_HBM] * (2 * n),
        input_output_aliases={i: i for i in range(2 * n)},
        compiler_params=pltpu.CompilerParams(has_side_effects=_EFFECT),
    )(*srcs, *lands, send_sems, recv_sems, after)
    return list(outs[:n]), list(outs[n:])


def _ew_block(rows, cols, slots):
    budget = 8 * 1024 * 1024
    br, bc = rows, cols
    while slots * br * bc * 4 > budget:
        if br % 2 == 0 and (br // 2) % (2 * SUBLANES) == 0:
            br //= 2
        elif bc % 2 == 0 and (bc // 2) % LANES == 0:
            bc //= 2
        else:
            break
    return br, bc


def _add_sibling(grads, recv, c_idx, name):
    _, rows, cols = grads.shape
    br, bc = _ew_block(rows, cols, 3)

    def body(c_ref, g_ref, r_ref, out_ref):
        del c_ref
        out_ref[...] = (g_ref[...].astype(F32) + r_ref[...].astype(F32)).astype(out_ref.dtype)

    grid_spec = pltpu.PrefetchScalarGridSpec(
        num_scalar_prefetch=1, grid=(N_CHIP, rows // br, cols // bc),
        in_specs=[pl.BlockSpec((1, br, bc), lambda k, i, j, c_ref: (c_ref[0] + 2 * k, i, j)),
                  pl.BlockSpec((1, br, bc), lambda k, i, j, c_ref: (k, i, j))],
        out_specs=pl.BlockSpec((1, br, bc), lambda k, i, j, c_ref: (k, i, j)))
    return pl.pallas_call(
        body, name=name, grid_spec=grid_spec, out_shape=jax.ShapeDtypeStruct((N_CHIP, rows, cols), grads.dtype),
        compiler_params=_params(("parallel", "parallel", "parallel"), 3 * _nbytes((br, bc), F32)),
    )(c_idx, grads, recv)


def _adam_math(g, w, m, v):
    m2 = ADAM_B1 * m + (1.0 - ADAM_B1) * g
    v2 = ADAM_B2 * v + (1.0 - ADAM_B2) * (g * g)
    m_hat = m2 * (1.0 / (1.0 - ADAM_B1 ** ADAM_STEP))
    v_hat = v2 * (1.0 / (1.0 - ADAM_B2 ** ADAM_STEP))
    return -ADAM_LR * (m_hat / (jnp.sqrt(v_hat) + ADAM_EPS) + ADAM_WD * w), m2, v2


def _adamw(slots, w, m, v, name, own=None, own_slot=None):
    ns, rows, cols = slots.shape
    br, bc = _ew_block(rows, cols, 2 * ns + 7)

    def update(g, w_ref, m_ref, v_ref, g_ref, d_ref, m2_ref, v2_ref):
        g_ref[...] = g
        d_ref[...], m2_ref[...], v2_ref[...] = _adam_math(g, w_ref[...], m_ref[...], v_ref[...])

    out_shape = [jax.ShapeDtypeStruct((rows, cols), F32)] * 4
    params = _params(("parallel", "parallel"), (2 * ns + 7) * _nbytes((br, bc), F32))
    grid = (rows // br, cols // bc)
    if own is None:
        def body(s_ref, *rest):
            g = s_ref[0].astype(F32)
            for k in range(1, ns):
                g = g + s_ref[k].astype(F32)
            update(g, *rest)

        blk = pl.BlockSpec((br, bc), lambda i, j: (i, j))
        return pl.pallas_call(
            body, name=name, grid=grid,
            in_specs=[pl.BlockSpec((ns, br, bc), lambda i, j: (0, i, j)), blk, blk, blk], out_specs=[blk] * 4,
            out_shape=out_shape, compiler_params=params,
        )(slots, w, m, v)

    def body_own(slot_ref, s_ref, o_ref, *rest):
        g = None
        for k in range(ns):
            term = jnp.where(slot_ref[0] == k, o_ref[k].astype(F32), s_ref[k].astype(F32))
            g = term if g is None else g + term
        update(g, *rest)

    blk = pl.BlockSpec((br, bc), lambda i, j, slot_ref: (i, j))
    stack = pl.BlockSpec((ns, br, bc), lambda i, j, slot_ref: (0, i, j))
    grid_spec = pltpu.PrefetchScalarGridSpec(num_scalar_prefetch=1, grid=grid, in_specs=[stack, stack, blk, blk, blk],
                                             out_specs=[blk] * 4)
    return pl.pallas_call(body_own, name=name, grid_spec=grid_spec, out_shape=out_shape, compiler_params=params,
                          )(own_slot, slots, own, w, m, v)


def _adamw_small(all_g, last_g, params, extra_shapes, name):
    names = [n for n, _ in _SMALL_ROWS if n in params]
    extras = [n for n, _ in _SMALL_ROWS if n not in params]
    offs = _small_offsets()
    n_p = len(names)

    def body(*refs):
        s_ref, last_ref = refs[0], refs[1]
        wmv = refs[2:2 + 3 * n_p]
        outs = refs[2 + 3 * n_p:2 + 7 * n_p]
        extra_refs = refs[2 + 7 * n_p:2 + 7 * n_p + len(extras)]
        summed = refs[-1]
        g, g_last = s_ref[0], last_ref[0]
        for k in range(1, N_DEV):
            g, g_last = g + s_ref[k], g_last + last_ref[k]
        summed[...] = g
        last_offs = _small_offsets(_LAST_SMALL)
        for n, rows in _LAST_SMALL:
            summed[offs[n]:offs[n] + rows, :] = g_last[last_offs[n]:last_offs[n] + rows, :]
        for i, n in enumerate(names):
            w_ref, m_ref, v_ref = wmv[3 * i:3 * i + 3]
            g_ref, d_ref, m2_ref, v2_ref = outs[4 * i:4 * i + 4]
            _rows_to(summed, offs[n], g_ref)
            d_ref[...], m2_ref[...], v2_ref[...] = _adam_math(g_ref[...], w_ref[...], m_ref[...], v_ref[...])
        for n, ref in zip(extras, extra_refs):
            _rows_to(summed, offs[n], ref)

    flat = [a for n in names for a in params[n]]
    out_shape = [jax.ShapeDtypeStruct(params[n][0].shape, F32) for n in names for _ in range(4)]
    out_shape += [jax.ShapeDtypeStruct(s, F32) for s in extra_shapes]
    vmem = pl.BlockSpec(memory_space=pltpu.VMEM)
    res = pl.pallas_call(
        body, name=name, in_specs=[vmem] * (2 + len(flat)), out_specs=[vmem] * len(out_shape), out_shape=out_shape,
        scratch_shapes=[pltpu.VMEM(all_g.shape[1:], F32)],
        compiler_params=pltpu.CompilerParams(vmem_limit_bytes=_vmem_limit(_nbytes(all_g.shape, F32))),
    )(all_g, last_g, *flat)
    return {n: res[4 * i:4 * i + 4] for i, n in enumerate(names)}, res[4 * n_p:]


def _mm_tiles(mode, m, n, k):
    tn = min(n, 1024)
    if mode == "tn":
        return min(m, 1024), tn, min(k, 4096)
    if k <= 1024:
        return min(m, 2048), tn, k
    if k <= 2048:
        return min(m, 1024), tn, k
    if k <= 4096:
        return min(m, 512), tn, k
    return min(m, 1024), tn, 2048


def _local_step(x, target, wts, small, exchange):
    t = x.shape[0]
    assert t % CONV_FWD_TILE == 0 and t % CONV_TILE == 0 and t % GMLP_TILE == 0 and t % (2 * ROW_TILE) == 0, t
    w_main_t, w_dt_t = wts["w_main_t"], wts["w_dt_t"]
    bsp_t = small["b_spatial"].T
    pad32 = lambda a: jnp.pad(a, ((0, 0), (0, DT_PAD - N_HEADS)))
    dtb, alog = pad32(small["dt_bias"]), pad32(small["a_log"])
    dskip_full = jnp.repeat(small["d_skip"], HEAD_DIM, axis=1)
    head_of_col = lax.broadcasted_iota(jnp.int32, (DT_PAD, D_INNER), 1) // HEAD_DIM
    e_bf = (head_of_col == lax.broadcasted_iota(jnp.int32, (DT_PAD, D_INNER), 0)).astype(BF16)

    def mm(a, b, mode, name, **kw):
        if mode == "nn":
            m, k, n = a.shape[0], a.shape[1], b.shape[1]
        elif mode == "nt":
            m, k, n = a.shape[0], a.shape[1], b.shape[0]
        else:
            m, k, n = a.shape[1], a.shape[0], b.shape[1]
        tm, tn, tk = _mm_tiles(mode, m, n, k)
        tm = min(tm, kw.pop("max_tm", tm))
        kw.setdefault("out_dtypes", (BF16,) if mode == "tn" else (F32,))
        if "extra_specs" in kw:
            kw["extra_specs"] = kw["extra_specs"](tm, tn)
        return _matmul(a, b, mode=mode, tm=tm, tn=tn, tk=tk, name=name, **kw)

    def out_tile(tm, tn):
        return (((tm, tn), lambda i, j: (i, j)),)

    def row_tiles(n_tiles, *vectors, gate_logits=False):
        def specs(tm, tn):
            out = [((tm, tn), lambda i, j: (i, j))] * n_tiles
            if gate_logits:
                out += [((tm, D_MODEL), lambda i, j, cb=COL_GATE // D_MODEL + half: (i, cb)) for half in range(2)]
            return tuple(out) + tuple(((1, w), lambda i, j, cb=cb: (0, cb)) for w, cb in vectors)
        return specs

    vec = lambda w: ((1, w), F32, (1, w), lambda i, j: (0, 0))
    fused_tm = 512

    h, dt_raw = _rms_fwd(x, small["norm_mix_g"], w_dt_t, "rms_mix", deps=exchange.begin())
    proj = mm(h, w_main_t, "nt", "proj_main", j_outer=True, out_dtypes=(BF16,))
    y_a = _gmlp_fwd(proj, small["v_norm_g"], small["v_norm_b"], small["w_spatial"], bsp_t, "gmlp_fwd")
    pre_conv, xc = _conv_fwd(proj, wts["conv_w"], small["conv_b"], "conv_fwd")
    y_ssd, y_b, sprev = _ssd_fwd(xc, proj, dt_raw, dtb, alog, dskip_full, small["ssm_norm_g"], e_bf, "ssd_fwd")
    wts = {**wts, **exchange.late_weights(y_b)}
    pa = mm(y_a, wts["w_proj_a"], "nn", "proj_a")
    pb, merged = mm(y_b, wts["w_proj_b"], "nn", "proj_b", epilogue=_merge_epilogue, out_dtypes=(F32, BF16), max_tm=fused_tm,
                    extras=(pa, proj, proj, small["b_gates"], small["b_gates"]),
                    extra_specs=row_tiles(1, (D_MODEL, 0), (D_MODEL, 1), gate_logits=True))
    x1, h2 = mm(merged, wts["w_out"], "nn", "out_proj", epilogue=_residual_rms_epilogue, out_dtypes=(F32, BF16),
                max_tm=2 * fused_tm, extras=(x, small["norm_mlp_g"]), extra_specs=row_tiles(1, (D_MODEL, 0)))

    def relu_sq(acc, ex, outs, first):
        r = jnp.maximum(acc, 0.0)
        outs[0][...] = (r * r).astype(BF16)

    act = mm(h2, wts["w_mlp_up"], "nn", "mlp_up", epilogue=relu_sq, out_dtypes=(BF16,), j_outer=True)
    dx2, dx2_b, g_final, _, loss = mm(
        act, wts["w_mlp_down"], "nn", "mlp_down", epilogue=_loss_epilogue, carry=True,
        out_dtypes=(F32, BF16, vec(D_MODEL), vec(D_MODEL), vec(LANES)),
        extras=(x1, small["norm_final_g"], target), extra_specs=lambda tm, tn: (
            ((tm, tn), lambda i, j: (i, j)), ((1, tn), lambda i, j: (0, 0)), ((tm, tn), lambda i, j: (i, j))))

    def relu_sq_bwd(acc, ex, outs, first):
        outs[0][...] = (acc * 2.0 * jnp.sqrt(ex[0][...].astype(F32))).astype(BF16)

    dup = mm(dx2_b, wts["w_mlp_down"], "nt", "d_act", epilogue=relu_sq_bwd, extras=(act,), extra_specs=out_tile,
             out_dtypes=(BF16,), j_outer=True)
    g_down = mm(act, dx2_b, "tn", "g_mlp_down")
    g_up = mm(h2, dup, "tn", "g_mlp_up")
    dx1, dx1_b, g_mlp = mm(
        dup, wts["w_mlp_up"], "nt", "d_h2", epilogue=_rms_bwd_epilogue, carry=True,
        out_dtypes=(F32, BF16, vec(D_MODEL)), extras=(x1, small["norm_mlp_g"], dx2), extra_specs=lambda tm, tn: (
            ((tm, tn), lambda i, j: (i, j)), ((1, tn), lambda i, j: (0, 0)), ((tm, tn), lambda i, j: (i, j))))

    g_out = mm(merged, dx1_b, "tn", "g_out")
    dpa, dpb, dproj, g_bgates = mm(
        dx1_b, wts["w_out"], "nt", "d_merged", epilogue=_merge_bwd_epilogue, carry=True, max_tm=fused_tm,
        out_dtypes=(BF16, BF16, ((t, MAIN_W), BF16, (fused_tm, 2 * D_MODEL), lambda i, j: (i, COL_GATE // (2 * D_MODEL))),
                    vec(2 * D_MODEL)),
        extras=(pa, pb, proj, proj, small["b_gates"], small["b_gates"]),
        extra_specs=row_tiles(2, (D_MODEL, 0), (D_MODEL, 1), gate_logits=True))
    g_pa = mm(y_a, dpa, "tn", "g_proj_a")
    g_pb = mm(y_b, dpb, "tn", "g_proj_b")
    started = exchange.reduce("late", {"w_mlp_down": g_down, "w_mlp_up": g_up, "w_out": g_out, "w_proj_a": g_pa,
                                       "w_proj_b": g_pb})
    dya = mm(dpa, wts["w_proj_a"], "nt", "d_ya", deps=started)
    dyb = mm(dpb, wts["w_proj_b"], "nt", "d_yb")

    dproj, g_wsp, g_bsp_t, g_vg, g_vb = _gmlp_bwd(proj, dya, small["v_norm_g"], small["v_norm_b"], small["w_spatial"],
                                                   bsp_t, dproj, "gmlp_bwd")
    dproj, dxc, ddt, g_ng, g_dskip, g_alog, g_dtb, g_dt_t = _ssd_bwd(dyb, y_ssd, xc, proj, dt_raw, sprev, dtb, alog, dskip_full,
                                                                     small["ssm_norm_g"], e_bf, h, dproj, "ssd_bwd")
    dproj, g_convw, g_convb = _conv_bwd(proj, pre_conv, dxc, wts["conv_w"], dproj, "conv_bwd")

    small_grads = {
        "conv_w": g_convw, "loss": loss,
        "conv_b": g_convb, "dt_bias": g_dtb, "a_log": g_alog, "d_skip": g_dskip, "ssm_norm_g": g_ng,
        "v_norm_g": g_vg, "v_norm_b": g_vb, "w_spatial": g_wsp.reshape(GROUPS * CHUNK, CHUNK), "b_spatial": g_bsp_t.T,
        "b_gates": g_bgates, "norm_mlp_g": g_mlp, "norm_final_g": g_final,
    }
    g_main_t = mm(dproj, h, "tn", "g_in_main", deps=exchange.small(small_grads))
    started = exchange.reduce("in", {"w_in": (g_main_t, g_dt_t.astype(BF16))})

    def input_grad(acc, ex, outs, first):
        x_ref, g_ref, res_ref, ddt_ref, wdt_ref = ex
        gg = jnp.zeros((1, D_MODEL), F32)
        for r in range(acc.shape[0] // ROW_TILE):
            rows = slice(r * ROW_TILE, (r + 1) * ROW_TILE)
            dh = acc[rows] + _dot(ddt_ref[rows, :], wdt_ref[...], _NN)
            dx, gg_r = _rms_pullback(x_ref[rows, :], g_ref[...], dh)
            outs[0][rows, :] = dx + res_ref[rows, :]
            gg = gg + gg_r

        _zero_when(first, outs[1])
        outs[1][...] += gg

    grad_x, g_mix = mm(
        dproj, w_main_t, "nn", "d_h", epilogue=input_grad, deps=started, carry=True,
        out_dtypes=(F32, vec(D_MODEL)), extras=(x, small["norm_mix_g"], dx1, ddt, w_dt_t), extra_specs=lambda tm, tn: (
            ((tm, tn), lambda i, j: (i, j)), ((1, tn), lambda i, j: (0, 0)), ((tm, tn), lambda i, j: (i, j)),
            ((tm, DT_PAD), lambda i, j: (i, 0)), ((DT_PAD, D_MODEL), lambda i, j: (0, 0))))

    return grad_x, g_mix


SHARD_ROWS = (MAIN_W + N_HEADS) // N_DEV
REGROUP_IN = 2048


def _main_rows_of(gathered, name):
    n_dev, shard, d = gathered.shape
    blk = 1024
    nb = MAIN_W // blk

    def first_feature(b):
        return b * blk + (N_HEADS if b * blk >= COL_GATE else 0)

    def body(a_ref, b_ref, out_ref):
        for b in range(nb):
            s0, r0 = divmod(first_feature(b), shard)
            n1 = min(shard - r0, blk)

            @pl.when(pl.program_id(0) == b)
            def _(r0=r0, n1=n1):
                out_ref[0:n1, :] = a_ref[0, r0:r0 + n1, :]
                if n1 < blk:
                    out_ref[n1:blk, :] = b_ref[0, 0:blk - n1, :]

    def slot(b):
        return (b * blk + jnp.where(b * blk >= COL_GATE, N_HEADS, 0)) // shard

    return pl.pallas_call(
        body, name=name, grid=(nb,),
        in_specs=[pl.BlockSpec((1, shard, d), lambda b: (slot(b), 0, 0)),
                  pl.BlockSpec((1, shard, d), lambda b: (jnp.minimum(slot(b) + 1, n_dev - 1), 0, 0))],
        out_specs=pl.BlockSpec((blk, d), lambda b: (b, 0)),
        out_shape=jax.ShapeDtypeStruct((MAIN_W, d), gathered.dtype),
        compiler_params=_params(("parallel",), 3 * _nbytes((shard, d), gathered.dtype)),
    )(gathered, gathered)


def _by_device_rows(g_main_t, g_dt_t, name):
    d = g_main_t.shape[1]
    n_blocks = MAIN_W // REGROUP_IN
    dt_dev, dt_row = divmod(COL_GATE, SHARD_ROWS)

    def main_start(s):
        return s * SHARD_ROWS - (N_HEADS if s > dt_dev else 0)

    def body(a_ref, b_ref, dt_ref, out_ref):
        for s in range(N_DEV):
            m0 = main_start(s)
            k0, off = divmod(m0, REGROUP_IN)
            pieces = []
            if s == dt_dev:
                pieces = [(0, dt_row, m0), (dt_row, N_HEADS, None), (dt_row + N_HEADS, SHARD_ROWS - dt_row - N_HEADS, m0 + dt_row)]
            else:
                pieces = [(0, SHARD_ROWS, m0)]

            @pl.when(pl.program_id(0) == s)
            def _(pieces=pieces, k0=k0):
                for dst, n, src in pieces:
                    if src is None:
                        out_ref[0, dst:dst + n, :] = dt_ref[0:n, :]
                        continue
                    lo = src - k0 * REGROUP_IN
                    n_a = max(0, min(n, REGROUP_IN - lo))
                    if n_a:
                        out_ref[0, dst:dst + n_a, :] = a_ref[lo:lo + n_a, :]
                    if n_a < n:
                        lo_b = max(lo - REGROUP_IN, 0)
                        out_ref[0, dst + n_a:dst + n, :] = b_ref[lo_b:lo_b + n - n_a, :]

    def first_block(s):
        return (s * SHARD_ROWS - jnp.where(s > dt_dev, N_HEADS, 0)) // REGROUP_IN

    return pl.pallas_call(
        body, name=name, grid=(N_DEV,),
        in_specs=[pl.BlockSpec((REGROUP_IN, d), lambda s: (first_block(s), 0)),
                  pl.BlockSpec((REGROUP_IN, d), lambda s: (jnp.minimum(first_block(s) + 1, n_blocks - 1), 0)),
                  pl.BlockSpec((DT_PAD, d), lambda s: (0, 0))],
        out_specs=pl.BlockSpec((1, SHARD_ROWS, d), lambda s: (s, 0, 0)),
        out_shape=jax.ShapeDtypeStruct((N_DEV, SHARD_ROWS, d), g_main_t.dtype),
        compiler_params=_params(("parallel",), 3 * _nbytes((REGROUP_IN, d), g_main_t.dtype)),
    )(g_main_t, g_main_t, g_dt_t)


_LATE = ["w_proj_a", "w_proj_b", "w_out", "w_mlp_up", "w_mlp_down"]
_BY_COLS = ("w_mlp_up",)


class _Exchange:
    def __init__(self, late_shards, late_lands):
        self.late_shards, self.late_lands = late_shards, late_lands
        self.c_idx = lax.axis_index("c").astype(jnp.int32).reshape(1)
        self.chip_idx = (2 * lax.axis_index("x") + lax.axis_index("y")).astype(jnp.int32).reshape(1)
        self.pending = []

    def begin(self):
        self.late = _split_start(self.late_shards, self.late_lands, _gather_copies, N_DEV - 1, "gather_late_start")
        return [self.late[-1]]

    def late_weights(self, after):
        _, lands = _split_wait(self.late, _gather_copies, after, "gather_late_wait")
        whole = {}
        for n, g in zip(_LATE, lands):
            whole[n] = jnp.transpose(g, (1, 0, 2)).reshape(g.shape[1], -1) if n in _BY_COLS else g.reshape(-1, g.shape[2])
        return whole

    def reduce(self, tag, grads):
        names = list(grads)
        by_dev = []
        for n in names:
            g = grads[n]
            if n == "w_in":
                by_dev.append(_by_device_rows(*g, "regroup_g_in"))
            elif n in _BY_COLS:
                by_dev.append(jnp.transpose(g.reshape(g.shape[0], N_DEV, -1), (1, 0, 2)))
            else:
                by_dev.append(g.reshape(N_DEV, -1, g.shape[1]))
        from_sibling = _swap_with_sibling(by_dev, "reduce_cores_" + tag)
        parts = [_add_sibling(g, r, self.c_idx, "add_cores_" + n) for n, g, r in zip(names, by_dev, from_sibling)]
        lands = [lax.empty(p.shape, p.dtype) for p in parts]
        started = _split_start(parts, lands, _scatter_copies, 3, "reduce_chips_start_" + tag)
        self.pending.append((tag, names, started))
        return [started[-1]]

    def small(self, grads):
        dev = 2 * self.chip_idx + self.c_idx
        packed, land = _pack_small(grads, dev, "pack_small")
        self.small_started = _split_start([packed], [land], _gather_copies, N_DEV - 1, "exchange_small_start")
        return [self.small_started[-1]]

    def finish(self, after):
        _, (all_small,) = _split_wait(self.small_started, _gather_copies, after, "exchange_small_wait")
        done = {}
        for tag, names, started in self.pending:
            parts, lands = _split_wait(started, _scatter_copies, after, "reduce_chips_wait_" + tag)
            for n, land, part in zip(names, lands, parts):
                done[n] = (land, part, self.chip_idx)
        return all_small, done


def kernel(x, norm_mix_g, w_in, conv_w, conv_b, dt_bias, a_log, d_skip, ssm_norm_g, v_norm_g, v_norm_b, w_spatial, b_spatial, b_gates, w_proj_a, w_proj_b, w_out, norm_mlp_g, w_mlp_up, w_mlp_down, norm_final_g, loss_target, m_norm_mix_g, m_w_in, m_conv_w, m_conv_b, m_dt_bias, m_a_log, m_d_skip, m_ssm_norm_g, m_v_norm_g, m_v_norm_b, m_w_spatial, m_b_spatial, m_b_gates, m_w_proj_a, m_w_proj_b, m_w_out, m_norm_mlp_g, m_w_mlp_up, m_w_mlp_down, m_norm_final_g, v_norm_mix_g, v_w_in, v_conv_w, v_conv_b, v_dt_bias, v_a_log, v_d_skip, v_ssm_norm_g, v_v_norm_g, v_v_norm_b, v_w_spatial, v_b_spatial, v_b_gates, v_w_proj_a, v_w_proj_b, v_w_out, v_norm_mlp_g, v_w_mlp_up, v_w_mlp_down, v_norm_final_g):
    given = dict(locals())
    names = ["norm_mix_g", "w_in", "conv_w", "conv_b", "dt_bias", "a_log", "d_skip", "ssm_norm_g", "v_norm_g", "v_norm_b",
             "w_spatial", "b_spatial", "b_gates", "w_proj_a", "w_proj_b", "w_out", "norm_mlp_g", "w_mlp_up", "w_mlp_down",
             "norm_final_g"]
    shapes = {n: given[n].shape for n in names}
    dev = 4 * lax.axis_index("x") + 2 * lax.axis_index("y") + lax.axis_index("c")

    shard2d = {"w_in": w_in[0].T, "w_proj_a": w_proj_a[0], "w_proj_b": w_proj_b[0], "w_out": w_out[0],
               "w_mlp_up": w_mlp_up[0], "w_mlp_down": w_mlp_down[0]}
    conv_shard = conv_w.reshape(CONV_WIDTH, -1)
    late_shards = [shard2d[n].astype(BF16) for n in _LATE]
    w_in_all, conv_all, *late_lands = _all_gather([shard2d["w_in"].astype(BF16), conv_shard], "gather_first",
                                                  own_only=late_shards)
    dt_dev, dt_row = divmod(COL_GATE, SHARD_ROWS)
    w_dt_t = jnp.pad(w_in_all[dt_dev, dt_row:dt_row + N_HEADS], ((0, DT_PAD - N_HEADS), (0, 0)))
    wts = {"w_main_t": _main_rows_of(w_in_all, "regroup_w_in"), "w_dt_t": w_dt_t, "conv_w": jnp.transpose(conv_all, (1, 0, 2)).reshape(CONV_WIDTH, -1)}
    small = {"norm_mix_g": norm_mix_g, "conv_b": conv_b, "dt_bias": dt_bias, "a_log": a_log, "d_skip": d_skip,
             "ssm_norm_g": ssm_norm_g, "v_norm_g": v_norm_g, "v_norm_b": v_norm_b, "w_spatial": w_spatial[0],
             "b_spatial": b_spatial[0], "b_gates": b_gates, "norm_mlp_g": norm_mlp_g,
             "norm_final_g": norm_final_g.reshape(1, -1)}

    exchange = _Exchange(late_shards, late_lands)
    grad_x, g_mix = _local_step(x[0], loss_target[0], wts, small, exchange)

    out = {}
    all_small, large = exchange.finish(grad_x)
    for n, (slots, own, own_slot) in large.items():
        moments = [given["m_" + n][0], given["v_" + n][0]]
        if n == "w_in":
            moments = [mom.T for mom in moments]
        res = _adamw(slots, shard2d[n], *moments, "adamw_" + n, own=own, own_slot=own_slot)
        out[n] = [(r.T if n == "w_in" else r).reshape(shapes[n]) for r in res]

    last_small = _exchange_small({"norm_mix_g": g_mix}, _LAST_SMALL, "exchange_last")
    small["w_spatial"] = small["w_spatial"].reshape(GROUPS * CHUNK, CHUNK)
    params = {n: (w2d, given["m_" + n].reshape(w2d.shape), given["v_" + n].reshape(w2d.shape)) for n, w2d in small.items()}
    updated, (g_conv_full, loss_all) = _adamw_small(all_small, last_small, params, [(CONV_WIDTH, CONV_DIM), (1, LANES)],
                                                    "adamw_small")
    for n, res in updated.items():
        out[n] = [r.reshape(shapes[n]) for r in res]
    width = shapes["conv_w"][-1]
    g_conv = lax.dynamic_slice(g_conv_full, (0, dev * width), (CONV_WIDTH, width))
    res = _adamw(g_conv[None], conv_shard, m_conv_w.reshape(CONV_WIDTH, -1), v_conv_w.reshape(CONV_WIDTH, -1), "adamw_conv_w")
    out["conv_w"] = [r.reshape(shapes["conv_w"]) for r in res]

    loss = loss_all[0, 0]
    return (loss, grad_x[None], *[out[n][0] for n in names], *[out[n][1] for n in names],
            *[out[n][2] for n in names], *[out[n][3] for n in names])
```

```python
import functools
import math

import jax
import jax.numpy as jnp
from jax import lax
from jax.experimental import pallas as pl
from jax.experimental.pallas import tpu as pltpu

F32 = jnp.float32
BF16 = jnp.bfloat16
MESH = pl.DeviceIdType.MESH

D_MODEL = 1024
NORM_EPS = 1e-6
CHUNK = 128
GROUPS = 8
D_INNER = 2048
HEAD_DIM = 64
N_HEADS = 32
D_STATE = 128
CONV_WIDTH = 4
CONV_DIM = 4096
D_FF = 4096
GROUP_W = D_INNER // GROUPS
N_DEV = 8
N_CHIP = 4

ADAM_LR = 0.001
ADAM_B1 = 0.9
ADAM_B2 = 0.999
ADAM_EPS = 1e-08
ADAM_WD = 0.01
ADAM_STEP = 10

MAIN_W = 2 * D_MODEL + D_INNER + CONV_DIM + 2 * D_MODEL
COL_Z = 2048
COL_XBC = 4096
COL_GATE = 8192
DT_PAD = 128

LANES = 128
SUBLANES = 8
VMEM_BYTES_V7X = 64 * 1024 * 1024
VMEM_BODY_TEMP = 24 * 1024 * 1024


def _vmem_limit(block_bytes):
    return int(min(2 * block_bytes + VMEM_BODY_TEMP, VMEM_BYTES_V7X - 8 * 1024 * 1024))


def _nbytes(shape, dtype):
    return math.prod(shape) * jnp.dtype(dtype).itemsize


_HBM = pl.BlockSpec(memory_space=pl.ANY)


def _params(sem, block_bytes):
    return pltpu.CompilerParams(dimension_semantics=sem, vmem_limit_bytes=_vmem_limit(block_bytes))


def _sigmoid(x):
    return 1.0 / (1.0 + jnp.exp(-x))


def _softplus(x):
    e = jnp.exp(-jnp.abs(x))
    u = 1.0 + e
    log1p_e = jnp.where(u == 1.0, e, jnp.log(u) * (e / jnp.where(u == 1.0, 1.0, u - 1.0)))
    return jnp.maximum(x, 0.0) + log1p_e


_SQRT_HALF = 0.7071067811865476
_INV_SQRT_2PI = 0.3989422804014327


def _normal_cdf(x):
    return 0.5 * (1.0 + lax.erf(x * _SQRT_HALF))


def _gelu_grad(x, cdf):
    return cdf + x * jnp.exp(-0.5 * x * x) * _INV_SQRT_2PI


def _dot(a, b, dims):
    return lax.dot_general(a, b, (dims, ((), ())), preferred_element_type=F32)


_NN = ((1,), (0,))
_NT = ((1,), (1,))
_TN = ((0,), (0,))


def _split3(x):
    hi = x.astype(BF16)
    r1 = x - hi.astype(F32)
    mid = r1.astype(BF16)
    lo = (r1 - mid.astype(F32)).astype(BF16)
    return hi, mid, lo


def _dot_exact_rhs(x, e, dims):
    hi, mid, lo = _split3(x)
    return _dot(hi, e, dims) + _dot(mid, e, dims) + _dot(lo, e, dims)


def _dot_exact_lhs(e, x, dims):
    hi, mid, lo = _split3(x)
    return _dot(e, hi, dims) + _dot(e, mid, dims) + _dot(e, lo, dims)


def _tri(lower):
    r = lax.broadcasted_iota(jnp.int32, (CHUNK, CHUNK), 0)
    c = lax.broadcasted_iota(jnp.int32, (CHUNK, CHUNK), 1)
    return (r >= c) if lower else (r <= c)


def _matmul(a, b, *, mode, tm, tn, tk, out_dtypes, name, epilogue=None, extras=(), extra_specs=(), deps=(),
            carry=False):
    if mode == "nn":
        (m, k), (_, n) = a.shape, b.shape
    elif mode == "nt":
        (m, k), (n, _) = a.shape, b.shape
    else:
        (k, m), (_, n) = a.shape, b.shape
    assert m % tm == 0 and n % tn == 0 and k % tk == 0, (name, m, n, k, tm, tn, tk)
    nk = k // tk
    n_extra, n_out = len(extras), len(out_dtypes)
    first_out = 2 + n_extra + len(deps)
    dims = {"nn": _NN, "nt": _NT, "tn": _TN}[mode]
    if epilogue is None:
        def epilogue(acc, ex, outs, first):
            outs[0][...] = acc.astype(outs[0].dtype)

    def body(*refs):
        a_ref, b_ref = refs[0], refs[1]
        ex_refs = refs[2:2 + n_extra]
        outs = refs[first_out:first_out + n_out]
        first_tile = pl.program_id(0) == 0
        p = _dot(a_ref[...], b_ref[...], dims)
        if nk == 1:
            epilogue(p, ex_refs, outs, first_tile)
            return
        acc_ref = refs[first_out + n_out]
        kk = pl.program_id(2)

        @pl.when(kk == 0)
        def _():
            acc_ref[...] = p

        @pl.when(kk > 0)
        def _():
            acc_ref[...] += p

        @pl.when(kk == nk - 1)
        def _():
            epilogue(acc_ref[...], ex_refs, outs, first_tile)

    grid = (m // tm, n // tn, nk)

    if mode == "nn":
        a_spec = pl.BlockSpec((tm, tk), (lambda i, j, kk: (i, kk)))
        b_spec = pl.BlockSpec((tk, tn), (lambda i, j, kk: (kk, j)))
        a_blk, b_blk = (tm, tk), (tk, tn)
    elif mode == "nt":
        a_spec = pl.BlockSpec((tm, tk), (lambda i, j, kk: (i, kk)))
        b_spec = pl.BlockSpec((tn, tk), (lambda i, j, kk: (j, kk)))
        a_blk, b_blk = (tm, tk), (tn, tk)
    else:
        a_spec = pl.BlockSpec((tk, tm), (lambda i, j, kk: (kk, i)))
        b_spec = pl.BlockSpec((tk, tn), (lambda i, j, kk: (kk, j)))
        a_blk, b_blk = (tk, tm), (tk, tn)
    ex_specs = [pl.BlockSpec(shape, (lambda i, j, kk, f=f: f(i, j))) for shape, f in extra_specs]
    outs = [o if isinstance(o, tuple) else ((m, n), o, (tm, tn), lambda i, j: (i, j)) for o in out_dtypes]
    out_spec = [pl.BlockSpec(blk_shape, (lambda i, j, kk, f=f: f(i, j))) for _, _, blk_shape, f in outs]
    out_shape = [jax.ShapeDtypeStruct(shape, dt) for shape, dt, _, _ in outs]
    blk = (_nbytes(a_blk, a.dtype) + _nbytes(b_blk, b.dtype) + sum(_nbytes(s, F32) for s, _ in extra_specs)
           + sum(_nbytes(blk_shape, dt) for _, dt, blk_shape, _ in outs) + _nbytes((tm, tn), F32))
    order = ("arbitrary",) * 3 if carry else ("parallel", "parallel", "arbitrary")
    res = pl.pallas_call(
        body, name=name, grid=grid,
        in_specs=[a_spec, b_spec] + ex_specs + [_HBM] * len(deps), out_specs=out_spec, out_shape=out_shape,
        scratch_shapes=[pltpu.VMEM((tm, tn), F32)] if nk > 1 else [],
        compiler_params=_params(order, blk),
    )(a, b, *extras, *deps)
    return res[0] if n_out == 1 else res


ROW_TILE = 256


def _row_spec(width, col_block=0, tile=ROW_TILE):
    return pl.BlockSpec((tile, width), lambda i, cb=col_block: (i, cb))


def _vec_spec(width, col_block=0):
    return pl.BlockSpec((1, width), lambda i, cb=col_block: (0, cb))


def _rms_fwd(x, g, w_t, name, deps=()):
    t = x.shape[0]
    n_small = w_t.shape[0]
    tile = 2 * ROW_TILE

    def body(x_ref, g_ref, w_ref, *rest):
        h_ref, small_ref = rest[-2:]
        xv = x_ref[...]
        r = lax.rsqrt(jnp.mean(xv * xv, axis=-1, keepdims=True) + NORM_EPS)
        h = (xv * r * g_ref[...]).astype(BF16)
        h_ref[...] = h
        small_ref[...] = _dot(h, w_ref[...], _NT)

    return pl.pallas_call(
        body, name=name, grid=(t // tile,),
        in_specs=[_row_spec(D_MODEL, 0, tile), _vec_spec(D_MODEL), pl.BlockSpec((n_small, D_MODEL), lambda i: (0, 0))]
        + [_HBM] * len(deps),
        out_specs=[_row_spec(D_MODEL, 0, tile), _row_spec(n_small, 0, tile)],
        out_shape=[jax.ShapeDtypeStruct((t, D_MODEL), BF16), jax.ShapeDtypeStruct((t, n_small), F32)],
        compiler_params=_params(("parallel",), 3 * _nbytes((tile, D_MODEL), F32)),
    )(x, g, w_t, *deps)


def _rms_scale(xv):
    r = lax.rsqrt(jnp.mean(xv * xv, axis=-1, keepdims=True) + NORM_EPS)
    return r, xv * r


def _rms_pullback(xv, g, dh):
    r, xh = _rms_scale(xv)
    dyg = dh * g
    return r * (dyg - xh * jnp.mean(dyg * xh, axis=-1, keepdims=True)), jnp.sum(dh * xh, axis=0, keepdims=True)


def _zero_when(first, *refs):
    @pl.when(first)
    def _():
        for ref in refs:
            ref[...] = jnp.zeros_like(ref)


def _residual_rms_epilogue(acc, ex, outs, first):
    x1 = acc + ex[0][...]
    outs[0][...] = x1
    _, xh = _rms_scale(x1)
    outs[1][...] = (xh * ex[1][...]).astype(BF16)


def _loss_epilogue(acc, ex, outs, first):
    dx_ref, dxb_ref, gg_ref, sq_ref, tot_ref = outs
    gv = ex[1][...]
    r, xh = _rms_scale(acc + ex[0][...])
    err = xh * gv - ex[2][...]
    dy = err * (1.0 / D_MODEL)
    dyg = dy * gv
    dx = r * (dyg - xh * jnp.mean(dyg * xh, axis=-1, keepdims=True))
    dx_ref[...] = dx
    dxb_ref[...] = dx.astype(BF16)

    _zero_when(first, gg_ref, sq_ref)
    gg_ref[...] += jnp.sum(dy * xh, axis=0, keepdims=True)
    sq_ref[...] += jnp.sum(err * err, axis=0, keepdims=True)
    tot_ref[...] = jnp.broadcast_to(jnp.sum(sq_ref[...], axis=1, keepdims=True) * (0.5 / D_MODEL), tot_ref.shape)


def _rms_bwd_epilogue(dh, ex, outs, first):
    dx, gg = _rms_pullback(ex[0][...], ex[1][...], dh)
    dx = dx + ex[2][...]
    outs[0][...] = dx
    if len(outs) == 3:
        outs[1][...] = dx.astype(BF16)

    _zero_when(first, outs[-1])
    outs[-1][...] += gg


def _merge_epilogue(acc, ex, outs, first):
    outs[0][...] = acc
    ga = _sigmoid(ex[1][...].astype(F32) + ex[3][...])
    gb = _sigmoid(ex[2][...].astype(F32) + ex[4][...])
    outs[1][...] = (ga * ex[0][...] + gb * acc).astype(BF16)


def _merge_bwd_epilogue(dm, ex, outs, first):
    dpa_ref, dpb_ref, dgl_ref, gb_ref = outs
    ga = _sigmoid(ex[2][...].astype(F32) + ex[4][...])
    gb = _sigmoid(ex[3][...].astype(F32) + ex[5][...])
    dpa_ref[...] = (dm * ga).astype(BF16)
    dpb_ref[...] = (dm * gb).astype(BF16)
    dla = dm * ex[0][...] * ga * (1.0 - ga)
    dlb = dm * ex[1][...] * gb * (1.0 - gb)
    dgl_ref[:, :D_MODEL] = dla.astype(BF16)
    dgl_ref[:, D_MODEL:] = dlb.astype(BF16)

    _zero_when(first, gb_ref)
    gb_ref[:, :D_MODEL] += jnp.sum(dla, axis=0, keepdims=True)
    gb_ref[:, D_MODEL:] += jnp.sum(dlb, axis=0, keepdims=True)


GMLP_TILE = 512
GMLP_NC = GMLP_TILE // CHUNK


def _gmlp_common(u_pre, v_pre, vg, vb):
    cdf_u, cdf_v = _normal_cdf(u_pre), _normal_cdf(v_pre)
    u = u_pre * cdf_u
    v = v_pre * cdf_v
    mu = jnp.mean(v, axis=-1, keepdims=True)
    vc = v - mu
    rstd = lax.rsqrt(jnp.mean(vc * vc, axis=-1, keepdims=True) + NORM_EPS)
    vh = vc * rstd
    vn = vh * vg + vb
    return u, vh, vn, rstd, cdf_u, cdf_v


def _chunks_to_lanes(x, g):
    return jnp.concatenate([x[c * CHUNK:(c + 1) * CHUNK, g * CHUNK:(g + 1) * CHUNK] for c in range(GMLP_NC)], axis=1)


def _gmlp_fwd(proj, vg, vb, wsp, bsp_t, name):
    t = proj.shape[0]

    def body(u_ref, v_ref, vg_ref, vb_ref, w_ref, b_ref, ya_ref):
        u, _, vn, _, _, _ = _gmlp_common(u_ref[...].astype(F32), v_ref[...].astype(F32), vg_ref[...], vb_ref[...])
        mask = _tri(True)
        bt = b_ref[...]
        for g in range(GROUPS):
            w = jnp.where(mask, w_ref[g], 0.0).astype(BF16)
            vcat = _chunks_to_lanes(vn, g).astype(BF16)
            s = _dot(w, vcat, _NN) + bt[:, g:g + 1]
            for c in range(GMLP_NC):
                rows, cols = slice(c * CHUNK, (c + 1) * CHUNK), slice(g * CHUNK, (g + 1) * CHUNK)
                ya_ref[rows, cols] = (u[rows, cols] * s[:, c * CHUNK:(c + 1) * CHUNK]).astype(BF16)

    return pl.pallas_call(
        body, name=name, grid=(t // GMLP_TILE,),
        in_specs=[_row_spec(D_MODEL, 0, GMLP_TILE), _row_spec(D_MODEL, 1, GMLP_TILE), _vec_spec(D_MODEL),
                  _vec_spec(D_MODEL), pl.BlockSpec((GROUPS, CHUNK, CHUNK), lambda i: (0, 0, 0)),
                  pl.BlockSpec((CHUNK, GROUPS), lambda i: (0, 0))],
        out_specs=_row_spec(D_MODEL, 0, GMLP_TILE),
        out_shape=jax.ShapeDtypeStruct((t, D_MODEL), BF16),
        compiler_params=_params(("parallel",), 3 * _nbytes((GMLP_TILE, D_MODEL), F32)),
    )(proj, proj, vg, vb, wsp, bsp_t)


def _gmlp_bwd(proj, dya, vg, vb, wsp, bsp_t, dproj, name):
    t = proj.shape[0]

    def body(u_ref, v_ref, dya_ref, vg_ref, vb_ref, w_ref, b_ref, dproj_in, duv_ref, gw_ref, gbt_ref, gvg_ref, gvb_ref,
             dvn_scr, du_scr):
        del dproj_in
        u_pre, v_pre = u_ref[...].astype(F32), v_ref[...].astype(F32)
        vgv = vg_ref[...]
        u, vh, vn, rstd, cdf_u, cdf_v = _gmlp_common(u_pre, v_pre, vgv, vb_ref[...])
        dya = dya_ref[...]
        mask = _tri(True)
        bt = b_ref[...]
        first = pl.program_id(0) == 0

        @pl.when(first)
        def _():
            gw_ref[...] = jnp.zeros_like(gw_ref)
            gbt_ref[...] = jnp.zeros_like(gbt_ref)
            gvg_ref[...] = jnp.zeros_like(gvg_ref)
            gvb_ref[...] = jnp.zeros_like(gvb_ref)

        lane = lax.broadcasted_iota(jnp.int32, (CHUNK, GROUPS), 1)
        gbt = jnp.zeros((CHUNK, GROUPS), F32)
        for g in range(GROUPS):
            w = jnp.where(mask, w_ref[g], 0.0).astype(BF16)
            vcat = _chunks_to_lanes(vn, g).astype(BF16)
            s = _dot(w, vcat, _NN) + bt[:, g:g + 1]
            ds = _chunks_to_lanes(dya * u, g)
            gbt = jnp.where(lane == g, jnp.sum(ds, axis=1, keepdims=True), gbt)
            dsb = ds.astype(BF16)
            gw_ref[g] += jnp.where(mask, _dot(dsb, vcat, _NT), 0.0)
            dv = _dot(w, dsb, _TN)
            for c in range(GMLP_NC):
                rows, cols = slice(c * CHUNK, (c + 1) * CHUNK), slice(g * CHUNK, (g + 1) * CHUNK)
                dvn_scr[rows, cols] = dv[:, c * CHUNK:(c + 1) * CHUNK]
                du_scr[rows, cols] = dya[rows, cols] * s[:, c * CHUNK:(c + 1) * CHUNK]
        gbt_ref[...] += gbt
        dvn = dvn_scr[...]
        gvg_ref[...] += jnp.sum(dvn * vh, axis=0, keepdims=True)
        gvb_ref[...] += jnp.sum(dvn, axis=0, keepdims=True)
        dvh = dvn * vgv
        dv = rstd * (dvh - jnp.mean(dvh, axis=-1, keepdims=True) - vh * jnp.mean(dvh * vh, axis=-1, keepdims=True))
        duv_ref[:, :D_MODEL] = (du_scr[...] * _gelu_grad(u_pre, cdf_u)).astype(BF16)
        duv_ref[:, D_MODEL:] = (dv * _gelu_grad(v_pre, cdf_v)).astype(BF16)

    return pl.pallas_call(
        body, name=name, grid=(t // GMLP_TILE,),
        in_specs=[_row_spec(D_MODEL, 0, GMLP_TILE), _row_spec(D_MODEL, 1, GMLP_TILE), _row_spec(D_MODEL, 0, GMLP_TILE),
                  _vec_spec(D_MODEL), _vec_spec(D_MODEL), pl.BlockSpec((GROUPS, CHUNK, CHUNK), lambda i: (0, 0, 0)),
                  pl.BlockSpec((CHUNK, GROUPS), lambda i: (0, 0)), pl.BlockSpec(memory_space=pl.ANY)],
        out_specs=[_row_spec(2 * D_MODEL, 0, GMLP_TILE), pl.BlockSpec((GROUPS, CHUNK, CHUNK), lambda i: (0, 0, 0)),
                   pl.BlockSpec((CHUNK, GROUPS), lambda i: (0, 0)), _vec_spec(D_MODEL), _vec_spec(D_MODEL)],
        out_shape=[jax.ShapeDtypeStruct(dproj.shape, BF16), jax.ShapeDtypeStruct((GROUPS, CHUNK, CHUNK), F32),
                   jax.ShapeDtypeStruct((CHUNK, GROUPS), F32), jax.ShapeDtypeStruct((1, D_MODEL), F32),
                   jax.ShapeDtypeStruct((1, D_MODEL), F32)],
        scratch_shapes=[pltpu.VMEM((GMLP_TILE, D_MODEL), F32), pltpu.VMEM((GMLP_TILE, D_MODEL), F32)],
        input_output_aliases={7: 0},
        compiler_params=_params(("arbitrary",), 6 * _nbytes((GMLP_TILE, D_MODEL), F32)),
    )(proj, proj, dya, vg, vb, wsp, bsp_t, dproj)


CONV_TILE = 1024
CONV_FWD_TILE = 2048
CONV_COLS = 1024
CONV_RB = 32
HALO = SUBLANES


def _conv_fwd(proj, cw, cb, name):
    t = proj.shape[0]
    nj = CONV_DIM // CONV_COLS
    xcb = COL_XBC // CONV_COLS
    before = 2 * HALO
    rb = CONV_FWD_TILE // before

    def body(x_ref, prev_ref, cw_ref, cb_ref, pre_ref, xc_ref):
        i = pl.program_id(1)
        cw_v = cw_ref[...]
        cb_v = cb_ref[...]
        for b in range(CONV_FWD_TILE // CONV_RB):
            if b == 0:
                prev = jnp.where(i > 0, prev_ref[...].astype(F32)[HALO:, :], 0.0)
                ext = jnp.concatenate([prev, x_ref[:CONV_RB, :].astype(F32)], axis=0)
            else:
                ext = x_ref[b * CONV_RB - before:(b + 1) * CONV_RB, :].astype(F32)[HALO:, :]
            pre = cb_v + cw_v[CONV_WIDTH - 1:CONV_WIDTH, :] * ext[HALO:, :]
            for k in range(CONV_WIDTH - 1):
                back = CONV_WIDTH - 1 - k
                pre = pre + cw_v[k:k + 1, :] * pltpu.roll(ext, back, 0)[HALO:, :]
            pre_ref[b * CONV_RB:(b + 1) * CONV_RB, :] = pre
            xc_ref[b * CONV_RB:(b + 1) * CONV_RB, :] = pre * _sigmoid(pre)

    tile = pl.BlockSpec((CONV_FWD_TILE, CONV_COLS), lambda j, i: (i, j))
    return pl.pallas_call(
        body, name=name, grid=(nj, t // CONV_FWD_TILE),
        in_specs=[pl.BlockSpec((CONV_FWD_TILE, CONV_COLS), lambda j, i: (i, xcb + j)),
                  pl.BlockSpec((before, CONV_COLS), lambda j, i: (jnp.maximum(i * rb - 1, 0), xcb + j)),
                  pl.BlockSpec((CONV_WIDTH, CONV_COLS), lambda j, i: (0, j)),
                  pl.BlockSpec((1, CONV_COLS), lambda j, i: (0, j))],
        out_specs=[tile, tile],
        out_shape=[jax.ShapeDtypeStruct((t, CONV_DIM), F32), jax.ShapeDtypeStruct((t, CONV_DIM), F32)],
        compiler_params=_params(("parallel", "parallel"), 4 * _nbytes((CONV_FWD_TILE, CONV_COLS), F32)),
    )(proj, proj, cw, cb)


def _fold_rows(v):
    out = v[:SUBLANES]
    for r in range(1, v.shape[0] // SUBLANES):
        out = out + v[r * SUBLANES:(r + 1) * SUBLANES]
    return out


def _conv_bwd(proj, pre, dxc, cw, dproj, name):
    t = proj.shape[0]
    nj = CONV_DIM // CONV_COLS
    ni = t // CONV_TILE
    xcb = COL_XBC // CONV_COLS
    rb = CONV_TILE // HALO
    last_rb = t // HALO - 1

    def body(x_ref, p_ref, pnext_ref, d_ref, dnext_ref, cw_ref, dproj_in, dx_ref, gw_ref, gb_ref):
        del dproj_in
        i = pl.program_id(1)
        cw_v = cw_ref[...]

        def dpre_of(p, d):
            sg = _sigmoid(p)
            return d * sg * (1.0 + p * (1.0 - sg))

        @pl.when(i == 0)
        def _():
            gw_ref[...] = jnp.zeros_like(gw_ref)
            gb_ref[...] = jnp.zeros_like(gb_ref)

        head = dpre_of(pnext_ref[...], jnp.where(i < ni - 1, dnext_ref[...], 0.0))
        gb_acc = jnp.zeros((SUBLANES, CONV_COLS), F32)
        gw_acc = [jnp.zeros((SUBLANES, CONV_COLS), F32) for _ in range(CONV_WIDTH)]
        for b in reversed(range(CONV_TILE // CONV_RB)):
            rows = slice(b * CONV_RB, (b + 1) * CONV_RB)
            cur = dpre_of(p_ref[rows, :], d_ref[rows, :])
            ext = jnp.concatenate([cur, head], axis=0)
            xv = x_ref[rows, :].astype(F32)
            dx = None
            for k in range(CONV_WIDTH):
                shift = CONV_WIDTH - 1 - k
                win = cur if shift == 0 else pltpu.roll(ext, CONV_RB + HALO - shift, 0)[:CONV_RB, :]
                term = cw_v[k:k + 1, :] * win
                dx = term if dx is None else dx + term
                gw_acc[k] = gw_acc[k] + _fold_rows(win * xv)
            dx_ref[rows, :] = dx.astype(BF16)
            gb_acc = gb_acc + _fold_rows(cur)
            head = cur[:HALO]
        gb_ref[...] += jnp.sum(gb_acc, axis=0, keepdims=True)
        for k in range(CONV_WIDTH):
            gw_ref[k:k + 1, :] += jnp.sum(gw_acc[k], axis=0, keepdims=True)

    tile = pl.BlockSpec((CONV_TILE, CONV_COLS), lambda j, i: (i, j))
    after = pl.BlockSpec((HALO, CONV_COLS), lambda j, i: (jnp.minimum((i + 1) * rb, last_rb), j))
    return pl.pallas_call(
        body, name=name, grid=(nj, ni),
        in_specs=[pl.BlockSpec((CONV_TILE, CONV_COLS), lambda j, i: (i, xcb + j)), tile, after, tile, after,
                  pl.BlockSpec((CONV_WIDTH, CONV_COLS), lambda j, i: (0, j)),
                  pl.BlockSpec(memory_space=pl.ANY)],
        out_specs=[pl.BlockSpec((CONV_TILE, CONV_COLS), lambda j, i: (i, xcb + j)),
                   pl.BlockSpec((CONV_WIDTH, CONV_COLS), lambda j, i: (0, j)),
                   pl.BlockSpec((1, CONV_COLS), lambda j, i: (0, j))],
        out_shape=[jax.ShapeDtypeStruct(dproj.shape, BF16), jax.ShapeDtypeStruct((CONV_WIDTH, CONV_DIM), F32),
                   jax.ShapeDtypeStruct((1, CONV_DIM), F32)],
        input_output_aliases={6: 0},
        compiler_params=_params(("parallel", "arbitrary"), 4 * _nbytes((CONV_TILE, CONV_COLS), F32)),
    )(proj, pre, pre, dxc, dxc, cw, dproj)


def _ssd_decays(dt_raw, dtb, alog, e_bf, tril_bf):
    dtv = _softplus(dt_raw + dtb)
    a = -jnp.exp(alog)
    cs = _dot_exact_lhs(tril_bf, dtv * a, _NN)
    cs_last = cs[CHUNK - 1:CHUNK, :]
    stack = jnp.concatenate([dtv, jnp.exp(cs), jnp.exp(cs_last - cs)], axis=0)
    full = _head_expand(stack, e_bf)
    return dtv, a, cs, full[:CHUNK], full[CHUNK:2 * CHUNK], full[2 * CHUNK:]


def _split2(x):
    hi = x.astype(BF16)
    return hi, (x - hi.astype(F32)).astype(BF16)


def _head_expand(x, e_bf):
    hi, mid = _split2(x)
    return _dot(hi, e_bf, _NN) + _dot(mid, e_bf, _NN)


def _head_sums(x, e_bf):
    hi, mid = _split2(x)
    return _dot(hi, e_bf, _NT) + _dot(mid, e_bf, _NT)


def _head_mats(cs, cs_t, cb, h, mask):
    seg = cs[:, h:h + 1] - cs_t[h:h + 1, :]
    lmat = jnp.exp(jnp.where(mask, seg, -jnp.inf))
    return lmat, cb * lmat


def _ssd_fwd(xc, proj, dt_raw, dtb, alog, dskip_full, ng, e_bf, name):
    t = xc.shape[0]
    nc = t // CHUNK
    zcb = COL_Z // D_INNER

    def body(xc_ref, z_ref, dt_ref, dtb_ref, alog_ref, dsk_ref, ng_ref, e_ref, y_ref, yb_ref, sprev_ref, s_scr):
        @pl.when(pl.program_id(0) == 0)
        def _():
            s_scr[...] = jnp.zeros_like(s_scr)

        mask = _tri(True)
        tril_bf = mask.astype(BF16)
        e_v = e_ref[...]
        _, _, cs, dt_full, ecs_full, decay_full = _ssd_decays(dt_ref[...], dtb_ref[...], alog_ref[...], e_v, tril_bf)
        cs_t = cs.T
        sprev_ref[0] = s_scr[...]
        for g in range(GROUPS):
            gc = slice(g * GROUP_W, (g + 1) * GROUP_W)
            xs = xc_ref[:, gc]
            xdt = xs * dt_full[:, gc]
            xdt_b = xdt.astype(BF16)
            xdec = (xdt * decay_full[:, gc]).astype(BF16)
            bg = xc_ref[:, D_INNER + g * D_STATE:D_INNER + (g + 1) * D_STATE].astype(BF16)
            cg = xc_ref[:, D_INNER + GROUPS * D_STATE + g * D_STATE:D_INNER + GROUPS * D_STATE + (g + 1) * D_STATE].astype(BF16)
            cb = _dot(cg, bg, _NT)
            s_prev = s_scr[:, gc]
            y_off = ecs_full[:, gc] * _dot(cg, s_prev.astype(BF16), _NN)
            s_scr[:, gc] = s_prev * ecs_full[CHUNK - 1:CHUNK, gc] + _dot(bg, xdec, _TN)
            parts = []
            for r in range(GROUP_W // HEAD_DIM):
                h = g * (GROUP_W // HEAD_DIM) + r
                _, m = _head_mats(cs, cs_t, cb, h, mask)
                parts.append(_dot(m.astype(BF16), xdt_b[:, r * HEAD_DIM:(r + 1) * HEAD_DIM], _NN))
            yg = jnp.concatenate(parts, axis=1) + y_off + dsk_ref[:, gc] * xs
            y_ref[:, gc] = yg
            zv = z_ref[:, gc].astype(F32)
            ygate = yg * (zv * _sigmoid(zv))
            rstd = lax.rsqrt(jnp.mean(ygate * ygate, axis=-1, keepdims=True) + NORM_EPS)
            yb_ref[:, gc] = (ygate * rstd * ng_ref[:, gc]).astype(BF16)

    vec = lambda w: pl.BlockSpec((1, w), lambda i: (0, 0))
    blk = _nbytes((CHUNK, CONV_DIM), F32) + 3 * _nbytes((CHUNK, D_INNER), F32) + _nbytes((D_STATE, D_INNER), F32)
    return pl.pallas_call(
        body, name=name, grid=(nc,),
        in_specs=[pl.BlockSpec((CHUNK, CONV_DIM), lambda i: (i, 0)), pl.BlockSpec((CHUNK, D_INNER), lambda i: (i, zcb)),
                  pl.BlockSpec((CHUNK, DT_PAD), lambda i: (i, 0)), vec(DT_PAD), vec(DT_PAD), vec(D_INNER), vec(D_INNER),
                  pl.BlockSpec((DT_PAD, D_INNER), lambda i: (0, 0))],
        out_specs=[pl.BlockSpec((CHUNK, D_INNER), lambda i: (i, 0)), pl.BlockSpec((CHUNK, D_INNER), lambda i: (i, 0)),
                   pl.BlockSpec((1, D_STATE, D_INNER), lambda i: (i, 0, 0))],
        out_shape=[jax.ShapeDtypeStruct((t, D_INNER), F32), jax.ShapeDtypeStruct((t, D_INNER), BF16),
                   jax.ShapeDtypeStruct((nc, D_STATE, D_INNER), F32)],
        scratch_shapes=[pltpu.VMEM((D_STATE, D_INNER), F32)],
        compiler_params=_params(("arbitrary",), blk),
    )(xc, proj, dt_raw, dtb, alog, dskip_full, ng, e_bf)


def _ssd_bwd(dyb, y, xc, proj, dt_raw, sprev, dtb, alog, dskip_full, ng, e_bf, h, dproj, name):
    t = xc.shape[0]
    nc = t // CHUNK
    zcb = COL_Z // D_INNER
    hpg = GROUP_W // HEAD_DIM
    rev = lambda i: nc - 1 - i

    def body(dyb_ref, y_ref, xc_ref, z_ref, dt_ref, sprev_ref, dtb_ref, alog_ref, dsk_ref, ng_ref, e_ref, h_ref, dproj_in,
             dz_ref, dxc_ref, ddt_ref, gng_ref, gdsk_ref, galog_ref, gdtb_ref, gwdt_ref, ds_scr, sums_scr):
        del dproj_in

        @pl.when(pl.program_id(0) == 0)
        def _():
            ds_scr[...] = jnp.zeros_like(ds_scr)
            gng_ref[...] = jnp.zeros_like(gng_ref)
            gdsk_ref[...] = jnp.zeros_like(gdsk_ref)
            galog_ref[...] = jnp.zeros_like(galog_ref)
            gdtb_ref[...] = jnp.zeros_like(gdtb_ref)
            gwdt_ref[...] = jnp.zeros_like(gwdt_ref)

        mask = _tri(True)
        tril_bf = mask.astype(BF16)
        triu_bf = _tri(False).astype(BF16)
        e_v = e_ref[...]
        dt_in = dt_ref[...] + dtb_ref[...]
        dtv, a, cs, dt_full, ecs_full, decay_full = _ssd_decays(dt_ref[...], dtb_ref[...], alog_ref[...], e_v, tril_bf)
        cs_t = cs.T

        lane_h = lax.broadcasted_iota(jnp.int32, (CHUNK, DT_PAD), 1)
        sub_h = lax.broadcasted_iota(jnp.int32, (DT_PAD, CHUNK), 0)
        dcs_rows = jnp.zeros((CHUNK, DT_PAD), F32)
        dcs_cols_t = jnp.zeros((DT_PAD, CHUNK), F32)
        last_cols, dsk_cols = [], []
        for g in range(GROUPS):
            gc = slice(g * GROUP_W, (g + 1) * GROUP_W)
            b_cols = slice(D_INNER + g * D_STATE, D_INNER + (g + 1) * D_STATE)
            c_cols = slice(D_INNER + GROUPS * D_STATE + g * D_STATE, D_INNER + GROUPS * D_STATE + (g + 1) * D_STATE)
            xs = xc_ref[:, gc]
            xdt = xs * dt_full[:, gc]
            xdt_b = xdt.astype(BF16)
            xdec = xdt * decay_full[:, gc]
            xdec_b = xdec.astype(BF16)
            zv = z_ref[:, gc].astype(F32)
            sg = _sigmoid(zv)
            gate = zv * sg
            yv = y_ref[:, gc]
            dybv = dyb_ref[:, gc]
            ygate = yv * gate
            rstd = lax.rsqrt(jnp.mean(ygate * ygate, axis=-1, keepdims=True) + NORM_EPS)
            yn = ygate * rstd
            gng_ref[:, gc] += jnp.sum(dybv * yn, axis=0, keepdims=True)
            dyn = dybv * ng_ref[:, gc]
            dyg = rstd * (dyn - yn * jnp.mean(dyn * yn, axis=-1, keepdims=True))
            dz_ref[:, gc] = (dyg * yv * sg * (1.0 + zv * (1.0 - sg))).astype(BF16)
            dy = dyg * gate
            dy_b = dy.astype(BF16)
            dyo = dy * ecs_full[:, gc]
            dyo_b = dyo.astype(BF16)
            dsk_cols.append(jnp.sum(dy * xs, axis=0, keepdims=True))

            bg = xc_ref[:, b_cols].astype(BF16)
            cg = xc_ref[:, c_cols].astype(BF16)
            s_prev = sprev_ref[0, :, gc]
            s_prev_b = s_prev.astype(BF16)
            dsg = ds_scr[:, gc]
            dsg_b = dsg.astype(BF16)
            cb = _dot(cg, bg, _NT)
            c_s = _dot(cg, s_prev_b, _NN)
            b_ds = _dot(bg, dsg_b, _NN)
            dcb = jnp.zeros((CHUNK, CHUNK), F32)
            parts = []
            for r in range(hpg):
                h = g * hpg + r
                hc = slice(r * HEAD_DIM, (r + 1) * HEAD_DIM)
                lmat, m = _head_mats(cs, cs_t, cb, h, mask)
                dm = _dot(dy_b[:, hc], xdt_b[:, hc], _NT)
                parts.append(_dot(m.astype(BF16), dy_b[:, hc], _TN))
                dcb = dcb + dm * lmat
                w = dm * m
                dcs_rows = jnp.where(lane_h == h, jnp.sum(w, axis=1, keepdims=True), dcs_rows)
                dcs_cols_t = jnp.where(sub_h == h, jnp.sum(w, axis=0, keepdims=True), dcs_cols_t)
            dxdt = jnp.concatenate(parts, axis=1) + decay_full[:, gc] * b_ds
            dcb_b = dcb.astype(BF16)
            dxc_ref[:, c_cols] = _dot(dcb_b, bg, _NN) + _dot(dyo_b, s_prev_b, _NT)
            dxc_ref[:, b_cols] = _dot(dcb_b, cg, _TN) + _dot(xdec_b, dsg_b, _NT)
            cdec = ecs_full[CHUNK - 1:CHUNK, gc]
            ds_scr[:, gc] = _dot(cg, dyo_b, _TN) + cdec * dsg
            dxc_ref[:, gc] = dxdt * dt_full[:, gc] + dsk_ref[:, gc] * dy
            dec_prod = xdec * b_ds
            sums_scr[:CHUNK, gc] = dyo * c_s - dec_prod
            sums_scr[CHUNK:, gc] = dxdt * xs
            last_cols.append(jnp.sum(dec_prod, axis=0, keepdims=True) + cdec * jnp.sum(dsg * s_prev, axis=0, keepdims=True))
        t_sums = _head_sums(sums_scr[...], e_v)
        tail = jnp.concatenate([jnp.concatenate(last_cols, axis=1), jnp.concatenate(dsk_cols, axis=1),
                                jnp.zeros((SUBLANES - 2, D_INNER), F32)], axis=0)
        t_tail = _dot_exact_rhs(tail, e_v, _NT)
        gdsk_ref[...] += t_tail[1:2, :]
        row = lax.broadcasted_iota(jnp.int32, (CHUNK, DT_PAD), 0)
        dcs = dcs_rows - dcs_cols_t.T + t_sums[:CHUNK] + jnp.where(row == CHUNK - 1, t_tail[0:1, :], 0.0)
        dda = _dot_exact_lhs(triu_bf, dcs, _NN)
        galog_ref[...] += jnp.sum(dda * dtv, axis=0, keepdims=True) * a
        ddt = dda * a + t_sums[CHUNK:]
        ddt_raw = jnp.where(lane_h < N_HEADS, ddt * _sigmoid(dt_in), 0.0)
        gdtb_ref[...] += jnp.sum(ddt_raw, axis=0, keepdims=True)
        ddt_b = ddt_raw.astype(BF16)
        ddt_ref[...] = ddt_b
        gwdt_ref[...] += _dot(ddt_b, h_ref[...], _TN)

    vec = lambda w: pl.BlockSpec((1, w), lambda i: (0, 0))
    blk = (2 * _nbytes((CHUNK, CONV_DIM), F32) + 4 * _nbytes((CHUNK, D_INNER), F32) + 4 * _nbytes((D_STATE, D_INNER), F32))
    return pl.pallas_call(
        body, name=name, grid=(nc,),
        in_specs=[pl.BlockSpec((CHUNK, D_INNER), lambda i: (rev(i), 0)), pl.BlockSpec((CHUNK, D_INNER), lambda i: (rev(i), 0)),
                  pl.BlockSpec((CHUNK, CONV_DIM), lambda i: (rev(i), 0)), pl.BlockSpec((CHUNK, D_INNER), lambda i: (rev(i), zcb)),
                  pl.BlockSpec((CHUNK, DT_PAD), lambda i: (rev(i), 0)), pl.BlockSpec((1, D_STATE, D_INNER), lambda i: (rev(i), 0, 0)),
                  vec(DT_PAD), vec(DT_PAD), vec(D_INNER), vec(D_INNER), pl.BlockSpec((DT_PAD, D_INNER), lambda i: (0, 0)),
                  pl.BlockSpec((CHUNK, D_MODEL), lambda i: (rev(i), 0)), pl.BlockSpec(memory_space=pl.ANY)],
        out_specs=[pl.BlockSpec((CHUNK, D_INNER), lambda i: (rev(i), zcb)), pl.BlockSpec((CHUNK, CONV_DIM), lambda i: (rev(i), 0)),
                   pl.BlockSpec((CHUNK, DT_PAD), lambda i: (rev(i), 0)), vec(D_INNER), vec(DT_PAD), vec(DT_PAD), vec(DT_PAD),
                   pl.BlockSpec((DT_PAD, D_MODEL), lambda i: (0, 0))],
        out_shape=[jax.ShapeDtypeStruct(dproj.shape, BF16), jax.ShapeDtypeStruct((t, CONV_DIM), F32),
                   jax.ShapeDtypeStruct((t, DT_PAD), BF16), jax.ShapeDtypeStruct((1, D_INNER), F32),
                   jax.ShapeDtypeStruct((1, DT_PAD), F32), jax.ShapeDtypeStruct((1, DT_PAD), F32),
                   jax.ShapeDtypeStruct((1, DT_PAD), F32), jax.ShapeDtypeStruct((DT_PAD, D_MODEL), F32)],
        scratch_shapes=[pltpu.VMEM((D_STATE, D_INNER), F32), pltpu.VMEM((2 * CHUNK, D_INNER), F32)],
        input_output_aliases={12: 0},
        compiler_params=_params(("arbitrary",), blk),
    )(dyb, y, xc, proj, dt_raw, sprev, dtb, alog, dskip_full, ng, e_bf, h, dproj)


def _mesh_pos():
    return lax.axis_index("x"), lax.axis_index("y"), lax.axis_index("c")


def _other_chips(x, y):
    return [(1 - x, y), (x, 1 - y), (1 - x, 1 - y)]


def _all_peers(x, y, c):
    peers = []
    for k in range(1, N_DEV):
        fx, fy, fc = (k >> 2) & 1, (k >> 1) & 1, k & 1
        px, py, pc = x + fx - 2 * x * fx, y + fy - 2 * y * fy, c + fc - 2 * c * fc
        peers.append(((px, py, pc), 4 * px + 2 * py + pc))
    return peers


def _all_gather(shards, name, own_only=()):
    n, n_own = len(shards), len(own_only)

    def body(*refs):
        ins, own_ins = refs[:n], refs[n:n + n_own]
        outs, own_outs = refs[n + n_own:2 * n + n_own], refs[2 * n + n_own:2 * (n + n_own)]
        send_sems, recv_sems, local_sems = refs[2 * (n + n_own):]
        x, y, c = _mesh_pos()
        me, sibling = (x, y, c), (x, y, 1 - c)
        chips = _other_chips(x, y)

        def slot(p):
            return 4 * p[0] + 2 * p[1] + p[2]

        def copy(a, k, block, to, src=None):
            dst = outs[a].at[slot(block)]
            return pltpu.make_async_remote_copy(
                src_ref=dst if src is None else src, dst_ref=dst, send_sem=send_sems.at[a * 7 + k],
                recv_sem=recv_sems.at[a * 7 + k], device_id=to, device_id_type=MESH)

        started = []
        own = []
        for a in range(n_own):
            mine = pltpu.make_async_copy(own_ins[a], own_outs[a].at[slot(me)], local_sems.at[n + a])
            mine.start()
            own.append(mine)
        for a in range(n):
            mine = pltpu.make_async_copy(ins[a], outs[a].at[slot(me)], local_sems.at[a])
            mine.start()
            own.append(mine)
            first = [copy(a, 0, me, sibling, src=ins[a])]
            first += [copy(a, 1 + j, me, (*chip, c), src=ins[a]) for j, chip in enumerate(chips)]
            for cp in first:
                cp.start()
            started += first
        for a in range(n):
            for j, chip in enumerate(chips):
                copy(a, 1 + j, (*chip, c), me).wait_recv()
                fwd = copy(a, 4 + j, (*chip, c), sibling)
                fwd.start()
                started.append(fwd)
        for a in range(n):
            copy(a, 0, sibling, me).wait_recv()
            for j, chip in enumerate(chips):
                copy(a, 4 + j, (*chip, 1 - c), me).wait_recv()
        for cp in started:
            cp.wait_send()
        for mine in own:
            mine.wait()

    return pl.pallas_call(
        body, name=name,
        in_specs=[_HBM] * (n + n_own), out_specs=[_HBM] * (n + n_own),
        out_shape=[jax.ShapeDtypeStruct((N_DEV,) + s.shape, s.dtype) for s in (*shards, *own_only)],
        scratch_shapes=[pltpu.SemaphoreType.DMA((7 * n,)), pltpu.SemaphoreType.DMA((7 * n,)),
                        pltpu.SemaphoreType.DMA((n + n_own,))],
    )(*shards, *own_only)


_SMALL_ROWS = (("norm_mix_g", 8), ("conv_b", 32), ("dt_bias", 1), ("a_log", 1), ("d_skip", 1), ("ssm_norm_g", 16),
               ("v_norm_g", 8), ("v_norm_b", 8), ("w_spatial", 1024), ("b_spatial", 8), ("b_gates", 16), ("norm_mlp_g", 8),
               ("norm_final_g", 8), ("conv_w", 128), ("loss", 1))
_LAST_SMALL = (("norm_mix_g", 8),)


def _packed_rows(table):
    return -(-sum(r for _, r in table) // SUBLANES) * SUBLANES


def _small_offsets(table=_SMALL_ROWS):
    offs, r = {}, 0
    for name, rows in table:
        offs[name] = r
        r += rows
    return offs


def _rows_from(src_ref, dst_ref, r0):
    k, w = src_ref.shape
    if w <= LANES:
        dst_ref[r0:r0 + k, 0:w] = src_ref[...]
        return
    per = w // LANES
    for i in range(k):
        for j in range(per):
            dst_ref[r0 + i * per + j:r0 + i * per + j + 1, :] = src_ref[i:i + 1, j * LANES:(j + 1) * LANES]


def _rows_to(src_ref, r0, dst_ref):
    k, w = dst_ref.shape
    if w <= LANES:
        dst_ref[...] = src_ref[r0:r0 + k, 0:w]
        return
    per = w // LANES
    for i in range(k):
        for j in range(per):
            dst_ref[i:i + 1, j * LANES:(j + 1) * LANES] = src_ref[r0 + i * per + j:r0 + i * per + j + 1, :]


def _pack_small(grads, slot_idx, name):
    names = [n for n, _ in _SMALL_ROWS if n in grads]
    offs = _small_offsets()
    rows = _packed_rows(_SMALL_ROWS)

    def body(slot_ref, *refs):
        del slot_ref
        ins, (packed_ref, land_ref) = refs[:len(names)], refs[len(names):]
        packed_ref[...] = jnp.zeros_like(packed_ref)
        for n, ref in zip(names, ins):
            _rows_from(ref, packed_ref, offs[n])
        land_ref[0] = packed_ref[...]

    whole = lambda shape: pl.BlockSpec(shape, lambda i, slot_ref: (0,) * len(shape))
    grid_spec = pltpu.PrefetchScalarGridSpec(
        num_scalar_prefetch=1, grid=(1,), in_specs=[whole(grads[n].shape) for n in names],
        out_specs=[whole((rows, LANES)), pl.BlockSpec((1, rows, LANES), lambda i, slot_ref: (slot_ref[0], 0, 0))])
    return pl.pallas_call(
        body, name=name, grid_spec=grid_spec,
        out_shape=[jax.ShapeDtypeStruct((rows, LANES), F32), jax.ShapeDtypeStruct((N_DEV, rows, LANES), F32)],
    )(slot_idx, *[grads[n] for n in names])


def _exchange_small(grads, table, name):
    names = [n for n, _ in table]
    offs = _small_offsets(table)
    n_in = len(names)
    packed_rows = _packed_rows(table)

    def body(*refs):
        ins, out_ref = refs[:n_in], refs[n_in]
        packed, send_sems, recv_sems, local_sem = refs[n_in + 1:]
        packed[...] = jnp.zeros_like(packed)
        for n, ref in zip(names, ins):
            _rows_from(ref, packed, offs[n])
        x, y, c = _mesh_pos()
        my_slot = 4 * x + 2 * y + c
        mine = pltpu.make_async_copy(packed, out_ref.at[my_slot], local_sem)
        mine.start()
        copies = []
        for k, (peer, peer_slot) in enumerate(_all_peers(x, y, c)):
            sems = dict(send_sem=send_sems.at[k], recv_sem=recv_sems.at[k], device_id=peer, device_id_type=MESH)
            send = pltpu.make_async_remote_copy(src_ref=packed, dst_ref=out_ref.at[my_slot], **sems)
            send.start()
            copies.append((send, pltpu.make_async_remote_copy(src_ref=packed, dst_ref=out_ref.at[peer_slot], **sems)))
        for send, recv in copies:
            send.wait_send()
            recv.wait_recv()
        mine.wait()

    return pl.pallas_call(
        body, name=name, in_specs=[pl.BlockSpec(memory_space=pltpu.VMEM)] * n_in, out_specs=_HBM,
        out_shape=jax.ShapeDtypeStruct((N_DEV, packed_rows, LANES), F32),
        scratch_shapes=[pltpu.VMEM((packed_rows, LANES), F32), pltpu.SemaphoreType.DMA((N_DEV - 1,)),
                        pltpu.SemaphoreType.DMA((N_DEV - 1,)), pltpu.SemaphoreType.DMA],
    )(*[grads[n] for n in names])


def _swap_with_sibling(grads, name):
    n = len(grads)

    def body(*refs):
        ins, outs = refs[:n], refs[n:2 * n]
        send_sems, recv_sems = refs[2 * n:]
        x, y, c = _mesh_pos()
        copies = []
        for a in range(n):
            for k in range(N_CHIP):
                cp = pltpu.make_async_remote_copy(
                    src_ref=ins[a].at[(1 - c) + 2 * k], dst_ref=outs[a].at[k], send_sem=send_sems.at[a * N_CHIP + k],
                    recv_sem=recv_sems.at[a * N_CHIP + k], device_id=(x, y, 1 - c), device_id_type=MESH)
                cp.start()
                copies.append(cp)
        for cp in copies:
            cp.wait()

    return pl.pallas_call(
        body, name=name, in_specs=[_HBM] * n, out_specs=[_HBM] * n,
        out_shape=[jax.ShapeDtypeStruct((N_CHIP,) + g.shape[1:], g.dtype) for g in grads],
        scratch_shapes=[pltpu.SemaphoreType.DMA((N_CHIP * n,)), pltpu.SemaphoreType.DMA((N_CHIP * n,))],
    )(*grads)


_SEM = pl.BlockSpec(memory_space=pltpu.SEMAPHORE)
_IN_HBM = pl.BlockSpec(memory_space=pltpu.HBM)
_EFFECT = pltpu.SideEffectType.DATAFLOW_SIDE_EFFECTING


def _in_hbm(a):
    return pltpu.with_memory_space_constraint(a, pltpu.HBM)


def _gather_copies(ins, lands, send_sems, recv_sems):
    x, y, c = _mesh_pos()
    my_slot = 4 * x + 2 * y + c
    pairs = []
    for a in range(len(ins)):
        for k, (peer, peer_slot) in enumerate(_all_peers(x, y, c)):
            sems = dict(send_sem=send_sems.at[a * (N_DEV - 1) + k], recv_sem=recv_sems.at[a * (N_DEV - 1) + k],
                        device_id=peer, device_id_type=MESH)
            pairs.append((pltpu.make_async_remote_copy(src_ref=ins[a], dst_ref=lands[a].at[my_slot], **sems),
                          pltpu.make_async_remote_copy(src_ref=ins[a], dst_ref=lands[a].at[peer_slot], **sems)))
    return pairs


def _scatter_copies(ins, lands, send_sems, recv_sems):
    x, y, c = _mesh_pos()
    my_chip = 2 * x + y
    pairs = []
    for a in range(len(ins)):
        for j, chip in enumerate(_other_chips(x, y)):
            there = 2 * chip[0] + chip[1]
            sems = dict(send_sem=send_sems.at[a * 3 + j], recv_sem=recv_sems.at[a * 3 + j],
                        device_id=(*chip, c), device_id_type=MESH)
            pairs.append((pltpu.make_async_remote_copy(src_ref=ins[a].at[there], dst_ref=lands[a].at[my_chip], **sems),
                          pltpu.make_async_remote_copy(src_ref=ins[a].at[my_chip], dst_ref=lands[a].at[there], **sems)))
    return pairs


def _split_start(srcs, lands, copies, per_array, name):
    n = len(srcs)

    def body(*refs):
        ins, land_refs = refs[:n], refs[n:2 * n]
        send_sems, recv_sems = refs[2 * n], refs[2 * n + 1]
        token = refs[-1]
        for send, _ in copies(ins, land_refs, send_sems, recv_sems):
            send.start()
        token[...] = jnp.zeros_like(token)

    outs = pl.pallas_call(
        body, name=name,
        out_shape=(pltpu.SemaphoreType.DMA((per_array * n,)), pltpu.SemaphoreType.DMA((per_array * n,)),
                   *[pltpu.HBM(s.shape, s.dtype) for s in srcs], *[pltpu.HBM(l.shape, l.dtype) for l in lands],
                   jax.ShapeDtypeStruct((SUBLANES, LANES), F32)),
        in_specs=[_IN_HBM] * (2 * n),
        out_specs=(_SEM, _SEM, *[_IN_HBM] * (2 * n), pl.BlockSpec(memory_space=pltpu.VMEM)),
        input_output_aliases={i: 2 + i for i in range(2 * n)},
        compiler_params=pltpu.CompilerParams(has_side_effects=_EFFECT),
    )(*[_in_hbm(s) for s in srcs], *[_in_hbm(l) for l in lands])
    return outs[0], outs[1], list(outs[2:2 + n]), list(outs[2 + n:2 + 2 * n]), outs[-1]


def _split_wait(started, copies, after, name):
    send_sems, recv_sems, srcs, lands, _ = started
    n = len(srcs)

    def body(*refs):
        ins, land_refs = refs[:n], refs[n:2 * n]
        for send, recv in copies(ins, land_refs, refs[2 * n], refs[2 * n + 1]):
            send.wait_send()
            recv.wait_recv()

    outs = pl.pallas_call(
        body, name=name,
        out_shape=(*[pltpu.HBM(s.shape, s.dtype) for s in srcs], *[pltpu.HBM(l.shape, l.dtype) for l in lands]),
        in_specs=[_IN_HBM] * (2 * n) + [_SEM, _SEM, _HBM],
        out_specs=[_IN_HBM] * (2 * n),
        input_output_aliases={i: i for i in range(2 * n)},
        compiler_params=pltpu.CompilerParams(has_side_effects=_EFFECT),
    )(*srcs, *lands, send_sems, recv_sems, after)
    return list(outs[:n]), list(outs[n:])


def _ew_block(rows, cols, slots):
    budget = 8 * 1024 * 1024
    br, bc = rows, cols
    while slots * br * bc * 4 > budget:
        if br % 2 == 0 and (br // 2) % (2 * SUBLANES) == 0:
            br //= 2
        elif bc % 2 == 0 and (bc // 2) % LANES == 0:
            bc //= 2
        else:
            break
    return br, bc


def _add_sibling(grads, recv, c_idx, name):
    _, rows, cols = grads.shape
    br, bc = _ew_block(rows, cols, 3)

    def body(c_ref, g_ref, r_ref, out_ref):
        del c_ref
        out_ref[...] = (g_ref[...].astype(F32) + r_ref[...].astype(F32)).astype(out_ref.dtype)

    grid_spec = pltpu.PrefetchScalarGridSpec(
        num_scalar_prefetch=1, grid=(N_CHIP, rows // br, cols // bc),
        in_specs=[pl.BlockSpec((1, br, bc), lambda k, i, j, c_ref: (c_ref[0] + 2 * k, i, j)),
                  pl.BlockSpec((1, br, bc), lambda k, i, j, c_ref: (k, i, j))],
        out_specs=pl.BlockSpec((1, br, bc), lambda k, i, j, c_ref: (k, i, j)))
    return pl.pallas_call(
        body, name=name, grid_spec=grid_spec, out_shape=jax.ShapeDtypeStruct((N_CHIP, rows, cols), grads.dtype),
        compiler_params=_params(("parallel", "parallel", "parallel"), 3 * _nbytes((br, bc), F32)),
    )(c_idx, grads, recv)


def _adam_math(g, w, m, v):
    m2 = ADAM_B1 * m + (1.0 - ADAM_B1) * g
    v2 = ADAM_B2 * v + (1.0 - ADAM_B2) * (g * g)
    m_hat = m2 * (1.0 / (1.0 - ADAM_B1 ** ADAM_STEP))
    v_hat = v2 * (1.0 / (1.0 - ADAM_B2 ** ADAM_STEP))
    return -ADAM_LR * (m_hat / (jnp.sqrt(v_hat) + ADAM_EPS) + ADAM_WD * w), m2, v2


def _adamw(slots, w, m, v, name, own=None, own_slot=None):
    ns, rows, cols = slots.shape
    br, bc = _ew_block(rows, cols, 2 * ns + 7)

    def update(g, w_ref, m_ref, v_ref, g_ref, d_ref, m2_ref, v2_ref):
        g_ref[...] = g
        d_ref[...], m2_ref[...], v2_ref[...] = _adam_math(g, w_ref[...], m_ref[...], v_ref[...])

    out_shape = [jax.ShapeDtypeStruct((rows, cols), F32)] * 4
    params = _params(("parallel", "parallel"), (2 * ns + 7) * _nbytes((br, bc), F32))
    grid = (rows // br, cols // bc)
    if own is None:
        def body(s_ref, *rest):
            g = s_ref[0].astype(F32)
            for k in range(1, ns):
                g = g + s_ref[k].astype(F32)
            update(g, *rest)

        blk = pl.BlockSpec((br, bc), lambda i, j: (i, j))
        return pl.pallas_call(
            body, name=name, grid=grid,
            in_specs=[pl.BlockSpec((ns, br, bc), lambda i, j: (0, i, j)), blk, blk, blk], out_specs=[blk] * 4,
            out_shape=out_shape, compiler_params=params,
        )(slots, w, m, v)

    def body_own(slot_ref, s_ref, o_ref, *rest):
        g = None
        for k in range(ns):
            term = jnp.where(slot_ref[0] == k, o_ref[k].astype(F32), s_ref[k].astype(F32))
            g = term if g is None else g + term
        update(g, *rest)

    blk = pl.BlockSpec((br, bc), lambda i, j, slot_ref: (i, j))
    stack = pl.BlockSpec((ns, br, bc), lambda i, j, slot_ref: (0, i, j))
    grid_spec = pltpu.PrefetchScalarGridSpec(num_scalar_prefetch=1, grid=grid, in_specs=[stack, stack, blk, blk, blk],
                                             out_specs=[blk] * 4)
    return pl.pallas_call(body_own, name=name, grid_spec=grid_spec, out_shape=out_shape, compiler_params=params,
                          )(own_slot, slots, own, w, m, v)


def _adamw_small(all_g, last_g, params, extra_shapes, name):
    names = [n for n, _ in _SMALL_ROWS if n in params]
    extras = [n for n, _ in _SMALL_ROWS if n not in params]
    offs = _small_offsets()
    n_p = len(names)

    def body(*refs):
        s_ref, last_ref = refs[0], refs[1]
        wmv = refs[2:2 + 3 * n_p]
        outs = refs[2 + 3 * n_p:2 + 7 * n_p]
        extra_refs = refs[2 + 7 * n_p:2 + 7 * n_p + len(extras)]
        summed = refs[-1]
        g, g_last = s_ref[0], last_ref[0]
        for k in range(1, N_DEV):
            g, g_last = g + s_ref[k], g_last + last_ref[k]
        summed[...] = g
        last_offs = _small_offsets(_LAST_SMALL)
        for n, rows in _LAST_SMALL:
            summed[offs[n]:offs[n] + rows, :] = g_last[last_offs[n]:last_offs[n] + rows, :]
        for i, n in enumerate(names):
            w_ref, m_ref, v_ref = wmv[3 * i:3 * i + 3]
            g_ref, d_ref, m2_ref, v2_ref = outs[4 * i:4 * i + 4]
            _rows_to(summed, offs[n], g_ref)
            d_ref[...], m2_ref[...], v2_ref[...] = _adam_math(g_ref[...], w_ref[...], m_ref[...], v_ref[...])
        for n, ref in zip(extras, extra_refs):
            _rows_to(summed, offs[n], ref)

    flat = [a for n in names for a in params[n]]
    out_shape = [jax.ShapeDtypeStruct(params[n][0].shape, F32) for n in names for _ in range(4)]
    out_shape += [jax.ShapeDtypeStruct(s, F32) for s in extra_shapes]
    vmem = pl.BlockSpec(memory_space=pltpu.VMEM)
    res = pl.pallas_call(
        body, name=name, in_specs=[vmem] * (2 + len(flat)), out_specs=[vmem] * len(out_shape), out_shape=out_shape,
        scratch_shapes=[pltpu.VMEM(all_g.shape[1:], F32)],
        compiler_params=pltpu.CompilerParams(vmem_limit_bytes=_vmem_limit(_nbytes(all_g.shape, F32))),
    )(all_g, last_g, *flat)
    return {n: res[4 * i:4 * i + 4] for i, n in enumerate(names)}, res[4 * n_p:]


def _mm_tiles(mode, m, n, k):
    tn = min(n, 1024)
    if mode == "tn":
        return min(m, 1024), tn, min(k, 4096)
    if k <= 1024:
        return min(m, 2048), tn, k
    if k <= 2048:
        return min(m, 1024), tn, k
    if k <= 4096:
        return min(m, 512), tn, k
    return min(m, 1024), tn, 2048


def _local_step(x, target, wts, small, exchange):
    t = x.shape[0]
    assert t % CONV_FWD_TILE == 0 and t % CONV_TILE == 0 and t % GMLP_TILE == 0 and t % (2 * ROW_TILE) == 0, t
    w_main_t, w_dt_t = wts["w_main_t"], wts["w_dt_t"]
    bsp_t = small["b_spatial"].T
    pad32 = lambda a: jnp.pad(a, ((0, 0), (0, DT_PAD - N_HEADS)))
    dtb, alog = pad32(small["dt_bias"]), pad32(small["a_log"])
    dskip_full = jnp.repeat(small["d_skip"], HEAD_DIM, axis=1)
    head_of_col = lax.broadcasted_iota(jnp.int32, (DT_PAD, D_INNER), 1) // HEAD_DIM
    e_bf = (head_of_col == lax.broadcasted_iota(jnp.int32, (DT_PAD, D_INNER), 0)).astype(BF16)

    def mm(a, b, mode, name, **kw):
        if mode == "nn":
            m, k, n = a.shape[0], a.shape[1], b.shape[1]
        elif mode == "nt":
            m, k, n = a.shape[0], a.shape[1], b.shape[0]
        else:
            m, k, n = a.shape[1], a.shape[0], b.shape[1]
        tm, tn, tk = _mm_tiles(mode, m, n, k)
        tm = min(tm, kw.pop("max_tm", tm))
        kw.setdefault("out_dtypes", (BF16,) if mode == "tn" else (F32,))
        if "extra_specs" in kw:
            kw["extra_specs"] = kw["extra_specs"](tm, tn)
        return _matmul(a, b, mode=mode, tm=tm, tn=tn, tk=tk, name=name, **kw)

    def out_tile(tm, tn):
        return (((tm, tn), lambda i, j: (i, j)),)

    def row_tiles(n_tiles, *vectors, gate_logits=False):
        def specs(tm, tn):
            out = [((tm, tn), lambda i, j: (i, j))] * n_tiles
            if gate_logits:
                out += [((tm, D_MODEL), lambda i, j, cb=COL_GATE // D_MODEL + half: (i, cb)) for half in range(2)]
            return tuple(out) + tuple(((1, w), lambda i, j, cb=cb: (0, cb)) for w, cb in vectors)
        return specs

    vec = lambda w: ((1, w), F32, (1, w), lambda i, j: (0, 0))
    fused_tm = 512

    h, dt_raw = _rms_fwd(x, small["norm_mix_g"], w_dt_t, "rms_mix", deps=exchange.begin())
    proj = mm(h, w_main_t, "nt", "proj_main", out_dtypes=(BF16,))
    y_a = _gmlp_fwd(proj, small["v_norm_g"], small["v_norm_b"], small["w_spatial"], bsp_t, "gmlp_fwd")
    pre_conv, xc = _conv_fwd(proj, wts["conv_w"], small["conv_b"], "conv_fwd")
    y_ssd, y_b, sprev = _ssd_fwd(xc, proj, dt_raw, dtb, alog, dskip_full, small["ssm_norm_g"], e_bf, "ssd_fwd")
    wts = {**wts, **exchange.late_weights(y_b)}
    pa = mm(y_a, wts["w_proj_a"], "nn", "proj_a")
    pb, merged = mm(y_b, wts["w_proj_b"], "nn", "proj_b", epilogue=_merge_epilogue, out_dtypes=(F32, BF16), max_tm=fused_tm,
                    extras=(pa, proj, proj, small["b_gates"], small["b_gates"]),
                    extra_specs=row_tiles(1, (D_MODEL, 0), (D_MODEL, 1), gate_logits=True))
    x1, h2 = mm(merged, wts["w_out"], "nn", "out_proj", epilogue=_residual_rms_epilogue, out_dtypes=(F32, BF16),
                max_tm=2 * fused_tm, extras=(x, small["norm_mlp_g"]), extra_specs=row_tiles(1, (D_MODEL, 0)))

    def relu_sq(acc, ex, outs, first):
        r = jnp.maximum(acc, 0.0)
        outs[0][...] = (r * r).astype(BF16)

    act = mm(h2, wts["w_mlp_up"], "nn", "mlp_up", epilogue=relu_sq, out_dtypes=(BF16,))
    dx2, dx2_b, g_final, _, loss = mm(
        act, wts["w_mlp_down"], "nn", "mlp_down", epilogue=_loss_epilogue, carry=True,
        out_dtypes=(F32, BF16, vec(D_MODEL), vec(D_MODEL), vec(LANES)),
        extras=(x1, small["norm_final_g"], target), extra_specs=lambda tm, tn: (
            ((tm, tn), lambda i, j: (i, j)), ((1, tn), lambda i, j: (0, 0)), ((tm, tn), lambda i, j: (i, j))))

    def relu_sq_bwd(acc, ex, outs, first):
        outs[0][...] = (acc * 2.0 * jnp.sqrt(ex[0][...].astype(F32))).astype(BF16)

    dup = mm(dx2_b, wts["w_mlp_down"], "nt", "d_act", epilogue=relu_sq_bwd, extras=(act,), extra_specs=out_tile,
             out_dtypes=(BF16,))
    g_down = mm(act, dx2_b, "tn", "g_mlp_down")
    g_up = mm(h2, dup, "tn", "g_mlp_up")
    dx1, dx1_b, g_mlp = mm(
        dup, wts["w_mlp_up"], "nt", "d_h2", epilogue=_rms_bwd_epilogue, carry=True,
        out_dtypes=(F32, BF16, vec(D_MODEL)), extras=(x1, small["norm_mlp_g"], dx2), extra_specs=lambda tm, tn: (
            ((tm, tn), lambda i, j: (i, j)), ((1, tn), lambda i, j: (0, 0)), ((tm, tn), lambda i, j: (i, j))))

    g_out = mm(merged, dx1_b, "tn", "g_out")
    dpa, dpb, dproj, g_bgates = mm(
        dx1_b, wts["w_out"], "nt", "d_merged", epilogue=_merge_bwd_epilogue, carry=True, max_tm=fused_tm,
        out_dtypes=(BF16, BF16, ((t, MAIN_W), BF16, (fused_tm, 2 * D_MODEL), lambda i, j: (i, COL_GATE // (2 * D_MODEL))),
                    vec(2 * D_MODEL)),
        extras=(pa, pb, proj, proj, small["b_gates"], small["b_gates"]),
        extra_specs=row_tiles(2, (D_MODEL, 0), (D_MODEL, 1), gate_logits=True))
    g_pa = mm(y_a, dpa, "tn", "g_proj_a")
    g_pb = mm(y_b, dpb, "tn", "g_proj_b")
    started = exchange.reduce("late", {"w_mlp_down": g_down, "w_mlp_up": g_up, "w_out": g_out, "w_proj_a": g_pa,
                                       "w_proj_b": g_pb})
    dya = mm(dpa, wts["w_proj_a"], "nt", "d_ya", deps=started)
    dyb = mm(dpb, wts["w_proj_b"], "nt", "d_yb")

    dproj, g_wsp, g_bsp_t, g_vg, g_vb = _gmlp_bwd(proj, dya, small["v_norm_g"], small["v_norm_b"], small["w_spatial"],
                                                   bsp_t, dproj, "gmlp_bwd")
    dproj, dxc, ddt, g_ng, g_dskip, g_alog, g_dtb, g_dt_t = _ssd_bwd(dyb, y_ssd, xc, proj, dt_raw, sprev, dtb, alog, dskip_full,
                                                                     small["ssm_norm_g"], e_bf, h, dproj, "ssd_bwd")
    dproj, g_convw, g_convb = _conv_bwd(proj, pre_conv, dxc, wts["conv_w"], dproj, "conv_bwd")

    small_grads = {
        "conv_w": g_convw, "loss": loss,
        "conv_b": g_convb, "dt_bias": g_dtb, "a_log": g_alog, "d_skip": g_dskip, "ssm_norm_g": g_ng,
        "v_norm_g": g_vg, "v_norm_b": g_vb, "w_spatial": g_wsp.reshape(GROUPS * CHUNK, CHUNK), "b_spatial": g_bsp_t.T,
        "b_gates": g_bgates, "norm_mlp_g": g_mlp, "norm_final_g": g_final,
    }
    g_main_t = mm(dproj, h, "tn", "g_in_main", deps=exchange.small(small_grads))
    started = exchange.reduce("in", {"w_in": (g_main_t, g_dt_t.astype(BF16))})

    def input_grad(acc, ex, outs, first):
        x_ref, g_ref, res_ref, ddt_ref, wdt_ref = ex
        gg = jnp.zeros((1, D_MODEL), F32)
        for r in range(acc.shape[0] // ROW_TILE):
            rows = slice(r * ROW_TILE, (r + 1) * ROW_TILE)
            dh = acc[rows] + _dot(ddt_ref[rows, :], wdt_ref[...], _NN)
            dx, gg_r = _rms_pullback(x_ref[rows, :], g_ref[...], dh)
            outs[0][rows, :] = dx + res_ref[rows, :]
            gg = gg + gg_r

        _zero_when(first, outs[1])
        outs[1][...] += gg

    grad_x, g_mix = mm(
        dproj, w_main_t, "nn", "d_h", epilogue=input_grad, deps=started, carry=True,
        out_dtypes=(F32, vec(D_MODEL)), extras=(x, small["norm_mix_g"], dx1, ddt, w_dt_t), extra_specs=lambda tm, tn: (
            ((tm, tn), lambda i, j: (i, j)), ((1, tn), lambda i, j: (0, 0)), ((tm, tn), lambda i, j: (i, j)),
            ((tm, DT_PAD), lambda i, j: (i, 0)), ((DT_PAD, D_MODEL), lambda i, j: (0, 0))))

    return grad_x, g_mix


SHARD_ROWS = (MAIN_W + N_HEADS) // N_DEV
REGROUP_IN = 2048


def _main_rows_of(gathered, name):
    n_dev, shard, d = gathered.shape
    blk = 1024
    nb = MAIN_W // blk

    def first_feature(b):
        return b * blk + (N_HEADS if b * blk >= COL_GATE else 0)

    def body(a_ref, b_ref, out_ref):
        for b in range(nb):
            s0, r0 = divmod(first_feature(b), shard)
            n1 = min(shard - r0, blk)

            @pl.when(pl.program_id(0) == b)
            def _(r0=r0, n1=n1):
                out_ref[0:n1, :] = a_ref[0, r0:r0 + n1, :]
                if n1 < blk:
                    out_ref[n1:blk, :] = b_ref[0, 0:blk - n1, :]

    def slot(b):
        return (b * blk + jnp.where(b * blk >= COL_GATE, N_HEADS, 0)) // shard

    return pl.pallas_call(
        body, name=name, grid=(nb,),
        in_specs=[pl.BlockSpec((1, shard, d), lambda b: (slot(b), 0, 0)),
                  pl.BlockSpec((1, shard, d), lambda b: (jnp.minimum(slot(b) + 1, n_dev - 1), 0, 0))],
        out_specs=pl.BlockSpec((blk, d), lambda b: (b, 0)),
        out_shape=jax.ShapeDtypeStruct((MAIN_W, d), gathered.dtype),
        compiler_params=_params(("parallel",), 3 * _nbytes((shard, d), gathered.dtype)),
    )(gathered, gathered)


def _by_device_rows(g_main_t, g_dt_t, name):
    d = g_main_t.shape[1]
    n_blocks = MAIN_W // REGROUP_IN
    dt_dev, dt_row = divmod(COL_GATE, SHARD_ROWS)

    def main_start(s):
        return s * SHARD_ROWS - (N_HEADS if s > dt_dev else 0)

    def body(a_ref, b_ref, dt_ref, out_ref):
        for s in range(N_DEV):
            m0 = main_start(s)
            k0, off = divmod(m0, REGROUP_IN)
            pieces = []
            if s == dt_dev:
                pieces = [(0, dt_row, m0), (dt_row, N_HEADS, None), (dt_row + N_HEADS, SHARD_ROWS - dt_row - N_HEADS, m0 + dt_row)]
            else:
                pieces = [(0, SHARD_ROWS, m0)]

            @pl.when(pl.program_id(0) == s)
            def _(pieces=pieces, k0=k0):
                for dst, n, src in pieces:
                    if src is None:
                        out_ref[0, dst:dst + n, :] = dt_ref[0:n, :]
                        continue
                    lo = src - k0 * REGROUP_IN
                    n_a = max(0, min(n, REGROUP_IN - lo))
                    if n_a:
                        out_ref[0, dst:dst + n_a, :] = a_ref[lo:lo + n_a, :]
                    if n_a < n:
                        lo_b = max(lo - REGROUP_IN, 0)
                        out_ref[0, dst + n_a:dst + n, :] = b_ref[lo_b:lo_b + n - n_a, :]

    def first_block(s):
        return (s * SHARD_ROWS - jnp.where(s > dt_dev, N_HEADS, 0)) // REGROUP_IN

    return pl.pallas_call(
        body, name=name, grid=(N_DEV,),
        in_specs=[pl.BlockSpec((REGROUP_IN, d), lambda s: (first_block(s), 0)),
                  pl.BlockSpec((REGROUP_IN, d), lambda s: (jnp.minimum(first_block(s) + 1, n_blocks - 1), 0)),
                  pl.BlockSpec((DT_PAD, d), lambda s: (0, 0))],
        out_specs=pl.BlockSpec((1, SHARD_ROWS, d), lambda s: (s, 0, 0)),
        out_shape=jax.ShapeDtypeStruct((N_DEV, SHARD_ROWS, d), g_main_t.dtype),
        compiler_params=_params(("parallel",), 3 * _nbytes((REGROUP_IN, d), g_main_t.dtype)),
    )(g_main_t, g_main_t, g_dt_t)


_LATE = ["w_proj_a", "w_proj_b", "w_out", "w_mlp_up", "w_mlp_down"]
_BY_COLS = ("w_mlp_up",)


class _Exchange:
    def __init__(self, late_shards, late_lands):
        self.late_shards, self.late_lands = late_shards, late_lands
        self.c_idx = lax.axis_index("c").astype(jnp.int32).reshape(1)
        self.chip_idx = (2 * lax.axis_index("x") + lax.axis_index("y")).astype(jnp.int32).reshape(1)
        self.pending = []

    def begin(self):
        self.late = _split_start(self.late_shards, self.late_lands, _gather_copies, N_DEV - 1, "gather_late_start")
        return [self.late[-1]]

    def late_weights(self, after):
        _, lands = _split_wait(self.late, _gather_copies, after, "gather_late_wait")
        whole = {}
        for n, g in zip(_LATE, lands):
            whole[n] = jnp.transpose(g, (1, 0, 2)).reshape(g.shape[1], -1) if n in _BY_COLS else g.reshape(-1, g.shape[2])
        return whole

    def reduce(self, tag, grads):
        names = list(grads)
        by_dev = []
        for n in names:
            g = grads[n]
            if n == "w_in":
                by_dev.append(_by_device_rows(*g, "regroup_g_in"))
            elif n in _BY_COLS:
                by_dev.append(jnp.transpose(g.reshape(g.shape[0], N_DEV, -1), (1, 0, 2)))
            else:
                by_dev.append(g.reshape(N_DEV, -1, g.shape[1]))
        from_sibling = _swap_with_sibling(by_dev, "reduce_cores_" + tag)
        parts = [_add_sibling(g, r, self.c_idx, "add_cores_" + n) for n, g, r in zip(names, by_dev, from_sibling)]
        lands = [lax.empty(p.shape, p.dtype) for p in parts]
        started = _split_start(parts, lands, _scatter_copies, 3, "reduce_chips_start_" + tag)
        self.pending.append((tag, names, started))
        return [started[-1]]

    def small(self, grads):
        dev = 2 * self.chip_idx + self.c_idx
        packed, land = _pack_small(grads, dev, "pack_small")
        self.small_started = _split_start([packed], [land], _gather_copies, N_DEV - 1, "exchange_small_start")
        return [self.small_started[-1]]

    def finish(self, after):
        _, (all_small,) = _split_wait(self.small_started, _gather_copies, after, "exchange_small_wait")
        done = {}
        for tag, names, started in self.pending:
            parts, lands = _split_wait(started, _scatter_copies, after, "reduce_chips_wait_" + tag)
            for n, land, part in zip(names, lands, parts):
                done[n] = (land, part, self.chip_idx)
        return all_small, done


def kernel(x, norm_mix_g, w_in, conv_w, conv_b, dt_bias, a_log, d_skip, ssm_norm_g, v_norm_g, v_norm_b, w_spatial, b_spatial, b_gates, w_proj_a, w_proj_b, w_out, norm_mlp_g, w_mlp_up, w_mlp_down, norm_final_g, loss_target, m_norm_mix_g, m_w_in, m_conv_w, m_conv_b, m_dt_bias, m_a_log, m_d_skip, m_ssm_norm_g, m_v_norm_g, m_v_norm_b, m_w_spatial, m_b_spatial, m_b_gates, m_w_proj_a, m_w_proj_b, m_w_out, m_norm_mlp_g, m_w_mlp_up, m_w_mlp_down, m_norm_final_g, v_norm_mix_g, v_w_in, v_conv_w, v_conv_b, v_dt_bias, v_a_log, v_d_skip, v_ssm_norm_g, v_v_norm_g, v_v_norm_b, v_w_spatial, v_b_spatial, v_b_gates, v_w_proj_a, v_w_proj_b, v_w_out, v_norm_mlp_g, v_w_mlp_up, v_w_mlp_down, v_norm_final_g):
    given = dict(locals())
    names = ["norm_mix_g", "w_in", "conv_w", "conv_b", "dt_bias", "a_log", "d_skip", "ssm_norm_g", "v_norm_g", "v_norm_b",
             "w_spatial", "b_spatial", "b_gates", "w_proj_a", "w_proj_b", "w_out", "norm_mlp_g", "w_mlp_up", "w_mlp_down",
             "norm_final_g"]
    shapes = {n: given[n].shape for n in names}
    dev = 4 * lax.axis_index("x") + 2 * lax.axis_index("y") + lax.axis_index("c")

    shard2d = {"w_in": w_in[0].T, "w_proj_a": w_proj_a[0], "w_proj_b": w_proj_b[0], "w_out": w_out[0],
               "w_mlp_up": w_mlp_up[0], "w_mlp_down": w_mlp_down[0]}
    conv_shard = conv_w.reshape(CONV_WIDTH, -1)
    late_shards = [shard2d[n].astype(BF16) for n in _LATE]
    w_in_all, conv_all, *late_lands = _all_gather([shard2d["w_in"].astype(BF16), conv_shard], "gather_first",
                                                  own_only=late_shards)
    dt_dev, dt_row = divmod(COL_GATE, SHARD_ROWS)
    w_dt_t = jnp.pad(w_in_all[dt_dev, dt_row:dt_row + N_HEADS], ((0, DT_PAD - N_HEADS), (0, 0)))
    wts = {"w_main_t": _main_rows_of(w_in_all, "regroup_w_in"), "w_dt_t": w_dt_t, "conv_w": jnp.transpose(conv_all, (1, 0, 2)).reshape(CONV_WIDTH, -1)}
    small = {"norm_mix_g": norm_mix_g, "conv_b": conv_b, "dt_bias": dt_bias, "a_log": a_log, "d_skip": d_skip,
             "ssm_norm_g": ssm_norm_g, "v_norm_g": v_norm_g, "v_norm_b": v_norm_b, "w_spatial": w_spatial[0],
             "b_spatial": b_spatial[0], "b_gates": b_gates, "norm_mlp_g": norm_mlp_g,
             "norm_final_g": norm_final_g.reshape(1, -1)}

    exchange = _Exchange(late_shards, late_lands)
    grad_x, g_mix = _local_step(x[0], loss_target[0], wts, small, exchange)

    out = {}
    all_small, large = exchange.finish(grad_x)
    for n, (slots, own, own_slot) in large.items():
        moments = [given["m_" + n][0], given["v_" + n][0]]
        if n == "w_in":
            moments = [mom.T for mom in moments]
        res = _adamw(slots, shard2d[n], *moments, "adamw_" + n, own=own, own_slot=own_slot)
        out[n] = [(r.T if n == "w_in" else r).reshape(shapes[n]) for r in res]

    last_small = _exchange_small({"norm_mix_g": g_mix}, _LAST_SMALL, "exchange_last")
    small["w_spatial"] = small["w_spatial"].reshape(GROUPS * CHUNK, CHUNK)
    params = {n: (w2d, given["m_" + n].reshape(w2d.shape), given["v_" + n].reshape(w2d.shape)) for n, w2d in small.items()}
    updated, (g_conv_full, loss_all) = _adamw_small(all_small, last_small, params, [(CONV_WIDTH, CONV_DIM), (1, LANES)],
                                                    "adamw_small")
    for n, res in updated.items():
        out[n] = [r.reshape(shapes[n]) for r in res]
    width = shapes["conv_w"][-1]
    g_conv = lax.dynamic_slice(g_conv_full, (0, dev * width), (CONV_WIDTH, width))
    res = _adamw(g_conv[None], conv_shard, m_conv_w.reshape(CONV_WIDTH, -1), v_conv_w.reshape(CONV_WIDTH, -1), "adamw_conv_w")
    out["conv_w"] = [r.reshape(shapes["conv_w"]) for r in res]

    loss = loss_all[0, 0]
    return (loss, grad_x[None], *[out[n][0] for n in names], *[out[n][1] for n in names],
            *[out[n][2] for n in names], *[out[n][3] for n in names])
```

```python
import functools
import math

import jax
import jax.numpy as jnp
from jax import lax
from jax.experimental import pallas as pl
from jax.experimental.pallas import tpu as pltpu

F32 = jnp.float32
BF16 = jnp.bfloat16
MESH = pl.DeviceIdType.MESH

D_MODEL = 1024
NORM_EPS = 1e-6
CHUNK = 128
GROUPS = 8
D_INNER = 2048
HEAD_DIM = 64
N_HEADS = 32
D_STATE = 128
CONV_WIDTH = 4
CONV_DIM = 4096
D_FF = 4096
GROUP_W = D_INNER // GROUPS
N_DEV = 8
N_CHIP = 4

ADAM_LR = 0.001
ADAM_B1 = 0.9
ADAM_B2 = 0.999
ADAM_EPS = 1e-08
ADAM_WD = 0.01
ADAM_STEP = 10

MAIN_W = 2 * D_MODEL + D_INNER + CONV_DIM + 2 * D_MODEL
COL_Z = 2048
COL_XBC = 4096
COL_GATE = 8192
DT_PAD = 128

LANES = 128
SUBLANES = 8
VMEM_BYTES_V7X = 64 * 1024 * 1024
VMEM_BODY_TEMP = 24 * 1024 * 1024


def _vmem_limit(block_bytes):
    return int(min(2 * block_bytes + VMEM_BODY_TEMP, VMEM_BYTES_V7X - 8 * 1024 * 1024))


def _nbytes(shape, dtype):
    return math.prod(shape) * jnp.dtype(dtype).itemsize


_HBM = pl.BlockSpec(memory_space=pl.ANY)


def _params(sem, block_bytes):
    return pltpu.CompilerParams(dimension_semantics=sem, vmem_limit_bytes=_vmem_limit(block_bytes))


def _sigmoid(x):
    return 1.0 / (1.0 + jnp.exp(-x))


def _softplus(x):
    e = jnp.exp(-jnp.abs(x))
    u = 1.0 + e
    log1p_e = jnp.where(u == 1.0, e, jnp.log(u) * (e / jnp.where(u == 1.0, 1.0, u - 1.0)))
    return jnp.maximum(x, 0.0) + log1p_e


_SQRT_HALF = 0.7071067811865476
_INV_SQRT_2PI = 0.3989422804014327


def _normal_cdf(x):
    return 0.5 * (1.0 + lax.erf(x * _SQRT_HALF))


def _gelu_grad(x, cdf):
    return cdf + x * jnp.exp(-0.5 * x * x) * _INV_SQRT_2PI


def _dot(a, b, dims):
    return lax.dot_general(a, b, (dims, ((), ())), preferred_element_type=F32)


_NN = ((1,), (0,))
_NT = ((1,), (1,))
_TN = ((0,), (0,))


def _split3(x):
    hi = x.astype(BF16)
    r1 = x - hi.astype(F32)
    mid = r1.astype(BF16)
    lo = (r1 - mid.astype(F32)).astype(BF16)
    return hi, mid, lo


def _dot_exact_rhs(x, e, dims):
    hi, mid, lo = _split3(x)
    return _dot(hi, e, dims) + _dot(mid, e, dims) + _dot(lo, e, dims)


def _dot_exact_lhs(e, x, dims):
    hi, mid, lo = _split3(x)
    return _dot(e, hi, dims) + _dot(e, mid, dims) + _dot(e, lo, dims)


def _tri(lower):
    r = lax.broadcasted_iota(jnp.int32, (CHUNK, CHUNK), 0)
    c = lax.broadcasted_iota(jnp.int32, (CHUNK, CHUNK), 1)
    return (r >= c) if lower else (r <= c)


def _matmul(a, b, *, mode, tm, tn, tk, out_dtypes, name, epilogue=None, extras=(), extra_specs=(), deps=(),
            carry=False):
    if mode == "nn":
        (m, k), (_, n) = a.shape, b.shape
    elif mode == "nt":
        (m, k), (n, _) = a.shape, b.shape
    else:
        (k, m), (_, n) = a.shape, b.shape
    assert m % tm == 0 and n % tn == 0 and k % tk == 0, (name, m, n, k, tm, tn, tk)
    nk = k // tk
    n_extra, n_out = len(extras), len(out_dtypes)
    first_out = 2 + n_extra + len(deps)
    dims = {"nn": _NN, "nt": _NT, "tn": _TN}[mode]
    if epilogue is None:
        def epilogue(acc, ex, outs, first):
            outs[0][...] = acc.astype(outs[0].dtype)

    def body(*refs):
        a_ref, b_ref = refs[0], refs[1]
        ex_refs = refs[2:2 + n_extra]
        outs = refs[first_out:first_out + n_out]
        first_tile = pl.program_id(0) == 0
        p = _dot(a_ref[...], b_ref[...], dims)
        if nk == 1:
            epilogue(p, ex_refs, outs, first_tile)
            return
        acc_ref = refs[first_out + n_out]
        kk = pl.program_id(2)

        @pl.when(kk == 0)
        def _():
            acc_ref[...] = p

        @pl.when(kk > 0)
        def _():
            acc_ref[...] += p

        @pl.when(kk == nk - 1)
        def _():
            epilogue(acc_ref[...], ex_refs, outs, first_tile)

    grid = (m // tm, n // tn, nk)

    if mode == "nn":
        a_spec = pl.BlockSpec((tm, tk), (lambda i, j, kk: (i, kk)))
        b_spec = pl.BlockSpec((tk, tn), (lambda i, j, kk: (kk, j)))
        a_blk, b_blk = (tm, tk), (tk, tn)
    elif mode == "nt":
        a_spec = pl.BlockSpec((tm, tk), (lambda i, j, kk: (i, kk)))
        b_spec = pl.BlockSpec((tn, tk), (lambda i, j, kk: (j, kk)))
        a_blk, b_blk = (tm, tk), (tn, tk)
    else:
        a_spec = pl.BlockSpec((tk, tm), (lambda i, j, kk: (kk, i)))
        b_spec = pl.BlockSpec((tk, tn), (lambda i, j, kk: (kk, j)))
        a_blk, b_blk = (tk, tm), (tk, tn)
    ex_specs = [pl.BlockSpec(shape, (lambda i, j, kk, f=f: f(i, j))) for shape, f in extra_specs]
    outs = [o if isinstance(o, tuple) else ((m, n), o, (tm, tn), lambda i, j: (i, j)) for o in out_dtypes]
    out_spec = [pl.BlockSpec(blk_shape, (lambda i, j, kk, f=f: f(i, j))) for _, _, blk_shape, f in outs]
    out_shape = [jax.ShapeDtypeStruct(shape, dt) for shape, dt, _, _ in outs]
    blk = (_nbytes(a_blk, a.dtype) + _nbytes(b_blk, b.dtype) + sum(_nbytes(s, F32) for s, _ in extra_specs)
           + sum(_nbytes(blk_shape, dt) for _, dt, blk_shape, _ in outs) + _nbytes((tm, tn), F32))
    order = ("arbitrary",) * 3 if carry else ("parallel", "parallel", "arbitrary")
    res = pl.pallas_call(
        body, name=name, grid=grid,
        in_specs=[a_spec, b_spec] + ex_specs + [_HBM] * len(deps), out_specs=out_spec, out_shape=out_shape,
        scratch_shapes=[pltpu.VMEM((tm, tn), F32)] if nk > 1 else [],
        compiler_params=_params(order, blk),
    )(a, b, *extras, *deps)
    return res[0] if n_out == 1 else res


ROW_TILE = 256


def _row_spec(width, col_block=0, tile=ROW_TILE):
    return pl.BlockSpec((tile, width), lambda i, cb=col_block: (i, cb))


def _vec_spec(width, col_block=0):
    return pl.BlockSpec((1, width), lambda i, cb=col_block: (0, cb))


def _rms_fwd(x, g, w_t, name, deps=()):
    t = x.shape[0]
    n_small = w_t.shape[0]
    tile = 2 * ROW_TILE

    def body(x_ref, g_ref, w_ref, *rest):
        h_ref, small_ref = rest[-2:]
        xv = x_ref[...]
        r = lax.rsqrt(jnp.mean(xv * xv, axis=-1, keepdims=True) + NORM_EPS)
        h = (xv * r * g_ref[...]).astype(BF16)
        h_ref[...] = h
        small_ref[...] = _dot(h, w_ref[...], _NT)

    return pl.pallas_call(
        body, name=name, grid=(t // tile,),
        in_specs=[_row_spec(D_MODEL, 0, tile), _vec_spec(D_MODEL), pl.BlockSpec((n_small, D_MODEL), lambda i: (0, 0))]
        + [_HBM] * len(deps),
        out_specs=[_row_spec(D_MODEL, 0, tile), _row_spec(n_small, 0, tile)],
        out_shape=[jax.ShapeDtypeStruct((t, D_MODEL), BF16), jax.ShapeDtypeStruct((t, n_small), F32)],
        compiler_params=_params(("parallel",), 3 * _nbytes((tile, D_MODEL), F32)),
    )(x, g, w_t, *deps)


def _rms_scale(xv):
    r = lax.rsqrt(jnp.mean(xv * xv, axis=-1, keepdims=True) + NORM_EPS)
    return r, xv * r


def _rms_pullback(xv, g, dh):
    r, xh = _rms_scale(xv)
    dyg = dh * g
    return r * (dyg - xh * jnp.mean(dyg * xh, axis=-1, keepdims=True)), jnp.sum(dh * xh, axis=0, keepdims=True)


def _zero_when(first, *refs):
    @pl.when(first)
    def _():
        for ref in refs:
            ref[...] = jnp.zeros_like(ref)


def _residual_rms_epilogue(acc, ex, outs, first):
    x1 = acc + ex[0][...]
    outs[0][...] = x1
    _, xh = _rms_scale(x1)
    outs[1][...] = (xh * ex[1][...]).astype(BF16)


def _loss_epilogue(acc, ex, outs, first):
    dx_ref, dxb_ref, gg_ref, sq_ref, tot_ref = outs
    gv = ex[1][...]
    r, xh = _rms_scale(acc + ex[0][...])
    err = xh * gv - ex[2][...]
    dy = err * (1.0 / D_MODEL)
    dyg = dy * gv
    dx = r * (dyg - xh * jnp.mean(dyg * xh, axis=-1, keepdims=True))
    dx_ref[...] = dx
    dxb_ref[...] = dx.astype(BF16)

    _zero_when(first, gg_ref, sq_ref)
    gg_ref[...] += jnp.sum(dy * xh, axis=0, keepdims=True)
    sq_ref[...] += jnp.sum(err * err, axis=0, keepdims=True)
    tot_ref[...] = jnp.broadcast_to(jnp.sum(sq_ref[...], axis=1, keepdims=True) * (0.5 / D_MODEL), tot_ref.shape)


def _rms_bwd_epilogue(dh, ex, outs, first):
    dx, gg = _rms_pullback(ex[0][...], ex[1][...], dh)
    dx = dx + ex[2][...]
    outs[0][...] = dx
    if len(outs) == 3:
        outs[1][...] = dx.astype(BF16)

    _zero_when(first, outs[-1])
    outs[-1][...] += gg


def _merge_epilogue(acc, ex, outs, first):
    outs[0][...] = acc.astype(outs[0].dtype)
    ga = _sigmoid(ex[1][...].astype(F32) + ex[3][...])
    gb = _sigmoid(ex[2][...].astype(F32) + ex[4][...])
    outs[1][...] = (ga * ex[0][...].astype(F32) + gb * acc).astype(BF16)


def _merge_bwd_epilogue(dm, ex, outs, first):
    dpa_ref, dpb_ref, dgl_ref, gb_ref = outs
    ga = _sigmoid(ex[2][...].astype(F32) + ex[4][...])
    gb = _sigmoid(ex[3][...].astype(F32) + ex[5][...])
    dpa_ref[...] = (dm * ga).astype(BF16)
    dpb_ref[...] = (dm * gb).astype(BF16)
    dla = dm * ex[0][...].astype(F32) * ga * (1.0 - ga)
    dlb = dm * ex[1][...].astype(F32) * gb * (1.0 - gb)
    dgl_ref[:, :D_MODEL] = dla.astype(BF16)
    dgl_ref[:, D_MODEL:] = dlb.astype(BF16)

    _zero_when(first, gb_ref)
    gb_ref[:, :D_MODEL] += jnp.sum(dla, axis=0, keepdims=True)
    gb_ref[:, D_MODEL:] += jnp.sum(dlb, axis=0, keepdims=True)


GMLP_TILE = 512
GMLP_NC = GMLP_TILE // CHUNK


def _gmlp_common(u_pre, v_pre, vg, vb):
    cdf_u, cdf_v = _normal_cdf(u_pre), _normal_cdf(v_pre)
    u = u_pre * cdf_u
    v = v_pre * cdf_v
    mu = jnp.mean(v, axis=-1, keepdims=True)
    vc = v - mu
    rstd = lax.rsqrt(jnp.mean(vc * vc, axis=-1, keepdims=True) + NORM_EPS)
    vh = vc * rstd
    vn = vh * vg + vb
    return u, vh, vn, rstd, cdf_u, cdf_v


def _chunks_to_lanes(x, g):
    return jnp.concatenate([x[c * CHUNK:(c + 1) * CHUNK, g * CHUNK:(g + 1) * CHUNK] for c in range(GMLP_NC)], axis=1)


def _gmlp_fwd(proj, vg, vb, wsp, bsp_t, name):
    t = proj.shape[0]

    def body(u_ref, v_ref, vg_ref, vb_ref, w_ref, b_ref, ya_ref):
        u, _, vn, _, _, _ = _gmlp_common(u_ref[...].astype(F32), v_ref[...].astype(F32), vg_ref[...], vb_ref[...])
        mask = _tri(True)
        bt = b_ref[...]
        for g in range(GROUPS):
            w = jnp.where(mask, w_ref[g], 0.0).astype(BF16)
            vcat = _chunks_to_lanes(vn, g).astype(BF16)
            s = _dot(w, vcat, _NN) + bt[:, g:g + 1]
            for c in range(GMLP_NC):
                rows, cols = slice(c * CHUNK, (c + 1) * CHUNK), slice(g * CHUNK, (g + 1) * CHUNK)
                ya_ref[rows, cols] = (u[rows, cols] * s[:, c * CHUNK:(c + 1) * CHUNK]).astype(BF16)

    return pl.pallas_call(
        body, name=name, grid=(t // GMLP_TILE,),
        in_specs=[_row_spec(D_MODEL, 0, GMLP_TILE), _row_spec(D_MODEL, 1, GMLP_TILE), _vec_spec(D_MODEL),
                  _vec_spec(D_MODEL), pl.BlockSpec((GROUPS, CHUNK, CHUNK), lambda i: (0, 0, 0)),
                  pl.BlockSpec((CHUNK, GROUPS), lambda i: (0, 0))],
        out_specs=_row_spec(D_MODEL, 0, GMLP_TILE),
        out_shape=jax.ShapeDtypeStruct((t, D_MODEL), BF16),
        compiler_params=_params(("parallel",), 3 * _nbytes((GMLP_TILE, D_MODEL), F32)),
    )(proj, proj, vg, vb, wsp, bsp_t)


def _gmlp_bwd(proj, dya, vg, vb, wsp, bsp_t, dproj, name):
    t = proj.shape[0]

    def body(u_ref, v_ref, dya_ref, vg_ref, vb_ref, w_ref, b_ref, dproj_in, duv_ref, gw_ref, gbt_ref, gvg_ref, gvb_ref,
             dvn_scr, du_scr):
        del dproj_in
        u_pre, v_pre = u_ref[...].astype(F32), v_ref[...].astype(F32)
        vgv = vg_ref[...]
        u, vh, vn, rstd, cdf_u, cdf_v = _gmlp_common(u_pre, v_pre, vgv, vb_ref[...])
        dya = dya_ref[...]
        mask = _tri(True)
        bt = b_ref[...]
        first = pl.program_id(0) == 0

        @pl.when(first)
        def _():
            gw_ref[...] = jnp.zeros_like(gw_ref)
            gbt_ref[...] = jnp.zeros_like(gbt_ref)
            gvg_ref[...] = jnp.zeros_like(gvg_ref)
            gvb_ref[...] = jnp.zeros_like(gvb_ref)

        lane = lax.broadcasted_iota(jnp.int32, (CHUNK, GROUPS), 1)
        gbt = jnp.zeros((CHUNK, GROUPS), F32)
        for g in range(GROUPS):
            w = jnp.where(mask, w_ref[g], 0.0).astype(BF16)
            vcat = _chunks_to_lanes(vn, g).astype(BF16)
            s = _dot(w, vcat, _NN) + bt[:, g:g + 1]
            ds = _chunks_to_lanes(dya * u, g)
            gbt = jnp.where(lane == g, jnp.sum(ds, axis=1, keepdims=True), gbt)
            dsb = ds.astype(BF16)
            gw_ref[g] += jnp.where(mask, _dot(dsb, vcat, _NT), 0.0)
            dv = _dot(w, dsb, _TN)
            for c in range(GMLP_NC):
                rows, cols = slice(c * CHUNK, (c + 1) * CHUNK), slice(g * CHUNK, (g + 1) * CHUNK)
                dvn_scr[rows, cols] = dv[:, c * CHUNK:(c + 1) * CHUNK]
                du_scr[rows, cols] = dya[rows, cols] * s[:, c * CHUNK:(c + 1) * CHUNK]
        gbt_ref[...] += gbt
        dvn = dvn_scr[...]
        gvg_ref[...] += jnp.sum(dvn * vh, axis=0, keepdims=True)
        gvb_ref[...] += jnp.sum(dvn, axis=0, keepdims=True)
        dvh = dvn * vgv
        dv = rstd * (dvh - jnp.mean(dvh, axis=-1, keepdims=True) - vh * jnp.mean(dvh * vh, axis=-1, keepdims=True))
        duv_ref[:, :D_MODEL] = (du_scr[...] * _gelu_grad(u_pre, cdf_u)).astype(BF16)
        duv_ref[:, D_MODEL:] = (dv * _gelu_grad(v_pre, cdf_v)).astype(BF16)

    return pl.pallas_call(
        body, name=name, grid=(t // GMLP_TILE,),
        in_specs=[_row_spec(D_MODEL, 0, GMLP_TILE), _row_spec(D_MODEL, 1, GMLP_TILE), _row_spec(D_MODEL, 0, GMLP_TILE),
                  _vec_spec(D_MODEL), _vec_spec(D_MODEL), pl.BlockSpec((GROUPS, CHUNK, CHUNK), lambda i: (0, 0, 0)),
                  pl.BlockSpec((CHUNK, GROUPS), lambda i: (0, 0)), pl.BlockSpec(memory_space=pl.ANY)],
        out_specs=[_row_spec(2 * D_MODEL, 0, GMLP_TILE), pl.BlockSpec((GROUPS, CHUNK, CHUNK), lambda i: (0, 0, 0)),
                   pl.BlockSpec((CHUNK, GROUPS), lambda i: (0, 0)), _vec_spec(D_MODEL), _vec_spec(D_MODEL)],
        out_shape=[jax.ShapeDtypeStruct(dproj.shape, BF16), jax.ShapeDtypeStruct((GROUPS, CHUNK, CHUNK), F32),
                   jax.ShapeDtypeStruct((CHUNK, GROUPS), F32), jax.ShapeDtypeStruct((1, D_MODEL), F32),
                   jax.ShapeDtypeStruct((1, D_MODEL), F32)],
        scratch_shapes=[pltpu.VMEM((GMLP_TILE, D_MODEL), F32), pltpu.VMEM((GMLP_TILE, D_MODEL), F32)],
        input_output_aliases={7: 0},
        compiler_params=_params(("arbitrary",), 6 * _nbytes((GMLP_TILE, D_MODEL), F32)),
    )(proj, proj, dya, vg, vb, wsp, bsp_t, dproj)


CONV_TILE = 1024
CONV_FWD_TILE = 2048
CONV_COLS = 1024
CONV_RB = 32
HALO = SUBLANES


def _conv_fwd(proj, cw, cb, name):
    t = proj.shape[0]
    nj = CONV_DIM // CONV_COLS
    xcb = COL_XBC // CONV_COLS
    before = 2 * HALO
    rb = CONV_FWD_TILE // before

    def body(x_ref, prev_ref, cw_ref, cb_ref, pre_ref, xc_ref):
        i = pl.program_id(1)
        cw_v = cw_ref[...]
        cb_v = cb_ref[...]
        for b in range(CONV_FWD_TILE // CONV_RB):
            if b == 0:
                prev = jnp.where(i > 0, prev_ref[...].astype(F32)[HALO:, :], 0.0)
                ext = jnp.concatenate([prev, x_ref[:CONV_RB, :].astype(F32)], axis=0)
            else:
                ext = x_ref[b * CONV_RB - before:(b + 1) * CONV_RB, :].astype(F32)[HALO:, :]
            pre = cb_v + cw_v[CONV_WIDTH - 1:CONV_WIDTH, :] * ext[HALO:, :]
            for k in range(CONV_WIDTH - 1):
                back = CONV_WIDTH - 1 - k
                pre = pre + cw_v[k:k + 1, :] * pltpu.roll(ext, back, 0)[HALO:, :]
            pre_ref[b * CONV_RB:(b + 1) * CONV_RB, :] = pre
            xc_ref[b * CONV_RB:(b + 1) * CONV_RB, :] = pre * _sigmoid(pre)

    tile = pl.BlockSpec((CONV_FWD_TILE, CONV_COLS), lambda j, i: (i, j))
    return pl.pallas_call(
        body, name=name, grid=(nj, t // CONV_FWD_TILE),
        in_specs=[pl.BlockSpec((CONV_FWD_TILE, CONV_COLS), lambda j, i: (i, xcb + j)),
                  pl.BlockSpec((before, CONV_COLS), lambda j, i: (jnp.maximum(i * rb - 1, 0), xcb + j)),
                  pl.BlockSpec((CONV_WIDTH, CONV_COLS), lambda j, i: (0, j)),
                  pl.BlockSpec((1, CONV_COLS), lambda j, i: (0, j))],
        out_specs=[tile, tile],
        out_shape=[jax.ShapeDtypeStruct((t, CONV_DIM), F32), jax.ShapeDtypeStruct((t, CONV_DIM), F32)],
        compiler_params=_params(("parallel", "parallel"), 4 * _nbytes((CONV_FWD_TILE, CONV_COLS), F32)),
    )(proj, proj, cw, cb)


def _fold_rows(v):
    out = v[:SUBLANES]
    for r in range(1, v.shape[0] // SUBLANES):
        out = out + v[r * SUBLANES:(r + 1) * SUBLANES]
    return out


def _conv_bwd(proj, pre, dxc, cw, dproj, name):
    t = proj.shape[0]
    nj = CONV_DIM // CONV_COLS
    ni = t // CONV_TILE
    xcb = COL_XBC // CONV_COLS
    rb = CONV_TILE // HALO
    last_rb = t // HALO - 1

    def body(x_ref, p_ref, pnext_ref, d_ref, dnext_ref, cw_ref, dproj_in, dx_ref, gw_ref, gb_ref):
        del dproj_in
        i = pl.program_id(1)
        cw_v = cw_ref[...]

        def dpre_of(p, d):
            sg = _sigmoid(p)
            return d * sg * (1.0 + p * (1.0 - sg))

        @pl.when(i == 0)
        def _():
            gw_ref[...] = jnp.zeros_like(gw_ref)
            gb_ref[...] = jnp.zeros_like(gb_ref)

        head = dpre_of(pnext_ref[...], jnp.where(i < ni - 1, dnext_ref[...], 0.0))
        gb_acc = jnp.zeros((SUBLANES, CONV_COLS), F32)
        gw_acc = [jnp.zeros((SUBLANES, CONV_COLS), F32) for _ in range(CONV_WIDTH)]
        for b in reversed(range(CONV_TILE // CONV_RB)):
            rows = slice(b * CONV_RB, (b + 1) * CONV_RB)
            cur = dpre_of(p_ref[rows, :], d_ref[rows, :])
            ext = jnp.concatenate([cur, head], axis=0)
            xv = x_ref[rows, :].astype(F32)
            dx = None
            for k in range(CONV_WIDTH):
                shift = CONV_WIDTH - 1 - k
                win = cur if shift == 0 else pltpu.roll(ext, CONV_RB + HALO - shift, 0)[:CONV_RB, :]
                term = cw_v[k:k + 1, :] * win
                dx = term if dx is None else dx + term
                gw_acc[k] = gw_acc[k] + _fold_rows(win * xv)
            dx_ref[rows, :] = dx.astype(BF16)
            gb_acc = gb_acc + _fold_rows(cur)
            head = cur[:HALO]
        gb_ref[...] += jnp.sum(gb_acc, axis=0, keepdims=True)
        for k in range(CONV_WIDTH):
            gw_ref[k:k + 1, :] += jnp.sum(gw_acc[k], axis=0, keepdims=True)

    tile = pl.BlockSpec((CONV_TILE, CONV_COLS), lambda j, i: (i, j))
    after = pl.BlockSpec((HALO, CONV_COLS), lambda j, i: (jnp.minimum((i + 1) * rb, last_rb), j))
    return pl.pallas_call(
        body, name=name, grid=(nj, ni),
        in_specs=[pl.BlockSpec((CONV_TILE, CONV_COLS), lambda j, i: (i, xcb + j)), tile, after, tile, after,
                  pl.BlockSpec((CONV_WIDTH, CONV_COLS), lambda j, i: (0, j)),
                  pl.BlockSpec(memory_space=pl.ANY)],
        out_specs=[pl.BlockSpec((CONV_TILE, CONV_COLS), lambda j, i: (i, xcb + j)),
                   pl.BlockSpec((CONV_WIDTH, CONV_COLS), lambda j, i: (0, j)),
                   pl.BlockSpec((1, CONV_COLS), lambda j, i: (0, j))],
        out_shape=[jax.ShapeDtypeStruct(dproj.shape, BF16), jax.ShapeDtypeStruct((CONV_WIDTH, CONV_DIM), F32),
                   jax.ShapeDtypeStruct((1, CONV_DIM), F32)],
        input_output_aliases={6: 0},
        compiler_params=_params(("parallel", "arbitrary"), 4 * _nbytes((CONV_TILE, CONV_COLS), F32)),
    )(proj, pre, pre, dxc, dxc, cw, dproj)


def _ssd_decays(dt_raw, dtb, alog, e_bf, tril_bf):
    dtv = _softplus(dt_raw + dtb)
    a = -jnp.exp(alog)
    cs = _dot_exact_lhs(tril_bf, dtv * a, _NN)
    cs_last = cs[CHUNK - 1:CHUNK, :]
    stack = jnp.concatenate([dtv, jnp.exp(cs), jnp.exp(cs_last - cs)], axis=0)
    full = _head_expand(stack, e_bf)
    return dtv, a, cs, full[:CHUNK], full[CHUNK:2 * CHUNK], full[2 * CHUNK:]


def _split2(x):
    hi = x.astype(BF16)
    return hi, (x - hi.astype(F32)).astype(BF16)


def _head_expand(x, e_bf):
    hi, mid = _split2(x)
    return _dot(hi, e_bf, _NN) + _dot(mid, e_bf, _NN)


def _head_sums(x, e_bf):
    hi, mid = _split2(x)
    return _dot(hi, e_bf, _NT) + _dot(mid, e_bf, _NT)


def _head_mats(cs, cs_t, cb, h, mask):
    seg = cs[:, h:h + 1] - cs_t[h:h + 1, :]
    lmat = jnp.exp(jnp.where(mask, seg, -jnp.inf))
    return lmat, cb * lmat


def _ssd_fwd(xc, proj, dt_raw, dtb, alog, dskip_full, ng, e_bf, name):
    t = xc.shape[0]
    nc = t // CHUNK
    zcb = COL_Z // D_INNER

    def body(xc_ref, z_ref, dt_ref, dtb_ref, alog_ref, dsk_ref, ng_ref, e_ref, y_ref, yb_ref, sprev_ref, s_scr):
        @pl.when(pl.program_id(0) == 0)
        def _():
            s_scr[...] = jnp.zeros_like(s_scr)

        mask = _tri(True)
        tril_bf = mask.astype(BF16)
        e_v = e_ref[...]
        _, _, cs, dt_full, ecs_full, decay_full = _ssd_decays(dt_ref[...], dtb_ref[...], alog_ref[...], e_v, tril_bf)
        cs_t = cs.T
        sprev_ref[0] = s_scr[...]
        for g in range(GROUPS):
            gc = slice(g * GROUP_W, (g + 1) * GROUP_W)
            xs = xc_ref[:, gc]
            xdt = xs * dt_full[:, gc]
            xdt_b = xdt.astype(BF16)
            xdec = (xdt * decay_full[:, gc]).astype(BF16)
            bg = xc_ref[:, D_INNER + g * D_STATE:D_INNER + (g + 1) * D_STATE].astype(BF16)
            cg = xc_ref[:, D_INNER + GROUPS * D_STATE + g * D_STATE:D_INNER + GROUPS * D_STATE + (g + 1) * D_STATE].astype(BF16)
            cb = _dot(cg, bg, _NT)
            s_prev = s_scr[:, gc]
            y_off = ecs_full[:, gc] * _dot(cg, s_prev.astype(BF16), _NN)
            s_scr[:, gc] = s_prev * ecs_full[CHUNK - 1:CHUNK, gc] + _dot(bg, xdec, _TN)
            parts = []
            for r in range(GROUP_W // HEAD_DIM):
                h = g * (GROUP_W // HEAD_DIM) + r
                _, m = _head_mats(cs, cs_t, cb, h, mask)
                parts.append(_dot(m.astype(BF16), xdt_b[:, r * HEAD_DIM:(r + 1) * HEAD_DIM], _NN))
            yg = jnp.concatenate(parts, axis=1) + y_off + dsk_ref[:, gc] * xs
            y_ref[:, gc] = yg
            zv = z_ref[:, gc].astype(F32)
            ygate = yg * (zv * _sigmoid(zv))
            rstd = lax.rsqrt(jnp.mean(ygate * ygate, axis=-1, keepdims=True) + NORM_EPS)
            yb_ref[:, gc] = (ygate * rstd * ng_ref[:, gc]).astype(BF16)

    vec = lambda w: pl.BlockSpec((1, w), lambda i: (0, 0))
    blk = _nbytes((CHUNK, CONV_DIM), F32) + 3 * _nbytes((CHUNK, D_INNER), F32) + _nbytes((D_STATE, D_INNER), F32)
    return pl.pallas_call(
        body, name=name, grid=(nc,),
        in_specs=[pl.BlockSpec((CHUNK, CONV_DIM), lambda i: (i, 0)), pl.BlockSpec((CHUNK, D_INNER), lambda i: (i, zcb)),
                  pl.BlockSpec((CHUNK, DT_PAD), lambda i: (i, 0)), vec(DT_PAD), vec(DT_PAD), vec(D_INNER), vec(D_INNER),
                  pl.BlockSpec((DT_PAD, D_INNER), lambda i: (0, 0))],
        out_specs=[pl.BlockSpec((CHUNK, D_INNER), lambda i: (i, 0)), pl.BlockSpec((CHUNK, D_INNER), lambda i: (i, 0)),
                   pl.BlockSpec((1, D_STATE, D_INNER), lambda i: (i, 0, 0))],
        out_shape=[jax.ShapeDtypeStruct((t, D_INNER), F32), jax.ShapeDtypeStruct((t, D_INNER), BF16),
                   jax.ShapeDtypeStruct((nc, D_STATE, D_INNER), F32)],
        scratch_shapes=[pltpu.VMEM((D_STATE, D_INNER), F32)],
        compiler_params=_params(("arbitrary",), blk),
    )(xc, proj, dt_raw, dtb, alog, dskip_full, ng, e_bf)


def _ssd_bwd(dyb, y, xc, proj, dt_raw, sprev, dtb, alog, dskip_full, ng, e_bf, h, dproj, name):
    t = xc.shape[0]
    nc = t // CHUNK
    zcb = COL_Z // D_INNER
    hpg = GROUP_W // HEAD_DIM
    rev = lambda i: nc - 1 - i

    def body(dyb_ref, y_ref, xc_ref, z_ref, dt_ref, sprev_ref, dtb_ref, alog_ref, dsk_ref, ng_ref, e_ref, h_ref, dproj_in,
             dz_ref, dxc_ref, ddt_ref, gng_ref, gdsk_ref, galog_ref, gdtb_ref, gwdt_ref, ds_scr, sums_scr):
        del dproj_in

        @pl.when(pl.program_id(0) == 0)
        def _():
            ds_scr[...] = jnp.zeros_like(ds_scr)
            gng_ref[...] = jnp.zeros_like(gng_ref)
            gdsk_ref[...] = jnp.zeros_like(gdsk_ref)
            galog_ref[...] = jnp.zeros_like(galog_ref)
            gdtb_ref[...] = jnp.zeros_like(gdtb_ref)
            gwdt_ref[...] = jnp.zeros_like(gwdt_ref)

        mask = _tri(True)
        tril_bf = mask.astype(BF16)
        triu_bf = _tri(False).astype(BF16)
        e_v = e_ref[...]
        dt_in = dt_ref[...] + dtb_ref[...]
        dtv, a, cs, dt_full, ecs_full, decay_full = _ssd_decays(dt_ref[...], dtb_ref[...], alog_ref[...], e_v, tril_bf)
        cs_t = cs.T

        lane_h = lax.broadcasted_iota(jnp.int32, (CHUNK, DT_PAD), 1)
        sub_h = lax.broadcasted_iota(jnp.int32, (DT_PAD, CHUNK), 0)
        dcs_rows = jnp.zeros((CHUNK, DT_PAD), F32)
        dcs_cols_t = jnp.zeros((DT_PAD, CHUNK), F32)
        last_cols, dsk_cols = [], []
        for g in range(GROUPS):
            gc = slice(g * GROUP_W, (g + 1) * GROUP_W)
            b_cols = slice(D_INNER + g * D_STATE, D_INNER + (g + 1) * D_STATE)
            c_cols = slice(D_INNER + GROUPS * D_STATE + g * D_STATE, D_INNER + GROUPS * D_STATE + (g + 1) * D_STATE)
            xs = xc_ref[:, gc]
            xdt = xs * dt_full[:, gc]
            xdt_b = xdt.astype(BF16)
            xdec = xdt * decay_full[:, gc]
            xdec_b = xdec.astype(BF16)
            zv = z_ref[:, gc].astype(F32)
            sg = _sigmoid(zv)
            gate = zv * sg
            yv = y_ref[:, gc]
            dybv = dyb_ref[:, gc]
            ygate = yv * gate
            rstd = lax.rsqrt(jnp.mean(ygate * ygate, axis=-1, keepdims=True) + NORM_EPS)
            yn = ygate * rstd
            gng_ref[:, gc] += jnp.sum(dybv * yn, axis=0, keepdims=True)
            dyn = dybv * ng_ref[:, gc]
            dyg = rstd * (dyn - yn * jnp.mean(dyn * yn, axis=-1, keepdims=True))
            dz_ref[:, gc] = (dyg * yv * sg * (1.0 + zv * (1.0 - sg))).astype(BF16)
            dy = dyg * gate
            dy_b = dy.astype(BF16)
            dyo = dy * ecs_full[:, gc]
            dyo_b = dyo.astype(BF16)
            dsk_cols.append(jnp.sum(dy * xs, axis=0, keepdims=True))

            bg = xc_ref[:, b_cols].astype(BF16)
            cg = xc_ref[:, c_cols].astype(BF16)
            s_prev = sprev_ref[0, :, gc]
            s_prev_b = s_prev.astype(BF16)
            dsg = ds_scr[:, gc]
            dsg_b = dsg.astype(BF16)
            cb = _dot(cg, bg, _NT)
            c_s = _dot(cg, s_prev_b, _NN)
            b_ds = _dot(bg, dsg_b, _NN)
            dcb = jnp.zeros((CHUNK, CHUNK), F32)
            parts = []
            for r in range(hpg):
                h = g * hpg + r
                hc = slice(r * HEAD_DIM, (r + 1) * HEAD_DIM)
                lmat, m = _head_mats(cs, cs_t, cb, h, mask)
                dm = _dot(dy_b[:, hc], xdt_b[:, hc], _NT)
                parts.append(_dot(m.astype(BF16), dy_b[:, hc], _TN))
                dcb = dcb + dm * lmat
                w = dm * m
                dcs_rows = jnp.where(lane_h == h, jnp.sum(w, axis=1, keepdims=True), dcs_rows)
                dcs_cols_t = jnp.where(sub_h == h, jnp.sum(w, axis=0, keepdims=True), dcs_cols_t)
            dxdt = jnp.concatenate(parts, axis=1) + decay_full[:, gc] * b_ds
            dcb_b = dcb.astype(BF16)
            dxc_ref[:, c_cols] = _dot(dcb_b, bg, _NN) + _dot(dyo_b, s_prev_b, _NT)
            dxc_ref[:, b_cols] = _dot(dcb_b, cg, _TN) + _dot(xdec_b, dsg_b, _NT)
            cdec = ecs_full[CHUNK - 1:CHUNK, gc]
            ds_scr[:, gc] = _dot(cg, dyo_b, _TN) + cdec * dsg
            dxc_ref[:, gc] = dxdt * dt_full[:, gc] + dsk_ref[:, gc] * dy
            dec_prod = xdec * b_ds
            sums_scr[:CHUNK, gc] = dyo * c_s - dec_prod
            sums_scr[CHUNK:, gc] = dxdt * xs
            last_cols.append(jnp.sum(dec_prod, axis=0, keepdims=True) + cdec * jnp.sum(dsg * s_prev, axis=0, keepdims=True))
        t_sums = _head_sums(sums_scr[...], e_v)
        tail = jnp.concatenate([jnp.concatenate(last_cols, axis=1), jnp.concatenate(dsk_cols, axis=1),
                                jnp.zeros((SUBLANES - 2, D_INNER), F32)], axis=0)
        t_tail = _dot_exact_rhs(tail, e_v, _NT)
        gdsk_ref[...] += t_tail[1:2, :]
        row = lax.broadcasted_iota(jnp.int32, (CHUNK, DT_PAD), 0)
        dcs = dcs_rows - dcs_cols_t.T + t_sums[:CHUNK] + jnp.where(row == CHUNK - 1, t_tail[0:1, :], 0.0)
        dda = _dot_exact_lhs(triu_bf, dcs, _NN)
        galog_ref[...] += jnp.sum(dda * dtv, axis=0, keepdims=True) * a
        ddt = dda * a + t_sums[CHUNK:]
        ddt_raw = jnp.where(lane_h < N_HEADS, ddt * _sigmoid(dt_in), 0.0)
        gdtb_ref[...] += jnp.sum(ddt_raw, axis=0, keepdims=True)
        ddt_b = ddt_raw.astype(BF16)
        ddt_ref[...] = ddt_b
        gwdt_ref[...] += _dot(ddt_b, h_ref[...], _TN)

    vec = lambda w: pl.BlockSpec((1, w), lambda i: (0, 0))
    blk = (2 * _nbytes((CHUNK, CONV_DIM), F32) + 4 * _nbytes((CHUNK, D_INNER), F32) + 4 * _nbytes((D_STATE, D_INNER), F32))
    return pl.pallas_call(
        body, name=name, grid=(nc,),
        in_specs=[pl.BlockSpec((CHUNK, D_INNER), lambda i: (rev(i), 0)), pl.BlockSpec((CHUNK, D_INNER), lambda i: (rev(i), 0)),
                  pl.BlockSpec((CHUNK, CONV_DIM), lambda i: (rev(i), 0)), pl.BlockSpec((CHUNK, D_INNER), lambda i: (rev(i), zcb)),
                  pl.BlockSpec((CHUNK, DT_PAD), lambda i: (rev(i), 0)), pl.BlockSpec((1, D_STATE, D_INNER), lambda i: (rev(i), 0, 0)),
                  vec(DT_PAD), vec(DT_PAD), vec(D_INNER), vec(D_INNER), pl.BlockSpec((DT_PAD, D_INNER), lambda i: (0, 0)),
                  pl.BlockSpec((CHUNK, D_MODEL), lambda i: (rev(i), 0)), pl.BlockSpec(memory_space=pl.ANY)],
        out_specs=[pl.BlockSpec((CHUNK, D_INNER), lambda i: (rev(i), zcb)), pl.BlockSpec((CHUNK, CONV_DIM), lambda i: (rev(i), 0)),
                   pl.BlockSpec((CHUNK, DT_PAD), lambda i: (rev(i), 0)), vec(D_INNER), vec(DT_PAD), vec(DT_PAD), vec(DT_PAD),
                   pl.BlockSpec((DT_PAD, D_MODEL), lambda i: (0, 0))],
        out_shape=[jax.ShapeDtypeStruct(dproj.shape, BF16), jax.ShapeDtypeStruct((t, CONV_DIM), F32),
                   jax.ShapeDtypeStruct((t, DT_PAD), BF16), jax.ShapeDtypeStruct((1, D_INNER), F32),
                   jax.ShapeDtypeStruct((1, DT_PAD), F32), jax.ShapeDtypeStruct((1, DT_PAD), F32),
                   jax.ShapeDtypeStruct((1, DT_PAD), F32), jax.ShapeDtypeStruct((DT_PAD, D_MODEL), F32)],
        scratch_shapes=[pltpu.VMEM((D_STATE, D_INNER), F32), pltpu.VMEM((2 * CHUNK, D_INNER), F32)],
        input_output_aliases={12: 0},
        compiler_params=_params(("arbitrary",), blk),
    )(dyb, y, xc, proj, dt_raw, sprev, dtb, alog, dskip_full, ng, e_bf, h, dproj)


def _mesh_pos():
    return lax.axis_index("x"), lax.axis_index("y"), lax.axis_index("c")


def _other_chips(x, y):
    return [(1 - x, y), (x, 1 - y), (1 - x, 1 - y)]


def _all_peers(x, y, c):
    peers = []
    for k in range(1, N_DEV):
        fx, fy, fc = (k >> 2) & 1, (k >> 1) & 1, k & 1
        px, py, pc = x + fx - 2 * x * fx, y + fy - 2 * y * fy, c + fc - 2 * c * fc
        peers.append(((px, py, pc), 4 * px + 2 * py + pc))
    return peers


def _all_gather(shards, name, own_only=()):
    n, n_own = len(shards), len(own_only)

    def body(*refs):
        ins, own_ins = refs[:n], refs[n:n + n_own]
        outs, own_outs = refs[n + n_own:2 * n + n_own], refs[2 * n + n_own:2 * (n + n_own)]
        send_sems, recv_sems, local_sems = refs[2 * (n + n_own):]
        x, y, c = _mesh_pos()
        me, sibling = (x, y, c), (x, y, 1 - c)
        chips = _other_chips(x, y)

        def slot(p):
            return 4 * p[0] + 2 * p[1] + p[2]

        def copy(a, k, block, to, src=None):
            dst = outs[a].at[slot(block)]
            return pltpu.make_async_remote_copy(
                src_ref=dst if src is None else src, dst_ref=dst, send_sem=send_sems.at[a * 7 + k],
                recv_sem=recv_sems.at[a * 7 + k], device_id=to, device_id_type=MESH)

        started = []
        own = []
        for a in range(n_own):
            mine = pltpu.make_async_copy(own_ins[a], own_outs[a].at[slot(me)], local_sems.at[n + a])
            mine.start()
            own.append(mine)
        for a in range(n):
            mine = pltpu.make_async_copy(ins[a], outs[a].at[slot(me)], local_sems.at[a])
            mine.start()
            own.append(mine)
            first = [copy(a, 0, me, sibling, src=ins[a])]
            first += [copy(a, 1 + j, me, (*chip, c), src=ins[a]) for j, chip in enumerate(chips)]
            for cp in first:
                cp.start()
            started += first
        for a in range(n):
            for j, chip in enumerate(chips):
                copy(a, 1 + j, (*chip, c), me).wait_recv()
                fwd = copy(a, 4 + j, (*chip, c), sibling)
                fwd.start()
                started.append(fwd)
        for a in range(n):
            copy(a, 0, sibling, me).wait_recv()
            for j, chip in enumerate(chips):
                copy(a, 4 + j, (*chip, 1 - c), me).wait_recv()
        for cp in started:
            cp.wait_send()
        for mine in own:
            mine.wait()

    return pl.pallas_call(
        body, name=name,
        in_specs=[_HBM] * (n + n_own), out_specs=[_HBM] * (n + n_own),
        out_shape=[jax.ShapeDtypeStruct((N_DEV,) + s.shape, s.dtype) for s in (*shards, *own_only)],
        scratch_shapes=[pltpu.SemaphoreType.DMA((7 * n,)), pltpu.SemaphoreType.DMA((7 * n,)),
                        pltpu.SemaphoreType.DMA((n + n_own,))],
    )(*shards, *own_only)


_SMALL_ROWS = (("norm_mix_g", 8), ("conv_b", 32), ("dt_bias", 1), ("a_log", 1), ("d_skip", 1), ("ssm_norm_g", 16),
               ("v_norm_g", 8), ("v_norm_b", 8), ("w_spatial", 1024), ("b_spatial", 8), ("b_gates", 16), ("norm_mlp_g", 8),
               ("norm_final_g", 8), ("conv_w", 128), ("loss", 1))
_LAST_SMALL = (("norm_mix_g", 8),)


def _packed_rows(table):
    return -(-sum(r for _, r in table) // SUBLANES) * SUBLANES


def _small_offsets(table=_SMALL_ROWS):
    offs, r = {}, 0
    for name, rows in table:
        offs[name] = r
        r += rows
    return offs


def _rows_from(src_ref, dst_ref, r0):
    k, w = src_ref.shape
    if w <= LANES:
        dst_ref[r0:r0 + k, 0:w] = src_ref[...]
        return
    per = w // LANES
    for i in range(k):
        for j in range(per):
            dst_ref[r0 + i * per + j:r0 + i * per + j + 1, :] = src_ref[i:i + 1, j * LANES:(j + 1) * LANES]


def _rows_to(src_ref, r0, dst_ref):
    k, w = dst_ref.shape
    if w <= LANES:
        dst_ref[...] = src_ref[r0:r0 + k, 0:w]
        return
    per = w // LANES
    for i in range(k):
        for j in range(per):
            dst_ref[i:i + 1, j * LANES:(j + 1) * LANES] = src_ref[r0 + i * per + j:r0 + i * per + j + 1, :]


def _pack_small(grads, slot_idx, name):
    names = [n for n, _ in _SMALL_ROWS if n in grads]
    offs = _small_offsets()
    rows = _packed_rows(_SMALL_ROWS)

    def body(slot_ref, *refs):
        del slot_ref
        ins, (packed_ref, land_ref) = refs[:len(names)], refs[len(names):]
        packed_ref[...] = jnp.zeros_like(packed_ref)
        for n, ref in zip(names, ins):
            _rows_from(ref, packed_ref, offs[n])
        land_ref[0] = packed_ref[...]

    whole = lambda shape: pl.BlockSpec(shape, lambda i, slot_ref: (0,) * len(shape))
    grid_spec = pltpu.PrefetchScalarGridSpec(
        num_scalar_prefetch=1, grid=(1,), in_specs=[whole(grads[n].shape) for n in names],
        out_specs=[whole((rows, LANES)), pl.BlockSpec((1, rows, LANES), lambda i, slot_ref: (slot_ref[0], 0, 0))])
    return pl.pallas_call(
        body, name=name, grid_spec=grid_spec,
        out_shape=[jax.ShapeDtypeStruct((rows, LANES), F32), jax.ShapeDtypeStruct((N_DEV, rows, LANES), F32)],
    )(slot_idx, *[grads[n] for n in names])


def _exchange_small(grads, table, name):
    names = [n for n, _ in table]
    offs = _small_offsets(table)
    n_in = len(names)
    packed_rows = _packed_rows(table)

    def body(*refs):
        ins, out_ref = refs[:n_in], refs[n_in]
        packed, send_sems, recv_sems, local_sem = refs[n_in + 1:]
        packed[...] = jnp.zeros_like(packed)
        for n, ref in zip(names, ins):
            _rows_from(ref, packed, offs[n])
        x, y, c = _mesh_pos()
        my_slot = 4 * x + 2 * y + c
        mine = pltpu.make_async_copy(packed, out_ref.at[my_slot], local_sem)
        mine.start()
        copies = []
        for k, (peer, peer_slot) in enumerate(_all_peers(x, y, c)):
            sems = dict(send_sem=send_sems.at[k], recv_sem=recv_sems.at[k], device_id=peer, device_id_type=MESH)
            send = pltpu.make_async_remote_copy(src_ref=packed, dst_ref=out_ref.at[my_slot], **sems)
            send.start()
            copies.append((send, pltpu.make_async_remote_copy(src_ref=packed, dst_ref=out_ref.at[peer_slot], **sems)))
        for send, recv in copies:
            send.wait_send()
            recv.wait_recv()
        mine.wait()

    return pl.pallas_call(
        body, name=name, in_specs=[pl.BlockSpec(memory_space=pltpu.VMEM)] * n_in, out_specs=_HBM,
        out_shape=jax.ShapeDtypeStruct((N_DEV, packed_rows, LANES), F32),
        scratch_shapes=[pltpu.VMEM((packed_rows, LANES), F32), pltpu.SemaphoreType.DMA((N_DEV - 1,)),
                        pltpu.SemaphoreType.DMA((N_DEV - 1,)), pltpu.SemaphoreType.DMA],
    )(*[grads[n] for n in names])


def _swap_with_sibling(grads, name):
    n = len(grads)

    def body(*refs):
        ins, outs = refs[:n], refs[n:2 * n]
        send_sems, recv_sems = refs[2 * n:]
        x, y, c = _mesh_pos()
        copies = []
        for a in range(n):
            for k in range(N_CHIP):
                cp = pltpu.make_async_remote_copy(
                    src_ref=ins[a].at[(1 - c) + 2 * k], dst_ref=outs[a].at[k], send_sem=send_sems.at[a * N_CHIP + k],
                    recv_sem=recv_sems.at[a * N_CHIP + k], device_id=(x, y, 1 - c), device_id_type=MESH)
                cp.start()
                copies.append(cp)
        for cp in copies:
            cp.wait()

    return pl.pallas_call(
        body, name=name, in_specs=[_HBM] * n, out_specs=[_HBM] * n,
        out_shape=[jax.ShapeDtypeStruct((N_CHIP,) + g.shape[1:], g.dtype) for g in grads],
        scratch_shapes=[pltpu.SemaphoreType.DMA((N_CHIP * n,)), pltpu.SemaphoreType.DMA((N_CHIP * n,))],
    )(*grads)


_SEM = pl.BlockSpec(memory_space=pltpu.SEMAPHORE)
_IN_HBM = pl.BlockSpec(memory_space=pltpu.HBM)
_EFFECT = pltpu.SideEffectType.DATAFLOW_SIDE_EFFECTING


def _in_hbm(a):
    return pltpu.with_memory_space_constraint(a, pltpu.HBM)


def _gather_copies(ins, lands, send_sems, recv_sems):
    x, y, c = _mesh_pos()
    my_slot = 4 * x + 2 * y + c
    pairs = []
    for a in range(len(ins)):
        for k, (peer, peer_slot) in enumerate(_all_peers(x, y, c)):
            sems = dict(send_sem=send_sems.at[a * (N_DEV - 1) + k], recv_sem=recv_sems.at[a * (N_DEV - 1) + k],
                        device_id=peer, device_id_type=MESH)
            pairs.append((pltpu.make_async_remote_copy(src_ref=ins[a], dst_ref=lands[a].at[my_slot], **sems),
                          pltpu.make_async_remote_copy(src_ref=ins[a], dst_ref=lands[a].at[peer_slot], **sems)))
    return pairs


def _scatter_copies(ins, lands, send_sems, recv_sems):
    x, y, c = _mesh_pos()
    my_chip = 2 * x + y
    pairs = []
    for a in range(len(ins)):
        for j, chip in enumerate(_other_chips(x, y)):
            there = 2 * chip[0] + chip[1]
            sems = dict(send_sem=send_sems.at[a * 3 + j], recv_sem=recv_sems.at[a * 3 + j],
                        device_id=(*chip, c), device_id_type=MESH)
            pairs.append((pltpu.make_async_remote_copy(src_ref=ins[a].at[there], dst_ref=lands[a].at[my_chip], **sems),
                          pltpu.make_async_remote_copy(src_ref=ins[a].at[my_chip], dst_ref=lands[a].at[there], **sems)))
    return pairs


def _split_start(srcs, lands, copies, per_array, name):
    n = len(srcs)

    def body(*refs):
        ins, land_refs = refs[:n], refs[n:2 * n]
        send_sems, recv_sems = refs[2 * n], refs[2 * n + 1]
        token = refs[-1]
        for send, _ in copies(ins, land_refs, send_sems, recv_sems):
            send.start()
        token[...] = jnp.zeros_like(token)

    outs = pl.pallas_call(
        body, name=name,
        out_shape=(pltpu.SemaphoreType.DMA((per_array * n,)), pltpu.SemaphoreType.DMA((per_array * n,)),
                   *[pltpu.HBM(s.shape, s.dtype) for s in srcs], *[pltpu.HBM(l.shape, l.dtype) for l in lands],
                   jax.ShapeDtypeStruct((SUBLANES, LANES), F32)),
        in_specs=[_IN_HBM] * (2 * n),
        out_specs=(_SEM, _SEM, *[_IN_HBM] * (2 * n), pl.BlockSpec(memory_space=pltpu.VMEM)),
        input_output_aliases={i: 2 + i for i in range(2 * n)},
        compiler_params=pltpu.CompilerParams(has_side_effects=_EFFECT),
    )(*[_in_hbm(s) for s in srcs], *[_in_hbm(l) for l in lands])
    return outs[0], outs[1], list(outs[2:2 + n]), list(outs[2 + n:2 + 2 * n]), outs[-1]


def _split_wait(started, copies, after, name):
    send_sems, recv_sems, srcs, lands, _ = started
    n = len(srcs)

    def body(*refs):
        ins, land_refs = refs[:n], refs[n:2 * n]
        for send, recv in copies(ins, land_refs, refs[2 * n], refs[2 * n + 1]):
            send.wait_send()
            recv.wait_recv()

    outs = pl.pallas_call(
        body, name=name,
        out_shape=(*[pltpu.HBM(s.shape, s.dtype) for s in srcs], *[pltpu.HBM(l.shape, l.dtype) for l in lands]),
        in_specs=[_IN_HBM] * (2 * n) + [_SEM, _SEM, _HBM],
        out_specs=[_IN_HBM] * (2 * n),
        input_output_aliases={i: i for i in range(2 * n)},
        compiler_params=pltpu.CompilerParams(has_side_effects=_EFFECT),
    )(*srcs, *lands, send_sems, recv_sems, after)
    return list(outs[:n]), list(outs[n:])


def _ew_block(rows, cols, slots):
    budget = 8 * 1024 * 1024
    br, bc = rows, cols
    while slots * br * bc * 4 > budget:
        if br % 2 == 0 and (br // 2) % (2 * SUBLANES) == 0:
            br //= 2
        elif bc % 2 == 0 and (bc // 2) % LANES == 0:
            bc //= 2
        else:
            break
    return br, bc


def _add_sibling(grads, recv, c_idx, name):
    _, rows, cols = grads.shape
    br, bc = _ew_block(rows, cols, 3)

    def body(c_ref, g_ref, r_ref, out_ref):
        del c_ref
        out_ref[...] = (g_ref[...].astype(F32) + r_ref[...].astype(F32)).astype(out_ref.dtype)

    grid_spec = pltpu.PrefetchScalarGridSpec(
        num_scalar_prefetch=1, grid=(N_CHIP, rows // br, cols // bc),
        in_specs=[pl.BlockSpec((1, br, bc), lambda k, i, j, c_ref: (c_ref[0] + 2 * k, i, j)),
                  pl.BlockSpec((1, br, bc), lambda k, i, j, c_ref: (k, i, j))],
        out_specs=pl.BlockSpec((1, br, bc), lambda k, i, j, c_ref: (k, i, j)))
    return pl.pallas_call(
        body, name=name, grid_spec=grid_spec, out_shape=jax.ShapeDtypeStruct((N_CHIP, rows, cols), grads.dtype),
        compiler_params=_params(("parallel", "parallel", "parallel"), 3 * _nbytes((br, bc), F32)),
    )(c_idx, grads, recv)


def _adam_math(g, w, m, v):
    m2 = ADAM_B1 * m + (1.0 - ADAM_B1) * g
    v2 = ADAM_B2 * v + (1.0 - ADAM_B2) * (g * g)
    m_hat = m2 * (1.0 / (1.0 - ADAM_B1 ** ADAM_STEP))
    v_hat = v2 * (1.0 / (1.0 - ADAM_B2 ** ADAM_STEP))
    return -ADAM_LR * (m_hat / (jnp.sqrt(v_hat) + ADAM_EPS) + ADAM_WD * w), m2, v2


def _adamw(slots, w, m, v, name, own=None, own_slot=None):
    ns, rows, cols = slots.shape
    br, bc = _ew_block(rows, cols, 2 * ns + 7)

    def update(g, w_ref, m_ref, v_ref, g_ref, d_ref, m2_ref, v2_ref):
        g_ref[...] = g
        d_ref[...], m2_ref[...], v2_ref[...] = _adam_math(g, w_ref[...], m_ref[...], v_ref[...])

    out_shape = [jax.ShapeDtypeStruct((rows, cols), F32)] * 4
    params = _params(("parallel", "parallel"), (2 * ns + 7) * _nbytes((br, bc), F32))
    grid = (rows // br, cols // bc)
    if own is None:
        def body(s_ref, *rest):
            g = s_ref[0].astype(F32)
            for k in range(1, ns):
                g = g + s_ref[k].astype(F32)
            update(g, *rest)

        blk = pl.BlockSpec((br, bc), lambda i, j: (i, j))
        return pl.pallas_call(
            body, name=name, grid=grid,
            in_specs=[pl.BlockSpec((ns, br, bc), lambda i, j: (0, i, j)), blk, blk, blk], out_specs=[blk] * 4,
            out_shape=out_shape, compiler_params=params,
        )(slots, w, m, v)

    def body_own(slot_ref, s_ref, o_ref, *rest):
        g = None
        for k in range(ns):
            term = jnp.where(slot_ref[0] == k, o_ref[k].astype(F32), s_ref[k].astype(F32))
            g = term if g is None else g + term
        update(g, *rest)

    blk = pl.BlockSpec((br, bc), lambda i, j, slot_ref: (i, j))
    stack = pl.BlockSpec((ns, br, bc), lambda i, j, slot_ref: (0, i, j))
    grid_spec = pltpu.PrefetchScalarGridSpec(num_scalar_prefetch=1, grid=grid, in_specs=[stack, stack, blk, blk, blk],
                                             out_specs=[blk] * 4)
    return pl.pallas_call(body_own, name=name, grid_spec=grid_spec, out_shape=out_shape, compiler_params=params,
                          )(own_slot, slots, own, w, m, v)


def _adamw_small(all_g, last_g, params, extra_shapes, name):
    names = [n for n, _ in _SMALL_ROWS if n in params]
    extras = [n for n, _ in _SMALL_ROWS if n not in params]
    offs = _small_offsets()
    n_p = len(names)

    def body(*refs):
        s_ref, last_ref = refs[0], refs[1]
        wmv = refs[2:2 + 3 * n_p]
        outs = refs[2 + 3 * n_p:2 + 7 * n_p]
        extra_refs = refs[2 + 7 * n_p:2 + 7 * n_p + len(extras)]
        summed = refs[-1]
        g, g_last = s_ref[0], last_ref[0]
        for k in range(1, N_DEV):
            g, g_last = g + s_ref[k], g_last + last_ref[k]
        summed[...] = g
        last_offs = _small_offsets(_LAST_SMALL)
        for n, rows in _LAST_SMALL:
            summed[offs[n]:offs[n] + rows, :] = g_last[last_offs[n]:last_offs[n] + rows, :]
        for i, n in enumerate(names):
            w_ref, m_ref, v_ref = wmv[3 * i:3 * i + 3]
            g_ref, d_ref, m2_ref, v2_ref = outs[4 * i:4 * i + 4]
            _rows_to(summed, offs[n], g_ref)
            d_ref[...], m2_ref[...], v2_ref[...] = _adam_math(g_ref[...], w_ref[...], m_ref[...], v_ref[...])
        for n, ref in zip(extras, extra_refs):
            _rows_to(summed, offs[n], ref)

    flat = [a for n in names for a in params[n]]
    out_shape = [jax.ShapeDtypeStruct(params[n][0].shape, F32) for n in names for _ in range(4)]
    out_shape += [jax.ShapeDtypeStruct(s, F32) for s in extra_shapes]
    vmem = pl.BlockSpec(memory_space=pltpu.VMEM)
    res = pl.pallas_call(
        body, name=name, in_specs=[vmem] * (2 + len(flat)), out_specs=[vmem] * len(out_shape), out_shape=out_shape,
        scratch_shapes=[pltpu.VMEM(all_g.shape[1:], F32)],
        compiler_params=pltpu.CompilerParams(vmem_limit_bytes=_vmem_limit(_nbytes(all_g.shape, F32))),
    )(all_g, last_g, *flat)
    return {n: res[4 * i:4 * i + 4] for i, n in enumerate(names)}, res[4 * n_p:]


def _mm_tiles(mode, m, n, k):
    tn = min(n, 1024)
    if mode == "tn":
        return min(m, 1024), tn, min(k, 4096)
    if k <= 1024:
        return min(m, 2048), tn, k
    if k <= 2048:
        return min(m, 1024), tn, k
    if k <= 4096:
        return min(m, 512), tn, k
    return min(m, 1024), tn, 2048


def _local_step(x, target, wts, small, exchange):
    t = x.shape[0]
    assert t % CONV_FWD_TILE == 0 and t % CONV_TILE == 0 and t % GMLP_TILE == 0 and t % (2 * ROW_TILE) == 0, t
    w_main_t, w_dt_t = wts["w_main_t"], wts["w_dt_t"]
    bsp_t = small["b_spatial"].T
    pad32 = lambda a: jnp.pad(a, ((0, 0), (0, DT_PAD - N_HEADS)))
    dtb, alog = pad32(small["dt_bias"]), pad32(small["a_log"])
    dskip_full = jnp.repeat(small["d_skip"], HEAD_DIM, axis=1)
    head_of_col = lax.broadcasted_iota(jnp.int32, (DT_PAD, D_INNER), 1) // HEAD_DIM
    e_bf = (head_of_col == lax.broadcasted_iota(jnp.int32, (DT_PAD, D_INNER), 0)).astype(BF16)

    def mm(a, b, mode, name, **kw):
        if mode == "nn":
            m, k, n = a.shape[0], a.shape[1], b.shape[1]
        elif mode == "nt":
            m, k, n = a.shape[0], a.shape[1], b.shape[0]
        else:
            m, k, n = a.shape[1], a.shape[0], b.shape[1]
        tm, tn, tk = _mm_tiles(mode, m, n, k)
        tm = min(tm, kw.pop("max_tm", tm))
        kw.setdefault("out_dtypes", (BF16,) if mode == "tn" else (F32,))
        if "extra_specs" in kw:
            kw["extra_specs"] = kw["extra_specs"](tm, tn)
        return _matmul(a, b, mode=mode, tm=tm, tn=tn, tk=tk, name=name, **kw)

    def out_tile(tm, tn):
        return (((tm, tn), lambda i, j: (i, j)),)

    def row_tiles(n_tiles, *vectors, gate_logits=False):
        def specs(tm, tn):
            out = [((tm, tn), lambda i, j: (i, j))] * n_tiles
            if gate_logits:
                out += [((tm, D_MODEL), lambda i, j, cb=COL_GATE // D_MODEL + half: (i, cb)) for half in range(2)]
            return tuple(out) + tuple(((1, w), lambda i, j, cb=cb: (0, cb)) for w, cb in vectors)
        return specs

    vec = lambda w: ((1, w), F32, (1, w), lambda i, j: (0, 0))
    fused_tm = 512

    h, dt_raw = _rms_fwd(x, small["norm_mix_g"], w_dt_t, "rms_mix", deps=exchange.begin())
    proj = mm(h, w_main_t, "nt", "proj_main", out_dtypes=(BF16,))
    y_a = _gmlp_fwd(proj, small["v_norm_g"], small["v_norm_b"], small["w_spatial"], bsp_t, "gmlp_fwd")
    pre_conv, xc = _conv_fwd(proj, wts["conv_w"], small["conv_b"], "conv_fwd")
    y_ssd, y_b, sprev = _ssd_fwd(xc, proj, dt_raw, dtb, alog, dskip_full, small["ssm_norm_g"], e_bf, "ssd_fwd")
    wts = {**wts, **exchange.late_weights(y_b)}
    pa = mm(y_a, wts["w_proj_a"], "nn", "proj_a", out_dtypes=(BF16,))
    pb, merged = mm(y_b, wts["w_proj_b"], "nn", "proj_b", epilogue=_merge_epilogue, out_dtypes=(BF16, BF16), max_tm=fused_tm,
                    extras=(pa, proj, proj, small["b_gates"], small["b_gates"]),
                    extra_specs=row_tiles(1, (D_MODEL, 0), (D_MODEL, 1), gate_logits=True))
    x1, h2 = mm(merged, wts["w_out"], "nn", "out_proj", epilogue=_residual_rms_epilogue, out_dtypes=(F32, BF16),
                max_tm=2 * fused_tm, extras=(x, small["norm_mlp_g"]), extra_specs=row_tiles(1, (D_MODEL, 0)))

    def relu_sq(acc, ex, outs, first):
        r = jnp.maximum(acc, 0.0)
        outs[0][...] = (r * r).astype(BF16)

    act = mm(h2, wts["w_mlp_up"], "nn", "mlp_up", epilogue=relu_sq, out_dtypes=(BF16,))
    dx2, dx2_b, g_final, _, loss = mm(
        act, wts["w_mlp_down"], "nn", "mlp_down", epilogue=_loss_epilogue, carry=True,
        out_dtypes=(F32, BF16, vec(D_MODEL), vec(D_MODEL), vec(LANES)),
        extras=(x1, small["norm_final_g"], target), extra_specs=lambda tm, tn: (
            ((tm, tn), lambda i, j: (i, j)), ((1, tn), lambda i, j: (0, 0)), ((tm, tn), lambda i, j: (i, j))))

    def relu_sq_bwd(acc, ex, outs, first):
        outs[0][...] = (acc * 2.0 * jnp.sqrt(ex[0][...].astype(F32))).astype(BF16)

    dup = mm(dx2_b, wts["w_mlp_down"], "nt", "d_act", epilogue=relu_sq_bwd, extras=(act,), extra_specs=out_tile,
             out_dtypes=(BF16,))
    g_down = mm(act, dx2_b, "tn", "g_mlp_down")
    g_up = mm(h2, dup, "tn", "g_mlp_up")
    dx1, dx1_b, g_mlp = mm(
        dup, wts["w_mlp_up"], "nt", "d_h2", epilogue=_rms_bwd_epilogue, carry=True,
        out_dtypes=(F32, BF16, vec(D_MODEL)), extras=(x1, small["norm_mlp_g"], dx2), extra_specs=lambda tm, tn: (
            ((tm, tn), lambda i, j: (i, j)), ((1, tn), lambda i, j: (0, 0)), ((tm, tn), lambda i, j: (i, j))))

    g_out = mm(merged, dx1_b, "tn", "g_out")
    dpa, dpb, dproj, g_bgates = mm(
        dx1_b, wts["w_out"], "nt", "d_merged", epilogue=_merge_bwd_epilogue, carry=True, max_tm=fused_tm,
        out_dtypes=(BF16, BF16, ((t, MAIN_W), BF16, (fused_tm, 2 * D_MODEL), lambda i, j: (i, COL_GATE // (2 * D_MODEL))),
                    vec(2 * D_MODEL)),
        extras=(pa, pb, proj, proj, small["b_gates"], small["b_gates"]),
        extra_specs=row_tiles(2, (D_MODEL, 0), (D_MODEL, 1), gate_logits=True))
    g_pa = mm(y_a, dpa, "tn", "g_proj_a")
    g_pb = mm(y_b, dpb, "tn", "g_proj_b")
    started = exchange.reduce("late", {"w_mlp_down": g_down, "w_mlp_up": g_up, "w_out": g_out, "w_proj_a": g_pa,
                                       "w_proj_b": g_pb})
    dya = mm(dpa, wts["w_proj_a"], "nt", "d_ya", deps=started)
    dyb = mm(dpb, wts["w_proj_b"], "nt", "d_yb")

    dproj, g_wsp, g_bsp_t, g_vg, g_vb = _gmlp_bwd(proj, dya, small["v_norm_g"], small["v_norm_b"], small["w_spatial"],
                                                   bsp_t, dproj, "gmlp_bwd")
    dproj, dxc, ddt, g_ng, g_dskip, g_alog, g_dtb, g_dt_t = _ssd_bwd(dyb, y_ssd, xc, proj, dt_raw, sprev, dtb, alog, dskip_full,
                                                                     small["ssm_norm_g"], e_bf, h, dproj, "ssd_bwd")
    dproj, g_convw, g_convb = _conv_bwd(proj, pre_conv, dxc, wts["conv_w"], dproj, "conv_bwd")

    small_grads = {
        "conv_w": g_convw, "loss": loss,
        "conv_b": g_convb, "dt_bias": g_dtb, "a_log": g_alog, "d_skip": g_dskip, "ssm_norm_g": g_ng,
        "v_norm_g": g_vg, "v_norm_b": g_vb, "w_spatial": g_wsp.reshape(GROUPS * CHUNK, CHUNK), "b_spatial": g_bsp_t.T,
        "b_gates": g_bgates, "norm_mlp_g": g_mlp, "norm_final_g": g_final,
    }
    g_main_t = mm(dproj, h, "tn", "g_in_main", deps=exchange.small(small_grads))
    started = exchange.reduce("in", {"w_in": (g_main_t, g_dt_t.astype(BF16))})

    def input_grad(acc, ex, outs, first):
        x_ref, g_ref, res_ref, ddt_ref, wdt_ref = ex
        gg = jnp.zeros((1, D_MODEL), F32)
        for r in range(acc.shape[0] // ROW_TILE):
            rows = slice(r * ROW_TILE, (r + 1) * ROW_TILE)
            dh = acc[rows] + _dot(ddt_ref[rows, :], wdt_ref[...], _NN)
            dx, gg_r = _rms_pullback(x_ref[rows, :], g_ref[...], dh)
            outs[0][rows, :] = dx + res_ref[rows, :]
            gg = gg + gg_r

        _zero_when(first, outs[1])
        outs[1][...] += gg

    grad_x, g_mix = mm(
        dproj, w_main_t, "nn", "d_h", epilogue=input_grad, deps=started, carry=True,
        out_dtypes=(F32, vec(D_MODEL)), extras=(x, small["norm_mix_g"], dx1, ddt, w_dt_t), extra_specs=lambda tm, tn: (
            ((tm, tn), lambda i, j: (i, j)), ((1, tn), lambda i, j: (0, 0)), ((tm, tn), lambda i, j: (i, j)),
            ((tm, DT_PAD), lambda i, j: (i, 0)), ((DT_PAD, D_MODEL), lambda i, j: (0, 0))))

    return grad_x, g_mix


SHARD_ROWS = (MAIN_W + N_HEADS) // N_DEV
REGROUP_IN = 2048


def _main_rows_of(gathered, name):
    n_dev, shard, d = gathered.shape
    blk = 1024
    nb = MAIN_W // blk

    def first_feature(b):
        return b * blk + (N_HEADS if b * blk >= COL_GATE else 0)

    def body(a_ref, b_ref, out_ref):
        for b in range(nb):
            s0, r0 = divmod(first_feature(b), shard)
            n1 = min(shard - r0, blk)

            @pl.when(pl.program_id(0) == b)
            def _(r0=r0, n1=n1):
                out_ref[0:n1, :] = a_ref[0, r0:r0 + n1, :]
                if n1 < blk:
                    out_ref[n1:blk, :] = b_ref[0, 0:blk - n1, :]

    def slot(b):
        return (b * blk + jnp.where(b * blk >= COL_GATE, N_HEADS, 0)) // shard

    return pl.pallas_call(
        body, name=name, grid=(nb,),
        in_specs=[pl.BlockSpec((1, shard, d), lambda b: (slot(b), 0, 0)),
                  pl.BlockSpec((1, shard, d), lambda b: (jnp.minimum(slot(b) + 1, n_dev - 1), 0, 0))],
        out_specs=pl.BlockSpec((blk, d), lambda b: (b, 0)),
        out_shape=jax.ShapeDtypeStruct((MAIN_W, d), gathered.dtype),
        compiler_params=_params(("parallel",), 3 * _nbytes((shard, d), gathered.dtype)),
    )(gathered, gathered)


def _by_device_rows(g_main_t, g_dt_t, name):
    d = g_main_t.shape[1]
    n_blocks = MAIN_W // REGROUP_IN
    dt_dev, dt_row = divmod(COL_GATE, SHARD_ROWS)

    def main_start(s):
        return s * SHARD_ROWS - (N_HEADS if s > dt_dev else 0)

    def body(a_ref, b_ref, dt_ref, out_ref):
        for s in range(N_DEV):
            m0 = main_start(s)
            k0, off = divmod(m0, REGROUP_IN)
            pieces = []
            if s == dt_dev:
                pieces = [(0, dt_row, m0), (dt_row, N_HEADS, None), (dt_row + N_HEADS, SHARD_ROWS - dt_row - N_HEADS, m0 + dt_row)]
            else:
                pieces = [(0, SHARD_ROWS, m0)]

            @pl.when(pl.program_id(0) == s)
            def _(pieces=pieces, k0=k0):
                for dst, n, src in pieces:
                    if src is None:
                        out_ref[0, dst:dst + n, :] = dt_ref[0:n, :]
                        continue
                    lo = src - k0 * REGROUP_IN
                    n_a = max(0, min(n, REGROUP_IN - lo))
                    if n_a:
                        out_ref[0, dst:dst + n_a, :] = a_ref[lo:lo + n_a, :]
                    if n_a < n:
                        lo_b = max(lo - REGROUP_IN, 0)
                        out_ref[0, dst + n_a:dst + n, :] = b_ref[lo_b:lo_b + n - n_a, :]

    def first_block(s):
        return (s * SHARD_ROWS - jnp.where(s > dt_dev, N_HEADS, 0)) // REGROUP_IN

    return pl.pallas_call(
        body, name=name, grid=(N_DEV,),
        in_specs=[pl.BlockSpec((REGROUP_IN, d), lambda s: (first_block(s), 0)),
                  pl.BlockSpec((REGROUP_IN, d), lambda s: (jnp.minimum(first_block(s) + 1, n_blocks - 1), 0)),
                  pl.BlockSpec((DT_PAD, d), lambda s: (0, 0))],
        out_specs=pl.BlockSpec((1, SHARD_ROWS, d), lambda s: (s, 0, 0)),
        out_shape=jax.ShapeDtypeStruct((N_DEV, SHARD_ROWS, d), g_main_t.dtype),
        compiler_params=_params(("parallel",), 3 * _nbytes((REGROUP_IN, d), g_main_t.dtype)),
    )(g_main_t, g_main_t, g_dt_t)


_LATE = ["w_proj_a", "w_proj_b", "w_out", "w_mlp_up", "w_mlp_down"]
_BY_COLS = ("w_mlp_up",)


class _Exchange:
    def __init__(self, late_shards, late_lands):
        self.late_shards, self.late_lands = late_shards, late_lands
        self.c_idx = lax.axis_index("c").astype(jnp.int32).reshape(1)
        self.chip_idx = (2 * lax.axis_index("x") + lax.axis_index("y")).astype(jnp.int32).reshape(1)
        self.pending = []

    def begin(self):
        self.late = _split_start(self.late_shards, self.late_lands, _gather_copies, N_DEV - 1, "gather_late_start")
        return [self.late[-1]]

    def late_weights(self, after):
        _, lands = _split_wait(self.late, _gather_copies, after, "gather_late_wait")
        whole = {}
        for n, g in zip(_LATE, lands):
            whole[n] = jnp.transpose(g, (1, 0, 2)).reshape(g.shape[1], -1) if n in _BY_COLS else g.reshape(-1, g.shape[2])
        return whole

    def reduce(self, tag, grads):
        names = list(grads)
        by_dev = []
        for n in names:
            g = grads[n]
            if n == "w_in":
                by_dev.append(_by_device_rows(*g, "regroup_g_in"))
            elif n in _BY_COLS:
                by_dev.append(jnp.transpose(g.reshape(g.shape[0], N_DEV, -1), (1, 0, 2)))
            else:
                by_dev.append(g.reshape(N_DEV, -1, g.shape[1]))
        from_sibling = _swap_with_sibling(by_dev, "reduce_cores_" + tag)
        parts = [_add_sibling(g, r, self.c_idx, "add_cores_" + n) for n, g, r in zip(names, by_dev, from_sibling)]
        lands = [lax.empty(p.shape, p.dtype) for p in parts]
        started = _split_start(parts, lands, _scatter_copies, 3, "reduce_chips_start_" + tag)
        self.pending.append((tag, names, started))
        return [started[-1]]

    def small(self, grads):
        dev = 2 * self.chip_idx + self.c_idx
        packed, land = _pack_small(grads, dev, "pack_small")
        self.small_started = _split_start([packed], [land], _gather_copies, N_DEV - 1, "exchange_small_start")
        return [self.small_started[-1]]

    def finish(self, after):
        _, (all_small,) = _split_wait(self.small_started, _gather_copies, after, "exchange_small_wait")
        done = {}
        for tag, names, started in self.pending:
            parts, lands = _split_wait(started, _scatter_copies, after, "reduce_chips_wait_" + tag)
            for n, land, part in zip(names, lands, parts):
                done[n] = (land, part, self.chip_idx)
        return all_small, done


def kernel(x, norm_mix_g, w_in, conv_w, conv_b, dt_bias, a_log, d_skip, ssm_norm_g, v_norm_g, v_norm_b, w_spatial, b_spatial, b_gates, w_proj_a, w_proj_b, w_out, norm_mlp_g, w_mlp_up, w_mlp_down, norm_final_g, loss_target, m_norm_mix_g, m_w_in, m_conv_w, m_conv_b, m_dt_bias, m_a_log, m_d_skip, m_ssm_norm_g, m_v_norm_g, m_v_norm_b, m_w_spatial, m_b_spatial, m_b_gates, m_w_proj_a, m_w_proj_b, m_w_out, m_norm_mlp_g, m_w_mlp_up, m_w_mlp_down, m_norm_final_g, v_norm_mix_g, v_w_in, v_conv_w, v_conv_b, v_dt_bias, v_a_log, v_d_skip, v_ssm_norm_g, v_v_norm_g, v_v_norm_b, v_w_spatial, v_b_spatial, v_b_gates, v_w_proj_a, v_w_proj_b, v_w_out, v_norm_mlp_g, v_w_mlp_up, v_w_mlp_down, v_norm_final_g):
    given = dict(locals())
    names = ["norm_mix_g", "w_in", "conv_w", "conv_b", "dt_bias", "a_log", "d_skip", "ssm_norm_g", "v_norm_g", "v_norm_b",
             "w_spatial", "b_spatial", "b_gates", "w_proj_a", "w_proj_b", "w_out", "norm_mlp_g", "w_mlp_up", "w_mlp_down",
             "norm_final_g"]
    shapes = {n: given[n].shape for n in names}
    dev = 4 * lax.axis_index("x") + 2 * lax.axis_index("y") + lax.axis_index("c")

    shard2d = {"w_in": w_in[0].T, "w_proj_a": w_proj_a[0], "w_proj_b": w_proj_b[0], "w_out": w_out[0],
               "w_mlp_up": w_mlp_up[0], "w_mlp_down": w_mlp_down[0]}
    conv_shard = conv_w.reshape(CONV_WIDTH, -1)
    late_shards = [shard2d[n].astype(BF16) for n in _LATE]
    w_in_all, conv_all, *late_lands = _all_gather([shard2d["w_in"].astype(BF16), conv_shard], "gather_first",
                                                  own_only=late_shards)
    dt_dev, dt_row = divmod(COL_GATE, SHARD_ROWS)
    w_dt_t = jnp.pad(w_in_all[dt_dev, dt_row:dt_row + N_HEADS], ((0, DT_PAD - N_HEADS), (0, 0)))
    wts = {"w_main_t": _main_rows_of(w_in_all, "regroup_w_in"), "w_dt_t": w_dt_t, "conv_w": jnp.transpose(conv_all, (1, 0, 2)).reshape(CONV_WIDTH, -1)}
    small = {"norm_mix_g": norm_mix_g, "conv_b": conv_b, "dt_bias": dt_bias, "a_log": a_log, "d_skip": d_skip,
             "ssm_norm_g": ssm_norm_g, "v_norm_g": v_norm_g, "v_norm_b": v_norm_b, "w_spatial": w_spatial[0],
             "b_spatial": b_spatial[0], "b_gates": b_gates, "norm_mlp_g": norm_mlp_g,
             "norm_final_g": norm_final_g.reshape(1, -1)}

    exchange = _Exchange(late_shards, late_lands)
    grad_x, g_mix = _local_step(x[0], loss_target[0], wts, small, exchange)

    out = {}
    all_small, large = exchange.finish(grad_x)
    for n, (slots, own, own_slot) in large.items():
        moments = [given["m_" + n][0], given["v_" + n][0]]
        if n == "w_in":
            moments = [mom.T for mom in moments]
        res = _adamw(slots, shard2d[n], *moments, "adamw_" + n, own=own, own_slot=own_slot)
        out[n] = [(r.T if n == "w_in" else r).reshape(shapes[n]) for r in res]

    last_small = _exchange_small({"norm_mix_g": g_mix}, _LAST_SMALL, "exchange_last")
    small["w_spatial"] = small["w_spatial"].reshape(GROUPS * CHUNK, CHUNK)
    params = {n: (w2d, given["m_" + n].reshape(w2d.shape), given["v_" + n].reshape(w2d.shape)) for n, w2d in small.items()}
    updated, (g_conv_full, loss_all) = _adamw_small(all_small, last_small, params, [(CONV_WIDTH, CONV_DIM), (1, LANES)],
                                                    "adamw_small")
    for n, res in updated.items():
        out[n] = [r.reshape(shapes[n]) for r in res]
    width = shapes["conv_w"][-1]
    g_conv = lax.dynamic_slice(g_conv_full, (0, dev * width), (CONV_WIDTH, width))
    res = _adamw(g_conv[None], conv_shard, m_conv_w.reshape(CONV_WIDTH, -1), v_conv_w.reshape(CONV_WIDTH, -1), "adamw_conv_w")
    out["conv_w"] = [r.reshape(shapes["conv_w"]) for r in res]

    loss = loss_all[0, 0]
    return (loss, grad_x[None], *[out[n][0] for n in names], *[out[n][1] for n in names],
            *[out[n][2] for n in names], *[out[n][3] for n in names])
```

```python
import functools
import math

import jax
import jax.numpy as jnp
from jax import lax
from jax.experimental import pallas as pl
from jax.experimental.pallas import tpu as pltpu

F32 = jnp.float32
BF16 = jnp.bfloat16
MESH = pl.DeviceIdType.MESH

D_MODEL = 1024
NORM_EPS = 1e-6
CHUNK = 128
GROUPS = 8
D_INNER = 2048
HEAD_DIM = 64
N_HEADS = 32
D_STATE = 128
CONV_WIDTH = 4
CONV_DIM = 4096
D_FF = 4096
GROUP_W = D_INNER // GROUPS
N_DEV = 8
N_CHIP = 4

ADAM_LR = 0.001
ADAM_B1 = 0.9
ADAM_B2 = 0.999
ADAM_EPS = 1e-08
ADAM_WD = 0.01
ADAM_STEP = 10

MAIN_W = 2 * D_MODEL + D_INNER + CONV_DIM + 2 * D_MODEL
COL_Z = 2048
COL_XBC = 4096
COL_GATE = 8192
DT_PAD = 128

LANES = 128
SUBLANES = 8
VMEM_BYTES_V7X = 64 * 1024 * 1024
VMEM_BODY_TEMP = 24 * 1024 * 1024


def _vmem_limit(block_bytes):
    return int(min(2 * block_bytes + VMEM_BODY_TEMP, VMEM_BYTES_V7X - 8 * 1024 * 1024))


def _nbytes(shape, dtype):
    return math.prod(shape) * jnp.dtype(dtype).itemsize


_HBM = pl.BlockSpec(memory_space=pl.ANY)


def _params(sem, block_bytes):
    return pltpu.CompilerParams(dimension_semantics=sem, vmem_limit_bytes=_vmem_limit(block_bytes))


def _sigmoid(x):
    return 1.0 / (1.0 + jnp.exp(-x))


def _softplus(x):
    e = jnp.exp(-jnp.abs(x))
    u = 1.0 + e
    log1p_e = jnp.where(u == 1.0, e, jnp.log(u) * (e / jnp.where(u == 1.0, 1.0, u - 1.0)))
    return jnp.maximum(x, 0.0) + log1p_e


_SQRT_HALF = 0.7071067811865476
_INV_SQRT_2PI = 0.3989422804014327


def _normal_cdf(x):
    return 0.5 * (1.0 + lax.erf(x * _SQRT_HALF))


def _gelu_grad(x, cdf):
    return cdf + x * jnp.exp(-0.5 * x * x) * _INV_SQRT_2PI


def _dot(a, b, dims):
    return lax.dot_general(a, b, (dims, ((), ())), preferred_element_type=F32)


_NN = ((1,), (0,))
_NT = ((1,), (1,))
_TN = ((0,), (0,))


def _split3(x):
    hi = x.astype(BF16)
    r1 = x - hi.astype(F32)
    mid = r1.astype(BF16)
    lo = (r1 - mid.astype(F32)).astype(BF16)
    return hi, mid, lo


def _dot_exact_rhs(x, e, dims):
    hi, mid, lo = _split3(x)
    return _dot(hi, e, dims) + _dot(mid, e, dims) + _dot(lo, e, dims)


def _dot_exact_lhs(e, x, dims):
    hi, mid, lo = _split3(x)
    return _dot(e, hi, dims) + _dot(e, mid, dims) + _dot(e, lo, dims)


def _tri(lower):
    r = lax.broadcasted_iota(jnp.int32, (CHUNK, CHUNK), 0)
    c = lax.broadcasted_iota(jnp.int32, (CHUNK, CHUNK), 1)
    return (r >= c) if lower else (r <= c)


def _matmul(a, b, *, mode, tm, tn, tk, out_dtypes, name, epilogue=None, extras=(), extra_specs=(), deps=(),
            carry=False):
    if mode == "nn":
        (m, k), (_, n) = a.shape, b.shape
    elif mode == "nt":
        (m, k), (n, _) = a.shape, b.shape
    else:
        (k, m), (_, n) = a.shape, b.shape
    assert m % tm == 0 and n % tn == 0 and k % tk == 0, (name, m, n, k, tm, tn, tk)
    nk = k // tk
    n_extra, n_out = len(extras), len(out_dtypes)
    first_out = 2 + n_extra + len(deps)
    dims = {"nn": _NN, "nt": _NT, "tn": _TN}[mode]
    if epilogue is None:
        def epilogue(acc, ex, outs, first):
            outs[0][...] = acc.astype(outs[0].dtype)

    def body(*refs):
        a_ref, b_ref = refs[0], refs[1]
        ex_refs = refs[2:2 + n_extra]
        outs = refs[first_out:first_out + n_out]
        first_tile = pl.program_id(0) == 0
        p = _dot(a_ref[...], b_ref[...], dims)
        if nk == 1:
            epilogue(p, ex_refs, outs, first_tile)
            return
        acc_ref = refs[first_out + n_out]
        kk = pl.program_id(2)

        @pl.when(kk == 0)
        def _():
            acc_ref[...] = p

        @pl.when(kk > 0)
        def _():
            acc_ref[...] += p

        @pl.when(kk == nk - 1)
        def _():
            epilogue(acc_ref[...], ex_refs, outs, first_tile)

    grid = (m // tm, n // tn, nk)

    if mode == "nn":
        a_spec = pl.BlockSpec((tm, tk), (lambda i, j, kk: (i, kk)))
        b_spec = pl.BlockSpec((tk, tn), (lambda i, j, kk: (kk, j)))
        a_blk, b_blk = (tm, tk), (tk, tn)
    elif mode == "nt":
        a_spec = pl.BlockSpec((tm, tk), (lambda i, j, kk: (i, kk)))
        b_spec = pl.BlockSpec((tn, tk), (lambda i, j, kk: (j, kk)))
        a_blk, b_blk = (tm, tk), (tn, tk)
    else:
        a_spec = pl.BlockSpec((tk, tm), (lambda i, j, kk: (kk, i)))
        b_spec = pl.BlockSpec((tk, tn), (lambda i, j, kk: (kk, j)))
        a_blk, b_blk = (tk, tm), (tk, tn)
    ex_specs = [pl.BlockSpec(shape, (lambda i, j, kk, f=f: f(i, j))) for shape, f in extra_specs]
    outs = [o if isinstance(o, tuple) else ((m, n), o, (tm, tn), lambda i, j: (i, j)) for o in out_dtypes]
    out_spec = [pl.BlockSpec(blk_shape, (lambda i, j, kk, f=f: f(i, j))) for _, _, blk_shape, f in outs]
    out_shape = [jax.ShapeDtypeStruct(shape, dt) for shape, dt, _, _ in outs]
    blk = (_nbytes(a_blk, a.dtype) + _nbytes(b_blk, b.dtype) + sum(_nbytes(s, F32) for s, _ in extra_specs)
           + sum(_nbytes(blk_shape, dt) for _, dt, blk_shape, _ in outs) + _nbytes((tm, tn), F32))
    order = ("arbitrary",) * 3 if carry else ("parallel", "parallel", "arbitrary")
    res = pl.pallas_call(
        body, name=name, grid=grid,
        in_specs=[a_spec, b_spec] + ex_specs + [_HBM] * len(deps), out_specs=out_spec, out_shape=out_shape,
        scratch_shapes=[pltpu.VMEM((tm, tn), F32)] if nk > 1 else [],
        compiler_params=_params(order, blk),
    )(a, b, *extras, *deps)
    return res[0] if n_out == 1 else res


ROW_TILE = 256


def _row_spec(width, col_block=0, tile=ROW_TILE):
    return pl.BlockSpec((tile, width), lambda i, cb=col_block: (i, cb))


def _vec_spec(width, col_block=0):
    return pl.BlockSpec((1, width), lambda i, cb=col_block: (0, cb))


def _rms_fwd(x, g, w_t, name, deps=()):
    t = x.shape[0]
    n_small = w_t.shape[0]
    tile = 2 * ROW_TILE

    def body(x_ref, g_ref, w_ref, *rest):
        h_ref, small_ref = rest[-2:]
        xv = x_ref[...]
        r = lax.rsqrt(jnp.mean(xv * xv, axis=-1, keepdims=True) + NORM_EPS)
        h = (xv * r * g_ref[...]).astype(BF16)
        h_ref[...] = h
        small_ref[...] = _dot(h, w_ref[...], _NT)

    return pl.pallas_call(
        body, name=name, grid=(t // tile,),
        in_specs=[_row_spec(D_MODEL, 0, tile), _vec_spec(D_MODEL), pl.BlockSpec((n_small, D_MODEL), lambda i: (0, 0))]
        + [_HBM] * len(deps),
        out_specs=[_row_spec(D_MODEL, 0, tile), _row_spec(n_small, 0, tile)],
        out_shape=[jax.ShapeDtypeStruct((t, D_MODEL), BF16), jax.ShapeDtypeStruct((t, n_small), F32)],
        compiler_params=_params(("parallel",), 3 * _nbytes((tile, D_MODEL), F32)),
    )(x, g, w_t, *deps)


def _rms_scale(xv):
    r = lax.rsqrt(jnp.mean(xv * xv, axis=-1, keepdims=True) + NORM_EPS)
    return r, xv * r


def _rms_pullback(xv, g, dh):
    r, xh = _rms_scale(xv)
    dyg = dh * g
    return r * (dyg - xh * jnp.mean(dyg * xh, axis=-1, keepdims=True)), jnp.sum(dh * xh, axis=0, keepdims=True)


def _zero_when(first, *refs):
    @pl.when(first)
    def _():
        for ref in refs:
            ref[...] = jnp.zeros_like(ref)


def _residual_rms_epilogue(acc, ex, outs, first):
    x1 = acc + ex[0][...]
    outs[0][...] = x1
    _, xh = _rms_scale(x1)
    outs[1][...] = (xh * ex[1][...]).astype(BF16)


def _loss_epilogue(acc, ex, outs, first):
    dx_ref, dxb_ref, gg_ref, sq_ref, tot_ref = outs
    gv = ex[1][...]
    r, xh = _rms_scale(acc + ex[0][...])
    err = xh * gv - ex[2][...]
    dy = err * (1.0 / D_MODEL)
    dyg = dy * gv
    dx = r * (dyg - xh * jnp.mean(dyg * xh, axis=-1, keepdims=True))
    dx_ref[...] = dx
    dxb_ref[...] = dx.astype(BF16)

    _zero_when(first, gg_ref, sq_ref)
    gg_ref[...] += jnp.sum(dy * xh, axis=0, keepdims=True)
    sq_ref[...] += jnp.sum(err * err, axis=0, keepdims=True)
    tot_ref[...] = jnp.broadcast_to(jnp.sum(sq_ref[...], axis=1, keepdims=True) * (0.5 / D_MODEL), tot_ref.shape)


def _rms_bwd_epilogue(dh, ex, outs, first):
    dx, gg = _rms_pullback(ex[0][...], ex[1][...], dh)
    dx = dx + ex[2][...]
    outs[0][...] = dx
    if len(outs) == 3:
        outs[1][...] = dx.astype(BF16)

    _zero_when(first, outs[-1])
    outs[-1][...] += gg


def _merge_epilogue(acc, ex, outs, first):
    outs[0][...] = acc.astype(outs[0].dtype)
    ga = _sigmoid(ex[1][...].astype(F32) + ex[3][...])
    gb = _sigmoid(ex[2][...].astype(F32) + ex[4][...])
    outs[1][...] = (ga * ex[0][...].astype(F32) + gb * acc).astype(BF16)


def _merge_bwd_epilogue(dm, ex, outs, first):
    dpa_ref, dpb_ref, dgl_ref, gb_ref = outs
    ga = _sigmoid(ex[2][...].astype(F32) + ex[4][...])
    gb = _sigmoid(ex[3][...].astype(F32) + ex[5][...])
    dpa_ref[...] = (dm * ga).astype(BF16)
    dpb_ref[...] = (dm * gb).astype(BF16)
    dla = dm * ex[0][...].astype(F32) * ga * (1.0 - ga)
    dlb = dm * ex[1][...].astype(F32) * gb * (1.0 - gb)
    dgl_ref[:, :D_MODEL] = dla.astype(BF16)
    dgl_ref[:, D_MODEL:] = dlb.astype(BF16)

    _zero_when(first, gb_ref)
    gb_ref[:, :D_MODEL] += jnp.sum(dla, axis=0, keepdims=True)
    gb_ref[:, D_MODEL:] += jnp.sum(dlb, axis=0, keepdims=True)


GMLP_TILE = 512
GMLP_NC = GMLP_TILE // CHUNK


def _gmlp_common(u_pre, v_pre, vg, vb):
    cdf_u, cdf_v = _normal_cdf(u_pre), _normal_cdf(v_pre)
    u = u_pre * cdf_u
    v = v_pre * cdf_v
    mu = jnp.mean(v, axis=-1, keepdims=True)
    vc = v - mu
    rstd = lax.rsqrt(jnp.mean(vc * vc, axis=-1, keepdims=True) + NORM_EPS)
    vh = vc * rstd
    vn = vh * vg + vb
    return u, vh, vn, rstd, cdf_u, cdf_v


def _chunks_to_lanes(x, g):
    return jnp.concatenate([x[c * CHUNK:(c + 1) * CHUNK, g * CHUNK:(g + 1) * CHUNK] for c in range(GMLP_NC)], axis=1)


def _gmlp_fwd(proj, vg, vb, wsp, bsp_t, name):
    t = proj.shape[0]

    def body(u_ref, v_ref, vg_ref, vb_ref, w_ref, b_ref, ya_ref):
        u, _, vn, _, _, _ = _gmlp_common(u_ref[...].astype(F32), v_ref[...].astype(F32), vg_ref[...], vb_ref[...])
        mask = _tri(True)
        bt = b_ref[...]
        for g in range(GROUPS):
            w = jnp.where(mask, w_ref[g], 0.0).astype(BF16)
            vcat = _chunks_to_lanes(vn, g).astype(BF16)
            s = _dot(w, vcat, _NN) + bt[:, g:g + 1]
            for c in range(GMLP_NC):
                rows, cols = slice(c * CHUNK, (c + 1) * CHUNK), slice(g * CHUNK, (g + 1) * CHUNK)
                ya_ref[rows, cols] = (u[rows, cols] * s[:, c * CHUNK:(c + 1) * CHUNK]).astype(BF16)

    return pl.pallas_call(
        body, name=name, grid=(t // GMLP_TILE,),
        in_specs=[_row_spec(D_MODEL, 0, GMLP_TILE), _row_spec(D_MODEL, 1, GMLP_TILE), _vec_spec(D_MODEL),
                  _vec_spec(D_MODEL), pl.BlockSpec((GROUPS, CHUNK, CHUNK), lambda i: (0, 0, 0)),
                  pl.BlockSpec((CHUNK, GROUPS), lambda i: (0, 0))],
        out_specs=_row_spec(D_MODEL, 0, GMLP_TILE),
        out_shape=jax.ShapeDtypeStruct((t, D_MODEL), BF16),
        compiler_params=_params(("parallel",), 3 * _nbytes((GMLP_TILE, D_MODEL), F32)),
    )(proj, proj, vg, vb, wsp, bsp_t)


def _gmlp_bwd(proj, dya, vg, vb, wsp, bsp_t, dproj, name):
    t = proj.shape[0]

    def body(u_ref, v_ref, dya_ref, vg_ref, vb_ref, w_ref, b_ref, dproj_in, duv_ref, gw_ref, gbt_ref, gvg_ref, gvb_ref,
             dvn_scr, du_scr):
        del dproj_in
        u_pre, v_pre = u_ref[...].astype(F32), v_ref[...].astype(F32)
        vgv = vg_ref[...]
        u, vh, vn, rstd, cdf_u, cdf_v = _gmlp_common(u_pre, v_pre, vgv, vb_ref[...])
        dya = dya_ref[...].astype(F32)
        mask = _tri(True)
        bt = b_ref[...]
        first = pl.program_id(0) == 0

        @pl.when(first)
        def _():
            gw_ref[...] = jnp.zeros_like(gw_ref)
            gbt_ref[...] = jnp.zeros_like(gbt_ref)
            gvg_ref[...] = jnp.zeros_like(gvg_ref)
            gvb_ref[...] = jnp.zeros_like(gvb_ref)

        lane = lax.broadcasted_iota(jnp.int32, (CHUNK, GROUPS), 1)
        gbt = jnp.zeros((CHUNK, GROUPS), F32)
        for g in range(GROUPS):
            w = jnp.where(mask, w_ref[g], 0.0).astype(BF16)
            vcat = _chunks_to_lanes(vn, g).astype(BF16)
            s = _dot(w, vcat, _NN) + bt[:, g:g + 1]
            ds = _chunks_to_lanes(dya * u, g)
            gbt = jnp.where(lane == g, jnp.sum(ds, axis=1, keepdims=True), gbt)
            dsb = ds.astype(BF16)
            gw_ref[g] += jnp.where(mask, _dot(dsb, vcat, _NT), 0.0)
            dv = _dot(w, dsb, _TN)
            for c in range(GMLP_NC):
                rows, cols = slice(c * CHUNK, (c + 1) * CHUNK), slice(g * CHUNK, (g + 1) * CHUNK)
                dvn_scr[rows, cols] = dv[:, c * CHUNK:(c + 1) * CHUNK]
                du_scr[rows, cols] = dya[rows, cols] * s[:, c * CHUNK:(c + 1) * CHUNK]
        gbt_ref[...] += gbt
        dvn = dvn_scr[...]
        gvg_ref[...] += jnp.sum(dvn * vh, axis=0, keepdims=True)
        gvb_ref[...] += jnp.sum(dvn, axis=0, keepdims=True)
        dvh = dvn * vgv
        dv = rstd * (dvh - jnp.mean(dvh, axis=-1, keepdims=True) - vh * jnp.mean(dvh * vh, axis=-1, keepdims=True))
        duv_ref[:, :D_MODEL] = (du_scr[...] * _gelu_grad(u_pre, cdf_u)).astype(BF16)
        duv_ref[:, D_MODEL:] = (dv * _gelu_grad(v_pre, cdf_v)).astype(BF16)

    return pl.pallas_call(
        body, name=name, grid=(t // GMLP_TILE,),
        in_specs=[_row_spec(D_MODEL, 0, GMLP_TILE), _row_spec(D_MODEL, 1, GMLP_TILE), _row_spec(D_MODEL, 0, GMLP_TILE),
                  _vec_spec(D_MODEL), _vec_spec(D_MODEL), pl.BlockSpec((GROUPS, CHUNK, CHUNK), lambda i: (0, 0, 0)),
                  pl.BlockSpec((CHUNK, GROUPS), lambda i: (0, 0)), pl.BlockSpec(memory_space=pl.ANY)],
        out_specs=[_row_spec(2 * D_MODEL, 0, GMLP_TILE), pl.BlockSpec((GROUPS, CHUNK, CHUNK), lambda i: (0, 0, 0)),
                   pl.BlockSpec((CHUNK, GROUPS), lambda i: (0, 0)), _vec_spec(D_MODEL), _vec_spec(D_MODEL)],
        out_shape=[jax.ShapeDtypeStruct(dproj.shape, BF16), jax.ShapeDtypeStruct((GROUPS, CHUNK, CHUNK), F32),
                   jax.ShapeDtypeStruct((CHUNK, GROUPS), F32), jax.ShapeDtypeStruct((1, D_MODEL), F32),
                   jax.ShapeDtypeStruct((1, D_MODEL), F32)],
        scratch_shapes=[pltpu.VMEM((GMLP_TILE, D_MODEL), F32), pltpu.VMEM((GMLP_TILE, D_MODEL), F32)],
        input_output_aliases={7: 0},
        compiler_params=_params(("arbitrary",), 6 * _nbytes((GMLP_TILE, D_MODEL), F32)),
    )(proj, proj, dya, vg, vb, wsp, bsp_t, dproj)


CONV_TILE = 1024
CONV_FWD_TILE = 2048
CONV_COLS = 1024
CONV_RB = 32
HALO = SUBLANES


def _conv_fwd(proj, cw, cb, name):
    t = proj.shape[0]
    nj = CONV_DIM // CONV_COLS
    xcb = COL_XBC // CONV_COLS
    before = 2 * HALO
    rb = CONV_FWD_TILE // before

    def body(x_ref, prev_ref, cw_ref, cb_ref, pre_ref, xc_ref):
        i = pl.program_id(1)
        cw_v = cw_ref[...]
        cb_v = cb_ref[...]
        for b in range(CONV_FWD_TILE // CONV_RB):
            if b == 0:
                prev = jnp.where(i > 0, prev_ref[...].astype(F32)[HALO:, :], 0.0)
                ext = jnp.concatenate([prev, x_ref[:CONV_RB, :].astype(F32)], axis=0)
            else:
                ext = x_ref[b * CONV_RB - before:(b + 1) * CONV_RB, :].astype(F32)[HALO:, :]
            pre = cb_v + cw_v[CONV_WIDTH - 1:CONV_WIDTH, :] * ext[HALO:, :]
            for k in range(CONV_WIDTH - 1):
                back = CONV_WIDTH - 1 - k
                pre = pre + cw_v[k:k + 1, :] * pltpu.roll(ext, back, 0)[HALO:, :]
            pre_ref[b * CONV_RB:(b + 1) * CONV_RB, :] = pre
            xc_ref[b * CONV_RB:(b + 1) * CONV_RB, :] = pre * _sigmoid(pre)

    tile = pl.BlockSpec((CONV_FWD_TILE, CONV_COLS), lambda j, i: (i, j))
    return pl.pallas_call(
        body, name=name, grid=(nj, t // CONV_FWD_TILE),
        in_specs=[pl.BlockSpec((CONV_FWD_TILE, CONV_COLS), lambda j, i: (i, xcb + j)),
                  pl.BlockSpec((before, CONV_COLS), lambda j, i: (jnp.maximum(i * rb - 1, 0), xcb + j)),
                  pl.BlockSpec((CONV_WIDTH, CONV_COLS), lambda j, i: (0, j)),
                  pl.BlockSpec((1, CONV_COLS), lambda j, i: (0, j))],
        out_specs=[tile, tile],
        out_shape=[jax.ShapeDtypeStruct((t, CONV_DIM), F32), jax.ShapeDtypeStruct((t, CONV_DIM), F32)],
        compiler_params=_params(("parallel", "parallel"), 4 * _nbytes((CONV_FWD_TILE, CONV_COLS), F32)),
    )(proj, proj, cw, cb)


def _fold_rows(v):
    out = v[:SUBLANES]
    for r in range(1, v.shape[0] // SUBLANES):
        out = out + v[r * SUBLANES:(r + 1) * SUBLANES]
    return out


def _conv_bwd(proj, pre, dxc, cw, dproj, name):
    t = proj.shape[0]
    nj = CONV_DIM // CONV_COLS
    ni = t // CONV_TILE
    xcb = COL_XBC // CONV_COLS
    rb = CONV_TILE // HALO
    last_rb = t // HALO - 1

    def body(x_ref, p_ref, pnext_ref, d_ref, dnext_ref, cw_ref, dproj_in, dx_ref, gw_ref, gb_ref):
        del dproj_in
        i = pl.program_id(1)
        cw_v = cw_ref[...]

        def dpre_of(p, d):
            sg = _sigmoid(p)
            return d * sg * (1.0 + p * (1.0 - sg))

        @pl.when(i == 0)
        def _():
            gw_ref[...] = jnp.zeros_like(gw_ref)
            gb_ref[...] = jnp.zeros_like(gb_ref)

        head = dpre_of(pnext_ref[...], jnp.where(i < ni - 1, dnext_ref[...], 0.0))
        gb_acc = jnp.zeros((SUBLANES, CONV_COLS), F32)
        gw_acc = [jnp.zeros((SUBLANES, CONV_COLS), F32) for _ in range(CONV_WIDTH)]
        for b in reversed(range(CONV_TILE // CONV_RB)):
            rows = slice(b * CONV_RB, (b + 1) * CONV_RB)
            cur = dpre_of(p_ref[rows, :], d_ref[rows, :])
            ext = jnp.concatenate([cur, head], axis=0)
            xv = x_ref[rows, :].astype(F32)
            dx = None
            for k in range(CONV_WIDTH):
                shift = CONV_WIDTH - 1 - k
                win = cur if shift == 0 else pltpu.roll(ext, CONV_RB + HALO - shift, 0)[:CONV_RB, :]
                term = cw_v[k:k + 1, :] * win
                dx = term if dx is None else dx + term
                gw_acc[k] = gw_acc[k] + _fold_rows(win * xv)
            dx_ref[rows, :] = dx.astype(BF16)
            gb_acc = gb_acc + _fold_rows(cur)
            head = cur[:HALO]
        gb_ref[...] += jnp.sum(gb_acc, axis=0, keepdims=True)
        for k in range(CONV_WIDTH):
            gw_ref[k:k + 1, :] += jnp.sum(gw_acc[k], axis=0, keepdims=True)

    tile = pl.BlockSpec((CONV_TILE, CONV_COLS), lambda j, i: (i, j))
    after = pl.BlockSpec((HALO, CONV_COLS), lambda j, i: (jnp.minimum((i + 1) * rb, last_rb), j))
    return pl.pallas_call(
        body, name=name, grid=(nj, ni),
        in_specs=[pl.BlockSpec((CONV_TILE, CONV_COLS), lambda j, i: (i, xcb + j)), tile, after, tile, after,
                  pl.BlockSpec((CONV_WIDTH, CONV_COLS), lambda j, i: (0, j)),
                  pl.BlockSpec(memory_space=pl.ANY)],
        out_specs=[pl.BlockSpec((CONV_TILE, CONV_COLS), lambda j, i: (i, xcb + j)),
                   pl.BlockSpec((CONV_WIDTH, CONV_COLS), lambda j, i: (0, j)),
                   pl.BlockSpec((1, CONV_COLS), lambda j, i: (0, j))],
        out_shape=[jax.ShapeDtypeStruct(dproj.shape, BF16), jax.ShapeDtypeStruct((CONV_WIDTH, CONV_DIM), F32),
                   jax.ShapeDtypeStruct((1, CONV_DIM), F32)],
        input_output_aliases={6: 0},
        compiler_params=_params(("parallel", "arbitrary"), 4 * _nbytes((CONV_TILE, CONV_COLS), F32)),
    )(proj, pre, pre, dxc, dxc, cw, dproj)


def _ssd_decays(dt_raw, dtb, alog, e_bf, tril_bf):
    dtv = _softplus(dt_raw + dtb)
    a = -jnp.exp(alog)
    cs = _dot_exact_lhs(tril_bf, dtv * a, _NN)
    cs_last = cs[CHUNK - 1:CHUNK, :]
    stack = jnp.concatenate([dtv, jnp.exp(cs), jnp.exp(cs_last - cs)], axis=0)
    full = _head_expand(stack, e_bf)
    return dtv, a, cs, full[:CHUNK], full[CHUNK:2 * CHUNK], full[2 * CHUNK:]


def _split2(x):
    hi = x.astype(BF16)
    return hi, (x - hi.astype(F32)).astype(BF16)


def _head_expand(x, e_bf):
    hi, mid = _split2(x)
    return _dot(hi, e_bf, _NN) + _dot(mid, e_bf, _NN)


def _head_sums(x, e_bf):
    hi, mid = _split2(x)
    return _dot(hi, e_bf, _NT) + _dot(mid, e_bf, _NT)


def _head_mats(cs, cs_t, cb, h, mask):
    seg = cs[:, h:h + 1] - cs_t[h:h + 1, :]
    lmat = jnp.exp(jnp.where(mask, seg, -jnp.inf))
    return lmat, cb * lmat


def _ssd_fwd(xc, proj, dt_raw, dtb, alog, dskip_full, ng, e_bf, name):
    t = xc.shape[0]
    nc = t // CHUNK
    zcb = COL_Z // D_INNER

    def body(xc_ref, z_ref, dt_ref, dtb_ref, alog_ref, dsk_ref, ng_ref, e_ref, y_ref, yb_ref, sprev_ref, s_scr):
        @pl.when(pl.program_id(0) == 0)
        def _():
            s_scr[...] = jnp.zeros_like(s_scr)

        mask = _tri(True)
        tril_bf = mask.astype(BF16)
        e_v = e_ref[...]
        _, _, cs, dt_full, ecs_full, decay_full = _ssd_decays(dt_ref[...], dtb_ref[...], alog_ref[...], e_v, tril_bf)
        cs_t = cs.T
        sprev_ref[0] = s_scr[...]
        for g in range(GROUPS):
            gc = slice(g * GROUP_W, (g + 1) * GROUP_W)
            xs = xc_ref[:, gc]
            xdt = xs * dt_full[:, gc]
            xdt_b = xdt.astype(BF16)
            xdec = (xdt * decay_full[:, gc]).astype(BF16)
            bg = xc_ref[:, D_INNER + g * D_STATE:D_INNER + (g + 1) * D_STATE].astype(BF16)
            cg = xc_ref[:, D_INNER + GROUPS * D_STATE + g * D_STATE:D_INNER + GROUPS * D_STATE + (g + 1) * D_STATE].astype(BF16)
            cb = _dot(cg, bg, _NT)
            s_prev = s_scr[:, gc]
            y_off = ecs_full[:, gc] * _dot(cg, s_prev.astype(BF16), _NN)
            s_scr[:, gc] = s_prev * ecs_full[CHUNK - 1:CHUNK, gc] + _dot(bg, xdec, _TN)
            parts = []
            for r in range(GROUP_W // HEAD_DIM):
                h = g * (GROUP_W // HEAD_DIM) + r
                _, m = _head_mats(cs, cs_t, cb, h, mask)
                parts.append(_dot(m.astype(BF16), xdt_b[:, r * HEAD_DIM:(r + 1) * HEAD_DIM], _NN))
            yg = jnp.concatenate(parts, axis=1) + y_off + dsk_ref[:, gc] * xs
            y_ref[:, gc] = yg
            zv = z_ref[:, gc].astype(F32)
            ygate = yg * (zv * _sigmoid(zv))
            rstd = lax.rsqrt(jnp.mean(ygate * ygate, axis=-1, keepdims=True) + NORM_EPS)
            yb_ref[:, gc] = (ygate * rstd * ng_ref[:, gc]).astype(BF16)

    vec = lambda w: pl.BlockSpec((1, w), lambda i: (0, 0))
    blk = _nbytes((CHUNK, CONV_DIM), F32) + 3 * _nbytes((CHUNK, D_INNER), F32) + _nbytes((D_STATE, D_INNER), F32)
    return pl.pallas_call(
        body, name=name, grid=(nc,),
        in_specs=[pl.BlockSpec((CHUNK, CONV_DIM), lambda i: (i, 0)), pl.BlockSpec((CHUNK, D_INNER), lambda i: (i, zcb)),
                  pl.BlockSpec((CHUNK, DT_PAD), lambda i: (i, 0)), vec(DT_PAD), vec(DT_PAD), vec(D_INNER), vec(D_INNER),
                  pl.BlockSpec((DT_PAD, D_INNER), lambda i: (0, 0))],
        out_specs=[pl.BlockSpec((CHUNK, D_INNER), lambda i: (i, 0)), pl.BlockSpec((CHUNK, D_INNER), lambda i: (i, 0)),
                   pl.BlockSpec((1, D_STATE, D_INNER), lambda i: (i, 0, 0))],
        out_shape=[jax.ShapeDtypeStruct((t, D_INNER), F32), jax.ShapeDtypeStruct((t, D_INNER), BF16),
                   jax.ShapeDtypeStruct((nc, D_STATE, D_INNER), F32)],
        scratch_shapes=[pltpu.VMEM((D_STATE, D_INNER), F32)],
        compiler_params=_params(("arbitrary",), blk),
    )(xc, proj, dt_raw, dtb, alog, dskip_full, ng, e_bf)


def _ssd_bwd(dyb, y, xc, proj, dt_raw, sprev, dtb, alog, dskip_full, ng, e_bf, h, dproj, name):
    t = xc.shape[0]
    nc = t // CHUNK
    zcb = COL_Z // D_INNER
    hpg = GROUP_W // HEAD_DIM
    rev = lambda i: nc - 1 - i

    def body(dyb_ref, y_ref, xc_ref, z_ref, dt_ref, sprev_ref, dtb_ref, alog_ref, dsk_ref, ng_ref, e_ref, h_ref, dproj_in,
             dz_ref, dxc_ref, ddt_ref, gng_ref, gdsk_ref, galog_ref, gdtb_ref, gwdt_ref, ds_scr, sums_scr):
        del dproj_in

        @pl.when(pl.program_id(0) == 0)
        def _():
            ds_scr[...] = jnp.zeros_like(ds_scr)
            gng_ref[...] = jnp.zeros_like(gng_ref)
            gdsk_ref[...] = jnp.zeros_like(gdsk_ref)
            galog_ref[...] = jnp.zeros_like(galog_ref)
            gdtb_ref[...] = jnp.zeros_like(gdtb_ref)
            gwdt_ref[...] = jnp.zeros_like(gwdt_ref)

        mask = _tri(True)
        tril_bf = mask.astype(BF16)
        triu_bf = _tri(False).astype(BF16)
        e_v = e_ref[...]
        dt_in = dt_ref[...] + dtb_ref[...]
        dtv, a, cs, dt_full, ecs_full, decay_full = _ssd_decays(dt_ref[...], dtb_ref[...], alog_ref[...], e_v, tril_bf)
        cs_t = cs.T

        lane_h = lax.broadcasted_iota(jnp.int32, (CHUNK, DT_PAD), 1)
        sub_h = lax.broadcasted_iota(jnp.int32, (DT_PAD, CHUNK), 0)
        dcs_rows = jnp.zeros((CHUNK, DT_PAD), F32)
        dcs_cols_t = jnp.zeros((DT_PAD, CHUNK), F32)
        last_cols, dsk_cols = [], []
        for g in range(GROUPS):
            gc = slice(g * GROUP_W, (g + 1) * GROUP_W)
            b_cols = slice(D_INNER + g * D_STATE, D_INNER + (g + 1) * D_STATE)
            c_cols = slice(D_INNER + GROUPS * D_STATE + g * D_STATE, D_INNER + GROUPS * D_STATE + (g + 1) * D_STATE)
            xs = xc_ref[:, gc]
            xdt = xs * dt_full[:, gc]
            xdt_b = xdt.astype(BF16)
            xdec = xdt * decay_full[:, gc]
            xdec_b = xdec.astype(BF16)
            zv = z_ref[:, gc].astype(F32)
            sg = _sigmoid(zv)
            gate = zv * sg
            yv = y_ref[:, gc]
            dybv = dyb_ref[:, gc].astype(F32)
            ygate = yv * gate
            rstd = lax.rsqrt(jnp.mean(ygate * ygate, axis=-1, keepdims=True) + NORM_EPS)
            yn = ygate * rstd
            gng_ref[:, gc] += jnp.sum(dybv * yn, axis=0, keepdims=True)
            dyn = dybv * ng_ref[:, gc]
            dyg = rstd * (dyn - yn * jnp.mean(dyn * yn, axis=-1, keepdims=True))
            dz_ref[:, gc] = (dyg * yv * sg * (1.0 + zv * (1.0 - sg))).astype(BF16)
            dy = dyg * gate
            dy_b = dy.astype(BF16)
            dyo = dy * ecs_full[:, gc]
            dyo_b = dyo.astype(BF16)
            dsk_cols.append(jnp.sum(dy * xs, axis=0, keepdims=True))

            bg = xc_ref[:, b_cols].astype(BF16)
            cg = xc_ref[:, c_cols].astype(BF16)
            s_prev = sprev_ref[0, :, gc]
            s_prev_b = s_prev.astype(BF16)
            dsg = ds_scr[:, gc]
            dsg_b = dsg.astype(BF16)
            cb = _dot(cg, bg, _NT)
            c_s = _dot(cg, s_prev_b, _NN)
            b_ds = _dot(bg, dsg_b, _NN)
            dcb = jnp.zeros((CHUNK, CHUNK), F32)
            parts = []
            for r in range(hpg):
                h = g * hpg + r
                hc = slice(r * HEAD_DIM, (r + 1) * HEAD_DIM)
                lmat, m = _head_mats(cs, cs_t, cb, h, mask)
                dm = _dot(dy_b[:, hc], xdt_b[:, hc], _NT)
                parts.append(_dot(m.astype(BF16), dy_b[:, hc], _TN))
                dcb = dcb + dm * lmat
                w = dm * m
                dcs_rows = jnp.where(lane_h == h, jnp.sum(w, axis=1, keepdims=True), dcs_rows)
                dcs_cols_t = jnp.where(sub_h == h, jnp.sum(w, axis=0, keepdims=True), dcs_cols_t)
            dxdt = jnp.concatenate(parts, axis=1) + decay_full[:, gc] * b_ds
            dcb_b = dcb.astype(BF16)
            dxc_ref[:, c_cols] = _dot(dcb_b, bg, _NN) + _dot(dyo_b, s_prev_b, _NT)
            dxc_ref[:, b_cols] = _dot(dcb_b, cg, _TN) + _dot(xdec_b, dsg_b, _NT)
            cdec = ecs_full[CHUNK - 1:CHUNK, gc]
            ds_scr[:, gc] = _dot(cg, dyo_b, _TN) + cdec * dsg
            dxc_ref[:, gc] = dxdt * dt_full[:, gc] + dsk_ref[:, gc] * dy
            dec_prod = xdec * b_ds
            sums_scr[:CHUNK, gc] = dyo * c_s - dec_prod
            sums_scr[CHUNK:, gc] = dxdt * xs
            last_cols.append(jnp.sum(dec_prod, axis=0, keepdims=True) + cdec * jnp.sum(dsg * s_prev, axis=0, keepdims=True))
        t_sums = _head_sums(sums_scr[...], e_v)
        tail = jnp.concatenate([jnp.concatenate(last_cols, axis=1), jnp.concatenate(dsk_cols, axis=1),
                                jnp.zeros((SUBLANES - 2, D_INNER), F32)], axis=0)
        t_tail = _dot_exact_rhs(tail, e_v, _NT)
        gdsk_ref[...] += t_tail[1:2, :]
        row = lax.broadcasted_iota(jnp.int32, (CHUNK, DT_PAD), 0)
        dcs = dcs_rows - dcs_cols_t.T + t_sums[:CHUNK] + jnp.where(row == CHUNK - 1, t_tail[0:1, :], 0.0)
        dda = _dot_exact_lhs(triu_bf, dcs, _NN)
        galog_ref[...] += jnp.sum(dda * dtv, axis=0, keepdims=True) * a
        ddt = dda * a + t_sums[CHUNK:]
        ddt_raw = jnp.where(lane_h < N_HEADS, ddt * _sigmoid(dt_in), 0.0)
        gdtb_ref[...] += jnp.sum(ddt_raw, axis=0, keepdims=True)
        ddt_b = ddt_raw.astype(BF16)
        ddt_ref[...] = ddt_b
        gwdt_ref[...] += _dot(ddt_b, h_ref[...], _TN)

    vec = lambda w: pl.BlockSpec((1, w), lambda i: (0, 0))
    blk = (2 * _nbytes((CHUNK, CONV_DIM), F32) + 4 * _nbytes((CHUNK, D_INNER), F32) + 4 * _nbytes((D_STATE, D_INNER), F32))
    return pl.pallas_call(
        body, name=name, grid=(nc,),
        in_specs=[pl.BlockSpec((CHUNK, D_INNER), lambda i: (rev(i), 0)), pl.BlockSpec((CHUNK, D_INNER), lambda i: (rev(i), 0)),
                  pl.BlockSpec((CHUNK, CONV_DIM), lambda i: (rev(i), 0)), pl.BlockSpec((CHUNK, D_INNER), lambda i: (rev(i), zcb)),
                  pl.BlockSpec((CHUNK, DT_PAD), lambda i: (rev(i), 0)), pl.BlockSpec((1, D_STATE, D_INNER), lambda i: (rev(i), 0, 0)),
                  vec(DT_PAD), vec(DT_PAD), vec(D_INNER), vec(D_INNER), pl.BlockSpec((DT_PAD, D_INNER), lambda i: (0, 0)),
                  pl.BlockSpec((CHUNK, D_MODEL), lambda i: (rev(i), 0)), pl.BlockSpec(memory_space=pl.ANY)],
        out_specs=[pl.BlockSpec((CHUNK, D_INNER), lambda i: (rev(i), zcb)), pl.BlockSpec((CHUNK, CONV_DIM), lambda i: (rev(i), 0)),
                   pl.BlockSpec((CHUNK, DT_PAD), lambda i: (rev(i), 0)), vec(D_INNER), vec(DT_PAD), vec(DT_PAD), vec(DT_PAD),
                   pl.BlockSpec((DT_PAD, D_MODEL), lambda i: (0, 0))],
        out_shape=[jax.ShapeDtypeStruct(dproj.shape, BF16), jax.ShapeDtypeStruct((t, CONV_DIM), F32),
                   jax.ShapeDtypeStruct((t, DT_PAD), BF16), jax.ShapeDtypeStruct((1, D_INNER), F32),
                   jax.ShapeDtypeStruct((1, DT_PAD), F32), jax.ShapeDtypeStruct((1, DT_PAD), F32),
                   jax.ShapeDtypeStruct((1, DT_PAD), F32), jax.ShapeDtypeStruct((DT_PAD, D_MODEL), F32)],
        scratch_shapes=[pltpu.VMEM((D_STATE, D_INNER), F32), pltpu.VMEM((2 * CHUNK, D_INNER), F32)],
        input_output_aliases={12: 0},
        compiler_params=_params(("arbitrary",), blk),
    )(dyb, y, xc, proj, dt_raw, sprev, dtb, alog, dskip_full, ng, e_bf, h, dproj)


def _mesh_pos():
    return lax.axis_index("x"), lax.axis_index("y"), lax.axis_index("c")


def _other_chips(x, y):
    return [(1 - x, y), (x, 1 - y), (1 - x, 1 - y)]


def _all_peers(x, y, c):
    peers = []
    for k in range(1, N_DEV):
        fx, fy, fc = (k >> 2) & 1, (k >> 1) & 1, k & 1
        px, py, pc = x + fx - 2 * x * fx, y + fy - 2 * y * fy, c + fc - 2 * c * fc
        peers.append(((px, py, pc), 4 * px + 2 * py + pc))
    return peers


def _all_gather(shards, name, own_only=()):
    n, n_own = len(shards), len(own_only)

    def body(*refs):
        ins, own_ins = refs[:n], refs[n:n + n_own]
        outs, own_outs = refs[n + n_own:2 * n + n_own], refs[2 * n + n_own:2 * (n + n_own)]
        send_sems, recv_sems, local_sems = refs[2 * (n + n_own):]
        x, y, c = _mesh_pos()
        me, sibling = (x, y, c), (x, y, 1 - c)
        chips = _other_chips(x, y)

        def slot(p):
            return 4 * p[0] + 2 * p[1] + p[2]

        def copy(a, k, block, to, src=None):
            dst = outs[a].at[slot(block)]
            return pltpu.make_async_remote_copy(
                src_ref=dst if src is None else src, dst_ref=dst, send_sem=send_sems.at[a * 7 + k],
                recv_sem=recv_sems.at[a * 7 + k], device_id=to, device_id_type=MESH)

        started = []
        own = []
        for a in range(n_own):
            mine = pltpu.make_async_copy(own_ins[a], own_outs[a].at[slot(me)], local_sems.at[n + a])
            mine.start()
            own.append(mine)
        for a in range(n):
            mine = pltpu.make_async_copy(ins[a], outs[a].at[slot(me)], local_sems.at[a])
            mine.start()
            own.append(mine)
            first = [copy(a, 0, me, sibling, src=ins[a])]
            first += [copy(a, 1 + j, me, (*chip, c), src=ins[a]) for j, chip in enumerate(chips)]
            for cp in first:
                cp.start()
            started += first
        for a in range(n):
            for j, chip in enumerate(chips):
                copy(a, 1 + j, (*chip, c), me).wait_recv()
                fwd = copy(a, 4 + j, (*chip, c), sibling)
                fwd.start()
                started.append(fwd)
        for a in range(n):
            copy(a, 0, sibling, me).wait_recv()
            for j, chip in enumerate(chips):
                copy(a, 4 + j, (*chip, 1 - c), me).wait_recv()
        for cp in started:
            cp.wait_send()
        for mine in own:
            mine.wait()

    return pl.pallas_call(
        body, name=name,
        in_specs=[_HBM] * (n + n_own), out_specs=[_HBM] * (n + n_own),
        out_shape=[jax.ShapeDtypeStruct((N_DEV,) + s.shape, s.dtype) for s in (*shards, *own_only)],
        scratch_shapes=[pltpu.SemaphoreType.DMA((7 * n,)), pltpu.SemaphoreType.DMA((7 * n,)),
                        pltpu.SemaphoreType.DMA((n + n_own,))],
    )(*shards, *own_only)


_SMALL_ROWS = (("norm_mix_g", 8), ("conv_b", 32), ("dt_bias", 1), ("a_log", 1), ("d_skip", 1), ("ssm_norm_g", 16),
               ("v_norm_g", 8), ("v_norm_b", 8), ("w_spatial", 1024), ("b_spatial", 8), ("b_gates", 16), ("norm_mlp_g", 8),
               ("norm_final_g", 8), ("conv_w", 128), ("loss", 1))
_LAST_SMALL = (("norm_mix_g", 8),)


def _packed_rows(table):
    return -(-sum(r for _, r in table) // SUBLANES) * SUBLANES


def _small_offsets(table=_SMALL_ROWS):
    offs, r = {}, 0
    for name, rows in table:
        offs[name] = r
        r += rows
    return offs


def _rows_from(src_ref, dst_ref, r0):
    k, w = src_ref.shape
    if w <= LANES:
        dst_ref[r0:r0 + k, 0:w] = src_ref[...]
        return
    per = w // LANES
    for i in range(k):
        for j in range(per):
            dst_ref[r0 + i * per + j:r0 + i * per + j + 1, :] = src_ref[i:i + 1, j * LANES:(j + 1) * LANES]


def _rows_to(src_ref, r0, dst_ref):
    k, w = dst_ref.shape
    if w <= LANES:
        dst_ref[...] = src_ref[r0:r0 + k, 0:w]
        return
    per = w // LANES
    for i in range(k):
        for j in range(per):
            dst_ref[i:i + 1, j * LANES:(j + 1) * LANES] = src_ref[r0 + i * per + j:r0 + i * per + j + 1, :]


def _pack_small(grads, slot_idx, name):
    names = [n for n, _ in _SMALL_ROWS if n in grads]
    offs = _small_offsets()
    rows = _packed_rows(_SMALL_ROWS)

    def body(slot_ref, *refs):
        del slot_ref
        ins, (packed_ref, land_ref) = refs[:len(names)], refs[len(names):]
        packed_ref[...] = jnp.zeros_like(packed_ref)
        for n, ref in zip(names, ins):
            _rows_from(ref, packed_ref, offs[n])
        land_ref[0] = packed_ref[...]

    whole = lambda shape: pl.BlockSpec(shape, lambda i, slot_ref: (0,) * len(shape))
    grid_spec = pltpu.PrefetchScalarGridSpec(
        num_scalar_prefetch=1, grid=(1,), in_specs=[whole(grads[n].shape) for n in names],
        out_specs=[whole((rows, LANES)), pl.BlockSpec((1, rows, LANES), lambda i, slot_ref: (slot_ref[0], 0, 0))])
    return pl.pallas_call(
        body, name=name, grid_spec=grid_spec,
        out_shape=[jax.ShapeDtypeStruct((rows, LANES), F32), jax.ShapeDtypeStruct((N_DEV, rows, LANES), F32)],
    )(slot_idx, *[grads[n] for n in names])


def _exchange_small(grads, table, name):
    names = [n for n, _ in table]
    offs = _small_offsets(table)
    n_in = len(names)
    packed_rows = _packed_rows(table)

    def body(*refs):
        ins, out_ref = refs[:n_in], refs[n_in]
        packed, send_sems, recv_sems, local_sem = refs[n_in + 1:]
        packed[...] = jnp.zeros_like(packed)
        for n, ref in zip(names, ins):
            _rows_from(ref, packed, offs[n])
        x, y, c = _mesh_pos()
        my_slot = 4 * x + 2 * y + c
        mine = pltpu.make_async_copy(packed, out_ref.at[my_slot], local_sem)
        mine.start()
        copies = []
        for k, (peer, peer_slot) in enumerate(_all_peers(x, y, c)):
            sems = dict(send_sem=send_sems.at[k], recv_sem=recv_sems.at[k], device_id=peer, device_id_type=MESH)
            send = pltpu.make_async_remote_copy(src_ref=packed, dst_ref=out_ref.at[my_slot], **sems)
            send.start()
            copies.append((send, pltpu.make_async_remote_copy(src_ref=packed, dst_ref=out_ref.at[peer_slot], **sems)))
        for send, recv in copies:
            send.wait_send()
            recv.wait_recv()
        mine.wait()

    return pl.pallas_call(
        body, name=name, in_specs=[pl.BlockSpec(memory_space=pltpu.VMEM)] * n_in, out_specs=_HBM,
        out_shape=jax.ShapeDtypeStruct((N_DEV, packed_rows, LANES), F32),
        scratch_shapes=[pltpu.VMEM((packed_rows, LANES), F32), pltpu.SemaphoreType.DMA((N_DEV - 1,)),
                        pltpu.SemaphoreType.DMA((N_DEV - 1,)), pltpu.SemaphoreType.DMA],
    )(*[grads[n] for n in names])


def _swap_with_sibling(grads, name):
    n = len(grads)

    def body(*refs):
        ins, outs = refs[:n], refs[n:2 * n]
        send_sems, recv_sems = refs[2 * n:]
        x, y, c = _mesh_pos()
        copies = []
        for a in range(n):
            for k in range(N_CHIP):
                cp = pltpu.make_async_remote_copy(
                    src_ref=ins[a].at[(1 - c) + 2 * k], dst_ref=outs[a].at[k], send_sem=send_sems.at[a * N_CHIP + k],
                    recv_sem=recv_sems.at[a * N_CHIP + k], device_id=(x, y, 1 - c), device_id_type=MESH)
                cp.start()
                copies.append(cp)
        for cp in copies:
            cp.wait()

    return pl.pallas_call(
        body, name=name, in_specs=[_HBM] * n, out_specs=[_HBM] * n,
        out_shape=[jax.ShapeDtypeStruct((N_CHIP,) + g.shape[1:], g.dtype) for g in grads],
        scratch_shapes=[pltpu.SemaphoreType.DMA((N_CHIP * n,)), pltpu.SemaphoreType.DMA((N_CHIP * n,))],
    )(*grads)


_SEM = pl.BlockSpec(memory_space=pltpu.SEMAPHORE)
_IN_HBM = pl.BlockSpec(memory_space=pltpu.HBM)
_EFFECT = pltpu.SideEffectType.DATAFLOW_SIDE_EFFECTING


def _in_hbm(a):
    return pltpu.with_memory_space_constraint(a, pltpu.HBM)


def _gather_copies(ins, lands, send_sems, recv_sems):
    x, y, c = _mesh_pos()
    my_slot = 4 * x + 2 * y + c
    pairs = []
    for a in range(len(ins)):
        for k, (peer, peer_slot) in enumerate(_all_peers(x, y, c)):
            sems = dict(send_sem=send_sems.at[a * (N_DEV - 1) + k], recv_sem=recv_sems.at[a * (N_DEV - 1) + k],
                        device_id=peer, device_id_type=MESH)
            pairs.append((pltpu.make_async_remote_copy(src_ref=ins[a], dst_ref=lands[a].at[my_slot], **sems),
                          pltpu.make_async_remote_copy(src_ref=ins[a], dst_ref=lands[a].at[peer_slot], **sems)))
    return pairs


def _scatter_copies(ins, lands, send_sems, recv_sems):
    x, y, c = _mesh_pos()
    my_chip = 2 * x + y
    pairs = []
    for a in range(len(ins)):
        for j, chip in enumerate(_other_chips(x, y)):
            there = 2 * chip[0] + chip[1]
            sems = dict(send_sem=send_sems.at[a * 3 + j], recv_sem=recv_sems.at[a * 3 + j],
                        device_id=(*chip, c), device_id_type=MESH)
            pairs.append((pltpu.make_async_remote_copy(src_ref=ins[a].at[there], dst_ref=lands[a].at[my_chip], **sems),
                          pltpu.make_async_remote_copy(src_ref=ins[a].at[my_chip], dst_ref=lands[a].at[there], **sems)))
    return pairs


def _split_start(srcs, lands, copies, per_array, name):
    n = len(srcs)

    def body(*refs):
        ins, land_refs = refs[:n], refs[n:2 * n]
        send_sems, recv_sems = refs[2 * n], refs[2 * n + 1]
        token = refs[-1]
        for send, _ in copies(ins, land_refs, send_sems, recv_sems):
            send.start()
        token[...] = jnp.zeros_like(token)

    outs = pl.pallas_call(
        body, name=name,
        out_shape=(pltpu.SemaphoreType.DMA((per_array * n,)), pltpu.SemaphoreType.DMA((per_array * n,)),
                   *[pltpu.HBM(s.shape, s.dtype) for s in srcs], *[pltpu.HBM(l.shape, l.dtype) for l in lands],
                   jax.ShapeDtypeStruct((SUBLANES, LANES), F32)),
        in_specs=[_IN_HBM] * (2 * n),
        out_specs=(_SEM, _SEM, *[_IN_HBM] * (2 * n), pl.BlockSpec(memory_space=pltpu.VMEM)),
        input_output_aliases={i: 2 + i for i in range(2 * n)},
        compiler_params=pltpu.CompilerParams(has_side_effects=_EFFECT),
    )(*[_in_hbm(s) for s in srcs], *[_in_hbm(l) for l in lands])
    return outs[0], outs[1], list(outs[2:2 + n]), list(outs[2 + n:2 + 2 * n]), outs[-1]


def _split_wait(started, copies, after, name):
    send_sems, recv_sems, srcs, lands, _ = started
    n = len(srcs)

    def body(*refs):
        ins, land_refs = refs[:n], refs[n:2 * n]
        for send, recv in copies(ins, land_refs, refs[2 * n], refs[2 * n + 1]):
            send.wait_send()
            recv.wait_recv()

    outs = pl.pallas_call(
        body, name=name,
        out_shape=(*[pltpu.HBM(s.shape, s.dtype) for s in srcs], *[pltpu.HBM(l.shape, l.dtype) for l in lands]),
        in_specs=[_IN_HBM] * (2 * n) + [_SEM, _SEM, _HBM],
        out_specs=[_IN_HBM] * (2 * n),
        input_output_aliases={i: i for i in range(2 * n)},
        compiler_params=pltpu.CompilerParams(has_side_effects=_EFFECT),
    )(*srcs, *lands, send_sems, recv_sems, after)
    return list(outs[:n]), list(outs[n:])


def _ew_block(rows, cols, slots):
    budget = 8 * 1024 * 1024
    br, bc = rows, cols
    while slots * br * bc * 4 > budget:
        if br % 2 == 0 and (br // 2) % (2 * SUBLANES) == 0:
            br //= 2
        elif bc % 2 == 0 and (bc // 2) % LANES == 0:
            bc //= 2
        else:
            break
    return br, bc


def _add_sibling(grads, recv, c_idx, name):
    _, rows, cols = grads.shape
    br, bc = _ew_block(rows, cols, 3)

    def body(c_ref, g_ref, r_ref, out_ref):
        del c_ref
        out_ref[...] = (g_ref[...].astype(F32) + r_ref[...].astype(F32)).astype(out_ref.dtype)

    grid_spec = pltpu.PrefetchScalarGridSpec(
        num_scalar_prefetch=1, grid=(N_CHIP, rows // br, cols // bc),
        in_specs=[pl.BlockSpec((1, br, bc), lambda k, i, j, c_ref: (c_ref[0] + 2 * k, i, j)),
                  pl.BlockSpec((1, br, bc), lambda k, i, j, c_ref: (k, i, j))],
        out_specs=pl.BlockSpec((1, br, bc), lambda k, i, j, c_ref: (k, i, j)))
    return pl.pallas_call(
        body, name=name, grid_spec=grid_spec, out_shape=jax.ShapeDtypeStruct((N_CHIP, rows, cols), grads.dtype),
        compiler_params=_params(("parallel", "parallel", "parallel"), 3 * _nbytes((br, bc), F32)),
    )(c_idx, grads, recv)


def _adam_math(g, w, m, v):
    m2 = ADAM_B1 * m + (1.0 - ADAM_B1) * g
    v2 = ADAM_B2 * v + (1.0 - ADAM_B2) * (g * g)
    m_hat = m2 * (1.0 / (1.0 - ADAM_B1 ** ADAM_STEP))
    v_hat = v2 * (1.0 / (1.0 - ADAM_B2 ** ADAM_STEP))
    return -ADAM_LR * (m_hat / (jnp.sqrt(v_hat) + ADAM_EPS) + ADAM_WD * w), m2, v2


def _adamw(slots, w, m, v, name, own=None, own_slot=None):
    ns, rows, cols = slots.shape
    br, bc = _ew_block(rows, cols, 2 * ns + 7)

    def update(g, w_ref, m_ref, v_ref, g_ref, d_ref, m2_ref, v2_ref):
        g_ref[...] = g
        d_ref[...], m2_ref[...], v2_ref[...] = _adam_math(g, w_ref[...], m_ref[...], v_ref[...])

    out_shape = [jax.ShapeDtypeStruct((rows, cols), F32)] * 4
    params = _params(("parallel", "parallel"), (2 * ns + 7) * _nbytes((br, bc), F32))
    grid = (rows // br, cols // bc)
    if own is None:
        def body(s_ref, *rest):
            g = s_ref[0].astype(F32)
            for k in range(1, ns):
                g = g + s_ref[k].astype(F32)
            update(g, *rest)

        blk = pl.BlockSpec((br, bc), lambda i, j: (i, j))
        return pl.pallas_call(
            body, name=name, grid=grid,
            in_specs=[pl.BlockSpec((ns, br, bc), lambda i, j: (0, i, j)), blk, blk, blk], out_specs=[blk] * 4,
            out_shape=out_shape, compiler_params=params,
        )(slots, w, m, v)

    def body_own(slot_ref, s_ref, o_ref, *rest):
        g = None
        for k in range(ns):
            term = jnp.where(slot_ref[0] == k, o_ref[k].astype(F32), s_ref[k].astype(F32))
            g = term if g is None else g + term
        update(g, *rest)

    blk = pl.BlockSpec((br, bc), lambda i, j, slot_ref: (i, j))
    stack = pl.BlockSpec((ns, br, bc), lambda i, j, slot_ref: (0, i, j))
    grid_spec = pltpu.PrefetchScalarGridSpec(num_scalar_prefetch=1, grid=grid, in_specs=[stack, stack, blk, blk, blk],
                                             out_specs=[blk] * 4)
    return pl.pallas_call(body_own, name=name, grid_spec=grid_spec, out_shape=out_shape, compiler_params=params,
                          )(own_slot, slots, own, w, m, v)


def _adamw_small(all_g, last_g, params, extra_shapes, name):
    names = [n for n, _ in _SMALL_ROWS if n in params]
    extras = [n for n, _ in _SMALL_ROWS if n not in params]
    offs = _small_offsets()
    n_p = len(names)

    def body(*refs):
        s_ref, last_ref = refs[0], refs[1]
        wmv = refs[2:2 + 3 * n_p]
        outs = refs[2 + 3 * n_p:2 + 7 * n_p]
        extra_refs = refs[2 + 7 * n_p:2 + 7 * n_p + len(extras)]
        summed = refs[-1]
        g, g_last = s_ref[0], last_ref[0]
        for k in range(1, N_DEV):
            g, g_last = g + s_ref[k], g_last + last_ref[k]
        summed[...] = g
        last_offs = _small_offsets(_LAST_SMALL)
        for n, rows in _LAST_SMALL:
            summed[offs[n]:offs[n] + rows, :] = g_last[last_offs[n]:last_offs[n] + rows, :]
        for i, n in enumerate(names):
            w_ref, m_ref, v_ref = wmv[3 * i:3 * i + 3]
            g_ref, d_ref, m2_ref, v2_ref = outs[4 * i:4 * i + 4]
            _rows_to(summed, offs[n], g_ref)
            d_ref[...], m2_ref[...], v2_ref[...] = _adam_math(g_ref[...], w_ref[...], m_ref[...], v_ref[...])
        for n, ref in zip(extras, extra_refs):
            _rows_to(summed, offs[n], ref)

    flat = [a for n in names for a in params[n]]
    out_shape = [jax.ShapeDtypeStruct(params[n][0].shape, F32) for n in names for _ in range(4)]
    out_shape += [jax.ShapeDtypeStruct(s, F32) for s in extra_shapes]
    vmem = pl.BlockSpec(memory_space=pltpu.VMEM)
    res = pl.pallas_call(
        body, name=name, in_specs=[vmem] * (2 + len(flat)), out_specs=[vmem] * len(out_shape), out_shape=out_shape,
        scratch_shapes=[pltpu.VMEM(all_g.shape[1:], F32)],
        compiler_params=pltpu.CompilerParams(vmem_limit_bytes=_vmem_limit(_nbytes(all_g.shape, F32))),
    )(all_g, last_g, *flat)
    return {n: res[4 * i:4 * i + 4] for i, n in enumerate(names)}, res[4 * n_p:]


def _mm_tiles(mode, m, n, k):
    tn = min(n, 1024)
    if mode == "tn":
        return min(m, 1024), tn, min(k, 4096)
    if k <= 1024:
        return min(m, 2048), tn, k
    if k <= 2048:
        return min(m, 1024), tn, k
    if k <= 4096:
        return min(m, 512), tn, k
    return min(m, 1024), tn, 2048


def _local_step(x, target, wts, small, exchange):
    t = x.shape[0]
    assert t % CONV_FWD_TILE == 0 and t % CONV_TILE == 0 and t % GMLP_TILE == 0 and t % (2 * ROW_TILE) == 0, t
    w_main_t, w_dt_t = wts["w_main_t"], wts["w_dt_t"]
    bsp_t = small["b_spatial"].T
    pad32 = lambda a: jnp.pad(a, ((0, 0), (0, DT_PAD - N_HEADS)))
    dtb, alog = pad32(small["dt_bias"]), pad32(small["a_log"])
    dskip_full = jnp.repeat(small["d_skip"], HEAD_DIM, axis=1)
    head_of_col = lax.broadcasted_iota(jnp.int32, (DT_PAD, D_INNER), 1) // HEAD_DIM
    e_bf = (head_of_col == lax.broadcasted_iota(jnp.int32, (DT_PAD, D_INNER), 0)).astype(BF16)

    def mm(a, b, mode, name, **kw):
        if mode == "nn":
            m, k, n = a.shape[0], a.shape[1], b.shape[1]
        elif mode == "nt":
            m, k, n = a.shape[0], a.shape[1], b.shape[0]
        else:
            m, k, n = a.shape[1], a.shape[0], b.shape[1]
        tm, tn, tk = _mm_tiles(mode, m, n, k)
        tm = min(tm, kw.pop("max_tm", tm))
        kw.setdefault("out_dtypes", (BF16,) if mode == "tn" else (F32,))
        if "extra_specs" in kw:
            kw["extra_specs"] = kw["extra_specs"](tm, tn)
        return _matmul(a, b, mode=mode, tm=tm, tn=tn, tk=tk, name=name, **kw)

    def out_tile(tm, tn):
        return (((tm, tn), lambda i, j: (i, j)),)

    def row_tiles(n_tiles, *vectors, gate_logits=False):
        def specs(tm, tn):
            out = [((tm, tn), lambda i, j: (i, j))] * n_tiles
            if gate_logits:
                out += [((tm, D_MODEL), lambda i, j, cb=COL_GATE // D_MODEL + half: (i, cb)) for half in range(2)]
            return tuple(out) + tuple(((1, w), lambda i, j, cb=cb: (0, cb)) for w, cb in vectors)
        return specs

    vec = lambda w: ((1, w), F32, (1, w), lambda i, j: (0, 0))
    fused_tm = 512

    h, dt_raw = _rms_fwd(x, small["norm_mix_g"], w_dt_t, "rms_mix", deps=exchange.begin())
    proj = mm(h, w_main_t, "nt", "proj_main", out_dtypes=(BF16,))
    y_a = _gmlp_fwd(proj, small["v_norm_g"], small["v_norm_b"], small["w_spatial"], bsp_t, "gmlp_fwd")
    pre_conv, xc = _conv_fwd(proj, wts["conv_w"], small["conv_b"], "conv_fwd")
    y_ssd, y_b, sprev = _ssd_fwd(xc, proj, dt_raw, dtb, alog, dskip_full, small["ssm_norm_g"], e_bf, "ssd_fwd")
    wts = {**wts, **exchange.late_weights(y_b)}
    pa = mm(y_a, wts["w_proj_a"], "nn", "proj_a", out_dtypes=(BF16,))
    pb, merged = mm(y_b, wts["w_proj_b"], "nn", "proj_b", epilogue=_merge_epilogue, out_dtypes=(BF16, BF16), max_tm=fused_tm,
                    extras=(pa, proj, proj, small["b_gates"], small["b_gates"]),
                    extra_specs=row_tiles(1, (D_MODEL, 0), (D_MODEL, 1), gate_logits=True))
    x1, h2 = mm(merged, wts["w_out"], "nn", "out_proj", epilogue=_residual_rms_epilogue, out_dtypes=(F32, BF16),
                max_tm=2 * fused_tm, extras=(x, small["norm_mlp_g"]), extra_specs=row_tiles(1, (D_MODEL, 0)))

    def relu_sq(acc, ex, outs, first):
        r = jnp.maximum(acc, 0.0)
        outs[0][...] = (r * r).astype(BF16)

    act = mm(h2, wts["w_mlp_up"], "nn", "mlp_up", epilogue=relu_sq, out_dtypes=(BF16,))
    dx2, dx2_b, g_final, _, loss = mm(
        act, wts["w_mlp_down"], "nn", "mlp_down", epilogue=_loss_epilogue, carry=True,
        out_dtypes=(F32, BF16, vec(D_MODEL), vec(D_MODEL), vec(LANES)),
        extras=(x1, small["norm_final_g"], target), extra_specs=lambda tm, tn: (
            ((tm, tn), lambda i, j: (i, j)), ((1, tn), lambda i, j: (0, 0)), ((tm, tn), lambda i, j: (i, j))))

    def relu_sq_bwd(acc, ex, outs, first):
        outs[0][...] = (acc * 2.0 * jnp.sqrt(ex[0][...].astype(F32))).astype(BF16)

    dup = mm(dx2_b, wts["w_mlp_down"], "nt", "d_act", epilogue=relu_sq_bwd, extras=(act,), extra_specs=out_tile,
             out_dtypes=(BF16,))
    g_down = mm(act, dx2_b, "tn", "g_mlp_down")
    g_up = mm(h2, dup, "tn", "g_mlp_up")
    dx1, dx1_b, g_mlp = mm(
        dup, wts["w_mlp_up"], "nt", "d_h2", epilogue=_rms_bwd_epilogue, carry=True,
        out_dtypes=(F32, BF16, vec(D_MODEL)), extras=(x1, small["norm_mlp_g"], dx2), extra_specs=lambda tm, tn: (
            ((tm, tn), lambda i, j: (i, j)), ((1, tn), lambda i, j: (0, 0)), ((tm, tn), lambda i, j: (i, j))))

    g_out = mm(merged, dx1_b, "tn", "g_out")
    dpa, dpb, dproj, g_bgates = mm(
        dx1_b, wts["w_out"], "nt", "d_merged", epilogue=_merge_bwd_epilogue, carry=True, max_tm=fused_tm,
        out_dtypes=(BF16, BF16, ((t, MAIN_W), BF16, (fused_tm, 2 * D_MODEL), lambda i, j: (i, COL_GATE // (2 * D_MODEL))),
                    vec(2 * D_MODEL)),
        extras=(pa, pb, proj, proj, small["b_gates"], small["b_gates"]),
        extra_specs=row_tiles(2, (D_MODEL, 0), (D_MODEL, 1), gate_logits=True))
    g_pa = mm(y_a, dpa, "tn", "g_proj_a")
    g_pb = mm(y_b, dpb, "tn", "g_proj_b")
    started = exchange.reduce("late", {"w_mlp_down": g_down, "w_mlp_up": g_up, "w_out": g_out, "w_proj_a": g_pa,
                                       "w_proj_b": g_pb})
    dya = mm(dpa, wts["w_proj_a"], "nt", "d_ya", deps=started, out_dtypes=(BF16,))
    dyb = mm(dpb, wts["w_proj_b"], "nt", "d_yb", out_dtypes=(BF16,))

    dproj, g_wsp, g_bsp_t, g_vg, g_vb = _gmlp_bwd(proj, dya, small["v_norm_g"], small["v_norm_b"], small["w_spatial"],
                                                   bsp_t, dproj, "gmlp_bwd")
    dproj, dxc, ddt, g_ng, g_dskip, g_alog, g_dtb, g_dt_t = _ssd_bwd(dyb, y_ssd, xc, proj, dt_raw, sprev, dtb, alog, dskip_full,
                                                                     small["ssm_norm_g"], e_bf, h, dproj, "ssd_bwd")
    dproj, g_convw, g_convb = _conv_bwd(proj, pre_conv, dxc, wts["conv_w"], dproj, "conv_bwd")

    small_grads = {
        "conv_w": g_convw, "loss": loss,
        "conv_b": g_convb, "dt_bias": g_dtb, "a_log": g_alog, "d_skip": g_dskip, "ssm_norm_g": g_ng,
        "v_norm_g": g_vg, "v_norm_b": g_vb, "w_spatial": g_wsp.reshape(GROUPS * CHUNK, CHUNK), "b_spatial": g_bsp_t.T,
        "b_gates": g_bgates, "norm_mlp_g": g_mlp, "norm_final_g": g_final,
    }
    g_main_t = mm(dproj, h, "tn", "g_in_main", deps=exchange.small(small_grads))
    started = exchange.reduce("in", {"w_in": (g_main_t, g_dt_t.astype(BF16))})

    def input_grad(acc, ex, outs, first):
        x_ref, g_ref, res_ref, ddt_ref, wdt_ref = ex
        gg = jnp.zeros((1, D_MODEL), F32)
        for r in range(acc.shape[0] // ROW_TILE):
            rows = slice(r * ROW_TILE, (r + 1) * ROW_TILE)
            dh = acc[rows] + _dot(ddt_ref[rows, :], wdt_ref[...], _NN)
            dx, gg_r = _rms_pullback(x_ref[rows, :], g_ref[...], dh)
            outs[0][rows, :] = dx + res_ref[rows, :]
            gg = gg + gg_r

        _zero_when(first, outs[1])
        outs[1][...] += gg

    grad_x, g_mix = mm(
        dproj, w_main_t, "nn", "d_h", epilogue=input_grad, deps=started, carry=True,
        out_dtypes=(F32, vec(D_MODEL)), extras=(x, small["norm_mix_g"], dx1, ddt, w_dt_t), extra_specs=lambda tm, tn: (
            ((tm, tn), lambda i, j: (i, j)), ((1, tn), lambda i, j: (0, 0)), ((tm, tn), lambda i, j: (i, j)),
            ((tm, DT_PAD), lambda i, j: (i, 0)), ((DT_PAD, D_MODEL), lambda i, j: (0, 0))))

    return grad_x, g_mix


SHARD_ROWS = (MAIN_W + N_HEADS) // N_DEV
REGROUP_IN = 2048


def _main_rows_of(gathered, name):
    n_dev, shard, d = gathered.shape
    blk = 1024
    nb = MAIN_W // blk

    def first_feature(b):
        return b * blk + (N_HEADS if b * blk >= COL_GATE else 0)

    def body(a_ref, b_ref, out_ref):
        for b in range(nb):
            s0, r0 = divmod(first_feature(b), shard)
            n1 = min(shard - r0, blk)

            @pl.when(pl.program_id(0) == b)
            def _(r0=r0, n1=n1):
                out_ref[0:n1, :] = a_ref[0, r0:r0 + n1, :]
                if n1 < blk:
                    out_ref[n1:blk, :] = b_ref[0, 0:blk - n1, :]

    def slot(b):
        return (b * blk + jnp.where(b * blk >= COL_GATE, N_HEADS, 0)) // shard

    return pl.pallas_call(
        body, name=name, grid=(nb,),
        in_specs=[pl.BlockSpec((1, shard, d), lambda b: (slot(b), 0, 0)),
                  pl.BlockSpec((1, shard, d), lambda b: (jnp.minimum(slot(b) + 1, n_dev - 1), 0, 0))],
        out_specs=pl.BlockSpec((blk, d), lambda b: (b, 0)),
        out_shape=jax.ShapeDtypeStruct((MAIN_W, d), gathered.dtype),
        compiler_params=_params(("parallel",), 3 * _nbytes((shard, d), gathered.dtype)),
    )(gathered, gathered)


def _by_device_rows(g_main_t, g_dt_t, name):
    d = g_main_t.shape[1]
    n_blocks = MAIN_W // REGROUP_IN
    dt_dev, dt_row = divmod(COL_GATE, SHARD_ROWS)

    def main_start(s):
        return s * SHARD_ROWS - (N_HEADS if s > dt_dev else 0)

    def body(a_ref, b_ref, dt_ref, out_ref):
        for s in range(N_DEV):
            m0 = main_start(s)
            k0, off = divmod(m0, REGROUP_IN)
            pieces = []
            if s == dt_dev:
                pieces = [(0, dt_row, m0), (dt_row, N_HEADS, None), (dt_row + N_HEADS, SHARD_ROWS - dt_row - N_HEADS, m0 + dt_row)]
            else:
                pieces = [(0, SHARD_ROWS, m0)]

            @pl.when(pl.program_id(0) == s)
            def _(pieces=pieces, k0=k0):
                for dst, n, src in pieces:
                    if src is None:
                        out_ref[0, dst:dst + n, :] = dt_ref[0:n, :]
                        continue
                    lo = src - k0 * REGROUP_IN
                    n_a = max(0, min(n, REGROUP_IN - lo))
                    if n_a:
                        out_ref[0, dst:dst + n_a, :] = a_ref[lo:lo + n_a, :]
                    if n_a < n:
                        lo_b = max(lo - REGROUP_IN, 0)
                        out_ref[0, dst + n_a:dst + n, :] = b_ref[lo_b:lo_b + n - n_a, :]

    def first_block(s):
        return (s * SHARD_ROWS - jnp.where(s > dt_dev, N_HEADS, 0)) // REGROUP_IN

    return pl.pallas_call(
        body, name=name, grid=(N_DEV,),
        in_specs=[pl.BlockSpec((REGROUP_IN, d), lambda s: (first_block(s), 0)),
                  pl.BlockSpec((REGROUP_IN, d), lambda s: (jnp.minimum(first_block(s) + 1, n_blocks - 1), 0)),
                  pl.BlockSpec((DT_PAD, d), lambda s: (0, 0))],
        out_specs=pl.BlockSpec((1, SHARD_ROWS, d), lambda s: (s, 0, 0)),
        out_shape=jax.ShapeDtypeStruct((N_DEV, SHARD_ROWS, d), g_main_t.dtype),
        compiler_params=_params(("parallel",), 3 * _nbytes((REGROUP_IN, d), g_main_t.dtype)),
    )(g_main_t, g_main_t, g_dt_t)


_LATE = ["w_proj_a", "w_proj_b", "w_out", "w_mlp_up", "w_mlp_down"]
_BY_COLS = ("w_mlp_up",)


class _Exchange:
    def __init__(self, late_shards, late_lands):
        self.late_shards, self.late_lands = late_shards, late_lands
        self.c_idx = lax.axis_index("c").astype(jnp.int32).reshape(1)
        self.chip_idx = (2 * lax.axis_index("x") + lax.axis_index("y")).astype(jnp.int32).reshape(1)
        self.pending = []

    def begin(self):
        self.late = _split_start(self.late_shards, self.late_lands, _gather_copies, N_DEV - 1, "gather_late_start")
        return [self.late[-1]]

    def late_weights(self, after):
        _, lands = _split_wait(self.late, _gather_copies, after, "gather_late_wait")
        whole = {}
        for n, g in zip(_LATE, lands):
            whole[n] = jnp.transpose(g, (1, 0, 2)).reshape(g.shape[1], -1) if n in _BY_COLS else g.reshape(-1, g.shape[2])
        return whole

    def reduce(self, tag, grads):
        names = list(grads)
        by_dev = []
        for n in names:
            g = grads[n]
            if n == "w_in":
                by_dev.append(_by_device_rows(*g, "regroup_g_in"))
            elif n in _BY_COLS:
                by_dev.append(jnp.transpose(g.reshape(g.shape[0], N_DEV, -1), (1, 0, 2)))
            else:
                by_dev.append(g.reshape(N_DEV, -1, g.shape[1]))
        from_sibling = _swap_with_sibling(by_dev, "reduce_cores_" + tag)
        parts = [_add_sibling(g, r, self.c_idx, "add_cores_" + n) for n, g, r in zip(names, by_dev, from_sibling)]
        lands = [lax.empty(p.shape, p.dtype) for p in parts]
        started = _split_start(parts, lands, _scatter_copies, 3, "reduce_chips_start_" + tag)
        self.pending.append((tag, names, started))
        return [started[-1]]

    def small(self, grads):
        dev = 2 * self.chip_idx + self.c_idx
        packed, land = _pack_small(grads, dev, "pack_small")
        self.small_started = _split_start([packed], [land], _gather_copies, N_DEV - 1, "exchange_small_start")
        return [self.small_started[-1]]

    def finish(self, after):
        _, (all_small,) = _split_wait(self.small_started, _gather_copies, after, "exchange_small_wait")
        done = {}
        for tag, names, started in self.pending:
            parts, lands = _split_wait(started, _scatter_copies, after, "reduce_chips_wait_" + tag)
            for n, land, part in zip(names, lands, parts):
                done[n] = (land, part, self.chip_idx)
        return all_small, done


def kernel(x, norm_mix_g, w_in, conv_w, conv_b, dt_bias, a_log, d_skip, ssm_norm_g, v_norm_g, v_norm_b, w_spatial, b_spatial, b_gates, w_proj_a, w_proj_b, w_out, norm_mlp_g, w_mlp_up, w_mlp_down, norm_final_g, loss_target, m_norm_mix_g, m_w_in, m_conv_w, m_conv_b, m_dt_bias, m_a_log, m_d_skip, m_ssm_norm_g, m_v_norm_g, m_v_norm_b, m_w_spatial, m_b_spatial, m_b_gates, m_w_proj_a, m_w_proj_b, m_w_out, m_norm_mlp_g, m_w_mlp_up, m_w_mlp_down, m_norm_final_g, v_norm_mix_g, v_w_in, v_conv_w, v_conv_b, v_dt_bias, v_a_log, v_d_skip, v_ssm_norm_g, v_v_norm_g, v_v_norm_b, v_w_spatial, v_b_spatial, v_b_gates, v_w_proj_a, v_w_proj_b, v_w_out, v_norm_mlp_g, v_w_mlp_up, v_w_mlp_down, v_norm_final_g):
    given = dict(locals())
    names = ["norm_mix_g", "w_in", "conv_w", "conv_b", "dt_bias", "a_log", "d_skip", "ssm_norm_g", "v_norm_g", "v_norm_b",
             "w_spatial", "b_spatial", "b_gates", "w_proj_a", "w_proj_b", "w_out", "norm_mlp_g", "w_mlp_up", "w_mlp_down",
             "norm_final_g"]
    shapes = {n: given[n].shape for n in names}
    dev = 4 * lax.axis_index("x") + 2 * lax.axis_index("y") + lax.axis_index("c")

    shard2d = {"w_in": w_in[0].T, "w_proj_a": w_proj_a[0], "w_proj_b": w_proj_b[0], "w_out": w_out[0],
               "w_mlp_up": w_mlp_up[0], "w_mlp_down": w_mlp_down[0]}
    conv_shard = conv_w.reshape(CONV_WIDTH, -1)
    late_shards = [shard2d[n].astype(BF16) for n in _LATE]
    w_in_all, conv_all, *late_lands = _all_gather([shard2d["w_in"].astype(BF16), conv_shard], "gather_first",
                                                  own_only=late_shards)
    dt_dev, dt_row = divmod(COL_GATE, SHARD_ROWS)
    w_dt_t = jnp.pad(w_in_all[dt_dev, dt_row:dt_row + N_HEADS], ((0, DT_PAD - N_HEADS), (0, 0)))
    wts = {"w_main_t": _main_rows_of(w_in_all, "regroup_w_in"), "w_dt_t": w_dt_t, "conv_w": jnp.transpose(conv_all, (1, 0, 2)).reshape(CONV_WIDTH, -1)}
    small = {"norm_mix_g": norm_mix_g, "conv_b": conv_b, "dt_bias": dt_bias, "a_log": a_log, "d_skip": d_skip,
             "ssm_norm_g": ssm_norm_g, "v_norm_g": v_norm_g, "v_norm_b": v_norm_b, "w_spatial": w_spatial[0],
             "b_spatial": b_spatial[0], "b_gates": b_gates, "norm_mlp_g": norm_mlp_g,
             "norm_final_g": norm_final_g.reshape(1, -1)}

    exchange = _Exchange(late_shards, late_lands)
    grad_x, g_mix = _local_step(x[0], loss_target[0], wts, small, exchange)

    out = {}
    all_small, large = exchange.finish(grad_x)
    for n, (slots, own, own_slot) in large.items():
        moments = [given["m_" + n][0], given["v_" + n][0]]
        if n == "w_in":
            moments = [mom.T for mom in moments]
        res = _adamw(slots, shard2d[n], *moments, "adamw_" + n, own=own, own_slot=own_slot)
        out[n] = [(r.T if n == "w_in" else r).reshape(shapes[n]) for r in res]

    last_small = _exchange_small({"norm_mix_g": g_mix}, _LAST_SMALL, "exchange_last")
    small["w_spatial"] = small["w_spatial"].reshape(GROUPS * CHUNK, CHUNK)
    params = {n: (w2d, given["m_" + n].reshape(w2d.shape), given["v_" + n].reshape(w2d.shape)) for n, w2d in small.items()}
    updated, (g_conv_full, loss_all) = _adamw_small(all_small, last_small, params, [(CONV_WIDTH, CONV_DIM), (1, LANES)],
                                                    "adamw_small")
    for n, res in updated.items():
        out[n] = [r.reshape(shapes[n]) for r in res]
    width = shapes["conv_w"][-1]
    g_conv = lax.dynamic_slice(g_conv_full, (0, dev * width), (CONV_WIDTH, width))
    res = _adamw(g_conv[None], conv_shard, m_conv_w.reshape(CONV_WIDTH, -1), v_conv_w.reshape(CONV_WIDTH, -1), "adamw_conv_w")
    out["conv_w"] = [r.reshape(shapes["conv_w"]) for r in res]

    loss = loss_all[0, 0]
    return (loss, grad_x[None], *[out[n][0] for n in names], *[out[n][1] for n in names],
            *[out[n][2] for n in names], *[out[n][3] for n in names])
```

```python
import functools
import math

import jax
import jax.numpy as jnp
from jax import lax
from jax.experimental import pallas as pl
from jax.experimental.pallas import tpu as pltpu

F32 = jnp.float32
BF16 = jnp.bfloat16
MESH = pl.DeviceIdType.MESH

D_MODEL = 1024
NORM_EPS = 1e-6
CHUNK = 128
GROUPS = 8
D_INNER = 2048
HEAD_DIM = 64
N_HEADS = 32
D_STATE = 128
CONV_WIDTH = 4
CONV_DIM = 4096
D_FF = 4096
GROUP_W = D_INNER // GROUPS
N_DEV = 8
N_CHIP = 4

ADAM_LR = 0.001
ADAM_B1 = 0.9
ADAM_B2 = 0.999
ADAM_EPS = 1e-08
ADAM_WD = 0.01
ADAM_STEP = 10

MAIN_W = 2 * D_MODEL + D_INNER + CONV_DIM + 2 * D_MODEL
COL_Z = 2048
COL_XBC = 4096
COL_GATE = 8192
DT_PAD = 128

LANES = 128
SUBLANES = 8
VMEM_BYTES_V7X = 64 * 1024 * 1024
VMEM_BODY_TEMP = 24 * 1024 * 1024


def _vmem_limit(block_bytes):
    return int(min(2 * block_bytes + VMEM_BODY_TEMP, VMEM_BYTES_V7X - 8 * 1024 * 1024))


def _nbytes(shape, dtype):
    return math.prod(shape) * jnp.dtype(dtype).itemsize


_HBM = pl.BlockSpec(memory_space=pl.ANY)


def _params(sem, block_bytes):
    return pltpu.CompilerParams(dimension_semantics=sem, vmem_limit_bytes=_vmem_limit(block_bytes))


def _sigmoid(x):
    return 1.0 / (1.0 + jnp.exp(-x))


def _softplus(x):
    e = jnp.exp(-jnp.abs(x))
    u = 1.0 + e
    log1p_e = jnp.where(u == 1.0, e, jnp.log(u) * (e / jnp.where(u == 1.0, 1.0, u - 1.0)))
    return jnp.maximum(x, 0.0) + log1p_e


_SQRT_HALF = 0.7071067811865476
_INV_SQRT_2PI = 0.3989422804014327


def _normal_cdf(x):
    return 0.5 * (1.0 + lax.erf(x * _SQRT_HALF))


def _gelu_grad(x, cdf):
    return cdf + x * jnp.exp(-0.5 * x * x) * _INV_SQRT_2PI


def _dot(a, b, dims):
    return lax.dot_general(a, b, (dims, ((), ())), preferred_element_type=F32)


_NN = ((1,), (0,))
_NT = ((1,), (1,))
_TN = ((0,), (0,))


def _split3(x):
    hi = x.astype(BF16)
    r1 = x - hi.astype(F32)
    mid = r1.astype(BF16)
    lo = (r1 - mid.astype(F32)).astype(BF16)
    return hi, mid, lo


def _dot_exact_rhs(x, e, dims):
    hi, mid, lo = _split3(x)
    return _dot(hi, e, dims) + _dot(mid, e, dims) + _dot(lo, e, dims)


def _dot_exact_lhs(e, x, dims):
    hi, mid, lo = _split3(x)
    return _dot(e, hi, dims) + _dot(e, mid, dims) + _dot(e, lo, dims)


def _tri(lower):
    r = lax.broadcasted_iota(jnp.int32, (CHUNK, CHUNK), 0)
    c = lax.broadcasted_iota(jnp.int32, (CHUNK, CHUNK), 1)
    return (r >= c) if lower else (r <= c)


def _matmul(a, b, *, mode, tm, tn, tk, out_dtypes, name, epilogue=None, extras=(), extra_specs=(), deps=(),
            carry=False):
    if mode == "nn":
        (m, k), (_, n) = a.shape, b.shape
    elif mode == "nt":
        (m, k), (n, _) = a.shape, b.shape
    else:
        (k, m), (_, n) = a.shape, b.shape
    assert m % tm == 0 and n % tn == 0 and k % tk == 0, (name, m, n, k, tm, tn, tk)
    nk = k // tk
    n_extra, n_out = len(extras), len(out_dtypes)
    first_out = 2 + n_extra + len(deps)
    dims = {"nn": _NN, "nt": _NT, "tn": _TN}[mode]
    if epilogue is None:
        def epilogue(acc, ex, outs, first):
            outs[0][...] = acc.astype(outs[0].dtype)

    def body(*refs):
        a_ref, b_ref = refs[0], refs[1]
        ex_refs = refs[2:2 + n_extra]
        outs = refs[first_out:first_out + n_out]
        first_tile = pl.program_id(0) == 0
        p = _dot(a_ref[...], b_ref[...], dims)
        if nk == 1:
            epilogue(p, ex_refs, outs, first_tile)
            return
        acc_ref = refs[first_out + n_out]
        kk = pl.program_id(2)

        @pl.when(kk == 0)
        def _():
            acc_ref[...] = p

        @pl.when(kk > 0)
        def _():
            acc_ref[...] += p

        @pl.when(kk == nk - 1)
        def _():
            epilogue(acc_ref[...], ex_refs, outs, first_tile)

    grid = (m // tm, n // tn, nk)

    if mode == "nn":
        a_spec = pl.BlockSpec((tm, tk), (lambda i, j, kk: (i, kk)))
        b_spec = pl.BlockSpec((tk, tn), (lambda i, j, kk: (kk, j)))
        a_blk, b_blk = (tm, tk), (tk, tn)
    elif mode == "nt":
        a_spec = pl.BlockSpec((tm, tk), (lambda i, j, kk: (i, kk)))
        b_spec = pl.BlockSpec((tn, tk), (lambda i, j, kk: (j, kk)))
        a_blk, b_blk = (tm, tk), (tn, tk)
    else:
        a_spec = pl.BlockSpec((tk, tm), (lambda i, j, kk: (kk, i)))
        b_spec = pl.BlockSpec((tk, tn), (lambda i, j, kk: (kk, j)))
        a_blk, b_blk = (tk, tm), (tk, tn)
    ex_specs = [pl.BlockSpec(shape, (lambda i, j, kk, f=f: f(i, j))) for shape, f in extra_specs]
    outs = [o if isinstance(o, tuple) else ((m, n), o, (tm, tn), lambda i, j: (i, j)) for o in out_dtypes]
    out_spec = [pl.BlockSpec(blk_shape, (lambda i, j, kk, f=f: f(i, j))) for _, _, blk_shape, f in outs]
    out_shape = [jax.ShapeDtypeStruct(shape, dt) for shape, dt, _, _ in outs]
    blk = (_nbytes(a_blk, a.dtype) + _nbytes(b_blk, b.dtype) + sum(_nbytes(s, F32) for s, _ in extra_specs)
           + sum(_nbytes(blk_shape, dt) for _, dt, blk_shape, _ in outs) + _nbytes((tm, tn), F32))
    order = ("arbitrary",) * 3 if carry else ("parallel", "parallel", "arbitrary")
    res = pl.pallas_call(
        body, name=name, grid=grid,
        in_specs=[a_spec, b_spec] + ex_specs + [_HBM] * len(deps), out_specs=out_spec, out_shape=out_shape,
        scratch_shapes=[pltpu.VMEM((tm, tn), F32)] if nk > 1 else [],
        compiler_params=_params(order, blk),
    )(a, b, *extras, *deps)
    return res[0] if n_out == 1 else res


ROW_TILE = 256


def _row_spec(width, col_block=0, tile=ROW_TILE):
    return pl.BlockSpec((tile, width), lambda i, cb=col_block: (i, cb))


def _vec_spec(width, col_block=0):
    return pl.BlockSpec((1, width), lambda i, cb=col_block: (0, cb))


def _rms_fwd(x, g, w_t, name, deps=()):
    t = x.shape[0]
    n_small = w_t.shape[0]
    tile = 2 * ROW_TILE

    def body(x_ref, g_ref, w_ref, *rest):
        h_ref, small_ref = rest[-2:]
        xv = x_ref[...]
        r = lax.rsqrt(jnp.mean(xv * xv, axis=-1, keepdims=True) + NORM_EPS)
        h = (xv * r * g_ref[...]).astype(BF16)
        h_ref[...] = h
        small_ref[...] = _dot(h, w_ref[...], _NT)

    return pl.pallas_call(
        body, name=name, grid=(t // tile,),
        in_specs=[_row_spec(D_MODEL, 0, tile), _vec_spec(D_MODEL), pl.BlockSpec((n_small, D_MODEL), lambda i: (0, 0))]
        + [_HBM] * len(deps),
        out_specs=[_row_spec(D_MODEL, 0, tile), _row_spec(n_small, 0, tile)],
        out_shape=[jax.ShapeDtypeStruct((t, D_MODEL), BF16), jax.ShapeDtypeStruct((t, n_small), F32)],
        compiler_params=_params(("parallel",), 3 * _nbytes((tile, D_MODEL), F32)),
    )(x, g, w_t, *deps)


def _rms_scale(xv):
    r = lax.rsqrt(jnp.mean(xv * xv, axis=-1, keepdims=True) + NORM_EPS)
    return r, xv * r


def _rms_pullback(xv, g, dh):
    r, xh = _rms_scale(xv)
    dyg = dh * g
    return r * (dyg - xh * jnp.mean(dyg * xh, axis=-1, keepdims=True)), jnp.sum(dh * xh, axis=0, keepdims=True)


def _zero_when(first, *refs):
    @pl.when(first)
    def _():
        for ref in refs:
            ref[...] = jnp.zeros_like(ref)


def _residual_rms_epilogue(acc, ex, outs, first):
    x1 = acc + ex[0][...]
    outs[0][...] = x1
    _, xh = _rms_scale(x1)
    outs[1][...] = (xh * ex[1][...]).astype(BF16)


def _loss_epilogue(acc, ex, outs, first):
    dx_ref, dxb_ref, gg_ref, sq_ref, tot_ref = outs
    gv = ex[1][...]
    r, xh = _rms_scale(acc + ex[0][...])
    err = xh * gv - ex[2][...]
    dy = err * (1.0 / D_MODEL)
    dyg = dy * gv
    dx = r * (dyg - xh * jnp.mean(dyg * xh, axis=-1, keepdims=True))
    dx_ref[...] = dx
    dxb_ref[...] = dx.astype(BF16)

    _zero_when(first, gg_ref, sq_ref)
    gg_ref[...] += jnp.sum(dy * xh, axis=0, keepdims=True)
    sq_ref[...] += jnp.sum(err * err, axis=0, keepdims=True)
    tot_ref[...] = jnp.broadcast_to(jnp.sum(sq_ref[...], axis=1, keepdims=True) * (0.5 / D_MODEL), tot_ref.shape)


def _rms_bwd_epilogue(dh, ex, outs, first):
    dx, gg = _rms_pullback(ex[0][...], ex[1][...], dh)
    dx = dx + ex[2][...]
    outs[0][...] = dx
    if len(outs) == 3:
        outs[1][...] = dx.astype(BF16)

    _zero_when(first, outs[-1])
    outs[-1][...] += gg


def _merge_epilogue(acc, ex, outs, first):
    outs[0][...] = acc.astype(outs[0].dtype)
    ga = _sigmoid(ex[1][...].astype(F32) + ex[3][...])
    gb = _sigmoid(ex[2][...].astype(F32) + ex[4][...])
    outs[1][...] = (ga * ex[0][...].astype(F32) + gb * acc).astype(BF16)


def _merge_bwd_epilogue(dm, ex, outs, first):
    dpa_ref, dpb_ref, dgl_ref, gb_ref = outs
    ga = _sigmoid(ex[2][...].astype(F32) + ex[4][...])
    gb = _sigmoid(ex[3][...].astype(F32) + ex[5][...])
    dpa_ref[...] = (dm * ga).astype(BF16)
    dpb_ref[...] = (dm * gb).astype(BF16)
    dla = dm * ex[0][...].astype(F32) * ga * (1.0 - ga)
    dlb = dm * ex[1][...].astype(F32) * gb * (1.0 - gb)
    dgl_ref[:, :D_MODEL] = dla.astype(BF16)
    dgl_ref[:, D_MODEL:] = dlb.astype(BF16)

    _zero_when(first, gb_ref)
    gb_ref[:, :D_MODEL] += jnp.sum(dla, axis=0, keepdims=True)
    gb_ref[:, D_MODEL:] += jnp.sum(dlb, axis=0, keepdims=True)


GMLP_TILE = 512
GMLP_NC = GMLP_TILE // CHUNK


def _gmlp_common(u_pre, v_pre, vg, vb):
    cdf_u, cdf_v = _normal_cdf(u_pre), _normal_cdf(v_pre)
    u = u_pre * cdf_u
    v = v_pre * cdf_v
    mu = jnp.mean(v, axis=-1, keepdims=True)
    vc = v - mu
    rstd = lax.rsqrt(jnp.mean(vc * vc, axis=-1, keepdims=True) + NORM_EPS)
    vh = vc * rstd
    vn = vh * vg + vb
    return u, vh, vn, rstd, cdf_u, cdf_v


def _chunks_to_lanes(x, g):
    return jnp.concatenate([x[c * CHUNK:(c + 1) * CHUNK, g * CHUNK:(g + 1) * CHUNK] for c in range(GMLP_NC)], axis=1)


def _gmlp_fwd(proj, vg, vb, wsp, bsp_t, name):
    t = proj.shape[0]

    def body(u_ref, v_ref, vg_ref, vb_ref, w_ref, b_ref, ya_ref):
        u, _, vn, _, _, _ = _gmlp_common(u_ref[...].astype(F32), v_ref[...].astype(F32), vg_ref[...], vb_ref[...])
        mask = _tri(True)
        bt = b_ref[...]
        for g in range(GROUPS):
            w = jnp.where(mask, w_ref[g], 0.0).astype(BF16)
            vcat = _chunks_to_lanes(vn, g).astype(BF16)
            s = _dot(w, vcat, _NN) + bt[:, g:g + 1]
            for c in range(GMLP_NC):
                rows, cols = slice(c * CHUNK, (c + 1) * CHUNK), slice(g * CHUNK, (g + 1) * CHUNK)
                ya_ref[rows, cols] = (u[rows, cols] * s[:, c * CHUNK:(c + 1) * CHUNK]).astype(BF16)

    return pl.pallas_call(
        body, name=name, grid=(t // GMLP_TILE,),
        in_specs=[_row_spec(D_MODEL, 0, GMLP_TILE), _row_spec(D_MODEL, 1, GMLP_TILE), _vec_spec(D_MODEL),
                  _vec_spec(D_MODEL), pl.BlockSpec((GROUPS, CHUNK, CHUNK), lambda i: (0, 0, 0)),
                  pl.BlockSpec((CHUNK, GROUPS), lambda i: (0, 0))],
        out_specs=_row_spec(D_MODEL, 0, GMLP_TILE),
        out_shape=jax.ShapeDtypeStruct((t, D_MODEL), BF16),
        compiler_params=_params(("parallel",), 3 * _nbytes((GMLP_TILE, D_MODEL), F32)),
    )(proj, proj, vg, vb, wsp, bsp_t)


def _gmlp_bwd(proj, dya, vg, vb, wsp, bsp_t, dproj, name):
    t = proj.shape[0]

    def body(u_ref, v_ref, dya_ref, vg_ref, vb_ref, w_ref, b_ref, dproj_in, duv_ref, gw_ref, gbt_ref, gvg_ref, gvb_ref,
             dvn_scr, du_scr):
        del dproj_in
        u_pre, v_pre = u_ref[...].astype(F32), v_ref[...].astype(F32)
        vgv = vg_ref[...]
        u, vh, vn, rstd, cdf_u, cdf_v = _gmlp_common(u_pre, v_pre, vgv, vb_ref[...])
        dya = dya_ref[...].astype(F32)
        mask = _tri(True)
        bt = b_ref[...]
        first = pl.program_id(0) == 0

        @pl.when(first)
        def _():
            gw_ref[...] = jnp.zeros_like(gw_ref)
            gbt_ref[...] = jnp.zeros_like(gbt_ref)
            gvg_ref[...] = jnp.zeros_like(gvg_ref)
            gvb_ref[...] = jnp.zeros_like(gvb_ref)

        lane = lax.broadcasted_iota(jnp.int32, (CHUNK, GROUPS), 1)
        gbt = jnp.zeros((CHUNK, GROUPS), F32)
        for g in range(GROUPS):
            w = jnp.where(mask, w_ref[g], 0.0).astype(BF16)
            vcat = _chunks_to_lanes(vn, g).astype(BF16)
            s = _dot(w, vcat, _NN) + bt[:, g:g + 1]
            ds = _chunks_to_lanes(dya * u, g)
            gbt = jnp.where(lane == g, jnp.sum(ds, axis=1, keepdims=True), gbt)
            dsb = ds.astype(BF16)
            gw_ref[g] += jnp.where(mask, _dot(dsb, vcat, _NT), 0.0)
            dv = _dot(w, dsb, _TN)
            for c in range(GMLP_NC):
                rows, cols = slice(c * CHUNK, (c + 1) * CHUNK), slice(g * CHUNK, (g + 1) * CHUNK)
                dvn_scr[rows, cols] = dv[:, c * CHUNK:(c + 1) * CHUNK]
                du_scr[rows, cols] = dya[rows, cols] * s[:, c * CHUNK:(c + 1) * CHUNK]
        gbt_ref[...] += gbt
        dvn = dvn_scr[...]
        gvg_ref[...] += jnp.sum(dvn * vh, axis=0, keepdims=True)
        gvb_ref[...] += jnp.sum(dvn, axis=0, keepdims=True)
        dvh = dvn * vgv
        dv = rstd * (dvh - jnp.mean(dvh, axis=-1, keepdims=True) - vh * jnp.mean(dvh * vh, axis=-1, keepdims=True))
        duv_ref[:, :D_MODEL] = (du_scr[...] * _gelu_grad(u_pre, cdf_u)).astype(BF16)
        duv_ref[:, D_MODEL:] = (dv * _gelu_grad(v_pre, cdf_v)).astype(BF16)

    return pl.pallas_call(
        body, name=name, grid=(t // GMLP_TILE,),
        in_specs=[_row_spec(D_MODEL, 0, GMLP_TILE), _row_spec(D_MODEL, 1, GMLP_TILE), _row_spec(D_MODEL, 0, GMLP_TILE),
                  _vec_spec(D_MODEL), _vec_spec(D_MODEL), pl.BlockSpec((GROUPS, CHUNK, CHUNK), lambda i: (0, 0, 0)),
                  pl.BlockSpec((CHUNK, GROUPS), lambda i: (0, 0)), pl.BlockSpec(memory_space=pl.ANY)],
        out_specs=[_row_spec(2 * D_MODEL, 0, GMLP_TILE), pl.BlockSpec((GROUPS, CHUNK, CHUNK), lambda i: (0, 0, 0)),
                   pl.BlockSpec((CHUNK, GROUPS), lambda i: (0, 0)), _vec_spec(D_MODEL), _vec_spec(D_MODEL)],
        out_shape=[jax.ShapeDtypeStruct(dproj.shape, BF16), jax.ShapeDtypeStruct((GROUPS, CHUNK, CHUNK), F32),
                   jax.ShapeDtypeStruct((CHUNK, GROUPS), F32), jax.ShapeDtypeStruct((1, D_MODEL), F32),
                   jax.ShapeDtypeStruct((1, D_MODEL), F32)],
        scratch_shapes=[pltpu.VMEM((GMLP_TILE, D_MODEL), F32), pltpu.VMEM((GMLP_TILE, D_MODEL), F32)],
        input_output_aliases={7: 0},
        compiler_params=_params(("arbitrary",), 6 * _nbytes((GMLP_TILE, D_MODEL), F32)),
    )(proj, proj, dya, vg, vb, wsp, bsp_t, dproj)


CONV_TILE = 1024
CONV_FWD_TILE = 2048
CONV_COLS = 1024
CONV_RB = 32
HALO = SUBLANES


def _conv_fwd(proj, cw, cb, name):
    t = proj.shape[0]
    nj = CONV_DIM // CONV_COLS
    xcb = COL_XBC // CONV_COLS
    before = 2 * HALO
    rb = CONV_FWD_TILE // before

    def body(x_ref, prev_ref, cw_ref, cb_ref, pre_ref, xc_ref):
        i = pl.program_id(1)
        cw_v = cw_ref[...]
        cb_v = cb_ref[...]
        for b in range(CONV_FWD_TILE // CONV_RB):
            if b == 0:
                prev = jnp.where(i > 0, prev_ref[...].astype(F32)[HALO:, :], 0.0)
                ext = jnp.concatenate([prev, x_ref[:CONV_RB, :].astype(F32)], axis=0)
            else:
                ext = x_ref[b * CONV_RB - before:(b + 1) * CONV_RB, :].astype(F32)[HALO:, :]
            pre = cb_v + cw_v[CONV_WIDTH - 1:CONV_WIDTH, :] * ext[HALO:, :]
            for k in range(CONV_WIDTH - 1):
                back = CONV_WIDTH - 1 - k
                pre = pre + cw_v[k:k + 1, :] * pltpu.roll(ext, back, 0)[HALO:, :]
            pre_ref[b * CONV_RB:(b + 1) * CONV_RB, :] = pre.astype(BF16)
            xc_ref[b * CONV_RB:(b + 1) * CONV_RB, :] = pre * _sigmoid(pre)

    tile = pl.BlockSpec((CONV_FWD_TILE, CONV_COLS), lambda j, i: (i, j))
    return pl.pallas_call(
        body, name=name, grid=(nj, t // CONV_FWD_TILE),
        in_specs=[pl.BlockSpec((CONV_FWD_TILE, CONV_COLS), lambda j, i: (i, xcb + j)),
                  pl.BlockSpec((before, CONV_COLS), lambda j, i: (jnp.maximum(i * rb - 1, 0), xcb + j)),
                  pl.BlockSpec((CONV_WIDTH, CONV_COLS), lambda j, i: (0, j)),
                  pl.BlockSpec((1, CONV_COLS), lambda j, i: (0, j))],
        out_specs=[tile, tile],
        out_shape=[jax.ShapeDtypeStruct((t, CONV_DIM), BF16), jax.ShapeDtypeStruct((t, CONV_DIM), F32)],
        compiler_params=_params(("parallel", "parallel"), 4 * _nbytes((CONV_FWD_TILE, CONV_COLS), F32)),
    )(proj, proj, cw, cb)


def _fold_rows(v):
    out = v[:SUBLANES]
    for r in range(1, v.shape[0] // SUBLANES):
        out = out + v[r * SUBLANES:(r + 1) * SUBLANES]
    return out


def _conv_bwd(proj, pre, dxc, cw, dproj, name):
    t = proj.shape[0]
    nj = CONV_DIM // CONV_COLS
    ni = t // CONV_TILE
    xcb = COL_XBC // CONV_COLS
    rb = CONV_TILE // HALO
    last_rb = t // HALO - 1

    def body(x_ref, p_ref, pnext_ref, d_ref, dnext_ref, cw_ref, dproj_in, dx_ref, gw_ref, gb_ref):
        del dproj_in
        i = pl.program_id(1)
        cw_v = cw_ref[...]

        def dpre_of(p, d):
            sg = _sigmoid(p)
            return d * sg * (1.0 + p * (1.0 - sg))

        @pl.when(i == 0)
        def _():
            gw_ref[...] = jnp.zeros_like(gw_ref)
            gb_ref[...] = jnp.zeros_like(gb_ref)

        head = dpre_of(pnext_ref[...].astype(F32)[:HALO, :], jnp.where(i < ni - 1, dnext_ref[...], 0.0))
        gb_acc = jnp.zeros((SUBLANES, CONV_COLS), F32)
        gw_acc = [jnp.zeros((SUBLANES, CONV_COLS), F32) for _ in range(CONV_WIDTH)]
        for b in reversed(range(CONV_TILE // CONV_RB)):
            rows = slice(b * CONV_RB, (b + 1) * CONV_RB)
            cur = dpre_of(p_ref[rows, :].astype(F32), d_ref[rows, :])
            ext = jnp.concatenate([cur, head], axis=0)
            xv = x_ref[rows, :].astype(F32)
            dx = None
            for k in range(CONV_WIDTH):
                shift = CONV_WIDTH - 1 - k
                win = cur if shift == 0 else pltpu.roll(ext, CONV_RB + HALO - shift, 0)[:CONV_RB, :]
                term = cw_v[k:k + 1, :] * win
                dx = term if dx is None else dx + term
                gw_acc[k] = gw_acc[k] + _fold_rows(win * xv)
            dx_ref[rows, :] = dx.astype(BF16)
            gb_acc = gb_acc + _fold_rows(cur)
            head = cur[:HALO]
        gb_ref[...] += jnp.sum(gb_acc, axis=0, keepdims=True)
        for k in range(CONV_WIDTH):
            gw_ref[k:k + 1, :] += jnp.sum(gw_acc[k], axis=0, keepdims=True)

    tile = pl.BlockSpec((CONV_TILE, CONV_COLS), lambda j, i: (i, j))
    after = pl.BlockSpec((HALO, CONV_COLS), lambda j, i: (jnp.minimum((i + 1) * rb, last_rb), j))
    after_pre = pl.BlockSpec((2 * HALO, CONV_COLS), lambda j, i: (jnp.minimum((i + 1) * (rb // 2), last_rb // 2), j))
    return pl.pallas_call(
        body, name=name, grid=(nj, ni),
        in_specs=[pl.BlockSpec((CONV_TILE, CONV_COLS), lambda j, i: (i, xcb + j)), tile, after_pre, tile, after,
                  pl.BlockSpec((CONV_WIDTH, CONV_COLS), lambda j, i: (0, j)),
                  pl.BlockSpec(memory_space=pl.ANY)],
        out_specs=[pl.BlockSpec((CONV_TILE, CONV_COLS), lambda j, i: (i, xcb + j)),
                   pl.BlockSpec((CONV_WIDTH, CONV_COLS), lambda j, i: (0, j)),
                   pl.BlockSpec((1, CONV_COLS), lambda j, i: (0, j))],
        out_shape=[jax.ShapeDtypeStruct(dproj.shape, BF16), jax.ShapeDtypeStruct((CONV_WIDTH, CONV_DIM), F32),
                   jax.ShapeDtypeStruct((1, CONV_DIM), F32)],
        input_output_aliases={6: 0},
        compiler_params=_params(("parallel", "arbitrary"), 4 * _nbytes((CONV_TILE, CONV_COLS), F32)),
    )(proj, pre, pre, dxc, dxc, cw, dproj)


def _ssd_decays(dt_raw, dtb, alog, e_bf, tril_bf):
    dtv = _softplus(dt_raw + dtb)
    a = -jnp.exp(alog)
    cs = _dot_exact_lhs(tril_bf, dtv * a, _NN)
    cs_last = cs[CHUNK - 1:CHUNK, :]
    stack = jnp.concatenate([dtv, jnp.exp(cs), jnp.exp(cs_last - cs)], axis=0)
    full = _head_expand(stack, e_bf)
    return dtv, a, cs, full[:CHUNK], full[CHUNK:2 * CHUNK], full[2 * CHUNK:]


def _split2(x):
    hi = x.astype(BF16)
    return hi, (x - hi.astype(F32)).astype(BF16)


def _head_expand(x, e_bf):
    hi, mid = _split2(x)
    return _dot(hi, e_bf, _NN) + _dot(mid, e_bf, _NN)


def _head_sums(x, e_bf):
    hi, mid = _split2(x)
    return _dot(hi, e_bf, _NT) + _dot(mid, e_bf, _NT)


def _head_mats(cs, cs_t, cb, h, mask):
    seg = cs[:, h:h + 1] - cs_t[h:h + 1, :]
    lmat = jnp.exp(jnp.where(mask, seg, -jnp.inf))
    return lmat, cb * lmat


def _ssd_fwd(xc, proj, dt_raw, dtb, alog, dskip_full, ng, e_bf, name):
    t = xc.shape[0]
    nc = t // CHUNK
    zcb = COL_Z // D_INNER

    def body(xc_ref, z_ref, dt_ref, dtb_ref, alog_ref, dsk_ref, ng_ref, e_ref, y_ref, yb_ref, sprev_ref, s_scr):
        @pl.when(pl.program_id(0) == 0)
        def _():
            s_scr[...] = jnp.zeros_like(s_scr)

        mask = _tri(True)
        tril_bf = mask.astype(BF16)
        e_v = e_ref[...]
        _, _, cs, dt_full, ecs_full, decay_full = _ssd_decays(dt_ref[...], dtb_ref[...], alog_ref[...], e_v, tril_bf)
        cs_t = cs.T
        sprev_ref[0] = s_scr[...]
        for g in range(GROUPS):
            gc = slice(g * GROUP_W, (g + 1) * GROUP_W)
            xs = xc_ref[:, gc]
            xdt = xs * dt_full[:, gc]
            xdt_b = xdt.astype(BF16)
            xdec = (xdt * decay_full[:, gc]).astype(BF16)
            bg = xc_ref[:, D_INNER + g * D_STATE:D_INNER + (g + 1) * D_STATE].astype(BF16)
            cg = xc_ref[:, D_INNER + GROUPS * D_STATE + g * D_STATE:D_INNER + GROUPS * D_STATE + (g + 1) * D_STATE].astype(BF16)
            cb = _dot(cg, bg, _NT)
            s_prev = s_scr[:, gc]
            y_off = ecs_full[:, gc] * _dot(cg, s_prev.astype(BF16), _NN)
            s_scr[:, gc] = s_prev * ecs_full[CHUNK - 1:CHUNK, gc] + _dot(bg, xdec, _TN)
            parts = []
            for r in range(GROUP_W // HEAD_DIM):
                h = g * (GROUP_W // HEAD_DIM) + r
                _, m = _head_mats(cs, cs_t, cb, h, mask)
                parts.append(_dot(m.astype(BF16), xdt_b[:, r * HEAD_DIM:(r + 1) * HEAD_DIM], _NN))
            yg = jnp.concatenate(parts, axis=1) + y_off + dsk_ref[:, gc] * xs
            y_ref[:, gc] = yg
            zv = z_ref[:, gc].astype(F32)
            ygate = yg * (zv * _sigmoid(zv))
            rstd = lax.rsqrt(jnp.mean(ygate * ygate, axis=-1, keepdims=True) + NORM_EPS)
            yb_ref[:, gc] = (ygate * rstd * ng_ref[:, gc]).astype(BF16)

    vec = lambda w: pl.BlockSpec((1, w), lambda i: (0, 0))
    blk = _nbytes((CHUNK, CONV_DIM), F32) + 3 * _nbytes((CHUNK, D_INNER), F32) + _nbytes((D_STATE, D_INNER), F32)
    return pl.pallas_call(
        body, name=name, grid=(nc,),
        in_specs=[pl.BlockSpec((CHUNK, CONV_DIM), lambda i: (i, 0)), pl.BlockSpec((CHUNK, D_INNER), lambda i: (i, zcb)),
                  pl.BlockSpec((CHUNK, DT_PAD), lambda i: (i, 0)), vec(DT_PAD), vec(DT_PAD), vec(D_INNER), vec(D_INNER),
                  pl.BlockSpec((DT_PAD, D_INNER), lambda i: (0, 0))],
        out_specs=[pl.BlockSpec((CHUNK, D_INNER), lambda i: (i, 0)), pl.BlockSpec((CHUNK, D_INNER), lambda i: (i, 0)),
                   pl.BlockSpec((1, D_STATE, D_INNER), lambda i: (i, 0, 0))],
        out_shape=[jax.ShapeDtypeStruct((t, D_INNER), F32), jax.ShapeDtypeStruct((t, D_INNER), BF16),
                   jax.ShapeDtypeStruct((nc, D_STATE, D_INNER), F32)],
        scratch_shapes=[pltpu.VMEM((D_STATE, D_INNER), F32)],
        compiler_params=_params(("arbitrary",), blk),
    )(xc, proj, dt_raw, dtb, alog, dskip_full, ng, e_bf)


def _ssd_bwd(dyb, y, xc, proj, dt_raw, sprev, dtb, alog, dskip_full, ng, e_bf, h, dproj, name):
    t = xc.shape[0]
    nc = t // CHUNK
    zcb = COL_Z // D_INNER
    hpg = GROUP_W // HEAD_DIM
    rev = lambda i: nc - 1 - i

    def body(dyb_ref, y_ref, xc_ref, z_ref, dt_ref, sprev_ref, dtb_ref, alog_ref, dsk_ref, ng_ref, e_ref, h_ref, dproj_in,
             dz_ref, dxc_ref, ddt_ref, gng_ref, gdsk_ref, galog_ref, gdtb_ref, gwdt_ref, ds_scr, sums_scr):
        del dproj_in

        @pl.when(pl.program_id(0) == 0)
        def _():
            ds_scr[...] = jnp.zeros_like(ds_scr)
            gng_ref[...] = jnp.zeros_like(gng_ref)
            gdsk_ref[...] = jnp.zeros_like(gdsk_ref)
            galog_ref[...] = jnp.zeros_like(galog_ref)
            gdtb_ref[...] = jnp.zeros_like(gdtb_ref)
            gwdt_ref[...] = jnp.zeros_like(gwdt_ref)

        mask = _tri(True)
        tril_bf = mask.astype(BF16)
        triu_bf = _tri(False).astype(BF16)
        e_v = e_ref[...]
        dt_in = dt_ref[...] + dtb_ref[...]
        dtv, a, cs, dt_full, ecs_full, decay_full = _ssd_decays(dt_ref[...], dtb_ref[...], alog_ref[...], e_v, tril_bf)
        cs_t = cs.T

        lane_h = lax.broadcasted_iota(jnp.int32, (CHUNK, DT_PAD), 1)
        sub_h = lax.broadcasted_iota(jnp.int32, (DT_PAD, CHUNK), 0)
        dcs_rows = jnp.zeros((CHUNK, DT_PAD), F32)
        dcs_cols_t = jnp.zeros((DT_PAD, CHUNK), F32)
        last_cols, dsk_cols = [], []
        for g in range(GROUPS):
            gc = slice(g * GROUP_W, (g + 1) * GROUP_W)
            b_cols = slice(D_INNER + g * D_STATE, D_INNER + (g + 1) * D_STATE)
            c_cols = slice(D_INNER + GROUPS * D_STATE + g * D_STATE, D_INNER + GROUPS * D_STATE + (g + 1) * D_STATE)
            xs = xc_ref[:, gc]
            xdt = xs * dt_full[:, gc]
            xdt_b = xdt.astype(BF16)
            xdec = xdt * decay_full[:, gc]
            xdec_b = xdec.astype(BF16)
            zv = z_ref[:, gc].astype(F32)
            sg = _sigmoid(zv)
            gate = zv * sg
            yv = y_ref[:, gc]
            dybv = dyb_ref[:, gc].astype(F32)
            ygate = yv * gate
            rstd = lax.rsqrt(jnp.mean(ygate * ygate, axis=-1, keepdims=True) + NORM_EPS)
            yn = ygate * rstd
            gng_ref[:, gc] += jnp.sum(dybv * yn, axis=0, keepdims=True)
            dyn = dybv * ng_ref[:, gc]
            dyg = rstd * (dyn - yn * jnp.mean(dyn * yn, axis=-1, keepdims=True))
            dz_ref[:, gc] = (dyg * yv * sg * (1.0 + zv * (1.0 - sg))).astype(BF16)
            dy = dyg * gate
            dy_b = dy.astype(BF16)
            dyo = dy * ecs_full[:, gc]
            dyo_b = dyo.astype(BF16)
            dsk_cols.append(jnp.sum(dy * xs, axis=0, keepdims=True))

            bg = xc_ref[:, b_cols].astype(BF16)
            cg = xc_ref[:, c_cols].astype(BF16)
            s_prev = sprev_ref[0, :, gc]
            s_prev_b = s_prev.astype(BF16)
            dsg = ds_scr[:, gc]
            dsg_b = dsg.astype(BF16)
            cb = _dot(cg, bg, _NT)
            c_s = _dot(cg, s_prev_b, _NN)
            b_ds = _dot(bg, dsg_b, _NN)
            dcb = jnp.zeros((CHUNK, CHUNK), F32)
            parts = []
            for r in range(hpg):
                h = g * hpg + r
                hc = slice(r * HEAD_DIM, (r + 1) * HEAD_DIM)
                lmat, m = _head_mats(cs, cs_t, cb, h, mask)
                dm = _dot(dy_b[:, hc], xdt_b[:, hc], _NT)
                parts.append(_dot(m.astype(BF16), dy_b[:, hc], _TN))
                dcb = dcb + dm * lmat
                w = dm * m
                dcs_rows = jnp.where(lane_h == h, jnp.sum(w, axis=1, keepdims=True), dcs_rows)
                dcs_cols_t = jnp.where(sub_h == h, jnp.sum(w, axis=0, keepdims=True), dcs_cols_t)
            dxdt = jnp.concatenate(parts, axis=1) + decay_full[:, gc] * b_ds
            dcb_b = dcb.astype(BF16)
            dxc_ref[:, c_cols] = _dot(dcb_b, bg, _NN) + _dot(dyo_b, s_prev_b, _NT)
            dxc_ref[:, b_cols] = _dot(dcb_b, cg, _TN) + _dot(xdec_b, dsg_b, _NT)
            cdec = ecs_full[CHUNK - 1:CHUNK, gc]
            ds_scr[:, gc] = _dot(cg, dyo_b, _TN) + cdec * dsg
            dxc_ref[:, gc] = dxdt * dt_full[:, gc] + dsk_ref[:, gc] * dy
            dec_prod = xdec * b_ds
            sums_scr[:CHUNK, gc] = dyo * c_s - dec_prod
            sums_scr[CHUNK:, gc] = dxdt * xs
            last_cols.append(jnp.sum(dec_prod, axis=0, keepdims=True) + cdec * jnp.sum(dsg * s_prev, axis=0, keepdims=True))
        t_sums = _head_sums(sums_scr[...], e_v)
        tail = jnp.concatenate([jnp.concatenate(last_cols, axis=1), jnp.concatenate(dsk_cols, axis=1),
                                jnp.zeros((SUBLANES - 2, D_INNER), F32)], axis=0)
        t_tail = _dot_exact_rhs(tail, e_v, _NT)
        gdsk_ref[...] += t_tail[1:2, :]
        row = lax.broadcasted_iota(jnp.int32, (CHUNK, DT_PAD), 0)
        dcs = dcs_rows - dcs_cols_t.T + t_sums[:CHUNK] + jnp.where(row == CHUNK - 1, t_tail[0:1, :], 0.0)
        dda = _dot_exact_lhs(triu_bf, dcs, _NN)
        galog_ref[...] += jnp.sum(dda * dtv, axis=0, keepdims=True) * a
        ddt = dda * a + t_sums[CHUNK:]
        ddt_raw = jnp.where(lane_h < N_HEADS, ddt * _sigmoid(dt_in), 0.0)
        gdtb_ref[...] += jnp.sum(ddt_raw, axis=0, keepdims=True)
        ddt_b = ddt_raw.astype(BF16)
        ddt_ref[...] = ddt_b
        gwdt_ref[...] += _dot(ddt_b, h_ref[...], _TN)

    vec = lambda w: pl.BlockSpec((1, w), lambda i: (0, 0))
    blk = (2 * _nbytes((CHUNK, CONV_DIM), F32) + 4 * _nbytes((CHUNK, D_INNER), F32) + 4 * _nbytes((D_STATE, D_INNER), F32))
    return pl.pallas_call(
        body, name=name, grid=(nc,),
        in_specs=[pl.BlockSpec((CHUNK, D_INNER), lambda i: (rev(i), 0)), pl.BlockSpec((CHUNK, D_INNER), lambda i: (rev(i), 0)),
                  pl.BlockSpec((CHUNK, CONV_DIM), lambda i: (rev(i), 0)), pl.BlockSpec((CHUNK, D_INNER), lambda i: (rev(i), zcb)),
                  pl.BlockSpec((CHUNK, DT_PAD), lambda i: (rev(i), 0)), pl.BlockSpec((1, D_STATE, D_INNER), lambda i: (rev(i), 0, 0)),
                  vec(DT_PAD), vec(DT_PAD), vec(D_INNER), vec(D_INNER), pl.BlockSpec((DT_PAD, D_INNER), lambda i: (0, 0)),
                  pl.BlockSpec((CHUNK, D_MODEL), lambda i: (rev(i), 0)), pl.BlockSpec(memory_space=pl.ANY)],
        out_specs=[pl.BlockSpec((CHUNK, D_INNER), lambda i: (rev(i), zcb)), pl.BlockSpec((CHUNK, CONV_DIM), lambda i: (rev(i), 0)),
                   pl.BlockSpec((CHUNK, DT_PAD), lambda i: (rev(i), 0)), vec(D_INNER), vec(DT_PAD), vec(DT_PAD), vec(DT_PAD),
                   pl.BlockSpec((DT_PAD, D_MODEL), lambda i: (0, 0))],
        out_shape=[jax.ShapeDtypeStruct(dproj.shape, BF16), jax.ShapeDtypeStruct((t, CONV_DIM), F32),
                   jax.ShapeDtypeStruct((t, DT_PAD), BF16), jax.ShapeDtypeStruct((1, D_INNER), F32),
                   jax.ShapeDtypeStruct((1, DT_PAD), F32), jax.ShapeDtypeStruct((1, DT_PAD), F32),
                   jax.ShapeDtypeStruct((1, DT_PAD), F32), jax.ShapeDtypeStruct((DT_PAD, D_MODEL), F32)],
        scratch_shapes=[pltpu.VMEM((D_STATE, D_INNER), F32), pltpu.VMEM((2 * CHUNK, D_INNER), F32)],
        input_output_aliases={12: 0},
        compiler_params=_params(("arbitrary",), blk),
    )(dyb, y, xc, proj, dt_raw, sprev, dtb, alog, dskip_full, ng, e_bf, h, dproj)


def _mesh_pos():
    return lax.axis_index("x"), lax.axis_index("y"), lax.axis_index("c")


def _other_chips(x, y):
    return [(1 - x, y), (x, 1 - y), (1 - x, 1 - y)]


def _all_peers(x, y, c):
    peers = []
    for k in range(1, N_DEV):
        fx, fy, fc = (k >> 2) & 1, (k >> 1) & 1, k & 1
        px, py, pc = x + fx - 2 * x * fx, y + fy - 2 * y * fy, c + fc - 2 * c * fc
        peers.append(((px, py, pc), 4 * px + 2 * py + pc))
    return peers


def _all_gather(shards, name, own_only=()):
    n, n_own = len(shards), len(own_only)

    def body(*refs):
        ins, own_ins = refs[:n], refs[n:n + n_own]
        outs, own_outs = refs[n + n_own:2 * n + n_own], refs[2 * n + n_own:2 * (n + n_own)]
        send_sems, recv_sems, local_sems = refs[2 * (n + n_own):]
        x, y, c = _mesh_pos()
        me, sibling = (x, y, c), (x, y, 1 - c)
        chips = _other_chips(x, y)

        def slot(p):
            return 4 * p[0] + 2 * p[1] + p[2]

        def copy(a, k, block, to, src=None):
            dst = outs[a].at[slot(block)]
            return pltpu.make_async_remote_copy(
                src_ref=dst if src is None else src, dst_ref=dst, send_sem=send_sems.at[a * 7 + k],
                recv_sem=recv_sems.at[a * 7 + k], device_id=to, device_id_type=MESH)

        started = []
        own = []
        for a in range(n_own):
            mine = pltpu.make_async_copy(own_ins[a], own_outs[a].at[slot(me)], local_sems.at[n + a])
            mine.start()
            own.append(mine)
        for a in range(n):
            mine = pltpu.make_async_copy(ins[a], outs[a].at[slot(me)], local_sems.at[a])
            mine.start()
            own.append(mine)
            first = [copy(a, 0, me, sibling, src=ins[a])]
            first += [copy(a, 1 + j, me, (*chip, c), src=ins[a]) for j, chip in enumerate(chips)]
            for cp in first:
                cp.start()
            started += first
        for a in range(n):
            for j, chip in enumerate(chips):
                copy(a, 1 + j, (*chip, c), me).wait_recv()
                fwd = copy(a, 4 + j, (*chip, c), sibling)
                fwd.start()
                started.append(fwd)
        for a in range(n):
            copy(a, 0, sibling, me).wait_recv()
            for j, chip in enumerate(chips):
                copy(a, 4 + j, (*chip, 1 - c), me).wait_recv()
        for cp in started:
            cp.wait_send()
        for mine in own:
            mine.wait()

    return pl.pallas_call(
        body, name=name,
        in_specs=[_HBM] * (n + n_own), out_specs=[_HBM] * (n + n_own),
        out_shape=[jax.ShapeDtypeStruct((N_DEV,) + s.shape, s.dtype) for s in (*shards, *own_only)],
        scratch_shapes=[pltpu.SemaphoreType.DMA((7 * n,)), pltpu.SemaphoreType.DMA((7 * n,)),
                        pltpu.SemaphoreType.DMA((n + n_own,))],
    )(*shards, *own_only)


_SMALL_ROWS = (("norm_mix_g", 8), ("conv_b", 32), ("dt_bias", 1), ("a_log", 1), ("d_skip", 1), ("ssm_norm_g", 16),
               ("v_norm_g", 8), ("v_norm_b", 8), ("w_spatial", 1024), ("b_spatial", 8), ("b_gates", 16), ("norm_mlp_g", 8),
               ("norm_final_g", 8), ("conv_w", 128), ("loss", 1))
_LAST_SMALL = (("norm_mix_g", 8),)


def _packed_rows(table):
    return -(-sum(r for _, r in table) // SUBLANES) * SUBLANES


def _small_offsets(table=_SMALL_ROWS):
    offs, r = {}, 0
    for name, rows in table:
        offs[name] = r
        r += rows
    return offs


def _rows_from(src_ref, dst_ref, r0):
    k, w = src_ref.shape
    if w <= LANES:
        dst_ref[r0:r0 + k, 0:w] = src_ref[...]
        return
    per = w // LANES
    for i in range(k):
        for j in range(per):
            dst_ref[r0 + i * per + j:r0 + i * per + j + 1, :] = src_ref[i:i + 1, j * LANES:(j + 1) * LANES]


def _rows_to(src_ref, r0, dst_ref):
    k, w = dst_ref.shape
    if w <= LANES:
        dst_ref[...] = src_ref[r0:r0 + k, 0:w]
        return
    per = w // LANES
    for i in range(k):
        for j in range(per):
            dst_ref[i:i + 1, j * LANES:(j + 1) * LANES] = src_ref[r0 + i * per + j:r0 + i * per + j + 1, :]


def _pack_small(grads, slot_idx, name):
    names = [n for n, _ in _SMALL_ROWS if n in grads]
    offs = _small_offsets()
    rows = _packed_rows(_SMALL_ROWS)

    def body(slot_ref, *refs):
        del slot_ref
        ins, (packed_ref, land_ref) = refs[:len(names)], refs[len(names):]
        packed_ref[...] = jnp.zeros_like(packed_ref)
        for n, ref in zip(names, ins):
            _rows_from(ref, packed_ref, offs[n])
        land_ref[0] = packed_ref[...]

    whole = lambda shape: pl.BlockSpec(shape, lambda i, slot_ref: (0,) * len(shape))
    grid_spec = pltpu.PrefetchScalarGridSpec(
        num_scalar_prefetch=1, grid=(1,), in_specs=[whole(grads[n].shape) for n in names],
        out_specs=[whole((rows, LANES)), pl.BlockSpec((1, rows, LANES), lambda i, slot_ref: (slot_ref[0], 0, 0))])
    return pl.pallas_call(
        body, name=name, grid_spec=grid_spec,
        out_shape=[jax.ShapeDtypeStruct((rows, LANES), F32), jax.ShapeDtypeStruct((N_DEV, rows, LANES), F32)],
    )(slot_idx, *[grads[n] for n in names])


def _exchange_small(grads, table, name):
    names = [n for n, _ in table]
    offs = _small_offsets(table)
    n_in = len(names)
    packed_rows = _packed_rows(table)

    def body(*refs):
        ins, out_ref = refs[:n_in], refs[n_in]
        packed, send_sems, recv_sems, local_sem = refs[n_in + 1:]
        packed[...] = jnp.zeros_like(packed)
        for n, ref in zip(names, ins):
            _rows_from(ref, packed, offs[n])
        x, y, c = _mesh_pos()
        my_slot = 4 * x + 2 * y + c
        mine = pltpu.make_async_copy(packed, out_ref.at[my_slot], local_sem)
        mine.start()
        copies = []
        for k, (peer, peer_slot) in enumerate(_all_peers(x, y, c)):
            sems = dict(send_sem=send_sems.at[k], recv_sem=recv_sems.at[k], device_id=peer, device_id_type=MESH)
            send = pltpu.make_async_remote_copy(src_ref=packed, dst_ref=out_ref.at[my_slot], **sems)
            send.start()
            copies.append((send, pltpu.make_async_remote_copy(src_ref=packed, dst_ref=out_ref.at[peer_slot], **sems)))
        for send, recv in copies:
            send.wait_send()
            recv.wait_recv()
        mine.wait()

    return pl.pallas_call(
        body, name=name, in_specs=[pl.BlockSpec(memory_space=pltpu.VMEM)] * n_in, out_specs=_HBM,
        out_shape=jax.ShapeDtypeStruct((N_DEV, packed_rows, LANES), F32),
        scratch_shapes=[pltpu.VMEM((packed_rows, LANES), F32), pltpu.SemaphoreType.DMA((N_DEV - 1,)),
                        pltpu.SemaphoreType.DMA((N_DEV - 1,)), pltpu.SemaphoreType.DMA],
    )(*[grads[n] for n in names])


def _swap_with_sibling(grads, name):
    n = len(grads)

    def body(*refs):
        ins, outs = refs[:n], refs[n:2 * n]
        send_sems, recv_sems = refs[2 * n:]
        x, y, c = _mesh_pos()
        copies = []
        for a in range(n):
            for k in range(N_CHIP):
                cp = pltpu.make_async_remote_copy(
                    src_ref=ins[a].at[(1 - c) + 2 * k], dst_ref=outs[a].at[k], send_sem=send_sems.at[a * N_CHIP + k],
                    recv_sem=recv_sems.at[a * N_CHIP + k], device_id=(x, y, 1 - c), device_id_type=MESH)
                cp.start()
                copies.append(cp)
        for cp in copies:
            cp.wait()

    return pl.pallas_call(
        body, name=name, in_specs=[_HBM] * n, out_specs=[_HBM] * n,
        out_shape=[jax.ShapeDtypeStruct((N_CHIP,) + g.shape[1:], g.dtype) for g in grads],
        scratch_shapes=[pltpu.SemaphoreType.DMA((N_CHIP * n,)), pltpu.SemaphoreType.DMA((N_CHIP * n,))],
    )(*grads)


_SEM = pl.BlockSpec(memory_space=pltpu.SEMAPHORE)
_IN_HBM = pl.BlockSpec(memory_space=pltpu.HBM)
_EFFECT = pltpu.SideEffectType.DATAFLOW_SIDE_EFFECTING


def _in_hbm(a):
    return pltpu.with_memory_space_constraint(a, pltpu.HBM)


def _gather_copies(ins, lands, send_sems, recv_sems):
    x, y, c = _mesh_pos()
    my_slot = 4 * x + 2 * y + c
    pairs = []
    for a in range(len(ins)):
        for k, (peer, peer_slot) in enumerate(_all_peers(x, y, c)):
            sems = dict(send_sem=send_sems.at[a * (N_DEV - 1) + k], recv_sem=recv_sems.at[a * (N_DEV - 1) + k],
                        device_id=peer, device_id_type=MESH)
            pairs.append((pltpu.make_async_remote_copy(src_ref=ins[a], dst_ref=lands[a].at[my_slot], **sems),
                          pltpu.make_async_remote_copy(src_ref=ins[a], dst_ref=lands[a].at[peer_slot], **sems)))
    return pairs


def _scatter_copies(ins, lands, send_sems, recv_sems):
    x, y, c = _mesh_pos()
    my_chip = 2 * x + y
    pairs = []
    for a in range(len(ins)):
        for j, chip in enumerate(_other_chips(x, y)):
            there = 2 * chip[0] + chip[1]
            sems = dict(send_sem=send_sems.at[a * 3 + j], recv_sem=recv_sems.at[a * 3 + j],
                        device_id=(*chip, c), device_id_type=MESH)
            pairs.append((pltpu.make_async_remote_copy(src_ref=ins[a].at[there], dst_ref=lands[a].at[my_chip], **sems),
                          pltpu.make_async_remote_copy(src_ref=ins[a].at[my_chip], dst_ref=lands[a].at[there], **sems)))
    return pairs


def _split_start(srcs, lands, copies, per_array, name):
    n = len(srcs)

    def body(*refs):
        ins, land_refs = refs[:n], refs[n:2 * n]
        send_sems, recv_sems = refs[2 * n], refs[2 * n + 1]
        token = refs[-1]
        for send, _ in copies(ins, land_refs, send_sems, recv_sems):
            send.start()
        token[...] = jnp.zeros_like(token)

    outs = pl.pallas_call(
        body, name=name,
        out_shape=(pltpu.SemaphoreType.DMA((per_array * n,)), pltpu.SemaphoreType.DMA((per_array * n,)),
                   *[pltpu.HBM(s.shape, s.dtype) for s in srcs], *[pltpu.HBM(l.shape, l.dtype) for l in lands],
                   jax.ShapeDtypeStruct((SUBLANES, LANES), F32)),
        in_specs=[_IN_HBM] * (2 * n),
        out_specs=(_SEM, _SEM, *[_IN_HBM] * (2 * n), pl.BlockSpec(memory_space=pltpu.VMEM)),
        input_output_aliases={i: 2 + i for i in range(2 * n)},
        compiler_params=pltpu.CompilerParams(has_side_effects=_EFFECT),
    )(*[_in_hbm(s) for s in srcs], *[_in_hbm(l) for l in lands])
    return outs[0], outs[1], list(outs[2:2 + n]), list(outs[2 + n:2 + 2 * n]), outs[-1]


def _split_wait(started, copies, after, name):
    send_sems, recv_sems, srcs, lands, _ = started
    n = len(srcs)

    def body(*refs):
        ins, land_refs = refs[:n], refs[n:2 * n]
        for send, recv in copies(ins, land_refs, refs[2 * n], refs[2 * n + 1]):
            send.wait_send()
            recv.wait_recv()

    outs = pl.pallas_call(
        body, name=name,
        out_shape=(*[pltpu.HBM(s.shape, s.dtype) for s in srcs], *[pltpu.HBM(l.shape, l.dtype) for l in lands]),
        in_specs=[_IN_HBM] * (2 * n) + [_SEM, _SEM, _HBM],
        out_specs=[_IN_HBM] * (2 * n),
        input_output_aliases={i: i for i in range(2 * n)},
        compiler_params=pltpu.CompilerParams(has_side_effects=_EFFECT),
    )(*srcs, *lands, send_sems, recv_sems, after)
    return list(outs[:n]), list(outs[n:])


def _ew_block(rows, cols, slots):
    budget = 8 * 1024 * 1024
    br, bc = rows, cols
    while slots * br * bc * 4 > budget:
        if br % 2 == 0 and (br // 2) % (2 * SUBLANES) == 0:
            br //= 2
        elif bc % 2 == 0 and (bc // 2) % LANES == 0:
            bc //= 2
        else:
            break
    return br, bc


def _add_sibling(grads, recv, c_idx, name):
    _, rows, cols = grads.shape
    br, bc = _ew_block(rows, cols, 3)

    def body(c_ref, g_ref, r_ref, out_ref):
        del c_ref
        out_ref[...] = (g_ref[...].astype(F32) + r_ref[...].astype(F32)).astype(out_ref.dtype)

    grid_spec = pltpu.PrefetchScalarGridSpec(
        num_scalar_prefetch=1, grid=(N_CHIP, rows // br, cols // bc),
        in_specs=[pl.BlockSpec((1, br, bc), lambda k, i, j, c_ref: (c_ref[0] + 2 * k, i, j)),
                  pl.BlockSpec((1, br, bc), lambda k, i, j, c_ref: (k, i, j))],
        out_specs=pl.BlockSpec((1, br, bc), lambda k, i, j, c_ref: (k, i, j)))
    return pl.pallas_call(
        body, name=name, grid_spec=grid_spec, out_shape=jax.ShapeDtypeStruct((N_CHIP, rows, cols), grads.dtype),
        compiler_params=_params(("parallel", "parallel", "parallel"), 3 * _nbytes((br, bc), F32)),
    )(c_idx, grads, recv)


def _adam_math(g, w, m, v):
    m2 = ADAM_B1 * m + (1.0 - ADAM_B1) * g
    v2 = ADAM_B2 * v + (1.0 - ADAM_B2) * (g * g)
    m_hat = m2 * (1.0 / (1.0 - ADAM_B1 ** ADAM_STEP))
    v_hat = v2 * (1.0 / (1.0 - ADAM_B2 ** ADAM_STEP))
    return -ADAM_LR * (m_hat / (jnp.sqrt(v_hat) + ADAM_EPS) + ADAM_WD * w), m2, v2


def _adamw(slots, w, m, v, name, own=None, own_slot=None):
    ns, rows, cols = slots.shape
    br, bc = _ew_block(rows, cols, 2 * ns + 7)

    def update(g, w_ref, m_ref, v_ref, g_ref, d_ref, m2_ref, v2_ref):
        g_ref[...] = g
        d_ref[...], m2_ref[...], v2_ref[...] = _adam_math(g, w_ref[...], m_ref[...], v_ref[...])

    out_shape = [jax.ShapeDtypeStruct((rows, cols), F32)] * 4
    params = _params(("parallel", "parallel"), (2 * ns + 7) * _nbytes((br, bc), F32))
    grid = (rows // br, cols // bc)
    if own is None:
        def body(s_ref, *rest):
            g = s_ref[0].astype(F32)
            for k in range(1, ns):
                g = g + s_ref[k].astype(F32)
            update(g, *rest)

        blk = pl.BlockSpec((br, bc), lambda i, j: (i, j))
        return pl.pallas_call(
            body, name=name, grid=grid,
            in_specs=[pl.BlockSpec((ns, br, bc), lambda i, j: (0, i, j)), blk, blk, blk], out_specs=[blk] * 4,
            out_shape=out_shape, compiler_params=params,
        )(slots, w, m, v)

    def body_own(slot_ref, s_ref, o_ref, *rest):
        g = None
        for k in range(ns):
            term = jnp.where(slot_ref[0] == k, o_ref[k].astype(F32), s_ref[k].astype(F32))
            g = term if g is None else g + term
        update(g, *rest)

    blk = pl.BlockSpec((br, bc), lambda i, j, slot_ref: (i, j))
    stack = pl.BlockSpec((ns, br, bc), lambda i, j, slot_ref: (0, i, j))
    grid_spec = pltpu.PrefetchScalarGridSpec(num_scalar_prefetch=1, grid=grid, in_specs=[stack, stack, blk, blk, blk],
                                             out_specs=[blk] * 4)
    return pl.pallas_call(body_own, name=name, grid_spec=grid_spec, out_shape=out_shape, compiler_params=params,
                          )(own_slot, slots, own, w, m, v)


def _adamw_small(all_g, last_g, params, extra_shapes, name):
    names = [n for n, _ in _SMALL_ROWS if n in params]
    extras = [n for n, _ in _SMALL_ROWS if n not in params]
    offs = _small_offsets()
    n_p = len(names)

    def body(*refs):
        s_ref, last_ref = refs[0], refs[1]
        wmv = refs[2:2 + 3 * n_p]
        outs = refs[2 + 3 * n_p:2 + 7 * n_p]
        extra_refs = refs[2 + 7 * n_p:2 + 7 * n_p + len(extras)]
        summed = refs[-1]
        g, g_last = s_ref[0], last_ref[0]
        for k in range(1, N_DEV):
            g, g_last = g + s_ref[k], g_last + last_ref[k]
        summed[...] = g
        last_offs = _small_offsets(_LAST_SMALL)
        for n, rows in _LAST_SMALL:
            summed[offs[n]:offs[n] + rows, :] = g_last[last_offs[n]:last_offs[n] + rows, :]
        for i, n in enumerate(names):
            w_ref, m_ref, v_ref = wmv[3 * i:3 * i + 3]
            g_ref, d_ref, m2_ref, v2_ref = outs[4 * i:4 * i + 4]
            _rows_to(summed, offs[n], g_ref)
            d_ref[...], m2_ref[...], v2_ref[...] = _adam_math(g_ref[...], w_ref[...], m_ref[...], v_ref[...])
        for n, ref in zip(extras, extra_refs):
            _rows_to(summed, offs[n], ref)

    flat = [a for n in names for a in params[n]]
    out_shape = [jax.ShapeDtypeStruct(params[n][0].shape, F32) for n in names for _ in range(4)]
    out_shape += [jax.ShapeDtypeStruct(s, F32) for s in extra_shapes]
    vmem = pl.BlockSpec(memory_space=pltpu.VMEM)
    res = pl.pallas_call(
        body, name=name, in_specs=[vmem] * (2 + len(flat)), out_specs=[vmem] * len(out_shape), out_shape=out_shape,
        scratch_shapes=[pltpu.VMEM(all_g.shape[1:], F32)],
        compiler_params=pltpu.CompilerParams(vmem_limit_bytes=_vmem_limit(_nbytes(all_g.shape, F32))),
    )(all_g, last_g, *flat)
    return {n: res[4 * i:4 * i + 4] for i, n in enumerate(names)}, res[4 * n_p:]


def _mm_tiles(mode, m, n, k):
    tn = min(n, 1024)
    if mode == "tn":
        return min(m, 1024), tn, min(k, 4096)
    if k <= 1024:
        return min(m, 2048), tn, k
    if k <= 2048:
        return min(m, 1024), tn, k
    if k <= 4096:
        return min(m, 512), tn, k
    return min(m, 1024), tn, 2048


def _local_step(x, target, wts, small, exchange):
    t = x.shape[0]
    assert t % CONV_FWD_TILE == 0 and t % CONV_TILE == 0 and t % GMLP_TILE == 0 and t % (2 * ROW_TILE) == 0, t
    w_main_t, w_dt_t = wts["w_main_t"], wts["w_dt_t"]
    bsp_t = small["b_spatial"].T
    pad32 = lambda a: jnp.pad(a, ((0, 0), (0, DT_PAD - N_HEADS)))
    dtb, alog = pad32(small["dt_bias"]), pad32(small["a_log"])
    dskip_full = jnp.repeat(small["d_skip"], HEAD_DIM, axis=1)
    head_of_col = lax.broadcasted_iota(jnp.int32, (DT_PAD, D_INNER), 1) // HEAD_DIM
    e_bf = (head_of_col == lax.broadcasted_iota(jnp.int32, (DT_PAD, D_INNER), 0)).astype(BF16)

    def mm(a, b, mode, name, **kw):
        if mode == "nn":
            m, k, n = a.shape[0], a.shape[1], b.shape[1]
        elif mode == "nt":
            m, k, n = a.shape[0], a.shape[1], b.shape[0]
        else:
            m, k, n = a.shape[1], a.shape[0], b.shape[1]
        tm, tn, tk = _mm_tiles(mode, m, n, k)
        tm = min(tm, kw.pop("max_tm", tm))
        kw.setdefault("out_dtypes", (BF16,) if mode == "tn" else (F32,))
        if "extra_specs" in kw:
            kw["extra_specs"] = kw["extra_specs"](tm, tn)
        return _matmul(a, b, mode=mode, tm=tm, tn=tn, tk=tk, name=name, **kw)

    def out_tile(tm, tn):
        return (((tm, tn), lambda i, j: (i, j)),)

    def row_tiles(n_tiles, *vectors, gate_logits=False):
        def specs(tm, tn):
            out = [((tm, tn), lambda i, j: (i, j))] * n_tiles
            if gate_logits:
                out += [((tm, D_MODEL), lambda i, j, cb=COL_GATE // D_MODEL + half: (i, cb)) for half in range(2)]
            return tuple(out) + tuple(((1, w), lambda i, j, cb=cb: (0, cb)) for w, cb in vectors)
        return specs

    vec = lambda w: ((1, w), F32, (1, w), lambda i, j: (0, 0))
    fused_tm = 512

    h, dt_raw = _rms_fwd(x, small["norm_mix_g"], w_dt_t, "rms_mix", deps=exchange.begin())
    proj = mm(h, w_main_t, "nt", "proj_main", out_dtypes=(BF16,))
    y_a = _gmlp_fwd(proj, small["v_norm_g"], small["v_norm_b"], small["w_spatial"], bsp_t, "gmlp_fwd")
    pre_conv, xc = _conv_fwd(proj, wts["conv_w"], small["conv_b"], "conv_fwd")
    y_ssd, y_b, sprev = _ssd_fwd(xc, proj, dt_raw, dtb, alog, dskip_full, small["ssm_norm_g"], e_bf, "ssd_fwd")
    wts = {**wts, **exchange.late_weights(y_b)}
    pa = mm(y_a, wts["w_proj_a"], "nn", "proj_a", out_dtypes=(BF16,))
    pb, merged = mm(y_b, wts["w_proj_b"], "nn", "proj_b", epilogue=_merge_epilogue, out_dtypes=(BF16, BF16), max_tm=fused_tm,
                    extras=(pa, proj, proj, small["b_gates"], small["b_gates"]),
                    extra_specs=row_tiles(1, (D_MODEL, 0), (D_MODEL, 1), gate_logits=True))
    x1, h2 = mm(merged, wts["w_out"], "nn", "out_proj", epilogue=_residual_rms_epilogue, out_dtypes=(F32, BF16),
                max_tm=2 * fused_tm, extras=(x, small["norm_mlp_g"]), extra_specs=row_tiles(1, (D_MODEL, 0)))

    def relu_sq(acc, ex, outs, first):
        r = jnp.maximum(acc, 0.0)
        outs[0][...] = (r * r).astype(BF16)

    act = mm(h2, wts["w_mlp_up"], "nn", "mlp_up", epilogue=relu_sq, out_dtypes=(BF16,))
    dx2, dx2_b, g_final, _, loss = mm(
        act, wts["w_mlp_down"], "nn", "mlp_down", epilogue=_loss_epilogue, carry=True,
        out_dtypes=(F32, BF16, vec(D_MODEL), vec(D_MODEL), vec(LANES)),
        extras=(x1, small["norm_final_g"], target), extra_specs=lambda tm, tn: (
            ((tm, tn), lambda i, j: (i, j)), ((1, tn), lambda i, j: (0, 0)), ((tm, tn), lambda i, j: (i, j))))

    def relu_sq_bwd(acc, ex, outs, first):
        outs[0][...] = (acc * 2.0 * jnp.sqrt(ex[0][...].astype(F32))).astype(BF16)

    dup = mm(dx2_b, wts["w_mlp_down"], "nt", "d_act", epilogue=relu_sq_bwd, extras=(act,), extra_specs=out_tile,
             out_dtypes=(BF16,))
    g_down = mm(act, dx2_b, "tn", "g_mlp_down")
    g_up = mm(h2, dup, "tn", "g_mlp_up")
    dx1, dx1_b, g_mlp = mm(
        dup, wts["w_mlp_up"], "nt", "d_h2", epilogue=_rms_bwd_epilogue, carry=True,
        out_dtypes=(F32, BF16, vec(D_MODEL)), extras=(x1, small["norm_mlp_g"], dx2), extra_specs=lambda tm, tn: (
            ((tm, tn), lambda i, j: (i, j)), ((1, tn), lambda i, j: (0, 0)), ((tm, tn), lambda i, j: (i, j))))

    g_out = mm(merged, dx1_b, "tn", "g_out")
    dpa, dpb, dproj, g_bgates = mm(
        dx1_b, wts["w_out"], "nt", "d_merged", epilogue=_merge_bwd_epilogue, carry=True, max_tm=fused_tm,
        out_dtypes=(BF16, BF16, ((t, MAIN_W), BF16, (fused_tm, 2 * D_MODEL), lambda i, j: (i, COL_GATE // (2 * D_MODEL))),
                    vec(2 * D_MODEL)),
        extras=(pa, pb, proj, proj, small["b_gates"], small["b_gates"]),
        extra_specs=row_tiles(2, (D_MODEL, 0), (D_MODEL, 1), gate_logits=True))
    g_pa = mm(y_a, dpa, "tn", "g_proj_a")
    g_pb = mm(y_b, dpb, "tn", "g_proj_b")
    started = exchange.reduce("late", {"w_mlp_down": g_down, "w_mlp_up": g_up, "w_out": g_out, "w_proj_a": g_pa,
                                       "w_proj_b": g_pb})
    dya = mm(dpa, wts["w_proj_a"], "nt", "d_ya", deps=started, out_dtypes=(BF16,))
    dyb = mm(dpb, wts["w_proj_b"], "nt", "d_yb", out_dtypes=(BF16,))

    dproj, g_wsp, g_bsp_t, g_vg, g_vb = _gmlp_bwd(proj, dya, small["v_norm_g"], small["v_norm_b"], small["w_spatial"],
                                                   bsp_t, dproj, "gmlp_bwd")
    dproj, dxc, ddt, g_ng, g_dskip, g_alog, g_dtb, g_dt_t = _ssd_bwd(dyb, y_ssd, xc, proj, dt_raw, sprev, dtb, alog, dskip_full,
                                                                     small["ssm_norm_g"], e_bf, h, dproj, "ssd_bwd")
    dproj, g_convw, g_convb = _conv_bwd(proj, pre_conv, dxc, wts["conv_w"], dproj, "conv_bwd")

    small_grads = {
        "conv_w": g_convw, "loss": loss,
        "conv_b": g_convb, "dt_bias": g_dtb, "a_log": g_alog, "d_skip": g_dskip, "ssm_norm_g": g_ng,
        "v_norm_g": g_vg, "v_norm_b": g_vb, "w_spatial": g_wsp.reshape(GROUPS * CHUNK, CHUNK), "b_spatial": g_bsp_t.T,
        "b_gates": g_bgates, "norm_mlp_g": g_mlp, "norm_final_g": g_final,
    }
    g_main_t = mm(dproj, h, "tn", "g_in_main", deps=exchange.small(small_grads))
    started = exchange.reduce("in", {"w_in": (g_main_t, g_dt_t.astype(BF16))})

    def input_grad(acc, ex, outs, first):
        x_ref, g_ref, res_ref, ddt_ref, wdt_ref = ex
        gg = jnp.zeros((1, D_MODEL), F32)
        for r in range(acc.shape[0] // ROW_TILE):
            rows = slice(r * ROW_TILE, (r + 1) * ROW_TILE)
            dh = acc[rows] + _dot(ddt_ref[rows, :], wdt_ref[...], _NN)
            dx, gg_r = _rms_pullback(x_ref[rows, :], g_ref[...], dh)
            outs[0][rows, :] = dx + res_ref[rows, :]
            gg = gg + gg_r

        _zero_when(first, outs[1])
        outs[1][...] += gg

    grad_x, g_mix = mm(
        dproj, w_main_t, "nn", "d_h", epilogue=input_grad, deps=started, carry=True,
        out_dtypes=(F32, vec(D_MODEL)), extras=(x, small["norm_mix_g"], dx1, ddt, w_dt_t), extra_specs=lambda tm, tn: (
            ((tm, tn), lambda i, j: (i, j)), ((1, tn), lambda i, j: (0, 0)), ((tm, tn), lambda i, j: (i, j)),
            ((tm, DT_PAD), lambda i, j: (i, 0)), ((DT_PAD, D_MODEL), lambda i, j: (0, 0))))

    return grad_x, g_mix


SHARD_ROWS = (MAIN_W + N_HEADS) // N_DEV
REGROUP_IN = 2048


def _main_rows_of(gathered, name):
    n_dev, shard, d = gathered.shape
    blk = 1024
    nb = MAIN_W // blk

    def first_feature(b):
        return b * blk + (N_HEADS if b * blk >= COL_GATE else 0)

    def body(a_ref, b_ref, out_ref):
        for b in range(nb):
            s0, r0 = divmod(first_feature(b), shard)
            n1 = min(shard - r0, blk)

            @pl.when(pl.program_id(0) == b)
            def _(r0=r0, n1=n1):
                out_ref[0:n1, :] = a_ref[0, r0:r0 + n1, :]
                if n1 < blk:
                    out_ref[n1:blk, :] = b_ref[0, 0:blk - n1, :]

    def slot(b):
        return (b * blk + jnp.where(b * blk >= COL_GATE, N_HEADS, 0)) // shard

    return pl.pallas_call(
        body, name=name, grid=(nb,),
        in_specs=[pl.BlockSpec((1, shard, d), lambda b: (slot(b), 0, 0)),
                  pl.BlockSpec((1, shard, d), lambda b: (jnp.minimum(slot(b) + 1, n_dev - 1), 0, 0))],
        out_specs=pl.BlockSpec((blk, d), lambda b: (b, 0)),
        out_shape=jax.ShapeDtypeStruct((MAIN_W, d), gathered.dtype),
        compiler_params=_params(("parallel",), 3 * _nbytes((shard, d), gathered.dtype)),
    )(gathered, gathered)


def _by_device_rows(g_main_t, g_dt_t, name):
    d = g_main_t.shape[1]
    n_blocks = MAIN_W // REGROUP_IN
    dt_dev, dt_row = divmod(COL_GATE, SHARD_ROWS)

    def main_start(s):
        return s * SHARD_ROWS - (N_HEADS if s > dt_dev else 0)

    def body(a_ref, b_ref, dt_ref, out_ref):
        for s in range(N_DEV):
            m0 = main_start(s)
            k0, off = divmod(m0, REGROUP_IN)
            pieces = []
            if s == dt_dev:
                pieces = [(0, dt_row, m0), (dt_row, N_HEADS, None), (dt_row + N_HEADS, SHARD_ROWS - dt_row - N_HEADS, m0 + dt_row)]
            else:
                pieces = [(0, SHARD_ROWS, m0)]

            @pl.when(pl.program_id(0) == s)
            def _(pieces=pieces, k0=k0):
                for dst, n, src in pieces:
                    if src is None:
                        out_ref[0, dst:dst + n, :] = dt_ref[0:n, :]
                        continue
                    lo = src - k0 * REGROUP_IN
                    n_a = max(0, min(n, REGROUP_IN - lo))
                    if n_a:
                        out_ref[0, dst:dst + n_a, :] = a_ref[lo:lo + n_a, :]
                    if n_a < n:
                        lo_b = max(lo - REGROUP_IN, 0)
                        out_ref[0, dst + n_a:dst + n, :] = b_ref[lo_b:lo_b + n - n_a, :]

    def first_block(s):
        return (s * SHARD_ROWS - jnp.where(s > dt_dev, N_HEADS, 0)) // REGROUP_IN

    return pl.pallas_call(
        body, name=name, grid=(N_DEV,),
        in_specs=[pl.BlockSpec((REGROUP_IN, d), lambda s: (first_block(s), 0)),
                  pl.BlockSpec((REGROUP_IN, d), lambda s: (jnp.minimum(first_block(s) + 1, n_blocks - 1), 0)),
                  pl.BlockSpec((DT_PAD, d), lambda s: (0, 0))],
        out_specs=pl.BlockSpec((1, SHARD_ROWS, d), lambda s: (s, 0, 0)),
        out_shape=jax.ShapeDtypeStruct((N_DEV, SHARD_ROWS, d), g_main_t.dtype),
        compiler_params=_params(("parallel",), 3 * _nbytes((REGROUP_IN, d), g_main_t.dtype)),
    )(g_main_t, g_main_t, g_dt_t)


_LATE = ["w_proj_a", "w_proj_b", "w_out", "w_mlp_up", "w_mlp_down"]
_BY_COLS = ("w_mlp_up",)


class _Exchange:
    def __init__(self, late_shards, late_lands):
        self.late_shards, self.late_lands = late_shards, late_lands
        self.c_idx = lax.axis_index("c").astype(jnp.int32).reshape(1)
        self.chip_idx = (2 * lax.axis_index("x") + lax.axis_index("y")).astype(jnp.int32).reshape(1)
        self.pending = []

    def begin(self):
        self.late = _split_start(self.late_shards, self.late_lands, _gather_copies, N_DEV - 1, "gather_late_start")
        return [self.late[-1]]

    def late_weights(self, after):
        _, lands = _split_wait(self.late, _gather_copies, after, "gather_late_wait")
        whole = {}
        for n, g in zip(_LATE, lands):
            whole[n] = jnp.transpose(g, (1, 0, 2)).reshape(g.shape[1], -1) if n in _BY_COLS else g.reshape(-1, g.shape[2])
        return whole

    def reduce(self, tag, grads):
        names = list(grads)
        by_dev = []
        for n in names:
            g = grads[n]
            if n == "w_in":
                by_dev.append(_by_device_rows(*g, "regroup_g_in"))
            elif n in _BY_COLS:
                by_dev.append(jnp.transpose(g.reshape(g.shape[0], N_DEV, -1), (1, 0, 2)))
            else:
                by_dev.append(g.reshape(N_DEV, -1, g.shape[1]))
        from_sibling = _swap_with_sibling(by_dev, "reduce_cores_" + tag)
        parts = [_add_sibling(g, r, self.c_idx, "add_cores_" + n) for n, g, r in zip(names, by_dev, from_sibling)]
        lands = [lax.empty(p.shape, p.dtype) for p in parts]
        started = _split_start(parts, lands, _scatter_copies, 3, "reduce_chips_start_" + tag)
        self.pending.append((tag, names, started))
        return [started[-1]]

    def small(self, grads):
        dev = 2 * self.chip_idx + self.c_idx
        packed, land = _pack_small(grads, dev, "pack_small")
        self.small_started = _split_start([packed], [land], _gather_copies, N_DEV - 1, "exchange_small_start")
        return [self.small_started[-1]]

    def finish(self, after):
        _, (all_small,) = _split_wait(self.small_started, _gather_copies, after, "exchange_small_wait")
        done = {}
        for tag, names, started in self.pending:
            parts, lands = _split_wait(started, _scatter_copies, after, "reduce_chips_wait_" + tag)
            for n, land, part in zip(names, lands, parts):
                done[n] = (land, part, self.chip_idx)
        return all_small, done


def kernel(x, norm_mix_g, w_in, conv_w, conv_b, dt_bias, a_log, d_skip, ssm_norm_g, v_norm_g, v_norm_b, w_spatial, b_spatial, b_gates, w_proj_a, w_proj_b, w_out, norm_mlp_g, w_mlp_up, w_mlp_down, norm_final_g, loss_target, m_norm_mix_g, m_w_in, m_conv_w, m_conv_b, m_dt_bias, m_a_log, m_d_skip, m_ssm_norm_g, m_v_norm_g, m_v_norm_b, m_w_spatial, m_b_spatial, m_b_gates, m_w_proj_a, m_w_proj_b, m_w_out, m_norm_mlp_g, m_w_mlp_up, m_w_mlp_down, m_norm_final_g, v_norm_mix_g, v_w_in, v_conv_w, v_conv_b, v_dt_bias, v_a_log, v_d_skip, v_ssm_norm_g, v_v_norm_g, v_v_norm_b, v_w_spatial, v_b_spatial, v_b_gates, v_w_proj_a, v_w_proj_b, v_w_out, v_norm_mlp_g, v_w_mlp_up, v_w_mlp_down, v_norm_final_g):
    given = dict(locals())
    names = ["norm_mix_g", "w_in", "conv_w", "conv_b", "dt_bias", "a_log", "d_skip", "ssm_norm_g", "v_norm_g", "v_norm_b",
             "w_spatial", "b_spatial", "b_gates", "w_proj_a", "w_proj_b", "w_out", "norm_mlp_g", "w_mlp_up", "w_mlp_down",
             "norm_final_g"]
    shapes = {n: given[n].shape for n in names}
    dev = 4 * lax.axis_index("x") + 2 * lax.axis_index("y") + lax.axis_index("c")

    shard2d = {"w_in": w_in[0].T, "w_proj_a": w_proj_a[0], "w_proj_b": w_proj_b[0], "w_out": w_out[0],
               "w_mlp_up": w_mlp_up[0], "w_mlp_down": w_mlp_down[0]}
    conv_shard = conv_w.reshape(CONV_WIDTH, -1)
    late_shards = [shard2d[n].astype(BF16) for n in _LATE]
    w_in_all, conv_all, *late_lands = _all_gather([shard2d["w_in"].astype(BF16), conv_shard], "gather_first",
                                                  own_only=late_shards)
    dt_dev, dt_row = divmod(COL_GATE, SHARD_ROWS)
    w_dt_t = jnp.pad(w_in_all[dt_dev, dt_row:dt_row + N_HEADS], ((0, DT_PAD - N_HEADS), (0, 0)))
    wts = {"w_main_t": _main_rows_of(w_in_all, "regroup_w_in"), "w_dt_t": w_dt_t, "conv_w": jnp.transpose(conv_all, (1, 0, 2)).reshape(CONV_WIDTH, -1)}
    small = {"norm_mix_g": norm_mix_g, "conv_b": conv_b, "dt_bias": dt_bias, "a_log": a_log, "d_skip": d_skip,
             "ssm_norm_g": ssm_norm_g, "v_norm_g": v_norm_g, "v_norm_b": v_norm_b, "w_spatial": w_spatial[0],
             "b_spatial": b_spatial[0], "b_gates": b_gates, "norm_mlp_g": norm_mlp_g,
             "norm_final_g": norm_final_g.reshape(1, -1)}

    exchange = _Exchange(late_shards, late_lands)
    grad_x, g_mix = _local_step(x[0], loss_target[0], wts, small, exchange)

    out = {}
    all_small, large = exchange.finish(grad_x)
    for n, (slots, own, own_slot) in large.items():
        moments = [given["m_" + n][0], given["v_" + n][0]]
        if n == "w_in":
            moments = [mom.T for mom in moments]
        res = _adamw(slots, shard2d[n], *moments, "adamw_" + n, own=own, own_slot=own_slot)
        out[n] = [(r.T if n == "w_in" else r).reshape(shapes[n]) for r in res]

    last_small = _exchange_small({"norm_mix_g": g_mix}, _LAST_SMALL, "exchange_last")
    small["w_spatial"] = small["w_spatial"].reshape(GROUPS * CHUNK, CHUNK)
    params = {n: (w2d, given["m_" + n].reshape(w2d.shape), given["v_" + n].reshape(w2d.shape)) for n, w2d in small.items()}
    updated, (g_conv_full, loss_all) = _adamw_small(all_small, last_small, params, [(CONV_WIDTH, CONV_DIM), (1, LANES)],
                                                    "adamw_small")
    for n, res in updated.items():
        out[n] = [r.reshape(shapes[n]) for r in res]
    width = shapes["conv_w"][-1]
    g_conv = lax.dynamic_slice(g_conv_full, (0, dev * width), (CONV_WIDTH, width))
    res = _adamw(g_conv[None], conv_shard, m_conv_w.reshape(CONV_WIDTH, -1), v_conv_w.reshape(CONV_WIDTH, -1), "adamw_conv_w")
    out["conv_w"] = [r.reshape(shapes["conv_w"]) for r in res]

    loss = loss_all[0, 0]
    return (loss, grad_x[None], *[out[n][0] for n in names], *[out[n][1] for n in names],
            *[out[n][2] for n in names], *[out[n][3] for n in names])
```

```python
import functools
import math

import jax
import jax.numpy as jnp
from jax import lax
from jax.experimental import pallas as pl
from jax.experimental.pallas import tpu as pltpu

F32 = jnp.float32
BF16 = jnp.bfloat16
MESH = pl.DeviceIdType.MESH

D_MODEL = 1024
NORM_EPS = 1e-6
CHUNK = 128
GROUPS = 8
D_INNER = 2048
HEAD_DIM = 64
N_HEADS = 32
D_STATE = 128
CONV_WIDTH = 4
CONV_DIM = 4096
D_FF = 4096
GROUP_W = D_INNER // GROUPS
N_DEV = 8
N_CHIP = 4

ADAM_LR = 0.001
ADAM_B1 = 0.9
ADAM_B2 = 0.999
ADAM_EPS = 1e-08
ADAM_WD = 0.01
ADAM_STEP = 10

MAIN_W = 2 * D_MODEL + D_INNER + CONV_DIM + 2 * D_MODEL
COL_Z = 2048
COL_XBC = 4096
COL_GATE = 8192
DT_PAD = 128

LANES = 128
SUBLANES = 8
VMEM_BYTES_V7X = 64 * 1024 * 1024
VMEM_BODY_TEMP = 24 * 1024 * 1024


def _vmem_limit(block_bytes):
    return int(min(2 * block_bytes + VMEM_BODY_TEMP, VMEM_BYTES_V7X - 8 * 1024 * 1024))


def _nbytes(shape, dtype):
    return math.prod(shape) * jnp.dtype(dtype).itemsize


_HBM = pl.BlockSpec(memory_space=pl.ANY)


def _params(sem, block_bytes):
    return pltpu.CompilerParams(dimension_semantics=sem, vmem_limit_bytes=_vmem_limit(block_bytes))


def _sigmoid(x):
    return 1.0 / (1.0 + jnp.exp(-x))


def _softplus(x):
    e = jnp.exp(-jnp.abs(x))
    u = 1.0 + e
    log1p_e = jnp.where(u == 1.0, e, jnp.log(u) * (e / jnp.where(u == 1.0, 1.0, u - 1.0)))
    return jnp.maximum(x, 0.0) + log1p_e


_SQRT_HALF = 0.7071067811865476
_INV_SQRT_2PI = 0.3989422804014327


def _normal_cdf(x):
    return 0.5 * (1.0 + lax.erf(x * _SQRT_HALF))


def _gelu_grad(x, cdf):
    return cdf + x * jnp.exp(-0.5 * x * x) * _INV_SQRT_2PI


def _dot(a, b, dims):
    return lax.dot_general(a, b, (dims, ((), ())), preferred_element_type=F32)


_NN = ((1,), (0,))
_NT = ((1,), (1,))
_TN = ((0,), (0,))


def _split3(x):
    hi = x.astype(BF16)
    r1 = x - hi.astype(F32)
    mid = r1.astype(BF16)
    lo = (r1 - mid.astype(F32)).astype(BF16)
    return hi, mid, lo


def _dot_exact_rhs(x, e, dims):
    hi, mid, lo = _split3(x)
    return _dot(hi, e, dims) + _dot(mid, e, dims) + _dot(lo, e, dims)


def _dot_exact_lhs(e, x, dims):
    hi, mid, lo = _split3(x)
    return _dot(e, hi, dims) + _dot(e, mid, dims) + _dot(e, lo, dims)


def _tri(lower):
    r = lax.broadcasted_iota(jnp.int32, (CHUNK, CHUNK), 0)
    c = lax.broadcasted_iota(jnp.int32, (CHUNK, CHUNK), 1)
    return (r >= c) if lower else (r <= c)


def _matmul(a, b, *, mode, tm, tn, tk, out_dtypes, name, epilogue=None, extras=(), extra_specs=(), deps=(),
            carry=False):
    if mode == "nn":
        (m, k), (_, n) = a.shape, b.shape
    elif mode == "nt":
        (m, k), (n, _) = a.shape, b.shape
    else:
        (k, m), (_, n) = a.shape, b.shape
    assert m % tm == 0 and n % tn == 0 and k % tk == 0, (name, m, n, k, tm, tn, tk)
    nk = k // tk
    n_extra, n_out = len(extras), len(out_dtypes)
    first_out = 2 + n_extra + len(deps)
    dims = {"nn": _NN, "nt": _NT, "tn": _TN}[mode]
    if epilogue is None:
        def epilogue(acc, ex, outs, first):
            outs[0][...] = acc.astype(outs[0].dtype)

    def body(*refs):
        a_ref, b_ref = refs[0], refs[1]
        ex_refs = refs[2:2 + n_extra]
        outs = refs[first_out:first_out + n_out]
        first_tile = pl.program_id(0) == 0
        p = _dot(a_ref[...], b_ref[...], dims)
        if nk == 1:
            epilogue(p, ex_refs, outs, first_tile)
            return
        acc_ref = refs[first_out + n_out]
        kk = pl.program_id(2)

        @pl.when(kk == 0)
        def _():
            acc_ref[...] = p

        @pl.when(kk > 0)
        def _():
            acc_ref[...] += p

        @pl.when(kk == nk - 1)
        def _():
            epilogue(acc_ref[...], ex_refs, outs, first_tile)

    grid = (m // tm, n // tn, nk)

    if mode == "nn":
        a_spec = pl.BlockSpec((tm, tk), (lambda i, j, kk: (i, kk)))
        b_spec = pl.BlockSpec((tk, tn), (lambda i, j, kk: (kk, j)))
        a_blk, b_blk = (tm, tk), (tk, tn)
    elif mode == "nt":
        a_spec = pl.BlockSpec((tm, tk), (lambda i, j, kk: (i, kk)))
        b_spec = pl.BlockSpec((tn, tk), (lambda i, j, kk: (j, kk)))
        a_blk, b_blk = (tm, tk), (tn, tk)
    else:
        a_spec = pl.BlockSpec((tk, tm), (lambda i, j, kk: (kk, i)))
        b_spec = pl.BlockSpec((tk, tn), (lambda i, j, kk: (kk, j)))
        a_blk, b_blk = (tk, tm), (tk, tn)
    ex_specs = [pl.BlockSpec(shape, (lambda i, j, kk, f=f: f(i, j))) for shape, f in extra_specs]
    outs = [o if isinstance(o, tuple) else ((m, n), o, (tm, tn), lambda i, j: (i, j)) for o in out_dtypes]
    out_spec = [pl.BlockSpec(blk_shape, (lambda i, j, kk, f=f: f(i, j))) for _, _, blk_shape, f in outs]
    out_shape = [jax.ShapeDtypeStruct(shape, dt) for shape, dt, _, _ in outs]
    blk = (_nbytes(a_blk, a.dtype) + _nbytes(b_blk, b.dtype) + sum(_nbytes(s, F32) for s, _ in extra_specs)
           + sum(_nbytes(blk_shape, dt) for _, dt, blk_shape, _ in outs) + _nbytes((tm, tn), F32))
    order = ("arbitrary",) * 3 if carry else ("parallel", "parallel", "arbitrary")
    res = pl.pallas_call(
        body, name=name, grid=grid,
        in_specs=[a_spec, b_spec] + ex_specs + [_HBM] * len(deps), out_specs=out_spec, out_shape=out_shape,
        scratch_shapes=[pltpu.VMEM((tm, tn), F32)] if nk > 1 else [],
        compiler_params=_params(order, blk),
    )(a, b, *extras, *deps)
    return res[0] if n_out == 1 else res


ROW_TILE = 256


def _row_spec(width, col_block=0, tile=ROW_TILE):
    return pl.BlockSpec((tile, width), lambda i, cb=col_block: (i, cb))


def _vec_spec(width, col_block=0):
    return pl.BlockSpec((1, width), lambda i, cb=col_block: (0, cb))


def _rms_fwd(x, g, w_t, name, deps=()):
    t = x.shape[0]
    n_small = w_t.shape[0]
    tile = 2 * ROW_TILE

    def body(x_ref, g_ref, w_ref, *rest):
        h_ref, small_ref = rest[-2:]
        xv = x_ref[...]
        r = lax.rsqrt(jnp.mean(xv * xv, axis=-1, keepdims=True) + NORM_EPS)
        h = (xv * r * g_ref[...]).astype(BF16)
        h_ref[...] = h
        small_ref[...] = _dot(h, w_ref[...], _NT)

    return pl.pallas_call(
        body, name=name, grid=(t // tile,),
        in_specs=[_row_spec(D_MODEL, 0, tile), _vec_spec(D_MODEL), pl.BlockSpec((n_small, D_MODEL), lambda i: (0, 0))]
        + [_HBM] * len(deps),
        out_specs=[_row_spec(D_MODEL, 0, tile), _row_spec(n_small, 0, tile)],
        out_shape=[jax.ShapeDtypeStruct((t, D_MODEL), BF16), jax.ShapeDtypeStruct((t, n_small), F32)],
        compiler_params=_params(("parallel",), 3 * _nbytes((tile, D_MODEL), F32)),
    )(x, g, w_t, *deps)


def _rms_scale(xv):
    r = lax.rsqrt(jnp.mean(xv * xv, axis=-1, keepdims=True) + NORM_EPS)
    return r, xv * r


def _rms_pullback(xv, g, dh):
    r, xh = _rms_scale(xv)
    dyg = dh * g
    return r * (dyg - xh * jnp.mean(dyg * xh, axis=-1, keepdims=True)), jnp.sum(dh * xh, axis=0, keepdims=True)


def _zero_when(first, *refs):
    @pl.when(first)
    def _():
        for ref in refs:
            ref[...] = jnp.zeros_like(ref)


def _residual_rms_epilogue(acc, ex, outs, first):
    x1 = acc + ex[0][...]
    outs[0][...] = x1
    _, xh = _rms_scale(x1)
    outs[1][...] = (xh * ex[1][...]).astype(BF16)


def _loss_epilogue(acc, ex, outs, first):
    dx_ref, dxb_ref, gg_ref, sq_ref, tot_ref = outs
    gv = ex[1][...]
    r, xh = _rms_scale(acc + ex[0][...])
    err = xh * gv - ex[2][...]
    dy = err * (1.0 / D_MODEL)
    dyg = dy * gv
    dx = r * (dyg - xh * jnp.mean(dyg * xh, axis=-1, keepdims=True))
    dx_ref[...] = dx
    dxb_ref[...] = dx.astype(BF16)

    _zero_when(first, gg_ref, sq_ref)
    gg_ref[...] += jnp.sum(dy * xh, axis=0, keepdims=True)
    sq_ref[...] += jnp.sum(err * err, axis=0, keepdims=True)
    tot_ref[...] = jnp.broadcast_to(jnp.sum(sq_ref[...], axis=1, keepdims=True) * (0.5 / D_MODEL), tot_ref.shape)


def _rms_bwd_epilogue(dh, ex, outs, first):
    dx, gg = _rms_pullback(ex[0][...], ex[1][...], dh)
    dx = dx + ex[2][...]
    outs[0][...] = dx
    if len(outs) == 3:
        outs[1][...] = dx.astype(BF16)

    _zero_when(first, outs[-1])
    outs[-1][...] += gg


def _merge_epilogue(acc, ex, outs, first):
    outs[0][...] = acc.astype(outs[0].dtype)
    ga = _sigmoid(ex[1][...].astype(F32) + ex[3][...])
    gb = _sigmoid(ex[2][...].astype(F32) + ex[4][...])
    outs[1][...] = (ga * ex[0][...].astype(F32) + gb * acc).astype(BF16)


def _merge_bwd_epilogue(dm, ex, outs, first):
    dpa_ref, dpb_ref, dgl_ref, gb_ref = outs
    ga = _sigmoid(ex[2][...].astype(F32) + ex[4][...])
    gb = _sigmoid(ex[3][...].astype(F32) + ex[5][...])
    dpa_ref[...] = (dm * ga).astype(BF16)
    dpb_ref[...] = (dm * gb).astype(BF16)
    dla = dm * ex[0][...].astype(F32) * ga * (1.0 - ga)
    dlb = dm * ex[1][...].astype(F32) * gb * (1.0 - gb)
    dgl_ref[:, :D_MODEL] = dla.astype(BF16)
    dgl_ref[:, D_MODEL:] = dlb.astype(BF16)

    _zero_when(first, gb_ref)
    gb_ref[:, :D_MODEL] += jnp.sum(dla, axis=0, keepdims=True)
    gb_ref[:, D_MODEL:] += jnp.sum(dlb, axis=0, keepdims=True)


GMLP_TILE = 512
GMLP_NC = GMLP_TILE // CHUNK


def _gmlp_common(u_pre, v_pre, vg, vb):
    cdf_u, cdf_v = _normal_cdf(u_pre), _normal_cdf(v_pre)
    u = u_pre * cdf_u
    v = v_pre * cdf_v
    mu = jnp.mean(v, axis=-1, keepdims=True)
    vc = v - mu
    rstd = lax.rsqrt(jnp.mean(vc * vc, axis=-1, keepdims=True) + NORM_EPS)
    vh = vc * rstd
    vn = vh * vg + vb
    return u, vh, vn, rstd, cdf_u, cdf_v


def _chunks_to_lanes(x, g):
    return jnp.concatenate([x[c * CHUNK:(c + 1) * CHUNK, g * CHUNK:(g + 1) * CHUNK] for c in range(GMLP_NC)], axis=1)


def _gmlp_fwd(proj, vg, vb, wsp, bsp_t, name):
    t = proj.shape[0]

    def body(u_ref, v_ref, vg_ref, vb_ref, w_ref, b_ref, ya_ref):
        u, _, vn, _, _, _ = _gmlp_common(u_ref[...].astype(F32), v_ref[...].astype(F32), vg_ref[...], vb_ref[...])
        mask = _tri(True)
        bt = b_ref[...]
        for g in range(GROUPS):
            w = jnp.where(mask, w_ref[g], 0.0).astype(BF16)
            vcat = _chunks_to_lanes(vn, g).astype(BF16)
            s = _dot(w, vcat, _NN) + bt[:, g:g + 1]
            for c in range(GMLP_NC):
                rows, cols = slice(c * CHUNK, (c + 1) * CHUNK), slice(g * CHUNK, (g + 1) * CHUNK)
                ya_ref[rows, cols] = (u[rows, cols] * s[:, c * CHUNK:(c + 1) * CHUNK]).astype(BF16)

    return pl.pallas_call(
        body, name=name, grid=(t // GMLP_TILE,),
        in_specs=[_row_spec(D_MODEL, 0, GMLP_TILE), _row_spec(D_MODEL, 1, GMLP_TILE), _vec_spec(D_MODEL),
                  _vec_spec(D_MODEL), pl.BlockSpec((GROUPS, CHUNK, CHUNK), lambda i: (0, 0, 0)),
                  pl.BlockSpec((CHUNK, GROUPS), lambda i: (0, 0))],
        out_specs=_row_spec(D_MODEL, 0, GMLP_TILE),
        out_shape=jax.ShapeDtypeStruct((t, D_MODEL), BF16),
        compiler_params=_params(("parallel",), 3 * _nbytes((GMLP_TILE, D_MODEL), F32)),
    )(proj, proj, vg, vb, wsp, bsp_t)


def _gmlp_bwd(proj, dya, vg, vb, wsp, bsp_t, dproj, name):
    t = proj.shape[0]

    def body(u_ref, v_ref, dya_ref, vg_ref, vb_ref, w_ref, b_ref, dproj_in, duv_ref, gw_ref, gbt_ref, gvg_ref, gvb_ref,
             dvn_scr, du_scr):
        del dproj_in
        u_pre, v_pre = u_ref[...].astype(F32), v_ref[...].astype(F32)
        vgv = vg_ref[...]
        u, vh, vn, rstd, cdf_u, cdf_v = _gmlp_common(u_pre, v_pre, vgv, vb_ref[...])
        dya = dya_ref[...].astype(F32)
        mask = _tri(True)
        bt = b_ref[...]
        first = pl.program_id(0) == 0

        @pl.when(first)
        def _():
            gw_ref[...] = jnp.zeros_like(gw_ref)
            gbt_ref[...] = jnp.zeros_like(gbt_ref)
            gvg_ref[...] = jnp.zeros_like(gvg_ref)
            gvb_ref[...] = jnp.zeros_like(gvb_ref)

        lane = lax.broadcasted_iota(jnp.int32, (CHUNK, GROUPS), 1)
        gbt = jnp.zeros((CHUNK, GROUPS), F32)
        for g in range(GROUPS):
            w = jnp.where(mask, w_ref[g], 0.0).astype(BF16)
            vcat = _chunks_to_lanes(vn, g).astype(BF16)
            s = _dot(w, vcat, _NN) + bt[:, g:g + 1]
            ds = _chunks_to_lanes(dya * u, g)
            gbt = jnp.where(lane == g, jnp.sum(ds, axis=1, keepdims=True), gbt)
            dsb = ds.astype(BF16)
            gw_ref[g] += jnp.where(mask, _dot(dsb, vcat, _NT), 0.0)
            dv = _dot(w, dsb, _TN)
            for c in range(GMLP_NC):
                rows, cols = slice(c * CHUNK, (c + 1) * CHUNK), slice(g * CHUNK, (g + 1) * CHUNK)
                dvn_scr[rows, cols] = dv[:, c * CHUNK:(c + 1) * CHUNK]
                du_scr[rows, cols] = dya[rows, cols] * s[:, c * CHUNK:(c + 1) * CHUNK]
        gbt_ref[...] += gbt
        dvn = dvn_scr[...]
        gvg_ref[...] += jnp.sum(dvn * vh, axis=0, keepdims=True)
        gvb_ref[...] += jnp.sum(dvn, axis=0, keepdims=True)
        dvh = dvn * vgv
        dv = rstd * (dvh - jnp.mean(dvh, axis=-1, keepdims=True) - vh * jnp.mean(dvh * vh, axis=-1, keepdims=True))
        duv_ref[:, :D_MODEL] = (du_scr[...] * _gelu_grad(u_pre, cdf_u)).astype(BF16)
        duv_ref[:, D_MODEL:] = (dv * _gelu_grad(v_pre, cdf_v)).astype(BF16)

    return pl.pallas_call(
        body, name=name, grid=(t // GMLP_TILE,),
        in_specs=[_row_spec(D_MODEL, 0, GMLP_TILE), _row_spec(D_MODEL, 1, GMLP_TILE), _row_spec(D_MODEL, 0, GMLP_TILE),
                  _vec_spec(D_MODEL), _vec_spec(D_MODEL), pl.BlockSpec((GROUPS, CHUNK, CHUNK), lambda i: (0, 0, 0)),
                  pl.BlockSpec((CHUNK, GROUPS), lambda i: (0, 0)), pl.BlockSpec(memory_space=pl.ANY)],
        out_specs=[_row_spec(2 * D_MODEL, 0, GMLP_TILE), pl.BlockSpec((GROUPS, CHUNK, CHUNK), lambda i: (0, 0, 0)),
                   pl.BlockSpec((CHUNK, GROUPS), lambda i: (0, 0)), _vec_spec(D_MODEL), _vec_spec(D_MODEL)],
        out_shape=[jax.ShapeDtypeStruct(dproj.shape, BF16), jax.ShapeDtypeStruct((GROUPS, CHUNK, CHUNK), F32),
                   jax.ShapeDtypeStruct((CHUNK, GROUPS), F32), jax.ShapeDtypeStruct((1, D_MODEL), F32),
                   jax.ShapeDtypeStruct((1, D_MODEL), F32)],
        scratch_shapes=[pltpu.VMEM((GMLP_TILE, D_MODEL), F32), pltpu.VMEM((GMLP_TILE, D_MODEL), F32)],
        input_output_aliases={7: 0},
        compiler_params=_params(("arbitrary",), 6 * _nbytes((GMLP_TILE, D_MODEL), F32)),
    )(proj, proj, dya, vg, vb, wsp, bsp_t, dproj)


CONV_TILE = 1024
CONV_FWD_TILE = 2048
CONV_COLS = 1024
CONV_RB = 32
HALO = SUBLANES


def _conv_fwd(proj, cw, cb, name):
    t = proj.shape[0]
    nj = CONV_DIM // CONV_COLS
    xcb = COL_XBC // CONV_COLS
    before = 2 * HALO
    rb = CONV_FWD_TILE // before

    def body(x_ref, prev_ref, cw_ref, cb_ref, pre_ref, xc_ref):
        i = pl.program_id(1)
        cw_v = cw_ref[...]
        cb_v = cb_ref[...]
        for b in range(CONV_FWD_TILE // CONV_RB):
            if b == 0:
                prev = jnp.where(i > 0, prev_ref[...].astype(F32)[HALO:, :], 0.0)
                ext = jnp.concatenate([prev, x_ref[:CONV_RB, :].astype(F32)], axis=0)
            else:
                ext = x_ref[b * CONV_RB - before:(b + 1) * CONV_RB, :].astype(F32)[HALO:, :]
            pre = cb_v + cw_v[CONV_WIDTH - 1:CONV_WIDTH, :] * ext[HALO:, :]
            for k in range(CONV_WIDTH - 1):
                back = CONV_WIDTH - 1 - k
                pre = pre + cw_v[k:k + 1, :] * pltpu.roll(ext, back, 0)[HALO:, :]
            pre_ref[b * CONV_RB:(b + 1) * CONV_RB, :] = pre
            xc_ref[b * CONV_RB:(b + 1) * CONV_RB, :] = pre * _sigmoid(pre)

    tile = pl.BlockSpec((CONV_FWD_TILE, CONV_COLS), lambda j, i: (i, j))
    return pl.pallas_call(
        body, name=name, grid=(nj, t // CONV_FWD_TILE),
        in_specs=[pl.BlockSpec((CONV_FWD_TILE, CONV_COLS), lambda j, i: (i, xcb + j)),
                  pl.BlockSpec((before, CONV_COLS), lambda j, i: (jnp.maximum(i * rb - 1, 0), xcb + j)),
                  pl.BlockSpec((CONV_WIDTH, CONV_COLS), lambda j, i: (0, j)),
                  pl.BlockSpec((1, CONV_COLS), lambda j, i: (0, j))],
        out_specs=[tile, tile],
        out_shape=[jax.ShapeDtypeStruct((t, CONV_DIM), F32), jax.ShapeDtypeStruct((t, CONV_DIM), F32)],
        compiler_params=_params(("parallel", "parallel"), 4 * _nbytes((CONV_FWD_TILE, CONV_COLS), F32)),
    )(proj, proj, cw, cb)


def _fold_rows(v):
    out = v[:SUBLANES]
    for r in range(1, v.shape[0] // SUBLANES):
        out = out + v[r * SUBLANES:(r + 1) * SUBLANES]
    return out


def _conv_bwd(proj, pre, dxc, cw, dproj, name):
    t = proj.shape[0]
    nj = CONV_DIM // CONV_COLS
    ni = t // CONV_TILE
    xcb = COL_XBC // CONV_COLS
    rb = CONV_TILE // HALO
    last_rb = t // HALO - 1

    def body(x_ref, p_ref, pnext_ref, d_ref, dnext_ref, cw_ref, dproj_in, dx_ref, gw_ref, gb_ref):
        del dproj_in
        i = pl.program_id(1)
        cw_v = cw_ref[...]

        def dpre_of(p, d):
            sg = _sigmoid(p)
            return d * sg * (1.0 + p * (1.0 - sg))

        @pl.when(i == 0)
        def _():
            gw_ref[...] = jnp.zeros_like(gw_ref)
            gb_ref[...] = jnp.zeros_like(gb_ref)

        head = dpre_of(pnext_ref[...], jnp.where(i < ni - 1, dnext_ref[...], 0.0))
        gb_acc = jnp.zeros((SUBLANES, CONV_COLS), F32)
        gw_acc = [jnp.zeros((SUBLANES, CONV_COLS), F32) for _ in range(CONV_WIDTH)]
        for b in reversed(range(CONV_TILE // CONV_RB)):
            rows = slice(b * CONV_RB, (b + 1) * CONV_RB)
            cur = dpre_of(p_ref[rows, :], d_ref[rows, :])
            ext = jnp.concatenate([cur, head], axis=0)
            xv = x_ref[rows, :].astype(F32)
            dx = None
            for k in range(CONV_WIDTH):
                shift = CONV_WIDTH - 1 - k
                win = cur if shift == 0 else pltpu.roll(ext, CONV_RB + HALO - shift, 0)[:CONV_RB, :]
                term = cw_v[k:k + 1, :] * win
                dx = term if dx is None else dx + term
                gw_acc[k] = gw_acc[k] + _fold_rows(win * xv)
            dx_ref[rows, :] = dx.astype(BF16)
            gb_acc = gb_acc + _fold_rows(cur)
            head = cur[:HALO]
        gb_ref[...] += jnp.sum(gb_acc, axis=0, keepdims=True)
        for k in range(CONV_WIDTH):
            gw_ref[k:k + 1, :] += jnp.sum(gw_acc[k], axis=0, keepdims=True)

    tile = pl.BlockSpec((CONV_TILE, CONV_COLS), lambda j, i: (i, j))
    after = pl.BlockSpec((HALO, CONV_COLS), lambda j, i: (jnp.minimum((i + 1) * rb, last_rb), j))
    return pl.pallas_call(
        body, name=name, grid=(nj, ni),
        in_specs=[pl.BlockSpec((CONV_TILE, CONV_COLS), lambda j, i: (i, xcb + j)), tile, after, tile, after,
                  pl.BlockSpec((CONV_WIDTH, CONV_COLS), lambda j, i: (0, j)),
                  pl.BlockSpec(memory_space=pl.ANY)],
        out_specs=[pl.BlockSpec((CONV_TILE, CONV_COLS), lambda j, i: (i, xcb + j)),
                   pl.BlockSpec((CONV_WIDTH, CONV_COLS), lambda j, i: (0, j)),
                   pl.BlockSpec((1, CONV_COLS), lambda j, i: (0, j))],
        out_shape=[jax.ShapeDtypeStruct(dproj.shape, BF16), jax.ShapeDtypeStruct((CONV_WIDTH, CONV_DIM), F32),
                   jax.ShapeDtypeStruct((1, CONV_DIM), F32)],
        input_output_aliases={6: 0},
        compiler_params=_params(("parallel", "arbitrary"), 4 * _nbytes((CONV_TILE, CONV_COLS), F32)),
    )(proj, pre, pre, dxc, dxc, cw, dproj)


def _ssd_decays(dt_raw, dtb, alog, e_bf, tril_bf):
    dtv = _softplus(dt_raw + dtb)
    a = -jnp.exp(alog)
    cs = _dot_exact_lhs(tril_bf, dtv * a, _NN)
    cs_last = cs[CHUNK - 1:CHUNK, :]
    stack = jnp.concatenate([dtv, jnp.exp(cs), jnp.exp(cs_last - cs)], axis=0)
    full = _head_expand(stack, e_bf)
    return dtv, a, cs, full[:CHUNK], full[CHUNK:2 * CHUNK], full[2 * CHUNK:]


def _split2(x):
    hi = x.astype(BF16)
    return hi, (x - hi.astype(F32)).astype(BF16)


def _head_expand(x, e_bf):
    hi, mid = _split2(x)
    return _dot(hi, e_bf, _NN) + _dot(mid, e_bf, _NN)


def _head_sums(x, e_bf):
    hi, mid = _split2(x)
    return _dot(hi, e_bf, _NT) + _dot(mid, e_bf, _NT)


def _head_mats(cs, cs_t, cb, h, mask):
    seg = cs[:, h:h + 1] - cs_t[h:h + 1, :]
    lmat = jnp.exp(jnp.where(mask, seg, -jnp.inf))
    return lmat, cb * lmat


def _ssd_fwd(xc, proj, dt_raw, dtb, alog, dskip_full, ng, e_bf, name):
    t = xc.shape[0]
    nc = t // CHUNK
    zcb = COL_Z // D_INNER

    def body(xc_ref, z_ref, dt_ref, dtb_ref, alog_ref, dsk_ref, ng_ref, e_ref, y_ref, yb_ref, sprev_ref, s_scr):
        @pl.when(pl.program_id(0) == 0)
        def _():
            s_scr[...] = jnp.zeros_like(s_scr)

        mask = _tri(True)
        tril_bf = mask.astype(BF16)
        e_v = e_ref[...]
        _, _, cs, dt_full, ecs_full, decay_full = _ssd_decays(dt_ref[...], dtb_ref[...], alog_ref[...], e_v, tril_bf)
        cs_t = cs.T
        sprev_ref[0] = s_scr[...]
        for g in range(GROUPS):
            gc = slice(g * GROUP_W, (g + 1) * GROUP_W)
            xs = xc_ref[:, gc]
            xdt = xs * dt_full[:, gc]
            xdt_b = xdt.astype(BF16)
            xdec = (xdt * decay_full[:, gc]).astype(BF16)
            bg = xc_ref[:, D_INNER + g * D_STATE:D_INNER + (g + 1) * D_STATE].astype(BF16)
            cg = xc_ref[:, D_INNER + GROUPS * D_STATE + g * D_STATE:D_INNER + GROUPS * D_STATE + (g + 1) * D_STATE].astype(BF16)
            cb = _dot(cg, bg, _NT)
            s_prev = s_scr[:, gc]
            y_off = ecs_full[:, gc] * _dot(cg, s_prev.astype(BF16), _NN)
            s_scr[:, gc] = s_prev * ecs_full[CHUNK - 1:CHUNK, gc] + _dot(bg, xdec, _TN)
            parts = []
            for r in range(GROUP_W // HEAD_DIM):
                h = g * (GROUP_W // HEAD_DIM) + r
                _, m = _head_mats(cs, cs_t, cb, h, mask)
                parts.append(_dot(m.astype(BF16), xdt_b[:, r * HEAD_DIM:(r + 1) * HEAD_DIM], _NN))
            yg = jnp.concatenate(parts, axis=1) + y_off + dsk_ref[:, gc] * xs
            y_ref[:, gc] = yg
            zv = z_ref[:, gc].astype(F32)
            ygate = yg * (zv * _sigmoid(zv))
            rstd = lax.rsqrt(jnp.mean(ygate * ygate, axis=-1, keepdims=True) + NORM_EPS)
            yb_ref[:, gc] = (ygate * rstd * ng_ref[:, gc]).astype(BF16)

    vec = lambda w: pl.BlockSpec((1, w), lambda i: (0, 0))
    blk = _nbytes((CHUNK, CONV_DIM), F32) + 3 * _nbytes((CHUNK, D_INNER), F32) + _nbytes((D_STATE, D_INNER), F32)
    return pl.pallas_call(
        body, name=name, grid=(nc,),
        in_specs=[pl.BlockSpec((CHUNK, CONV_DIM), lambda i: (i, 0)), pl.BlockSpec((CHUNK, D_INNER), lambda i: (i, zcb)),
                  pl.BlockSpec((CHUNK, DT_PAD), lambda i: (i, 0)), vec(DT_PAD), vec(DT_PAD), vec(D_INNER), vec(D_INNER),
                  pl.BlockSpec((DT_PAD, D_INNER), lambda i: (0, 0))],
        out_specs=[pl.BlockSpec((CHUNK, D_INNER), lambda i: (i, 0)), pl.BlockSpec((CHUNK, D_INNER), lambda i: (i, 0)),
                   pl.BlockSpec((1, D_STATE, D_INNER), lambda i: (i, 0, 0))],
        out_shape=[jax.ShapeDtypeStruct((t, D_INNER), F32), jax.ShapeDtypeStruct((t, D_INNER), BF16),
                   jax.ShapeDtypeStruct((nc, D_STATE, D_INNER), F32)],
        scratch_shapes=[pltpu.VMEM((D_STATE, D_INNER), F32)],
        compiler_params=_params(("arbitrary",), blk),
    )(xc, proj, dt_raw, dtb, alog, dskip_full, ng, e_bf)


def _ssd_bwd(dyb, y, xc, proj, dt_raw, sprev, dtb, alog, dskip_full, ng, e_bf, h, dproj, name):
    t = xc.shape[0]
    nc = t // CHUNK
    zcb = COL_Z // D_INNER
    hpg = GROUP_W // HEAD_DIM
    rev = lambda i: nc - 1 - i

    def body(dyb_ref, y_ref, xc_ref, z_ref, dt_ref, sprev_ref, dtb_ref, alog_ref, dsk_ref, ng_ref, e_ref, h_ref, dproj_in,
             dz_ref, dxc_ref, ddt_ref, gng_ref, gdsk_ref, galog_ref, gdtb_ref, gwdt_ref, ds_scr, sums_scr):
        del dproj_in

        @pl.when(pl.program_id(0) == 0)
        def _():
            ds_scr[...] = jnp.zeros_like(ds_scr)
            gng_ref[...] = jnp.zeros_like(gng_ref)
            gdsk_ref[...] = jnp.zeros_like(gdsk_ref)
            galog_ref[...] = jnp.zeros_like(galog_ref)
            gdtb_ref[...] = jnp.zeros_like(gdtb_ref)
            gwdt_ref[...] = jnp.zeros_like(gwdt_ref)

        mask = _tri(True)
        tril_bf = mask.astype(BF16)
        triu_bf = _tri(False).astype(BF16)
        e_v = e_ref[...]
        dt_in = dt_ref[...] + dtb_ref[...]
        dtv, a, cs, dt_full, ecs_full, decay_full = _ssd_decays(dt_ref[...], dtb_ref[...], alog_ref[...], e_v, tril_bf)
        cs_t = cs.T

        lane_h = lax.broadcasted_iota(jnp.int32, (CHUNK, DT_PAD), 1)
        sub_h = lax.broadcasted_iota(jnp.int32, (DT_PAD, CHUNK), 0)
        dcs_rows = jnp.zeros((CHUNK, DT_PAD), F32)
        dcs_cols_t = jnp.zeros((DT_PAD, CHUNK), F32)
        last_cols, dsk_cols = [], []
        for g in range(GROUPS):
            gc = slice(g * GROUP_W, (g + 1) * GROUP_W)
            b_cols = slice(D_INNER + g * D_STATE, D_INNER + (g + 1) * D_STATE)
            c_cols = slice(D_INNER + GROUPS * D_STATE + g * D_STATE, D_INNER + GROUPS * D_STATE + (g + 1) * D_STATE)
            xs = xc_ref[:, gc]
            xdt = xs * dt_full[:, gc]
            xdt_b = xdt.astype(BF16)
            xdec = xdt * decay_full[:, gc]
            xdec_b = xdec.astype(BF16)
            zv = z_ref[:, gc].astype(F32)
            sg = _sigmoid(zv)
            gate = zv * sg
            yv = y_ref[:, gc]
            dybv = dyb_ref[:, gc].astype(F32)
            ygate = yv * gate
            rstd = lax.rsqrt(jnp.mean(ygate * ygate, axis=-1, keepdims=True) + NORM_EPS)
            yn = ygate * rstd
            gng_ref[:, gc] += jnp.sum(dybv * yn, axis=0, keepdims=True)
            dyn = dybv * ng_ref[:, gc]
            dyg = rstd * (dyn - yn * jnp.mean(dyn * yn, axis=-1, keepdims=True))
            dz_ref[:, gc] = (dyg * yv * sg * (1.0 + zv * (1.0 - sg))).astype(BF16)
            dy = dyg * gate
            dy_b = dy.astype(BF16)
            dyo = dy * ecs_full[:, gc]
            dyo_b = dyo.astype(BF16)
            dsk_cols.append(jnp.sum(dy * xs, axis=0, keepdims=True))

            bg = xc_ref[:, b_cols].astype(BF16)
            cg = xc_ref[:, c_cols].astype(BF16)
            s_prev = sprev_ref[0, :, gc]
            s_prev_b = s_prev.astype(BF16)
            dsg = ds_scr[:, gc]
            dsg_b = dsg.astype(BF16)
            cb = _dot(cg, bg, _NT)
            c_s = _dot(cg, s_prev_b, _NN)
            b_ds = _dot(bg, dsg_b, _NN)
            dcb = jnp.zeros((CHUNK, CHUNK), F32)
            parts = []
            for r in range(hpg):
                h = g * hpg + r
                hc = slice(r * HEAD_DIM, (r + 1) * HEAD_DIM)
                lmat, m = _head_mats(cs, cs_t, cb, h, mask)
                dm = _dot(dy_b[:, hc], xdt_b[:, hc], _NT)
                parts.append(_dot(m.astype(BF16), dy_b[:, hc], _TN))
                dcb = dcb + dm * lmat
                w = dm * m
                dcs_rows = jnp.where(lane_h == h, jnp.sum(w, axis=1, keepdims=True), dcs_rows)
                dcs_cols_t = jnp.where(sub_h == h, jnp.sum(w, axis=0, keepdims=True), dcs_cols_t)
            dxdt = jnp.concatenate(parts, axis=1) + decay_full[:, gc] * b_ds
            dcb_b = dcb.astype(BF16)
            dxc_ref[:, c_cols] = _dot(dcb_b, bg, _NN) + _dot(dyo_b, s_prev_b, _NT)
            dxc_ref[:, b_cols] = _dot(dcb_b, cg, _TN) + _dot(xdec_b, dsg_b, _NT)
            cdec = ecs_full[CHUNK - 1:CHUNK, gc]
            ds_scr[:, gc] = _dot(cg, dyo_b, _TN) + cdec * dsg
            dxc_ref[:, gc] = dxdt * dt_full[:, gc] + dsk_ref[:, gc] * dy
            dec_prod = xdec * b_ds
            sums_scr[:CHUNK, gc] = dyo * c_s - dec_prod
            sums_scr[CHUNK:, gc] = dxdt * xs
            last_cols.append(jnp.sum(dec_prod, axis=0, keepdims=True) + cdec * jnp.sum(dsg * s_prev, axis=0, keepdims=True))
        t_sums = _head_sums(sums_scr[...], e_v)
        tail = jnp.concatenate([jnp.concatenate(last_cols, axis=1), jnp.concatenate(dsk_cols, axis=1),
                                jnp.zeros((SUBLANES - 2, D_INNER), F32)], axis=0)
        t_tail = _dot_exact_rhs(tail, e_v, _NT)
        gdsk_ref[...] += t_tail[1:2, :]
        row = lax.broadcasted_iota(jnp.int32, (CHUNK, DT_PAD), 0)
        dcs = dcs_rows - dcs_cols_t.T + t_sums[:CHUNK] + jnp.where(row == CHUNK - 1, t_tail[0:1, :], 0.0)
        dda = _dot_exact_lhs(triu_bf, dcs, _NN)
        galog_ref[...] += jnp.sum(dda * dtv, axis=0, keepdims=True) * a
        ddt = dda * a + t_sums[CHUNK:]
        ddt_raw = jnp.where(lane_h < N_HEADS, ddt * _sigmoid(dt_in), 0.0)
        gdtb_ref[...] += jnp.sum(ddt_raw, axis=0, keepdims=True)
        ddt_b = ddt_raw.astype(BF16)
        ddt_ref[...] = ddt_b
        gwdt_ref[...] += _dot(ddt_b, h_ref[...], _TN)

    vec = lambda w: pl.BlockSpec((1, w), lambda i: (0, 0))
    blk = (2 * _nbytes((CHUNK, CONV_DIM), F32) + 4 * _nbytes((CHUNK, D_INNER), F32) + 4 * _nbytes((D_STATE, D_INNER), F32))
    return pl.pallas_call(
        body, name=name, grid=(nc,),
        in_specs=[pl.BlockSpec((CHUNK, D_INNER), lambda i: (rev(i), 0)), pl.BlockSpec((CHUNK, D_INNER), lambda i: (rev(i), 0)),
                  pl.BlockSpec((CHUNK, CONV_DIM), lambda i: (rev(i), 0)), pl.BlockSpec((CHUNK, D_INNER), lambda i: (rev(i), zcb)),
                  pl.BlockSpec((CHUNK, DT_PAD), lambda i: (rev(i), 0)), pl.BlockSpec((1, D_STATE, D_INNER), lambda i: (rev(i), 0, 0)),
                  vec(DT_PAD), vec(DT_PAD), vec(D_INNER), vec(D_INNER), pl.BlockSpec((DT_PAD, D_INNER), lambda i: (0, 0)),
                  pl.BlockSpec((CHUNK, D_MODEL), lambda i: (rev(i), 0)), pl.BlockSpec(memory_space=pl.ANY)],
        out_specs=[pl.BlockSpec((CHUNK, D_INNER), lambda i: (rev(i), zcb)), pl.BlockSpec((CHUNK, CONV_DIM), lambda i: (rev(i), 0)),
                   pl.BlockSpec((CHUNK, DT_PAD), lambda i: (rev(i), 0)), vec(D_INNER), vec(DT_PAD), vec(DT_PAD), vec(DT_PAD),
                   pl.BlockSpec((DT_PAD, D_MODEL), lambda i: (0, 0))],
        out_shape=[jax.ShapeDtypeStruct(dproj.shape, BF16), jax.ShapeDtypeStruct((t, CONV_DIM), F32),
                   jax.ShapeDtypeStruct((t, DT_PAD), BF16), jax.ShapeDtypeStruct((1, D_INNER), F32),
                   jax.ShapeDtypeStruct((1, DT_PAD), F32), jax.ShapeDtypeStruct((1, DT_PAD), F32),
                   jax.ShapeDtypeStruct((1, DT_PAD), F32), jax.ShapeDtypeStruct((DT_PAD, D_MODEL), F32)],
        scratch_shapes=[pltpu.VMEM((D_STATE, D_INNER), F32), pltpu.VMEM((2 * CHUNK, D_INNER), F32)],
        input_output_aliases={12: 0},
        compiler_params=_params(("arbitrary",), blk),
    )(dyb, y, xc, proj, dt_raw, sprev, dtb, alog, dskip_full, ng, e_bf, h, dproj)


NORM_TILE = 512


def _mesh_pos():
    return lax.axis_index("x"), lax.axis_index("y"), lax.axis_index("c")


def _other_chips(x, y):
    return [(1 - x, y), (x, 1 - y), (1 - x, 1 - y)]


def _all_peers(x, y, c):
    peers = []
    for k in range(1, N_DEV):
        fx, fy, fc = (k >> 2) & 1, (k >> 1) & 1, k & 1
        px, py, pc = x + fx - 2 * x * fx, y + fy - 2 * y * fy, c + fc - 2 * c * fc
        peers.append(((px, py, pc), 4 * px + 2 * py + pc))
    return peers


def _all_gather(shards, name, own_only=(), norm_of=None):
    n, n_own = len(shards), len(own_only)
    n_norm = 0 if norm_of is None else 1

    def body(*refs):
        ins, own_ins = refs[:n], refs[n:n + n_own]
        n_in = n + n_own + 2 * n_norm
        outs, own_outs = refs[n_in:n_in + n], refs[n_in + n:n_in + n + n_own]
        send_sems, recv_sems, local_sems = refs[n_in + n + n_own + n_norm:n_in + n + n_own + n_norm + 3]
        x, y, c = _mesh_pos()
        me, sibling = (x, y, c), (x, y, 1 - c)
        chips = _other_chips(x, y)

        def slot(p):
            return 4 * p[0] + 2 * p[1] + p[2]

        def copy(a, k, block, to, src=None):
            dst = outs[a].at[slot(block)]
            return pltpu.make_async_remote_copy(
                src_ref=dst if src is None else src, dst_ref=dst, send_sem=send_sems.at[a * 7 + k],
                recv_sem=recv_sems.at[a * 7 + k], device_id=to, device_id_type=MESH)

        started = []
        own = []
        for a in range(n_own):
            mine = pltpu.make_async_copy(own_ins[a], own_outs[a].at[slot(me)], local_sems.at[n + a])
            mine.start()
            own.append(mine)
        for a in range(n):
            mine = pltpu.make_async_copy(ins[a], outs[a].at[slot(me)], local_sems.at[a])
            mine.start()
            own.append(mine)
            first = [copy(a, 0, me, sibling, src=ins[a])]
            first += [copy(a, 1 + j, me, (*chip, c), src=ins[a]) for j, chip in enumerate(chips)]
            for cp in first:
                cp.start()
            started += first
        if n_norm:
            x_hbm, g_ref = refs[n + n_own], refs[n + n_own + 1]
            h_hbm = refs[n_in + n + n_own]
            x_buf, h_buf, tile_sem = refs[n_in + n + n_own + n_norm + 3:]
            for i in range(x_hbm.shape[0] // NORM_TILE):
                rows = pl.ds(i * NORM_TILE, NORM_TILE)
                load = pltpu.make_async_copy(x_hbm.at[rows], x_buf, tile_sem)
                load.start()
                load.wait()
                xv = x_buf[...]
                r = lax.rsqrt(jnp.mean(xv * xv, axis=-1, keepdims=True) + NORM_EPS)
                h_buf[...] = (xv * r * g_ref[...]).astype(BF16)
                store = pltpu.make_async_copy(h_buf, h_hbm.at[rows], tile_sem)
                store.start()
                store.wait()
        for a in range(n):
            for j, chip in enumerate(chips):
                copy(a, 1 + j, (*chip, c), me).wait_recv()
                fwd = copy(a, 4 + j, (*chip, c), sibling)
                fwd.start()
                started.append(fwd)
        for a in range(n):
            copy(a, 0, sibling, me).wait_recv()
            for j, chip in enumerate(chips):
                copy(a, 4 + j, (*chip, 1 - c), me).wait_recv()
        for cp in started:
            cp.wait_send()
        for mine in own:
            mine.wait()

    extra_in = [] if norm_of is None else list(norm_of)
    extra_specs = [] if norm_of is None else [_HBM, pl.BlockSpec(memory_space=pltpu.VMEM)]
    extra_out = [] if norm_of is None else [jax.ShapeDtypeStruct(norm_of[0].shape, BF16)]
    extra_scratch = [] if norm_of is None else [pltpu.VMEM((NORM_TILE, norm_of[0].shape[1]), F32),
                                                pltpu.VMEM((NORM_TILE, norm_of[0].shape[1]), BF16), pltpu.SemaphoreType.DMA]
    return pl.pallas_call(
        body, name=name,
        in_specs=[_HBM] * (n + n_own) + extra_specs, out_specs=[_HBM] * (n + n_own + n_norm),
        out_shape=[jax.ShapeDtypeStruct((N_DEV,) + s.shape, s.dtype) for s in (*shards, *own_only)] + extra_out,
        scratch_shapes=[pltpu.SemaphoreType.DMA((7 * n,)), pltpu.SemaphoreType.DMA((7 * n,)),
                        pltpu.SemaphoreType.DMA((n + n_own,))] + extra_scratch,
    )(*shards, *own_only, *extra_in)


_SMALL_ROWS = (("norm_mix_g", 8), ("conv_b", 32), ("dt_bias", 1), ("a_log", 1), ("d_skip", 1), ("ssm_norm_g", 16),
               ("v_norm_g", 8), ("v_norm_b", 8), ("w_spatial", 1024), ("b_spatial", 8), ("b_gates", 16), ("norm_mlp_g", 8),
               ("norm_final_g", 8), ("conv_w", 128), ("loss", 1))
_LAST_SMALL = (("norm_mix_g", 8),)


def _packed_rows(table):
    return -(-sum(r for _, r in table) // SUBLANES) * SUBLANES


def _small_offsets(table=_SMALL_ROWS):
    offs, r = {}, 0
    for name, rows in table:
        offs[name] = r
        r += rows
    return offs


def _rows_from(src_ref, dst_ref, r0):
    k, w = src_ref.shape
    if w <= LANES:
        dst_ref[r0:r0 + k, 0:w] = src_ref[...]
        return
    per = w // LANES
    for i in range(k):
        for j in range(per):
            dst_ref[r0 + i * per + j:r0 + i * per + j + 1, :] = src_ref[i:i + 1, j * LANES:(j + 1) * LANES]


def _rows_to(src_ref, r0, dst_ref):
    k, w = dst_ref.shape
    if w <= LANES:
        dst_ref[...] = src_ref[r0:r0 + k, 0:w]
        return
    per = w // LANES
    for i in range(k):
        for j in range(per):
            dst_ref[i:i + 1, j * LANES:(j + 1) * LANES] = src_ref[r0 + i * per + j:r0 + i * per + j + 1, :]


def _pack_small(grads, slot_idx, name):
    names = [n for n, _ in _SMALL_ROWS if n in grads]
    offs = _small_offsets()
    rows = _packed_rows(_SMALL_ROWS)

    def body(slot_ref, *refs):
        del slot_ref
        ins, (packed_ref, land_ref) = refs[:len(names)], refs[len(names):]
        packed_ref[...] = jnp.zeros_like(packed_ref)
        for n, ref in zip(names, ins):
            _rows_from(ref, packed_ref, offs[n])
        land_ref[0] = packed_ref[...]

    whole = lambda shape: pl.BlockSpec(shape, lambda i, slot_ref: (0,) * len(shape))
    grid_spec = pltpu.PrefetchScalarGridSpec(
        num_scalar_prefetch=1, grid=(1,), in_specs=[whole(grads[n].shape) for n in names],
        out_specs=[whole((rows, LANES)), pl.BlockSpec((1, rows, LANES), lambda i, slot_ref: (slot_ref[0], 0, 0))])
    return pl.pallas_call(
        body, name=name, grid_spec=grid_spec,
        out_shape=[jax.ShapeDtypeStruct((rows, LANES), F32), jax.ShapeDtypeStruct((N_DEV, rows, LANES), F32)],
    )(slot_idx, *[grads[n] for n in names])


def _exchange_small(grads, table, name):
    names = [n for n, _ in table]
    offs = _small_offsets(table)
    n_in = len(names)
    packed_rows = _packed_rows(table)

    def body(*refs):
        ins, out_ref = refs[:n_in], refs[n_in]
        packed, send_sems, recv_sems, local_sem = refs[n_in + 1:]
        packed[...] = jnp.zeros_like(packed)
        for n, ref in zip(names, ins):
            _rows_from(ref, packed, offs[n])
        x, y, c = _mesh_pos()
        my_slot = 4 * x + 2 * y + c
        mine = pltpu.make_async_copy(packed, out_ref.at[my_slot], local_sem)
        mine.start()
        copies = []
        for k, (peer, peer_slot) in enumerate(_all_peers(x, y, c)):
            sems = dict(send_sem=send_sems.at[k], recv_sem=recv_sems.at[k], device_id=peer, device_id_type=MESH)
            send = pltpu.make_async_remote_copy(src_ref=packed, dst_ref=out_ref.at[my_slot], **sems)
            send.start()
            copies.append((send, pltpu.make_async_remote_copy(src_ref=packed, dst_ref=out_ref.at[peer_slot], **sems)))
        for send, recv in copies:
            send.wait_send()
            recv.wait_recv()
        mine.wait()

    return pl.pallas_call(
        body, name=name, in_specs=[pl.BlockSpec(memory_space=pltpu.VMEM)] * n_in, out_specs=_HBM,
        out_shape=jax.ShapeDtypeStruct((N_DEV, packed_rows, LANES), F32),
        scratch_shapes=[pltpu.VMEM((packed_rows, LANES), F32), pltpu.SemaphoreType.DMA((N_DEV - 1,)),
                        pltpu.SemaphoreType.DMA((N_DEV - 1,)), pltpu.SemaphoreType.DMA],
    )(*[grads[n] for n in names])


def _swap_with_sibling(grads, name):
    n = len(grads)

    def body(*refs):
        ins, outs = refs[:n], refs[n:2 * n]
        send_sems, recv_sems = refs[2 * n:]
        x, y, c = _mesh_pos()
        copies = []
        for a in range(n):
            for k in range(N_CHIP):
                cp = pltpu.make_async_remote_copy(
                    src_ref=ins[a].at[(1 - c) + 2 * k], dst_ref=outs[a].at[k], send_sem=send_sems.at[a * N_CHIP + k],
                    recv_sem=recv_sems.at[a * N_CHIP + k], device_id=(x, y, 1 - c), device_id_type=MESH)
                cp.start()
                copies.append(cp)
        for cp in copies:
            cp.wait()

    return pl.pallas_call(
        body, name=name, in_specs=[_HBM] * n, out_specs=[_HBM] * n,
        out_shape=[jax.ShapeDtypeStruct((N_CHIP,) + g.shape[1:], g.dtype) for g in grads],
        scratch_shapes=[pltpu.SemaphoreType.DMA((N_CHIP * n,)), pltpu.SemaphoreType.DMA((N_CHIP * n,))],
    )(*grads)


_SEM = pl.BlockSpec(memory_space=pltpu.SEMAPHORE)
_IN_HBM = pl.BlockSpec(memory_space=pltpu.HBM)
_EFFECT = pltpu.SideEffectType.DATAFLOW_SIDE_EFFECTING


def _in_hbm(a):
    return pltpu.with_memory_space_constraint(a, pltpu.HBM)


def _gather_copies(ins, lands, send_sems, recv_sems):
    x, y, c = _mesh_pos()
    my_slot = 4 * x + 2 * y + c
    pairs = []
    for a in range(len(ins)):
        for k, (peer, peer_slot) in enumerate(_all_peers(x, y, c)):
            sems = dict(send_sem=send_sems.at[a * (N_DEV - 1) + k], recv_sem=recv_sems.at[a * (N_DEV - 1) + k],
                        device_id=peer, device_id_type=MESH)
            pairs.append((pltpu.make_async_remote_copy(src_ref=ins[a], dst_ref=lands[a].at[my_slot], **sems),
                          pltpu.make_async_remote_copy(src_ref=ins[a], dst_ref=lands[a].at[peer_slot], **sems)))
    return pairs


def _scatter_copies(ins, lands, send_sems, recv_sems):
    x, y, c = _mesh_pos()
    my_chip = 2 * x + y
    pairs = []
    for a in range(len(ins)):
        for j, chip in enumerate(_other_chips(x, y)):
            there = 2 * chip[0] + chip[1]
            sems = dict(send_sem=send_sems.at[a * 3 + j], recv_sem=recv_sems.at[a * 3 + j],
                        device_id=(*chip, c), device_id_type=MESH)
            pairs.append((pltpu.make_async_remote_copy(src_ref=ins[a].at[there], dst_ref=lands[a].at[my_chip], **sems),
                          pltpu.make_async_remote_copy(src_ref=ins[a].at[my_chip], dst_ref=lands[a].at[there], **sems)))
    return pairs


def _split_start(srcs, lands, copies, per_array, name):
    n = len(srcs)

    def body(*refs):
        ins, land_refs = refs[:n], refs[n:2 * n]
        send_sems, recv_sems = refs[2 * n], refs[2 * n + 1]
        token = refs[-1]
        for send, _ in copies(ins, land_refs, send_sems, recv_sems):
            send.start()
        token[...] = jnp.zeros_like(token)

    outs = pl.pallas_call(
        body, name=name,
        out_shape=(pltpu.SemaphoreType.DMA((per_array * n,)), pltpu.SemaphoreType.DMA((per_array * n,)),
                   *[pltpu.HBM(s.shape, s.dtype) for s in srcs], *[pltpu.HBM(l.shape, l.dtype) for l in lands],
                   jax.ShapeDtypeStruct((SUBLANES, LANES), F32)),
        in_specs=[_IN_HBM] * (2 * n),
        out_specs=(_SEM, _SEM, *[_IN_HBM] * (2 * n), pl.BlockSpec(memory_space=pltpu.VMEM)),
        input_output_aliases={i: 2 + i for i in range(2 * n)},
        compiler_params=pltpu.CompilerParams(has_side_effects=_EFFECT),
    )(*[_in_hbm(s) for s in srcs], *[_in_hbm(l) for l in lands])
    return outs[0], outs[1], list(outs[2:2 + n]), list(outs[2 + n:2 + 2 * n]), outs[-1]


def _split_wait(started, copies, after, name):
    send_sems, recv_sems, srcs, lands, _ = started
    n = len(srcs)

    def body(*refs):
        ins, land_refs = refs[:n], refs[n:2 * n]
        for send, recv in copies(ins, land_refs, refs[2 * n], refs[2 * n + 1]):
            send.wait_send()
            recv.wait_recv()

    outs = pl.pallas_call(
        body, name=name,
        out_shape=(*[pltpu.HBM(s.shape, s.dtype) for s in srcs], *[pltpu.HBM(l.shape, l.dtype) for l in lands]),
        in_specs=[_IN_HBM] * (2 * n) + [_SEM, _SEM, _HBM],
        out_specs=[_IN_HBM] * (2 * n),
        input_output_aliases={i: i for i in range(2 * n)},
        compiler_params=pltpu.CompilerParams(has_side_effects=_EFFECT),
    )(*srcs, *lands, send_sems, recv_sems, after)
    return list(outs[:n]), list(outs[n:])


def _ew_block(rows, cols, slots):
    budget = 8 * 1024 * 1024
    br, bc = rows, cols
    while slots * br * bc * 4 > budget:
        if br % 2 == 0 and (br // 2) % (2 * SUBLANES) == 0:
            br //= 2
        elif bc % 2 == 0 and (bc // 2) % LANES == 0:
            bc //= 2
        else:
            break
    return br, bc


def _add_sibling(grads, recv, c_idx, name):
    _, rows, cols = grads.shape
    br, bc = _ew_block(rows, cols, 3)

    def body(c_ref, g_ref, r_ref, out_ref):
        del c_ref
        out_ref[...] = (g_ref[...].astype(F32) + r_ref[...].astype(F32)).astype(out_ref.dtype)

    grid_spec = pltpu.PrefetchScalarGridSpec(
        num_scalar_prefetch=1, grid=(N_CHIP, rows // br, cols // bc),
        in_specs=[pl.BlockSpec((1, br, bc), lambda k, i, j, c_ref: (c_ref[0] + 2 * k, i, j)),
                  pl.BlockSpec((1, br, bc), lambda k, i, j, c_ref: (k, i, j))],
        out_specs=pl.BlockSpec((1, br, bc), lambda k, i, j, c_ref: (k, i, j)))
    return pl.pallas_call(
        body, name=name, grid_spec=grid_spec, out_shape=jax.ShapeDtypeStruct((N_CHIP, rows, cols), grads.dtype),
        compiler_params=_params(("parallel", "parallel", "parallel"), 3 * _nbytes((br, bc), F32)),
    )(c_idx, grads, recv)


def _adam_math(g, w, m, v):
    m2 = ADAM_B1 * m + (1.0 - ADAM_B1) * g
    v2 = ADAM_B2 * v + (1.0 - ADAM_B2) * (g * g)
    m_hat = m2 * (1.0 / (1.0 - ADAM_B1 ** ADAM_STEP))
    v_hat = v2 * (1.0 / (1.0 - ADAM_B2 ** ADAM_STEP))
    return -ADAM_LR * (m_hat / (jnp.sqrt(v_hat) + ADAM_EPS) + ADAM_WD * w), m2, v2


def _adamw(slots, w, m, v, name, own=None, own_slot=None):
    ns, rows, cols = slots.shape
    br, bc = _ew_block(rows, cols, 2 * ns + 7)

    def update(g, w_ref, m_ref, v_ref, g_ref, d_ref, m2_ref, v2_ref):
        g_ref[...] = g
        d_ref[...], m2_ref[...], v2_ref[...] = _adam_math(g, w_ref[...], m_ref[...], v_ref[...])

    out_shape = [jax.ShapeDtypeStruct((rows, cols), F32)] * 4
    params = _params(("parallel", "parallel"), (2 * ns + 7) * _nbytes((br, bc), F32))
    grid = (rows // br, cols // bc)
    if own is None:
        def body(s_ref, *rest):
            g = s_ref[0].astype(F32)
            for k in range(1, ns):
                g = g + s_ref[k].astype(F32)
            update(g, *rest)

        blk = pl.BlockSpec((br, bc), lambda i, j: (i, j))
        return pl.pallas_call(
            body, name=name, grid=grid,
            in_specs=[pl.BlockSpec((ns, br, bc), lambda i, j: (0, i, j)), blk, blk, blk], out_specs=[blk] * 4,
            out_shape=out_shape, compiler_params=params,
        )(slots, w, m, v)

    def body_own(slot_ref, s_ref, o_ref, *rest):
        g = None
        for k in range(ns):
            term = jnp.where(slot_ref[0] == k, o_ref[k].astype(F32), s_ref[k].astype(F32))
            g = term if g is None else g + term
        update(g, *rest)

    blk = pl.BlockSpec((br, bc), lambda i, j, slot_ref: (i, j))
    stack = pl.BlockSpec((ns, br, bc), lambda i, j, slot_ref: (0, i, j))
    grid_spec = pltpu.PrefetchScalarGridSpec(num_scalar_prefetch=1, grid=grid, in_specs=[stack, stack, blk, blk, blk],
                                             out_specs=[blk] * 4)
    return pl.pallas_call(body_own, name=name, grid_spec=grid_spec, out_shape=out_shape, compiler_params=params,
                          )(own_slot, slots, own, w, m, v)


def _adamw_small(all_g, last_g, params, extra_shapes, name):
    names = [n for n, _ in _SMALL_ROWS if n in params]
    extras = [n for n, _ in _SMALL_ROWS if n not in params]
    offs = _small_offsets()
    n_p = len(names)

    def body(*refs):
        s_ref, last_ref = refs[0], refs[1]
        wmv = refs[2:2 + 3 * n_p]
        outs = refs[2 + 3 * n_p:2 + 7 * n_p]
        extra_refs = refs[2 + 7 * n_p:2 + 7 * n_p + len(extras)]
        summed = refs[-1]
        g, g_last = s_ref[0], last_ref[0]
        for k in range(1, N_DEV):
            g, g_last = g + s_ref[k], g_last + last_ref[k]
        summed[...] = g
        last_offs = _small_offsets(_LAST_SMALL)
        for n, rows in _LAST_SMALL:
            summed[offs[n]:offs[n] + rows, :] = g_last[last_offs[n]:last_offs[n] + rows, :]
        for i, n in enumerate(names):
            w_ref, m_ref, v_ref = wmv[3 * i:3 * i + 3]
            g_ref, d_ref, m2_ref, v2_ref = outs[4 * i:4 * i + 4]
            _rows_to(summed, offs[n], g_ref)
            d_ref[...], m2_ref[...], v2_ref[...] = _adam_math(g_ref[...], w_ref[...], m_ref[...], v_ref[...])
        for n, ref in zip(extras, extra_refs):
            _rows_to(summed, offs[n], ref)

    flat = [a for n in names for a in params[n]]
    out_shape = [jax.ShapeDtypeStruct(params[n][0].shape, F32) for n in names for _ in range(4)]
    out_shape += [jax.ShapeDtypeStruct(s, F32) for s in extra_shapes]
    vmem = pl.BlockSpec(memory_space=pltpu.VMEM)
    res = pl.pallas_call(
        body, name=name, in_specs=[vmem] * (2 + len(flat)), out_specs=[vmem] * len(out_shape), out_shape=out_shape,
        scratch_shapes=[pltpu.VMEM(all_g.shape[1:], F32)],
        compiler_params=pltpu.CompilerParams(vmem_limit_bytes=_vmem_limit(_nbytes(all_g.shape, F32))),
    )(all_g, last_g, *flat)
    return {n: res[4 * i:4 * i + 4] for i, n in enumerate(names)}, res[4 * n_p:]


def _mm_tiles(mode, m, n, k):
    tn = min(n, 1024)
    if mode == "tn":
        return min(m, 1024), tn, min(k, 4096)
    if k <= 1024:
        return min(m, 2048), tn, k
    if k <= 2048:
        return min(m, 1024), tn, k
    if k <= 4096:
        return min(m, 512), tn, k
    return min(m, 1024), tn, 2048


def _local_step(x, h, target, wts, small, exchange):
    t = x.shape[0]
    assert t % CONV_FWD_TILE == 0 and t % CONV_TILE == 0 and t % GMLP_TILE == 0 and t % (2 * ROW_TILE) == 0, t
    w_main_t, w_dt_t = wts["w_main_t"], wts["w_dt_t"]
    bsp_t = small["b_spatial"].T
    pad32 = lambda a: jnp.pad(a, ((0, 0), (0, DT_PAD - N_HEADS)))
    dtb, alog = pad32(small["dt_bias"]), pad32(small["a_log"])
    dskip_full = jnp.repeat(small["d_skip"], HEAD_DIM, axis=1)
    head_of_col = lax.broadcasted_iota(jnp.int32, (DT_PAD, D_INNER), 1) // HEAD_DIM
    e_bf = (head_of_col == lax.broadcasted_iota(jnp.int32, (DT_PAD, D_INNER), 0)).astype(BF16)

    def mm(a, b, mode, name, **kw):
        if mode == "nn":
            m, k, n = a.shape[0], a.shape[1], b.shape[1]
        elif mode == "nt":
            m, k, n = a.shape[0], a.shape[1], b.shape[0]
        else:
            m, k, n = a.shape[1], a.shape[0], b.shape[1]
        tm, tn, tk = _mm_tiles(mode, m, n, k)
        tm = min(tm, kw.pop("max_tm", tm))
        kw.setdefault("out_dtypes", (BF16,) if mode == "tn" else (F32,))
        if "extra_specs" in kw:
            kw["extra_specs"] = kw["extra_specs"](tm, tn)
        return _matmul(a, b, mode=mode, tm=tm, tn=tn, tk=tk, name=name, **kw)

    def out_tile(tm, tn):
        return (((tm, tn), lambda i, j: (i, j)),)

    def row_tiles(n_tiles, *vectors, gate_logits=False):
        def specs(tm, tn):
            out = [((tm, tn), lambda i, j: (i, j))] * n_tiles
            if gate_logits:
                out += [((tm, D_MODEL), lambda i, j, cb=COL_GATE // D_MODEL + half: (i, cb)) for half in range(2)]
            return tuple(out) + tuple(((1, w), lambda i, j, cb=cb: (0, cb)) for w, cb in vectors)
        return specs

    vec = lambda w: ((1, w), F32, (1, w), lambda i, j: (0, 0))
    fused_tm = 512

    started = exchange.begin()
    dt_raw = mm(h, w_dt_t, "nt", "proj_dt", deps=started)
    proj = mm(h, w_main_t, "nt", "proj_main", out_dtypes=(BF16,), deps=started)
    y_a = _gmlp_fwd(proj, small["v_norm_g"], small["v_norm_b"], small["w_spatial"], bsp_t, "gmlp_fwd")
    pre_conv, xc = _conv_fwd(proj, wts["conv_w"], small["conv_b"], "conv_fwd")
    y_ssd, y_b, sprev = _ssd_fwd(xc, proj, dt_raw, dtb, alog, dskip_full, small["ssm_norm_g"], e_bf, "ssd_fwd")
    wts = {**wts, **exchange.late_weights(y_b)}
    pa = mm(y_a, wts["w_proj_a"], "nn", "proj_a", out_dtypes=(BF16,))
    pb, merged = mm(y_b, wts["w_proj_b"], "nn", "proj_b", epilogue=_merge_epilogue, out_dtypes=(BF16, BF16), max_tm=fused_tm,
                    extras=(pa, proj, proj, small["b_gates"], small["b_gates"]),
                    extra_specs=row_tiles(1, (D_MODEL, 0), (D_MODEL, 1), gate_logits=True))
    x1, h2 = mm(merged, wts["w_out"], "nn", "out_proj", epilogue=_residual_rms_epilogue, out_dtypes=(F32, BF16),
                max_tm=2 * fused_tm, extras=(x, small["norm_mlp_g"]), extra_specs=row_tiles(1, (D_MODEL, 0)))

    def relu_sq(acc, ex, outs, first):
        r = jnp.maximum(acc, 0.0)
        outs[0][...] = (r * r).astype(BF16)

    act = mm(h2, wts["w_mlp_up"], "nn", "mlp_up", epilogue=relu_sq, out_dtypes=(BF16,))
    dx2, dx2_b, g_final, _, loss = mm(
        act, wts["w_mlp_down"], "nn", "mlp_down", epilogue=_loss_epilogue, carry=True,
        out_dtypes=(F32, BF16, vec(D_MODEL), vec(D_MODEL), vec(LANES)),
        extras=(x1, small["norm_final_g"], target), extra_specs=lambda tm, tn: (
            ((tm, tn), lambda i, j: (i, j)), ((1, tn), lambda i, j: (0, 0)), ((tm, tn), lambda i, j: (i, j))))

    def relu_sq_bwd(acc, ex, outs, first):
        outs[0][...] = (acc * 2.0 * jnp.sqrt(ex[0][...].astype(F32))).astype(BF16)

    dup = mm(dx2_b, wts["w_mlp_down"], "nt", "d_act", epilogue=relu_sq_bwd, extras=(act,), extra_specs=out_tile,
             out_dtypes=(BF16,))
    g_down = mm(act, dx2_b, "tn", "g_mlp_down")
    g_up = mm(h2, dup, "tn", "g_mlp_up")
    dx1, dx1_b, g_mlp = mm(
        dup, wts["w_mlp_up"], "nt", "d_h2", epilogue=_rms_bwd_epilogue, carry=True,
        out_dtypes=(F32, BF16, vec(D_MODEL)), extras=(x1, small["norm_mlp_g"], dx2), extra_specs=lambda tm, tn: (
            ((tm, tn), lambda i, j: (i, j)), ((1, tn), lambda i, j: (0, 0)), ((tm, tn), lambda i, j: (i, j))))

    g_out = mm(merged, dx1_b, "tn", "g_out")
    dpa, dpb, dproj, g_bgates = mm(
        dx1_b, wts["w_out"], "nt", "d_merged", epilogue=_merge_bwd_epilogue, carry=True, max_tm=fused_tm,
        out_dtypes=(BF16, BF16, ((t, MAIN_W), BF16, (fused_tm, 2 * D_MODEL), lambda i, j: (i, COL_GATE // (2 * D_MODEL))),
                    vec(2 * D_MODEL)),
        extras=(pa, pb, proj, proj, small["b_gates"], small["b_gates"]),
        extra_specs=row_tiles(2, (D_MODEL, 0), (D_MODEL, 1), gate_logits=True))
    g_pa = mm(y_a, dpa, "tn", "g_proj_a")
    g_pb = mm(y_b, dpb, "tn", "g_proj_b")
    started = exchange.reduce("late", {"w_mlp_down": g_down, "w_mlp_up": g_up, "w_out": g_out, "w_proj_a": g_pa,
                                       "w_proj_b": g_pb})
    dya = mm(dpa, wts["w_proj_a"], "nt", "d_ya", deps=started, out_dtypes=(BF16,))
    dyb = mm(dpb, wts["w_proj_b"], "nt", "d_yb", out_dtypes=(BF16,))

    dproj, g_wsp, g_bsp_t, g_vg, g_vb = _gmlp_bwd(proj, dya, small["v_norm_g"], small["v_norm_b"], small["w_spatial"],
                                                   bsp_t, dproj, "gmlp_bwd")
    dproj, dxc, ddt, g_ng, g_dskip, g_alog, g_dtb, g_dt_t = _ssd_bwd(dyb, y_ssd, xc, proj, dt_raw, sprev, dtb, alog, dskip_full,
                                                                     small["ssm_norm_g"], e_bf, h, dproj, "ssd_bwd")
    dproj, g_convw, g_convb = _conv_bwd(proj, pre_conv, dxc, wts["conv_w"], dproj, "conv_bwd")

    small_grads = {
        "conv_w": g_convw, "loss": loss,
        "conv_b": g_convb, "dt_bias": g_dtb, "a_log": g_alog, "d_skip": g_dskip, "ssm_norm_g": g_ng,
        "v_norm_g": g_vg, "v_norm_b": g_vb, "w_spatial": g_wsp.reshape(GROUPS * CHUNK, CHUNK), "b_spatial": g_bsp_t.T,
        "b_gates": g_bgates, "norm_mlp_g": g_mlp, "norm_final_g": g_final,
    }
    g_main_t = mm(dproj, h, "tn", "g_in_main", deps=exchange.small(small_grads))
    started = exchange.reduce("in", {"w_in": (g_main_t, g_dt_t.astype(BF16))})

    def input_grad(acc, ex, outs, first):
        x_ref, g_ref, res_ref, ddt_ref, wdt_ref = ex
        gg = jnp.zeros((1, D_MODEL), F32)
        for r in range(acc.shape[0] // ROW_TILE):
            rows = slice(r * ROW_TILE, (r + 1) * ROW_TILE)
            dh = acc[rows] + _dot(ddt_ref[rows, :], wdt_ref[...], _NN)
            dx, gg_r = _rms_pullback(x_ref[rows, :], g_ref[...], dh)
            outs[0][rows, :] = dx + res_ref[rows, :]
            gg = gg + gg_r

        _zero_when(first, outs[1])
        outs[1][...] += gg

    grad_x, g_mix = mm(
        dproj, w_main_t, "nn", "d_h", epilogue=input_grad, deps=started, carry=True,
        out_dtypes=(F32, vec(D_MODEL)), extras=(x, small["norm_mix_g"], dx1, ddt, w_dt_t), extra_specs=lambda tm, tn: (
            ((tm, tn), lambda i, j: (i, j)), ((1, tn), lambda i, j: (0, 0)), ((tm, tn), lambda i, j: (i, j)),
            ((tm, DT_PAD), lambda i, j: (i, 0)), ((DT_PAD, D_MODEL), lambda i, j: (0, 0))))

    return grad_x, g_mix


SHARD_ROWS = (MAIN_W + N_HEADS) // N_DEV
REGROUP_IN = 2048


def _main_rows_of(gathered, name):
    n_dev, shard, d = gathered.shape
    blk = 1024
    nb = MAIN_W // blk

    def first_feature(b):
        return b * blk + (N_HEADS if b * blk >= COL_GATE else 0)

    def body(a_ref, b_ref, out_ref):
        for b in range(nb):
            s0, r0 = divmod(first_feature(b), shard)
            n1 = min(shard - r0, blk)

            @pl.when(pl.program_id(0) == b)
            def _(r0=r0, n1=n1):
                out_ref[0:n1, :] = a_ref[0, r0:r0 + n1, :]
                if n1 < blk:
                    out_ref[n1:blk, :] = b_ref[0, 0:blk - n1, :]

    def slot(b):
        return (b * blk + jnp.where(b * blk >= COL_GATE, N_HEADS, 0)) // shard

    return pl.pallas_call(
        body, name=name, grid=(nb,),
        in_specs=[pl.BlockSpec((1, shard, d), lambda b: (slot(b), 0, 0)),
                  pl.BlockSpec((1, shard, d), lambda b: (jnp.minimum(slot(b) + 1, n_dev - 1), 0, 0))],
        out_specs=pl.BlockSpec((blk, d), lambda b: (b, 0)),
        out_shape=jax.ShapeDtypeStruct((MAIN_W, d), gathered.dtype),
        compiler_params=_params(("parallel",), 3 * _nbytes((shard, d), gathered.dtype)),
    )(gathered, gathered)


def _by_device_rows(g_main_t, g_dt_t, name):
    d = g_main_t.shape[1]
    n_blocks = MAIN_W // REGROUP_IN
    dt_dev, dt_row = divmod(COL_GATE, SHARD_ROWS)

    def main_start(s):
        return s * SHARD_ROWS - (N_HEADS if s > dt_dev else 0)

    def body(a_ref, b_ref, dt_ref, out_ref):
        for s in range(N_DEV):
            m0 = main_start(s)
            k0, off = divmod(m0, REGROUP_IN)
            pieces = []
            if s == dt_dev:
                pieces = [(0, dt_row, m0), (dt_row, N_HEADS, None), (dt_row + N_HEADS, SHARD_ROWS - dt_row - N_HEADS, m0 + dt_row)]
            else:
                pieces = [(0, SHARD_ROWS, m0)]

            @pl.when(pl.program_id(0) == s)
            def _(pieces=pieces, k0=k0):
                for dst, n, src in pieces:
                    if src is None:
                        out_ref[0, dst:dst + n, :] = dt_ref[0:n, :]
                        continue
                    lo = src - k0 * REGROUP_IN
                    n_a = max(0, min(n, REGROUP_IN - lo))
                    if n_a:
                        out_ref[0, dst:dst + n_a, :] = a_ref[lo:lo + n_a, :]
                    if n_a < n:
                        lo_b = max(lo - REGROUP_IN, 0)
                        out_ref[0, dst + n_a:dst + n, :] = b_ref[lo_b:lo_b + n - n_a, :]

    def first_block(s):
        return (s * SHARD_ROWS - jnp.where(s > dt_dev, N_HEADS, 0)) // REGROUP_IN

    return pl.pallas_call(
        body, name=name, grid=(N_DEV,),
        in_specs=[pl.BlockSpec((REGROUP_IN, d), lambda s: (first_block(s), 0)),
                  pl.BlockSpec((REGROUP_IN, d), lambda s: (jnp.minimum(first_block(s) + 1, n_blocks - 1), 0)),
                  pl.BlockSpec((DT_PAD, d), lambda s: (0, 0))],
        out_specs=pl.BlockSpec((1, SHARD_ROWS, d), lambda s: (s, 0, 0)),
        out_shape=jax.ShapeDtypeStruct((N_DEV, SHARD_ROWS, d), g_main_t.dtype),
        compiler_params=_params(("parallel",), 3 * _nbytes((REGROUP_IN, d), g_main_t.dtype)),
    )(g_main_t, g_main_t, g_dt_t)


_LATE = ["w_proj_a", "w_proj_b", "w_out", "w_mlp_up", "w_mlp_down"]
_BY_COLS = ("w_mlp_up",)


class _Exchange:
    def __init__(self, late_shards, late_lands):
        self.late_shards, self.late_lands = late_shards, late_lands
        self.c_idx = lax.axis_index("c").astype(jnp.int32).reshape(1)
        self.chip_idx = (2 * lax.axis_index("x") + lax.axis_index("y")).astype(jnp.int32).reshape(1)
        self.pending = []

    def begin(self):
        self.late = _split_start(self.late_shards, self.late_lands, _gather_copies, N_DEV - 1, "gather_late_start")
        return [self.late[-1]]

    def late_weights(self, after):
        _, lands = _split_wait(self.late, _gather_copies, after, "gather_late_wait")
        whole = {}
        for n, g in zip(_LATE, lands):
            whole[n] = jnp.transpose(g, (1, 0, 2)).reshape(g.shape[1], -1) if n in _BY_COLS else g.reshape(-1, g.shape[2])
        return whole

    def reduce(self, tag, grads):
        names = list(grads)
        by_dev = []
        for n in names:
            g = grads[n]
            if n == "w_in":
                by_dev.append(_by_device_rows(*g, "regroup_g_in"))
            elif n in _BY_COLS:
                by_dev.append(jnp.transpose(g.reshape(g.shape[0], N_DEV, -1), (1, 0, 2)))
            else:
                by_dev.append(g.reshape(N_DEV, -1, g.shape[1]))
        from_sibling = _swap_with_sibling(by_dev, "reduce_cores_" + tag)
        parts = [_add_sibling(g, r, self.c_idx, "add_cores_" + n) for n, g, r in zip(names, by_dev, from_sibling)]
        lands = [lax.empty(p.shape, p.dtype) for p in parts]
        started = _split_start(parts, lands, _scatter_copies, 3, "reduce_chips_start_" + tag)
        self.pending.append((tag, names, started))
        return [started[-1]]

    def small(self, grads):
        dev = 2 * self.chip_idx + self.c_idx
        packed, land = _pack_small(grads, dev, "pack_small")
        self.small_started = _split_start([packed], [land], _gather_copies, N_DEV - 1, "exchange_small_start")
        return [self.small_started[-1]]

    def finish(self, after):
        _, (all_small,) = _split_wait(self.small_started, _gather_copies, after, "exchange_small_wait")
        done = {}
        for tag, names, started in self.pending:
            parts, lands = _split_wait(started, _scatter_copies, after, "reduce_chips_wait_" + tag)
            for n, land, part in zip(names, lands, parts):
                done[n] = (land, part, self.chip_idx)
        return all_small, done


def kernel(x, norm_mix_g, w_in, conv_w, conv_b, dt_bias, a_log, d_skip, ssm_norm_g, v_norm_g, v_norm_b, w_spatial, b_spatial, b_gates, w_proj_a, w_proj_b, w_out, norm_mlp_g, w_mlp_up, w_mlp_down, norm_final_g, loss_target, m_norm_mix_g, m_w_in, m_conv_w, m_conv_b, m_dt_bias, m_a_log, m_d_skip, m_ssm_norm_g, m_v_norm_g, m_v_norm_b, m_w_spatial, m_b_spatial, m_b_gates, m_w_proj_a, m_w_proj_b, m_w_out, m_norm_mlp_g, m_w_mlp_up, m_w_mlp_down, m_norm_final_g, v_norm_mix_g, v_w_in, v_conv_w, v_conv_b, v_dt_bias, v_a_log, v_d_skip, v_ssm_norm_g, v_v_norm_g, v_v_norm_b, v_w_spatial, v_b_spatial, v_b_gates, v_w_proj_a, v_w_proj_b, v_w_out, v_norm_mlp_g, v_w_mlp_up, v_w_mlp_down, v_norm_final_g):
    given = dict(locals())
    names = ["norm_mix_g", "w_in", "conv_w", "conv_b", "dt_bias", "a_log", "d_skip", "ssm_norm_g", "v_norm_g", "v_norm_b",
             "w_spatial", "b_spatial", "b_gates", "w_proj_a", "w_proj_b", "w_out", "norm_mlp_g", "w_mlp_up", "w_mlp_down",
             "norm_final_g"]
    shapes = {n: given[n].shape for n in names}
    dev = 4 * lax.axis_index("x") + 2 * lax.axis_index("y") + lax.axis_index("c")

    shard2d = {"w_in": w_in[0].T, "w_proj_a": w_proj_a[0], "w_proj_b": w_proj_b[0], "w_out": w_out[0],
               "w_mlp_up": w_mlp_up[0], "w_mlp_down": w_mlp_down[0]}
    conv_shard = conv_w.reshape(CONV_WIDTH, -1)
    late_shards = [shard2d[n].astype(BF16) for n in _LATE]
    w_in_all, conv_all, *late_lands, h = _all_gather([shard2d["w_in"].astype(BF16), conv_shard], "gather_first",
                                                     own_only=late_shards, norm_of=(x[0], norm_mix_g))
    dt_dev, dt_row = divmod(COL_GATE, SHARD_ROWS)
    w_dt_t = jnp.pad(w_in_all[dt_dev, dt_row:dt_row + N_HEADS], ((0, DT_PAD - N_HEADS), (0, 0)))
    wts = {"w_main_t": _main_rows_of(w_in_all, "regroup_w_in"), "w_dt_t": w_dt_t, "conv_w": jnp.transpose(conv_all, (1, 0, 2)).reshape(CONV_WIDTH, -1)}
    small = {"norm_mix_g": norm_mix_g, "conv_b": conv_b, "dt_bias": dt_bias, "a_log": a_log, "d_skip": d_skip,
             "ssm_norm_g": ssm_norm_g, "v_norm_g": v_norm_g, "v_norm_b": v_norm_b, "w_spatial": w_spatial[0],
             "b_spatial": b_spatial[0], "b_gates": b_gates, "norm_mlp_g": norm_mlp_g,
             "norm_final_g": norm_final_g.reshape(1, -1)}

    exchange = _Exchange(late_shards, late_lands)
    grad_x, g_mix = _local_step(x[0], h, loss_target[0], wts, small, exchange)

    out = {}
    all_small, large = exchange.finish(grad_x)
    for n, (slots, own, own_slot) in large.items():
        moments = [given["m_" + n][0], given["v_" + n][0]]
        if n == "w_in":
            moments = [mom.T for mom in moments]
        res = _adamw(slots, shard2d[n], *moments, "adamw_" + n, own=own, own_slot=own_slot)
        out[n] = [(r.T if n == "w_in" else r).reshape(shapes[n]) for r in res]

    last_small = _exchange_small({"norm_mix_g": g_mix}, _LAST_SMALL, "exchange_last")
    small["w_spatial"] = small["w_spatial"].reshape(GROUPS * CHUNK, CHUNK)
    params = {n: (w2d, given["m_" + n].reshape(w2d.shape), given["v_" + n].reshape(w2d.shape)) for n, w2d in small.items()}
    updated, (g_conv_full, loss_all) = _adamw_small(all_small, last_small, params, [(CONV_WIDTH, CONV_DIM), (1, LANES)],
                                                    "adamw_small")
    for n, res in updated.items():
        out[n] = [r.reshape(shapes[n]) for r in res]
    width = shapes["conv_w"][-1]
    g_conv = lax.dynamic_slice(g_conv_full, (0, dev * width), (CONV_WIDTH, width))
    res = _adamw(g_conv[None], conv_shard, m_conv_w.reshape(CONV_WIDTH, -1), v_conv_w.reshape(CONV_WIDTH, -1), "adamw_conv_w")
    out["conv_w"] = [r.reshape(shapes["conv_w"]) for r in res]

    loss = loss_all[0, 0]
    return (loss, grad_x[None], *[out[n][0] for n in names], *[out[n][1] for n in names],
            *[out[n][2] for n in names], *[out[n][3] for n in names])
```
